```python
import jax, jax.numpy as jnp
from jax import lax
import numpy as np

D_MODEL = 1024
BATCH = 16
SEQ = 256
DEPTH = 1
DEC_BATCH = 2
DEC_SEQ = 1024
PAST_LEN = 512

GRID_W = 64
D_MIX = D_MODEL
D_A = D_MIX // 2
D_B = D_MIX - D_A
H_A = 4
H_B = 4
DV_A = D_A // H_A
DK_A = DV_A // 2
DV_B = D_B // H_B
DK_B = DV_B // 2
R_ALPHA = 16
TAU_GLA = 16.0
CONV_K = 3
CHUNK = 64
D_FF = 4 * D_MODEL
EPS = 1e-6
SPLIT_SIZES = (H_A * DK_A, H_A * DK_A, D_A, D_A, 2 * R_ALPHA,
               H_B * DK_B, H_B * DK_B, D_B, D_B, 4 * H_B)
D_IN = sum(SPLIT_SIZES)

kernel_name = "hybrid_gla_mlstm_diffusion_step"


def rmsnorm(x, w):
    x32 = x.astype(jnp.float32)
    y = x32 * lax.rsqrt(jnp.mean(x32 * x32, axis=-1, keepdims=True) + EPS)
    return (y * w.astype(jnp.float32)).astype(x.dtype)


def head_rmsnorm(o, w):
    y = o * lax.rsqrt(jnp.mean(o * o, axis=-1, keepdims=True) + EPS)
    B, T, H, dv = o.shape
    return y.reshape(B, T, H * dv) * w.astype(jnp.float32)


def rev(a):
    return jnp.flip(a, axis=1)


def to_chunks(a):
    B, T, H = a.shape[:3]
    N = T // CHUNK
    if a.ndim == 4:
        return a.reshape(B, N, CHUNK, H, a.shape[-1]).transpose(1, 0, 3, 2, 4)
    return a.reshape(B, N, CHUNK, H).transpose(1, 0, 3, 2)


def from_chunks(o):
    N, B, H, L, d = o.shape
    return o.transpose(1, 0, 3, 2, 4).reshape(B, N * L, H, d)


def grid_dwconv(u, w, rows):
    B, T, C = u.shape
    img = u.reshape(B, rows, T // rows, C)
    out = lax.conv_general_dilated(img, w[:, :, None, :].astype(u.dtype), window_strides=(1, 1), padding='SAME',
                                   dimension_numbers=('NHWC', 'HWIO', 'NHWC'), feature_group_count=C)
    return out.reshape(B, T, C)


def gla_scan(q, k, v, g, S0):
    mask = jnp.tril(jnp.ones((CHUNK, CHUNK), dtype=bool))

    def step(S, inp):
        qc, kc, vc, gc = inp
        b = jnp.cumsum(gc, axis=2)
        diff = jnp.where(mask[:, :, None], b[:, :, :, None, :] - b[:, :, None, :, :], -jnp.inf)
        scores = jnp.einsum('bhtd,bhsd,bhtsd->bhts', qc, kc, jnp.exp(diff))
        o = jnp.einsum('bhts,bhsv->bhtv', scores, vc) + jnp.einsum('bhtd,bhdv->bhtv', qc * jnp.exp(b), S)
        bL = b[:, :, -1]
        S = jnp.exp(bL)[..., None] * S + jnp.einsum('bhsd,bhsv->bhdv', kc * jnp.exp(bL[:, :, None] - b), vc)
        return S, o

    S, o = lax.scan(step, S0.astype(jnp.float32), (to_chunks(q), to_chunks(k), to_chunks(v), to_chunks(g)))
    return from_chunks(o), S


def mlstm_scan(q, k, v, ig, lf, C0, n0, m0):
    mask = jnp.tril(jnp.ones((CHUNK, CHUNK), dtype=bool))

    def step(carry, inp):
        C, n, m = carry
        qc, kc, vc, ic, fc = inp
        F = jnp.cumsum(fc, axis=-1)
        D = jnp.where(mask, F[..., :, None] - F[..., None, :] + ic[..., None, :], -jnp.inf)
        inter = F + m[..., None]
        mt = jnp.maximum(inter, jnp.max(D, axis=-1))
        w_inter = jnp.exp(inter - mt)
        s = jnp.einsum('bhtd,bhsd->bhts', qc, kc) * jnp.exp(D - mt[..., None])
        num = jnp.einsum('bhts,bhsv->bhtv', s, vc) + w_inter[..., None] * jnp.einsum('bhtd,bhdv->bhtv', qc, C)
        den = jnp.sum(s, axis=-1) + w_inter * jnp.einsum('bhtd,bhd->bht', qc, n)
        h = num / jnp.maximum(jnp.abs(den), jnp.exp(-mt))[..., None]
        FL = F[..., -1]
        m_new = mt[..., -1]
        a = jnp.exp(FL + m - m_new)
        wk = jnp.exp(FL[..., None] - F + ic - m_new[..., None])
        C = a[..., None, None] * C + jnp.einsum('bhs,bhsd,bhsv->bhdv', wk, kc, vc)
        n = a[..., None] * n + jnp.einsum('bhs,bhsd->bhd', wk, kc)
        return (C, n, m_new), h

    f32 = jnp.float32
    (C, n, m), h = lax.scan(step, (C0.astype(f32), n0.astype(f32), m0.astype(f32)),
                            (to_chunks(q), to_chunks(k), to_chunks(v), to_chunks(ig), to_chunks(lf)))
    return from_chunks(h), C, n, m


def mixer(h, rows, S_gla, C_m, n_m, m_m, w_in, w_alpha2, b_alpha, b_mgate, conv_w, gnorm_a_w, gnorm_b_w, w_out):
    f32 = jnp.float32
    B, T, _ = h.shape
    z = h @ w_in
    qa, ka, va, ga, ra, qb, kb, vb, ob, gb = jnp.split(z, np.cumsum(SPLIT_SIZES)[:-1].tolist(), axis=-1)
    qa = qa.astype(f32).reshape(B, T, H_A, DK_A) * (DK_A ** -0.5)
    ka = ka.astype(f32).reshape(B, T, H_A, DK_A)
    va = va.astype(f32).reshape(B, T, H_A, DV_A)
    ra = ra.astype(f32).reshape(B, T, 2, R_ALPHA)
    loga = jax.nn.log_sigmoid(jnp.einsum('btzr,zrk->btzk', ra, w_alpha2.astype(f32)) + b_alpha.astype(f32)) / TAU_GLA
    loga = loga.reshape(B, T, 2, H_A, DK_A)
    oa_f, Sa_f = gla_scan(qa, ka, va, loga[:, :, 0], S_gla[:, 0])
    oa_b, Sa_b = gla_scan(rev(qa), rev(ka), rev(va), rev(loga[:, :, 1]), S_gla[:, 1])
    out_a = head_rmsnorm(oa_f + rev(oa_b), gnorm_a_w) * jax.nn.silu(ga.astype(f32))
    qk = jax.nn.silu(grid_dwconv(jnp.concatenate([qb, kb], axis=-1), conv_w, rows)).astype(f32)
    qb, kb = jnp.split(qk, 2, axis=-1)
    qb = qb.reshape(B, T, H_B, DK_B) * (DK_B ** -0.5)
    kb = kb.reshape(B, T, H_B, DK_B)
    vb = vb.astype(f32).reshape(B, T, H_B, DV_B)
    gates = gb.astype(f32).reshape(B, T, 4, H_B) + b_mgate.astype(f32)
    hb_f, C_f, n_f, m_f = mlstm_scan(qb, kb, vb, gates[:, :, 0], jax.nn.log_sigmoid(gates[:, :, 1]),
                                     C_m[:, 0], n_m[:, 0], m_m[:, 0])
    hb_b, C_b, n_b, m_b = mlstm_scan(rev(qb), rev(kb), rev(vb), rev(gates[:, :, 2]),
                                     rev(jax.nn.log_sigmoid(gates[:, :, 3])), C_m[:, 1], n_m[:, 1], m_m[:, 1])
    out_b = head_rmsnorm(hb_f + rev(hb_b), gnorm_b_w) * jax.nn.sigmoid(ob.astype(f32))
    out = jnp.concatenate([out_a, out_b], axis=-1).astype(h.dtype) @ w_out
    new_states = (jnp.stack([Sa_f, Sa_b], axis=1), jnp.stack([C_f, C_b], axis=1),
                  jnp.stack([n_f, n_b], axis=1), jnp.stack([m_f, m_b], axis=1))
    return out, new_states


def block(x, cond, rows, states, w_ada, b_ada, norm1_w, norm2_w, mix_params, w_ff1, w_ff2):
    mod = (jax.nn.silu(cond) @ w_ada + b_ada)[:, None, :]
    sh1, sc1, g1, sh2, sc2, g2 = jnp.split(mod.astype(x.dtype), 6, axis=-1)
    h = rmsnorm(x, norm1_w) * (1 + sc1) + sh1
    y, new_states = mixer(h, rows, *states, *mix_params)
    x = x + g1 * y
    h = rmsnorm(x, norm2_w) * (1 + sc2) + sh2
    x = x + g2 * (jnp.square(jax.nn.relu(h @ w_ff1)) @ w_ff2)
    return x, new_states


def setup_inputs(seed: int = 0) -> dict:
    key = jax.random.key(seed)
    ks = jax.random.split(key, 24)
    f32 = jnp.float32

    def nrm(k, shape, scale):
        return jax.random.normal(k, shape, f32) * scale

    gate_offset = jnp.array([0.0, 3.0, 0.0, 3.0], f32)[None, :, None]
    return {
        "x_prompt": nrm(ks[0], (BATCH, SEQ, D_MODEL), 1.0),
        "x_sample": nrm(ks[1], (DEC_BATCH, DEC_SEQ, D_MODEL), 1.0),
        "c": nrm(ks[2], (DEC_BATCH, D_MODEL), 1.0),
        "state_gla": nrm(ks[3], (DEC_BATCH, DEPTH, 2, H_A, DK_A, DV_A), 0.5),
        "state_mlstm_C": nrm(ks[4], (DEC_BATCH, DEPTH, 2, H_B, DK_B, DV_B), 0.5),
        "state_mlstm_n": jnp.abs(nrm(ks[5], (DEC_BATCH, DEPTH, 2, H_B, DK_B), 0.5)),
        "state_mlstm_m": nrm(ks[6], (DEC_BATCH, DEPTH, 2, H_B), 0.5),
        "c_ctx": nrm(ks[7], (D_MODEL,), 1.0),
        "w_ada": nrm(ks[8], (DEPTH, D_MODEL, 6 * D_MODEL), 0.5 * D_MODEL ** -0.5),
        "b_ada": nrm(ks[9], (DEPTH, 6 * D_MODEL), 0.02),
        "norm1_w": 1.0 + nrm(ks[10], (DEPTH, D_MODEL), 0.02),
        "norm2_w": 1.0 + nrm(ks[11], (DEPTH, D_MODEL), 0.02),
        "w_in": nrm(ks[12], (DEPTH, D_MODEL, D_IN), D_MODEL ** -0.5),
        "w_alpha2": nrm(ks[13], (DEPTH, 2, R_ALPHA, H_A * DK_A), R_ALPHA ** -0.5),
        "b_alpha": nrm(ks[14], (DEPTH, 2, H_A * DK_A), 0.1),
        "b_mgate": gate_offset + nrm(ks[15], (DEPTH, 4, H_B), 0.1),
        "conv_w": nrm(ks[16], (DEPTH, CONV_K, CONV_K, 2 * H_B * DK_B), 1.0 / CONV_K),
        "gnorm_a_w": 1.0 + nrm(ks[17], (DEPTH, D_A), 0.02),
        "gnorm_b_w": 1.0 + nrm(ks[18], (DEPTH, D_B), 0.02),
        "w_out": nrm(ks[19], (DEPTH, D_MIX, D_MODEL), D_MIX ** -0.5),
        "w_ff1": nrm(ks[20], (DEPTH, D_MODEL, D_FF), D_MODEL ** -0.5),
        "w_ff2": nrm(ks[21], (DEPTH, D_FF, D_MODEL), D_FF ** -0.5),
        "final_norm_w": 1.0 + nrm(ks[22], (D_MODEL,), 0.02),
    }


def reference(x_prompt, x_sample, c, state_gla, state_mlstm_C, state_mlstm_n, state_mlstm_m, c_ctx,
              w_ada, b_ada, norm1_w, norm2_w, w_in, w_alpha2, b_alpha, b_mgate, conv_w,
              gnorm_a_w, gnorm_b_w, w_out, w_ff1, w_ff2, final_norm_w):
    f32 = jnp.float32
    Bp = x_prompt.shape[0]
    rows_lat = x_sample.shape[1] // GRID_W
    zero_states = (jnp.zeros((Bp, 2, H_A, DK_A, DV_A), f32), jnp.zeros((Bp, 2, H_B, DK_B, DV_B), f32),
                   jnp.zeros((Bp, 2, H_B, DK_B), f32), jnp.zeros((Bp, 2, H_B), f32))
    xp = x_prompt
    xs = x_sample
    s_gla, s_C, s_n, s_m = [], [], [], []
    for l in range(DEPTH):
        mix_params = (w_in[l], w_alpha2[l], b_alpha[l], b_mgate[l], conv_w[l], gnorm_a_w[l], gnorm_b_w[l], w_out[l])
        xp, ctx_states = block(xp, c_ctx[None, :], 1, zero_states, w_ada[l], b_ada[l], norm1_w[l], norm2_w[l],
                               mix_params, w_ff1[l], w_ff2[l])
        s_gla.append(ctx_states[0]); s_C.append(ctx_states[1]); s_n.append(ctx_states[2]); s_m.append(ctx_states[3])
        cached = (state_gla[:, l], state_mlstm_C[:, l], state_mlstm_n[:, l], state_mlstm_m[:, l])
        xs, _ = block(xs, c, rows_lat, cached, w_ada[l], b_ada[l], norm1_w[l], norm2_w[l],
                      mix_params, w_ff1[l], w_ff2[l])
    y_prompt = rmsnorm(xp, final_norm_w)
    y_sample = rmsnorm(xs, final_norm_w)
    dt = x_prompt.dtype
    new_state_gla = jnp.stack(s_gla, axis=1).astype(dt)
    new_state_mlstm_C = jnp.stack(s_C, axis=1).astype(dt)
    new_state_mlstm_n = jnp.stack(s_n, axis=1).astype(dt)
    new_state_mlstm_m = jnp.stack(s_m, axis=1).astype(dt)
    return (y_prompt, y_sample, new_state_gla, new_state_mlstm_C, new_state_mlstm_n, new_state_mlstm_m)
```

```python
import functools

import jax
import jax.numpy as jnp
from jax import lax
from jax.experimental import pallas as pl
from jax.experimental.pallas import tpu as pltpu

F32 = jnp.float32
BF16 = jnp.bfloat16

GRID_W = 64
H_A = 4
H_B = 4
R_ALPHA = 16
TAU_GLA = 16.0
CHUNK = 64
EPS = 1e-6
LANES = 128
COND_ROWS = 8
SMALL_W = LANES
GATE_LANE0 = 2 * R_ALPHA
VMEM_LIMIT = 56 * 1024 * 1024


def _sigmoid(x):
    return 1.0 / (1.0 + jnp.exp(-x))


def _silu(x):
    return x * _sigmoid(x)


def _log_sigmoid(x):
    return jnp.minimum(x, 0.0) - jnp.log1p(jnp.exp(-jnp.abs(x)))


def _dot(a, b):
    return jnp.dot(a, b, preferred_element_type=F32)


def _dot_nt(a, b):
    return lax.dot_general(a, b, (((1,), (1,)), ((), ())), preferred_element_type=F32)


def _rms(x, w):
    return x * lax.rsqrt(jnp.mean(x * x, axis=-1, keepdims=True) + EPS) * w


def _tri_sum(tri, x):
    hi = x.astype(BF16)
    r1 = x - hi.astype(F32)
    mid = r1.astype(BF16)
    lo = (r1 - mid.astype(F32)).astype(BF16)
    return _dot(tri, hi) + _dot(tri, mid) + _dot(tri, lo)


def _chunk_masks(L):
    row = lax.broadcasted_iota(jnp.int32, (L, L), 0)
    col = lax.broadcasted_iota(jnp.int32, (L, L), 1)
    lower = row >= col
    upper = row <= col
    return lower, upper


def _ada_kernel(c_ref, w_ref, b_ref, o_ref):
    s = _silu(c_ref[...])
    o_ref[...] = _dot(s.astype(BF16), w_ref[...].astype(BF16)) + b_ref[...]


def _ada_call(cond, w_ada, b_ada):
    D = cond.shape[1]
    n_out = w_ada.shape[1]
    tn = 1024
    return pl.pallas_call(
        _ada_kernel,
        grid=(n_out // tn,),
        in_specs=[
            pl.BlockSpec((COND_ROWS, D), lambda j: (0, 0)),
            pl.BlockSpec((D, tn), lambda j: (0, j)),
            pl.BlockSpec((1, tn), lambda j: (0, j)),
        ],
        out_specs=pl.BlockSpec((COND_ROWS, tn), lambda j: (0, j)),
        out_shape=jax.ShapeDtypeStruct((COND_ROWS, n_out), F32),
        compiler_params=pltpu.CompilerParams(dimension_semantics=("arbitrary",)),
        name="ada_mod",
    )(cond, w_ada, b_ada.reshape(1, n_out))


def _mod_row(mod_row0, tiles_per_batch):
    if tiles_per_batch is None:
        return mod_row0
    return mod_row0 + pl.program_id(0) // tiles_per_batch


def _inproj_kernel(x_ref, mod_ref, nw_ref, w_ref, z_ref, *, mod_row0, tiles_per_batch):
    D = x_ref.shape[1]
    row = _mod_row(mod_row0, tiles_per_batch)
    sh1 = mod_ref[pl.ds(row, 1), 0:D]
    sc1 = mod_ref[pl.ds(row, 1), D:2 * D]
    h = _rms(x_ref[...], nw_ref[...]) * (1.0 + sc1) + sh1
    z_ref[...] = _dot(h.astype(BF16), w_ref[...])


def _inproj_call(x2d, mod, norm_w, w_in_p, *, tm, mod_row0, tiles_per_batch):
    M, D = x2d.shape
    n_out = w_in_p.shape[1]
    kern = functools.partial(_inproj_kernel, mod_row0=mod_row0, tiles_per_batch=tiles_per_batch)
    return pl.pallas_call(
        kern,
        grid=(M // tm,),
        in_specs=[
            pl.BlockSpec((tm, D), lambda i: (i, 0)),
            pl.BlockSpec(mod.shape, lambda i: (0, 0)),
            pl.BlockSpec((1, D), lambda i: (0, 0)),
            pl.BlockSpec((D, n_out), lambda i: (0, 0), pipeline_mode=pl.Buffered(1)),
        ],
        out_specs=pl.BlockSpec((tm, n_out), lambda i: (i, 0)),
        out_shape=jax.ShapeDtypeStruct((M, n_out), F32),
        compiler_params=pltpu.CompilerParams(dimension_semantics=("arbitrary",),
                                             vmem_limit_bytes=VMEM_LIMIT),
        name="norm_inproj",
    )(x2d, mod, norm_w.reshape(1, D), w_in_p)


def _gla_kernel(*refs, has_state, write_state):
    refs = list(refs)
    q_ref, k_ref, v_ref, g_ref, sm_ref = refs[:5]
    del refs[:5]
    s0_ref = refs.pop(0) if has_state else None
    wal_ref, bal_ref, gw_ref, out_ref = refs[:4]
    del refs[:4]
    snew_ref = refs.pop(0) if write_state else None
    of_scr, ob_scr, st_scr = refs
    T = q_ref.shape[0]
    L = CHUNK
    N = T // L
    HK = q_ref.shape[1]
    DK = HK // H_A
    DV = v_ref.shape[1] // H_A
    scale = DK ** -0.5
    n_pairs = HK // LANES

    lower, upper = _chunk_masks(L)
    tri = (lower.astype(BF16), upper.astype(BF16))
    tmask = (lower, upper)
    lane = lax.broadcasted_iota(jnp.int32, (1, LANES), 1)
    head_mask = (lane < DK, lane >= DK)
    o_scr = (of_scr, ob_scr)

    for d in range(2):
        for p in range(n_pairs):
            if has_state:
                st_scr[d, p] = s0_ref[d, p].T
            else:
                st_scr[d, p] = jnp.zeros((LANES, LANES), F32)

    def do_chunk(d, r0):
        rows = pl.ds(r0, L)
        pre = _dot(sm_ref[rows, :].astype(BF16), wal_ref[:, d * HK:(d + 1) * HK]) + bal_ref[:, d * HK:(d + 1) * HK]
        g = _log_sigmoid(pre) * (1.0 / TAU_GLA)
        b = _tri_sum(tri[d], g)
        bend = b[L - 1:L, :] if d == 0 else b[0:1, :]
        q = q_ref[rows, :] * scale
        k = k_ref[rows, :]
        qh = q * jnp.exp(b - bend)
        ks = k * jnp.exp(bend - b)
        qs = q * jnp.exp(b)
        dec = jnp.exp(bend)
        for p in range(n_pairs):
            ls = slice(p * LANES, (p + 1) * LANES)
            kp = ks[:, ls].astype(BF16)
            st = st_scr[d, p]
            st_b = st.astype(BF16)
            upd = []
            for j in range(2):
                h = 2 * p + j
                vs = slice(h * DV, (h + 1) * DV)
                qm = jnp.where(head_mask[j], qh[:, ls], 0.0).astype(BF16)
                qsm = jnp.where(head_mask[j], qs[:, ls], 0.0).astype(BF16)
                a = jnp.where(tmask[d], _dot_nt(qm, kp), 0.0)
                v = v_ref[rows, vs]
                o_scr[d][rows, vs] = _dot(a.astype(BF16), v.astype(BF16)) + _dot_nt(qsm, st_b)
                upd.append(_dot(v.T.astype(BF16), kp))
            st_scr[d, p] = st * dec[:, ls] + jnp.where(head_mask[0], upd[0], upd[1])

    def body(c, carry):
        do_chunk(0, pl.multiple_of(c * L, L))
        do_chunk(1, pl.multiple_of((N - 1 - c) * L, L))
        return carry

    lax.fori_loop(0, N, body, 0)

    def epilogue(i, carry):
        rows = pl.ds(pl.multiple_of(i * L, L), L)
        for h in range(H_A):
            vs = slice(h * DV, (h + 1) * DV)
            o = of_scr[rows, vs] + ob_scr[rows, vs]
            out_ref[rows, vs] = _rms(o, gw_ref[:, vs]) * _silu(g_ref[rows, vs])
        return carry

    lax.fori_loop(0, N, epilogue, 0)

    if write_state:
        for d in range(2):
            for p in range(n_pairs):
                snew_ref[d, p] = st_scr[d, p].T


def _gla_call(z3, s0, wal_p, bal_p, gw, *, has_state, write_state):
    B, T, _ = z3.shape
    HK = wal_p.shape[1] // 2
    DA = gw.shape[0]
    n_pairs = HK // LANES
    small_blk = (z3.shape[2] - SMALL_W) // SMALL_W
    kern = functools.partial(_gla_kernel, has_state=has_state, write_state=write_state)
    state_shape = (B, 2, n_pairs, LANES, LANES)
    state_spec = pl.BlockSpec((None, 2, n_pairs, LANES, LANES), lambda b: (b, 0, 0, 0, 0))
    in_specs = [
        pl.BlockSpec((None, T, HK), lambda b: (b, 0, 0)),
        pl.BlockSpec((None, T, HK), lambda b: (b, 0, 1)),
        pl.BlockSpec((None, T, DA), lambda b: (b, 0, 1)),
        pl.BlockSpec((None, T, DA), lambda b: (b, 0, 2)),
        pl.BlockSpec((None, T, SMALL_W), lambda b: (b, 0, small_blk)),
    ]
    args = [z3, z3, z3, z3, z3]
    if has_state:
        in_specs.append(state_spec)
        args.append(s0)
    in_specs += [
        pl.BlockSpec(wal_p.shape, lambda b: (0, 0)),
        pl.BlockSpec(bal_p.shape, lambda b: (0, 0)),
        pl.BlockSpec((1, DA), lambda b: (0, 0)),
    ]
    args += [wal_p, bal_p, gw.reshape(1, DA)]
    out_specs = [pl.BlockSpec((None, T, DA), lambda b: (b, 0, 0))]
    out_shape = [jax.ShapeDtypeStruct((B, T, DA), F32)]
    if write_state:
        out_specs.append(state_spec)
        out_shape.append(jax.ShapeDtypeStruct(state_shape, F32))
    return pl.pallas_call(
        kern,
        grid=(B,),
        in_specs=in_specs,
        out_specs=out_specs,
        out_shape=out_shape,
        scratch_shapes=[
            pltpu.VMEM((T, DA), F32),
            pltpu.VMEM((T, DA), F32),
            pltpu.VMEM((2, n_pairs, LANES, LANES), F32),
        ],
        compiler_params=pltpu.CompilerParams(dimension_semantics=("arbitrary",),
                                             vmem_limit_bytes=VMEM_LIMIT),
        name="gla_scan",
    )(*args)


def _mlstm_kernel(*refs, grid_w, has_state, write_state):
    refs = list(refs)
    qk_ref, v_ref, og_ref, sm_ref = refs[:4]
    del refs[:4]
    if has_state:
        c0_ref, n0_ref, m0_ref = refs[:3]
        del refs[:3]
    cw_ref, bm_ref, gw_ref, out_ref = refs[:4]
    del refs[:4]
    if write_state:
        cnew_ref, nnew_ref, mnew_ref = refs[:3]
        del refs[:3]
    pad_scr, qk_scr, y_scr, hf_scr, hb_scr, c_scr, n_scr, m_scr = refs
    T = qk_ref.shape[0]
    L = CHUNK
    N = T // L
    C2 = qk_ref.shape[1]
    HK = C2 // 2
    DK = HK // H_B
    DV = v_ref.shape[1] // H_B
    scale = DK ** -0.5
    n_pairs = HK // LANES
    P = pad_scr.shape[0] - T
    P0 = P // 2
    rows_img = T // grid_w

    lower, upper = _chunk_masks(L)
    tri = (lower.astype(BF16), upper.astype(BF16))
    tmask = (lower, upper)
    lane = lax.broadcasted_iota(jnp.int32, (1, LANES), 1)
    head_mask = (lane < DK, lane >= DK)
    h_scr = (hf_scr, hb_scr)

    for d in range(2):
        for p in range(n_pairs):
            if has_state:
                c_scr[d, p] = c0_ref[d, p]
                n_scr[2 * d + p:2 * d + p + 1, :] = n0_ref[d, p:p + 1, :]
            else:
                c_scr[d, p] = jnp.zeros((LANES, LANES), F32)
                n_scr[2 * d + p:2 * d + p + 1, :] = jnp.zeros((1, LANES), F32)
    if has_state:
        m_scr[0:2, 0:H_B] = m0_ref[...]
    else:
        m_scr[0:2, 0:H_B] = jnp.zeros((2, H_B), F32)

    pad_scr[0:P0, :] = jnp.zeros((P0, C2), F32)
    pad_scr[P0 + T:P + T, :] = jnp.zeros((P - P0, C2), F32)

    def copy_in(i, carry):
        r0 = pl.multiple_of(i * L, L)
        pad_scr[pl.ds(P0 + r0, L), :] = qk_ref[pl.ds(r0, L), :]
        return carry

    lax.fori_loop(0, N, copy_in, 0)

    lane_c = lax.broadcasted_iota(jnp.int32, (1, C2), 1)
    qscale = jnp.where(lane_c < HK, scale, 1.0).astype(F32)
    sub = lax.broadcasted_iota(jnp.int32, (L, 1), 0)
    img_rows = (0,) if rows_img == 1 else (-1, 0, 1)

    def conv_tile(i, carry):
        r0 = pl.multiple_of(i * L, L)
        col = lax.rem(r0, grid_w) + sub
        ok_left = col >= 1
        ok_right = col <= grid_w - 2
        acc = jnp.zeros((L, C2), F32)
        for di in img_rows:
            blk = pad_scr[pl.ds(P0 + r0 + di * grid_w - 8, L + 16), :]
            left = jnp.where(ok_left, blk[7:7 + L, :], 0.0)
            mid = blk[8:8 + L, :]
            right = jnp.where(ok_right, blk[9:9 + L, :], 0.0)
            wr = 3 * (di + 1)
            acc = acc + left * cw_ref[wr:wr + 1, :] + mid * cw_ref[wr + 1:wr + 2, :] + right * cw_ref[wr + 2:wr + 3, :]
        qk_scr[pl.ds(r0, L), :] = _silu(acc) * qscale
        return carry

    lax.fori_loop(0, N, conv_tile, 0)

    gl = lane - GATE_LANE0
    is_f = ((gl >= H_B) & (gl < 2 * H_B)) | ((gl >= 3 * H_B) & (gl < 4 * H_B))

    def gate_tile(i, carry):
        rows = pl.ds(pl.multiple_of(i * L, L), L)
        x = sm_ref[rows, :] + bm_ref[...]
        y_scr[rows, :] = jnp.where(is_f, _log_sigmoid(x), x)
        return carry

    lax.fori_loop(0, N, gate_tile, 0)

    def do_chunk(d, r0):
        rows = pl.ds(r0, L)
        x = y_scr[rows, :]
        y = jnp.where(is_f, _tri_sum(tri[d], x), x)
        yt = y.T
        e_row = L - 1 if d == 0 else 0
        for p in range(n_pairs):
            ls = slice(p * LANES, (p + 1) * LANES)
            qp = qk_scr[rows, ls]
            kp = qk_scr[rows, HK + p * LANES:HK + (p + 1) * LANES]
            kpb = kp.astype(BF16)
            cst = c_scr[d, p]
            cst_b = cst.astype(BF16)
            npair = n_scr[2 * d + p:2 * d + p + 1, :]
            wks, a_s = [], []
            for j in range(2):
                h = 2 * p + j
                vs = slice(h * DV, (h + 1) * DV)
                li = GATE_LANE0 + 2 * H_B * d + h
                lf = li + H_B
                f_col, i_col = y[:, lf:lf + 1], y[:, li:li + 1]
                f_row, i_row = yt[lf:lf + 1, :], yt[li:li + 1, :]
                m_prev = m_scr[d:d + 1, h:h + 1]
                dmat = jnp.where(tmask[d], f_col - f_row + i_row, -jnp.inf)
                inter = f_col + m_prev
                mt = jnp.maximum(inter, jnp.max(dmat, axis=-1, keepdims=True))
                w_inter = jnp.exp(inter - mt)
                qm = jnp.where(head_mask[j], qp, 0.0)
                qmb = qm.astype(BF16)
                s = _dot_nt(qmb, kpb) * jnp.exp(dmat - mt)
                vh = v_ref[rows, vs].astype(BF16)
                num = _dot(s.astype(BF16), vh) + w_inter * _dot(qmb, cst_b)
                den = jnp.sum(s, axis=-1, keepdims=True) + w_inter * jnp.sum(qm * npair, axis=-1, keepdims=True)
                h_scr[d][rows, vs] = num / jnp.maximum(jnp.abs(den), jnp.exp(-mt))
                f_end = f_col[e_row:e_row + 1, :]
                m_new = mt[e_row:e_row + 1, :]
                a_s.append(jnp.exp(f_end + m_prev - m_new))
                wks.append(jnp.exp(f_end - f_col + i_col - m_new))
                m_scr[d:d + 1, h:h + 1] = m_new
            kw = kp * jnp.where(head_mask[0], wks[0], wks[1])
            kwt = kw.T.astype(BF16)
            for j in range(2):
                h = 2 * p + j
                vh = v_ref[rows, h * DV:(h + 1) * DV].astype(BF16)
                c_scr[d, p, j * DK:(j + 1) * DK, :] = (a_s[j] * cst[j * DK:(j + 1) * DK, :]
                                                      + _dot(kwt[j * DK:(j + 1) * DK, :], vh))
            n_scr[2 * d + p:2 * d + p + 1, :] = (jnp.where(head_mask[0], a_s[0], a_s[1]) * npair
                                                 + jnp.sum(kw, axis=0, keepdims=True))

    def body(c, carry):
        do_chunk(0, pl.multiple_of(c * L, L))
        do_chunk(1, pl.multiple_of((N - 1 - c) * L, L))
        return carry

    lax.fori_loop(0, N, body, 0)

    def epilogue(i, carry):
        rows = pl.ds(pl.multiple_of(i * L, L), L)
        for h in range(H_B):
            vs = slice(h * DV, (h + 1) * DV)
            o = hf_scr[rows, vs] + hb_scr[rows, vs]
            out_ref[rows, vs] = _rms(o, gw_ref[:, vs]) * _sigmoid(og_ref[rows, vs])
        return carry

    lax.fori_loop(0, N, epilogue, 0)

    if write_state:
        for d in range(2):
            for p in range(n_pairs):
                cnew_ref[d, p] = c_scr[d, p]
                nnew_ref[d, p:p + 1, :] = n_scr[2 * d + p:2 * d + p + 1, :]
        mnew_ref[...] = m_scr[0:2, 0:H_B]


def _mlstm_call(z3, c0, n0, m0, conv9, bm_row, gw, *, grid_w, has_state, write_state):
    B, T, _ = z3.shape
    C2 = conv9.shape[1]
    HK = C2 // 2
    DB = gw.shape[0]
    n_pairs = HK // LANES
    small_blk = (z3.shape[2] - SMALL_W) // SMALL_W
    qk_blk = (3 * DB) // C2
    v_blk = (3 * DB + C2) // DB
    pad_rows = 2 * (grid_w + 8) if T // grid_w > 1 else 16
    kern = functools.partial(_mlstm_kernel, grid_w=grid_w, has_state=has_state, write_state=write_state)
    c_spec = pl.BlockSpec((None, 2, n_pairs, LANES, LANES), lambda b: (b, 0, 0, 0, 0))
    n_spec = pl.BlockSpec((None, 2, n_pairs, LANES), lambda b: (b, 0, 0, 0))
    m_spec = pl.BlockSpec((None, 2, H_B), lambda b: (b, 0, 0))
    in_specs = [
        pl.BlockSpec((None, T, C2), lambda b: (b, 0, qk_blk)),
        pl.BlockSpec((None, T, DB), lambda b: (b, 0, v_blk)),
        pl.BlockSpec((None, T, DB), lambda b: (b, 0, v_blk + 1)),
        pl.BlockSpec((None, T, SMALL_W), lambda b: (b, 0, small_blk)),
    ]
    args = [z3, z3, z3, z3]
    if has_state:
        in_specs += [c_spec, n_spec, m_spec]
        args += [c0, n0, m0]
    in_specs += [
        pl.BlockSpec(conv9.shape, lambda b: (0, 0)),
        pl.BlockSpec((1, SMALL_W), lambda b: (0, 0)),
        pl.BlockSpec((1, DB), lambda b: (0, 0)),
    ]
    args += [conv9, bm_row, gw.reshape(1, DB)]
    out_specs = [pl.BlockSpec((None, T, DB), lambda b: (b, 0, 0))]
    out_shape = [jax.ShapeDtypeStruct((B, T, DB), F32)]
    if write_state:
        out_specs += [c_spec, n_spec, m_spec]
        out_shape += [
            jax.ShapeDtypeStruct((B, 2, n_pairs, LANES, LANES), F32),
            jax.ShapeDtypeStruct((B, 2, n_pairs, LANES), F32),
            jax.ShapeDtypeStruct((B, 2, H_B), F32),
        ]
    return pl.pallas_call(
        kern,
        grid=(B,),
        in_specs=in_specs,
        out_specs=out_specs,
        out_shape=out_shape,
        scratch_shapes=[
            pltpu.VMEM((T + pad_rows, C2), F32),
            pltpu.VMEM((T, C2), F32),
            pltpu.VMEM((T, SMALL_W), F32),
            pltpu.VMEM((T, DB), F32),
            pltpu.VMEM((T, DB), F32),
            pltpu.VMEM((2, n_pairs, LANES, LANES), F32),
            pltpu.VMEM((8, LANES), F32),
            pltpu.VMEM((8, LANES), F32),
        ],
        compiler_params=pltpu.CompilerParams(dimension_semantics=("arbitrary",),
                                             vmem_limit_bytes=VMEM_LIMIT),
        name="mlstm_scan",
    )(*args)


def _outff_kernel(x_ref, a_ref, b_ref, mod_ref, n2_ref, fn_ref, wo_ref, w1_ref, w2_ref, y_ref,
                  *, mod_row0, tiles_per_batch, ff_chunk, final_norm):
    D = x_ref.shape[1]
    DA = a_ref.shape[1]
    row = _mod_row(mod_row0, tiles_per_batch)

    def mod(k):
        return mod_ref[pl.ds(row, 1), k * D:(k + 1) * D]

    y = _dot(a_ref[...].astype(BF16), wo_ref[0:DA, :]) + _dot(b_ref[...].astype(BF16), wo_ref[DA:, :])
    x1 = x_ref[...] + mod(2) * y
    h2 = (_rms(x1, n2_ref[...]) * (1.0 + mod(4)) + mod(3)).astype(BF16)
    acc = jnp.zeros(x1.shape, F32)
    for c0 in range(0, w1_ref.shape[1], ff_chunk):
        u = jnp.maximum(_dot(h2, w1_ref[:, c0:c0 + ff_chunk]), 0.0)
        acc = acc + _dot((u * u).astype(BF16), w2_ref[c0:c0 + ff_chunk, :])
    x2 = x1 + mod(5) * acc
    y_ref[...] = _rms(x2, fn_ref[...]) if final_norm else x2


def _outff_call(x2d, a2d, b2d, mod, norm2_w, final_w, wo, w1, w2, *, tm, mod_row0, tiles_per_batch,
                final_norm):
    M, D = x2d.shape
    DA = a2d.shape[1]
    DFF = w1.shape[1]
    kern = functools.partial(_outff_kernel, mod_row0=mod_row0, tiles_per_batch=tiles_per_batch,
                             ff_chunk=512, final_norm=final_norm)
    once = pl.Buffered(1)
    return pl.pallas_call(
        kern,
        grid=(M // tm,),
        in_specs=[
            pl.BlockSpec((tm, D), lambda i: (i, 0)),
            pl.BlockSpec((tm, DA), lambda i: (i, 0)),
            pl.BlockSpec((tm, D - DA), lambda i: (i, 0)),
            pl.BlockSpec(mod.shape, lambda i: (0, 0)),
            pl.BlockSpec((1, D), lambda i: (0, 0)),
            pl.BlockSpec((1, D), lambda i: (0, 0)),
            pl.BlockSpec((D, D), lambda i: (0, 0), pipeline_mode=once),
            pl.BlockSpec((D, DFF), lambda i: (0, 0), pipeline_mode=once),
            pl.BlockSpec((DFF, D), lambda i: (0, 0), pipeline_mode=once),
        ],
        out_specs=pl.BlockSpec((tm, D), lambda i: (i, 0)),
        out_shape=jax.ShapeDtypeStruct((M, D), F32),
        compiler_params=pltpu.CompilerParams(dimension_semantics=("arbitrary",),
                                             vmem_limit_bytes=VMEM_LIMIT),
        name="outproj_mlp",
    )(x2d, a2d, b2d, mod, norm2_w.reshape(1, D), final_w.reshape(1, D), wo, w1, w2)


def _block(x, mod, mod_row0, per_batch, grid_w, states, lw, final_w, final_norm, write_state):
    B, T, D = x.shape
    tm = 512 if T % 512 == 0 else T
    tiles_per_batch = (T // tm) if per_batch else None
    x2d = x.reshape(B * T, D)
    z = _inproj_call(x2d, mod, lw["norm1_w"], lw["w_in_p"], tm=tm, mod_row0=mod_row0,
                     tiles_per_batch=tiles_per_batch)
    z3 = z.reshape(B, T, z.shape[1])
    has_state = states is not None
    n_pairs_a = lw["wal_p"].shape[1] // 2 // LANES
    n_pairs_b = lw["conv9"].shape[1] // 2 // LANES
    s_gla = s_c = s_n = s_m = None
    if has_state:
        s_gla, s_c, s_n, s_m = states
        s_gla = s_gla.reshape(B, 2, n_pairs_a, LANES, LANES)
        s_c = s_c.reshape(B, 2, n_pairs_b, LANES, LANES)
        s_n = s_n.reshape(B, 2, n_pairs_b, LANES)
    res_a = _gla_call(z3, s_gla, lw["wal_p"], lw["bal_p"], lw["gnorm_a_w"],
                      has_state=has_state, write_state=write_state)
    res_b = _mlstm_call(z3, s_c, s_n, s_m, lw["conv9"], lw["bm_row"], lw["gnorm_b_w"],
                        grid_w=grid_w, has_state=has_state, write_state=write_state)
    out_a, out_b = res_a[0], res_b[0]
    new_states = (res_a[1], res_b[1], res_b[2], res_b[3]) if write_state else None
    y = _outff_call(x2d, out_a.reshape(B * T, -1), out_b.reshape(B * T, -1), mod, lw["norm2_w"], final_w,
                    lw["w_out"], lw["w_ff1"], lw["w_ff2"], tm=tm, mod_row0=mod_row0,
                    tiles_per_batch=tiles_per_batch, final_norm=final_norm)
    return y.reshape(B, T, D), new_states


def _layer_weights(l, norm1_w, norm2_w, w_in, w_alpha2, b_alpha, b_mgate, conv_w, gnorm_a_w, gnorm_b_w,
                   w_out, w_ff1, w_ff2):
    D = w_in.shape[1]
    hk_a = w_alpha2.shape[-1]
    d_a = gnorm_a_w.shape[-1]
    d_b = gnorm_b_w.shape[-1]
    hk_b = conv_w.shape[-1] // 2
    sizes = (hk_a, hk_a, d_a, d_a, 2 * R_ALPHA, hk_b, hk_b, d_b, d_b, 4 * H_B)
    offs = [0]
    for s in sizes:
        offs.append(offs[-1] + s)
    w = w_in[l]
    big = [w[:, offs[i]:offs[i + 1]] for i in (0, 1, 2, 3, 5, 6, 7, 8)]
    small = [w[:, offs[4]:offs[5]], w[:, offs[9]:offs[10]],
             jnp.zeros((D, SMALL_W - 2 * R_ALPHA - 4 * H_B), w.dtype)]
    w_in_p = jnp.concatenate(big + small, axis=1).astype(BF16)
    wal = w_alpha2[l]
    wal_p = jnp.zeros((SMALL_W, 2 * hk_a), F32)
    wal_p = wal_p.at[0:R_ALPHA, 0:hk_a].set(wal[0]).at[R_ALPHA:2 * R_ALPHA, hk_a:].set(wal[1]).astype(BF16)
    bm_row = jnp.zeros((1, SMALL_W), F32).at[0, GATE_LANE0:GATE_LANE0 + 4 * H_B].set(b_mgate[l].reshape(-1))
    return dict(
        norm1_w=norm1_w[l], norm2_w=norm2_w[l], w_in_p=w_in_p, wal_p=wal_p,
        bal_p=b_alpha[l].reshape(1, -1), bm_row=bm_row,
        conv9=conv_w[l].reshape(-1, conv_w.shape[-1]),
        gnorm_a_w=gnorm_a_w[l], gnorm_b_w=gnorm_b_w[l],
        w_out=w_out[l].astype(BF16), w_ff1=w_ff1[l].astype(BF16), w_ff2=w_ff2[l].astype(BF16),
    )


def kernel(x_prompt, x_sample, c, state_gla, state_mlstm_C, state_mlstm_n, state_mlstm_m, c_ctx, w_ada, b_ada, norm1_w, norm2_w, w_in, w_alpha2, b_alpha, b_mgate, conv_w, gnorm_a_w, gnorm_b_w, w_out, w_ff1, w_ff2, final_norm_w):
    depth = w_in.shape[0]
    D = x_prompt.shape[-1]
    Bp, Tp, _ = x_prompt.shape
    Bs = x_sample.shape[0]
    assert 1 + Bs <= COND_ROWS
    cond = jnp.concatenate([c_ctx[None, :], c, jnp.zeros((COND_ROWS - 1 - Bs, D), F32)], axis=0)
    xp, xs = x_prompt, x_sample
    s_gla, s_c, s_n, s_m = [], [], [], []
    for l in range(depth):
        lw = _layer_weights(l, norm1_w, norm2_w, w_in, w_alpha2, b_alpha, b_mgate, conv_w,
                            gnorm_a_w, gnorm_b_w, w_out, w_ff1, w_ff2)
        mod = _ada_call(cond, w_ada[l], b_ada[l])
        last = l == depth - 1
        xp, ctx = _block(xp, mod, 0, False, Tp, None, lw, final_norm_w, last, True)
        s_gla.append(ctx[0].reshape(Bp, 2, H_A, -1, ctx[0].shape[-1]))
        s_c.append(ctx[1].reshape(Bp, 2, H_B, -1, ctx[1].shape[-1]))
        s_n.append(ctx[2].reshape(Bp, 2, H_B, -1))
        s_m.append(ctx[3])
        cached = (state_gla[:, l], state_mlstm_C[:, l], state_mlstm_n[:, l], state_mlstm_m[:, l])
        xs, _ = _block(xs, mod, 1, True, GRID_W, cached, lw, final_norm_w, last, False)
    dt = x_prompt.dtype
    return (xp, xs, jnp.stack(s_gla, axis=1).astype(dt), jnp.stack(s_c, axis=1).astype(dt),
            jnp.stack(s_n, axis=1).astype(dt), jnp.stack(s_m, axis=1).astype(dt))
```

```python
import functools

import jax
import jax.numpy as jnp
from jax import lax
from jax.experimental import pallas as pl
from jax.experimental.pallas import tpu as pltpu

F32 = jnp.float32
BF16 = jnp.bfloat16

GRID_W = 64
H_A = 4
H_B = 4
R_ALPHA = 16
TAU_GLA = 16.0
CHUNK = 64
EPS = 1e-6
LANES = 128
COND_ROWS = 8
SMALL_W = LANES
GATE_LANE0 = 2 * R_ALPHA
VMEM_LIMIT = 56 * 1024 * 1024
SCAN_UNROLL = 4


def _sigmoid(x):
    return 1.0 / (1.0 + jnp.exp(-x))


def _silu(x):
    return x * _sigmoid(x)


def _log_sigmoid(x):
    return jnp.minimum(x, 0.0) - jnp.log1p(jnp.exp(-jnp.abs(x)))


def _dot(a, b):
    return jnp.dot(a, b, preferred_element_type=F32)


def _dot_nt(a, b):
    return lax.dot_general(a, b, (((1,), (1,)), ((), ())), preferred_element_type=F32)


def _rms(x, w):
    return x * lax.rsqrt(jnp.mean(x * x, axis=-1, keepdims=True) + EPS) * w


def _tri_sum(tri, x):
    hi = x.astype(BF16)
    r1 = x - hi.astype(F32)
    mid = r1.astype(BF16)
    lo = (r1 - mid.astype(F32)).astype(BF16)
    return _dot(tri, hi) + _dot(tri, mid) + _dot(tri, lo)


def _chunk_masks(L):
    row = lax.broadcasted_iota(jnp.int32, (L, L), 0)
    col = lax.broadcasted_iota(jnp.int32, (L, L), 1)
    lower = row >= col
    upper = row <= col
    return lower, upper


def _ada_kernel(c_ref, w_ref, b_ref, o_ref):
    s = _silu(c_ref[...])
    o_ref[...] = _dot(s.astype(BF16), w_ref[...].astype(BF16)) + b_ref[...]


def _ada_call(cond, w_ada, b_ada):
    D = cond.shape[1]
    n_out = w_ada.shape[1]
    tn = 1024
    return pl.pallas_call(
        _ada_kernel,
        grid=(n_out // tn,),
        in_specs=[
            pl.BlockSpec((COND_ROWS, D), lambda j: (0, 0)),
            pl.BlockSpec((D, tn), lambda j: (0, j)),
            pl.BlockSpec((1, tn), lambda j: (0, j)),
        ],
        out_specs=pl.BlockSpec((COND_ROWS, tn), lambda j: (0, j)),
        out_shape=jax.ShapeDtypeStruct((COND_ROWS, n_out), F32),
        compiler_params=pltpu.CompilerParams(dimension_semantics=("arbitrary",)),
        name="ada_mod",
    )(cond, w_ada, b_ada.reshape(1, n_out))


def _mod_row(mod_row0, tiles_per_batch):
    if tiles_per_batch is None:
        return mod_row0
    return mod_row0 + pl.program_id(0) // tiles_per_batch


def _inproj_kernel(x_ref, mod_ref, nw_ref, w_ref, z_ref, *, mod_row0, tiles_per_batch):
    D = x_ref.shape[1]
    row = _mod_row(mod_row0, tiles_per_batch)
    sh1 = mod_ref[pl.ds(row, 1), 0:D]
    sc1 = mod_ref[pl.ds(row, 1), D:2 * D]
    h = _rms(x_ref[...], nw_ref[...]) * (1.0 + sc1) + sh1
    z_ref[...] = _dot(h.astype(BF16), w_ref[...])


def _inproj_call(x2d, mod, norm_w, w_in_p, *, tm, mod_row0, tiles_per_batch):
    M, D = x2d.shape
    n_out = w_in_p.shape[1]
    kern = functools.partial(_inproj_kernel, mod_row0=mod_row0, tiles_per_batch=tiles_per_batch)
    return pl.pallas_call(
        kern,
        grid=(M // tm,),
        in_specs=[
            pl.BlockSpec((tm, D), lambda i: (i, 0)),
            pl.BlockSpec(mod.shape, lambda i: (0, 0)),
            pl.BlockSpec((1, D), lambda i: (0, 0)),
            pl.BlockSpec((D, n_out), lambda i: (0, 0), pipeline_mode=pl.Buffered(1)),
        ],
        out_specs=pl.BlockSpec((tm, n_out), lambda i: (i, 0)),
        out_shape=jax.ShapeDtypeStruct((M, n_out), F32),
        compiler_params=pltpu.CompilerParams(dimension_semantics=("arbitrary",),
                                             vmem_limit_bytes=VMEM_LIMIT),
        name="norm_inproj",
    )(x2d, mod, norm_w.reshape(1, D), w_in_p)


def _chunk_loop(n_chunks, unroll, fn):
    if unroll >= n_chunks:
        fn(list(range(n_chunks)))
        return

    def body(i, carry):
        fn([i * unroll + u for u in range(unroll)])
        return carry

    lax.fori_loop(0, n_chunks // unroll, body, 0)


def _chunk_rows(n):
    if isinstance(n, int):
        return pl.ds(n * CHUNK, CHUNK)
    return pl.ds(pl.multiple_of(n * CHUNK, CHUNK), CHUNK)


def _gla_kernel(*refs, has_state, write_state, unroll):
    refs = list(refs)
    q_ref, k_ref, v_ref, g_ref, sm_ref = refs[:5]
    del refs[:5]
    s0_ref = refs.pop(0) if has_state else None
    wal_ref, bal_ref, gw_ref, out_ref = refs[:4]
    del refs[:4]
    snew_ref = refs.pop(0) if write_state else None
    of_scr, ob_scr, st_scr, sall_scr, qh_scr, qs_scr, kh_scr = refs
    T = q_ref.shape[0]
    L = CHUNK
    N = T // L
    HK = q_ref.shape[1]
    DK = HK // H_A
    DV = v_ref.shape[1] // H_A
    scale = DK ** -0.5
    n_pairs = HK // LANES

    lower, upper = _chunk_masks(L)
    tri = (lower.astype(BF16), upper.astype(BF16))
    tmask = (lower, upper)
    lane = lax.broadcasted_iota(jnp.int32, (1, LANES), 1)
    head_mask = (lane < DK, lane >= DK)
    o_scr = (of_scr, ob_scr)

    for d in range(2):
        for p in range(n_pairs):
            if has_state:
                st_scr[d, p] = s0_ref[d, p].T
            else:
                st_scr[d, p] = jnp.zeros((LANES, LANES), F32)

    def state_group(ns):
        units = [u for n in ns for u in ((0, n), (1, N - 1 - n))]
        rows = [_chunk_rows(n) for _, n in units]
        pre = [_dot(sm_ref[r, :].astype(BF16), wal_ref[:, d * HK:(d + 1) * HK]) + bal_ref[:, d * HK:(d + 1) * HK]
               for (d, _), r in zip(units, rows)]
        g = [_log_sigmoid(x) * (1.0 / TAU_GLA) for x in pre]
        b = [_tri_sum(tri[d], gi) for (d, _), gi in zip(units, g)]
        ks_all, dec_all = [], []
        for (d, _), r, bi in zip(units, rows, b):
            bend = bi[L - 1:L, :] if d == 0 else bi[0:1, :]
            q = q_ref[r, :] * scale
            ks = (k_ref[r, :] * jnp.exp(bend - bi)).astype(BF16)
            qh_scr[d, r, :] = (q * jnp.exp(bi - bend)).astype(BF16)
            qs_scr[d, r, :] = (q * jnp.exp(bi)).astype(BF16)
            kh_scr[d, r, :] = ks
            ks_all.append(ks)
            dec_all.append(jnp.exp(bend))
        upd_all = [[[_dot(v_ref[r, (2 * p + j) * DV:(2 * p + j + 1) * DV].T.astype(BF16),
                          ks[:, p * LANES:(p + 1) * LANES]) for j in range(2)]
                    for p in range(n_pairs)]
                   for r, ks in zip(rows, ks_all)]
        for (d, n), dec, upd in zip(units, dec_all, upd_all):
            for p in range(n_pairs):
                st = st_scr[d, p]
                sall_scr[d, n, p] = st.astype(BF16)
                st_scr[d, p] = (st * dec[:, p * LANES:(p + 1) * LANES]
                                + jnp.where(head_mask[0], upd[p][0], upd[p][1]))

    _chunk_loop(N, unroll, state_group)

    def out_group(ns):
        units = [(d, n, p, j) for n in ns for d in range(2) for p in range(n_pairs) for j in range(2)]
        scores, inter = [], []
        for d, n, p, j in units:
            r = _chunk_rows(n)
            ls = slice(p * LANES, (p + 1) * LANES)
            qh = qh_scr[d, r, ls]
            qs = qs_scr[d, r, ls]
            qm = jnp.where(head_mask[j], qh, jnp.zeros_like(qh))
            qsm = jnp.where(head_mask[j], qs, jnp.zeros_like(qs))
            scores.append(_dot_nt(qm, kh_scr[d, r, ls]))
            inter.append(_dot_nt(qsm, sall_scr[d, n, p]))
        probs = [jnp.where(tmask[d], a, 0.0).astype(BF16) for (d, _, _, _), a in zip(units, scores)]
        for (d, n, p, j), a, it in zip(units, probs, inter):
            vs = slice((2 * p + j) * DV, (2 * p + j + 1) * DV)
            r = _chunk_rows(n)
            o_scr[d][r, vs] = _dot(a, v_ref[r, vs].astype(BF16)) + it

    _chunk_loop(N, unroll, out_group)

    def epilogue(i, carry):
        rows = pl.ds(pl.multiple_of(i * L, L), L)
        for h in range(H_A):
            vs = slice(h * DV, (h + 1) * DV)
            o = of_scr[rows, vs] + ob_scr[rows, vs]
            out_ref[rows, vs] = _rms(o, gw_ref[:, vs]) * _silu(g_ref[rows, vs])
        return carry

    lax.fori_loop(0, N, epilogue, 0)

    if write_state:
        for d in range(2):
            for p in range(n_pairs):
                snew_ref[d, p] = st_scr[d, p].T


def _gla_call(z3, s0, wal_p, bal_p, gw, *, has_state, write_state):
    B, T, _ = z3.shape
    HK = wal_p.shape[1] // 2
    DA = gw.shape[0]
    n_pairs = HK // LANES
    small_blk = (z3.shape[2] - SMALL_W) // SMALL_W
    n_chunks = T // CHUNK
    kern = functools.partial(_gla_kernel, has_state=has_state, write_state=write_state,
                             unroll=min(n_chunks, SCAN_UNROLL))
    state_shape = (B, 2, n_pairs, LANES, LANES)
    state_spec = pl.BlockSpec((None, 2, n_pairs, LANES, LANES), lambda b: (b, 0, 0, 0, 0))
    in_specs = [
        pl.BlockSpec((None, T, HK), lambda b: (b, 0, 0)),
        pl.BlockSpec((None, T, HK), lambda b: (b, 0, 1)),
        pl.BlockSpec((None, T, DA), lambda b: (b, 0, 1)),
        pl.BlockSpec((None, T, DA), lambda b: (b, 0, 2)),
        pl.BlockSpec((None, T, SMALL_W), lambda b: (b, 0, small_blk)),
    ]
    args = [z3, z3, z3, z3, z3]
    if has_state:
        in_specs.append(state_spec)
        args.append(s0)
    in_specs += [
        pl.BlockSpec(wal_p.shape, lambda b: (0, 0)),
        pl.BlockSpec(bal_p.shape, lambda b: (0, 0)),
        pl.BlockSpec((1, DA), lambda b: (0, 0)),
    ]
    args += [wal_p, bal_p, gw.reshape(1, DA)]
    out_specs = [pl.BlockSpec((None, T, DA), lambda b: (b, 0, 0))]
    out_shape = [jax.ShapeDtypeStruct((B, T, DA), F32)]
    if write_state:
        out_specs.append(state_spec)
        out_shape.append(jax.ShapeDtypeStruct(state_shape, F32))
    return pl.pallas_call(
        kern,
        grid=(B,),
        in_specs=in_specs,
        out_specs=out_specs,
        out_shape=out_shape,
        scratch_shapes=[
            pltpu.VMEM((T, DA), F32),
            pltpu.VMEM((T, DA), F32),
            pltpu.VMEM((2, n_pairs, LANES, LANES), F32),
            pltpu.VMEM((2, n_chunks, n_pairs, LANES, LANES), BF16),
            pltpu.VMEM((2, T, HK), BF16),
            pltpu.VMEM((2, T, HK), BF16),
            pltpu.VMEM((2, T, HK), BF16),
        ],
        compiler_params=pltpu.CompilerParams(dimension_semantics=("arbitrary",),
                                             vmem_limit_bytes=VMEM_LIMIT),
        name="gla_scan",
    )(*args)


def _mlstm_kernel(*refs, grid_w, has_state, write_state, unroll):
    refs = list(refs)
    qk_ref, v_ref, og_ref, sm_ref = refs[:4]
    del refs[:4]
    if has_state:
        c0_ref, n0_ref, m0_ref = refs[:3]
        del refs[:3]
    cw_ref, bm_ref, gw_ref, out_ref = refs[:4]
    del refs[:4]
    if write_state:
        cnew_ref, nnew_ref, mnew_ref = refs[:3]
        del refs[:3]
    (pad_scr, qk_scr, y_scr, hf_scr, hb_scr, c_scr, n_scr, m_scr,
     call_scr, nall_scr, mall_scr, g_scr, fy_scr) = refs
    T = qk_ref.shape[0]
    L = CHUNK
    N = T // L
    C2 = qk_ref.shape[1]
    HK = C2 // 2
    DK = HK // H_B
    DV = v_ref.shape[1] // H_B
    scale = DK ** -0.5
    n_pairs = HK // LANES
    P = pad_scr.shape[0] - T
    P0 = P // 2
    rows_img = T // grid_w

    lower, upper = _chunk_masks(L)
    tri = (lower.astype(BF16), upper.astype(BF16))
    tmask = (lower, upper)
    lane = lax.broadcasted_iota(jnp.int32, (1, LANES), 1)
    head_mask = (lane < DK, lane >= DK)
    h_scr = (hf_scr, hb_scr)

    for d in range(2):
        for p in range(n_pairs):
            if has_state:
                c_scr[d, p] = c0_ref[d, p]
                n_scr[2 * d + p:2 * d + p + 1, :] = n0_ref[d, p:p + 1, :]
            else:
                c_scr[d, p] = jnp.zeros((LANES, LANES), F32)
                n_scr[2 * d + p:2 * d + p + 1, :] = jnp.zeros((1, LANES), F32)
    eye_h = (lax.broadcasted_iota(jnp.int32, (H_B, H_B), 0) == lax.broadcasted_iota(jnp.int32, (H_B, H_B), 1))

    def to_col(row):
        return jnp.sum(jnp.where(eye_h, row, 0.0), axis=1, keepdims=True)

    def to_row(col):
        return jnp.sum(jnp.where(eye_h, col, 0.0), axis=0, keepdims=True)

    for d in range(2):
        if has_state:
            m_scr[H_B * d:H_B * (d + 1), 0:1] = to_col(m0_ref[d:d + 1, :])
        else:
            m_scr[H_B * d:H_B * (d + 1), 0:1] = jnp.zeros((H_B, 1), F32)

    pad_scr[0:P0, :] = jnp.zeros((P0, C2), F32)
    pad_scr[P0 + T:P + T, :] = jnp.zeros((P - P0, C2), F32)

    def copy_in(i, carry):
        r0 = pl.multiple_of(i * L, L)
        pad_scr[pl.ds(P0 + r0, L), :] = qk_ref[pl.ds(r0, L), :]
        return carry

    lax.fori_loop(0, N, copy_in, 0)

    lane_c = lax.broadcasted_iota(jnp.int32, (1, C2), 1)
    qscale = jnp.where(lane_c < HK, scale, 1.0).astype(F32)
    sub = lax.broadcasted_iota(jnp.int32, (L, 1), 0)
    img_rows = (0,) if rows_img == 1 else (-1, 0, 1)

    def conv_tile(i, carry):
        r0 = pl.multiple_of(i * L, L)
        col = lax.rem(r0, grid_w) + sub
        ok_left = col >= 1
        ok_right = col <= grid_w - 2
        acc = jnp.zeros((L, C2), F32)
        for di in img_rows:
            blk = pad_scr[pl.ds(P0 + r0 + di * grid_w - 8, L + 16), :]
            left = jnp.where(ok_left, blk[7:7 + L, :], 0.0)
            mid = blk[8:8 + L, :]
            right = jnp.where(ok_right, blk[9:9 + L, :], 0.0)
            wr = 3 * (di + 1)
            acc = acc + left * cw_ref[wr:wr + 1, :] + mid * cw_ref[wr + 1:wr + 2, :] + right * cw_ref[wr + 2:wr + 3, :]
        qk_scr[pl.ds(r0, L), :] = _silu(acc) * qscale
        return carry

    lax.fori_loop(0, N, conv_tile, 0)

    gl = lane - GATE_LANE0
    is_f = ((gl >= H_B) & (gl < 2 * H_B)) | ((gl >= 3 * H_B) & (gl < 4 * H_B))

    def gate_tile(i, carry):
        rows = pl.ds(pl.multiple_of(i * L, L), L)
        x = sm_ref[rows, :] + bm_ref[...]
        y_scr[rows, :] = jnp.where(is_f, _log_sigmoid(x), x)
        return carry

    lax.fori_loop(0, N, gate_tile, 0)


    def state_group(ns):
        units = [u for n in ns for u in ((0, n), (1, N - 1 - n))]
        rows = [_chunk_rows(n) for _, n in units]
        xs = [y_scr[r, :] for r in rows]
        fsum = [_tri_sum(tri[d], x) for (d, _), x in zip(units, xs)]
        wk_all, f_end, c_end = [], [], []
        for (d, n), r, x, fs in zip(units, rows, xs, fsum):
            y = jnp.where(is_f, fs, x)
            fy_scr[d, r, :] = y
            yt = y.T
            li0 = GATE_LANE0 + 2 * H_B * d
            grow = yt[li0:li0 + H_B, :] - yt[li0 + H_B:li0 + 2 * H_B, :]
            g_scr[d, n, 0:H_B, 0:L] = grow
            e_col = L - 1 if d == 0 else 0
            f_end.append(yt[li0 + H_B:li0 + 2 * H_B, e_col:e_col + 1])
            ce = jnp.max(grow, axis=1, keepdims=True)
            c_end.append(ce)
            wk_all.append(jnp.exp(grow - ce))
        kv_all, ksum_all = [], []
        for r, wk4 in zip(rows, wk_all):
            kv_u, ks_u = [], []
            for p in range(n_pairs):
                kp = qk_scr[r, HK + p * LANES:HK + (p + 1) * LANES]
                kpb = kp.astype(BF16)
                kt = kp.T
                for j in range(2):
                    h = 2 * p + j
                    wk = wk4[h:h + 1, :]
                    kwt = (kt[j * DK:(j + 1) * DK, :] * wk).astype(BF16)
                    kv_u.append(_dot(kwt, v_ref[r, h * DV:(h + 1) * DV].astype(BF16)))
                    ks_u.append(_dot(jnp.broadcast_to(wk, (8, L)).astype(BF16), kpb)[0:1, :])
            kv_all.append(kv_u)
            ksum_all.append(ks_u)
        for (d, n), fe, ce, kv_u, ks_u in zip(units, f_end, c_end, kv_all, ksum_all):
            m_prev = m_scr[H_B * d:H_B * (d + 1), 0:1]
            mall_scr[d, n, 0:H_B, 0:1] = m_prev
            mx = jnp.maximum(m_prev, ce)
            a_all = jnp.exp(m_prev - mx)
            b_all = jnp.exp(ce - mx)
            m_scr[H_B * d:H_B * (d + 1), 0:1] = fe + mx
            for p in range(n_pairs):
                npair = n_scr[2 * d + p:2 * d + p + 1, :]
                nall_scr[d, n, p:p + 1, :] = npair
                a_s = [a_all[2 * p + j:2 * p + j + 1, :] for j in range(2)]
                b_s = [b_all[2 * p + j:2 * p + j + 1, :] for j in range(2)]
                for j in range(2):
                    hr = slice(j * DK, (j + 1) * DK)
                    cj = c_scr[d, p, hr, :]
                    call_scr[d, n, p, hr, :] = cj.astype(BF16)
                    c_scr[d, p, hr, :] = a_s[j] * cj + b_s[j] * kv_u[2 * p + j]
                n_scr[2 * d + p:2 * d + p + 1, :] = (
                    jnp.where(head_mask[0], a_s[0], a_s[1]) * npair
                    + jnp.where(head_mask[0], b_s[0] * ks_u[2 * p], b_s[1] * ks_u[2 * p + 1]))

    _chunk_loop(N, unroll, state_group)

    def out_group(ns):
        units = [(d, n, p, j) for n in ns for d in range(2) for p in range(n_pairs) for j in range(2)]
        qms, qks, qcs = [], [], []
        for d, n, p, j in units:
            r = _chunk_rows(n)
            qm = jnp.where(head_mask[j], qk_scr[r, p * LANES:(p + 1) * LANES], 0.0)
            qmb = qm.astype(BF16)
            qms.append(qm)
            qks.append(_dot_nt(qmb, qk_scr[r, HK + p * LANES:HK + (p + 1) * LANES].astype(BF16)))
            qcs.append(_dot(qmb, call_scr[d, n, p]))
        s_all, cmax_all = [], []
        for (d, n, p, j), qk in zip(units, qks):
            grow = g_scr[d, n, 2 * p + j:2 * p + j + 1, 0:L]
            e = jnp.where(tmask[d], grow, -jnp.inf)
            cmax = jnp.max(e, axis=-1, keepdims=True)
            cmax_all.append(cmax)
            s_all.append(qk * jnp.exp(e - cmax))
        nums = [_dot(s.astype(BF16), v_ref[_chunk_rows(n), (2 * p + j) * DV:(2 * p + j + 1) * DV].astype(BF16))
                for (d, n, p, j), s in zip(units, s_all)]
        for (d, n, p, j), qm, qc, s, cmax, num_loc in zip(units, qms, qcs, s_all, cmax_all, nums):
            h = 2 * p + j
            r = _chunk_rows(n)
            lf = GATE_LANE0 + 2 * H_B * d + H_B + h
            den_loc = jnp.sum(s, axis=-1, keepdims=True)
            m_prev = mall_scr[d, n, h:h + 1, 0:1]
            mu = jnp.maximum(m_prev, cmax)
            w_inter = jnp.exp(m_prev - mu)
            w_loc = jnp.exp(cmax - mu)
            f_col = jnp.sum(jnp.where(lane == lf, fy_scr[d, r, :], 0.0), axis=-1, keepdims=True)
            npair = nall_scr[d, n, p:p + 1, :]
            num = w_loc * num_loc + w_inter * qc
            den = w_loc * den_loc + w_inter * jnp.sum(qm * npair, axis=-1, keepdims=True)
            h_scr[d][r, h * DV:(h + 1) * DV] = num / jnp.maximum(jnp.abs(den), jnp.exp(-(f_col + mu)))

    _chunk_loop(N, unroll, out_group)

    def epilogue(i, carry):
        rows = pl.ds(pl.multiple_of(i * L, L), L)
        for h in range(H_B):
            vs = slice(h * DV, (h + 1) * DV)
            o = hf_scr[rows, vs] + hb_scr[rows, vs]
            out_ref[rows, vs] = _rms(o, gw_ref[:, vs]) * _sigmoid(og_ref[rows, vs])
        return carry

    lax.fori_loop(0, N, epilogue, 0)

    if write_state:
        for d in range(2):
            for p in range(n_pairs):
                cnew_ref[d, p] = c_scr[d, p]
                nnew_ref[d, p:p + 1, :] = n_scr[2 * d + p:2 * d + p + 1, :]
            mnew_ref[d:d + 1, :] = to_row(m_scr[H_B * d:H_B * (d + 1), 0:1])


def _mlstm_call(z3, c0, n0, m0, conv9, bm_row, gw, *, grid_w, has_state, write_state):
    B, T, _ = z3.shape
    C2 = conv9.shape[1]
    HK = C2 // 2
    DB = gw.shape[0]
    n_pairs = HK // LANES
    small_blk = (z3.shape[2] - SMALL_W) // SMALL_W
    qk_blk = (3 * DB) // C2
    v_blk = (3 * DB + C2) // DB
    pad_rows = 2 * (grid_w + 8) if T // grid_w > 1 else 16
    n_chunks = T // CHUNK
    kern = functools.partial(_mlstm_kernel, grid_w=grid_w, has_state=has_state, write_state=write_state,
                             unroll=min(n_chunks, SCAN_UNROLL))
    c_spec = pl.BlockSpec((None, 2, n_pairs, LANES, LANES), lambda b: (b, 0, 0, 0, 0))
    n_spec = pl.BlockSpec((None, 2, n_pairs, LANES), lambda b: (b, 0, 0, 0))
    m_spec = pl.BlockSpec((None, 2, H_B), lambda b: (b, 0, 0))
    in_specs = [
        pl.BlockSpec((None, T, C2), lambda b: (b, 0, qk_blk)),
        pl.BlockSpec((None, T, DB), lambda b: (b, 0, v_blk)),
        pl.BlockSpec((None, T, DB), lambda b: (b, 0, v_blk + 1)),
        pl.BlockSpec((None, T, SMALL_W), lambda b: (b, 0, small_blk)),
    ]
    args = [z3, z3, z3, z3]
    if has_state:
        in_specs += [c_spec, n_spec, m_spec]
        args += [c0, n0, m0]
    in_specs += [
        pl.BlockSpec(conv9.shape, lambda b: (0, 0)),
        pl.BlockSpec((1, SMALL_W), lambda b: (0, 0)),
        pl.BlockSpec((1, DB), lambda b: (0, 0)),
    ]
    args += [conv9, bm_row, gw.reshape(1, DB)]
    out_specs = [pl.BlockSpec((None, T, DB), lambda b: (b, 0, 0))]
    out_shape = [jax.ShapeDtypeStruct((B, T, DB), F32)]
    if write_state:
        out_specs += [c_spec, n_spec, m_spec]
        out_shape += [
            jax.ShapeDtypeStruct((B, 2, n_pairs, LANES, LANES), F32),
            jax.ShapeDtypeStruct((B, 2, n_pairs, LANES), F32),
            jax.ShapeDtypeStruct((B, 2, H_B), F32),
        ]
    return pl.pallas_call(
        kern,
        grid=(B,),
        in_specs=in_specs,
        out_specs=out_specs,
        out_shape=out_shape,
        scratch_shapes=[
            pltpu.VMEM((T + pad_rows, C2), F32),
            pltpu.VMEM((T, C2), F32),
            pltpu.VMEM((T, SMALL_W), F32),
            pltpu.VMEM((T, DB), F32),
            pltpu.VMEM((T, DB), F32),
            pltpu.VMEM((2, n_pairs, LANES, LANES), F32),
            pltpu.VMEM((8, LANES), F32),
            pltpu.VMEM((8, LANES), F32),
            pltpu.VMEM((2, n_chunks, n_pairs, LANES, LANES), BF16),
            pltpu.VMEM((2, n_chunks, 8, LANES), F32),
            pltpu.VMEM((2, n_chunks, 8, LANES), F32),
            pltpu.VMEM((2, n_chunks, 8, LANES), F32),
            pltpu.VMEM((2, T, SMALL_W), F32),
        ],
        compiler_params=pltpu.CompilerParams(dimension_semantics=("arbitrary",),
                                             vmem_limit_bytes=VMEM_LIMIT),
        name="mlstm_scan",
    )(*args)


def _outff_kernel(x_ref, a_ref, b_ref, mod_ref, n2_ref, fn_ref, wo_ref, w1_ref, w2_ref, y_ref,
                  *, mod_row0, tiles_per_batch, ff_chunk, final_norm):
    D = x_ref.shape[1]
    DA = a_ref.shape[1]
    row = _mod_row(mod_row0, tiles_per_batch)

    def mod(k):
        return mod_ref[pl.ds(row, 1), k * D:(k + 1) * D]

    y = _dot(a_ref[...].astype(BF16), wo_ref[0:DA, :]) + _dot(b_ref[...].astype(BF16), wo_ref[DA:, :])
    x1 = x_ref[...] + mod(2) * y
    h2 = (_rms(x1, n2_ref[...]) * (1.0 + mod(4)) + mod(3)).astype(BF16)
    acc = jnp.zeros(x1.shape, F32)
    for c0 in range(0, w1_ref.shape[1], ff_chunk):
        u = jnp.maximum(_dot(h2, w1_ref[:, c0:c0 + ff_chunk]), 0.0)
        acc = acc + _dot((u * u).astype(BF16), w2_ref[c0:c0 + ff_chunk, :])
    x2 = x1 + mod(5) * acc
    y_ref[...] = _rms(x2, fn_ref[...]) if final_norm else x2


def _outff_call(x2d, a2d, b2d, mod, norm2_w, final_w, wo, w1, w2, *, tm, mod_row0, tiles_per_batch,
                final_norm):
    M, D = x2d.shape
    DA = a2d.shape[1]
    DFF = w1.shape[1]
    kern = functools.partial(_outff_kernel, mod_row0=mod_row0, tiles_per_batch=tiles_per_batch,
                             ff_chunk=512, final_norm=final_norm)
    once = pl.Buffered(1)
    return pl.pallas_call(
        kern,
        grid=(M // tm,),
        in_specs=[
            pl.BlockSpec((tm, D), lambda i: (i, 0)),
            pl.BlockSpec((tm, DA), lambda i: (i, 0)),
            pl.BlockSpec((tm, D - DA), lambda i: (i, 0)),
            pl.BlockSpec(mod.shape, lambda i: (0, 0)),
            pl.BlockSpec((1, D), lambda i: (0, 0)),
            pl.BlockSpec((1, D), lambda i: (0, 0)),
            pl.BlockSpec((D, D), lambda i: (0, 0), pipeline_mode=once),
            pl.BlockSpec((D, DFF), lambda i: (0, 0), pipeline_mode=once),
            pl.BlockSpec((DFF, D), lambda i: (0, 0), pipeline_mode=once),
        ],
        out_specs=pl.BlockSpec((tm, D), lambda i: (i, 0)),
        out_shape=jax.ShapeDtypeStruct((M, D), F32),
        compiler_params=pltpu.CompilerParams(dimension_semantics=("arbitrary",),
                                             vmem_limit_bytes=VMEM_LIMIT),
        name="outproj_mlp",
    )(x2d, a2d, b2d, mod, norm2_w.reshape(1, D), final_w.reshape(1, D), wo, w1, w2)


def _block(x, mod, mod_row0, per_batch, grid_w, states, lw, final_w, final_norm, write_state):
    B, T, D = x.shape
    tm = 512 if T % 512 == 0 else T
    tiles_per_batch = (T // tm) if per_batch else None
    x2d = x.reshape(B * T, D)
    z = _inproj_call(x2d, mod, lw["norm1_w"], lw["w_in_p"], tm=tm, mod_row0=mod_row0,
                     tiles_per_batch=tiles_per_batch)
    z3 = z.reshape(B, T, z.shape[1])
    has_state = states is not None
    n_pairs_a = lw["wal_p"].shape[1] // 2 // LANES
    n_pairs_b = lw["conv9"].shape[1] // 2 // LANES
    s_gla = s_c = s_n = s_m = None
    if has_state:
        s_gla, s_c, s_n, s_m = states
        s_gla = s_gla.reshape(B, 2, n_pairs_a, LANES, LANES)
        s_c = s_c.reshape(B, 2, n_pairs_b, LANES, LANES)
        s_n = s_n.reshape(B, 2, n_pairs_b, LANES)
    res_a = _gla_call(z3, s_gla, lw["wal_p"], lw["bal_p"], lw["gnorm_a_w"],
                      has_state=has_state, write_state=write_state)
    res_b = _mlstm_call(z3, s_c, s_n, s_m, lw["conv9"], lw["bm_row"], lw["gnorm_b_w"],
                        grid_w=grid_w, has_state=has_state, write_state=write_state)
    out_a, out_b = res_a[0], res_b[0]
    new_states = (res_a[1], res_b[1], res_b[2], res_b[3]) if write_state else None
    y = _outff_call(x2d, out_a.reshape(B * T, -1), out_b.reshape(B * T, -1), mod, lw["norm2_w"], final_w,
                    lw["w_out"], lw["w_ff1"], lw["w_ff2"], tm=tm, mod_row0=mod_row0,
                    tiles_per_batch=tiles_per_batch, final_norm=final_norm)
    return y.reshape(B, T, D), new_states


def _layer_weights(l, norm1_w, norm2_w, w_in, w_alpha2, b_alpha, b_mgate, conv_w, gnorm_a_w, gnorm_b_w,
                   w_out, w_ff1, w_ff2):
    D = w_in.shape[1]
    hk_a = w_alpha2.shape[-1]
    d_a = gnorm_a_w.shape[-1]
    d_b = gnorm_b_w.shape[-1]
    hk_b = conv_w.shape[-1] // 2
    sizes = (hk_a, hk_a, d_a, d_a, 2 * R_ALPHA, hk_b, hk_b, d_b, d_b, 4 * H_B)
    offs = [0]
    for s in sizes:
        offs.append(offs[-1] + s)
    w = w_in[l]
    big = [w[:, offs[i]:offs[i + 1]] for i in (0, 1, 2, 3, 5, 6, 7, 8)]
    small = [w[:, offs[4]:offs[5]], w[:, offs[9]:offs[10]],
             jnp.zeros((D, SMALL_W - 2 * R_ALPHA - 4 * H_B), w.dtype)]
    w_in_p = jnp.concatenate(big + small, axis=1).astype(BF16)
    wal = w_alpha2[l]
    wal_p = jnp.zeros((SMALL_W, 2 * hk_a), F32)
    wal_p = wal_p.at[0:R_ALPHA, 0:hk_a].set(wal[0]).at[R_ALPHA:2 * R_ALPHA, hk_a:].set(wal[1]).astype(BF16)
    bm_row = jnp.zeros((1, SMALL_W), F32).at[0, GATE_LANE0:GATE_LANE0 + 4 * H_B].set(b_mgate[l].reshape(-1))
    return dict(
        norm1_w=norm1_w[l], norm2_w=norm2_w[l], w_in_p=w_in_p, wal_p=wal_p,
        bal_p=b_alpha[l].reshape(1, -1), bm_row=bm_row,
        conv9=conv_w[l].reshape(-1, conv_w.shape[-1]),
        gnorm_a_w=gnorm_a_w[l], gnorm_b_w=gnorm_b_w[l],
        w_out=w_out[l].astype(BF16), w_ff1=w_ff1[l].astype(BF16), w_ff2=w_ff2[l].astype(BF16),
    )


def kernel(x_prompt, x_sample, c, state_gla, state_mlstm_C, state_mlstm_n, state_mlstm_m, c_ctx, w_ada, b_ada, norm1_w, norm2_w, w_in, w_alpha2, b_alpha, b_mgate, conv_w, gnorm_a_w, gnorm_b_w, w_out, w_ff1, w_ff2, final_norm_w):
    depth = w_in.shape[0]
    D = x_prompt.shape[-1]
    Bp, Tp, _ = x_prompt.shape
    Bs = x_sample.shape[0]
    assert 1 + Bs <= COND_ROWS
    cond = jnp.concatenate([c_ctx[None, :], c, jnp.zeros((COND_ROWS - 1 - Bs, D), F32)], axis=0)
    xp, xs = x_prompt, x_sample
    s_gla, s_c, s_n, s_m = [], [], [], []
    for l in range(depth):
        lw = _layer_weights(l, norm1_w, norm2_w, w_in, w_alpha2, b_alpha, b_mgate, conv_w,
                            gnorm_a_w, gnorm_b_w, w_out, w_ff1, w_ff2)
        mod = _ada_call(cond, w_ada[l], b_ada[l])
        last = l == depth - 1
        xp, ctx = _block(xp, mod, 0, False, Tp, None, lw, final_norm_w, last, True)
        s_gla.append(ctx[0].reshape(Bp, 2, H_A, -1, ctx[0].shape[-1]))
        s_c.append(ctx[1].reshape(Bp, 2, H_B, -1, ctx[1].shape[-1]))
        s_n.append(ctx[2].reshape(Bp, 2, H_B, -1))
        s_m.append(ctx[3])
        cached = (state_gla[:, l], state_mlstm_C[:, l], state_mlstm_n[:, l], state_mlstm_m[:, l])
        xs, _ = _block(xs, mod, 1, True, GRID_W, cached, lw, final_norm_w, last, False)
    dt = x_prompt.dtype
    return (xp, xs, jnp.stack(s_gla, axis=1).astype(dt), jnp.stack(s_c, axis=1).astype(dt),
            jnp.stack(s_n, axis=1).astype(dt), jnp.stack(s_m, axis=1).astype(dt))
```

```python
import functools

import jax
import jax.numpy as jnp
from jax import lax
from jax.experimental import pallas as pl
from jax.experimental.pallas import tpu as pltpu

F32 = jnp.float32
BF16 = jnp.bfloat16

GRID_W = 64
H_A = 4
H_B = 4
R_ALPHA = 16
TAU_GLA = 16.0
CHUNK = 64
EPS = 1e-6
LANES = 128
COND_ROWS = 8
SMALL_W = LANES
GATE_LANE0 = 2 * R_ALPHA
VMEM_LIMIT = 56 * 1024 * 1024
SCAN_UNROLL = 4
TOKEN_TILE = 512


def _sigmoid(x):
    return 1.0 / (1.0 + jnp.exp(-x))


def _silu(x):
    return x * _sigmoid(x)


def _log_sigmoid(x):
    return jnp.minimum(x, 0.0) - jnp.log1p(jnp.exp(-jnp.abs(x)))


def _dot(a, b):
    return jnp.dot(a, b, preferred_element_type=F32)


def _dot_nt(a, b):
    return lax.dot_general(a, b, (((1,), (1,)), ((), ())), preferred_element_type=F32)


def _rms(x, w):
    return x * lax.rsqrt(jnp.mean(x * x, axis=-1, keepdims=True) + EPS) * w


def _tri_sum(tri, x):
    hi = x.astype(BF16)
    r1 = x - hi.astype(F32)
    mid = r1.astype(BF16)
    lo = (r1 - mid.astype(F32)).astype(BF16)
    return _dot(tri, hi) + _dot(tri, mid) + _dot(tri, lo)


def _chunk_masks(L):
    row = lax.broadcasted_iota(jnp.int32, (L, L), 0)
    col = lax.broadcasted_iota(jnp.int32, (L, L), 1)
    lower = row >= col
    upper = row <= col
    return lower, upper


def _ada_kernel(c_ref, w_ref, b_ref, o_ref):
    s = _silu(c_ref[...])
    o_ref[...] = _dot(s.astype(BF16), w_ref[...].astype(BF16)) + b_ref[...]


def _ada_call(cond, w_ada, b_ada):
    D = cond.shape[1]
    n_out = w_ada.shape[1]
    tn = 1024
    return pl.pallas_call(
        _ada_kernel,
        grid=(n_out // tn,),
        in_specs=[
            pl.BlockSpec((COND_ROWS, D), lambda j: (0, 0)),
            pl.BlockSpec((D, tn), lambda j: (0, j)),
            pl.BlockSpec((1, tn), lambda j: (0, j)),
        ],
        out_specs=pl.BlockSpec((COND_ROWS, tn), lambda j: (0, j)),
        out_shape=jax.ShapeDtypeStruct((COND_ROWS, n_out), F32),
        compiler_params=pltpu.CompilerParams(dimension_semantics=("arbitrary",)),
        name="ada_mod",
    )(cond, w_ada, b_ada.reshape(1, n_out))


def _mod_row(mod_row0, tiles_per_batch):
    if tiles_per_batch is None:
        return mod_row0
    return mod_row0 + pl.program_id(0) // tiles_per_batch


def _inproj_kernel(x_ref, mod_ref, nw_ref, wt_ref, z_ref, wb_scr, *, mod_row0, tiles_per_batch, big_rows,
                   small_rows):
    D = x_ref.shape[1]

    @pl.when(pl.program_id(0) == 0)
    def _():
        col = 0
        for r0, n in big_rows:
            for k in range(n // LANES):
                blk = wt_ref[r0 + k * LANES:r0 + (k + 1) * LANES, :]
                wb_scr[:, col:col + LANES] = blk.T.astype(BF16)
                col += LANES
        parts = [wt_ref[r0:r0 + n, :] for r0, n in small_rows]
        n_small = sum(n for _, n in small_rows)
        parts.append(jnp.zeros((SMALL_W - n_small, D), F32))
        wb_scr[:, col:col + SMALL_W] = jnp.concatenate(parts, axis=0).T.astype(BF16)

    row = _mod_row(mod_row0, tiles_per_batch)
    sh1 = mod_ref[pl.ds(row, 1), 0:D]
    sc1 = mod_ref[pl.ds(row, 1), D:2 * D]
    h = _rms(x_ref[...], nw_ref[...]) * (1.0 + sc1) + sh1
    z_ref[...] = _dot(h.astype(BF16), wb_scr[...])


def _inproj_call(x2d, mod, norm_w, w_in_t, *, tm, mod_row0, tiles_per_batch, big_rows, small_rows):
    M, D = x2d.shape
    n_out = sum(n for _, n in big_rows) + SMALL_W
    kern = functools.partial(_inproj_kernel, mod_row0=mod_row0, tiles_per_batch=tiles_per_batch,
                             big_rows=big_rows, small_rows=small_rows)
    return pl.pallas_call(
        kern,
        grid=(M // tm,),
        in_specs=[
            pl.BlockSpec((tm, D), lambda i: (i, 0)),
            pl.BlockSpec(mod.shape, lambda i: (0, 0)),
            pl.BlockSpec((1, D), lambda i: (0, 0)),
            pl.BlockSpec(w_in_t.shape, lambda i: (0, 0), pipeline_mode=pl.Buffered(1)),
        ],
        out_specs=pl.BlockSpec((tm, n_out), lambda i: (i, 0)),
        out_shape=jax.ShapeDtypeStruct((M, n_out), F32),
        scratch_shapes=[pltpu.VMEM((D, n_out), BF16)],
        compiler_params=pltpu.CompilerParams(dimension_semantics=("arbitrary",),
                                             vmem_limit_bytes=VMEM_LIMIT),
        name="norm_inproj",
    )(x2d, mod, norm_w.reshape(1, D), w_in_t)


def _chunk_loop(n_chunks, unroll, fn):
    if unroll >= n_chunks:
        fn(list(range(n_chunks)))
        return

    def body(i, carry):
        fn([i * unroll + u for u in range(unroll)])
        return carry

    lax.fori_loop(0, n_chunks // unroll, body, 0)


def _chunk_rows(n):
    if isinstance(n, int):
        return pl.ds(n * CHUNK, CHUNK)
    return pl.ds(pl.multiple_of(n * CHUNK, CHUNK), CHUNK)


def _gla_kernel(*refs, has_state, write_state, unroll):
    refs = list(refs)
    q_ref, k_ref, v_ref, g_ref, sm_ref = refs[:5]
    del refs[:5]
    s0_ref = refs.pop(0) if has_state else None
    wal_ref, bal_ref, gw_ref, out_ref = refs[:4]
    del refs[:4]
    snew_ref = refs.pop(0) if write_state else None
    of_scr, ob_scr, st_scr, sall_scr, qh_scr, qs_scr, kh_scr = refs
    T = q_ref.shape[0]
    L = CHUNK
    N = T // L
    HK = q_ref.shape[1]
    DK = HK // H_A
    DV = v_ref.shape[1] // H_A
    scale = DK ** -0.5
    n_pairs = HK // LANES

    lower, upper = _chunk_masks(L)
    tri = (lower.astype(BF16), upper.astype(BF16))
    tmask = (lower, upper)
    lane = lax.broadcasted_iota(jnp.int32, (1, LANES), 1)
    head_mask = (lane < DK, lane >= DK)
    o_scr = (of_scr, ob_scr)

    for d in range(2):
        for p in range(n_pairs):
            if has_state:
                st_scr[d, p] = s0_ref[d, p].T
            else:
                st_scr[d, p] = jnp.zeros((LANES, LANES), F32)

    def state_group(ns):
        units = [u for n in ns for u in ((0, n), (1, N - 1 - n))]
        rows = [_chunk_rows(n) for _, n in units]
        pre = [_dot(sm_ref[r, :].astype(BF16), wal_ref[:, d * HK:(d + 1) * HK]) + bal_ref[:, d * HK:(d + 1) * HK]
               for (d, _), r in zip(units, rows)]
        g = [_log_sigmoid(x) * (1.0 / TAU_GLA) for x in pre]
        b = [_tri_sum(tri[d], gi) for (d, _), gi in zip(units, g)]
        ks_all, dec_all = [], []
        for (d, _), r, bi in zip(units, rows, b):
            bend = bi[L - 1:L, :] if d == 0 else bi[0:1, :]
            q = q_ref[r, :] * scale
            ks = (k_ref[r, :] * jnp.exp(bend - bi)).astype(BF16)
            qh_scr[d, r, :] = (q * jnp.exp(bi - bend)).astype(BF16)
            qs_scr[d, r, :] = (q * jnp.exp(bi)).astype(BF16)
            kh_scr[d, r, :] = ks
            ks_all.append(ks)
            dec_all.append(jnp.exp(bend))
        upd_all = [[[_dot(v_ref[r, (2 * p + j) * DV:(2 * p + j + 1) * DV].T.astype(BF16),
                          ks[:, p * LANES:(p + 1) * LANES]) for j in range(2)]
                    for p in range(n_pairs)]
                   for r, ks in zip(rows, ks_all)]
        for (d, n), dec, upd in zip(units, dec_all, upd_all):
            for p in range(n_pairs):
                st = st_scr[d, p]
                sall_scr[d, n, p] = st.astype(BF16)
                st_scr[d, p] = (st * dec[:, p * LANES:(p + 1) * LANES]
                                + jnp.where(head_mask[0], upd[p][0], upd[p][1]))

    _chunk_loop(N, unroll, state_group)

    def out_group(ns):
        units = [(d, n, p, j) for n in ns for d in range(2) for p in range(n_pairs) for j in range(2)]
        scores, inter = [], []
        for d, n, p, j in units:
            r = _chunk_rows(n)
            ls = slice(p * LANES, (p + 1) * LANES)
            qh = qh_scr[d, r, ls]
            qs = qs_scr[d, r, ls]
            qm = jnp.where(head_mask[j], qh, jnp.zeros_like(qh))
            qsm = jnp.where(head_mask[j], qs, jnp.zeros_like(qs))
            scores.append(_dot_nt(qm, kh_scr[d, r, ls]))
            inter.append(_dot_nt(qsm, sall_scr[d, n, p]))
        probs = [jnp.where(tmask[d], a, 0.0).astype(BF16) for (d, _, _, _), a in zip(units, scores)]
        for (d, n, p, j), a, it in zip(units, probs, inter):
            vs = slice((2 * p + j) * DV, (2 * p + j + 1) * DV)
            r = _chunk_rows(n)
            o_scr[d][r, vs] = _dot(a, v_ref[r, vs].astype(BF16)) + it

    _chunk_loop(N, unroll, out_group)

    def epilogue(i, carry):
        rows = pl.ds(pl.multiple_of(i * L, L), L)
        for h in range(H_A):
            vs = slice(h * DV, (h + 1) * DV)
            o = of_scr[rows, vs] + ob_scr[rows, vs]
            out_ref[rows, vs] = _rms(o, gw_ref[:, vs]) * _silu(g_ref[rows, vs])
        return carry

    lax.fori_loop(0, N, epilogue, 0)

    if write_state:
        for d in range(2):
            for p in range(n_pairs):
                snew_ref[d, p] = st_scr[d, p].T


def _gla_call(z3, s0, wal_p, bal_p, gw, *, has_state, write_state):
    B, T, _ = z3.shape
    HK = wal_p.shape[1] // 2
    DA = gw.shape[0]
    n_pairs = HK // LANES
    small_blk = (z3.shape[2] - SMALL_W) // SMALL_W
    n_chunks = T // CHUNK
    kern = functools.partial(_gla_kernel, has_state=has_state, write_state=write_state,
                             unroll=min(n_chunks, SCAN_UNROLL))
    state_shape = (B, 2, n_pairs, LANES, LANES)
    state_spec = pl.BlockSpec((None, 2, n_pairs, LANES, LANES), lambda b: (b, 0, 0, 0, 0))
    in_specs = [
        pl.BlockSpec((None, T, HK), lambda b: (b, 0, 0)),
        pl.BlockSpec((None, T, HK), lambda b: (b, 0, 1)),
        pl.BlockSpec((None, T, DA), lambda b: (b, 0, 1)),
        pl.BlockSpec((None, T, DA), lambda b: (b, 0, 2)),
        pl.BlockSpec((None, T, SMALL_W), lambda b: (b, 0, small_blk)),
    ]
    args = [z3, z3, z3, z3, z3]
    if has_state:
        in_specs.append(state_spec)
        args.append(s0)
    in_specs += [
        pl.BlockSpec(wal_p.shape, lambda b: (0, 0)),
        pl.BlockSpec(bal_p.shape, lambda b: (0, 0)),
        pl.BlockSpec((1, DA), lambda b: (0, 0)),
    ]
    args += [wal_p, bal_p, gw.reshape(1, DA)]
    out_specs = [pl.BlockSpec((None, T, DA), lambda b: (b, 0, 0))]
    out_shape = [jax.ShapeDtypeStruct((B, T, DA), F32)]
    if write_state:
        out_specs.append(state_spec)
        out_shape.append(jax.ShapeDtypeStruct(state_shape, F32))
    return pl.pallas_call(
        kern,
        grid=(B,),
        in_specs=in_specs,
        out_specs=out_specs,
        out_shape=out_shape,
        scratch_shapes=[
            pltpu.VMEM((T, DA), F32),
            pltpu.VMEM((T, DA), F32),
            pltpu.VMEM((2, n_pairs, LANES, LANES), F32),
            pltpu.VMEM((2, n_chunks, n_pairs, LANES, LANES), BF16),
            pltpu.VMEM((2, T, HK), BF16),
            pltpu.VMEM((2, T, HK), BF16),
            pltpu.VMEM((2, T, HK), BF16),
        ],
        compiler_params=pltpu.CompilerParams(dimension_semantics=("arbitrary",),
                                             vmem_limit_bytes=VMEM_LIMIT),
        name="gla_scan",
    )(*args)


def _mlstm_kernel(*refs, grid_w, has_state, write_state, unroll):
    refs = list(refs)
    qk_ref, v_ref, og_ref, sm_ref = refs[:4]
    del refs[:4]
    if has_state:
        c0_ref, n0_ref, m0_ref = refs[:3]
        del refs[:3]
    cw_ref, bm_ref, gw_ref, out_ref = refs[:4]
    del refs[:4]
    if write_state:
        cnew_ref, nnew_ref, mnew_ref = refs[:3]
        del refs[:3]
    (pad_scr, qk_scr, y_scr, hf_scr, hb_scr, c_scr, n_scr, m_scr,
     call_scr, nall_scr, mall_scr, g_scr, fy_scr) = refs
    T = qk_ref.shape[0]
    L = CHUNK
    N = T // L
    C2 = qk_ref.shape[1]
    HK = C2 // 2
    DK = HK // H_B
    DV = v_ref.shape[1] // H_B
    scale = DK ** -0.5
    n_pairs = HK // LANES
    P = pad_scr.shape[0] - T
    P0 = P // 2
    rows_img = T // grid_w

    lower, upper = _chunk_masks(L)
    tri = (lower.astype(BF16), upper.astype(BF16))
    tmask = (lower, upper)
    lane = lax.broadcasted_iota(jnp.int32, (1, LANES), 1)
    head_mask = (lane < DK, lane >= DK)
    h_scr = (hf_scr, hb_scr)

    for d in range(2):
        for p in range(n_pairs):
            if has_state:
                c_scr[d, p] = c0_ref[d, p]
                n_scr[2 * d + p:2 * d + p + 1, :] = n0_ref[d, p:p + 1, :]
            else:
                c_scr[d, p] = jnp.zeros((LANES, LANES), F32)
                n_scr[2 * d + p:2 * d + p + 1, :] = jnp.zeros((1, LANES), F32)
    eye_h = (lax.broadcasted_iota(jnp.int32, (H_B, H_B), 0) == lax.broadcasted_iota(jnp.int32, (H_B, H_B), 1))

    def to_col(row):
        return jnp.sum(jnp.where(eye_h, row, 0.0), axis=1, keepdims=True)

    def to_row(col):
        return jnp.sum(jnp.where(eye_h, col, 0.0), axis=0, keepdims=True)

    for d in range(2):
        if has_state:
            m_scr[H_B * d:H_B * (d + 1), 0:1] = to_col(m0_ref[d:d + 1, :])
        else:
            m_scr[H_B * d:H_B * (d + 1), 0:1] = jnp.zeros((H_B, 1), F32)

    pad_scr[0:P0, :] = jnp.zeros((P0, C2), F32)
    pad_scr[P0 + T:P + T, :] = jnp.zeros((P - P0, C2), F32)

    def copy_in(i, carry):
        r0 = pl.multiple_of(i * L, L)
        pad_scr[pl.ds(P0 + r0, L), :] = qk_ref[pl.ds(r0, L), :]
        return carry

    lax.fori_loop(0, N, copy_in, 0)

    lane_c = lax.broadcasted_iota(jnp.int32, (1, C2), 1)
    qscale = jnp.where(lane_c < HK, scale, 1.0).astype(F32)
    sub = lax.broadcasted_iota(jnp.int32, (L, 1), 0)
    img_rows = (0,) if rows_img == 1 else (-1, 0, 1)

    def conv_tile(i, carry):
        r0 = pl.multiple_of(i * L, L)
        col = lax.rem(r0, grid_w) + sub
        ok_left = col >= 1
        ok_right = col <= grid_w - 2
        acc = jnp.zeros((L, C2), F32)
        for di in img_rows:
            blk = pad_scr[pl.ds(P0 + r0 + di * grid_w - 8, L + 16), :]
            left = jnp.where(ok_left, blk[7:7 + L, :], 0.0)
            mid = blk[8:8 + L, :]
            right = jnp.where(ok_right, blk[9:9 + L, :], 0.0)
            wr = 3 * (di + 1)
            acc = acc + left * cw_ref[wr:wr + 1, :] + mid * cw_ref[wr + 1:wr + 2, :] + right * cw_ref[wr + 2:wr + 3, :]
        qk_scr[pl.ds(r0, L), :] = _silu(acc) * qscale
        return carry

    lax.fori_loop(0, N, conv_tile, 0)

    gl = lane - GATE_LANE0
    is_f = ((gl >= H_B) & (gl < 2 * H_B)) | ((gl >= 3 * H_B) & (gl < 4 * H_B))

    def gate_tile(i, carry):
        rows = pl.ds(pl.multiple_of(i * L, L), L)
        x = sm_ref[rows, :] + bm_ref[...]
        y_scr[rows, :] = jnp.where(is_f, _log_sigmoid(x), x)
        return carry

    lax.fori_loop(0, N, gate_tile, 0)


    def state_group(ns):
        units = [u for n in ns for u in ((0, n), (1, N - 1 - n))]
        rows = [_chunk_rows(n) for _, n in units]
        xs = [y_scr[r, :] for r in rows]
        fsum = [_tri_sum(tri[d], x) for (d, _), x in zip(units, xs)]
        wk_all, f_end, c_end = [], [], []
        for (d, n), r, x, fs in zip(units, rows, xs, fsum):
            y = jnp.where(is_f, fs, x)
            fy_scr[d, r, :] = y
            yt = y.T
            li0 = GATE_LANE0 + 2 * H_B * d
            grow = yt[li0:li0 + H_B, :] - yt[li0 + H_B:li0 + 2 * H_B, :]
            g_scr[d, n, 0:H_B, 0:L] = grow
            e_col = L - 1 if d == 0 else 0
            f_end.append(yt[li0 + H_B:li0 + 2 * H_B, e_col:e_col + 1])
            ce = jnp.max(grow, axis=1, keepdims=True)
            c_end.append(ce)
            wk_all.append(jnp.exp(grow - ce))
        kv_all, ksum_all = [], []
        for r, wk4 in zip(rows, wk_all):
            kv_u, ks_u = [], []
            for p in range(n_pairs):
                kp = qk_scr[r, HK + p * LANES:HK + (p + 1) * LANES]
                kpb = kp.astype(BF16)
                kt = kp.T
                for j in range(2):
                    h = 2 * p + j
                    wk = wk4[h:h + 1, :]
                    kwt = (kt[j * DK:(j + 1) * DK, :] * wk).astype(BF16)
                    kv_u.append(_dot(kwt, v_ref[r, h * DV:(h + 1) * DV].astype(BF16)))
                    ks_u.append(_dot(jnp.broadcast_to(wk, (8, L)).astype(BF16), kpb)[0:1, :])
            kv_all.append(kv_u)
            ksum_all.append(ks_u)
        for (d, n), fe, ce, kv_u, ks_u in zip(units, f_end, c_end, kv_all, ksum_all):
            m_prev = m_scr[H_B * d:H_B * (d + 1), 0:1]
            mall_scr[d, n, 0:H_B, 0:1] = m_prev
            mx = jnp.maximum(m_prev, ce)
            a_all = jnp.exp(m_prev - mx)
            b_all = jnp.exp(ce - mx)
            m_scr[H_B * d:H_B * (d + 1), 0:1] = fe + mx
            for p in range(n_pairs):
                npair = n_scr[2 * d + p:2 * d + p + 1, :]
                nall_scr[d, n, p:p + 1, :] = npair
                a_s = [a_all[2 * p + j:2 * p + j + 1, :] for j in range(2)]
                b_s = [b_all[2 * p + j:2 * p + j + 1, :] for j in range(2)]
                for j in range(2):
                    hr = slice(j * DK, (j + 1) * DK)
                    cj = c_scr[d, p, hr, :]
                    call_scr[d, n, p, hr, :] = cj.astype(BF16)
                    c_scr[d, p, hr, :] = a_s[j] * cj + b_s[j] * kv_u[2 * p + j]
                n_scr[2 * d + p:2 * d + p + 1, :] = (
                    jnp.where(head_mask[0], a_s[0], a_s[1]) * npair
                    + jnp.where(head_mask[0], b_s[0] * ks_u[2 * p], b_s[1] * ks_u[2 * p + 1]))

    _chunk_loop(N, unroll, state_group)

    def out_group(ns):
        units = [(d, n, p, j) for n in ns for d in range(2) for p in range(n_pairs) for j in range(2)]
        qms, qks, qcs = [], [], []
        for d, n, p, j in units:
            r = _chunk_rows(n)
            qm = jnp.where(head_mask[j], qk_scr[r, p * LANES:(p + 1) * LANES], 0.0)
            qmb = qm.astype(BF16)
            qms.append(qm)
            qks.append(_dot_nt(qmb, qk_scr[r, HK + p * LANES:HK + (p + 1) * LANES].astype(BF16)))
            qcs.append(_dot(qmb, call_scr[d, n, p]))
        s_all, cmax_all = [], []
        for (d, n, p, j), qk in zip(units, qks):
            grow = g_scr[d, n, 2 * p + j:2 * p + j + 1, 0:L]
            e = jnp.where(tmask[d], grow, -jnp.inf)
            cmax = jnp.max(e, axis=-1, keepdims=True)
            cmax_all.append(cmax)
            s_all.append(qk * jnp.exp(e - cmax))
        nums = [_dot(s.astype(BF16), v_ref[_chunk_rows(n), (2 * p + j) * DV:(2 * p + j + 1) * DV].astype(BF16))
                for (d, n, p, j), s in zip(units, s_all)]
        for (d, n, p, j), qm, qc, s, cmax, num_loc in zip(units, qms, qcs, s_all, cmax_all, nums):
            h = 2 * p + j
            r = _chunk_rows(n)
            lf = GATE_LANE0 + 2 * H_B * d + H_B + h
            den_loc = jnp.sum(s, axis=-1, keepdims=True)
            m_prev = mall_scr[d, n, h:h + 1, 0:1]
            mu = jnp.maximum(m_prev, cmax)
            w_inter = jnp.exp(m_prev - mu)
            w_loc = jnp.exp(cmax - mu)
            f_col = jnp.sum(jnp.where(lane == lf, fy_scr[d, r, :], 0.0), axis=-1, keepdims=True)
            npair = nall_scr[d, n, p:p + 1, :]
            num = w_loc * num_loc + w_inter * qc
            den = w_loc * den_loc + w_inter * jnp.sum(qm * npair, axis=-1, keepdims=True)
            h_scr[d][r, h * DV:(h + 1) * DV] = num / jnp.maximum(jnp.abs(den), jnp.exp(-(f_col + mu)))

    _chunk_loop(N, unroll, out_group)

    def epilogue(i, carry):
        rows = pl.ds(pl.multiple_of(i * L, L), L)
        for h in range(H_B):
            vs = slice(h * DV, (h + 1) * DV)
            o = hf_scr[rows, vs] + hb_scr[rows, vs]
            out_ref[rows, vs] = _rms(o, gw_ref[:, vs]) * _sigmoid(og_ref[rows, vs])
        return carry

    lax.fori_loop(0, N, epilogue, 0)

    if write_state:
        for d in range(2):
            for p in range(n_pairs):
                cnew_ref[d, p] = c_scr[d, p]
                nnew_ref[d, p:p + 1, :] = n_scr[2 * d + p:2 * d + p + 1, :]
            mnew_ref[d:d + 1, :] = to_row(m_scr[H_B * d:H_B * (d + 1), 0:1])


def _mlstm_call(z3, c0, n0, m0, conv9, bm_row, gw, *, grid_w, has_state, write_state):
    B, T, _ = z3.shape
    C2 = conv9.shape[1]
    HK = C2 // 2
    DB = gw.shape[0]
    n_pairs = HK // LANES
    small_blk = (z3.shape[2] - SMALL_W) // SMALL_W
    qk_blk = (3 * DB) // C2
    v_blk = (3 * DB + C2) // DB
    pad_rows = 2 * (grid_w + 8) if T // grid_w > 1 else 16
    n_chunks = T // CHUNK
    kern = functools.partial(_mlstm_kernel, grid_w=grid_w, has_state=has_state, write_state=write_state,
                             unroll=min(n_chunks, SCAN_UNROLL))
    c_spec = pl.BlockSpec((None, 2, n_pairs, LANES, LANES), lambda b: (b, 0, 0, 0, 0))
    n_spec = pl.BlockSpec((None, 2, n_pairs, LANES), lambda b: (b, 0, 0, 0))
    m_spec = pl.BlockSpec((None, 2, H_B), lambda b: (b, 0, 0))
    in_specs = [
        pl.BlockSpec((None, T, C2), lambda b: (b, 0, qk_blk)),
        pl.BlockSpec((None, T, DB), lambda b: (b, 0, v_blk)),
        pl.BlockSpec((None, T, DB), lambda b: (b, 0, v_blk + 1)),
        pl.BlockSpec((None, T, SMALL_W), lambda b: (b, 0, small_blk)),
    ]
    args = [z3, z3, z3, z3]
    if has_state:
        in_specs += [c_spec, n_spec, m_spec]
        args += [c0, n0, m0]
    in_specs += [
        pl.BlockSpec(conv9.shape, lambda b: (0, 0)),
        pl.BlockSpec((1, SMALL_W), lambda b: (0, 0)),
        pl.BlockSpec((1, DB), lambda b: (0, 0)),
    ]
    args += [conv9, bm_row, gw.reshape(1, DB)]
    out_specs = [pl.BlockSpec((None, T, DB), lambda b: (b, 0, 0))]
    out_shape = [jax.ShapeDtypeStruct((B, T, DB), F32)]
    if write_state:
        out_specs += [c_spec, n_spec, m_spec]
        out_shape += [
            jax.ShapeDtypeStruct((B, 2, n_pairs, LANES, LANES), F32),
            jax.ShapeDtypeStruct((B, 2, n_pairs, LANES), F32),
            jax.ShapeDtypeStruct((B, 2, H_B), F32),
        ]
    return pl.pallas_call(
        kern,
        grid=(B,),
        in_specs=in_specs,
        out_specs=out_specs,
        out_shape=out_shape,
        scratch_shapes=[
            pltpu.VMEM((T + pad_rows, C2), F32),
            pltpu.VMEM((T, C2), F32),
            pltpu.VMEM((T, SMALL_W), F32),
            pltpu.VMEM((T, DB), F32),
            pltpu.VMEM((T, DB), F32),
            pltpu.VMEM((2, n_pairs, LANES, LANES), F32),
            pltpu.VMEM((8, LANES), F32),
            pltpu.VMEM((8, LANES), F32),
            pltpu.VMEM((2, n_chunks, n_pairs, LANES, LANES), BF16),
            pltpu.VMEM((2, n_chunks, 8, LANES), F32),
            pltpu.VMEM((2, n_chunks, 8, LANES), F32),
            pltpu.VMEM((2, n_chunks, 8, LANES), F32),
            pltpu.VMEM((2, T, SMALL_W), F32),
        ],
        compiler_params=pltpu.CompilerParams(dimension_semantics=("arbitrary",),
                                             vmem_limit_bytes=VMEM_LIMIT),
        name="mlstm_scan",
    )(*args)


def _outff_kernel(x_ref, a_ref, b_ref, mod_ref, n2_ref, fn_ref, wo_ref, w1_ref, w2_ref, y_ref,
                  *, mod_row0, tiles_per_batch, ff_chunk, final_norm):
    D = x_ref.shape[1]
    DA = a_ref.shape[1]
    row = _mod_row(mod_row0, tiles_per_batch)

    def mod(k):
        return mod_ref[pl.ds(row, 1), k * D:(k + 1) * D]

    y = _dot(a_ref[...].astype(BF16), wo_ref[0:DA, :]) + _dot(b_ref[...].astype(BF16), wo_ref[DA:, :])
    x1 = x_ref[...] + mod(2) * y
    h2 = (_rms(x1, n2_ref[...]) * (1.0 + mod(4)) + mod(3)).astype(BF16)
    acc = jnp.zeros(x1.shape, F32)
    for c0 in range(0, w1_ref.shape[1], ff_chunk):
        u = jnp.maximum(_dot(h2, w1_ref[:, c0:c0 + ff_chunk]), 0.0)
        acc = acc + _dot((u * u).astype(BF16), w2_ref[c0:c0 + ff_chunk, :])
    x2 = x1 + mod(5) * acc
    y_ref[...] = _rms(x2, fn_ref[...]) if final_norm else x2


def _outff_call(x2d, a2d, b2d, mod, norm2_w, final_w, wo, w1, w2, *, tm, mod_row0, tiles_per_batch,
                final_norm):
    M, D = x2d.shape
    DA = a2d.shape[1]
    DFF = w1.shape[1]
    kern = functools.partial(_outff_kernel, mod_row0=mod_row0, tiles_per_batch=tiles_per_batch,
                             ff_chunk=512, final_norm=final_norm)
    once = pl.Buffered(1)
    return pl.pallas_call(
        kern,
        grid=(M // tm,),
        in_specs=[
            pl.BlockSpec((tm, D), lambda i: (i, 0)),
            pl.BlockSpec((tm, DA), lambda i: (i, 0)),
            pl.BlockSpec((tm, D - DA), lambda i: (i, 0)),
            pl.BlockSpec(mod.shape, lambda i: (0, 0)),
            pl.BlockSpec((1, D), lambda i: (0, 0)),
            pl.BlockSpec((1, D), lambda i: (0, 0)),
            pl.BlockSpec((D, D), lambda i: (0, 0), pipeline_mode=once),
            pl.BlockSpec((D, DFF), lambda i: (0, 0), pipeline_mode=once),
            pl.BlockSpec((DFF, D), lambda i: (0, 0), pipeline_mode=once),
        ],
        out_specs=pl.BlockSpec((tm, D), lambda i: (i, 0)),
        out_shape=jax.ShapeDtypeStruct((M, D), F32),
        compiler_params=pltpu.CompilerParams(dimension_semantics=("arbitrary",),
                                             vmem_limit_bytes=VMEM_LIMIT),
        name="outproj_mlp",
    )(x2d, a2d, b2d, mod, norm2_w.reshape(1, D), final_w.reshape(1, D), wo, w1, w2)


def _block(x, mod, mod_row0, per_batch, grid_w, states, lw, final_w, final_norm, write_state):
    B, T, D = x.shape
    span = T if per_batch else B * T
    tm = TOKEN_TILE if span % TOKEN_TILE == 0 else T
    tiles_per_batch = (T // tm) if per_batch else None
    x2d = x.reshape(B * T, D)
    z = _inproj_call(x2d, mod, lw["norm1_w"], lw["w_in_t"], tm=tm, mod_row0=mod_row0,
                     tiles_per_batch=tiles_per_batch, big_rows=lw["big_rows"], small_rows=lw["small_rows"])
    z3 = z.reshape(B, T, z.shape[1])
    has_state = states is not None
    n_pairs_a = lw["wal_p"].shape[1] // 2 // LANES
    n_pairs_b = lw["conv9"].shape[1] // 2 // LANES
    s_gla = s_c = s_n = s_m = None
    if has_state:
        s_gla, s_c, s_n, s_m = states
        s_gla = s_gla.reshape(B, 2, n_pairs_a, LANES, LANES)
        s_c = s_c.reshape(B, 2, n_pairs_b, LANES, LANES)
        s_n = s_n.reshape(B, 2, n_pairs_b, LANES)
    res_a = _gla_call(z3, s_gla, lw["wal_p"], lw["bal_p"], lw["gnorm_a_w"],
                      has_state=has_state, write_state=write_state)
    res_b = _mlstm_call(z3, s_c, s_n, s_m, lw["conv9"], lw["bm_row"], lw["gnorm_b_w"],
                        grid_w=grid_w, has_state=has_state, write_state=write_state)
    out_a, out_b = res_a[0], res_b[0]
    new_states = (res_a[1], res_b[1], res_b[2], res_b[3]) if write_state else None
    y = _outff_call(x2d, out_a.reshape(B * T, -1), out_b.reshape(B * T, -1), mod, lw["norm2_w"], final_w,
                    lw["w_out"], lw["w_ff1"], lw["w_ff2"], tm=tm, mod_row0=mod_row0,
                    tiles_per_batch=tiles_per_batch, final_norm=final_norm)
    return y.reshape(B, T, D), new_states


def _layer_weights(l, norm1_w, norm2_w, w_in, w_alpha2, b_alpha, b_mgate, conv_w, gnorm_a_w, gnorm_b_w,
                   w_out, w_ff1, w_ff2):
    D = w_in.shape[1]
    hk_a = w_alpha2.shape[-1]
    d_a = gnorm_a_w.shape[-1]
    d_b = gnorm_b_w.shape[-1]
    hk_b = conv_w.shape[-1] // 2
    sizes = (hk_a, hk_a, d_a, d_a, 2 * R_ALPHA, hk_b, hk_b, d_b, d_b, 4 * H_B)
    offs = [0]
    for s in sizes:
        offs.append(offs[-1] + s)
    big_rows = ((offs[0], offs[4] - offs[0]), (offs[5], offs[9] - offs[5]))
    small_rows = ((offs[4], offs[5] - offs[4]), (offs[9], offs[10] - offs[9]))
    assert all(n % LANES == 0 and r % 16 == 0 for r, n in big_rows)
    wal = w_alpha2[l]
    wal_p = jnp.zeros((SMALL_W, 2 * hk_a), F32)
    wal_p = wal_p.at[0:R_ALPHA, 0:hk_a].set(wal[0]).at[R_ALPHA:2 * R_ALPHA, hk_a:].set(wal[1]).astype(BF16)
    bm_row = jnp.zeros((1, SMALL_W), F32).at[0, GATE_LANE0:GATE_LANE0 + 4 * H_B].set(b_mgate[l].reshape(-1))
    return dict(
        norm1_w=norm1_w[l], norm2_w=norm2_w[l], w_in_t=jnp.swapaxes(w_in[l], 0, 1),
        big_rows=big_rows, small_rows=small_rows, wal_p=wal_p,
        bal_p=b_alpha[l].reshape(1, -1), bm_row=bm_row,
        conv9=conv_w[l].reshape(-1, conv_w.shape[-1]),
        gnorm_a_w=gnorm_a_w[l], gnorm_b_w=gnorm_b_w[l],
        w_out=w_out[l].astype(BF16), w_ff1=w_ff1[l].astype(BF16), w_ff2=w_ff2[l].astype(BF16),
    )


def kernel(x_prompt, x_sample, c, state_gla, state_mlstm_C, state_mlstm_n, state_mlstm_m, c_ctx, w_ada, b_ada, norm1_w, norm2_w, w_in, w_alpha2, b_alpha, b_mgate, conv_w, gnorm_a_w, gnorm_b_w, w_out, w_ff1, w_ff2, final_norm_w):
    depth = w_in.shape[0]
    D = x_prompt.shape[-1]
    Bp, Tp, _ = x_prompt.shape
    Bs = x_sample.shape[0]
    assert 1 + Bs <= COND_ROWS
    cond = jnp.concatenate([c_ctx[None, :], c, jnp.zeros((COND_ROWS - 1 - Bs, D), F32)], axis=0)
    xp, xs = x_prompt, x_sample
    s_gla, s_c, s_n, s_m = [], [], [], []
    for l in range(depth):
        lw = _layer_weights(l, norm1_w, norm2_w, w_in, w_alpha2, b_alpha, b_mgate, conv_w,
                            gnorm_a_w, gnorm_b_w, w_out, w_ff1, w_ff2)
        mod = _ada_call(cond, w_ada[l], b_ada[l])
        last = l == depth - 1
        xp, ctx = _block(xp, mod, 0, False, Tp, None, lw, final_norm_w, last, True)
        s_gla.append(ctx[0].reshape(Bp, 2, H_A, -1, ctx[0].shape[-1]))
        s_c.append(ctx[1].reshape(Bp, 2, H_B, -1, ctx[1].shape[-1]))
        s_n.append(ctx[2].reshape(Bp, 2, H_B, -1))
        s_m.append(ctx[3])
        cached = (state_gla[:, l], state_mlstm_C[:, l], state_mlstm_n[:, l], state_mlstm_m[:, l])
        xs, _ = _block(xs, mod, 1, True, GRID_W, cached, lw, final_norm_w, last, False)
    dt = x_prompt.dtype
    return (xp, xs, jnp.stack(s_gla, axis=1).astype(dt), jnp.stack(s_c, axis=1).astype(dt),
            jnp.stack(s_n, axis=1).astype(dt), jnp.stack(s_m, axis=1).astype(dt))
```

```python
import functools

import jax
import jax.numpy as jnp
from jax import lax
from jax.experimental import pallas as pl
from jax.experimental.pallas import tpu as pltpu

F32 = jnp.float32
BF16 = jnp.bfloat16

GRID_W = 64
H_A = 4
H_B = 4
R_ALPHA = 16
TAU_GLA = 16.0
CHUNK = 64
EPS = 1e-6
LANES = 128
COND_ROWS = 8
SMALL_W = LANES
GATE_LANE0 = 2 * R_ALPHA
VMEM_LIMIT = 56 * 1024 * 1024
SCAN_UNROLL = 4
TOKEN_TILE = 512


def _sigmoid(x):
    return 1.0 / (1.0 + jnp.exp(-x))


def _silu(x):
    return x * _sigmoid(x)


def _log_sigmoid(x):
    return jnp.minimum(x, 0.0) - jnp.log1p(jnp.exp(-jnp.abs(x)))


def _dot(a, b):
    return jnp.dot(a, b, preferred_element_type=F32)


def _dot_nt(a, b):
    return lax.dot_general(a, b, (((1,), (1,)), ((), ())), preferred_element_type=F32)


def _rms(x, w):
    return x * lax.rsqrt(jnp.mean(x * x, axis=-1, keepdims=True) + EPS) * w


def _tri_sum(tri, x):
    hi = x.astype(BF16)
    r1 = x - hi.astype(F32)
    mid = r1.astype(BF16)
    lo = (r1 - mid.astype(F32)).astype(BF16)
    return _dot(tri, hi) + _dot(tri, mid) + _dot(tri, lo)


def _chunk_masks(L):
    row = lax.broadcasted_iota(jnp.int32, (L, L), 0)
    col = lax.broadcasted_iota(jnp.int32, (L, L), 1)
    lower = row >= col
    upper = row <= col
    return lower, upper


def _ada_kernel(c_ref, w_ref, b_ref, o_ref):
    s = _silu(c_ref[...])
    o_ref[...] = _dot(s.astype(BF16), w_ref[...].astype(BF16)) + b_ref[...]


def _ada_call(cond, w_ada, b_ada):
    D = cond.shape[1]
    n_out = w_ada.shape[1]
    tn = 1024
    return pl.pallas_call(
        _ada_kernel,
        grid=(n_out // tn,),
        in_specs=[
            pl.BlockSpec((COND_ROWS, D), lambda j: (0, 0)),
            pl.BlockSpec((D, tn), lambda j: (0, j)),
            pl.BlockSpec((1, tn), lambda j: (0, j)),
        ],
        out_specs=pl.BlockSpec((COND_ROWS, tn), lambda j: (0, j)),
        out_shape=jax.ShapeDtypeStruct((COND_ROWS, n_out), F32),
        compiler_params=pltpu.CompilerParams(dimension_semantics=("arbitrary",)),
        name="ada_mod",
    )(cond, w_ada, b_ada.reshape(1, n_out))


def _mod_row(mod_row0, tiles_per_batch):
    if tiles_per_batch is None:
        return mod_row0
    return mod_row0 + pl.program_id(0) // tiles_per_batch


def _inproj_kernel(x_ref, mod_ref, nw_ref, wt_ref, z_ref, wb_scr, *, mod_row0, tiles_per_batch, big_rows,
                   small_rows):
    D = x_ref.shape[1]

    @pl.when(pl.program_id(0) == 0)
    def _():
        col = 0
        for r0, n in big_rows:
            for k in range(n // LANES):
                blk = wt_ref[r0 + k * LANES:r0 + (k + 1) * LANES, :]
                wb_scr[:, col:col + LANES] = blk.T.astype(BF16)
                col += LANES
        parts = [wt_ref[r0:r0 + n, :] for r0, n in small_rows]
        n_small = sum(n for _, n in small_rows)
        parts.append(jnp.zeros((SMALL_W - n_small, D), F32))
        wb_scr[:, col:col + SMALL_W] = jnp.concatenate(parts, axis=0).T.astype(BF16)

    row = _mod_row(mod_row0, tiles_per_batch)
    sh1 = mod_ref[pl.ds(row, 1), 0:D]
    sc1 = mod_ref[pl.ds(row, 1), D:2 * D]
    h = _rms(x_ref[...], nw_ref[...]) * (1.0 + sc1) + sh1
    z_ref[...] = _dot(h.astype(BF16), wb_scr[...])


def _inproj_call(x2d, mod, norm_w, w_in_t, *, tm, mod_row0, tiles_per_batch, big_rows, small_rows):
    M, D = x2d.shape
    n_out = sum(n for _, n in big_rows) + SMALL_W
    kern = functools.partial(_inproj_kernel, mod_row0=mod_row0, tiles_per_batch=tiles_per_batch,
                             big_rows=big_rows, small_rows=small_rows)
    return pl.pallas_call(
        kern,
        grid=(M // tm,),
        in_specs=[
            pl.BlockSpec((tm, D), lambda i: (i, 0)),
            pl.BlockSpec(mod.shape, lambda i: (0, 0)),
            pl.BlockSpec((1, D), lambda i: (0, 0)),
            pl.BlockSpec(w_in_t.shape, lambda i: (0, 0), pipeline_mode=pl.Buffered(1)),
        ],
        out_specs=pl.BlockSpec((tm, n_out), lambda i: (i, 0)),
        out_shape=jax.ShapeDtypeStruct((M, n_out), F32),
        scratch_shapes=[pltpu.VMEM((D, n_out), BF16)],
        compiler_params=pltpu.CompilerParams(dimension_semantics=("arbitrary",),
                                             vmem_limit_bytes=VMEM_LIMIT),
        name="norm_inproj",
    )(x2d, mod, norm_w.reshape(1, D), w_in_t)


def _chunk_loop(n_chunks, unroll, fn):
    if unroll >= n_chunks:
        fn(list(range(n_chunks)))
        return

    def body(i, carry):
        fn([i * unroll + u for u in range(unroll)])
        return carry

    lax.fori_loop(0, n_chunks // unroll, body, 0)


def _chunk_rows(n):
    if isinstance(n, int):
        return pl.ds(n * CHUNK, CHUNK)
    return pl.ds(pl.multiple_of(n * CHUNK, CHUNK), CHUNK)


def _gla_kernel(*refs, has_state, write_state, unroll):
    refs = list(refs)
    q_ref, k_ref, v_ref, g_ref, sm_ref = refs[:5]
    del refs[:5]
    s0_ref = refs.pop(0) if has_state else None
    wal_ref, bal_ref, gw_ref, out_ref = refs[:4]
    del refs[:4]
    snew_ref = refs.pop(0) if write_state else None
    of_scr, ob_scr, st_scr, sall_scr, qh_scr, qs_scr, kh_scr = refs
    T = q_ref.shape[0]
    L = CHUNK
    N = T // L
    HK = q_ref.shape[1]
    DK = HK // H_A
    DV = v_ref.shape[1] // H_A
    scale = DK ** -0.5
    n_pairs = HK // LANES

    lower, upper = _chunk_masks(L)
    tri = (lower.astype(BF16), upper.astype(BF16))
    tmask = (lower, upper)
    lane = lax.broadcasted_iota(jnp.int32, (1, LANES), 1)
    head_mask = (lane < DK, lane >= DK)
    o_scr = (of_scr, ob_scr)

    for d in range(2):
        for p in range(n_pairs):
            if has_state:
                st_scr[d, p] = s0_ref[d, p].T
            else:
                st_scr[d, p] = jnp.zeros((LANES, LANES), F32)

    def state_group(ns):
        units = [u for n in ns for u in ((0, n), (1, N - 1 - n))]
        rows = [_chunk_rows(n) for _, n in units]
        pre = [_dot(sm_ref[r, :].astype(BF16), wal_ref[:, d * HK:(d + 1) * HK]) + bal_ref[:, d * HK:(d + 1) * HK]
               for (d, _), r in zip(units, rows)]
        g = [_log_sigmoid(x) * (1.0 / TAU_GLA) for x in pre]
        b = [_tri_sum(tri[d], gi) for (d, _), gi in zip(units, g)]
        ks_all, dec_all = [], []
        for (d, _), r, bi in zip(units, rows, b):
            bend = bi[L - 1:L, :] if d == 0 else bi[0:1, :]
            q = q_ref[r, :] * scale
            ks = (k_ref[r, :] * jnp.exp(bend - bi)).astype(BF16)
            qh_scr[d, r, :] = (q * jnp.exp(bi - bend)).astype(BF16)
            qs_scr[d, r, :] = (q * jnp.exp(bi)).astype(BF16)
            kh_scr[d, r, :] = ks
            ks_all.append(ks)
            dec_all.append(jnp.exp(bend))
        upd_all = [[[_dot(v_ref[r, (2 * p + j) * DV:(2 * p + j + 1) * DV].T.astype(BF16),
                          ks[:, p * LANES:(p + 1) * LANES]) for j in range(2)]
                    for p in range(n_pairs)]
                   for r, ks in zip(rows, ks_all)]
        for (d, n), dec, upd in zip(units, dec_all, upd_all):
            for p in range(n_pairs):
                st = st_scr[d, p]
                sall_scr[d, n, p] = st.astype(BF16)
                st_scr[d, p] = (st * dec[:, p * LANES:(p + 1) * LANES]
                                + jnp.where(head_mask[0], upd[p][0], upd[p][1]))

    _chunk_loop(N, unroll, state_group)

    def out_group(ns):
        units = [(d, n, p, j) for n in ns for d in range(2) for p in range(n_pairs) for j in range(2)]
        scores, inter = [], []
        for d, n, p, j in units:
            r = _chunk_rows(n)
            ls = slice(p * LANES, (p + 1) * LANES)
            qh = qh_scr[d, r, ls]
            qs = qs_scr[d, r, ls]
            qm = jnp.where(head_mask[j], qh, jnp.zeros_like(qh))
            qsm = jnp.where(head_mask[j], qs, jnp.zeros_like(qs))
            scores.append(_dot_nt(qm, kh_scr[d, r, ls]))
            inter.append(_dot_nt(qsm, sall_scr[d, n, p]))
        probs = [jnp.where(tmask[d], a, 0.0).astype(BF16) for (d, _, _, _), a in zip(units, scores)]
        for (d, n, p, j), a, it in zip(units, probs, inter):
            vs = slice((2 * p + j) * DV, (2 * p + j + 1) * DV)
            r = _chunk_rows(n)
            o_scr[d][r, vs] = _dot(a, v_ref[r, vs].astype(BF16)) + it

    _chunk_loop(N, unroll, out_group)

    def epilogue(i, carry):
        rows = pl.ds(pl.multiple_of(i * L, L), L)
        for h in range(H_A):
            vs = slice(h * DV, (h + 1) * DV)
            o = of_scr[rows, vs] + ob_scr[rows, vs]
            out_ref[rows, vs] = _rms(o, gw_ref[:, vs]) * _silu(g_ref[rows, vs])
        return carry

    lax.fori_loop(0, N, epilogue, 0)

    if write_state:
        for d in range(2):
            for p in range(n_pairs):
                snew_ref[d, p] = st_scr[d, p].T


def _gla_call(z3, s0, wal_p, bal_p, gw, *, has_state, write_state):
    B, T, _ = z3.shape
    HK = wal_p.shape[1] // 2
    DA = gw.shape[0]
    n_pairs = HK // LANES
    small_blk = (z3.shape[2] - SMALL_W) // SMALL_W
    n_chunks = T // CHUNK
    kern = functools.partial(_gla_kernel, has_state=has_state, write_state=write_state,
                             unroll=min(n_chunks, SCAN_UNROLL))
    state_shape = (B, 2, n_pairs, LANES, LANES)
    state_spec = pl.BlockSpec((None, 2, n_pairs, LANES, LANES), lambda b: (b, 0, 0, 0, 0))
    in_specs = [
        pl.BlockSpec((None, T, HK), lambda b: (b, 0, 0)),
        pl.BlockSpec((None, T, HK), lambda b: (b, 0, 1)),
        pl.BlockSpec((None, T, DA), lambda b: (b, 0, 1)),
        pl.BlockSpec((None, T, DA), lambda b: (b, 0, 2)),
        pl.BlockSpec((None, T, SMALL_W), lambda b: (b, 0, small_blk)),
    ]
    args = [z3, z3, z3, z3, z3]
    if has_state:
        in_specs.append(state_spec)
        args.append(s0)
    in_specs += [
        pl.BlockSpec(wal_p.shape, lambda b: (0, 0)),
        pl.BlockSpec(bal_p.shape, lambda b: (0, 0)),
        pl.BlockSpec((1, DA), lambda b: (0, 0)),
    ]
    args += [wal_p, bal_p, gw.reshape(1, DA)]
    out_specs = [pl.BlockSpec((None, T, DA), lambda b: (b, 0, 0))]
    out_shape = [jax.ShapeDtypeStruct((B, T, DA), F32)]
    if write_state:
        out_specs.append(state_spec)
        out_shape.append(jax.ShapeDtypeStruct(state_shape, F32))
    return pl.pallas_call(
        kern,
        grid=(B,),
        in_specs=in_specs,
        out_specs=out_specs,
        out_shape=out_shape,
        scratch_shapes=[
            pltpu.VMEM((T, DA), F32),
            pltpu.VMEM((T, DA), F32),
            pltpu.VMEM((2, n_pairs, LANES, LANES), F32),
            pltpu.VMEM((2, n_chunks, n_pairs, LANES, LANES), BF16),
            pltpu.VMEM((2, T, HK), BF16),
            pltpu.VMEM((2, T, HK), BF16),
            pltpu.VMEM((2, T, HK), BF16),
        ],
        compiler_params=pltpu.CompilerParams(dimension_semantics=("arbitrary",),
                                             vmem_limit_bytes=VMEM_LIMIT),
        name="gla_scan",
    )(*args)


def _mlstm_kernel(*refs, grid_w, has_state, write_state, unroll):
    refs = list(refs)
    qk_ref, v_ref, og_ref, sm_ref = refs[:4]
    del refs[:4]
    if has_state:
        c0_ref, n0_ref, m0_ref = refs[:3]
        del refs[:3]
    cw_ref, bm_ref, gw_ref, out_ref = refs[:4]
    del refs[:4]
    if write_state:
        cnew_ref, nnew_ref, mnew_ref = refs[:3]
        del refs[:3]
    (pad_scr, qk_scr, y_scr, c_scr, n_scr, m_scr,
     call_scr, nall_scr, mall_scr, g_scr, f_scr) = refs
    T = qk_ref.shape[0]
    L = CHUNK
    N = T // L
    C2 = qk_ref.shape[1]
    HK = C2 // 2
    DK = HK // H_B
    DV = v_ref.shape[1] // H_B
    scale = DK ** -0.5
    n_pairs = HK // LANES
    P = pad_scr.shape[0] - T
    P0 = P // 2
    rows_img = T // grid_w

    lower, upper = _chunk_masks(L)
    tri = (lower.astype(BF16), upper.astype(BF16))
    tmask = (lower, upper)
    lane = lax.broadcasted_iota(jnp.int32, (1, LANES), 1)
    head_mask = (lane < DK, lane >= DK)
    lane_in = lane & (L - 1)

    def lane_cummax(x, d):
        k = 1
        while k < L:
            if d == 0:
                x = jnp.maximum(x, jnp.where(lane_in >= k, pltpu.roll(x, k, axis=1), -jnp.inf))
            else:
                x = jnp.maximum(x, jnp.where(lane_in < L - k, pltpu.roll(x, LANES - k, axis=1), -jnp.inf))
            k *= 2
        return x

    for d in range(2):
        for p in range(n_pairs):
            if has_state:
                c_scr[d, p] = c0_ref[d, p]
                n_scr[2 * d + p:2 * d + p + 1, :] = n0_ref[d, p:p + 1, :]
            else:
                c_scr[d, p] = jnp.zeros((LANES, LANES), F32)
                n_scr[2 * d + p:2 * d + p + 1, :] = jnp.zeros((1, LANES), F32)
    eye_h = (lax.broadcasted_iota(jnp.int32, (H_B, H_B), 0) == lax.broadcasted_iota(jnp.int32, (H_B, H_B), 1))

    def to_col(row):
        return jnp.sum(jnp.where(eye_h, row, 0.0), axis=1, keepdims=True)

    def to_row(col):
        return jnp.sum(jnp.where(eye_h, col, 0.0), axis=0, keepdims=True)

    for d in range(2):
        if has_state:
            m_scr[H_B * d:H_B * (d + 1), 0:1] = to_col(m0_ref[d:d + 1, :])
        else:
            m_scr[H_B * d:H_B * (d + 1), 0:1] = jnp.zeros((H_B, 1), F32)

    pad_scr[0:P0, :] = jnp.zeros((P0, C2), F32)
    pad_scr[P0 + T:P + T, :] = jnp.zeros((P - P0, C2), F32)

    def copy_in(i, carry):
        r0 = pl.multiple_of(i * L, L)
        pad_scr[pl.ds(P0 + r0, L), :] = qk_ref[pl.ds(r0, L), :]
        return carry

    lax.fori_loop(0, N, copy_in, 0)

    lane_c = lax.broadcasted_iota(jnp.int32, (1, C2), 1)
    qscale = jnp.where(lane_c < HK, scale, 1.0).astype(F32)
    sub = lax.broadcasted_iota(jnp.int32, (L, 1), 0)
    img_rows = (0,) if rows_img == 1 else (-1, 0, 1)

    def conv_tile(i, carry):
        r0 = pl.multiple_of(i * L, L)
        col = lax.rem(r0, grid_w) + sub
        ok_left = col >= 1
        ok_right = col <= grid_w - 2
        acc = jnp.zeros((L, C2), F32)
        for di in img_rows:
            blk = pad_scr[pl.ds(P0 + r0 + di * grid_w - 8, L + 16), :]
            left = jnp.where(ok_left, blk[7:7 + L, :], 0.0)
            mid = blk[8:8 + L, :]
            right = jnp.where(ok_right, blk[9:9 + L, :], 0.0)
            wr = 3 * (di + 1)
            acc = acc + left * cw_ref[wr:wr + 1, :] + mid * cw_ref[wr + 1:wr + 2, :] + right * cw_ref[wr + 2:wr + 3, :]
        qk_scr[pl.ds(r0, L), :] = _silu(acc) * qscale
        return carry

    lax.fori_loop(0, N, conv_tile, 0)

    gl = lane - GATE_LANE0
    is_f = ((gl >= H_B) & (gl < 2 * H_B)) | ((gl >= 3 * H_B) & (gl < 4 * H_B))

    def gate_tile(i, carry):
        rows = pl.ds(pl.multiple_of(i * L, L), L)
        x = sm_ref[rows, :] + bm_ref[...]
        y_scr[rows, :] = jnp.where(is_f, _log_sigmoid(x), x)
        return carry

    lax.fori_loop(0, N, gate_tile, 0)


    def state_group(ns):
        units = [u for n in ns for u in ((0, n), (1, N - 1 - n))]
        rows = [_chunk_rows(n) for _, n in units]
        xs = [y_scr[r, :] for r in rows]
        fsum = [_tri_sum(tri[d], x) for (d, _), x in zip(units, xs)]
        wk_all, f_end, c_end = [], [], []
        for (d, n), r, x, fs in zip(units, rows, xs, fsum):
            y = jnp.where(is_f, fs, x)
            li0 = GATE_LANE0 + 2 * H_B * d
            blk = jnp.concatenate([y, y], axis=0).T[li0:li0 + 2 * H_B, :]
            frow = pltpu.roll(blk, H_B, axis=0)
            grow = blk - frow
            g_scr[d, n] = grow
            f_scr[d, n] = frow
            e_col = L - 1 if d == 0 else 0
            f_end.append(frow[0:H_B, e_col:e_col + 1])
            ce = jnp.max(grow[0:H_B, :], axis=1, keepdims=True)
            c_end.append(ce)
            wk_all.append(jnp.exp(grow[0:H_B, 0:L] - ce))
        kv_all, ksum_all = [], []
        for r, wk4 in zip(rows, wk_all):
            kv_u, ks_u = [], []
            for p in range(n_pairs):
                kp = qk_scr[r, HK + p * LANES:HK + (p + 1) * LANES]
                kpb = kp.astype(BF16)
                kt = kp.T
                for j in range(2):
                    h = 2 * p + j
                    wk = wk4[h:h + 1, :]
                    kwt = (kt[j * DK:(j + 1) * DK, :] * wk).astype(BF16)
                    kv_u.append(_dot(kwt, v_ref[r, h * DV:(h + 1) * DV].astype(BF16)))
                    ks_u.append(_dot(jnp.broadcast_to(wk, (8, L)).astype(BF16), kpb)[0:1, :])
            kv_all.append(kv_u)
            ksum_all.append(ks_u)
        for (d, n), fe, ce, kv_u, ks_u in zip(units, f_end, c_end, kv_all, ksum_all):
            m_prev = m_scr[H_B * d:H_B * (d + 1), 0:1]
            mall_scr[d, n, 0:H_B, 0:1] = m_prev
            mx = jnp.maximum(m_prev, ce)
            a_all = jnp.exp(m_prev - mx)
            b_all = jnp.exp(ce - mx)
            m_scr[H_B * d:H_B * (d + 1), 0:1] = fe + mx
            for p in range(n_pairs):
                npair = n_scr[2 * d + p:2 * d + p + 1, :]
                nall_scr[d, n, p:p + 1, :] = npair
                a_s = [a_all[2 * p + j:2 * p + j + 1, :] for j in range(2)]
                b_s = [b_all[2 * p + j:2 * p + j + 1, :] for j in range(2)]
                for j in range(2):
                    hr = slice(j * DK, (j + 1) * DK)
                    cj = c_scr[d, p, hr, :]
                    call_scr[d, n, p, hr, :] = cj.astype(BF16)
                    c_scr[d, p, hr, :] = a_s[j] * cj + b_s[j] * kv_u[2 * p + j]
                n_scr[2 * d + p:2 * d + p + 1, :] = (
                    jnp.where(head_mask[0], a_s[0], a_s[1]) * npair
                    + jnp.where(head_mask[0], b_s[0] * ks_u[2 * p], b_s[1] * ks_u[2 * p + 1]))

    _chunk_loop(N, unroll, state_group)

    eye2 = ((lax.broadcasted_iota(jnp.int32, (L, LANES), 1) & (L - 1))
            == lax.broadcasted_iota(jnp.int32, (L, LANES), 0))
    ones8 = jnp.ones((8, L), BF16)
    sub_h = lax.broadcasted_iota(jnp.int32, (H_B, LANES), 0)

    def head_rows(vals):
        out = vals[0][0:H_B, :]
        for h in range(1, H_B):
            out = jnp.where(sub_h == h, vals[h][0:H_B, :], out)
        return out

    def out_group(ns):
        chunks = [(d, n) for n in ns for d in range(2)]
        units = [(d, n, p, j) for d, n in chunks for p in range(n_pairs) for j in range(2)]
        cms = [lane_cummax(g_scr[d, n], d)[0:H_B, :] for d, n in chunks]
        qks, qcs, qns = [], [], []
        for d, n, p, j in units:
            r = _chunk_rows(n)
            qmb = jnp.where(head_mask[j], qk_scr[r, p * LANES:(p + 1) * LANES], 0.0).astype(BF16)
            qks.append(_dot_nt(qmb, qk_scr[r, HK + p * LANES:HK + (p + 1) * LANES].astype(BF16)))
            qcs.append(_dot(qmb, call_scr[d, n, p]))
            n8 = jnp.broadcast_to(nall_scr[d, n, p:p + 1, :], (8, LANES)).astype(BF16)
            qns.append(_dot_nt(n8, jnp.concatenate([qmb, qmb], axis=0)))
        s_all = []
        for (d, n, p, j), qk in zip(units, qks):
            grow = g_scr[d, n, 2 * p + j:2 * p + j + 1, 0:L]
            e = jnp.where(tmask[d], grow, -jnp.inf)
            cmax = jnp.max(e, axis=-1, keepdims=True)
            s_all.append((qk * jnp.exp(e - cmax)).astype(BF16))
        nums = [_dot(s, v_ref[_chunk_rows(n), (2 * p + j) * DV:(2 * p + j + 1) * DV].astype(BF16))
                for (d, n, p, j), s in zip(units, s_all)]
        dens = [_dot_nt(ones8, jnp.concatenate([s, s], axis=0)) for s in s_all]
        scales = []
        for ci, (d, n) in enumerate(chunks):
            den_loc = head_rows(dens[ci * H_B:(ci + 1) * H_B])
            qn = head_rows(qns[ci * H_B:(ci + 1) * H_B])
            cm = cms[ci]
            m_prev = mall_scr[d, n, 0:H_B, 0:1]
            delta = cm - m_prev
            t = jnp.exp(-jnp.abs(delta))
            w_loc = jnp.where(delta <= 0.0, t, 1.0)
            w_inter = jnp.where(delta <= 0.0, 1.0, t)
            mt = f_scr[d, n, 0:H_B, :] + jnp.maximum(m_prev, cm)
            den = w_loc * den_loc + w_inter * qn
            rinv = 1.0 / jnp.maximum(jnp.abs(den), jnp.exp(-mt))
            scales.append(jnp.where(lane < L, w_loc * rinv, w_inter * rinv))
        hs = []
        for ui, (d, n, p, j) in enumerate(units):
            h = 2 * p + j
            lhs = jnp.where(eye2, scales[ui // H_B][h:h + 1, :], 0.0).astype(BF16)
            rhs = jnp.concatenate([nums[ui].astype(BF16), qcs[ui].astype(BF16)], axis=0)
            hs.append(_dot(lhs, rhs))
        for ni, n in enumerate(ns):
            r = _chunk_rows(n)
            for h in range(H_B):
                vs = slice(h * DV, (h + 1) * DV)
                o = hs[(2 * ni) * H_B + h] + hs[(2 * ni + 1) * H_B + h]
                out_ref[r, vs] = _rms(o, gw_ref[:, vs]) * _sigmoid(og_ref[r, vs])

    _chunk_loop(N, unroll, out_group)

    if write_state:
        for d in range(2):
            for p in range(n_pairs):
                cnew_ref[d, p] = c_scr[d, p]
                nnew_ref[d, p:p + 1, :] = n_scr[2 * d + p:2 * d + p + 1, :]
            mnew_ref[d:d + 1, :] = to_row(m_scr[H_B * d:H_B * (d + 1), 0:1])


def _mlstm_call(z3, c0, n0, m0, conv9, bm_row, gw, *, grid_w, has_state, write_state):
    B, T, _ = z3.shape
    C2 = conv9.shape[1]
    HK = C2 // 2
    DB = gw.shape[0]
    n_pairs = HK // LANES
    small_blk = (z3.shape[2] - SMALL_W) // SMALL_W
    qk_blk = (3 * DB) // C2
    v_blk = (3 * DB + C2) // DB
    pad_rows = 2 * (grid_w + 8) if T // grid_w > 1 else 16
    n_chunks = T // CHUNK
    kern = functools.partial(_mlstm_kernel, grid_w=grid_w, has_state=has_state, write_state=write_state,
                             unroll=min(n_chunks, SCAN_UNROLL))
    c_spec = pl.BlockSpec((None, 2, n_pairs, LANES, LANES), lambda b: (b, 0, 0, 0, 0))
    n_spec = pl.BlockSpec((None, 2, n_pairs, LANES), lambda b: (b, 0, 0, 0))
    m_spec = pl.BlockSpec((None, 2, H_B), lambda b: (b, 0, 0))
    in_specs = [
        pl.BlockSpec((None, T, C2), lambda b: (b, 0, qk_blk)),
        pl.BlockSpec((None, T, DB), lambda b: (b, 0, v_blk)),
        pl.BlockSpec((None, T, DB), lambda b: (b, 0, v_blk + 1)),
        pl.BlockSpec((None, T, SMALL_W), lambda b: (b, 0, small_blk)),
    ]
    args = [z3, z3, z3, z3]
    if has_state:
        in_specs += [c_spec, n_spec, m_spec]
        args += [c0, n0, m0]
    in_specs += [
        pl.BlockSpec(conv9.shape, lambda b: (0, 0)),
        pl.BlockSpec((1, SMALL_W), lambda b: (0, 0)),
        pl.BlockSpec((1, DB), lambda b: (0, 0)),
    ]
    args += [conv9, bm_row, gw.reshape(1, DB)]
    out_specs = [pl.BlockSpec((None, T, DB), lambda b: (b, 0, 0))]
    out_shape = [jax.ShapeDtypeStruct((B, T, DB), F32)]
    if write_state:
        out_specs += [c_spec, n_spec, m_spec]
        out_shape += [
            jax.ShapeDtypeStruct((B, 2, n_pairs, LANES, LANES), F32),
            jax.ShapeDtypeStruct((B, 2, n_pairs, LANES), F32),
            jax.ShapeDtypeStruct((B, 2, H_B), F32),
        ]
    return pl.pallas_call(
        kern,
        grid=(B,),
        in_specs=in_specs,
        out_specs=out_specs,
        out_shape=out_shape,
        scratch_shapes=[
            pltpu.VMEM((T + pad_rows, C2), F32),
            pltpu.VMEM((T, C2), F32),
            pltpu.VMEM((T, SMALL_W), F32),
            pltpu.VMEM((2, n_pairs, LANES, LANES), F32),
            pltpu.VMEM((8, LANES), F32),
            pltpu.VMEM((8, LANES), F32),
            pltpu.VMEM((2, n_chunks, n_pairs, LANES, LANES), BF16),
            pltpu.VMEM((2, n_chunks, 8, LANES), F32),
            pltpu.VMEM((2, n_chunks, 8, LANES), F32),
            pltpu.VMEM((2, n_chunks, 8, LANES), F32),
            pltpu.VMEM((2, n_chunks, 8, LANES), F32),
        ],
        compiler_params=pltpu.CompilerParams(dimension_semantics=("arbitrary",),
                                             vmem_limit_bytes=VMEM_LIMIT),
        name="mlstm_scan",
    )(*args)


def _outff_kernel(x_ref, a_ref, b_ref, mod_ref, n2_ref, fn_ref, wo_ref, w1_ref, w2_ref, y_ref,
                  *, mod_row0, tiles_per_batch, ff_chunk, final_norm):
    D = x_ref.shape[1]
    DA = a_ref.shape[1]
    row = _mod_row(mod_row0, tiles_per_batch)

    def mod(k):
        return mod_ref[pl.ds(row, 1), k * D:(k + 1) * D]

    y = _dot(a_ref[...].astype(BF16), wo_ref[0:DA, :]) + _dot(b_ref[...].astype(BF16), wo_ref[DA:, :])
    x1 = x_ref[...] + mod(2) * y
    h2 = (_rms(x1, n2_ref[...]) * (1.0 + mod(4)) + mod(3)).astype(BF16)
    acc = jnp.zeros(x1.shape, F32)
    for c0 in range(0, w1_ref.shape[1], ff_chunk):
        u = jnp.maximum(_dot(h2, w1_ref[:, c0:c0 + ff_chunk]), 0.0)
        acc = acc + _dot((u * u).astype(BF16), w2_ref[c0:c0 + ff_chunk, :])
    x2 = x1 + mod(5) * acc
    y_ref[...] = _rms(x2, fn_ref[...]) if final_norm else x2


def _outff_call(x2d, a2d, b2d, mod, norm2_w, final_w, wo, w1, w2, *, tm, mod_row0, tiles_per_batch,
                final_norm):
    M, D = x2d.shape
    DA = a2d.shape[1]
    DFF = w1.shape[1]
    kern = functools.partial(_outff_kernel, mod_row0=mod_row0, tiles_per_batch=tiles_per_batch,
                             ff_chunk=512, final_norm=final_norm)
    once = pl.Buffered(1)
    return pl.pallas_call(
        kern,
        grid=(M // tm,),
        in_specs=[
            pl.BlockSpec((tm, D), lambda i: (i, 0)),
            pl.BlockSpec((tm, DA), lambda i: (i, 0)),
            pl.BlockSpec((tm, D - DA), lambda i: (i, 0)),
            pl.BlockSpec(mod.shape, lambda i: (0, 0)),
            pl.BlockSpec((1, D), lambda i: (0, 0)),
            pl.BlockSpec((1, D), lambda i: (0, 0)),
            pl.BlockSpec((D, D), lambda i: (0, 0), pipeline_mode=once),
            pl.BlockSpec((D, DFF), lambda i: (0, 0), pipeline_mode=once),
            pl.BlockSpec((DFF, D), lambda i: (0, 0), pipeline_mode=once),
        ],
        out_specs=pl.BlockSpec((tm, D), lambda i: (i, 0)),
        out_shape=jax.ShapeDtypeStruct((M, D), F32),
        compiler_params=pltpu.CompilerParams(dimension_semantics=("arbitrary",),
                                             vmem_limit_bytes=VMEM_LIMIT),
        name="outproj_mlp",
    )(x2d, a2d, b2d, mod, norm2_w.reshape(1, D), final_w.reshape(1, D), wo, w1, w2)


def _block(x, mod, mod_row0, per_batch, grid_w, states, lw, final_w, final_norm, write_state):
    B, T, D = x.shape
    span = T if per_batch else B * T
    tm = TOKEN_TILE if span % TOKEN_TILE == 0 else T
    tiles_per_batch = (T // tm) if per_batch else None
    x2d = x.reshape(B * T, D)
    z = _inproj_call(x2d, mod, lw["norm1_w"], lw["w_in_t"], tm=tm, mod_row0=mod_row0,
                     tiles_per_batch=tiles_per_batch, big_rows=lw["big_rows"], small_rows=lw["small_rows"])
    z3 = z.reshape(B, T, z.shape[1])
    has_state = states is not None
    n_pairs_a = lw["wal_p"].shape[1] // 2 // LANES
    n_pairs_b = lw["conv9"].shape[1] // 2 // LANES
    s_gla = s_c = s_n = s_m = None
    if has_state:
        s_gla, s_c, s_n, s_m = states
        s_gla = s_gla.reshape(B, 2, n_pairs_a, LANES, LANES)
        s_c = s_c.reshape(B, 2, n_pairs_b, LANES, LANES)
        s_n = s_n.reshape(B, 2, n_pairs_b, LANES)
    res_a = _gla_call(z3, s_gla, lw["wal_p"], lw["bal_p"], lw["gnorm_a_w"],
                      has_state=has_state, write_state=write_state)
    res_b = _mlstm_call(z3, s_c, s_n, s_m, lw["conv9"], lw["bm_row"], lw["gnorm_b_w"],
                        grid_w=grid_w, has_state=has_state, write_state=write_state)
    out_a, out_b = res_a[0], res_b[0]
    new_states = (res_a[1], res_b[1], res_b[2], res_b[3]) if write_state else None
    y = _outff_call(x2d, out_a.reshape(B * T, -1), out_b.reshape(B * T, -1), mod, lw["norm2_w"], final_w,
                    lw["w_out"], lw["w_ff1"], lw["w_ff2"], tm=tm, mod_row0=mod_row0,
                    tiles_per_batch=tiles_per_batch, final_norm=final_norm)
    return y.reshape(B, T, D), new_states


def _layer_weights(l, norm1_w, norm2_w, w_in, w_alpha2, b_alpha, b_mgate, conv_w, gnorm_a_w, gnorm_b_w,
                   w_out, w_ff1, w_ff2):
    D = w_in.shape[1]
    hk_a = w_alpha2.shape[-1]
    d_a = gnorm_a_w.shape[-1]
    d_b = gnorm_b_w.shape[-1]
    hk_b = conv_w.shape[-1] // 2
    sizes = (hk_a, hk_a, d_a, d_a, 2 * R_ALPHA, hk_b, hk_b, d_b, d_b, 4 * H_B)
    offs = [0]
    for s in sizes:
        offs.append(offs[-1] + s)
    big_rows = ((offs[0], offs[4] - offs[0]), (offs[5], offs[9] - offs[5]))
    small_rows = ((offs[4], offs[5] - offs[4]), (offs[9], offs[10] - offs[9]))
    assert all(n % LANES == 0 and r % 16 == 0 for r, n in big_rows)
    wal = w_alpha2[l]
    wal_p = jnp.zeros((SMALL_W, 2 * hk_a), F32)
    wal_p = wal_p.at[0:R_ALPHA, 0:hk_a].set(wal[0]).at[R_ALPHA:2 * R_ALPHA, hk_a:].set(wal[1]).astype(BF16)
    bm_row = jnp.zeros((1, SMALL_W), F32).at[0, GATE_LANE0:GATE_LANE0 + 4 * H_B].set(b_mgate[l].reshape(-1))
    return dict(
        norm1_w=norm1_w[l], norm2_w=norm2_w[l], w_in_t=jnp.swapaxes(w_in[l], 0, 1),
        big_rows=big_rows, small_rows=small_rows, wal_p=wal_p,
        bal_p=b_alpha[l].reshape(1, -1), bm_row=bm_row,
        conv9=conv_w[l].reshape(-1, conv_w.shape[-1]),
        gnorm_a_w=gnorm_a_w[l], gnorm_b_w=gnorm_b_w[l],
        w_out=w_out[l].astype(BF16), w_ff1=w_ff1[l].astype(BF16), w_ff2=w_ff2[l].astype(BF16),
    )


def kernel(x_prompt, x_sample, c, state_gla, state_mlstm_C, state_mlstm_n, state_mlstm_m, c_ctx, w_ada, b_ada, norm1_w, norm2_w, w_in, w_alpha2, b_alpha, b_mgate, conv_w, gnorm_a_w, gnorm_b_w, w_out, w_ff1, w_ff2, final_norm_w):
    depth = w_in.shape[0]
    D = x_prompt.shape[-1]
    Bp, Tp, _ = x_prompt.shape
    Bs = x_sample.shape[0]
    assert 1 + Bs <= COND_ROWS
    cond = jnp.concatenate([c_ctx[None, :], c, jnp.zeros((COND_ROWS - 1 - Bs, D), F32)], axis=0)
    xp, xs = x_prompt, x_sample
    s_gla, s_c, s_n, s_m = [], [], [], []
    for l in range(depth):
        lw = _layer_weights(l, norm1_w, norm2_w, w_in, w_alpha2, b_alpha, b_mgate, conv_w,
                            gnorm_a_w, gnorm_b_w, w_out, w_ff1, w_ff2)
        mod = _ada_call(cond, w_ada[l], b_ada[l])
        last = l == depth - 1
        xp, ctx = _block(xp, mod, 0, False, Tp, None, lw, final_norm_w, last, True)
        s_gla.append(ctx[0].reshape(Bp, 2, H_A, -1, ctx[0].shape[-1]))
        s_c.append(ctx[1].reshape(Bp, 2, H_B, -1, ctx[1].shape[-1]))
        s_n.append(ctx[2].reshape(Bp, 2, H_B, -1))
        s_m.append(ctx[3])
        cached = (state_gla[:, l], state_mlstm_C[:, l], state_mlstm_n[:, l], state_mlstm_m[:, l])
        xs, _ = _block(xs, mod, 1, True, GRID_W, cached, lw, final_norm_w, last, False)
    dt = x_prompt.dtype
    return (xp, xs, jnp.stack(s_gla, axis=1).astype(dt), jnp.stack(s_c, axis=1).astype(dt),
            jnp.stack(s_n, axis=1).astype(dt), jnp.stack(s_m, axis=1).astype(dt))
```

```python
import functools

import jax
import jax.numpy as jnp
from jax import lax
from jax.experimental import pallas as pl
from jax.experimental.pallas import tpu as pltpu

F32 = jnp.float32
BF16 = jnp.bfloat16

GRID_W = 64
H_A = 4
H_B = 4
R_ALPHA = 16
TAU_GLA = 16.0
CHUNK = 64
EPS = 1e-6
LANES = 128
COND_ROWS = 8
SMALL_W = LANES
GATE_LANE0 = 2 * R_ALPHA
VMEM_LIMIT = 56 * 1024 * 1024
SCAN_UNROLL = 4
TOKEN_TILE = 512


def _sigmoid(x):
    return 1.0 / (1.0 + jnp.exp(-x))


def _silu(x):
    return x * _sigmoid(x)


def _log_sigmoid(x):
    return jnp.minimum(x, 0.0) - jnp.log1p(jnp.exp(-jnp.abs(x)))


def _dot(a, b):
    return jnp.dot(a, b, preferred_element_type=F32)


def _dot_nt(a, b):
    return lax.dot_general(a, b, (((1,), (1,)), ((), ())), preferred_element_type=F32)


def _rms(x, w):
    return x * lax.rsqrt(jnp.mean(x * x, axis=-1, keepdims=True) + EPS) * w


def _tri_sum(tri, x):
    hi = x.astype(BF16)
    r1 = x - hi.astype(F32)
    mid = r1.astype(BF16)
    lo = (r1 - mid.astype(F32)).astype(BF16)
    return _dot(tri, hi) + _dot(tri, mid) + _dot(tri, lo)


def _chunk_masks(L):
    row = lax.broadcasted_iota(jnp.int32, (L, L), 0)
    col = lax.broadcasted_iota(jnp.int32, (L, L), 1)
    lower = row >= col
    upper = row <= col
    return lower, upper


def _ada_kernel(c_ref, w_ref, b_ref, o_ref):
    s = _silu(c_ref[...])
    o_ref[...] = _dot(s.astype(BF16), w_ref[...].astype(BF16)) + b_ref[...]


def _ada_call(cond, w_ada, b_ada):
    D = cond.shape[1]
    n_out = w_ada.shape[1]
    tn = 1024
    return pl.pallas_call(
        _ada_kernel,
        grid=(n_out // tn,),
        in_specs=[
            pl.BlockSpec((COND_ROWS, D), lambda j: (0, 0)),
            pl.BlockSpec((D, tn), lambda j: (0, j)),
            pl.BlockSpec((1, tn), lambda j: (0, j)),
        ],
        out_specs=pl.BlockSpec((COND_ROWS, tn), lambda j: (0, j)),
        out_shape=jax.ShapeDtypeStruct((COND_ROWS, n_out), F32),
        compiler_params=pltpu.CompilerParams(dimension_semantics=("arbitrary",)),
        name="ada_mod",
    )(cond, w_ada, b_ada.reshape(1, n_out))


def _mod_row(mod_row0, tiles_per_batch):
    if tiles_per_batch is None:
        return mod_row0
    return mod_row0 + pl.program_id(0) // tiles_per_batch


def _inproj_kernel(x_ref, mod_ref, nw_ref, wt_ref, z_ref, wb_scr, *, mod_row0, tiles_per_batch, big_rows,
                   small_rows):
    D = x_ref.shape[1]

    @pl.when(pl.program_id(0) == 0)
    def _():
        col = 0
        for r0, n in big_rows:
            for k in range(n // LANES):
                blk = wt_ref[r0 + k * LANES:r0 + (k + 1) * LANES, :]
                wb_scr[:, col:col + LANES] = blk.T.astype(BF16)
                col += LANES
        parts = [wt_ref[r0:r0 + n, :] for r0, n in small_rows]
        n_small = sum(n for _, n in small_rows)
        parts.append(jnp.zeros((SMALL_W - n_small, D), F32))
        wb_scr[:, col:col + SMALL_W] = jnp.concatenate(parts, axis=0).T.astype(BF16)

    row = _mod_row(mod_row0, tiles_per_batch)
    sh1 = mod_ref[pl.ds(row, 1), 0:D]
    sc1 = mod_ref[pl.ds(row, 1), D:2 * D]
    h = _rms(x_ref[...], nw_ref[...]) * (1.0 + sc1) + sh1
    z_ref[...] = _dot(h.astype(BF16), wb_scr[...])


def _inproj_call(x2d, mod, norm_w, w_in_t, *, tm, mod_row0, tiles_per_batch, big_rows, small_rows):
    M, D = x2d.shape
    n_out = sum(n for _, n in big_rows) + SMALL_W
    kern = functools.partial(_inproj_kernel, mod_row0=mod_row0, tiles_per_batch=tiles_per_batch,
                             big_rows=big_rows, small_rows=small_rows)
    return pl.pallas_call(
        kern,
        grid=(M // tm,),
        in_specs=[
            pl.BlockSpec((tm, D), lambda i: (i, 0)),
            pl.BlockSpec(mod.shape, lambda i: (0, 0)),
            pl.BlockSpec((1, D), lambda i: (0, 0)),
            pl.BlockSpec(w_in_t.shape, lambda i: (0, 0), pipeline_mode=pl.Buffered(1)),
        ],
        out_specs=pl.BlockSpec((tm, n_out), lambda i: (i, 0)),
        out_shape=jax.ShapeDtypeStruct((M, n_out), F32),
        scratch_shapes=[pltpu.VMEM((D, n_out), BF16)],
        compiler_params=pltpu.CompilerParams(dimension_semantics=("arbitrary",),
                                             vmem_limit_bytes=VMEM_LIMIT),
        name="norm_inproj",
    )(x2d, mod, norm_w.reshape(1, D), w_in_t)


def _chunk_loop(n_chunks, unroll, fn):
    if unroll >= n_chunks:
        fn(list(range(n_chunks)))
        return

    def body(i, carry):
        fn([i * unroll + u for u in range(unroll)])
        return carry

    lax.fori_loop(0, n_chunks // unroll, body, 0)


def _chunk_rows(n):
    if isinstance(n, int):
        return pl.ds(n * CHUNK, CHUNK)
    return pl.ds(pl.multiple_of(n * CHUNK, CHUNK), CHUNK)


def _cast_specs(casts, n_steps):
    in_specs, out_specs, out_shape, args = [], [], [], []
    for w, axis in casts:
        blk = list(w.shape)
        assert blk[axis] % n_steps == 0
        blk[axis] //= n_steps
        assert blk[0] % 16 == 0 and blk[1] % LANES == 0
        idx = (lambda b: (b, 0)) if axis == 0 else (lambda b: (0, b))
        in_specs.append(pl.BlockSpec(tuple(blk), idx))
        out_specs.append(pl.BlockSpec(tuple(blk), idx))
        out_shape.append(jax.ShapeDtypeStruct(w.shape, BF16))
        args.append(w)
    return in_specs, out_specs, out_shape, args


def _gla_kernel(*refs, has_state, write_state, unroll, n_cast):
    refs = list(refs)
    q_ref, k_ref, v_ref, g_ref, sm_ref = refs[:5]
    del refs[:5]
    s0_ref = refs.pop(0) if has_state else None
    wal_ref, bal_ref, gw_ref = refs[:3]
    del refs[:3]
    cast_in = refs[:n_cast]
    del refs[:n_cast]
    out_ref = refs.pop(0)
    snew_ref = refs.pop(0) if write_state else None
    cast_out = refs[:n_cast]
    del refs[:n_cast]
    of_scr, ob_scr, st_scr, sall_scr, qh_scr, qs_scr, kh_scr = refs

    for src, dst in zip(cast_in, cast_out):
        dst[...] = src[...].astype(BF16)

    T = q_ref.shape[0]
    L = CHUNK
    N = T // L
    HK = q_ref.shape[1]
    DK = HK // H_A
    DV = v_ref.shape[1] // H_A
    scale = DK ** -0.5
    n_pairs = HK // LANES

    lower, upper = _chunk_masks(L)
    tri = (lower.astype(BF16), upper.astype(BF16))
    tmask = (lower, upper)
    lane = lax.broadcasted_iota(jnp.int32, (1, LANES), 1)
    head_mask = (lane < DK, lane >= DK)
    o_scr = (of_scr, ob_scr)

    for d in range(2):
        for p in range(n_pairs):
            if has_state:
                st_scr[d, p] = s0_ref[d, p].T
            else:
                st_scr[d, p] = jnp.zeros((LANES, LANES), F32)

    def state_group(ns):
        units = [u for n in ns for u in ((0, n), (1, N - 1 - n))]
        rows = [_chunk_rows(n) for _, n in units]
        pre = [_dot(sm_ref[r, :].astype(BF16), wal_ref[:, d * HK:(d + 1) * HK]) + bal_ref[:, d * HK:(d + 1) * HK]
               for (d, _), r in zip(units, rows)]
        g = [_log_sigmoid(x) * (1.0 / TAU_GLA) for x in pre]
        b = [_tri_sum(tri[d], gi) for (d, _), gi in zip(units, g)]
        ks_all, dec_all = [], []
        for (d, _), r, bi in zip(units, rows, b):
            bend = bi[L - 1:L, :] if d == 0 else bi[0:1, :]
            q = q_ref[r, :] * scale
            ks = (k_ref[r, :] * jnp.exp(bend - bi)).astype(BF16)
            qh_scr[d, r, :] = (q * jnp.exp(bi - bend)).astype(BF16)
            qs_scr[d, r, :] = (q * jnp.exp(bi)).astype(BF16)
            kh_scr[d, r, :] = ks
            ks_all.append(ks)
            dec_all.append(jnp.exp(bend))
        upd_all = [[[_dot(v_ref[r, (2 * p + j) * DV:(2 * p + j + 1) * DV].T.astype(BF16),
                          ks[:, p * LANES:(p + 1) * LANES]) for j in range(2)]
                    for p in range(n_pairs)]
                   for r, ks in zip(rows, ks_all)]
        for (d, n), dec, upd in zip(units, dec_all, upd_all):
            for p in range(n_pairs):
                st = st_scr[d, p]
                sall_scr[d, n, p] = st.astype(BF16)
                st_scr[d, p] = (st * dec[:, p * LANES:(p + 1) * LANES]
                                + jnp.where(head_mask[0], upd[p][0], upd[p][1]))

    _chunk_loop(N, unroll, state_group)

    def out_group(ns):
        units = [(d, n, p, j) for n in ns for d in range(2) for p in range(n_pairs) for j in range(2)]
        scores, inter = [], []
        for d, n, p, j in units:
            r = _chunk_rows(n)
            ls = slice(p * LANES, (p + 1) * LANES)
            qh = qh_scr[d, r, ls]
            qs = qs_scr[d, r, ls]
            qm = jnp.where(head_mask[j], qh, jnp.zeros_like(qh))
            qsm = jnp.where(head_mask[j], qs, jnp.zeros_like(qs))
            scores.append(_dot_nt(qm, kh_scr[d, r, ls]))
            inter.append(_dot_nt(qsm, sall_scr[d, n, p]))
        probs = [jnp.where(tmask[d], a, 0.0).astype(BF16) for (d, _, _, _), a in zip(units, scores)]
        for (d, n, p, j), a, it in zip(units, probs, inter):
            vs = slice((2 * p + j) * DV, (2 * p + j + 1) * DV)
            r = _chunk_rows(n)
            o_scr[d][r, vs] = _dot(a, v_ref[r, vs].astype(BF16)) + it

    _chunk_loop(N, unroll, out_group)

    def epilogue(i, carry):
        rows = pl.ds(pl.multiple_of(i * L, L), L)
        for h in range(H_A):
            vs = slice(h * DV, (h + 1) * DV)
            o = of_scr[rows, vs] + ob_scr[rows, vs]
            out_ref[rows, vs] = (_rms(o, gw_ref[:, vs]) * _silu(g_ref[rows, vs])).astype(out_ref.dtype)
        return carry

    lax.fori_loop(0, N, epilogue, 0)

    if write_state:
        for d in range(2):
            for p in range(n_pairs):
                snew_ref[d, p] = st_scr[d, p].T


def _gla_call(z3, s0, wal_p, bal_p, gw, *, has_state, write_state, casts=()):
    B, T, _ = z3.shape
    HK = wal_p.shape[1] // 2
    DA = gw.shape[0]
    n_pairs = HK // LANES
    small_blk = (z3.shape[2] - SMALL_W) // SMALL_W
    n_chunks = T // CHUNK
    cast_in_specs, cast_out_specs, cast_out_shape, cast_args = _cast_specs(casts, B)
    kern = functools.partial(_gla_kernel, has_state=has_state, write_state=write_state,
                             unroll=min(n_chunks, SCAN_UNROLL), n_cast=len(casts))
    state_shape = (B, 2, n_pairs, LANES, LANES)
    state_spec = pl.BlockSpec((None, 2, n_pairs, LANES, LANES), lambda b: (b, 0, 0, 0, 0))
    in_specs = [
        pl.BlockSpec((None, T, HK), lambda b: (b, 0, 0)),
        pl.BlockSpec((None, T, HK), lambda b: (b, 0, 1)),
        pl.BlockSpec((None, T, DA), lambda b: (b, 0, 1)),
        pl.BlockSpec((None, T, DA), lambda b: (b, 0, 2)),
        pl.BlockSpec((None, T, SMALL_W), lambda b: (b, 0, small_blk)),
    ]
    args = [z3, z3, z3, z3, z3]
    if has_state:
        in_specs.append(state_spec)
        args.append(s0)
    in_specs += [
        pl.BlockSpec(wal_p.shape, lambda b: (0, 0)),
        pl.BlockSpec(bal_p.shape, lambda b: (0, 0)),
        pl.BlockSpec((1, DA), lambda b: (0, 0)),
    ]
    args += [wal_p, bal_p, gw.reshape(1, DA)] + cast_args
    in_specs += cast_in_specs
    out_specs = [pl.BlockSpec((None, T, DA), lambda b: (b, 0, 0))]
    out_shape = [jax.ShapeDtypeStruct((B, T, DA), BF16)]
    if write_state:
        out_specs.append(state_spec)
        out_shape.append(jax.ShapeDtypeStruct(state_shape, F32))
    out_specs += cast_out_specs
    out_shape += cast_out_shape
    return pl.pallas_call(
        kern,
        grid=(B,),
        in_specs=in_specs,
        out_specs=out_specs,
        out_shape=out_shape,
        scratch_shapes=[
            pltpu.VMEM((T, DA), F32),
            pltpu.VMEM((T, DA), F32),
            pltpu.VMEM((2, n_pairs, LANES, LANES), F32),
            pltpu.VMEM((2, n_chunks, n_pairs, LANES, LANES), BF16),
            pltpu.VMEM((2, T, HK), BF16),
            pltpu.VMEM((2, T, HK), BF16),
            pltpu.VMEM((2, T, HK), BF16),
        ],
        compiler_params=pltpu.CompilerParams(dimension_semantics=("arbitrary",),
                                             vmem_limit_bytes=VMEM_LIMIT),
        name="gla_scan",
    )(*args)


def _mlstm_kernel(*refs, grid_w, has_state, write_state, unroll, n_cast):
    refs = list(refs)
    qk_ref, v_ref, og_ref, sm_ref = refs[:4]
    del refs[:4]
    if has_state:
        c0_ref, n0_ref, m0_ref = refs[:3]
        del refs[:3]
    cw_ref, bm_ref, gw_ref = refs[:3]
    del refs[:3]
    cast_in = refs[:n_cast]
    del refs[:n_cast]
    out_ref = refs.pop(0)
    if write_state:
        cnew_ref, nnew_ref, mnew_ref = refs[:3]
        del refs[:3]
    cast_out = refs[:n_cast]
    del refs[:n_cast]
    (pad_scr, qk_scr, y_scr, c_scr, n_scr, m_scr,
     call_scr, nall_scr, mall_scr, g_scr, f_scr) = refs

    for src, dst in zip(cast_in, cast_out):
        dst[...] = src[...].astype(BF16)

    T = qk_ref.shape[0]
    L = CHUNK
    N = T // L
    C2 = qk_ref.shape[1]
    HK = C2 // 2
    DK = HK // H_B
    DV = v_ref.shape[1] // H_B
    scale = DK ** -0.5
    n_pairs = HK // LANES
    P = pad_scr.shape[0] - T
    P0 = P // 2
    rows_img = T // grid_w

    lower, upper = _chunk_masks(L)
    tri = (lower.astype(BF16), upper.astype(BF16))
    tmask = (lower, upper)
    lane = lax.broadcasted_iota(jnp.int32, (1, LANES), 1)
    head_mask = (lane < DK, lane >= DK)
    lane_in = lane & (L - 1)

    def lane_cummax(x, d):
        k = 1
        while k < L:
            if d == 0:
                x = jnp.maximum(x, jnp.where(lane_in >= k, pltpu.roll(x, k, axis=1), -jnp.inf))
            else:
                x = jnp.maximum(x, jnp.where(lane_in < L - k, pltpu.roll(x, LANES - k, axis=1), -jnp.inf))
            k *= 2
        return x

    for d in range(2):
        for p in range(n_pairs):
            if has_state:
                c_scr[d, p] = c0_ref[d, p]
                n_scr[2 * d + p:2 * d + p + 1, :] = n0_ref[d, p:p + 1, :]
            else:
                c_scr[d, p] = jnp.zeros((LANES, LANES), F32)
                n_scr[2 * d + p:2 * d + p + 1, :] = jnp.zeros((1, LANES), F32)
    eye_h = (lax.broadcasted_iota(jnp.int32, (H_B, H_B), 0) == lax.broadcasted_iota(jnp.int32, (H_B, H_B), 1))

    def to_col(row):
        return jnp.sum(jnp.where(eye_h, row, 0.0), axis=1, keepdims=True)

    def to_row(col):
        return jnp.sum(jnp.where(eye_h, col, 0.0), axis=0, keepdims=True)

    for d in range(2):
        if has_state:
            m_scr[H_B * d:H_B * (d + 1), 0:1] = to_col(m0_ref[d:d + 1, :])
        else:
            m_scr[H_B * d:H_B * (d + 1), 0:1] = jnp.zeros((H_B, 1), F32)

    pad_scr[0:P0, :] = jnp.zeros((P0, C2), F32)
    pad_scr[P0 + T:P + T, :] = jnp.zeros((P - P0, C2), F32)

    def copy_in(i, carry):
        r0 = pl.multiple_of(i * L, L)
        pad_scr[pl.ds(P0 + r0, L), :] = qk_ref[pl.ds(r0, L), :]
        return carry

    lax.fori_loop(0, N, copy_in, 0)

    lane_c = lax.broadcasted_iota(jnp.int32, (1, C2), 1)
    qscale = jnp.where(lane_c < HK, scale, 1.0).astype(F32)
    sub = lax.broadcasted_iota(jnp.int32, (L, 1), 0)
    img_rows = (0,) if rows_img == 1 else (-1, 0, 1)

    def conv_tile(i, carry):
        r0 = pl.multiple_of(i * L, L)
        col = lax.rem(r0, grid_w) + sub
        ok_left = col >= 1
        ok_right = col <= grid_w - 2
        acc = jnp.zeros((L, C2), F32)
        for di in img_rows:
            blk = pad_scr[pl.ds(P0 + r0 + di * grid_w - 8, L + 16), :]
            left = jnp.where(ok_left, blk[7:7 + L, :], 0.0)
            mid = blk[8:8 + L, :]
            right = jnp.where(ok_right, blk[9:9 + L, :], 0.0)
            wr = 3 * (di + 1)
            acc = acc + left * cw_ref[wr:wr + 1, :] + mid * cw_ref[wr + 1:wr + 2, :] + right * cw_ref[wr + 2:wr + 3, :]
        qk_scr[pl.ds(r0, L), :] = _silu(acc) * qscale
        return carry

    lax.fori_loop(0, N, conv_tile, 0)

    gl = lane - GATE_LANE0
    is_f = ((gl >= H_B) & (gl < 2 * H_B)) | ((gl >= 3 * H_B) & (gl < 4 * H_B))

    def gate_tile(i, carry):
        rows = pl.ds(pl.multiple_of(i * L, L), L)
        x = sm_ref[rows, :] + bm_ref[...]
        y_scr[rows, :] = jnp.where(is_f, _log_sigmoid(x), x)
        return carry

    lax.fori_loop(0, N, gate_tile, 0)


    def state_group(ns):
        units = [u for n in ns for u in ((0, n), (1, N - 1 - n))]
        rows = [_chunk_rows(n) for _, n in units]
        xs = [y_scr[r, :] for r in rows]
        fsum = [_tri_sum(tri[d], x) for (d, _), x in zip(units, xs)]
        wk_all, f_end, c_end = [], [], []
        for (d, n), r, x, fs in zip(units, rows, xs, fsum):
            y = jnp.where(is_f, fs, x)
            li0 = GATE_LANE0 + 2 * H_B * d
            blk = jnp.concatenate([y, y], axis=0).T[li0:li0 + 2 * H_B, :]
            frow = pltpu.roll(blk, H_B, axis=0)
            grow = blk - frow
            g_scr[d, n] = grow
            f_scr[d, n] = frow
            e_col = L - 1 if d == 0 else 0
            f_end.append(frow[0:H_B, e_col:e_col + 1])
            ce = jnp.max(grow[0:H_B, :], axis=1, keepdims=True)
            c_end.append(ce)
            wk_all.append(jnp.exp(grow[0:H_B, 0:L] - ce))
        kv_all, ksum_all = [], []
        for r, wk4 in zip(rows, wk_all):
            kv_u, ks_u = [], []
            for p in range(n_pairs):
                kp = qk_scr[r, HK + p * LANES:HK + (p + 1) * LANES]
                kpb = kp.astype(BF16)
                kt = kp.T
                for j in range(2):
                    h = 2 * p + j
                    wk = wk4[h:h + 1, :]
                    kwt = (kt[j * DK:(j + 1) * DK, :] * wk).astype(BF16)
                    kv_u.append(_dot(kwt, v_ref[r, h * DV:(h + 1) * DV].astype(BF16)))
                    ks_u.append(_dot(jnp.broadcast_to(wk, (8, L)).astype(BF16), kpb)[0:1, :])
            kv_all.append(kv_u)
            ksum_all.append(ks_u)
        for (d, n), fe, ce, kv_u, ks_u in zip(units, f_end, c_end, kv_all, ksum_all):
            m_prev = m_scr[H_B * d:H_B * (d + 1), 0:1]
            mall_scr[d, n, 0:H_B, 0:1] = m_prev
            mx = jnp.maximum(m_prev, ce)
            a_all = jnp.exp(m_prev - mx)
            b_all = jnp.exp(ce - mx)
            m_scr[H_B * d:H_B * (d + 1), 0:1] = fe + mx
            for p in range(n_pairs):
                npair = n_scr[2 * d + p:2 * d + p + 1, :]
                nall_scr[d, n, p:p + 1, :] = npair
                a_s = [a_all[2 * p + j:2 * p + j + 1, :] for j in range(2)]
                b_s = [b_all[2 * p + j:2 * p + j + 1, :] for j in range(2)]
                for j in range(2):
                    hr = slice(j * DK, (j + 1) * DK)
                    cj = c_scr[d, p, hr, :]
                    call_scr[d, n, p, hr, :] = cj.astype(BF16)
                    c_scr[d, p, hr, :] = a_s[j] * cj + b_s[j] * kv_u[2 * p + j]
                n_scr[2 * d + p:2 * d + p + 1, :] = (
                    jnp.where(head_mask[0], a_s[0], a_s[1]) * npair
                    + jnp.where(head_mask[0], b_s[0] * ks_u[2 * p], b_s[1] * ks_u[2 * p + 1]))

    _chunk_loop(N, unroll, state_group)

    eye2 = ((lax.broadcasted_iota(jnp.int32, (L, LANES), 1) & (L - 1))
            == lax.broadcasted_iota(jnp.int32, (L, LANES), 0))
    ones8 = jnp.ones((8, L), BF16)
    sub_h = lax.broadcasted_iota(jnp.int32, (H_B, LANES), 0)

    def head_rows(vals):
        out = vals[0][0:H_B, :]
        for h in range(1, H_B):
            out = jnp.where(sub_h == h, vals[h][0:H_B, :], out)
        return out

    def out_group(ns):
        chunks = [(d, n) for n in ns for d in range(2)]
        units = [(d, n, p, j) for d, n in chunks for p in range(n_pairs) for j in range(2)]
        cms = [lane_cummax(g_scr[d, n], d)[0:H_B, :] for d, n in chunks]
        qks, qcs, qns = [], [], []
        for d, n, p, j in units:
            r = _chunk_rows(n)
            qmb = jnp.where(head_mask[j], qk_scr[r, p * LANES:(p + 1) * LANES], 0.0).astype(BF16)
            qks.append(_dot_nt(qmb, qk_scr[r, HK + p * LANES:HK + (p + 1) * LANES].astype(BF16)))
            qcs.append(_dot(qmb, call_scr[d, n, p]))
            n8 = jnp.broadcast_to(nall_scr[d, n, p:p + 1, :], (8, LANES)).astype(BF16)
            qns.append(_dot_nt(n8, jnp.concatenate([qmb, qmb], axis=0)))
        s_all = []
        for (d, n, p, j), qk in zip(units, qks):
            grow = g_scr[d, n, 2 * p + j:2 * p + j + 1, 0:L]
            e = jnp.where(tmask[d], grow, -jnp.inf)
            cmax = jnp.max(e, axis=-1, keepdims=True)
            s_all.append((qk * jnp.exp(e - cmax)).astype(BF16))
        nums = [_dot(s, v_ref[_chunk_rows(n), (2 * p + j) * DV:(2 * p + j + 1) * DV].astype(BF16))
                for (d, n, p, j), s in zip(units, s_all)]
        dens = [_dot_nt(ones8, jnp.concatenate([s, s], axis=0)) for s in s_all]
        scales = []
        for ci, (d, n) in enumerate(chunks):
            den_loc = head_rows(dens[ci * H_B:(ci + 1) * H_B])
            qn = head_rows(qns[ci * H_B:(ci + 1) * H_B])
            cm = cms[ci]
            m_prev = mall_scr[d, n, 0:H_B, 0:1]
            delta = cm - m_prev
            t = jnp.exp(-jnp.abs(delta))
            w_loc = jnp.where(delta <= 0.0, t, 1.0)
            w_inter = jnp.where(delta <= 0.0, 1.0, t)
            mt = f_scr[d, n, 0:H_B, :] + jnp.maximum(m_prev, cm)
            den = w_loc * den_loc + w_inter * qn
            rinv = 1.0 / jnp.maximum(jnp.abs(den), jnp.exp(-mt))
            scales.append(jnp.where(lane < L, w_loc * rinv, w_inter * rinv))
        hs = []
        for ui, (d, n, p, j) in enumerate(units):
            h = 2 * p + j
            lhs = jnp.where(eye2, scales[ui // H_B][h:h + 1, :], 0.0).astype(BF16)
            rhs = jnp.concatenate([nums[ui].astype(BF16), qcs[ui].astype(BF16)], axis=0)
            hs.append(_dot(lhs, rhs))
        for ni, n in enumerate(ns):
            r = _chunk_rows(n)
            for h in range(H_B):
                vs = slice(h * DV, (h + 1) * DV)
                o = hs[(2 * ni) * H_B + h] + hs[(2 * ni + 1) * H_B + h]
                out_ref[r, vs] = (_rms(o, gw_ref[:, vs]) * _sigmoid(og_ref[r, vs])).astype(out_ref.dtype)

    _chunk_loop(N, unroll, out_group)

    if write_state:
        for d in range(2):
            for p in range(n_pairs):
                cnew_ref[d, p] = c_scr[d, p]
                nnew_ref[d, p:p + 1, :] = n_scr[2 * d + p:2 * d + p + 1, :]
            mnew_ref[d:d + 1, :] = to_row(m_scr[H_B * d:H_B * (d + 1), 0:1])


def _mlstm_call(z3, c0, n0, m0, conv9, bm_row, gw, *, grid_w, has_state, write_state, casts=()):
    B, T, _ = z3.shape
    C2 = conv9.shape[1]
    HK = C2 // 2
    DB = gw.shape[0]
    n_pairs = HK // LANES
    small_blk = (z3.shape[2] - SMALL_W) // SMALL_W
    qk_blk = (3 * DB) // C2
    v_blk = (3 * DB + C2) // DB
    pad_rows = 2 * (grid_w + 8) if T // grid_w > 1 else 16
    n_chunks = T // CHUNK
    cast_in_specs, cast_out_specs, cast_out_shape, cast_args = _cast_specs(casts, B)
    kern = functools.partial(_mlstm_kernel, grid_w=grid_w, has_state=has_state, write_state=write_state,
                             unroll=min(n_chunks, SCAN_UNROLL), n_cast=len(casts))
    c_spec = pl.BlockSpec((None, 2, n_pairs, LANES, LANES), lambda b: (b, 0, 0, 0, 0))
    n_spec = pl.BlockSpec((None, 2, n_pairs, LANES), lambda b: (b, 0, 0, 0))
    m_spec = pl.BlockSpec((None, 2, H_B), lambda b: (b, 0, 0))
    in_specs = [
        pl.BlockSpec((None, T, C2), lambda b: (b, 0, qk_blk)),
        pl.BlockSpec((None, T, DB), lambda b: (b, 0, v_blk)),
        pl.BlockSpec((None, T, DB), lambda b: (b, 0, v_blk + 1)),
        pl.BlockSpec((None, T, SMALL_W), lambda b: (b, 0, small_blk)),
    ]
    args = [z3, z3, z3, z3]
    if has_state:
        in_specs += [c_spec, n_spec, m_spec]
        args += [c0, n0, m0]
    in_specs += [
        pl.BlockSpec(conv9.shape, lambda b: (0, 0)),
        pl.BlockSpec((1, SMALL_W), lambda b: (0, 0)),
        pl.BlockSpec((1, DB), lambda b: (0, 0)),
    ]
    args += [conv9, bm_row, gw.reshape(1, DB)] + cast_args
    in_specs += cast_in_specs
    out_specs = [pl.BlockSpec((None, T, DB), lambda b: (b, 0, 0))]
    out_shape = [jax.ShapeDtypeStruct((B, T, DB), BF16)]
    if write_state:
        out_specs += [c_spec, n_spec, m_spec]
        out_shape += [
            jax.ShapeDtypeStruct((B, 2, n_pairs, LANES, LANES), F32),
            jax.ShapeDtypeStruct((B, 2, n_pairs, LANES), F32),
            jax.ShapeDtypeStruct((B, 2, H_B), F32),
        ]
    out_specs += cast_out_specs
    out_shape += cast_out_shape
    return pl.pallas_call(
        kern,
        grid=(B,),
        in_specs=in_specs,
        out_specs=out_specs,
        out_shape=out_shape,
        scratch_shapes=[
            pltpu.VMEM((T + pad_rows, C2), F32),
            pltpu.VMEM((T, C2), F32),
            pltpu.VMEM((T, SMALL_W), F32),
            pltpu.VMEM((2, n_pairs, LANES, LANES), F32),
            pltpu.VMEM((8, LANES), F32),
            pltpu.VMEM((8, LANES), F32),
            pltpu.VMEM((2, n_chunks, n_pairs, LANES, LANES), BF16),
            pltpu.VMEM((2, n_chunks, 8, LANES), F32),
            pltpu.VMEM((2, n_chunks, 8, LANES), F32),
            pltpu.VMEM((2, n_chunks, 8, LANES), F32),
            pltpu.VMEM((2, n_chunks, 8, LANES), F32),
        ],
        compiler_params=pltpu.CompilerParams(dimension_semantics=("arbitrary",),
                                             vmem_limit_bytes=VMEM_LIMIT),
        name="mlstm_scan",
    )(*args)


def _outff_kernel(x_ref, a_ref, b_ref, mod_ref, n2_ref, fn_ref, wo_ref, w1_ref, w2_ref, y_ref,
                  *, mod_row0, tiles_per_batch, ff_chunk, final_norm):
    D = x_ref.shape[1]
    DA = a_ref.shape[1]
    row = _mod_row(mod_row0, tiles_per_batch)

    def mod(k):
        return mod_ref[pl.ds(row, 1), k * D:(k + 1) * D]

    y = _dot(a_ref[...], wo_ref[0:DA, :]) + _dot(b_ref[...], wo_ref[DA:, :])
    x1 = x_ref[...] + mod(2) * y
    h2 = (_rms(x1, n2_ref[...]) * (1.0 + mod(4)) + mod(3)).astype(BF16)
    acc = jnp.zeros(x1.shape, F32)
    for c0 in range(0, w1_ref.shape[1], ff_chunk):
        u = jnp.maximum(_dot(h2, w1_ref[:, c0:c0 + ff_chunk]), 0.0)
        acc = acc + _dot((u * u).astype(BF16), w2_ref[c0:c0 + ff_chunk, :])
    x2 = x1 + mod(5) * acc
    y_ref[...] = _rms(x2, fn_ref[...]) if final_norm else x2


def _outff_call(x2d, a2d, b2d, mod, norm2_w, final_w, wo, w1, w2, *, tm, mod_row0, tiles_per_batch,
                final_norm):
    M, D = x2d.shape
    DA = a2d.shape[1]
    DFF = w1.shape[1]
    kern = functools.partial(_outff_kernel, mod_row0=mod_row0, tiles_per_batch=tiles_per_batch,
                             ff_chunk=512, final_norm=final_norm)
    once = pl.Buffered(1)
    return pl.pallas_call(
        kern,
        grid=(M // tm,),
        in_specs=[
            pl.BlockSpec((tm, D), lambda i: (i, 0)),
            pl.BlockSpec((tm, DA), lambda i: (i, 0)),
            pl.BlockSpec((tm, D - DA), lambda i: (i, 0)),
            pl.BlockSpec(mod.shape, lambda i: (0, 0)),
            pl.BlockSpec((1, D), lambda i: (0, 0)),
            pl.BlockSpec((1, D), lambda i: (0, 0)),
            pl.BlockSpec((D, D), lambda i: (0, 0), pipeline_mode=once),
            pl.BlockSpec((D, DFF), lambda i: (0, 0), pipeline_mode=once),
            pl.BlockSpec((DFF, D), lambda i: (0, 0), pipeline_mode=once),
        ],
        out_specs=pl.BlockSpec((tm, D), lambda i: (i, 0)),
        out_shape=jax.ShapeDtypeStruct((M, D), F32),
        compiler_params=pltpu.CompilerParams(dimension_semantics=("arbitrary",),
                                             vmem_limit_bytes=VMEM_LIMIT),
        name="outproj_mlp",
    )(x2d, a2d, b2d, mod, norm2_w.reshape(1, D), final_w.reshape(1, D), wo, w1, w2)


def _block(x, mod, mod_row0, per_batch, grid_w, states, lw, ffw, final_w, final_norm, write_state):
    B, T, D = x.shape
    convert = ffw[0].dtype != BF16
    span = T if per_batch else B * T
    tm = TOKEN_TILE if span % TOKEN_TILE == 0 else T
    tiles_per_batch = (T // tm) if per_batch else None
    x2d = x.reshape(B * T, D)
    z = _inproj_call(x2d, mod, lw["norm1_w"], lw["w_in_t"], tm=tm, mod_row0=mod_row0,
                     tiles_per_batch=tiles_per_batch, big_rows=lw["big_rows"], small_rows=lw["small_rows"])
    z3 = z.reshape(B, T, z.shape[1])
    has_state = states is not None
    n_pairs_a = lw["wal_p"].shape[1] // 2 // LANES
    n_pairs_b = lw["conv9"].shape[1] // 2 // LANES
    s_gla = s_c = s_n = s_m = None
    if has_state:
        s_gla, s_c, s_n, s_m = states
        s_gla = s_gla.reshape(B, 2, n_pairs_a, LANES, LANES)
        s_c = s_c.reshape(B, 2, n_pairs_b, LANES, LANES)
        s_n = s_n.reshape(B, 2, n_pairs_b, LANES)
    w_out, w_ff1, w_ff2 = ffw
    res_a = _gla_call(z3, s_gla, lw["wal_p"], lw["bal_p"], lw["gnorm_a_w"], has_state=has_state,
                      write_state=write_state, casts=((w_ff1, 1), (w_out, 0)) if convert else ())
    res_b = _mlstm_call(z3, s_c, s_n, s_m, lw["conv9"], lw["bm_row"], lw["gnorm_b_w"], grid_w=grid_w,
                        has_state=has_state, write_state=write_state, casts=((w_ff2, 0),) if convert else ())
    out_a, out_b = res_a[0], res_b[0]
    new_states = (res_a[1], res_b[1], res_b[2], res_b[3]) if write_state else None
    if convert:
        ffw = (res_a[-1], res_a[-2], res_b[-1])
    y = _outff_call(x2d, out_a.reshape(B * T, -1), out_b.reshape(B * T, -1), mod, lw["norm2_w"], final_w,
                    *ffw, tm=tm, mod_row0=mod_row0, tiles_per_batch=tiles_per_batch, final_norm=final_norm)
    return y.reshape(B, T, D), new_states, ffw


def _layer_weights(l, norm1_w, norm2_w, w_in, w_alpha2, b_alpha, b_mgate, conv_w, gnorm_a_w, gnorm_b_w):
    hk_a = w_alpha2.shape[-1]
    d_a = gnorm_a_w.shape[-1]
    d_b = gnorm_b_w.shape[-1]
    hk_b = conv_w.shape[-1] // 2
    sizes = (hk_a, hk_a, d_a, d_a, 2 * R_ALPHA, hk_b, hk_b, d_b, d_b, 4 * H_B)
    offs = [0]
    for s in sizes:
        offs.append(offs[-1] + s)
    big_rows = ((offs[0], offs[4] - offs[0]), (offs[5], offs[9] - offs[5]))
    small_rows = ((offs[4], offs[5] - offs[4]), (offs[9], offs[10] - offs[9]))
    assert all(n % LANES == 0 and r % 16 == 0 for r, n in big_rows)
    wal = w_alpha2[l]
    wal_p = jnp.zeros((SMALL_W, 2 * hk_a), F32)
    wal_p = wal_p.at[0:R_ALPHA, 0:hk_a].set(wal[0]).at[R_ALPHA:2 * R_ALPHA, hk_a:].set(wal[1]).astype(BF16)
    bm_row = jnp.zeros((1, SMALL_W), F32).at[0, GATE_LANE0:GATE_LANE0 + 4 * H_B].set(b_mgate[l].reshape(-1))
    return dict(
        norm1_w=norm1_w[l], norm2_w=norm2_w[l], w_in_t=jnp.swapaxes(w_in[l], 0, 1),
        big_rows=big_rows, small_rows=small_rows, wal_p=wal_p,
        bal_p=b_alpha[l].reshape(1, -1), bm_row=bm_row,
        conv9=conv_w[l].reshape(-1, conv_w.shape[-1]),
        gnorm_a_w=gnorm_a_w[l], gnorm_b_w=gnorm_b_w[l],
    )


def kernel(x_prompt, x_sample, c, state_gla, state_mlstm_C, state_mlstm_n, state_mlstm_m, c_ctx, w_ada, b_ada, norm1_w, norm2_w, w_in, w_alpha2, b_alpha, b_mgate, conv_w, gnorm_a_w, gnorm_b_w, w_out, w_ff1, w_ff2, final_norm_w):
    depth = w_in.shape[0]
    D = x_prompt.shape[-1]
    Bp, Tp, _ = x_prompt.shape
    Bs = x_sample.shape[0]
    assert 1 + Bs <= COND_ROWS
    cond = jnp.concatenate([c_ctx[None, :], c, jnp.zeros((COND_ROWS - 1 - Bs, D), F32)], axis=0)
    xp, xs = x_prompt, x_sample
    s_gla, s_c, s_n, s_m = [], [], [], []
    for l in range(depth):
        lw = _layer_weights(l, norm1_w, norm2_w, w_in, w_alpha2, b_alpha, b_mgate, conv_w,
                            gnorm_a_w, gnorm_b_w)
        mod = _ada_call(cond, w_ada[l], b_ada[l])
        last = l == depth - 1
        xp, ctx, ffw = _block(xp, mod, 0, False, Tp, None, lw, (w_out[l], w_ff1[l], w_ff2[l]),
                              final_norm_w, last, True)
        s_gla.append(ctx[0].reshape(Bp, 2, H_A, -1, ctx[0].shape[-1]))
        s_c.append(ctx[1].reshape(Bp, 2, H_B, -1, ctx[1].shape[-1]))
        s_n.append(ctx[2].reshape(Bp, 2, H_B, -1))
        s_m.append(ctx[3])
        cached = (state_gla[:, l], state_mlstm_C[:, l], state_mlstm_n[:, l], state_mlstm_m[:, l])
        xs, _, _ = _block(xs, mod, 1, True, GRID_W, cached, lw, ffw, final_norm_w, last, False)
    dt = x_prompt.dtype
    return (xp, xs, jnp.stack(s_gla, axis=1).astype(dt), jnp.stack(s_c, axis=1).astype(dt),
            jnp.stack(s_n, axis=1).astype(dt), jnp.stack(s_m, axis=1).astype(dt))
```

```python
import functools

import jax
import jax.numpy as jnp
from jax import lax
from jax.experimental import pallas as pl
from jax.experimental.pallas import tpu as pltpu

F32 = jnp.float32
BF16 = jnp.bfloat16

GRID_W = 64
H_A = 4
H_B = 4
R_ALPHA = 16
TAU_GLA = 16.0
CHUNK = 64
EPS = 1e-6
LANES = 128
COND_ROWS = 8
SMALL_W = LANES
GATE_LANE0 = 2 * R_ALPHA
VMEM_LIMIT = 56 * 1024 * 1024
SCAN_UNROLL = 4
TOKEN_TILE = 512


def _sigmoid(x):
    return 1.0 / (1.0 + jnp.exp(-x))


def _silu(x):
    return x * _sigmoid(x)


def _log_sigmoid(x):
    return jnp.minimum(x, 0.0) - jnp.log1p(jnp.exp(-jnp.abs(x)))


def _dot(a, b):
    return jnp.dot(a, b, preferred_element_type=F32)


def _dot_nt(a, b):
    return lax.dot_general(a, b, (((1,), (1,)), ((), ())), preferred_element_type=F32)


def _rms(x, w):
    return x * lax.rsqrt(jnp.mean(x * x, axis=-1, keepdims=True) + EPS) * w


def _tri_sum(tri, x):
    hi = x.astype(BF16)
    r1 = x - hi.astype(F32)
    mid = r1.astype(BF16)
    lo = (r1 - mid.astype(F32)).astype(BF16)
    return _dot(tri, hi) + _dot(tri, mid) + _dot(tri, lo)


def _chunk_masks(L):
    row = lax.broadcasted_iota(jnp.int32, (L, L), 0)
    col = lax.broadcasted_iota(jnp.int32, (L, L), 1)
    lower = row >= col
    upper = row <= col
    return lower, upper


def _ada_kernel(c_ref, w_ref, b_ref, o_ref):
    s = _silu(c_ref[...])
    o_ref[...] = _dot(s.astype(BF16), w_ref[...].astype(BF16)) + b_ref[...]


def _ada_call(cond, w_ada, b_ada):
    D = cond.shape[1]
    n_out = w_ada.shape[1]
    tn = 1024
    return pl.pallas_call(
        _ada_kernel,
        grid=(n_out // tn,),
        in_specs=[
            pl.BlockSpec((COND_ROWS, D), lambda j: (0, 0)),
            pl.BlockSpec((D, tn), lambda j: (0, j)),
            pl.BlockSpec((1, tn), lambda j: (0, j)),
        ],
        out_specs=pl.BlockSpec((COND_ROWS, tn), lambda j: (0, j)),
        out_shape=jax.ShapeDtypeStruct((COND_ROWS, n_out), F32),
        compiler_params=pltpu.CompilerParams(dimension_semantics=("arbitrary",)),
        name="ada_mod",
    )(cond, w_ada, b_ada.reshape(1, n_out))


def _mod_row(mod_row0, tiles_per_batch):
    if tiles_per_batch is None:
        return mod_row0
    return mod_row0 + pl.program_id(0) // tiles_per_batch


def _inproj_kernel(x_ref, mod_ref, nw_ref, wt_ref, z_ref, wb_scr, *, mod_row0, tiles_per_batch, big_rows,
                   small_rows):
    D = x_ref.shape[1]

    @pl.when(pl.program_id(0) == 0)
    def _():
        col = 0
        for r0, n in big_rows:
            for k in range(n // LANES):
                blk = wt_ref[r0 + k * LANES:r0 + (k + 1) * LANES, :]
                wb_scr[:, col:col + LANES] = blk.T.astype(BF16)
                col += LANES
        parts = [wt_ref[r0:r0 + n, :] for r0, n in small_rows]
        n_small = sum(n for _, n in small_rows)
        parts.append(jnp.zeros((SMALL_W - n_small, D), F32))
        wb_scr[:, col:col + SMALL_W] = jnp.concatenate(parts, axis=0).T.astype(BF16)

    row = _mod_row(mod_row0, tiles_per_batch)
    sh1 = mod_ref[pl.ds(row, 1), 0:D]
    sc1 = mod_ref[pl.ds(row, 1), D:2 * D]
    h = _rms(x_ref[...], nw_ref[...]) * (1.0 + sc1) + sh1
    z_ref[...] = _dot(h.astype(BF16), wb_scr[...])


def _inproj_call(x2d, mod, norm_w, w_in_t, *, tm, mod_row0, tiles_per_batch, big_rows, small_rows):
    M, D = x2d.shape
    n_out = sum(n for _, n in big_rows) + SMALL_W
    kern = functools.partial(_inproj_kernel, mod_row0=mod_row0, tiles_per_batch=tiles_per_batch,
                             big_rows=big_rows, small_rows=small_rows)
    return pl.pallas_call(
        kern,
        grid=(M // tm,),
        in_specs=[
            pl.BlockSpec((tm, D), lambda i: (i, 0)),
            pl.BlockSpec(mod.shape, lambda i: (0, 0)),
            pl.BlockSpec((1, D), lambda i: (0, 0)),
            pl.BlockSpec(w_in_t.shape, lambda i: (0, 0), pipeline_mode=pl.Buffered(1)),
        ],
        out_specs=pl.BlockSpec((tm, n_out), lambda i: (i, 0)),
        out_shape=jax.ShapeDtypeStruct((M, n_out), F32),
        scratch_shapes=[pltpu.VMEM((D, n_out), BF16)],
        compiler_params=pltpu.CompilerParams(dimension_semantics=("arbitrary",),
                                             vmem_limit_bytes=VMEM_LIMIT),
        name="norm_inproj",
    )(x2d, mod, norm_w.reshape(1, D), w_in_t)


def _chunk_loop(n_chunks, unroll, fn):
    if unroll >= n_chunks:
        fn(list(range(n_chunks)))
        return

    def body(i, carry):
        fn([i * unroll + u for u in range(unroll)])
        return carry

    lax.fori_loop(0, n_chunks // unroll, body, 0)


def _chunk_rows(n):
    if isinstance(n, int):
        return pl.ds(n * CHUNK, CHUNK)
    return pl.ds(pl.multiple_of(n * CHUNK, CHUNK), CHUNK)


def _cast_specs(casts, n_steps):
    in_specs, out_specs, out_shape, args = [], [], [], []
    for w, axis in casts:
        blk = list(w.shape)
        assert blk[axis] % n_steps == 0
        blk[axis] //= n_steps
        assert blk[0] % 16 == 0 and blk[1] % LANES == 0
        idx = (lambda b: (b, 0)) if axis == 0 else (lambda b: (0, b))
        in_specs.append(pl.BlockSpec(tuple(blk), idx))
        out_specs.append(pl.BlockSpec(tuple(blk), idx))
        out_shape.append(jax.ShapeDtypeStruct(w.shape, BF16))
        args.append(w)
    return in_specs, out_specs, out_shape, args


def _gla_body(q_ref, k_ref, v_ref, g_ref, sm_ref, s0_ref, wal_ref, bal_ref, gw_ref, out_ref, snew_ref,
              of_scr, ob_scr, st_scr, sall_scr, qh_scr, qs_scr, kh_scr, *, unroll):
    has_state = s0_ref is not None
    write_state = snew_ref is not None
    T = q_ref.shape[0]
    L = CHUNK
    N = T // L
    HK = q_ref.shape[1]
    DK = HK // H_A
    DV = v_ref.shape[1] // H_A
    scale = DK ** -0.5
    n_pairs = HK // LANES

    lower, upper = _chunk_masks(L)
    tri = (lower.astype(BF16), upper.astype(BF16))
    tmask = (lower, upper)
    lane = lax.broadcasted_iota(jnp.int32, (1, LANES), 1)
    head_mask = (lane < DK, lane >= DK)
    o_scr = (of_scr, ob_scr)

    for d in range(2):
        for p in range(n_pairs):
            if has_state:
                st_scr[d, p] = s0_ref[d, p].T
            else:
                st_scr[d, p] = jnp.zeros((LANES, LANES), F32)

    def state_group(ns):
        units = [u for n in ns for u in ((0, n), (1, N - 1 - n))]
        rows = [_chunk_rows(n) for _, n in units]
        pre = [_dot(sm_ref[r, :].astype(BF16), wal_ref[:, d * HK:(d + 1) * HK]) + bal_ref[:, d * HK:(d + 1) * HK]
               for (d, _), r in zip(units, rows)]
        g = [_log_sigmoid(x) * (1.0 / TAU_GLA) for x in pre]
        b = [_tri_sum(tri[d], gi) for (d, _), gi in zip(units, g)]
        ks_all, dec_all = [], []
        for (d, _), r, bi in zip(units, rows, b):
            bend = bi[L - 1:L, :] if d == 0 else bi[0:1, :]
            q = q_ref[r, :] * scale
            ks = (k_ref[r, :] * jnp.exp(bend - bi)).astype(BF16)
            qh_scr[d, r, :] = (q * jnp.exp(bi - bend)).astype(BF16)
            qs_scr[d, r, :] = (q * jnp.exp(bi)).astype(BF16)
            kh_scr[d, r, :] = ks
            ks_all.append(ks)
            dec_all.append(jnp.exp(bend))
        upd_all = [[[_dot(v_ref[r, (2 * p + j) * DV:(2 * p + j + 1) * DV].T.astype(BF16),
                          ks[:, p * LANES:(p + 1) * LANES]) for j in range(2)]
                    for p in range(n_pairs)]
                   for r, ks in zip(rows, ks_all)]
        for (d, n), dec, upd in zip(units, dec_all, upd_all):
            for p in range(n_pairs):
                st = st_scr[d, p]
                sall_scr[d, n, p] = st.astype(BF16)
                st_scr[d, p] = (st * dec[:, p * LANES:(p + 1) * LANES]
                                + jnp.where(head_mask[0], upd[p][0], upd[p][1]))

    _chunk_loop(N, unroll, state_group)

    def out_group(ns):
        units = [(d, n, p, j) for n in ns for d in range(2) for p in range(n_pairs) for j in range(2)]
        scores, inter = [], []
        for d, n, p, j in units:
            r = _chunk_rows(n)
            ls = slice(p * LANES, (p + 1) * LANES)
            qh = qh_scr[d, r, ls]
            qs = qs_scr[d, r, ls]
            qm = jnp.where(head_mask[j], qh, jnp.zeros_like(qh))
            qsm = jnp.where(head_mask[j], qs, jnp.zeros_like(qs))
            scores.append(_dot_nt(qm, kh_scr[d, r, ls]))
            inter.append(_dot_nt(qsm, sall_scr[d, n, p]))
        probs = [jnp.where(tmask[d], a, 0.0).astype(BF16) for (d, _, _, _), a in zip(units, scores)]
        for (d, n, p, j), a, it in zip(units, probs, inter):
            vs = slice((2 * p + j) * DV, (2 * p + j + 1) * DV)
            r = _chunk_rows(n)
            o_scr[d][r, vs] = _dot(a, v_ref[r, vs].astype(BF16)) + it

    _chunk_loop(N, unroll, out_group)

    def epilogue(i, carry):
        rows = pl.ds(pl.multiple_of(i * L, L), L)
        for h in range(H_A):
            vs = slice(h * DV, (h + 1) * DV)
            o = of_scr[rows, vs] + ob_scr[rows, vs]
            out_ref[rows, vs] = (_rms(o, gw_ref[:, vs]) * _silu(g_ref[rows, vs])).astype(out_ref.dtype)
        return carry

    lax.fori_loop(0, N, epilogue, 0)

    if write_state:
        for d in range(2):
            for p in range(n_pairs):
                snew_ref[d, p] = st_scr[d, p].T


def _gla_scratch(T, HK, DA):
    n_pairs = HK // LANES
    n_chunks = T // CHUNK
    return [
        pltpu.VMEM((T, DA), F32),
        pltpu.VMEM((T, DA), F32),
        pltpu.VMEM((2, n_pairs, LANES, LANES), F32),
        pltpu.VMEM((2, n_chunks, n_pairs, LANES, LANES), BF16),
        pltpu.VMEM((2, T, HK), BF16),
        pltpu.VMEM((2, T, HK), BF16),
        pltpu.VMEM((2, T, HK), BF16),
    ]


def _mlstm_body(qk_ref, v_ref, og_ref, sm_ref, c0_ref, n0_ref, m0_ref, cw_ref, bm_ref, gw_ref,
                out_ref, cnew_ref, nnew_ref, mnew_ref,
                pad_scr, qk_scr, y_scr, c_scr, n_scr, m_scr, call_scr, nall_scr, mall_scr, g_scr, f_scr,
                *, grid_w, unroll):
    has_state = c0_ref is not None
    write_state = cnew_ref is not None
    T = qk_ref.shape[0]
    L = CHUNK
    N = T // L
    C2 = qk_ref.shape[1]
    HK = C2 // 2
    DK = HK // H_B
    DV = v_ref.shape[1] // H_B
    scale = DK ** -0.5
    n_pairs = HK // LANES
    P = pad_scr.shape[0] - T
    P0 = P // 2
    rows_img = T // grid_w

    lower, upper = _chunk_masks(L)
    tri = (lower.astype(BF16), upper.astype(BF16))
    tmask = (lower, upper)
    lane = lax.broadcasted_iota(jnp.int32, (1, LANES), 1)
    head_mask = (lane < DK, lane >= DK)
    lane_in = lane & (L - 1)

    def lane_cummax(x, d):
        k = 1
        while k < L:
            if d == 0:
                x = jnp.maximum(x, jnp.where(lane_in >= k, pltpu.roll(x, k, axis=1), -jnp.inf))
            else:
                x = jnp.maximum(x, jnp.where(lane_in < L - k, pltpu.roll(x, LANES - k, axis=1), -jnp.inf))
            k *= 2
        return x

    for d in range(2):
        for p in range(n_pairs):
            if has_state:
                c_scr[d, p] = c0_ref[d, p]
                n_scr[2 * d + p:2 * d + p + 1, :] = n0_ref[d, p:p + 1, :]
            else:
                c_scr[d, p] = jnp.zeros((LANES, LANES), F32)
                n_scr[2 * d + p:2 * d + p + 1, :] = jnp.zeros((1, LANES), F32)
    eye_h = (lax.broadcasted_iota(jnp.int32, (H_B, H_B), 0) == lax.broadcasted_iota(jnp.int32, (H_B, H_B), 1))

    def to_col(row):
        return jnp.sum(jnp.where(eye_h, row, 0.0), axis=1, keepdims=True)

    def to_row(col):
        return jnp.sum(jnp.where(eye_h, col, 0.0), axis=0, keepdims=True)

    for d in range(2):
        if has_state:
            m_scr[H_B * d:H_B * (d + 1), 0:1] = to_col(m0_ref[d:d + 1, :])
        else:
            m_scr[H_B * d:H_B * (d + 1), 0:1] = jnp.zeros((H_B, 1), F32)

    pad_scr[0:P0, :] = jnp.zeros((P0, C2), F32)
    pad_scr[P0 + T:P + T, :] = jnp.zeros((P - P0, C2), F32)

    def copy_in(i, carry):
        r0 = pl.multiple_of(i * L, L)
        pad_scr[pl.ds(P0 + r0, L), :] = qk_ref[pl.ds(r0, L), :]
        return carry

    lax.fori_loop(0, N, copy_in, 0)

    lane_c = lax.broadcasted_iota(jnp.int32, (1, C2), 1)
    qscale = jnp.where(lane_c < HK, scale, 1.0).astype(F32)
    sub = lax.broadcasted_iota(jnp.int32, (L, 1), 0)
    img_rows = (0,) if rows_img == 1 else (-1, 0, 1)

    def conv_tile(i, carry):
        r0 = pl.multiple_of(i * L, L)
        col = lax.rem(r0, grid_w) + sub
        ok_left = col >= 1
        ok_right = col <= grid_w - 2
        acc = jnp.zeros((L, C2), F32)
        for di in img_rows:
            blk = pad_scr[pl.ds(P0 + r0 + di * grid_w - 8, L + 16), :]
            left = jnp.where(ok_left, blk[7:7 + L, :], 0.0)
            mid = blk[8:8 + L, :]
            right = jnp.where(ok_right, blk[9:9 + L, :], 0.0)
            wr = 3 * (di + 1)
            acc = acc + left * cw_ref[wr:wr + 1, :] + mid * cw_ref[wr + 1:wr + 2, :] + right * cw_ref[wr + 2:wr + 3, :]
        qk_scr[pl.ds(r0, L), :] = _silu(acc) * qscale
        return carry

    lax.fori_loop(0, N, conv_tile, 0)

    gl = lane - GATE_LANE0
    is_f = ((gl >= H_B) & (gl < 2 * H_B)) | ((gl >= 3 * H_B) & (gl < 4 * H_B))

    def gate_tile(i, carry):
        rows = pl.ds(pl.multiple_of(i * L, L), L)
        x = sm_ref[rows, :] + bm_ref[...]
        y_scr[rows, :] = jnp.where(is_f, _log_sigmoid(x), x)
        return carry

    lax.fori_loop(0, N, gate_tile, 0)


    def state_group(ns):
        units = [u for n in ns for u in ((0, n), (1, N - 1 - n))]
        rows = [_chunk_rows(n) for _, n in units]
        xs = [y_scr[r, :] for r in rows]
        fsum = [_tri_sum(tri[d], x) for (d, _), x in zip(units, xs)]
        wk_all, f_end, c_end = [], [], []
        for (d, n), r, x, fs in zip(units, rows, xs, fsum):
            y = jnp.where(is_f, fs, x)
            li0 = GATE_LANE0 + 2 * H_B * d
            blk = jnp.concatenate([y, y], axis=0).T[li0:li0 + 2 * H_B, :]
            frow = pltpu.roll(blk, H_B, axis=0)
            grow = blk - frow
            g_scr[d, n] = grow
            f_scr[d, n] = frow
            e_col = L - 1 if d == 0 else 0
            f_end.append(frow[0:H_B, e_col:e_col + 1])
            ce = jnp.max(grow[0:H_B, :], axis=1, keepdims=True)
            c_end.append(ce)
            wk_all.append(jnp.exp(grow[0:H_B, 0:L] - ce))
        kv_all, ksum_all = [], []
        for r, wk4 in zip(rows, wk_all):
            kv_u, ks_u = [], []
            for p in range(n_pairs):
                kp = qk_scr[r, HK + p * LANES:HK + (p + 1) * LANES]
                kpb = kp.astype(BF16)
                kt = kp.T
                for j in range(2):
                    h = 2 * p + j
                    wk = wk4[h:h + 1, :]
                    kwt = (kt[j * DK:(j + 1) * DK, :] * wk).astype(BF16)
                    kv_u.append(_dot(kwt, v_ref[r, h * DV:(h + 1) * DV].astype(BF16)))
                    ks_u.append(_dot(jnp.broadcast_to(wk, (8, L)).astype(BF16), kpb)[0:1, :])
            kv_all.append(kv_u)
            ksum_all.append(ks_u)
        for (d, n), fe, ce, kv_u, ks_u in zip(units, f_end, c_end, kv_all, ksum_all):
            m_prev = m_scr[H_B * d:H_B * (d + 1), 0:1]
            mall_scr[d, n, 0:H_B, 0:1] = m_prev
            mx = jnp.maximum(m_prev, ce)
            a_all = jnp.exp(m_prev - mx)
            b_all = jnp.exp(ce - mx)
            m_scr[H_B * d:H_B * (d + 1), 0:1] = fe + mx
            for p in range(n_pairs):
                npair = n_scr[2 * d + p:2 * d + p + 1, :]
                nall_scr[d, n, p:p + 1, :] = npair
                a_s = [a_all[2 * p + j:2 * p + j + 1, :] for j in range(2)]
                b_s = [b_all[2 * p + j:2 * p + j + 1, :] for j in range(2)]
                for j in range(2):
                    hr = slice(j * DK, (j + 1) * DK)
                    cj = c_scr[d, p, hr, :]
                    call_scr[d, n, p, hr, :] = cj.astype(BF16)
                    c_scr[d, p, hr, :] = a_s[j] * cj + b_s[j] * kv_u[2 * p + j]
                n_scr[2 * d + p:2 * d + p + 1, :] = (
                    jnp.where(head_mask[0], a_s[0], a_s[1]) * npair
                    + jnp.where(head_mask[0], b_s[0] * ks_u[2 * p], b_s[1] * ks_u[2 * p + 1]))

    _chunk_loop(N, unroll, state_group)

    eye2 = ((lax.broadcasted_iota(jnp.int32, (L, LANES), 1) & (L - 1))
            == lax.broadcasted_iota(jnp.int32, (L, LANES), 0))
    ones8 = jnp.ones((8, L), BF16)
    sub_h = lax.broadcasted_iota(jnp.int32, (H_B, LANES), 0)

    def head_rows(vals):
        out = vals[0][0:H_B, :]
        for h in range(1, H_B):
            out = jnp.where(sub_h == h, vals[h][0:H_B, :], out)
        return out

    def out_group(ns):
        chunks = [(d, n) for n in ns for d in range(2)]
        units = [(d, n, p, j) for d, n in chunks for p in range(n_pairs) for j in range(2)]
        cms = [lane_cummax(g_scr[d, n], d)[0:H_B, :] for d, n in chunks]
        qks, qcs, qns = [], [], []
        for d, n, p, j in units:
            r = _chunk_rows(n)
            qmb = jnp.where(head_mask[j], qk_scr[r, p * LANES:(p + 1) * LANES], 0.0).astype(BF16)
            qks.append(_dot_nt(qmb, qk_scr[r, HK + p * LANES:HK + (p + 1) * LANES].astype(BF16)))
            qcs.append(_dot(qmb, call_scr[d, n, p]))
            n8 = jnp.broadcast_to(nall_scr[d, n, p:p + 1, :], (8, LANES)).astype(BF16)
            qns.append(_dot_nt(n8, jnp.concatenate([qmb, qmb], axis=0)))
        s_all = []
        for (d, n, p, j), qk in zip(units, qks):
            grow = g_scr[d, n, 2 * p + j:2 * p + j + 1, 0:L]
            e = jnp.where(tmask[d], grow, -jnp.inf)
            cmax = jnp.max(e, axis=-1, keepdims=True)
            s_all.append((qk * jnp.exp(e - cmax)).astype(BF16))
        nums = [_dot(s, v_ref[_chunk_rows(n), (2 * p + j) * DV:(2 * p + j + 1) * DV].astype(BF16))
                for (d, n, p, j), s in zip(units, s_all)]
        dens = [_dot_nt(ones8, jnp.concatenate([s, s], axis=0)) for s in s_all]
        scales = []
        for ci, (d, n) in enumerate(chunks):
            den_loc = head_rows(dens[ci * H_B:(ci + 1) * H_B])
            qn = head_rows(qns[ci * H_B:(ci + 1) * H_B])
            cm = cms[ci]
            m_prev = mall_scr[d, n, 0:H_B, 0:1]
            delta = cm - m_prev
            t = jnp.exp(-jnp.abs(delta))
            w_loc = jnp.where(delta <= 0.0, t, 1.0)
            w_inter = jnp.where(delta <= 0.0, 1.0, t)
            mt = f_scr[d, n, 0:H_B, :] + jnp.maximum(m_prev, cm)
            den = w_loc * den_loc + w_inter * qn
            rinv = 1.0 / jnp.maximum(jnp.abs(den), jnp.exp(-mt))
            scales.append(jnp.where(lane < L, w_loc * rinv, w_inter * rinv))
        hs = []
        for ui, (d, n, p, j) in enumerate(units):
            h = 2 * p + j
            lhs = jnp.where(eye2, scales[ui // H_B][h:h + 1, :], 0.0).astype(BF16)
            rhs = jnp.concatenate([nums[ui].astype(BF16), qcs[ui].astype(BF16)], axis=0)
            hs.append(_dot(lhs, rhs))
        for ni, n in enumerate(ns):
            r = _chunk_rows(n)
            for h in range(H_B):
                vs = slice(h * DV, (h + 1) * DV)
                o = hs[(2 * ni) * H_B + h] + hs[(2 * ni + 1) * H_B + h]
                out_ref[r, vs] = (_rms(o, gw_ref[:, vs]) * _sigmoid(og_ref[r, vs])).astype(out_ref.dtype)

    _chunk_loop(N, unroll, out_group)

    if write_state:
        for d in range(2):
            for p in range(n_pairs):
                cnew_ref[d, p] = c_scr[d, p]
                nnew_ref[d, p:p + 1, :] = n_scr[2 * d + p:2 * d + p + 1, :]
            mnew_ref[d:d + 1, :] = to_row(m_scr[H_B * d:H_B * (d + 1), 0:1])


def _mlstm_scratch(T, C2, grid_w):
    n_pairs = C2 // 2 // LANES
    n_chunks = T // CHUNK
    pad_rows = 2 * (grid_w + 8) if T // grid_w > 1 else 16
    return [
        pltpu.VMEM((T + pad_rows, C2), F32),
        pltpu.VMEM((T, C2), F32),
        pltpu.VMEM((T, SMALL_W), F32),
        pltpu.VMEM((2, n_pairs, LANES, LANES), F32),
        pltpu.VMEM((8, LANES), F32),
        pltpu.VMEM((8, LANES), F32),
        pltpu.VMEM((2, n_chunks, n_pairs, LANES, LANES), BF16),
        pltpu.VMEM((2, n_chunks, 8, LANES), F32),
        pltpu.VMEM((2, n_chunks, 8, LANES), F32),
        pltpu.VMEM((2, n_chunks, 8, LANES), F32),
        pltpu.VMEM((2, n_chunks, 8, LANES), F32),
    ]


N_GLA_SCRATCH = 7
N_MLSTM_SCRATCH = 11


def _scan_kernel(*refs, cols, has_state, write_state, n_cast, grid_w, unroll):
    refs = list(refs)
    z_ref = refs.pop(0)
    s0_ref = c0_ref = n0_ref = m0_ref = None
    if has_state:
        s0_ref, c0_ref, n0_ref, m0_ref = refs[:4]
        del refs[:4]
    wal_ref, bal_ref, gwa_ref, cw_ref, bm_ref, gwb_ref = refs[:6]
    del refs[:6]
    cast_in = refs[:n_cast]
    del refs[:n_cast]
    outa_ref, outb_ref = refs[:2]
    del refs[:2]
    snew_ref = cnew_ref = nnew_ref = mnew_ref = None
    if write_state:
        snew_ref, cnew_ref, nnew_ref, mnew_ref = refs[:4]
        del refs[:4]
    cast_out = refs[:n_cast]
    del refs[:n_cast]
    gla_scr = refs[:N_GLA_SCRATCH]
    mlstm_scr = refs[N_GLA_SCRATCH:]

    for src, dst in zip(cast_in, cast_out):
        dst[...] = src[...].astype(BF16)

    def view(name):
        c0, w = cols[name]
        return z_ref.at[:, pl.ds(c0, w)]

    sm_ref = view("small")
    _gla_body(view("qa"), view("ka"), view("va"), view("ga"), sm_ref, s0_ref, wal_ref, bal_ref, gwa_ref,
              outa_ref, snew_ref, *gla_scr, unroll=unroll)
    _mlstm_body(view("qkb"), view("vb"), view("ob"), sm_ref, c0_ref, n0_ref, m0_ref, cw_ref, bm_ref, gwb_ref,
                outb_ref, cnew_ref, nnew_ref, mnew_ref, *mlstm_scr, grid_w=grid_w, unroll=unroll)


def _scan_call(z3, states, lw, *, grid_w, write_state, casts=()):
    B, T, n_z = z3.shape
    HK = lw["wal_p"].shape[1] // 2
    DA = lw["gnorm_a_w"].shape[0]
    C2 = lw["conv9"].shape[1]
    DB = lw["gnorm_b_w"].shape[0]
    pa, pb = HK // LANES, C2 // 2 // LANES
    n_chunks = T // CHUNK
    has_state = states is not None
    widths = (("qa", HK), ("ka", HK), ("va", DA), ("ga", DA), ("qkb", C2), ("vb", DB), ("ob", DB),
              ("small", SMALL_W))
    cols, c0 = {}, 0
    for name, w in widths:
        cols[name] = (c0, w)
        c0 += w
    assert c0 == n_z
    cast_in_specs, cast_out_specs, cast_out_shape, cast_args = _cast_specs(casts, B)
    kern = functools.partial(_scan_kernel, cols=cols, has_state=has_state, write_state=write_state,
                             n_cast=len(casts), grid_w=grid_w, unroll=min(n_chunks, SCAN_UNROLL))

    def per_batch(shape):
        nd = len(shape)
        return pl.BlockSpec((None,) + tuple(shape), lambda b: (b,) + (0,) * nd)

    def whole(a):
        return pl.BlockSpec(a.shape, lambda b: (0,) * a.ndim)

    state_shapes = ((2, pa, LANES, LANES), (2, pb, LANES, LANES), (2, pb, LANES), (2, H_B))
    in_specs = [per_batch((T, n_z))]
    args = [z3]
    if has_state:
        s_gla, s_c, s_n, s_m = states
        args += [s_gla.reshape((B,) + state_shapes[0]), s_c.reshape((B,) + state_shapes[1]),
                 s_n.reshape((B,) + state_shapes[2]), s_m]
        in_specs += [per_batch(s) for s in state_shapes]
    small = [lw["wal_p"], lw["bal_p"], lw["gnorm_a_w"].reshape(1, DA), lw["conv9"], lw["bm_row"],
             lw["gnorm_b_w"].reshape(1, DB)]
    args += small + cast_args
    in_specs += [whole(a) for a in small] + cast_in_specs
    out_specs = [per_batch((T, DA)), per_batch((T, DB))]
    out_shape = [jax.ShapeDtypeStruct((B, T, DA), BF16), jax.ShapeDtypeStruct((B, T, DB), BF16)]
    if write_state:
        out_specs += [per_batch(s) for s in state_shapes]
        out_shape += [jax.ShapeDtypeStruct((B,) + s, F32) for s in state_shapes]
    out_specs += cast_out_specs
    out_shape += cast_out_shape
    scratch = _gla_scratch(T, HK, DA) + _mlstm_scratch(T, C2, grid_w)
    assert len(scratch) == N_GLA_SCRATCH + N_MLSTM_SCRATCH
    return pl.pallas_call(
        kern,
        grid=(B,),
        in_specs=in_specs,
        out_specs=out_specs,
        out_shape=out_shape,
        scratch_shapes=scratch,
        compiler_params=pltpu.CompilerParams(dimension_semantics=("arbitrary",),
                                             vmem_limit_bytes=VMEM_LIMIT),
        name="mixer_scans",
    )(*args)


def _outff_kernel(x_ref, a_ref, b_ref, mod_ref, n2_ref, fn_ref, wo_ref, w1_ref, w2_ref, y_ref,
                  *, mod_row0, tiles_per_batch, ff_chunk, final_norm):
    D = x_ref.shape[1]
    DA = a_ref.shape[1]
    row = _mod_row(mod_row0, tiles_per_batch)

    def mod(k):
        return mod_ref[pl.ds(row, 1), k * D:(k + 1) * D]

    y = _dot(a_ref[...], wo_ref[0:DA, :]) + _dot(b_ref[...], wo_ref[DA:, :])
    x1 = x_ref[...] + mod(2) * y
    h2 = (_rms(x1, n2_ref[...]) * (1.0 + mod(4)) + mod(3)).astype(BF16)
    acc = jnp.zeros(x1.shape, F32)
    for c0 in range(0, w1_ref.shape[1], ff_chunk):
        u = jnp.maximum(_dot(h2, w1_ref[:, c0:c0 + ff_chunk]), 0.0)
        acc = acc + _dot((u * u).astype(BF16), w2_ref[c0:c0 + ff_chunk, :])
    x2 = x1 + mod(5) * acc
    y_ref[...] = _rms(x2, fn_ref[...]) if final_norm else x2


def _outff_call(x2d, a2d, b2d, mod, norm2_w, final_w, wo, w1, w2, *, tm, mod_row0, tiles_per_batch,
                final_norm):
    M, D = x2d.shape
    DA = a2d.shape[1]
    DFF = w1.shape[1]
    kern = functools.partial(_outff_kernel, mod_row0=mod_row0, tiles_per_batch=tiles_per_batch,
                             ff_chunk=512, final_norm=final_norm)
    once = pl.Buffered(1)
    return pl.pallas_call(
        kern,
        grid=(M // tm,),
        in_specs=[
            pl.BlockSpec((tm, D), lambda i: (i, 0)),
            pl.BlockSpec((tm, DA), lambda i: (i, 0)),
            pl.BlockSpec((tm, D - DA), lambda i: (i, 0)),
            pl.BlockSpec(mod.shape, lambda i: (0, 0)),
            pl.BlockSpec((1, D), lambda i: (0, 0)),
            pl.BlockSpec((1, D), lambda i: (0, 0)),
            pl.BlockSpec((D, D), lambda i: (0, 0), pipeline_mode=once),
            pl.BlockSpec((D, DFF), lambda i: (0, 0), pipeline_mode=once),
            pl.BlockSpec((DFF, D), lambda i: (0, 0), pipeline_mode=once),
        ],
        out_specs=pl.BlockSpec((tm, D), lambda i: (i, 0)),
        out_shape=jax.ShapeDtypeStruct((M, D), F32),
        compiler_params=pltpu.CompilerParams(dimension_semantics=("arbitrary",),
                                             vmem_limit_bytes=VMEM_LIMIT),
        name="outproj_mlp",
    )(x2d, a2d, b2d, mod, norm2_w.reshape(1, D), final_w.reshape(1, D), wo, w1, w2)


def _block(x, mod, mod_row0, per_batch, grid_w, states, lw, ffw, final_w, final_norm, write_state):
    B, T, D = x.shape
    convert = ffw[0].dtype != BF16
    span = T if per_batch else B * T
    tm = TOKEN_TILE if span % TOKEN_TILE == 0 else T
    tiles_per_batch = (T // tm) if per_batch else None
    x2d = x.reshape(B * T, D)
    z = _inproj_call(x2d, mod, lw["norm1_w"], lw["w_in_t"], tm=tm, mod_row0=mod_row0,
                     tiles_per_batch=tiles_per_batch, big_rows=lw["big_rows"], small_rows=lw["small_rows"])
    z3 = z.reshape(B, T, z.shape[1])
    w_out, w_ff1, w_ff2 = ffw
    res = _scan_call(z3, states, lw, grid_w=grid_w, write_state=write_state,
                     casts=((w_out, 0), (w_ff1, 1), (w_ff2, 0)) if convert else ())
    out_a, out_b = res[0], res[1]
    new_states = tuple(res[2:6]) if write_state else None
    if convert:
        ffw = tuple(res[-3:])
    y = _outff_call(x2d, out_a.reshape(B * T, -1), out_b.reshape(B * T, -1), mod, lw["norm2_w"], final_w,
                    *ffw, tm=tm, mod_row0=mod_row0, tiles_per_batch=tiles_per_batch, final_norm=final_norm)
    return y.reshape(B, T, D), new_states, ffw


def _layer_weights(l, norm1_w, norm2_w, w_in, w_alpha2, b_alpha, b_mgate, conv_w, gnorm_a_w, gnorm_b_w):
    hk_a = w_alpha2.shape[-1]
    d_a = gnorm_a_w.shape[-1]
    d_b = gnorm_b_w.shape[-1]
    hk_b = conv_w.shape[-1] // 2
    sizes = (hk_a, hk_a, d_a, d_a, 2 * R_ALPHA, hk_b, hk_b, d_b, d_b, 4 * H_B)
    offs = [0]
    for s in sizes:
        offs.append(offs[-1] + s)
    big_rows = ((offs[0], offs[4] - offs[0]), (offs[5], offs[9] - offs[5]))
    small_rows = ((offs[4], offs[5] - offs[4]), (offs[9], offs[10] - offs[9]))
    assert all(n % LANES == 0 and r % 16 == 0 for r, n in big_rows)
    wal = w_alpha2[l]
    wal_p = jnp.zeros((SMALL_W, 2 * hk_a), F32)
    wal_p = wal_p.at[0:R_ALPHA, 0:hk_a].set(wal[0]).at[R_ALPHA:2 * R_ALPHA, hk_a:].set(wal[1]).astype(BF16)
    bm_row = jnp.zeros((1, SMALL_W), F32).at[0, GATE_LANE0:GATE_LANE0 + 4 * H_B].set(b_mgate[l].reshape(-1))
    return dict(
        norm1_w=norm1_w[l], norm2_w=norm2_w[l], w_in_t=jnp.swapaxes(w_in[l], 0, 1),
        big_rows=big_rows, small_rows=small_rows, wal_p=wal_p,
        bal_p=b_alpha[l].reshape(1, -1), bm_row=bm_row,
        conv9=conv_w[l].reshape(-1, conv_w.shape[-1]),
        gnorm_a_w=gnorm_a_w[l], gnorm_b_w=gnorm_b_w[l],
    )


def kernel(x_prompt, x_sample, c, state_gla, state_mlstm_C, state_mlstm_n, state_mlstm_m, c_ctx, w_ada, b_ada, norm1_w, norm2_w, w_in, w_alpha2, b_alpha, b_mgate, conv_w, gnorm_a_w, gnorm_b_w, w_out, w_ff1, w_ff2, final_norm_w):
    depth = w_in.shape[0]
    D = x_prompt.shape[-1]
    Bp, Tp, _ = x_prompt.shape
    Bs = x_sample.shape[0]
    assert 1 + Bs <= COND_ROWS
    cond = jnp.concatenate([c_ctx[None, :], c, jnp.zeros((COND_ROWS - 1 - Bs, D), F32)], axis=0)
    xp, xs = x_prompt, x_sample
    s_gla, s_c, s_n, s_m = [], [], [], []
    for l in range(depth):
        lw = _layer_weights(l, norm1_w, norm2_w, w_in, w_alpha2, b_alpha, b_mgate, conv_w,
                            gnorm_a_w, gnorm_b_w)
        mod = _ada_call(cond, w_ada[l], b_ada[l])
        last = l == depth - 1
        xp, ctx, ffw = _block(xp, mod, 0, False, Tp, None, lw, (w_out[l], w_ff1[l], w_ff2[l]),
                              final_norm_w, last, True)
        s_gla.append(ctx[0].reshape(Bp, 2, H_A, -1, ctx[0].shape[-1]))
        s_c.append(ctx[1].reshape(Bp, 2, H_B, -1, ctx[1].shape[-1]))
        s_n.append(ctx[2].reshape(Bp, 2, H_B, -1))
        s_m.append(ctx[3])
        cached = (state_gla[:, l], state_mlstm_C[:, l], state_mlstm_n[:, l], state_mlstm_m[:, l])
        xs, _, _ = _block(xs, mod, 1, True, GRID_W, cached, lw, ffw, final_norm_w, last, False)
    dt = x_prompt.dtype
    return (xp, xs, jnp.stack(s_gla, axis=1).astype(dt), jnp.stack(s_c, axis=1).astype(dt),
            jnp.stack(s_n, axis=1).astype(dt), jnp.stack(s_m, axis=1).astype(dt))
```

```python
import functools

import jax
import jax.numpy as jnp
from jax import lax
from jax.experimental import pallas as pl
from jax.experimental.pallas import tpu as pltpu

F32 = jnp.float32
BF16 = jnp.bfloat16

GRID_W = 64
H_A = 4
H_B = 4
R_ALPHA = 16
TAU_GLA = 16.0
CHUNK = 64
EPS = 1e-6
LANES = 128
COND_ROWS = 8
SMALL_W = LANES
GATE_LANE0 = 2 * R_ALPHA
VMEM_LIMIT = 56 * 1024 * 1024
SCAN_UNROLL = 4
TOKEN_TILE = 512


def _sigmoid(x):
    return 1.0 / (1.0 + jnp.exp(-x))


def _silu(x):
    return x * _sigmoid(x)


def _log_sigmoid(x):
    return jnp.minimum(x, 0.0) - jnp.log1p(jnp.exp(-jnp.abs(x)))


def _dot(a, b):
    return jnp.dot(a, b, preferred_element_type=F32)


def _dot_nt(a, b):
    return lax.dot_general(a, b, (((1,), (1,)), ((), ())), preferred_element_type=F32)


def _rms(x, w):
    return x * lax.rsqrt(jnp.mean(x * x, axis=-1, keepdims=True) + EPS) * w


def _tri_sum(tri, x):
    hi = x.astype(BF16)
    r1 = x - hi.astype(F32)
    mid = r1.astype(BF16)
    lo = (r1 - mid.astype(F32)).astype(BF16)
    return _dot(tri, hi) + _dot(tri, mid) + _dot(tri, lo)


def _chunk_masks(L):
    row = lax.broadcasted_iota(jnp.int32, (L, L), 0)
    col = lax.broadcasted_iota(jnp.int32, (L, L), 1)
    lower = row >= col
    upper = row <= col
    return lower, upper


def _ada_kernel(c_ref, w_ref, b_ref, o_ref):
    s = _silu(c_ref[...])
    o_ref[...] = _dot(s.astype(BF16), w_ref[...].astype(BF16)) + b_ref[...]


def _ada_call(cond, w_ada, b_ada):
    D = cond.shape[1]
    n_out = w_ada.shape[1]
    tn = 1024
    return pl.pallas_call(
        _ada_kernel,
        grid=(n_out // tn,),
        in_specs=[
            pl.BlockSpec((COND_ROWS, D), lambda j: (0, 0)),
            pl.BlockSpec((D, tn), lambda j: (0, j)),
            pl.BlockSpec((1, tn), lambda j: (0, j)),
        ],
        out_specs=pl.BlockSpec((COND_ROWS, tn), lambda j: (0, j)),
        out_shape=jax.ShapeDtypeStruct((COND_ROWS, n_out), F32),
        compiler_params=pltpu.CompilerParams(dimension_semantics=("arbitrary",)),
        name="ada_mod",
    )(cond, w_ada, b_ada.reshape(1, n_out))


def _tile_group(n_ctx, tiles_per_req):
    i = pl.program_id(0)
    is_ctx = i < n_ctx
    row = jnp.where(is_ctx, 0, 1 + jnp.maximum(i - n_ctx, 0) // tiles_per_req)
    return is_ctx, row


def _ctx_tile(n_ctx):
    return lambda i: (jnp.minimum(i, n_ctx - 1), 0)


def _lat_tile(n_ctx):
    return lambda i: (jnp.maximum(i - n_ctx, 0), 0)


def _inproj_kernel(xc_ref, xl_ref, mod_ref, nw_ref, wt_ref, z_ref, wb_scr, *, n_ctx, tiles_per_req, big_rows,
                   small_rows):
    D = xc_ref.shape[1]

    @pl.when(pl.program_id(0) == 0)
    def _():
        col = 0
        for r0, n in big_rows:
            for k in range(n // LANES):
                blk = wt_ref[r0 + k * LANES:r0 + (k + 1) * LANES, :]
                wb_scr[:, col:col + LANES] = blk.T.astype(BF16)
                col += LANES
        parts = [wt_ref[r0:r0 + n, :] for r0, n in small_rows]
        n_small = sum(n for _, n in small_rows)
        parts.append(jnp.zeros((SMALL_W - n_small, D), F32))
        wb_scr[:, col:col + SMALL_W] = jnp.concatenate(parts, axis=0).T.astype(BF16)

    is_ctx, row = _tile_group(n_ctx, tiles_per_req)
    x = jnp.where(is_ctx, xc_ref[...], xl_ref[...])
    sh1 = mod_ref[pl.ds(row, 1), 0:D]
    sc1 = mod_ref[pl.ds(row, 1), D:2 * D]
    h = _rms(x, nw_ref[...]) * (1.0 + sc1) + sh1
    z_ref[...] = _dot(h.astype(BF16), wb_scr[...])


def _inproj_call(xc2d, xl2d, mod, norm_w, w_in_t, *, tm, tiles_per_req, big_rows, small_rows):
    (Mc, D), Ml = xc2d.shape, xl2d.shape[0]
    n_ctx = Mc // tm
    n_out = sum(n for _, n in big_rows) + SMALL_W
    kern = functools.partial(_inproj_kernel, n_ctx=n_ctx, tiles_per_req=tiles_per_req,
                             big_rows=big_rows, small_rows=small_rows)
    return pl.pallas_call(
        kern,
        grid=((Mc + Ml) // tm,),
        in_specs=[
            pl.BlockSpec((tm, D), _ctx_tile(n_ctx)),
            pl.BlockSpec((tm, D), _lat_tile(n_ctx)),
            pl.BlockSpec(mod.shape, lambda i: (0, 0)),
            pl.BlockSpec((1, D), lambda i: (0, 0)),
            pl.BlockSpec(w_in_t.shape, lambda i: (0, 0), pipeline_mode=pl.Buffered(1)),
        ],
        out_specs=pl.BlockSpec((tm, n_out), lambda i: (i, 0)),
        out_shape=jax.ShapeDtypeStruct((Mc + Ml, n_out), F32),
        scratch_shapes=[pltpu.VMEM((D, n_out), BF16)],
        compiler_params=pltpu.CompilerParams(dimension_semantics=("arbitrary",),
                                             vmem_limit_bytes=VMEM_LIMIT),
        name="norm_inproj",
    )(xc2d, xl2d, mod, norm_w.reshape(1, D), w_in_t)


def _chunk_loop(n_chunks, unroll, fn):
    if unroll >= n_chunks:
        fn(list(range(n_chunks)))
        return

    def body(i, carry):
        fn([i * unroll + u for u in range(unroll)])
        return carry

    lax.fori_loop(0, n_chunks // unroll, body, 0)


def _chunk_rows(n):
    if isinstance(n, int):
        return pl.ds(n * CHUNK, CHUNK)
    return pl.ds(pl.multiple_of(n * CHUNK, CHUNK), CHUNK)


def _cast_specs(casts, n_steps):
    in_specs, out_specs, out_shape, args = [], [], [], []
    for w, axis in casts:
        blk = list(w.shape)
        assert blk[axis] % n_steps == 0
        blk[axis] //= n_steps
        assert blk[0] % 16 == 0 and blk[1] % LANES == 0
        idx = (lambda b: (b, 0)) if axis == 0 else (lambda b: (0, b))
        in_specs.append(pl.BlockSpec(tuple(blk), idx))
        out_specs.append(pl.BlockSpec(tuple(blk), idx))
        out_shape.append(jax.ShapeDtypeStruct(w.shape, BF16))
        args.append(w)
    return in_specs, out_specs, out_shape, args


def _gla_body(q_ref, k_ref, v_ref, g_ref, sm_ref, s0_ref, wal_ref, bal_ref, gw_ref, out_ref, snew_ref,
              of_scr, ob_scr, st_scr, sall_scr, qh_scr, qs_scr, kh_scr, *, unroll):
    has_state = s0_ref is not None
    write_state = snew_ref is not None
    T = q_ref.shape[0]
    L = CHUNK
    N = T // L
    HK = q_ref.shape[1]
    DK = HK // H_A
    DV = v_ref.shape[1] // H_A
    scale = DK ** -0.5
    n_pairs = HK // LANES

    lower, upper = _chunk_masks(L)
    tri = (lower.astype(BF16), upper.astype(BF16))
    tmask = (lower, upper)
    lane = lax.broadcasted_iota(jnp.int32, (1, LANES), 1)
    head_mask = (lane < DK, lane >= DK)
    o_scr = (of_scr, ob_scr)

    for d in range(2):
        for p in range(n_pairs):
            if has_state:
                st_scr[d, p] = s0_ref[d, p].T
            else:
                st_scr[d, p] = jnp.zeros((LANES, LANES), F32)

    def state_group(ns):
        units = [u for n in ns for u in ((0, n), (1, N - 1 - n))]
        rows = [_chunk_rows(n) for _, n in units]
        pre = [_dot(sm_ref[r, :].astype(BF16), wal_ref[:, d * HK:(d + 1) * HK]) + bal_ref[:, d * HK:(d + 1) * HK]
               for (d, _), r in zip(units, rows)]
        g = [_log_sigmoid(x) * (1.0 / TAU_GLA) for x in pre]
        b = [_tri_sum(tri[d], gi) for (d, _), gi in zip(units, g)]
        ks_all, dec_all = [], []
        for (d, _), r, bi in zip(units, rows, b):
            bend = bi[L - 1:L, :] if d == 0 else bi[0:1, :]
            q = q_ref[r, :] * scale
            ks = (k_ref[r, :] * jnp.exp(bend - bi)).astype(BF16)
            qh_scr[d, r, :] = (q * jnp.exp(bi - bend)).astype(BF16)
            qs_scr[d, r, :] = (q * jnp.exp(bi)).astype(BF16)
            kh_scr[d, r, :] = ks
            ks_all.append(ks)
            dec_all.append(jnp.exp(bend))
        upd_all = [[[_dot(v_ref[r, (2 * p + j) * DV:(2 * p + j + 1) * DV].T.astype(BF16),
                          ks[:, p * LANES:(p + 1) * LANES]) for j in range(2)]
                    for p in range(n_pairs)]
                   for r, ks in zip(rows, ks_all)]
        for (d, n), dec, upd in zip(units, dec_all, upd_all):
            for p in range(n_pairs):
                st = st_scr[d, p]
                sall_scr[d, n, p] = st.astype(BF16)
                st_scr[d, p] = (st * dec[:, p * LANES:(p + 1) * LANES]
                                + jnp.where(head_mask[0], upd[p][0], upd[p][1]))

    _chunk_loop(N, unroll, state_group)

    def out_group(ns):
        units = [(d, n, p, j) for n in ns for d in range(2) for p in range(n_pairs) for j in range(2)]
        scores, inter = [], []
        for d, n, p, j in units:
            r = _chunk_rows(n)
            ls = slice(p * LANES, (p + 1) * LANES)
            qh = qh_scr[d, r, ls]
            qs = qs_scr[d, r, ls]
            qm = jnp.where(head_mask[j], qh, jnp.zeros_like(qh))
            qsm = jnp.where(head_mask[j], qs, jnp.zeros_like(qs))
            scores.append(_dot_nt(qm, kh_scr[d, r, ls]))
            inter.append(_dot_nt(qsm, sall_scr[d, n, p]))
        probs = [jnp.where(tmask[d], a, 0.0).astype(BF16) for (d, _, _, _), a in zip(units, scores)]
        for (d, n, p, j), a, it in zip(units, probs, inter):
            vs = slice((2 * p + j) * DV, (2 * p + j + 1) * DV)
            r = _chunk_rows(n)
            o_scr[d][r, vs] = _dot(a, v_ref[r, vs].astype(BF16)) + it

    _chunk_loop(N, unroll, out_group)

    def epilogue(i, carry):
        rows = pl.ds(pl.multiple_of(i * L, L), L)
        for h in range(H_A):
            vs = slice(h * DV, (h + 1) * DV)
            o = of_scr[rows, vs] + ob_scr[rows, vs]
            out_ref[rows, vs] = (_rms(o, gw_ref[:, vs]) * _silu(g_ref[rows, vs])).astype(out_ref.dtype)
        return carry

    lax.fori_loop(0, N, epilogue, 0)

    if write_state:
        for d in range(2):
            for p in range(n_pairs):
                snew_ref[d, p] = st_scr[d, p].T


def _gla_scratch(T, HK, DA):
    n_pairs = HK // LANES
    n_chunks = T // CHUNK
    return [
        pltpu.VMEM((T, DA), F32),
        pltpu.VMEM((T, DA), F32),
        pltpu.VMEM((2, n_pairs, LANES, LANES), F32),
        pltpu.VMEM((2, n_chunks, n_pairs, LANES, LANES), BF16),
        pltpu.VMEM((2, T, HK), BF16),
        pltpu.VMEM((2, T, HK), BF16),
        pltpu.VMEM((2, T, HK), BF16),
    ]


def _mlstm_body(qk_ref, v_ref, og_ref, sm_ref, c0_ref, n0_ref, m0_ref, cw_ref, bm_ref, gw_ref,
                out_ref, cnew_ref, nnew_ref, mnew_ref,
                pad_scr, qk_scr, y_scr, c_scr, n_scr, m_scr, call_scr, nall_scr, mall_scr, g_scr, f_scr,
                *, grid_w, unroll):
    has_state = c0_ref is not None
    write_state = cnew_ref is not None
    T = qk_ref.shape[0]
    L = CHUNK
    N = T // L
    C2 = qk_ref.shape[1]
    HK = C2 // 2
    DK = HK // H_B
    DV = v_ref.shape[1] // H_B
    scale = DK ** -0.5
    n_pairs = HK // LANES
    P = pad_scr.shape[0] - T
    P0 = P // 2
    rows_img = T // grid_w

    lower, upper = _chunk_masks(L)
    tri = (lower.astype(BF16), upper.astype(BF16))
    tmask = (lower, upper)
    lane = lax.broadcasted_iota(jnp.int32, (1, LANES), 1)
    head_mask = (lane < DK, lane >= DK)
    lane_in = lane & (L - 1)

    def lane_cummax(x, d):
        k = 1
        while k < L:
            if d == 0:
                x = jnp.maximum(x, jnp.where(lane_in >= k, pltpu.roll(x, k, axis=1), -jnp.inf))
            else:
                x = jnp.maximum(x, jnp.where(lane_in < L - k, pltpu.roll(x, LANES - k, axis=1), -jnp.inf))
            k *= 2
        return x

    for d in range(2):
        for p in range(n_pairs):
            if has_state:
                c_scr[d, p] = c0_ref[d, p]
                n_scr[2 * d + p:2 * d + p + 1, :] = n0_ref[d, p:p + 1, :]
            else:
                c_scr[d, p] = jnp.zeros((LANES, LANES), F32)
                n_scr[2 * d + p:2 * d + p + 1, :] = jnp.zeros((1, LANES), F32)
    eye_h = (lax.broadcasted_iota(jnp.int32, (H_B, H_B), 0) == lax.broadcasted_iota(jnp.int32, (H_B, H_B), 1))

    def to_col(row):
        return jnp.sum(jnp.where(eye_h, row, 0.0), axis=1, keepdims=True)

    def to_row(col):
        return jnp.sum(jnp.where(eye_h, col, 0.0), axis=0, keepdims=True)

    for d in range(2):
        if has_state:
            m_scr[H_B * d:H_B * (d + 1), 0:1] = to_col(m0_ref[d:d + 1, :])
        else:
            m_scr[H_B * d:H_B * (d + 1), 0:1] = jnp.zeros((H_B, 1), F32)

    pad_scr[0:P0, :] = jnp.zeros((P0, C2), F32)
    pad_scr[P0 + T:P + T, :] = jnp.zeros((P - P0, C2), F32)

    def copy_in(i, carry):
        r0 = pl.multiple_of(i * L, L)
        pad_scr[pl.ds(P0 + r0, L), :] = qk_ref[pl.ds(r0, L), :]
        return carry

    lax.fori_loop(0, N, copy_in, 0)

    lane_c = lax.broadcasted_iota(jnp.int32, (1, C2), 1)
    qscale = jnp.where(lane_c < HK, scale, 1.0).astype(F32)
    sub = lax.broadcasted_iota(jnp.int32, (L, 1), 0)
    img_rows = (0,) if rows_img == 1 else (-1, 0, 1)

    def conv_tile(i, carry):
        r0 = pl.multiple_of(i * L, L)
        col = lax.rem(r0, grid_w) + sub
        ok_left = col >= 1
        ok_right = col <= grid_w - 2
        acc = jnp.zeros((L, C2), F32)
        for di in img_rows:
            blk = pad_scr[pl.ds(P0 + r0 + di * grid_w - 8, L + 16), :]
            left = jnp.where(ok_left, blk[7:7 + L, :], 0.0)
            mid = blk[8:8 + L, :]
            right = jnp.where(ok_right, blk[9:9 + L, :], 0.0)
            wr = 3 * (di + 1)
            acc = acc + left * cw_ref[wr:wr + 1, :] + mid * cw_ref[wr + 1:wr + 2, :] + right * cw_ref[wr + 2:wr + 3, :]
        qk_scr[pl.ds(r0, L), :] = _silu(acc) * qscale
        return carry

    lax.fori_loop(0, N, conv_tile, 0)

    gl = lane - GATE_LANE0
    is_f = ((gl >= H_B) & (gl < 2 * H_B)) | ((gl >= 3 * H_B) & (gl < 4 * H_B))

    def gate_tile(i, carry):
        rows = pl.ds(pl.multiple_of(i * L, L), L)
        x = sm_ref[rows, :] + bm_ref[...]
        y_scr[rows, :] = jnp.where(is_f, _log_sigmoid(x), x)
        return carry

    lax.fori_loop(0, N, gate_tile, 0)


    def state_group(ns):
        units = [u for n in ns for u in ((0, n), (1, N - 1 - n))]
        rows = [_chunk_rows(n) for _, n in units]
        xs = [y_scr[r, :] for r in rows]
        fsum = [_tri_sum(tri[d], x) for (d, _), x in zip(units, xs)]
        wk_all, f_end, c_end = [], [], []
        for (d, n), r, x, fs in zip(units, rows, xs, fsum):
            y = jnp.where(is_f, fs, x)
            li0 = GATE_LANE0 + 2 * H_B * d
            blk = jnp.concatenate([y, y], axis=0).T[li0:li0 + 2 * H_B, :]
            frow = pltpu.roll(blk, H_B, axis=0)
            grow = blk - frow
            g_scr[d, n] = grow
            f_scr[d, n] = frow
            e_col = L - 1 if d == 0 else 0
            f_end.append(frow[0:H_B, e_col:e_col + 1])
            ce = jnp.max(grow[0:H_B, :], axis=1, keepdims=True)
            c_end.append(ce)
            wk_all.append(jnp.exp(grow[0:H_B, 0:L] - ce))
        kv_all, ksum_all = [], []
        for r, wk4 in zip(rows, wk_all):
            kv_u, ks_u = [], []
            for p in range(n_pairs):
                kp = qk_scr[r, HK + p * LANES:HK + (p + 1) * LANES]
                kpb = kp.astype(BF16)
                kt = kp.T
                for j in range(2):
                    h = 2 * p + j
                    wk = wk4[h:h + 1, :]
                    kwt = (kt[j * DK:(j + 1) * DK, :] * wk).astype(BF16)
                    kv_u.append(_dot(kwt, v_ref[r, h * DV:(h + 1) * DV].astype(BF16)))
                    ks_u.append(_dot(jnp.broadcast_to(wk, (8, L)).astype(BF16), kpb)[0:1, :])
            kv_all.append(kv_u)
            ksum_all.append(ks_u)
        for (d, n), fe, ce, kv_u, ks_u in zip(units, f_end, c_end, kv_all, ksum_all):
            m_prev = m_scr[H_B * d:H_B * (d + 1), 0:1]
            mall_scr[d, n, 0:H_B, 0:1] = m_prev
            mx = jnp.maximum(m_prev, ce)
            a_all = jnp.exp(m_prev - mx)
            b_all = jnp.exp(ce - mx)
            m_scr[H_B * d:H_B * (d + 1), 0:1] = fe + mx
            for p in range(n_pairs):
                npair = n_scr[2 * d + p:2 * d + p + 1, :]
                nall_scr[d, n, p:p + 1, :] = npair
                a_s = [a_all[2 * p + j:2 * p + j + 1, :] for j in range(2)]
                b_s = [b_all[2 * p + j:2 * p + j + 1, :] for j in range(2)]
                for j in range(2):
                    hr = slice(j * DK, (j + 1) * DK)
                    cj = c_scr[d, p, hr, :]
                    call_scr[d, n, p, hr, :] = cj.astype(BF16)
                    c_scr[d, p, hr, :] = a_s[j] * cj + b_s[j] * kv_u[2 * p + j]
                n_scr[2 * d + p:2 * d + p + 1, :] = (
                    jnp.where(head_mask[0], a_s[0], a_s[1]) * npair
                    + jnp.where(head_mask[0], b_s[0] * ks_u[2 * p], b_s[1] * ks_u[2 * p + 1]))

    _chunk_loop(N, unroll, state_group)

    eye2 = ((lax.broadcasted_iota(jnp.int32, (L, LANES), 1) & (L - 1))
            == lax.broadcasted_iota(jnp.int32, (L, LANES), 0))
    ones8 = jnp.ones((8, L), BF16)
    sub_h = lax.broadcasted_iota(jnp.int32, (H_B, LANES), 0)

    def head_rows(vals):
        out = vals[0][0:H_B, :]
        for h in range(1, H_B):
            out = jnp.where(sub_h == h, vals[h][0:H_B, :], out)
        return out

    def out_group(ns):
        chunks = [(d, n) for n in ns for d in range(2)]
        units = [(d, n, p, j) for d, n in chunks for p in range(n_pairs) for j in range(2)]
        cms = [lane_cummax(g_scr[d, n], d)[0:H_B, :] for d, n in chunks]
        qks, qcs, qns = [], [], []
        for d, n, p, j in units:
            r = _chunk_rows(n)
            qmb = jnp.where(head_mask[j], qk_scr[r, p * LANES:(p + 1) * LANES], 0.0).astype(BF16)
            qks.append(_dot_nt(qmb, qk_scr[r, HK + p * LANES:HK + (p + 1) * LANES].astype(BF16)))
            qcs.append(_dot(qmb, call_scr[d, n, p]))
            n8 = jnp.broadcast_to(nall_scr[d, n, p:p + 1, :], (8, LANES)).astype(BF16)
            qns.append(_dot_nt(n8, jnp.concatenate([qmb, qmb], axis=0)))
        s_all = []
        for (d, n, p, j), qk in zip(units, qks):
            grow = g_scr[d, n, 2 * p + j:2 * p + j + 1, 0:L]
            e = jnp.where(tmask[d], grow, -jnp.inf)
            cmax = jnp.max(e, axis=-1, keepdims=True)
            s_all.append((qk * jnp.exp(e - cmax)).astype(BF16))
        nums = [_dot(s, v_ref[_chunk_rows(n), (2 * p + j) * DV:(2 * p + j + 1) * DV].astype(BF16))
                for (d, n, p, j), s in zip(units, s_all)]
        dens = [_dot_nt(ones8, jnp.concatenate([s, s], axis=0)) for s in s_all]
        scales = []
        for ci, (d, n) in enumerate(chunks):
            den_loc = head_rows(dens[ci * H_B:(ci + 1) * H_B])
            qn = head_rows(qns[ci * H_B:(ci + 1) * H_B])
            cm = cms[ci]
            m_prev = mall_scr[d, n, 0:H_B, 0:1]
            delta = cm - m_prev
            t = jnp.exp(-jnp.abs(delta))
            w_loc = jnp.where(delta <= 0.0, t, 1.0)
            w_inter = jnp.where(delta <= 0.0, 1.0, t)
            mt = f_scr[d, n, 0:H_B, :] + jnp.maximum(m_prev, cm)
            den = w_loc * den_loc + w_inter * qn
            rinv = 1.0 / jnp.maximum(jnp.abs(den), jnp.exp(-mt))
            scales.append(jnp.where(lane < L, w_loc * rinv, w_inter * rinv))
        hs = []
        for ui, (d, n, p, j) in enumerate(units):
            h = 2 * p + j
            lhs = jnp.where(eye2, scales[ui // H_B][h:h + 1, :], 0.0).astype(BF16)
            rhs = jnp.concatenate([nums[ui].astype(BF16), qcs[ui].astype(BF16)], axis=0)
            hs.append(_dot(lhs, rhs))
        for ni, n in enumerate(ns):
            r = _chunk_rows(n)
            for h in range(H_B):
                vs = slice(h * DV, (h + 1) * DV)
                o = hs[(2 * ni) * H_B + h] + hs[(2 * ni + 1) * H_B + h]
                out_ref[r, vs] = (_rms(o, gw_ref[:, vs]) * _sigmoid(og_ref[r, vs])).astype(out_ref.dtype)

    _chunk_loop(N, unroll, out_group)

    if write_state:
        for d in range(2):
            for p in range(n_pairs):
                cnew_ref[d, p] = c_scr[d, p]
                nnew_ref[d, p:p + 1, :] = n_scr[2 * d + p:2 * d + p + 1, :]
            mnew_ref[d:d + 1, :] = to_row(m_scr[H_B * d:H_B * (d + 1), 0:1])


def _mlstm_scratch(T, C2, grid_w):
    n_pairs = C2 // 2 // LANES
    n_chunks = T // CHUNK
    pad_rows = 2 * (grid_w + 8) if T // grid_w > 1 else 16
    return [
        pltpu.VMEM((T + pad_rows, C2), F32),
        pltpu.VMEM((T, C2), F32),
        pltpu.VMEM((T, SMALL_W), F32),
        pltpu.VMEM((2, n_pairs, LANES, LANES), F32),
        pltpu.VMEM((8, LANES), F32),
        pltpu.VMEM((8, LANES), F32),
        pltpu.VMEM((2, n_chunks, n_pairs, LANES, LANES), BF16),
        pltpu.VMEM((2, n_chunks, 8, LANES), F32),
        pltpu.VMEM((2, n_chunks, 8, LANES), F32),
        pltpu.VMEM((2, n_chunks, 8, LANES), F32),
        pltpu.VMEM((2, n_chunks, 8, LANES), F32),
    ]


N_GLA_SCRATCH = 7
N_MLSTM_SCRATCH = 11


def _scan_kernel(*refs, cols, has_state, write_state, n_cast, grid_w, unroll):
    refs = list(refs)
    z_ref = refs.pop(0)
    s0_ref = c0_ref = n0_ref = m0_ref = None
    if has_state:
        s0_ref, c0_ref, n0_ref, m0_ref = refs[:4]
        del refs[:4]
    wal_ref, bal_ref, gwa_ref, cw_ref, bm_ref, gwb_ref = refs[:6]
    del refs[:6]
    cast_in = refs[:n_cast]
    del refs[:n_cast]
    outa_ref, outb_ref = refs[:2]
    del refs[:2]
    snew_ref = cnew_ref = nnew_ref = mnew_ref = None
    if write_state:
        snew_ref, cnew_ref, nnew_ref, mnew_ref = refs[:4]
        del refs[:4]
    cast_out = refs[:n_cast]
    del refs[:n_cast]
    gla_scr = refs[:N_GLA_SCRATCH]
    mlstm_scr = refs[N_GLA_SCRATCH:]

    for src, dst in zip(cast_in, cast_out):
        dst[...] = src[...].astype(BF16)

    def view(name):
        c0, w = cols[name]
        return z_ref.at[:, pl.ds(c0, w)]

    sm_ref = view("small")
    _gla_body(view("qa"), view("ka"), view("va"), view("ga"), sm_ref, s0_ref, wal_ref, bal_ref, gwa_ref,
              outa_ref, snew_ref, *gla_scr, unroll=unroll)
    _mlstm_body(view("qkb"), view("vb"), view("ob"), sm_ref, c0_ref, n0_ref, m0_ref, cw_ref, bm_ref, gwb_ref,
                outb_ref, cnew_ref, nnew_ref, mnew_ref, *mlstm_scr, grid_w=grid_w, unroll=unroll)


def _scan_call(z2d, row0, B, T, states, lw, *, grid_w, write_state, casts=()):
    n_z = z2d.shape[1]
    assert row0 % T == 0 and z2d.shape[0] % T == 0
    z3 = z2d.reshape(z2d.shape[0] // T, T, n_z)
    blk0 = row0 // T
    HK = lw["wal_p"].shape[1] // 2
    DA = lw["gnorm_a_w"].shape[0]
    C2 = lw["conv9"].shape[1]
    DB = lw["gnorm_b_w"].shape[0]
    pa, pb = HK // LANES, C2 // 2 // LANES
    n_chunks = T // CHUNK
    has_state = states is not None
    widths = (("qa", HK), ("ka", HK), ("va", DA), ("ga", DA), ("qkb", C2), ("vb", DB), ("ob", DB),
              ("small", SMALL_W))
    cols, c0 = {}, 0
    for name, w in widths:
        cols[name] = (c0, w)
        c0 += w
    assert c0 == n_z
    cast_in_specs, cast_out_specs, cast_out_shape, cast_args = _cast_specs(casts, B)
    kern = functools.partial(_scan_kernel, cols=cols, has_state=has_state, write_state=write_state,
                             n_cast=len(casts), grid_w=grid_w, unroll=min(n_chunks, SCAN_UNROLL))

    def per_batch(shape):
        nd = len(shape)
        return pl.BlockSpec((None,) + tuple(shape), lambda b: (b,) + (0,) * nd)

    def whole(a):
        return pl.BlockSpec(a.shape, lambda b: (0,) * a.ndim)

    state_shapes = ((2, pa, LANES, LANES), (2, pb, LANES, LANES), (2, pb, LANES), (2, H_B))
    in_specs = [pl.BlockSpec((None, T, n_z), lambda b: (b + blk0, 0, 0))]
    args = [z3]
    if has_state:
        s_gla, s_c, s_n, s_m = states
        args += [s_gla.reshape((B,) + state_shapes[0]), s_c.reshape((B,) + state_shapes[1]),
                 s_n.reshape((B,) + state_shapes[2]), s_m]
        in_specs += [per_batch(s) for s in state_shapes]
    small = [lw["wal_p"], lw["bal_p"], lw["gnorm_a_w"].reshape(1, DA), lw["conv9"], lw["bm_row"],
             lw["gnorm_b_w"].reshape(1, DB)]
    args += small + cast_args
    in_specs += [whole(a) for a in small] + cast_in_specs
    out_specs = [per_batch((T, DA)), per_batch((T, DB))]
    out_shape = [jax.ShapeDtypeStruct((B, T, DA), BF16), jax.ShapeDtypeStruct((B, T, DB), BF16)]
    if write_state:
        out_specs += [per_batch(s) for s in state_shapes]
        out_shape += [jax.ShapeDtypeStruct((B,) + s, F32) for s in state_shapes]
    out_specs += cast_out_specs
    out_shape += cast_out_shape
    scratch = _gla_scratch(T, HK, DA) + _mlstm_scratch(T, C2, grid_w)
    assert len(scratch) == N_GLA_SCRATCH + N_MLSTM_SCRATCH
    return pl.pallas_call(
        kern,
        grid=(B,),
        in_specs=in_specs,
        out_specs=out_specs,
        out_shape=out_shape,
        scratch_shapes=scratch,
        compiler_params=pltpu.CompilerParams(dimension_semantics=("arbitrary",),
                                             vmem_limit_bytes=VMEM_LIMIT),
        name="mixer_scans",
    )(*args)


def _outff_kernel(xc_ref, xl_ref, ac_ref, al_ref, bc_ref, bl_ref, mod_ref, n2_ref, fn_ref, wo_ref, w1_ref, w2_ref,
                  yc_ref, yl_ref, *, n_ctx, tiles_per_req, ff_chunk, final_norm):
    D = xc_ref.shape[1]
    DA = ac_ref.shape[1]
    is_ctx, row = _tile_group(n_ctx, tiles_per_req)

    def mod(k):
        return mod_ref[pl.ds(row, 1), k * D:(k + 1) * D]

    x = jnp.where(is_ctx, xc_ref[...], xl_ref[...])
    a = jnp.where(is_ctx, ac_ref[...], al_ref[...])
    b = jnp.where(is_ctx, bc_ref[...], bl_ref[...])
    y = _dot(a, wo_ref[0:DA, :]) + _dot(b, wo_ref[DA:, :])
    x1 = x + mod(2) * y
    h2 = (_rms(x1, n2_ref[...]) * (1.0 + mod(4)) + mod(3)).astype(BF16)
    acc = jnp.zeros(x1.shape, F32)
    for c0 in range(0, w1_ref.shape[1], ff_chunk):
        u = jnp.maximum(_dot(h2, w1_ref[:, c0:c0 + ff_chunk]), 0.0)
        acc = acc + _dot((u * u).astype(BF16), w2_ref[c0:c0 + ff_chunk, :])
    x2 = x1 + mod(5) * acc
    out = _rms(x2, fn_ref[...]) if final_norm else x2

    @pl.when(is_ctx)
    def _():
        yc_ref[...] = out

    @pl.when(jnp.logical_not(is_ctx))
    def _():
        yl_ref[...] = out


def _outff_call(xc2d, xl2d, ac, al, bc, bl, mod, norm2_w, final_w, wo, w1, w2, *, tm, tiles_per_req, final_norm):
    (Mc, D), Ml = xc2d.shape, xl2d.shape[0]
    n_ctx = Mc // tm
    DA = ac.shape[1]
    DFF = w1.shape[1]
    kern = functools.partial(_outff_kernel, n_ctx=n_ctx, tiles_per_req=tiles_per_req, ff_chunk=512,
                             final_norm=final_norm)
    once = pl.Buffered(1)
    ctx, lat = _ctx_tile(n_ctx), _lat_tile(n_ctx)
    return pl.pallas_call(
        kern,
        grid=((Mc + Ml) // tm,),
        in_specs=[
            pl.BlockSpec((tm, D), ctx), pl.BlockSpec((tm, D), lat),
            pl.BlockSpec((tm, DA), ctx), pl.BlockSpec((tm, DA), lat),
            pl.BlockSpec((tm, D - DA), ctx), pl.BlockSpec((tm, D - DA), lat),
            pl.BlockSpec(mod.shape, lambda i: (0, 0)),
            pl.BlockSpec((1, D), lambda i: (0, 0)),
            pl.BlockSpec((1, D), lambda i: (0, 0)),
            pl.BlockSpec((D, D), lambda i: (0, 0), pipeline_mode=once),
            pl.BlockSpec((D, DFF), lambda i: (0, 0), pipeline_mode=once),
            pl.BlockSpec((DFF, D), lambda i: (0, 0), pipeline_mode=once),
        ],
        out_specs=[pl.BlockSpec((tm, D), ctx), pl.BlockSpec((tm, D), lat)],
        out_shape=[jax.ShapeDtypeStruct((Mc, D), F32), jax.ShapeDtypeStruct((Ml, D), F32)],
        compiler_params=pltpu.CompilerParams(dimension_semantics=("arbitrary",),
                                             vmem_limit_bytes=VMEM_LIMIT),
        name="outproj_mlp",
    )(xc2d, xl2d, ac, al, bc, bl, mod, norm2_w.reshape(1, D), final_w.reshape(1, D), wo, w1, w2)


def _layer(xc, xl, mod, cached, lw, ffw, final_w, final_norm):
    (Bc, Tc, D), (Bl, Tl, _) = xc.shape, xl.shape
    tm = TOKEN_TILE
    assert (Bc * Tc) % tm == 0 and Tl % tm == 0 and (Bc * Tc) % Tl == 0
    xc2d, xl2d = xc.reshape(Bc * Tc, D), xl.reshape(Bl * Tl, D)
    z = _inproj_call(xc2d, xl2d, mod, lw["norm1_w"], lw["w_in_t"], tm=tm, tiles_per_req=Tl // tm,
                     big_rows=lw["big_rows"], small_rows=lw["small_rows"])
    res_c = _scan_call(z, 0, Bc, Tc, None, lw, grid_w=Tc, write_state=True,
                       casts=((ffw[0], 0), (ffw[1], 1), (ffw[2], 0)))
    res_l = _scan_call(z, Bc * Tc, Bl, Tl, cached, lw, grid_w=GRID_W, write_state=False)
    yc, yl = _outff_call(xc2d, xl2d, res_c[0].reshape(Bc * Tc, -1), res_l[0].reshape(Bl * Tl, -1),
                         res_c[1].reshape(Bc * Tc, -1), res_l[1].reshape(Bl * Tl, -1), mod, lw["norm2_w"],
                         final_w, *res_c[-3:], tm=tm, tiles_per_req=Tl // tm, final_norm=final_norm)
    return yc.reshape(Bc, Tc, D), yl.reshape(Bl, Tl, D), tuple(res_c[2:6])


def _layer_weights(l, norm1_w, norm2_w, w_in, w_alpha2, b_alpha, b_mgate, conv_w, gnorm_a_w, gnorm_b_w):
    hk_a = w_alpha2.shape[-1]
    d_a = gnorm_a_w.shape[-1]
    d_b = gnorm_b_w.shape[-1]
    hk_b = conv_w.shape[-1] // 2
    sizes = (hk_a, hk_a, d_a, d_a, 2 * R_ALPHA, hk_b, hk_b, d_b, d_b, 4 * H_B)
    offs = [0]
    for s in sizes:
        offs.append(offs[-1] + s)
    big_rows = ((offs[0], offs[4] - offs[0]), (offs[5], offs[9] - offs[5]))
    small_rows = ((offs[4], offs[5] - offs[4]), (offs[9], offs[10] - offs[9]))
    assert all(n % LANES == 0 and r % 16 == 0 for r, n in big_rows)
    wal = w_alpha2[l]
    wal_p = jnp.zeros((SMALL_W, 2 * hk_a), F32)
    wal_p = wal_p.at[0:R_ALPHA, 0:hk_a].set(wal[0]).at[R_ALPHA:2 * R_ALPHA, hk_a:].set(wal[1]).astype(BF16)
    bm_row = jnp.zeros((1, SMALL_W), F32).at[0, GATE_LANE0:GATE_LANE0 + 4 * H_B].set(b_mgate[l].reshape(-1))
    return dict(
        norm1_w=norm1_w[l], norm2_w=norm2_w[l], w_in_t=jnp.swapaxes(w_in[l], 0, 1),
        big_rows=big_rows, small_rows=small_rows, wal_p=wal_p,
        bal_p=b_alpha[l].reshape(1, -1), bm_row=bm_row,
        conv9=conv_w[l].reshape(-1, conv_w.shape[-1]),
        gnorm_a_w=gnorm_a_w[l], gnorm_b_w=gnorm_b_w[l],
    )


def kernel(x_prompt, x_sample, c, state_gla, state_mlstm_C, state_mlstm_n, state_mlstm_m, c_ctx, w_ada, b_ada, norm1_w, norm2_w, w_in, w_alpha2, b_alpha, b_mgate, conv_w, gnorm_a_w, gnorm_b_w, w_out, w_ff1, w_ff2, final_norm_w):
    depth = w_in.shape[0]
    D = x_prompt.shape[-1]
    Bp, Tp, _ = x_prompt.shape
    Bs = x_sample.shape[0]
    assert 1 + Bs <= COND_ROWS
    cond = jnp.concatenate([c_ctx[None, :], c, jnp.zeros((COND_ROWS - 1 - Bs, D), F32)], axis=0)
    xp, xs = x_prompt, x_sample
    s_gla, s_c, s_n, s_m = [], [], [], []
    for l in range(depth):
        lw = _layer_weights(l, norm1_w, norm2_w, w_in, w_alpha2, b_alpha, b_mgate, conv_w,
                            gnorm_a_w, gnorm_b_w)
        mod = _ada_call(cond, w_ada[l], b_ada[l])
        cached = (state_gla[:, l], state_mlstm_C[:, l], state_mlstm_n[:, l], state_mlstm_m[:, l])
        xp, xs, ctx = _layer(xp, xs, mod, cached, lw, (w_out[l], w_ff1[l], w_ff2[l]), final_norm_w,
                             l == depth - 1)
        s_gla.append(ctx[0].reshape(Bp, 2, H_A, -1, ctx[0].shape[-1]))
        s_c.append(ctx[1].reshape(Bp, 2, H_B, -1, ctx[1].shape[-1]))
        s_n.append(ctx[2].reshape(Bp, 2, H_B, -1))
        s_m.append(ctx[3])
    dt = x_prompt.dtype
    return (xp, xs, jnp.stack(s_gla, axis=1).astype(dt), jnp.stack(s_c, axis=1).astype(dt),
            jnp.stack(s_n, axis=1).astype(dt), jnp.stack(s_m, axis=1).astype(dt))
```

```python
import functools

import jax
import jax.numpy as jnp
from jax import lax
from jax.experimental import pallas as pl
from jax.experimental.pallas import tpu as pltpu

F32 = jnp.float32
BF16 = jnp.bfloat16

GRID_W = 64
H_A = 4
H_B = 4
R_ALPHA = 16
TAU_GLA = 16.0
CHUNK = 64
EPS = 1e-6
LANES = 128
COND_ROWS = 8
SMALL_W = LANES
GATE_LANE0 = 2 * R_ALPHA
VMEM_LIMIT = 56 * 1024 * 1024
SCAN_UNROLL = 4
TOKEN_TILE = 512


def _sigmoid(x):
    return 1.0 / (1.0 + jnp.exp(-x))


def _silu(x):
    return x * _sigmoid(x)


def _log_sigmoid(x):
    return jnp.minimum(x, 0.0) - jnp.log1p(jnp.exp(-jnp.abs(x)))


def _dot(a, b):
    return jnp.dot(a, b, preferred_element_type=F32)


def _dot_nt(a, b):
    return lax.dot_general(a, b, (((1,), (1,)), ((), ())), preferred_element_type=F32)


def _rms(x, w):
    return x * lax.rsqrt(jnp.mean(x * x, axis=-1, keepdims=True) + EPS) * w


def _tri_sum(tri, x):
    hi = x.astype(BF16)
    r1 = x - hi.astype(F32)
    mid = r1.astype(BF16)
    lo = (r1 - mid.astype(F32)).astype(BF16)
    return _dot(tri, hi) + _dot(tri, mid) + _dot(tri, lo)


def _chunk_masks(L):
    row = lax.broadcasted_iota(jnp.int32, (L, L), 0)
    col = lax.broadcasted_iota(jnp.int32, (L, L), 1)
    lower = row >= col
    upper = row <= col
    return lower, upper


def _ada_kernel(c_ref, w_ref, b_ref, o_ref):
    s = _silu(c_ref[...])
    o_ref[...] = _dot(s.astype(BF16), w_ref[...].astype(BF16)) + b_ref[...]


def _ada_call(cond, w_ada, b_ada):
    D = cond.shape[1]
    n_out = w_ada.shape[1]
    tn = 2048 if n_out % 2048 == 0 else 1024
    return pl.pallas_call(
        _ada_kernel,
        grid=(n_out // tn,),
        in_specs=[
            pl.BlockSpec((COND_ROWS, D), lambda j: (0, 0)),
            pl.BlockSpec((D, tn), lambda j: (0, j)),
            pl.BlockSpec((1, tn), lambda j: (0, j)),
        ],
        out_specs=pl.BlockSpec((COND_ROWS, tn), lambda j: (0, j)),
        out_shape=jax.ShapeDtypeStruct((COND_ROWS, n_out), F32),
        compiler_params=pltpu.CompilerParams(dimension_semantics=("arbitrary",),
                                             vmem_limit_bytes=VMEM_LIMIT),
        name="ada_mod",
    )(cond, w_ada, b_ada.reshape(1, n_out))


def _tile_group(n_ctx, tiles_per_req):
    i = pl.program_id(0)
    is_ctx = i < n_ctx
    row = jnp.where(is_ctx, 0, 1 + jnp.maximum(i - n_ctx, 0) // tiles_per_req)
    return is_ctx, row


def _ctx_tile(n_ctx):
    return lambda i: (jnp.minimum(i, n_ctx - 1), 0)


def _lat_tile(n_ctx):
    return lambda i: (jnp.maximum(i - n_ctx, 0), 0)


def _inproj_kernel(xc_ref, xl_ref, mod_ref, nw_ref, wt_ref, z_ref, wb_scr, *, n_ctx, tiles_per_req, big_rows,
                   small_rows):
    D = xc_ref.shape[1]

    @pl.when(pl.program_id(0) == 0)
    def _():
        col = 0
        for r0, n in big_rows:
            for k in range(n // LANES):
                blk = wt_ref[r0 + k * LANES:r0 + (k + 1) * LANES, :]
                wb_scr[:, col:col + LANES] = blk.T.astype(BF16)
                col += LANES
        parts = [wt_ref[r0:r0 + n, :] for r0, n in small_rows]
        n_small = sum(n for _, n in small_rows)
        parts.append(jnp.zeros((SMALL_W - n_small, D), F32))
        wb_scr[:, col:col + SMALL_W] = jnp.concatenate(parts, axis=0).T.astype(BF16)

    is_ctx, row = _tile_group(n_ctx, tiles_per_req)

    def tile(x_ref):
        sh1 = mod_ref[pl.ds(row, 1), 0:D]
        sc1 = mod_ref[pl.ds(row, 1), D:2 * D]
        h = _rms(x_ref[...], nw_ref[...]) * (1.0 + sc1) + sh1
        z_ref[...] = _dot(h.astype(BF16), wb_scr[...])

    @pl.when(is_ctx)
    def _():
        tile(xc_ref)

    @pl.when(jnp.logical_not(is_ctx))
    def _():
        tile(xl_ref)


def _inproj_call(xc2d, xl2d, mod, norm_w, w_in_t, *, tm, tiles_per_req, big_rows, small_rows):
    (Mc, D), Ml = xc2d.shape, xl2d.shape[0]
    n_ctx = Mc // tm
    n_out = sum(n for _, n in big_rows) + SMALL_W
    kern = functools.partial(_inproj_kernel, n_ctx=n_ctx, tiles_per_req=tiles_per_req,
                             big_rows=big_rows, small_rows=small_rows)
    return pl.pallas_call(
        kern,
        grid=((Mc + Ml) // tm,),
        in_specs=[
            pl.BlockSpec((tm, D), _ctx_tile(n_ctx)),
            pl.BlockSpec((tm, D), _lat_tile(n_ctx)),
            pl.BlockSpec(mod.shape, lambda i: (0, 0)),
            pl.BlockSpec((1, D), lambda i: (0, 0)),
            pl.BlockSpec(w_in_t.shape, lambda i: (0, 0), pipeline_mode=pl.Buffered(1)),
        ],
        out_specs=pl.BlockSpec((tm, n_out), lambda i: (i, 0)),
        out_shape=jax.ShapeDtypeStruct((Mc + Ml, n_out), F32),
        scratch_shapes=[pltpu.VMEM((D, n_out), BF16)],
        compiler_params=pltpu.CompilerParams(dimension_semantics=("arbitrary",),
                                             vmem_limit_bytes=VMEM_LIMIT),
        name="norm_inproj",
    )(xc2d, xl2d, mod, norm_w.reshape(1, D), w_in_t)


def _chunk_loop(n_chunks, unroll, fn):
    if unroll >= n_chunks:
        fn(list(range(n_chunks)))
        return

    def body(i, carry):
        fn([i * unroll + u for u in range(unroll)])
        return carry

    lax.fori_loop(0, n_chunks // unroll, body, 0)


def _chunk_rows(n):
    if isinstance(n, int):
        return pl.ds(n * CHUNK, CHUNK)
    return pl.ds(pl.multiple_of(n * CHUNK, CHUNK), CHUNK)


def _cast_specs(casts, n_steps):
    in_specs, out_specs, out_shape, args = [], [], [], []
    for w, axis in casts:
        blk = list(w.shape)
        assert blk[axis] % n_steps == 0
        blk[axis] //= n_steps
        assert blk[0] % 16 == 0 and blk[1] % LANES == 0
        idx = (lambda b: (b, 0)) if axis == 0 else (lambda b: (0, b))
        in_specs.append(pl.BlockSpec(tuple(blk), idx))
        out_specs.append(pl.BlockSpec(tuple(blk), idx))
        out_shape.append(jax.ShapeDtypeStruct(w.shape, BF16))
        args.append(w)
    return in_specs, out_specs, out_shape, args


def _gla_body(q_ref, k_ref, v_ref, g_ref, sm_ref, s0_ref, wal_ref, bal_ref, gw_ref, out_ref, snew_ref,
              of_scr, ob_scr, st_scr, sall_scr, qh_scr, qs_scr, kh_scr, *, unroll):
    has_state = s0_ref is not None
    write_state = snew_ref is not None
    T = q_ref.shape[0]
    L = CHUNK
    N = T // L
    HK = q_ref.shape[1]
    DK = HK // H_A
    DV = v_ref.shape[1] // H_A
    scale = DK ** -0.5
    n_pairs = HK // LANES

    lower, upper = _chunk_masks(L)
    tri = (lower.astype(BF16), upper.astype(BF16))
    tmask = (lower, upper)
    lane = lax.broadcasted_iota(jnp.int32, (1, LANES), 1)
    head_mask = (lane < DK, lane >= DK)
    o_scr = (of_scr, ob_scr)

    for d in range(2):
        for p in range(n_pairs):
            if has_state:
                st_scr[d, p] = s0_ref[d, p].T
            else:
                st_scr[d, p] = jnp.zeros((LANES, LANES), F32)

    def state_group(ns):
        units = [u for n in ns for u in ((0, n), (1, N - 1 - n))]
        rows = [_chunk_rows(n) for _, n in units]
        pre = [_dot(sm_ref[r, :].astype(BF16), wal_ref[:, d * HK:(d + 1) * HK]) + bal_ref[:, d * HK:(d + 1) * HK]
               for (d, _), r in zip(units, rows)]
        g = [_log_sigmoid(x) * (1.0 / TAU_GLA) for x in pre]
        b = [_tri_sum(tri[d], gi) for (d, _), gi in zip(units, g)]
        ks_all, dec_all = [], []
        for (d, _), r, bi in zip(units, rows, b):
            bend = bi[L - 1:L, :] if d == 0 else bi[0:1, :]
            q = q_ref[r, :] * scale
            ks = (k_ref[r, :] * jnp.exp(bend - bi)).astype(BF16)
            qh_scr[d, r, :] = (q * jnp.exp(bi - bend)).astype(BF16)
            qs_scr[d, r, :] = (q * jnp.exp(bi)).astype(BF16)
            kh_scr[d, r, :] = ks
            ks_all.append(ks)
            dec_all.append(jnp.exp(bend))
        upd_all = [[[_dot(v_ref[r, (2 * p + j) * DV:(2 * p + j + 1) * DV].T.astype(BF16),
                          ks[:, p * LANES:(p + 1) * LANES]) for j in range(2)]
                    for p in range(n_pairs)]
                   for r, ks in zip(rows, ks_all)]
        for (d, n), dec, upd in zip(units, dec_all, upd_all):
            for p in range(n_pairs):
                st = st_scr[d, p]
                sall_scr[d, n, p] = st.astype(BF16)
                st_scr[d, p] = (st * dec[:, p * LANES:(p + 1) * LANES]
                                + jnp.where(head_mask[0], upd[p][0], upd[p][1]))

    _chunk_loop(N, unroll, state_group)

    def out_group(ns):
        units = [(d, n, p, j) for n in ns for d in range(2) for p in range(n_pairs) for j in range(2)]
        scores, inter = [], []
        for d, n, p, j in units:
            r = _chunk_rows(n)
            ls = slice(p * LANES, (p + 1) * LANES)
            qh = qh_scr[d, r, ls]
            qs = qs_scr[d, r, ls]
            qm = jnp.where(head_mask[j], qh, jnp.zeros_like(qh))
            qsm = jnp.where(head_mask[j], qs, jnp.zeros_like(qs))
            scores.append(_dot_nt(qm, kh_scr[d, r, ls]))
            inter.append(_dot_nt(qsm, sall_scr[d, n, p]))
        probs = [jnp.where(tmask[d], a, 0.0).astype(BF16) for (d, _, _, _), a in zip(units, scores)]
        for (d, n, p, j), a, it in zip(units, probs, inter):
            vs = slice((2 * p + j) * DV, (2 * p + j + 1) * DV)
            r = _chunk_rows(n)
            o_scr[d][r, vs] = _dot(a, v_ref[r, vs].astype(BF16)) + it

    _chunk_loop(N, unroll, out_group)

    def epilogue(i, carry):
        rows = pl.ds(pl.multiple_of(i * L, L), L)
        for h in range(H_A):
            vs = slice(h * DV, (h + 1) * DV)
            o = of_scr[rows, vs] + ob_scr[rows, vs]
            out_ref[rows, vs] = (_rms(o, gw_ref[:, vs]) * _silu(g_ref[rows, vs])).astype(out_ref.dtype)
        return carry

    lax.fori_loop(0, N, epilogue, 0)

    if write_state:
        for d in range(2):
            for p in range(n_pairs):
                snew_ref[d, p] = st_scr[d, p].T


def _gla_scratch(T, HK, DA):
    n_pairs = HK // LANES
    n_chunks = T // CHUNK
    return [
        pltpu.VMEM((T, DA), F32),
        pltpu.VMEM((T, DA), F32),
        pltpu.VMEM((2, n_pairs, LANES, LANES), F32),
        pltpu.VMEM((2, n_chunks, n_pairs, LANES, LANES), BF16),
        pltpu.VMEM((2, T, HK), BF16),
        pltpu.VMEM((2, T, HK), BF16),
        pltpu.VMEM((2, T, HK), BF16),
    ]


def _mlstm_body(qk_ref, v_ref, og_ref, sm_ref, c0_ref, n0_ref, m0_ref, cw_ref, bm_ref, gw_ref,
                out_ref, cnew_ref, nnew_ref, mnew_ref,
                pad_scr, qk_scr, y_scr, c_scr, n_scr, m_scr, call_scr, nall_scr, mall_scr, g_scr, f_scr,
                *, grid_w, unroll):
    has_state = c0_ref is not None
    write_state = cnew_ref is not None
    T = qk_ref.shape[0]
    L = CHUNK
    N = T // L
    C2 = qk_ref.shape[1]
    HK = C2 // 2
    DK = HK // H_B
    DV = v_ref.shape[1] // H_B
    scale = DK ** -0.5
    n_pairs = HK // LANES
    P = pad_scr.shape[0] - T
    P0 = P // 2
    rows_img = T // grid_w

    lower, upper = _chunk_masks(L)
    tri = (lower.astype(BF16), upper.astype(BF16))
    tmask = (lower, upper)
    lane = lax.broadcasted_iota(jnp.int32, (1, LANES), 1)
    head_mask = (lane < DK, lane >= DK)
    lane_in = lane & (L - 1)

    def lane_cummax(x, d):
        k = 1
        while k < L:
            if d == 0:
                x = jnp.maximum(x, jnp.where(lane_in >= k, pltpu.roll(x, k, axis=1), -jnp.inf))
            else:
                x = jnp.maximum(x, jnp.where(lane_in < L - k, pltpu.roll(x, LANES - k, axis=1), -jnp.inf))
            k *= 2
        return x

    for d in range(2):
        for p in range(n_pairs):
            if has_state:
                c_scr[d, p] = c0_ref[d, p]
                n_scr[2 * d + p:2 * d + p + 1, :] = n0_ref[d, p:p + 1, :]
            else:
                c_scr[d, p] = jnp.zeros((LANES, LANES), F32)
                n_scr[2 * d + p:2 * d + p + 1, :] = jnp.zeros((1, LANES), F32)
    eye_h = (lax.broadcasted_iota(jnp.int32, (H_B, H_B), 0) == lax.broadcasted_iota(jnp.int32, (H_B, H_B), 1))

    def to_col(row):
        return jnp.sum(jnp.where(eye_h, row, 0.0), axis=1, keepdims=True)

    def to_row(col):
        return jnp.sum(jnp.where(eye_h, col, 0.0), axis=0, keepdims=True)

    for d in range(2):
        if has_state:
            m_scr[H_B * d:H_B * (d + 1), 0:1] = to_col(m0_ref[d:d + 1, :])
        else:
            m_scr[H_B * d:H_B * (d + 1), 0:1] = jnp.zeros((H_B, 1), F32)

    pad_scr[0:P0, :] = jnp.zeros((P0, C2), F32)
    pad_scr[P0 + T:P + T, :] = jnp.zeros((P - P0, C2), F32)

    def copy_in(i, carry):
        r0 = pl.multiple_of(i * L, L)
        pad_scr[pl.ds(P0 + r0, L), :] = qk_ref[pl.ds(r0, L), :]
        return carry

    lax.fori_loop(0, N, copy_in, 0)

    lane_c = lax.broadcasted_iota(jnp.int32, (1, C2), 1)
    qscale = jnp.where(lane_c < HK, scale, 1.0).astype(F32)
    sub = lax.broadcasted_iota(jnp.int32, (L, 1), 0)
    img_rows = (0,) if rows_img == 1 else (-1, 0, 1)

    def conv_tile(i, carry):
        r0 = pl.multiple_of(i * L, L)
        col = lax.rem(r0, grid_w) + sub
        ok_left = col >= 1
        ok_right = col <= grid_w - 2
        acc = jnp.zeros((L, C2), F32)
        for di in img_rows:
            blk = pad_scr[pl.ds(P0 + r0 + di * grid_w - 8, L + 16), :]
            left = jnp.where(ok_left, blk[7:7 + L, :], 0.0)
            mid = blk[8:8 + L, :]
            right = jnp.where(ok_right, blk[9:9 + L, :], 0.0)
            wr = 3 * (di + 1)
            acc = acc + left * cw_ref[wr:wr + 1, :] + mid * cw_ref[wr + 1:wr + 2, :] + right * cw_ref[wr + 2:wr + 3, :]
        qk_scr[pl.ds(r0, L), :] = _silu(acc) * qscale
        return carry

    lax.fori_loop(0, N, conv_tile, 0)

    gl = lane - GATE_LANE0
    is_f = ((gl >= H_B) & (gl < 2 * H_B)) | ((gl >= 3 * H_B) & (gl < 4 * H_B))

    def gate_tile(i, carry):
        rows = pl.ds(pl.multiple_of(i * L, L), L)
        x = sm_ref[rows, :] + bm_ref[...]
        y_scr[rows, :] = jnp.where(is_f, _log_sigmoid(x), x)
        return carry

    lax.fori_loop(0, N, gate_tile, 0)


    def state_group(ns):
        units = [u for n in ns for u in ((0, n), (1, N - 1 - n))]
        rows = [_chunk_rows(n) for _, n in units]
        xs = [y_scr[r, :] for r in rows]
        fsum = [_tri_sum(tri[d], x) for (d, _), x in zip(units, xs)]
        wk_all, f_end, c_end = [], [], []
        for (d, n), r, x, fs in zip(units, rows, xs, fsum):
            y = jnp.where(is_f, fs, x)
            li0 = GATE_LANE0 + 2 * H_B * d
            blk = jnp.concatenate([y, y], axis=0).T[li0:li0 + 2 * H_B, :]
            frow = pltpu.roll(blk, H_B, axis=0)
            grow = blk - frow
            g_scr[d, n] = grow
            f_scr[d, n] = frow
            e_col = L - 1 if d == 0 else 0
            f_end.append(frow[0:H_B, e_col:e_col + 1])
            ce = jnp.max(grow[0:H_B, :], axis=1, keepdims=True)
            c_end.append(ce)
            wk_all.append(jnp.exp(grow[0:H_B, 0:L] - ce))
        kv_all, ksum_all = [], []
        for r, wk4 in zip(rows, wk_all):
            kv_u, ks_u = [], []
            for p in range(n_pairs):
                kp = qk_scr[r, HK + p * LANES:HK + (p + 1) * LANES]
                kpb = kp.astype(BF16)
                kt = kp.T
                for j in range(2):
                    h = 2 * p + j
                    wk = wk4[h:h + 1, :]
                    kwt = (kt[j * DK:(j + 1) * DK, :] * wk).astype(BF16)
                    kv_u.append(_dot(kwt, v_ref[r, h * DV:(h + 1) * DV].astype(BF16)))
                    ks_u.append(_dot(jnp.broadcast_to(wk, (8, L)).astype(BF16), kpb)[0:1, :])
            kv_all.append(kv_u)
            ksum_all.append(ks_u)
        for (d, n), fe, ce, kv_u, ks_u in zip(units, f_end, c_end, kv_all, ksum_all):
            m_prev = m_scr[H_B * d:H_B * (d + 1), 0:1]
            mall_scr[d, n, 0:H_B, 0:1] = m_prev
            mx = jnp.maximum(m_prev, ce)
            a_all = jnp.exp(m_prev - mx)
            b_all = jnp.exp(ce - mx)
            m_scr[H_B * d:H_B * (d + 1), 0:1] = fe + mx
            for p in range(n_pairs):
                npair = n_scr[2 * d + p:2 * d + p + 1, :]
                nall_scr[d, n, p:p + 1, :] = npair
                a_s = [a_all[2 * p + j:2 * p + j + 1, :] for j in range(2)]
                b_s = [b_all[2 * p + j:2 * p + j + 1, :] for j in range(2)]
                for j in range(2):
                    hr = slice(j * DK, (j + 1) * DK)
                    cj = c_scr[d, p, hr, :]
                    call_scr[d, n, p, hr, :] = cj.astype(BF16)
                    c_scr[d, p, hr, :] = a_s[j] * cj + b_s[j] * kv_u[2 * p + j]
                n_scr[2 * d + p:2 * d + p + 1, :] = (
                    jnp.where(head_mask[0], a_s[0], a_s[1]) * npair
                    + jnp.where(head_mask[0], b_s[0] * ks_u[2 * p], b_s[1] * ks_u[2 * p + 1]))

    _chunk_loop(N, unroll, state_group)

    eye2 = ((lax.broadcasted_iota(jnp.int32, (L, LANES), 1) & (L - 1))
            == lax.broadcasted_iota(jnp.int32, (L, LANES), 0))
    ones8 = jnp.ones((8, L), BF16)
    sub_h = lax.broadcasted_iota(jnp.int32, (H_B, LANES), 0)

    def head_rows(vals):
        out = vals[0][0:H_B, :]
        for h in range(1, H_B):
            out = jnp.where(sub_h == h, vals[h][0:H_B, :], out)
        return out

    def out_group(ns):
        chunks = [(d, n) for n in ns for d in range(2)]
        units = [(d, n, p, j) for d, n in chunks for p in range(n_pairs) for j in range(2)]
        cms = [lane_cummax(g_scr[d, n], d)[0:H_B, :] for d, n in chunks]
        qks, qcs, qns = [], [], []
        for d, n, p, j in units:
            r = _chunk_rows(n)
            qmb = jnp.where(head_mask[j], qk_scr[r, p * LANES:(p + 1) * LANES], 0.0).astype(BF16)
            qks.append(_dot_nt(qmb, qk_scr[r, HK + p * LANES:HK + (p + 1) * LANES].astype(BF16)))
            qcs.append(_dot(qmb, call_scr[d, n, p]))
            n8 = jnp.broadcast_to(nall_scr[d, n, p:p + 1, :], (8, LANES)).astype(BF16)
            qns.append(_dot_nt(n8, jnp.concatenate([qmb, qmb], axis=0)))
        s_all = []
        for (d, n, p, j), qk in zip(units, qks):
            grow = g_scr[d, n, 2 * p + j:2 * p + j + 1, 0:L]
            e = jnp.where(tmask[d], grow, -jnp.inf)
            cmax = jnp.max(e, axis=-1, keepdims=True)
            s_all.append((qk * jnp.exp(e - cmax)).astype(BF16))
        nums = [_dot(s, v_ref[_chunk_rows(n), (2 * p + j) * DV:(2 * p + j + 1) * DV].astype(BF16))
                for (d, n, p, j), s in zip(units, s_all)]
        dens = [_dot_nt(ones8, jnp.concatenate([s, s], axis=0)) for s in s_all]
        scales = []
        for ci, (d, n) in enumerate(chunks):
            den_loc = head_rows(dens[ci * H_B:(ci + 1) * H_B])
            qn = head_rows(qns[ci * H_B:(ci + 1) * H_B])
            cm = cms[ci]
            m_prev = mall_scr[d, n, 0:H_B, 0:1]
            delta = cm - m_prev
            t = jnp.exp(-jnp.abs(delta))
            w_loc = jnp.where(delta <= 0.0, t, 1.0)
            w_inter = jnp.where(delta <= 0.0, 1.0, t)
            mt = f_scr[d, n, 0:H_B, :] + jnp.maximum(m_prev, cm)
            den = w_loc * den_loc + w_inter * qn
            rinv = 1.0 / jnp.maximum(jnp.abs(den), jnp.exp(-mt))
            scales.append(jnp.where(lane < L, w_loc * rinv, w_inter * rinv))
        hs = []
        for ui, (d, n, p, j) in enumerate(units):
            h = 2 * p + j
            lhs = jnp.where(eye2, scales[ui // H_B][h:h + 1, :], 0.0).astype(BF16)
            rhs = jnp.concatenate([nums[ui].astype(BF16), qcs[ui].astype(BF16)], axis=0)
            hs.append(_dot(lhs, rhs))
        for ni, n in enumerate(ns):
            r = _chunk_rows(n)
            for h in range(H_B):
                vs = slice(h * DV, (h + 1) * DV)
                o = hs[(2 * ni) * H_B + h] + hs[(2 * ni + 1) * H_B + h]
                out_ref[r, vs] = (_rms(o, gw_ref[:, vs]) * _sigmoid(og_ref[r, vs])).astype(out_ref.dtype)

    _chunk_loop(N, unroll, out_group)

    if write_state:
        for d in range(2):
            for p in range(n_pairs):
                cnew_ref[d, p] = c_scr[d, p]
                nnew_ref[d, p:p + 1, :] = n_scr[2 * d + p:2 * d + p + 1, :]
            mnew_ref[d:d + 1, :] = to_row(m_scr[H_B * d:H_B * (d + 1), 0:1])


def _mlstm_scratch(T, C2, grid_w):
    n_pairs = C2 // 2 // LANES
    n_chunks = T // CHUNK
    pad_rows = 2 * (grid_w + 8) if T // grid_w > 1 else 16
    return [
        pltpu.VMEM((T + pad_rows, C2), F32),
        pltpu.VMEM((T, C2), F32),
        pltpu.VMEM((T, SMALL_W), F32),
        pltpu.VMEM((2, n_pairs, LANES, LANES), F32),
        pltpu.VMEM((8, LANES), F32),
        pltpu.VMEM((8, LANES), F32),
        pltpu.VMEM((2, n_chunks, n_pairs, LANES, LANES), BF16),
        pltpu.VMEM((2, n_chunks, 8, LANES), F32),
        pltpu.VMEM((2, n_chunks, 8, LANES), F32),
        pltpu.VMEM((2, n_chunks, 8, LANES), F32),
        pltpu.VMEM((2, n_chunks, 8, LANES), F32),
    ]


N_GLA_SCRATCH = 7
N_MLSTM_SCRATCH = 11


def _scan_kernel(*refs, cols, has_state, write_state, n_cast, grid_w, unroll):
    refs = list(refs)
    z_ref = refs.pop(0)
    s0_ref = c0_ref = n0_ref = m0_ref = None
    if has_state:
        s0_ref, c0_ref, n0_ref, m0_ref = refs[:4]
        del refs[:4]
    wal_ref, bal_ref, gwa_ref, cw_ref, bm_ref, gwb_ref = refs[:6]
    del refs[:6]
    cast_in = refs[:n_cast]
    del refs[:n_cast]
    outa_ref, outb_ref = refs[:2]
    del refs[:2]
    snew_ref = cnew_ref = nnew_ref = mnew_ref = None
    if write_state:
        snew_ref, cnew_ref, nnew_ref, mnew_ref = refs[:4]
        del refs[:4]
    cast_out = refs[:n_cast]
    del refs[:n_cast]
    gla_scr = refs[:N_GLA_SCRATCH]
    mlstm_scr = refs[N_GLA_SCRATCH:]

    for src, dst in zip(cast_in, cast_out):
        dst[...] = src[...].astype(BF16)

    def view(name):
        c0, w = cols[name]
        return z_ref.at[:, pl.ds(c0, w)]

    sm_ref = view("small")
    _gla_body(view("qa"), view("ka"), view("va"), view("ga"), sm_ref, s0_ref, wal_ref, bal_ref, gwa_ref,
              outa_ref, snew_ref, *gla_scr, unroll=unroll)
    _mlstm_body(view("qkb"), view("vb"), view("ob"), sm_ref, c0_ref, n0_ref, m0_ref, cw_ref, bm_ref, gwb_ref,
                outb_ref, cnew_ref, nnew_ref, mnew_ref, *mlstm_scr, grid_w=grid_w, unroll=unroll)


def _scan_call(z2d, row0, B, T, states, lw, *, grid_w, write_state, casts=()):
    n_z = z2d.shape[1]
    assert row0 % T == 0 and z2d.shape[0] % T == 0
    z3 = z2d.reshape(z2d.shape[0] // T, T, n_z)
    blk0 = row0 // T
    HK = lw["wal_p"].shape[1] // 2
    DA = lw["gnorm_a_w"].shape[0]
    C2 = lw["conv9"].shape[1]
    DB = lw["gnorm_b_w"].shape[0]
    pa, pb = HK // LANES, C2 // 2 // LANES
    n_chunks = T // CHUNK
    has_state = states is not None
    widths = (("qa", HK), ("ka", HK), ("va", DA), ("ga", DA), ("qkb", C2), ("vb", DB), ("ob", DB),
              ("small", SMALL_W))
    cols, c0 = {}, 0
    for name, w in widths:
        cols[name] = (c0, w)
        c0 += w
    assert c0 == n_z
    cast_in_specs, cast_out_specs, cast_out_shape, cast_args = _cast_specs(casts, B)
    kern = functools.partial(_scan_kernel, cols=cols, has_state=has_state, write_state=write_state,
                             n_cast=len(casts), grid_w=grid_w, unroll=min(n_chunks, SCAN_UNROLL))

    def per_batch(shape):
        nd = len(shape)
        return pl.BlockSpec((None,) + tuple(shape), lambda b: (b,) + (0,) * nd)

    def whole(a):
        return pl.BlockSpec(a.shape, lambda b: (0,) * a.ndim)

    state_shapes = ((2, pa, LANES, LANES), (2, pb, LANES, LANES), (2, pb, LANES), (2, H_B))
    in_specs = [pl.BlockSpec((None, T, n_z), lambda b: (b + blk0, 0, 0))]
    args = [z3]
    if has_state:
        s_gla, s_c, s_n, s_m = states
        args += [s_gla.reshape((B,) + state_shapes[0]), s_c.reshape((B,) + state_shapes[1]),
                 s_n.reshape((B,) + state_shapes[2]), s_m]
        in_specs += [per_batch(s) for s in state_shapes]
    small = [lw["wal_p"], lw["bal_p"], lw["gnorm_a_w"].reshape(1, DA), lw["conv9"], lw["bm_row"],
             lw["gnorm_b_w"].reshape(1, DB)]
    args += small + cast_args
    in_specs += [whole(a) for a in small] + cast_in_specs
    out_specs = [per_batch((T, DA)), per_batch((T, DB))]
    out_shape = [jax.ShapeDtypeStruct((B, T, DA), BF16), jax.ShapeDtypeStruct((B, T, DB), BF16)]
    if write_state:
        out_specs += [per_batch(s) for s in state_shapes]
        out_shape += [jax.ShapeDtypeStruct((B,) + s, F32) for s in state_shapes]
    out_specs += cast_out_specs
    out_shape += cast_out_shape
    scratch = _gla_scratch(T, HK, DA) + _mlstm_scratch(T, C2, grid_w)
    assert len(scratch) == N_GLA_SCRATCH + N_MLSTM_SCRATCH
    return pl.pallas_call(
        kern,
        grid=(B,),
        in_specs=in_specs,
        out_specs=out_specs,
        out_shape=out_shape,
        scratch_shapes=scratch,
        compiler_params=pltpu.CompilerParams(dimension_semantics=("arbitrary",),
                                             vmem_limit_bytes=VMEM_LIMIT),
        name="mixer_scans",
    )(*args)


def _outff_kernel(xc_ref, xl_ref, ac_ref, al_ref, bc_ref, bl_ref, mod_ref, n2_ref, fn_ref, wo_ref, w1_ref, w2_ref,
                  yc_ref, yl_ref, *, n_ctx, tiles_per_req, ff_chunk, final_norm):
    D = xc_ref.shape[1]
    DA = ac_ref.shape[1]
    is_ctx, row = _tile_group(n_ctx, tiles_per_req)

    def mod(k):
        return mod_ref[pl.ds(row, 1), k * D:(k + 1) * D]

    def tile(x_ref, a_ref, b_ref, y_ref):
        y = _dot(a_ref[...], wo_ref[0:DA, :]) + _dot(b_ref[...], wo_ref[DA:, :])
        x1 = x_ref[...] + mod(2) * y
        h2 = (_rms(x1, n2_ref[...]) * (1.0 + mod(4)) + mod(3)).astype(BF16)
        acc = jnp.zeros(x1.shape, F32)
        for c0 in range(0, w1_ref.shape[1], ff_chunk):
            u = jnp.maximum(_dot(h2, w1_ref[:, c0:c0 + ff_chunk]), 0.0)
            acc = acc + _dot((u * u).astype(BF16), w2_ref[c0:c0 + ff_chunk, :])
        x2 = x1 + mod(5) * acc
        y_ref[...] = _rms(x2, fn_ref[...]) if final_norm else x2

    @pl.when(is_ctx)
    def _():
        tile(xc_ref, ac_ref, bc_ref, yc_ref)

    @pl.when(jnp.logical_not(is_ctx))
    def _():
        tile(xl_ref, al_ref, bl_ref, yl_ref)


def _outff_call(xc2d, xl2d, ac, al, bc, bl, mod, norm2_w, final_w, wo, w1, w2, *, tm, tiles_per_req, final_norm):
    (Mc, D), Ml = xc2d.shape, xl2d.shape[0]
    n_ctx = Mc // tm
    DA = ac.shape[1]
    DFF = w1.shape[1]
    kern = functools.partial(_outff_kernel, n_ctx=n_ctx, tiles_per_req=tiles_per_req, ff_chunk=512,
                             final_norm=final_norm)
    once = pl.Buffered(1)
    ctx, lat = _ctx_tile(n_ctx), _lat_tile(n_ctx)
    return pl.pallas_call(
        kern,
        grid=((Mc + Ml) // tm,),
        in_specs=[
            pl.BlockSpec((tm, D), ctx), pl.BlockSpec((tm, D), lat),
            pl.BlockSpec((tm, DA), ctx), pl.BlockSpec((tm, DA), lat),
            pl.BlockSpec((tm, D - DA), ctx), pl.BlockSpec((tm, D - DA), lat),
            pl.BlockSpec(mod.shape, lambda i: (0, 0)),
            pl.BlockSpec((1, D), lambda i: (0, 0)),
            pl.BlockSpec((1, D), lambda i: (0, 0)),
            pl.BlockSpec((D, D), lambda i: (0, 0), pipeline_mode=once),
            pl.BlockSpec((D, DFF), lambda i: (0, 0), pipeline_mode=once),
            pl.BlockSpec((DFF, D), lambda i: (0, 0), pipeline_mode=once),
        ],
        out_specs=[pl.BlockSpec((tm, D), ctx), pl.BlockSpec((tm, D), lat)],
        out_shape=[jax.ShapeDtypeStruct((Mc, D), F32), jax.ShapeDtypeStruct((Ml, D), F32)],
        compiler_params=pltpu.CompilerParams(dimension_semantics=("arbitrary",),
                                             vmem_limit_bytes=VMEM_LIMIT),
        name="outproj_mlp",
    )(xc2d, xl2d, ac, al, bc, bl, mod, norm2_w.reshape(1, D), final_w.reshape(1, D), wo, w1, w2)


def _layer(xc, xl, mod, cached, lw, ffw, final_w, final_norm):
    (Bc, Tc, D), (Bl, Tl, _) = xc.shape, xl.shape
    tm = TOKEN_TILE
    assert (Bc * Tc) % tm == 0 and Tl % tm == 0 and (Bc * Tc) % Tl == 0
    xc2d, xl2d = xc.reshape(Bc * Tc, D), xl.reshape(Bl * Tl, D)
    z = _inproj_call(xc2d, xl2d, mod, lw["norm1_w"], lw["w_in_t"], tm=tm, tiles_per_req=Tl // tm,
                     big_rows=lw["big_rows"], small_rows=lw["small_rows"])
    res_c = _scan_call(z, 0, Bc, Tc, None, lw, grid_w=Tc, write_state=True,
                       casts=((ffw[0], 0), (ffw[1], 1), (ffw[2], 0)))
    res_l = _scan_call(z, Bc * Tc, Bl, Tl, cached, lw, grid_w=GRID_W, write_state=False)
    yc, yl = _outff_call(xc2d, xl2d, res_c[0].reshape(Bc * Tc, -1), res_l[0].reshape(Bl * Tl, -1),
                         res_c[1].reshape(Bc * Tc, -1), res_l[1].reshape(Bl * Tl, -1), mod, lw["norm2_w"],
                         final_w, *res_c[-3:], tm=tm, tiles_per_req=Tl // tm, final_norm=final_norm)
    return yc.reshape(Bc, Tc, D), yl.reshape(Bl, Tl, D), tuple(res_c[2:6])


def _layer_weights(l, norm1_w, norm2_w, w_in, w_alpha2, b_alpha, b_mgate, conv_w, gnorm_a_w, gnorm_b_w):
    hk_a = w_alpha2.shape[-1]
    d_a = gnorm_a_w.shape[-1]
    d_b = gnorm_b_w.shape[-1]
    hk_b = conv_w.shape[-1] // 2
    sizes = (hk_a, hk_a, d_a, d_a, 2 * R_ALPHA, hk_b, hk_b, d_b, d_b, 4 * H_B)
    offs = [0]
    for s in sizes:
        offs.append(offs[-1] + s)
    big_rows = ((offs[0], offs[4] - offs[0]), (offs[5], offs[9] - offs[5]))
    small_rows = ((offs[4], offs[5] - offs[4]), (offs[9], offs[10] - offs[9]))
    assert all(n % LANES == 0 and r % 16 == 0 for r, n in big_rows)
    wal = w_alpha2[l]
    wal_p = jnp.zeros((SMALL_W, 2 * hk_a), F32)
    wal_p = wal_p.at[0:R_ALPHA, 0:hk_a].set(wal[0]).at[R_ALPHA:2 * R_ALPHA, hk_a:].set(wal[1]).astype(BF16)
    bm_row = jnp.zeros((1, SMALL_W), F32).at[0, GATE_LANE0:GATE_LANE0 + 4 * H_B].set(b_mgate[l].reshape(-1))
    return dict(
        norm1_w=norm1_w[l], norm2_w=norm2_w[l], w_in_t=jnp.swapaxes(w_in[l], 0, 1),
        big_rows=big_rows, small_rows=small_rows, wal_p=wal_p,
        bal_p=b_alpha[l].reshape(1, -1), bm_row=bm_row,
        conv9=conv_w[l].reshape(-1, conv_w.shape[-1]),
        gnorm_a_w=gnorm_a_w[l], gnorm_b_w=gnorm_b_w[l],
    )


def kernel(x_prompt, x_sample, c, state_gla, state_mlstm_C, state_mlstm_n, state_mlstm_m, c_ctx, w_ada, b_ada, norm1_w, norm2_w, w_in, w_alpha2, b_alpha, b_mgate, conv_w, gnorm_a_w, gnorm_b_w, w_out, w_ff1, w_ff2, final_norm_w):
    depth = w_in.shape[0]
    D = x_prompt.shape[-1]
    Bp, Tp, _ = x_prompt.shape
    Bs = x_sample.shape[0]
    assert 1 + Bs <= COND_ROWS
    cond = jnp.concatenate([c_ctx[None, :], c, jnp.zeros((COND_ROWS - 1 - Bs, D), F32)], axis=0)
    xp, xs = x_prompt, x_sample
    s_gla, s_c, s_n, s_m = [], [], [], []
    for l in range(depth):
        lw = _layer_weights(l, norm1_w, norm2_w, w_in, w_alpha2, b_alpha, b_mgate, conv_w,
                            gnorm_a_w, gnorm_b_w)
        mod = _ada_call(cond, w_ada[l], b_ada[l])
        cached = (state_gla[:, l], state_mlstm_C[:, l], state_mlstm_n[:, l], state_mlstm_m[:, l])
        xp, xs, ctx = _layer(xp, xs, mod, cached, lw, (w_out[l], w_ff1[l], w_ff2[l]), final_norm_w,
                             l == depth - 1)
        s_gla.append(ctx[0].reshape(Bp, 2, H_A, -1, ctx[0].shape[-1]))
        s_c.append(ctx[1].reshape(Bp, 2, H_B, -1, ctx[1].shape[-1]))
        s_n.append(ctx[2].reshape(Bp, 2, H_B, -1))
        s_m.append(ctx[3])
    dt = x_prompt.dtype
    return (xp, xs, jnp.stack(s_gla, axis=1).astype(dt), jnp.stack(s_c, axis=1).astype(dt),
            jnp.stack(s_n, axis=1).astype(dt), jnp.stack(s_m, axis=1).astype(dt))
```

```python
import functools

import jax
import jax.numpy as jnp
from jax import lax
from jax.experimental import pallas as pl
from jax.experimental.pallas import tpu as pltpu

F32 = jnp.float32
BF16 = jnp.bfloat16

GRID_W = 64
H_A = 4
H_B = 4
R_ALPHA = 16
TAU_GLA = 16.0
CHUNK = 64
EPS = 1e-6
LANES = 128
COND_ROWS = 8
SMALL_W = LANES
GATE_LANE0 = 2 * R_ALPHA
VMEM_LIMIT = 56 * 1024 * 1024
SCAN_UNROLL = 4
TOKEN_TILE = 512


def _sigmoid(x):
    return 1.0 / (1.0 + jnp.exp(-x))


def _silu(x):
    return x * _sigmoid(x)


def _log_sigmoid(x):
    return jnp.minimum(x, 0.0) - jnp.log1p(jnp.exp(-jnp.abs(x)))


def _dot(a, b):
    return jnp.dot(a, b, preferred_element_type=F32)


def _dot_nt(a, b):
    return lax.dot_general(a, b, (((1,), (1,)), ((), ())), preferred_element_type=F32)


def _rms(x, w):
    return x * lax.rsqrt(jnp.mean(x * x, axis=-1, keepdims=True) + EPS) * w


def _tri_sum(tri, x):
    hi = x.astype(BF16)
    r1 = x - hi.astype(F32)
    mid = r1.astype(BF16)
    lo = (r1 - mid.astype(F32)).astype(BF16)
    return _dot(tri, hi) + _dot(tri, mid) + _dot(tri, lo)


def _chunk_masks(L):
    row = lax.broadcasted_iota(jnp.int32, (L, L), 0)
    col = lax.broadcasted_iota(jnp.int32, (L, L), 1)
    lower = row >= col
    upper = row <= col
    return lower, upper


def _ada_kernel(c_ref, w_ref, b_ref, o_ref):
    s = _silu(c_ref[...])
    o_ref[...] = _dot(s.astype(BF16), w_ref[...].astype(BF16)) + b_ref[...]


def _ada_call(cond, w_ada, b_ada):
    D = cond.shape[1]
    n_out = w_ada.shape[1]
    tn = 2048 if n_out % 2048 == 0 else 1024
    return pl.pallas_call(
        _ada_kernel,
        grid=(n_out // tn,),
        in_specs=[
            pl.BlockSpec((COND_ROWS, D), lambda j: (0, 0)),
            pl.BlockSpec((D, tn), lambda j: (0, j)),
            pl.BlockSpec((1, tn), lambda j: (0, j)),
        ],
        out_specs=pl.BlockSpec((COND_ROWS, tn), lambda j: (0, j)),
        out_shape=jax.ShapeDtypeStruct((COND_ROWS, n_out), F32),
        compiler_params=pltpu.CompilerParams(dimension_semantics=("arbitrary",),
                                             vmem_limit_bytes=VMEM_LIMIT),
        name="ada_mod",
    )(cond, w_ada, b_ada.reshape(1, n_out))


def _tile_group(n_ctx, tiles_per_req):
    i = pl.program_id(0)
    is_ctx = i < n_ctx
    row = jnp.where(is_ctx, 0, 1 + jnp.maximum(i - n_ctx, 0) // tiles_per_req)
    return is_ctx, row


def _ctx_tile(n_ctx):
    return lambda i: (jnp.minimum(i, n_ctx - 1), 0)


def _lat_tile(n_ctx):
    return lambda i: (jnp.maximum(i - n_ctx, 0), 0)


def _inproj_kernel(xc_ref, xl_ref, mod_ref, nw_ref, wt_ref, z_ref, wb_scr, *, n_ctx, tiles_per_req, big_rows,
                   small_rows):
    D = xc_ref.shape[1]

    @pl.when(pl.program_id(0) == 0)
    def _():
        col = 0
        for r0, n in big_rows:
            for k in range(n // LANES):
                blk = wt_ref[r0 + k * LANES:r0 + (k + 1) * LANES, :]
                wb_scr[:, col:col + LANES] = blk.T.astype(BF16)
                col += LANES
        parts = [wt_ref[r0:r0 + n, :] for r0, n in small_rows]
        n_small = sum(n for _, n in small_rows)
        parts.append(jnp.zeros((SMALL_W - n_small, D), F32))
        wb_scr[:, col:col + SMALL_W] = jnp.concatenate(parts, axis=0).T.astype(BF16)

    is_ctx, row = _tile_group(n_ctx, tiles_per_req)

    def tile(x_ref):
        sh1 = mod_ref[pl.ds(row, 1), 0:D]
        sc1 = mod_ref[pl.ds(row, 1), D:2 * D]
        h = _rms(x_ref[...], nw_ref[...]) * (1.0 + sc1) + sh1
        z_ref[...] = _dot(h.astype(BF16), wb_scr[...])

    @pl.when(is_ctx)
    def _():
        tile(xc_ref)

    @pl.when(jnp.logical_not(is_ctx))
    def _():
        tile(xl_ref)


def _inproj_call(xc2d, xl2d, mod, norm_w, w_in_t, *, tm, tiles_per_req, big_rows, small_rows):
    (Mc, D), Ml = xc2d.shape, xl2d.shape[0]
    n_ctx = Mc // tm
    n_out = sum(n for _, n in big_rows) + SMALL_W
    kern = functools.partial(_inproj_kernel, n_ctx=n_ctx, tiles_per_req=tiles_per_req,
                             big_rows=big_rows, small_rows=small_rows)
    return pl.pallas_call(
        kern,
        grid=((Mc + Ml) // tm,),
        in_specs=[
            pl.BlockSpec((tm, D), _ctx_tile(n_ctx)),
            pl.BlockSpec((tm, D), _lat_tile(n_ctx)),
            pl.BlockSpec(mod.shape, lambda i: (0, 0)),
            pl.BlockSpec((1, D), lambda i: (0, 0)),
            pl.BlockSpec(w_in_t.shape, lambda i: (0, 0), pipeline_mode=pl.Buffered(1)),
        ],
        out_specs=pl.BlockSpec((tm, n_out), lambda i: (i, 0)),
        out_shape=jax.ShapeDtypeStruct((Mc + Ml, n_out), F32),
        scratch_shapes=[pltpu.VMEM((D, n_out), BF16)],
        compiler_params=pltpu.CompilerParams(dimension_semantics=("arbitrary",),
                                             vmem_limit_bytes=VMEM_LIMIT),
        name="norm_inproj",
    )(xc2d, xl2d, mod, norm_w.reshape(1, D), w_in_t)


def _chunk_loop(n_chunks, unroll, fn):
    if unroll >= n_chunks:
        fn(list(range(n_chunks)))
        return

    def body(i, carry):
        fn([i * unroll + u for u in range(unroll)])
        return carry

    lax.fori_loop(0, n_chunks // unroll, body, 0)


def _chunk_rows(n):
    if isinstance(n, int):
        return pl.ds(n * CHUNK, CHUNK)
    return pl.ds(pl.multiple_of(n * CHUNK, CHUNK), CHUNK)


def _cast_specs(casts, n_steps):
    in_specs, out_specs, out_shape, args = [], [], [], []
    for w, axis in casts:
        blk = list(w.shape)
        assert blk[axis] % n_steps == 0
        blk[axis] //= n_steps
        assert blk[0] % 16 == 0 and blk[1] % LANES == 0
        idx = (lambda b: (b, 0)) if axis == 0 else (lambda b: (0, b))
        in_specs.append(pl.BlockSpec(tuple(blk), idx))
        out_specs.append(pl.BlockSpec(tuple(blk), idx))
        out_shape.append(jax.ShapeDtypeStruct(w.shape, BF16))
        args.append(w)
    return in_specs, out_specs, out_shape, args


def _gla_body(q_ref, k_ref, v_ref, g_ref, sm_ref, s0_ref, wal_ref, bal_ref, gw_ref, out_ref, snew_ref,
              of_scr, ob_scr, st_scr, sall_scr, qh_scr, qs_scr, kh_scr, *, unroll):
    has_state = s0_ref is not None
    write_state = snew_ref is not None
    T = q_ref.shape[0]
    L = CHUNK
    N = T // L
    HK = q_ref.shape[1]
    DK = HK // H_A
    DV = v_ref.shape[1] // H_A
    scale = DK ** -0.5
    n_pairs = HK // LANES

    lower, upper = _chunk_masks(L)
    tri = (lower.astype(BF16), upper.astype(BF16))
    tmask = (lower, upper)
    lane = lax.broadcasted_iota(jnp.int32, (1, LANES), 1)
    head_mask = (lane < DK, lane >= DK)
    o_scr = (of_scr, ob_scr)

    for d in range(2):
        for p in range(n_pairs):
            if has_state:
                st_scr[d, p] = s0_ref[d, p].T
            else:
                st_scr[d, p] = jnp.zeros((LANES, LANES), F32)

    def state_group(ns):
        units = [u for n in ns for u in ((0, n), (1, N - 1 - n))]
        rows = [_chunk_rows(n) for _, n in units]
        pre = [_dot(sm_ref[r, :].astype(BF16), wal_ref[:, d * HK:(d + 1) * HK]) + bal_ref[:, d * HK:(d + 1) * HK]
               for (d, _), r in zip(units, rows)]
        g = [_log_sigmoid(x) * (1.0 / TAU_GLA) for x in pre]
        b = [_tri_sum(tri[d], gi) for (d, _), gi in zip(units, g)]
        ks_all, dec_all = [], []
        for (d, _), r, bi in zip(units, rows, b):
            bend = bi[L - 1:L, :] if d == 0 else bi[0:1, :]
            q = q_ref[r, :] * scale
            ks = (k_ref[r, :] * jnp.exp(bend - bi)).astype(BF16)
            qh_scr[d, r, :] = (q * jnp.exp(bi - bend)).astype(BF16)
            qs_scr[d, r, :] = (q * jnp.exp(bi)).astype(BF16)
            kh_scr[d, r, :] = ks
            ks_all.append(ks)
            dec_all.append(jnp.exp(bend))
        upd_all = [[[_dot(v_ref[r, (2 * p + j) * DV:(2 * p + j + 1) * DV].T.astype(BF16),
                          ks[:, p * LANES:(p + 1) * LANES]) for j in range(2)]
                    for p in range(n_pairs)]
                   for r, ks in zip(rows, ks_all)]
        for (d, n), dec, upd in zip(units, dec_all, upd_all):
            for p in range(n_pairs):
                st = st_scr[d, p]
                sall_scr[d, n, p] = st.astype(BF16)
                st_scr[d, p] = (st * dec[:, p * LANES:(p + 1) * LANES]
                                + jnp.where(head_mask[0], upd[p][0], upd[p][1]))

    _chunk_loop(N, unroll, state_group)

    def out_group(ns):
        pairs = [(d, n, p) for n in ns for d in range(2) for p in range(n_pairs)]
        scores, inter = [], []
        for d, n, p in pairs:
            r = _chunk_rows(n)
            ls = slice(p * LANES, (p + 1) * LANES)
            qh = qh_scr[d, r, ls]
            qs = qs_scr[d, r, ls]
            zero = jnp.zeros_like(qh)
            q2 = jnp.concatenate([jnp.where(head_mask[j], qh, zero) for j in range(2)], axis=0)
            qs2 = jnp.concatenate([jnp.where(head_mask[j], qs, zero) for j in range(2)], axis=0)
            scores.append(_dot_nt(q2, kh_scr[d, r, ls]))
            inter.append(_dot_nt(qs2, sall_scr[d, n, p]))
        probs = [[jnp.where(tmask[d], sc[j * L:(j + 1) * L, :], 0.0).astype(BF16) for j in range(2)]
                 for (d, _, _), sc in zip(pairs, scores)]
        for (d, n, p), pr, it in zip(pairs, probs, inter):
            r = _chunk_rows(n)
            for j in range(2):
                vs = slice((2 * p + j) * DV, (2 * p + j + 1) * DV)
                o_scr[d][r, vs] = _dot(pr[j], v_ref[r, vs].astype(BF16)) + it[j * L:(j + 1) * L, :]

    _chunk_loop(N, unroll, out_group)

    def epilogue(i, carry):
        rows = pl.ds(pl.multiple_of(i * L, L), L)
        for h in range(H_A):
            vs = slice(h * DV, (h + 1) * DV)
            o = of_scr[rows, vs] + ob_scr[rows, vs]
            out_ref[rows, vs] = (_rms(o, gw_ref[:, vs]) * _silu(g_ref[rows, vs])).astype(out_ref.dtype)
        return carry

    lax.fori_loop(0, N, epilogue, 0)

    if write_state:
        for d in range(2):
            for p in range(n_pairs):
                snew_ref[d, p] = st_scr[d, p].T


def _gla_scratch(T, HK, DA):
    n_pairs = HK // LANES
    n_chunks = T // CHUNK
    return [
        pltpu.VMEM((T, DA), F32),
        pltpu.VMEM((T, DA), F32),
        pltpu.VMEM((2, n_pairs, LANES, LANES), F32),
        pltpu.VMEM((2, n_chunks, n_pairs, LANES, LANES), BF16),
        pltpu.VMEM((2, T, HK), BF16),
        pltpu.VMEM((2, T, HK), BF16),
        pltpu.VMEM((2, T, HK), BF16),
    ]


def _mlstm_body(qk_ref, v_ref, og_ref, sm_ref, c0_ref, n0_ref, m0_ref, cw_ref, bm_ref, gw_ref,
                out_ref, cnew_ref, nnew_ref, mnew_ref,
                pad_scr, qk_scr, y_scr, c_scr, n_scr, m_scr, call_scr, nall_scr, mall_scr, g_scr, f_scr,
                *, grid_w, unroll):
    has_state = c0_ref is not None
    write_state = cnew_ref is not None
    T = qk_ref.shape[0]
    L = CHUNK
    N = T // L
    C2 = qk_ref.shape[1]
    HK = C2 // 2
    DK = HK // H_B
    DV = v_ref.shape[1] // H_B
    scale = DK ** -0.5
    n_pairs = HK // LANES
    P = pad_scr.shape[0] - T
    P0 = P // 2
    rows_img = T // grid_w

    lower, upper = _chunk_masks(L)
    tri = (lower.astype(BF16), upper.astype(BF16))
    tmask = (lower, upper)
    lane = lax.broadcasted_iota(jnp.int32, (1, LANES), 1)
    head_mask = (lane < DK, lane >= DK)
    lane_in = lane & (L - 1)

    def lane_cummax(x, d):
        k = 1
        while k < L:
            if d == 0:
                x = jnp.maximum(x, jnp.where(lane_in >= k, pltpu.roll(x, k, axis=1), -jnp.inf))
            else:
                x = jnp.maximum(x, jnp.where(lane_in < L - k, pltpu.roll(x, LANES - k, axis=1), -jnp.inf))
            k *= 2
        return x

    for d in range(2):
        for p in range(n_pairs):
            if has_state:
                c_scr[d, p] = c0_ref[d, p]
                n_scr[2 * d + p:2 * d + p + 1, :] = n0_ref[d, p:p + 1, :]
            else:
                c_scr[d, p] = jnp.zeros((LANES, LANES), F32)
                n_scr[2 * d + p:2 * d + p + 1, :] = jnp.zeros((1, LANES), F32)
    eye_h = (lax.broadcasted_iota(jnp.int32, (H_B, H_B), 0) == lax.broadcasted_iota(jnp.int32, (H_B, H_B), 1))

    def to_col(row):
        return jnp.sum(jnp.where(eye_h, row, 0.0), axis=1, keepdims=True)

    def to_row(col):
        return jnp.sum(jnp.where(eye_h, col, 0.0), axis=0, keepdims=True)

    for d in range(2):
        if has_state:
            m_scr[H_B * d:H_B * (d + 1), 0:1] = to_col(m0_ref[d:d + 1, :])
        else:
            m_scr[H_B * d:H_B * (d + 1), 0:1] = jnp.zeros((H_B, 1), F32)

    pad_scr[0:P0, :] = jnp.zeros((P0, C2), F32)
    pad_scr[P0 + T:P + T, :] = jnp.zeros((P - P0, C2), F32)

    def copy_in(i, carry):
        r0 = pl.multiple_of(i * L, L)
        pad_scr[pl.ds(P0 + r0, L), :] = qk_ref[pl.ds(r0, L), :]
        return carry

    lax.fori_loop(0, N, copy_in, 0)

    lane_c = lax.broadcasted_iota(jnp.int32, (1, C2), 1)
    qscale = jnp.where(lane_c < HK, scale, 1.0).astype(F32)
    sub = lax.broadcasted_iota(jnp.int32, (L, 1), 0)
    img_rows = (0,) if rows_img == 1 else (-1, 0, 1)

    def conv_tile(i, carry):
        r0 = pl.multiple_of(i * L, L)
        col = lax.rem(r0, grid_w) + sub
        ok_left = col >= 1
        ok_right = col <= grid_w - 2
        acc = jnp.zeros((L, C2), F32)
        for di in img_rows:
            blk = pad_scr[pl.ds(P0 + r0 + di * grid_w - 8, L + 16), :]
            left = jnp.where(ok_left, blk[7:7 + L, :], 0.0)
            mid = blk[8:8 + L, :]
            right = jnp.where(ok_right, blk[9:9 + L, :], 0.0)
            wr = 3 * (di + 1)
            acc = acc + left * cw_ref[wr:wr + 1, :] + mid * cw_ref[wr + 1:wr + 2, :] + right * cw_ref[wr + 2:wr + 3, :]
        qk_scr[pl.ds(r0, L), :] = _silu(acc) * qscale
        return carry

    lax.fori_loop(0, N, conv_tile, 0)

    gl = lane - GATE_LANE0
    is_f = ((gl >= H_B) & (gl < 2 * H_B)) | ((gl >= 3 * H_B) & (gl < 4 * H_B))

    def gate_tile(i, carry):
        rows = pl.ds(pl.multiple_of(i * L, L), L)
        x = sm_ref[rows, :] + bm_ref[...]
        y_scr[rows, :] = jnp.where(is_f, _log_sigmoid(x), x)
        return carry

    lax.fori_loop(0, N, gate_tile, 0)


    def state_group(ns):
        units = [u for n in ns for u in ((0, n), (1, N - 1 - n))]
        rows = [_chunk_rows(n) for _, n in units]
        xs = [y_scr[r, :] for r in rows]
        fsum = [_tri_sum(tri[d], x) for (d, _), x in zip(units, xs)]
        wk_all, f_end, c_end = [], [], []
        for (d, n), r, x, fs in zip(units, rows, xs, fsum):
            y = jnp.where(is_f, fs, x)
            li0 = GATE_LANE0 + 2 * H_B * d
            blk = jnp.concatenate([y, y], axis=0).T[li0:li0 + 2 * H_B, :]
            frow = pltpu.roll(blk, H_B, axis=0)
            grow = blk - frow
            g_scr[d, n] = grow
            f_scr[d, n] = frow
            e_col = L - 1 if d == 0 else 0
            f_end.append(frow[0:H_B, e_col:e_col + 1])
            ce8 = jnp.max(grow, axis=1, keepdims=True)
            c_end.append(ce8[0:H_B, :])
            wk_all.append(jnp.exp(grow[:, 0:L] - ce8))
        kv_all, ksum_all = [], []
        for r, wk8 in zip(rows, wk_all):
            kv_u, ks_u = [], []
            wk8b = wk8.astype(BF16)
            for p in range(n_pairs):
                kp = qk_scr[r, HK + p * LANES:HK + (p + 1) * LANES]
                kt = kp.T
                ks8 = _dot(wk8b, kp.astype(BF16))
                for j in range(2):
                    h = 2 * p + j
                    kwt = (kt[j * DK:(j + 1) * DK, :] * wk8[h:h + 1, :]).astype(BF16)
                    kv_u.append(_dot(kwt, v_ref[r, h * DV:(h + 1) * DV].astype(BF16)))
                    ks_u.append(ks8[h:h + 1, :])
            kv_all.append(kv_u)
            ksum_all.append(ks_u)
        for (d, n), fe, ce, kv_u, ks_u in zip(units, f_end, c_end, kv_all, ksum_all):
            m_prev = m_scr[H_B * d:H_B * (d + 1), 0:1]
            mall_scr[d, n, 0:H_B, 0:1] = m_prev
            mx = jnp.maximum(m_prev, ce)
            a_all = jnp.exp(m_prev - mx)
            b_all = jnp.exp(ce - mx)
            m_scr[H_B * d:H_B * (d + 1), 0:1] = fe + mx
            for p in range(n_pairs):
                npair = n_scr[2 * d + p:2 * d + p + 1, :]
                nall_scr[d, n, p:p + 1, :] = npair
                a_s = [a_all[2 * p + j:2 * p + j + 1, :] for j in range(2)]
                b_s = [b_all[2 * p + j:2 * p + j + 1, :] for j in range(2)]
                for j in range(2):
                    hr = slice(j * DK, (j + 1) * DK)
                    cj = c_scr[d, p, hr, :]
                    call_scr[d, n, p, hr, :] = cj.astype(BF16)
                    c_scr[d, p, hr, :] = a_s[j] * cj + b_s[j] * kv_u[2 * p + j]
                n_scr[2 * d + p:2 * d + p + 1, :] = (
                    jnp.where(head_mask[0], a_s[0], a_s[1]) * npair
                    + jnp.where(head_mask[0], b_s[0] * ks_u[2 * p], b_s[1] * ks_u[2 * p + 1]))

    _chunk_loop(N, unroll, state_group)

    eye = lower & upper
    ones8 = jnp.ones((8, L), BF16)
    sub8 = lax.broadcasted_iota(jnp.int32, (8, LANES), 0)
    sub_h = lax.broadcasted_iota(jnp.int32, (H_B, L), 0)
    n_rows = [((sub8 == 2 * p) & head_mask[0]) | ((sub8 == 2 * p + 1) & head_mask[1]) for p in range(n_pairs)]

    def head_rows(vals):
        out = vals[0][0:H_B, :]
        for h in range(1, H_B):
            out = jnp.where(sub_h == h, vals[h][0:H_B, :], out)
        return out

    def out_group(ns):
        chunks = [(d, n) for n in ns for d in range(2)]
        pairs = [(d, n, p) for d, n in chunks for p in range(n_pairs)]
        units = [(d, n, p, j) for d, n, p in pairs for j in range(2)]
        cms = [lane_cummax(g_scr[d, n], d)[0:H_B, 0:L] for d, n in chunks]
        qk2s, qc2s, qn2s = [], [], []
        for d, n, p in pairs:
            r = _chunk_rows(n)
            qp = qk_scr[r, p * LANES:(p + 1) * LANES]
            q2 = jnp.concatenate([jnp.where(head_mask[j], qp, 0.0) for j in range(2)], axis=0).astype(BF16)
            qk2s.append(_dot_nt(q2, qk_scr[r, HK + p * LANES:HK + (p + 1) * LANES].astype(BF16)))
            qc2s.append(_dot(q2, call_scr[d, n, p]))
            nsel = jnp.where(n_rows[p], nall_scr[d, n, p:p + 1, :], 0.0).astype(BF16)
            qn2s.append(_dot_nt(nsel, qp.astype(BF16)))
        s_all = []
        for ui, (d, n, p, j) in enumerate(units):
            grow = g_scr[d, n, 2 * p + j:2 * p + j + 1, 0:L]
            e = jnp.where(tmask[d], grow, -jnp.inf)
            cmax = jnp.max(e, axis=-1, keepdims=True)
            s_all.append((qk2s[ui // 2][j * L:(j + 1) * L, :] * jnp.exp(e - cmax)).astype(BF16))
        nums = [_dot(s, v_ref[_chunk_rows(n), (2 * p + j) * DV:(2 * p + j + 1) * DV].astype(BF16))
                for (d, n, p, j), s in zip(units, s_all)]
        dens = [_dot_nt(ones8, s) for s in s_all]
        scales = []
        for ci, (d, n) in enumerate(chunks):
            den_loc = head_rows(dens[ci * H_B:(ci + 1) * H_B])
            qn = qn2s[ci * n_pairs][0:H_B, :]
            for p in range(1, n_pairs):
                qn = qn + qn2s[ci * n_pairs + p][0:H_B, :]
            cm = cms[ci]
            m_prev = mall_scr[d, n, 0:H_B, 0:1]
            delta = cm - m_prev
            t = jnp.exp(-jnp.abs(delta))
            w_loc = jnp.where(delta <= 0.0, t, 1.0)
            w_inter = jnp.where(delta <= 0.0, 1.0, t)
            mt = f_scr[d, n, 0:H_B, 0:L] + jnp.maximum(m_prev, cm)
            den = w_loc * den_loc + w_inter * qn
            rinv = 1.0 / jnp.maximum(jnp.abs(den), jnp.exp(-mt))
            scales.append((w_loc * rinv, w_inter * rinv))
        hs = []
        for ui, (d, n, p, j) in enumerate(units):
            h = 2 * p + j
            sc_loc, sc_inter = scales[ui // H_B]
            d_loc = jnp.where(eye, sc_loc[h:h + 1, :], 0.0).astype(BF16)
            d_inter = jnp.where(eye, sc_inter[h:h + 1, :], 0.0).astype(BF16)
            hs.append(_dot(d_loc, nums[ui].astype(BF16))
                      + _dot(d_inter, qc2s[ui // 2][j * L:(j + 1) * L, :].astype(BF16)))
        for ni, n in enumerate(ns):
            r = _chunk_rows(n)
            for h in range(H_B):
                vs = slice(h * DV, (h + 1) * DV)
                o = hs[(2 * ni) * H_B + h] + hs[(2 * ni + 1) * H_B + h]
                out_ref[r, vs] = (_rms(o, gw_ref[:, vs]) * _sigmoid(og_ref[r, vs])).astype(out_ref.dtype)

    _chunk_loop(N, unroll, out_group)

    if write_state:
        for d in range(2):
            for p in range(n_pairs):
                cnew_ref[d, p] = c_scr[d, p]
                nnew_ref[d, p:p + 1, :] = n_scr[2 * d + p:2 * d + p + 1, :]
            mnew_ref[d:d + 1, :] = to_row(m_scr[H_B * d:H_B * (d + 1), 0:1])


def _mlstm_scratch(T, C2, grid_w):
    n_pairs = C2 // 2 // LANES
    n_chunks = T // CHUNK
    pad_rows = 2 * (grid_w + 8) if T // grid_w > 1 else 16
    return [
        pltpu.VMEM((T + pad_rows, C2), F32),
        pltpu.VMEM((T, C2), F32),
        pltpu.VMEM((T, SMALL_W), F32),
        pltpu.VMEM((2, n_pairs, LANES, LANES), F32),
        pltpu.VMEM((8, LANES), F32),
        pltpu.VMEM((8, LANES), F32),
        pltpu.VMEM((2, n_chunks, n_pairs, LANES, LANES), BF16),
        pltpu.VMEM((2, n_chunks, 8, LANES), F32),
        pltpu.VMEM((2, n_chunks, 8, LANES), F32),
        pltpu.VMEM((2, n_chunks, 8, LANES), F32),
        pltpu.VMEM((2, n_chunks, 8, LANES), F32),
    ]


N_GLA_SCRATCH = 7
N_MLSTM_SCRATCH = 11


def _scan_kernel(*refs, cols, has_state, write_state, n_cast, grid_w, unroll):
    refs = list(refs)
    z_ref = refs.pop(0)
    s0_ref = c0_ref = n0_ref = m0_ref = None
    if has_state:
        s0_ref, c0_ref, n0_ref, m0_ref = refs[:4]
        del refs[:4]
    wal_ref, bal_ref, gwa_ref, cw_ref, bm_ref, gwb_ref = refs[:6]
    del refs[:6]
    cast_in = refs[:n_cast]
    del refs[:n_cast]
    outa_ref, outb_ref = refs[:2]
    del refs[:2]
    snew_ref = cnew_ref = nnew_ref = mnew_ref = None
    if write_state:
        snew_ref, cnew_ref, nnew_ref, mnew_ref = refs[:4]
        del refs[:4]
    cast_out = refs[:n_cast]
    del refs[:n_cast]
    gla_scr = refs[:N_GLA_SCRATCH]
    mlstm_scr = refs[N_GLA_SCRATCH:]

    for src, dst in zip(cast_in, cast_out):
        dst[...] = src[...].astype(BF16)

    def view(name):
        c0, w = cols[name]
        return z_ref.at[:, pl.ds(c0, w)]

    sm_ref = view("small")
    _gla_body(view("qa"), view("ka"), view("va"), view("ga"), sm_ref, s0_ref, wal_ref, bal_ref, gwa_ref,
              outa_ref, snew_ref, *gla_scr, unroll=unroll)
    _mlstm_body(view("qkb"), view("vb"), view("ob"), sm_ref, c0_ref, n0_ref, m0_ref, cw_ref, bm_ref, gwb_ref,
                outb_ref, cnew_ref, nnew_ref, mnew_ref, *mlstm_scr, grid_w=grid_w, unroll=unroll)


def _scan_call(z2d, row0, B, T, states, lw, *, grid_w, write_state, casts=()):
    n_z = z2d.shape[1]
    assert row0 % T == 0 and z2d.shape[0] % T == 0
    z3 = z2d.reshape(z2d.shape[0] // T, T, n_z)
    blk0 = row0 // T
    HK = lw["wal_p"].shape[1] // 2
    DA = lw["gnorm_a_w"].shape[0]
    C2 = lw["conv9"].shape[1]
    DB = lw["gnorm_b_w"].shape[0]
    pa, pb = HK // LANES, C2 // 2 // LANES
    n_chunks = T // CHUNK
    has_state = states is not None
    widths = (("qa", HK), ("ka", HK), ("va", DA), ("ga", DA), ("qkb", C2), ("vb", DB), ("ob", DB),
              ("small", SMALL_W))
    cols, c0 = {}, 0
    for name, w in widths:
        cols[name] = (c0, w)
        c0 += w
    assert c0 == n_z
    cast_in_specs, cast_out_specs, cast_out_shape, cast_args = _cast_specs(casts, B)
    kern = functools.partial(_scan_kernel, cols=cols, has_state=has_state, write_state=write_state,
                             n_cast=len(casts), grid_w=grid_w, unroll=min(n_chunks, SCAN_UNROLL))

    def per_batch(shape):
        nd = len(shape)
        return pl.BlockSpec((None,) + tuple(shape), lambda b: (b,) + (0,) * nd)

    def whole(a):
        return pl.BlockSpec(a.shape, lambda b: (0,) * a.ndim)

    state_shapes = ((2, pa, LANES, LANES), (2, pb, LANES, LANES), (2, pb, LANES), (2, H_B))
    in_specs = [pl.BlockSpec((None, T, n_z), lambda b: (b + blk0, 0, 0))]
    args = [z3]
    if has_state:
        s_gla, s_c, s_n, s_m = states
        args += [s_gla.reshape((B,) + state_shapes[0]), s_c.reshape((B,) + state_shapes[1]),
                 s_n.reshape((B,) + state_shapes[2]), s_m]
        in_specs += [per_batch(s) for s in state_shapes]
    small = [lw["wal_p"], lw["bal_p"], lw["gnorm_a_w"].reshape(1, DA), lw["conv9"], lw["bm_row"],
             lw["gnorm_b_w"].reshape(1, DB)]
    args += small + cast_args
    in_specs += [whole(a) for a in small] + cast_in_specs
    out_specs = [per_batch((T, DA)), per_batch((T, DB))]
    out_shape = [jax.ShapeDtypeStruct((B, T, DA), BF16), jax.ShapeDtypeStruct((B, T, DB), BF16)]
    if write_state:
        out_specs += [per_batch(s) for s in state_shapes]
        out_shape += [jax.ShapeDtypeStruct((B,) + s, F32) for s in state_shapes]
    out_specs += cast_out_specs
    out_shape += cast_out_shape
    scratch = _gla_scratch(T, HK, DA) + _mlstm_scratch(T, C2, grid_w)
    assert len(scratch) == N_GLA_SCRATCH + N_MLSTM_SCRATCH
    return pl.pallas_call(
        kern,
        grid=(B,),
        in_specs=in_specs,
        out_specs=out_specs,
        out_shape=out_shape,
        scratch_shapes=scratch,
        compiler_params=pltpu.CompilerParams(dimension_semantics=("arbitrary",),
                                             vmem_limit_bytes=VMEM_LIMIT),
        name="mixer_scans",
    )(*args)


def _outff_kernel(xc_ref, xl_ref, ac_ref, al_ref, bc_ref, bl_ref, mod_ref, n2_ref, fn_ref, wo_ref, w1_ref, w2_ref,
                  yc_ref, yl_ref, *, n_ctx, tiles_per_req, ff_chunk, final_norm):
    D = xc_ref.shape[1]
    DA = ac_ref.shape[1]
    is_ctx, row = _tile_group(n_ctx, tiles_per_req)

    def mod(k):
        return mod_ref[pl.ds(row, 1), k * D:(k + 1) * D]

    def tile(x_ref, a_ref, b_ref, y_ref):
        y = _dot(a_ref[...], wo_ref[0:DA, :]) + _dot(b_ref[...], wo_ref[DA:, :])
        x1 = x_ref[...] + mod(2) * y
        h2 = (_rms(x1, n2_ref[...]) * (1.0 + mod(4)) + mod(3)).astype(BF16)
        acc = jnp.zeros(x1.shape, F32)
        for c0 in range(0, w1_ref.shape[1], ff_chunk):
            u = jnp.maximum(_dot(h2, w1_ref[:, c0:c0 + ff_chunk]), 0.0)
            acc = acc + _dot((u * u).astype(BF16), w2_ref[c0:c0 + ff_chunk, :])
        x2 = x1 + mod(5) * acc
        y_ref[...] = _rms(x2, fn_ref[...]) if final_norm else x2

    @pl.when(is_ctx)
    def _():
        tile(xc_ref, ac_ref, bc_ref, yc_ref)

    @pl.when(jnp.logical_not(is_ctx))
    def _():
        tile(xl_ref, al_ref, bl_ref, yl_ref)


def _outff_call(xc2d, xl2d, ac, al, bc, bl, mod, norm2_w, final_w, wo, w1, w2, *, tm, tiles_per_req, final_norm):
    (Mc, D), Ml = xc2d.shape, xl2d.shape[0]
    n_ctx = Mc // tm
    DA = ac.shape[1]
    DFF = w1.shape[1]
    kern = functools.partial(_outff_kernel, n_ctx=n_ctx, tiles_per_req=tiles_per_req, ff_chunk=512,
                             final_norm=final_norm)
    once = pl.Buffered(1)
    ctx, lat = _ctx_tile(n_ctx), _lat_tile(n_ctx)
    return pl.pallas_call(
        kern,
        grid=((Mc + Ml) // tm,),
        in_specs=[
            pl.BlockSpec((tm, D), ctx), pl.BlockSpec((tm, D), lat),
            pl.BlockSpec((tm, DA), ctx), pl.BlockSpec((tm, DA), lat),
            pl.BlockSpec((tm, D - DA), ctx), pl.BlockSpec((tm, D - DA), lat),
            pl.BlockSpec(mod.shape, lambda i: (0, 0)),
            pl.BlockSpec((1, D), lambda i: (0, 0)),
            pl.BlockSpec((1, D), lambda i: (0, 0)),
            pl.BlockSpec((D, D), lambda i: (0, 0), pipeline_mode=once),
            pl.BlockSpec((D, DFF), lambda i: (0, 0), pipeline_mode=once),
            pl.BlockSpec((DFF, D), lambda i: (0, 0), pipeline_mode=once),
        ],
        out_specs=[pl.BlockSpec((tm, D), ctx), pl.BlockSpec((tm, D), lat)],
        out_shape=[jax.ShapeDtypeStruct((Mc, D), F32), jax.ShapeDtypeStruct((Ml, D), F32)],
        compiler_params=pltpu.CompilerParams(dimension_semantics=("arbitrary",),
                                             vmem_limit_bytes=VMEM_LIMIT),
        name="outproj_mlp",
    )(xc2d, xl2d, ac, al, bc, bl, mod, norm2_w.reshape(1, D), final_w.reshape(1, D), wo, w1, w2)


def _layer(xc, xl, mod, cached, lw, ffw, final_w, final_norm):
    (Bc, Tc, D), (Bl, Tl, _) = xc.shape, xl.shape
    tm = TOKEN_TILE
    assert (Bc * Tc) % tm == 0 and Tl % tm == 0 and (Bc * Tc) % Tl == 0
    xc2d, xl2d = xc.reshape(Bc * Tc, D), xl.reshape(Bl * Tl, D)
    z = _inproj_call(xc2d, xl2d, mod, lw["norm1_w"], lw["w_in_t"], tm=tm, tiles_per_req=Tl // tm,
                     big_rows=lw["big_rows"], small_rows=lw["small_rows"])
    res_c = _scan_call(z, 0, Bc, Tc, None, lw, grid_w=Tc, write_state=True,
                       casts=((ffw[0], 0), (ffw[1], 1), (ffw[2], 0)))
    res_l = _scan_call(z, Bc * Tc, Bl, Tl, cached, lw, grid_w=GRID_W, write_state=False)
    yc, yl = _outff_call(xc2d, xl2d, res_c[0].reshape(Bc * Tc, -1), res_l[0].reshape(Bl * Tl, -1),
                         res_c[1].reshape(Bc * Tc, -1), res_l[1].reshape(Bl * Tl, -1), mod, lw["norm2_w"],
                         final_w, *res_c[-3:], tm=tm, tiles_per_req=Tl // tm, final_norm=final_norm)
    return yc.reshape(Bc, Tc, D), yl.reshape(Bl, Tl, D), tuple(res_c[2:6])


def _layer_weights(l, norm1_w, norm2_w, w_in, w_alpha2, b_alpha, b_mgate, conv_w, gnorm_a_w, gnorm_b_w):
    hk_a = w_alpha2.shape[-1]
    d_a = gnorm_a_w.shape[-1]
    d_b = gnorm_b_w.shape[-1]
    hk_b = conv_w.shape[-1] // 2
    sizes = (hk_a, hk_a, d_a, d_a, 2 * R_ALPHA, hk_b, hk_b, d_b, d_b, 4 * H_B)
    offs = [0]
    for s in sizes:
        offs.append(offs[-1] + s)
    big_rows = ((offs[0], offs[4] - offs[0]), (offs[5], offs[9] - offs[5]))
    small_rows = ((offs[4], offs[5] - offs[4]), (offs[9], offs[10] - offs[9]))
    assert all(n % LANES == 0 and r % 16 == 0 for r, n in big_rows)
    wal = w_alpha2[l]
    wal_p = jnp.zeros((SMALL_W, 2 * hk_a), F32)
    wal_p = wal_p.at[0:R_ALPHA, 0:hk_a].set(wal[0]).at[R_ALPHA:2 * R_ALPHA, hk_a:].set(wal[1]).astype(BF16)
    bm_row = jnp.zeros((1, SMALL_W), F32).at[0, GATE_LANE0:GATE_LANE0 + 4 * H_B].set(b_mgate[l].reshape(-1))
    return dict(
        norm1_w=norm1_w[l], norm2_w=norm2_w[l], w_in_t=jnp.swapaxes(w_in[l], 0, 1),
        big_rows=big_rows, small_rows=small_rows, wal_p=wal_p,
        bal_p=b_alpha[l].reshape(1, -1), bm_row=bm_row,
        conv9=conv_w[l].reshape(-1, conv_w.shape[-1]),
        gnorm_a_w=gnorm_a_w[l], gnorm_b_w=gnorm_b_w[l],
    )


def kernel(x_prompt, x_sample, c, state_gla, state_mlstm_C, state_mlstm_n, state_mlstm_m, c_ctx, w_ada, b_ada, norm1_w, norm2_w, w_in, w_alpha2, b_alpha, b_mgate, conv_w, gnorm_a_w, gnorm_b_w, w_out, w_ff1, w_ff2, final_norm_w):
    depth = w_in.shape[0]
    D = x_prompt.shape[-1]
    Bp, Tp, _ = x_prompt.shape
    Bs = x_sample.shape[0]
    assert 1 + Bs <= COND_ROWS
    cond = jnp.concatenate([c_ctx[None, :], c, jnp.zeros((COND_ROWS - 1 - Bs, D), F32)], axis=0)
    xp, xs = x_prompt, x_sample
    s_gla, s_c, s_n, s_m = [], [], [], []
    for l in range(depth):
        lw = _layer_weights(l, norm1_w, norm2_w, w_in, w_alpha2, b_alpha, b_mgate, conv_w,
                            gnorm_a_w, gnorm_b_w)
        mod = _ada_call(cond, w_ada[l], b_ada[l])
        cached = (state_gla[:, l], state_mlstm_C[:, l], state_mlstm_n[:, l], state_mlstm_m[:, l])
        xp, xs, ctx = _layer(xp, xs, mod, cached, lw, (w_out[l], w_ff1[l], w_ff2[l]), final_norm_w,
                             l == depth - 1)
        s_gla.append(ctx[0].reshape(Bp, 2, H_A, -1, ctx[0].shape[-1]))
        s_c.append(ctx[1].reshape(Bp, 2, H_B, -1, ctx[1].shape[-1]))
        s_n.append(ctx[2].reshape(Bp, 2, H_B, -1))
        s_m.append(ctx[3])
    dt = x_prompt.dtype
    return (xp, xs, jnp.stack(s_gla, axis=1).astype(dt), jnp.stack(s_c, axis=1).astype(dt),
            jnp.stack(s_n, axis=1).astype(dt), jnp.stack(s_m, axis=1).astype(dt))
```

```python
import functools

import jax
import jax.numpy as jnp
from jax import lax
from jax.experimental import pallas as pl
from jax.experimental.pallas import tpu as pltpu

F32 = jnp.float32
BF16 = jnp.bfloat16

GRID_W = 64
H_A = 4
H_B = 4
R_ALPHA = 16
TAU_GLA = 16.0
CHUNK = 64
EPS = 1e-6
LANES = 128
COND_ROWS = 8
SMALL_W = LANES
GATE_LANE0 = 2 * R_ALPHA
VMEM_LIMIT = 56 * 1024 * 1024
SCAN_UNROLL = 4
TOKEN_TILE = 512


def _sigmoid(x):
    return 1.0 / (1.0 + jnp.exp(-x))


def _silu(x):
    return x * _sigmoid(x)


def _log_sigmoid(x):
    return jnp.minimum(x, 0.0) - jnp.log1p(jnp.exp(-jnp.abs(x)))


def _dot(a, b):
    return jnp.dot(a, b, preferred_element_type=F32)


def _dot_nt(a, b):
    return lax.dot_general(a, b, (((1,), (1,)), ((), ())), preferred_element_type=F32)


def _rms(x, w):
    return x * lax.rsqrt(jnp.mean(x * x, axis=-1, keepdims=True) + EPS) * w


def _tri_sum(tri, x):
    hi = x.astype(BF16)
    r1 = x - hi.astype(F32)
    mid = r1.astype(BF16)
    lo = (r1 - mid.astype(F32)).astype(BF16)
    return _dot(tri, hi) + _dot(tri, mid) + _dot(tri, lo)


def _chunk_masks(L):
    row = lax.broadcasted_iota(jnp.int32, (L, L), 0)
    col = lax.broadcasted_iota(jnp.int32, (L, L), 1)
    lower = row >= col
    upper = row <= col
    return lower, upper


def _ada_kernel(c_ref, w_ref, b_ref, o_ref):
    s = _silu(c_ref[...])
    o_ref[...] = _dot(s.astype(BF16), w_ref[...].astype(BF16)) + b_ref[...]


def _ada_call(cond, w_ada, b_ada):
    D = cond.shape[1]
    n_out = w_ada.shape[1]
    tn = 2048 if n_out % 2048 == 0 else 1024
    return pl.pallas_call(
        _ada_kernel,
        grid=(n_out // tn,),
        in_specs=[
            pl.BlockSpec((COND_ROWS, D), lambda j: (0, 0)),
            pl.BlockSpec((D, tn), lambda j: (0, j)),
            pl.BlockSpec((1, tn), lambda j: (0, j)),
        ],
        out_specs=pl.BlockSpec((COND_ROWS, tn), lambda j: (0, j)),
        out_shape=jax.ShapeDtypeStruct((COND_ROWS, n_out), F32),
        compiler_params=pltpu.CompilerParams(dimension_semantics=("arbitrary",),
                                             vmem_limit_bytes=VMEM_LIMIT),
        name="ada_mod",
    )(cond, w_ada, b_ada.reshape(1, n_out))


def _tile_group(n_ctx, tiles_per_req):
    i = pl.program_id(0)
    is_ctx = i < n_ctx
    row = jnp.where(is_ctx, 0, 1 + jnp.maximum(i - n_ctx, 0) // tiles_per_req)
    return is_ctx, row


def _ctx_tile(n_ctx):
    return lambda i: (jnp.minimum(i, n_ctx - 1), 0)


def _lat_tile(n_ctx):
    return lambda i: (jnp.maximum(i - n_ctx, 0), 0)


def _inproj_kernel(xc_ref, xl_ref, mod_ref, nw_ref, wt_ref, z_ref, wb_scr, *, n_ctx, tiles_per_req, big_rows,
                   small_rows):
    D = xc_ref.shape[1]

    @pl.when(pl.program_id(0) == 0)
    def _():
        col = 0
        for r0, n in big_rows:
            for k in range(n // LANES):
                blk = wt_ref[r0 + k * LANES:r0 + (k + 1) * LANES, :]
                wb_scr[:, col:col + LANES] = blk.T.astype(BF16)
                col += LANES
        parts = [wt_ref[r0:r0 + n, :] for r0, n in small_rows]
        n_small = sum(n for _, n in small_rows)
        parts.append(jnp.zeros((SMALL_W - n_small, D), F32))
        wb_scr[:, col:col + SMALL_W] = jnp.concatenate(parts, axis=0).T.astype(BF16)

    is_ctx, row = _tile_group(n_ctx, tiles_per_req)

    def tile(x_ref):
        sh1 = mod_ref[pl.ds(row, 1), 0:D]
        sc1 = mod_ref[pl.ds(row, 1), D:2 * D]
        h = _rms(x_ref[...], nw_ref[...]) * (1.0 + sc1) + sh1
        z_ref[...] = _dot(h.astype(BF16), wb_scr[...])

    @pl.when(is_ctx)
    def _():
        tile(xc_ref)

    @pl.when(jnp.logical_not(is_ctx))
    def _():
        tile(xl_ref)


def _inproj_call(xc2d, xl2d, mod, norm_w, w_in_t, *, tm, tiles_per_req, big_rows, small_rows):
    (Mc, D), Ml = xc2d.shape, xl2d.shape[0]
    n_ctx = Mc // tm
    n_out = sum(n for _, n in big_rows) + SMALL_W
    kern = functools.partial(_inproj_kernel, n_ctx=n_ctx, tiles_per_req=tiles_per_req,
                             big_rows=big_rows, small_rows=small_rows)
    return pl.pallas_call(
        kern,
        grid=((Mc + Ml) // tm,),
        in_specs=[
            pl.BlockSpec((tm, D), _ctx_tile(n_ctx)),
            pl.BlockSpec((tm, D), _lat_tile(n_ctx)),
            pl.BlockSpec(mod.shape, lambda i: (0, 0)),
            pl.BlockSpec((1, D), lambda i: (0, 0)),
            pl.BlockSpec(w_in_t.shape, lambda i: (0, 0), pipeline_mode=pl.Buffered(1)),
        ],
        out_specs=pl.BlockSpec((tm, n_out), lambda i: (i, 0)),
        out_shape=jax.ShapeDtypeStruct((Mc + Ml, n_out), F32),
        scratch_shapes=[pltpu.VMEM((D, n_out), BF16)],
        compiler_params=pltpu.CompilerParams(dimension_semantics=("arbitrary",),
                                             vmem_limit_bytes=VMEM_LIMIT),
        name="norm_inproj",
    )(xc2d, xl2d, mod, norm_w.reshape(1, D), w_in_t)


def _chunk_loop(n_chunks, unroll, fn):
    if unroll >= n_chunks:
        fn(list(range(n_chunks)))
        return

    def body(i, carry):
        fn([i * unroll + u for u in range(unroll)])
        return carry

    lax.fori_loop(0, n_chunks // unroll, body, 0)


def _chunk_rows(n):
    if isinstance(n, int):
        return pl.ds(n * CHUNK, CHUNK)
    return pl.ds(pl.multiple_of(n * CHUNK, CHUNK), CHUNK)


def _cast_specs(casts, n_steps):
    in_specs, out_specs, out_shape, args = [], [], [], []
    for w, axis in casts:
        blk = list(w.shape)
        assert blk[axis] % n_steps == 0
        blk[axis] //= n_steps
        assert blk[0] % 16 == 0 and blk[1] % LANES == 0
        idx = (lambda b: (b, 0)) if axis == 0 else (lambda b: (0, b))
        in_specs.append(pl.BlockSpec(tuple(blk), idx))
        out_specs.append(pl.BlockSpec(tuple(blk), idx))
        out_shape.append(jax.ShapeDtypeStruct(w.shape, BF16))
        args.append(w)
    return in_specs, out_specs, out_shape, args


def _gla_body(q_ref, k_ref, v_ref, g_ref, sm_ref, s0_ref, wal_ref, bal_ref, gw_ref, out_ref, snew_ref,
              of_scr, ob_scr, st_scr, sall_scr, qh_scr, qs_scr, kh_scr, *, unroll):
    has_state = s0_ref is not None
    write_state = snew_ref is not None
    T = q_ref.shape[0]
    L = CHUNK
    N = T // L
    HK = q_ref.shape[1]
    DK = HK // H_A
    DV = v_ref.shape[1] // H_A
    scale = DK ** -0.5
    n_pairs = HK // LANES

    lower, upper = _chunk_masks(L)
    tri = (lower.astype(BF16), upper.astype(BF16))
    tmask = (lower, upper)
    lane = lax.broadcasted_iota(jnp.int32, (1, LANES), 1)
    head_mask = (lane < DK, lane >= DK)
    o_scr = (of_scr, ob_scr)

    for d in range(2):
        for p in range(n_pairs):
            if has_state:
                st_scr[d, p] = s0_ref[d, p].T
            else:
                st_scr[d, p] = jnp.zeros((LANES, LANES), F32)

    def state_group(ns):
        units = [u for n in ns for u in ((0, n), (1, N - 1 - n))]
        rows = [_chunk_rows(n) for _, n in units]
        vt_all = [[jnp.concatenate([v_ref[r, (2 * p + j) * DV:(2 * p + j + 1) * DV] for j in range(2)],
                                   axis=0).T.astype(BF16) for p in range(n_pairs)] for r in rows]
        pre = [_dot(sm_ref[r, :].astype(BF16), wal_ref[:, d * HK:(d + 1) * HK]) + bal_ref[:, d * HK:(d + 1) * HK]
               for (d, _), r in zip(units, rows)]
        g = [_log_sigmoid(x) * (1.0 / TAU_GLA) for x in pre]
        b = [_tri_sum(tri[d], gi) for (d, _), gi in zip(units, g)]
        ks_all, dec_all = [], []
        for (d, _), r, bi in zip(units, rows, b):
            bend = bi[L - 1:L, :] if d == 0 else bi[0:1, :]
            q = q_ref[r, :] * scale
            ks = (k_ref[r, :] * jnp.exp(bend - bi)).astype(BF16)
            qh_scr[d, r, :] = (q * jnp.exp(bi - bend)).astype(BF16)
            qs_scr[d, r, :] = (q * jnp.exp(bi)).astype(BF16)
            kh_scr[d, r, :] = ks
            ks_all.append(ks)
            dec_all.append(jnp.exp(bend))
        upd_all = []
        for vt_u, ks in zip(vt_all, ks_all):
            upd_u = []
            for p in range(n_pairs):
                kp = ks[:, p * LANES:(p + 1) * LANES]
                kk = jnp.concatenate([jnp.where(head_mask[j], kp, jnp.zeros_like(kp)) for j in range(2)], axis=0)
                upd_u.append(_dot(vt_u[p], kk))
            upd_all.append(upd_u)
        st = [[st_scr[d, p] for p in range(n_pairs)] for d in range(2)]
        for (d, n), dec, upd in zip(units, dec_all, upd_all):
            for p in range(n_pairs):
                sall_scr[d, n, p] = st[d][p].astype(BF16)
                st[d][p] = st[d][p] * dec[:, p * LANES:(p + 1) * LANES] + upd[p]
        for d in range(2):
            for p in range(n_pairs):
                st_scr[d, p] = st[d][p]

    _chunk_loop(N, unroll, state_group)

    def out_group(ns):
        pairs = [(d, n, p) for n in ns for d in range(2) for p in range(n_pairs)]
        scores, inter = [], []
        for d, n, p in pairs:
            r = _chunk_rows(n)
            ls = slice(p * LANES, (p + 1) * LANES)
            qh = qh_scr[d, r, ls]
            qs = qs_scr[d, r, ls]
            zero = jnp.zeros_like(qh)
            q2 = jnp.concatenate([jnp.where(head_mask[j], qh, zero) for j in range(2)], axis=0)
            qs2 = jnp.concatenate([jnp.where(head_mask[j], qs, zero) for j in range(2)], axis=0)
            scores.append(_dot_nt(q2, kh_scr[d, r, ls]))
            inter.append(_dot_nt(qs2, sall_scr[d, n, p]))
        probs = [[jnp.where(tmask[d], sc[j * L:(j + 1) * L, :], 0.0).astype(BF16) for j in range(2)]
                 for (d, _, _), sc in zip(pairs, scores)]
        for (d, n, p), pr, it in zip(pairs, probs, inter):
            r = _chunk_rows(n)
            for j in range(2):
                vs = slice((2 * p + j) * DV, (2 * p + j + 1) * DV)
                o_scr[d][r, vs] = _dot(pr[j], v_ref[r, vs].astype(BF16)) + it[j * L:(j + 1) * L, :]

    _chunk_loop(N, unroll, out_group)

    def epilogue(i, carry):
        rows = pl.ds(pl.multiple_of(i * L, L), L)
        for h in range(H_A):
            vs = slice(h * DV, (h + 1) * DV)
            o = of_scr[rows, vs] + ob_scr[rows, vs]
            out_ref[rows, vs] = (_rms(o, gw_ref[:, vs]) * _silu(g_ref[rows, vs])).astype(out_ref.dtype)
        return carry

    lax.fori_loop(0, N, epilogue, 0)

    if write_state:
        for d in range(2):
            for p in range(n_pairs):
                snew_ref[d, p] = st_scr[d, p].T


def _gla_scratch(T, HK, DA):
    n_pairs = HK // LANES
    n_chunks = T // CHUNK
    return [
        pltpu.VMEM((T, DA), F32),
        pltpu.VMEM((T, DA), F32),
        pltpu.VMEM((2, n_pairs, LANES, LANES), F32),
        pltpu.VMEM((2, n_chunks, n_pairs, LANES, LANES), BF16),
        pltpu.VMEM((2, T, HK), BF16),
        pltpu.VMEM((2, T, HK), BF16),
        pltpu.VMEM((2, T, HK), BF16),
    ]


def _mlstm_body(qk_ref, v_ref, og_ref, sm_ref, c0_ref, n0_ref, m0_ref, cw_ref, bm_ref, gw_ref,
                out_ref, cnew_ref, nnew_ref, mnew_ref,
                pad_scr, qk_scr, y_scr, c_scr, n_scr, m_scr, call_scr, nall_scr, mall_scr, g_scr, f_scr,
                *, grid_w, unroll):
    has_state = c0_ref is not None
    write_state = cnew_ref is not None
    T = qk_ref.shape[0]
    L = CHUNK
    N = T // L
    C2 = qk_ref.shape[1]
    HK = C2 // 2
    DK = HK // H_B
    DV = v_ref.shape[1] // H_B
    scale = DK ** -0.5
    n_pairs = HK // LANES
    P = pad_scr.shape[0] - T
    P0 = P // 2
    rows_img = T // grid_w

    lower, upper = _chunk_masks(L)
    tri = (lower.astype(BF16), upper.astype(BF16))
    tmask = (lower, upper)
    lane = lax.broadcasted_iota(jnp.int32, (1, LANES), 1)
    head_mask = (lane < DK, lane >= DK)
    lane_in = lane & (L - 1)

    def lane_cummax(x, d):
        k = 1
        while k < L:
            if d == 0:
                x = jnp.maximum(x, jnp.where(lane_in >= k, pltpu.roll(x, k, axis=1), -jnp.inf))
            else:
                x = jnp.maximum(x, jnp.where(lane_in < L - k, pltpu.roll(x, LANES - k, axis=1), -jnp.inf))
            k *= 2
        return x

    for d in range(2):
        for p in range(n_pairs):
            if has_state:
                c_scr[d, p] = c0_ref[d, p]
                n_scr[2 * d + p:2 * d + p + 1, :] = n0_ref[d, p:p + 1, :]
            else:
                c_scr[d, p] = jnp.zeros((LANES, LANES), F32)
                n_scr[2 * d + p:2 * d + p + 1, :] = jnp.zeros((1, LANES), F32)
    eye_h = (lax.broadcasted_iota(jnp.int32, (H_B, H_B), 0) == lax.broadcasted_iota(jnp.int32, (H_B, H_B), 1))

    def to_col(row):
        return jnp.sum(jnp.where(eye_h, row, 0.0), axis=1, keepdims=True)

    def to_row(col):
        return jnp.sum(jnp.where(eye_h, col, 0.0), axis=0, keepdims=True)

    for d in range(2):
        if has_state:
            m_scr[H_B * d:H_B * (d + 1), 0:1] = to_col(m0_ref[d:d + 1, :])
        else:
            m_scr[H_B * d:H_B * (d + 1), 0:1] = jnp.zeros((H_B, 1), F32)

    pad_scr[0:P0, :] = jnp.zeros((P0, C2), F32)
    pad_scr[P0 + T:P + T, :] = jnp.zeros((P - P0, C2), F32)

    def copy_in(i, carry):
        r0 = pl.multiple_of(i * L, L)
        pad_scr[pl.ds(P0 + r0, L), :] = qk_ref[pl.ds(r0, L), :]
        return carry

    lax.fori_loop(0, N, copy_in, 0)

    lane_c = lax.broadcasted_iota(jnp.int32, (1, C2), 1)
    qscale = jnp.where(lane_c < HK, scale, 1.0).astype(F32)
    sub = lax.broadcasted_iota(jnp.int32, (L, 1), 0)
    img_rows = (0,) if rows_img == 1 else (-1, 0, 1)

    def conv_tile(i, carry):
        r0 = pl.multiple_of(i * L, L)
        col = lax.rem(r0, grid_w) + sub
        ok_left = col >= 1
        ok_right = col <= grid_w - 2
        acc = jnp.zeros((L, C2), F32)
        for di in img_rows:
            blk = pad_scr[pl.ds(P0 + r0 + di * grid_w - 8, L + 16), :]
            left = jnp.where(ok_left, blk[7:7 + L, :], 0.0)
            mid = blk[8:8 + L, :]
            right = jnp.where(ok_right, blk[9:9 + L, :], 0.0)
            wr = 3 * (di + 1)
            acc = acc + left * cw_ref[wr:wr + 1, :] + mid * cw_ref[wr + 1:wr + 2, :] + right * cw_ref[wr + 2:wr + 3, :]
        qk_scr[pl.ds(r0, L), :] = _silu(acc) * qscale
        return carry

    lax.fori_loop(0, N, conv_tile, 0)

    gl = lane - GATE_LANE0
    is_f = ((gl >= H_B) & (gl < 2 * H_B)) | ((gl >= 3 * H_B) & (gl < 4 * H_B))

    def gate_tile(i, carry):
        rows = pl.ds(pl.multiple_of(i * L, L), L)
        x = sm_ref[rows, :] + bm_ref[...]
        y_scr[rows, :] = jnp.where(is_f, _log_sigmoid(x), x)
        return carry

    lax.fori_loop(0, N, gate_tile, 0)


    def state_group(ns):
        units = [u for n in ns for u in ((0, n), (1, N - 1 - n))]
        rows = [_chunk_rows(n) for _, n in units]
        kt_all = [[qk_scr[r, HK + p * LANES:HK + (p + 1) * LANES].T for p in range(n_pairs)] for r in rows]
        xs = [y_scr[r, :] for r in rows]
        fsum = [_tri_sum(tri[d], x) for (d, _), x in zip(units, xs)]
        wk_all, f_end, c_end = [], [], []
        for (d, n), r, x, fs in zip(units, rows, xs, fsum):
            y = jnp.where(is_f, fs, x)
            li0 = GATE_LANE0 + 2 * H_B * d
            blk = jnp.concatenate([y, y], axis=0).T[li0:li0 + 2 * H_B, :]
            frow = pltpu.roll(blk, H_B, axis=0)
            grow = blk - frow
            g_scr[d, n] = grow
            f_scr[d, n] = frow
            e_col = L - 1 if d == 0 else 0
            f_end.append(frow[0:H_B, e_col:e_col + 1])
            ce8 = jnp.max(grow, axis=1, keepdims=True)
            c_end.append(ce8[0:H_B, :])
            wk_all.append(jnp.exp(grow[:, 0:L] - ce8))
        kv_all, ksum_all = [], []
        for r, wk8, kt_u in zip(rows, wk_all, kt_all):
            kv_u, ks_u = [], []
            wk8b = wk8.astype(BF16)
            for p in range(n_pairs):
                kpb = qk_scr[r, HK + p * LANES:HK + (p + 1) * LANES].astype(BF16)
                ks8 = _dot(wk8b, kpb)
                for j in range(2):
                    h = 2 * p + j
                    kwt = (kt_u[p][j * DK:(j + 1) * DK, :] * wk8[h:h + 1, :]).astype(BF16)
                    kv_u.append(_dot(kwt, v_ref[r, h * DV:(h + 1) * DV].astype(BF16)))
                    ks_u.append(ks8[h:h + 1, :])
            kv_all.append(kv_u)
            ksum_all.append(ks_u)
        m_run = [m_scr[H_B * d:H_B * (d + 1), 0:1] for d in range(2)]
        a_all, b_all = [], []
        for (d, n), fe, ce in zip(units, f_end, c_end):
            mall_scr[d, n, 0:H_B, 0:1] = m_run[d]
            mx = jnp.maximum(m_run[d], ce)
            a_all.append(jnp.exp(m_run[d] - mx))
            b_all.append(jnp.exp(ce - mx))
            m_run[d] = fe + mx
        for d in range(2):
            m_scr[H_B * d:H_B * (d + 1), 0:1] = m_run[d]
        c_run = [[[c_scr[d, p, j * DK:(j + 1) * DK, :] for j in range(2)] for p in range(n_pairs)] for d in range(2)]
        n_run = [[n_scr[2 * d + p:2 * d + p + 1, :] for p in range(n_pairs)] for d in range(2)]
        for (d, n), a4, b4, kv_u, ks_u in zip(units, a_all, b_all, kv_all, ksum_all):
            for p in range(n_pairs):
                nall_scr[d, n, p:p + 1, :] = n_run[d][p]
                a_s = [a4[2 * p + j:2 * p + j + 1, :] for j in range(2)]
                b_s = [b4[2 * p + j:2 * p + j + 1, :] for j in range(2)]
                for j in range(2):
                    cj = c_run[d][p][j]
                    call_scr[d, n, p, j * DK:(j + 1) * DK, :] = cj.astype(BF16)
                    c_run[d][p][j] = a_s[j] * cj + b_s[j] * kv_u[2 * p + j]
                n_run[d][p] = (jnp.where(head_mask[0], a_s[0], a_s[1]) * n_run[d][p]
                               + jnp.where(head_mask[0], b_s[0] * ks_u[2 * p], b_s[1] * ks_u[2 * p + 1]))
        for d in range(2):
            for p in range(n_pairs):
                n_scr[2 * d + p:2 * d + p + 1, :] = n_run[d][p]
                for j in range(2):
                    c_scr[d, p, j * DK:(j + 1) * DK, :] = c_run[d][p][j]

    _chunk_loop(N, unroll, state_group)

    eye = lower & upper
    ones8 = jnp.ones((8, L), BF16)
    sub8 = lax.broadcasted_iota(jnp.int32, (8, LANES), 0)
    sub_h = lax.broadcasted_iota(jnp.int32, (H_B, L), 0)
    n_rows = [((sub8 == 2 * p) & head_mask[0]) | ((sub8 == 2 * p + 1) & head_mask[1]) for p in range(n_pairs)]

    def head_rows(vals):
        out = vals[0][0:H_B, :]
        for h in range(1, H_B):
            out = jnp.where(sub_h == h, vals[h][0:H_B, :], out)
        return out

    def out_group(ns):
        chunks = [(d, n) for n in ns for d in range(2)]
        pairs = [(d, n, p) for d, n in chunks for p in range(n_pairs)]
        units = [(d, n, p, j) for d, n, p in pairs for j in range(2)]
        cms = [lane_cummax(g_scr[d, n], d)[0:H_B, 0:L] for d, n in chunks]
        qk2s, qc2s, qn2s = [], [], []
        for d, n, p in pairs:
            r = _chunk_rows(n)
            qp = qk_scr[r, p * LANES:(p + 1) * LANES]
            q2 = jnp.concatenate([jnp.where(head_mask[j], qp, 0.0) for j in range(2)], axis=0).astype(BF16)
            qk2s.append(_dot_nt(q2, qk_scr[r, HK + p * LANES:HK + (p + 1) * LANES].astype(BF16)))
            qc2s.append(_dot(q2, call_scr[d, n, p]))
            nsel = jnp.where(n_rows[p], nall_scr[d, n, p:p + 1, :], 0.0).astype(BF16)
            qn2s.append(_dot_nt(nsel, qp.astype(BF16)))
        s_all = []
        for ui, (d, n, p, j) in enumerate(units):
            grow = g_scr[d, n, 2 * p + j:2 * p + j + 1, 0:L]
            e = jnp.where(tmask[d], grow, -jnp.inf)
            cmax = jnp.max(e, axis=-1, keepdims=True)
            s_all.append((qk2s[ui // 2][j * L:(j + 1) * L, :] * jnp.exp(e - cmax)).astype(BF16))
        nums = [_dot(s, v_ref[_chunk_rows(n), (2 * p + j) * DV:(2 * p + j + 1) * DV].astype(BF16))
                for (d, n, p, j), s in zip(units, s_all)]
        dens = [_dot_nt(ones8, s) for s in s_all]
        scales = []
        for ci, (d, n) in enumerate(chunks):
            den_loc = head_rows(dens[ci * H_B:(ci + 1) * H_B])
            qn = qn2s[ci * n_pairs][0:H_B, :]
            for p in range(1, n_pairs):
                qn = qn + qn2s[ci * n_pairs + p][0:H_B, :]
            cm = cms[ci]
            m_prev = mall_scr[d, n, 0:H_B, 0:1]
            delta = cm - m_prev
            t = jnp.exp(-jnp.abs(delta))
            w_loc = jnp.where(delta <= 0.0, t, 1.0)
            w_inter = jnp.where(delta <= 0.0, 1.0, t)
            mt = f_scr[d, n, 0:H_B, 0:L] + jnp.maximum(m_prev, cm)
            den = w_loc * den_loc + w_inter * qn
            rinv = 1.0 / jnp.maximum(jnp.abs(den), jnp.exp(-mt))
            scales.append((w_loc * rinv, w_inter * rinv))
        hs = []
        for ui, (d, n, p, j) in enumerate(units):
            h = 2 * p + j
            sc_loc, sc_inter = scales[ui // H_B]
            d_loc = jnp.where(eye, sc_loc[h:h + 1, :], 0.0).astype(BF16)
            d_inter = jnp.where(eye, sc_inter[h:h + 1, :], 0.0).astype(BF16)
            hs.append(_dot(d_loc, nums[ui].astype(BF16))
                      + _dot(d_inter, qc2s[ui // 2][j * L:(j + 1) * L, :].astype(BF16)))
        for ni, n in enumerate(ns):
            r = _chunk_rows(n)
            for h in range(H_B):
                vs = slice(h * DV, (h + 1) * DV)
                o = hs[(2 * ni) * H_B + h] + hs[(2 * ni + 1) * H_B + h]
                out_ref[r, vs] = (_rms(o, gw_ref[:, vs]) * _sigmoid(og_ref[r, vs])).astype(out_ref.dtype)

    _chunk_loop(N, unroll, out_group)

    if write_state:
        for d in range(2):
            for p in range(n_pairs):
                cnew_ref[d, p] = c_scr[d, p]
                nnew_ref[d, p:p + 1, :] = n_scr[2 * d + p:2 * d + p + 1, :]
            mnew_ref[d:d + 1, :] = to_row(m_scr[H_B * d:H_B * (d + 1), 0:1])


def _mlstm_scratch(T, C2, grid_w):
    n_pairs = C2 // 2 // LANES
    n_chunks = T // CHUNK
    pad_rows = 2 * (grid_w + 8) if T // grid_w > 1 else 16
    return [
        pltpu.VMEM((T + pad_rows, C2), F32),
        pltpu.VMEM((T, C2), F32),
        pltpu.VMEM((T, SMALL_W), F32),
        pltpu.VMEM((2, n_pairs, LANES, LANES), F32),
        pltpu.VMEM((8, LANES), F32),
        pltpu.VMEM((8, LANES), F32),
        pltpu.VMEM((2, n_chunks, n_pairs, LANES, LANES), BF16),
        pltpu.VMEM((2, n_chunks, 8, LANES), F32),
        pltpu.VMEM((2, n_chunks, 8, LANES), F32),
        pltpu.VMEM((2, n_chunks, 8, LANES), F32),
        pltpu.VMEM((2, n_chunks, 8, LANES), F32),
    ]


N_GLA_SCRATCH = 7
N_MLSTM_SCRATCH = 11


def _scan_kernel(*refs, cols, has_state, write_state, n_cast, grid_w, unroll):
    refs = list(refs)
    z_ref = refs.pop(0)
    s0_ref = c0_ref = n0_ref = m0_ref = None
    if has_state:
        s0_ref, c0_ref, n0_ref, m0_ref = refs[:4]
        del refs[:4]
    wal_ref, bal_ref, gwa_ref, cw_ref, bm_ref, gwb_ref = refs[:6]
    del refs[:6]
    cast_in = refs[:n_cast]
    del refs[:n_cast]
    outa_ref, outb_ref = refs[:2]
    del refs[:2]
    snew_ref = cnew_ref = nnew_ref = mnew_ref = None
    if write_state:
        snew_ref, cnew_ref, nnew_ref, mnew_ref = refs[:4]
        del refs[:4]
    cast_out = refs[:n_cast]
    del refs[:n_cast]
    gla_scr = refs[:N_GLA_SCRATCH]
    mlstm_scr = refs[N_GLA_SCRATCH:]

    for src, dst in zip(cast_in, cast_out):
        dst[...] = src[...].astype(BF16)

    def view(name):
        c0, w = cols[name]
        return z_ref.at[:, pl.ds(c0, w)]

    sm_ref = view("small")
    _gla_body(view("qa"), view("ka"), view("va"), view("ga"), sm_ref, s0_ref, wal_ref, bal_ref, gwa_ref,
              outa_ref, snew_ref, *gla_scr, unroll=unroll)
    _mlstm_body(view("qkb"), view("vb"), view("ob"), sm_ref, c0_ref, n0_ref, m0_ref, cw_ref, bm_ref, gwb_ref,
                outb_ref, cnew_ref, nnew_ref, mnew_ref, *mlstm_scr, grid_w=grid_w, unroll=unroll)


def _scan_call(z2d, row0, B, T, states, lw, *, grid_w, write_state, casts=()):
    n_z = z2d.shape[1]
    assert row0 % T == 0 and z2d.shape[0] % T == 0
    z3 = z2d.reshape(z2d.shape[0] // T, T, n_z)
    blk0 = row0 // T
    HK = lw["wal_p"].shape[1] // 2
    DA = lw["gnorm_a_w"].shape[0]
    C2 = lw["conv9"].shape[1]
    DB = lw["gnorm_b_w"].shape[0]
    pa, pb = HK // LANES, C2 // 2 // LANES
    n_chunks = T // CHUNK
    has_state = states is not None
    widths = (("qa", HK), ("ka", HK), ("va", DA), ("ga", DA), ("qkb", C2), ("vb", DB), ("ob", DB),
              ("small", SMALL_W))
    cols, c0 = {}, 0
    for name, w in widths:
        cols[name] = (c0, w)
        c0 += w
    assert c0 == n_z
    cast_in_specs, cast_out_specs, cast_out_shape, cast_args = _cast_specs(casts, B)
    kern = functools.partial(_scan_kernel, cols=cols, has_state=has_state, write_state=write_state,
                             n_cast=len(casts), grid_w=grid_w, unroll=min(n_chunks, SCAN_UNROLL))

    def per_batch(shape):
        nd = len(shape)
        return pl.BlockSpec((None,) + tuple(shape), lambda b: (b,) + (0,) * nd)

    def whole(a):
        return pl.BlockSpec(a.shape, lambda b: (0,) * a.ndim)

    state_shapes = ((2, pa, LANES, LANES), (2, pb, LANES, LANES), (2, pb, LANES), (2, H_B))
    in_specs = [pl.BlockSpec((None, T, n_z), lambda b: (b + blk0, 0, 0))]
    args = [z3]
    if has_state:
        s_gla, s_c, s_n, s_m = states
        args += [s_gla.reshape((B,) + state_shapes[0]), s_c.reshape((B,) + state_shapes[1]),
                 s_n.reshape((B,) + state_shapes[2]), s_m]
        in_specs += [per_batch(s) for s in state_shapes]
    small = [lw["wal_p"], lw["bal_p"], lw["gnorm_a_w"].reshape(1, DA), lw["conv9"], lw["bm_row"],
             lw["gnorm_b_w"].reshape(1, DB)]
    args += small + cast_args
    in_specs += [whole(a) for a in small] + cast_in_specs
    out_specs = [per_batch((T, DA)), per_batch((T, DB))]
    out_shape = [jax.ShapeDtypeStruct((B, T, DA), BF16), jax.ShapeDtypeStruct((B, T, DB), BF16)]
    if write_state:
        out_specs += [per_batch(s) for s in state_shapes]
        out_shape += [jax.ShapeDtypeStruct((B,) + s, F32) for s in state_shapes]
    out_specs += cast_out_specs
    out_shape += cast_out_shape
    scratch = _gla_scratch(T, HK, DA) + _mlstm_scratch(T, C2, grid_w)
    assert len(scratch) == N_GLA_SCRATCH + N_MLSTM_SCRATCH
    return pl.pallas_call(
        kern,
        grid=(B,),
        in_specs=in_specs,
        out_specs=out_specs,
        out_shape=out_shape,
        scratch_shapes=scratch,
        compiler_params=pltpu.CompilerParams(dimension_semantics=("arbitrary",),
                                             vmem_limit_bytes=VMEM_LIMIT),
        name="mixer_scans",
    )(*args)


def _outff_kernel(xc_ref, xl_ref, ac_ref, al_ref, bc_ref, bl_ref, mod_ref, n2_ref, fn_ref, wo_ref, w1_ref, w2_ref,
                  yc_ref, yl_ref, *, n_ctx, tiles_per_req, ff_chunk, final_norm):
    D = xc_ref.shape[1]
    DA = ac_ref.shape[1]
    is_ctx, row = _tile_group(n_ctx, tiles_per_req)

    def mod(k):
        return mod_ref[pl.ds(row, 1), k * D:(k + 1) * D]

    def tile(x_ref, a_ref, b_ref, y_ref):
        y = _dot(a_ref[...], wo_ref[0:DA, :]) + _dot(b_ref[...], wo_ref[DA:, :])
        x1 = x_ref[...] + mod(2) * y
        h2 = (_rms(x1, n2_ref[...]) * (1.0 + mod(4)) + mod(3)).astype(BF16)
        acc = jnp.zeros(x1.shape, F32)
        for c0 in range(0, w1_ref.shape[1], ff_chunk):
            u = jnp.maximum(_dot(h2, w1_ref[:, c0:c0 + ff_chunk]), 0.0)
            acc = acc + _dot((u * u).astype(BF16), w2_ref[c0:c0 + ff_chunk, :])
        x2 = x1 + mod(5) * acc
        y_ref[...] = _rms(x2, fn_ref[...]) if final_norm else x2

    @pl.when(is_ctx)
    def _():
        tile(xc_ref, ac_ref, bc_ref, yc_ref)

    @pl.when(jnp.logical_not(is_ctx))
    def _():
        tile(xl_ref, al_ref, bl_ref, yl_ref)


def _outff_call(xc2d, xl2d, ac, al, bc, bl, mod, norm2_w, final_w, wo, w1, w2, *, tm, tiles_per_req, final_norm):
    (Mc, D), Ml = xc2d.shape, xl2d.shape[0]
    n_ctx = Mc // tm
    DA = ac.shape[1]
    DFF = w1.shape[1]
    kern = functools.partial(_outff_kernel, n_ctx=n_ctx, tiles_per_req=tiles_per_req, ff_chunk=512,
                             final_norm=final_norm)
    once = pl.Buffered(1)
    ctx, lat = _ctx_tile(n_ctx), _lat_tile(n_ctx)
    return pl.pallas_call(
        kern,
        grid=((Mc + Ml) // tm,),
        in_specs=[
            pl.BlockSpec((tm, D), ctx), pl.BlockSpec((tm, D), lat),
            pl.BlockSpec((tm, DA), ctx), pl.BlockSpec((tm, DA), lat),
            pl.BlockSpec((tm, D - DA), ctx), pl.BlockSpec((tm, D - DA), lat),
            pl.BlockSpec(mod.shape, lambda i: (0, 0)),
            pl.BlockSpec((1, D), lambda i: (0, 0)),
            pl.BlockSpec((1, D), lambda i: (0, 0)),
            pl.BlockSpec((D, D), lambda i: (0, 0), pipeline_mode=once),
            pl.BlockSpec((D, DFF), lambda i: (0, 0), pipeline_mode=once),
            pl.BlockSpec((DFF, D), lambda i: (0, 0), pipeline_mode=once),
        ],
        out_specs=[pl.BlockSpec((tm, D), ctx), pl.BlockSpec((tm, D), lat)],
        out_shape=[jax.ShapeDtypeStruct((Mc, D), F32), jax.ShapeDtypeStruct((Ml, D), F32)],
        compiler_params=pltpu.CompilerParams(dimension_semantics=("arbitrary",),
                                             vmem_limit_bytes=VMEM_LIMIT),
        name="outproj_mlp",
    )(xc2d, xl2d, ac, al, bc, bl, mod, norm2_w.reshape(1, D), final_w.reshape(1, D), wo, w1, w2)


def _layer(xc, xl, mod, cached, lw, ffw, final_w, final_norm):
    (Bc, Tc, D), (Bl, Tl, _) = xc.shape, xl.shape
    tm = TOKEN_TILE
    assert (Bc * Tc) % tm == 0 and Tl % tm == 0 and (Bc * Tc) % Tl == 0
    xc2d, xl2d = xc.reshape(Bc * Tc, D), xl.reshape(Bl * Tl, D)
    z = _inproj_call(xc2d, xl2d, mod, lw["norm1_w"], lw["w_in_t"], tm=tm, tiles_per_req=Tl // tm,
                     big_rows=lw["big_rows"], small_rows=lw["small_rows"])
    res_c = _scan_call(z, 0, Bc, Tc, None, lw, grid_w=Tc, write_state=True,
                       casts=((ffw[0], 0), (ffw[1], 1), (ffw[2], 0)))
    res_l = _scan_call(z, Bc * Tc, Bl, Tl, cached, lw, grid_w=GRID_W, write_state=False)
    yc, yl = _outff_call(xc2d, xl2d, res_c[0].reshape(Bc * Tc, -1), res_l[0].reshape(Bl * Tl, -1),
                         res_c[1].reshape(Bc * Tc, -1), res_l[1].reshape(Bl * Tl, -1), mod, lw["norm2_w"],
                         final_w, *res_c[-3:], tm=tm, tiles_per_req=Tl // tm, final_norm=final_norm)
    return yc.reshape(Bc, Tc, D), yl.reshape(Bl, Tl, D), tuple(res_c[2:6])


def _layer_weights(l, norm1_w, norm2_w, w_in, w_alpha2, b_alpha, b_mgate, conv_w, gnorm_a_w, gnorm_b_w):
    hk_a = w_alpha2.shape[-1]
    d_a = gnorm_a_w.shape[-1]
    d_b = gnorm_b_w.shape[-1]
    hk_b = conv_w.shape[-1] // 2
    sizes = (hk_a, hk_a, d_a, d_a, 2 * R_ALPHA, hk_b, hk_b, d_b, d_b, 4 * H_B)
    offs = [0]
    for s in sizes:
        offs.append(offs[-1] + s)
    big_rows = ((offs[0], offs[4] - offs[0]), (offs[5], offs[9] - offs[5]))
    small_rows = ((offs[4], offs[5] - offs[4]), (offs[9], offs[10] - offs[9]))
    assert all(n % LANES == 0 and r % 16 == 0 for r, n in big_rows)
    wal = w_alpha2[l]
    wal_p = jnp.zeros((SMALL_W, 2 * hk_a), F32)
    wal_p = wal_p.at[0:R_ALPHA, 0:hk_a].set(wal[0]).at[R_ALPHA:2 * R_ALPHA, hk_a:].set(wal[1]).astype(BF16)
    bm_row = jnp.zeros((1, SMALL_W), F32).at[0, GATE_LANE0:GATE_LANE0 + 4 * H_B].set(b_mgate[l].reshape(-1))
    return dict(
        norm1_w=norm1_w[l], norm2_w=norm2_w[l], w_in_t=jnp.swapaxes(w_in[l], 0, 1),
        big_rows=big_rows, small_rows=small_rows, wal_p=wal_p,
        bal_p=b_alpha[l].reshape(1, -1), bm_row=bm_row,
        conv9=conv_w[l].reshape(-1, conv_w.shape[-1]),
        gnorm_a_w=gnorm_a_w[l], gnorm_b_w=gnorm_b_w[l],
    )


def kernel(x_prompt, x_sample, c, state_gla, state_mlstm_C, state_mlstm_n, state_mlstm_m, c_ctx, w_ada, b_ada, norm1_w, norm2_w, w_in, w_alpha2, b_alpha, b_mgate, conv_w, gnorm_a_w, gnorm_b_w, w_out, w_ff1, w_ff2, final_norm_w):
    depth = w_in.shape[0]
    D = x_prompt.shape[-1]
    Bp, Tp, _ = x_prompt.shape
    Bs = x_sample.shape[0]
    assert 1 + Bs <= COND_ROWS
    cond = jnp.concatenate([c_ctx[None, :], c, jnp.zeros((COND_ROWS - 1 - Bs, D), F32)], axis=0)
    xp, xs = x_prompt, x_sample
    s_gla, s_c, s_n, s_m = [], [], [], []
    for l in range(depth):
        lw = _layer_weights(l, norm1_w, norm2_w, w_in, w_alpha2, b_alpha, b_mgate, conv_w,
                            gnorm_a_w, gnorm_b_w)
        mod = _ada_call(cond, w_ada[l], b_ada[l])
        cached = (state_gla[:, l], state_mlstm_C[:, l], state_mlstm_n[:, l], state_mlstm_m[:, l])
        xp, xs, ctx = _layer(xp, xs, mod, cached, lw, (w_out[l], w_ff1[l], w_ff2[l]), final_norm_w,
                             l == depth - 1)
        s_gla.append(ctx[0].reshape(Bp, 2, H_A, -1, ctx[0].shape[-1]))
        s_c.append(ctx[1].reshape(Bp, 2, H_B, -1, ctx[1].shape[-1]))
        s_n.append(ctx[2].reshape(Bp, 2, H_B, -1))
        s_m.append(ctx[3])
    dt = x_prompt.dtype
    return (xp, xs, jnp.stack(s_gla, axis=1).astype(dt), jnp.stack(s_c, axis=1).astype(dt),
            jnp.stack(s_n, axis=1).astype(dt), jnp.stack(s_m, axis=1).astype(dt))
```

```python
import functools

import jax
import jax.numpy as jnp
from jax import lax
from jax.experimental import pallas as pl
from jax.experimental.pallas import tpu as pltpu

F32 = jnp.float32
BF16 = jnp.bfloat16

GRID_W = 64
H_A = 4
H_B = 4
R_ALPHA = 16
TAU_GLA = 16.0
CHUNK = 64
EPS = 1e-6
LANES = 128
COND_ROWS = 8
SMALL_W = LANES
GATE_LANE0 = 2 * R_ALPHA
VMEM_LIMIT = 56 * 1024 * 1024
SCAN_UNROLL = 4
TOKEN_TILE = 512


def _sigmoid(x):
    return 1.0 / (1.0 + jnp.exp(-x))


def _silu(x):
    return x * _sigmoid(x)


def _log_sigmoid(x):
    return jnp.minimum(x, 0.0) - jnp.log1p(jnp.exp(-jnp.abs(x)))


def _dot(a, b):
    return jnp.dot(a, b, preferred_element_type=F32)


def _dot_nt(a, b):
    return lax.dot_general(a, b, (((1,), (1,)), ((), ())), preferred_element_type=F32)


def _rms(x, w):
    return x * lax.rsqrt(jnp.mean(x * x, axis=-1, keepdims=True) + EPS) * w


def _tri_sum(tri, x):
    hi = x.astype(BF16)
    r1 = x - hi.astype(F32)
    mid = r1.astype(BF16)
    lo = (r1 - mid.astype(F32)).astype(BF16)
    return _dot(tri, hi) + _dot(tri, mid) + _dot(tri, lo)


def _chunk_masks(L):
    row = lax.broadcasted_iota(jnp.int32, (L, L), 0)
    col = lax.broadcasted_iota(jnp.int32, (L, L), 1)
    lower = row >= col
    upper = row <= col
    return lower, upper


def _ada_kernel(c_ref, w_ref, b_ref, o_ref):
    s = _silu(c_ref[...])
    o_ref[...] = _dot(s.astype(BF16), w_ref[...].astype(BF16)) + b_ref[...]


def _ada_call(cond, w_ada, b_ada):
    D = cond.shape[1]
    n_out = w_ada.shape[1]
    tn = 2048 if n_out % 2048 == 0 else 1024
    return pl.pallas_call(
        _ada_kernel,
        grid=(n_out // tn,),
        in_specs=[
            pl.BlockSpec((COND_ROWS, D), lambda j: (0, 0)),
            pl.BlockSpec((D, tn), lambda j: (0, j)),
            pl.BlockSpec((1, tn), lambda j: (0, j)),
        ],
        out_specs=pl.BlockSpec((COND_ROWS, tn), lambda j: (0, j)),
        out_shape=jax.ShapeDtypeStruct((COND_ROWS, n_out), F32),
        compiler_params=pltpu.CompilerParams(dimension_semantics=("arbitrary",),
                                             vmem_limit_bytes=VMEM_LIMIT),
        name="ada_mod",
    )(cond, w_ada, b_ada.reshape(1, n_out))


def _tile_group(n_ctx, tiles_per_req):
    i = pl.program_id(0)
    is_ctx = i < n_ctx
    row = jnp.where(is_ctx, 0, 1 + jnp.maximum(i - n_ctx, 0) // tiles_per_req)
    return is_ctx, row


def _ctx_tile(n_ctx):
    return lambda i: (jnp.minimum(i, n_ctx - 1), 0)


def _lat_tile(n_ctx):
    return lambda i: (jnp.maximum(i - n_ctx, 0), 0)


def _inproj_kernel(xc_ref, xl_ref, mod_ref, nw_ref, wt_ref, z_ref, wb_scr, *, n_ctx, tiles_per_req, big_rows,
                   small_rows):
    D = xc_ref.shape[1]

    @pl.when(pl.program_id(0) == 0)
    def _():
        col = 0
        for r0, n in big_rows:
            for k in range(n // LANES):
                blk = wt_ref[r0 + k * LANES:r0 + (k + 1) * LANES, :]
                wb_scr[:, col:col + LANES] = blk.T.astype(BF16)
                col += LANES
        parts = [wt_ref[r0:r0 + n, :] for r0, n in small_rows]
        n_small = sum(n for _, n in small_rows)
        parts.append(jnp.zeros((SMALL_W - n_small, D), F32))
        wb_scr[:, col:col + SMALL_W] = jnp.concatenate(parts, axis=0).T.astype(BF16)

    is_ctx, row = _tile_group(n_ctx, tiles_per_req)

    def tile(x_ref):
        sh1 = mod_ref[pl.ds(row, 1), 0:D]
        sc1 = mod_ref[pl.ds(row, 1), D:2 * D]
        h = _rms(x_ref[...], nw_ref[...]) * (1.0 + sc1) + sh1
        z_ref[...] = _dot(h.astype(BF16), wb_scr[...])

    @pl.when(is_ctx)
    def _():
        tile(xc_ref)

    @pl.when(jnp.logical_not(is_ctx))
    def _():
        tile(xl_ref)


def _inproj_call(xc2d, xl2d, mod, norm_w, w_in_t, *, tm, tiles_per_req, big_rows, small_rows):
    (Mc, D), Ml = xc2d.shape, xl2d.shape[0]
    n_ctx = Mc // tm
    n_out = sum(n for _, n in big_rows) + SMALL_W
    kern = functools.partial(_inproj_kernel, n_ctx=n_ctx, tiles_per_req=tiles_per_req,
                             big_rows=big_rows, small_rows=small_rows)
    return pl.pallas_call(
        kern,
        grid=((Mc + Ml) // tm,),
        in_specs=[
            pl.BlockSpec((tm, D), _ctx_tile(n_ctx)),
            pl.BlockSpec((tm, D), _lat_tile(n_ctx)),
            pl.BlockSpec(mod.shape, lambda i: (0, 0)),
            pl.BlockSpec((1, D), lambda i: (0, 0)),
            pl.BlockSpec(w_in_t.shape, lambda i: (0, 0), pipeline_mode=pl.Buffered(1)),
        ],
        out_specs=pl.BlockSpec((tm, n_out), lambda i: (i, 0)),
        out_shape=jax.ShapeDtypeStruct((Mc + Ml, n_out), F32),
        scratch_shapes=[pltpu.VMEM((D, n_out), BF16)],
        compiler_params=pltpu.CompilerParams(dimension_semantics=("arbitrary",),
                                             vmem_limit_bytes=VMEM_LIMIT),
        name="norm_inproj",
    )(xc2d, xl2d, mod, norm_w.reshape(1, D), w_in_t)


def _chunk_loop(n_chunks, unroll, stage_fns):
    def step(ns):
        gens = [fn(ns) for fn in stage_fns]
        while gens:
            alive = []
            for g in gens:
                try:
                    next(g)
                    alive.append(g)
                except StopIteration:
                    pass
            gens = alive

    if unroll >= n_chunks:
        step(list(range(n_chunks)))
        return

    def body(i, carry):
        step([i * unroll + u for u in range(unroll)])
        return carry

    lax.fori_loop(0, n_chunks // unroll, body, 0)


def _chunk_rows(n):
    if isinstance(n, int):
        return pl.ds(n * CHUNK, CHUNK)
    return pl.ds(pl.multiple_of(n * CHUNK, CHUNK), CHUNK)


def _cast_specs(casts, n_steps):
    in_specs, out_specs, out_shape, args = [], [], [], []
    for w, axis in casts:
        blk = list(w.shape)
        assert blk[axis] % n_steps == 0
        blk[axis] //= n_steps
        assert blk[0] % 16 == 0 and blk[1] % LANES == 0
        idx = (lambda b: (b, 0)) if axis == 0 else (lambda b: (0, b))
        in_specs.append(pl.BlockSpec(tuple(blk), idx))
        out_specs.append(pl.BlockSpec(tuple(blk), idx))
        out_shape.append(jax.ShapeDtypeStruct(w.shape, BF16))
        args.append(w)
    return in_specs, out_specs, out_shape, args


def _gla_body(q_ref, k_ref, v_ref, g_ref, sm_ref, s0_ref, wal_ref, bal_ref, gw_ref, out_ref, snew_ref,
              st_scr, sall_scr, qh_scr, qs_scr, kh_scr):
    has_state = s0_ref is not None
    write_state = snew_ref is not None
    T = q_ref.shape[0]
    L = CHUNK
    N = T // L
    HK = q_ref.shape[1]
    DK = HK // H_A
    DV = v_ref.shape[1] // H_A
    scale = DK ** -0.5
    n_pairs = HK // LANES

    lower, upper = _chunk_masks(L)
    tri = (lower.astype(BF16), upper.astype(BF16))
    tmask = (lower, upper)
    lane = lax.broadcasted_iota(jnp.int32, (1, LANES), 1)
    head_mask = (lane < DK, lane >= DK)

    for d in range(2):
        for p in range(n_pairs):
            if has_state:
                st_scr[d, p] = s0_ref[d, p].T
            else:
                st_scr[d, p] = jnp.zeros((LANES, LANES), F32)

    def state_group(ns):
        units = [u for n in ns for u in ((0, n), (1, N - 1 - n))]
        rows = [_chunk_rows(n) for _, n in units]
        vt_all = [[jnp.concatenate([v_ref[r, (2 * p + j) * DV:(2 * p + j + 1) * DV] for j in range(2)],
                                   axis=0).T.astype(BF16) for p in range(n_pairs)] for r in rows]
        yield
        pre = [_dot(sm_ref[r, :].astype(BF16), wal_ref[:, d * HK:(d + 1) * HK]) + bal_ref[:, d * HK:(d + 1) * HK]
               for (d, _), r in zip(units, rows)]
        yield
        g = [_log_sigmoid(x) * (1.0 / TAU_GLA) for x in pre]
        yield
        b = [_tri_sum(tri[d], gi) for (d, _), gi in zip(units, g)]
        yield
        ks_all, dec_all = [], []
        for (d, _), r, bi in zip(units, rows, b):
            bend = bi[L - 1:L, :] if d == 0 else bi[0:1, :]
            q = q_ref[r, :] * scale
            ks = (k_ref[r, :] * jnp.exp(bend - bi)).astype(BF16)
            qh_scr[d, r, :] = (q * jnp.exp(bi - bend)).astype(BF16)
            qs_scr[d, r, :] = (q * jnp.exp(bi)).astype(BF16)
            kh_scr[d, r, :] = ks
            ks_all.append(ks)
            dec_all.append(jnp.exp(bend))
        yield
        upd_all = []
        for vt_u, ks in zip(vt_all, ks_all):
            upd_u = []
            for p in range(n_pairs):
                kp = ks[:, p * LANES:(p + 1) * LANES]
                kk = jnp.concatenate([jnp.where(head_mask[j], kp, jnp.zeros_like(kp)) for j in range(2)], axis=0)
                upd_u.append(_dot(vt_u[p], kk))
            upd_all.append(upd_u)
        yield
        st = [[st_scr[d, p] for p in range(n_pairs)] for d in range(2)]
        for (d, n), dec, upd in zip(units, dec_all, upd_all):
            for p in range(n_pairs):
                sall_scr[d, n, p] = st[d][p].astype(BF16)
                st[d][p] = st[d][p] * dec[:, p * LANES:(p + 1) * LANES] + upd[p]
        for d in range(2):
            for p in range(n_pairs):
                st_scr[d, p] = st[d][p]

    def out_group(ns):
        pairs = [(d, ni, p) for ni in range(len(ns)) for d in range(2) for p in range(n_pairs)]
        scores, inter = [], []
        for d, ni, p in pairs:
            r = _chunk_rows(ns[ni])
            ls = slice(p * LANES, (p + 1) * LANES)
            qh = qh_scr[d, r, ls]
            qs = qs_scr[d, r, ls]
            zero = jnp.zeros_like(qh)
            q2 = jnp.concatenate([jnp.where(head_mask[j], qh, zero) for j in range(2)], axis=0)
            qs2 = jnp.concatenate([jnp.where(head_mask[j], qs, zero) for j in range(2)], axis=0)
            scores.append(_dot_nt(q2, kh_scr[d, r, ls]))
            inter.append(_dot_nt(qs2, sall_scr[d, ns[ni], p]))
        yield
        probs = [[jnp.where(tmask[d], sc[j * L:(j + 1) * L, :], 0.0).astype(BF16) for j in range(2)]
                 for (d, _, _), sc in zip(pairs, scores)]
        yield
        outs = {}
        for (d, ni, p), pr, it in zip(pairs, probs, inter):
            r = _chunk_rows(ns[ni])
            for j in range(2):
                vs = slice((2 * p + j) * DV, (2 * p + j + 1) * DV)
                outs[(d, ni, 2 * p + j)] = _dot(pr[j], v_ref[r, vs].astype(BF16)) + it[j * L:(j + 1) * L, :]
        yield
        for ni, n in enumerate(ns):
            r = _chunk_rows(n)
            for h in range(H_A):
                vs = slice(h * DV, (h + 1) * DV)
                o = outs[(0, ni, h)] + outs[(1, ni, h)]
                out_ref[r, vs] = (_rms(o, gw_ref[:, vs]) * _silu(g_ref[r, vs])).astype(out_ref.dtype)

    def finish():
        if write_state:
            for d in range(2):
                for p in range(n_pairs):
                    snew_ref[d, p] = st_scr[d, p].T

    return state_group, out_group, finish


def _gla_scratch(T, HK):
    n_pairs = HK // LANES
    n_chunks = T // CHUNK
    return [
        pltpu.VMEM((2, n_pairs, LANES, LANES), F32),
        pltpu.VMEM((2, n_chunks, n_pairs, LANES, LANES), BF16),
        pltpu.VMEM((2, T, HK), BF16),
        pltpu.VMEM((2, T, HK), BF16),
        pltpu.VMEM((2, T, HK), BF16),
    ]


def _mlstm_body(qk_ref, v_ref, og_ref, sm_ref, c0_ref, n0_ref, m0_ref, cw_ref, bm_ref, gw_ref,
                out_ref, cnew_ref, nnew_ref, mnew_ref,
                pad_scr, qk_scr, y_scr, c_scr, n_scr, m_scr, call_scr, nall_scr, mall_scr, g_scr, f_scr,
                *, grid_w):
    has_state = c0_ref is not None
    write_state = cnew_ref is not None
    T = qk_ref.shape[0]
    L = CHUNK
    N = T // L
    C2 = qk_ref.shape[1]
    HK = C2 // 2
    DK = HK // H_B
    DV = v_ref.shape[1] // H_B
    scale = DK ** -0.5
    n_pairs = HK // LANES
    P = pad_scr.shape[0] - T
    P0 = P // 2
    rows_img = T // grid_w

    lower, upper = _chunk_masks(L)
    tri = (lower.astype(BF16), upper.astype(BF16))
    tmask = (lower, upper)
    lane = lax.broadcasted_iota(jnp.int32, (1, LANES), 1)
    head_mask = (lane < DK, lane >= DK)
    lane_in = lane & (L - 1)

    def lane_cummax(x, d):
        k = 1
        while k < L:
            if d == 0:
                x = jnp.maximum(x, jnp.where(lane_in >= k, pltpu.roll(x, k, axis=1), -jnp.inf))
            else:
                x = jnp.maximum(x, jnp.where(lane_in < L - k, pltpu.roll(x, LANES - k, axis=1), -jnp.inf))
            k *= 2
        return x

    for d in range(2):
        for p in range(n_pairs):
            if has_state:
                c_scr[d, p] = c0_ref[d, p]
                n_scr[2 * d + p:2 * d + p + 1, :] = n0_ref[d, p:p + 1, :]
            else:
                c_scr[d, p] = jnp.zeros((LANES, LANES), F32)
                n_scr[2 * d + p:2 * d + p + 1, :] = jnp.zeros((1, LANES), F32)
    eye_h = (lax.broadcasted_iota(jnp.int32, (H_B, H_B), 0) == lax.broadcasted_iota(jnp.int32, (H_B, H_B), 1))

    def to_col(row):
        return jnp.sum(jnp.where(eye_h, row, 0.0), axis=1, keepdims=True)

    def to_row(col):
        return jnp.sum(jnp.where(eye_h, col, 0.0), axis=0, keepdims=True)

    for d in range(2):
        if has_state:
            m_scr[H_B * d:H_B * (d + 1), 0:1] = to_col(m0_ref[d:d + 1, :])
        else:
            m_scr[H_B * d:H_B * (d + 1), 0:1] = jnp.zeros((H_B, 1), F32)

    pad_scr[0:P0, :] = jnp.zeros((P0, C2), F32)
    pad_scr[P0 + T:P + T, :] = jnp.zeros((P - P0, C2), F32)

    def copy_in(i, carry):
        r0 = pl.multiple_of(i * L, L)
        pad_scr[pl.ds(P0 + r0, L), :] = qk_ref[pl.ds(r0, L), :]
        return carry

    lax.fori_loop(0, N, copy_in, 0)

    lane_c = lax.broadcasted_iota(jnp.int32, (1, C2), 1)
    qscale = jnp.where(lane_c < HK, scale, 1.0).astype(F32)
    sub = lax.broadcasted_iota(jnp.int32, (L, 1), 0)
    img_rows = (0,) if rows_img == 1 else (-1, 0, 1)

    def conv_tile(i, carry):
        r0 = pl.multiple_of(i * L, L)
        col = lax.rem(r0, grid_w) + sub
        ok_left = col >= 1
        ok_right = col <= grid_w - 2
        acc = jnp.zeros((L, C2), F32)
        for di in img_rows:
            blk = pad_scr[pl.ds(P0 + r0 + di * grid_w - 8, L + 16), :]
            left = jnp.where(ok_left, blk[7:7 + L, :], 0.0)
            mid = blk[8:8 + L, :]
            right = jnp.where(ok_right, blk[9:9 + L, :], 0.0)
            wr = 3 * (di + 1)
            acc = acc + left * cw_ref[wr:wr + 1, :] + mid * cw_ref[wr + 1:wr + 2, :] + right * cw_ref[wr + 2:wr + 3, :]
        qk_scr[pl.ds(r0, L), :] = _silu(acc) * qscale
        return carry

    lax.fori_loop(0, N, conv_tile, 0)

    gl = lane - GATE_LANE0
    is_f = ((gl >= H_B) & (gl < 2 * H_B)) | ((gl >= 3 * H_B) & (gl < 4 * H_B))

    def gate_tile(i, carry):
        rows = pl.ds(pl.multiple_of(i * L, L), L)
        x = sm_ref[rows, :] + bm_ref[...]
        y_scr[rows, :] = jnp.where(is_f, _log_sigmoid(x), x)
        return carry

    lax.fori_loop(0, N, gate_tile, 0)


    def state_group(ns):
        units = [u for n in ns for u in ((0, n), (1, N - 1 - n))]
        rows = [_chunk_rows(n) for _, n in units]
        kt_all = [[qk_scr[r, HK + p * LANES:HK + (p + 1) * LANES].T for p in range(n_pairs)] for r in rows]
        yield
        xs = [y_scr[r, :] for r in rows]
        fsum = [_tri_sum(tri[d], x) for (d, _), x in zip(units, xs)]
        yield
        wk_all, f_end, c_end = [], [], []
        for (d, n), r, x, fs in zip(units, rows, xs, fsum):
            y = jnp.where(is_f, fs, x)
            li0 = GATE_LANE0 + 2 * H_B * d
            blk = jnp.concatenate([y, y], axis=0).T[li0:li0 + 2 * H_B, :]
            frow = pltpu.roll(blk, H_B, axis=0)
            grow = blk - frow
            g_scr[d, n] = grow
            f_scr[d, n] = frow
            e_col = L - 1 if d == 0 else 0
            f_end.append(frow[0:H_B, e_col:e_col + 1])
            ce8 = jnp.max(grow, axis=1, keepdims=True)
            c_end.append(ce8[0:H_B, :])
            wk_all.append(jnp.exp(grow[:, 0:L] - ce8))
        yield
        kv_all, ksum_all = [], []
        for r, wk8, kt_u in zip(rows, wk_all, kt_all):
            kv_u, ks_u = [], []
            wk8b = wk8.astype(BF16)
            for p in range(n_pairs):
                kpb = qk_scr[r, HK + p * LANES:HK + (p + 1) * LANES].astype(BF16)
                ks8 = _dot(wk8b, kpb)
                for j in range(2):
                    h = 2 * p + j
                    kwt = (kt_u[p][j * DK:(j + 1) * DK, :] * wk8[h:h + 1, :]).astype(BF16)
                    kv_u.append(_dot(kwt, v_ref[r, h * DV:(h + 1) * DV].astype(BF16)))
                    ks_u.append(ks8[h:h + 1, :])
            kv_all.append(kv_u)
            ksum_all.append(ks_u)
        yield
        m_run = [m_scr[H_B * d:H_B * (d + 1), 0:1] for d in range(2)]
        a_all, b_all = [], []
        for (d, n), fe, ce in zip(units, f_end, c_end):
            mall_scr[d, n, 0:H_B, 0:1] = m_run[d]
            mx = jnp.maximum(m_run[d], ce)
            a_all.append(jnp.exp(m_run[d] - mx))
            b_all.append(jnp.exp(ce - mx))
            m_run[d] = fe + mx
        for d in range(2):
            m_scr[H_B * d:H_B * (d + 1), 0:1] = m_run[d]
        yield
        c_run = [[[c_scr[d, p, j * DK:(j + 1) * DK, :] for j in range(2)] for p in range(n_pairs)] for d in range(2)]
        n_run = [[n_scr[2 * d + p:2 * d + p + 1, :] for p in range(n_pairs)] for d in range(2)]
        for (d, n), a4, b4, kv_u, ks_u in zip(units, a_all, b_all, kv_all, ksum_all):
            for p in range(n_pairs):
                nall_scr[d, n, p:p + 1, :] = n_run[d][p]
                a_s = [a4[2 * p + j:2 * p + j + 1, :] for j in range(2)]
                b_s = [b4[2 * p + j:2 * p + j + 1, :] for j in range(2)]
                for j in range(2):
                    cj = c_run[d][p][j]
                    call_scr[d, n, p, j * DK:(j + 1) * DK, :] = cj.astype(BF16)
                    c_run[d][p][j] = a_s[j] * cj + b_s[j] * kv_u[2 * p + j]
                n_run[d][p] = (jnp.where(head_mask[0], a_s[0], a_s[1]) * n_run[d][p]
                               + jnp.where(head_mask[0], b_s[0] * ks_u[2 * p], b_s[1] * ks_u[2 * p + 1]))
        for d in range(2):
            for p in range(n_pairs):
                n_scr[2 * d + p:2 * d + p + 1, :] = n_run[d][p]
                for j in range(2):
                    c_scr[d, p, j * DK:(j + 1) * DK, :] = c_run[d][p][j]

    eye = lower & upper
    ones8 = jnp.ones((8, L), BF16)
    sub8 = lax.broadcasted_iota(jnp.int32, (8, LANES), 0)
    sub_h = lax.broadcasted_iota(jnp.int32, (H_B, L), 0)
    n_rows = [((sub8 == 2 * p) & head_mask[0]) | ((sub8 == 2 * p + 1) & head_mask[1]) for p in range(n_pairs)]

    def head_rows(vals):
        out = vals[0][0:H_B, :]
        for h in range(1, H_B):
            out = jnp.where(sub_h == h, vals[h][0:H_B, :], out)
        return out

    def out_group(ns):
        chunks = [(d, n) for n in ns for d in range(2)]
        pairs = [(d, n, p) for d, n in chunks for p in range(n_pairs)]
        units = [(d, n, p, j) for d, n, p in pairs for j in range(2)]
        cms = [lane_cummax(g_scr[d, n], d)[0:H_B, 0:L] for d, n in chunks]
        qk2s, qc2s, qn2s = [], [], []
        for d, n, p in pairs:
            r = _chunk_rows(n)
            qp = qk_scr[r, p * LANES:(p + 1) * LANES]
            q2 = jnp.concatenate([jnp.where(head_mask[j], qp, 0.0) for j in range(2)], axis=0).astype(BF16)
            qk2s.append(_dot_nt(q2, qk_scr[r, HK + p * LANES:HK + (p + 1) * LANES].astype(BF16)))
            qc2s.append(_dot(q2, call_scr[d, n, p]))
            nsel = jnp.where(n_rows[p], nall_scr[d, n, p:p + 1, :], 0.0).astype(BF16)
            qn2s.append(_dot_nt(nsel, qp.astype(BF16)))
        yield
        s_all = []
        for ui, (d, n, p, j) in enumerate(units):
            grow = g_scr[d, n, 2 * p + j:2 * p + j + 1, 0:L]
            e = jnp.where(tmask[d], grow, -jnp.inf)
            cmax = jnp.max(e, axis=-1, keepdims=True)
            s_all.append((qk2s[ui // 2][j * L:(j + 1) * L, :] * jnp.exp(e - cmax)).astype(BF16))
        yield
        nums =[_dot(s, v_ref[_chunk_rows(n), (2 * p + j) * DV:(2 * p + j + 1) * DV].astype(BF16))
                for (d, n, p, j), s in zip(units, s_all)]
        dens = [_dot_nt(ones8, s) for s in s_all]
        yield
        scales = []
        for ci, (d, n) in enumerate(chunks):
            den_loc = head_rows(dens[ci * H_B:(ci + 1) * H_B])
            qn = qn2s[ci * n_pairs][0:H_B, :]
            for p in range(1, n_pairs):
                qn = qn + qn2s[ci * n_pairs + p][0:H_B, :]
            cm = cms[ci]
            m_prev = mall_scr[d, n, 0:H_B, 0:1]
            delta = cm - m_prev
            t = jnp.exp(-jnp.abs(delta))
            w_loc = jnp.where(delta <= 0.0, t, 1.0)
            w_inter = jnp.where(delta <= 0.0, 1.0, t)
            mt = f_scr[d, n, 0:H_B, 0:L] + jnp.maximum(m_prev, cm)
            den = w_loc * den_loc + w_inter * qn
            rinv = 1.0 / jnp.maximum(jnp.abs(den), jnp.exp(-mt))
            scales.append((w_loc * rinv, w_inter * rinv))
        yield
        hs = []
        for ui, (d, n, p, j) in enumerate(units):
            h = 2 * p + j
            sc_loc, sc_inter = scales[ui // H_B]
            d_loc = jnp.where(eye, sc_loc[h:h + 1, :], 0.0).astype(BF16)
            d_inter = jnp.where(eye, sc_inter[h:h + 1, :], 0.0).astype(BF16)
            hs.append(_dot(d_loc, nums[ui].astype(BF16))
                      + _dot(d_inter, qc2s[ui // 2][j * L:(j + 1) * L, :].astype(BF16)))
        yield
        for ni, n in enumerate(ns):
            r = _chunk_rows(n)
            for h in range(H_B):
                vs = slice(h * DV, (h + 1) * DV)
                o = hs[(2 * ni) * H_B + h] + hs[(2 * ni + 1) * H_B + h]
                out_ref[r, vs] = (_rms(o, gw_ref[:, vs]) * _sigmoid(og_ref[r, vs])).astype(out_ref.dtype)

    def finish():
        if write_state:
            for d in range(2):
                for p in range(n_pairs):
                    cnew_ref[d, p] = c_scr[d, p]
                    nnew_ref[d, p:p + 1, :] = n_scr[2 * d + p:2 * d + p + 1, :]
                mnew_ref[d:d + 1, :] = to_row(m_scr[H_B * d:H_B * (d + 1), 0:1])

    return state_group, out_group, finish


def _mlstm_scratch(T, C2, grid_w):
    n_pairs = C2 // 2 // LANES
    n_chunks = T // CHUNK
    pad_rows = 2 * (grid_w + 8) if T // grid_w > 1 else 16
    return [
        pltpu.VMEM((T + pad_rows, C2), F32),
        pltpu.VMEM((T, C2), F32),
        pltpu.VMEM((T, SMALL_W), F32),
        pltpu.VMEM((2, n_pairs, LANES, LANES), F32),
        pltpu.VMEM((8, LANES), F32),
        pltpu.VMEM((8, LANES), F32),
        pltpu.VMEM((2, n_chunks, n_pairs, LANES, LANES), BF16),
        pltpu.VMEM((2, n_chunks, 8, LANES), F32),
        pltpu.VMEM((2, n_chunks, 8, LANES), F32),
        pltpu.VMEM((2, n_chunks, 8, LANES), F32),
        pltpu.VMEM((2, n_chunks, 8, LANES), F32),
    ]


N_GLA_SCRATCH = 5
N_MLSTM_SCRATCH = 11


def _scan_kernel(*refs, cols, has_state, write_state, n_cast, grid_w, unroll):
    refs = list(refs)
    z_ref = refs.pop(0)
    s0_ref = c0_ref = n0_ref = m0_ref = None
    if has_state:
        s0_ref, c0_ref, n0_ref, m0_ref = refs[:4]
        del refs[:4]
    wal_ref, bal_ref, gwa_ref, cw_ref, bm_ref, gwb_ref = refs[:6]
    del refs[:6]
    cast_in = refs[:n_cast]
    del refs[:n_cast]
    outa_ref, outb_ref = refs[:2]
    del refs[:2]
    snew_ref = cnew_ref = nnew_ref = mnew_ref = None
    if write_state:
        snew_ref, cnew_ref, nnew_ref, mnew_ref = refs[:4]
        del refs[:4]
    cast_out = refs[:n_cast]
    del refs[:n_cast]
    gla_scr = refs[:N_GLA_SCRATCH]
    mlstm_scr = refs[N_GLA_SCRATCH:]

    for src, dst in zip(cast_in, cast_out):
        dst[...] = src[...].astype(BF16)

    def view(name):
        c0, w = cols[name]
        return z_ref.at[:, pl.ds(c0, w)]

    sm_ref = view("small")
    n_chunks = z_ref.shape[0] // CHUNK
    gla = _gla_body(view("qa"), view("ka"), view("va"), view("ga"), sm_ref, s0_ref, wal_ref, bal_ref, gwa_ref,
                    outa_ref, snew_ref, *gla_scr)
    mlstm = _mlstm_body(view("qkb"), view("vb"), view("ob"), sm_ref, c0_ref, n0_ref, m0_ref, cw_ref, bm_ref,
                        gwb_ref, outb_ref, cnew_ref, nnew_ref, mnew_ref, *mlstm_scr, grid_w=grid_w)
    _chunk_loop(n_chunks, unroll, [mlstm[0], gla[0]])
    _chunk_loop(n_chunks, unroll, [mlstm[1], gla[1]])
    gla[2]()
    mlstm[2]()


def _scan_call(z2d, row0, B, T, states, lw, *, grid_w, write_state, casts=()):
    n_z = z2d.shape[1]
    assert row0 % T == 0 and z2d.shape[0] % T == 0
    z3 = z2d.reshape(z2d.shape[0] // T, T, n_z)
    blk0 = row0 // T
    HK = lw["wal_p"].shape[1] // 2
    DA = lw["gnorm_a_w"].shape[0]
    C2 = lw["conv9"].shape[1]
    DB = lw["gnorm_b_w"].shape[0]
    pa, pb = HK // LANES, C2 // 2 // LANES
    n_chunks = T // CHUNK
    has_state = states is not None
    widths = (("qa", HK), ("ka", HK), ("va", DA), ("ga", DA), ("qkb", C2), ("vb", DB), ("ob", DB),
              ("small", SMALL_W))
    cols, c0 = {}, 0
    for name, w in widths:
        cols[name] = (c0, w)
        c0 += w
    assert c0 == n_z
    cast_in_specs, cast_out_specs, cast_out_shape, cast_args = _cast_specs(casts, B)
    kern = functools.partial(_scan_kernel, cols=cols, has_state=has_state, write_state=write_state,
                             n_cast=len(casts), grid_w=grid_w, unroll=min(n_chunks, SCAN_UNROLL))

    def per_batch(shape):
        nd = len(shape)
        return pl.BlockSpec((None,) + tuple(shape), lambda b: (b,) + (0,) * nd)

    def whole(a):
        return pl.BlockSpec(a.shape, lambda b: (0,) * a.ndim)

    state_shapes = ((2, pa, LANES, LANES), (2, pb, LANES, LANES), (2, pb, LANES), (2, H_B))
    in_specs = [pl.BlockSpec((None, T, n_z), lambda b: (b + blk0, 0, 0))]
    args = [z3]
    if has_state:
        s_gla, s_c, s_n, s_m = states
        args += [s_gla.reshape((B,) + state_shapes[0]), s_c.reshape((B,) + state_shapes[1]),
                 s_n.reshape((B,) + state_shapes[2]), s_m]
        in_specs += [per_batch(s) for s in state_shapes]
    small = [lw["wal_p"], lw["bal_p"], lw["gnorm_a_w"].reshape(1, DA), lw["conv9"], lw["bm_row"],
             lw["gnorm_b_w"].reshape(1, DB)]
    args += small + cast_args
    in_specs += [whole(a) for a in small] + cast_in_specs
    out_specs = [per_batch((T, DA)), per_batch((T, DB))]
    out_shape = [jax.ShapeDtypeStruct((B, T, DA), BF16), jax.ShapeDtypeStruct((B, T, DB), BF16)]
    if write_state:
        out_specs += [per_batch(s) for s in state_shapes]
        out_shape += [jax.ShapeDtypeStruct((B,) + s, F32) for s in state_shapes]
    out_specs += cast_out_specs
    out_shape += cast_out_shape
    scratch = _gla_scratch(T, HK) + _mlstm_scratch(T, C2, grid_w)
    assert len(scratch) == N_GLA_SCRATCH + N_MLSTM_SCRATCH
    return pl.pallas_call(
        kern,
        grid=(B,),
        in_specs=in_specs,
        out_specs=out_specs,
        out_shape=out_shape,
        scratch_shapes=scratch,
        compiler_params=pltpu.CompilerParams(dimension_semantics=("arbitrary",),
                                             vmem_limit_bytes=VMEM_LIMIT),
        name="mixer_scans",
    )(*args)


def _outff_kernel(xc_ref, xl_ref, ac_ref, al_ref, bc_ref, bl_ref, mod_ref, n2_ref, fn_ref, wo_ref, w1_ref, w2_ref,
                  yc_ref, yl_ref, *, n_ctx, tiles_per_req, ff_chunk, final_norm):
    D = xc_ref.shape[1]
    DA = ac_ref.shape[1]
    is_ctx, row = _tile_group(n_ctx, tiles_per_req)

    def mod(k):
        return mod_ref[pl.ds(row, 1), k * D:(k + 1) * D]

    def tile(x_ref, a_ref, b_ref, y_ref):
        y = _dot(a_ref[...], wo_ref[0:DA, :]) + _dot(b_ref[...], wo_ref[DA:, :])
        x1 = x_ref[...] + mod(2) * y
        h2 = (_rms(x1, n2_ref[...]) * (1.0 + mod(4)) + mod(3)).astype(BF16)
        acc = jnp.zeros(x1.shape, F32)
        for c0 in range(0, w1_ref.shape[1], ff_chunk):
            u = jnp.maximum(_dot(h2, w1_ref[:, c0:c0 + ff_chunk]), 0.0)
            acc = acc + _dot((u * u).astype(BF16), w2_ref[c0:c0 + ff_chunk, :])
        x2 = x1 + mod(5) * acc
        y_ref[...] = _rms(x2, fn_ref[...]) if final_norm else x2

    @pl.when(is_ctx)
    def _():
        tile(xc_ref, ac_ref, bc_ref, yc_ref)

    @pl.when(jnp.logical_not(is_ctx))
    def _():
        tile(xl_ref, al_ref, bl_ref, yl_ref)


def _outff_call(xc2d, xl2d, ac, al, bc, bl, mod, norm2_w, final_w, wo, w1, w2, *, tm, tiles_per_req, final_norm):
    (Mc, D), Ml = xc2d.shape, xl2d.shape[0]
    n_ctx = Mc // tm
    DA = ac.shape[1]
    DFF = w1.shape[1]
    kern = functools.partial(_outff_kernel, n_ctx=n_ctx, tiles_per_req=tiles_per_req, ff_chunk=512,
                             final_norm=final_norm)
    once = pl.Buffered(1)
    ctx, lat = _ctx_tile(n_ctx), _lat_tile(n_ctx)
    return pl.pallas_call(
        kern,
        grid=((Mc + Ml) // tm,),
        in_specs=[
            pl.BlockSpec((tm, D), ctx), pl.BlockSpec((tm, D), lat),
            pl.BlockSpec((tm, DA), ctx), pl.BlockSpec((tm, DA), lat),
            pl.BlockSpec((tm, D - DA), ctx), pl.BlockSpec((tm, D - DA), lat),
            pl.BlockSpec(mod.shape, lambda i: (0, 0)),
            pl.BlockSpec((1, D), lambda i: (0, 0)),
            pl.BlockSpec((1, D), lambda i: (0, 0)),
            pl.BlockSpec((D, D), lambda i: (0, 0), pipeline_mode=once),
            pl.BlockSpec((D, DFF), lambda i: (0, 0), pipeline_mode=once),
            pl.BlockSpec((DFF, D), lambda i: (0, 0), pipeline_mode=once),
        ],
        out_specs=[pl.BlockSpec((tm, D), ctx), pl.BlockSpec((tm, D), lat)],
        out_shape=[jax.ShapeDtypeStruct((Mc, D), F32), jax.ShapeDtypeStruct((Ml, D), F32)],
        compiler_params=pltpu.CompilerParams(dimension_semantics=("arbitrary",),
                                             vmem_limit_bytes=VMEM_LIMIT),
        name="outproj_mlp",
    )(xc2d, xl2d, ac, al, bc, bl, mod, norm2_w.reshape(1, D), final_w.reshape(1, D), wo, w1, w2)


def _layer(xc, xl, mod, cached, lw, ffw, final_w, final_norm):
    (Bc, Tc, D), (Bl, Tl, _) = xc.shape, xl.shape
    tm = TOKEN_TILE
    assert (Bc * Tc) % tm == 0 and Tl % tm == 0 and (Bc * Tc) % Tl == 0
    xc2d, xl2d = xc.reshape(Bc * Tc, D), xl.reshape(Bl * Tl, D)
    z = _inproj_call(xc2d, xl2d, mod, lw["norm1_w"], lw["w_in_t"], tm=tm, tiles_per_req=Tl // tm,
                     big_rows=lw["big_rows"], small_rows=lw["small_rows"])
    res_c = _scan_call(z, 0, Bc, Tc, None, lw, grid_w=Tc, write_state=True,
                       casts=((ffw[0], 0), (ffw[1], 1), (ffw[2], 0)))
    res_l = _scan_call(z, Bc * Tc, Bl, Tl, cached, lw, grid_w=GRID_W, write_state=False)
    yc, yl = _outff_call(xc2d, xl2d, res_c[0].reshape(Bc * Tc, -1), res_l[0].reshape(Bl * Tl, -1),
                         res_c[1].reshape(Bc * Tc, -1), res_l[1].reshape(Bl * Tl, -1), mod, lw["norm2_w"],
                         final_w, *res_c[-3:], tm=tm, tiles_per_req=Tl // tm, final_norm=final_norm)
    return yc.reshape(Bc, Tc, D), yl.reshape(Bl, Tl, D), tuple(res_c[2:6])


def _layer_weights(l, norm1_w, norm2_w, w_in, w_alpha2, b_alpha, b_mgate, conv_w, gnorm_a_w, gnorm_b_w):
    hk_a = w_alpha2.shape[-1]
    d_a = gnorm_a_w.shape[-1]
    d_b = gnorm_b_w.shape[-1]
    hk_b = conv_w.shape[-1] // 2
    sizes = (hk_a, hk_a, d_a, d_a, 2 * R_ALPHA, hk_b, hk_b, d_b, d_b, 4 * H_B)
    offs = [0]
    for s in sizes:
        offs.append(offs[-1] + s)
    big_rows = ((offs[0], offs[4] - offs[0]), (offs[5], offs[9] - offs[5]))
    small_rows = ((offs[4], offs[5] - offs[4]), (offs[9], offs[10] - offs[9]))
    assert all(n % LANES == 0 and r % 16 == 0 for r, n in big_rows)
    wal = w_alpha2[l]
    wal_p = jnp.zeros((SMALL_W, 2 * hk_a), F32)
    wal_p = wal_p.at[0:R_ALPHA, 0:hk_a].set(wal[0]).at[R_ALPHA:2 * R_ALPHA, hk_a:].set(wal[1]).astype(BF16)
    bm_row = jnp.zeros((1, SMALL_W), F32).at[0, GATE_LANE0:GATE_LANE0 + 4 * H_B].set(b_mgate[l].reshape(-1))
    return dict(
        norm1_w=norm1_w[l], norm2_w=norm2_w[l], w_in_t=jnp.swapaxes(w_in[l], 0, 1),
        big_rows=big_rows, small_rows=small_rows, wal_p=wal_p,
        bal_p=b_alpha[l].reshape(1, -1), bm_row=bm_row,
        conv9=conv_w[l].reshape(-1, conv_w.shape[-1]),
        gnorm_a_w=gnorm_a_w[l], gnorm_b_w=gnorm_b_w[l],
    )


def kernel(x_prompt, x_sample, c, state_gla, state_mlstm_C, state_mlstm_n, state_mlstm_m, c_ctx, w_ada, b_ada, norm1_w, norm2_w, w_in, w_alpha2, b_alpha, b_mgate, conv_w, gnorm_a_w, gnorm_b_w, w_out, w_ff1, w_ff2, final_norm_w):
    depth = w_in.shape[0]
    D = x_prompt.shape[-1]
    Bp, Tp, _ = x_prompt.shape
    Bs = x_sample.shape[0]
    assert 1 + Bs <= COND_ROWS
    cond = jnp.concatenate([c_ctx[None, :], c, jnp.zeros((COND_ROWS - 1 - Bs, D), F32)], axis=0)
    xp, xs = x_prompt, x_sample
    s_gla, s_c, s_n, s_m = [], [], [], []
    for l in range(depth):
        lw = _layer_weights(l, norm1_w, norm2_w, w_in, w_alpha2, b_alpha, b_mgate, conv_w,
                            gnorm_a_w, gnorm_b_w)
        mod = _ada_call(cond, w_ada[l], b_ada[l])
        cached = (state_gla[:, l], state_mlstm_C[:, l], state_mlstm_n[:, l], state_mlstm_m[:, l])
        xp, xs, ctx = _layer(xp, xs, mod, cached, lw, (w_out[l], w_ff1[l], w_ff2[l]), final_norm_w,
                             l == depth - 1)
        s_gla.append(ctx[0].reshape(Bp, 2, H_A, -1, ctx[0].shape[-1]))
        s_c.append(ctx[1].reshape(Bp, 2, H_B, -1, ctx[1].shape[-1]))
        s_n.append(ctx[2].reshape(Bp, 2, H_B, -1))
        s_m.append(ctx[3])
    dt = x_prompt.dtype
    return (xp, xs, jnp.stack(s_gla, axis=1).astype(dt), jnp.stack(s_c, axis=1).astype(dt),
            jnp.stack(s_n, axis=1).astype(dt), jnp.stack(s_m, axis=1).astype(dt))
```

```python
import functools

import jax
import jax.numpy as jnp
from jax import lax
from jax.experimental import pallas as pl
from jax.experimental.pallas import tpu as pltpu

F32 = jnp.float32
BF16 = jnp.bfloat16

GRID_W = 64
H_A = 4
H_B = 4
R_ALPHA = 16
TAU_GLA = 16.0
CHUNK = 64
EPS = 1e-6
LANES = 128
COND_ROWS = 8
SMALL_W = LANES
GATE_LANE0 = 2 * R_ALPHA
VMEM_LIMIT = 56 * 1024 * 1024
SCAN_UNROLL = 4
TOKEN_TILE = 512
MOD_SPLIT = 2


def _sigmoid(x):
    return 1.0 / (1.0 + jnp.exp(-x))


def _silu(x):
    return x * _sigmoid(x)


def _log_sigmoid(x):
    return jnp.minimum(x, 0.0) - jnp.log1p(jnp.exp(-jnp.abs(x)))


def _dot(a, b):
    return jnp.dot(a, b, preferred_element_type=F32)


def _dot_nt(a, b):
    return lax.dot_general(a, b, (((1,), (1,)), ((), ())), preferred_element_type=F32)


def _rms(x, w):
    return x * lax.rsqrt(jnp.mean(x * x, axis=-1, keepdims=True) + EPS) * w


def _tri_sum(tri, x):
    hi = x.astype(BF16)
    r1 = x - hi.astype(F32)
    mid = r1.astype(BF16)
    lo = (r1 - mid.astype(F32)).astype(BF16)
    return _dot(tri, hi) + _dot(tri, mid) + _dot(tri, lo)


def _chunk_masks(L):
    row = lax.broadcasted_iota(jnp.int32, (L, L), 0)
    col = lax.broadcasted_iota(jnp.int32, (L, L), 1)
    lower = row >= col
    upper = row <= col
    return lower, upper


def _ada_tile(cc_ref, c_ref, w_ref, b_ref, o_ref):
    D = cc_ref.shape[1]
    sub = lax.broadcasted_iota(jnp.int32, (COND_ROWS, D), 0)
    cond = jnp.where(sub == 0, cc_ref[...], 0.0)
    for r in range(c_ref.shape[0]):
        cond = jnp.where(sub == 1 + r, c_ref[r:r + 1, :], cond)
    o_ref[...] = _dot(_silu(cond).astype(BF16), w_ref[...].astype(BF16)) + b_ref[...]


def _ada_call(cc, c, w_ada, b_ada, n_cols):
    D = cc.shape[1]
    tn = 1024
    return pl.pallas_call(
        _ada_tile,
        grid=(n_cols // tn,),
        in_specs=[
            pl.BlockSpec(cc.shape, lambda j: (0, 0)),
            pl.BlockSpec(c.shape, lambda j: (0, 0)),
            pl.BlockSpec((D, tn), lambda j: (0, j)),
            pl.BlockSpec((1, tn), lambda j: (0, j)),
        ],
        out_specs=pl.BlockSpec((COND_ROWS, tn), lambda j: (0, j)),
        out_shape=jax.ShapeDtypeStruct((COND_ROWS, n_cols), F32),
        compiler_params=pltpu.CompilerParams(dimension_semantics=("arbitrary",),
                                             vmem_limit_bytes=VMEM_LIMIT),
        name="ada_mod",
    )(cc, c, w_ada, b_ada)


def _tile_group(n_ctx, tiles_per_req):
    i = pl.program_id(0)
    is_ctx = i < n_ctx
    row = jnp.where(is_ctx, 0, 1 + jnp.maximum(i - n_ctx, 0) // tiles_per_req)
    return is_ctx, row


def _ctx_tile(n_ctx):
    return lambda i: (jnp.minimum(i, n_ctx - 1), 0)


def _lat_tile(n_ctx):
    return lambda i: (jnp.maximum(i - n_ctx, 0), 0)


def _inproj_kernel(xc_ref, xl_ref, mod_ref, nw_ref, wt_ref, z_ref, wb_scr, *, n_ctx, tiles_per_req, big_rows,
                   small_rows):
    D = xc_ref.shape[1]

    @pl.when(pl.program_id(0) == 0)
    def _():
        col = 0
        for r0, n in big_rows:
            for k in range(n // LANES):
                blk = wt_ref[r0 + k * LANES:r0 + (k + 1) * LANES, :]
                wb_scr[:, col:col + LANES] = blk.T.astype(BF16)
                col += LANES
        parts = [wt_ref[r0:r0 + n, :] for r0, n in small_rows]
        n_small = sum(n for _, n in small_rows)
        parts.append(jnp.zeros((SMALL_W - n_small, D), F32))
        wb_scr[:, col:col + SMALL_W] = jnp.concatenate(parts, axis=0).T.astype(BF16)

    is_ctx, row = _tile_group(n_ctx, tiles_per_req)

    def tile(x_ref):
        sh1 = mod_ref[pl.ds(row, 1), 0:D]
        sc1 = mod_ref[pl.ds(row, 1), D:2 * D]
        h = _rms(x_ref[...], nw_ref[...]) * (1.0 + sc1) + sh1
        z_ref[...] = _dot(h.astype(BF16), wb_scr[...])

    @pl.when(is_ctx)
    def _():
        tile(xc_ref)

    @pl.when(jnp.logical_not(is_ctx))
    def _():
        tile(xl_ref)


def _inproj_call(xc2d, xl2d, mod, norm_w, w_in_t, *, tm, tiles_per_req, big_rows, small_rows):
    (Mc, D), Ml = xc2d.shape, xl2d.shape[0]
    n_ctx = Mc // tm
    n_out = sum(n for _, n in big_rows) + SMALL_W
    kern = functools.partial(_inproj_kernel, n_ctx=n_ctx, tiles_per_req=tiles_per_req,
                             big_rows=big_rows, small_rows=small_rows)
    return pl.pallas_call(
        kern,
        grid=((Mc + Ml) // tm,),
        in_specs=[
            pl.BlockSpec((tm, D), _ctx_tile(n_ctx)),
            pl.BlockSpec((tm, D), _lat_tile(n_ctx)),
            pl.BlockSpec(mod.shape, lambda i: (0, 0)),
            pl.BlockSpec((1, D), lambda i: (0, 0)),
            pl.BlockSpec(w_in_t.shape, lambda i: (0, 0), pipeline_mode=pl.Buffered(1)),
        ],
        out_specs=pl.BlockSpec((tm, n_out), lambda i: (i, 0)),
        out_shape=jax.ShapeDtypeStruct((Mc + Ml, n_out), F32),
        scratch_shapes=[pltpu.VMEM((D, n_out), BF16)],
        compiler_params=pltpu.CompilerParams(dimension_semantics=("arbitrary",),
                                             vmem_limit_bytes=VMEM_LIMIT),
        name="norm_inproj",
    )(xc2d, xl2d, mod, norm_w.reshape(1, D), w_in_t)


def _chunk_loop(n_chunks, unroll, stage_fns):
    def step(ns):
        gens = [fn(ns) for fn in stage_fns]
        while gens:
            alive = []
            for g in gens:
                try:
                    next(g)
                    alive.append(g)
                except StopIteration:
                    pass
            gens = alive

    if unroll >= n_chunks:
        step(list(range(n_chunks)))
        return

    def body(i, carry):
        step([i * unroll + u for u in range(unroll)])
        return carry

    lax.fori_loop(0, n_chunks // unroll, body, 0)


def _chunk_rows(n):
    if isinstance(n, int):
        return pl.ds(n * CHUNK, CHUNK)
    return pl.ds(pl.multiple_of(n * CHUNK, CHUNK), CHUNK)


def _cast_specs(casts, n_steps):
    in_specs, out_specs, out_shape, args = [], [], [], []
    for w, axis in casts:
        blk = list(w.shape)
        assert blk[axis] % n_steps == 0
        blk[axis] //= n_steps
        assert blk[0] % 16 == 0 and blk[1] % LANES == 0
        idx = (lambda b: (b, 0)) if axis == 0 else (lambda b: (0, b))
        in_specs.append(pl.BlockSpec(tuple(blk), idx))
        out_specs.append(pl.BlockSpec(tuple(blk), idx))
        out_shape.append(jax.ShapeDtypeStruct(w.shape, BF16))
        args.append(w)
    return in_specs, out_specs, out_shape, args


def _gla_body(q_ref, k_ref, v_ref, g_ref, sm_ref, s0_ref, wal_ref, bal_ref, gw_ref, out_ref, snew_ref,
              st_scr, sall_scr, qh_scr, qs_scr, kh_scr):
    has_state = s0_ref is not None
    write_state = snew_ref is not None
    T = q_ref.shape[0]
    L = CHUNK
    N = T // L
    HK = q_ref.shape[1]
    DK = HK // H_A
    DV = v_ref.shape[1] // H_A
    scale = DK ** -0.5
    n_pairs = HK // LANES

    lower, upper = _chunk_masks(L)
    tri = (lower.astype(BF16), upper.astype(BF16))
    tmask = (lower, upper)
    lane = lax.broadcasted_iota(jnp.int32, (1, LANES), 1)
    head_mask = (lane < DK, lane >= DK)

    for d in range(2):
        for p in range(n_pairs):
            if has_state:
                st_scr[d, p] = s0_ref[d, p].T
            else:
                st_scr[d, p] = jnp.zeros((LANES, LANES), F32)

    def state_group(ns):
        units = [u for n in ns for u in ((0, n), (1, N - 1 - n))]
        rows = [_chunk_rows(n) for _, n in units]
        vt_all = [[jnp.concatenate([v_ref[r, (2 * p + j) * DV:(2 * p + j + 1) * DV] for j in range(2)],
                                   axis=0).T.astype(BF16) for p in range(n_pairs)] for r in rows]
        yield
        pre = [_dot(sm_ref[r, :].astype(BF16), wal_ref[:, d * HK:(d + 1) * HK]) + bal_ref[d:d + 1, :]
               for (d, _), r in zip(units, rows)]
        yield
        g = [_log_sigmoid(x) * (1.0 / TAU_GLA) for x in pre]
        yield
        b = [_tri_sum(tri[d], gi) for (d, _), gi in zip(units, g)]
        yield
        ks_all, dec_all = [], []
        for (d, _), r, bi in zip(units, rows, b):
            bend = bi[L - 1:L, :] if d == 0 else bi[0:1, :]
            q = q_ref[r, :] * scale
            ks = (k_ref[r, :] * jnp.exp(bend - bi)).astype(BF16)
            qh_scr[d, r, :] = (q * jnp.exp(bi - bend)).astype(BF16)
            qs_scr[d, r, :] = (q * jnp.exp(bi)).astype(BF16)
            kh_scr[d, r, :] = ks
            ks_all.append(ks)
            dec_all.append(jnp.exp(bend))
        yield
        upd_all = []
        for vt_u, ks in zip(vt_all, ks_all):
            upd_u = []
            for p in range(n_pairs):
                kp = ks[:, p * LANES:(p + 1) * LANES]
                kk = jnp.concatenate([jnp.where(head_mask[j], kp, jnp.zeros_like(kp)) for j in range(2)], axis=0)
                upd_u.append(_dot(vt_u[p], kk))
            upd_all.append(upd_u)
        yield
        st = [[st_scr[d, p] for p in range(n_pairs)] for d in range(2)]
        for (d, n), dec, upd in zip(units, dec_all, upd_all):
            for p in range(n_pairs):
                sall_scr[d, n, p] = st[d][p].astype(BF16)
                st[d][p] = st[d][p] * dec[:, p * LANES:(p + 1) * LANES] + upd[p]
        for d in range(2):
            for p in range(n_pairs):
                st_scr[d, p] = st[d][p]

    def out_group(ns):
        pairs = [(d, ni, p) for ni in range(len(ns)) for d in range(2) for p in range(n_pairs)]
        scores, inter = [], []
        for d, ni, p in pairs:
            r = _chunk_rows(ns[ni])
            ls = slice(p * LANES, (p + 1) * LANES)
            qh = qh_scr[d, r, ls]
            qs = qs_scr[d, r, ls]
            zero = jnp.zeros_like(qh)
            q2 = jnp.concatenate([jnp.where(head_mask[j], qh, zero) for j in range(2)], axis=0)
            qs2 = jnp.concatenate([jnp.where(head_mask[j], qs, zero) for j in range(2)], axis=0)
            scores.append(_dot_nt(q2, kh_scr[d, r, ls]))
            inter.append(_dot_nt(qs2, sall_scr[d, ns[ni], p]))
        yield
        probs = [[jnp.where(tmask[d], sc[j * L:(j + 1) * L, :], 0.0).astype(BF16) for j in range(2)]
                 for (d, _, _), sc in zip(pairs, scores)]
        yield
        outs = {}
        for (d, ni, p), pr, it in zip(pairs, probs, inter):
            r = _chunk_rows(ns[ni])
            for j in range(2):
                vs = slice((2 * p + j) * DV, (2 * p + j + 1) * DV)
                outs[(d, ni, 2 * p + j)] = _dot(pr[j], v_ref[r, vs].astype(BF16)) + it[j * L:(j + 1) * L, :]
        yield
        for ni, n in enumerate(ns):
            r = _chunk_rows(n)
            for h in range(H_A):
                vs = slice(h * DV, (h + 1) * DV)
                o = outs[(0, ni, h)] + outs[(1, ni, h)]
                out_ref[r, vs] = (_rms(o, gw_ref[:, vs]) * _silu(g_ref[r, vs])).astype(out_ref.dtype)

    def finish():
        if write_state:
            for d in range(2):
                for p in range(n_pairs):
                    snew_ref[d, p] = st_scr[d, p].T

    return state_group, out_group, finish


def _gla_scratch(T, HK):
    n_pairs = HK // LANES
    n_chunks = T // CHUNK
    return [
        pltpu.VMEM((2, n_pairs, LANES, LANES), F32),
        pltpu.VMEM((2, n_chunks, n_pairs, LANES, LANES), BF16),
        pltpu.VMEM((2, T, HK), BF16),
        pltpu.VMEM((2, T, HK), BF16),
        pltpu.VMEM((2, T, HK), BF16),
    ]


def _mlstm_body(qk_ref, v_ref, og_ref, sm_ref, c0_ref, n0_ref, m0_ref, cw_ref, bm_ref, gw_ref,
                out_ref, cnew_ref, nnew_ref, mnew_ref,
                pad_scr, qk_scr, y_scr, c_scr, n_scr, m_scr, call_scr, nall_scr, mall_scr, g_scr, f_scr,
                *, grid_w):
    has_state = c0_ref is not None
    write_state = cnew_ref is not None
    T = qk_ref.shape[0]
    L = CHUNK
    N = T // L
    C2 = qk_ref.shape[1]
    HK = C2 // 2
    DK = HK // H_B
    DV = v_ref.shape[1] // H_B
    scale = DK ** -0.5
    n_pairs = HK // LANES
    P = pad_scr.shape[0] - T
    P0 = P // 2
    rows_img = T // grid_w

    lower, upper = _chunk_masks(L)
    tri = (lower.astype(BF16), upper.astype(BF16))
    tmask = (lower, upper)
    lane = lax.broadcasted_iota(jnp.int32, (1, LANES), 1)
    head_mask = (lane < DK, lane >= DK)
    lane_in = lane & (L - 1)

    def lane_cummax(x, d):
        k = 1
        while k < L:
            if d == 0:
                x = jnp.maximum(x, jnp.where(lane_in >= k, pltpu.roll(x, k, axis=1), -jnp.inf))
            else:
                x = jnp.maximum(x, jnp.where(lane_in < L - k, pltpu.roll(x, LANES - k, axis=1), -jnp.inf))
            k *= 2
        return x

    for d in range(2):
        for p in range(n_pairs):
            if has_state:
                c_scr[d, p] = c0_ref[d, p]
                n_scr[2 * d + p:2 * d + p + 1, :] = jnp.concatenate(
                    [n0_ref[d, 2 * p + j:2 * p + j + 1, :] for j in range(2)], axis=1)
            else:
                c_scr[d, p] = jnp.zeros((LANES, LANES), F32)
                n_scr[2 * d + p:2 * d + p + 1, :] = jnp.zeros((1, LANES), F32)
    eye_h = (lax.broadcasted_iota(jnp.int32, (H_B, H_B), 0) == lax.broadcasted_iota(jnp.int32, (H_B, H_B), 1))

    def to_col(row):
        return jnp.sum(jnp.where(eye_h, row, 0.0), axis=1, keepdims=True)

    def to_row(col):
        return jnp.sum(jnp.where(eye_h, col, 0.0), axis=0, keepdims=True)

    for d in range(2):
        if has_state:
            m_scr[H_B * d:H_B * (d + 1), 0:1] = to_col(m0_ref[d:d + 1, :])
        else:
            m_scr[H_B * d:H_B * (d + 1), 0:1] = jnp.zeros((H_B, 1), F32)

    pad_scr[0:P0, :] = jnp.zeros((P0, C2), F32)
    pad_scr[P0 + T:P + T, :] = jnp.zeros((P - P0, C2), F32)

    def copy_in(i, carry):
        r0 = pl.multiple_of(i * L, L)
        pad_scr[pl.ds(P0 + r0, L), :] = qk_ref[pl.ds(r0, L), :]
        return carry

    lax.fori_loop(0, N, copy_in, 0)

    lane_c = lax.broadcasted_iota(jnp.int32, (1, C2), 1)
    qscale = jnp.where(lane_c < HK, scale, 1.0).astype(F32)
    sub = lax.broadcasted_iota(jnp.int32, (L, 1), 0)
    img_rows = (0,) if rows_img == 1 else (-1, 0, 1)

    def conv_tile(i, carry):
        r0 = pl.multiple_of(i * L, L)
        col = lax.rem(r0, grid_w) + sub
        ok_left = col >= 1
        ok_right = col <= grid_w - 2
        acc = jnp.zeros((L, C2), F32)
        for di in img_rows:
            blk = pad_scr[pl.ds(P0 + r0 + di * grid_w - 8, L + 16), :]
            left = jnp.where(ok_left, blk[7:7 + L, :], 0.0)
            mid = blk[8:8 + L, :]
            right = jnp.where(ok_right, blk[9:9 + L, :], 0.0)
            acc = (acc + left * cw_ref[di + 1, 0:1, :] + mid * cw_ref[di + 1, 1:2, :]
                   + right * cw_ref[di + 1, 2:3, :])
        qk_scr[pl.ds(r0, L), :] = _silu(acc) * qscale
        return carry

    lax.fori_loop(0, N, conv_tile, 0)

    gl = lane - GATE_LANE0
    is_f = ((gl >= H_B) & (gl < 2 * H_B)) | ((gl >= 3 * H_B) & (gl < 4 * H_B))

    def gate_tile(i, carry):
        rows = pl.ds(pl.multiple_of(i * L, L), L)
        x = sm_ref[rows, :] + bm_ref[...]
        y_scr[rows, :] = jnp.where(is_f, _log_sigmoid(x), x)
        return carry

    lax.fori_loop(0, N, gate_tile, 0)


    def state_group(ns):
        units = [u for n in ns for u in ((0, n), (1, N - 1 - n))]
        rows = [_chunk_rows(n) for _, n in units]
        kt_all = [[qk_scr[r, HK + p * LANES:HK + (p + 1) * LANES].T for p in range(n_pairs)] for r in rows]
        yield
        xs = [y_scr[r, :] for r in rows]
        fsum = [_tri_sum(tri[d], x) for (d, _), x in zip(units, xs)]
        yield
        wk_all, f_end, c_end = [], [], []
        for (d, n), r, x, fs in zip(units, rows, xs, fsum):
            y = jnp.where(is_f, fs, x)
            li0 = GATE_LANE0 + 2 * H_B * d
            blk = jnp.concatenate([y, y], axis=0).T[li0:li0 + 2 * H_B, :]
            frow = pltpu.roll(blk, H_B, axis=0)
            grow = blk - frow
            g_scr[d, n] = grow
            f_scr[d, n] = frow
            e_col = L - 1 if d == 0 else 0
            f_end.append(frow[0:H_B, e_col:e_col + 1])
            ce8 = jnp.max(grow, axis=1, keepdims=True)
            c_end.append(ce8[0:H_B, :])
            wk_all.append(jnp.exp(grow[:, 0:L] - ce8))
        yield
        kv_all, ksum_all = [], []
        for r, wk8, kt_u in zip(rows, wk_all, kt_all):
            kv_u, ks_u = [], []
            wk8b = wk8.astype(BF16)
            for p in range(n_pairs):
                kpb = qk_scr[r, HK + p * LANES:HK + (p + 1) * LANES].astype(BF16)
                ks8 = _dot(wk8b, kpb)
                for j in range(2):
                    h = 2 * p + j
                    kwt = (kt_u[p][j * DK:(j + 1) * DK, :] * wk8[h:h + 1, :]).astype(BF16)
                    kv_u.append(_dot(kwt, v_ref[r, h * DV:(h + 1) * DV].astype(BF16)))
                    ks_u.append(ks8[h:h + 1, :])
            kv_all.append(kv_u)
            ksum_all.append(ks_u)
        yield
        m_run = [m_scr[H_B * d:H_B * (d + 1), 0:1] for d in range(2)]
        a_all, b_all = [], []
        for (d, n), fe, ce in zip(units, f_end, c_end):
            mall_scr[d, n, 0:H_B, 0:1] = m_run[d]
            mx = jnp.maximum(m_run[d], ce)
            a_all.append(jnp.exp(m_run[d] - mx))
            b_all.append(jnp.exp(ce - mx))
            m_run[d] = fe + mx
        for d in range(2):
            m_scr[H_B * d:H_B * (d + 1), 0:1] = m_run[d]
        yield
        c_run = [[[c_scr[d, p, j * DK:(j + 1) * DK, :] for j in range(2)] for p in range(n_pairs)] for d in range(2)]
        n_run = [[n_scr[2 * d + p:2 * d + p + 1, :] for p in range(n_pairs)] for d in range(2)]
        for (d, n), a4, b4, kv_u, ks_u in zip(units, a_all, b_all, kv_all, ksum_all):
            for p in range(n_pairs):
                nall_scr[d, n, p:p + 1, :] = n_run[d][p]
                a_s = [a4[2 * p + j:2 * p + j + 1, :] for j in range(2)]
                b_s = [b4[2 * p + j:2 * p + j + 1, :] for j in range(2)]
                for j in range(2):
                    cj = c_run[d][p][j]
                    call_scr[d, n, p, j * DK:(j + 1) * DK, :] = cj.astype(BF16)
                    c_run[d][p][j] = a_s[j] * cj + b_s[j] * kv_u[2 * p + j]
                n_run[d][p] = (jnp.where(head_mask[0], a_s[0], a_s[1]) * n_run[d][p]
                               + jnp.where(head_mask[0], b_s[0] * ks_u[2 * p], b_s[1] * ks_u[2 * p + 1]))
        for d in range(2):
            for p in range(n_pairs):
                n_scr[2 * d + p:2 * d + p + 1, :] = n_run[d][p]
                for j in range(2):
                    c_scr[d, p, j * DK:(j + 1) * DK, :] = c_run[d][p][j]

    eye = lower & upper
    ones8 = jnp.ones((8, L), BF16)
    sub8 = lax.broadcasted_iota(jnp.int32, (8, LANES), 0)
    sub_h = lax.broadcasted_iota(jnp.int32, (H_B, L), 0)
    n_rows = [((sub8 == 2 * p) & head_mask[0]) | ((sub8 == 2 * p + 1) & head_mask[1]) for p in range(n_pairs)]

    def head_rows(vals):
        out = vals[0][0:H_B, :]
        for h in range(1, H_B):
            out = jnp.where(sub_h == h, vals[h][0:H_B, :], out)
        return out

    def out_group(ns):
        chunks = [(d, n) for n in ns for d in range(2)]
        pairs = [(d, n, p) for d, n in chunks for p in range(n_pairs)]
        units = [(d, n, p, j) for d, n, p in pairs for j in range(2)]
        cms = [lane_cummax(g_scr[d, n], d)[0:H_B, 0:L] for d, n in chunks]
        qk2s, qc2s, qn2s = [], [], []
        for d, n, p in pairs:
            r = _chunk_rows(n)
            qp = qk_scr[r, p * LANES:(p + 1) * LANES]
            q2 = jnp.concatenate([jnp.where(head_mask[j], qp, 0.0) for j in range(2)], axis=0).astype(BF16)
            qk2s.append(_dot_nt(q2, qk_scr[r, HK + p * LANES:HK + (p + 1) * LANES].astype(BF16)))
            qc2s.append(_dot(q2, call_scr[d, n, p]))
            nsel = jnp.where(n_rows[p], nall_scr[d, n, p:p + 1, :], 0.0).astype(BF16)
            qn2s.append(_dot_nt(nsel, qp.astype(BF16)))
        yield
        s_all = []
        for ui, (d, n, p, j) in enumerate(units):
            grow = g_scr[d, n, 2 * p + j:2 * p + j + 1, 0:L]
            e = jnp.where(tmask[d], grow, -jnp.inf)
            cmax = jnp.max(e, axis=-1, keepdims=True)
            s_all.append((qk2s[ui // 2][j * L:(j + 1) * L, :] * jnp.exp(e - cmax)).astype(BF16))
        yield
        nums =[_dot(s, v_ref[_chunk_rows(n), (2 * p + j) * DV:(2 * p + j + 1) * DV].astype(BF16))
                for (d, n, p, j), s in zip(units, s_all)]
        dens = [_dot_nt(ones8, s) for s in s_all]
        yield
        scales = []
        for ci, (d, n) in enumerate(chunks):
            den_loc = head_rows(dens[ci * H_B:(ci + 1) * H_B])
            qn = qn2s[ci * n_pairs][0:H_B, :]
            for p in range(1, n_pairs):
                qn = qn + qn2s[ci * n_pairs + p][0:H_B, :]
            cm = cms[ci]
            m_prev = mall_scr[d, n, 0:H_B, 0:1]
            delta = cm - m_prev
            t = jnp.exp(-jnp.abs(delta))
            w_loc = jnp.where(delta <= 0.0, t, 1.0)
            w_inter = jnp.where(delta <= 0.0, 1.0, t)
            mt = f_scr[d, n, 0:H_B, 0:L] + jnp.maximum(m_prev, cm)
            den = w_loc * den_loc + w_inter * qn
            rinv = 1.0 / jnp.maximum(jnp.abs(den), jnp.exp(-mt))
            scales.append((w_loc * rinv, w_inter * rinv))
        yield
        hs = []
        for ui, (d, n, p, j) in enumerate(units):
            h = 2 * p + j
            sc_loc, sc_inter = scales[ui // H_B]
            d_loc = jnp.where(eye, sc_loc[h:h + 1, :], 0.0).astype(BF16)
            d_inter = jnp.where(eye, sc_inter[h:h + 1, :], 0.0).astype(BF16)
            hs.append(_dot(d_loc, nums[ui].astype(BF16))
                      + _dot(d_inter, qc2s[ui // 2][j * L:(j + 1) * L, :].astype(BF16)))
        yield
        for ni, n in enumerate(ns):
            r = _chunk_rows(n)
            for h in range(H_B):
                vs = slice(h * DV, (h + 1) * DV)
                o = hs[(2 * ni) * H_B + h] + hs[(2 * ni + 1) * H_B + h]
                out_ref[r, vs] = (_rms(o, gw_ref[:, vs]) * _sigmoid(og_ref[r, vs])).astype(out_ref.dtype)

    def finish():
        if write_state:
            for d in range(2):
                for p in range(n_pairs):
                    cnew_ref[d, p] = c_scr[d, p]
                    for j in range(2):
                        nnew_ref[d, 2 * p + j:2 * p + j + 1, :] = n_scr[2 * d + p:2 * d + p + 1, j * DK:(j + 1) * DK]
                mnew_ref[d:d + 1, :] = to_row(m_scr[H_B * d:H_B * (d + 1), 0:1])

    return state_group, out_group, finish


def _mlstm_scratch(T, C2, grid_w):
    n_pairs = C2 // 2 // LANES
    n_chunks = T // CHUNK
    pad_rows = 2 * (grid_w + 8) if T // grid_w > 1 else 16
    return [
        pltpu.VMEM((T + pad_rows, C2), F32),
        pltpu.VMEM((T, C2), F32),
        pltpu.VMEM((T, SMALL_W), F32),
        pltpu.VMEM((2, n_pairs, LANES, LANES), F32),
        pltpu.VMEM((8, LANES), F32),
        pltpu.VMEM((8, LANES), F32),
        pltpu.VMEM((2, n_chunks, n_pairs, LANES, LANES), BF16),
        pltpu.VMEM((2, n_chunks, 8, LANES), F32),
        pltpu.VMEM((2, n_chunks, 8, LANES), F32),
        pltpu.VMEM((2, n_chunks, 8, LANES), F32),
        pltpu.VMEM((2, n_chunks, 8, LANES), F32),
    ]


N_GLA_SCRATCH = 5
N_MLSTM_SCRATCH = 11


def _scan_kernel(*refs, cols, layer, has_state, write_state, n_cast, ride_ada, grid_w, unroll):
    refs = list(refs)
    z_ref = refs.pop(0)
    s0_ref = c0_ref = n0_ref = m0_ref = None
    if has_state:
        s0_ref, c0_ref, n0_ref, m0_ref = refs[:4]
        del refs[:4]
    wa_ref, bal_ref, gwa_ref, cw_ref, bmg_ref, gwb_ref = refs[:6]
    del refs[:6]
    cast_in = refs[:n_cast]
    del refs[:n_cast]
    if ride_ada:
        ada_in = refs[:4]
        del refs[:4]
    outa_ref, outb_ref = refs[:2]
    del refs[:2]
    snew_ref = cnew_ref = nnew_ref = mnew_ref = None
    if write_state:
        snew_ref, cnew_ref, nnew_ref, mnew_ref = refs[:4]
        del refs[:4]
    cast_out = refs[:n_cast]
    del refs[:n_cast]
    if ride_ada:
        ada_out = refs.pop(0)
    wal_scr, bm_scr = refs[:2]
    del refs[:2]
    gla_scr = refs[:N_GLA_SCRATCH]
    mlstm_scr = refs[N_GLA_SCRATCH:]

    for src, dst in zip(cast_in, cast_out):
        dst[...] = src[...].astype(BF16)
    if ride_ada:
        _ada_tile(*ada_in, ada_out)

    R, HK = wa_ref.shape[1], wa_ref.shape[2]
    wal_scr[...] = jnp.zeros(wal_scr.shape, BF16)
    for d in range(2):
        wal_scr[d * R:(d + 1) * R, d * HK:(d + 1) * HK] = wa_ref[d].astype(BF16)
    lane = lax.broadcasted_iota(jnp.int32, (1, LANES), 1)
    bm = jnp.zeros((1, LANES), F32)
    for g in range(bmg_ref.shape[1]):
        for h in range(H_B):
            bm = jnp.where(lane == GATE_LANE0 + H_B * g + h, bmg_ref[layer, g, h], bm)
    bm_scr[0:1, :] = bm

    def view(name):
        c0, w = cols[name]
        return z_ref.at[:, pl.ds(c0, w)]

    sm_ref = view("small")
    n_chunks = z_ref.shape[0] // CHUNK
    gla = _gla_body(view("qa"), view("ka"), view("va"), view("ga"), sm_ref, s0_ref, wal_scr, bal_ref, gwa_ref,
                    outa_ref, snew_ref, *gla_scr)
    mlstm = _mlstm_body(view("qkb"), view("vb"), view("ob"), sm_ref, c0_ref, n0_ref, m0_ref, cw_ref,
                        bm_scr.at[0:1, :], gwb_ref, outb_ref, cnew_ref, nnew_ref, mnew_ref, *mlstm_scr,
                        grid_w=grid_w)
    _chunk_loop(n_chunks, unroll, [mlstm[0], gla[0]])
    _chunk_loop(n_chunks, unroll, [mlstm[1], gla[1]])
    gla[2]()
    mlstm[2]()


def _scan_call(z2d, row0, B, T, states, lw, layer, *, grid_w, write_state, casts=(), ada=None):
    n_z = z2d.shape[1]
    assert row0 % T == 0 and z2d.shape[0] % T == 0
    z3 = z2d.reshape(z2d.shape[0] // T, T, n_z)
    blk0 = row0 // T
    HK = lw["w_alpha2"].shape[-1]
    DA = lw["gnorm_a_w"].shape[0]
    C2 = lw["conv_w"].shape[-1]
    DB = lw["gnorm_b_w"].shape[0]
    DK_A, DK_B = HK // H_A, C2 // 2 // H_B
    pa, pb = HK // LANES, C2 // 2 // LANES
    n_chunks = T // CHUNK
    has_state = states is not None
    widths = (("qa", HK), ("ka", HK), ("va", DA), ("ga", DA), ("qkb", C2), ("vb", DB), ("ob", DB),
              ("small", SMALL_W))
    cols, c0 = {}, 0
    for name, w in widths:
        cols[name] = (c0, w)
        c0 += w
    assert c0 == n_z
    cast_in_specs, cast_out_specs, cast_out_shape, cast_args = _cast_specs(casts, B)
    kern = functools.partial(_scan_kernel, cols=cols, layer=layer, has_state=has_state, write_state=write_state,
                             n_cast=len(casts), ride_ada=ada is not None, grid_w=grid_w,
                             unroll=min(n_chunks, SCAN_UNROLL))

    def per_batch(shape):
        nd = len(shape)
        return pl.BlockSpec((None,) + tuple(shape), lambda b: (b,) + (0,) * nd)

    def per_batch_layer(shape):
        nd = len(shape)
        return pl.BlockSpec((None, None) + tuple(shape), lambda b: (b, layer) + (0,) * nd)

    def of_layer(a):
        return pl.BlockSpec((None,) + a.shape[1:], lambda b: (layer,) + (0,) * (a.ndim - 1))

    def whole(a):
        return pl.BlockSpec(a.shape, lambda b: (0,) * a.ndim)

    state_shapes = ((2, pa, LANES, LANES), (2, pb, LANES, LANES), (2, H_B, DK_B), (2, H_B))
    in_specs = [pl.BlockSpec((None, T, n_z), lambda b: (b + blk0, 0, 0))]
    args = [z3]
    if has_state:
        s_gla, s_c, s_n, s_m = states
        depth = s_gla.shape[1]
        args += [s_gla.reshape((B, depth) + state_shapes[0]), s_c.reshape((B, depth) + state_shapes[1]), s_n, s_m]
        in_specs += [per_batch_layer(s) for s in state_shapes]
    args += [lw["w_alpha2"], lw["b_alpha"], lw["gnorm_a_w"].reshape(1, DA), lw["conv_w"], lw["b_mgate"],
             lw["gnorm_b_w"].reshape(1, DB)]
    in_specs += [of_layer(lw["w_alpha2"]), of_layer(lw["b_alpha"]), pl.BlockSpec((1, DA), lambda b: (0, 0)),
                 whole(lw["conv_w"]), pl.BlockSpec(memory_space=pltpu.SMEM), pl.BlockSpec((1, DB), lambda b: (0, 0))]
    args += cast_args
    in_specs += cast_in_specs
    out_specs = [per_batch((T, DA)), per_batch((T, DB))]
    out_shape = [jax.ShapeDtypeStruct((B, T, DA), BF16), jax.ShapeDtypeStruct((B, T, DB), BF16)]
    if write_state:
        out_specs += [per_batch(s) for s in state_shapes]
        out_shape += [jax.ShapeDtypeStruct((B,) + s, F32) for s in state_shapes]
    out_specs += cast_out_specs
    out_shape += cast_out_shape
    if ada is not None:
        cc, c, w_ada, b_ada, col0 = ada
        n_rest = w_ada.shape[1] - col0
        wcol = n_rest // B
        assert n_rest % B == 0 and wcol % LANES == 0 and col0 % wcol == 0
        args += [cc, c, w_ada, b_ada]
        in_specs += [whole(cc), whole(c),
                     pl.BlockSpec((w_ada.shape[0], wcol), lambda b: (0, col0 // wcol + b)),
                     pl.BlockSpec((1, wcol), lambda b: (0, col0 // wcol + b))]
        out_specs.append(pl.BlockSpec((COND_ROWS, wcol), lambda b: (0, b)))
        out_shape.append(jax.ShapeDtypeStruct((COND_ROWS, n_rest), F32))
    scratch = ([pltpu.VMEM((SMALL_W, 2 * HK), BF16), pltpu.VMEM((8, LANES), F32)]
               + _gla_scratch(T, HK) + _mlstm_scratch(T, C2, grid_w))
    assert len(scratch) == 2 + N_GLA_SCRATCH + N_MLSTM_SCRATCH
    return pl.pallas_call(
        kern,
        grid=(B,),
        in_specs=in_specs,
        out_specs=out_specs,
        out_shape=out_shape,
        scratch_shapes=scratch,
        compiler_params=pltpu.CompilerParams(dimension_semantics=("arbitrary",),
                                             vmem_limit_bytes=VMEM_LIMIT),
        name="mixer_scans",
    )(*args)


def _outff_kernel(xc_ref, xl_ref, ac_ref, al_ref, bc_ref, bl_ref, mod_ref, n2_ref, fn_ref, wo_ref, w1_ref, w2_ref,
                  yc_ref, yl_ref, *, n_ctx, tiles_per_req, ff_chunk, final_norm):
    D = xc_ref.shape[1]
    DA = ac_ref.shape[1]
    is_ctx, row = _tile_group(n_ctx, tiles_per_req)

    def mod(k):
        return mod_ref[pl.ds(row, 1), (k - MOD_SPLIT) * D:(k - MOD_SPLIT + 1) * D]

    def tile(x_ref, a_ref, b_ref, y_ref):
        y = _dot(a_ref[...], wo_ref[0:DA, :]) + _dot(b_ref[...], wo_ref[DA:, :])
        x1 = x_ref[...] + mod(2) * y
        h2 = (_rms(x1, n2_ref[...]) * (1.0 + mod(4)) + mod(3)).astype(BF16)
        acc = jnp.zeros(x1.shape, F32)
        for c0 in range(0, w1_ref.shape[1], ff_chunk):
            u = jnp.maximum(_dot(h2, w1_ref[:, c0:c0 + ff_chunk]), 0.0)
            acc = acc + _dot((u * u).astype(BF16), w2_ref[c0:c0 + ff_chunk, :])
        x2 = x1 + mod(5) * acc
        y_ref[...] = _rms(x2, fn_ref[...]) if final_norm else x2

    @pl.when(is_ctx)
    def _():
        tile(xc_ref, ac_ref, bc_ref, yc_ref)

    @pl.when(jnp.logical_not(is_ctx))
    def _():
        tile(xl_ref, al_ref, bl_ref, yl_ref)


def _outff_call(xc2d, xl2d, ac, al, bc, bl, mod, norm2_w, final_w, wo, w1, w2, *, tm, tiles_per_req, final_norm):
    (Mc, D), Ml = xc2d.shape, xl2d.shape[0]
    n_ctx = Mc // tm
    DA = ac.shape[1]
    DFF = w1.shape[1]
    kern = functools.partial(_outff_kernel, n_ctx=n_ctx, tiles_per_req=tiles_per_req, ff_chunk=512,
                             final_norm=final_norm)
    once = pl.Buffered(1)
    ctx, lat = _ctx_tile(n_ctx), _lat_tile(n_ctx)
    return pl.pallas_call(
        kern,
        grid=((Mc + Ml) // tm,),
        in_specs=[
            pl.BlockSpec((tm, D), ctx), pl.BlockSpec((tm, D), lat),
            pl.BlockSpec((tm, DA), ctx), pl.BlockSpec((tm, DA), lat),
            pl.BlockSpec((tm, D - DA), ctx), pl.BlockSpec((tm, D - DA), lat),
            pl.BlockSpec(mod.shape, lambda i: (0, 0)),
            pl.BlockSpec((1, D), lambda i: (0, 0)),
            pl.BlockSpec((1, D), lambda i: (0, 0)),
            pl.BlockSpec((D, D), lambda i: (0, 0), pipeline_mode=once),
            pl.BlockSpec((D, DFF), lambda i: (0, 0), pipeline_mode=once),
            pl.BlockSpec((DFF, D), lambda i: (0, 0), pipeline_mode=once),
        ],
        out_specs=[pl.BlockSpec((tm, D), ctx), pl.BlockSpec((tm, D), lat)],
        out_shape=[jax.ShapeDtypeStruct((Mc, D), F32), jax.ShapeDtypeStruct((Ml, D), F32)],
        compiler_params=pltpu.CompilerParams(dimension_semantics=("arbitrary",),
                                             vmem_limit_bytes=VMEM_LIMIT),
        name="outproj_mlp",
    )(xc2d, xl2d, ac, al, bc, bl, mod, norm2_w.reshape(1, D), final_w.reshape(1, D), wo, w1, w2)


def _layer(xc, xl, cond, ada_w, cached, lw, layer, ffw, final_w, final_norm):
    (Bc, Tc, D), (Bl, Tl, _) = xc.shape, xl.shape
    tm = TOKEN_TILE
    assert (Bc * Tc) % tm == 0 and Tl % tm == 0 and (Bc * Tc) % Tl == 0
    xc2d, xl2d = xc.reshape(Bc * Tc, D), xl.reshape(Bl * Tl, D)
    mod_in = _ada_call(*cond, *ada_w, MOD_SPLIT * D)
    z = _inproj_call(xc2d, xl2d, mod_in, lw["norm1_w"], lw["w_in_t"], tm=tm, tiles_per_req=Tl // tm,
                     big_rows=lw["big_rows"], small_rows=lw["small_rows"])
    res_c = _scan_call(z, 0, Bc, Tc, None, lw, layer, grid_w=Tc, write_state=True,
                       casts=((ffw[0], 0), (ffw[1], 1), (ffw[2], 0)), ada=(*cond, *ada_w, MOD_SPLIT * D))
    res_l = _scan_call(z, Bc * Tc, Bl, Tl, cached, lw, layer, grid_w=GRID_W, write_state=False)
    wo_b, w1_b, w2_b, mod_out = res_c[-4:]
    yc, yl = _outff_call(xc2d, xl2d, res_c[0].reshape(Bc * Tc, -1), res_l[0].reshape(Bl * Tl, -1),
                         res_c[1].reshape(Bc * Tc, -1), res_l[1].reshape(Bl * Tl, -1), mod_out, lw["norm2_w"],
                         final_w, wo_b, w1_b, w2_b, tm=tm, tiles_per_req=Tl // tm, final_norm=final_norm)
    return yc.reshape(Bc, Tc, D), yl.reshape(Bl, Tl, D), tuple(res_c[2:6])


def _layer_weights(l, norm1_w, norm2_w, w_in, w_alpha2, b_alpha, b_mgate, conv_w, gnorm_a_w, gnorm_b_w):
    hk_a = w_alpha2.shape[-1]
    d_a = gnorm_a_w.shape[-1]
    d_b = gnorm_b_w.shape[-1]
    hk_b = conv_w.shape[-1] // 2
    sizes = (hk_a, hk_a, d_a, d_a, 2 * R_ALPHA, hk_b, hk_b, d_b, d_b, 4 * H_B)
    assert w_alpha2.shape[2] == R_ALPHA and b_mgate.shape[1] * b_mgate.shape[2] == 4 * H_B
    offs = [0]
    for s in sizes:
        offs.append(offs[-1] + s)
    big_rows = ((offs[0], offs[4] - offs[0]), (offs[5], offs[9] - offs[5]))
    small_rows = ((offs[4], offs[5] - offs[4]), (offs[9], offs[10] - offs[9]))
    assert all(n % LANES == 0 and r % 16 == 0 for r, n in big_rows)
    return dict(
        norm1_w=norm1_w[l], norm2_w=norm2_w[l], w_in_t=jnp.swapaxes(w_in[l], 0, 1),
        big_rows=big_rows, small_rows=small_rows,
        w_alpha2=w_alpha2, b_alpha=b_alpha, b_mgate=b_mgate, conv_w=conv_w[l],
        gnorm_a_w=gnorm_a_w[l], gnorm_b_w=gnorm_b_w[l],
    )


def kernel(x_prompt, x_sample, c, state_gla, state_mlstm_C, state_mlstm_n, state_mlstm_m, c_ctx, w_ada, b_ada, norm1_w, norm2_w, w_in, w_alpha2, b_alpha, b_mgate, conv_w, gnorm_a_w, gnorm_b_w, w_out, w_ff1, w_ff2, final_norm_w):
    depth = w_in.shape[0]
    D = x_prompt.shape[-1]
    Bp, Tp, _ = x_prompt.shape
    Bs = x_sample.shape[0]
    assert 1 + Bs <= COND_ROWS
    cond = (c_ctx.reshape(1, D), c)
    cached = (state_gla, state_mlstm_C, state_mlstm_n, state_mlstm_m)
    xp, xs = x_prompt, x_sample
    s_gla, s_c, s_n, s_m = [], [], [], []
    for l in range(depth):
        lw = _layer_weights(l, norm1_w, norm2_w, w_in, w_alpha2, b_alpha, b_mgate, conv_w,
                            gnorm_a_w, gnorm_b_w)
        xp, xs, ctx = _layer(xp, xs, cond, (w_ada[l], b_ada[l].reshape(1, -1)), cached, lw, l,
                             (w_out[l], w_ff1[l], w_ff2[l]), final_norm_w, l == depth - 1)
        s_gla.append(ctx[0].reshape(Bp, 2, H_A, -1, ctx[0].shape[-1]))
        s_c.append(ctx[1].reshape(Bp, 2, H_B, -1, ctx[1].shape[-1]))
        s_n.append(ctx[2])
        s_m.append(ctx[3])
    dt = x_prompt.dtype
    return (xp, xs, jnp.stack(s_gla, axis=1).astype(dt), jnp.stack(s_c, axis=1).astype(dt),
            jnp.stack(s_n, axis=1).astype(dt), jnp.stack(s_m, axis=1).astype(dt))
```

```python
import functools

import jax
import jax.numpy as jnp
from jax import lax
from jax.experimental import pallas as pl
from jax.experimental.pallas import tpu as pltpu

F32 = jnp.float32
BF16 = jnp.bfloat16

GRID_W = 64
H_A = 4
H_B = 4
R_ALPHA = 16
TAU_GLA = 16.0
CHUNK = 64
EPS = 1e-6
LANES = 128
COND_ROWS = 8
SMALL_W = LANES
GATE_LANE0 = 2 * R_ALPHA
VMEM_LIMIT = 56 * 1024 * 1024
SCAN_UNROLL = 4
PIPELINE_STARTS = 4
TOKEN_TILE = 512
MOD_SPLIT = 2


def _sigmoid(x):
    return 1.0 / (1.0 + jnp.exp(-x))


def _silu(x):
    return x * _sigmoid(x)


def _log_sigmoid(x):
    return jnp.minimum(x, 0.0) - jnp.log1p(jnp.exp(-jnp.abs(x)))


def _dot(a, b):
    return jnp.dot(a, b, preferred_element_type=F32)


def _dot_nt(a, b):
    return lax.dot_general(a, b, (((1,), (1,)), ((), ())), preferred_element_type=F32)


def _rms(x, w):
    return x * lax.rsqrt(jnp.mean(x * x, axis=-1, keepdims=True) + EPS) * w


def _tri_sum(tri, x):
    hi = x.astype(BF16)
    r1 = x - hi.astype(F32)
    mid = r1.astype(BF16)
    lo = (r1 - mid.astype(F32)).astype(BF16)
    return _dot(tri, hi) + _dot(tri, mid) + _dot(tri, lo)


def _chunk_masks(L):
    row = lax.broadcasted_iota(jnp.int32, (L, L), 0)
    col = lax.broadcasted_iota(jnp.int32, (L, L), 1)
    lower = row >= col
    upper = row <= col
    return lower, upper


def _ada_tile(cc_ref, c_ref, w_ref, b_ref, o_ref):
    D = cc_ref.shape[1]
    sub = lax.broadcasted_iota(jnp.int32, (COND_ROWS, D), 0)
    cond = jnp.where(sub == 0, cc_ref[...], 0.0)
    for r in range(c_ref.shape[0]):
        cond = jnp.where(sub == 1 + r, c_ref[r:r + 1, :], cond)
    o_ref[...] = _dot(_silu(cond).astype(BF16), w_ref[...].astype(BF16)) + b_ref[...]


def _ada_call(cc, c, w_ada, b_ada, n_cols):
    D = cc.shape[1]
    tn = 1024
    return pl.pallas_call(
        _ada_tile,
        grid=(n_cols // tn,),
        in_specs=[
            pl.BlockSpec(cc.shape, lambda j: (0, 0)),
            pl.BlockSpec(c.shape, lambda j: (0, 0)),
            pl.BlockSpec((D, tn), lambda j: (0, j)),
            pl.BlockSpec((1, tn), lambda j: (0, j)),
        ],
        out_specs=pl.BlockSpec((COND_ROWS, tn), lambda j: (0, j)),
        out_shape=jax.ShapeDtypeStruct((COND_ROWS, n_cols), F32),
        compiler_params=pltpu.CompilerParams(dimension_semantics=("arbitrary",),
                                             vmem_limit_bytes=VMEM_LIMIT),
        name="ada_mod",
    )(cc, c, w_ada, b_ada)


def _tile_group(n_ctx, tiles_per_req):
    i = pl.program_id(0)
    is_ctx = i < n_ctx
    row = jnp.where(is_ctx, 0, 1 + jnp.maximum(i - n_ctx, 0) // tiles_per_req)
    return is_ctx, row


def _ctx_tile(n_ctx):
    return lambda i: (jnp.minimum(i, n_ctx - 1), 0)


def _lat_tile(n_ctx):
    return lambda i: (jnp.maximum(i - n_ctx, 0), 0)


def _inproj_kernel(xc_ref, xl_ref, mod_ref, nw_ref, wt_ref, z_ref, wb_scr, *, n_ctx, tiles_per_req, big_rows,
                   small_rows):
    D = xc_ref.shape[1]

    @pl.when(pl.program_id(0) == 0)
    def _():
        col = 0
        for r0, n in big_rows:
            for k in range(n // LANES):
                blk = wt_ref[r0 + k * LANES:r0 + (k + 1) * LANES, :]
                wb_scr[:, col:col + LANES] = blk.T.astype(BF16)
                col += LANES
        parts = [wt_ref[r0:r0 + n, :] for r0, n in small_rows]
        n_small = sum(n for _, n in small_rows)
        parts.append(jnp.zeros((SMALL_W - n_small, D), F32))
        wb_scr[:, col:col + SMALL_W] = jnp.concatenate(parts, axis=0).T.astype(BF16)

    is_ctx, row = _tile_group(n_ctx, tiles_per_req)

    def tile(x_ref):
        sh1 = mod_ref[pl.ds(row, 1), 0:D]
        sc1 = mod_ref[pl.ds(row, 1), D:2 * D]
        h = _rms(x_ref[...], nw_ref[...]) * (1.0 + sc1) + sh1
        z_ref[...] = _dot(h.astype(BF16), wb_scr[...])

    @pl.when(is_ctx)
    def _():
        tile(xc_ref)

    @pl.when(jnp.logical_not(is_ctx))
    def _():
        tile(xl_ref)


def _inproj_call(xc2d, xl2d, mod, norm_w, w_in_t, *, tm, tiles_per_req, big_rows, small_rows):
    (Mc, D), Ml = xc2d.shape, xl2d.shape[0]
    n_ctx = Mc // tm
    n_out = sum(n for _, n in big_rows) + SMALL_W
    kern = functools.partial(_inproj_kernel, n_ctx=n_ctx, tiles_per_req=tiles_per_req,
                             big_rows=big_rows, small_rows=small_rows)
    return pl.pallas_call(
        kern,
        grid=((Mc + Ml) // tm,),
        in_specs=[
            pl.BlockSpec((tm, D), _ctx_tile(n_ctx)),
            pl.BlockSpec((tm, D), _lat_tile(n_ctx)),
            pl.BlockSpec(mod.shape, lambda i: (0, 0)),
            pl.BlockSpec((1, D), lambda i: (0, 0)),
            pl.BlockSpec(w_in_t.shape, lambda i: (0, 0), pipeline_mode=pl.Buffered(1)),
        ],
        out_specs=pl.BlockSpec((tm, n_out), lambda i: (i, 0)),
        out_shape=jax.ShapeDtypeStruct((Mc + Ml, n_out), F32),
        scratch_shapes=[pltpu.VMEM((D, n_out), BF16)],
        compiler_params=pltpu.CompilerParams(dimension_semantics=("arbitrary",),
                                             vmem_limit_bytes=VMEM_LIMIT),
        name="norm_inproj",
    )(xc2d, xl2d, mod, norm_w.reshape(1, D), w_in_t)


def _chunk_loop(n_chunks, unroll, make_units):
    def step(ns):
        pending = list(make_units(ns))
        active = []
        while pending or active:
            for _ in range(min(PIPELINE_STARTS, len(pending))):
                active.append(pending.pop(0))
            alive = []
            for g in active:
                try:
                    next(g)
                    alive.append(g)
                except StopIteration:
                    pass
            active = alive

    if unroll >= n_chunks:
        step(list(range(n_chunks)))
        return

    def body(i, carry):
        step([i * unroll + u for u in range(unroll)])
        return carry

    lax.fori_loop(0, n_chunks // unroll, body, 0)


def _chunk_rows(n):
    if isinstance(n, int):
        return pl.ds(n * CHUNK, CHUNK)
    return pl.ds(pl.multiple_of(n * CHUNK, CHUNK), CHUNK)


def _cast_specs(casts, n_steps):
    in_specs, out_specs, out_shape, args = [], [], [], []
    for w, axis in casts:
        blk = list(w.shape)
        assert blk[axis] % n_steps == 0
        blk[axis] //= n_steps
        assert blk[0] % 16 == 0 and blk[1] % LANES == 0
        idx = (lambda b: (b, 0)) if axis == 0 else (lambda b: (0, b))
        in_specs.append(pl.BlockSpec(tuple(blk), idx))
        out_specs.append(pl.BlockSpec(tuple(blk), idx))
        out_shape.append(jax.ShapeDtypeStruct(w.shape, BF16))
        args.append(w)
    return in_specs, out_specs, out_shape, args


def _gla_body(q_ref, k_ref, v_ref, g_ref, sm_ref, s0_ref, wal_ref, bal_ref, gw_ref, out_ref, snew_ref,
              st_scr, sall_scr, qh_scr, qs_scr, kh_scr):
    has_state = s0_ref is not None
    write_state = snew_ref is not None
    T = q_ref.shape[0]
    L = CHUNK
    N = T // L
    HK = q_ref.shape[1]
    DK = HK // H_A
    DV = v_ref.shape[1] // H_A
    scale = DK ** -0.5
    n_pairs = HK // LANES

    lower, upper = _chunk_masks(L)
    tri = (lower.astype(BF16), upper.astype(BF16))
    tmask = (lower, upper)
    lane = lax.broadcasted_iota(jnp.int32, (1, LANES), 1)
    head_mask = (lane < DK, lane >= DK)

    for d in range(2):
        for p in range(n_pairs):
            if has_state:
                st_scr[d, p] = s0_ref[d, p].T
            else:
                st_scr[d, p] = jnp.zeros((LANES, LANES), F32)

    def state_group(ns, dirs=(0, 1)):
        units = [(d, n if d == 0 else N - 1 - n) for n in ns for d in dirs]
        rows = [_chunk_rows(n) for _, n in units]
        vt_all = [[jnp.concatenate([v_ref[r, (2 * p + j) * DV:(2 * p + j + 1) * DV] for j in range(2)],
                                   axis=0).T.astype(BF16) for p in range(n_pairs)] for r in rows]
        yield
        pre = [_dot(sm_ref[r, :].astype(BF16), wal_ref[:, d * HK:(d + 1) * HK]) + bal_ref[d:d + 1, :]
               for (d, _), r in zip(units, rows)]
        yield
        g = [_log_sigmoid(x) * (1.0 / TAU_GLA) for x in pre]
        yield
        b = [_tri_sum(tri[d], gi) for (d, _), gi in zip(units, g)]
        yield
        ks_all, dec_all = [], []
        for (d, _), r, bi in zip(units, rows, b):
            bend = bi[L - 1:L, :] if d == 0 else bi[0:1, :]
            q = q_ref[r, :] * scale
            ks = (k_ref[r, :] * jnp.exp(bend - bi)).astype(BF16)
            qh_scr[d, r, :] = (q * jnp.exp(bi - bend)).astype(BF16)
            qs_scr[d, r, :] = (q * jnp.exp(bi)).astype(BF16)
            kh_scr[d, r, :] = ks
            ks_all.append(ks)
            dec_all.append(jnp.exp(bend))
        yield
        upd_all = []
        for vt_u, ks in zip(vt_all, ks_all):
            upd_u = []
            for p in range(n_pairs):
                kp = ks[:, p * LANES:(p + 1) * LANES]
                kk = jnp.concatenate([jnp.where(head_mask[j], kp, jnp.zeros_like(kp)) for j in range(2)], axis=0)
                upd_u.append(_dot(vt_u[p], kk))
            upd_all.append(upd_u)
        yield
        st = {d: [st_scr[d, p] for p in range(n_pairs)] for d in dirs}
        for (d, n), dec, upd in zip(units, dec_all, upd_all):
            for p in range(n_pairs):
                sall_scr[d, n, p] = st[d][p].astype(BF16)
                st[d][p] = st[d][p] * dec[:, p * LANES:(p + 1) * LANES] + upd[p]
        for d in dirs:
            for p in range(n_pairs):
                st_scr[d, p] = st[d][p]

    def out_group(ns):
        pairs = [(d, ni, p) for ni in range(len(ns)) for d in range(2) for p in range(n_pairs)]
        scores, inter = [], []
        for d, ni, p in pairs:
            r = _chunk_rows(ns[ni])
            ls = slice(p * LANES, (p + 1) * LANES)
            qh = qh_scr[d, r, ls]
            qs = qs_scr[d, r, ls]
            zero = jnp.zeros_like(qh)
            q2 = jnp.concatenate([jnp.where(head_mask[j], qh, zero) for j in range(2)], axis=0)
            qs2 = jnp.concatenate([jnp.where(head_mask[j], qs, zero) for j in range(2)], axis=0)
            scores.append(_dot_nt(q2, kh_scr[d, r, ls]))
            inter.append(_dot_nt(qs2, sall_scr[d, ns[ni], p]))
        yield
        probs = [[jnp.where(tmask[d], sc[j * L:(j + 1) * L, :], 0.0).astype(BF16) for j in range(2)]
                 for (d, _, _), sc in zip(pairs, scores)]
        yield
        outs = {}
        for (d, ni, p), pr, it in zip(pairs, probs, inter):
            r = _chunk_rows(ns[ni])
            for j in range(2):
                vs = slice((2 * p + j) * DV, (2 * p + j + 1) * DV)
                outs[(d, ni, 2 * p + j)] = _dot(pr[j], v_ref[r, vs].astype(BF16)) + it[j * L:(j + 1) * L, :]
        yield
        for ni, n in enumerate(ns):
            r = _chunk_rows(n)
            for h in range(H_A):
                vs = slice(h * DV, (h + 1) * DV)
                o = outs[(0, ni, h)] + outs[(1, ni, h)]
                out_ref[r, vs] = (_rms(o, gw_ref[:, vs]) * _silu(g_ref[r, vs])).astype(out_ref.dtype)

    def finish():
        if write_state:
            for d in range(2):
                for p in range(n_pairs):
                    snew_ref[d, p] = st_scr[d, p].T

    return state_group, out_group, finish


def _gla_scratch(T, HK):
    n_pairs = HK // LANES
    n_chunks = T // CHUNK
    return [
        pltpu.VMEM((2, n_pairs, LANES, LANES), F32),
        pltpu.VMEM((2, n_chunks, n_pairs, LANES, LANES), BF16),
        pltpu.VMEM((2, T, HK), BF16),
        pltpu.VMEM((2, T, HK), BF16),
        pltpu.VMEM((2, T, HK), BF16),
    ]


def _mlstm_body(qk_ref, v_ref, og_ref, sm_ref, c0_ref, n0_ref, m0_ref, cw_ref, bm_ref, gw_ref,
                out_ref, cnew_ref, nnew_ref, mnew_ref,
                pad_scr, qk_scr, y_scr, c_scr, n_scr, m_scr, call_scr, nall_scr, mall_scr, g_scr, f_scr,
                *, grid_w):
    has_state = c0_ref is not None
    write_state = cnew_ref is not None
    T = qk_ref.shape[0]
    L = CHUNK
    N = T // L
    C2 = qk_ref.shape[1]
    HK = C2 // 2
    DK = HK // H_B
    DV = v_ref.shape[1] // H_B
    scale = DK ** -0.5
    n_pairs = HK // LANES
    P = pad_scr.shape[0] - T
    P0 = P // 2
    rows_img = T // grid_w

    lower, upper = _chunk_masks(L)
    tri = (lower.astype(BF16), upper.astype(BF16))
    tmask = (lower, upper)
    lane = lax.broadcasted_iota(jnp.int32, (1, LANES), 1)
    head_mask = (lane < DK, lane >= DK)
    lane_in = lane & (L - 1)

    def lane_cummax(x, d):
        k = 1
        while k < L:
            if d == 0:
                x = jnp.maximum(x, jnp.where(lane_in >= k, pltpu.roll(x, k, axis=1), -jnp.inf))
            else:
                x = jnp.maximum(x, jnp.where(lane_in < L - k, pltpu.roll(x, LANES - k, axis=1), -jnp.inf))
            k *= 2
        return x

    for d in range(2):
        for p in range(n_pairs):
            if has_state:
                c_scr[d, p] = c0_ref[d, p]
                n_scr[2 * d + p:2 * d + p + 1, :] = jnp.concatenate(
                    [n0_ref[d, 2 * p + j:2 * p + j + 1, :] for j in range(2)], axis=1)
            else:
                c_scr[d, p] = jnp.zeros((LANES, LANES), F32)
                n_scr[2 * d + p:2 * d + p + 1, :] = jnp.zeros((1, LANES), F32)
    eye_h = (lax.broadcasted_iota(jnp.int32, (H_B, H_B), 0) == lax.broadcasted_iota(jnp.int32, (H_B, H_B), 1))

    def to_col(row):
        return jnp.sum(jnp.where(eye_h, row, 0.0), axis=1, keepdims=True)

    def to_row(col):
        return jnp.sum(jnp.where(eye_h, col, 0.0), axis=0, keepdims=True)

    for d in range(2):
        if has_state:
            m_scr[H_B * d:H_B * (d + 1), 0:1] = to_col(m0_ref[d:d + 1, :])
        else:
            m_scr[H_B * d:H_B * (d + 1), 0:1] = jnp.zeros((H_B, 1), F32)

    pad_scr[0:P0, :] = jnp.zeros((P0, C2), F32)
    pad_scr[P0 + T:P + T, :] = jnp.zeros((P - P0, C2), F32)

    def copy_in(i, carry):
        r0 = pl.multiple_of(i * L, L)
        pad_scr[pl.ds(P0 + r0, L), :] = qk_ref[pl.ds(r0, L), :]
        return carry

    lax.fori_loop(0, N, copy_in, 0)

    lane_c = lax.broadcasted_iota(jnp.int32, (1, C2), 1)
    qscale = jnp.where(lane_c < HK, scale, 1.0).astype(F32)
    sub = lax.broadcasted_iota(jnp.int32, (L, 1), 0)
    img_rows = (0,) if rows_img == 1 else (-1, 0, 1)

    def conv_tile(i, carry):
        r0 = pl.multiple_of(i * L, L)
        col = lax.rem(r0, grid_w) + sub
        ok_left = col >= 1
        ok_right = col <= grid_w - 2
        acc = jnp.zeros((L, C2), F32)
        for di in img_rows:
            blk = pad_scr[pl.ds(P0 + r0 + di * grid_w - 8, L + 16), :]
            left = jnp.where(ok_left, blk[7:7 + L, :], 0.0)
            mid = blk[8:8 + L, :]
            right = jnp.where(ok_right, blk[9:9 + L, :], 0.0)
            acc = (acc + left * cw_ref[di + 1, 0:1, :] + mid * cw_ref[di + 1, 1:2, :]
                   + right * cw_ref[di + 1, 2:3, :])
        qk_scr[pl.ds(r0, L), :] = _silu(acc) * qscale
        return carry

    lax.fori_loop(0, N, conv_tile, 0)

    gl = lane - GATE_LANE0
    is_f = ((gl >= H_B) & (gl < 2 * H_B)) | ((gl >= 3 * H_B) & (gl < 4 * H_B))

    def gate_tile(i, carry):
        rows = pl.ds(pl.multiple_of(i * L, L), L)
        x = sm_ref[rows, :] + bm_ref[...]
        y_scr[rows, :] = jnp.where(is_f, _log_sigmoid(x), x)
        return carry

    lax.fori_loop(0, N, gate_tile, 0)


    def state_group(ns, dirs=(0, 1)):
        units = [(d, n if d == 0 else N - 1 - n) for n in ns for d in dirs]
        rows = [_chunk_rows(n) for _, n in units]
        kt_all = [[qk_scr[r, HK + p * LANES:HK + (p + 1) * LANES].T for p in range(n_pairs)] for r in rows]
        yield
        xs = [y_scr[r, :] for r in rows]
        fsum = [_tri_sum(tri[d], x) for (d, _), x in zip(units, xs)]
        yield
        wk_all, f_end, c_end = [], [], []
        for (d, n), r, x, fs in zip(units, rows, xs, fsum):
            y = jnp.where(is_f, fs, x)
            li0 = GATE_LANE0 + 2 * H_B * d
            blk = jnp.concatenate([y, y], axis=0).T[li0:li0 + 2 * H_B, :]
            frow = pltpu.roll(blk, H_B, axis=0)
            grow = blk - frow
            g_scr[d, n] = grow
            f_scr[d, n] = frow
            e_col = L - 1 if d == 0 else 0
            f_end.append(frow[0:H_B, e_col:e_col + 1])
            ce8 = jnp.max(grow, axis=1, keepdims=True)
            c_end.append(ce8[0:H_B, :])
            wk_all.append(jnp.exp(grow[:, 0:L] - ce8))
        yield
        kv_all, ksum_all = [], []
        for r, wk8, kt_u in zip(rows, wk_all, kt_all):
            kv_u, ks_u = [], []
            wk8b = wk8.astype(BF16)
            for p in range(n_pairs):
                kpb = qk_scr[r, HK + p * LANES:HK + (p + 1) * LANES].astype(BF16)
                ks8 = _dot(wk8b, kpb)
                for j in range(2):
                    h = 2 * p + j
                    kwt = (kt_u[p][j * DK:(j + 1) * DK, :] * wk8[h:h + 1, :]).astype(BF16)
                    kv_u.append(_dot(kwt, v_ref[r, h * DV:(h + 1) * DV].astype(BF16)))
                    ks_u.append(ks8[h:h + 1, :])
            kv_all.append(kv_u)
            ksum_all.append(ks_u)
        yield
        m_run = {d: m_scr[H_B * d:H_B * (d + 1), 0:1] for d in dirs}
        a_all, b_all = [], []
        for (d, n), fe, ce in zip(units, f_end, c_end):
            mall_scr[d, n, 0:H_B, 0:1] = m_run[d]
            mx = jnp.maximum(m_run[d], ce)
            a_all.append(jnp.exp(m_run[d] - mx))
            b_all.append(jnp.exp(ce - mx))
            m_run[d] = fe + mx
        for d in dirs:
            m_scr[H_B * d:H_B * (d + 1), 0:1] = m_run[d]
        yield
        c_run = {d: [[c_scr[d, p, j * DK:(j + 1) * DK, :] for j in range(2)] for p in range(n_pairs)] for d in dirs}
        n_run = {d: [n_scr[2 * d + p:2 * d + p + 1, :] for p in range(n_pairs)] for d in dirs}
        for (d, n), a4, b4, kv_u, ks_u in zip(units, a_all, b_all, kv_all, ksum_all):
            for p in range(n_pairs):
                nall_scr[d, n, p:p + 1, :] = n_run[d][p]
                a_s = [a4[2 * p + j:2 * p + j + 1, :] for j in range(2)]
                b_s = [b4[2 * p + j:2 * p + j + 1, :] for j in range(2)]
                for j in range(2):
                    cj = c_run[d][p][j]
                    call_scr[d, n, p, j * DK:(j + 1) * DK, :] = cj.astype(BF16)
                    c_run[d][p][j] = a_s[j] * cj + b_s[j] * kv_u[2 * p + j]
                n_run[d][p] = (jnp.where(head_mask[0], a_s[0], a_s[1]) * n_run[d][p]
                               + jnp.where(head_mask[0], b_s[0] * ks_u[2 * p], b_s[1] * ks_u[2 * p + 1]))
        for d in dirs:
            for p in range(n_pairs):
                n_scr[2 * d + p:2 * d + p + 1, :] = n_run[d][p]
                for j in range(2):
                    c_scr[d, p, j * DK:(j + 1) * DK, :] = c_run[d][p][j]

    eye = lower & upper
    ones8 = jnp.ones((8, L), BF16)
    sub8 = lax.broadcasted_iota(jnp.int32, (8, LANES), 0)
    sub_h = lax.broadcasted_iota(jnp.int32, (H_B, L), 0)
    n_rows = [((sub8 == 2 * p) & head_mask[0]) | ((sub8 == 2 * p + 1) & head_mask[1]) for p in range(n_pairs)]

    def head_rows(vals):
        out = vals[0][0:H_B, :]
        for h in range(1, H_B):
            out = jnp.where(sub_h == h, vals[h][0:H_B, :], out)
        return out

    def out_group(ns):
        chunks = [(d, n) for n in ns for d in range(2)]
        pairs = [(d, n, p) for d, n in chunks for p in range(n_pairs)]
        units = [(d, n, p, j) for d, n, p in pairs for j in range(2)]
        cms = [lane_cummax(g_scr[d, n], d)[0:H_B, 0:L] for d, n in chunks]
        qk2s, qc2s, qn2s = [], [], []
        for d, n, p in pairs:
            r = _chunk_rows(n)
            qp = qk_scr[r, p * LANES:(p + 1) * LANES]
            q2 = jnp.concatenate([jnp.where(head_mask[j], qp, 0.0) for j in range(2)], axis=0).astype(BF16)
            qk2s.append(_dot_nt(q2, qk_scr[r, HK + p * LANES:HK + (p + 1) * LANES].astype(BF16)))
            qc2s.append(_dot(q2, call_scr[d, n, p]))
            nsel = jnp.where(n_rows[p], nall_scr[d, n, p:p + 1, :], 0.0).astype(BF16)
            qn2s.append(_dot_nt(nsel, qp.astype(BF16)))
        yield
        s_all = []
        for ui, (d, n, p, j) in enumerate(units):
            grow = g_scr[d, n, 2 * p + j:2 * p + j + 1, 0:L]
            e = jnp.where(tmask[d], grow, -jnp.inf)
            cmax = jnp.max(e, axis=-1, keepdims=True)
            s_all.append((qk2s[ui // 2][j * L:(j + 1) * L, :] * jnp.exp(e - cmax)).astype(BF16))
        yield
        nums =[_dot(s, v_ref[_chunk_rows(n), (2 * p + j) * DV:(2 * p + j + 1) * DV].astype(BF16))
                for (d, n, p, j), s in zip(units, s_all)]
        dens = [_dot_nt(ones8, s) for s in s_all]
        yield
        scales = []
        for ci, (d, n) in enumerate(chunks):
            den_loc = head_rows(dens[ci * H_B:(ci + 1) * H_B])
            qn = qn2s[ci * n_pairs][0:H_B, :]
            for p in range(1, n_pairs):
                qn = qn + qn2s[ci * n_pairs + p][0:H_B, :]
            cm = cms[ci]
            m_prev = mall_scr[d, n, 0:H_B, 0:1]
            delta = cm - m_prev
            t = jnp.exp(-jnp.abs(delta))
            w_loc = jnp.where(delta <= 0.0, t, 1.0)
            w_inter = jnp.where(delta <= 0.0, 1.0, t)
            mt = f_scr[d, n, 0:H_B, 0:L] + jnp.maximum(m_prev, cm)
            den = w_loc * den_loc + w_inter * qn
            rinv = 1.0 / jnp.maximum(jnp.abs(den), jnp.exp(-mt))
            scales.append((w_loc * rinv, w_inter * rinv))
        yield
        hs = []
        for ui, (d, n, p, j) in enumerate(units):
            h = 2 * p + j
            sc_loc, sc_inter = scales[ui // H_B]
            d_loc = jnp.where(eye, sc_loc[h:h + 1, :], 0.0).astype(BF16)
            d_inter = jnp.where(eye, sc_inter[h:h + 1, :], 0.0).astype(BF16)
            hs.append(_dot(d_loc, nums[ui].astype(BF16))
                      + _dot(d_inter, qc2s[ui // 2][j * L:(j + 1) * L, :].astype(BF16)))
        yield
        for ni, n in enumerate(ns):
            r = _chunk_rows(n)
            for h in range(H_B):
                vs = slice(h * DV, (h + 1) * DV)
                o = hs[(2 * ni) * H_B + h] + hs[(2 * ni + 1) * H_B + h]
                out_ref[r, vs] = (_rms(o, gw_ref[:, vs]) * _sigmoid(og_ref[r, vs])).astype(out_ref.dtype)

    def finish():
        if write_state:
            for d in range(2):
                for p in range(n_pairs):
                    cnew_ref[d, p] = c_scr[d, p]
                    for j in range(2):
                        nnew_ref[d, 2 * p + j:2 * p + j + 1, :] = n_scr[2 * d + p:2 * d + p + 1, j * DK:(j + 1) * DK]
                mnew_ref[d:d + 1, :] = to_row(m_scr[H_B * d:H_B * (d + 1), 0:1])

    return state_group, out_group, finish


def _mlstm_scratch(T, C2, grid_w):
    n_pairs = C2 // 2 // LANES
    n_chunks = T // CHUNK
    pad_rows = 2 * (grid_w + 8) if T // grid_w > 1 else 16
    return [
        pltpu.VMEM((T + pad_rows, C2), F32),
        pltpu.VMEM((T, C2), F32),
        pltpu.VMEM((T, SMALL_W), F32),
        pltpu.VMEM((2, n_pairs, LANES, LANES), F32),
        pltpu.VMEM((8, LANES), F32),
        pltpu.VMEM((8, LANES), F32),
        pltpu.VMEM((2, n_chunks, n_pairs, LANES, LANES), BF16),
        pltpu.VMEM((2, n_chunks, 8, LANES), F32),
        pltpu.VMEM((2, n_chunks, 8, LANES), F32),
        pltpu.VMEM((2, n_chunks, 8, LANES), F32),
        pltpu.VMEM((2, n_chunks, 8, LANES), F32),
    ]


N_GLA_SCRATCH = 5
N_MLSTM_SCRATCH = 11


def _scan_kernel(*refs, cols, layer, has_state, write_state, n_cast, ride_ada, grid_w, unroll):
    refs = list(refs)
    z_ref = refs.pop(0)
    s0_ref = c0_ref = n0_ref = m0_ref = None
    if has_state:
        s0_ref, c0_ref, n0_ref, m0_ref = refs[:4]
        del refs[:4]
    wa_ref, bal_ref, gwa_ref, cw_ref, bmg_ref, gwb_ref = refs[:6]
    del refs[:6]
    cast_in = refs[:n_cast]
    del refs[:n_cast]
    if ride_ada:
        ada_in = refs[:4]
        del refs[:4]
    outa_ref, outb_ref = refs[:2]
    del refs[:2]
    snew_ref = cnew_ref = nnew_ref = mnew_ref = None
    if write_state:
        snew_ref, cnew_ref, nnew_ref, mnew_ref = refs[:4]
        del refs[:4]
    cast_out = refs[:n_cast]
    del refs[:n_cast]
    if ride_ada:
        ada_out = refs.pop(0)
    wal_scr, bm_scr = refs[:2]
    del refs[:2]
    gla_scr = refs[:N_GLA_SCRATCH]
    mlstm_scr = refs[N_GLA_SCRATCH:]

    for src, dst in zip(cast_in, cast_out):
        dst[...] = src[...].astype(BF16)
    if ride_ada:
        _ada_tile(*ada_in, ada_out)

    R, HK = wa_ref.shape[1], wa_ref.shape[2]
    wal_scr[...] = jnp.zeros(wal_scr.shape, BF16)
    for d in range(2):
        wal_scr[d * R:(d + 1) * R, d * HK:(d + 1) * HK] = wa_ref[d].astype(BF16)
    lane = lax.broadcasted_iota(jnp.int32, (1, LANES), 1)
    bm = jnp.zeros((1, LANES), F32)
    for g in range(bmg_ref.shape[1]):
        for h in range(H_B):
            bm = jnp.where(lane == GATE_LANE0 + H_B * g + h, bmg_ref[layer, g, h], bm)
    bm_scr[0:1, :] = bm

    def view(name):
        c0, w = cols[name]
        return z_ref.at[:, pl.ds(c0, w)]

    sm_ref = view("small")
    n_chunks = z_ref.shape[0] // CHUNK
    gla = _gla_body(view("qa"), view("ka"), view("va"), view("ga"), sm_ref, s0_ref, wal_scr, bal_ref, gwa_ref,
                    outa_ref, snew_ref, *gla_scr)
    mlstm = _mlstm_body(view("qkb"), view("vb"), view("ob"), sm_ref, c0_ref, n0_ref, m0_ref, cw_ref,
                        bm_scr.at[0:1, :], gwb_ref, outb_ref, cnew_ref, nnew_ref, mnew_ref, *mlstm_scr,
                        grid_w=grid_w)
    _chunk_loop(n_chunks, unroll, lambda ns: [fn([n], (d,)) for n in ns for d in range(2)
                                              for fn in (mlstm[0], gla[0])])
    _chunk_loop(n_chunks, unroll, lambda ns: [fn([n]) for n in ns for fn in (mlstm[1], gla[1])])
    gla[2]()
    mlstm[2]()


def _scan_call(z2d, row0, B, T, states, lw, layer, *, grid_w, write_state, casts=(), ada=None):
    n_z = z2d.shape[1]
    assert row0 % T == 0 and z2d.shape[0] % T == 0
    z3 = z2d.reshape(z2d.shape[0] // T, T, n_z)
    blk0 = row0 // T
    HK = lw["w_alpha2"].shape[-1]
    DA = lw["gnorm_a_w"].shape[0]
    C2 = lw["conv_w"].shape[-1]
    DB = lw["gnorm_b_w"].shape[0]
    DK_A, DK_B = HK // H_A, C2 // 2 // H_B
    pa, pb = HK // LANES, C2 // 2 // LANES
    n_chunks = T // CHUNK
    has_state = states is not None
    widths = (("qa", HK), ("ka", HK), ("va", DA), ("ga", DA), ("qkb", C2), ("vb", DB), ("ob", DB),
              ("small", SMALL_W))
    cols, c0 = {}, 0
    for name, w in widths:
        cols[name] = (c0, w)
        c0 += w
    assert c0 == n_z
    cast_in_specs, cast_out_specs, cast_out_shape, cast_args = _cast_specs(casts, B)
    kern = functools.partial(_scan_kernel, cols=cols, layer=layer, has_state=has_state, write_state=write_state,
                             n_cast=len(casts), ride_ada=ada is not None, grid_w=grid_w,
                             unroll=min(n_chunks, SCAN_UNROLL))

    def per_batch(shape):
        nd = len(shape)
        return pl.BlockSpec((None,) + tuple(shape), lambda b: (b,) + (0,) * nd)

    def per_batch_layer(shape):
        nd = len(shape)
        return pl.BlockSpec((None, None) + tuple(shape), lambda b: (b, layer) + (0,) * nd)

    def of_layer(a):
        return pl.BlockSpec((None,) + a.shape[1:], lambda b: (layer,) + (0,) * (a.ndim - 1))

    def whole(a):
        return pl.BlockSpec(a.shape, lambda b: (0,) * a.ndim)

    state_shapes = ((2, pa, LANES, LANES), (2, pb, LANES, LANES), (2, H_B, DK_B), (2, H_B))
    in_specs = [pl.BlockSpec((None, T, n_z), lambda b: (b + blk0, 0, 0))]
    args = [z3]
    if has_state:
        s_gla, s_c, s_n, s_m = states
        depth = s_gla.shape[1]
        args += [s_gla.reshape((B, depth) + state_shapes[0]), s_c.reshape((B, depth) + state_shapes[1]), s_n, s_m]
        in_specs += [per_batch_layer(s) for s in state_shapes]
    args += [lw["w_alpha2"], lw["b_alpha"], lw["gnorm_a_w"].reshape(1, DA), lw["conv_w"], lw["b_mgate"],
             lw["gnorm_b_w"].reshape(1, DB)]
    in_specs += [of_layer(lw["w_alpha2"]), of_layer(lw["b_alpha"]), pl.BlockSpec((1, DA), lambda b: (0, 0)),
                 whole(lw["conv_w"]), pl.BlockSpec(memory_space=pltpu.SMEM), pl.BlockSpec((1, DB), lambda b: (0, 0))]
    args += cast_args
    in_specs += cast_in_specs
    out_specs = [per_batch((T, DA)), per_batch((T, DB))]
    out_shape = [jax.ShapeDtypeStruct((B, T, DA), BF16), jax.ShapeDtypeStruct((B, T, DB), BF16)]
    if write_state:
        out_specs += [per_batch(s) for s in state_shapes]
        out_shape += [jax.ShapeDtypeStruct((B,) + s, F32) for s in state_shapes]
    out_specs += cast_out_specs
    out_shape += cast_out_shape
    if ada is not None:
        cc, c, w_ada, b_ada, col0 = ada
        n_rest = w_ada.shape[1] - col0
        wcol = n_rest // B
        assert n_rest % B == 0 and wcol % LANES == 0 and col0 % wcol == 0
        args += [cc, c, w_ada, b_ada]
        in_specs += [whole(cc), whole(c),
                     pl.BlockSpec((w_ada.shape[0], wcol), lambda b: (0, col0 // wcol + b)),
                     pl.BlockSpec((1, wcol), lambda b: (0, col0 // wcol + b))]
        out_specs.append(pl.BlockSpec((COND_ROWS, wcol), lambda b: (0, b)))
        out_shape.append(jax.ShapeDtypeStruct((COND_ROWS, n_rest), F32))
    scratch = ([pltpu.VMEM((SMALL_W, 2 * HK), BF16), pltpu.VMEM((8, LANES), F32)]
               + _gla_scratch(T, HK) + _mlstm_scratch(T, C2, grid_w))
    assert len(scratch) == 2 + N_GLA_SCRATCH + N_MLSTM_SCRATCH
    return pl.pallas_call(
        kern,
        grid=(B,),
        in_specs=in_specs,
        out_specs=out_specs,
        out_shape=out_shape,
        scratch_shapes=scratch,
        compiler_params=pltpu.CompilerParams(dimension_semantics=("arbitrary",),
                                             vmem_limit_bytes=VMEM_LIMIT),
        name="mixer_scans",
    )(*args)


def _outff_kernel(xc_ref, xl_ref, ac_ref, al_ref, bc_ref, bl_ref, mod_ref, n2_ref, fn_ref, wo_ref, w1_ref, w2_ref,
                  yc_ref, yl_ref, *, n_ctx, tiles_per_req, ff_chunk, final_norm):
    D = xc_ref.shape[1]
    DA = ac_ref.shape[1]
    is_ctx, row = _tile_group(n_ctx, tiles_per_req)

    def mod(k):
        return mod_ref[pl.ds(row, 1), (k - MOD_SPLIT) * D:(k - MOD_SPLIT + 1) * D]

    def tile(x_ref, a_ref, b_ref, y_ref):
        y = _dot(a_ref[...], wo_ref[0:DA, :]) + _dot(b_ref[...], wo_ref[DA:, :])
        x1 = x_ref[...] + mod(2) * y
        h2 = (_rms(x1, n2_ref[...]) * (1.0 + mod(4)) + mod(3)).astype(BF16)
        acc = jnp.zeros(x1.shape, F32)
        for c0 in range(0, w1_ref.shape[1], ff_chunk):
            u = jnp.maximum(_dot(h2, w1_ref[:, c0:c0 + ff_chunk]), 0.0)
            acc = acc + _dot((u * u).astype(BF16), w2_ref[c0:c0 + ff_chunk, :])
        x2 = x1 + mod(5) * acc
        y_ref[...] = _rms(x2, fn_ref[...]) if final_norm else x2

    @pl.when(is_ctx)
    def _():
        tile(xc_ref, ac_ref, bc_ref, yc_ref)

    @pl.when(jnp.logical_not(is_ctx))
    def _():
        tile(xl_ref, al_ref, bl_ref, yl_ref)


def _outff_call(xc2d, xl2d, ac, al, bc, bl, mod, norm2_w, final_w, wo, w1, w2, *, tm, tiles_per_req, final_norm):
    (Mc, D), Ml = xc2d.shape, xl2d.shape[0]
    n_ctx = Mc // tm
    DA = ac.shape[1]
    DFF = w1.shape[1]
    kern = functools.partial(_outff_kernel, n_ctx=n_ctx, tiles_per_req=tiles_per_req, ff_chunk=512,
                             final_norm=final_norm)
    once = pl.Buffered(1)
    ctx, lat = _ctx_tile(n_ctx), _lat_tile(n_ctx)
    return pl.pallas_call(
        kern,
        grid=((Mc + Ml) // tm,),
        in_specs=[
            pl.BlockSpec((tm, D), ctx), pl.BlockSpec((tm, D), lat),
            pl.BlockSpec((tm, DA), ctx), pl.BlockSpec((tm, DA), lat),
            pl.BlockSpec((tm, D - DA), ctx), pl.BlockSpec((tm, D - DA), lat),
            pl.BlockSpec(mod.shape, lambda i: (0, 0)),
            pl.BlockSpec((1, D), lambda i: (0, 0)),
            pl.BlockSpec((1, D), lambda i: (0, 0)),
            pl.BlockSpec((D, D), lambda i: (0, 0), pipeline_mode=once),
            pl.BlockSpec((D, DFF), lambda i: (0, 0), pipeline_mode=once),
            pl.BlockSpec((DFF, D), lambda i: (0, 0), pipeline_mode=once),
        ],
        out_specs=[pl.BlockSpec((tm, D), ctx), pl.BlockSpec((tm, D), lat)],
        out_shape=[jax.ShapeDtypeStruct((Mc, D), F32), jax.ShapeDtypeStruct((Ml, D), F32)],
        compiler_params=pltpu.CompilerParams(dimension_semantics=("arbitrary",),
                                             vmem_limit_bytes=VMEM_LIMIT),
        name="outproj_mlp",
    )(xc2d, xl2d, ac, al, bc, bl, mod, norm2_w.reshape(1, D), final_w.reshape(1, D), wo, w1, w2)


def _layer(xc, xl, cond, ada_w, cached, lw, layer, ffw, final_w, final_norm):
    (Bc, Tc, D), (Bl, Tl, _) = xc.shape, xl.shape
    tm = TOKEN_TILE
    assert (Bc * Tc) % tm == 0 and Tl % tm == 0 and (Bc * Tc) % Tl == 0
    xc2d, xl2d = xc.reshape(Bc * Tc, D), xl.reshape(Bl * Tl, D)
    mod_in = _ada_call(*cond, *ada_w, MOD_SPLIT * D)
    z = _inproj_call(xc2d, xl2d, mod_in, lw["norm1_w"], lw["w_in_t"], tm=tm, tiles_per_req=Tl // tm,
                     big_rows=lw["big_rows"], small_rows=lw["small_rows"])
    res_c = _scan_call(z, 0, Bc, Tc, None, lw, layer, grid_w=Tc, write_state=True,
                       casts=((ffw[0], 0), (ffw[1], 1), (ffw[2], 0)), ada=(*cond, *ada_w, MOD_SPLIT * D))
    res_l = _scan_call(z, Bc * Tc, Bl, Tl, cached, lw, layer, grid_w=GRID_W, write_state=False)
    wo_b, w1_b, w2_b, mod_out = res_c[-4:]
    yc, yl = _outff_call(xc2d, xl2d, res_c[0].reshape(Bc * Tc, -1), res_l[0].reshape(Bl * Tl, -1),
                         res_c[1].reshape(Bc * Tc, -1), res_l[1].reshape(Bl * Tl, -1), mod_out, lw["norm2_w"],
                         final_w, wo_b, w1_b, w2_b, tm=tm, tiles_per_req=Tl // tm, final_norm=final_norm)
    return yc.reshape(Bc, Tc, D), yl.reshape(Bl, Tl, D), tuple(res_c[2:6])


def _layer_weights(l, norm1_w, norm2_w, w_in, w_alpha2, b_alpha, b_mgate, conv_w, gnorm_a_w, gnorm_b_w):
    hk_a = w_alpha2.shape[-1]
    d_a = gnorm_a_w.shape[-1]
    d_b = gnorm_b_w.shape[-1]
    hk_b = conv_w.shape[-1] // 2
    sizes = (hk_a, hk_a, d_a, d_a, 2 * R_ALPHA, hk_b, hk_b, d_b, d_b, 4 * H_B)
    assert w_alpha2.shape[2] == R_ALPHA and b_mgate.shape[1] * b_mgate.shape[2] == 4 * H_B
    offs = [0]
    for s in sizes:
        offs.append(offs[-1] + s)
    big_rows = ((offs[0], offs[4] - offs[0]), (offs[5], offs[9] - offs[5]))
    small_rows = ((offs[4], offs[5] - offs[4]), (offs[9], offs[10] - offs[9]))
    assert all(n % LANES == 0 and r % 16 == 0 for r, n in big_rows)
    return dict(
        norm1_w=norm1_w[l], norm2_w=norm2_w[l], w_in_t=jnp.swapaxes(w_in[l], 0, 1),
        big_rows=big_rows, small_rows=small_rows,
        w_alpha2=w_alpha2, b_alpha=b_alpha, b_mgate=b_mgate, conv_w=conv_w[l],
        gnorm_a_w=gnorm_a_w[l], gnorm_b_w=gnorm_b_w[l],
    )


def kernel(x_prompt, x_sample, c, state_gla, state_mlstm_C, state_mlstm_n, state_mlstm_m, c_ctx, w_ada, b_ada, norm1_w, norm2_w, w_in, w_alpha2, b_alpha, b_mgate, conv_w, gnorm_a_w, gnorm_b_w, w_out, w_ff1, w_ff2, final_norm_w):
    depth = w_in.shape[0]
    D = x_prompt.shape[-1]
    Bp, Tp, _ = x_prompt.shape
    Bs = x_sample.shape[0]
    assert 1 + Bs <= COND_ROWS
    cond = (c_ctx.reshape(1, D), c)
    cached = (state_gla, state_mlstm_C, state_mlstm_n, state_mlstm_m)
    xp, xs = x_prompt, x_sample
    s_gla, s_c, s_n, s_m = [], [], [], []
    for l in range(depth):
        lw = _layer_weights(l, norm1_w, norm2_w, w_in, w_alpha2, b_alpha, b_mgate, conv_w,
                            gnorm_a_w, gnorm_b_w)
        xp, xs, ctx = _layer(xp, xs, cond, (w_ada[l], b_ada[l].reshape(1, -1)), cached, lw, l,
                             (w_out[l], w_ff1[l], w_ff2[l]), final_norm_w, l == depth - 1)
        s_gla.append(ctx[0].reshape(Bp, 2, H_A, -1, ctx[0].shape[-1]))
        s_c.append(ctx[1].reshape(Bp, 2, H_B, -1, ctx[1].shape[-1]))
        s_n.append(ctx[2])
        s_m.append(ctx[3])
    dt = x_prompt.dtype
    return (xp, xs, jnp.stack(s_gla, axis=1).astype(dt), jnp.stack(s_c, axis=1).astype(dt),
            jnp.stack(s_n, axis=1).astype(dt), jnp.stack(s_m, axis=1).astype(dt))
```

```python
import functools

import jax
import jax.numpy as jnp
from jax import lax
from jax.experimental import pallas as pl
from jax.experimental.pallas import tpu as pltpu

F32 = jnp.float32
BF16 = jnp.bfloat16

GRID_W = 64
H_A = 4
H_B = 4
R_ALPHA = 16
TAU_GLA = 16.0
CHUNK = 64
EPS = 1e-6
LANES = 128
COND_ROWS = 8
SMALL_W = LANES
GATE_LANE0 = 2 * R_ALPHA
VMEM_LIMIT = 56 * 1024 * 1024
SCAN_UNROLL = 4
PIPELINE_STARTS = 4
TOKEN_TILE = 512
FF_CHUNK = 512
MOD_SPLIT = 2


def _sigmoid(x):
    return 1.0 / (1.0 + jnp.exp(-x))


def _silu(x):
    return x * _sigmoid(x)


def _log_sigmoid(x):
    return jnp.minimum(x, 0.0) - jnp.log1p(jnp.exp(-jnp.abs(x)))


def _dot(a, b):
    return jnp.dot(a, b, preferred_element_type=F32)


def _dot_nt(a, b):
    return lax.dot_general(a, b, (((1,), (1,)), ((), ())), preferred_element_type=F32)


def _rms(x, w):
    return x * lax.rsqrt(jnp.mean(x * x, axis=-1, keepdims=True) + EPS) * w


def _tri_sum(tri, x):
    hi = x.astype(BF16)
    r1 = x - hi.astype(F32)
    mid = r1.astype(BF16)
    lo = (r1 - mid.astype(F32)).astype(BF16)
    return _dot(tri, hi) + _dot(tri, mid) + _dot(tri, lo)


def _chunk_masks(L):
    row = lax.broadcasted_iota(jnp.int32, (L, L), 0)
    col = lax.broadcasted_iota(jnp.int32, (L, L), 1)
    lower = row >= col
    upper = row <= col
    return lower, upper


def _ada_tile(cc_ref, c_ref, w_ref, b_ref, o_ref):
    D = cc_ref.shape[1]
    sub = lax.broadcasted_iota(jnp.int32, (COND_ROWS, D), 0)
    cond = jnp.where(sub == 0, cc_ref[...], 0.0)
    for r in range(c_ref.shape[0]):
        cond = jnp.where(sub == 1 + r, c_ref[r:r + 1, :], cond)
    o_ref[...] = _dot(_silu(cond).astype(BF16), w_ref[...].astype(BF16)) + b_ref[...]


def _ada_call(cc, c, w_ada, b_ada, n_cols):
    D = cc.shape[1]
    tn = 1024
    return pl.pallas_call(
        _ada_tile,
        grid=(n_cols // tn,),
        in_specs=[
            pl.BlockSpec(cc.shape, lambda j: (0, 0)),
            pl.BlockSpec(c.shape, lambda j: (0, 0)),
            pl.BlockSpec((D, tn), lambda j: (0, j)),
            pl.BlockSpec((1, tn), lambda j: (0, j)),
        ],
        out_specs=pl.BlockSpec((COND_ROWS, tn), lambda j: (0, j)),
        out_shape=jax.ShapeDtypeStruct((COND_ROWS, n_cols), F32),
        compiler_params=pltpu.CompilerParams(dimension_semantics=("arbitrary",),
                                             vmem_limit_bytes=VMEM_LIMIT),
        name="ada_mod",
    )(cc, c, w_ada, b_ada)


def _tile_group(n_ctx, tiles_per_req):
    i = pl.program_id(0)
    is_ctx = i < n_ctx
    row = jnp.where(is_ctx, 0, 1 + jnp.maximum(i - n_ctx, 0) // tiles_per_req)
    return is_ctx, row


def _ctx_tile(n_ctx):
    return lambda i: (jnp.minimum(i, n_ctx - 1), 0)


def _lat_tile(n_ctx):
    return lambda i: (jnp.maximum(i - n_ctx, 0), 0)


def _inproj_kernel(xc_ref, xl_ref, mod_ref, nw_ref, wt_ref, z_ref, wb_scr, *, n_ctx, tiles_per_req, big_rows,
                   small_rows):
    D = xc_ref.shape[1]

    @pl.when(pl.program_id(0) == 0)
    def _():
        col = 0
        for r0, n in big_rows:
            for k in range(n // LANES):
                blk = wt_ref[r0 + k * LANES:r0 + (k + 1) * LANES, :]
                wb_scr[:, col:col + LANES] = blk.T.astype(BF16)
                col += LANES
        parts = [wt_ref[r0:r0 + n, :] for r0, n in small_rows]
        n_small = sum(n for _, n in small_rows)
        parts.append(jnp.zeros((SMALL_W - n_small, D), F32))
        wb_scr[:, col:col + SMALL_W] = jnp.concatenate(parts, axis=0).T.astype(BF16)

    is_ctx, row = _tile_group(n_ctx, tiles_per_req)

    def tile(x_ref):
        sh1 = mod_ref[pl.ds(row, 1), 0:D]
        sc1 = mod_ref[pl.ds(row, 1), D:2 * D]
        h = _rms(x_ref[...], nw_ref[...]) * (1.0 + sc1) + sh1
        z_ref[...] = _dot(h.astype(BF16), wb_scr[...])

    @pl.when(is_ctx)
    def _():
        tile(xc_ref)

    @pl.when(jnp.logical_not(is_ctx))
    def _():
        tile(xl_ref)


def _inproj_call(xc2d, xl2d, mod, norm_w, w_in_t, *, tm, tiles_per_req, big_rows, small_rows):
    (Mc, D), Ml = xc2d.shape, xl2d.shape[0]
    n_ctx = Mc // tm
    n_out = sum(n for _, n in big_rows) + SMALL_W
    kern = functools.partial(_inproj_kernel, n_ctx=n_ctx, tiles_per_req=tiles_per_req,
                             big_rows=big_rows, small_rows=small_rows)
    return pl.pallas_call(
        kern,
        grid=((Mc + Ml) // tm,),
        in_specs=[
            pl.BlockSpec((tm, D), _ctx_tile(n_ctx)),
            pl.BlockSpec((tm, D), _lat_tile(n_ctx)),
            pl.BlockSpec(mod.shape, lambda i: (0, 0)),
            pl.BlockSpec((1, D), lambda i: (0, 0)),
            pl.BlockSpec(w_in_t.shape, lambda i: (0, 0), pipeline_mode=pl.Buffered(1)),
        ],
        out_specs=pl.BlockSpec((tm, n_out), lambda i: (i, 0)),
        out_shape=jax.ShapeDtypeStruct((Mc + Ml, n_out), F32),
        scratch_shapes=[pltpu.VMEM((D, n_out), BF16)],
        compiler_params=pltpu.CompilerParams(dimension_semantics=("arbitrary",),
                                             vmem_limit_bytes=VMEM_LIMIT),
        name="norm_inproj",
    )(xc2d, xl2d, mod, norm_w.reshape(1, D), w_in_t)


def _chunk_loop(n_chunks, unroll, make_units):
    def step(ns):
        pending = list(make_units(ns))
        active = []
        while pending or active:
            for _ in range(min(PIPELINE_STARTS, len(pending))):
                active.append(pending.pop(0))
            alive = []
            for g in active:
                try:
                    next(g)
                    alive.append(g)
                except StopIteration:
                    pass
            active = alive

    if unroll >= n_chunks:
        step(list(range(n_chunks)))
        return

    def body(i, carry):
        step([i * unroll + u for u in range(unroll)])
        return carry

    lax.fori_loop(0, n_chunks // unroll, body, 0)


def _chunk_rows(n):
    if isinstance(n, int):
        return pl.ds(n * CHUNK, CHUNK)
    return pl.ds(pl.multiple_of(n * CHUNK, CHUNK), CHUNK)


def _cast_specs(casts, n_steps):
    in_specs, out_specs, out_shape, args = [], [], [], []
    for w, axis in casts:
        blk = list(w.shape)
        assert blk[axis] % n_steps == 0
        blk[axis] //= n_steps
        assert blk[0] % 16 == 0 and blk[1] % LANES == 0
        idx = (lambda b: (b, 0)) if axis == 0 else (lambda b: (0, b))
        in_specs.append(pl.BlockSpec(tuple(blk), idx))
        out_specs.append(pl.BlockSpec(tuple(blk), idx))
        out_shape.append(jax.ShapeDtypeStruct(w.shape, BF16))
        args.append(w)
    return in_specs, out_specs, out_shape, args


def _gla_body(q_ref, k_ref, v_ref, g_ref, sm_ref, s0_ref, wal_ref, bal_ref, gw_ref, out_ref, snew_ref,
              st_scr, sall_scr, qh_scr, qs_scr, kh_scr):
    has_state = s0_ref is not None
    write_state = snew_ref is not None
    T = q_ref.shape[0]
    L = CHUNK
    N = T // L
    HK = q_ref.shape[1]
    DK = HK // H_A
    DV = v_ref.shape[1] // H_A
    scale = DK ** -0.5
    n_pairs = HK // LANES

    lower, upper = _chunk_masks(L)
    tri = (lower.astype(BF16), upper.astype(BF16))
    tmask = (lower, upper)
    lane = lax.broadcasted_iota(jnp.int32, (1, LANES), 1)
    head_mask = (lane < DK, lane >= DK)

    for d in range(2):
        for p in range(n_pairs):
            if has_state:
                st_scr[d, p] = s0_ref[d, p].T
            else:
                st_scr[d, p] = jnp.zeros((LANES, LANES), F32)

    def state_group(ns, dirs=(0, 1)):
        units = [(d, n if d == 0 else N - 1 - n) for n in ns for d in dirs]
        rows = [_chunk_rows(n) for _, n in units]
        vt_all = [[jnp.concatenate([v_ref[r, (2 * p + j) * DV:(2 * p + j + 1) * DV] for j in range(2)],
                                   axis=0).T.astype(BF16) for p in range(n_pairs)] for r in rows]
        yield
        pre = [_dot(sm_ref[r, :].astype(BF16), wal_ref[:, d * HK:(d + 1) * HK]) + bal_ref[d:d + 1, :]
               for (d, _), r in zip(units, rows)]
        yield
        g = [_log_sigmoid(x) * (1.0 / TAU_GLA) for x in pre]
        yield
        b = [_tri_sum(tri[d], gi) for (d, _), gi in zip(units, g)]
        yield
        ks_all, dec_all = [], []
        for (d, _), r, bi in zip(units, rows, b):
            bend = bi[L - 1:L, :] if d == 0 else bi[0:1, :]
            q = q_ref[r, :] * scale
            ks = (k_ref[r, :] * jnp.exp(bend - bi)).astype(BF16)
            qh_scr[d, r, :] = (q * jnp.exp(bi - bend)).astype(BF16)
            qs_scr[d, r, :] = (q * jnp.exp(bi)).astype(BF16)
            kh_scr[d, r, :] = ks
            ks_all.append(ks)
            dec_all.append(jnp.exp(bend))
        yield
        upd_all = []
        for vt_u, ks in zip(vt_all, ks_all):
            upd_u = []
            for p in range(n_pairs):
                kp = ks[:, p * LANES:(p + 1) * LANES]
                kk = jnp.concatenate([jnp.where(head_mask[j], kp, jnp.zeros_like(kp)) for j in range(2)], axis=0)
                upd_u.append(_dot(vt_u[p], kk))
            upd_all.append(upd_u)
        yield
        st = {d: [st_scr[d, p] for p in range(n_pairs)] for d in dirs}
        for (d, n), dec, upd in zip(units, dec_all, upd_all):
            for p in range(n_pairs):
                sall_scr[d, n, p] = st[d][p].astype(BF16)
                st[d][p] = st[d][p] * dec[:, p * LANES:(p + 1) * LANES] + upd[p]
        for d in dirs:
            for p in range(n_pairs):
                st_scr[d, p] = st[d][p]

    def out_group(ns):
        pairs = [(d, ni, p) for ni in range(len(ns)) for d in range(2) for p in range(n_pairs)]
        scores, inter = [], []
        for d, ni, p in pairs:
            r = _chunk_rows(ns[ni])
            ls = slice(p * LANES, (p + 1) * LANES)
            qh = qh_scr[d, r, ls]
            qs = qs_scr[d, r, ls]
            zero = jnp.zeros_like(qh)
            q2 = jnp.concatenate([jnp.where(head_mask[j], qh, zero) for j in range(2)], axis=0)
            qs2 = jnp.concatenate([jnp.where(head_mask[j], qs, zero) for j in range(2)], axis=0)
            scores.append(_dot_nt(q2, kh_scr[d, r, ls]))
            inter.append(_dot_nt(qs2, sall_scr[d, ns[ni], p]))
        yield
        probs = [[jnp.where(tmask[d], sc[j * L:(j + 1) * L, :], 0.0).astype(BF16) for j in range(2)]
                 for (d, _, _), sc in zip(pairs, scores)]
        yield
        outs = {}
        for (d, ni, p), pr, it in zip(pairs, probs, inter):
            r = _chunk_rows(ns[ni])
            for j in range(2):
                vs = slice((2 * p + j) * DV, (2 * p + j + 1) * DV)
                outs[(d, ni, 2 * p + j)] = _dot(pr[j], v_ref[r, vs].astype(BF16)) + it[j * L:(j + 1) * L, :]
        yield
        for ni, n in enumerate(ns):
            r = _chunk_rows(n)
            for h in range(H_A):
                vs = slice(h * DV, (h + 1) * DV)
                o = outs[(0, ni, h)] + outs[(1, ni, h)]
                out_ref[r, vs] = (_rms(o, gw_ref[:, vs]) * _silu(g_ref[r, vs])).astype(out_ref.dtype)

    def finish():
        if write_state:
            for d in range(2):
                for p in range(n_pairs):
                    snew_ref[d, p] = st_scr[d, p].T

    return state_group, out_group, finish


def _gla_scratch(T, HK):
    n_pairs = HK // LANES
    n_chunks = T // CHUNK
    return [
        pltpu.VMEM((2, n_pairs, LANES, LANES), F32),
        pltpu.VMEM((2, n_chunks, n_pairs, LANES, LANES), BF16),
        pltpu.VMEM((2, T, HK), BF16),
        pltpu.VMEM((2, T, HK), BF16),
        pltpu.VMEM((2, T, HK), BF16),
    ]


def _mlstm_body(qk_ref, v_ref, og_ref, sm_ref, c0_ref, n0_ref, m0_ref, cw_ref, bm_ref, gw_ref,
                out_ref, cnew_ref, nnew_ref, mnew_ref,
                pad_scr, qk_scr, y_scr, c_scr, n_scr, m_scr, call_scr, nall_scr, mall_scr, g_scr, f_scr,
                *, grid_w):
    has_state = c0_ref is not None
    write_state = cnew_ref is not None
    T = qk_ref.shape[0]
    L = CHUNK
    N = T // L
    C2 = qk_ref.shape[1]
    HK = C2 // 2
    DK = HK // H_B
    DV = v_ref.shape[1] // H_B
    scale = DK ** -0.5
    n_pairs = HK // LANES
    P = pad_scr.shape[0] - T
    P0 = P // 2
    rows_img = T // grid_w

    lower, upper = _chunk_masks(L)
    tri = (lower.astype(BF16), upper.astype(BF16))
    tmask = (lower, upper)
    lane = lax.broadcasted_iota(jnp.int32, (1, LANES), 1)
    head_mask = (lane < DK, lane >= DK)
    lane_in = lane & (L - 1)

    def lane_cummax(x, d):
        k = 1
        while k < L:
            if d == 0:
                x = jnp.maximum(x, jnp.where(lane_in >= k, pltpu.roll(x, k, axis=1), -jnp.inf))
            else:
                x = jnp.maximum(x, jnp.where(lane_in < L - k, pltpu.roll(x, LANES - k, axis=1), -jnp.inf))
            k *= 2
        return x

    for d in range(2):
        for p in range(n_pairs):
            if has_state:
                c_scr[d, p] = c0_ref[d, p]
                n_scr[2 * d + p:2 * d + p + 1, :] = jnp.concatenate(
                    [n0_ref[d, 2 * p + j:2 * p + j + 1, :] for j in range(2)], axis=1)
            else:
                c_scr[d, p] = jnp.zeros((LANES, LANES), F32)
                n_scr[2 * d + p:2 * d + p + 1, :] = jnp.zeros((1, LANES), F32)
    eye_h = (lax.broadcasted_iota(jnp.int32, (H_B, H_B), 0) == lax.broadcasted_iota(jnp.int32, (H_B, H_B), 1))

    def to_col(row):
        return jnp.sum(jnp.where(eye_h, row, 0.0), axis=1, keepdims=True)

    def to_row(col):
        return jnp.sum(jnp.where(eye_h, col, 0.0), axis=0, keepdims=True)

    for d in range(2):
        if has_state:
            m_scr[H_B * d:H_B * (d + 1), 0:1] = to_col(m0_ref[d:d + 1, :])
        else:
            m_scr[H_B * d:H_B * (d + 1), 0:1] = jnp.zeros((H_B, 1), F32)

    pad_scr[0:P0, :] = jnp.zeros((P0, C2), F32)
    pad_scr[P0 + T:P + T, :] = jnp.zeros((P - P0, C2), F32)

    def copy_in(i, carry):
        r0 = pl.multiple_of(i * L, L)
        pad_scr[pl.ds(P0 + r0, L), :] = qk_ref[pl.ds(r0, L), :]
        return carry

    lax.fori_loop(0, N, copy_in, 0)

    lane_c = lax.broadcasted_iota(jnp.int32, (1, C2), 1)
    qscale = jnp.where(lane_c < HK, scale, 1.0).astype(F32)
    sub = lax.broadcasted_iota(jnp.int32, (L, 1), 0)
    img_rows = (0,) if rows_img == 1 else (-1, 0, 1)

    def conv_tile(i, carry):
        r0 = pl.multiple_of(i * L, L)
        col = lax.rem(r0, grid_w) + sub
        ok_left = col >= 1
        ok_right = col <= grid_w - 2
        sums = [None, None, None]
        for di in img_rows:
            blk = pad_scr[pl.ds(P0 + r0 + di * grid_w - 8, L + 16), :]
            for k in range(3):
                term = blk * cw_ref[di + 1, k:k + 1, :]
                sums[k] = term if sums[k] is None else sums[k] + term
        acc = (sums[1][8:8 + L, :] + jnp.where(ok_left, sums[0][7:7 + L, :], 0.0)
               + jnp.where(ok_right, sums[2][9:9 + L, :], 0.0))
        qk_scr[pl.ds(r0, L), :] = _silu(acc) * qscale
        return carry

    lax.fori_loop(0, N, conv_tile, 0)

    gl = lane - GATE_LANE0
    is_f = ((gl >= H_B) & (gl < 2 * H_B)) | ((gl >= 3 * H_B) & (gl < 4 * H_B))

    def gate_tile(i, carry):
        rows = pl.ds(pl.multiple_of(i * L, L), L)
        x = sm_ref[rows, :] + bm_ref[...]
        y_scr[rows, :] = jnp.where(is_f, _log_sigmoid(x), x)
        return carry

    lax.fori_loop(0, N, gate_tile, 0)


    def state_group(ns, dirs=(0, 1)):
        units = [(d, n if d == 0 else N - 1 - n) for n in ns for d in dirs]
        rows = [_chunk_rows(n) for _, n in units]
        kt_all = [[qk_scr[r, HK + p * LANES:HK + (p + 1) * LANES].T for p in range(n_pairs)] for r in rows]
        yield
        xs = [y_scr[r, :] for r in rows]
        fsum = [_tri_sum(tri[d], x) for (d, _), x in zip(units, xs)]
        yield
        wk_all, f_end, c_end = [], [], []
        for (d, n), r, x, fs in zip(units, rows, xs, fsum):
            y = jnp.where(is_f, fs, x)
            li0 = GATE_LANE0 + 2 * H_B * d
            blk = jnp.concatenate([y, y], axis=0).T[li0:li0 + 2 * H_B, :]
            frow = pltpu.roll(blk, H_B, axis=0)
            grow = blk - frow
            g_scr[d, n] = grow
            f_scr[d, n] = frow
            e_col = L - 1 if d == 0 else 0
            f_end.append(frow[0:H_B, e_col:e_col + 1])
            ce8 = jnp.max(grow, axis=1, keepdims=True)
            c_end.append(ce8[0:H_B, :])
            wk_all.append(jnp.exp(grow[:, 0:L] - ce8))
        yield
        kv_all, ksum_all = [], []
        for r, wk8, kt_u in zip(rows, wk_all, kt_all):
            kv_u, ks_u = [], []
            wk8b = wk8.astype(BF16)
            for p in range(n_pairs):
                kpb = qk_scr[r, HK + p * LANES:HK + (p + 1) * LANES].astype(BF16)
                ks8 = _dot(wk8b, kpb)
                for j in range(2):
                    h = 2 * p + j
                    kwt = (kt_u[p][j * DK:(j + 1) * DK, :] * wk8[h:h + 1, :]).astype(BF16)
                    kv_u.append(_dot(kwt, v_ref[r, h * DV:(h + 1) * DV].astype(BF16)))
                    ks_u.append(ks8[h:h + 1, :])
            kv_all.append(kv_u)
            ksum_all.append(ks_u)
        yield
        m_run = {d: m_scr[H_B * d:H_B * (d + 1), 0:1] for d in dirs}
        a_all, b_all = [], []
        for (d, n), fe, ce in zip(units, f_end, c_end):
            mall_scr[d, n, 0:H_B, 0:1] = m_run[d]
            mx = jnp.maximum(m_run[d], ce)
            a_all.append(jnp.exp(m_run[d] - mx))
            b_all.append(jnp.exp(ce - mx))
            m_run[d] = fe + mx
        for d in dirs:
            m_scr[H_B * d:H_B * (d + 1), 0:1] = m_run[d]
        yield
        c_run = {d: [[c_scr[d, p, j * DK:(j + 1) * DK, :] for j in range(2)] for p in range(n_pairs)] for d in dirs}
        n_run = {d: [n_scr[2 * d + p:2 * d + p + 1, :] for p in range(n_pairs)] for d in dirs}
        for (d, n), a4, b4, kv_u, ks_u in zip(units, a_all, b_all, kv_all, ksum_all):
            for p in range(n_pairs):
                nall_scr[d, n, p:p + 1, :] = n_run[d][p]
                a_s = [a4[2 * p + j:2 * p + j + 1, :] for j in range(2)]
                b_s = [b4[2 * p + j:2 * p + j + 1, :] for j in range(2)]
                for j in range(2):
                    cj = c_run[d][p][j]
                    call_scr[d, n, p, j * DK:(j + 1) * DK, :] = cj.astype(BF16)
                    c_run[d][p][j] = a_s[j] * cj + b_s[j] * kv_u[2 * p + j]
                n_run[d][p] = (jnp.where(head_mask[0], a_s[0], a_s[1]) * n_run[d][p]
                               + jnp.where(head_mask[0], b_s[0] * ks_u[2 * p], b_s[1] * ks_u[2 * p + 1]))
        for d in dirs:
            for p in range(n_pairs):
                n_scr[2 * d + p:2 * d + p + 1, :] = n_run[d][p]
                for j in range(2):
                    c_scr[d, p, j * DK:(j + 1) * DK, :] = c_run[d][p][j]

    eye = lower & upper
    ones8 = jnp.ones((8, L), BF16)
    sub8 = lax.broadcasted_iota(jnp.int32, (8, LANES), 0)
    sub_h = lax.broadcasted_iota(jnp.int32, (H_B, L), 0)
    n_rows = [((sub8 == 2 * p) & head_mask[0]) | ((sub8 == 2 * p + 1) & head_mask[1]) for p in range(n_pairs)]

    def head_rows(vals):
        out = vals[0][0:H_B, :]
        for h in range(1, H_B):
            out = jnp.where(sub_h == h, vals[h][0:H_B, :], out)
        return out

    def out_group(ns):
        chunks = [(d, n) for n in ns for d in range(2)]
        pairs = [(d, n, p) for d, n in chunks for p in range(n_pairs)]
        units = [(d, n, p, j) for d, n, p in pairs for j in range(2)]
        cms = [lane_cummax(g_scr[d, n], d)[0:H_B, 0:L] for d, n in chunks]
        qk2s, qc2s, qn2s = [], [], []
        for d, n, p in pairs:
            r = _chunk_rows(n)
            qp = qk_scr[r, p * LANES:(p + 1) * LANES]
            q2 = jnp.concatenate([jnp.where(head_mask[j], qp, 0.0) for j in range(2)], axis=0).astype(BF16)
            qk2s.append(_dot_nt(q2, qk_scr[r, HK + p * LANES:HK + (p + 1) * LANES].astype(BF16)))
            qc2s.append(_dot(q2, call_scr[d, n, p]))
            nsel = jnp.where(n_rows[p], nall_scr[d, n, p:p + 1, :], 0.0).astype(BF16)
            qn2s.append(_dot_nt(nsel, qp.astype(BF16)))
        yield
        s_all = []
        for ui, (d, n, p, j) in enumerate(units):
            grow = g_scr[d, n, 2 * p + j:2 * p + j + 1, 0:L]
            e = jnp.where(tmask[d], grow, -jnp.inf)
            cmax = jnp.max(e, axis=-1, keepdims=True)
            s_all.append((qk2s[ui // 2][j * L:(j + 1) * L, :] * jnp.exp(e - cmax)).astype(BF16))
        yield
        nums =[_dot(s, v_ref[_chunk_rows(n), (2 * p + j) * DV:(2 * p + j + 1) * DV].astype(BF16))
                for (d, n, p, j), s in zip(units, s_all)]
        dens = [_dot_nt(ones8, s) for s in s_all]
        yield
        scales = []
        for ci, (d, n) in enumerate(chunks):
            den_loc = head_rows(dens[ci * H_B:(ci + 1) * H_B])
            qn = qn2s[ci * n_pairs][0:H_B, :]
            for p in range(1, n_pairs):
                qn = qn + qn2s[ci * n_pairs + p][0:H_B, :]
            cm = cms[ci]
            m_prev = mall_scr[d, n, 0:H_B, 0:1]
            delta = cm - m_prev
            t = jnp.exp(-jnp.abs(delta))
            w_loc = jnp.where(delta <= 0.0, t, 1.0)
            w_inter = jnp.where(delta <= 0.0, 1.0, t)
            mt = f_scr[d, n, 0:H_B, 0:L] + jnp.maximum(m_prev, cm)
            den = w_loc * den_loc + w_inter * qn
            rinv = 1.0 / jnp.maximum(jnp.abs(den), jnp.exp(-mt))
            scales.append((w_loc * rinv, w_inter * rinv))
        yield
        hs = []
        for ui, (d, n, p, j) in enumerate(units):
            h = 2 * p + j
            sc_loc, sc_inter = scales[ui // H_B]
            d_loc = jnp.where(eye, sc_loc[h:h + 1, :], 0.0).astype(BF16)
            d_inter = jnp.where(eye, sc_inter[h:h + 1, :], 0.0).astype(BF16)
            hs.append(_dot(d_loc, nums[ui].astype(BF16))
                      + _dot(d_inter, qc2s[ui // 2][j * L:(j + 1) * L, :].astype(BF16)))
        yield
        for ni, n in enumerate(ns):
            r = _chunk_rows(n)
            for h in range(H_B):
                vs = slice(h * DV, (h + 1) * DV)
                o = hs[(2 * ni) * H_B + h] + hs[(2 * ni + 1) * H_B + h]
                out_ref[r, vs] = (_rms(o, gw_ref[:, vs]) * _sigmoid(og_ref[r, vs])).astype(out_ref.dtype)

    def finish():
        if write_state:
            for d in range(2):
                for p in range(n_pairs):
                    cnew_ref[d, p] = c_scr[d, p]
                    for j in range(2):
                        nnew_ref[d, 2 * p + j:2 * p + j + 1, :] = n_scr[2 * d + p:2 * d + p + 1, j * DK:(j + 1) * DK]
                mnew_ref[d:d + 1, :] = to_row(m_scr[H_B * d:H_B * (d + 1), 0:1])

    return state_group, out_group, finish


def _mlstm_scratch(T, C2, grid_w):
    n_pairs = C2 // 2 // LANES
    n_chunks = T // CHUNK
    pad_rows = 2 * (grid_w + 8) if T // grid_w > 1 else 16
    return [
        pltpu.VMEM((T + pad_rows, C2), F32),
        pltpu.VMEM((T, C2), F32),
        pltpu.VMEM((T, SMALL_W), F32),
        pltpu.VMEM((2, n_pairs, LANES, LANES), F32),
        pltpu.VMEM((8, LANES), F32),
        pltpu.VMEM((8, LANES), F32),
        pltpu.VMEM((2, n_chunks, n_pairs, LANES, LANES), BF16),
        pltpu.VMEM((2, n_chunks, 8, LANES), F32),
        pltpu.VMEM((2, n_chunks, 8, LANES), F32),
        pltpu.VMEM((2, n_chunks, 8, LANES), F32),
        pltpu.VMEM((2, n_chunks, 8, LANES), F32),
    ]


N_GLA_SCRATCH = 5
N_MLSTM_SCRATCH = 11


def _scan_kernel(*refs, cols, layer, has_state, write_state, n_cast, ride_ada, grid_w, unroll):
    refs = list(refs)
    z_ref = refs.pop(0)
    s0_ref = c0_ref = n0_ref = m0_ref = None
    if has_state:
        s0_ref, c0_ref, n0_ref, m0_ref = refs[:4]
        del refs[:4]
    wa_ref, bal_ref, gwa_ref, cw_ref, bmg_ref, gwb_ref = refs[:6]
    del refs[:6]
    cast_in = refs[:n_cast]
    del refs[:n_cast]
    if ride_ada:
        ada_in = refs[:4]
        del refs[:4]
    outa_ref, outb_ref = refs[:2]
    del refs[:2]
    snew_ref = cnew_ref = nnew_ref = mnew_ref = None
    if write_state:
        snew_ref, cnew_ref, nnew_ref, mnew_ref = refs[:4]
        del refs[:4]
    cast_out = refs[:n_cast]
    del refs[:n_cast]
    if ride_ada:
        ada_out = refs.pop(0)
    wal_scr, bm_scr = refs[:2]
    del refs[:2]
    gla_scr = refs[:N_GLA_SCRATCH]
    mlstm_scr = refs[N_GLA_SCRATCH:]

    for src, dst in zip(cast_in, cast_out):
        dst[...] = src[...].astype(BF16)
    if ride_ada:
        _ada_tile(*ada_in, ada_out)

    R, HK = wa_ref.shape[1], wa_ref.shape[2]
    wal_scr[...] = jnp.zeros(wal_scr.shape, BF16)
    for d in range(2):
        wal_scr[d * R:(d + 1) * R, d * HK:(d + 1) * HK] = wa_ref[d].astype(BF16)
    lane = lax.broadcasted_iota(jnp.int32, (1, LANES), 1)
    bm = jnp.zeros((1, LANES), F32)
    for g in range(bmg_ref.shape[1]):
        for h in range(H_B):
            bm = jnp.where(lane == GATE_LANE0 + H_B * g + h, bmg_ref[layer, g, h], bm)
    bm_scr[0:1, :] = bm

    def view(name):
        c0, w = cols[name]
        return z_ref.at[:, pl.ds(c0, w)]

    sm_ref = view("small")
    n_chunks = z_ref.shape[0] // CHUNK
    gla = _gla_body(view("qa"), view("ka"), view("va"), view("ga"), sm_ref, s0_ref, wal_scr, bal_ref, gwa_ref,
                    outa_ref, snew_ref, *gla_scr)
    mlstm = _mlstm_body(view("qkb"), view("vb"), view("ob"), sm_ref, c0_ref, n0_ref, m0_ref, cw_ref,
                        bm_scr.at[0:1, :], gwb_ref, outb_ref, cnew_ref, nnew_ref, mnew_ref, *mlstm_scr,
                        grid_w=grid_w)
    _chunk_loop(n_chunks, unroll, lambda ns: [fn([n], (d,)) for n in ns for d in range(2)
                                              for fn in (mlstm[0], gla[0])])
    _chunk_loop(n_chunks, unroll, lambda ns: [fn([n]) for n in ns for fn in (mlstm[1], gla[1])])
    gla[2]()
    mlstm[2]()


def _scan_call(z2d, row0, B, T, states, lw, layer, *, grid_w, write_state, casts=(), ada=None):
    n_z = z2d.shape[1]
    assert row0 % T == 0 and z2d.shape[0] % T == 0
    z3 = z2d.reshape(z2d.shape[0] // T, T, n_z)
    blk0 = row0 // T
    HK = lw["w_alpha2"].shape[-1]
    DA = lw["gnorm_a_w"].shape[0]
    C2 = lw["conv_w"].shape[-1]
    DB = lw["gnorm_b_w"].shape[0]
    DK_A, DK_B = HK // H_A, C2 // 2 // H_B
    pa, pb = HK // LANES, C2 // 2 // LANES
    n_chunks = T // CHUNK
    has_state = states is not None
    widths = (("qa", HK), ("ka", HK), ("va", DA), ("ga", DA), ("qkb", C2), ("vb", DB), ("ob", DB),
              ("small", SMALL_W))
    cols, c0 = {}, 0
    for name, w in widths:
        cols[name] = (c0, w)
        c0 += w
    assert c0 == n_z
    cast_in_specs, cast_out_specs, cast_out_shape, cast_args = _cast_specs(casts, B)
    kern = functools.partial(_scan_kernel, cols=cols, layer=layer, has_state=has_state, write_state=write_state,
                             n_cast=len(casts), ride_ada=ada is not None, grid_w=grid_w,
                             unroll=min(n_chunks, SCAN_UNROLL))

    def per_batch(shape):
        nd = len(shape)
        return pl.BlockSpec((None,) + tuple(shape), lambda b: (b,) + (0,) * nd)

    def per_batch_layer(shape):
        nd = len(shape)
        return pl.BlockSpec((None, None) + tuple(shape), lambda b: (b, layer) + (0,) * nd)

    def of_layer(a):
        return pl.BlockSpec((None,) + a.shape[1:], lambda b: (layer,) + (0,) * (a.ndim - 1))

    def whole(a):
        return pl.BlockSpec(a.shape, lambda b: (0,) * a.ndim)

    state_shapes = ((2, pa, LANES, LANES), (2, pb, LANES, LANES), (2, H_B, DK_B), (2, H_B))
    in_specs = [pl.BlockSpec((None, T, n_z), lambda b: (b + blk0, 0, 0))]
    args = [z3]
    if has_state:
        s_gla, s_c, s_n, s_m = states
        depth = s_gla.shape[1]
        args += [s_gla.reshape((B, depth) + state_shapes[0]), s_c.reshape((B, depth) + state_shapes[1]), s_n, s_m]
        in_specs += [per_batch_layer(s) for s in state_shapes]
    args += [lw["w_alpha2"], lw["b_alpha"], lw["gnorm_a_w"].reshape(1, DA), lw["conv_w"], lw["b_mgate"],
             lw["gnorm_b_w"].reshape(1, DB)]
    in_specs += [of_layer(lw["w_alpha2"]), of_layer(lw["b_alpha"]), pl.BlockSpec((1, DA), lambda b: (0, 0)),
                 whole(lw["conv_w"]), pl.BlockSpec(memory_space=pltpu.SMEM), pl.BlockSpec((1, DB), lambda b: (0, 0))]
    args += cast_args
    in_specs += cast_in_specs
    out_specs = [per_batch((T, DA)), per_batch((T, DB))]
    out_shape = [jax.ShapeDtypeStruct((B, T, DA), BF16), jax.ShapeDtypeStruct((B, T, DB), BF16)]
    if write_state:
        out_specs += [per_batch(s) for s in state_shapes]
        out_shape += [jax.ShapeDtypeStruct((B,) + s, F32) for s in state_shapes]
    out_specs += cast_out_specs
    out_shape += cast_out_shape
    if ada is not None:
        cc, c, w_ada, b_ada, col0 = ada
        n_rest = w_ada.shape[1] - col0
        wcol = n_rest // B
        assert n_rest % B == 0 and wcol % LANES == 0 and col0 % wcol == 0
        args += [cc, c, w_ada, b_ada]
        in_specs += [whole(cc), whole(c),
                     pl.BlockSpec((w_ada.shape[0], wcol), lambda b: (0, col0 // wcol + b)),
                     pl.BlockSpec((1, wcol), lambda b: (0, col0 // wcol + b))]
        out_specs.append(pl.BlockSpec((COND_ROWS, wcol), lambda b: (0, b)))
        out_shape.append(jax.ShapeDtypeStruct((COND_ROWS, n_rest), F32))
    scratch = ([pltpu.VMEM((SMALL_W, 2 * HK), BF16), pltpu.VMEM((8, LANES), F32)]
               + _gla_scratch(T, HK) + _mlstm_scratch(T, C2, grid_w))
    assert len(scratch) == 2 + N_GLA_SCRATCH + N_MLSTM_SCRATCH
    return pl.pallas_call(
        kern,
        grid=(B,),
        in_specs=in_specs,
        out_specs=out_specs,
        out_shape=out_shape,
        scratch_shapes=scratch,
        compiler_params=pltpu.CompilerParams(dimension_semantics=("arbitrary",),
                                             vmem_limit_bytes=VMEM_LIMIT),
        name="mixer_scans",
    )(*args)


def _outff_kernel(xc_ref, xl_ref, ac_ref, al_ref, bc_ref, bl_ref, mod_ref, n2_ref, fn_ref, wo_ref, w1_ref, w2_ref,
                  yc_ref, yl_ref, *, n_ctx, tiles_per_req, ff_chunk, final_norm):
    D = xc_ref.shape[1]
    DA = ac_ref.shape[1]
    is_ctx, row = _tile_group(n_ctx, tiles_per_req)

    def mod(k):
        return mod_ref[pl.ds(row, 1), (k - MOD_SPLIT) * D:(k - MOD_SPLIT + 1) * D]

    def tile(x_ref, a_ref, b_ref, y_ref):
        y = _dot(a_ref[...], wo_ref[0:DA, :]) + _dot(b_ref[...], wo_ref[DA:, :])
        x1 = x_ref[...] + mod(2) * y
        h2 = (_rms(x1, n2_ref[...]) * (1.0 + mod(4)) + mod(3)).astype(BF16)
        acc = jnp.zeros(x1.shape, F32)
        for c0 in range(0, w1_ref.shape[1], ff_chunk):
            u = jnp.maximum(_dot(h2, w1_ref[:, c0:c0 + ff_chunk]), 0.0)
            acc = acc + _dot((u * u).astype(BF16), w2_ref[c0:c0 + ff_chunk, :])
        x2 = x1 + mod(5) * acc
        y_ref[...] = _rms(x2, fn_ref[...]) if final_norm else x2

    @pl.when(is_ctx)
    def _():
        tile(xc_ref, ac_ref, bc_ref, yc_ref)

    @pl.when(jnp.logical_not(is_ctx))
    def _():
        tile(xl_ref, al_ref, bl_ref, yl_ref)


def _outff_call(xc2d, xl2d, ac, al, bc, bl, mod, norm2_w, final_w, wo, w1, w2, *, tm, tiles_per_req, final_norm):
    (Mc, D), Ml = xc2d.shape, xl2d.shape[0]
    n_ctx = Mc // tm
    DA = ac.shape[1]
    DFF = w1.shape[1]
    kern = functools.partial(_outff_kernel, n_ctx=n_ctx, tiles_per_req=tiles_per_req, ff_chunk=FF_CHUNK,
                             final_norm=final_norm)
    once = pl.Buffered(1)
    ctx, lat = _ctx_tile(n_ctx), _lat_tile(n_ctx)
    return pl.pallas_call(
        kern,
        grid=((Mc + Ml) // tm,),
        in_specs=[
            pl.BlockSpec((tm, D), ctx), pl.BlockSpec((tm, D), lat),
            pl.BlockSpec((tm, DA), ctx), pl.BlockSpec((tm, DA), lat),
            pl.BlockSpec((tm, D - DA), ctx), pl.BlockSpec((tm, D - DA), lat),
            pl.BlockSpec(mod.shape, lambda i: (0, 0)),
            pl.BlockSpec((1, D), lambda i: (0, 0)),
            pl.BlockSpec((1, D), lambda i: (0, 0)),
            pl.BlockSpec((D, D), lambda i: (0, 0), pipeline_mode=once),
            pl.BlockSpec((D, DFF), lambda i: (0, 0), pipeline_mode=once),
            pl.BlockSpec((DFF, D), lambda i: (0, 0), pipeline_mode=once),
        ],
        out_specs=[pl.BlockSpec((tm, D), ctx), pl.BlockSpec((tm, D), lat)],
        out_shape=[jax.ShapeDtypeStruct((Mc, D), F32), jax.ShapeDtypeStruct((Ml, D), F32)],
        compiler_params=pltpu.CompilerParams(dimension_semantics=("arbitrary",),
                                             vmem_limit_bytes=VMEM_LIMIT),
        name="outproj_mlp",
    )(xc2d, xl2d, ac, al, bc, bl, mod, norm2_w.reshape(1, D), final_w.reshape(1, D), wo, w1, w2)


def _layer(xc, xl, cond, ada_w, cached, lw, layer, ffw, final_w, final_norm):
    (Bc, Tc, D), (Bl, Tl, _) = xc.shape, xl.shape
    tm = TOKEN_TILE
    assert (Bc * Tc) % tm == 0 and Tl % tm == 0 and (Bc * Tc) % Tl == 0
    xc2d, xl2d = xc.reshape(Bc * Tc, D), xl.reshape(Bl * Tl, D)
    mod_in = _ada_call(*cond, *ada_w, MOD_SPLIT * D)
    z = _inproj_call(xc2d, xl2d, mod_in, lw["norm1_w"], lw["w_in_t"], tm=tm, tiles_per_req=Tl // tm,
                     big_rows=lw["big_rows"], small_rows=lw["small_rows"])
    res_c = _scan_call(z, 0, Bc, Tc, None, lw, layer, grid_w=Tc, write_state=True,
                       casts=((ffw[0], 0), (ffw[1], 1), (ffw[2], 0)), ada=(*cond, *ada_w, MOD_SPLIT * D))
    res_l = _scan_call(z, Bc * Tc, Bl, Tl, cached, lw, layer, grid_w=GRID_W, write_state=False)
    wo_b, w1_b, w2_b, mod_out = res_c[-4:]
    yc, yl = _outff_call(xc2d, xl2d, res_c[0].reshape(Bc * Tc, -1), res_l[0].reshape(Bl * Tl, -1),
                         res_c[1].reshape(Bc * Tc, -1), res_l[1].reshape(Bl * Tl, -1), mod_out, lw["norm2_w"],
                         final_w, wo_b, w1_b, w2_b, tm=tm, tiles_per_req=Tl // tm, final_norm=final_norm)
    return yc.reshape(Bc, Tc, D), yl.reshape(Bl, Tl, D), tuple(res_c[2:6])


def _layer_weights(l, norm1_w, norm2_w, w_in, w_alpha2, b_alpha, b_mgate, conv_w, gnorm_a_w, gnorm_b_w):
    hk_a = w_alpha2.shape[-1]
    d_a = gnorm_a_w.shape[-1]
    d_b = gnorm_b_w.shape[-1]
    hk_b = conv_w.shape[-1] // 2
    sizes = (hk_a, hk_a, d_a, d_a, 2 * R_ALPHA, hk_b, hk_b, d_b, d_b, 4 * H_B)
    assert w_alpha2.shape[2] == R_ALPHA and b_mgate.shape[1] * b_mgate.shape[2] == 4 * H_B
    offs = [0]
    for s in sizes:
        offs.append(offs[-1] + s)
    big_rows = ((offs[0], offs[4] - offs[0]), (offs[5], offs[9] - offs[5]))
    small_rows = ((offs[4], offs[5] - offs[4]), (offs[9], offs[10] - offs[9]))
    assert all(n % LANES == 0 and r % 16 == 0 for r, n in big_rows)
    return dict(
        norm1_w=norm1_w[l], norm2_w=norm2_w[l], w_in_t=jnp.swapaxes(w_in[l], 0, 1),
        big_rows=big_rows, small_rows=small_rows,
        w_alpha2=w_alpha2, b_alpha=b_alpha, b_mgate=b_mgate, conv_w=conv_w[l],
        gnorm_a_w=gnorm_a_w[l], gnorm_b_w=gnorm_b_w[l],
    )


def kernel(x_prompt, x_sample, c, state_gla, state_mlstm_C, state_mlstm_n, state_mlstm_m, c_ctx, w_ada, b_ada, norm1_w, norm2_w, w_in, w_alpha2, b_alpha, b_mgate, conv_w, gnorm_a_w, gnorm_b_w, w_out, w_ff1, w_ff2, final_norm_w):
    depth = w_in.shape[0]
    D = x_prompt.shape[-1]
    Bp, Tp, _ = x_prompt.shape
    Bs = x_sample.shape[0]
    assert 1 + Bs <= COND_ROWS
    cond = (c_ctx.reshape(1, D), c)
    cached = (state_gla, state_mlstm_C, state_mlstm_n, state_mlstm_m)
    xp, xs = x_prompt, x_sample
    s_gla, s_c, s_n, s_m = [], [], [], []
    for l in range(depth):
        lw = _layer_weights(l, norm1_w, norm2_w, w_in, w_alpha2, b_alpha, b_mgate, conv_w,
                            gnorm_a_w, gnorm_b_w)
        xp, xs, ctx = _layer(xp, xs, cond, (w_ada[l], b_ada[l].reshape(1, -1)), cached, lw, l,
                             (w_out[l], w_ff1[l], w_ff2[l]), final_norm_w, l == depth - 1)
        s_gla.append(ctx[0].reshape(Bp, 2, H_A, -1, ctx[0].shape[-1]))
        s_c.append(ctx[1].reshape(Bp, 2, H_B, -1, ctx[1].shape[-1]))
        s_n.append(ctx[2])
        s_m.append(ctx[3])
    dt = x_prompt.dtype
    return (xp, xs, jnp.stack(s_gla, axis=1).astype(dt), jnp.stack(s_c, axis=1).astype(dt),
            jnp.stack(s_n, axis=1).astype(dt), jnp.stack(s_m, axis=1).astype(dt))
```

```python
import functools

import jax
import jax.numpy as jnp
from jax import lax
from jax.experimental import pallas as pl
from jax.experimental.pallas import tpu as pltpu

F32 = jnp.float32
BF16 = jnp.bfloat16

GRID_W = 64
H_A = 4
H_B = 4
R_ALPHA = 16
TAU_GLA = 16.0
CHUNK = 64
EPS = 1e-6
LANES = 128
COND_ROWS = 8
SMALL_W = LANES
GATE_LANE0 = 2 * R_ALPHA
VMEM_LIMIT = 56 * 1024 * 1024
SCAN_UNROLL = 4
PIPELINE_STARTS = 4
TOKEN_TILE = 512
FF_CHUNK = 512
GLA_FACTORED_DECAY_MAX = 60.0
MOD_SPLIT = 2


def _sigmoid(x):
    return 1.0 / (1.0 + jnp.exp(-x))


def _silu(x):
    return x * _sigmoid(x)


def _log_sigmoid(x):
    return jnp.minimum(x, 0.0) - jnp.log1p(jnp.exp(-jnp.abs(x)))


def _dot(a, b):
    return jnp.dot(a, b, preferred_element_type=F32)


def _dot_nt(a, b):
    return lax.dot_general(a, b, (((1,), (1,)), ((), ())), preferred_element_type=F32)


def _rms(x, w):
    return x * lax.rsqrt(jnp.mean(x * x, axis=-1, keepdims=True) + EPS) * w


def _tri_sum(tri, x):
    hi = x.astype(BF16)
    r1 = x - hi.astype(F32)
    mid = r1.astype(BF16)
    lo = (r1 - mid.astype(F32)).astype(BF16)
    return _dot(tri, hi) + _dot(tri, mid) + _dot(tri, lo)


def _chunk_masks(L):
    row = lax.broadcasted_iota(jnp.int32, (L, L), 0)
    col = lax.broadcasted_iota(jnp.int32, (L, L), 1)
    lower = row >= col
    upper = row <= col
    return lower, upper


def _ada_tile(cc_ref, c_ref, w_ref, b_ref, o_ref):
    D = cc_ref.shape[1]
    sub = lax.broadcasted_iota(jnp.int32, (COND_ROWS, D), 0)
    cond = jnp.where(sub == 0, cc_ref[...], 0.0)
    for r in range(c_ref.shape[0]):
        cond = jnp.where(sub == 1 + r, c_ref[r:r + 1, :], cond)
    o_ref[...] = _dot(_silu(cond).astype(BF16), w_ref[...].astype(BF16)) + b_ref[...]


def _ada_call(cc, c, w_ada, b_ada, n_cols):
    D = cc.shape[1]
    tn = 1024
    return pl.pallas_call(
        _ada_tile,
        grid=(n_cols // tn,),
        in_specs=[
            pl.BlockSpec(cc.shape, lambda j: (0, 0)),
            pl.BlockSpec(c.shape, lambda j: (0, 0)),
            pl.BlockSpec((D, tn), lambda j: (0, j)),
            pl.BlockSpec((1, tn), lambda j: (0, j)),
        ],
        out_specs=pl.BlockSpec((COND_ROWS, tn), lambda j: (0, j)),
        out_shape=jax.ShapeDtypeStruct((COND_ROWS, n_cols), F32),
        compiler_params=pltpu.CompilerParams(dimension_semantics=("arbitrary",),
                                             vmem_limit_bytes=VMEM_LIMIT),
        name="ada_mod",
    )(cc, c, w_ada, b_ada)


def _tile_group(n_ctx, tiles_per_req):
    i = pl.program_id(0)
    is_ctx = i < n_ctx
    row = jnp.where(is_ctx, 0, 1 + jnp.maximum(i - n_ctx, 0) // tiles_per_req)
    return is_ctx, row


def _ctx_tile(n_ctx):
    return lambda i: (jnp.minimum(i, n_ctx - 1), 0)


def _lat_tile(n_ctx):
    return lambda i: (jnp.maximum(i - n_ctx, 0), 0)


def _inproj_kernel(xc_ref, xl_ref, mod_ref, nw_ref, wt_ref, z_ref, wb_scr, *, n_ctx, tiles_per_req, big_rows,
                   small_rows):
    D = xc_ref.shape[1]

    @pl.when(pl.program_id(0) == 0)
    def _():
        col = 0
        for r0, n in big_rows:
            for k in range(n // LANES):
                blk = wt_ref[r0 + k * LANES:r0 + (k + 1) * LANES, :]
                wb_scr[:, col:col + LANES] = blk.T.astype(BF16)
                col += LANES
        parts = [wt_ref[r0:r0 + n, :] for r0, n in small_rows]
        n_small = sum(n for _, n in small_rows)
        parts.append(jnp.zeros((SMALL_W - n_small, D), F32))
        wb_scr[:, col:col + SMALL_W] = jnp.concatenate(parts, axis=0).T.astype(BF16)

    is_ctx, row = _tile_group(n_ctx, tiles_per_req)

    def tile(x_ref):
        sh1 = mod_ref[pl.ds(row, 1), 0:D]
        sc1 = mod_ref[pl.ds(row, 1), D:2 * D]
        h = _rms(x_ref[...], nw_ref[...]) * (1.0 + sc1) + sh1
        z_ref[...] = _dot(h.astype(BF16), wb_scr[...])

    @pl.when(is_ctx)
    def _():
        tile(xc_ref)

    @pl.when(jnp.logical_not(is_ctx))
    def _():
        tile(xl_ref)


def _inproj_call(xc2d, xl2d, mod, norm_w, w_in_t, *, tm, tiles_per_req, big_rows, small_rows):
    (Mc, D), Ml = xc2d.shape, xl2d.shape[0]
    n_ctx = Mc // tm
    n_out = sum(n for _, n in big_rows) + SMALL_W
    kern = functools.partial(_inproj_kernel, n_ctx=n_ctx, tiles_per_req=tiles_per_req,
                             big_rows=big_rows, small_rows=small_rows)
    return pl.pallas_call(
        kern,
        grid=((Mc + Ml) // tm,),
        in_specs=[
            pl.BlockSpec((tm, D), _ctx_tile(n_ctx)),
            pl.BlockSpec((tm, D), _lat_tile(n_ctx)),
            pl.BlockSpec(mod.shape, lambda i: (0, 0)),
            pl.BlockSpec((1, D), lambda i: (0, 0)),
            pl.BlockSpec(w_in_t.shape, lambda i: (0, 0), pipeline_mode=pl.Buffered(1)),
        ],
        out_specs=pl.BlockSpec((tm, n_out), lambda i: (i, 0)),
        out_shape=jax.ShapeDtypeStruct((Mc + Ml, n_out), F32),
        scratch_shapes=[pltpu.VMEM((D, n_out), BF16)],
        compiler_params=pltpu.CompilerParams(dimension_semantics=("arbitrary",),
                                             vmem_limit_bytes=VMEM_LIMIT),
        name="norm_inproj",
    )(xc2d, xl2d, mod, norm_w.reshape(1, D), w_in_t)


def _chunk_loop(n_chunks, unroll, make_units):
    def step(ns):
        pending = list(make_units(ns))
        active = []
        while pending or active:
            for _ in range(min(PIPELINE_STARTS, len(pending))):
                active.append(pending.pop(0))
            alive = []
            for g in active:
                try:
                    next(g)
                    alive.append(g)
                except StopIteration:
                    pass
            active = alive

    if unroll >= n_chunks:
        step(list(range(n_chunks)))
        return

    def body(i, carry):
        step([i * unroll + u for u in range(unroll)])
        return carry

    lax.fori_loop(0, n_chunks // unroll, body, 0)


def _chunk_rows(n):
    if isinstance(n, int):
        return pl.ds(n * CHUNK, CHUNK)
    return pl.ds(pl.multiple_of(n * CHUNK, CHUNK), CHUNK)


def _cast_specs(casts, n_steps):
    in_specs, out_specs, out_shape, args = [], [], [], []
    for w, axis in casts:
        blk = list(w.shape)
        assert blk[axis] % n_steps == 0
        blk[axis] //= n_steps
        assert blk[0] % 16 == 0 and blk[1] % LANES == 0
        idx = (lambda b: (b, 0)) if axis == 0 else (lambda b: (0, b))
        in_specs.append(pl.BlockSpec(tuple(blk), idx))
        out_specs.append(pl.BlockSpec(tuple(blk), idx))
        out_shape.append(jax.ShapeDtypeStruct(w.shape, BF16))
        args.append(w)
    return in_specs, out_specs, out_shape, args


def _gla_body(q_ref, k_ref, v_ref, g_ref, sm_ref, s0_ref, wal_ref, bal_ref, gw_ref, out_ref, snew_ref,
              st_scr, sall_scr, qh_scr, qs_scr, kh_scr, rng_scr):
    has_state = s0_ref is not None
    write_state = snew_ref is not None
    T = q_ref.shape[0]
    L = CHUNK
    N = T // L
    HK = q_ref.shape[1]
    DK = HK // H_A
    DV = v_ref.shape[1] // H_A
    scale = DK ** -0.5
    n_pairs = HK // LANES

    lower, upper = _chunk_masks(L)
    tri = (lower.astype(BF16), upper.astype(BF16))
    tmask = (lower, upper)
    lane = lax.broadcasted_iota(jnp.int32, (1, LANES), 1)
    head_mask = (lane < DK, lane >= DK)

    for d in range(2):
        for p in range(n_pairs):
            if has_state:
                st_scr[d, p] = s0_ref[d, p].T
            else:
                st_scr[d, p] = jnp.zeros((LANES, LANES), F32)
    rng_scr[...] = jnp.zeros(rng_scr.shape, F32)

    def log_decay(d, r):
        pre = _dot(sm_ref[r, :].astype(BF16), wal_ref[:, d * HK:(d + 1) * HK]) + bal_ref[d:d + 1, :]
        return _tri_sum(tri[d], _log_sigmoid(pre) * (1.0 / TAU_GLA))

    def state_group(ns, dirs=(0, 1)):
        units = [(d, n if d == 0 else N - 1 - n) for n in ns for d in dirs]
        rows = [_chunk_rows(n) for _, n in units]
        vt_all = [[jnp.concatenate([v_ref[r, (2 * p + j) * DV:(2 * p + j + 1) * DV] for j in range(2)],
                                   axis=0).T.astype(BF16) for p in range(n_pairs)] for r in rows]
        yield
        pre = [_dot(sm_ref[r, :].astype(BF16), wal_ref[:, d * HK:(d + 1) * HK]) + bal_ref[d:d + 1, :]
               for (d, _), r in zip(units, rows)]
        yield
        g = [_log_sigmoid(x) * (1.0 / TAU_GLA) for x in pre]
        yield
        b = [_tri_sum(tri[d], gi) for (d, _), gi in zip(units, g)]
        yield
        ks_all, dec_all, span_all = [], [], []
        for (d, _), r, bi in zip(units, rows, b):
            bend = bi[L - 1:L, :] if d == 0 else bi[0:1, :]
            span_all.append(jnp.max(-bend, axis=1, keepdims=True))
            q = q_ref[r, :] * scale
            ks = (k_ref[r, :] * jnp.exp(bend - bi)).astype(BF16)
            qh_scr[d, r, :] = (q * jnp.exp(bi - bend)).astype(BF16)
            qs_scr[d, r, :] = (q * jnp.exp(bi)).astype(BF16)
            kh_scr[d, r, :] = ks
            ks_all.append(ks)
            dec_all.append(jnp.exp(bend))
        yield
        upd_all = []
        for vt_u, ks in zip(vt_all, ks_all):
            upd_u = []
            for p in range(n_pairs):
                kp = ks[:, p * LANES:(p + 1) * LANES]
                kk = jnp.concatenate([jnp.where(head_mask[j], kp, jnp.zeros_like(kp)) for j in range(2)], axis=0)
                upd_u.append(_dot(vt_u[p], kk))
            upd_all.append(upd_u)
        yield
        st = {d: [st_scr[d, p] for p in range(n_pairs)] for d in dirs}
        for (d, n), dec, upd in zip(units, dec_all, upd_all):
            for p in range(n_pairs):
                sall_scr[d, n, p] = st[d][p].astype(BF16)
                st[d][p] = st[d][p] * dec[:, p * LANES:(p + 1) * LANES] + upd[p]
        for d in dirs:
            for p in range(n_pairs):
                st_scr[d, p] = st[d][p]
        span = span_all[0]
        for s in span_all[1:]:
            span = jnp.maximum(span, s)
        rng_scr[0:1, :] = jnp.maximum(rng_scr[0:1, :], span)

    def decay_range():
        return jnp.max(rng_scr[0:1, :])

    def stack_heads(x):
        return jnp.concatenate([jnp.where(head_mask[j], x, jnp.zeros_like(x)) for j in range(2)], axis=0)

    tok = lax.broadcasted_iota(jnp.int32, (L, 1), 0)
    row_t = lax.broadcasted_iota(jnp.int32, (2 * L, L), 0) & (L - 1)
    col_s = lax.broadcasted_iota(jnp.int32, (2 * L, L), 1)

    def exact_scores(d, r, p):
        ls = slice(p * LANES, (p + 1) * LANES)
        b = log_decay(d, r)[:, ls]
        q = q_ref[r, ls] * scale
        k = k_ref[r, ls]
        acc = jnp.where(row_t == col_s, _dot_nt(stack_heads(q).astype(BF16), k.astype(BF16)), 0.0)
        src = lax.broadcasted_iota(jnp.int32, (L, L), 1)
        h = L // 2
        while h >= 1:
            first = tok & ~(2 * h - 1)
            edge = first + (h - 1 if d == 0 else h)
            b_edge = _tri_sum((src == edge).astype(BF16), b)
            upper = (tok & (2 * h - 1)) >= h
            later, earlier = (upper, ~upper) if d == 0 else (~upper, upper)
            qt = jnp.where(later, q * jnp.exp(b - b_edge), 0.0)
            kt = jnp.where(earlier, k * jnp.exp(b_edge - b), 0.0)
            sc = _dot_nt(stack_heads(qt).astype(BF16), kt.astype(BF16))
            acc = acc + jnp.where((row_t & ~(2 * h - 1)) == (col_s & ~(2 * h - 1)), sc, 0.0)
            h //= 2
        return acc

    def out_group(ns, exact_decay=False):
        pairs = [(d, ni, p) for ni in range(len(ns)) for d in range(2) for p in range(n_pairs)]
        scores, inter = [], []
        for d, ni, p in pairs:
            r = _chunk_rows(ns[ni])
            ls = slice(p * LANES, (p + 1) * LANES)
            if exact_decay:
                scores.append(exact_scores(d, r, p))
            else:
                scores.append(_dot_nt(stack_heads(qh_scr[d, r, ls]), kh_scr[d, r, ls]))
            inter.append(_dot_nt(stack_heads(qs_scr[d, r, ls]), sall_scr[d, ns[ni], p]))
        yield
        probs = [[jnp.where(tmask[d], sc[j * L:(j + 1) * L, :], 0.0).astype(BF16) for j in range(2)]
                 for (d, _, _), sc in zip(pairs, scores)]
        yield
        outs = {}
        for (d, ni, p), pr, it in zip(pairs, probs, inter):
            r = _chunk_rows(ns[ni])
            for j in range(2):
                vs = slice((2 * p + j) * DV, (2 * p + j + 1) * DV)
                outs[(d, ni, 2 * p + j)] = _dot(pr[j], v_ref[r, vs].astype(BF16)) + it[j * L:(j + 1) * L, :]
        yield
        for ni, n in enumerate(ns):
            r = _chunk_rows(n)
            for h in range(H_A):
                vs = slice(h * DV, (h + 1) * DV)
                o = outs[(0, ni, h)] + outs[(1, ni, h)]
                out_ref[r, vs] = (_rms(o, gw_ref[:, vs]) * _silu(g_ref[r, vs])).astype(out_ref.dtype)

    def finish():
        if write_state:
            for d in range(2):
                for p in range(n_pairs):
                    snew_ref[d, p] = st_scr[d, p].T

    return state_group, out_group, finish, decay_range


def _gla_scratch(T, HK):
    n_pairs = HK // LANES
    n_chunks = T // CHUNK
    return [
        pltpu.VMEM((2, n_pairs, LANES, LANES), F32),
        pltpu.VMEM((2, n_chunks, n_pairs, LANES, LANES), BF16),
        pltpu.VMEM((2, T, HK), BF16),
        pltpu.VMEM((2, T, HK), BF16),
        pltpu.VMEM((2, T, HK), BF16),
        pltpu.VMEM((8, LANES), F32),
    ]


def _mlstm_body(qk_ref, v_ref, og_ref, sm_ref, c0_ref, n0_ref, m0_ref, cw_ref, bm_ref, gw_ref,
                out_ref, cnew_ref, nnew_ref, mnew_ref,
                pad_scr, qk_scr, y_scr, c_scr, n_scr, m_scr, call_scr, nall_scr, mall_scr, g_scr, f_scr,
                *, grid_w):
    has_state = c0_ref is not None
    write_state = cnew_ref is not None
    T = qk_ref.shape[0]
    L = CHUNK
    N = T // L
    C2 = qk_ref.shape[1]
    HK = C2 // 2
    DK = HK // H_B
    DV = v_ref.shape[1] // H_B
    scale = DK ** -0.5
    n_pairs = HK // LANES
    P = pad_scr.shape[0] - T
    P0 = P // 2
    rows_img = T // grid_w

    lower, upper = _chunk_masks(L)
    tri = (lower.astype(BF16), upper.astype(BF16))
    tmask = (lower, upper)
    lane = lax.broadcasted_iota(jnp.int32, (1, LANES), 1)
    head_mask = (lane < DK, lane >= DK)
    lane_in = lane & (L - 1)

    def lane_cummax(x, d):
        k = 1
        while k < L:
            if d == 0:
                x = jnp.maximum(x, jnp.where(lane_in >= k, pltpu.roll(x, k, axis=1), -jnp.inf))
            else:
                x = jnp.maximum(x, jnp.where(lane_in < L - k, pltpu.roll(x, LANES - k, axis=1), -jnp.inf))
            k *= 2
        return x

    for d in range(2):
        for p in range(n_pairs):
            if has_state:
                c_scr[d, p] = c0_ref[d, p]
                n_scr[2 * d + p:2 * d + p + 1, :] = jnp.concatenate(
                    [n0_ref[d, 2 * p + j:2 * p + j + 1, :] for j in range(2)], axis=1)
            else:
                c_scr[d, p] = jnp.zeros((LANES, LANES), F32)
                n_scr[2 * d + p:2 * d + p + 1, :] = jnp.zeros((1, LANES), F32)
    eye_h = (lax.broadcasted_iota(jnp.int32, (H_B, H_B), 0) == lax.broadcasted_iota(jnp.int32, (H_B, H_B), 1))

    def to_col(row):
        return jnp.sum(jnp.where(eye_h, row, 0.0), axis=1, keepdims=True)

    def to_row(col):
        return jnp.sum(jnp.where(eye_h, col, 0.0), axis=0, keepdims=True)

    for d in range(2):
        if has_state:
            m_scr[H_B * d:H_B * (d + 1), 0:1] = to_col(m0_ref[d:d + 1, :])
        else:
            m_scr[H_B * d:H_B * (d + 1), 0:1] = jnp.zeros((H_B, 1), F32)

    pad_scr[0:P0, :] = jnp.zeros((P0, C2), F32)
    pad_scr[P0 + T:P + T, :] = jnp.zeros((P - P0, C2), F32)

    def copy_in(i, carry):
        r0 = pl.multiple_of(i * L, L)
        pad_scr[pl.ds(P0 + r0, L), :] = qk_ref[pl.ds(r0, L), :]
        return carry

    lax.fori_loop(0, N, copy_in, 0)

    lane_c = lax.broadcasted_iota(jnp.int32, (1, C2), 1)
    qscale = jnp.where(lane_c < HK, scale, 1.0).astype(F32)
    sub = lax.broadcasted_iota(jnp.int32, (L, 1), 0)
    img_rows = (0,) if rows_img == 1 else (-1, 0, 1)

    def conv_tile(i, carry):
        r0 = pl.multiple_of(i * L, L)
        col = lax.rem(r0, grid_w) + sub
        ok_left = col >= 1
        ok_right = col <= grid_w - 2
        sums = [None, None, None]
        for di in img_rows:
            blk = pad_scr[pl.ds(P0 + r0 + di * grid_w - 8, L + 16), :]
            for k in range(3):
                term = blk * cw_ref[di + 1, k:k + 1, :]
                sums[k] = term if sums[k] is None else sums[k] + term
        acc = (sums[1][8:8 + L, :] + jnp.where(ok_left, sums[0][7:7 + L, :], 0.0)
               + jnp.where(ok_right, sums[2][9:9 + L, :], 0.0))
        qk_scr[pl.ds(r0, L), :] = _silu(acc) * qscale
        return carry

    lax.fori_loop(0, N, conv_tile, 0)

    gl = lane - GATE_LANE0
    is_f = ((gl >= H_B) & (gl < 2 * H_B)) | ((gl >= 3 * H_B) & (gl < 4 * H_B))

    def gate_tile(i, carry):
        rows = pl.ds(pl.multiple_of(i * L, L), L)
        x = sm_ref[rows, :] + bm_ref[...]
        y_scr[rows, :] = jnp.where(is_f, _log_sigmoid(x), x)
        return carry

    lax.fori_loop(0, N, gate_tile, 0)


    def state_group(ns, dirs=(0, 1)):
        units = [(d, n if d == 0 else N - 1 - n) for n in ns for d in dirs]
        rows = [_chunk_rows(n) for _, n in units]
        kt_all = [[qk_scr[r, HK + p * LANES:HK + (p + 1) * LANES].T for p in range(n_pairs)] for r in rows]
        yield
        xs = [y_scr[r, :] for r in rows]
        fsum = [_tri_sum(tri[d], x) for (d, _), x in zip(units, xs)]
        yield
        wk_all, f_end, c_end = [], [], []
        for (d, n), r, x, fs in zip(units, rows, xs, fsum):
            y = jnp.where(is_f, fs, x)
            li0 = GATE_LANE0 + 2 * H_B * d
            blk = jnp.concatenate([y, y], axis=0).T[li0:li0 + 2 * H_B, :]
            frow = pltpu.roll(blk, H_B, axis=0)
            grow = blk - frow
            g_scr[d, n] = grow
            f_scr[d, n] = frow
            e_col = L - 1 if d == 0 else 0
            f_end.append(frow[0:H_B, e_col:e_col + 1])
            ce8 = jnp.max(grow, axis=1, keepdims=True)
            c_end.append(ce8[0:H_B, :])
            wk_all.append(jnp.exp(grow[:, 0:L] - ce8))
        yield
        kv_all, ksum_all = [], []
        for r, wk8, kt_u in zip(rows, wk_all, kt_all):
            kv_u, ks_u = [], []
            wk8b = wk8.astype(BF16)
            for p in range(n_pairs):
                kpb = qk_scr[r, HK + p * LANES:HK + (p + 1) * LANES].astype(BF16)
                ks8 = _dot(wk8b, kpb)
                for j in range(2):
                    h = 2 * p + j
                    kwt = (kt_u[p][j * DK:(j + 1) * DK, :] * wk8[h:h + 1, :]).astype(BF16)
                    kv_u.append(_dot(kwt, v_ref[r, h * DV:(h + 1) * DV].astype(BF16)))
                    ks_u.append(ks8[h:h + 1, :])
            kv_all.append(kv_u)
            ksum_all.append(ks_u)
        yield
        m_run = {d: m_scr[H_B * d:H_B * (d + 1), 0:1] for d in dirs}
        a_all, b_all = [], []
        for (d, n), fe, ce in zip(units, f_end, c_end):
            mall_scr[d, n, 0:H_B, 0:1] = m_run[d]
            mx = jnp.maximum(m_run[d], ce)
            a_all.append(jnp.exp(m_run[d] - mx))
            b_all.append(jnp.exp(ce - mx))
            m_run[d] = fe + mx
        for d in dirs:
            m_scr[H_B * d:H_B * (d + 1), 0:1] = m_run[d]
        yield
        c_run = {d: [[c_scr[d, p, j * DK:(j + 1) * DK, :] for j in range(2)] for p in range(n_pairs)] for d in dirs}
        n_run = {d: [n_scr[2 * d + p:2 * d + p + 1, :] for p in range(n_pairs)] for d in dirs}
        for (d, n), a4, b4, kv_u, ks_u in zip(units, a_all, b_all, kv_all, ksum_all):
            for p in range(n_pairs):
                nall_scr[d, n, p:p + 1, :] = n_run[d][p]
                a_s = [a4[2 * p + j:2 * p + j + 1, :] for j in range(2)]
                b_s = [b4[2 * p + j:2 * p + j + 1, :] for j in range(2)]
                for j in range(2):
                    cj = c_run[d][p][j]
                    call_scr[d, n, p, j * DK:(j + 1) * DK, :] = cj.astype(BF16)
                    c_run[d][p][j] = a_s[j] * cj + b_s[j] * kv_u[2 * p + j]
                n_run[d][p] = (jnp.where(head_mask[0], a_s[0], a_s[1]) * n_run[d][p]
                               + jnp.where(head_mask[0], b_s[0] * ks_u[2 * p], b_s[1] * ks_u[2 * p + 1]))
        for d in dirs:
            for p in range(n_pairs):
                n_scr[2 * d + p:2 * d + p + 1, :] = n_run[d][p]
                for j in range(2):
                    c_scr[d, p, j * DK:(j + 1) * DK, :] = c_run[d][p][j]

    eye = lower & upper
    ones8 = jnp.ones((8, L), BF16)
    sub8 = lax.broadcasted_iota(jnp.int32, (8, LANES), 0)
    sub_h = lax.broadcasted_iota(jnp.int32, (H_B, L), 0)
    n_rows = [((sub8 == 2 * p) & head_mask[0]) | ((sub8 == 2 * p + 1) & head_mask[1]) for p in range(n_pairs)]

    def head_rows(vals):
        out = vals[0][0:H_B, :]
        for h in range(1, H_B):
            out = jnp.where(sub_h == h, vals[h][0:H_B, :], out)
        return out

    def out_group(ns):
        chunks = [(d, n) for n in ns for d in range(2)]
        pairs = [(d, n, p) for d, n in chunks for p in range(n_pairs)]
        units = [(d, n, p, j) for d, n, p in pairs for j in range(2)]
        cms = [lane_cummax(g_scr[d, n], d)[0:H_B, 0:L] for d, n in chunks]
        qk2s, qc2s, qn2s = [], [], []
        for d, n, p in pairs:
            r = _chunk_rows(n)
            qp = qk_scr[r, p * LANES:(p + 1) * LANES]
            q2 = jnp.concatenate([jnp.where(head_mask[j], qp, 0.0) for j in range(2)], axis=0).astype(BF16)
            qk2s.append(_dot_nt(q2, qk_scr[r, HK + p * LANES:HK + (p + 1) * LANES].astype(BF16)))
            qc2s.append(_dot(q2, call_scr[d, n, p]))
            nsel = jnp.where(n_rows[p], nall_scr[d, n, p:p + 1, :], 0.0).astype(BF16)
            qn2s.append(_dot_nt(nsel, qp.astype(BF16)))
        yield
        s_all = []
        for ui, (d, n, p, j) in enumerate(units):
            grow = g_scr[d, n, 2 * p + j:2 * p + j + 1, 0:L]
            e = jnp.where(tmask[d], grow, -jnp.inf)
            cmax = jnp.max(e, axis=-1, keepdims=True)
            s_all.append((qk2s[ui // 2][j * L:(j + 1) * L, :] * jnp.exp(e - cmax)).astype(BF16))
        yield
        nums =[_dot(s, v_ref[_chunk_rows(n), (2 * p + j) * DV:(2 * p + j + 1) * DV].astype(BF16))
                for (d, n, p, j), s in zip(units, s_all)]
        dens = [_dot_nt(ones8, s) for s in s_all]
        yield
        scales = []
        for ci, (d, n) in enumerate(chunks):
            den_loc = head_rows(dens[ci * H_B:(ci + 1) * H_B])
            qn = qn2s[ci * n_pairs][0:H_B, :]
            for p in range(1, n_pairs):
                qn = qn + qn2s[ci * n_pairs + p][0:H_B, :]
            cm = cms[ci]
            m_prev = mall_scr[d, n, 0:H_B, 0:1]
            delta = cm - m_prev
            t = jnp.exp(-jnp.abs(delta))
            w_loc = jnp.where(delta <= 0.0, t, 1.0)
            w_inter = jnp.where(delta <= 0.0, 1.0, t)
            mt = f_scr[d, n, 0:H_B, 0:L] + jnp.maximum(m_prev, cm)
            den = w_loc * den_loc + w_inter * qn
            rinv = 1.0 / jnp.maximum(jnp.abs(den), jnp.exp(-mt))
            scales.append((w_loc * rinv, w_inter * rinv))
        yield
        hs = []
        for ui, (d, n, p, j) in enumerate(units):
            h = 2 * p + j
            sc_loc, sc_inter = scales[ui // H_B]
            d_loc = jnp.where(eye, sc_loc[h:h + 1, :], 0.0).astype(BF16)
            d_inter = jnp.where(eye, sc_inter[h:h + 1, :], 0.0).astype(BF16)
            hs.append(_dot(d_loc, nums[ui].astype(BF16))
                      + _dot(d_inter, qc2s[ui // 2][j * L:(j + 1) * L, :].astype(BF16)))
        yield
        for ni, n in enumerate(ns):
            r = _chunk_rows(n)
            for h in range(H_B):
                vs = slice(h * DV, (h + 1) * DV)
                o = hs[(2 * ni) * H_B + h] + hs[(2 * ni + 1) * H_B + h]
                out_ref[r, vs] = (_rms(o, gw_ref[:, vs]) * _sigmoid(og_ref[r, vs])).astype(out_ref.dtype)

    def finish():
        if write_state:
            for d in range(2):
                for p in range(n_pairs):
                    cnew_ref[d, p] = c_scr[d, p]
                    for j in range(2):
                        nnew_ref[d, 2 * p + j:2 * p + j + 1, :] = n_scr[2 * d + p:2 * d + p + 1, j * DK:(j + 1) * DK]
                mnew_ref[d:d + 1, :] = to_row(m_scr[H_B * d:H_B * (d + 1), 0:1])

    return state_group, out_group, finish


def _mlstm_scratch(T, C2, grid_w):
    n_pairs = C2 // 2 // LANES
    n_chunks = T // CHUNK
    pad_rows = 2 * (grid_w + 8) if T // grid_w > 1 else 16
    return [
        pltpu.VMEM((T + pad_rows, C2), F32),
        pltpu.VMEM((T, C2), F32),
        pltpu.VMEM((T, SMALL_W), F32),
        pltpu.VMEM((2, n_pairs, LANES, LANES), F32),
        pltpu.VMEM((8, LANES), F32),
        pltpu.VMEM((8, LANES), F32),
        pltpu.VMEM((2, n_chunks, n_pairs, LANES, LANES), BF16),
        pltpu.VMEM((2, n_chunks, 8, LANES), F32),
        pltpu.VMEM((2, n_chunks, 8, LANES), F32),
        pltpu.VMEM((2, n_chunks, 8, LANES), F32),
        pltpu.VMEM((2, n_chunks, 8, LANES), F32),
    ]


N_GLA_SCRATCH = 6
N_MLSTM_SCRATCH = 11


def _scan_kernel(*refs, cols, layer, has_state, write_state, n_cast, ride_ada, grid_w, unroll):
    refs = list(refs)
    z_ref = refs.pop(0)
    s0_ref = c0_ref = n0_ref = m0_ref = None
    if has_state:
        s0_ref, c0_ref, n0_ref, m0_ref = refs[:4]
        del refs[:4]
    wa_ref, bal_ref, gwa_ref, cw_ref, bmg_ref, gwb_ref = refs[:6]
    del refs[:6]
    cast_in = refs[:n_cast]
    del refs[:n_cast]
    if ride_ada:
        ada_in = refs[:4]
        del refs[:4]
    outa_ref, outb_ref = refs[:2]
    del refs[:2]
    snew_ref = cnew_ref = nnew_ref = mnew_ref = None
    if write_state:
        snew_ref, cnew_ref, nnew_ref, mnew_ref = refs[:4]
        del refs[:4]
    cast_out = refs[:n_cast]
    del refs[:n_cast]
    if ride_ada:
        ada_out = refs.pop(0)
    wal_scr, bm_scr = refs[:2]
    del refs[:2]
    gla_scr = refs[:N_GLA_SCRATCH]
    mlstm_scr = refs[N_GLA_SCRATCH:]

    for src, dst in zip(cast_in, cast_out):
        dst[...] = src[...].astype(BF16)
    if ride_ada:
        _ada_tile(*ada_in, ada_out)

    R, HK = wa_ref.shape[1], wa_ref.shape[2]
    wal_scr[...] = jnp.zeros(wal_scr.shape, BF16)
    for d in range(2):
        wal_scr[d * R:(d + 1) * R, d * HK:(d + 1) * HK] = wa_ref[d].astype(BF16)
    lane = lax.broadcasted_iota(jnp.int32, (1, LANES), 1)
    bm = jnp.zeros((1, LANES), F32)
    for g in range(bmg_ref.shape[1]):
        for h in range(H_B):
            bm = jnp.where(lane == GATE_LANE0 + H_B * g + h, bmg_ref[layer, g, h], bm)
    bm_scr[0:1, :] = bm

    def view(name):
        c0, w = cols[name]
        return z_ref.at[:, pl.ds(c0, w)]

    sm_ref = view("small")
    n_chunks = z_ref.shape[0] // CHUNK
    gla = _gla_body(view("qa"), view("ka"), view("va"), view("ga"), sm_ref, s0_ref, wal_scr, bal_ref, gwa_ref,
                    outa_ref, snew_ref, *gla_scr)
    mlstm = _mlstm_body(view("qkb"), view("vb"), view("ob"), sm_ref, c0_ref, n0_ref, m0_ref, cw_ref,
                        bm_scr.at[0:1, :], gwb_ref, outb_ref, cnew_ref, nnew_ref, mnew_ref, *mlstm_scr,
                        grid_w=grid_w)
    _chunk_loop(n_chunks, unroll, lambda ns: [fn([n], (d,)) for n in ns for d in range(2)
                                              for fn in (mlstm[0], gla[0])])
    wide_decay = gla[3]() > GLA_FACTORED_DECAY_MAX

    @pl.when(jnp.logical_not(wide_decay))
    def _():
        _chunk_loop(n_chunks, unroll, lambda ns: [fn([n]) for n in ns for fn in (mlstm[1], gla[1])])

    @pl.when(wide_decay)
    def _():
        _chunk_loop(n_chunks, unroll,
                    lambda ns: [fn([n]) for n in ns for fn in (mlstm[1], functools.partial(gla[1], exact_decay=True))])

    gla[2]()
    mlstm[2]()


def _scan_call(z2d, row0, B, T, states, lw, layer, *, grid_w, write_state, casts=(), ada=None):
    n_z = z2d.shape[1]
    assert row0 % T == 0 and z2d.shape[0] % T == 0
    z3 = z2d.reshape(z2d.shape[0] // T, T, n_z)
    blk0 = row0 // T
    HK = lw["w_alpha2"].shape[-1]
    DA = lw["gnorm_a_w"].shape[0]
    C2 = lw["conv_w"].shape[-1]
    DB = lw["gnorm_b_w"].shape[0]
    DK_A, DK_B = HK // H_A, C2 // 2 // H_B
    pa, pb = HK // LANES, C2 // 2 // LANES
    n_chunks = T // CHUNK
    has_state = states is not None
    widths = (("qa", HK), ("ka", HK), ("va", DA), ("ga", DA), ("qkb", C2), ("vb", DB), ("ob", DB),
              ("small", SMALL_W))
    cols, c0 = {}, 0
    for name, w in widths:
        cols[name] = (c0, w)
        c0 += w
    assert c0 == n_z
    cast_in_specs, cast_out_specs, cast_out_shape, cast_args = _cast_specs(casts, B)
    kern = functools.partial(_scan_kernel, cols=cols, layer=layer, has_state=has_state, write_state=write_state,
                             n_cast=len(casts), ride_ada=ada is not None, grid_w=grid_w,
                             unroll=min(n_chunks, SCAN_UNROLL))

    def per_batch(shape):
        nd = len(shape)
        return pl.BlockSpec((None,) + tuple(shape), lambda b: (b,) + (0,) * nd)

    def per_batch_layer(shape):
        nd = len(shape)
        return pl.BlockSpec((None, None) + tuple(shape), lambda b: (b, layer) + (0,) * nd)

    def of_layer(a):
        return pl.BlockSpec((None,) + a.shape[1:], lambda b: (layer,) + (0,) * (a.ndim - 1))

    def whole(a):
        return pl.BlockSpec(a.shape, lambda b: (0,) * a.ndim)

    state_shapes = ((2, pa, LANES, LANES), (2, pb, LANES, LANES), (2, H_B, DK_B), (2, H_B))
    in_specs = [pl.BlockSpec((None, T, n_z), lambda b: (b + blk0, 0, 0))]
    args = [z3]
    if has_state:
        s_gla, s_c, s_n, s_m = states
        depth = s_gla.shape[1]
        args += [s_gla.reshape((B, depth) + state_shapes[0]), s_c.reshape((B, depth) + state_shapes[1]), s_n, s_m]
        in_specs += [per_batch_layer(s) for s in state_shapes]
    args += [lw["w_alpha2"], lw["b_alpha"], lw["gnorm_a_w"].reshape(1, DA), lw["conv_w"], lw["b_mgate"],
             lw["gnorm_b_w"].reshape(1, DB)]
    in_specs += [of_layer(lw["w_alpha2"]), of_layer(lw["b_alpha"]), pl.BlockSpec((1, DA), lambda b: (0, 0)),
                 whole(lw["conv_w"]), pl.BlockSpec(memory_space=pltpu.SMEM), pl.BlockSpec((1, DB), lambda b: (0, 0))]
    args += cast_args
    in_specs += cast_in_specs
    out_specs = [per_batch((T, DA)), per_batch((T, DB))]
    out_shape = [jax.ShapeDtypeStruct((B, T, DA), BF16), jax.ShapeDtypeStruct((B, T, DB), BF16)]
    if write_state:
        out_specs += [per_batch(s) for s in state_shapes]
        out_shape += [jax.ShapeDtypeStruct((B,) + s, F32) for s in state_shapes]
    out_specs += cast_out_specs
    out_shape += cast_out_shape
    if ada is not None:
        cc, c, w_ada, b_ada, col0 = ada
        n_rest = w_ada.shape[1] - col0
        wcol = n_rest // B
        assert n_rest % B == 0 and wcol % LANES == 0 and col0 % wcol == 0
        args += [cc, c, w_ada, b_ada]
        in_specs += [whole(cc), whole(c),
                     pl.BlockSpec((w_ada.shape[0], wcol), lambda b: (0, col0 // wcol + b)),
                     pl.BlockSpec((1, wcol), lambda b: (0, col0 // wcol + b))]
        out_specs.append(pl.BlockSpec((COND_ROWS, wcol), lambda b: (0, b)))
        out_shape.append(jax.ShapeDtypeStruct((COND_ROWS, n_rest), F32))
    scratch = ([pltpu.VMEM((SMALL_W, 2 * HK), BF16), pltpu.VMEM((8, LANES), F32)]
               + _gla_scratch(T, HK) + _mlstm_scratch(T, C2, grid_w))
    assert len(scratch) == 2 + N_GLA_SCRATCH + N_MLSTM_SCRATCH
    return pl.pallas_call(
        kern,
        grid=(B,),
        in_specs=in_specs,
        out_specs=out_specs,
        out_shape=out_shape,
        scratch_shapes=scratch,
        compiler_params=pltpu.CompilerParams(dimension_semantics=("arbitrary",),
                                             vmem_limit_bytes=VMEM_LIMIT),
        name="mixer_scans",
    )(*args)


def _outff_kernel(xc_ref, xl_ref, ac_ref, al_ref, bc_ref, bl_ref, mod_ref, n2_ref, fn_ref, wo_ref, w1_ref, w2_ref,
                  yc_ref, yl_ref, *, n_ctx, tiles_per_req, ff_chunk, final_norm):
    D = xc_ref.shape[1]
    DA = ac_ref.shape[1]
    is_ctx, row = _tile_group(n_ctx, tiles_per_req)

    def mod(k):
        return mod_ref[pl.ds(row, 1), (k - MOD_SPLIT) * D:(k - MOD_SPLIT + 1) * D]

    def tile(x_ref, a_ref, b_ref, y_ref):
        y = _dot(a_ref[...], wo_ref[0:DA, :]) + _dot(b_ref[...], wo_ref[DA:, :])
        x1 = x_ref[...] + mod(2) * y
        h2 = (_rms(x1, n2_ref[...]) * (1.0 + mod(4)) + mod(3)).astype(BF16)
        acc = jnp.zeros(x1.shape, F32)
        for c0 in range(0, w1_ref.shape[1], ff_chunk):
            u = jnp.maximum(_dot(h2, w1_ref[:, c0:c0 + ff_chunk]), 0.0)
            acc = acc + _dot((u * u).astype(BF16), w2_ref[c0:c0 + ff_chunk, :])
        x2 = x1 + mod(5) * acc
        y_ref[...] = _rms(x2, fn_ref[...]) if final_norm else x2

    @pl.when(is_ctx)
    def _():
        tile(xc_ref, ac_ref, bc_ref, yc_ref)

    @pl.when(jnp.logical_not(is_ctx))
    def _():
        tile(xl_ref, al_ref, bl_ref, yl_ref)


def _outff_call(xc2d, xl2d, ac, al, bc, bl, mod, norm2_w, final_w, wo, w1, w2, *, tm, tiles_per_req, final_norm):
    (Mc, D), Ml = xc2d.shape, xl2d.shape[0]
    n_ctx = Mc // tm
    DA = ac.shape[1]
    DFF = w1.shape[1]
    kern = functools.partial(_outff_kernel, n_ctx=n_ctx, tiles_per_req=tiles_per_req, ff_chunk=FF_CHUNK,
                             final_norm=final_norm)
    once = pl.Buffered(1)
    ctx, lat = _ctx_tile(n_ctx), _lat_tile(n_ctx)
    return pl.pallas_call(
        kern,
        grid=((Mc + Ml) // tm,),
        in_specs=[
            pl.BlockSpec((tm, D), ctx), pl.BlockSpec((tm, D), lat),
            pl.BlockSpec((tm, DA), ctx), pl.BlockSpec((tm, DA), lat),
            pl.BlockSpec((tm, D - DA), ctx), pl.BlockSpec((tm, D - DA), lat),
            pl.BlockSpec(mod.shape, lambda i: (0, 0)),
            pl.BlockSpec((1, D), lambda i: (0, 0)),
            pl.BlockSpec((1, D), lambda i: (0, 0)),
            pl.BlockSpec((D, D), lambda i: (0, 0), pipeline_mode=once),
            pl.BlockSpec((D, DFF), lambda i: (0, 0), pipeline_mode=once),
            pl.BlockSpec((DFF, D), lambda i: (0, 0), pipeline_mode=once),
        ],
        out_specs=[pl.BlockSpec((tm, D), ctx), pl.BlockSpec((tm, D), lat)],
        out_shape=[jax.ShapeDtypeStruct((Mc, D), F32), jax.ShapeDtypeStruct((Ml, D), F32)],
        compiler_params=pltpu.CompilerParams(dimension_semantics=("arbitrary",),
                                             vmem_limit_bytes=VMEM_LIMIT),
        name="outproj_mlp",
    )(xc2d, xl2d, ac, al, bc, bl, mod, norm2_w.reshape(1, D), final_w.reshape(1, D), wo, w1, w2)


def _layer(xc, xl, cond, ada_w, cached, lw, layer, ffw, final_w, final_norm):
    (Bc, Tc, D), (Bl, Tl, _) = xc.shape, xl.shape
    tm = TOKEN_TILE
    assert (Bc * Tc) % tm == 0 and Tl % tm == 0 and (Bc * Tc) % Tl == 0
    xc2d, xl2d = xc.reshape(Bc * Tc, D), xl.reshape(Bl * Tl, D)
    mod_in = _ada_call(*cond, *ada_w, MOD_SPLIT * D)
    z = _inproj_call(xc2d, xl2d, mod_in, lw["norm1_w"], lw["w_in_t"], tm=tm, tiles_per_req=Tl // tm,
                     big_rows=lw["big_rows"], small_rows=lw["small_rows"])
    res_c = _scan_call(z, 0, Bc, Tc, None, lw, layer, grid_w=Tc, write_state=True,
                       casts=((ffw[0], 0), (ffw[1], 1), (ffw[2], 0)), ada=(*cond, *ada_w, MOD_SPLIT * D))
    res_l = _scan_call(z, Bc * Tc, Bl, Tl, cached, lw, layer, grid_w=GRID_W, write_state=False)
    wo_b, w1_b, w2_b, mod_out = res_c[-4:]
    yc, yl = _outff_call(xc2d, xl2d, res_c[0].reshape(Bc * Tc, -1), res_l[0].reshape(Bl * Tl, -1),
                         res_c[1].reshape(Bc * Tc, -1), res_l[1].reshape(Bl * Tl, -1), mod_out, lw["norm2_w"],
                         final_w, wo_b, w1_b, w2_b, tm=tm, tiles_per_req=Tl // tm, final_norm=final_norm)
    return yc.reshape(Bc, Tc, D), yl.reshape(Bl, Tl, D), tuple(res_c[2:6])


def _layer_weights(l, norm1_w, norm2_w, w_in, w_alpha2, b_alpha, b_mgate, conv_w, gnorm_a_w, gnorm_b_w):
    hk_a = w_alpha2.shape[-1]
    d_a = gnorm_a_w.shape[-1]
    d_b = gnorm_b_w.shape[-1]
    hk_b = conv_w.shape[-1] // 2
    sizes = (hk_a, hk_a, d_a, d_a, 2 * R_ALPHA, hk_b, hk_b, d_b, d_b, 4 * H_B)
    assert w_alpha2.shape[2] == R_ALPHA and b_mgate.shape[1] * b_mgate.shape[2] == 4 * H_B
    offs = [0]
    for s in sizes:
        offs.append(offs[-1] + s)
    big_rows = ((offs[0], offs[4] - offs[0]), (offs[5], offs[9] - offs[5]))
    small_rows = ((offs[4], offs[5] - offs[4]), (offs[9], offs[10] - offs[9]))
    assert all(n % LANES == 0 and r % 16 == 0 for r, n in big_rows)
    return dict(
        norm1_w=norm1_w[l], norm2_w=norm2_w[l], w_in_t=jnp.swapaxes(w_in[l], 0, 1),
        big_rows=big_rows, small_rows=small_rows,
        w_alpha2=w_alpha2, b_alpha=b_alpha, b_mgate=b_mgate, conv_w=conv_w[l],
        gnorm_a_w=gnorm_a_w[l], gnorm_b_w=gnorm_b_w[l],
    )


def kernel(x_prompt, x_sample, c, state_gla, state_mlstm_C, state_mlstm_n, state_mlstm_m, c_ctx, w_ada, b_ada, norm1_w, norm2_w, w_in, w_alpha2, b_alpha, b_mgate, conv_w, gnorm_a_w, gnorm_b_w, w_out, w_ff1, w_ff2, final_norm_w):
    depth = w_in.shape[0]
    D = x_prompt.shape[-1]
    Bp, Tp, _ = x_prompt.shape
    Bs = x_sample.shape[0]
    assert 1 + Bs <= COND_ROWS
    cond = (c_ctx.reshape(1, D), c)
    cached = (state_gla, state_mlstm_C, state_mlstm_n, state_mlstm_m)
    xp, xs = x_prompt, x_sample
    s_gla, s_c, s_n, s_m = [], [], [], []
    for l in range(depth):
        lw = _layer_weights(l, norm1_w, norm2_w, w_in, w_alpha2, b_alpha, b_mgate, conv_w,
                            gnorm_a_w, gnorm_b_w)
        xp, xs, ctx = _layer(xp, xs, cond, (w_ada[l], b_ada[l].reshape(1, -1)), cached, lw, l,
                             (w_out[l], w_ff1[l], w_ff2[l]), final_norm_w, l == depth - 1)
        s_gla.append(ctx[0].reshape(Bp, 2, H_A, -1, ctx[0].shape[-1]))
        s_c.append(ctx[1].reshape(Bp, 2, H_B, -1, ctx[1].shape[-1]))
        s_n.append(ctx[2])
        s_m.append(ctx[3])
    dt = x_prompt.dtype
    return (xp, xs, jnp.stack(s_gla, axis=1).astype(dt), jnp.stack(s_c, axis=1).astype(dt),
            jnp.stack(s_n, axis=1).astype(dt), jnp.stack(s_m, axis=1).astype(dt))
```

```python
import functools

import jax
import jax.numpy as jnp
from jax import lax
from jax.experimental import pallas as pl
from jax.experimental.pallas import tpu as pltpu

F32 = jnp.float32
BF16 = jnp.bfloat16

GRID_W = 64
H_A = 4
H_B = 4
R_ALPHA = 16
TAU_GLA = 16.0
CHUNK = 64
EPS = 1e-6
LANES = 128
COND_ROWS = 8
SMALL_W = LANES
GATE_LANE0 = 2 * R_ALPHA
VMEM_LIMIT = 56 * 1024 * 1024
SCAN_UNROLL = 4
PIPELINE_STARTS = 4
TOKEN_TILE = 512
FF_CHUNK = 512
GLA_FACTORED_DECAY_MAX = 60.0
MOD_SPLIT = 2


def _sigmoid(x):
    return 1.0 / (1.0 + jnp.exp(-x))


def _silu(x):
    return x * _sigmoid(x)


def _log_sigmoid(x):
    return jnp.minimum(x, 0.0) - jnp.log1p(jnp.exp(-jnp.abs(x)))


def _dot(a, b):
    return jnp.dot(a, b, preferred_element_type=F32)


def _dot_nt(a, b):
    return lax.dot_general(a, b, (((1,), (1,)), ((), ())), preferred_element_type=F32)


def _rms(x, w):
    return x * lax.rsqrt(jnp.mean(x * x, axis=-1, keepdims=True) + EPS) * w


def _tri_sum(tri, x):
    hi = x.astype(BF16)
    r1 = x - hi.astype(F32)
    mid = r1.astype(BF16)
    lo = (r1 - mid.astype(F32)).astype(BF16)
    return _dot(tri, hi) + _dot(tri, mid) + _dot(tri, lo)


def _chunk_masks(L):
    row = lax.broadcasted_iota(jnp.int32, (L, L), 0)
    col = lax.broadcasted_iota(jnp.int32, (L, L), 1)
    lower = row >= col
    upper = row <= col
    return lower, upper


def _ada_tile(cc_ref, c_ref, w_ref, b_ref, o_ref):
    D = cc_ref.shape[1]
    sub = lax.broadcasted_iota(jnp.int32, (COND_ROWS, D), 0)
    cond = jnp.where(sub == 0, cc_ref[...], 0.0)
    for r in range(c_ref.shape[0]):
        cond = jnp.where(sub == 1 + r, c_ref[r:r + 1, :], cond)
    o_ref[...] = _dot(_silu(cond).astype(BF16), w_ref[...].astype(BF16)) + b_ref[...]


def _ada_call(cc, c, w_ada, b_ada, n_cols):
    D = cc.shape[1]
    tn = 1024
    return pl.pallas_call(
        _ada_tile,
        grid=(n_cols // tn,),
        in_specs=[
            pl.BlockSpec(cc.shape, lambda j: (0, 0)),
            pl.BlockSpec(c.shape, lambda j: (0, 0)),
            pl.BlockSpec((D, tn), lambda j: (0, j)),
            pl.BlockSpec((1, tn), lambda j: (0, j)),
        ],
        out_specs=pl.BlockSpec((COND_ROWS, tn), lambda j: (0, j)),
        out_shape=jax.ShapeDtypeStruct((COND_ROWS, n_cols), F32),
        compiler_params=pltpu.CompilerParams(dimension_semantics=("arbitrary",),
                                             vmem_limit_bytes=VMEM_LIMIT),
        name="ada_mod",
    )(cc, c, w_ada, b_ada)


def _tile_group(n_ctx, tiles_per_req):
    i = pl.program_id(0)
    is_ctx = i < n_ctx
    row = jnp.where(is_ctx, 0, 1 + jnp.maximum(i - n_ctx, 0) // tiles_per_req)
    return is_ctx, row


def _ctx_tile(n_ctx):
    return lambda i: (jnp.minimum(i, n_ctx - 1), 0)


def _lat_tile(n_ctx):
    return lambda i: (jnp.maximum(i - n_ctx, 0), 0)


def _inproj_kernel(xc_ref, xl_ref, mod_ref, nw_ref, wt_ref, z_ref, wb_scr, *, n_ctx, tiles_per_req, big_rows,
                   small_rows):
    D = xc_ref.shape[1]

    @pl.when(pl.program_id(0) == 0)
    def _():
        col = 0
        for r0, n in big_rows:
            for k in range(n // LANES):
                blk = wt_ref[r0 + k * LANES:r0 + (k + 1) * LANES, :]
                wb_scr[:, col:col + LANES] = blk.T.astype(BF16)
                col += LANES
        parts = [wt_ref[r0:r0 + n, :] for r0, n in small_rows]
        n_small = sum(n for _, n in small_rows)
        parts.append(jnp.zeros((SMALL_W - n_small, D), F32))
        wb_scr[:, col:col + SMALL_W] = jnp.concatenate(parts, axis=0).T.astype(BF16)

    is_ctx, row = _tile_group(n_ctx, tiles_per_req)

    def tile(x_ref):
        sh1 = mod_ref[pl.ds(row, 1), 0:D]
        sc1 = mod_ref[pl.ds(row, 1), D:2 * D]
        h = _rms(x_ref[...], nw_ref[...]) * (1.0 + sc1) + sh1
        z_ref[...] = _dot(h.astype(BF16), wb_scr[...])

    @pl.when(is_ctx)
    def _():
        tile(xc_ref)

    @pl.when(jnp.logical_not(is_ctx))
    def _():
        tile(xl_ref)


def _inproj_call(xc2d, xl2d, mod, norm_w, w_in_t, *, tm, tiles_per_req, big_rows, small_rows):
    (Mc, D), Ml = xc2d.shape, xl2d.shape[0]
    n_ctx = Mc // tm
    n_out = sum(n for _, n in big_rows) + SMALL_W
    kern = functools.partial(_inproj_kernel, n_ctx=n_ctx, tiles_per_req=tiles_per_req,
                             big_rows=big_rows, small_rows=small_rows)
    return pl.pallas_call(
        kern,
        grid=((Mc + Ml) // tm,),
        in_specs=[
            pl.BlockSpec((tm, D), _ctx_tile(n_ctx)),
            pl.BlockSpec((tm, D), _lat_tile(n_ctx)),
            pl.BlockSpec(mod.shape, lambda i: (0, 0)),
            pl.BlockSpec((1, D), lambda i: (0, 0)),
            pl.BlockSpec(w_in_t.shape, lambda i: (0, 0), pipeline_mode=pl.Buffered(1)),
        ],
        out_specs=pl.BlockSpec((tm, n_out), lambda i: (i, 0)),
        out_shape=jax.ShapeDtypeStruct((Mc + Ml, n_out), F32),
        scratch_shapes=[pltpu.VMEM((D, n_out), BF16)],
        compiler_params=pltpu.CompilerParams(dimension_semantics=("arbitrary",),
                                             vmem_limit_bytes=VMEM_LIMIT),
        name="norm_inproj",
    )(xc2d, xl2d, mod, norm_w.reshape(1, D), w_in_t)


def _chunk_loop(n_chunks, unroll, make_units):
    def step(ns):
        pending = list(make_units(ns))
        active = []
        while pending or active:
            for _ in range(min(PIPELINE_STARTS, len(pending))):
                active.append(pending.pop(0))
            alive = []
            for g in active:
                try:
                    next(g)
                    alive.append(g)
                except StopIteration:
                    pass
            active = alive

    if unroll >= n_chunks:
        step(list(range(n_chunks)))
        return

    def body(i, carry):
        step([i * unroll + u for u in range(unroll)])
        return carry

    lax.fori_loop(0, n_chunks // unroll, body, 0)


def _chunk_rows(n):
    if isinstance(n, int):
        return pl.ds(n * CHUNK, CHUNK)
    return pl.ds(pl.multiple_of(n * CHUNK, CHUNK), CHUNK)


def _cast_specs(casts, n_steps):
    in_specs, out_specs, out_shape, args = [], [], [], []
    for w, axis in casts:
        blk = list(w.shape)
        assert blk[axis] % n_steps == 0
        blk[axis] //= n_steps
        assert blk[0] % 16 == 0 and blk[1] % LANES == 0
        idx = (lambda b: (b, 0)) if axis == 0 else (lambda b: (0, b))
        in_specs.append(pl.BlockSpec(tuple(blk), idx))
        out_specs.append(pl.BlockSpec(tuple(blk), idx))
        out_shape.append(jax.ShapeDtypeStruct(w.shape, BF16))
        args.append(w)
    return in_specs, out_specs, out_shape, args


def _gla_body(q_ref, k_ref, v_ref, g_ref, sm_ref, s0_ref, wal_ref, bal_ref, gw_ref, out_ref, snew_ref,
              st_scr, sall_scr, qh_scr, qs_scr, kh_scr, ga_scr):
    has_state = s0_ref is not None
    write_state = snew_ref is not None
    T = q_ref.shape[0]
    L = CHUNK
    N = T // L
    HK = q_ref.shape[1]
    DK = HK // H_A
    DV = v_ref.shape[1] // H_A
    scale = DK ** -0.5
    n_pairs = HK // LANES

    lower, upper = _chunk_masks(L)
    tri = (lower.astype(BF16), upper.astype(BF16))
    tmask = (lower, upper)
    lane = lax.broadcasted_iota(jnp.int32, (1, LANES), 1)
    head_mask = (lane < DK, lane >= DK)

    for d in range(2):
        for p in range(n_pairs):
            if has_state:
                st_scr[d, p] = s0_ref[d, p].T
            else:
                st_scr[d, p] = jnp.zeros((LANES, LANES), F32)

    bal = jnp.concatenate([bal_ref[0:1, :], bal_ref[1:2, :]], axis=1)

    def decay_tile(i, span):
        r = pl.ds(pl.multiple_of(i * L, L), L)
        g = _log_sigmoid(_dot(sm_ref[r, :].astype(BF16), wal_ref[...]) + bal) * (1.0 / TAU_GLA)
        ga_scr[r, :] = g
        total = jnp.sum(g, axis=0, keepdims=True)
        return jnp.maximum(span, jnp.max(-total, axis=1, keepdims=True))

    decay_span = jnp.max(lax.fori_loop(0, N, decay_tile, jnp.zeros((1, 1), F32)))

    def state_group(ns, dirs=(0, 1)):
        units = [(d, n if d == 0 else N - 1 - n) for n in ns for d in dirs]
        rows = [_chunk_rows(n) for _, n in units]
        vt_all = [[jnp.concatenate([v_ref[r, (2 * p + j) * DV:(2 * p + j + 1) * DV] for j in range(2)],
                                   axis=0).T.astype(BF16) for p in range(n_pairs)] for r in rows]
        yield
        b = [_tri_sum(tri[d], ga_scr[r, d * HK:(d + 1) * HK]) for (d, _), r in zip(units, rows)]
        yield
        ks_all, dec_all = [], []
        for (d, _), r, bi in zip(units, rows, b):
            bend = bi[L - 1:L, :] if d == 0 else bi[0:1, :]
            q = q_ref[r, :] * scale
            ks = (k_ref[r, :] * jnp.exp(bend - bi)).astype(BF16)
            qh_scr[d, r, :] = (q * jnp.exp(bi - bend)).astype(BF16)
            qs_scr[d, r, :] = (q * jnp.exp(bi)).astype(BF16)
            kh_scr[d, r, :] = ks
            ks_all.append(ks)
            dec_all.append(jnp.exp(bend))
        yield
        upd_all = []
        for vt_u, ks in zip(vt_all, ks_all):
            upd_u = []
            for p in range(n_pairs):
                kp = ks[:, p * LANES:(p + 1) * LANES]
                kk = jnp.concatenate([jnp.where(head_mask[j], kp, jnp.zeros_like(kp)) for j in range(2)], axis=0)
                upd_u.append(_dot(vt_u[p], kk))
            upd_all.append(upd_u)
        yield
        st = {d: [st_scr[d, p] for p in range(n_pairs)] for d in dirs}
        for (d, n), dec, upd in zip(units, dec_all, upd_all):
            for p in range(n_pairs):
                sall_scr[d, n, p] = st[d][p].astype(BF16)
                st[d][p] = st[d][p] * dec[:, p * LANES:(p + 1) * LANES] + upd[p]
        for d in dirs:
            for p in range(n_pairs):
                st_scr[d, p] = st[d][p]

    def stack_heads(x):
        return jnp.concatenate([jnp.where(head_mask[j], x, jnp.zeros_like(x)) for j in range(2)], axis=0)

    tok = lax.broadcasted_iota(jnp.int32, (L, 1), 0)
    row_t = lax.broadcasted_iota(jnp.int32, (2 * L, L), 0) & (L - 1)
    col_s = lax.broadcasted_iota(jnp.int32, (2 * L, L), 1)

    def exact_scores(d, r, p):
        ls = slice(p * LANES, (p + 1) * LANES)
        b = _tri_sum(tri[d], ga_scr[r, d * HK + p * LANES:d * HK + (p + 1) * LANES])
        q = q_ref[r, ls] * scale
        k = k_ref[r, ls]
        acc = jnp.where(row_t == col_s, _dot_nt(stack_heads(q).astype(BF16), k.astype(BF16)), 0.0)
        src = lax.broadcasted_iota(jnp.int32, (L, L), 1)
        h = L // 2
        while h >= 1:
            first = tok & ~(2 * h - 1)
            edge = first + (h - 1 if d == 0 else h)
            b_edge = _tri_sum((src == edge).astype(BF16), b)
            upper = (tok & (2 * h - 1)) >= h
            later, earlier = (upper, ~upper) if d == 0 else (~upper, upper)
            qt = jnp.where(later, q * jnp.exp(b - b_edge), 0.0)
            kt = jnp.where(earlier, k * jnp.exp(b_edge - b), 0.0)
            sc = _dot_nt(stack_heads(qt).astype(BF16), kt.astype(BF16))
            acc = acc + jnp.where((row_t & ~(2 * h - 1)) == (col_s & ~(2 * h - 1)), sc, 0.0)
            h //= 2
        return acc

    def out_group(ns, exact_decay=False):
        pairs = [(d, ni, p) for ni in range(len(ns)) for d in range(2) for p in range(n_pairs)]
        scores, inter = [], []
        for d, ni, p in pairs:
            r = _chunk_rows(ns[ni])
            ls = slice(p * LANES, (p + 1) * LANES)
            if exact_decay:
                scores.append(exact_scores(d, r, p))
            else:
                scores.append(_dot_nt(stack_heads(qh_scr[d, r, ls]), kh_scr[d, r, ls]))
            inter.append(_dot_nt(stack_heads(qs_scr[d, r, ls]), sall_scr[d, ns[ni], p]))
        yield
        probs = [[jnp.where(tmask[d], sc[j * L:(j + 1) * L, :], 0.0).astype(BF16) for j in range(2)]
                 for (d, _, _), sc in zip(pairs, scores)]
        yield
        outs = {}
        for (d, ni, p), pr, it in zip(pairs, probs, inter):
            r = _chunk_rows(ns[ni])
            for j in range(2):
                vs = slice((2 * p + j) * DV, (2 * p + j + 1) * DV)
                outs[(d, ni, 2 * p + j)] = _dot(pr[j], v_ref[r, vs].astype(BF16)) + it[j * L:(j + 1) * L, :]
        yield
        for ni, n in enumerate(ns):
            r = _chunk_rows(n)
            for h in range(H_A):
                vs = slice(h * DV, (h + 1) * DV)
                o = outs[(0, ni, h)] + outs[(1, ni, h)]
                out_ref[r, vs] = (_rms(o, gw_ref[:, vs]) * _silu(g_ref[r, vs])).astype(out_ref.dtype)

    def finish():
        if write_state:
            for d in range(2):
                for p in range(n_pairs):
                    snew_ref[d, p] = st_scr[d, p].T

    return state_group, out_group, finish, decay_span


def _gla_scratch(T, HK):
    n_pairs = HK // LANES
    n_chunks = T // CHUNK
    return [
        pltpu.VMEM((2, n_pairs, LANES, LANES), F32),
        pltpu.VMEM((2, n_chunks, n_pairs, LANES, LANES), BF16),
        pltpu.VMEM((2, T, HK), BF16),
        pltpu.VMEM((2, T, HK), BF16),
        pltpu.VMEM((2, T, HK), BF16),
        pltpu.VMEM((T, 2 * HK), F32),
    ]


def _mlstm_body(qk_ref, v_ref, og_ref, sm_ref, c0_ref, n0_ref, m0_ref, cw_ref, bm_ref, gw_ref,
                out_ref, cnew_ref, nnew_ref, mnew_ref,
                pad_scr, qk_scr, y_scr, c_scr, n_scr, m_scr, call_scr, nall_scr, mall_scr, g_scr, f_scr,
                *, grid_w):
    has_state = c0_ref is not None
    write_state = cnew_ref is not None
    T = qk_ref.shape[0]
    L = CHUNK
    N = T // L
    C2 = qk_ref.shape[1]
    HK = C2 // 2
    DK = HK // H_B
    DV = v_ref.shape[1] // H_B
    scale = DK ** -0.5
    n_pairs = HK // LANES
    P = pad_scr.shape[0] - T
    P0 = P // 2
    rows_img = T // grid_w

    lower, upper = _chunk_masks(L)
    tri = (lower.astype(BF16), upper.astype(BF16))
    tmask = (lower, upper)
    lane = lax.broadcasted_iota(jnp.int32, (1, LANES), 1)
    head_mask = (lane < DK, lane >= DK)
    lane_in = lane & (L - 1)

    def lane_cummax(x, d):
        k = 1
        while k < L:
            if d == 0:
                x = jnp.maximum(x, jnp.where(lane_in >= k, pltpu.roll(x, k, axis=1), -jnp.inf))
            else:
                x = jnp.maximum(x, jnp.where(lane_in < L - k, pltpu.roll(x, LANES - k, axis=1), -jnp.inf))
            k *= 2
        return x

    for d in range(2):
        for p in range(n_pairs):
            if has_state:
                c_scr[d, p] = c0_ref[d, p]
                n_scr[2 * d + p:2 * d + p + 1, :] = jnp.concatenate(
                    [n0_ref[d, 2 * p + j:2 * p + j + 1, :] for j in range(2)], axis=1)
            else:
                c_scr[d, p] = jnp.zeros((LANES, LANES), F32)
                n_scr[2 * d + p:2 * d + p + 1, :] = jnp.zeros((1, LANES), F32)
    eye_h = (lax.broadcasted_iota(jnp.int32, (H_B, H_B), 0) == lax.broadcasted_iota(jnp.int32, (H_B, H_B), 1))

    def to_col(row):
        return jnp.sum(jnp.where(eye_h, row, 0.0), axis=1, keepdims=True)

    def to_row(col):
        return jnp.sum(jnp.where(eye_h, col, 0.0), axis=0, keepdims=True)

    for d in range(2):
        if has_state:
            m_scr[H_B * d:H_B * (d + 1), 0:1] = to_col(m0_ref[d:d + 1, :])
        else:
            m_scr[H_B * d:H_B * (d + 1), 0:1] = jnp.zeros((H_B, 1), F32)

    pad_scr[0:P0, :] = jnp.zeros((P0, C2), F32)
    pad_scr[P0 + T:P + T, :] = jnp.zeros((P - P0, C2), F32)

    def copy_in(i, carry):
        r0 = pl.multiple_of(i * L, L)
        pad_scr[pl.ds(P0 + r0, L), :] = qk_ref[pl.ds(r0, L), :]
        return carry

    lax.fori_loop(0, N, copy_in, 0)

    lane_c = lax.broadcasted_iota(jnp.int32, (1, C2), 1)
    qscale = jnp.where(lane_c < HK, scale, 1.0).astype(F32)
    sub = lax.broadcasted_iota(jnp.int32, (L, 1), 0)
    img_rows = (0,) if rows_img == 1 else (-1, 0, 1)

    def conv_tile(i, carry):
        r0 = pl.multiple_of(i * L, L)
        col = lax.rem(r0, grid_w) + sub
        ok_left = col >= 1
        ok_right = col <= grid_w - 2
        sums = [None, None, None]
        for di in img_rows:
            blk = pad_scr[pl.ds(P0 + r0 + di * grid_w - 8, L + 16), :]
            for k in range(3):
                term = blk * cw_ref[di + 1, k:k + 1, :]
                sums[k] = term if sums[k] is None else sums[k] + term
        acc = (sums[1][8:8 + L, :] + jnp.where(ok_left, sums[0][7:7 + L, :], 0.0)
               + jnp.where(ok_right, sums[2][9:9 + L, :], 0.0))
        qk_scr[pl.ds(r0, L), :] = _silu(acc) * qscale
        return carry

    lax.fori_loop(0, N, conv_tile, 0)

    gl = lane - GATE_LANE0
    is_f = ((gl >= H_B) & (gl < 2 * H_B)) | ((gl >= 3 * H_B) & (gl < 4 * H_B))

    def gate_tile(i, carry):
        rows = pl.ds(pl.multiple_of(i * L, L), L)
        x = sm_ref[rows, :] + bm_ref[...]
        y_scr[rows, :] = jnp.where(is_f, _log_sigmoid(x), x)
        return carry

    lax.fori_loop(0, N, gate_tile, 0)


    def state_group(ns, dirs=(0, 1)):
        units = [(d, n if d == 0 else N - 1 - n) for n in ns for d in dirs]
        rows = [_chunk_rows(n) for _, n in units]
        kt_all = [[qk_scr[r, HK + p * LANES:HK + (p + 1) * LANES].T for p in range(n_pairs)] for r in rows]
        yield
        xs = [y_scr[r, :] for r in rows]
        fsum = [_tri_sum(tri[d], x) for (d, _), x in zip(units, xs)]
        yield
        wk_all, f_end, c_end = [], [], []
        for (d, n), r, x, fs in zip(units, rows, xs, fsum):
            y = jnp.where(is_f, fs, x)
            li0 = GATE_LANE0 + 2 * H_B * d
            blk = jnp.concatenate([y, y], axis=0).T[li0:li0 + 2 * H_B, :]
            frow = pltpu.roll(blk, H_B, axis=0)
            grow = blk - frow
            g_scr[d, n] = grow
            f_scr[d, n] = frow
            e_col = L - 1 if d == 0 else 0
            f_end.append(frow[0:H_B, e_col:e_col + 1])
            ce8 = jnp.max(grow, axis=1, keepdims=True)
            c_end.append(ce8[0:H_B, :])
            wk_all.append(jnp.exp(grow[:, 0:L] - ce8))
        yield
        kv_all, ksum_all = [], []
        for r, wk8, kt_u in zip(rows, wk_all, kt_all):
            kv_u, ks_u = [], []
            wk8b = wk8.astype(BF16)
            for p in range(n_pairs):
                kpb = qk_scr[r, HK + p * LANES:HK + (p + 1) * LANES].astype(BF16)
                ks8 = _dot(wk8b, kpb)
                for j in range(2):
                    h = 2 * p + j
                    kwt = (kt_u[p][j * DK:(j + 1) * DK, :] * wk8[h:h + 1, :]).astype(BF16)
                    kv_u.append(_dot(kwt, v_ref[r, h * DV:(h + 1) * DV].astype(BF16)))
                    ks_u.append(ks8[h:h + 1, :])
            kv_all.append(kv_u)
            ksum_all.append(ks_u)
        yield
        m_run = {d: m_scr[H_B * d:H_B * (d + 1), 0:1] for d in dirs}
        a_all, b_all = [], []
        for (d, n), fe, ce in zip(units, f_end, c_end):
            mall_scr[d, n, 0:H_B, 0:1] = m_run[d]
            mx = jnp.maximum(m_run[d], ce)
            a_all.append(jnp.exp(m_run[d] - mx))
            b_all.append(jnp.exp(ce - mx))
            m_run[d] = fe + mx
        for d in dirs:
            m_scr[H_B * d:H_B * (d + 1), 0:1] = m_run[d]
        yield
        c_run = {d: [[c_scr[d, p, j * DK:(j + 1) * DK, :] for j in range(2)] for p in range(n_pairs)] for d in dirs}
        n_run = {d: [n_scr[2 * d + p:2 * d + p + 1, :] for p in range(n_pairs)] for d in dirs}
        for (d, n), a4, b4, kv_u, ks_u in zip(units, a_all, b_all, kv_all, ksum_all):
            for p in range(n_pairs):
                nall_scr[d, n, p:p + 1, :] = n_run[d][p]
                a_s = [a4[2 * p + j:2 * p + j + 1, :] for j in range(2)]
                b_s = [b4[2 * p + j:2 * p + j + 1, :] for j in range(2)]
                for j in range(2):
                    cj = c_run[d][p][j]
                    call_scr[d, n, p, j * DK:(j + 1) * DK, :] = cj.astype(BF16)
                    c_run[d][p][j] = a_s[j] * cj + b_s[j] * kv_u[2 * p + j]
                n_run[d][p] = (jnp.where(head_mask[0], a_s[0], a_s[1]) * n_run[d][p]
                               + jnp.where(head_mask[0], b_s[0] * ks_u[2 * p], b_s[1] * ks_u[2 * p + 1]))
        for d in dirs:
            for p in range(n_pairs):
                n_scr[2 * d + p:2 * d + p + 1, :] = n_run[d][p]
                for j in range(2):
                    c_scr[d, p, j * DK:(j + 1) * DK, :] = c_run[d][p][j]

    eye = lower & upper
    ones8 = jnp.ones((8, L), BF16)
    sub8 = lax.broadcasted_iota(jnp.int32, (8, LANES), 0)
    sub_h = lax.broadcasted_iota(jnp.int32, (H_B, L), 0)
    n_rows = [((sub8 == 2 * p) & head_mask[0]) | ((sub8 == 2 * p + 1) & head_mask[1]) for p in range(n_pairs)]

    def head_rows(vals):
        out = vals[0][0:H_B, :]
        for h in range(1, H_B):
            out = jnp.where(sub_h == h, vals[h][0:H_B, :], out)
        return out

    def out_group(ns):
        chunks = [(d, n) for n in ns for d in range(2)]
        pairs = [(d, n, p) for d, n in chunks for p in range(n_pairs)]
        units = [(d, n, p, j) for d, n, p in pairs for j in range(2)]
        cms = [lane_cummax(g_scr[d, n], d)[0:H_B, 0:L] for d, n in chunks]
        qk2s, qc2s, qn2s = [], [], []
        for d, n, p in pairs:
            r = _chunk_rows(n)
            qp = qk_scr[r, p * LANES:(p + 1) * LANES]
            q2 = jnp.concatenate([jnp.where(head_mask[j], qp, 0.0) for j in range(2)], axis=0).astype(BF16)
            qk2s.append(_dot_nt(q2, qk_scr[r, HK + p * LANES:HK + (p + 1) * LANES].astype(BF16)))
            qc2s.append(_dot(q2, call_scr[d, n, p]))
            nsel = jnp.where(n_rows[p], nall_scr[d, n, p:p + 1, :], 0.0).astype(BF16)
            qn2s.append(_dot_nt(nsel, qp.astype(BF16)))
        yield
        s_all = []
        for ui, (d, n, p, j) in enumerate(units):
            grow = g_scr[d, n, 2 * p + j:2 * p + j + 1, 0:L]
            e = jnp.where(tmask[d], grow, -jnp.inf)
            cmax = jnp.max(e, axis=-1, keepdims=True)
            s_all.append((qk2s[ui // 2][j * L:(j + 1) * L, :] * jnp.exp(e - cmax)).astype(BF16))
        yield
        nums =[_dot(s, v_ref[_chunk_rows(n), (2 * p + j) * DV:(2 * p + j + 1) * DV].astype(BF16))
                for (d, n, p, j), s in zip(units, s_all)]
        dens = [_dot_nt(ones8, s) for s in s_all]
        yield
        scales = []
        for ci, (d, n) in enumerate(chunks):
            den_loc = head_rows(dens[ci * H_B:(ci + 1) * H_B])
            qn = qn2s[ci * n_pairs][0:H_B, :]
            for p in range(1, n_pairs):
                qn = qn + qn2s[ci * n_pairs + p][0:H_B, :]
            cm = cms[ci]
            m_prev = mall_scr[d, n, 0:H_B, 0:1]
            delta = cm - m_prev
            t = jnp.exp(-jnp.abs(delta))
            w_loc = jnp.where(delta <= 0.0, t, 1.0)
            w_inter = jnp.where(delta <= 0.0, 1.0, t)
            mt = f_scr[d, n, 0:H_B, 0:L] + jnp.maximum(m_prev, cm)
            den = w_loc * den_loc + w_inter * qn
            rinv = 1.0 / jnp.maximum(jnp.abs(den), jnp.exp(-mt))
            scales.append((w_loc * rinv, w_inter * rinv))
        yield
        hs = []
        for ui, (d, n, p, j) in enumerate(units):
            h = 2 * p + j
            sc_loc, sc_inter = scales[ui // H_B]
            d_loc = jnp.where(eye, sc_loc[h:h + 1, :], 0.0).astype(BF16)
            d_inter = jnp.where(eye, sc_inter[h:h + 1, :], 0.0).astype(BF16)
            hs.append(_dot(d_loc, nums[ui].astype(BF16))
                      + _dot(d_inter, qc2s[ui // 2][j * L:(j + 1) * L, :].astype(BF16)))
        yield
        for ni, n in enumerate(ns):
            r = _chunk_rows(n)
            for h in range(H_B):
                vs = slice(h * DV, (h + 1) * DV)
                o = hs[(2 * ni) * H_B + h] + hs[(2 * ni + 1) * H_B + h]
                out_ref[r, vs] = (_rms(o, gw_ref[:, vs]) * _sigmoid(og_ref[r, vs])).astype(out_ref.dtype)

    def finish():
        if write_state:
            for d in range(2):
                for p in range(n_pairs):
                    cnew_ref[d, p] = c_scr[d, p]
                    for j in range(2):
                        nnew_ref[d, 2 * p + j:2 * p + j + 1, :] = n_scr[2 * d + p:2 * d + p + 1, j * DK:(j + 1) * DK]
                mnew_ref[d:d + 1, :] = to_row(m_scr[H_B * d:H_B * (d + 1), 0:1])

    return state_group, out_group, finish


def _mlstm_scratch(T, C2, grid_w):
    n_pairs = C2 // 2 // LANES
    n_chunks = T // CHUNK
    pad_rows = 2 * (grid_w + 8) if T // grid_w > 1 else 16
    return [
        pltpu.VMEM((T + pad_rows, C2), F32),
        pltpu.VMEM((T, C2), F32),
        pltpu.VMEM((T, SMALL_W), F32),
        pltpu.VMEM((2, n_pairs, LANES, LANES), F32),
        pltpu.VMEM((8, LANES), F32),
        pltpu.VMEM((8, LANES), F32),
        pltpu.VMEM((2, n_chunks, n_pairs, LANES, LANES), BF16),
        pltpu.VMEM((2, n_chunks, 8, LANES), F32),
        pltpu.VMEM((2, n_chunks, 8, LANES), F32),
        pltpu.VMEM((2, n_chunks, 8, LANES), F32),
        pltpu.VMEM((2, n_chunks, 8, LANES), F32),
    ]


N_GLA_SCRATCH = 6
N_MLSTM_SCRATCH = 11


def _scan_kernel(*refs, cols, layer, has_state, write_state, n_cast, ride_ada, grid_w, unroll):
    refs = list(refs)
    z_ref = refs.pop(0)
    s0_ref = c0_ref = n0_ref = m0_ref = None
    if has_state:
        s0_ref, c0_ref, n0_ref, m0_ref = refs[:4]
        del refs[:4]
    wa_ref, bal_ref, gwa_ref, cw_ref, bmg_ref, gwb_ref = refs[:6]
    del refs[:6]
    cast_in = refs[:n_cast]
    del refs[:n_cast]
    if ride_ada:
        ada_in = refs[:4]
        del refs[:4]
    outa_ref, outb_ref = refs[:2]
    del refs[:2]
    snew_ref = cnew_ref = nnew_ref = mnew_ref = None
    if write_state:
        snew_ref, cnew_ref, nnew_ref, mnew_ref = refs[:4]
        del refs[:4]
    cast_out = refs[:n_cast]
    del refs[:n_cast]
    if ride_ada:
        ada_out = refs.pop(0)
    wal_scr, bm_scr = refs[:2]
    del refs[:2]
    gla_scr = refs[:N_GLA_SCRATCH]
    mlstm_scr = refs[N_GLA_SCRATCH:]

    for src, dst in zip(cast_in, cast_out):
        dst[...] = src[...].astype(BF16)
    if ride_ada:
        _ada_tile(*ada_in, ada_out)

    R, HK = wa_ref.shape[1], wa_ref.shape[2]
    wal_scr[...] = jnp.zeros(wal_scr.shape, BF16)
    for d in range(2):
        wal_scr[d * R:(d + 1) * R, d * HK:(d + 1) * HK] = wa_ref[d].astype(BF16)
    lane = lax.broadcasted_iota(jnp.int32, (1, LANES), 1)
    bm = jnp.zeros((1, LANES), F32)
    for g in range(bmg_ref.shape[1]):
        for h in range(H_B):
            bm = jnp.where(lane == GATE_LANE0 + H_B * g + h, bmg_ref[layer, g, h], bm)
    bm_scr[0:1, :] = bm

    def view(name):
        c0, w = cols[name]
        return z_ref.at[:, pl.ds(c0, w)]

    sm_ref = view("small")
    n_chunks = z_ref.shape[0] // CHUNK
    gla = _gla_body(view("qa"), view("ka"), view("va"), view("ga"), sm_ref, s0_ref, wal_scr, bal_ref, gwa_ref,
                    outa_ref, snew_ref, *gla_scr)
    mlstm = _mlstm_body(view("qkb"), view("vb"), view("ob"), sm_ref, c0_ref, n0_ref, m0_ref, cw_ref,
                        bm_scr.at[0:1, :], gwb_ref, outb_ref, cnew_ref, nnew_ref, mnew_ref, *mlstm_scr,
                        grid_w=grid_w)
    gla_state, gla_out, gla_finish, decay_span = gla
    mlstm_state, mlstm_out, mlstm_finish = mlstm

    def passes(gla_out_fn):
        _chunk_loop(n_chunks, unroll, lambda ns: [fn([n], (d,)) for n in ns for d in range(2)
                                                  for fn in (mlstm_state, gla_state)])
        _chunk_loop(n_chunks, unroll, lambda ns: [fn([n]) for n in ns for fn in (mlstm_out, gla_out_fn)])

    wide_decay = decay_span > GLA_FACTORED_DECAY_MAX

    @pl.when(jnp.logical_not(wide_decay))
    def _():
        passes(gla_out)

    @pl.when(wide_decay)
    def _():
        passes(functools.partial(gla_out, exact_decay=True))

    gla_finish()
    mlstm_finish()


def _scan_call(z2d, row0, B, T, states, lw, layer, *, grid_w, write_state, casts=(), ada=None):
    n_z = z2d.shape[1]
    assert row0 % T == 0 and z2d.shape[0] % T == 0
    z3 = z2d.reshape(z2d.shape[0] // T, T, n_z)
    blk0 = row0 // T
    HK = lw["w_alpha2"].shape[-1]
    DA = lw["gnorm_a_w"].shape[0]
    C2 = lw["conv_w"].shape[-1]
    DB = lw["gnorm_b_w"].shape[0]
    DK_A, DK_B = HK // H_A, C2 // 2 // H_B
    pa, pb = HK // LANES, C2 // 2 // LANES
    n_chunks = T // CHUNK
    has_state = states is not None
    widths = (("qa", HK), ("ka", HK), ("va", DA), ("ga", DA), ("qkb", C2), ("vb", DB), ("ob", DB),
              ("small", SMALL_W))
    cols, c0 = {}, 0
    for name, w in widths:
        cols[name] = (c0, w)
        c0 += w
    assert c0 == n_z
    cast_in_specs, cast_out_specs, cast_out_shape, cast_args = _cast_specs(casts, B)
    kern = functools.partial(_scan_kernel, cols=cols, layer=layer, has_state=has_state, write_state=write_state,
                             n_cast=len(casts), ride_ada=ada is not None, grid_w=grid_w,
                             unroll=min(n_chunks, SCAN_UNROLL))

    def per_batch(shape):
        nd = len(shape)
        return pl.BlockSpec((None,) + tuple(shape), lambda b: (b,) + (0,) * nd)

    def per_batch_layer(shape):
        nd = len(shape)
        return pl.BlockSpec((None, None) + tuple(shape), lambda b: (b, layer) + (0,) * nd)

    def of_layer(a):
        return pl.BlockSpec((None,) + a.shape[1:], lambda b: (layer,) + (0,) * (a.ndim - 1))

    def whole(a):
        return pl.BlockSpec(a.shape, lambda b: (0,) * a.ndim)

    state_shapes = ((2, pa, LANES, LANES), (2, pb, LANES, LANES), (2, H_B, DK_B), (2, H_B))
    in_specs = [pl.BlockSpec((None, T, n_z), lambda b: (b + blk0, 0, 0))]
    args = [z3]
    if has_state:
        s_gla, s_c, s_n, s_m = states
        depth = s_gla.shape[1]
        args += [s_gla.reshape((B, depth) + state_shapes[0]), s_c.reshape((B, depth) + state_shapes[1]), s_n, s_m]
        in_specs += [per_batch_layer(s) for s in state_shapes]
    args += [lw["w_alpha2"], lw["b_alpha"], lw["gnorm_a_w"].reshape(1, DA), lw["conv_w"], lw["b_mgate"],
             lw["gnorm_b_w"].reshape(1, DB)]
    in_specs += [of_layer(lw["w_alpha2"]), of_layer(lw["b_alpha"]), pl.BlockSpec((1, DA), lambda b: (0, 0)),
                 whole(lw["conv_w"]), pl.BlockSpec(memory_space=pltpu.SMEM), pl.BlockSpec((1, DB), lambda b: (0, 0))]
    args += cast_args
    in_specs += cast_in_specs
    out_specs = [per_batch((T, DA)), per_batch((T, DB))]
    out_shape = [jax.ShapeDtypeStruct((B, T, DA), BF16), jax.ShapeDtypeStruct((B, T, DB), BF16)]
    if write_state:
        out_specs += [per_batch(s) for s in state_shapes]
        out_shape += [jax.ShapeDtypeStruct((B,) + s, F32) for s in state_shapes]
    out_specs += cast_out_specs
    out_shape += cast_out_shape
    if ada is not None:
        cc, c, w_ada, b_ada, col0 = ada
        n_rest = w_ada.shape[1] - col0
        wcol = n_rest // B
        assert n_rest % B == 0 and wcol % LANES == 0 and col0 % wcol == 0
        args += [cc, c, w_ada, b_ada]
        in_specs += [whole(cc), whole(c),
                     pl.BlockSpec((w_ada.shape[0], wcol), lambda b: (0, col0 // wcol + b)),
                     pl.BlockSpec((1, wcol), lambda b: (0, col0 // wcol + b))]
        out_specs.append(pl.BlockSpec((COND_ROWS, wcol), lambda b: (0, b)))
        out_shape.append(jax.ShapeDtypeStruct((COND_ROWS, n_rest), F32))
    scratch = ([pltpu.VMEM((SMALL_W, 2 * HK), BF16), pltpu.VMEM((8, LANES), F32)]
               + _gla_scratch(T, HK) + _mlstm_scratch(T, C2, grid_w))
    assert len(scratch) == 2 + N_GLA_SCRATCH + N_MLSTM_SCRATCH
    return pl.pallas_call(
        kern,
        grid=(B,),
        in_specs=in_specs,
        out_specs=out_specs,
        out_shape=out_shape,
        scratch_shapes=scratch,
        compiler_params=pltpu.CompilerParams(dimension_semantics=("arbitrary",),
                                             vmem_limit_bytes=VMEM_LIMIT),
        name="mixer_scans",
    )(*args)


def _outff_kernel(xc_ref, xl_ref, ac_ref, al_ref, bc_ref, bl_ref, mod_ref, n2_ref, fn_ref, wo_ref, w1_ref, w2_ref,
                  yc_ref, yl_ref, *, n_ctx, tiles_per_req, ff_chunk, final_norm):
    D = xc_ref.shape[1]
    DA = ac_ref.shape[1]
    is_ctx, row = _tile_group(n_ctx, tiles_per_req)

    def mod(k):
        return mod_ref[pl.ds(row, 1), (k - MOD_SPLIT) * D:(k - MOD_SPLIT + 1) * D]

    def tile(x_ref, a_ref, b_ref, y_ref):
        y = _dot(a_ref[...], wo_ref[0:DA, :]) + _dot(b_ref[...], wo_ref[DA:, :])
        x1 = x_ref[...] + mod(2) * y
        h2 = (_rms(x1, n2_ref[...]) * (1.0 + mod(4)) + mod(3)).astype(BF16)
        acc = jnp.zeros(x1.shape, F32)
        for c0 in range(0, w1_ref.shape[1], ff_chunk):
            u = jnp.maximum(_dot(h2, w1_ref[:, c0:c0 + ff_chunk]), 0.0)
            acc = acc + _dot((u * u).astype(BF16), w2_ref[c0:c0 + ff_chunk, :])
        x2 = x1 + mod(5) * acc
        y_ref[...] = _rms(x2, fn_ref[...]) if final_norm else x2

    @pl.when(is_ctx)
    def _():
        tile(xc_ref, ac_ref, bc_ref, yc_ref)

    @pl.when(jnp.logical_not(is_ctx))
    def _():
        tile(xl_ref, al_ref, bl_ref, yl_ref)


def _outff_call(xc2d, xl2d, ac, al, bc, bl, mod, norm2_w, final_w, wo, w1, w2, *, tm, tiles_per_req, final_norm):
    (Mc, D), Ml = xc2d.shape, xl2d.shape[0]
    n_ctx = Mc // tm
    DA = ac.shape[1]
    DFF = w1.shape[1]
    kern = functools.partial(_outff_kernel, n_ctx=n_ctx, tiles_per_req=tiles_per_req, ff_chunk=FF_CHUNK,
                             final_norm=final_norm)
    once = pl.Buffered(1)
    ctx, lat = _ctx_tile(n_ctx), _lat_tile(n_ctx)
    return pl.pallas_call(
        kern,
        grid=((Mc + Ml) // tm,),
        in_specs=[
            pl.BlockSpec((tm, D), ctx), pl.BlockSpec((tm, D), lat),
            pl.BlockSpec((tm, DA), ctx), pl.BlockSpec((tm, DA), lat),
            pl.BlockSpec((tm, D - DA), ctx), pl.BlockSpec((tm, D - DA), lat),
            pl.BlockSpec(mod.shape, lambda i: (0, 0)),
            pl.BlockSpec((1, D), lambda i: (0, 0)),
            pl.BlockSpec((1, D), lambda i: (0, 0)),
            pl.BlockSpec((D, D), lambda i: (0, 0), pipeline_mode=once),
            pl.BlockSpec((D, DFF), lambda i: (0, 0), pipeline_mode=once),
            pl.BlockSpec((DFF, D), lambda i: (0, 0), pipeline_mode=once),
        ],
        out_specs=[pl.BlockSpec((tm, D), ctx), pl.BlockSpec((tm, D), lat)],
        out_shape=[jax.ShapeDtypeStruct((Mc, D), F32), jax.ShapeDtypeStruct((Ml, D), F32)],
        compiler_params=pltpu.CompilerParams(dimension_semantics=("arbitrary",),
                                             vmem_limit_bytes=VMEM_LIMIT),
        name="outproj_mlp",
    )(xc2d, xl2d, ac, al, bc, bl, mod, norm2_w.reshape(1, D), final_w.reshape(1, D), wo, w1, w2)


def _layer(xc, xl, cond, ada_w, cached, lw, layer, ffw, final_w, final_norm):
    (Bc, Tc, D), (Bl, Tl, _) = xc.shape, xl.shape
    tm = TOKEN_TILE
    assert (Bc * Tc) % tm == 0 and Tl % tm == 0 and (Bc * Tc) % Tl == 0
    xc2d, xl2d = xc.reshape(Bc * Tc, D), xl.reshape(Bl * Tl, D)
    mod_in = _ada_call(*cond, *ada_w, MOD_SPLIT * D)
    z = _inproj_call(xc2d, xl2d, mod_in, lw["norm1_w"], lw["w_in_t"], tm=tm, tiles_per_req=Tl // tm,
                     big_rows=lw["big_rows"], small_rows=lw["small_rows"])
    res_c = _scan_call(z, 0, Bc, Tc, None, lw, layer, grid_w=Tc, write_state=True,
                       casts=((ffw[0], 0), (ffw[1], 1), (ffw[2], 0)), ada=(*cond, *ada_w, MOD_SPLIT * D))
    res_l = _scan_call(z, Bc * Tc, Bl, Tl, cached, lw, layer, grid_w=GRID_W, write_state=False)
    wo_b, w1_b, w2_b, mod_out = res_c[-4:]
    yc, yl = _outff_call(xc2d, xl2d, res_c[0].reshape(Bc * Tc, -1), res_l[0].reshape(Bl * Tl, -1),
                         res_c[1].reshape(Bc * Tc, -1), res_l[1].reshape(Bl * Tl, -1), mod_out, lw["norm2_w"],
                         final_w, wo_b, w1_b, w2_b, tm=tm, tiles_per_req=Tl // tm, final_norm=final_norm)
    return yc.reshape(Bc, Tc, D), yl.reshape(Bl, Tl, D), tuple(res_c[2:6])


def _layer_weights(l, norm1_w, norm2_w, w_in, w_alpha2, b_alpha, b_mgate, conv_w, gnorm_a_w, gnorm_b_w):
    hk_a = w_alpha2.shape[-1]
    d_a = gnorm_a_w.shape[-1]
    d_b = gnorm_b_w.shape[-1]
    hk_b = conv_w.shape[-1] // 2
    sizes = (hk_a, hk_a, d_a, d_a, 2 * R_ALPHA, hk_b, hk_b, d_b, d_b, 4 * H_B)
    assert w_alpha2.shape[2] == R_ALPHA and b_mgate.shape[1] * b_mgate.shape[2] == 4 * H_B
    offs = [0]
    for s in sizes:
        offs.append(offs[-1] + s)
    big_rows = ((offs[0], offs[4] - offs[0]), (offs[5], offs[9] - offs[5]))
    small_rows = ((offs[4], offs[5] - offs[4]), (offs[9], offs[10] - offs[9]))
    assert all(n % LANES == 0 and r % 16 == 0 for r, n in big_rows)
    return dict(
        norm1_w=norm1_w[l], norm2_w=norm2_w[l], w_in_t=jnp.swapaxes(w_in[l], 0, 1),
        big_rows=big_rows, small_rows=small_rows,
        w_alpha2=w_alpha2, b_alpha=b_alpha, b_mgate=b_mgate, conv_w=conv_w[l],
        gnorm_a_w=gnorm_a_w[l], gnorm_b_w=gnorm_b_w[l],
    )


def kernel(x_prompt, x_sample, c, state_gla, state_mlstm_C, state_mlstm_n, state_mlstm_m, c_ctx, w_ada, b_ada, norm1_w, norm2_w, w_in, w_alpha2, b_alpha, b_mgate, conv_w, gnorm_a_w, gnorm_b_w, w_out, w_ff1, w_ff2, final_norm_w):
    depth = w_in.shape[0]
    D = x_prompt.shape[-1]
    Bp, Tp, _ = x_prompt.shape
    Bs = x_sample.shape[0]
    assert 1 + Bs <= COND_ROWS
    cond = (c_ctx.reshape(1, D), c)
    cached = (state_gla, state_mlstm_C, state_mlstm_n, state_mlstm_m)
    xp, xs = x_prompt, x_sample
    s_gla, s_c, s_n, s_m = [], [], [], []
    for l in range(depth):
        lw = _layer_weights(l, norm1_w, norm2_w, w_in, w_alpha2, b_alpha, b_mgate, conv_w,
                            gnorm_a_w, gnorm_b_w)
        xp, xs, ctx = _layer(xp, xs, cond, (w_ada[l], b_ada[l].reshape(1, -1)), cached, lw, l,
                             (w_out[l], w_ff1[l], w_ff2[l]), final_norm_w, l == depth - 1)
        s_gla.append(ctx[0].reshape(Bp, 2, H_A, -1, ctx[0].shape[-1]))
        s_c.append(ctx[1].reshape(Bp, 2, H_B, -1, ctx[1].shape[-1]))
        s_n.append(ctx[2])
        s_m.append(ctx[3])
    dt = x_prompt.dtype
    return (xp, xs, jnp.stack(s_gla, axis=1).astype(dt), jnp.stack(s_c, axis=1).astype(dt),
            jnp.stack(s_n, axis=1).astype(dt), jnp.stack(s_m, axis=1).astype(dt))
```

```python
import functools

import jax
import jax.numpy as jnp
from jax import lax
from jax.experimental import pallas as pl
from jax.experimental.pallas import tpu as pltpu

F32 = jnp.float32
BF16 = jnp.bfloat16

GRID_W = 64
H_A = 4
H_B = 4
R_ALPHA = 16
TAU_GLA = 16.0
CHUNK = 64
EPS = 1e-6
LANES = 128
COND_ROWS = 8
SMALL_W = LANES
GATE_LANE0 = 2 * R_ALPHA
VMEM_LIMIT = 56 * 1024 * 1024
SCAN_UNROLL = 4
PIPELINE_STARTS = 4
TOKEN_TILE = 512
FF_CHUNK = 512
GLA_FACTORED_DECAY_MAX = 60.0
MOD_SPLIT = 2


def _sigmoid(x):
    return 1.0 / (1.0 + jnp.exp(-x))


def _silu(x):
    return x * _sigmoid(x)


def _log_sigmoid(x):
    return jnp.minimum(x, 0.0) - jnp.log1p(jnp.exp(-jnp.abs(x)))


def _dot(a, b):
    return jnp.dot(a, b, preferred_element_type=F32)


def _dot_nt(a, b):
    return lax.dot_general(a, b, (((1,), (1,)), ((), ())), preferred_element_type=F32)


def _rms(x, w):
    return x * lax.rsqrt(jnp.mean(x * x, axis=-1, keepdims=True) + EPS) * w


def _tri_sum(tri, x):
    hi = x.astype(BF16)
    r1 = x - hi.astype(F32)
    mid = r1.astype(BF16)
    lo = (r1 - mid.astype(F32)).astype(BF16)
    return _dot(tri, hi) + _dot(tri, mid) + _dot(tri, lo)


def _chunk_masks(L):
    row = lax.broadcasted_iota(jnp.int32, (L, L), 0)
    col = lax.broadcasted_iota(jnp.int32, (L, L), 1)
    lower = row >= col
    upper = row <= col
    return lower, upper


def _ada_tile(cc_ref, c_ref, w_ref, b_ref, o_ref):
    D = cc_ref.shape[1]
    sub = lax.broadcasted_iota(jnp.int32, (COND_ROWS, D), 0)
    cond = jnp.where(sub == 0, cc_ref[...], 0.0)
    for r in range(c_ref.shape[0]):
        cond = jnp.where(sub == 1 + r, c_ref[r:r + 1, :], cond)
    o_ref[...] = _dot(_silu(cond).astype(BF16), w_ref[...].astype(BF16)) + b_ref[...]


def _ada_call(cc, c, w_ada, b_ada, n_cols):
    D = cc.shape[1]
    tn = 1024
    return pl.pallas_call(
        _ada_tile,
        grid=(n_cols // tn,),
        in_specs=[
            pl.BlockSpec(cc.shape, lambda j: (0, 0)),
            pl.BlockSpec(c.shape, lambda j: (0, 0)),
            pl.BlockSpec((D, tn), lambda j: (0, j)),
            pl.BlockSpec((1, tn), lambda j: (0, j)),
        ],
        out_specs=pl.BlockSpec((COND_ROWS, tn), lambda j: (0, j)),
        out_shape=jax.ShapeDtypeStruct((COND_ROWS, n_cols), F32),
        compiler_params=pltpu.CompilerParams(dimension_semantics=("arbitrary",),
                                             vmem_limit_bytes=VMEM_LIMIT),
        name="ada_mod",
    )(cc, c, w_ada, b_ada)


def _tile_group(n_ctx, tiles_per_req):
    i = pl.program_id(0)
    is_ctx = i < n_ctx
    row = jnp.where(is_ctx, 0, 1 + jnp.maximum(i - n_ctx, 0) // tiles_per_req)
    return is_ctx, row


def _ctx_tile(n_ctx):
    return lambda i: (jnp.minimum(i, n_ctx - 1), 0)


def _lat_tile(n_ctx):
    return lambda i: (jnp.maximum(i - n_ctx, 0), 0)


def _inproj_kernel(xc_ref, xl_ref, mod_ref, nw_ref, wt_ref, z_ref, wb_scr, *, n_ctx, tiles_per_req, big_rows,
                   small_rows):
    D = xc_ref.shape[1]

    @pl.when(pl.program_id(0) == 0)
    def _():
        col = 0
        for r0, n in big_rows:
            for k in range(n // LANES):
                blk = wt_ref[r0 + k * LANES:r0 + (k + 1) * LANES, :]
                wb_scr[:, col:col + LANES] = blk.T.astype(BF16)
                col += LANES
        parts = [wt_ref[r0:r0 + n, :] for r0, n in small_rows]
        n_small = sum(n for _, n in small_rows)
        parts.append(jnp.zeros((SMALL_W - n_small, D), F32))
        wb_scr[:, col:col + SMALL_W] = jnp.concatenate(parts, axis=0).T.astype(BF16)

    is_ctx, row = _tile_group(n_ctx, tiles_per_req)

    def tile(x_ref):
        sh1 = mod_ref[pl.ds(row, 1), 0:D]
        sc1 = mod_ref[pl.ds(row, 1), D:2 * D]
        h = _rms(x_ref[...], nw_ref[...]) * (1.0 + sc1) + sh1
        z_ref[...] = _dot(h.astype(BF16), wb_scr[...])

    @pl.when(is_ctx)
    def _():
        tile(xc_ref)

    @pl.when(jnp.logical_not(is_ctx))
    def _():
        tile(xl_ref)


def _inproj_call(xc2d, xl2d, mod, norm_w, w_in_t, *, tm, tiles_per_req, big_rows, small_rows):
    (Mc, D), Ml = xc2d.shape, xl2d.shape[0]
    n_ctx = Mc // tm
    n_out = sum(n for _, n in big_rows) + SMALL_W
    kern = functools.partial(_inproj_kernel, n_ctx=n_ctx, tiles_per_req=tiles_per_req,
                             big_rows=big_rows, small_rows=small_rows)
    return pl.pallas_call(
        kern,
        grid=((Mc + Ml) // tm,),
        in_specs=[
            pl.BlockSpec((tm, D), _ctx_tile(n_ctx)),
            pl.BlockSpec((tm, D), _lat_tile(n_ctx)),
            pl.BlockSpec(mod.shape, lambda i: (0, 0)),
            pl.BlockSpec((1, D), lambda i: (0, 0)),
            pl.BlockSpec(w_in_t.shape, lambda i: (0, 0), pipeline_mode=pl.Buffered(1)),
        ],
        out_specs=pl.BlockSpec((tm, n_out), lambda i: (i, 0)),
        out_shape=jax.ShapeDtypeStruct((Mc + Ml, n_out), F32),
        scratch_shapes=[pltpu.VMEM((D, n_out), BF16)],
        compiler_params=pltpu.CompilerParams(dimension_semantics=("arbitrary",),
                                             vmem_limit_bytes=VMEM_LIMIT),
        name="norm_inproj",
    )(xc2d, xl2d, mod, norm_w.reshape(1, D), w_in_t)


def _chunk_loop(n_chunks, unroll, make_units):
    def step(ns):
        pending = list(make_units(ns))
        active = []
        while pending or active:
            for _ in range(min(PIPELINE_STARTS, len(pending))):
                active.append(pending.pop(0))
            alive = []
            for g in active:
                try:
                    next(g)
                    alive.append(g)
                except StopIteration:
                    pass
            active = alive

    if unroll >= n_chunks:
        step(list(range(n_chunks)))
        return

    def body(i, carry):
        step([i * unroll + u for u in range(unroll)])
        return carry

    lax.fori_loop(0, n_chunks // unroll, body, 0)


def _chunk_rows(n):
    if isinstance(n, int):
        return pl.ds(n * CHUNK, CHUNK)
    return pl.ds(pl.multiple_of(n * CHUNK, CHUNK), CHUNK)


def _cast_specs(casts, n_steps):
    in_specs, out_specs, out_shape, args = [], [], [], []
    for w, axis in casts:
        blk = list(w.shape)
        assert blk[axis] % n_steps == 0
        blk[axis] //= n_steps
        assert blk[0] % 16 == 0 and blk[1] % LANES == 0
        idx = (lambda b: (b, 0)) if axis == 0 else (lambda b: (0, b))
        in_specs.append(pl.BlockSpec(tuple(blk), idx))
        out_specs.append(pl.BlockSpec(tuple(blk), idx))
        out_shape.append(jax.ShapeDtypeStruct(w.shape, BF16))
        args.append(w)
    return in_specs, out_specs, out_shape, args


def _gla_body(q_ref, k_ref, v_ref, g_ref, sm_ref, s0_ref, wal_ref, bal_ref, gw_ref, out_ref, snew_ref,
              st_scr, sall_scr, qh_scr, qs_scr, kh_scr):
    has_state = s0_ref is not None
    write_state = snew_ref is not None
    T = q_ref.shape[0]
    L = CHUNK
    N = T // L
    HK = q_ref.shape[1]
    DK = HK // H_A
    DV = v_ref.shape[1] // H_A
    scale = DK ** -0.5
    n_pairs = HK // LANES

    lower, upper = _chunk_masks(L)
    tri = (lower.astype(BF16), upper.astype(BF16))
    tmask = (lower, upper)
    lane = lax.broadcasted_iota(jnp.int32, (1, LANES), 1)
    head_mask = (lane < DK, lane >= DK)

    for d in range(2):
        for p in range(n_pairs):
            if has_state:
                st_scr[d, p] = s0_ref[d, p].T
            else:
                st_scr[d, p] = jnp.zeros((LANES, LANES), F32)

    def decay_pre(d, r):
        return _dot(sm_ref[r, :].astype(BF16), wal_ref[:, d * HK:(d + 1) * HK]) + bal_ref[d:d + 1, :]

    def bound_tile(i, span):
        r = pl.ds(pl.multiple_of(i * L, L), L)
        for d in range(2):
            total = jnp.sum(jnp.maximum(-decay_pre(d, r), 0.0), axis=0, keepdims=True)
            span = jnp.maximum(span, jnp.max(total, axis=1, keepdims=True))
        return span

    decay_span = (jnp.max(lax.fori_loop(0, N, bound_tile, jnp.zeros((1, 1), F32)))
                  + L * 0.6931472) * (1.0 / TAU_GLA)

    def state_group(ns, dirs=(0, 1)):
        units = [(d, n if d == 0 else N - 1 - n) for n in ns for d in dirs]
        rows = [_chunk_rows(n) for _, n in units]
        vt_all = [[jnp.concatenate([v_ref[r, (2 * p + j) * DV:(2 * p + j + 1) * DV] for j in range(2)],
                                   axis=0).T.astype(BF16) for p in range(n_pairs)] for r in rows]
        yield
        pre = [decay_pre(d, r) for (d, _), r in zip(units, rows)]
        yield
        g = [_log_sigmoid(x) * (1.0 / TAU_GLA) for x in pre]
        yield
        b = [_tri_sum(tri[d], gi) for (d, _), gi in zip(units, g)]
        yield
        ks_all, dec_all = [], []
        for (d, _), r, bi in zip(units, rows, b):
            bend = bi[L - 1:L, :] if d == 0 else bi[0:1, :]
            q = q_ref[r, :] * scale
            ks = (k_ref[r, :] * jnp.exp(bend - bi)).astype(BF16)
            qh_scr[d, r, :] = (q * jnp.exp(bi - bend)).astype(BF16)
            qs_scr[d, r, :] = (q * jnp.exp(bi)).astype(BF16)
            kh_scr[d, r, :] = ks
            ks_all.append(ks)
            dec_all.append(jnp.exp(bend))
        yield
        upd_all = []
        for vt_u, ks in zip(vt_all, ks_all):
            upd_u = []
            for p in range(n_pairs):
                kp = ks[:, p * LANES:(p + 1) * LANES]
                kk = jnp.concatenate([jnp.where(head_mask[j], kp, jnp.zeros_like(kp)) for j in range(2)], axis=0)
                upd_u.append(_dot(vt_u[p], kk))
            upd_all.append(upd_u)
        yield
        st = {d: [st_scr[d, p] for p in range(n_pairs)] for d in dirs}
        for (d, n), dec, upd in zip(units, dec_all, upd_all):
            for p in range(n_pairs):
                sall_scr[d, n, p] = st[d][p].astype(BF16)
                st[d][p] = st[d][p] * dec[:, p * LANES:(p + 1) * LANES] + upd[p]
        for d in dirs:
            for p in range(n_pairs):
                st_scr[d, p] = st[d][p]

    def stack_heads(x):
        return jnp.concatenate([jnp.where(head_mask[j], x, jnp.zeros_like(x)) for j in range(2)], axis=0)

    tok = lax.broadcasted_iota(jnp.int32, (L, 1), 0)
    row_t = lax.broadcasted_iota(jnp.int32, (2 * L, L), 0) & (L - 1)
    col_s = lax.broadcasted_iota(jnp.int32, (2 * L, L), 1)

    def exact_scores(d, r, p):
        ls = slice(p * LANES, (p + 1) * LANES)
        b = _tri_sum(tri[d], _log_sigmoid(decay_pre(d, r)[:, ls]) * (1.0 / TAU_GLA))
        q = q_ref[r, ls] * scale
        k = k_ref[r, ls]
        acc = jnp.where(row_t == col_s, _dot_nt(stack_heads(q).astype(BF16), k.astype(BF16)), 0.0)
        src = lax.broadcasted_iota(jnp.int32, (L, L), 1)
        h = L // 2
        while h >= 1:
            first = tok & ~(2 * h - 1)
            edge = first + (h - 1 if d == 0 else h)
            b_edge = _tri_sum((src == edge).astype(BF16), b)
            upper = (tok & (2 * h - 1)) >= h
            later, earlier = (upper, ~upper) if d == 0 else (~upper, upper)
            qt = jnp.where(later, q * jnp.exp(b - b_edge), 0.0)
            kt = jnp.where(earlier, k * jnp.exp(b_edge - b), 0.0)
            sc = _dot_nt(stack_heads(qt).astype(BF16), kt.astype(BF16))
            acc = acc + jnp.where((row_t & ~(2 * h - 1)) == (col_s & ~(2 * h - 1)), sc, 0.0)
            h //= 2
        return acc

    def out_group(ns, exact_decay=False):
        pairs = [(d, ni, p) for ni in range(len(ns)) for d in range(2) for p in range(n_pairs)]
        scores, inter = [], []
        for d, ni, p in pairs:
            r = _chunk_rows(ns[ni])
            ls = slice(p * LANES, (p + 1) * LANES)
            if exact_decay:
                scores.append(exact_scores(d, r, p))
            else:
                scores.append(_dot_nt(stack_heads(qh_scr[d, r, ls]), kh_scr[d, r, ls]))
            inter.append(_dot_nt(stack_heads(qs_scr[d, r, ls]), sall_scr[d, ns[ni], p]))
        yield
        probs = [[jnp.where(tmask[d], sc[j * L:(j + 1) * L, :], 0.0).astype(BF16) for j in range(2)]
                 for (d, _, _), sc in zip(pairs, scores)]
        yield
        outs = {}
        for (d, ni, p), pr, it in zip(pairs, probs, inter):
            r = _chunk_rows(ns[ni])
            for j in range(2):
                vs = slice((2 * p + j) * DV, (2 * p + j + 1) * DV)
                outs[(d, ni, 2 * p + j)] = _dot(pr[j], v_ref[r, vs].astype(BF16)) + it[j * L:(j + 1) * L, :]
        yield
        for ni, n in enumerate(ns):
            r = _chunk_rows(n)
            for h in range(H_A):
                vs = slice(h * DV, (h + 1) * DV)
                o = outs[(0, ni, h)] + outs[(1, ni, h)]
                out_ref[r, vs] = (_rms(o, gw_ref[:, vs]) * _silu(g_ref[r, vs])).astype(out_ref.dtype)

    def finish():
        if write_state:
            for d in range(2):
                for p in range(n_pairs):
                    snew_ref[d, p] = st_scr[d, p].T

    return state_group, out_group, finish, decay_span


def _gla_scratch(T, HK):
    n_pairs = HK // LANES
    n_chunks = T // CHUNK
    return [
        pltpu.VMEM((2, n_pairs, LANES, LANES), F32),
        pltpu.VMEM((2, n_chunks, n_pairs, LANES, LANES), BF16),
        pltpu.VMEM((2, T, HK), BF16),
        pltpu.VMEM((2, T, HK), BF16),
        pltpu.VMEM((2, T, HK), BF16),
    ]


def _mlstm_body(qk_ref, v_ref, og_ref, sm_ref, c0_ref, n0_ref, m0_ref, cw_ref, bm_ref, gw_ref,
                out_ref, cnew_ref, nnew_ref, mnew_ref,
                pad_scr, qk_scr, y_scr, c_scr, n_scr, m_scr, call_scr, nall_scr, mall_scr, g_scr, f_scr,
                *, grid_w):
    has_state = c0_ref is not None
    write_state = cnew_ref is not None
    T = qk_ref.shape[0]
    L = CHUNK
    N = T // L
    C2 = qk_ref.shape[1]
    HK = C2 // 2
    DK = HK // H_B
    DV = v_ref.shape[1] // H_B
    scale = DK ** -0.5
    n_pairs = HK // LANES
    P = pad_scr.shape[0] - T
    P0 = P // 2
    rows_img = T // grid_w

    lower, upper = _chunk_masks(L)
    tri = (lower.astype(BF16), upper.astype(BF16))
    tmask = (lower, upper)
    lane = lax.broadcasted_iota(jnp.int32, (1, LANES), 1)
    head_mask = (lane < DK, lane >= DK)
    lane_in = lane & (L - 1)

    def lane_cummax(x, d):
        k = 1
        while k < L:
            if d == 0:
                x = jnp.maximum(x, jnp.where(lane_in >= k, pltpu.roll(x, k, axis=1), -jnp.inf))
            else:
                x = jnp.maximum(x, jnp.where(lane_in < L - k, pltpu.roll(x, LANES - k, axis=1), -jnp.inf))
            k *= 2
        return x

    for d in range(2):
        for p in range(n_pairs):
            if has_state:
                c_scr[d, p] = c0_ref[d, p]
                n_scr[2 * d + p:2 * d + p + 1, :] = jnp.concatenate(
                    [n0_ref[d, 2 * p + j:2 * p + j + 1, :] for j in range(2)], axis=1)
            else:
                c_scr[d, p] = jnp.zeros((LANES, LANES), F32)
                n_scr[2 * d + p:2 * d + p + 1, :] = jnp.zeros((1, LANES), F32)
    eye_h = (lax.broadcasted_iota(jnp.int32, (H_B, H_B), 0) == lax.broadcasted_iota(jnp.int32, (H_B, H_B), 1))

    def to_col(row):
        return jnp.sum(jnp.where(eye_h, row, 0.0), axis=1, keepdims=True)

    def to_row(col):
        return jnp.sum(jnp.where(eye_h, col, 0.0), axis=0, keepdims=True)

    for d in range(2):
        if has_state:
            m_scr[H_B * d:H_B * (d + 1), 0:1] = to_col(m0_ref[d:d + 1, :])
        else:
            m_scr[H_B * d:H_B * (d + 1), 0:1] = jnp.zeros((H_B, 1), F32)

    pad_scr[0:P0, :] = jnp.zeros((P0, C2), F32)
    pad_scr[P0 + T:P + T, :] = jnp.zeros((P - P0, C2), F32)

    def copy_in(i, carry):
        r0 = pl.multiple_of(i * L, L)
        pad_scr[pl.ds(P0 + r0, L), :] = qk_ref[pl.ds(r0, L), :]
        return carry

    lax.fori_loop(0, N, copy_in, 0)

    lane_c = lax.broadcasted_iota(jnp.int32, (1, C2), 1)
    qscale = jnp.where(lane_c < HK, scale, 1.0).astype(F32)
    sub = lax.broadcasted_iota(jnp.int32, (L, 1), 0)
    img_rows = (0,) if rows_img == 1 else (-1, 0, 1)

    def conv_tile(i, carry):
        r0 = pl.multiple_of(i * L, L)
        col = lax.rem(r0, grid_w) + sub
        ok_left = col >= 1
        ok_right = col <= grid_w - 2
        sums = [None, None, None]
        for di in img_rows:
            blk = pad_scr[pl.ds(P0 + r0 + di * grid_w - 8, L + 16), :]
            for k in range(3):
                term = blk * cw_ref[di + 1, k:k + 1, :]
                sums[k] = term if sums[k] is None else sums[k] + term
        acc = (sums[1][8:8 + L, :] + jnp.where(ok_left, sums[0][7:7 + L, :], 0.0)
               + jnp.where(ok_right, sums[2][9:9 + L, :], 0.0))
        qk_scr[pl.ds(r0, L), :] = _silu(acc) * qscale
        return carry

    lax.fori_loop(0, N, conv_tile, 0)

    gl = lane - GATE_LANE0
    is_f = ((gl >= H_B) & (gl < 2 * H_B)) | ((gl >= 3 * H_B) & (gl < 4 * H_B))

    def gate_tile(i, carry):
        rows = pl.ds(pl.multiple_of(i * L, L), L)
        x = sm_ref[rows, :] + bm_ref[...]
        y_scr[rows, :] = jnp.where(is_f, _log_sigmoid(x), x)
        return carry

    lax.fori_loop(0, N, gate_tile, 0)


    def state_group(ns, dirs=(0, 1)):
        units = [(d, n if d == 0 else N - 1 - n) for n in ns for d in dirs]
        rows = [_chunk_rows(n) for _, n in units]
        kt_all = [[qk_scr[r, HK + p * LANES:HK + (p + 1) * LANES].T for p in range(n_pairs)] for r in rows]
        yield
        xs = [y_scr[r, :] for r in rows]
        fsum = [_tri_sum(tri[d], x) for (d, _), x in zip(units, xs)]
        yield
        wk_all, f_end, c_end = [], [], []
        for (d, n), r, x, fs in zip(units, rows, xs, fsum):
            y = jnp.where(is_f, fs, x)
            li0 = GATE_LANE0 + 2 * H_B * d
            blk = jnp.concatenate([y, y], axis=0).T[li0:li0 + 2 * H_B, :]
            frow = pltpu.roll(blk, H_B, axis=0)
            grow = blk - frow
            g_scr[d, n] = grow
            f_scr[d, n] = frow
            e_col = L - 1 if d == 0 else 0
            f_end.append(frow[0:H_B, e_col:e_col + 1])
            ce8 = jnp.max(grow, axis=1, keepdims=True)
            c_end.append(ce8[0:H_B, :])
            wk_all.append(jnp.exp(grow[:, 0:L] - ce8))
        yield
        kv_all, ksum_all = [], []
        for r, wk8, kt_u in zip(rows, wk_all, kt_all):
            kv_u, ks_u = [], []
            wk8b = wk8.astype(BF16)
            for p in range(n_pairs):
                kpb = qk_scr[r, HK + p * LANES:HK + (p + 1) * LANES].astype(BF16)
                ks8 = _dot(wk8b, kpb)
                for j in range(2):
                    h = 2 * p + j
                    kwt = (kt_u[p][j * DK:(j + 1) * DK, :] * wk8[h:h + 1, :]).astype(BF16)
                    kv_u.append(_dot(kwt, v_ref[r, h * DV:(h + 1) * DV].astype(BF16)))
                    ks_u.append(ks8[h:h + 1, :])
            kv_all.append(kv_u)
            ksum_all.append(ks_u)
        yield
        m_run = {d: m_scr[H_B * d:H_B * (d + 1), 0:1] for d in dirs}
        a_all, b_all = [], []
        for (d, n), fe, ce in zip(units, f_end, c_end):
            mall_scr[d, n, 0:H_B, 0:1] = m_run[d]
            mx = jnp.maximum(m_run[d], ce)
            a_all.append(jnp.exp(m_run[d] - mx))
            b_all.append(jnp.exp(ce - mx))
            m_run[d] = fe + mx
        for d in dirs:
            m_scr[H_B * d:H_B * (d + 1), 0:1] = m_run[d]
        yield
        c_run = {d: [[c_scr[d, p, j * DK:(j + 1) * DK, :] for j in range(2)] for p in range(n_pairs)] for d in dirs}
        n_run = {d: [n_scr[2 * d + p:2 * d + p + 1, :] for p in range(n_pairs)] for d in dirs}
        for (d, n), a4, b4, kv_u, ks_u in zip(units, a_all, b_all, kv_all, ksum_all):
            for p in range(n_pairs):
                nall_scr[d, n, p:p + 1, :] = n_run[d][p]
                a_s = [a4[2 * p + j:2 * p + j + 1, :] for j in range(2)]
                b_s = [b4[2 * p + j:2 * p + j + 1, :] for j in range(2)]
                for j in range(2):
                    cj = c_run[d][p][j]
                    call_scr[d, n, p, j * DK:(j + 1) * DK, :] = cj.astype(BF16)
                    c_run[d][p][j] = a_s[j] * cj + b_s[j] * kv_u[2 * p + j]
                n_run[d][p] = (jnp.where(head_mask[0], a_s[0], a_s[1]) * n_run[d][p]
                               + jnp.where(head_mask[0], b_s[0] * ks_u[2 * p], b_s[1] * ks_u[2 * p + 1]))
        for d in dirs:
            for p in range(n_pairs):
                n_scr[2 * d + p:2 * d + p + 1, :] = n_run[d][p]
                for j in range(2):
                    c_scr[d, p, j * DK:(j + 1) * DK, :] = c_run[d][p][j]

    eye = lower & upper
    ones8 = jnp.ones((8, L), BF16)
    sub8 = lax.broadcasted_iota(jnp.int32, (8, LANES), 0)
    sub_h = lax.broadcasted_iota(jnp.int32, (H_B, L), 0)
    n_rows = [((sub8 == 2 * p) & head_mask[0]) | ((sub8 == 2 * p + 1) & head_mask[1]) for p in range(n_pairs)]

    def head_rows(vals):
        out = vals[0][0:H_B, :]
        for h in range(1, H_B):
            out = jnp.where(sub_h == h, vals[h][0:H_B, :], out)
        return out

    def out_group(ns):
        chunks = [(d, n) for n in ns for d in range(2)]
        pairs = [(d, n, p) for d, n in chunks for p in range(n_pairs)]
        units = [(d, n, p, j) for d, n, p in pairs for j in range(2)]
        cms = [lane_cummax(g_scr[d, n], d)[0:H_B, 0:L] for d, n in chunks]
        qk2s, qc2s, qn2s = [], [], []
        for d, n, p in pairs:
            r = _chunk_rows(n)
            qp = qk_scr[r, p * LANES:(p + 1) * LANES]
            q2 = jnp.concatenate([jnp.where(head_mask[j], qp, 0.0) for j in range(2)], axis=0).astype(BF16)
            qk2s.append(_dot_nt(q2, qk_scr[r, HK + p * LANES:HK + (p + 1) * LANES].astype(BF16)))
            qc2s.append(_dot(q2, call_scr[d, n, p]))
            nsel = jnp.where(n_rows[p], nall_scr[d, n, p:p + 1, :], 0.0).astype(BF16)
            qn2s.append(_dot_nt(nsel, qp.astype(BF16)))
        yield
        s_all = []
        for ui, (d, n, p, j) in enumerate(units):
            grow = g_scr[d, n, 2 * p + j:2 * p + j + 1, 0:L]
            e = jnp.where(tmask[d], grow, -jnp.inf)
            cmax = jnp.max(e, axis=-1, keepdims=True)
            s_all.append((qk2s[ui // 2][j * L:(j + 1) * L, :] * jnp.exp(e - cmax)).astype(BF16))
        yield
        nums =[_dot(s, v_ref[_chunk_rows(n), (2 * p + j) * DV:(2 * p + j + 1) * DV].astype(BF16))
                for (d, n, p, j), s in zip(units, s_all)]
        dens = [_dot_nt(ones8, s) for s in s_all]
        yield
        scales = []
        for ci, (d, n) in enumerate(chunks):
            den_loc = head_rows(dens[ci * H_B:(ci + 1) * H_B])
            qn = qn2s[ci * n_pairs][0:H_B, :]
            for p in range(1, n_pairs):
                qn = qn + qn2s[ci * n_pairs + p][0:H_B, :]
            cm = cms[ci]
            m_prev = mall_scr[d, n, 0:H_B, 0:1]
            delta = cm - m_prev
            t = jnp.exp(-jnp.abs(delta))
            w_loc = jnp.where(delta <= 0.0, t, 1.0)
            w_inter = jnp.where(delta <= 0.0, 1.0, t)
            mt = f_scr[d, n, 0:H_B, 0:L] + jnp.maximum(m_prev, cm)
            den = w_loc * den_loc + w_inter * qn
            rinv = 1.0 / jnp.maximum(jnp.abs(den), jnp.exp(-mt))
            scales.append((w_loc * rinv, w_inter * rinv))
        yield
        hs = []
        for ui, (d, n, p, j) in enumerate(units):
            h = 2 * p + j
            sc_loc, sc_inter = scales[ui // H_B]
            d_loc = jnp.where(eye, sc_loc[h:h + 1, :], 0.0).astype(BF16)
            d_inter = jnp.where(eye, sc_inter[h:h + 1, :], 0.0).astype(BF16)
            hs.append(_dot(d_loc, nums[ui].astype(BF16))
                      + _dot(d_inter, qc2s[ui // 2][j * L:(j + 1) * L, :].astype(BF16)))
        yield
        for ni, n in enumerate(ns):
            r = _chunk_rows(n)
            for h in range(H_B):
                vs = slice(h * DV, (h + 1) * DV)
                o = hs[(2 * ni) * H_B + h] + hs[(2 * ni + 1) * H_B + h]
                out_ref[r, vs] = (_rms(o, gw_ref[:, vs]) * _sigmoid(og_ref[r, vs])).astype(out_ref.dtype)

    def finish():
        if write_state:
            for d in range(2):
                for p in range(n_pairs):
                    cnew_ref[d, p] = c_scr[d, p]
                    for j in range(2):
                        nnew_ref[d, 2 * p + j:2 * p + j + 1, :] = n_scr[2 * d + p:2 * d + p + 1, j * DK:(j + 1) * DK]
                mnew_ref[d:d + 1, :] = to_row(m_scr[H_B * d:H_B * (d + 1), 0:1])

    return state_group, out_group, finish


def _mlstm_scratch(T, C2, grid_w):
    n_pairs = C2 // 2 // LANES
    n_chunks = T // CHUNK
    pad_rows = 2 * (grid_w + 8) if T // grid_w > 1 else 16
    return [
        pltpu.VMEM((T + pad_rows, C2), F32),
        pltpu.VMEM((T, C2), F32),
        pltpu.VMEM((T, SMALL_W), F32),
        pltpu.VMEM((2, n_pairs, LANES, LANES), F32),
        pltpu.VMEM((8, LANES), F32),
        pltpu.VMEM((8, LANES), F32),
        pltpu.VMEM((2, n_chunks, n_pairs, LANES, LANES), BF16),
        pltpu.VMEM((2, n_chunks, 8, LANES), F32),
        pltpu.VMEM((2, n_chunks, 8, LANES), F32),
        pltpu.VMEM((2, n_chunks, 8, LANES), F32),
        pltpu.VMEM((2, n_chunks, 8, LANES), F32),
    ]


N_GLA_SCRATCH = 5
N_MLSTM_SCRATCH = 11


def _scan_kernel(*refs, cols, layer, has_state, write_state, n_cast, ride_ada, grid_w, unroll):
    refs = list(refs)
    z_ref = refs.pop(0)
    s0_ref = c0_ref = n0_ref = m0_ref = None
    if has_state:
        s0_ref, c0_ref, n0_ref, m0_ref = refs[:4]
        del refs[:4]
    wa_ref, bal_ref, gwa_ref, cw_ref, bmg_ref, gwb_ref = refs[:6]
    del refs[:6]
    cast_in = refs[:n_cast]
    del refs[:n_cast]
    if ride_ada:
        ada_in = refs[:4]
        del refs[:4]
    outa_ref, outb_ref = refs[:2]
    del refs[:2]
    snew_ref = cnew_ref = nnew_ref = mnew_ref = None
    if write_state:
        snew_ref, cnew_ref, nnew_ref, mnew_ref = refs[:4]
        del refs[:4]
    cast_out = refs[:n_cast]
    del refs[:n_cast]
    if ride_ada:
        ada_out = refs.pop(0)
    wal_scr, bm_scr = refs[:2]
    del refs[:2]
    gla_scr = refs[:N_GLA_SCRATCH]
    mlstm_scr = refs[N_GLA_SCRATCH:]

    for src, dst in zip(cast_in, cast_out):
        dst[...] = src[...].astype(BF16)
    if ride_ada:
        _ada_tile(*ada_in, ada_out)

    R, HK = wa_ref.shape[1], wa_ref.shape[2]
    wal_scr[...] = jnp.zeros(wal_scr.shape, BF16)
    for d in range(2):
        wal_scr[d * R:(d + 1) * R, d * HK:(d + 1) * HK] = wa_ref[d].astype(BF16)
    lane = lax.broadcasted_iota(jnp.int32, (1, LANES), 1)
    bm = jnp.zeros((1, LANES), F32)
    for g in range(bmg_ref.shape[1]):
        for h in range(H_B):
            bm = jnp.where(lane == GATE_LANE0 + H_B * g + h, bmg_ref[layer, g, h], bm)
    bm_scr[0:1, :] = bm

    def view(name):
        c0, w = cols[name]
        return z_ref.at[:, pl.ds(c0, w)]

    sm_ref = view("small")
    n_chunks = z_ref.shape[0] // CHUNK
    gla = _gla_body(view("qa"), view("ka"), view("va"), view("ga"), sm_ref, s0_ref, wal_scr, bal_ref, gwa_ref,
                    outa_ref, snew_ref, *gla_scr)
    mlstm = _mlstm_body(view("qkb"), view("vb"), view("ob"), sm_ref, c0_ref, n0_ref, m0_ref, cw_ref,
                        bm_scr.at[0:1, :], gwb_ref, outb_ref, cnew_ref, nnew_ref, mnew_ref, *mlstm_scr,
                        grid_w=grid_w)
    gla_state, gla_out, gla_finish, decay_span = gla
    mlstm_state, mlstm_out, mlstm_finish = mlstm

    def passes(gla_out_fn):
        _chunk_loop(n_chunks, unroll, lambda ns: [fn([n], (d,)) for n in ns for d in range(2)
                                                  for fn in (mlstm_state, gla_state)])
        _chunk_loop(n_chunks, unroll, lambda ns: [fn([n]) for n in ns for fn in (mlstm_out, gla_out_fn)])

    wide_decay = decay_span > GLA_FACTORED_DECAY_MAX

    @pl.when(jnp.logical_not(wide_decay))
    def _():
        passes(gla_out)

    @pl.when(wide_decay)
    def _():
        passes(functools.partial(gla_out, exact_decay=True))

    gla_finish()
    mlstm_finish()


def _scan_call(z2d, row0, B, T, states, lw, layer, *, grid_w, write_state, casts=(), ada=None):
    n_z = z2d.shape[1]
    assert row0 % T == 0 and z2d.shape[0] % T == 0
    z3 = z2d.reshape(z2d.shape[0] // T, T, n_z)
    blk0 = row0 // T
    HK = lw["w_alpha2"].shape[-1]
    DA = lw["gnorm_a_w"].shape[0]
    C2 = lw["conv_w"].shape[-1]
    DB = lw["gnorm_b_w"].shape[0]
    DK_A, DK_B = HK // H_A, C2 // 2 // H_B
    pa, pb = HK // LANES, C2 // 2 // LANES
    n_chunks = T // CHUNK
    has_state = states is not None
    widths = (("qa", HK), ("ka", HK), ("va", DA), ("ga", DA), ("qkb", C2), ("vb", DB), ("ob", DB),
              ("small", SMALL_W))
    cols, c0 = {}, 0
    for name, w in widths:
        cols[name] = (c0, w)
        c0 += w
    assert c0 == n_z
    cast_in_specs, cast_out_specs, cast_out_shape, cast_args = _cast_specs(casts, B)
    kern = functools.partial(_scan_kernel, cols=cols, layer=layer, has_state=has_state, write_state=write_state,
                             n_cast=len(casts), ride_ada=ada is not None, grid_w=grid_w,
                             unroll=min(n_chunks, SCAN_UNROLL))

    def per_batch(shape):
        nd = len(shape)
        return pl.BlockSpec((None,) + tuple(shape), lambda b: (b,) + (0,) * nd)

    def per_batch_layer(shape):
        nd = len(shape)
        return pl.BlockSpec((None, None) + tuple(shape), lambda b: (b, layer) + (0,) * nd)

    def of_layer(a):
        return pl.BlockSpec((None,) + a.shape[1:], lambda b: (layer,) + (0,) * (a.ndim - 1))

    def whole(a):
        return pl.BlockSpec(a.shape, lambda b: (0,) * a.ndim)

    state_shapes = ((2, pa, LANES, LANES), (2, pb, LANES, LANES), (2, H_B, DK_B), (2, H_B))
    in_specs = [pl.BlockSpec((None, T, n_z), lambda b: (b + blk0, 0, 0))]
    args = [z3]
    if has_state:
        s_gla, s_c, s_n, s_m = states
        depth = s_gla.shape[1]
        args += [s_gla.reshape((B, depth) + state_shapes[0]), s_c.reshape((B, depth) + state_shapes[1]), s_n, s_m]
        in_specs += [per_batch_layer(s) for s in state_shapes]
    args += [lw["w_alpha2"], lw["b_alpha"], lw["gnorm_a_w"].reshape(1, DA), lw["conv_w"], lw["b_mgate"],
             lw["gnorm_b_w"].reshape(1, DB)]
    in_specs += [of_layer(lw["w_alpha2"]), of_layer(lw["b_alpha"]), pl.BlockSpec((1, DA), lambda b: (0, 0)),
                 whole(lw["conv_w"]), pl.BlockSpec(memory_space=pltpu.SMEM), pl.BlockSpec((1, DB), lambda b: (0, 0))]
    args += cast_args
    in_specs += cast_in_specs
    out_specs = [per_batch((T, DA)), per_batch((T, DB))]
    out_shape = [jax.ShapeDtypeStruct((B, T, DA), BF16), jax.ShapeDtypeStruct((B, T, DB), BF16)]
    if write_state:
        out_specs += [per_batch(s) for s in state_shapes]
        out_shape += [jax.ShapeDtypeStruct((B,) + s, F32) for s in state_shapes]
    out_specs += cast_out_specs
    out_shape += cast_out_shape
    if ada is not None:
        cc, c, w_ada, b_ada, col0 = ada
        n_rest = w_ada.shape[1] - col0
        wcol = n_rest // B
        assert n_rest % B == 0 and wcol % LANES == 0 and col0 % wcol == 0
        args += [cc, c, w_ada, b_ada]
        in_specs += [whole(cc), whole(c),
                     pl.BlockSpec((w_ada.shape[0], wcol), lambda b: (0, col0 // wcol + b)),
                     pl.BlockSpec((1, wcol), lambda b: (0, col0 // wcol + b))]
        out_specs.append(pl.BlockSpec((COND_ROWS, wcol), lambda b: (0, b)))
        out_shape.append(jax.ShapeDtypeStruct((COND_ROWS, n_rest), F32))
    scratch = ([pltpu.VMEM((SMALL_W, 2 * HK), BF16), pltpu.VMEM((8, LANES), F32)]
               + _gla_scratch(T, HK) + _mlstm_scratch(T, C2, grid_w))
    assert len(scratch) == 2 + N_GLA_SCRATCH + N_MLSTM_SCRATCH
    return pl.pallas_call(
        kern,
        grid=(B,),
        in_specs=in_specs,
        out_specs=out_specs,
        out_shape=out_shape,
        scratch_shapes=scratch,
        compiler_params=pltpu.CompilerParams(dimension_semantics=("arbitrary",),
                                             vmem_limit_bytes=VMEM_LIMIT),
        name="mixer_scans",
    )(*args)


def _outff_kernel(xc_ref, xl_ref, ac_ref, al_ref, bc_ref, bl_ref, mod_ref, n2_ref, fn_ref, wo_ref, w1_ref, w2_ref,
                  yc_ref, yl_ref, *, n_ctx, tiles_per_req, ff_chunk, final_norm):
    D = xc_ref.shape[1]
    DA = ac_ref.shape[1]
    is_ctx, row = _tile_group(n_ctx, tiles_per_req)

    def mod(k):
        return mod_ref[pl.ds(row, 1), (k - MOD_SPLIT) * D:(k - MOD_SPLIT + 1) * D]

    def tile(x_ref, a_ref, b_ref, y_ref):
        y = _dot(a_ref[...], wo_ref[0:DA, :]) + _dot(b_ref[...], wo_ref[DA:, :])
        x1 = x_ref[...] + mod(2) * y
        h2 = (_rms(x1, n2_ref[...]) * (1.0 + mod(4)) + mod(3)).astype(BF16)
        acc = jnp.zeros(x1.shape, F32)
        for c0 in range(0, w1_ref.shape[1], ff_chunk):
            u = jnp.maximum(_dot(h2, w1_ref[:, c0:c0 + ff_chunk]), 0.0)
            acc = acc + _dot((u * u).astype(BF16), w2_ref[c0:c0 + ff_chunk, :])
        x2 = x1 + mod(5) * acc
        y_ref[...] = _rms(x2, fn_ref[...]) if final_norm else x2

    @pl.when(is_ctx)
    def _():
        tile(xc_ref, ac_ref, bc_ref, yc_ref)

    @pl.when(jnp.logical_not(is_ctx))
    def _():
        tile(xl_ref, al_ref, bl_ref, yl_ref)


def _outff_call(xc2d, xl2d, ac, al, bc, bl, mod, norm2_w, final_w, wo, w1, w2, *, tm, tiles_per_req, final_norm):
    (Mc, D), Ml = xc2d.shape, xl2d.shape[0]
    n_ctx = Mc // tm
    DA = ac.shape[1]
    DFF = w1.shape[1]
    kern = functools.partial(_outff_kernel, n_ctx=n_ctx, tiles_per_req=tiles_per_req, ff_chunk=FF_CHUNK,
                             final_norm=final_norm)
    once = pl.Buffered(1)
    ctx, lat = _ctx_tile(n_ctx), _lat_tile(n_ctx)
    return pl.pallas_call(
        kern,
        grid=((Mc + Ml) // tm,),
        in_specs=[
            pl.BlockSpec((tm, D), ctx), pl.BlockSpec((tm, D), lat),
            pl.BlockSpec((tm, DA), ctx), pl.BlockSpec((tm, DA), lat),
            pl.BlockSpec((tm, D - DA), ctx), pl.BlockSpec((tm, D - DA), lat),
            pl.BlockSpec(mod.shape, lambda i: (0, 0)),
            pl.BlockSpec((1, D), lambda i: (0, 0)),
            pl.BlockSpec((1, D), lambda i: (0, 0)),
            pl.BlockSpec((D, D), lambda i: (0, 0), pipeline_mode=once),
            pl.BlockSpec((D, DFF), lambda i: (0, 0), pipeline_mode=once),
            pl.BlockSpec((DFF, D), lambda i: (0, 0), pipeline_mode=once),
        ],
        out_specs=[pl.BlockSpec((tm, D), ctx), pl.BlockSpec((tm, D), lat)],
        out_shape=[jax.ShapeDtypeStruct((Mc, D), F32), jax.ShapeDtypeStruct((Ml, D), F32)],
        compiler_params=pltpu.CompilerParams(dimension_semantics=("arbitrary",),
                                             vmem_limit_bytes=VMEM_LIMIT),
        name="outproj_mlp",
    )(xc2d, xl2d, ac, al, bc, bl, mod, norm2_w.reshape(1, D), final_w.reshape(1, D), wo, w1, w2)


def _layer(xc, xl, cond, ada_w, cached, lw, layer, ffw, final_w, final_norm):
    (Bc, Tc, D), (Bl, Tl, _) = xc.shape, xl.shape
    tm = TOKEN_TILE
    assert (Bc * Tc) % tm == 0 and Tl % tm == 0 and (Bc * Tc) % Tl == 0
    xc2d, xl2d = xc.reshape(Bc * Tc, D), xl.reshape(Bl * Tl, D)
    mod_in = _ada_call(*cond, *ada_w, MOD_SPLIT * D)
    z = _inproj_call(xc2d, xl2d, mod_in, lw["norm1_w"], lw["w_in_t"], tm=tm, tiles_per_req=Tl // tm,
                     big_rows=lw["big_rows"], small_rows=lw["small_rows"])
    res_c = _scan_call(z, 0, Bc, Tc, None, lw, layer, grid_w=Tc, write_state=True,
                       casts=((ffw[0], 0), (ffw[1], 1), (ffw[2], 0)), ada=(*cond, *ada_w, MOD_SPLIT * D))
    res_l = _scan_call(z, Bc * Tc, Bl, Tl, cached, lw, layer, grid_w=GRID_W, write_state=False)
    wo_b, w1_b, w2_b, mod_out = res_c[-4:]
    yc, yl = _outff_call(xc2d, xl2d, res_c[0].reshape(Bc * Tc, -1), res_l[0].reshape(Bl * Tl, -1),
                         res_c[1].reshape(Bc * Tc, -1), res_l[1].reshape(Bl * Tl, -1), mod_out, lw["norm2_w"],
                         final_w, wo_b, w1_b, w2_b, tm=tm, tiles_per_req=Tl // tm, final_norm=final_norm)
    return yc.reshape(Bc, Tc, D), yl.reshape(Bl, Tl, D), tuple(res_c[2:6])


def _layer_weights(l, norm1_w, norm2_w, w_in, w_alpha2, b_alpha, b_mgate, conv_w, gnorm_a_w, gnorm_b_w):
    hk_a = w_alpha2.shape[-1]
    d_a = gnorm_a_w.shape[-1]
    d_b = gnorm_b_w.shape[-1]
    hk_b = conv_w.shape[-1] // 2
    sizes = (hk_a, hk_a, d_a, d_a, 2 * R_ALPHA, hk_b, hk_b, d_b, d_b, 4 * H_B)
    assert w_alpha2.shape[2] == R_ALPHA and b_mgate.shape[1] * b_mgate.shape[2] == 4 * H_B
    offs = [0]
    for s in sizes:
        offs.append(offs[-1] + s)
    big_rows = ((offs[0], offs[4] - offs[0]), (offs[5], offs[9] - offs[5]))
    small_rows = ((offs[4], offs[5] - offs[4]), (offs[9], offs[10] - offs[9]))
    assert all(n % LANES == 0 and r % 16 == 0 for r, n in big_rows)
    return dict(
        norm1_w=norm1_w[l], norm2_w=norm2_w[l], w_in_t=jnp.swapaxes(w_in[l], 0, 1),
        big_rows=big_rows, small_rows=small_rows,
        w_alpha2=w_alpha2, b_alpha=b_alpha, b_mgate=b_mgate, conv_w=conv_w[l],
        gnorm_a_w=gnorm_a_w[l], gnorm_b_w=gnorm_b_w[l],
    )


def kernel(x_prompt, x_sample, c, state_gla, state_mlstm_C, state_mlstm_n, state_mlstm_m, c_ctx, w_ada, b_ada, norm1_w, norm2_w, w_in, w_alpha2, b_alpha, b_mgate, conv_w, gnorm_a_w, gnorm_b_w, w_out, w_ff1, w_ff2, final_norm_w):
    depth = w_in.shape[0]
    D = x_prompt.shape[-1]
    Bp, Tp, _ = x_prompt.shape
    Bs = x_sample.shape[0]
    assert 1 + Bs <= COND_ROWS
    cond = (c_ctx.reshape(1, D), c)
    cached = (state_gla, state_mlstm_C, state_mlstm_n, state_mlstm_m)
    xp, xs = x_prompt, x_sample
    s_gla, s_c, s_n, s_m = [], [], [], []
    for l in range(depth):
        lw = _layer_weights(l, norm1_w, norm2_w, w_in, w_alpha2, b_alpha, b_mgate, conv_w,
                            gnorm_a_w, gnorm_b_w)
        xp, xs, ctx = _layer(xp, xs, cond, (w_ada[l], b_ada[l].reshape(1, -1)), cached, lw, l,
                             (w_out[l], w_ff1[l], w_ff2[l]), final_norm_w, l == depth - 1)
        s_gla.append(ctx[0].reshape(Bp, 2, H_A, -1, ctx[0].shape[-1]))
        s_c.append(ctx[1].reshape(Bp, 2, H_B, -1, ctx[1].shape[-1]))
        s_n.append(ctx[2])
        s_m.append(ctx[3])
    dt = x_prompt.dtype
    return (xp, xs, jnp.stack(s_gla, axis=1).astype(dt), jnp.stack(s_c, axis=1).astype(dt),
            jnp.stack(s_n, axis=1).astype(dt), jnp.stack(s_m, axis=1).astype(dt))
```

```python
import functools

import jax
import jax.numpy as jnp
from jax import lax
from jax.experimental import pallas as pl
from jax.experimental.pallas import tpu as pltpu

F32 = jnp.float32
BF16 = jnp.bfloat16

GRID_W = 64
H_A = 4
H_B = 4
R_ALPHA = 16
TAU_GLA = 16.0
CHUNK = 64
EPS = 1e-6
LANES = 128
COND_ROWS = 8
SMALL_W = LANES
GATE_LANE0 = 2 * R_ALPHA
VMEM_LIMIT = 56 * 1024 * 1024
SCAN_UNROLL = 4
PIPELINE_STARTS = 4
TOKEN_TILE = 512
FF_CHUNK = 512
GLA_FACTORED_DECAY_MAX = 60.0
MOD_SPLIT = 2


def _sigmoid(x):
    return 1.0 / (1.0 + jnp.exp(-x))


def _silu(x):
    return x * _sigmoid(x)


def _log_sigmoid(x):
    return jnp.minimum(x, 0.0) - jnp.log1p(jnp.exp(-jnp.abs(x)))


def _dot(a, b):
    return jnp.dot(a, b, preferred_element_type=F32)


def _dot_nt(a, b):
    return lax.dot_general(a, b, (((1,), (1,)), ((), ())), preferred_element_type=F32)


def _rms(x, w):
    return x * lax.rsqrt(jnp.mean(x * x, axis=-1, keepdims=True) + EPS) * w


def _tri_sum(tri, x):
    hi = x.astype(BF16)
    r1 = x - hi.astype(F32)
    mid = r1.astype(BF16)
    lo = (r1 - mid.astype(F32)).astype(BF16)
    return _dot(tri, hi) + _dot(tri, mid) + _dot(tri, lo)


def _chunk_masks(L):
    row = lax.broadcasted_iota(jnp.int32, (L, L), 0)
    col = lax.broadcasted_iota(jnp.int32, (L, L), 1)
    lower = row >= col
    upper = row <= col
    return lower, upper


def _ada_tile(cc_ref, c_ref, w_ref, b_ref, o_ref):
    D = cc_ref.shape[1]
    sub = lax.broadcasted_iota(jnp.int32, (COND_ROWS, D), 0)
    cond = jnp.where(sub == 0, cc_ref[...], 0.0)
    for r in range(c_ref.shape[0]):
        cond = jnp.where(sub == 1 + r, c_ref[r:r + 1, :], cond)
    o_ref[...] = _dot(_silu(cond).astype(BF16), w_ref[...].astype(BF16)) + b_ref[...]


def _ada_call(cc, c, w_ada, b_ada, n_cols):
    D = cc.shape[1]
    tn = 1024
    return pl.pallas_call(
        _ada_tile,
        grid=(n_cols // tn,),
        in_specs=[
            pl.BlockSpec(cc.shape, lambda j: (0, 0)),
            pl.BlockSpec(c.shape, lambda j: (0, 0)),
            pl.BlockSpec((D, tn), lambda j: (0, j)),
            pl.BlockSpec((1, tn), lambda j: (0, j)),
        ],
        out_specs=pl.BlockSpec((COND_ROWS, tn), lambda j: (0, j)),
        out_shape=jax.ShapeDtypeStruct((COND_ROWS, n_cols), F32),
        compiler_params=pltpu.CompilerParams(dimension_semantics=("arbitrary",),
                                             vmem_limit_bytes=VMEM_LIMIT),
        name="ada_mod",
    )(cc, c, w_ada, b_ada)


def _tile_group(n_ctx, tiles_per_req):
    i = pl.program_id(0)
    is_ctx = i < n_ctx
    row = jnp.where(is_ctx, 0, 1 + jnp.maximum(i - n_ctx, 0) // tiles_per_req)
    return is_ctx, row


def _ctx_tile(n_ctx):
    return lambda i: (jnp.minimum(i, n_ctx - 1), 0)


def _lat_tile(n_ctx):
    return lambda i: (jnp.maximum(i - n_ctx, 0), 0)


def _inproj_kernel(xc_ref, xl_ref, mod_ref, nw_ref, wt_ref, z_ref, wb_scr, *, n_ctx, tiles_per_req, big_rows,
                   small_rows):
    D = xc_ref.shape[1]

    @pl.when(pl.program_id(0) == 0)
    def _():
        col = 0
        for r0, n in big_rows:
            for k in range(n // LANES):
                blk = wt_ref[r0 + k * LANES:r0 + (k + 1) * LANES, :]
                wb_scr[:, col:col + LANES] = blk.T.astype(BF16)
                col += LANES
        parts = [wt_ref[r0:r0 + n, :] for r0, n in small_rows]
        n_small = sum(n for _, n in small_rows)
        parts.append(jnp.zeros((SMALL_W - n_small, D), F32))
        wb_scr[:, col:col + SMALL_W] = jnp.concatenate(parts, axis=0).T.astype(BF16)

    is_ctx, row = _tile_group(n_ctx, tiles_per_req)

    def tile(x_ref):
        sh1 = mod_ref[pl.ds(row, 1), 0:D]
        sc1 = mod_ref[pl.ds(row, 1), D:2 * D]
        h = _rms(x_ref[...], nw_ref[...]) * (1.0 + sc1) + sh1
        z_ref[...] = _dot(h.astype(BF16), wb_scr[...])

    @pl.when(is_ctx)
    def _():
        tile(xc_ref)

    @pl.when(jnp.logical_not(is_ctx))
    def _():
        tile(xl_ref)


def _inproj_call(xc2d, xl2d, mod, norm_w, w_in_t, *, tm, tiles_per_req, big_rows, small_rows):
    (Mc, D), Ml = xc2d.shape, xl2d.shape[0]
    n_ctx = Mc // tm
    n_out = sum(n for _, n in big_rows) + SMALL_W
    kern = functools.partial(_inproj_kernel, n_ctx=n_ctx, tiles_per_req=tiles_per_req,
                             big_rows=big_rows, small_rows=small_rows)
    return pl.pallas_call(
        kern,
        grid=((Mc + Ml) // tm,),
        in_specs=[
            pl.BlockSpec((tm, D), _ctx_tile(n_ctx)),
            pl.BlockSpec((tm, D), _lat_tile(n_ctx)),
            pl.BlockSpec(mod.shape, lambda i: (0, 0)),
            pl.BlockSpec((1, D), lambda i: (0, 0)),
            pl.BlockSpec(w_in_t.shape, lambda i: (0, 0), pipeline_mode=pl.Buffered(1)),
        ],
        out_specs=pl.BlockSpec((tm, n_out), lambda i: (i, 0)),
        out_shape=jax.ShapeDtypeStruct((Mc + Ml, n_out), F32),
        scratch_shapes=[pltpu.VMEM((D, n_out), BF16)],
        compiler_params=pltpu.CompilerParams(dimension_semantics=("arbitrary",),
                                             vmem_limit_bytes=VMEM_LIMIT),
        name="norm_inproj",
    )(xc2d, xl2d, mod, norm_w.reshape(1, D), w_in_t)


def _chunk_loop(n_chunks, unroll, make_units):
    def step(ns):
        pending = list(make_units(ns))
        active = []
        while pending or active:
            for _ in range(min(PIPELINE_STARTS, len(pending))):
                active.append(pending.pop(0))
            alive = []
            for g in active:
                try:
                    next(g)
                    alive.append(g)
                except StopIteration:
                    pass
            active = alive

    if unroll >= n_chunks:
        step(list(range(n_chunks)))
        return

    def body(i, carry):
        step([i * unroll + u for u in range(unroll)])
        return carry

    lax.fori_loop(0, n_chunks // unroll, body, 0)


def _chunk_rows(n):
    if isinstance(n, int):
        return pl.ds(n * CHUNK, CHUNK)
    return pl.ds(pl.multiple_of(n * CHUNK, CHUNK), CHUNK)


def _cast_specs(casts, n_steps):
    in_specs, out_specs, out_shape, args = [], [], [], []
    for w, axis in casts:
        blk = list(w.shape)
        assert blk[axis] % n_steps == 0
        blk[axis] //= n_steps
        assert blk[0] % 16 == 0 and blk[1] % LANES == 0
        idx = (lambda b: (b, 0)) if axis == 0 else (lambda b: (0, b))
        in_specs.append(pl.BlockSpec(tuple(blk), idx))
        out_specs.append(pl.BlockSpec(tuple(blk), idx))
        out_shape.append(jax.ShapeDtypeStruct(w.shape, BF16))
        args.append(w)
    return in_specs, out_specs, out_shape, args


def _gla_body(q_ref, k_ref, v_ref, g_ref, sm_ref, s0_ref, wal_ref, bal_ref, gw_ref, out_ref, snew_ref,
              st_scr, sall_scr, qh_scr, qs_scr, kh_scr):
    has_state = s0_ref is not None
    write_state = snew_ref is not None
    T = q_ref.shape[0]
    L = CHUNK
    N = T // L
    HK = q_ref.shape[1]
    DK = HK // H_A
    DV = v_ref.shape[1] // H_A
    scale = DK ** -0.5
    n_pairs = HK // LANES

    lower, upper = _chunk_masks(L)
    tri = (lower.astype(BF16), upper.astype(BF16))
    tmask = (lower, upper)
    lane = lax.broadcasted_iota(jnp.int32, (1, LANES), 1)
    head_mask = (lane < DK, lane >= DK)

    for d in range(2):
        for p in range(n_pairs):
            if has_state:
                st_scr[d, p] = s0_ref[d, p].T
            else:
                st_scr[d, p] = jnp.zeros((LANES, LANES), F32)

    def decay_pre(d, r):
        return _dot(sm_ref[r, :].astype(BF16), wal_ref[:, d * HK:(d + 1) * HK]) + bal_ref[d:d + 1, :]

    neg_pre = jnp.maximum(-(_dot(sm_ref[...].astype(BF16), wal_ref[...])
                            + jnp.concatenate([bal_ref[0:1, :], bal_ref[1:2, :]], axis=1)), 0.0)
    chunk_sums = jnp.sum(neg_pre.reshape(N, L, 2 * HK), axis=1)
    decay_span = (jnp.max(chunk_sums) + L * 0.6931472) * (1.0 / TAU_GLA)

    def state_group(ns, dirs=(0, 1)):
        units = [(d, n if d == 0 else N - 1 - n) for n in ns for d in dirs]
        rows = [_chunk_rows(n) for _, n in units]
        vt_all = [[jnp.concatenate([v_ref[r, (2 * p + j) * DV:(2 * p + j + 1) * DV] for j in range(2)],
                                   axis=0).T.astype(BF16) for p in range(n_pairs)] for r in rows]
        yield
        pre = [decay_pre(d, r) for (d, _), r in zip(units, rows)]
        yield
        g = [_log_sigmoid(x) * (1.0 / TAU_GLA) for x in pre]
        yield
        b = [_tri_sum(tri[d], gi) for (d, _), gi in zip(units, g)]
        yield
        ks_all, dec_all = [], []
        for (d, _), r, bi in zip(units, rows, b):
            bend = bi[L - 1:L, :] if d == 0 else bi[0:1, :]
            q = q_ref[r, :] * scale
            ks = (k_ref[r, :] * jnp.exp(bend - bi)).astype(BF16)
            qh_scr[d, r, :] = (q * jnp.exp(bi - bend)).astype(BF16)
            qs_scr[d, r, :] = (q * jnp.exp(bi)).astype(BF16)
            kh_scr[d, r, :] = ks
            ks_all.append(ks)
            dec_all.append(jnp.exp(bend))
        yield
        upd_all = []
        for vt_u, ks in zip(vt_all, ks_all):
            upd_u = []
            for p in range(n_pairs):
                kp = ks[:, p * LANES:(p + 1) * LANES]
                kk = jnp.concatenate([jnp.where(head_mask[j], kp, jnp.zeros_like(kp)) for j in range(2)], axis=0)
                upd_u.append(_dot(vt_u[p], kk))
            upd_all.append(upd_u)
        yield
        st = {d: [st_scr[d, p] for p in range(n_pairs)] for d in dirs}
        for (d, n), dec, upd in zip(units, dec_all, upd_all):
            for p in range(n_pairs):
                sall_scr[d, n, p] = st[d][p].astype(BF16)
                st[d][p] = st[d][p] * dec[:, p * LANES:(p + 1) * LANES] + upd[p]
        for d in dirs:
            for p in range(n_pairs):
                st_scr[d, p] = st[d][p]

    def stack_heads(x):
        return jnp.concatenate([jnp.where(head_mask[j], x, jnp.zeros_like(x)) for j in range(2)], axis=0)

    tok = lax.broadcasted_iota(jnp.int32, (L, 1), 0)
    row_t = lax.broadcasted_iota(jnp.int32, (2 * L, L), 0) & (L - 1)
    col_s = lax.broadcasted_iota(jnp.int32, (2 * L, L), 1)

    def exact_scores(d, r, p):
        ls = slice(p * LANES, (p + 1) * LANES)
        b = _tri_sum(tri[d], _log_sigmoid(decay_pre(d, r)[:, ls]) * (1.0 / TAU_GLA))
        q = q_ref[r, ls] * scale
        k = k_ref[r, ls]
        acc = jnp.where(row_t == col_s, _dot_nt(stack_heads(q).astype(BF16), k.astype(BF16)), 0.0)
        src = lax.broadcasted_iota(jnp.int32, (L, L), 1)
        h = L // 2
        while h >= 1:
            first = tok & ~(2 * h - 1)
            edge = first + (h - 1 if d == 0 else h)
            b_edge = _tri_sum((src == edge).astype(BF16), b)
            upper = (tok & (2 * h - 1)) >= h
            later, earlier = (upper, ~upper) if d == 0 else (~upper, upper)
            qt = jnp.where(later, q * jnp.exp(b - b_edge), 0.0)
            kt = jnp.where(earlier, k * jnp.exp(b_edge - b), 0.0)
            sc = _dot_nt(stack_heads(qt).astype(BF16), kt.astype(BF16))
            acc = acc + jnp.where((row_t & ~(2 * h - 1)) == (col_s & ~(2 * h - 1)), sc, 0.0)
            h //= 2
        return acc

    def out_group(ns, exact_decay=False):
        pairs = [(d, ni, p) for ni in range(len(ns)) for d in range(2) for p in range(n_pairs)]
        scores, inter = [], []
        for d, ni, p in pairs:
            r = _chunk_rows(ns[ni])
            ls = slice(p * LANES, (p + 1) * LANES)
            if exact_decay:
                scores.append(exact_scores(d, r, p))
            else:
                scores.append(_dot_nt(stack_heads(qh_scr[d, r, ls]), kh_scr[d, r, ls]))
            inter.append(_dot_nt(stack_heads(qs_scr[d, r, ls]), sall_scr[d, ns[ni], p]))
        yield
        probs = [[jnp.where(tmask[d], sc[j * L:(j + 1) * L, :], 0.0).astype(BF16) for j in range(2)]
                 for (d, _, _), sc in zip(pairs, scores)]
        yield
        outs = {}
        for (d, ni, p), pr, it in zip(pairs, probs, inter):
            r = _chunk_rows(ns[ni])
            for j in range(2):
                vs = slice((2 * p + j) * DV, (2 * p + j + 1) * DV)
                outs[(d, ni, 2 * p + j)] = _dot(pr[j], v_ref[r, vs].astype(BF16)) + it[j * L:(j + 1) * L, :]
        yield
        for ni, n in enumerate(ns):
            r = _chunk_rows(n)
            for h in range(H_A):
                vs = slice(h * DV, (h + 1) * DV)
                o = outs[(0, ni, h)] + outs[(1, ni, h)]
                out_ref[r, vs] = (_rms(o, gw_ref[:, vs]) * _silu(g_ref[r, vs])).astype(out_ref.dtype)

    def finish():
        if write_state:
            for d in range(2):
                for p in range(n_pairs):
                    snew_ref[d, p] = st_scr[d, p].T

    return state_group, out_group, finish, decay_span


def _gla_scratch(T, HK):
    n_pairs = HK // LANES
    n_chunks = T // CHUNK
    return [
        pltpu.VMEM((2, n_pairs, LANES, LANES), F32),
        pltpu.VMEM((2, n_chunks, n_pairs, LANES, LANES), BF16),
        pltpu.VMEM((2, T, HK), BF16),
        pltpu.VMEM((2, T, HK), BF16),
        pltpu.VMEM((2, T, HK), BF16),
    ]


def _mlstm_body(qk_ref, v_ref, og_ref, sm_ref, c0_ref, n0_ref, m0_ref, cw_ref, bm_ref, gw_ref,
                out_ref, cnew_ref, nnew_ref, mnew_ref,
                pad_scr, qk_scr, y_scr, c_scr, n_scr, m_scr, call_scr, nall_scr, mall_scr, g_scr, f_scr,
                *, grid_w):
    has_state = c0_ref is not None
    write_state = cnew_ref is not None
    T = qk_ref.shape[0]
    L = CHUNK
    N = T // L
    C2 = qk_ref.shape[1]
    HK = C2 // 2
    DK = HK // H_B
    DV = v_ref.shape[1] // H_B
    scale = DK ** -0.5
    n_pairs = HK // LANES
    P = pad_scr.shape[0] - T
    P0 = P // 2
    rows_img = T // grid_w

    lower, upper = _chunk_masks(L)
    tri = (lower.astype(BF16), upper.astype(BF16))
    tmask = (lower, upper)
    lane = lax.broadcasted_iota(jnp.int32, (1, LANES), 1)
    head_mask = (lane < DK, lane >= DK)
    lane_in = lane & (L - 1)

    def lane_cummax(x, d):
        k = 1
        while k < L:
            if d == 0:
                x = jnp.maximum(x, jnp.where(lane_in >= k, pltpu.roll(x, k, axis=1), -jnp.inf))
            else:
                x = jnp.maximum(x, jnp.where(lane_in < L - k, pltpu.roll(x, LANES - k, axis=1), -jnp.inf))
            k *= 2
        return x

    for d in range(2):
        for p in range(n_pairs):
            if has_state:
                c_scr[d, p] = c0_ref[d, p]
                n_scr[2 * d + p:2 * d + p + 1, :] = jnp.concatenate(
                    [n0_ref[d, 2 * p + j:2 * p + j + 1, :] for j in range(2)], axis=1)
            else:
                c_scr[d, p] = jnp.zeros((LANES, LANES), F32)
                n_scr[2 * d + p:2 * d + p + 1, :] = jnp.zeros((1, LANES), F32)
    eye_h = (lax.broadcasted_iota(jnp.int32, (H_B, H_B), 0) == lax.broadcasted_iota(jnp.int32, (H_B, H_B), 1))

    def to_col(row):
        return jnp.sum(jnp.where(eye_h, row, 0.0), axis=1, keepdims=True)

    def to_row(col):
        return jnp.sum(jnp.where(eye_h, col, 0.0), axis=0, keepdims=True)

    for d in range(2):
        if has_state:
            m_scr[H_B * d:H_B * (d + 1), 0:1] = to_col(m0_ref[d:d + 1, :])
        else:
            m_scr[H_B * d:H_B * (d + 1), 0:1] = jnp.zeros((H_B, 1), F32)

    pad_scr[0:P0, :] = jnp.zeros((P0, C2), F32)
    pad_scr[P0 + T:P + T, :] = jnp.zeros((P - P0, C2), F32)

    def copy_in(i, carry):
        r0 = pl.multiple_of(i * L, L)
        pad_scr[pl.ds(P0 + r0, L), :] = qk_ref[pl.ds(r0, L), :]
        return carry

    lax.fori_loop(0, N, copy_in, 0)

    lane_c = lax.broadcasted_iota(jnp.int32, (1, C2), 1)
    qscale = jnp.where(lane_c < HK, scale, 1.0).astype(F32)
    sub = lax.broadcasted_iota(jnp.int32, (L, 1), 0)
    img_rows = (0,) if rows_img == 1 else (-1, 0, 1)

    def conv_tile(i, carry):
        r0 = pl.multiple_of(i * L, L)
        col = lax.rem(r0, grid_w) + sub
        ok_left = col >= 1
        ok_right = col <= grid_w - 2
        sums = [None, None, None]
        for di in img_rows:
            blk = pad_scr[pl.ds(P0 + r0 + di * grid_w - 8, L + 16), :]
            for k in range(3):
                term = blk * cw_ref[di + 1, k:k + 1, :]
                sums[k] = term if sums[k] is None else sums[k] + term
        acc = (sums[1][8:8 + L, :] + jnp.where(ok_left, sums[0][7:7 + L, :], 0.0)
               + jnp.where(ok_right, sums[2][9:9 + L, :], 0.0))
        qk_scr[pl.ds(r0, L), :] = _silu(acc) * qscale
        return carry

    lax.fori_loop(0, N, conv_tile, 0)

    gl = lane - GATE_LANE0
    is_f = ((gl >= H_B) & (gl < 2 * H_B)) | ((gl >= 3 * H_B) & (gl < 4 * H_B))

    def gate_tile(i, carry):
        rows = pl.ds(pl.multiple_of(i * L, L), L)
        x = sm_ref[rows, :] + bm_ref[...]
        y_scr[rows, :] = jnp.where(is_f, _log_sigmoid(x), x)
        return carry

    lax.fori_loop(0, N, gate_tile, 0)


    def state_group(ns, dirs=(0, 1)):
        units = [(d, n if d == 0 else N - 1 - n) for n in ns for d in dirs]
        rows = [_chunk_rows(n) for _, n in units]
        kt_all = [[qk_scr[r, HK + p * LANES:HK + (p + 1) * LANES].T for p in range(n_pairs)] for r in rows]
        yield
        xs = [y_scr[r, :] for r in rows]
        fsum = [_tri_sum(tri[d], x) for (d, _), x in zip(units, xs)]
        yield
        wk_all, f_end, c_end = [], [], []
        for (d, n), r, x, fs in zip(units, rows, xs, fsum):
            y = jnp.where(is_f, fs, x)
            li0 = GATE_LANE0 + 2 * H_B * d
            blk = jnp.concatenate([y, y], axis=0).T[li0:li0 + 2 * H_B, :]
            frow = pltpu.roll(blk, H_B, axis=0)
            grow = blk - frow
            g_scr[d, n] = grow
            f_scr[d, n] = frow
            e_col = L - 1 if d == 0 else 0
            f_end.append(frow[0:H_B, e_col:e_col + 1])
            ce8 = jnp.max(grow, axis=1, keepdims=True)
            c_end.append(ce8[0:H_B, :])
            wk_all.append(jnp.exp(grow[:, 0:L] - ce8))
        yield
        kv_all, ksum_all = [], []
        for r, wk8, kt_u in zip(rows, wk_all, kt_all):
            kv_u, ks_u = [], []
            wk8b = wk8.astype(BF16)
            for p in range(n_pairs):
                kpb = qk_scr[r, HK + p * LANES:HK + (p + 1) * LANES].astype(BF16)
                ks8 = _dot(wk8b, kpb)
                for j in range(2):
                    h = 2 * p + j
                    kwt = (kt_u[p][j * DK:(j + 1) * DK, :] * wk8[h:h + 1, :]).astype(BF16)
                    kv_u.append(_dot(kwt, v_ref[r, h * DV:(h + 1) * DV].astype(BF16)))
                    ks_u.append(ks8[h:h + 1, :])
            kv_all.append(kv_u)
            ksum_all.append(ks_u)
        yield
        m_run = {d: m_scr[H_B * d:H_B * (d + 1), 0:1] for d in dirs}
        a_all, b_all = [], []
        for (d, n), fe, ce in zip(units, f_end, c_end):
            mall_scr[d, n, 0:H_B, 0:1] = m_run[d]
            mx = jnp.maximum(m_run[d], ce)
            a_all.append(jnp.exp(m_run[d] - mx))
            b_all.append(jnp.exp(ce - mx))
            m_run[d] = fe + mx
        for d in dirs:
            m_scr[H_B * d:H_B * (d + 1), 0:1] = m_run[d]
        yield
        c_run = {d: [[c_scr[d, p, j * DK:(j + 1) * DK, :] for j in range(2)] for p in range(n_pairs)] for d in dirs}
        n_run = {d: [n_scr[2 * d + p:2 * d + p + 1, :] for p in range(n_pairs)] for d in dirs}
        for (d, n), a4, b4, kv_u, ks_u in zip(units, a_all, b_all, kv_all, ksum_all):
            for p in range(n_pairs):
                nall_scr[d, n, p:p + 1, :] = n_run[d][p]
                a_s = [a4[2 * p + j:2 * p + j + 1, :] for j in range(2)]
                b_s = [b4[2 * p + j:2 * p + j + 1, :] for j in range(2)]
                for j in range(2):
                    cj = c_run[d][p][j]
                    call_scr[d, n, p, j * DK:(j + 1) * DK, :] = cj.astype(BF16)
                    c_run[d][p][j] = a_s[j] * cj + b_s[j] * kv_u[2 * p + j]
                n_run[d][p] = (jnp.where(head_mask[0], a_s[0], a_s[1]) * n_run[d][p]
                               + jnp.where(head_mask[0], b_s[0] * ks_u[2 * p], b_s[1] * ks_u[2 * p + 1]))
        for d in dirs:
            for p in range(n_pairs):
                n_scr[2 * d + p:2 * d + p + 1, :] = n_run[d][p]
                for j in range(2):
                    c_scr[d, p, j * DK:(j + 1) * DK, :] = c_run[d][p][j]

    eye = lower & upper
    ones8 = jnp.ones((8, L), BF16)
    sub8 = lax.broadcasted_iota(jnp.int32, (8, LANES), 0)
    sub_h = lax.broadcasted_iota(jnp.int32, (H_B, L), 0)
    n_rows = [((sub8 == 2 * p) & head_mask[0]) | ((sub8 == 2 * p + 1) & head_mask[1]) for p in range(n_pairs)]

    def head_rows(vals):
        out = vals[0][0:H_B, :]
        for h in range(1, H_B):
            out = jnp.where(sub_h == h, vals[h][0:H_B, :], out)
        return out

    def out_group(ns):
        chunks = [(d, n) for n in ns for d in range(2)]
        pairs = [(d, n, p) for d, n in chunks for p in range(n_pairs)]
        units = [(d, n, p, j) for d, n, p in pairs for j in range(2)]
        cms = [lane_cummax(g_scr[d, n], d)[0:H_B, 0:L] for d, n in chunks]
        qk2s, qc2s, qn2s = [], [], []
        for d, n, p in pairs:
            r = _chunk_rows(n)
            qp = qk_scr[r, p * LANES:(p + 1) * LANES]
            q2 = jnp.concatenate([jnp.where(head_mask[j], qp, 0.0) for j in range(2)], axis=0).astype(BF16)
            qk2s.append(_dot_nt(q2, qk_scr[r, HK + p * LANES:HK + (p + 1) * LANES].astype(BF16)))
            qc2s.append(_dot(q2, call_scr[d, n, p]))
            nsel = jnp.where(n_rows[p], nall_scr[d, n, p:p + 1, :], 0.0).astype(BF16)
            qn2s.append(_dot_nt(nsel, qp.astype(BF16)))
        yield
        s_all = []
        for ui, (d, n, p, j) in enumerate(units):
            grow = g_scr[d, n, 2 * p + j:2 * p + j + 1, 0:L]
            e = jnp.where(tmask[d], grow, -jnp.inf)
            cmax = jnp.max(e, axis=-1, keepdims=True)
            s_all.append((qk2s[ui // 2][j * L:(j + 1) * L, :] * jnp.exp(e - cmax)).astype(BF16))
        yield
        nums =[_dot(s, v_ref[_chunk_rows(n), (2 * p + j) * DV:(2 * p + j + 1) * DV].astype(BF16))
                for (d, n, p, j), s in zip(units, s_all)]
        dens = [_dot_nt(ones8, s) for s in s_all]
        yield
        scales = []
        for ci, (d, n) in enumerate(chunks):
            den_loc = head_rows(dens[ci * H_B:(ci + 1) * H_B])
            qn = qn2s[ci * n_pairs][0:H_B, :]
            for p in range(1, n_pairs):
                qn = qn + qn2s[ci * n_pairs + p][0:H_B, :]
            cm = cms[ci]
            m_prev = mall_scr[d, n, 0:H_B, 0:1]
            delta = cm - m_prev
            t = jnp.exp(-jnp.abs(delta))
            w_loc = jnp.where(delta <= 0.0, t, 1.0)
            w_inter = jnp.where(delta <= 0.0, 1.0, t)
            mt = f_scr[d, n, 0:H_B, 0:L] + jnp.maximum(m_prev, cm)
            den = w_loc * den_loc + w_inter * qn
            rinv = 1.0 / jnp.maximum(jnp.abs(den), jnp.exp(-mt))
            scales.append((w_loc * rinv, w_inter * rinv))
        yield
        hs = []
        for ui, (d, n, p, j) in enumerate(units):
            h = 2 * p + j
            sc_loc, sc_inter = scales[ui // H_B]
            d_loc = jnp.where(eye, sc_loc[h:h + 1, :], 0.0).astype(BF16)
            d_inter = jnp.where(eye, sc_inter[h:h + 1, :], 0.0).astype(BF16)
            hs.append(_dot(d_loc, nums[ui].astype(BF16))
                      + _dot(d_inter, qc2s[ui // 2][j * L:(j + 1) * L, :].astype(BF16)))
        yield
        for ni, n in enumerate(ns):
            r = _chunk_rows(n)
            for h in range(H_B):
                vs = slice(h * DV, (h + 1) * DV)
                o = hs[(2 * ni) * H_B + h] + hs[(2 * ni + 1) * H_B + h]
                out_ref[r, vs] = (_rms(o, gw_ref[:, vs]) * _sigmoid(og_ref[r, vs])).astype(out_ref.dtype)

    def finish():
        if write_state:
            for d in range(2):
                for p in range(n_pairs):
                    cnew_ref[d, p] = c_scr[d, p]
                    for j in range(2):
                        nnew_ref[d, 2 * p + j:2 * p + j + 1, :] = n_scr[2 * d + p:2 * d + p + 1, j * DK:(j + 1) * DK]
                mnew_ref[d:d + 1, :] = to_row(m_scr[H_B * d:H_B * (d + 1), 0:1])

    return state_group, out_group, finish


def _mlstm_scratch(T, C2, grid_w):
    n_pairs = C2 // 2 // LANES
    n_chunks = T // CHUNK
    pad_rows = 2 * (grid_w + 8) if T // grid_w > 1 else 16
    return [
        pltpu.VMEM((T + pad_rows, C2), F32),
        pltpu.VMEM((T, C2), F32),
        pltpu.VMEM((T, SMALL_W), F32),
        pltpu.VMEM((2, n_pairs, LANES, LANES), F32),
        pltpu.VMEM((8, LANES), F32),
        pltpu.VMEM((8, LANES), F32),
        pltpu.VMEM((2, n_chunks, n_pairs, LANES, LANES), BF16),
        pltpu.VMEM((2, n_chunks, 8, LANES), F32),
        pltpu.VMEM((2, n_chunks, 8, LANES), F32),
        pltpu.VMEM((2, n_chunks, 8, LANES), F32),
        pltpu.VMEM((2, n_chunks, 8, LANES), F32),
    ]


N_GLA_SCRATCH = 5
N_MLSTM_SCRATCH = 11


def _scan_kernel(*refs, cols, layer, has_state, write_state, n_cast, ride_ada, grid_w, unroll):
    refs = list(refs)
    z_ref = refs.pop(0)
    s0_ref = c0_ref = n0_ref = m0_ref = None
    if has_state:
        s0_ref, c0_ref, n0_ref, m0_ref = refs[:4]
        del refs[:4]
    wa_ref, bal_ref, gwa_ref, cw_ref, bmg_ref, gwb_ref = refs[:6]
    del refs[:6]
    cast_in = refs[:n_cast]
    del refs[:n_cast]
    if ride_ada:
        ada_in = refs[:4]
        del refs[:4]
    outa_ref, outb_ref = refs[:2]
    del refs[:2]
    snew_ref = cnew_ref = nnew_ref = mnew_ref = None
    if write_state:
        snew_ref, cnew_ref, nnew_ref, mnew_ref = refs[:4]
        del refs[:4]
    cast_out = refs[:n_cast]
    del refs[:n_cast]
    if ride_ada:
        ada_out = refs.pop(0)
    wal_scr, bm_scr = refs[:2]
    del refs[:2]
    gla_scr = refs[:N_GLA_SCRATCH]
    mlstm_scr = refs[N_GLA_SCRATCH:]

    for src, dst in zip(cast_in, cast_out):
        dst[...] = src[...].astype(BF16)
    if ride_ada:
        _ada_tile(*ada_in, ada_out)

    R, HK = wa_ref.shape[1], wa_ref.shape[2]
    wal_scr[...] = jnp.zeros(wal_scr.shape, BF16)
    for d in range(2):
        wal_scr[d * R:(d + 1) * R, d * HK:(d + 1) * HK] = wa_ref[d].astype(BF16)
    lane = lax.broadcasted_iota(jnp.int32, (1, LANES), 1)
    bm = jnp.zeros((1, LANES), F32)
    for g in range(bmg_ref.shape[1]):
        for h in range(H_B):
            bm = jnp.where(lane == GATE_LANE0 + H_B * g + h, bmg_ref[layer, g, h], bm)
    bm_scr[0:1, :] = bm

    def view(name):
        c0, w = cols[name]
        return z_ref.at[:, pl.ds(c0, w)]

    sm_ref = view("small")
    n_chunks = z_ref.shape[0] // CHUNK
    gla = _gla_body(view("qa"), view("ka"), view("va"), view("ga"), sm_ref, s0_ref, wal_scr, bal_ref, gwa_ref,
                    outa_ref, snew_ref, *gla_scr)
    mlstm = _mlstm_body(view("qkb"), view("vb"), view("ob"), sm_ref, c0_ref, n0_ref, m0_ref, cw_ref,
                        bm_scr.at[0:1, :], gwb_ref, outb_ref, cnew_ref, nnew_ref, mnew_ref, *mlstm_scr,
                        grid_w=grid_w)
    gla_state, gla_out, gla_finish, decay_span = gla
    mlstm_state, mlstm_out, mlstm_finish = mlstm

    def passes(gla_out_fn):
        _chunk_loop(n_chunks, unroll, lambda ns: [fn([n], (d,)) for n in ns for d in range(2)
                                                  for fn in (mlstm_state, gla_state)])
        _chunk_loop(n_chunks, unroll, lambda ns: [fn([n]) for n in ns for fn in (mlstm_out, gla_out_fn)])

    wide_decay = decay_span > GLA_FACTORED_DECAY_MAX

    @pl.when(jnp.logical_not(wide_decay))
    def _():
        passes(gla_out)

    @pl.when(wide_decay)
    def _():
        passes(functools.partial(gla_out, exact_decay=True))

    gla_finish()
    mlstm_finish()


def _scan_call(z2d, row0, B, T, states, lw, layer, *, grid_w, write_state, casts=(), ada=None):
    n_z = z2d.shape[1]
    assert row0 % T == 0 and z2d.shape[0] % T == 0
    z3 = z2d.reshape(z2d.shape[0] // T, T, n_z)
    blk0 = row0 // T
    HK = lw["w_alpha2"].shape[-1]
    DA = lw["gnorm_a_w"].shape[0]
    C2 = lw["conv_w"].shape[-1]
    DB = lw["gnorm_b_w"].shape[0]
    DK_A, DK_B = HK // H_A, C2 // 2 // H_B
    pa, pb = HK // LANES, C2 // 2 // LANES
    n_chunks = T // CHUNK
    has_state = states is not None
    widths = (("qa", HK), ("ka", HK), ("va", DA), ("ga", DA), ("qkb", C2), ("vb", DB), ("ob", DB),
              ("small", SMALL_W))
    cols, c0 = {}, 0
    for name, w in widths:
        cols[name] = (c0, w)
        c0 += w
    assert c0 == n_z
    cast_in_specs, cast_out_specs, cast_out_shape, cast_args = _cast_specs(casts, B)
    kern = functools.partial(_scan_kernel, cols=cols, layer=layer, has_state=has_state, write_state=write_state,
                             n_cast=len(casts), ride_ada=ada is not None, grid_w=grid_w,
                             unroll=min(n_chunks, SCAN_UNROLL))

    def per_batch(shape):
        nd = len(shape)
        return pl.BlockSpec((None,) + tuple(shape), lambda b: (b,) + (0,) * nd)

    def per_batch_layer(shape):
        nd = len(shape)
        return pl.BlockSpec((None, None) + tuple(shape), lambda b: (b, layer) + (0,) * nd)

    def of_layer(a):
        return pl.BlockSpec((None,) + a.shape[1:], lambda b: (layer,) + (0,) * (a.ndim - 1))

    def whole(a):
        return pl.BlockSpec(a.shape, lambda b: (0,) * a.ndim)

    state_shapes = ((2, pa, LANES, LANES), (2, pb, LANES, LANES), (2, H_B, DK_B), (2, H_B))
    in_specs = [pl.BlockSpec((None, T, n_z), lambda b: (b + blk0, 0, 0))]
    args = [z3]
    if has_state:
        s_gla, s_c, s_n, s_m = states
        depth = s_gla.shape[1]
        args += [s_gla.reshape((B, depth) + state_shapes[0]), s_c.reshape((B, depth) + state_shapes[1]), s_n, s_m]
        in_specs += [per_batch_layer(s) for s in state_shapes]
    args += [lw["w_alpha2"], lw["b_alpha"], lw["gnorm_a_w"].reshape(1, DA), lw["conv_w"], lw["b_mgate"],
             lw["gnorm_b_w"].reshape(1, DB)]
    in_specs += [of_layer(lw["w_alpha2"]), of_layer(lw["b_alpha"]), pl.BlockSpec((1, DA), lambda b: (0, 0)),
                 whole(lw["conv_w"]), pl.BlockSpec(memory_space=pltpu.SMEM), pl.BlockSpec((1, DB), lambda b: (0, 0))]
    args += cast_args
    in_specs += cast_in_specs
    out_specs = [per_batch((T, DA)), per_batch((T, DB))]
    out_shape = [jax.ShapeDtypeStruct((B, T, DA), BF16), jax.ShapeDtypeStruct((B, T, DB), BF16)]
    if write_state:
        out_specs += [per_batch(s) for s in state_shapes]
        out_shape += [jax.ShapeDtypeStruct((B,) + s, F32) for s in state_shapes]
    out_specs += cast_out_specs
    out_shape += cast_out_shape
    if ada is not None:
        cc, c, w_ada, b_ada, col0 = ada
        n_rest = w_ada.shape[1] - col0
        wcol = n_rest // B
        assert n_rest % B == 0 and wcol % LANES == 0 and col0 % wcol == 0
        args += [cc, c, w_ada, b_ada]
        in_specs += [whole(cc), whole(c),
                     pl.BlockSpec((w_ada.shape[0], wcol), lambda b: (0, col0 // wcol + b)),
                     pl.BlockSpec((1, wcol), lambda b: (0, col0 // wcol + b))]
        out_specs.append(pl.BlockSpec((COND_ROWS, wcol), lambda b: (0, b)))
        out_shape.append(jax.ShapeDtypeStruct((COND_ROWS, n_rest), F32))
    scratch = ([pltpu.VMEM((SMALL_W, 2 * HK), BF16), pltpu.VMEM((8, LANES), F32)]
               + _gla_scratch(T, HK) + _mlstm_scratch(T, C2, grid_w))
    assert len(scratch) == 2 + N_GLA_SCRATCH + N_MLSTM_SCRATCH
    return pl.pallas_call(
        kern,
        grid=(B,),
        in_specs=in_specs,
        out_specs=out_specs,
        out_shape=out_shape,
        scratch_shapes=scratch,
        compiler_params=pltpu.CompilerParams(dimension_semantics=("arbitrary",),
                                             vmem_limit_bytes=VMEM_LIMIT),
        name="mixer_scans",
    )(*args)


def _outff_kernel(xc_ref, xl_ref, ac_ref, al_ref, bc_ref, bl_ref, mod_ref, n2_ref, fn_ref, wo_ref, w1_ref, w2_ref,
                  yc_ref, yl_ref, *, n_ctx, tiles_per_req, ff_chunk, final_norm):
    D = xc_ref.shape[1]
    DA = ac_ref.shape[1]
    is_ctx, row = _tile_group(n_ctx, tiles_per_req)

    def mod(k):
        return mod_ref[pl.ds(row, 1), (k - MOD_SPLIT) * D:(k - MOD_SPLIT + 1) * D]

    def tile(x_ref, a_ref, b_ref, y_ref):
        y = _dot(a_ref[...], wo_ref[0:DA, :]) + _dot(b_ref[...], wo_ref[DA:, :])
        x1 = x_ref[...] + mod(2) * y
        h2 = (_rms(x1, n2_ref[...]) * (1.0 + mod(4)) + mod(3)).astype(BF16)
        acc = jnp.zeros(x1.shape, F32)
        for c0 in range(0, w1_ref.shape[1], ff_chunk):
            u = jnp.maximum(_dot(h2, w1_ref[:, c0:c0 + ff_chunk]), 0.0)
            acc = acc + _dot((u * u).astype(BF16), w2_ref[c0:c0 + ff_chunk, :])
        x2 = x1 + mod(5) * acc
        y_ref[...] = _rms(x2, fn_ref[...]) if final_norm else x2

    @pl.when(is_ctx)
    def _():
        tile(xc_ref, ac_ref, bc_ref, yc_ref)

    @pl.when(jnp.logical_not(is_ctx))
    def _():
        tile(xl_ref, al_ref, bl_ref, yl_ref)


def _outff_call(xc2d, xl2d, ac, al, bc, bl, mod, norm2_w, final_w, wo, w1, w2, *, tm, tiles_per_req, final_norm):
    (Mc, D), Ml = xc2d.shape, xl2d.shape[0]
    n_ctx = Mc // tm
    DA = ac.shape[1]
    DFF = w1.shape[1]
    kern = functools.partial(_outff_kernel, n_ctx=n_ctx, tiles_per_req=tiles_per_req, ff_chunk=FF_CHUNK,
                             final_norm=final_norm)
    once = pl.Buffered(1)
    ctx, lat = _ctx_tile(n_ctx), _lat_tile(n_ctx)
    return pl.pallas_call(
        kern,
        grid=((Mc + Ml) // tm,),
        in_specs=[
            pl.BlockSpec((tm, D), ctx), pl.BlockSpec((tm, D), lat),
            pl.BlockSpec((tm, DA), ctx), pl.BlockSpec((tm, DA), lat),
            pl.BlockSpec((tm, D - DA), ctx), pl.BlockSpec((tm, D - DA), lat),
            pl.BlockSpec(mod.shape, lambda i: (0, 0)),
            pl.BlockSpec((1, D), lambda i: (0, 0)),
            pl.BlockSpec((1, D), lambda i: (0, 0)),
            pl.BlockSpec((D, D), lambda i: (0, 0), pipeline_mode=once),
            pl.BlockSpec((D, DFF), lambda i: (0, 0), pipeline_mode=once),
            pl.BlockSpec((DFF, D), lambda i: (0, 0), pipeline_mode=once),
        ],
        out_specs=[pl.BlockSpec((tm, D), ctx), pl.BlockSpec((tm, D), lat)],
        out_shape=[jax.ShapeDtypeStruct((Mc, D), F32), jax.ShapeDtypeStruct((Ml, D), F32)],
        compiler_params=pltpu.CompilerParams(dimension_semantics=("arbitrary",),
                                             vmem_limit_bytes=VMEM_LIMIT),
        name="outproj_mlp",
    )(xc2d, xl2d, ac, al, bc, bl, mod, norm2_w.reshape(1, D), final_w.reshape(1, D), wo, w1, w2)


def _layer(xc, xl, cond, ada_w, cached, lw, layer, ffw, final_w, final_norm):
    (Bc, Tc, D), (Bl, Tl, _) = xc.shape, xl.shape
    tm = TOKEN_TILE
    assert (Bc * Tc) % tm == 0 and Tl % tm == 0 and (Bc * Tc) % Tl == 0
    xc2d, xl2d = xc.reshape(Bc * Tc, D), xl.reshape(Bl * Tl, D)
    mod_in = _ada_call(*cond, *ada_w, MOD_SPLIT * D)
    z = _inproj_call(xc2d, xl2d, mod_in, lw["norm1_w"], lw["w_in_t"], tm=tm, tiles_per_req=Tl // tm,
                     big_rows=lw["big_rows"], small_rows=lw["small_rows"])
    res_c = _scan_call(z, 0, Bc, Tc, None, lw, layer, grid_w=Tc, write_state=True,
                       casts=((ffw[0], 0), (ffw[1], 1), (ffw[2], 0)), ada=(*cond, *ada_w, MOD_SPLIT * D))
    res_l = _scan_call(z, Bc * Tc, Bl, Tl, cached, lw, layer, grid_w=GRID_W, write_state=False)
    wo_b, w1_b, w2_b, mod_out = res_c[-4:]
    yc, yl = _outff_call(xc2d, xl2d, res_c[0].reshape(Bc * Tc, -1), res_l[0].reshape(Bl * Tl, -1),
                         res_c[1].reshape(Bc * Tc, -1), res_l[1].reshape(Bl * Tl, -1), mod_out, lw["norm2_w"],
                         final_w, wo_b, w1_b, w2_b, tm=tm, tiles_per_req=Tl // tm, final_norm=final_norm)
    return yc.reshape(Bc, Tc, D), yl.reshape(Bl, Tl, D), tuple(res_c[2:6])


def _layer_weights(l, norm1_w, norm2_w, w_in, w_alpha2, b_alpha, b_mgate, conv_w, gnorm_a_w, gnorm_b_w):
    hk_a = w_alpha2.shape[-1]
    d_a = gnorm_a_w.shape[-1]
    d_b = gnorm_b_w.shape[-1]
    hk_b = conv_w.shape[-1] // 2
    sizes = (hk_a, hk_a, d_a, d_a, 2 * R_ALPHA, hk_b, hk_b, d_b, d_b, 4 * H_B)
    assert w_alpha2.shape[2] == R_ALPHA and b_mgate.shape[1] * b_mgate.shape[2] == 4 * H_B
    offs = [0]
    for s in sizes:
        offs.append(offs[-1] + s)
    big_rows = ((offs[0], offs[4] - offs[0]), (offs[5], offs[9] - offs[5]))
    small_rows = ((offs[4], offs[5] - offs[4]), (offs[9], offs[10] - offs[9]))
    assert all(n % LANES == 0 and r % 16 == 0 for r, n in big_rows)
    return dict(
        norm1_w=norm1_w[l], norm2_w=norm2_w[l], w_in_t=jnp.swapaxes(w_in[l], 0, 1),
        big_rows=big_rows, small_rows=small_rows,
        w_alpha2=w_alpha2, b_alpha=b_alpha, b_mgate=b_mgate, conv_w=conv_w[l],
        gnorm_a_w=gnorm_a_w[l], gnorm_b_w=gnorm_b_w[l],
    )


def kernel(x_prompt, x_sample, c, state_gla, state_mlstm_C, state_mlstm_n, state_mlstm_m, c_ctx, w_ada, b_ada, norm1_w, norm2_w, w_in, w_alpha2, b_alpha, b_mgate, conv_w, gnorm_a_w, gnorm_b_w, w_out, w_ff1, w_ff2, final_norm_w):
    depth = w_in.shape[0]
    D = x_prompt.shape[-1]
    Bp, Tp, _ = x_prompt.shape
    Bs = x_sample.shape[0]
    assert 1 + Bs <= COND_ROWS
    cond = (c_ctx.reshape(1, D), c)
    cached = (state_gla, state_mlstm_C, state_mlstm_n, state_mlstm_m)
    xp, xs = x_prompt, x_sample
    s_gla, s_c, s_n, s_m = [], [], [], []
    for l in range(depth):
        lw = _layer_weights(l, norm1_w, norm2_w, w_in, w_alpha2, b_alpha, b_mgate, conv_w,
                            gnorm_a_w, gnorm_b_w)
        xp, xs, ctx = _layer(xp, xs, cond, (w_ada[l], b_ada[l].reshape(1, -1)), cached, lw, l,
                             (w_out[l], w_ff1[l], w_ff2[l]), final_norm_w, l == depth - 1)
        s_gla.append(ctx[0].reshape(Bp, 2, H_A, -1, ctx[0].shape[-1]))
        s_c.append(ctx[1].reshape(Bp, 2, H_B, -1, ctx[1].shape[-1]))
        s_n.append(ctx[2])
        s_m.append(ctx[3])
    dt = x_prompt.dtype
    return (xp, xs, jnp.stack(s_gla, axis=1).astype(dt), jnp.stack(s_c, axis=1).astype(dt),
            jnp.stack(s_n, axis=1).astype(dt), jnp.stack(s_m, axis=1).astype(dt))
```

```python
import functools
import math

import jax
import jax.numpy as jnp
from jax import lax
from jax.experimental import pallas as pl
from jax.experimental.pallas import tpu as pltpu

F32 = jnp.float32
BF16 = jnp.bfloat16

GRID_W = 64
H_A = 4
H_B = 4
R_ALPHA = 16
TAU_GLA = 16.0
CHUNK = 64
EPS = 1e-6
LANES = 128
COND_ROWS = 8
SMALL_W = LANES
GATE_LANE0 = 2 * R_ALPHA
VMEM_LIMIT = 56 * 1024 * 1024
SCAN_UNROLL = 4
PIPELINE_STARTS = 8
TOKEN_TILE = 512
FF_CHUNK = 512
GLA_FACTORED_DECAY_MAX = 60.0
MOD_SPLIT = 2
ADA_TILE = 1024


def _sigmoid(x):
    return 1.0 / (1.0 + jnp.exp(-x))


def _silu(x):
    return x * _sigmoid(x)


def _log_sigmoid(x):
    return jnp.minimum(x, 0.0) - jnp.log1p(jnp.exp(-jnp.abs(x)))


def _dot(a, b):
    return jnp.dot(a, b, preferred_element_type=F32)


def _dot_nt(a, b):
    return lax.dot_general(a, b, (((1,), (1,)), ((), ())), preferred_element_type=F32)


def _rms(x, w):
    return x * lax.rsqrt(jnp.mean(x * x, axis=-1, keepdims=True) + EPS) * w


def _tri_sum(tri, x):
    hi = x.astype(BF16)
    r1 = x - hi.astype(F32)
    mid = r1.astype(BF16)
    lo = (r1 - mid.astype(F32)).astype(BF16)
    return _dot(tri, hi) + _dot(tri, mid) + _dot(tri, lo)


def _chunk_masks(L):
    row = lax.broadcasted_iota(jnp.int32, (L, L), 0)
    col = lax.broadcasted_iota(jnp.int32, (L, L), 1)
    lower = row >= col
    upper = row <= col
    return lower, upper


def _ada_tile(cc_ref, c_ref, w_ref, b_ref, o_ref):
    D = cc_ref.shape[1]
    sub = lax.broadcasted_iota(jnp.int32, (COND_ROWS, D), 0)
    cond = jnp.where(sub == 0, cc_ref[...], 0.0)
    for r in range(c_ref.shape[0]):
        cond = jnp.where(sub == 1 + r, c_ref[r:r + 1, :], cond)
    o_ref[...] = _dot(_silu(cond).astype(BF16), w_ref[...].astype(BF16)) + b_ref[...]


def _ada_call(cc, c, w_ada, b_ada, n_cols):
    D = cc.shape[1]
    tn = min(n_cols, ADA_TILE)
    return pl.pallas_call(
        _ada_tile,
        grid=(n_cols // tn,),
        in_specs=[
            pl.BlockSpec(cc.shape, lambda j: (0, 0)),
            pl.BlockSpec(c.shape, lambda j: (0, 0)),
            pl.BlockSpec((D, tn), lambda j: (0, j)),
            pl.BlockSpec((1, tn), lambda j: (0, j)),
        ],
        out_specs=pl.BlockSpec((COND_ROWS, tn), lambda j: (0, j)),
        out_shape=jax.ShapeDtypeStruct((COND_ROWS, n_cols), F32),
        compiler_params=pltpu.CompilerParams(dimension_semantics=("arbitrary",),
                                             vmem_limit_bytes=VMEM_LIMIT),
        name="ada_mod",
    )(cc, c, w_ada, b_ada)


def _tile_group(n_ctx, tiles_per_req):
    i = pl.program_id(0)
    is_ctx = i < n_ctx
    row = jnp.where(is_ctx, 0, 1 + jnp.maximum(i - n_ctx, 0) // tiles_per_req)
    return is_ctx, row


def _ctx_tile(n_ctx):
    return lambda i: (jnp.minimum(i, n_ctx - 1), 0)


def _lat_tile(n_ctx):
    return lambda i: (jnp.maximum(i - n_ctx, 0), 0)


def _inproj_kernel(xc_ref, xl_ref, mod_ref, nw_ref, wt_ref, z_ref, wb_scr, *, n_ctx, tiles_per_req, big_rows,
                   small_rows):
    D = xc_ref.shape[1]

    @pl.when(pl.program_id(0) == 0)
    def _():
        col = 0
        for r0, n in big_rows:
            for k in range(n // LANES):
                blk = wt_ref[r0 + k * LANES:r0 + (k + 1) * LANES, :]
                wb_scr[:, col:col + LANES] = blk.T.astype(BF16)
                col += LANES
        parts = [wt_ref[r0:r0 + n, :] for r0, n in small_rows]
        n_small = sum(n for _, n in small_rows)
        parts.append(jnp.zeros((SMALL_W - n_small, D), F32))
        wb_scr[:, col:col + SMALL_W] = jnp.concatenate(parts, axis=0).T.astype(BF16)

    is_ctx, row = _tile_group(n_ctx, tiles_per_req)

    def tile(x_ref):
        sh1 = mod_ref[pl.ds(row, 1), 0:D]
        sc1 = mod_ref[pl.ds(row, 1), D:2 * D]
        h = _rms(x_ref[...], nw_ref[...]) * (1.0 + sc1) + sh1
        z_ref[...] = _dot(h.astype(BF16), wb_scr[...])

    @pl.when(is_ctx)
    def _():
        tile(xc_ref)

    @pl.when(jnp.logical_not(is_ctx))
    def _():
        tile(xl_ref)


def _inproj_call(xc2d, xl2d, mod, norm_w, w_in_t, *, tm, tiles_per_req, big_rows, small_rows):
    (Mc, D), Ml = xc2d.shape, xl2d.shape[0]
    n_ctx = Mc // tm
    n_out = sum(n for _, n in big_rows) + SMALL_W
    kern = functools.partial(_inproj_kernel, n_ctx=n_ctx, tiles_per_req=tiles_per_req,
                             big_rows=big_rows, small_rows=small_rows)
    return pl.pallas_call(
        kern,
        grid=((Mc + Ml) // tm,),
        in_specs=[
            pl.BlockSpec((tm, D), _ctx_tile(n_ctx)),
            pl.BlockSpec((tm, D), _lat_tile(n_ctx)),
            pl.BlockSpec(mod.shape, lambda i: (0, 0)),
            pl.BlockSpec((1, D), lambda i: (0, 0)),
            pl.BlockSpec(w_in_t.shape, lambda i: (0, 0), pipeline_mode=pl.Buffered(1)),
        ],
        out_specs=pl.BlockSpec((tm, n_out), lambda i: (i, 0)),
        out_shape=jax.ShapeDtypeStruct((Mc + Ml, n_out), F32),
        scratch_shapes=[pltpu.VMEM((D, n_out), BF16)],
        compiler_params=pltpu.CompilerParams(dimension_semantics=("arbitrary",),
                                             vmem_limit_bytes=VMEM_LIMIT),
        name="norm_inproj",
    )(xc2d, xl2d, mod, norm_w.reshape(1, D), w_in_t)


def _chunk_loop(n_chunks, unroll, make_units):
    def step(ns):
        pending = list(make_units(ns))
        active = []
        while pending or active:
            for _ in range(min(PIPELINE_STARTS, len(pending))):
                active.append(pending.pop(0))
            alive = []
            for g in active:
                try:
                    next(g)
                    alive.append(g)
                except StopIteration:
                    pass
            active = alive

    if unroll >= n_chunks:
        step(list(range(n_chunks)))
        return

    def body(i, carry):
        step([i * unroll + u for u in range(unroll)])
        return carry

    lax.fori_loop(0, n_chunks // unroll, body, 0)


def _chunk_rows(n):
    if isinstance(n, int):
        return pl.ds(n * CHUNK, CHUNK)
    return pl.ds(pl.multiple_of(n * CHUNK, CHUNK), CHUNK)


def _cast_specs(casts, n_steps):
    in_specs, out_specs, out_shape, args = [], [], [], []
    for w, axis in casts:
        blk = list(w.shape)
        assert blk[axis] % n_steps == 0
        blk[axis] //= n_steps
        assert blk[0] % 16 == 0 and blk[1] % LANES == 0
        idx = (lambda b: (b, 0)) if axis == 0 else (lambda b: (0, b))
        in_specs.append(pl.BlockSpec(tuple(blk), idx))
        out_specs.append(pl.BlockSpec(tuple(blk), idx))
        out_shape.append(jax.ShapeDtypeStruct(w.shape, BF16))
        args.append(w)
    return in_specs, out_specs, out_shape, args


def _gla_body(q_ref, k_ref, v_ref, g_ref, sm_ref, s0_ref, wal_ref, bal_ref, gw_ref, out_ref, snew_ref,
              st_scr, sall_scr, qh_scr, qs_scr, kh_scr):
    has_state = s0_ref is not None
    write_state = snew_ref is not None
    T = q_ref.shape[0]
    L = CHUNK
    N = T // L
    HK = q_ref.shape[1]
    DK = HK // H_A
    DV = v_ref.shape[1] // H_A
    scale = DK ** -0.5
    n_pairs = HK // LANES

    lower, upper = _chunk_masks(L)
    tri = (lower.astype(BF16), upper.astype(BF16))
    tmask = (lower, upper)
    lane = lax.broadcasted_iota(jnp.int32, (1, LANES), 1)
    head_mask = (lane < DK, lane >= DK)

    for d in range(2):
        for p in range(n_pairs):
            if has_state:
                st_scr[d, p] = s0_ref[d, p].T
            else:
                st_scr[d, p] = jnp.zeros((LANES, LANES), F32)

    def decay_pre(d, r):
        return _dot(sm_ref[r, :].astype(BF16), wal_ref[:, d * HK:(d + 1) * HK]) + bal_ref[d:d + 1, :]

    neg_pre = jnp.maximum(-(_dot(sm_ref[...].astype(BF16), wal_ref[...])
                            + jnp.concatenate([bal_ref[0:1, :], bal_ref[1:2, :]], axis=1)), 0.0)
    chunk_sums = jnp.sum(neg_pre.reshape(N, L, 2 * HK), axis=1)
    decay_span = (jnp.max(chunk_sums) + L * math.log(2.0)) * (1.0 / TAU_GLA)

    def state_group(ns, dirs=(0, 1)):
        units = [(d, n if d == 0 else N - 1 - n) for n in ns for d in dirs]
        rows = [_chunk_rows(n) for _, n in units]
        vt_all = [[jnp.concatenate([v_ref[r, (2 * p + j) * DV:(2 * p + j + 1) * DV] for j in range(2)],
                                   axis=0).T.astype(BF16) for p in range(n_pairs)] for r in rows]
        yield
        pre = [decay_pre(d, r) for (d, _), r in zip(units, rows)]
        yield
        g = [_log_sigmoid(x) * (1.0 / TAU_GLA) for x in pre]
        yield
        b = [_tri_sum(tri[d], gi) for (d, _), gi in zip(units, g)]
        yield
        ks_all, dec_all = [], []
        for (d, _), r, bi in zip(units, rows, b):
            bend = bi[L - 1:L, :] if d == 0 else bi[0:1, :]
            q = q_ref[r, :] * scale
            ks = (k_ref[r, :] * jnp.exp(bend - bi)).astype(BF16)
            qh_scr[d, r, :] = (q * jnp.exp(bi - bend)).astype(BF16)
            qs_scr[d, r, :] = (q * jnp.exp(bi)).astype(BF16)
            kh_scr[d, r, :] = ks
            ks_all.append(ks)
            dec_all.append(jnp.exp(bend))
        yield
        upd_all = []
        for vt_u, ks in zip(vt_all, ks_all):
            upd_u = []
            for p in range(n_pairs):
                kp = ks[:, p * LANES:(p + 1) * LANES]
                kk = jnp.concatenate([jnp.where(head_mask[j], kp, jnp.zeros_like(kp)) for j in range(2)], axis=0)
                upd_u.append(_dot(vt_u[p], kk))
            upd_all.append(upd_u)
        yield
        st = {d: [st_scr[d, p] for p in range(n_pairs)] for d in dirs}
        for (d, n), dec, upd in zip(units, dec_all, upd_all):
            for p in range(n_pairs):
                sall_scr[d, n, p] = st[d][p].astype(BF16)
                st[d][p] = st[d][p] * dec[:, p * LANES:(p + 1) * LANES] + upd[p]
        for d in dirs:
            for p in range(n_pairs):
                st_scr[d, p] = st[d][p]

    def stack_heads(x):
        return jnp.concatenate([jnp.where(head_mask[j], x, jnp.zeros_like(x)) for j in range(2)], axis=0)

    tok = lax.broadcasted_iota(jnp.int32, (L, 1), 0)
    row_t = lax.broadcasted_iota(jnp.int32, (2 * L, L), 0) & (L - 1)
    col_s = lax.broadcasted_iota(jnp.int32, (2 * L, L), 1)

    def exact_scores(d, r, p):
        ls = slice(p * LANES, (p + 1) * LANES)
        b = _tri_sum(tri[d], _log_sigmoid(decay_pre(d, r)[:, ls]) * (1.0 / TAU_GLA))
        q = q_ref[r, ls] * scale
        k = k_ref[r, ls]
        acc = jnp.where(row_t == col_s, _dot_nt(stack_heads(q).astype(BF16), k.astype(BF16)), 0.0)
        src = lax.broadcasted_iota(jnp.int32, (L, L), 1)
        h = L // 2
        while h >= 1:
            first = tok & ~(2 * h - 1)
            edge = first + (h - 1 if d == 0 else h)
            b_edge = _tri_sum((src == edge).astype(BF16), b)
            upper = (tok & (2 * h - 1)) >= h
            later, earlier = (upper, ~upper) if d == 0 else (~upper, upper)
            qt = jnp.where(later, q * jnp.exp(b - b_edge), 0.0)
            kt = jnp.where(earlier, k * jnp.exp(b_edge - b), 0.0)
            sc = _dot_nt(stack_heads(qt).astype(BF16), kt.astype(BF16))
            acc = acc + jnp.where((row_t & ~(2 * h - 1)) == (col_s & ~(2 * h - 1)), sc, 0.0)
            h //= 2
        return acc

    def out_group(ns, exact_decay=False):
        pairs = [(d, ni, p) for ni in range(len(ns)) for d in range(2) for p in range(n_pairs)]
        scores, inter = [], []
        for d, ni, p in pairs:
            r = _chunk_rows(ns[ni])
            ls = slice(p * LANES, (p + 1) * LANES)
            if exact_decay:
                scores.append(exact_scores(d, r, p))
            else:
                scores.append(_dot_nt(stack_heads(qh_scr[d, r, ls]), kh_scr[d, r, ls]))
            inter.append(_dot_nt(stack_heads(qs_scr[d, r, ls]), sall_scr[d, ns[ni], p]))
        yield
        probs = [[jnp.where(tmask[d], sc[j * L:(j + 1) * L, :], 0.0).astype(BF16) for j in range(2)]
                 for (d, _, _), sc in zip(pairs, scores)]
        yield
        outs = {}
        for (d, ni, p), pr, it in zip(pairs, probs, inter):
            r = _chunk_rows(ns[ni])
            for j in range(2):
                vs = slice((2 * p + j) * DV, (2 * p + j + 1) * DV)
                outs[(d, ni, 2 * p + j)] = _dot(pr[j], v_ref[r, vs].astype(BF16)) + it[j * L:(j + 1) * L, :]
        yield
        for ni, n in enumerate(ns):
            r = _chunk_rows(n)
            for h in range(H_A):
                vs = slice(h * DV, (h + 1) * DV)
                o = outs[(0, ni, h)] + outs[(1, ni, h)]
                out_ref[r, vs] = (_rms(o, gw_ref[:, vs]) * _silu(g_ref[r, vs])).astype(out_ref.dtype)

    def finish():
        if write_state:
            for d in range(2):
                for p in range(n_pairs):
                    snew_ref[d, p] = st_scr[d, p].T

    return state_group, out_group, finish, decay_span


def _gla_scratch(T, HK):
    n_pairs = HK // LANES
    n_chunks = T // CHUNK
    return [
        pltpu.VMEM((2, n_pairs, LANES, LANES), F32),
        pltpu.VMEM((2, n_chunks, n_pairs, LANES, LANES), BF16),
        pltpu.VMEM((2, T, HK), BF16),
        pltpu.VMEM((2, T, HK), BF16),
        pltpu.VMEM((2, T, HK), BF16),
    ]


def _mlstm_body(qk_ref, v_ref, og_ref, sm_ref, c0_ref, n0_ref, m0_ref, cw_ref, bm_ref, gw_ref,
                out_ref, cnew_ref, nnew_ref, mnew_ref,
                pad_scr, qk_scr, y_scr, c_scr, n_scr, m_scr, call_scr, nall_scr, mall_scr, g_scr, f_scr,
                *, grid_w):
    has_state = c0_ref is not None
    write_state = cnew_ref is not None
    T = qk_ref.shape[0]
    L = CHUNK
    N = T // L
    C2 = qk_ref.shape[1]
    HK = C2 // 2
    DK = HK // H_B
    DV = v_ref.shape[1] // H_B
    scale = DK ** -0.5
    n_pairs = HK // LANES
    P = pad_scr.shape[0] - T
    P0 = P // 2
    rows_img = T // grid_w

    lower, upper = _chunk_masks(L)
    tri = (lower.astype(BF16), upper.astype(BF16))
    tmask = (lower, upper)
    lane = lax.broadcasted_iota(jnp.int32, (1, LANES), 1)
    head_mask = (lane < DK, lane >= DK)
    lane_in = lane & (L - 1)

    def lane_cummax(x, d):
        k = 1
        while k < L:
            if d == 0:
                x = jnp.maximum(x, jnp.where(lane_in >= k, pltpu.roll(x, k, axis=1), -jnp.inf))
            else:
                x = jnp.maximum(x, jnp.where(lane_in < L - k, pltpu.roll(x, LANES - k, axis=1), -jnp.inf))
            k *= 2
        return x

    for d in range(2):
        for p in range(n_pairs):
            if has_state:
                c_scr[d, p] = c0_ref[d, p]
                n_scr[2 * d + p:2 * d + p + 1, :] = jnp.concatenate(
                    [n0_ref[d, 2 * p + j:2 * p + j + 1, :] for j in range(2)], axis=1)
            else:
                c_scr[d, p] = jnp.zeros((LANES, LANES), F32)
                n_scr[2 * d + p:2 * d + p + 1, :] = jnp.zeros((1, LANES), F32)
    eye_h = (lax.broadcasted_iota(jnp.int32, (H_B, H_B), 0) == lax.broadcasted_iota(jnp.int32, (H_B, H_B), 1))

    def to_col(row):
        return jnp.sum(jnp.where(eye_h, row, 0.0), axis=1, keepdims=True)

    def to_row(col):
        return jnp.sum(jnp.where(eye_h, col, 0.0), axis=0, keepdims=True)

    for d in range(2):
        if has_state:
            m_scr[H_B * d:H_B * (d + 1), 0:1] = to_col(m0_ref[d:d + 1, :])
        else:
            m_scr[H_B * d:H_B * (d + 1), 0:1] = jnp.zeros((H_B, 1), F32)

    pad_scr[0:P0, :] = jnp.zeros((P0, C2), F32)
    pad_scr[P0 + T:P + T, :] = jnp.zeros((P - P0, C2), F32)

    def copy_in(i, carry):
        r0 = pl.multiple_of(i * L, L)
        pad_scr[pl.ds(P0 + r0, L), :] = qk_ref[pl.ds(r0, L), :]
        return carry

    lax.fori_loop(0, N, copy_in, 0)

    lane_c = lax.broadcasted_iota(jnp.int32, (1, C2), 1)
    qscale = jnp.where(lane_c < HK, scale, 1.0).astype(F32)
    sub = lax.broadcasted_iota(jnp.int32, (L, 1), 0)
    img_rows = (0,) if rows_img == 1 else (-1, 0, 1)

    def conv_tile(i, carry):
        r0 = pl.multiple_of(i * L, L)
        col = lax.rem(r0, grid_w) + sub
        ok_left = col >= 1
        ok_right = col <= grid_w - 2
        sums = [None, None, None]
        for di in img_rows:
            blk = pad_scr[pl.ds(P0 + r0 + di * grid_w - 8, L + 16), :]
            for k in range(3):
                term = blk * cw_ref[di + 1, k:k + 1, :]
                sums[k] = term if sums[k] is None else sums[k] + term
        acc = (sums[1][8:8 + L, :] + jnp.where(ok_left, sums[0][7:7 + L, :], 0.0)
               + jnp.where(ok_right, sums[2][9:9 + L, :], 0.0))
        qk_scr[pl.ds(r0, L), :] = _silu(acc) * qscale
        return carry

    lax.fori_loop(0, N, conv_tile, 0)

    gl = lane - GATE_LANE0
    is_f = ((gl >= H_B) & (gl < 2 * H_B)) | ((gl >= 3 * H_B) & (gl < 4 * H_B))

    def gate_tile(i, carry):
        rows = pl.ds(pl.multiple_of(i * L, L), L)
        x = sm_ref[rows, :] + bm_ref[...]
        y_scr[rows, :] = jnp.where(is_f, _log_sigmoid(x), x)
        return carry

    lax.fori_loop(0, N, gate_tile, 0)


    def state_group(ns, dirs=(0, 1)):
        units = [(d, n if d == 0 else N - 1 - n) for n in ns for d in dirs]
        rows = [_chunk_rows(n) for _, n in units]
        kt_all = [[qk_scr[r, HK + p * LANES:HK + (p + 1) * LANES].T for p in range(n_pairs)] for r in rows]
        yield
        xs = [y_scr[r, :] for r in rows]
        fsum = [_tri_sum(tri[d], x) for (d, _), x in zip(units, xs)]
        yield
        wk_all, f_end, c_end = [], [], []
        for (d, n), r, x, fs in zip(units, rows, xs, fsum):
            y = jnp.where(is_f, fs, x)
            li0 = GATE_LANE0 + 2 * H_B * d
            blk = jnp.concatenate([y, y], axis=0).T[li0:li0 + 2 * H_B, :]
            frow = pltpu.roll(blk, H_B, axis=0)
            grow = blk - frow
            g_scr[d, n] = grow
            f_scr[d, n] = frow
            e_col = L - 1 if d == 0 else 0
            f_end.append(frow[0:H_B, e_col:e_col + 1])
            ce8 = jnp.max(grow, axis=1, keepdims=True)
            c_end.append(ce8[0:H_B, :])
            wk_all.append(jnp.exp(grow[:, 0:L] - ce8))
        yield
        kv_all, ksum_all = [], []
        for r, wk8, kt_u in zip(rows, wk_all, kt_all):
            kv_u, ks_u = [], []
            wk8b = wk8.astype(BF16)
            for p in range(n_pairs):
                kpb = qk_scr[r, HK + p * LANES:HK + (p + 1) * LANES].astype(BF16)
                ks8 = _dot(wk8b, kpb)
                for j in range(2):
                    h = 2 * p + j
                    kwt = (kt_u[p][j * DK:(j + 1) * DK, :] * wk8[h:h + 1, :]).astype(BF16)
                    kv_u.append(_dot(kwt, v_ref[r, h * DV:(h + 1) * DV].astype(BF16)))
                    ks_u.append(ks8[h:h + 1, :])
            kv_all.append(kv_u)
            ksum_all.append(ks_u)
        yield
        m_run = {d: m_scr[H_B * d:H_B * (d + 1), 0:1] for d in dirs}
        a_all, b_all = [], []
        for (d, n), fe, ce in zip(units, f_end, c_end):
            mall_scr[d, n, 0:H_B, 0:1] = m_run[d]
            mx = jnp.maximum(m_run[d], ce)
            a_all.append(jnp.exp(m_run[d] - mx))
            b_all.append(jnp.exp(ce - mx))
            m_run[d] = fe + mx
        for d in dirs:
            m_scr[H_B * d:H_B * (d + 1), 0:1] = m_run[d]
        yield
        c_run = {d: [[c_scr[d, p, j * DK:(j + 1) * DK, :] for j in range(2)] for p in range(n_pairs)] for d in dirs}
        n_run = {d: [n_scr[2 * d + p:2 * d + p + 1, :] for p in range(n_pairs)] for d in dirs}
        for (d, n), a4, b4, kv_u, ks_u in zip(units, a_all, b_all, kv_all, ksum_all):
            for p in range(n_pairs):
                nall_scr[d, n, p:p + 1, :] = n_run[d][p]
                a_s = [a4[2 * p + j:2 * p + j + 1, :] for j in range(2)]
                b_s = [b4[2 * p + j:2 * p + j + 1, :] for j in range(2)]
                for j in range(2):
                    cj = c_run[d][p][j]
                    call_scr[d, n, p, j * DK:(j + 1) * DK, :] = cj.astype(BF16)
                    c_run[d][p][j] = a_s[j] * cj + b_s[j] * kv_u[2 * p + j]
                n_run[d][p] = (jnp.where(head_mask[0], a_s[0], a_s[1]) * n_run[d][p]
                               + jnp.where(head_mask[0], b_s[0] * ks_u[2 * p], b_s[1] * ks_u[2 * p + 1]))
        for d in dirs:
            for p in range(n_pairs):
                n_scr[2 * d + p:2 * d + p + 1, :] = n_run[d][p]
                for j in range(2):
                    c_scr[d, p, j * DK:(j + 1) * DK, :] = c_run[d][p][j]

    eye = lower & upper
    ones8 = jnp.ones((8, L), BF16)
    sub8 = lax.broadcasted_iota(jnp.int32, (8, LANES), 0)
    sub_h = lax.broadcasted_iota(jnp.int32, (H_B, L), 0)
    n_rows = [((sub8 == 2 * p) & head_mask[0]) | ((sub8 == 2 * p + 1) & head_mask[1]) for p in range(n_pairs)]

    def head_rows(vals):
        out = vals[0][0:H_B, :]
        for h in range(1, H_B):
            out = jnp.where(sub_h == h, vals[h][0:H_B, :], out)
        return out

    def out_group(ns):
        chunks = [(d, n) for n in ns for d in range(2)]
        pairs = [(d, n, p) for d, n in chunks for p in range(n_pairs)]
        units = [(d, n, p, j) for d, n, p in pairs for j in range(2)]
        cms = [lane_cummax(g_scr[d, n], d)[0:H_B, 0:L] for d, n in chunks]
        qk2s, qc2s, qn2s = [], [], []
        for d, n, p in pairs:
            r = _chunk_rows(n)
            qp = qk_scr[r, p * LANES:(p + 1) * LANES]
            q2 = jnp.concatenate([jnp.where(head_mask[j], qp, 0.0) for j in range(2)], axis=0).astype(BF16)
            qk2s.append(_dot_nt(q2, qk_scr[r, HK + p * LANES:HK + (p + 1) * LANES].astype(BF16)))
            qc2s.append(_dot(q2, call_scr[d, n, p]))
            nsel = jnp.where(n_rows[p], nall_scr[d, n, p:p + 1, :], 0.0).astype(BF16)
            qn2s.append(_dot_nt(nsel, qp.astype(BF16)))
        yield
        s_all = []
        for ui, (d, n, p, j) in enumerate(units):
            grow = g_scr[d, n, 2 * p + j:2 * p + j + 1, 0:L]
            e = jnp.where(tmask[d], grow, -jnp.inf)
            cmax = jnp.max(e, axis=-1, keepdims=True)
            s_all.append((qk2s[ui // 2][j * L:(j + 1) * L, :] * jnp.exp(e - cmax)).astype(BF16))
        yield
        nums =[_dot(s, v_ref[_chunk_rows(n), (2 * p + j) * DV:(2 * p + j + 1) * DV].astype(BF16))
                for (d, n, p, j), s in zip(units, s_all)]
        dens = [_dot_nt(ones8, s) for s in s_all]
        yield
        scales = []
        for ci, (d, n) in enumerate(chunks):
            den_loc = head_rows(dens[ci * H_B:(ci + 1) * H_B])
            qn = qn2s[ci * n_pairs][0:H_B, :]
            for p in range(1, n_pairs):
                qn = qn + qn2s[ci * n_pairs + p][0:H_B, :]
            cm = cms[ci]
            m_prev = mall_scr[d, n, 0:H_B, 0:1]
            delta = cm - m_prev
            t = jnp.exp(-jnp.abs(delta))
            w_loc = jnp.where(delta <= 0.0, t, 1.0)
            w_inter = jnp.where(delta <= 0.0, 1.0, t)
            mt = f_scr[d, n, 0:H_B, 0:L] + jnp.maximum(m_prev, cm)
            den = w_loc * den_loc + w_inter * qn
            rinv = 1.0 / jnp.maximum(jnp.abs(den), jnp.exp(-mt))
            scales.append((w_loc * rinv, w_inter * rinv))
        yield
        hs = []
        for ui, (d, n, p, j) in enumerate(units):
            h = 2 * p + j
            sc_loc, sc_inter = scales[ui // H_B]
            d_loc = jnp.where(eye, sc_loc[h:h + 1, :], 0.0).astype(BF16)
            d_inter = jnp.where(eye, sc_inter[h:h + 1, :], 0.0).astype(BF16)
            hs.append(_dot(d_loc, nums[ui].astype(BF16))
                      + _dot(d_inter, qc2s[ui // 2][j * L:(j + 1) * L, :].astype(BF16)))
        yield
        for ni, n in enumerate(ns):
            r = _chunk_rows(n)
            for h in range(H_B):
                vs = slice(h * DV, (h + 1) * DV)
                o = hs[(2 * ni) * H_B + h] + hs[(2 * ni + 1) * H_B + h]
                out_ref[r, vs] = (_rms(o, gw_ref[:, vs]) * _sigmoid(og_ref[r, vs])).astype(out_ref.dtype)

    def finish():
        if write_state:
            for d in range(2):
                for p in range(n_pairs):
                    cnew_ref[d, p] = c_scr[d, p]
                    for j in range(2):
                        nnew_ref[d, 2 * p + j:2 * p + j + 1, :] = n_scr[2 * d + p:2 * d + p + 1, j * DK:(j + 1) * DK]
                mnew_ref[d:d + 1, :] = to_row(m_scr[H_B * d:H_B * (d + 1), 0:1])

    return state_group, out_group, finish


def _mlstm_scratch(T, C2, grid_w):
    n_pairs = C2 // 2 // LANES
    n_chunks = T // CHUNK
    pad_rows = 2 * (grid_w + 8) if T // grid_w > 1 else 16
    return [
        pltpu.VMEM((T + pad_rows, C2), F32),
        pltpu.VMEM((T, C2), F32),
        pltpu.VMEM((T, SMALL_W), F32),
        pltpu.VMEM((2, n_pairs, LANES, LANES), F32),
        pltpu.VMEM((8, LANES), F32),
        pltpu.VMEM((8, LANES), F32),
        pltpu.VMEM((2, n_chunks, n_pairs, LANES, LANES), BF16),
        pltpu.VMEM((2, n_chunks, 8, LANES), F32),
        pltpu.VMEM((2, n_chunks, 8, LANES), F32),
        pltpu.VMEM((2, n_chunks, 8, LANES), F32),
        pltpu.VMEM((2, n_chunks, 8, LANES), F32),
    ]


N_GLA_SCRATCH = 5
N_MLSTM_SCRATCH = 11


def _scan_kernel(*refs, cols, layer, has_state, write_state, n_cast, ride_ada, grid_w, unroll):
    refs = list(refs)
    z_ref = refs.pop(0)
    s0_ref = c0_ref = n0_ref = m0_ref = None
    if has_state:
        s0_ref, c0_ref, n0_ref, m0_ref = refs[:4]
        del refs[:4]
    wa_ref, bal_ref, gwa_ref, cw_ref, bmg_ref, gwb_ref = refs[:6]
    del refs[:6]
    cast_in = refs[:n_cast]
    del refs[:n_cast]
    if ride_ada:
        ada_in = refs[:4]
        del refs[:4]
    outa_ref, outb_ref = refs[:2]
    del refs[:2]
    snew_ref = cnew_ref = nnew_ref = mnew_ref = None
    if write_state:
        snew_ref, cnew_ref, nnew_ref, mnew_ref = refs[:4]
        del refs[:4]
    cast_out = refs[:n_cast]
    del refs[:n_cast]
    if ride_ada:
        ada_out = refs.pop(0)
    wal_scr, bm_scr = refs[:2]
    del refs[:2]
    gla_scr = refs[:N_GLA_SCRATCH]
    mlstm_scr = refs[N_GLA_SCRATCH:]

    for src, dst in zip(cast_in, cast_out):
        dst[...] = src[...].astype(BF16)
    if ride_ada:
        _ada_tile(*ada_in, ada_out)

    R, HK = wa_ref.shape[1], wa_ref.shape[2]
    wal_scr[...] = jnp.zeros(wal_scr.shape, BF16)
    for d in range(2):
        wal_scr[d * R:(d + 1) * R, d * HK:(d + 1) * HK] = wa_ref[d].astype(BF16)
    lane = lax.broadcasted_iota(jnp.int32, (1, LANES), 1)
    bm = jnp.zeros((1, LANES), F32)
    for g in range(bmg_ref.shape[1]):
        for h in range(H_B):
            bm = jnp.where(lane == GATE_LANE0 + H_B * g + h, bmg_ref[layer, g, h], bm)
    bm_scr[0:1, :] = bm

    def view(name):
        c0, w = cols[name]
        return z_ref.at[:, pl.ds(c0, w)]

    sm_ref = view("small")
    n_chunks = z_ref.shape[0] // CHUNK
    gla = _gla_body(view("qa"), view("ka"), view("va"), view("ga"), sm_ref, s0_ref, wal_scr, bal_ref, gwa_ref,
                    outa_ref, snew_ref, *gla_scr)
    mlstm = _mlstm_body(view("qkb"), view("vb"), view("ob"), sm_ref, c0_ref, n0_ref, m0_ref, cw_ref,
                        bm_scr.at[0:1, :], gwb_ref, outb_ref, cnew_ref, nnew_ref, mnew_ref, *mlstm_scr,
                        grid_w=grid_w)
    gla_state, gla_out, gla_finish, decay_span = gla
    mlstm_state, mlstm_out, mlstm_finish = mlstm

    def passes(gla_out_fn):
        _chunk_loop(n_chunks, unroll, lambda ns: [fn([n], (d,)) for n in ns for d in range(2)
                                                  for fn in (mlstm_state, gla_state)])
        _chunk_loop(n_chunks, unroll, lambda ns: [fn([n]) for n in ns for fn in (mlstm_out, gla_out_fn)])

    wide_decay = decay_span > GLA_FACTORED_DECAY_MAX

    @pl.when(jnp.logical_not(wide_decay))
    def _():
        passes(gla_out)

    @pl.when(wide_decay)
    def _():
        passes(functools.partial(gla_out, exact_decay=True))

    gla_finish()
    mlstm_finish()


def _scan_call(z2d, row0, B, T, states, lw, layer, *, grid_w, write_state, casts=(), ada=None):
    n_z = z2d.shape[1]
    assert row0 % T == 0 and z2d.shape[0] % T == 0
    z3 = z2d.reshape(z2d.shape[0] // T, T, n_z)
    blk0 = row0 // T
    HK = lw["w_alpha2"].shape[-1]
    DA = lw["gnorm_a_w"].shape[0]
    C2 = lw["conv_w"].shape[-1]
    DB = lw["gnorm_b_w"].shape[0]
    DK_A, DK_B = HK // H_A, C2 // 2 // H_B
    pa, pb = HK // LANES, C2 // 2 // LANES
    n_chunks = T // CHUNK
    has_state = states is not None
    widths = (("qa", HK), ("ka", HK), ("va", DA), ("ga", DA), ("qkb", C2), ("vb", DB), ("ob", DB),
              ("small", SMALL_W))
    cols, c0 = {}, 0
    for name, w in widths:
        cols[name] = (c0, w)
        c0 += w
    assert c0 == n_z
    cast_in_specs, cast_out_specs, cast_out_shape, cast_args = _cast_specs(casts, B)
    kern = functools.partial(_scan_kernel, cols=cols, layer=layer, has_state=has_state, write_state=write_state,
                             n_cast=len(casts), ride_ada=ada is not None, grid_w=grid_w,
                             unroll=min(n_chunks, SCAN_UNROLL))

    def per_batch(shape):
        nd = len(shape)
        return pl.BlockSpec((None,) + tuple(shape), lambda b: (b,) + (0,) * nd)

    def per_batch_layer(shape):
        nd = len(shape)
        return pl.BlockSpec((None, None) + tuple(shape), lambda b: (b, layer) + (0,) * nd)

    def of_layer(a):
        return pl.BlockSpec((None,) + a.shape[1:], lambda b: (layer,) + (0,) * (a.ndim - 1))

    def whole(a):
        return pl.BlockSpec(a.shape, lambda b: (0,) * a.ndim)

    state_shapes = ((2, pa, LANES, LANES), (2, pb, LANES, LANES), (2, H_B, DK_B), (2, H_B))
    in_specs = [pl.BlockSpec((None, T, n_z), lambda b: (b + blk0, 0, 0))]
    args = [z3]
    if has_state:
        s_gla, s_c, s_n, s_m = states
        depth = s_gla.shape[1]
        args += [s_gla.reshape((B, depth) + state_shapes[0]), s_c.reshape((B, depth) + state_shapes[1]), s_n, s_m]
        in_specs += [per_batch_layer(s) for s in state_shapes]
    args += [lw["w_alpha2"], lw["b_alpha"], lw["gnorm_a_w"].reshape(1, DA), lw["conv_w"], lw["b_mgate"],
             lw["gnorm_b_w"].reshape(1, DB)]
    in_specs += [of_layer(lw["w_alpha2"]), of_layer(lw["b_alpha"]), pl.BlockSpec((1, DA), lambda b: (0, 0)),
                 whole(lw["conv_w"]), pl.BlockSpec(memory_space=pltpu.SMEM), pl.BlockSpec((1, DB), lambda b: (0, 0))]
    args += cast_args
    in_specs += cast_in_specs
    out_specs = [per_batch((T, DA)), per_batch((T, DB))]
    out_shape = [jax.ShapeDtypeStruct((B, T, DA), BF16), jax.ShapeDtypeStruct((B, T, DB), BF16)]
    if write_state:
        out_specs += [per_batch(s) for s in state_shapes]
        out_shape += [jax.ShapeDtypeStruct((B,) + s, F32) for s in state_shapes]
    out_specs += cast_out_specs
    out_shape += cast_out_shape
    if ada is not None:
        cc, c, w_ada, b_ada, col0 = ada
        n_rest = w_ada.shape[1] - col0
        wcol = n_rest // B
        assert n_rest % B == 0 and wcol % LANES == 0 and col0 % wcol == 0
        args += [cc, c, w_ada, b_ada]
        in_specs += [whole(cc), whole(c),
                     pl.BlockSpec((w_ada.shape[0], wcol), lambda b: (0, col0 // wcol + b)),
                     pl.BlockSpec((1, wcol), lambda b: (0, col0 // wcol + b))]
        out_specs.append(pl.BlockSpec((COND_ROWS, wcol), lambda b: (0, b)))
        out_shape.append(jax.ShapeDtypeStruct((COND_ROWS, n_rest), F32))
    scratch = ([pltpu.VMEM((SMALL_W, 2 * HK), BF16), pltpu.VMEM((8, LANES), F32)]
               + _gla_scratch(T, HK) + _mlstm_scratch(T, C2, grid_w))
    assert len(scratch) == 2 + N_GLA_SCRATCH + N_MLSTM_SCRATCH
    return pl.pallas_call(
        kern,
        grid=(B,),
        in_specs=in_specs,
        out_specs=out_specs,
        out_shape=out_shape,
        scratch_shapes=scratch,
        compiler_params=pltpu.CompilerParams(dimension_semantics=("arbitrary",),
                                             vmem_limit_bytes=VMEM_LIMIT),
        name="mixer_scans",
    )(*args)


def _outff_kernel(xc_ref, xl_ref, ac_ref, al_ref, bc_ref, bl_ref, mod_ref, n2_ref, fn_ref, wo_ref, w1_ref, w2_ref,
                  yc_ref, yl_ref, *, n_ctx, tiles_per_req, ff_chunk, final_norm):
    D = xc_ref.shape[1]
    DA = ac_ref.shape[1]
    is_ctx, row = _tile_group(n_ctx, tiles_per_req)

    def mod(k):
        return mod_ref[pl.ds(row, 1), (k - MOD_SPLIT) * D:(k - MOD_SPLIT + 1) * D]

    def tile(x_ref, a_ref, b_ref, y_ref):
        y = _dot(a_ref[...], wo_ref[0:DA, :]) + _dot(b_ref[...], wo_ref[DA:, :])
        x1 = x_ref[...] + mod(2) * y
        h2 = (_rms(x1, n2_ref[...]) * (1.0 + mod(4)) + mod(3)).astype(BF16)
        acc = jnp.zeros(x1.shape, F32)
        for c0 in range(0, w1_ref.shape[1], ff_chunk):
            u = jnp.maximum(_dot(h2, w1_ref[:, c0:c0 + ff_chunk]), 0.0)
            acc = acc + _dot((u * u).astype(BF16), w2_ref[c0:c0 + ff_chunk, :])
        x2 = x1 + mod(5) * acc
        y_ref[...] = _rms(x2, fn_ref[...]) if final_norm else x2

    @pl.when(is_ctx)
    def _():
        tile(xc_ref, ac_ref, bc_ref, yc_ref)

    @pl.when(jnp.logical_not(is_ctx))
    def _():
        tile(xl_ref, al_ref, bl_ref, yl_ref)


def _outff_call(xc2d, xl2d, ac, al, bc, bl, mod, norm2_w, final_w, wo, w1, w2, *, tm, tiles_per_req, final_norm):
    (Mc, D), Ml = xc2d.shape, xl2d.shape[0]
    n_ctx = Mc // tm
    DA = ac.shape[1]
    DFF = w1.shape[1]
    kern = functools.partial(_outff_kernel, n_ctx=n_ctx, tiles_per_req=tiles_per_req, ff_chunk=FF_CHUNK,
                             final_norm=final_norm)
    once = pl.Buffered(1)
    ctx, lat = _ctx_tile(n_ctx), _lat_tile(n_ctx)
    return pl.pallas_call(
        kern,
        grid=((Mc + Ml) // tm,),
        in_specs=[
            pl.BlockSpec((tm, D), ctx), pl.BlockSpec((tm, D), lat),
            pl.BlockSpec((tm, DA), ctx), pl.BlockSpec((tm, DA), lat),
            pl.BlockSpec((tm, D - DA), ctx), pl.BlockSpec((tm, D - DA), lat),
            pl.BlockSpec(mod.shape, lambda i: (0, 0)),
            pl.BlockSpec((1, D), lambda i: (0, 0)),
            pl.BlockSpec((1, D), lambda i: (0, 0)),
            pl.BlockSpec((D, D), lambda i: (0, 0), pipeline_mode=once),
            pl.BlockSpec((D, DFF), lambda i: (0, 0), pipeline_mode=once),
            pl.BlockSpec((DFF, D), lambda i: (0, 0), pipeline_mode=once),
        ],
        out_specs=[pl.BlockSpec((tm, D), ctx), pl.BlockSpec((tm, D), lat)],
        out_shape=[jax.ShapeDtypeStruct((Mc, D), F32), jax.ShapeDtypeStruct((Ml, D), F32)],
        compiler_params=pltpu.CompilerParams(dimension_semantics=("arbitrary",),
                                             vmem_limit_bytes=VMEM_LIMIT),
        name="outproj_mlp",
    )(xc2d, xl2d, ac, al, bc, bl, mod, norm2_w.reshape(1, D), final_w.reshape(1, D), wo, w1, w2)


def _layer(xc, xl, cond, ada_w, cached, lw, layer, ffw, final_w, final_norm):
    (Bc, Tc, D), (Bl, Tl, _) = xc.shape, xl.shape
    tm = TOKEN_TILE
    assert (Bc * Tc) % tm == 0 and Tl % tm == 0 and (Bc * Tc) % Tl == 0
    xc2d, xl2d = xc.reshape(Bc * Tc, D), xl.reshape(Bl * Tl, D)
    mod_in = _ada_call(*cond, *ada_w, MOD_SPLIT * D)
    z = _inproj_call(xc2d, xl2d, mod_in, lw["norm1_w"], lw["w_in_t"], tm=tm, tiles_per_req=Tl // tm,
                     big_rows=lw["big_rows"], small_rows=lw["small_rows"])
    res_c = _scan_call(z, 0, Bc, Tc, None, lw, layer, grid_w=Tc, write_state=True,
                       casts=((ffw[0], 0), (ffw[1], 1), (ffw[2], 0)), ada=(*cond, *ada_w, MOD_SPLIT * D))
    res_l = _scan_call(z, Bc * Tc, Bl, Tl, cached, lw, layer, grid_w=GRID_W, write_state=False)
    wo_b, w1_b, w2_b, mod_out = res_c[-4:]
    yc, yl = _outff_call(xc2d, xl2d, res_c[0].reshape(Bc * Tc, -1), res_l[0].reshape(Bl * Tl, -1),
                         res_c[1].reshape(Bc * Tc, -1), res_l[1].reshape(Bl * Tl, -1), mod_out, lw["norm2_w"],
                         final_w, wo_b, w1_b, w2_b, tm=tm, tiles_per_req=Tl // tm, final_norm=final_norm)
    return yc.reshape(Bc, Tc, D), yl.reshape(Bl, Tl, D), tuple(res_c[2:6])


def _layer_weights(l, norm1_w, norm2_w, w_in, w_alpha2, b_alpha, b_mgate, conv_w, gnorm_a_w, gnorm_b_w):
    hk_a = w_alpha2.shape[-1]
    d_a = gnorm_a_w.shape[-1]
    d_b = gnorm_b_w.shape[-1]
    hk_b = conv_w.shape[-1] // 2
    sizes = (hk_a, hk_a, d_a, d_a, 2 * R_ALPHA, hk_b, hk_b, d_b, d_b, 4 * H_B)
    assert w_alpha2.shape[2] == R_ALPHA and b_mgate.shape[1] * b_mgate.shape[2] == 4 * H_B
    offs = [0]
    for s in sizes:
        offs.append(offs[-1] + s)
    big_rows = ((offs[0], offs[4] - offs[0]), (offs[5], offs[9] - offs[5]))
    small_rows = ((offs[4], offs[5] - offs[4]), (offs[9], offs[10] - offs[9]))
    assert all(n % LANES == 0 and r % 16 == 0 for r, n in big_rows)
    return dict(
        norm1_w=norm1_w[l], norm2_w=norm2_w[l], w_in_t=jnp.swapaxes(w_in[l], 0, 1),
        big_rows=big_rows, small_rows=small_rows,
        w_alpha2=w_alpha2, b_alpha=b_alpha, b_mgate=b_mgate, conv_w=conv_w[l],
        gnorm_a_w=gnorm_a_w[l], gnorm_b_w=gnorm_b_w[l],
    )


def kernel(x_prompt, x_sample, c, state_gla, state_mlstm_C, state_mlstm_n, state_mlstm_m, c_ctx, w_ada, b_ada, norm1_w, norm2_w, w_in, w_alpha2, b_alpha, b_mgate, conv_w, gnorm_a_w, gnorm_b_w, w_out, w_ff1, w_ff2, final_norm_w):
    depth = w_in.shape[0]
    D = x_prompt.shape[-1]
    Bp, Tp, _ = x_prompt.shape
    Bs = x_sample.shape[0]
    assert 1 + Bs <= COND_ROWS
    cond = (c_ctx.reshape(1, D), c)
    cached = (state_gla, state_mlstm_C, state_mlstm_n, state_mlstm_m)
    xp, xs = x_prompt, x_sample
    s_gla, s_c, s_n, s_m = [], [], [], []
    for l in range(depth):
        lw = _layer_weights(l, norm1_w, norm2_w, w_in, w_alpha2, b_alpha, b_mgate, conv_w,
                            gnorm_a_w, gnorm_b_w)
        xp, xs, ctx = _layer(xp, xs, cond, (w_ada[l], b_ada[l].reshape(1, -1)), cached, lw, l,
                             (w_out[l], w_ff1[l], w_ff2[l]), final_norm_w, l == depth - 1)
        s_gla.append(ctx[0].reshape(Bp, 2, H_A, -1, ctx[0].shape[-1]))
        s_c.append(ctx[1].reshape(Bp, 2, H_B, -1, ctx[1].shape[-1]))
        s_n.append(ctx[2])
        s_m.append(ctx[3])
    dt = x_prompt.dtype
    return (xp, xs, jnp.stack(s_gla, axis=1).astype(dt), jnp.stack(s_c, axis=1).astype(dt),
            jnp.stack(s_n, axis=1).astype(dt), jnp.stack(s_m, axis=1).astype(dt))
```

```python
import functools
import math

import jax
import jax.numpy as jnp
from jax import lax
from jax.experimental import pallas as pl
from jax.experimental.pallas import tpu as pltpu

F32 = jnp.float32
BF16 = jnp.bfloat16

GRID_W = 64
H_A = 4
H_B = 4
R_ALPHA = 16
TAU_GLA = 16.0
CHUNK = 64
EPS = 1e-6
LANES = 128
COND_ROWS = 8
SMALL_W = LANES
GATE_LANE0 = 2 * R_ALPHA
VMEM_LIMIT = 56 * 1024 * 1024
SCAN_UNROLL = 4
PIPELINE_STARTS = 8
TOKEN_TILE = 512
FF_CHUNK = 512
GLA_FACTORED_DECAY_MAX = 60.0
MOD_SPLIT = 2
ADA_TILE = 1024


def _sigmoid(x):
    return 1.0 / (1.0 + jnp.exp(-x))


def _silu(x):
    return x * _sigmoid(x)


def _log_sigmoid(x):
    return jnp.minimum(x, 0.0) - jnp.log1p(jnp.exp(-jnp.abs(x)))


def _dot(a, b):
    return jnp.dot(a, b, preferred_element_type=F32)


def _dot_nt(a, b):
    return lax.dot_general(a, b, (((1,), (1,)), ((), ())), preferred_element_type=F32)


def _rms(x, w):
    return x * lax.rsqrt(jnp.mean(x * x, axis=-1, keepdims=True) + EPS) * w


def _tri_sum(tri, x):
    hi = x.astype(BF16)
    r1 = x - hi.astype(F32)
    mid = r1.astype(BF16)
    lo = (r1 - mid.astype(F32)).astype(BF16)
    return _dot(tri, hi) + _dot(tri, mid) + _dot(tri, lo)


def _chunk_masks(L):
    row = lax.broadcasted_iota(jnp.int32, (L, L), 0)
    col = lax.broadcasted_iota(jnp.int32, (L, L), 1)
    lower = row >= col
    upper = row <= col
    return lower, upper


def _ada_tile(cc_ref, c_ref, w_ref, b_ref, o_ref):
    D = cc_ref.shape[1]
    sub = lax.broadcasted_iota(jnp.int32, (COND_ROWS, D), 0)
    cond = jnp.where(sub == 0, cc_ref[...], 0.0)
    for r in range(c_ref.shape[0]):
        cond = jnp.where(sub == 1 + r, c_ref[r:r + 1, :], cond)
    o_ref[...] = _dot(_silu(cond).astype(BF16), w_ref[...].astype(BF16)) + b_ref[...]


def _ada_call(cc, c, w_ada, b_ada, n_cols):
    D = cc.shape[1]
    tn = min(n_cols, ADA_TILE)
    return pl.pallas_call(
        _ada_tile,
        grid=(n_cols // tn,),
        in_specs=[
            pl.BlockSpec(cc.shape, lambda j: (0, 0)),
            pl.BlockSpec(c.shape, lambda j: (0, 0)),
            pl.BlockSpec((D, tn), lambda j: (0, j)),
            pl.BlockSpec((1, tn), lambda j: (0, j)),
        ],
        out_specs=pl.BlockSpec((COND_ROWS, tn), lambda j: (0, j)),
        out_shape=jax.ShapeDtypeStruct((COND_ROWS, n_cols), F32),
        compiler_params=pltpu.CompilerParams(dimension_semantics=("arbitrary",),
                                             vmem_limit_bytes=VMEM_LIMIT),
        name="ada_mod",
    )(cc, c, w_ada, b_ada)


def _tile_group(n_ctx, tiles_per_req):
    i = pl.program_id(0)
    is_ctx = i < n_ctx
    row = jnp.where(is_ctx, 0, 1 + jnp.maximum(i - n_ctx, 0) // tiles_per_req)
    return is_ctx, row


def _ctx_tile(n_ctx):
    return lambda i: (jnp.minimum(i, n_ctx - 1), 0)


def _lat_tile(n_ctx):
    return lambda i: (jnp.maximum(i - n_ctx, 0), 0)


def _inproj_kernel(xc_ref, xl_ref, mod_ref, nw_ref, wt_ref, z_ref, wb_scr, *, n_ctx, tiles_per_req, big_rows,
                   small_rows):
    D = xc_ref.shape[1]

    @pl.when(pl.program_id(0) == 0)
    def _():
        col = 0
        for r0, n in big_rows:
            for k in range(n // LANES):
                blk = wt_ref[r0 + k * LANES:r0 + (k + 1) * LANES, :]
                wb_scr[:, col:col + LANES] = blk.T.astype(BF16)
                col += LANES
        parts = [wt_ref[r0:r0 + n, :] for r0, n in small_rows]
        n_small = sum(n for _, n in small_rows)
        parts.append(jnp.zeros((SMALL_W - n_small, D), F32))
        wb_scr[:, col:col + SMALL_W] = jnp.concatenate(parts, axis=0).T.astype(BF16)

    is_ctx, row = _tile_group(n_ctx, tiles_per_req)

    def tile(x_ref):
        sh1 = mod_ref[pl.ds(row, 1), 0:D]
        sc1 = mod_ref[pl.ds(row, 1), D:2 * D]
        h = _rms(x_ref[...], nw_ref[...]) * (1.0 + sc1) + sh1
        z_ref[...] = _dot(h.astype(BF16), wb_scr[...])

    @pl.when(is_ctx)
    def _():
        tile(xc_ref)

    @pl.when(jnp.logical_not(is_ctx))
    def _():
        tile(xl_ref)


def _inproj_call(xc2d, xl2d, mod, norm_w, w_in_t, *, tm, tiles_per_req, big_rows, small_rows):
    (Mc, D), Ml = xc2d.shape, xl2d.shape[0]
    n_ctx = Mc // tm
    n_out = sum(n for _, n in big_rows) + SMALL_W
    kern = functools.partial(_inproj_kernel, n_ctx=n_ctx, tiles_per_req=tiles_per_req,
                             big_rows=big_rows, small_rows=small_rows)
    return pl.pallas_call(
        kern,
        grid=((Mc + Ml) // tm,),
        in_specs=[
            pl.BlockSpec((tm, D), _ctx_tile(n_ctx)),
            pl.BlockSpec((tm, D), _lat_tile(n_ctx)),
            pl.BlockSpec(mod.shape, lambda i: (0, 0)),
            pl.BlockSpec((1, D), lambda i: (0, 0)),
            pl.BlockSpec(w_in_t.shape, lambda i: (0, 0), pipeline_mode=pl.Buffered(1)),
        ],
        out_specs=pl.BlockSpec((tm, n_out), lambda i: (i, 0)),
        out_shape=jax.ShapeDtypeStruct((Mc + Ml, n_out), F32),
        scratch_shapes=[pltpu.VMEM((D, n_out), BF16)],
        compiler_params=pltpu.CompilerParams(dimension_semantics=("arbitrary",),
                                             vmem_limit_bytes=VMEM_LIMIT),
        name="norm_inproj",
    )(xc2d, xl2d, mod, norm_w.reshape(1, D), w_in_t)


def _chunk_loop(n_chunks, unroll, make_units):
    def step(ns):
        pending = list(make_units(ns))
        active = []
        while pending or active:
            for _ in range(min(PIPELINE_STARTS, len(pending))):
                active.append(pending.pop(0))
            alive = []
            for g in active:
                try:
                    next(g)
                    alive.append(g)
                except StopIteration:
                    pass
            active = alive

    if unroll >= n_chunks:
        step(list(range(n_chunks)))
        return

    def body(i, carry):
        step([i * unroll + u for u in range(unroll)])
        return carry

    lax.fori_loop(0, n_chunks // unroll, body, 0)


def _chunk_rows(n):
    if isinstance(n, int):
        return pl.ds(n * CHUNK, CHUNK)
    return pl.ds(pl.multiple_of(n * CHUNK, CHUNK), CHUNK)


def _cast_specs(casts, n_steps):
    in_specs, out_specs, out_shape, args = [], [], [], []
    for w, axis in casts:
        blk = list(w.shape)
        assert blk[axis] % n_steps == 0
        blk[axis] //= n_steps
        assert blk[0] % 16 == 0 and blk[1] % LANES == 0
        idx = (lambda b: (b, 0)) if axis == 0 else (lambda b: (0, b))
        in_specs.append(pl.BlockSpec(tuple(blk), idx))
        out_specs.append(pl.BlockSpec(tuple(blk), idx))
        out_shape.append(jax.ShapeDtypeStruct(w.shape, BF16))
        args.append(w)
    return in_specs, out_specs, out_shape, args


def _gla_body(q_ref, k_ref, v_ref, g_ref, sm_ref, s0_ref, wal_ref, bal_ref, gw_ref, out_ref, snew_ref,
              st_scr, sall_scr, qh_scr, qs_scr, kh_scr):
    has_state = s0_ref is not None
    write_state = snew_ref is not None
    T = q_ref.shape[0]
    L = CHUNK
    N = T // L
    HK = q_ref.shape[1]
    DK = HK // H_A
    DV = v_ref.shape[1] // H_A
    scale = DK ** -0.5
    n_pairs = HK // LANES

    lower, upper = _chunk_masks(L)
    tri = (lower.astype(BF16), upper.astype(BF16))
    tmask = (lower, upper)
    lane = lax.broadcasted_iota(jnp.int32, (1, LANES), 1)
    head_mask = (lane < DK, lane >= DK)

    for d in range(2):
        for p in range(n_pairs):
            if has_state:
                st_scr[d, p] = s0_ref[d, p].T
            else:
                st_scr[d, p] = jnp.zeros((LANES, LANES), F32)

    def decay_pre(d, r):
        return _dot(sm_ref[r, :].astype(BF16), wal_ref[:, d * HK:(d + 1) * HK]) + bal_ref[d:d + 1, :]

    neg_pre = jnp.maximum(-(_dot(sm_ref[...].astype(BF16), wal_ref[...])
                            + jnp.concatenate([bal_ref[0:1, :], bal_ref[1:2, :]], axis=1)), 0.0)
    chunk_sums = jnp.sum(neg_pre.reshape(N, L, 2 * HK), axis=1)
    decay_span = (jnp.max(chunk_sums) + L * math.log(2.0)) * (1.0 / TAU_GLA)

    def state_group(ns, dirs=(0, 1)):
        units = [(d, n if d == 0 else N - 1 - n) for n in ns for d in dirs]
        rows = [_chunk_rows(n) for _, n in units]
        vt_all = [[jnp.concatenate([v_ref[r, (2 * p + j) * DV:(2 * p + j + 1) * DV] for j in range(2)],
                                   axis=0).T.astype(BF16) for p in range(n_pairs)] for r in rows]
        yield
        pre = [decay_pre(d, r) for (d, _), r in zip(units, rows)]
        yield
        g = [_log_sigmoid(x) * (1.0 / TAU_GLA) for x in pre]
        yield
        b = [_tri_sum(tri[d], gi) for (d, _), gi in zip(units, g)]
        yield
        ks_all, dec_all = [], []
        for (d, _), r, bi in zip(units, rows, b):
            bend = bi[L - 1:L, :] if d == 0 else bi[0:1, :]
            q = q_ref[r, :] * scale
            ks = (k_ref[r, :] * jnp.exp(bend - bi)).astype(BF16)
            qh_scr[d, r, :] = (q * jnp.exp(bi - bend)).astype(BF16)
            qs_scr[d, r, :] = (q * jnp.exp(bi)).astype(BF16)
            kh_scr[d, r, :] = ks
            ks_all.append(ks)
            dec_all.append(jnp.exp(bend))
        yield
        upd_all = []
        for vt_u, ks in zip(vt_all, ks_all):
            upd_u = []
            for p in range(n_pairs):
                kp = ks[:, p * LANES:(p + 1) * LANES]
                kk = jnp.concatenate([jnp.where(head_mask[j], kp, jnp.zeros_like(kp)) for j in range(2)], axis=0)
                upd_u.append(_dot(vt_u[p], kk))
            upd_all.append(upd_u)
        yield
        st = {d: [st_scr[d, p] for p in range(n_pairs)] for d in dirs}
        for (d, n), dec, upd in zip(units, dec_all, upd_all):
            for p in range(n_pairs):
                sall_scr[d, n, p] = st[d][p].astype(BF16)
                st[d][p] = st[d][p] * dec[:, p * LANES:(p + 1) * LANES] + upd[p]
        for d in dirs:
            for p in range(n_pairs):
                st_scr[d, p] = st[d][p]

    def stack_heads(x):
        return jnp.concatenate([jnp.where(head_mask[j], x, jnp.zeros_like(x)) for j in range(2)], axis=0)

    tok = lax.broadcasted_iota(jnp.int32, (L, 1), 0)
    row_t = lax.broadcasted_iota(jnp.int32, (2 * L, L), 0) & (L - 1)
    col_s = lax.broadcasted_iota(jnp.int32, (2 * L, L), 1)

    def exact_scores(d, r, p):
        ls = slice(p * LANES, (p + 1) * LANES)
        b = _tri_sum(tri[d], _log_sigmoid(decay_pre(d, r)[:, ls]) * (1.0 / TAU_GLA))
        q = q_ref[r, ls] * scale
        k = k_ref[r, ls]
        acc = jnp.where(row_t == col_s, _dot_nt(stack_heads(q).astype(BF16), k.astype(BF16)), 0.0)
        src = lax.broadcasted_iota(jnp.int32, (L, L), 1)
        h = L // 2
        while h >= 1:
            first = tok & ~(2 * h - 1)
            edge = first + (h - 1 if d == 0 else h)
            b_edge = _tri_sum((src == edge).astype(BF16), b)
            upper = (tok & (2 * h - 1)) >= h
            later, earlier = (upper, ~upper) if d == 0 else (~upper, upper)
            qt = jnp.where(later, q * jnp.exp(b - b_edge), 0.0)
            kt = jnp.where(earlier, k * jnp.exp(b_edge - b), 0.0)
            sc = _dot_nt(stack_heads(qt).astype(BF16), kt.astype(BF16))
            acc = acc + jnp.where((row_t & ~(2 * h - 1)) == (col_s & ~(2 * h - 1)), sc, 0.0)
            h //= 2
        return acc

    def out_group(ns, exact_decay=False):
        pairs = [(d, ni, p) for ni in range(len(ns)) for d in range(2) for p in range(n_pairs)]
        scores, inter = [], []
        for d, ni, p in pairs:
            r = _chunk_rows(ns[ni])
            ls = slice(p * LANES, (p + 1) * LANES)
            if exact_decay:
                scores.append(exact_scores(d, r, p))
            else:
                scores.append(_dot_nt(stack_heads(qh_scr[d, r, ls]), kh_scr[d, r, ls]))
            inter.append(_dot_nt(stack_heads(qs_scr[d, r, ls]), sall_scr[d, ns[ni], p]))
        yield
        probs = [[jnp.where(tmask[d], sc[j * L:(j + 1) * L, :], 0.0).astype(BF16) for j in range(2)]
                 for (d, _, _), sc in zip(pairs, scores)]
        yield
        outs = {}
        for (d, ni, p), pr, it in zip(pairs, probs, inter):
            r = _chunk_rows(ns[ni])
            for j in range(2):
                vs = slice((2 * p + j) * DV, (2 * p + j + 1) * DV)
                outs[(d, ni, 2 * p + j)] = _dot(pr[j], v_ref[r, vs].astype(BF16)) + it[j * L:(j + 1) * L, :]
        yield
        for ni, n in enumerate(ns):
            r = _chunk_rows(n)
            for h in range(H_A):
                vs = slice(h * DV, (h + 1) * DV)
                o = outs[(0, ni, h)] + outs[(1, ni, h)]
                out_ref[r, vs] = (_rms(o, gw_ref[:, vs]) * _silu(g_ref[r, vs])).astype(out_ref.dtype)

    def finish():
        if write_state:
            for d in range(2):
                for p in range(n_pairs):
                    snew_ref[d, p] = st_scr[d, p].T

    return state_group, out_group, finish, decay_span


def _gla_scratch(T, HK):
    n_pairs = HK // LANES
    n_chunks = T // CHUNK
    return [
        pltpu.VMEM((2, n_pairs, LANES, LANES), F32),
        pltpu.VMEM((2, n_chunks, n_pairs, LANES, LANES), BF16),
        pltpu.VMEM((2, T, HK), BF16),
        pltpu.VMEM((2, T, HK), BF16),
        pltpu.VMEM((2, T, HK), BF16),
    ]


def _mlstm_body(qk_ref, v_ref, og_ref, sm_ref, c0_ref, n0_ref, m0_ref, cw_ref, bm_ref, gw_ref,
                out_ref, cnew_ref, nnew_ref, mnew_ref,
                pad_scr, qk_scr, y_scr, c_scr, n_scr, m_scr, call_scr, nall_scr, mall_scr, g_scr, f_scr,
                *, grid_w):
    has_state = c0_ref is not None
    write_state = cnew_ref is not None
    T = qk_ref.shape[0]
    L = CHUNK
    N = T // L
    C2 = qk_ref.shape[1]
    HK = C2 // 2
    DK = HK // H_B
    DV = v_ref.shape[1] // H_B
    scale = DK ** -0.5
    n_pairs = HK // LANES
    P = pad_scr.shape[0] - T
    P0 = P // 2
    rows_img = T // grid_w

    lower, upper = _chunk_masks(L)
    tri = (lower.astype(BF16), upper.astype(BF16))
    tmask = (lower, upper)
    lane = lax.broadcasted_iota(jnp.int32, (1, LANES), 1)
    head_mask = (lane < DK, lane >= DK)
    lane_in = lane & (L - 1)

    def lane_cummax(x, d):
        k = 1
        while k < L:
            if d == 0:
                x = jnp.maximum(x, jnp.where(lane_in >= k, pltpu.roll(x, k, axis=1), -jnp.inf))
            else:
                x = jnp.maximum(x, jnp.where(lane_in < L - k, pltpu.roll(x, LANES - k, axis=1), -jnp.inf))
            k *= 2
        return x

    for d in range(2):
        for p in range(n_pairs):
            if has_state:
                c_scr[d, p] = c0_ref[d, p]
                n_scr[2 * d + p:2 * d + p + 1, :] = jnp.concatenate(
                    [n0_ref[d, 2 * p + j:2 * p + j + 1, :] for j in range(2)], axis=1)
            else:
                c_scr[d, p] = jnp.zeros((LANES, LANES), F32)
                n_scr[2 * d + p:2 * d + p + 1, :] = jnp.zeros((1, LANES), F32)
    eye_h = (lax.broadcasted_iota(jnp.int32, (H_B, H_B), 0) == lax.broadcasted_iota(jnp.int32, (H_B, H_B), 1))

    def to_col(row):
        return jnp.sum(jnp.where(eye_h, row, 0.0), axis=1, keepdims=True)

    def to_row(col):
        return jnp.sum(jnp.where(eye_h, col, 0.0), axis=0, keepdims=True)

    for d in range(2):
        if has_state:
            m_scr[H_B * d:H_B * (d + 1), 0:1] = to_col(m0_ref[d:d + 1, :])
        else:
            m_scr[H_B * d:H_B * (d + 1), 0:1] = jnp.zeros((H_B, 1), F32)

    pad_scr[0:P0, :] = jnp.zeros((P0, C2), F32)
    pad_scr[P0 + T:P + T, :] = jnp.zeros((P - P0, C2), F32)

    def copy_in(i, carry):
        r0 = pl.multiple_of(i * L, L)
        pad_scr[pl.ds(P0 + r0, L), :] = qk_ref[pl.ds(r0, L), :]
        return carry

    lax.fori_loop(0, N, copy_in, 0)

    lane_c = lax.broadcasted_iota(jnp.int32, (1, C2), 1)
    qscale = jnp.where(lane_c < HK, scale, 1.0).astype(F32)
    sub = lax.broadcasted_iota(jnp.int32, (L, 1), 0)
    img_rows = (0,) if rows_img == 1 else (-1, 0, 1)

    def conv_tile(i, carry):
        r0 = pl.multiple_of(i * L, L)
        col = lax.rem(r0, grid_w) + sub
        ok_left = col >= 1
        ok_right = col <= grid_w - 2
        sums = [None, None, None]
        for di in img_rows:
            blk = pad_scr[pl.ds(P0 + r0 + di * grid_w - 8, L + 16), :]
            for k in range(3):
                term = blk * cw_ref[di + 1, k:k + 1, :]
                sums[k] = term if sums[k] is None else sums[k] + term
        acc = (sums[1][8:8 + L, :] + jnp.where(ok_left, sums[0][7:7 + L, :], 0.0)
               + jnp.where(ok_right, sums[2][9:9 + L, :], 0.0))
        qk_scr[pl.ds(r0, L), :] = _silu(acc) * qscale
        return carry

    lax.fori_loop(0, N, conv_tile, 0)

    gl = lane - GATE_LANE0
    is_f = ((gl >= H_B) & (gl < 2 * H_B)) | ((gl >= 3 * H_B) & (gl < 4 * H_B))

    def gate_tile(i, carry):
        rows = pl.ds(pl.multiple_of(i * L, L), L)
        x = sm_ref[rows, :] + bm_ref[...]
        y_scr[rows, :] = jnp.where(is_f, _log_sigmoid(x), x)
        return carry

    lax.fori_loop(0, N, gate_tile, 0)


    def state_group(ns, dirs=(0, 1)):
        units = [(d, n if d == 0 else N - 1 - n) for n in ns for d in dirs]
        rows = [_chunk_rows(n) for _, n in units]
        kt_all = [[qk_scr[r, HK + p * LANES:HK + (p + 1) * LANES].T for p in range(n_pairs)] for r in rows]
        yield
        xs = [y_scr[r, :] for r in rows]
        fsum = [_tri_sum(tri[d], x) for (d, _), x in zip(units, xs)]
        yield
        wk_all, f_end, c_end = [], [], []
        for (d, n), r, x, fs in zip(units, rows, xs, fsum):
            y = jnp.where(is_f, fs, x)
            li0 = GATE_LANE0 + 2 * H_B * d
            blk = jnp.concatenate([y, y], axis=0).T[li0:li0 + 2 * H_B, :]
            frow = pltpu.roll(blk, H_B, axis=0)
            grow = blk - frow
            g_scr[d, n] = grow
            f_scr[d, n] = frow
            e_col = L - 1 if d == 0 else 0
            f_end.append(frow[0:H_B, e_col:e_col + 1])
            ce8 = jnp.max(grow, axis=1, keepdims=True)
            c_end.append(ce8[0:H_B, :])
            wk_all.append(jnp.exp(grow[:, 0:L] - ce8))
        yield
        kv_all, ksum_all = [], []
        for r, wk8, kt_u in zip(rows, wk_all, kt_all):
            kv_u, ks_u = [], []
            wk8b = wk8.astype(BF16)
            for p in range(n_pairs):
                kpb = qk_scr[r, HK + p * LANES:HK + (p + 1) * LANES].astype(BF16)
                ks8 = _dot(wk8b, kpb)
                for j in range(2):
                    h = 2 * p + j
                    kwt = (kt_u[p][j * DK:(j + 1) * DK, :] * wk8[h:h + 1, :]).astype(BF16)
                    kv_u.append(_dot(kwt, v_ref[r, h * DV:(h + 1) * DV].astype(BF16)))
                    ks_u.append(ks8[h:h + 1, :])
            kv_all.append(kv_u)
            ksum_all.append(ks_u)
        yield
        m_run = {d: m_scr[H_B * d:H_B * (d + 1), 0:1] for d in dirs}
        a_all, b_all = [], []
        for (d, n), fe, ce in zip(units, f_end, c_end):
            mall_scr[d, n, 0:H_B, 0:1] = m_run[d]
            mx = jnp.maximum(m_run[d], ce)
            a_all.append(jnp.exp(m_run[d] - mx))
            b_all.append(jnp.exp(ce - mx))
            m_run[d] = fe + mx
        for d in dirs:
            m_scr[H_B * d:H_B * (d + 1), 0:1] = m_run[d]
        yield
        c_run = {d: [[c_scr[d, p, j * DK:(j + 1) * DK, :] for j in range(2)] for p in range(n_pairs)] for d in dirs}
        n_run = {d: [n_scr[2 * d + p:2 * d + p + 1, :] for p in range(n_pairs)] for d in dirs}
        for (d, n), a4, b4, kv_u, ks_u in zip(units, a_all, b_all, kv_all, ksum_all):
            for p in range(n_pairs):
                nall_scr[d, n, p:p + 1, :] = n_run[d][p]
                a_s = [a4[2 * p + j:2 * p + j + 1, :] for j in range(2)]
                b_s = [b4[2 * p + j:2 * p + j + 1, :] for j in range(2)]
                for j in range(2):
                    cj = c_run[d][p][j]
                    call_scr[d, n, p, j * DK:(j + 1) * DK, :] = cj.astype(BF16)
                    c_run[d][p][j] = a_s[j] * cj + b_s[j] * kv_u[2 * p + j]
                n_run[d][p] = (jnp.where(head_mask[0], a_s[0], a_s[1]) * n_run[d][p]
                               + jnp.where(head_mask[0], b_s[0] * ks_u[2 * p], b_s[1] * ks_u[2 * p + 1]))
        for d in dirs:
            for p in range(n_pairs):
                n_scr[2 * d + p:2 * d + p + 1, :] = n_run[d][p]
                for j in range(2):
                    c_scr[d, p, j * DK:(j + 1) * DK, :] = c_run[d][p][j]

    eye = lower & upper
    ones8 = jnp.ones((8, L), BF16)
    sub8 = lax.broadcasted_iota(jnp.int32, (8, LANES), 0)
    sub_h = lax.broadcasted_iota(jnp.int32, (H_B, L), 0)
    n_rows = [((sub8 == 2 * p) & head_mask[0]) | ((sub8 == 2 * p + 1) & head_mask[1]) for p in range(n_pairs)]

    def head_rows(vals):
        out = vals[0][0:H_B, :]
        for h in range(1, H_B):
            out = jnp.where(sub_h == h, vals[h][0:H_B, :], out)
        return out

    def out_group(ns):
        chunks = [(d, n) for n in ns for d in range(2)]
        pairs = [(d, n, p) for d, n in chunks for p in range(n_pairs)]
        units = [(d, n, p, j) for d, n, p in pairs for j in range(2)]
        cms = [lane_cummax(g_scr[d, n], d)[0:H_B, 0:L] for d, n in chunks]
        qk2s, qc2s, qn2s = [], [], []
        for d, n, p in pairs:
            r = _chunk_rows(n)
            qp = qk_scr[r, p * LANES:(p + 1) * LANES]
            q2 = jnp.concatenate([jnp.where(head_mask[j], qp, 0.0) for j in range(2)], axis=0).astype(BF16)
            qk2s.append(_dot_nt(q2, qk_scr[r, HK + p * LANES:HK + (p + 1) * LANES].astype(BF16)))
            qc2s.append(_dot(q2, call_scr[d, n, p]))
            nsel = jnp.where(n_rows[p], nall_scr[d, n, p:p + 1, :], 0.0).astype(BF16)
            qn2s.append(_dot_nt(nsel, qp.astype(BF16)))
        yield
        s_all = []
        for ui, (d, n, p, j) in enumerate(units):
            grow = g_scr[d, n, 2 * p + j:2 * p + j + 1, 0:L]
            e = jnp.where(tmask[d], grow, -jnp.inf)
            cmax = jnp.max(e, axis=-1, keepdims=True)
            s_all.append((qk2s[ui // 2][j * L:(j + 1) * L, :] * jnp.exp(e - cmax)).astype(BF16))
        yield
        nums =[_dot(s, v_ref[_chunk_rows(n), (2 * p + j) * DV:(2 * p + j + 1) * DV].astype(BF16))
                for (d, n, p, j), s in zip(units, s_all)]
        dens = [_dot_nt(ones8, s) for s in s_all]
        yield
        scales = []
        for ci, (d, n) in enumerate(chunks):
            den_loc = head_rows(dens[ci * H_B:(ci + 1) * H_B])
            qn = qn2s[ci * n_pairs][0:H_B, :]
            for p in range(1, n_pairs):
                qn = qn + qn2s[ci * n_pairs + p][0:H_B, :]
            cm = cms[ci]
            m_prev = mall_scr[d, n, 0:H_B, 0:1]
            delta = cm - m_prev
            t = jnp.exp(-jnp.abs(delta))
            w_loc = jnp.where(delta <= 0.0, t, 1.0)
            w_inter = jnp.where(delta <= 0.0, 1.0, t)
            mt = f_scr[d, n, 0:H_B, 0:L] + jnp.maximum(m_prev, cm)
            den = w_loc * den_loc + w_inter * qn
            rinv = 1.0 / jnp.maximum(jnp.abs(den), jnp.exp(-mt))
            scales.append((w_loc * rinv, w_inter * rinv))
        yield
        hs = []
        for ui, (d, n, p, j) in enumerate(units):
            h = 2 * p + j
            sc_loc, sc_inter = scales[ui // H_B]
            d_loc = jnp.where(eye, sc_loc[h:h + 1, :], 0.0).astype(BF16)
            d_inter = jnp.where(eye, sc_inter[h:h + 1, :], 0.0).astype(BF16)
            hs.append(_dot(d_loc, nums[ui].astype(BF16))
                      + _dot(d_inter, qc2s[ui // 2][j * L:(j + 1) * L, :].astype(BF16)))
        yield
        for ni, n in enumerate(ns):
            r = _chunk_rows(n)
            for h in range(H_B):
                vs = slice(h * DV, (h + 1) * DV)
                o = hs[(2 * ni) * H_B + h] + hs[(2 * ni + 1) * H_B + h]
                out_ref[r, vs] = (_rms(o, gw_ref[:, vs]) * _sigmoid(og_ref[r, vs])).astype(out_ref.dtype)

    def finish():
        if write_state:
            for d in range(2):
                for p in range(n_pairs):
                    cnew_ref[d, p] = c_scr[d, p]
                    for j in range(2):
                        nnew_ref[d, 2 * p + j:2 * p + j + 1, :] = n_scr[2 * d + p:2 * d + p + 1, j * DK:(j + 1) * DK]
                mnew_ref[d:d + 1, :] = to_row(m_scr[H_B * d:H_B * (d + 1), 0:1])

    return state_group, out_group, finish


def _mlstm_scratch(T, C2, grid_w):
    n_pairs = C2 // 2 // LANES
    n_chunks = T // CHUNK
    pad_rows = 2 * (grid_w + 8) if T // grid_w > 1 else 16
    return [
        pltpu.VMEM((T + pad_rows, C2), F32),
        pltpu.VMEM((T, C2), F32),
        pltpu.VMEM((T, SMALL_W), F32),
        pltpu.VMEM((2, n_pairs, LANES, LANES), F32),
        pltpu.VMEM((8, LANES), F32),
        pltpu.VMEM((8, LANES), F32),
        pltpu.VMEM((2, n_chunks, n_pairs, LANES, LANES), BF16),
        pltpu.VMEM((2, n_chunks, 8, LANES), F32),
        pltpu.VMEM((2, n_chunks, 8, LANES), F32),
        pltpu.VMEM((2, n_chunks, 8, LANES), F32),
        pltpu.VMEM((2, n_chunks, 8, LANES), F32),
    ]


N_GLA_SCRATCH = 5
N_MLSTM_SCRATCH = 11


def _scan_kernel(*refs, cols, layer, has_state, write_state, n_cast, ride_ada, grid_w, unroll):
    refs = list(refs)
    z_ref = refs.pop(0)
    s0_ref = c0_ref = n0_ref = m0_ref = None
    if has_state:
        s0_ref, c0_ref, n0_ref, m0_ref = refs[:4]
        del refs[:4]
    wa_ref, bal_ref, gwa_ref, cw_ref, bmg_ref, gwb_ref = refs[:6]
    del refs[:6]
    cast_in = refs[:n_cast]
    del refs[:n_cast]
    if ride_ada:
        ada_in = refs[:4]
        del refs[:4]
    outa_ref, outb_ref = refs[:2]
    del refs[:2]
    snew_ref = cnew_ref = nnew_ref = mnew_ref = None
    if write_state:
        snew_ref, cnew_ref, nnew_ref, mnew_ref = refs[:4]
        del refs[:4]
    cast_out = refs[:n_cast]
    del refs[:n_cast]
    if ride_ada:
        ada_out = refs.pop(0)
    wal_scr, bm_scr = refs[:2]
    del refs[:2]
    gla_scr = refs[:N_GLA_SCRATCH]
    mlstm_scr = refs[N_GLA_SCRATCH:]

    def cast_rider():
        for src, dst in zip(cast_in, cast_out):
            step = min(src.shape[0], 256)
            for r0 in range(0, src.shape[0], step):
                dst[r0:r0 + step, :] = src[r0:r0 + step, :].astype(BF16)
                yield

    def ada_rider():
        if ride_ada:
            _ada_tile(*ada_in, ada_out)
        yield

    n_chunks = z_ref.shape[0] // CHUNK
    ride_in_passes = unroll >= n_chunks
    if not ride_in_passes:
        for _ in cast_rider():
            pass
        for _ in ada_rider():
            pass

    R, HK = wa_ref.shape[1], wa_ref.shape[2]
    wal_scr[...] = jnp.zeros(wal_scr.shape, BF16)
    for d in range(2):
        wal_scr[d * R:(d + 1) * R, d * HK:(d + 1) * HK] = wa_ref[d].astype(BF16)
    lane = lax.broadcasted_iota(jnp.int32, (1, LANES), 1)
    bm = jnp.zeros((1, LANES), F32)
    for g in range(bmg_ref.shape[1]):
        for h in range(H_B):
            bm = jnp.where(lane == GATE_LANE0 + H_B * g + h, bmg_ref[layer, g, h], bm)
    bm_scr[0:1, :] = bm

    def view(name):
        c0, w = cols[name]
        return z_ref.at[:, pl.ds(c0, w)]

    sm_ref = view("small")
    gla = _gla_body(view("qa"), view("ka"), view("va"), view("ga"), sm_ref, s0_ref, wal_scr, bal_ref, gwa_ref,
                    outa_ref, snew_ref, *gla_scr)
    mlstm = _mlstm_body(view("qkb"), view("vb"), view("ob"), sm_ref, c0_ref, n0_ref, m0_ref, cw_ref,
                        bm_scr.at[0:1, :], gwb_ref, outb_ref, cnew_ref, nnew_ref, mnew_ref, *mlstm_scr,
                        grid_w=grid_w)
    gla_state, gla_out, gla_finish, decay_span = gla
    mlstm_state, mlstm_out, mlstm_finish = mlstm

    def passes(gla_out_fn):
        def state_units(ns):
            units = [fn([n], (d,)) for n in ns for d in range(2) for fn in (mlstm_state, gla_state)]
            return ([ada_rider()] if ride_in_passes else []) + units

        def out_units(ns):
            units = [fn([n]) for n in ns for fn in (mlstm_out, gla_out_fn)]
            return ([cast_rider()] if ride_in_passes else []) + units

        _chunk_loop(n_chunks, unroll, state_units)
        _chunk_loop(n_chunks, unroll, out_units)

    wide_decay = decay_span > GLA_FACTORED_DECAY_MAX

    @pl.when(jnp.logical_not(wide_decay))
    def _():
        passes(gla_out)

    @pl.when(wide_decay)
    def _():
        passes(functools.partial(gla_out, exact_decay=True))

    gla_finish()
    mlstm_finish()


def _scan_call(z2d, row0, B, T, states, lw, layer, *, grid_w, write_state, casts=(), ada=None):
    n_z = z2d.shape[1]
    assert row0 % T == 0 and z2d.shape[0] % T == 0
    z3 = z2d.reshape(z2d.shape[0] // T, T, n_z)
    blk0 = row0 // T
    HK = lw["w_alpha2"].shape[-1]
    DA = lw["gnorm_a_w"].shape[0]
    C2 = lw["conv_w"].shape[-1]
    DB = lw["gnorm_b_w"].shape[0]
    DK_A, DK_B = HK // H_A, C2 // 2 // H_B
    pa, pb = HK // LANES, C2 // 2 // LANES
    n_chunks = T // CHUNK
    has_state = states is not None
    widths = (("qa", HK), ("ka", HK), ("va", DA), ("ga", DA), ("qkb", C2), ("vb", DB), ("ob", DB),
              ("small", SMALL_W))
    cols, c0 = {}, 0
    for name, w in widths:
        cols[name] = (c0, w)
        c0 += w
    assert c0 == n_z
    cast_in_specs, cast_out_specs, cast_out_shape, cast_args = _cast_specs(casts, B)
    kern = functools.partial(_scan_kernel, cols=cols, layer=layer, has_state=has_state, write_state=write_state,
                             n_cast=len(casts), ride_ada=ada is not None, grid_w=grid_w,
                             unroll=min(n_chunks, SCAN_UNROLL))

    def per_batch(shape):
        nd = len(shape)
        return pl.BlockSpec((None,) + tuple(shape), lambda b: (b,) + (0,) * nd)

    def per_batch_layer(shape):
        nd = len(shape)
        return pl.BlockSpec((None, None) + tuple(shape), lambda b: (b, layer) + (0,) * nd)

    def of_layer(a):
        return pl.BlockSpec((None,) + a.shape[1:], lambda b: (layer,) + (0,) * (a.ndim - 1))

    def whole(a):
        return pl.BlockSpec(a.shape, lambda b: (0,) * a.ndim)

    state_shapes = ((2, pa, LANES, LANES), (2, pb, LANES, LANES), (2, H_B, DK_B), (2, H_B))
    in_specs = [pl.BlockSpec((None, T, n_z), lambda b: (b + blk0, 0, 0))]
    args = [z3]
    if has_state:
        s_gla, s_c, s_n, s_m = states
        depth = s_gla.shape[1]
        args += [s_gla.reshape((B, depth) + state_shapes[0]), s_c.reshape((B, depth) + state_shapes[1]), s_n, s_m]
        in_specs += [per_batch_layer(s) for s in state_shapes]
    args += [lw["w_alpha2"], lw["b_alpha"], lw["gnorm_a_w"].reshape(1, DA), lw["conv_w"], lw["b_mgate"],
             lw["gnorm_b_w"].reshape(1, DB)]
    in_specs += [of_layer(lw["w_alpha2"]), of_layer(lw["b_alpha"]), pl.BlockSpec((1, DA), lambda b: (0, 0)),
                 whole(lw["conv_w"]), pl.BlockSpec(memory_space=pltpu.SMEM), pl.BlockSpec((1, DB), lambda b: (0, 0))]
    args += cast_args
    in_specs += cast_in_specs
    out_specs = [per_batch((T, DA)), per_batch((T, DB))]
    out_shape = [jax.ShapeDtypeStruct((B, T, DA), BF16), jax.ShapeDtypeStruct((B, T, DB), BF16)]
    if write_state:
        out_specs += [per_batch(s) for s in state_shapes]
        out_shape += [jax.ShapeDtypeStruct((B,) + s, F32) for s in state_shapes]
    out_specs += cast_out_specs
    out_shape += cast_out_shape
    if ada is not None:
        cc, c, w_ada, b_ada, col0 = ada
        n_rest = w_ada.shape[1] - col0
        wcol = n_rest // B
        assert n_rest % B == 0 and wcol % LANES == 0 and col0 % wcol == 0
        args += [cc, c, w_ada, b_ada]
        in_specs += [whole(cc), whole(c),
                     pl.BlockSpec((w_ada.shape[0], wcol), lambda b: (0, col0 // wcol + b)),
                     pl.BlockSpec((1, wcol), lambda b: (0, col0 // wcol + b))]
        out_specs.append(pl.BlockSpec((COND_ROWS, wcol), lambda b: (0, b)))
        out_shape.append(jax.ShapeDtypeStruct((COND_ROWS, n_rest), F32))
    scratch = ([pltpu.VMEM((SMALL_W, 2 * HK), BF16), pltpu.VMEM((8, LANES), F32)]
               + _gla_scratch(T, HK) + _mlstm_scratch(T, C2, grid_w))
    assert len(scratch) == 2 + N_GLA_SCRATCH + N_MLSTM_SCRATCH
    return pl.pallas_call(
        kern,
        grid=(B,),
        in_specs=in_specs,
        out_specs=out_specs,
        out_shape=out_shape,
        scratch_shapes=scratch,
        compiler_params=pltpu.CompilerParams(dimension_semantics=("arbitrary",),
                                             vmem_limit_bytes=VMEM_LIMIT),
        name="mixer_scans",
    )(*args)


def _outff_kernel(xc_ref, xl_ref, ac_ref, al_ref, bc_ref, bl_ref, mod_ref, n2_ref, fn_ref, wo_ref, w1_ref, w2_ref,
                  yc_ref, yl_ref, *, n_ctx, tiles_per_req, ff_chunk, final_norm):
    D = xc_ref.shape[1]
    DA = ac_ref.shape[1]
    is_ctx, row = _tile_group(n_ctx, tiles_per_req)

    def mod(k):
        return mod_ref[pl.ds(row, 1), (k - MOD_SPLIT) * D:(k - MOD_SPLIT + 1) * D]

    def tile(x_ref, a_ref, b_ref, y_ref):
        y = _dot(a_ref[...], wo_ref[0:DA, :]) + _dot(b_ref[...], wo_ref[DA:, :])
        x1 = x_ref[...] + mod(2) * y
        h2 = (_rms(x1, n2_ref[...]) * (1.0 + mod(4)) + mod(3)).astype(BF16)
        acc = jnp.zeros(x1.shape, F32)
        for c0 in range(0, w1_ref.shape[1], ff_chunk):
            u = jnp.maximum(_dot(h2, w1_ref[:, c0:c0 + ff_chunk]), 0.0)
            acc = acc + _dot((u * u).astype(BF16), w2_ref[c0:c0 + ff_chunk, :])
        x2 = x1 + mod(5) * acc
        y_ref[...] = _rms(x2, fn_ref[...]) if final_norm else x2

    @pl.when(is_ctx)
    def _():
        tile(xc_ref, ac_ref, bc_ref, yc_ref)

    @pl.when(jnp.logical_not(is_ctx))
    def _():
        tile(xl_ref, al_ref, bl_ref, yl_ref)


def _outff_call(xc2d, xl2d, ac, al, bc, bl, mod, norm2_w, final_w, wo, w1, w2, *, tm, tiles_per_req, final_norm):
    (Mc, D), Ml = xc2d.shape, xl2d.shape[0]
    n_ctx = Mc // tm
    DA = ac.shape[1]
    DFF = w1.shape[1]
    kern = functools.partial(_outff_kernel, n_ctx=n_ctx, tiles_per_req=tiles_per_req, ff_chunk=FF_CHUNK,
                             final_norm=final_norm)
    once = pl.Buffered(1)
    ctx, lat = _ctx_tile(n_ctx), _lat_tile(n_ctx)
    return pl.pallas_call(
        kern,
        grid=((Mc + Ml) // tm,),
        in_specs=[
            pl.BlockSpec((tm, D), ctx), pl.BlockSpec((tm, D), lat),
            pl.BlockSpec((tm, DA), ctx), pl.BlockSpec((tm, DA), lat),
            pl.BlockSpec((tm, D - DA), ctx), pl.BlockSpec((tm, D - DA), lat),
            pl.BlockSpec(mod.shape, lambda i: (0, 0)),
            pl.BlockSpec((1, D), lambda i: (0, 0)),
            pl.BlockSpec((1, D), lambda i: (0, 0)),
            pl.BlockSpec((D, D), lambda i: (0, 0), pipeline_mode=once),
            pl.BlockSpec((D, DFF), lambda i: (0, 0), pipeline_mode=once),
            pl.BlockSpec((DFF, D), lambda i: (0, 0), pipeline_mode=once),
        ],
        out_specs=[pl.BlockSpec((tm, D), ctx), pl.BlockSpec((tm, D), lat)],
        out_shape=[jax.ShapeDtypeStruct((Mc, D), F32), jax.ShapeDtypeStruct((Ml, D), F32)],
        compiler_params=pltpu.CompilerParams(dimension_semantics=("arbitrary",),
                                             vmem_limit_bytes=VMEM_LIMIT),
        name="outproj_mlp",
    )(xc2d, xl2d, ac, al, bc, bl, mod, norm2_w.reshape(1, D), final_w.reshape(1, D), wo, w1, w2)


def _layer(xc, xl, cond, ada_w, cached, lw, layer, ffw, final_w, final_norm):
    (Bc, Tc, D), (Bl, Tl, _) = xc.shape, xl.shape
    tm = TOKEN_TILE
    assert (Bc * Tc) % tm == 0 and Tl % tm == 0 and (Bc * Tc) % Tl == 0
    xc2d, xl2d = xc.reshape(Bc * Tc, D), xl.reshape(Bl * Tl, D)
    mod_in = _ada_call(*cond, *ada_w, MOD_SPLIT * D)
    z = _inproj_call(xc2d, xl2d, mod_in, lw["norm1_w"], lw["w_in_t"], tm=tm, tiles_per_req=Tl // tm,
                     big_rows=lw["big_rows"], small_rows=lw["small_rows"])
    res_c = _scan_call(z, 0, Bc, Tc, None, lw, layer, grid_w=Tc, write_state=True,
                       casts=((ffw[0], 0), (ffw[1], 1), (ffw[2], 0)), ada=(*cond, *ada_w, MOD_SPLIT * D))
    res_l = _scan_call(z, Bc * Tc, Bl, Tl, cached, lw, layer, grid_w=GRID_W, write_state=False)
    wo_b, w1_b, w2_b, mod_out = res_c[-4:]
    yc, yl = _outff_call(xc2d, xl2d, res_c[0].reshape(Bc * Tc, -1), res_l[0].reshape(Bl * Tl, -1),
                         res_c[1].reshape(Bc * Tc, -1), res_l[1].reshape(Bl * Tl, -1), mod_out, lw["norm2_w"],
                         final_w, wo_b, w1_b, w2_b, tm=tm, tiles_per_req=Tl // tm, final_norm=final_norm)
    return yc.reshape(Bc, Tc, D), yl.reshape(Bl, Tl, D), tuple(res_c[2:6])


def _layer_weights(l, norm1_w, norm2_w, w_in, w_alpha2, b_alpha, b_mgate, conv_w, gnorm_a_w, gnorm_b_w):
    hk_a = w_alpha2.shape[-1]
    d_a = gnorm_a_w.shape[-1]
    d_b = gnorm_b_w.shape[-1]
    hk_b = conv_w.shape[-1] // 2
    sizes = (hk_a, hk_a, d_a, d_a, 2 * R_ALPHA, hk_b, hk_b, d_b, d_b, 4 * H_B)
    assert w_alpha2.shape[2] == R_ALPHA and b_mgate.shape[1] * b_mgate.shape[2] == 4 * H_B
    offs = [0]
    for s in sizes:
        offs.append(offs[-1] + s)
    big_rows = ((offs[0], offs[4] - offs[0]), (offs[5], offs[9] - offs[5]))
    small_rows = ((offs[4], offs[5] - offs[4]), (offs[9], offs[10] - offs[9]))
    assert all(n % LANES == 0 and r % 16 == 0 for r, n in big_rows)
    return dict(
        norm1_w=norm1_w[l], norm2_w=norm2_w[l], w_in_t=jnp.swapaxes(w_in[l], 0, 1),
        big_rows=big_rows, small_rows=small_rows,
        w_alpha2=w_alpha2, b_alpha=b_alpha, b_mgate=b_mgate, conv_w=conv_w[l],
        gnorm_a_w=gnorm_a_w[l], gnorm_b_w=gnorm_b_w[l],
    )


def kernel(x_prompt, x_sample, c, state_gla, state_mlstm_C, state_mlstm_n, state_mlstm_m, c_ctx, w_ada, b_ada, norm1_w, norm2_w, w_in, w_alpha2, b_alpha, b_mgate, conv_w, gnorm_a_w, gnorm_b_w, w_out, w_ff1, w_ff2, final_norm_w):
    depth = w_in.shape[0]
    D = x_prompt.shape[-1]
    Bp, Tp, _ = x_prompt.shape
    Bs = x_sample.shape[0]
    assert 1 + Bs <= COND_ROWS
    cond = (c_ctx.reshape(1, D), c)
    cached = (state_gla, state_mlstm_C, state_mlstm_n, state_mlstm_m)
    xp, xs = x_prompt, x_sample
    s_gla, s_c, s_n, s_m = [], [], [], []
    for l in range(depth):
        lw = _layer_weights(l, norm1_w, norm2_w, w_in, w_alpha2, b_alpha, b_mgate, conv_w,
                            gnorm_a_w, gnorm_b_w)
        xp, xs, ctx = _layer(xp, xs, cond, (w_ada[l], b_ada[l].reshape(1, -1)), cached, lw, l,
                             (w_out[l], w_ff1[l], w_ff2[l]), final_norm_w, l == depth - 1)
        s_gla.append(ctx[0].reshape(Bp, 2, H_A, -1, ctx[0].shape[-1]))
        s_c.append(ctx[1].reshape(Bp, 2, H_B, -1, ctx[1].shape[-1]))
        s_n.append(ctx[2])
        s_m.append(ctx[3])
    dt = x_prompt.dtype
    return (xp, xs, jnp.stack(s_gla, axis=1).astype(dt), jnp.stack(s_c, axis=1).astype(dt),
            jnp.stack(s_n, axis=1).astype(dt), jnp.stack(s_m, axis=1).astype(dt))
```

```python
import functools
import math

import jax
import jax.numpy as jnp
from jax import lax
from jax.experimental import pallas as pl
from jax.experimental.pallas import tpu as pltpu

F32 = jnp.float32
BF16 = jnp.bfloat16

GRID_W = 64
H_A = 4
H_B = 4
R_ALPHA = 16
TAU_GLA = 16.0
CHUNK = 64
EPS = 1e-6
LANES = 128
COND_ROWS = 8
SMALL_W = LANES
GATE_LANE0 = 2 * R_ALPHA
VMEM_LIMIT = 56 * 1024 * 1024
SCAN_UNROLL = 4
PIPELINE_STARTS = 8
TOKEN_TILE = 512
FF_CHUNK = 512
GLA_FACTORED_DECAY_MAX = 60.0
MOD_SPLIT = 2
ADA_TILE = 1024


def _sigmoid(x):
    return 1.0 / (1.0 + jnp.exp(-x))


def _silu(x):
    return x * _sigmoid(x)


def _log_sigmoid(x):
    return jnp.minimum(x, 0.0) - jnp.log1p(jnp.exp(-jnp.abs(x)))


def _dot(a, b):
    return jnp.dot(a, b, preferred_element_type=F32)


def _dot_nt(a, b):
    return lax.dot_general(a, b, (((1,), (1,)), ((), ())), preferred_element_type=F32)


def _rms(x, w):
    return x * lax.rsqrt(jnp.mean(x * x, axis=-1, keepdims=True) + EPS) * w


def _tri_sum(tri, x):
    hi = x.astype(BF16)
    r1 = x - hi.astype(F32)
    mid = r1.astype(BF16)
    lo = (r1 - mid.astype(F32)).astype(BF16)
    return _dot(tri, hi) + _dot(tri, mid) + _dot(tri, lo)


def _chunk_masks(L):
    row = lax.broadcasted_iota(jnp.int32, (L, L), 0)
    col = lax.broadcasted_iota(jnp.int32, (L, L), 1)
    lower = row >= col
    upper = row <= col
    return lower, upper


def _ada_tile(cc_ref, c_ref, w_ref, b_ref, o_ref):
    D = cc_ref.shape[1]
    sub = lax.broadcasted_iota(jnp.int32, (COND_ROWS, D), 0)
    cond = jnp.where(sub == 0, cc_ref[...], 0.0)
    for r in range(c_ref.shape[0]):
        cond = jnp.where(sub == 1 + r, c_ref[r:r + 1, :], cond)
    o_ref[...] = _dot(_silu(cond).astype(BF16), w_ref[...].astype(BF16)) + b_ref[...]


def _ada_call(cc, c, w_ada, b_ada, n_cols):
    D = cc.shape[1]
    tn = min(n_cols, ADA_TILE)
    return pl.pallas_call(
        _ada_tile,
        grid=(n_cols // tn,),
        in_specs=[
            pl.BlockSpec(cc.shape, lambda j: (0, 0)),
            pl.BlockSpec(c.shape, lambda j: (0, 0)),
            pl.BlockSpec((D, tn), lambda j: (0, j)),
            pl.BlockSpec((1, tn), lambda j: (0, j)),
        ],
        out_specs=pl.BlockSpec((COND_ROWS, tn), lambda j: (0, j)),
        out_shape=jax.ShapeDtypeStruct((COND_ROWS, n_cols), F32),
        compiler_params=pltpu.CompilerParams(dimension_semantics=("arbitrary",),
                                             vmem_limit_bytes=VMEM_LIMIT),
        name="ada_mod",
    )(cc, c, w_ada, b_ada)


def _tile_group(n_ctx, tiles_per_req):
    i = pl.program_id(0)
    is_ctx = i < n_ctx
    row = jnp.where(is_ctx, 0, 1 + jnp.maximum(i - n_ctx, 0) // tiles_per_req)
    return is_ctx, row


def _ctx_tile(n_ctx):
    return lambda i: (jnp.minimum(i, n_ctx - 1), 0)


def _lat_tile(n_ctx):
    return lambda i: (jnp.maximum(i - n_ctx, 0), 0)


def _inproj_kernel(xc_ref, xl_ref, mod_ref, nw_ref, wt_ref, z_ref, wb_scr, *, n_ctx, tiles_per_req, big_rows,
                   small_rows):
    D = xc_ref.shape[1]

    @pl.when(pl.program_id(0) == 0)
    def _():
        col = 0
        for r0, n in big_rows:
            for k in range(n // LANES):
                blk = wt_ref[r0 + k * LANES:r0 + (k + 1) * LANES, :]
                wb_scr[:, col:col + LANES] = blk.T.astype(BF16)
                col += LANES
        parts = [wt_ref[r0:r0 + n, :] for r0, n in small_rows]
        n_small = sum(n for _, n in small_rows)
        parts.append(jnp.zeros((SMALL_W - n_small, D), F32))
        wb_scr[:, col:col + SMALL_W] = jnp.concatenate(parts, axis=0).T.astype(BF16)

    is_ctx, row = _tile_group(n_ctx, tiles_per_req)

    def tile(x_ref):
        sh1 = mod_ref[pl.ds(row, 1), 0:D]
        sc1 = mod_ref[pl.ds(row, 1), D:2 * D]
        h = _rms(x_ref[...], nw_ref[...]) * (1.0 + sc1) + sh1
        z_ref[...] = _dot(h.astype(BF16), wb_scr[...])

    @pl.when(is_ctx)
    def _():
        tile(xc_ref)

    @pl.when(jnp.logical_not(is_ctx))
    def _():
        tile(xl_ref)


def _inproj_call(xc2d, xl2d, mod, norm_w, w_in_t, *, tm, tiles_per_req, big_rows, small_rows):
    (Mc, D), Ml = xc2d.shape, xl2d.shape[0]
    n_ctx = Mc // tm
    n_out = sum(n for _, n in big_rows) + SMALL_W
    kern = functools.partial(_inproj_kernel, n_ctx=n_ctx, tiles_per_req=tiles_per_req,
                             big_rows=big_rows, small_rows=small_rows)
    return pl.pallas_call(
        kern,
        grid=((Mc + Ml) // tm,),
        in_specs=[
            pl.BlockSpec((tm, D), _ctx_tile(n_ctx)),
            pl.BlockSpec((tm, D), _lat_tile(n_ctx)),
            pl.BlockSpec(mod.shape, lambda i: (0, 0)),
            pl.BlockSpec((1, D), lambda i: (0, 0)),
            pl.BlockSpec(w_in_t.shape, lambda i: (0, 0), pipeline_mode=pl.Buffered(1)),
        ],
        out_specs=pl.BlockSpec((tm, n_out), lambda i: (i, 0)),
        out_shape=jax.ShapeDtypeStruct((Mc + Ml, n_out), F32),
        scratch_shapes=[pltpu.VMEM((D, n_out), BF16)],
        compiler_params=pltpu.CompilerParams(dimension_semantics=("arbitrary",),
                                             vmem_limit_bytes=VMEM_LIMIT),
        name="norm_inproj",
    )(xc2d, xl2d, mod, norm_w.reshape(1, D), w_in_t)


def _chunk_loop(n_chunks, unroll, make_units):
    def step(ns):
        pending = list(make_units(ns))
        active = []
        while pending or active:
            for _ in range(min(PIPELINE_STARTS, len(pending))):
                active.append(pending.pop(0))
            alive = []
            for g in active:
                try:
                    next(g)
                    alive.append(g)
                except StopIteration:
                    pass
            active = alive

    if unroll >= n_chunks:
        step(list(range(n_chunks)))
        return

    def body(i, carry):
        step([i * unroll + u for u in range(unroll)])
        return carry

    lax.fori_loop(0, n_chunks // unroll, body, 0)


def _chunk_rows(n):
    if isinstance(n, int):
        return pl.ds(n * CHUNK, CHUNK)
    return pl.ds(pl.multiple_of(n * CHUNK, CHUNK), CHUNK)


def _cast_specs(casts, n_steps):
    in_specs, out_specs, out_shape, args = [], [], [], []
    for w, axis in casts:
        blk = list(w.shape)
        assert blk[axis] % n_steps == 0
        blk[axis] //= n_steps
        assert blk[0] % 16 == 0 and blk[1] % LANES == 0
        idx = (lambda b: (b, 0)) if axis == 0 else (lambda b: (0, b))
        in_specs.append(pl.BlockSpec(tuple(blk), idx))
        out_specs.append(pl.BlockSpec(tuple(blk), idx))
        out_shape.append(jax.ShapeDtypeStruct(w.shape, BF16))
        args.append(w)
    return in_specs, out_specs, out_shape, args


def _gla_body(q_ref, k_ref, v_ref, g_ref, sm_ref, s0_ref, wal_ref, bal_ref, gw_ref, out_ref, snew_ref,
              st_scr, sall_scr, qh_scr, qs_scr, kh_scr):
    has_state = s0_ref is not None
    write_state = snew_ref is not None
    T = q_ref.shape[0]
    L = CHUNK
    N = T // L
    HK = q_ref.shape[1]
    DK = HK // H_A
    DV = v_ref.shape[1] // H_A
    scale = DK ** -0.5
    n_pairs = HK // LANES

    lower, upper = _chunk_masks(L)
    tri = (lower.astype(BF16), upper.astype(BF16))
    tmask = (lower, upper)
    lane = lax.broadcasted_iota(jnp.int32, (1, LANES), 1)
    head_mask = (lane < DK, lane >= DK)

    for d in range(2):
        for p in range(n_pairs):
            if has_state:
                st_scr[d, p] = s0_ref[d, p].T
            else:
                st_scr[d, p] = jnp.zeros((LANES, LANES), F32)

    def decay_pre(d, r):
        return _dot(sm_ref[r, :].astype(BF16), wal_ref[:, d * HK:(d + 1) * HK]) + bal_ref[d:d + 1, :]

    neg_pre = jnp.maximum(-(_dot(sm_ref[...].astype(BF16), wal_ref[...])
                            + jnp.concatenate([bal_ref[0:1, :], bal_ref[1:2, :]], axis=1)), 0.0)
    chunk_sums = jnp.sum(neg_pre.reshape(N, L, 2 * HK), axis=1)
    decay_span = (jnp.max(chunk_sums) + L * math.log(2.0)) * (1.0 / TAU_GLA)

    def state_group(ns, dirs=(0, 1)):
        units = [(d, n if d == 0 else N - 1 - n) for n in ns for d in dirs]
        rows = [_chunk_rows(n) for _, n in units]
        vt_all = [[jnp.concatenate([v_ref[r, (2 * p + j) * DV:(2 * p + j + 1) * DV] for j in range(2)],
                                   axis=0).T.astype(BF16) for p in range(n_pairs)] for r in rows]
        yield
        pre = [decay_pre(d, r) for (d, _), r in zip(units, rows)]
        yield
        g = [_log_sigmoid(x) * (1.0 / TAU_GLA) for x in pre]
        yield
        b = [_tri_sum(tri[d], gi) for (d, _), gi in zip(units, g)]
        yield
        ks_all, dec_all = [], []
        for (d, _), r, bi in zip(units, rows, b):
            bend = bi[L - 1:L, :] if d == 0 else bi[0:1, :]
            q = q_ref[r, :] * scale
            ks = (k_ref[r, :] * jnp.exp(bend - bi)).astype(BF16)
            qh_scr[d, r, :] = (q * jnp.exp(bi - bend)).astype(BF16)
            qs_scr[d, r, :] = (q * jnp.exp(bi)).astype(BF16)
            kh_scr[d, r, :] = ks
            ks_all.append(ks)
            dec_all.append(jnp.exp(bend))
        yield
        upd_all = []
        for vt_u, ks in zip(vt_all, ks_all):
            upd_u = []
            for p in range(n_pairs):
                kp = ks[:, p * LANES:(p + 1) * LANES]
                kk = jnp.concatenate([jnp.where(head_mask[j], kp, jnp.zeros_like(kp)) for j in range(2)], axis=0)
                upd_u.append(_dot(vt_u[p], kk))
            upd_all.append(upd_u)
        yield
        st = {d: [st_scr[d, p] for p in range(n_pairs)] for d in dirs}
        for (d, n), dec, upd in zip(units, dec_all, upd_all):
            for p in range(n_pairs):
                sall_scr[d, n, p] = st[d][p].astype(BF16)
                st[d][p] = st[d][p] * dec[:, p * LANES:(p + 1) * LANES] + upd[p]
        for d in dirs:
            for p in range(n_pairs):
                st_scr[d, p] = st[d][p]

    def stack_heads(x):
        return jnp.concatenate([jnp.where(head_mask[j], x, jnp.zeros_like(x)) for j in range(2)], axis=0)

    tok = lax.broadcasted_iota(jnp.int32, (L, 1), 0)
    row_t = lax.broadcasted_iota(jnp.int32, (2 * L, L), 0) & (L - 1)
    col_s = lax.broadcasted_iota(jnp.int32, (2 * L, L), 1)

    def exact_scores(d, r, p):
        ls = slice(p * LANES, (p + 1) * LANES)
        b = _tri_sum(tri[d], _log_sigmoid(decay_pre(d, r)[:, ls]) * (1.0 / TAU_GLA))
        q = q_ref[r, ls] * scale
        k = k_ref[r, ls]
        acc = jnp.where(row_t == col_s, _dot_nt(stack_heads(q).astype(BF16), k.astype(BF16)), 0.0)
        src = lax.broadcasted_iota(jnp.int32, (L, L), 1)
        h = L // 2
        while h >= 1:
            first = tok & ~(2 * h - 1)
            edge = first + (h - 1 if d == 0 else h)
            b_edge = _tri_sum((src == edge).astype(BF16), b)
            upper = (tok & (2 * h - 1)) >= h
            later, earlier = (upper, ~upper) if d == 0 else (~upper, upper)
            qt = jnp.where(later, q * jnp.exp(b - b_edge), 0.0)
            kt = jnp.where(earlier, k * jnp.exp(b_edge - b), 0.0)
            sc = _dot_nt(stack_heads(qt).astype(BF16), kt.astype(BF16))
            acc = acc + jnp.where((row_t & ~(2 * h - 1)) == (col_s & ~(2 * h - 1)), sc, 0.0)
            h //= 2
        return acc

    def out_group(ns, exact_decay=False):
        pairs = [(d, ni, p) for ni in range(len(ns)) for d in range(2) for p in range(n_pairs)]
        scores, inter = [], []
        for d, ni, p in pairs:
            r = _chunk_rows(ns[ni])
            ls = slice(p * LANES, (p + 1) * LANES)
            if exact_decay:
                scores.append(exact_scores(d, r, p))
            else:
                scores.append(_dot_nt(stack_heads(qh_scr[d, r, ls]), kh_scr[d, r, ls]))
            inter.append(_dot_nt(stack_heads(qs_scr[d, r, ls]), sall_scr[d, ns[ni], p]))
        yield
        probs = [[jnp.where(tmask[d], sc[j * L:(j + 1) * L, :], 0.0).astype(BF16) for j in range(2)]
                 for (d, _, _), sc in zip(pairs, scores)]
        yield
        outs = {}
        for (d, ni, p), pr, it in zip(pairs, probs, inter):
            r = _chunk_rows(ns[ni])
            for j in range(2):
                vs = slice((2 * p + j) * DV, (2 * p + j + 1) * DV)
                outs[(d, ni, 2 * p + j)] = _dot(pr[j], v_ref[r, vs].astype(BF16)) + it[j * L:(j + 1) * L, :]
        yield
        for ni, n in enumerate(ns):
            r = _chunk_rows(n)
            for h in range(H_A):
                vs = slice(h * DV, (h + 1) * DV)
                o = outs[(0, ni, h)] + outs[(1, ni, h)]
                out_ref[r, vs] = (_rms(o, gw_ref[:, vs]) * _silu(g_ref[r, vs])).astype(out_ref.dtype)

    def finish():
        if write_state:
            for d in range(2):
                for p in range(n_pairs):
                    snew_ref[d, p] = st_scr[d, p].T

    return state_group, out_group, finish, decay_span


def _gla_scratch(T, HK):
    n_pairs = HK // LANES
    n_chunks = T // CHUNK
    return [
        pltpu.VMEM((2, n_pairs, LANES, LANES), F32),
        pltpu.VMEM((2, n_chunks, n_pairs, LANES, LANES), BF16),
        pltpu.VMEM((2, T, HK), BF16),
        pltpu.VMEM((2, T, HK), BF16),
        pltpu.VMEM((2, T, HK), BF16),
    ]


def _mlstm_body(qk_ref, v_ref, og_ref, sm_ref, c0_ref, n0_ref, m0_ref, cw_ref, bm_ref, gw_ref,
                out_ref, cnew_ref, nnew_ref, mnew_ref,
                pad_scr, qk_scr, y_scr, c_scr, n_scr, m_scr, call_scr, nall_scr, mall_scr, g_scr, f_scr,
                *, grid_w):
    has_state = c0_ref is not None
    write_state = cnew_ref is not None
    T = qk_ref.shape[0]
    L = CHUNK
    N = T // L
    C2 = qk_ref.shape[1]
    HK = C2 // 2
    DK = HK // H_B
    DV = v_ref.shape[1] // H_B
    scale = DK ** -0.5
    n_pairs = HK // LANES
    P = pad_scr.shape[0] - T
    P0 = P // 2
    rows_img = T // grid_w

    lower, upper = _chunk_masks(L)
    tri = (lower.astype(BF16), upper.astype(BF16))
    tmask = (lower, upper)
    lane = lax.broadcasted_iota(jnp.int32, (1, LANES), 1)
    head_mask = (lane < DK, lane >= DK)
    lane_in = lane & (L - 1)

    def lane_cummax(x, d):
        k = 1
        while k < L:
            if d == 0:
                x = jnp.maximum(x, jnp.where(lane_in >= k, pltpu.roll(x, k, axis=1), -jnp.inf))
            else:
                x = jnp.maximum(x, jnp.where(lane_in < L - k, pltpu.roll(x, LANES - k, axis=1), -jnp.inf))
            k *= 2
        return x

    for d in range(2):
        for p in range(n_pairs):
            if has_state:
                c_scr[d, p] = c0_ref[d, p]
                n_scr[2 * d + p:2 * d + p + 1, :] = jnp.concatenate(
                    [n0_ref[d, 2 * p + j:2 * p + j + 1, :] for j in range(2)], axis=1)
            else:
                c_scr[d, p] = jnp.zeros((LANES, LANES), F32)
                n_scr[2 * d + p:2 * d + p + 1, :] = jnp.zeros((1, LANES), F32)
    eye_h = (lax.broadcasted_iota(jnp.int32, (H_B, H_B), 0) == lax.broadcasted_iota(jnp.int32, (H_B, H_B), 1))

    def to_col(row):
        return jnp.sum(jnp.where(eye_h, row, 0.0), axis=1, keepdims=True)

    def to_row(col):
        return jnp.sum(jnp.where(eye_h, col, 0.0), axis=0, keepdims=True)

    for d in range(2):
        if has_state:
            m_scr[H_B * d:H_B * (d + 1), 0:1] = to_col(m0_ref[d:d + 1, :])
        else:
            m_scr[H_B * d:H_B * (d + 1), 0:1] = jnp.zeros((H_B, 1), F32)

    pad_scr[0:P0, :] = jnp.zeros((P0, C2), F32)
    pad_scr[P0 + T:P + T, :] = jnp.zeros((P - P0, C2), F32)

    def copy_in(i, carry):
        r0 = pl.multiple_of(i * L, L)
        pad_scr[pl.ds(P0 + r0, L), :] = qk_ref[pl.ds(r0, L), :]
        return carry

    lax.fori_loop(0, N, copy_in, 0)

    lane_c = lax.broadcasted_iota(jnp.int32, (1, C2), 1)
    qscale = jnp.where(lane_c < HK, scale, 1.0).astype(F32)
    sub = lax.broadcasted_iota(jnp.int32, (L, 1), 0)
    img_rows = (0,) if rows_img == 1 else (-1, 0, 1)

    def conv_tile(i, carry):
        r0 = pl.multiple_of(i * L, L)
        col = lax.rem(r0, grid_w) + sub
        ok_left = col >= 1
        ok_right = col <= grid_w - 2
        sums = [None, None, None]
        for di in img_rows:
            blk = pad_scr[pl.ds(P0 + r0 + di * grid_w - 8, L + 16), :]
            for k in range(3):
                term = blk * cw_ref[di + 1, k:k + 1, :]
                sums[k] = term if sums[k] is None else sums[k] + term
        acc = (sums[1][8:8 + L, :] + jnp.where(ok_left, sums[0][7:7 + L, :], 0.0)
               + jnp.where(ok_right, sums[2][9:9 + L, :], 0.0))
        qk_scr[pl.ds(r0, L), :] = _silu(acc) * qscale
        return carry

    lax.fori_loop(0, N, conv_tile, 0)

    gl = lane - GATE_LANE0
    is_f = ((gl >= H_B) & (gl < 2 * H_B)) | ((gl >= 3 * H_B) & (gl < 4 * H_B))

    def gate_tile(i, carry):
        rows = pl.ds(pl.multiple_of(i * L, L), L)
        x = sm_ref[rows, :] + bm_ref[...]
        y_scr[rows, :] = jnp.where(is_f, _log_sigmoid(x), x)
        return carry

    lax.fori_loop(0, N, gate_tile, 0)


    def state_group(ns, dirs=(0, 1)):
        units = [(d, n if d == 0 else N - 1 - n) for n in ns for d in dirs]
        rows = [_chunk_rows(n) for _, n in units]
        kt_all = [[qk_scr[r, HK + p * LANES:HK + (p + 1) * LANES].T for p in range(n_pairs)] for r in rows]
        yield
        xs = [y_scr[r, :] for r in rows]
        fsum = [_tri_sum(tri[d], x) for (d, _), x in zip(units, xs)]
        yield
        wk_all, f_end, c_end = [], [], []
        for (d, n), r, x, fs in zip(units, rows, xs, fsum):
            y = jnp.where(is_f, fs, x)
            li0 = GATE_LANE0 + 2 * H_B * d
            blk = jnp.concatenate([y, y], axis=0).T[li0:li0 + 2 * H_B, :]
            frow = pltpu.roll(blk, H_B, axis=0)
            grow = blk - frow
            g_scr[d, n] = grow
            f_scr[d, n] = frow
            e_col = L - 1 if d == 0 else 0
            f_end.append(frow[0:H_B, e_col:e_col + 1])
            ce8 = jnp.max(grow, axis=1, keepdims=True)
            c_end.append(ce8[0:H_B, :])
            wk_all.append(jnp.exp(grow[:, 0:L] - ce8))
        yield
        kv_all, ksum_all = [], []
        for r, wk8, kt_u in zip(rows, wk_all, kt_all):
            kv_u, ks_u = [], []
            wk8b = wk8.astype(BF16)
            for p in range(n_pairs):
                kpb = qk_scr[r, HK + p * LANES:HK + (p + 1) * LANES].astype(BF16)
                ks8 = _dot(wk8b, kpb)
                for j in range(2):
                    h = 2 * p + j
                    kwt = (kt_u[p][j * DK:(j + 1) * DK, :] * wk8[h:h + 1, :]).astype(BF16)
                    kv_u.append(_dot(kwt, v_ref[r, h * DV:(h + 1) * DV].astype(BF16)))
                    ks_u.append(ks8[h:h + 1, :])
            kv_all.append(kv_u)
            ksum_all.append(ks_u)
        yield
        m_run = {d: m_scr[H_B * d:H_B * (d + 1), 0:1] for d in dirs}
        a_all, b_all = [], []
        for (d, n), fe, ce in zip(units, f_end, c_end):
            mall_scr[d, n, 0:H_B, 0:1] = m_run[d]
            mx = jnp.maximum(m_run[d], ce)
            a_all.append(jnp.exp(m_run[d] - mx))
            b_all.append(jnp.exp(ce - mx))
            m_run[d] = fe + mx
        for d in dirs:
            m_scr[H_B * d:H_B * (d + 1), 0:1] = m_run[d]
        yield
        c_run = {d: [[c_scr[d, p, j * DK:(j + 1) * DK, :] for j in range(2)] for p in range(n_pairs)] for d in dirs}
        n_run = {d: [n_scr[2 * d + p:2 * d + p + 1, :] for p in range(n_pairs)] for d in dirs}
        for (d, n), a4, b4, kv_u, ks_u in zip(units, a_all, b_all, kv_all, ksum_all):
            for p in range(n_pairs):
                nall_scr[d, n, p:p + 1, :] = n_run[d][p]
                a_s = [a4[2 * p + j:2 * p + j + 1, :] for j in range(2)]
                b_s = [b4[2 * p + j:2 * p + j + 1, :] for j in range(2)]
                for j in range(2):
                    cj = c_run[d][p][j]
                    call_scr[d, n, p, j * DK:(j + 1) * DK, :] = cj.astype(BF16)
                    c_run[d][p][j] = a_s[j] * cj + b_s[j] * kv_u[2 * p + j]
                n_run[d][p] = (jnp.where(head_mask[0], a_s[0], a_s[1]) * n_run[d][p]
                               + jnp.where(head_mask[0], b_s[0] * ks_u[2 * p], b_s[1] * ks_u[2 * p + 1]))
        for d in dirs:
            for p in range(n_pairs):
                n_scr[2 * d + p:2 * d + p + 1, :] = n_run[d][p]
                for j in range(2):
                    c_scr[d, p, j * DK:(j + 1) * DK, :] = c_run[d][p][j]

    eye = lower & upper
    ones8 = jnp.ones((8, L), BF16)
    sub8 = lax.broadcasted_iota(jnp.int32, (8, LANES), 0)
    sub_h = lax.broadcasted_iota(jnp.int32, (H_B, L), 0)
    n_rows = [((sub8 == 2 * p) & head_mask[0]) | ((sub8 == 2 * p + 1) & head_mask[1]) for p in range(n_pairs)]

    def head_rows(vals):
        out = vals[0][0:H_B, :]
        for h in range(1, H_B):
            out = jnp.where(sub_h == h, vals[h][0:H_B, :], out)
        return out

    def out_group(ns):
        chunks = [(d, n) for n in ns for d in range(2)]
        pairs = [(d, n, p) for d, n in chunks for p in range(n_pairs)]
        units = [(d, n, p, j) for d, n, p in pairs for j in range(2)]
        cms = [lane_cummax(g_scr[d, n], d)[0:H_B, 0:L] for d, n in chunks]
        qk2s, qc2s, qn2s = [], [], []
        for d, n, p in pairs:
            r = _chunk_rows(n)
            qp = qk_scr[r, p * LANES:(p + 1) * LANES]
            q2 = jnp.concatenate([jnp.where(head_mask[j], qp, 0.0) for j in range(2)], axis=0).astype(BF16)
            qk2s.append(_dot_nt(q2, qk_scr[r, HK + p * LANES:HK + (p + 1) * LANES].astype(BF16)))
            qc2s.append(_dot(q2, call_scr[d, n, p]))
            nsel = jnp.where(n_rows[p], nall_scr[d, n, p:p + 1, :], 0.0).astype(BF16)
            qn2s.append(_dot_nt(nsel, qp.astype(BF16)))
        yield
        s_all = []
        for ui, (d, n, p, j) in enumerate(units):
            grow = g_scr[d, n, 2 * p + j:2 * p + j + 1, 0:L]
            e = jnp.where(tmask[d], grow, -jnp.inf)
            cmax = jnp.max(e, axis=-1, keepdims=True)
            s_all.append((qk2s[ui // 2][j * L:(j + 1) * L, :] * jnp.exp(e - cmax)).astype(BF16))
        yield
        nums =[_dot(s, v_ref[_chunk_rows(n), (2 * p + j) * DV:(2 * p + j + 1) * DV].astype(BF16))
                for (d, n, p, j), s in zip(units, s_all)]
        dens = [_dot_nt(ones8, s) for s in s_all]
        yield
        scales = []
        for ci, (d, n) in enumerate(chunks):
            den_loc = head_rows(dens[ci * H_B:(ci + 1) * H_B])
            qn = qn2s[ci * n_pairs][0:H_B, :]
            for p in range(1, n_pairs):
                qn = qn + qn2s[ci * n_pairs + p][0:H_B, :]
            cm = cms[ci]
            m_prev = mall_scr[d, n, 0:H_B, 0:1]
            delta = cm - m_prev
            t = jnp.exp(-jnp.abs(delta))
            w_loc = jnp.where(delta <= 0.0, t, 1.0)
            w_inter = jnp.where(delta <= 0.0, 1.0, t)
            mt = f_scr[d, n, 0:H_B, 0:L] + jnp.maximum(m_prev, cm)
            den = w_loc * den_loc + w_inter * qn
            rinv = 1.0 / jnp.maximum(jnp.abs(den), jnp.exp(-mt))
            scales.append((w_loc * rinv, w_inter * rinv))
        yield
        hs = []
        for ui, (d, n, p, j) in enumerate(units):
            h = 2 * p + j
            sc_loc, sc_inter = scales[ui // H_B]
            d_loc = jnp.where(eye, sc_loc[h:h + 1, :], 0.0).astype(BF16)
            d_inter = jnp.where(eye, sc_inter[h:h + 1, :], 0.0).astype(BF16)
            hs.append(_dot(d_loc, nums[ui].astype(BF16))
                      + _dot(d_inter, qc2s[ui // 2][j * L:(j + 1) * L, :].astype(BF16)))
        yield
        for ni, n in enumerate(ns):
            r = _chunk_rows(n)
            for h in range(H_B):
                vs = slice(h * DV, (h + 1) * DV)
                o = hs[(2 * ni) * H_B + h] + hs[(2 * ni + 1) * H_B + h]
                out_ref[r, vs] = (_rms(o, gw_ref[:, vs]) * _sigmoid(og_ref[r, vs])).astype(out_ref.dtype)

    def finish():
        if write_state:
            for d in range(2):
                for p in range(n_pairs):
                    cnew_ref[d, p] = c_scr[d, p]
                    for j in range(2):
                        nnew_ref[d, 2 * p + j:2 * p + j + 1, :] = n_scr[2 * d + p:2 * d + p + 1, j * DK:(j + 1) * DK]
                mnew_ref[d:d + 1, :] = to_row(m_scr[H_B * d:H_B * (d + 1), 0:1])

    return state_group, out_group, finish


def _mlstm_scratch(T, C2, grid_w):
    n_pairs = C2 // 2 // LANES
    n_chunks = T // CHUNK
    pad_rows = 2 * (grid_w + 8) if T // grid_w > 1 else 16
    return [
        pltpu.VMEM((T + pad_rows, C2), F32),
        pltpu.VMEM((T, C2), F32),
        pltpu.VMEM((T, SMALL_W), F32),
        pltpu.VMEM((2, n_pairs, LANES, LANES), F32),
        pltpu.VMEM((8, LANES), F32),
        pltpu.VMEM((8, LANES), F32),
        pltpu.VMEM((2, n_chunks, n_pairs, LANES, LANES), BF16),
        pltpu.VMEM((2, n_chunks, 8, LANES), F32),
        pltpu.VMEM((2, n_chunks, 8, LANES), F32),
        pltpu.VMEM((2, n_chunks, 8, LANES), F32),
        pltpu.VMEM((2, n_chunks, 8, LANES), F32),
    ]


N_GLA_SCRATCH = 5
N_MLSTM_SCRATCH = 11


def _scan_kernel(*refs, cols, layer, has_state, write_state, n_cast, ride_ada, grid_w, unroll):
    refs = list(refs)
    z_ref = refs.pop(0)
    s0_ref = c0_ref = n0_ref = m0_ref = None
    if has_state:
        s0_ref, c0_ref, n0_ref, m0_ref = refs[:4]
        del refs[:4]
    wa_ref, bal_ref, gwa_ref, cw_ref, bmg_ref, gwb_ref = refs[:6]
    del refs[:6]
    cast_in = refs[:n_cast]
    del refs[:n_cast]
    if ride_ada:
        ada_in = refs[:4]
        del refs[:4]
    outa_ref, outb_ref = refs[:2]
    del refs[:2]
    snew_ref = cnew_ref = nnew_ref = mnew_ref = None
    if write_state:
        snew_ref, cnew_ref, nnew_ref, mnew_ref = refs[:4]
        del refs[:4]
    cast_out = refs[:n_cast]
    del refs[:n_cast]
    if ride_ada:
        ada_out = refs.pop(0)
    wal_scr, bm_scr = refs[:2]
    del refs[:2]
    gla_scr = refs[:N_GLA_SCRATCH]
    mlstm_scr = refs[N_GLA_SCRATCH:]

    for src, dst in zip(cast_in, cast_out):
        dst[...] = src[...].astype(BF16)
    if ride_ada:
        _ada_tile(*ada_in, ada_out)

    R, HK = wa_ref.shape[1], wa_ref.shape[2]
    wal_scr[...] = jnp.zeros(wal_scr.shape, BF16)
    for d in range(2):
        wal_scr[d * R:(d + 1) * R, d * HK:(d + 1) * HK] = wa_ref[d].astype(BF16)
    lane = lax.broadcasted_iota(jnp.int32, (1, LANES), 1)
    bm = jnp.zeros((1, LANES), F32)
    for g in range(bmg_ref.shape[1]):
        for h in range(H_B):
            bm = jnp.where(lane == GATE_LANE0 + H_B * g + h, bmg_ref[layer, g, h], bm)
    bm_scr[0:1, :] = bm

    def view(name):
        c0, w = cols[name]
        return z_ref.at[:, pl.ds(c0, w)]

    sm_ref = view("small")
    n_chunks = z_ref.shape[0] // CHUNK
    gla = _gla_body(view("qa"), view("ka"), view("va"), view("ga"), sm_ref, s0_ref, wal_scr, bal_ref, gwa_ref,
                    outa_ref, snew_ref, *gla_scr)
    mlstm = _mlstm_body(view("qkb"), view("vb"), view("ob"), sm_ref, c0_ref, n0_ref, m0_ref, cw_ref,
                        bm_scr.at[0:1, :], gwb_ref, outb_ref, cnew_ref, nnew_ref, mnew_ref, *mlstm_scr,
                        grid_w=grid_w)
    gla_state, gla_out, gla_finish, decay_span = gla
    mlstm_state, mlstm_out, mlstm_finish = mlstm

    def passes(gla_out_fn):
        _chunk_loop(n_chunks, unroll, lambda ns: [fn([n], (d,)) for n in ns for d in range(2)
                                                  for fn in (mlstm_state, gla_state)])
        _chunk_loop(n_chunks, unroll, lambda ns: [fn([n]) for n in ns for fn in (mlstm_out, gla_out_fn)])

    wide_decay = decay_span > GLA_FACTORED_DECAY_MAX

    @pl.when(jnp.logical_not(wide_decay))
    def _():
        passes(gla_out)

    @pl.when(wide_decay)
    def _():
        passes(functools.partial(gla_out, exact_decay=True))

    gla_finish()
    mlstm_finish()


def _scan_call(z2d, row0, B, T, states, lw, layer, *, grid_w, write_state, casts=(), ada=None):
    n_z = z2d.shape[1]
    assert row0 % T == 0 and z2d.shape[0] % T == 0
    z3 = z2d.reshape(z2d.shape[0] // T, T, n_z)
    blk0 = row0 // T
    HK = lw["w_alpha2"].shape[-1]
    DA = lw["gnorm_a_w"].shape[0]
    C2 = lw["conv_w"].shape[-1]
    DB = lw["gnorm_b_w"].shape[0]
    DK_A, DK_B = HK // H_A, C2 // 2 // H_B
    pa, pb = HK // LANES, C2 // 2 // LANES
    n_chunks = T // CHUNK
    has_state = states is not None
    widths = (("qa", HK), ("ka", HK), ("va", DA), ("ga", DA), ("qkb", C2), ("vb", DB), ("ob", DB),
              ("small", SMALL_W))
    cols, c0 = {}, 0
    for name, w in widths:
        cols[name] = (c0, w)
        c0 += w
    assert c0 == n_z
    cast_in_specs, cast_out_specs, cast_out_shape, cast_args = _cast_specs(casts, B)
    kern = functools.partial(_scan_kernel, cols=cols, layer=layer, has_state=has_state, write_state=write_state,
                             n_cast=len(casts), ride_ada=ada is not None, grid_w=grid_w,
                             unroll=min(n_chunks, SCAN_UNROLL))

    def per_batch(shape):
        nd = len(shape)
        return pl.BlockSpec((None,) + tuple(shape), lambda b: (b,) + (0,) * nd)

    def per_batch_layer(shape):
        nd = len(shape)
        return pl.BlockSpec((None, None) + tuple(shape), lambda b: (b, layer) + (0,) * nd)

    def of_layer(a):
        return pl.BlockSpec((None,) + a.shape[1:], lambda b: (layer,) + (0,) * (a.ndim - 1))

    def whole(a):
        return pl.BlockSpec(a.shape, lambda b: (0,) * a.ndim)

    state_shapes = ((2, pa, LANES, LANES), (2, pb, LANES, LANES), (2, H_B, DK_B), (2, H_B))
    in_specs = [pl.BlockSpec((None, T, n_z), lambda b: (b + blk0, 0, 0))]
    args = [z3]
    if has_state:
        s_gla, s_c, s_n, s_m = states
        depth = s_gla.shape[1]
        args += [s_gla.reshape((B, depth) + state_shapes[0]), s_c.reshape((B, depth) + state_shapes[1]), s_n, s_m]
        in_specs += [per_batch_layer(s) for s in state_shapes]
    args += [lw["w_alpha2"], lw["b_alpha"], lw["gnorm_a_w"].reshape(1, DA), lw["conv_w"], lw["b_mgate"],
             lw["gnorm_b_w"].reshape(1, DB)]
    in_specs += [of_layer(lw["w_alpha2"]), of_layer(lw["b_alpha"]), pl.BlockSpec((1, DA), lambda b: (0, 0)),
                 whole(lw["conv_w"]), pl.BlockSpec(memory_space=pltpu.SMEM), pl.BlockSpec((1, DB), lambda b: (0, 0))]
    args += cast_args
    in_specs += cast_in_specs
    out_specs = [per_batch((T, DA)), per_batch((T, DB))]
    out_shape = [jax.ShapeDtypeStruct((B, T, DA), BF16), jax.ShapeDtypeStruct((B, T, DB), BF16)]
    if write_state:
        out_specs += [per_batch(s) for s in state_shapes]
        out_shape += [jax.ShapeDtypeStruct((B,) + s, F32) for s in state_shapes]
    out_specs += cast_out_specs
    out_shape += cast_out_shape
    if ada is not None:
        cc, c, w_ada, b_ada, col0 = ada
        n_rest = w_ada.shape[1] - col0
        wcol = n_rest // B
        assert n_rest % B == 0 and wcol % LANES == 0 and col0 % wcol == 0
        args += [cc, c, w_ada, b_ada]
        in_specs += [whole(cc), whole(c),
                     pl.BlockSpec((w_ada.shape[0], wcol), lambda b: (0, col0 // wcol + b)),
                     pl.BlockSpec((1, wcol), lambda b: (0, col0 // wcol + b))]
        out_specs.append(pl.BlockSpec((COND_ROWS, wcol), lambda b: (0, b)))
        out_shape.append(jax.ShapeDtypeStruct((COND_ROWS, n_rest), F32))
    scratch = ([pltpu.VMEM((SMALL_W, 2 * HK), BF16), pltpu.VMEM((8, LANES), F32)]
               + _gla_scratch(T, HK) + _mlstm_scratch(T, C2, grid_w))
    assert len(scratch) == 2 + N_GLA_SCRATCH + N_MLSTM_SCRATCH
    return pl.pallas_call(
        kern,
        grid=(B,),
        in_specs=in_specs,
        out_specs=out_specs,
        out_shape=out_shape,
        scratch_shapes=scratch,
        compiler_params=pltpu.CompilerParams(dimension_semantics=("arbitrary",),
                                             vmem_limit_bytes=VMEM_LIMIT),
        name="mixer_scans",
    )(*args)


def _outff_kernel(xc_ref, xl_ref, ac_ref, al_ref, bc_ref, bl_ref, mod_ref, n2_ref, fn_ref, wo_ref, w1_ref, w2_ref,
                  yc_ref, yl_ref, *, n_ctx, tiles_per_req, ff_chunk, final_norm):
    D = xc_ref.shape[1]
    DA = ac_ref.shape[1]
    is_ctx, row = _tile_group(n_ctx, tiles_per_req)

    def mod(k):
        return mod_ref[pl.ds(row, 1), (k - MOD_SPLIT) * D:(k - MOD_SPLIT + 1) * D]

    def tile(x_ref, a_ref, b_ref, y_ref):
        y = _dot(a_ref[...], wo_ref[0:DA, :]) + _dot(b_ref[...], wo_ref[DA:, :])
        x1 = x_ref[...] + mod(2) * y
        h2 = (_rms(x1, n2_ref[...]) * (1.0 + mod(4)) + mod(3)).astype(BF16)
        acc = jnp.zeros(x1.shape, F32)
        for c0 in range(0, w1_ref.shape[1], ff_chunk):
            u = jnp.maximum(_dot(h2, w1_ref[:, c0:c0 + ff_chunk]), 0.0)
            acc = acc + _dot((u * u).astype(BF16), w2_ref[c0:c0 + ff_chunk, :])
        x2 = x1 + mod(5) * acc
        y_ref[...] = _rms(x2, fn_ref[...]) if final_norm else x2

    @pl.when(is_ctx)
    def _():
        tile(xc_ref, ac_ref, bc_ref, yc_ref)

    @pl.when(jnp.logical_not(is_ctx))
    def _():
        tile(xl_ref, al_ref, bl_ref, yl_ref)


def _outff_call(xc2d, xl2d, ac, al, bc, bl, mod, norm2_w, final_w, wo, w1, w2, *, tm, tiles_per_req, final_norm):
    (Mc, D), Ml = xc2d.shape, xl2d.shape[0]
    n_ctx = Mc // tm
    DA = ac.shape[1]
    DFF = w1.shape[1]
    kern = functools.partial(_outff_kernel, n_ctx=n_ctx, tiles_per_req=tiles_per_req, ff_chunk=FF_CHUNK,
                             final_norm=final_norm)
    once = pl.Buffered(1)
    ctx, lat = _ctx_tile(n_ctx), _lat_tile(n_ctx)
    return pl.pallas_call(
        kern,
        grid=((Mc + Ml) // tm,),
        in_specs=[
            pl.BlockSpec((tm, D), ctx), pl.BlockSpec((tm, D), lat),
            pl.BlockSpec((tm, DA), ctx), pl.BlockSpec((tm, DA), lat),
            pl.BlockSpec((tm, D - DA), ctx), pl.BlockSpec((tm, D - DA), lat),
            pl.BlockSpec(mod.shape, lambda i: (0, 0)),
            pl.BlockSpec((1, D), lambda i: (0, 0)),
            pl.BlockSpec((1, D), lambda i: (0, 0)),
            pl.BlockSpec((D, D), lambda i: (0, 0), pipeline_mode=once),
            pl.BlockSpec((D, DFF), lambda i: (0, 0), pipeline_mode=once),
            pl.BlockSpec((DFF, D), lambda i: (0, 0), pipeline_mode=once),
        ],
        out_specs=[pl.BlockSpec((tm, D), ctx), pl.BlockSpec((tm, D), lat)],
        out_shape=[jax.ShapeDtypeStruct((Mc, D), F32), jax.ShapeDtypeStruct((Ml, D), F32)],
        compiler_params=pltpu.CompilerParams(dimension_semantics=("arbitrary",),
                                             vmem_limit_bytes=VMEM_LIMIT),
        name="outproj_mlp",
    )(xc2d, xl2d, ac, al, bc, bl, mod, norm2_w.reshape(1, D), final_w.reshape(1, D), wo, w1, w2)


def _layer(xc, xl, cond, ada_w, cached, lw, layer, ffw, final_w, final_norm):
    (Bc, Tc, D), (Bl, Tl, _) = xc.shape, xl.shape
    tm = TOKEN_TILE
    assert (Bc * Tc) % tm == 0 and Tl % tm == 0 and (Bc * Tc) % Tl == 0
    xc2d, xl2d = xc.reshape(Bc * Tc, D), xl.reshape(Bl * Tl, D)
    mod_in = _ada_call(*cond, *ada_w, MOD_SPLIT * D)
    z = _inproj_call(xc2d, xl2d, mod_in, lw["norm1_w"], lw["w_in_t"], tm=tm, tiles_per_req=Tl // tm,
                     big_rows=lw["big_rows"], small_rows=lw["small_rows"])
    res_c = _scan_call(z, 0, Bc, Tc, None, lw, layer, grid_w=Tc, write_state=True,
                       casts=((ffw[0], 0), (ffw[1], 0), (ffw[2], 0)), ada=(*cond, *ada_w, MOD_SPLIT * D))
    res_l = _scan_call(z, Bc * Tc, Bl, Tl, cached, lw, layer, grid_w=GRID_W, write_state=False)
    wo_b, w1_b, w2_b, mod_out = res_c[-4:]
    yc, yl = _outff_call(xc2d, xl2d, res_c[0].reshape(Bc * Tc, -1), res_l[0].reshape(Bl * Tl, -1),
                         res_c[1].reshape(Bc * Tc, -1), res_l[1].reshape(Bl * Tl, -1), mod_out, lw["norm2_w"],
                         final_w, wo_b, w1_b, w2_b, tm=tm, tiles_per_req=Tl // tm, final_norm=final_norm)
    return yc.reshape(Bc, Tc, D), yl.reshape(Bl, Tl, D), tuple(res_c[2:6])


def _layer_weights(l, norm1_w, norm2_w, w_in, w_alpha2, b_alpha, b_mgate, conv_w, gnorm_a_w, gnorm_b_w):
    hk_a = w_alpha2.shape[-1]
    d_a = gnorm_a_w.shape[-1]
    d_b = gnorm_b_w.shape[-1]
    hk_b = conv_w.shape[-1] // 2
    sizes = (hk_a, hk_a, d_a, d_a, 2 * R_ALPHA, hk_b, hk_b, d_b, d_b, 4 * H_B)
    assert w_alpha2.shape[2] == R_ALPHA and b_mgate.shape[1] * b_mgate.shape[2] == 4 * H_B
    offs = [0]
    for s in sizes:
        offs.append(offs[-1] + s)
    big_rows = ((offs[0], offs[4] - offs[0]), (offs[5], offs[9] - offs[5]))
    small_rows = ((offs[4], offs[5] - offs[4]), (offs[9], offs[10] - offs[9]))
    assert all(n % LANES == 0 and r % 16 == 0 for r, n in big_rows)
    return dict(
        norm1_w=norm1_w[l], norm2_w=norm2_w[l], w_in_t=jnp.swapaxes(w_in[l], 0, 1),
        big_rows=big_rows, small_rows=small_rows,
        w_alpha2=w_alpha2, b_alpha=b_alpha, b_mgate=b_mgate, conv_w=conv_w[l],
        gnorm_a_w=gnorm_a_w[l], gnorm_b_w=gnorm_b_w[l],
    )


def kernel(x_prompt, x_sample, c, state_gla, state_mlstm_C, state_mlstm_n, state_mlstm_m, c_ctx, w_ada, b_ada, norm1_w, norm2_w, w_in, w_alpha2, b_alpha, b_mgate, conv_w, gnorm_a_w, gnorm_b_w, w_out, w_ff1, w_ff2, final_norm_w):
    depth = w_in.shape[0]
    D = x_prompt.shape[-1]
    Bp, Tp, _ = x_prompt.shape
    Bs = x_sample.shape[0]
    assert 1 + Bs <= COND_ROWS
    cond = (c_ctx.reshape(1, D), c)
    cached = (state_gla, state_mlstm_C, state_mlstm_n, state_mlstm_m)
    xp, xs = x_prompt, x_sample
    s_gla, s_c, s_n, s_m = [], [], [], []
    for l in range(depth):
        lw = _layer_weights(l, norm1_w, norm2_w, w_in, w_alpha2, b_alpha, b_mgate, conv_w,
                            gnorm_a_w, gnorm_b_w)
        xp, xs, ctx = _layer(xp, xs, cond, (w_ada[l], b_ada[l].reshape(1, -1)), cached, lw, l,
                             (w_out[l], w_ff1[l], w_ff2[l]), final_norm_w, l == depth - 1)
        s_gla.append(ctx[0].reshape(Bp, 2, H_A, -1, ctx[0].shape[-1]))
        s_c.append(ctx[1].reshape(Bp, 2, H_B, -1, ctx[1].shape[-1]))
        s_n.append(ctx[2])
        s_m.append(ctx[3])
    dt = x_prompt.dtype
    return (xp, xs, jnp.stack(s_gla, axis=1).astype(dt), jnp.stack(s_c, axis=1).astype(dt),
            jnp.stack(s_n, axis=1).astype(dt), jnp.stack(s_m, axis=1).astype(dt))
```

```python
import functools
import math

import jax
import jax.numpy as jnp
from jax import lax
from jax.experimental import pallas as pl
from jax.experimental.pallas import tpu as pltpu

F32 = jnp.float32
BF16 = jnp.bfloat16

GRID_W = 64
H_A = 4
H_B = 4
R_ALPHA = 16
TAU_GLA = 16.0
CHUNK = 64
EPS = 1e-6
LANES = 128
COND_ROWS = 8
SMALL_W = LANES
GATE_LANE0 = 2 * R_ALPHA
VMEM_LIMIT = 56 * 1024 * 1024
SCAN_UNROLL = 4
PIPELINE_STARTS = 8
TOKEN_TILE = 512
FF_CHUNK = 512
GLA_FACTORED_DECAY_MAX = 60.0
MOD_SPLIT = 2
ADA_TILE = 1024


def _sigmoid(x):
    return 1.0 / (1.0 + jnp.exp(-x))


def _silu(x):
    return x * _sigmoid(x)


def _log_sigmoid(x):
    return jnp.minimum(x, 0.0) - jnp.log(1.0 + jnp.exp(-jnp.abs(x)))


def _dot(a, b):
    return jnp.dot(a, b, preferred_element_type=F32)


def _dot_nt(a, b):
    return lax.dot_general(a, b, (((1,), (1,)), ((), ())), preferred_element_type=F32)


def _rms(x, w):
    return x * lax.rsqrt(jnp.mean(x * x, axis=-1, keepdims=True) + EPS) * w


def _tri_sum(tri, x, terms=3):
    acc, rest = None, x
    for t in range(terms):
        part = rest.astype(BF16)
        prod = _dot(tri, part)
        acc = prod if acc is None else acc + prod
        if t + 1 < terms:
            rest = rest - part.astype(F32)
    return acc


def _chunk_masks(L):
    row = lax.broadcasted_iota(jnp.int32, (L, L), 0)
    col = lax.broadcasted_iota(jnp.int32, (L, L), 1)
    lower = row >= col
    upper = row <= col
    return lower, upper


def _ada_tile(cc_ref, c_ref, w_ref, b_ref, o_ref):
    D = cc_ref.shape[1]
    sub = lax.broadcasted_iota(jnp.int32, (COND_ROWS, D), 0)
    cond = jnp.where(sub == 0, cc_ref[...], 0.0)
    for r in range(c_ref.shape[0]):
        cond = jnp.where(sub == 1 + r, c_ref[r:r + 1, :], cond)
    o_ref[...] = _dot(_silu(cond).astype(BF16), w_ref[...].astype(BF16)) + b_ref[...]


def _ada_call(cc, c, w_ada, b_ada, n_cols):
    D = cc.shape[1]
    tn = min(n_cols, ADA_TILE)
    return pl.pallas_call(
        _ada_tile,
        grid=(n_cols // tn,),
        in_specs=[
            pl.BlockSpec(cc.shape, lambda j: (0, 0)),
            pl.BlockSpec(c.shape, lambda j: (0, 0)),
            pl.BlockSpec((D, tn), lambda j: (0, j)),
            pl.BlockSpec((1, tn), lambda j: (0, j)),
        ],
        out_specs=pl.BlockSpec((COND_ROWS, tn), lambda j: (0, j)),
        out_shape=jax.ShapeDtypeStruct((COND_ROWS, n_cols), F32),
        compiler_params=pltpu.CompilerParams(dimension_semantics=("arbitrary",),
                                             vmem_limit_bytes=VMEM_LIMIT),
        name="ada_mod",
    )(cc, c, w_ada, b_ada)


def _tile_group(n_ctx, tiles_per_req):
    i = pl.program_id(0)
    is_ctx = i < n_ctx
    row = jnp.where(is_ctx, 0, 1 + jnp.maximum(i - n_ctx, 0) // tiles_per_req)
    return is_ctx, row


def _ctx_tile(n_ctx):
    return lambda i: (jnp.minimum(i, n_ctx - 1), 0)


def _lat_tile(n_ctx):
    return lambda i: (jnp.maximum(i - n_ctx, 0), 0)


def _inproj_kernel(xc_ref, xl_ref, mod_ref, nw_ref, wt_ref, z_ref, wb_scr, *, n_ctx, tiles_per_req, big_rows,
                   small_rows):
    D = xc_ref.shape[1]

    @pl.when(pl.program_id(0) == 0)
    def _():
        col = 0
        for r0, n in big_rows:
            for k in range(n // LANES):
                blk = wt_ref[r0 + k * LANES:r0 + (k + 1) * LANES, :]
                wb_scr[:, col:col + LANES] = blk.T.astype(BF16)
                col += LANES
        parts = [wt_ref[r0:r0 + n, :] for r0, n in small_rows]
        n_small = sum(n for _, n in small_rows)
        parts.append(jnp.zeros((SMALL_W - n_small, D), F32))
        wb_scr[:, col:col + SMALL_W] = jnp.concatenate(parts, axis=0).T.astype(BF16)

    is_ctx, row = _tile_group(n_ctx, tiles_per_req)

    def tile(x_ref):
        sh1 = mod_ref[pl.ds(row, 1), 0:D]
        sc1 = mod_ref[pl.ds(row, 1), D:2 * D]
        h = _rms(x_ref[...], nw_ref[...]) * (1.0 + sc1) + sh1
        z_ref[...] = _dot(h.astype(BF16), wb_scr[...])

    @pl.when(is_ctx)
    def _():
        tile(xc_ref)

    @pl.when(jnp.logical_not(is_ctx))
    def _():
        tile(xl_ref)


def _inproj_call(xc2d, xl2d, mod, norm_w, w_in_t, *, tm, tiles_per_req, big_rows, small_rows):
    (Mc, D), Ml = xc2d.shape, xl2d.shape[0]
    n_ctx = Mc // tm
    n_out = sum(n for _, n in big_rows) + SMALL_W
    kern = functools.partial(_inproj_kernel, n_ctx=n_ctx, tiles_per_req=tiles_per_req,
                             big_rows=big_rows, small_rows=small_rows)
    return pl.pallas_call(
        kern,
        grid=((Mc + Ml) // tm,),
        in_specs=[
            pl.BlockSpec((tm, D), _ctx_tile(n_ctx)),
            pl.BlockSpec((tm, D), _lat_tile(n_ctx)),
            pl.BlockSpec(mod.shape, lambda i: (0, 0)),
            pl.BlockSpec((1, D), lambda i: (0, 0)),
            pl.BlockSpec(w_in_t.shape, lambda i: (0, 0), pipeline_mode=pl.Buffered(1)),
        ],
        out_specs=pl.BlockSpec((tm, n_out), lambda i: (i, 0)),
        out_shape=jax.ShapeDtypeStruct((Mc + Ml, n_out), F32),
        scratch_shapes=[pltpu.VMEM((D, n_out), BF16)],
        compiler_params=pltpu.CompilerParams(dimension_semantics=("arbitrary",),
                                             vmem_limit_bytes=VMEM_LIMIT),
        name="norm_inproj",
    )(xc2d, xl2d, mod, norm_w.reshape(1, D), w_in_t)


def _chunk_loop(n_chunks, unroll, make_units):
    def step(ns):
        pending = list(make_units(ns))
        active = []
        while pending or active:
            for _ in range(min(PIPELINE_STARTS, len(pending))):
                active.append(pending.pop(0))
            alive = []
            for g in active:
                try:
                    next(g)
                    alive.append(g)
                except StopIteration:
                    pass
            active = alive

    if unroll >= n_chunks:
        step(list(range(n_chunks)))
        return

    def body(i, carry):
        step([i * unroll + u for u in range(unroll)])
        return carry

    lax.fori_loop(0, n_chunks // unroll, body, 0)


def _chunk_rows(n):
    if isinstance(n, int):
        return pl.ds(n * CHUNK, CHUNK)
    return pl.ds(pl.multiple_of(n * CHUNK, CHUNK), CHUNK)


def _cast_specs(casts, n_steps):
    in_specs, out_specs, out_shape, args = [], [], [], []
    for w, axis in casts:
        blk = list(w.shape)
        assert blk[axis] % n_steps == 0
        blk[axis] //= n_steps
        assert blk[0] % 16 == 0 and blk[1] % LANES == 0
        idx = (lambda b: (b, 0)) if axis == 0 else (lambda b: (0, b))
        in_specs.append(pl.BlockSpec(tuple(blk), idx))
        out_specs.append(pl.BlockSpec(tuple(blk), idx))
        out_shape.append(jax.ShapeDtypeStruct(w.shape, BF16))
        args.append(w)
    return in_specs, out_specs, out_shape, args


def _gla_body(q_ref, k_ref, v_ref, g_ref, sm_ref, s0_ref, wal_ref, bal_ref, gw_ref, out_ref, snew_ref,
              st_scr, sall_scr, qh_scr, qs_scr, kh_scr):
    has_state = s0_ref is not None
    write_state = snew_ref is not None
    T = q_ref.shape[0]
    L = CHUNK
    N = T // L
    HK = q_ref.shape[1]
    DK = HK // H_A
    DV = v_ref.shape[1] // H_A
    scale = DK ** -0.5
    n_pairs = HK // LANES

    lower, upper = _chunk_masks(L)
    tri = (lower.astype(BF16), upper.astype(BF16))
    tmask = (lower, upper)
    lane = lax.broadcasted_iota(jnp.int32, (1, LANES), 1)
    head_mask = (lane < DK, lane >= DK)

    for d in range(2):
        for p in range(n_pairs):
            if has_state:
                st_scr[d, p] = s0_ref[d, p].T
            else:
                st_scr[d, p] = jnp.zeros((LANES, LANES), F32)

    def decay_pre(d, r):
        return _dot(sm_ref[r, :].astype(BF16), wal_ref[:, d * HK:(d + 1) * HK]) + bal_ref[d:d + 1, :]

    neg_pre = jnp.maximum(-(_dot(sm_ref[...].astype(BF16), wal_ref[...])
                            + jnp.concatenate([bal_ref[0:1, :], bal_ref[1:2, :]], axis=1)), 0.0)
    chunk_sums = jnp.sum(neg_pre.reshape(N, L, 2 * HK), axis=1)
    decay_span = (jnp.max(chunk_sums) + L * math.log(2.0)) * (1.0 / TAU_GLA)

    def state_group(ns, dirs=(0, 1)):
        units = [(d, n if d == 0 else N - 1 - n) for n in ns for d in dirs]
        rows = [_chunk_rows(n) for _, n in units]
        vt_all = [[jnp.concatenate([v_ref[r, (2 * p + j) * DV:(2 * p + j + 1) * DV] for j in range(2)],
                                   axis=0).T.astype(BF16) for p in range(n_pairs)] for r in rows]
        yield
        pre = [decay_pre(d, r) for (d, _), r in zip(units, rows)]
        yield
        g = [_log_sigmoid(x) * (1.0 / TAU_GLA) for x in pre]
        yield
        b = [_tri_sum(tri[d], gi, terms=2) for (d, _), gi in zip(units, g)]
        yield
        ks_all, dec_all = [], []
        for (d, _), r, bi in zip(units, rows, b):
            bend = bi[L - 1:L, :] if d == 0 else bi[0:1, :]
            q = q_ref[r, :] * scale
            ks = (k_ref[r, :] * jnp.exp(bend - bi)).astype(BF16)
            qh_scr[d, r, :] = (q * jnp.exp(bi - bend)).astype(BF16)
            qs_scr[d, r, :] = (q * jnp.exp(bi)).astype(BF16)
            kh_scr[d, r, :] = ks
            ks_all.append(ks)
            dec_all.append(jnp.exp(bend))
        yield
        upd_all = []
        for vt_u, ks in zip(vt_all, ks_all):
            upd_u = []
            for p in range(n_pairs):
                kp = ks[:, p * LANES:(p + 1) * LANES]
                kk = jnp.concatenate([jnp.where(head_mask[j], kp, jnp.zeros_like(kp)) for j in range(2)], axis=0)
                upd_u.append(_dot(vt_u[p], kk))
            upd_all.append(upd_u)
        yield
        st = {d: [st_scr[d, p] for p in range(n_pairs)] for d in dirs}
        for (d, n), dec, upd in zip(units, dec_all, upd_all):
            for p in range(n_pairs):
                sall_scr[d, n, p] = st[d][p].astype(BF16)
                st[d][p] = st[d][p] * dec[:, p * LANES:(p + 1) * LANES] + upd[p]
        for d in dirs:
            for p in range(n_pairs):
                st_scr[d, p] = st[d][p]

    def stack_heads(x):
        return jnp.concatenate([jnp.where(head_mask[j], x, jnp.zeros_like(x)) for j in range(2)], axis=0)

    tok = lax.broadcasted_iota(jnp.int32, (L, 1), 0)
    row_t = lax.broadcasted_iota(jnp.int32, (2 * L, L), 0) & (L - 1)
    col_s = lax.broadcasted_iota(jnp.int32, (2 * L, L), 1)

    def exact_scores(d, r, p):
        ls = slice(p * LANES, (p + 1) * LANES)
        b = _tri_sum(tri[d], _log_sigmoid(decay_pre(d, r)[:, ls]) * (1.0 / TAU_GLA))
        q = q_ref[r, ls] * scale
        k = k_ref[r, ls]
        acc = jnp.where(row_t == col_s, _dot_nt(stack_heads(q).astype(BF16), k.astype(BF16)), 0.0)
        src = lax.broadcasted_iota(jnp.int32, (L, L), 1)
        h = L // 2
        while h >= 1:
            first = tok & ~(2 * h - 1)
            edge = first + (h - 1 if d == 0 else h)
            b_edge = _tri_sum((src == edge).astype(BF16), b)
            upper = (tok & (2 * h - 1)) >= h
            later, earlier = (upper, ~upper) if d == 0 else (~upper, upper)
            qt = jnp.where(later, q * jnp.exp(b - b_edge), 0.0)
            kt = jnp.where(earlier, k * jnp.exp(b_edge - b), 0.0)
            sc = _dot_nt(stack_heads(qt).astype(BF16), kt.astype(BF16))
            acc = acc + jnp.where((row_t & ~(2 * h - 1)) == (col_s & ~(2 * h - 1)), sc, 0.0)
            h //= 2
        return acc

    def out_group(ns, exact_decay=False):
        pairs = [(d, ni, p) for ni in range(len(ns)) for d in range(2) for p in range(n_pairs)]
        scores, inter = [], []
        for d, ni, p in pairs:
            r = _chunk_rows(ns[ni])
            ls = slice(p * LANES, (p + 1) * LANES)
            if exact_decay:
                scores.append(exact_scores(d, r, p))
            else:
                scores.append(_dot_nt(stack_heads(qh_scr[d, r, ls]), kh_scr[d, r, ls]))
            inter.append(_dot_nt(stack_heads(qs_scr[d, r, ls]), sall_scr[d, ns[ni], p]))
        yield
        probs = [[jnp.where(tmask[d], sc[j * L:(j + 1) * L, :], 0.0).astype(BF16) for j in range(2)]
                 for (d, _, _), sc in zip(pairs, scores)]
        yield
        outs = {}
        for (d, ni, p), pr, it in zip(pairs, probs, inter):
            r = _chunk_rows(ns[ni])
            for j in range(2):
                vs = slice((2 * p + j) * DV, (2 * p + j + 1) * DV)
                outs[(d, ni, 2 * p + j)] = _dot(pr[j], v_ref[r, vs].astype(BF16)) + it[j * L:(j + 1) * L, :]
        yield
        for ni, n in enumerate(ns):
            r = _chunk_rows(n)
            for h in range(H_A):
                vs = slice(h * DV, (h + 1) * DV)
                o = outs[(0, ni, h)] + outs[(1, ni, h)]
                out_ref[r, vs] = (_rms(o, gw_ref[:, vs]) * _silu(g_ref[r, vs])).astype(out_ref.dtype)

    def finish():
        if write_state:
            for d in range(2):
                for p in range(n_pairs):
                    snew_ref[d, p] = st_scr[d, p].T

    return state_group, out_group, finish, decay_span


def _gla_scratch(T, HK):
    n_pairs = HK // LANES
    n_chunks = T // CHUNK
    return [
        pltpu.VMEM((2, n_pairs, LANES, LANES), F32),
        pltpu.VMEM((2, n_chunks, n_pairs, LANES, LANES), BF16),
        pltpu.VMEM((2, T, HK), BF16),
        pltpu.VMEM((2, T, HK), BF16),
        pltpu.VMEM((2, T, HK), BF16),
    ]


def _mlstm_body(qk_ref, v_ref, og_ref, sm_ref, c0_ref, n0_ref, m0_ref, cw_ref, bm_ref, gw_ref,
                out_ref, cnew_ref, nnew_ref, mnew_ref,
                pad_scr, qk_scr, y_scr, c_scr, n_scr, m_scr, call_scr, nall_scr, mall_scr, g_scr, f_scr,
                *, grid_w):
    has_state = c0_ref is not None
    write_state = cnew_ref is not None
    T = qk_ref.shape[0]
    L = CHUNK
    N = T // L
    C2 = qk_ref.shape[1]
    HK = C2 // 2
    DK = HK // H_B
    DV = v_ref.shape[1] // H_B
    scale = DK ** -0.5
    n_pairs = HK // LANES
    P = pad_scr.shape[0] - T
    P0 = P // 2
    rows_img = T // grid_w

    lower, upper = _chunk_masks(L)
    tri = (lower.astype(BF16), upper.astype(BF16))
    tmask = (lower, upper)
    lane = lax.broadcasted_iota(jnp.int32, (1, LANES), 1)
    head_mask = (lane < DK, lane >= DK)
    lane_in = lane & (L - 1)

    def lane_cummax(x, d):
        k = 1
        while k < L:
            if d == 0:
                x = jnp.maximum(x, jnp.where(lane_in >= k, pltpu.roll(x, k, axis=1), -jnp.inf))
            else:
                x = jnp.maximum(x, jnp.where(lane_in < L - k, pltpu.roll(x, LANES - k, axis=1), -jnp.inf))
            k *= 2
        return x

    for d in range(2):
        for p in range(n_pairs):
            if has_state:
                c_scr[d, p] = c0_ref[d, p]
                n_scr[2 * d + p:2 * d + p + 1, :] = jnp.concatenate(
                    [n0_ref[d, 2 * p + j:2 * p + j + 1, :] for j in range(2)], axis=1)
            else:
                c_scr[d, p] = jnp.zeros((LANES, LANES), F32)
                n_scr[2 * d + p:2 * d + p + 1, :] = jnp.zeros((1, LANES), F32)
    eye_h = (lax.broadcasted_iota(jnp.int32, (H_B, H_B), 0) == lax.broadcasted_iota(jnp.int32, (H_B, H_B), 1))

    def to_col(row):
        return jnp.sum(jnp.where(eye_h, row, 0.0), axis=1, keepdims=True)

    def to_row(col):
        return jnp.sum(jnp.where(eye_h, col, 0.0), axis=0, keepdims=True)

    for d in range(2):
        if has_state:
            m_scr[H_B * d:H_B * (d + 1), 0:1] = to_col(m0_ref[d:d + 1, :])
        else:
            m_scr[H_B * d:H_B * (d + 1), 0:1] = jnp.zeros((H_B, 1), F32)

    pad_scr[0:P0, :] = jnp.zeros((P0, C2), F32)
    pad_scr[P0 + T:P + T, :] = jnp.zeros((P - P0, C2), F32)

    def copy_in(i, carry):
        r0 = pl.multiple_of(i * L, L)
        pad_scr[pl.ds(P0 + r0, L), :] = qk_ref[pl.ds(r0, L), :]
        return carry

    lax.fori_loop(0, N, copy_in, 0)

    lane_c = lax.broadcasted_iota(jnp.int32, (1, C2), 1)
    qscale = jnp.where(lane_c < HK, scale, 1.0).astype(F32)
    sub = lax.broadcasted_iota(jnp.int32, (L, 1), 0)
    img_rows = (0,) if rows_img == 1 else (-1, 0, 1)

    def conv_tile(i, carry):
        r0 = pl.multiple_of(i * L, L)
        col = lax.rem(r0, grid_w) + sub
        ok_left = col >= 1
        ok_right = col <= grid_w - 2
        sums = [None, None, None]
        for di in img_rows:
            blk = pad_scr[pl.ds(P0 + r0 + di * grid_w - 8, L + 16), :]
            for k in range(3):
                term = blk * cw_ref[di + 1, k:k + 1, :]
                sums[k] = term if sums[k] is None else sums[k] + term
        acc = (sums[1][8:8 + L, :] + jnp.where(ok_left, sums[0][7:7 + L, :], 0.0)
               + jnp.where(ok_right, sums[2][9:9 + L, :], 0.0))
        qk_scr[pl.ds(r0, L), :] = _silu(acc) * qscale
        return carry

    lax.fori_loop(0, N, conv_tile, 0)

    gl = lane - GATE_LANE0
    is_f = ((gl >= H_B) & (gl < 2 * H_B)) | ((gl >= 3 * H_B) & (gl < 4 * H_B))

    def gate_tile(i, carry):
        rows = pl.ds(pl.multiple_of(i * L, L), L)
        x = sm_ref[rows, :] + bm_ref[...]
        y_scr[rows, :] = jnp.where(is_f, _log_sigmoid(x), x)
        return carry

    lax.fori_loop(0, N, gate_tile, 0)


    def state_group(ns, dirs=(0, 1)):
        units = [(d, n if d == 0 else N - 1 - n) for n in ns for d in dirs]
        rows = [_chunk_rows(n) for _, n in units]
        kt_all = [[qk_scr[r, HK + p * LANES:HK + (p + 1) * LANES].T for p in range(n_pairs)] for r in rows]
        yield
        xs = [y_scr[r, :] for r in rows]
        fsum = [_tri_sum(tri[d], x) for (d, _), x in zip(units, xs)]
        yield
        wk_all, f_end, c_end = [], [], []
        for (d, n), r, x, fs in zip(units, rows, xs, fsum):
            y = jnp.where(is_f, fs, x)
            li0 = GATE_LANE0 + 2 * H_B * d
            blk = jnp.concatenate([y, y], axis=0).T[li0:li0 + 2 * H_B, :]
            frow = pltpu.roll(blk, H_B, axis=0)
            grow = blk - frow
            g_scr[d, n] = grow
            f_scr[d, n] = frow
            e_col = L - 1 if d == 0 else 0
            f_end.append(frow[0:H_B, e_col:e_col + 1])
            ce8 = jnp.max(grow, axis=1, keepdims=True)
            c_end.append(ce8[0:H_B, :])
            wk_all.append(jnp.exp(grow[:, 0:L] - ce8))
        yield
        kv_all, ksum_all = [], []
        for r, wk8, kt_u in zip(rows, wk_all, kt_all):
            kv_u, ks_u = [], []
            wk8b = wk8.astype(BF16)
            for p in range(n_pairs):
                kpb = qk_scr[r, HK + p * LANES:HK + (p + 1) * LANES].astype(BF16)
                ks8 = _dot(wk8b, kpb)
                for j in range(2):
                    h = 2 * p + j
                    kwt = (kt_u[p][j * DK:(j + 1) * DK, :] * wk8[h:h + 1, :]).astype(BF16)
                    kv_u.append(_dot(kwt, v_ref[r, h * DV:(h + 1) * DV].astype(BF16)))
                    ks_u.append(ks8[h:h + 1, :])
            kv_all.append(kv_u)
            ksum_all.append(ks_u)
        yield
        m_run = {d: m_scr[H_B * d:H_B * (d + 1), 0:1] for d in dirs}
        a_all, b_all = [], []
        for (d, n), fe, ce in zip(units, f_end, c_end):
            mall_scr[d, n, 0:H_B, 0:1] = m_run[d]
            mx = jnp.maximum(m_run[d], ce)
            a_all.append(jnp.exp(m_run[d] - mx))
            b_all.append(jnp.exp(ce - mx))
            m_run[d] = fe + mx
        for d in dirs:
            m_scr[H_B * d:H_B * (d + 1), 0:1] = m_run[d]
        yield
        c_run = {d: [[c_scr[d, p, j * DK:(j + 1) * DK, :] for j in range(2)] for p in range(n_pairs)] for d in dirs}
        n_run = {d: [n_scr[2 * d + p:2 * d + p + 1, :] for p in range(n_pairs)] for d in dirs}
        for (d, n), a4, b4, kv_u, ks_u in zip(units, a_all, b_all, kv_all, ksum_all):
            for p in range(n_pairs):
                nall_scr[d, n, p:p + 1, :] = n_run[d][p]
                a_s = [a4[2 * p + j:2 * p + j + 1, :] for j in range(2)]
                b_s = [b4[2 * p + j:2 * p + j + 1, :] for j in range(2)]
                for j in range(2):
                    cj = c_run[d][p][j]
                    call_scr[d, n, p, j * DK:(j + 1) * DK, :] = cj.astype(BF16)
                    c_run[d][p][j] = a_s[j] * cj + b_s[j] * kv_u[2 * p + j]
                n_run[d][p] = (jnp.where(head_mask[0], a_s[0], a_s[1]) * n_run[d][p]
                               + jnp.where(head_mask[0], b_s[0] * ks_u[2 * p], b_s[1] * ks_u[2 * p + 1]))
        for d in dirs:
            for p in range(n_pairs):
                n_scr[2 * d + p:2 * d + p + 1, :] = n_run[d][p]
                for j in range(2):
                    c_scr[d, p, j * DK:(j + 1) * DK, :] = c_run[d][p][j]

    eye = lower & upper
    ones8 = jnp.ones((8, L), BF16)
    sub8 = lax.broadcasted_iota(jnp.int32, (8, LANES), 0)
    sub_h = lax.broadcasted_iota(jnp.int32, (H_B, L), 0)
    n_rows = [((sub8 == 2 * p) & head_mask[0]) | ((sub8 == 2 * p + 1) & head_mask[1]) for p in range(n_pairs)]

    def head_rows(vals):
        out = vals[0][0:H_B, :]
        for h in range(1, H_B):
            out = jnp.where(sub_h == h, vals[h][0:H_B, :], out)
        return out

    def out_group(ns):
        chunks = [(d, n) for n in ns for d in range(2)]
        pairs = [(d, n, p) for d, n in chunks for p in range(n_pairs)]
        units = [(d, n, p, j) for d, n, p in pairs for j in range(2)]
        cms = [lane_cummax(g_scr[d, n], d)[0:H_B, 0:L] for d, n in chunks]
        qk2s, qc2s, qn2s = [], [], []
        for d, n, p in pairs:
            r = _chunk_rows(n)
            qp = qk_scr[r, p * LANES:(p + 1) * LANES]
            q2 = jnp.concatenate([jnp.where(head_mask[j], qp, 0.0) for j in range(2)], axis=0).astype(BF16)
            qk2s.append(_dot_nt(q2, qk_scr[r, HK + p * LANES:HK + (p + 1) * LANES].astype(BF16)))
            qc2s.append(_dot(q2, call_scr[d, n, p]))
            nsel = jnp.where(n_rows[p], nall_scr[d, n, p:p + 1, :], 0.0).astype(BF16)
            qn2s.append(_dot_nt(nsel, qp.astype(BF16)))
        yield
        s_all = []
        for ui, (d, n, p, j) in enumerate(units):
            grow = g_scr[d, n, 2 * p + j:2 * p + j + 1, 0:L]
            e = jnp.where(tmask[d], grow, -jnp.inf)
            cmax = jnp.max(e, axis=-1, keepdims=True)
            s_all.append((qk2s[ui // 2][j * L:(j + 1) * L, :] * jnp.exp(e - cmax)).astype(BF16))
        yield
        nums =[_dot(s, v_ref[_chunk_rows(n), (2 * p + j) * DV:(2 * p + j + 1) * DV].astype(BF16))
                for (d, n, p, j), s in zip(units, s_all)]
        dens = [_dot_nt(ones8, s) for s in s_all]
        yield
        scales = []
        for ci, (d, n) in enumerate(chunks):
            den_loc = head_rows(dens[ci * H_B:(ci + 1) * H_B])
            qn = qn2s[ci * n_pairs][0:H_B, :]
            for p in range(1, n_pairs):
                qn = qn + qn2s[ci * n_pairs + p][0:H_B, :]
            cm = cms[ci]
            m_prev = mall_scr[d, n, 0:H_B, 0:1]
            delta = cm - m_prev
            t = jnp.exp(-jnp.abs(delta))
            w_loc = jnp.where(delta <= 0.0, t, 1.0)
            w_inter = jnp.where(delta <= 0.0, 1.0, t)
            mt = f_scr[d, n, 0:H_B, 0:L] + jnp.maximum(m_prev, cm)
            den = w_loc * den_loc + w_inter * qn
            rinv = 1.0 / jnp.maximum(jnp.abs(den), jnp.exp(-mt))
            scales.append((w_loc * rinv, w_inter * rinv))
        yield
        hs = []
        for ui, (d, n, p, j) in enumerate(units):
            h = 2 * p + j
            sc_loc, sc_inter = scales[ui // H_B]
            d_loc = jnp.where(eye, sc_loc[h:h + 1, :], 0.0).astype(BF16)
            d_inter = jnp.where(eye, sc_inter[h:h + 1, :], 0.0).astype(BF16)
            hs.append(_dot(d_loc, nums[ui].astype(BF16))
                      + _dot(d_inter, qc2s[ui // 2][j * L:(j + 1) * L, :].astype(BF16)))
        yield
        for ni, n in enumerate(ns):
            r = _chunk_rows(n)
            for h in range(H_B):
                vs = slice(h * DV, (h + 1) * DV)
                o = hs[(2 * ni) * H_B + h] + hs[(2 * ni + 1) * H_B + h]
                out_ref[r, vs] = (_rms(o, gw_ref[:, vs]) * _sigmoid(og_ref[r, vs])).astype(out_ref.dtype)

    def finish():
        if write_state:
            for d in range(2):
                for p in range(n_pairs):
                    cnew_ref[d, p] = c_scr[d, p]
                    for j in range(2):
                        nnew_ref[d, 2 * p + j:2 * p + j + 1, :] = n_scr[2 * d + p:2 * d + p + 1, j * DK:(j + 1) * DK]
                mnew_ref[d:d + 1, :] = to_row(m_scr[H_B * d:H_B * (d + 1), 0:1])

    return state_group, out_group, finish


def _mlstm_scratch(T, C2, grid_w):
    n_pairs = C2 // 2 // LANES
    n_chunks = T // CHUNK
    pad_rows = 2 * (grid_w + 8) if T // grid_w > 1 else 16
    return [
        pltpu.VMEM((T + pad_rows, C2), F32),
        pltpu.VMEM((T, C2), F32),
        pltpu.VMEM((T, SMALL_W), F32),
        pltpu.VMEM((2, n_pairs, LANES, LANES), F32),
        pltpu.VMEM((8, LANES), F32),
        pltpu.VMEM((8, LANES), F32),
        pltpu.VMEM((2, n_chunks, n_pairs, LANES, LANES), BF16),
        pltpu.VMEM((2, n_chunks, 8, LANES), F32),
        pltpu.VMEM((2, n_chunks, 8, LANES), F32),
        pltpu.VMEM((2, n_chunks, 8, LANES), F32),
        pltpu.VMEM((2, n_chunks, 8, LANES), F32),
    ]


N_GLA_SCRATCH = 5
N_MLSTM_SCRATCH = 11


def _scan_kernel(*refs, cols, layer, has_state, write_state, n_cast, ride_ada, grid_w, unroll):
    refs = list(refs)
    z_ref = refs.pop(0)
    s0_ref = c0_ref = n0_ref = m0_ref = None
    if has_state:
        s0_ref, c0_ref, n0_ref, m0_ref = refs[:4]
        del refs[:4]
    wa_ref, bal_ref, gwa_ref, cw_ref, bmg_ref, gwb_ref = refs[:6]
    del refs[:6]
    cast_in = refs[:n_cast]
    del refs[:n_cast]
    if ride_ada:
        ada_in = refs[:4]
        del refs[:4]
    outa_ref, outb_ref = refs[:2]
    del refs[:2]
    snew_ref = cnew_ref = nnew_ref = mnew_ref = None
    if write_state:
        snew_ref, cnew_ref, nnew_ref, mnew_ref = refs[:4]
        del refs[:4]
    cast_out = refs[:n_cast]
    del refs[:n_cast]
    if ride_ada:
        ada_out = refs.pop(0)
    wal_scr, bm_scr = refs[:2]
    del refs[:2]
    gla_scr = refs[:N_GLA_SCRATCH]
    mlstm_scr = refs[N_GLA_SCRATCH:]

    for src, dst in zip(cast_in, cast_out):
        dst[...] = src[...].astype(BF16)
    if ride_ada:
        _ada_tile(*ada_in, ada_out)

    R, HK = wa_ref.shape[1], wa_ref.shape[2]
    wal_scr[...] = jnp.zeros(wal_scr.shape, BF16)
    for d in range(2):
        wal_scr[d * R:(d + 1) * R, d * HK:(d + 1) * HK] = wa_ref[d].astype(BF16)
    lane = lax.broadcasted_iota(jnp.int32, (1, LANES), 1)
    bm = jnp.zeros((1, LANES), F32)
    for g in range(bmg_ref.shape[1]):
        for h in range(H_B):
            bm = jnp.where(lane == GATE_LANE0 + H_B * g + h, bmg_ref[layer, g, h], bm)
    bm_scr[0:1, :] = bm

    def view(name):
        c0, w = cols[name]
        return z_ref.at[:, pl.ds(c0, w)]

    sm_ref = view("small")
    n_chunks = z_ref.shape[0] // CHUNK
    gla = _gla_body(view("qa"), view("ka"), view("va"), view("ga"), sm_ref, s0_ref, wal_scr, bal_ref, gwa_ref,
                    outa_ref, snew_ref, *gla_scr)
    mlstm = _mlstm_body(view("qkb"), view("vb"), view("ob"), sm_ref, c0_ref, n0_ref, m0_ref, cw_ref,
                        bm_scr.at[0:1, :], gwb_ref, outb_ref, cnew_ref, nnew_ref, mnew_ref, *mlstm_scr,
                        grid_w=grid_w)
    gla_state, gla_out, gla_finish, decay_span = gla
    mlstm_state, mlstm_out, mlstm_finish = mlstm

    def passes(gla_out_fn):
        _chunk_loop(n_chunks, unroll, lambda ns: [fn([n], (d,)) for n in ns for d in range(2)
                                                  for fn in (mlstm_state, gla_state)])
        _chunk_loop(n_chunks, unroll, lambda ns: [fn([n]) for n in ns for fn in (mlstm_out, gla_out_fn)])

    wide_decay = decay_span > GLA_FACTORED_DECAY_MAX

    @pl.when(jnp.logical_not(wide_decay))
    def _():
        passes(gla_out)

    @pl.when(wide_decay)
    def _():
        passes(functools.partial(gla_out, exact_decay=True))

    gla_finish()
    mlstm_finish()


def _scan_call(z2d, row0, B, T, states, lw, layer, *, grid_w, write_state, casts=(), ada=None):
    n_z = z2d.shape[1]
    assert row0 % T == 0 and z2d.shape[0] % T == 0
    z3 = z2d.reshape(z2d.shape[0] // T, T, n_z)
    blk0 = row0 // T
    HK = lw["w_alpha2"].shape[-1]
    DA = lw["gnorm_a_w"].shape[0]
    C2 = lw["conv_w"].shape[-1]
    DB = lw["gnorm_b_w"].shape[0]
    DK_A, DK_B = HK // H_A, C2 // 2 // H_B
    pa, pb = HK // LANES, C2 // 2 // LANES
    n_chunks = T // CHUNK
    has_state = states is not None
    widths = (("qa", HK), ("ka", HK), ("va", DA), ("ga", DA), ("qkb", C2), ("vb", DB), ("ob", DB),
              ("small", SMALL_W))
    cols, c0 = {}, 0
    for name, w in widths:
        cols[name] = (c0, w)
        c0 += w
    assert c0 == n_z
    cast_in_specs, cast_out_specs, cast_out_shape, cast_args = _cast_specs(casts, B)
    kern = functools.partial(_scan_kernel, cols=cols, layer=layer, has_state=has_state, write_state=write_state,
                             n_cast=len(casts), ride_ada=ada is not None, grid_w=grid_w,
                             unroll=min(n_chunks, SCAN_UNROLL))

    def per_batch(shape):
        nd = len(shape)
        return pl.BlockSpec((None,) + tuple(shape), lambda b: (b,) + (0,) * nd)

    def per_batch_layer(shape):
        nd = len(shape)
        return pl.BlockSpec((None, None) + tuple(shape), lambda b: (b, layer) + (0,) * nd)

    def of_layer(a):
        return pl.BlockSpec((None,) + a.shape[1:], lambda b: (layer,) + (0,) * (a.ndim - 1))

    def whole(a):
        return pl.BlockSpec(a.shape, lambda b: (0,) * a.ndim)

    state_shapes = ((2, pa, LANES, LANES), (2, pb, LANES, LANES), (2, H_B, DK_B), (2, H_B))
    in_specs = [pl.BlockSpec((None, T, n_z), lambda b: (b + blk0, 0, 0))]
    args = [z3]
    if has_state:
        s_gla, s_c, s_n, s_m = states
        depth = s_gla.shape[1]
        args += [s_gla.reshape((B, depth) + state_shapes[0]), s_c.reshape((B, depth) + state_shapes[1]), s_n, s_m]
        in_specs += [per_batch_layer(s) for s in state_shapes]
    args += [lw["w_alpha2"], lw["b_alpha"], lw["gnorm_a_w"].reshape(1, DA), lw["conv_w"], lw["b_mgate"],
             lw["gnorm_b_w"].reshape(1, DB)]
    in_specs += [of_layer(lw["w_alpha2"]), of_layer(lw["b_alpha"]), pl.BlockSpec((1, DA), lambda b: (0, 0)),
                 whole(lw["conv_w"]), pl.BlockSpec(memory_space=pltpu.SMEM), pl.BlockSpec((1, DB), lambda b: (0, 0))]
    args += cast_args
    in_specs += cast_in_specs
    out_specs = [per_batch((T, DA)), per_batch((T, DB))]
    out_shape = [jax.ShapeDtypeStruct((B, T, DA), BF16), jax.ShapeDtypeStruct((B, T, DB), BF16)]
    if write_state:
        out_specs += [per_batch(s) for s in state_shapes]
        out_shape += [jax.ShapeDtypeStruct((B,) + s, F32) for s in state_shapes]
    out_specs += cast_out_specs
    out_shape += cast_out_shape
    if ada is not None:
        cc, c, w_ada, b_ada, col0 = ada
        n_rest = w_ada.shape[1] - col0
        wcol = n_rest // B
        assert n_rest % B == 0 and wcol % LANES == 0 and col0 % wcol == 0
        args += [cc, c, w_ada, b_ada]
        in_specs += [whole(cc), whole(c),
                     pl.BlockSpec((w_ada.shape[0], wcol), lambda b: (0, col0 // wcol + b)),
                     pl.BlockSpec((1, wcol), lambda b: (0, col0 // wcol + b))]
        out_specs.append(pl.BlockSpec((COND_ROWS, wcol), lambda b: (0, b)))
        out_shape.append(jax.ShapeDtypeStruct((COND_ROWS, n_rest), F32))
    scratch = ([pltpu.VMEM((SMALL_W, 2 * HK), BF16), pltpu.VMEM((8, LANES), F32)]
               + _gla_scratch(T, HK) + _mlstm_scratch(T, C2, grid_w))
    assert len(scratch) == 2 + N_GLA_SCRATCH + N_MLSTM_SCRATCH
    return pl.pallas_call(
        kern,
        grid=(B,),
        in_specs=in_specs,
        out_specs=out_specs,
        out_shape=out_shape,
        scratch_shapes=scratch,
        compiler_params=pltpu.CompilerParams(dimension_semantics=("arbitrary",),
                                             vmem_limit_bytes=VMEM_LIMIT),
        name="mixer_scans",
    )(*args)


def _outff_kernel(xc_ref, xl_ref, ac_ref, al_ref, bc_ref, bl_ref, mod_ref, n2_ref, fn_ref, wo_ref, w1_ref, w2_ref,
                  yc_ref, yl_ref, *, n_ctx, tiles_per_req, ff_chunk, final_norm):
    D = xc_ref.shape[1]
    DA = ac_ref.shape[1]
    is_ctx, row = _tile_group(n_ctx, tiles_per_req)

    def mod(k):
        return mod_ref[pl.ds(row, 1), (k - MOD_SPLIT) * D:(k - MOD_SPLIT + 1) * D]

    def tile(x_ref, a_ref, b_ref, y_ref):
        y = _dot(a_ref[...], wo_ref[0:DA, :]) + _dot(b_ref[...], wo_ref[DA:, :])
        x1 = x_ref[...] + mod(2) * y
        h2 = (_rms(x1, n2_ref[...]) * (1.0 + mod(4)) + mod(3)).astype(BF16)
        acc = jnp.zeros(x1.shape, F32)
        for c0 in range(0, w1_ref.shape[1], ff_chunk):
            u = jnp.maximum(_dot(h2, w1_ref[:, c0:c0 + ff_chunk]), 0.0)
            acc = acc + _dot((u * u).astype(BF16), w2_ref[c0:c0 + ff_chunk, :])
        x2 = x1 + mod(5) * acc
        y_ref[...] = _rms(x2, fn_ref[...]) if final_norm else x2

    @pl.when(is_ctx)
    def _():
        tile(xc_ref, ac_ref, bc_ref, yc_ref)

    @pl.when(jnp.logical_not(is_ctx))
    def _():
        tile(xl_ref, al_ref, bl_ref, yl_ref)


def _outff_call(xc2d, xl2d, ac, al, bc, bl, mod, norm2_w, final_w, wo, w1, w2, *, tm, tiles_per_req, final_norm):
    (Mc, D), Ml = xc2d.shape, xl2d.shape[0]
    n_ctx = Mc // tm
    DA = ac.shape[1]
    DFF = w1.shape[1]
    kern = functools.partial(_outff_kernel, n_ctx=n_ctx, tiles_per_req=tiles_per_req, ff_chunk=FF_CHUNK,
                             final_norm=final_norm)
    once = pl.Buffered(1)
    ctx, lat = _ctx_tile(n_ctx), _lat_tile(n_ctx)
    return pl.pallas_call(
        kern,
        grid=((Mc + Ml) // tm,),
        in_specs=[
            pl.BlockSpec((tm, D), ctx), pl.BlockSpec((tm, D), lat),
            pl.BlockSpec((tm, DA), ctx), pl.BlockSpec((tm, DA), lat),
            pl.BlockSpec((tm, D - DA), ctx), pl.BlockSpec((tm, D - DA), lat),
            pl.BlockSpec(mod.shape, lambda i: (0, 0)),
            pl.BlockSpec((1, D), lambda i: (0, 0)),
            pl.BlockSpec((1, D), lambda i: (0, 0)),
            pl.BlockSpec((D, D), lambda i: (0, 0), pipeline_mode=once),
            pl.BlockSpec((D, DFF), lambda i: (0, 0), pipeline_mode=once),
            pl.BlockSpec((DFF, D), lambda i: (0, 0), pipeline_mode=once),
        ],
        out_specs=[pl.BlockSpec((tm, D), ctx), pl.BlockSpec((tm, D), lat)],
        out_shape=[jax.ShapeDtypeStruct((Mc, D), F32), jax.ShapeDtypeStruct((Ml, D), F32)],
        compiler_params=pltpu.CompilerParams(dimension_semantics=("arbitrary",),
                                             vmem_limit_bytes=VMEM_LIMIT),
        name="outproj_mlp",
    )(xc2d, xl2d, ac, al, bc, bl, mod, norm2_w.reshape(1, D), final_w.reshape(1, D), wo, w1, w2)


def _layer(xc, xl, cond, ada_w, cached, lw, layer, ffw, final_w, final_norm):
    (Bc, Tc, D), (Bl, Tl, _) = xc.shape, xl.shape
    tm = TOKEN_TILE
    assert (Bc * Tc) % tm == 0 and Tl % tm == 0 and (Bc * Tc) % Tl == 0
    xc2d, xl2d = xc.reshape(Bc * Tc, D), xl.reshape(Bl * Tl, D)
    mod_in = _ada_call(*cond, *ada_w, MOD_SPLIT * D)
    z = _inproj_call(xc2d, xl2d, mod_in, lw["norm1_w"], lw["w_in_t"], tm=tm, tiles_per_req=Tl // tm,
                     big_rows=lw["big_rows"], small_rows=lw["small_rows"])
    res_c = _scan_call(z, 0, Bc, Tc, None, lw, layer, grid_w=Tc, write_state=True,
                       casts=((ffw[0], 0), (ffw[1], 0), (ffw[2], 0)), ada=(*cond, *ada_w, MOD_SPLIT * D))
    res_l = _scan_call(z, Bc * Tc, Bl, Tl, cached, lw, layer, grid_w=GRID_W, write_state=False)
    wo_b, w1_b, w2_b, mod_out = res_c[-4:]
    yc, yl = _outff_call(xc2d, xl2d, res_c[0].reshape(Bc * Tc, -1), res_l[0].reshape(Bl * Tl, -1),
                         res_c[1].reshape(Bc * Tc, -1), res_l[1].reshape(Bl * Tl, -1), mod_out, lw["norm2_w"],
                         final_w, wo_b, w1_b, w2_b, tm=tm, tiles_per_req=Tl // tm, final_norm=final_norm)
    return yc.reshape(Bc, Tc, D), yl.reshape(Bl, Tl, D), tuple(res_c[2:6])


def _layer_weights(l, norm1_w, norm2_w, w_in, w_alpha2, b_alpha, b_mgate, conv_w, gnorm_a_w, gnorm_b_w):
    hk_a = w_alpha2.shape[-1]
    d_a = gnorm_a_w.shape[-1]
    d_b = gnorm_b_w.shape[-1]
    hk_b = conv_w.shape[-1] // 2
    sizes = (hk_a, hk_a, d_a, d_a, 2 * R_ALPHA, hk_b, hk_b, d_b, d_b, 4 * H_B)
    assert w_alpha2.shape[2] == R_ALPHA and b_mgate.shape[1] * b_mgate.shape[2] == 4 * H_B
    offs = [0]
    for s in sizes:
        offs.append(offs[-1] + s)
    big_rows = ((offs[0], offs[4] - offs[0]), (offs[5], offs[9] - offs[5]))
    small_rows = ((offs[4], offs[5] - offs[4]), (offs[9], offs[10] - offs[9]))
    assert all(n % LANES == 0 and r % 16 == 0 for r, n in big_rows)
    return dict(
        norm1_w=norm1_w[l], norm2_w=norm2_w[l], w_in_t=jnp.swapaxes(w_in[l], 0, 1),
        big_rows=big_rows, small_rows=small_rows,
        w_alpha2=w_alpha2, b_alpha=b_alpha, b_mgate=b_mgate, conv_w=conv_w[l],
        gnorm_a_w=gnorm_a_w[l], gnorm_b_w=gnorm_b_w[l],
    )


def kernel(x_prompt, x_sample, c, state_gla, state_mlstm_C, state_mlstm_n, state_mlstm_m, c_ctx, w_ada, b_ada, norm1_w, norm2_w, w_in, w_alpha2, b_alpha, b_mgate, conv_w, gnorm_a_w, gnorm_b_w, w_out, w_ff1, w_ff2, final_norm_w):
    depth = w_in.shape[0]
    D = x_prompt.shape[-1]
    Bp, Tp, _ = x_prompt.shape
    Bs = x_sample.shape[0]
    assert 1 + Bs <= COND_ROWS
    cond = (c_ctx.reshape(1, D), c)
    cached = (state_gla, state_mlstm_C, state_mlstm_n, state_mlstm_m)
    xp, xs = x_prompt, x_sample
    s_gla, s_c, s_n, s_m = [], [], [], []
    for l in range(depth):
        lw = _layer_weights(l, norm1_w, norm2_w, w_in, w_alpha2, b_alpha, b_mgate, conv_w,
                            gnorm_a_w, gnorm_b_w)
        xp, xs, ctx = _layer(xp, xs, cond, (w_ada[l], b_ada[l].reshape(1, -1)), cached, lw, l,
                             (w_out[l], w_ff1[l], w_ff2[l]), final_norm_w, l == depth - 1)
        s_gla.append(ctx[0].reshape(Bp, 2, H_A, -1, ctx[0].shape[-1]))
        s_c.append(ctx[1].reshape(Bp, 2, H_B, -1, ctx[1].shape[-1]))
        s_n.append(ctx[2])
        s_m.append(ctx[3])
    dt = x_prompt.dtype
    return (xp, xs, jnp.stack(s_gla, axis=1).astype(dt), jnp.stack(s_c, axis=1).astype(dt),
            jnp.stack(s_n, axis=1).astype(dt), jnp.stack(s_m, axis=1).astype(dt))
```

```python
import functools
import math

import jax
import jax.numpy as jnp
from jax import lax
from jax.experimental import pallas as pl
from jax.experimental.pallas import tpu as pltpu

F32 = jnp.float32
BF16 = jnp.bfloat16

GRID_W = 64
H_A = 4
H_B = 4
R_ALPHA = 16
TAU_GLA = 16.0
CHUNK = 64
EPS = 1e-6
LANES = 128
COND_ROWS = 8
SMALL_W = LANES
GATE_LANE0 = 2 * R_ALPHA
VMEM_LIMIT = 56 * 1024 * 1024
SCAN_UNROLL = 4
PIPELINE_STARTS = 8
TOKEN_TILE = 512
FF_CHUNK = 512
GLA_FACTORED_DECAY_MAX = 60.0
MOD_SPLIT = 2


def _sigmoid(x):
    return 1.0 / (1.0 + jnp.exp(-x))


def _silu(x):
    return x * _sigmoid(x)


def _log_sigmoid(x):
    return jnp.minimum(x, 0.0) - jnp.log(1.0 + jnp.exp(-jnp.abs(x)))


def _dot(a, b):
    return jnp.dot(a, b, preferred_element_type=F32)


def _dot_nt(a, b):
    return lax.dot_general(a, b, (((1,), (1,)), ((), ())), preferred_element_type=F32)


def _rms(x, w):
    return x * lax.rsqrt(jnp.mean(x * x, axis=-1, keepdims=True) + EPS) * w


def _tri_sum(tri, x, terms=3):
    acc, rest = None, x
    for t in range(terms):
        part = rest.astype(BF16)
        prod = _dot(tri, part)
        acc = prod if acc is None else acc + prod
        if t + 1 < terms:
            rest = rest - part.astype(F32)
    return acc


def _chunk_masks(L):
    row = lax.broadcasted_iota(jnp.int32, (L, L), 0)
    col = lax.broadcasted_iota(jnp.int32, (L, L), 1)
    lower = row >= col
    upper = row <= col
    return lower, upper


def _ada_tile(cc_ref, c_ref, w_ref, b_ref, o_ref):
    D = cc_ref.shape[1]
    sub = lax.broadcasted_iota(jnp.int32, (COND_ROWS, D), 0)
    cond = jnp.where(sub == 0, cc_ref[...], 0.0)
    for r in range(c_ref.shape[0]):
        cond = jnp.where(sub == 1 + r, c_ref[r:r + 1, :], cond)
    o_ref[...] = _dot(_silu(cond).astype(BF16), w_ref[...].astype(BF16)) + b_ref[...]


def _tile_group(n_ctx, tiles_per_req):
    i = pl.program_id(0)
    is_ctx = i < n_ctx
    row = jnp.where(is_ctx, 0, 1 + jnp.maximum(i - n_ctx, 0) // tiles_per_req)
    return is_ctx, row


def _ctx_tile(n_ctx):
    return lambda i: (jnp.minimum(i, n_ctx - 1), 0)


def _lat_tile(n_ctx):
    return lambda i: (jnp.maximum(i - n_ctx, 0), 0)


def _inproj_kernel(xc_ref, xl_ref, cc_ref, c_ref, wa_ref, ba_ref, nw_ref, wt_ref, z_ref, wb_scr, mod_ref,
                   *, n_ctx, tiles_per_req, big_rows, small_rows):
    D = xc_ref.shape[1]

    @pl.when(pl.program_id(0) == 0)
    def _():
        _ada_tile(cc_ref, c_ref, wa_ref, ba_ref, mod_ref)
        col = 0
        for r0, n in big_rows:
            for k in range(n // LANES):
                blk = wt_ref[r0 + k * LANES:r0 + (k + 1) * LANES, :]
                wb_scr[:, col:col + LANES] = blk.T.astype(BF16)
                col += LANES
        parts = [wt_ref[r0:r0 + n, :] for r0, n in small_rows]
        n_small = sum(n for _, n in small_rows)
        parts.append(jnp.zeros((SMALL_W - n_small, D), F32))
        wb_scr[:, col:col + SMALL_W] = jnp.concatenate(parts, axis=0).T.astype(BF16)

    is_ctx, row = _tile_group(n_ctx, tiles_per_req)

    def tile(x_ref):
        sh1 = mod_ref[pl.ds(row, 1), 0:D]
        sc1 = mod_ref[pl.ds(row, 1), D:2 * D]
        h = _rms(x_ref[...], nw_ref[...]) * (1.0 + sc1) + sh1
        z_ref[...] = _dot(h.astype(BF16), wb_scr[...])

    @pl.when(is_ctx)
    def _():
        tile(xc_ref)

    @pl.when(jnp.logical_not(is_ctx))
    def _():
        tile(xl_ref)


def _inproj_call(xc2d, xl2d, cc, c, w_ada, b_ada, norm_w, w_in_t, *, tm, tiles_per_req, big_rows, small_rows):
    (Mc, D), Ml = xc2d.shape, xl2d.shape[0]
    n_ctx = Mc // tm
    n_out = sum(n for _, n in big_rows) + SMALL_W
    n_mod = MOD_SPLIT * D
    kern = functools.partial(_inproj_kernel, n_ctx=n_ctx, tiles_per_req=tiles_per_req,
                             big_rows=big_rows, small_rows=small_rows)
    once = pl.Buffered(1)
    return pl.pallas_call(
        kern,
        grid=((Mc + Ml) // tm,),
        in_specs=[
            pl.BlockSpec((tm, D), _ctx_tile(n_ctx)),
            pl.BlockSpec((tm, D), _lat_tile(n_ctx)),
            pl.BlockSpec(cc.shape, lambda i: (0, 0)),
            pl.BlockSpec(c.shape, lambda i: (0, 0)),
            pl.BlockSpec((D, n_mod), lambda i: (0, 0), pipeline_mode=once),
            pl.BlockSpec((1, n_mod), lambda i: (0, 0)),
            pl.BlockSpec((1, D), lambda i: (0, 0)),
            pl.BlockSpec(w_in_t.shape, lambda i: (0, 0), pipeline_mode=once),
        ],
        out_specs=pl.BlockSpec((tm, n_out), lambda i: (i, 0)),
        out_shape=jax.ShapeDtypeStruct((Mc + Ml, n_out), F32),
        scratch_shapes=[pltpu.VMEM((D, n_out), BF16), pltpu.VMEM((COND_ROWS, n_mod), F32)],
        compiler_params=pltpu.CompilerParams(dimension_semantics=("arbitrary",),
                                             vmem_limit_bytes=VMEM_LIMIT),
        name="norm_inproj",
    )(xc2d, xl2d, cc, c, w_ada, b_ada, norm_w.reshape(1, D), w_in_t)


def _chunk_loop(n_chunks, unroll, make_units):
    def step(ns):
        pending = list(make_units(ns))
        active = []
        while pending or active:
            for _ in range(min(PIPELINE_STARTS, len(pending))):
                active.append(pending.pop(0))
            alive = []
            for g in active:
                try:
                    next(g)
                    alive.append(g)
                except StopIteration:
                    pass
            active = alive

    if unroll >= n_chunks:
        step(list(range(n_chunks)))
        return

    def body(i, carry):
        step([i * unroll + u for u in range(unroll)])
        return carry

    lax.fori_loop(0, n_chunks // unroll, body, 0)


def _chunk_rows(n):
    if isinstance(n, int):
        return pl.ds(n * CHUNK, CHUNK)
    return pl.ds(pl.multiple_of(n * CHUNK, CHUNK), CHUNK)


def _cast_specs(casts, n_steps):
    in_specs, out_specs, out_shape, args = [], [], [], []
    for w, axis in casts:
        blk = list(w.shape)
        assert blk[axis] % n_steps == 0
        blk[axis] //= n_steps
        assert blk[0] % 16 == 0 and blk[1] % LANES == 0
        idx = (lambda b: (b, 0)) if axis == 0 else (lambda b: (0, b))
        in_specs.append(pl.BlockSpec(tuple(blk), idx))
        out_specs.append(pl.BlockSpec(tuple(blk), idx))
        out_shape.append(jax.ShapeDtypeStruct(w.shape, BF16))
        args.append(w)
    return in_specs, out_specs, out_shape, args


def _gla_body(q_ref, k_ref, v_ref, g_ref, sm_ref, s0_ref, wal_ref, bal_ref, gw_ref, out_ref, snew_ref,
              st_scr, sall_scr, qh_scr, qs_scr, kh_scr):
    has_state = s0_ref is not None
    write_state = snew_ref is not None
    T = q_ref.shape[0]
    L = CHUNK
    N = T // L
    HK = q_ref.shape[1]
    DK = HK // H_A
    DV = v_ref.shape[1] // H_A
    scale = DK ** -0.5
    n_pairs = HK // LANES

    lower, upper = _chunk_masks(L)
    tri = (lower.astype(BF16), upper.astype(BF16))
    tmask = (lower, upper)
    lane = lax.broadcasted_iota(jnp.int32, (1, LANES), 1)
    head_mask = (lane < DK, lane >= DK)

    for d in range(2):
        for p in range(n_pairs):
            if has_state:
                st_scr[d, p] = s0_ref[d, p].T
            else:
                st_scr[d, p] = jnp.zeros((LANES, LANES), F32)

    def decay_pre(d, r):
        return _dot(sm_ref[r, :].astype(BF16), wal_ref[:, d * HK:(d + 1) * HK]) + bal_ref[d:d + 1, :]

    neg_pre = jnp.maximum(-(_dot(sm_ref[...].astype(BF16), wal_ref[...])
                            + jnp.concatenate([bal_ref[0:1, :], bal_ref[1:2, :]], axis=1)), 0.0)
    chunk_sums = jnp.sum(neg_pre.reshape(N, L, 2 * HK), axis=1)
    decay_span = (jnp.max(chunk_sums) + L * math.log(2.0)) * (1.0 / TAU_GLA)

    def state_group(ns, dirs=(0, 1)):
        units = [(d, n if d == 0 else N - 1 - n) for n in ns for d in dirs]
        rows = [_chunk_rows(n) for _, n in units]
        vt_all = [[jnp.concatenate([v_ref[r, (2 * p + j) * DV:(2 * p + j + 1) * DV] for j in range(2)],
                                   axis=0).T.astype(BF16) for p in range(n_pairs)] for r in rows]
        yield
        pre = [decay_pre(d, r) for (d, _), r in zip(units, rows)]
        yield
        g = [_log_sigmoid(x) * (1.0 / TAU_GLA) for x in pre]
        yield
        b = [_tri_sum(tri[d], gi, terms=2) for (d, _), gi in zip(units, g)]
        yield
        ks_all, dec_all = [], []
        for (d, _), r, bi in zip(units, rows, b):
            bend = bi[L - 1:L, :] if d == 0 else bi[0:1, :]
            q = q_ref[r, :] * scale
            ks = (k_ref[r, :] * jnp.exp(bend - bi)).astype(BF16)
            qh_scr[d, r, :] = (q * jnp.exp(bi - bend)).astype(BF16)
            qs_scr[d, r, :] = (q * jnp.exp(bi)).astype(BF16)
            kh_scr[d, r, :] = ks
            ks_all.append(ks)
            dec_all.append(jnp.exp(bend))
        yield
        upd_all = []
        for vt_u, ks in zip(vt_all, ks_all):
            upd_u = []
            for p in range(n_pairs):
                kp = ks[:, p * LANES:(p + 1) * LANES]
                kk = jnp.concatenate([jnp.where(head_mask[j], kp, jnp.zeros_like(kp)) for j in range(2)], axis=0)
                upd_u.append(_dot(vt_u[p], kk))
            upd_all.append(upd_u)
        yield
        st = {d: [st_scr[d, p] for p in range(n_pairs)] for d in dirs}
        for (d, n), dec, upd in zip(units, dec_all, upd_all):
            for p in range(n_pairs):
                sall_scr[d, n, p] = st[d][p].astype(BF16)
                st[d][p] = st[d][p] * dec[:, p * LANES:(p + 1) * LANES] + upd[p]
        for d in dirs:
            for p in range(n_pairs):
                st_scr[d, p] = st[d][p]

    def stack_heads(x):
        return jnp.concatenate([jnp.where(head_mask[j], x, jnp.zeros_like(x)) for j in range(2)], axis=0)

    tok = lax.broadcasted_iota(jnp.int32, (L, 1), 0)
    row_t = lax.broadcasted_iota(jnp.int32, (2 * L, L), 0) & (L - 1)
    col_s = lax.broadcasted_iota(jnp.int32, (2 * L, L), 1)

    def exact_scores(d, r, p):
        ls = slice(p * LANES, (p + 1) * LANES)
        b = _tri_sum(tri[d], _log_sigmoid(decay_pre(d, r)[:, ls]) * (1.0 / TAU_GLA))
        q = q_ref[r, ls] * scale
        k = k_ref[r, ls]
        acc = jnp.where(row_t == col_s, _dot_nt(stack_heads(q).astype(BF16), k.astype(BF16)), 0.0)
        src = lax.broadcasted_iota(jnp.int32, (L, L), 1)
        h = L // 2
        while h >= 1:
            first = tok & ~(2 * h - 1)
            edge = first + (h - 1 if d == 0 else h)
            b_edge = _tri_sum((src == edge).astype(BF16), b)
            upper = (tok & (2 * h - 1)) >= h
            later, earlier = (upper, ~upper) if d == 0 else (~upper, upper)
            qt = jnp.where(later, q * jnp.exp(b - b_edge), 0.0)
            kt = jnp.where(earlier, k * jnp.exp(b_edge - b), 0.0)
            sc = _dot_nt(stack_heads(qt).astype(BF16), kt.astype(BF16))
            acc = acc + jnp.where((row_t & ~(2 * h - 1)) == (col_s & ~(2 * h - 1)), sc, 0.0)
            h //= 2
        return acc

    def out_group(ns, exact_decay=False):
        pairs = [(d, ni, p) for ni in range(len(ns)) for d in range(2) for p in range(n_pairs)]
        scores, inter = [], []
        for d, ni, p in pairs:
            r = _chunk_rows(ns[ni])
            ls = slice(p * LANES, (p + 1) * LANES)
            if exact_decay:
                scores.append(exact_scores(d, r, p))
            else:
                scores.append(_dot_nt(stack_heads(qh_scr[d, r, ls]), kh_scr[d, r, ls]))
            inter.append(_dot_nt(stack_heads(qs_scr[d, r, ls]), sall_scr[d, ns[ni], p]))
        yield
        probs = [[jnp.where(tmask[d], sc[j * L:(j + 1) * L, :], 0.0).astype(BF16) for j in range(2)]
                 for (d, _, _), sc in zip(pairs, scores)]
        yield
        outs = {}
        for (d, ni, p), pr, it in zip(pairs, probs, inter):
            r = _chunk_rows(ns[ni])
            for j in range(2):
                vs = slice((2 * p + j) * DV, (2 * p + j + 1) * DV)
                outs[(d, ni, 2 * p + j)] = _dot(pr[j], v_ref[r, vs].astype(BF16)) + it[j * L:(j + 1) * L, :]
        yield
        for ni, n in enumerate(ns):
            r = _chunk_rows(n)
            for h in range(H_A):
                vs = slice(h * DV, (h + 1) * DV)
                o = outs[(0, ni, h)] + outs[(1, ni, h)]
                out_ref[r, vs] = (_rms(o, gw_ref[:, vs]) * _silu(g_ref[r, vs])).astype(out_ref.dtype)

    def finish():
        if write_state:
            for d in range(2):
                for p in range(n_pairs):
                    snew_ref[d, p] = st_scr[d, p].T

    return state_group, out_group, finish, decay_span


def _gla_scratch(T, HK):
    n_pairs = HK // LANES
    n_chunks = T // CHUNK
    return [
        pltpu.VMEM((2, n_pairs, LANES, LANES), F32),
        pltpu.VMEM((2, n_chunks, n_pairs, LANES, LANES), BF16),
        pltpu.VMEM((2, T, HK), BF16),
        pltpu.VMEM((2, T, HK), BF16),
        pltpu.VMEM((2, T, HK), BF16),
    ]


def _mlstm_body(qk_ref, v_ref, og_ref, sm_ref, c0_ref, n0_ref, m0_ref, cw_ref, bm_ref, gw_ref,
                out_ref, cnew_ref, nnew_ref, mnew_ref,
                pad_scr, qk_scr, y_scr, c_scr, n_scr, m_scr, call_scr, nall_scr, mall_scr, g_scr, f_scr,
                *, grid_w):
    has_state = c0_ref is not None
    write_state = cnew_ref is not None
    T = qk_ref.shape[0]
    L = CHUNK
    N = T // L
    C2 = qk_ref.shape[1]
    HK = C2 // 2
    DK = HK // H_B
    DV = v_ref.shape[1] // H_B
    scale = DK ** -0.5
    n_pairs = HK // LANES
    P = pad_scr.shape[0] - T
    P0 = P // 2
    rows_img = T // grid_w

    lower, upper = _chunk_masks(L)
    tri = (lower.astype(BF16), upper.astype(BF16))
    tmask = (lower, upper)
    lane = lax.broadcasted_iota(jnp.int32, (1, LANES), 1)
    head_mask = (lane < DK, lane >= DK)
    lane_in = lane & (L - 1)

    def lane_cummax(x, d):
        k = 1
        while k < L:
            if d == 0:
                x = jnp.maximum(x, jnp.where(lane_in >= k, pltpu.roll(x, k, axis=1), -jnp.inf))
            else:
                x = jnp.maximum(x, jnp.where(lane_in < L - k, pltpu.roll(x, LANES - k, axis=1), -jnp.inf))
            k *= 2
        return x

    for d in range(2):
        for p in range(n_pairs):
            if has_state:
                c_scr[d, p] = c0_ref[d, p]
                n_scr[2 * d + p:2 * d + p + 1, :] = jnp.concatenate(
                    [n0_ref[d, 2 * p + j:2 * p + j + 1, :] for j in range(2)], axis=1)
            else:
                c_scr[d, p] = jnp.zeros((LANES, LANES), F32)
                n_scr[2 * d + p:2 * d + p + 1, :] = jnp.zeros((1, LANES), F32)
    eye_h = (lax.broadcasted_iota(jnp.int32, (H_B, H_B), 0) == lax.broadcasted_iota(jnp.int32, (H_B, H_B), 1))

    def to_col(row):
        return jnp.sum(jnp.where(eye_h, row, 0.0), axis=1, keepdims=True)

    def to_row(col):
        return jnp.sum(jnp.where(eye_h, col, 0.0), axis=0, keepdims=True)

    for d in range(2):
        if has_state:
            m_scr[H_B * d:H_B * (d + 1), 0:1] = to_col(m0_ref[d:d + 1, :])
        else:
            m_scr[H_B * d:H_B * (d + 1), 0:1] = jnp.zeros((H_B, 1), F32)

    pad_scr[0:P0, :] = jnp.zeros((P0, C2), F32)
    pad_scr[P0 + T:P + T, :] = jnp.zeros((P - P0, C2), F32)

    def copy_in(i, carry):
        r0 = pl.multiple_of(i * L, L)
        pad_scr[pl.ds(P0 + r0, L), :] = qk_ref[pl.ds(r0, L), :]
        return carry

    lax.fori_loop(0, N, copy_in, 0)

    lane_c = lax.broadcasted_iota(jnp.int32, (1, C2), 1)
    qscale = jnp.where(lane_c < HK, scale, 1.0).astype(F32)
    sub = lax.broadcasted_iota(jnp.int32, (L, 1), 0)
    img_rows = (0,) if rows_img == 1 else (-1, 0, 1)

    def conv_tile(i, carry):
        r0 = pl.multiple_of(i * L, L)
        col = lax.rem(r0, grid_w) + sub
        ok_left = col >= 1
        ok_right = col <= grid_w - 2
        sums = [None, None, None]
        for di in img_rows:
            blk = pad_scr[pl.ds(P0 + r0 + di * grid_w - 8, L + 16), :]
            for k in range(3):
                term = blk * cw_ref[di + 1, k:k + 1, :]
                sums[k] = term if sums[k] is None else sums[k] + term
        acc = (sums[1][8:8 + L, :] + jnp.where(ok_left, sums[0][7:7 + L, :], 0.0)
               + jnp.where(ok_right, sums[2][9:9 + L, :], 0.0))
        qk_scr[pl.ds(r0, L), :] = _silu(acc) * qscale
        return carry

    lax.fori_loop(0, N, conv_tile, 0)

    gl = lane - GATE_LANE0
    is_f = ((gl >= H_B) & (gl < 2 * H_B)) | ((gl >= 3 * H_B) & (gl < 4 * H_B))

    def gate_tile(i, carry):
        rows = pl.ds(pl.multiple_of(i * L, L), L)
        x = sm_ref[rows, :] + bm_ref[...]
        y_scr[rows, :] = jnp.where(is_f, _log_sigmoid(x), x)
        return carry

    lax.fori_loop(0, N, gate_tile, 0)


    def state_group(ns, dirs=(0, 1)):
        units = [(d, n if d == 0 else N - 1 - n) for n in ns for d in dirs]
        rows = [_chunk_rows(n) for _, n in units]
        kt_all = [[qk_scr[r, HK + p * LANES:HK + (p + 1) * LANES].T for p in range(n_pairs)] for r in rows]
        yield
        xs = [y_scr[r, :] for r in rows]
        fsum = [_tri_sum(tri[d], x) for (d, _), x in zip(units, xs)]
        yield
        wk_all, f_end, c_end = [], [], []
        for (d, n), r, x, fs in zip(units, rows, xs, fsum):
            y = jnp.where(is_f, fs, x)
            li0 = GATE_LANE0 + 2 * H_B * d
            blk = jnp.concatenate([y, y], axis=0).T[li0:li0 + 2 * H_B, :]
            frow = pltpu.roll(blk, H_B, axis=0)
            grow = blk - frow
            g_scr[d, n] = grow
            f_scr[d, n] = frow
            e_col = L - 1 if d == 0 else 0
            f_end.append(frow[0:H_B, e_col:e_col + 1])
            ce8 = jnp.max(grow, axis=1, keepdims=True)
            c_end.append(ce8[0:H_B, :])
            wk_all.append(jnp.exp(grow[:, 0:L] - ce8))
        yield
        kv_all, ksum_all = [], []
        for r, wk8, kt_u in zip(rows, wk_all, kt_all):
            kv_u, ks_u = [], []
            wk8b = wk8.astype(BF16)
            for p in range(n_pairs):
                kpb = qk_scr[r, HK + p * LANES:HK + (p + 1) * LANES].astype(BF16)
                ks8 = _dot(wk8b, kpb)
                for j in range(2):
                    h = 2 * p + j
                    kwt = (kt_u[p][j * DK:(j + 1) * DK, :] * wk8[h:h + 1, :]).astype(BF16)
                    kv_u.append(_dot(kwt, v_ref[r, h * DV:(h + 1) * DV].astype(BF16)))
                    ks_u.append(ks8[h:h + 1, :])
            kv_all.append(kv_u)
            ksum_all.append(ks_u)
        yield
        m_run = {d: m_scr[H_B * d:H_B * (d + 1), 0:1] for d in dirs}
        a_all, b_all = [], []
        for (d, n), fe, ce in zip(units, f_end, c_end):
            mall_scr[d, n, 0:H_B, 0:1] = m_run[d]
            mx = jnp.maximum(m_run[d], ce)
            a_all.append(jnp.exp(m_run[d] - mx))
            b_all.append(jnp.exp(ce - mx))
            m_run[d] = fe + mx
        for d in dirs:
            m_scr[H_B * d:H_B * (d + 1), 0:1] = m_run[d]
        yield
        c_run = {d: [[c_scr[d, p, j * DK:(j + 1) * DK, :] for j in range(2)] for p in range(n_pairs)] for d in dirs}
        n_run = {d: [n_scr[2 * d + p:2 * d + p + 1, :] for p in range(n_pairs)] for d in dirs}
        for (d, n), a4, b4, kv_u, ks_u in zip(units, a_all, b_all, kv_all, ksum_all):
            for p in range(n_pairs):
                nall_scr[d, n, p:p + 1, :] = n_run[d][p]
                a_s = [a4[2 * p + j:2 * p + j + 1, :] for j in range(2)]
                b_s = [b4[2 * p + j:2 * p + j + 1, :] for j in range(2)]
                for j in range(2):
                    cj = c_run[d][p][j]
                    call_scr[d, n, p, j * DK:(j + 1) * DK, :] = cj.astype(BF16)
                    c_run[d][p][j] = a_s[j] * cj + b_s[j] * kv_u[2 * p + j]
                n_run[d][p] = (jnp.where(head_mask[0], a_s[0], a_s[1]) * n_run[d][p]
                               + jnp.where(head_mask[0], b_s[0] * ks_u[2 * p], b_s[1] * ks_u[2 * p + 1]))
        for d in dirs:
            for p in range(n_pairs):
                n_scr[2 * d + p:2 * d + p + 1, :] = n_run[d][p]
                for j in range(2):
                    c_scr[d, p, j * DK:(j + 1) * DK, :] = c_run[d][p][j]

    eye = lower & upper
    ones8 = jnp.ones((8, L), BF16)
    sub8 = lax.broadcasted_iota(jnp.int32, (8, LANES), 0)
    sub_h = lax.broadcasted_iota(jnp.int32, (H_B, L), 0)
    n_rows = [((sub8 == 2 * p) & head_mask[0]) | ((sub8 == 2 * p + 1) & head_mask[1]) for p in range(n_pairs)]

    def head_rows(vals):
        out = vals[0][0:H_B, :]
        for h in range(1, H_B):
            out = jnp.where(sub_h == h, vals[h][0:H_B, :], out)
        return out

    def out_group(ns):
        chunks = [(d, n) for n in ns for d in range(2)]
        pairs = [(d, n, p) for d, n in chunks for p in range(n_pairs)]
        units = [(d, n, p, j) for d, n, p in pairs for j in range(2)]
        cms = [lane_cummax(g_scr[d, n], d)[0:H_B, 0:L] for d, n in chunks]
        qk2s, qc2s, qn2s = [], [], []
        for d, n, p in pairs:
            r = _chunk_rows(n)
            qp = qk_scr[r, p * LANES:(p + 1) * LANES]
            q2 = jnp.concatenate([jnp.where(head_mask[j], qp, 0.0) for j in range(2)], axis=0).astype(BF16)
            qk2s.append(_dot_nt(q2, qk_scr[r, HK + p * LANES:HK + (p + 1) * LANES].astype(BF16)))
            qc2s.append(_dot(q2, call_scr[d, n, p]))
            nsel = jnp.where(n_rows[p], nall_scr[d, n, p:p + 1, :], 0.0).astype(BF16)
            qn2s.append(_dot_nt(nsel, qp.astype(BF16)))
        yield
        s_all = []
        for ui, (d, n, p, j) in enumerate(units):
            grow = g_scr[d, n, 2 * p + j:2 * p + j + 1, 0:L]
            e = jnp.where(tmask[d], grow, -jnp.inf)
            cmax = jnp.max(e, axis=-1, keepdims=True)
            s_all.append((qk2s[ui // 2][j * L:(j + 1) * L, :] * jnp.exp(e - cmax)).astype(BF16))
        yield
        nums =[_dot(s, v_ref[_chunk_rows(n), (2 * p + j) * DV:(2 * p + j + 1) * DV].astype(BF16))
                for (d, n, p, j), s in zip(units, s_all)]
        dens = [_dot_nt(ones8, s) for s in s_all]
        yield
        scales = []
        for ci, (d, n) in enumerate(chunks):
            den_loc = head_rows(dens[ci * H_B:(ci + 1) * H_B])
            qn = qn2s[ci * n_pairs][0:H_B, :]
            for p in range(1, n_pairs):
                qn = qn + qn2s[ci * n_pairs + p][0:H_B, :]
            cm = cms[ci]
            m_prev = mall_scr[d, n, 0:H_B, 0:1]
            delta = cm - m_prev
            t = jnp.exp(-jnp.abs(delta))
            w_loc = jnp.where(delta <= 0.0, t, 1.0)
            w_inter = jnp.where(delta <= 0.0, 1.0, t)
            mt = f_scr[d, n, 0:H_B, 0:L] + jnp.maximum(m_prev, cm)
            den = w_loc * den_loc + w_inter * qn
            rinv = 1.0 / jnp.maximum(jnp.abs(den), jnp.exp(-mt))
            scales.append((w_loc * rinv, w_inter * rinv))
        yield
        hs = []
        for ui, (d, n, p, j) in enumerate(units):
            h = 2 * p + j
            sc_loc, sc_inter = scales[ui // H_B]
            d_loc = jnp.where(eye, sc_loc[h:h + 1, :], 0.0).astype(BF16)
            d_inter = jnp.where(eye, sc_inter[h:h + 1, :], 0.0).astype(BF16)
            hs.append(_dot(d_loc, nums[ui].astype(BF16))
                      + _dot(d_inter, qc2s[ui // 2][j * L:(j + 1) * L, :].astype(BF16)))
        yield
        for ni, n in enumerate(ns):
            r = _chunk_rows(n)
            for h in range(H_B):
                vs = slice(h * DV, (h + 1) * DV)
                o = hs[(2 * ni) * H_B + h] + hs[(2 * ni + 1) * H_B + h]
                out_ref[r, vs] = (_rms(o, gw_ref[:, vs]) * _sigmoid(og_ref[r, vs])).astype(out_ref.dtype)

    def finish():
        if write_state:
            for d in range(2):
                for p in range(n_pairs):
                    cnew_ref[d, p] = c_scr[d, p]
                    for j in range(2):
                        nnew_ref[d, 2 * p + j:2 * p + j + 1, :] = n_scr[2 * d + p:2 * d + p + 1, j * DK:(j + 1) * DK]
                mnew_ref[d:d + 1, :] = to_row(m_scr[H_B * d:H_B * (d + 1), 0:1])

    return state_group, out_group, finish


def _mlstm_scratch(T, C2, grid_w):
    n_pairs = C2 // 2 // LANES
    n_chunks = T // CHUNK
    pad_rows = 2 * (grid_w + 8) if T // grid_w > 1 else 16
    return [
        pltpu.VMEM((T + pad_rows, C2), F32),
        pltpu.VMEM((T, C2), F32),
        pltpu.VMEM((T, SMALL_W), F32),
        pltpu.VMEM((2, n_pairs, LANES, LANES), F32),
        pltpu.VMEM((8, LANES), F32),
        pltpu.VMEM((8, LANES), F32),
        pltpu.VMEM((2, n_chunks, n_pairs, LANES, LANES), BF16),
        pltpu.VMEM((2, n_chunks, 8, LANES), F32),
        pltpu.VMEM((2, n_chunks, 8, LANES), F32),
        pltpu.VMEM((2, n_chunks, 8, LANES), F32),
        pltpu.VMEM((2, n_chunks, 8, LANES), F32),
    ]


N_GLA_SCRATCH = 5
N_MLSTM_SCRATCH = 11


def _scan_kernel(*refs, cols, layer, has_state, write_state, n_cast, ride_ada, grid_w, unroll):
    refs = list(refs)
    z_ref = refs.pop(0)
    s0_ref = c0_ref = n0_ref = m0_ref = None
    if has_state:
        s0_ref, c0_ref, n0_ref, m0_ref = refs[:4]
        del refs[:4]
    wa_ref, bal_ref, gwa_ref, cw_ref, bmg_ref, gwb_ref = refs[:6]
    del refs[:6]
    cast_in = refs[:n_cast]
    del refs[:n_cast]
    if ride_ada:
        ada_in = refs[:4]
        del refs[:4]
    outa_ref, outb_ref = refs[:2]
    del refs[:2]
    snew_ref = cnew_ref = nnew_ref = mnew_ref = None
    if write_state:
        snew_ref, cnew_ref, nnew_ref, mnew_ref = refs[:4]
        del refs[:4]
    cast_out = refs[:n_cast]
    del refs[:n_cast]
    if ride_ada:
        ada_out = refs.pop(0)
    wal_scr, bm_scr = refs[:2]
    del refs[:2]
    gla_scr = refs[:N_GLA_SCRATCH]
    mlstm_scr = refs[N_GLA_SCRATCH:]

    for src, dst in zip(cast_in, cast_out):
        dst[...] = src[...].astype(BF16)
    if ride_ada:
        _ada_tile(*ada_in, ada_out)

    R, HK = wa_ref.shape[1], wa_ref.shape[2]
    wal_scr[...] = jnp.zeros(wal_scr.shape, BF16)
    for d in range(2):
        wal_scr[d * R:(d + 1) * R, d * HK:(d + 1) * HK] = wa_ref[d].astype(BF16)
    lane = lax.broadcasted_iota(jnp.int32, (1, LANES), 1)
    bm = jnp.zeros((1, LANES), F32)
    for g in range(bmg_ref.shape[1]):
        for h in range(H_B):
            bm = jnp.where(lane == GATE_LANE0 + H_B * g + h, bmg_ref[layer, g, h], bm)
    bm_scr[0:1, :] = bm

    def view(name):
        c0, w = cols[name]
        return z_ref.at[:, pl.ds(c0, w)]

    sm_ref = view("small")
    n_chunks = z_ref.shape[0] // CHUNK
    gla = _gla_body(view("qa"), view("ka"), view("va"), view("ga"), sm_ref, s0_ref, wal_scr, bal_ref, gwa_ref,
                    outa_ref, snew_ref, *gla_scr)
    mlstm = _mlstm_body(view("qkb"), view("vb"), view("ob"), sm_ref, c0_ref, n0_ref, m0_ref, cw_ref,
                        bm_scr.at[0:1, :], gwb_ref, outb_ref, cnew_ref, nnew_ref, mnew_ref, *mlstm_scr,
                        grid_w=grid_w)
    gla_state, gla_out, gla_finish, decay_span = gla
    mlstm_state, mlstm_out, mlstm_finish = mlstm

    def passes(gla_out_fn):
        _chunk_loop(n_chunks, unroll, lambda ns: [fn([n], (d,)) for n in ns for d in range(2)
                                                  for fn in (mlstm_state, gla_state)])
        _chunk_loop(n_chunks, unroll, lambda ns: [fn([n]) for n in ns for fn in (mlstm_out, gla_out_fn)])

    wide_decay = decay_span > GLA_FACTORED_DECAY_MAX

    @pl.when(jnp.logical_not(wide_decay))
    def _():
        passes(gla_out)

    @pl.when(wide_decay)
    def _():
        passes(functools.partial(gla_out, exact_decay=True))

    gla_finish()
    mlstm_finish()


def _scan_call(z2d, row0, B, T, states, lw, layer, *, grid_w, write_state, casts=(), ada=None):
    n_z = z2d.shape[1]
    assert row0 % T == 0 and z2d.shape[0] % T == 0
    z3 = z2d.reshape(z2d.shape[0] // T, T, n_z)
    blk0 = row0 // T
    HK = lw["w_alpha2"].shape[-1]
    DA = lw["gnorm_a_w"].shape[0]
    C2 = lw["conv_w"].shape[-1]
    DB = lw["gnorm_b_w"].shape[0]
    DK_A, DK_B = HK // H_A, C2 // 2 // H_B
    pa, pb = HK // LANES, C2 // 2 // LANES
    n_chunks = T // CHUNK
    has_state = states is not None
    widths = (("qa", HK), ("ka", HK), ("va", DA), ("ga", DA), ("qkb", C2), ("vb", DB), ("ob", DB),
              ("small", SMALL_W))
    cols, c0 = {}, 0
    for name, w in widths:
        cols[name] = (c0, w)
        c0 += w
    assert c0 == n_z
    cast_in_specs, cast_out_specs, cast_out_shape, cast_args = _cast_specs(casts, B)
    kern = functools.partial(_scan_kernel, cols=cols, layer=layer, has_state=has_state, write_state=write_state,
                             n_cast=len(casts), ride_ada=ada is not None, grid_w=grid_w,
                             unroll=min(n_chunks, SCAN_UNROLL))

    def per_batch(shape):
        nd = len(shape)
        return pl.BlockSpec((None,) + tuple(shape), lambda b: (b,) + (0,) * nd)

    def per_batch_layer(shape):
        nd = len(shape)
        return pl.BlockSpec((None, None) + tuple(shape), lambda b: (b, layer) + (0,) * nd)

    def of_layer(a):
        return pl.BlockSpec((None,) + a.shape[1:], lambda b: (layer,) + (0,) * (a.ndim - 1))

    def whole(a):
        return pl.BlockSpec(a.shape, lambda b: (0,) * a.ndim)

    state_shapes = ((2, pa, LANES, LANES), (2, pb, LANES, LANES), (2, H_B, DK_B), (2, H_B))
    in_specs = [pl.BlockSpec((None, T, n_z), lambda b: (b + blk0, 0, 0))]
    args = [z3]
    if has_state:
        s_gla, s_c, s_n, s_m = states
        depth = s_gla.shape[1]
        args += [s_gla.reshape((B, depth) + state_shapes[0]), s_c.reshape((B, depth) + state_shapes[1]), s_n, s_m]
        in_specs += [per_batch_layer(s) for s in state_shapes]
    args += [lw["w_alpha2"], lw["b_alpha"], lw["gnorm_a_w"].reshape(1, DA), lw["conv_w"], lw["b_mgate"],
             lw["gnorm_b_w"].reshape(1, DB)]
    in_specs += [of_layer(lw["w_alpha2"]), of_layer(lw["b_alpha"]), pl.BlockSpec((1, DA), lambda b: (0, 0)),
                 whole(lw["conv_w"]), pl.BlockSpec(memory_space=pltpu.SMEM), pl.BlockSpec((1, DB), lambda b: (0, 0))]
    args += cast_args
    in_specs += cast_in_specs
    out_specs = [per_batch((T, DA)), per_batch((T, DB))]
    out_shape = [jax.ShapeDtypeStruct((B, T, DA), BF16), jax.ShapeDtypeStruct((B, T, DB), BF16)]
    if write_state:
        out_specs += [per_batch(s) for s in state_shapes]
        out_shape += [jax.ShapeDtypeStruct((B,) + s, F32) for s in state_shapes]
    out_specs += cast_out_specs
    out_shape += cast_out_shape
    if ada is not None:
        cc, c, w_ada, b_ada, col0 = ada
        n_rest = w_ada.shape[1] - col0
        wcol = n_rest // B
        assert n_rest % B == 0 and wcol % LANES == 0 and col0 % wcol == 0
        args += [cc, c, w_ada, b_ada]
        in_specs += [whole(cc), whole(c),
                     pl.BlockSpec((w_ada.shape[0], wcol), lambda b: (0, col0 // wcol + b)),
                     pl.BlockSpec((1, wcol), lambda b: (0, col0 // wcol + b))]
        out_specs.append(pl.BlockSpec((COND_ROWS, wcol), lambda b: (0, b)))
        out_shape.append(jax.ShapeDtypeStruct((COND_ROWS, n_rest), F32))
    scratch = ([pltpu.VMEM((SMALL_W, 2 * HK), BF16), pltpu.VMEM((8, LANES), F32)]
               + _gla_scratch(T, HK) + _mlstm_scratch(T, C2, grid_w))
    assert len(scratch) == 2 + N_GLA_SCRATCH + N_MLSTM_SCRATCH
    return pl.pallas_call(
        kern,
        grid=(B,),
        in_specs=in_specs,
        out_specs=out_specs,
        out_shape=out_shape,
        scratch_shapes=scratch,
        compiler_params=pltpu.CompilerParams(dimension_semantics=("arbitrary",),
                                             vmem_limit_bytes=VMEM_LIMIT),
        name="mixer_scans",
    )(*args)


def _outff_kernel(xc_ref, xl_ref, ac_ref, al_ref, bc_ref, bl_ref, mod_ref, n2_ref, fn_ref, wo_ref, w1_ref, w2_ref,
                  yc_ref, yl_ref, *, n_ctx, tiles_per_req, ff_chunk, final_norm):
    D = xc_ref.shape[1]
    DA = ac_ref.shape[1]
    is_ctx, row = _tile_group(n_ctx, tiles_per_req)

    def mod(k):
        return mod_ref[pl.ds(row, 1), (k - MOD_SPLIT) * D:(k - MOD_SPLIT + 1) * D]

    def tile(x_ref, a_ref, b_ref, y_ref):
        y = _dot(a_ref[...], wo_ref[0:DA, :]) + _dot(b_ref[...], wo_ref[DA:, :])
        x1 = x_ref[...] + mod(2) * y
        h2 = (_rms(x1, n2_ref[...]) * (1.0 + mod(4)) + mod(3)).astype(BF16)
        acc = jnp.zeros(x1.shape, F32)
        for c0 in range(0, w1_ref.shape[1], ff_chunk):
            u = jnp.maximum(_dot(h2, w1_ref[:, c0:c0 + ff_chunk]), 0.0)
            acc = acc + _dot((u * u).astype(BF16), w2_ref[c0:c0 + ff_chunk, :])
        x2 = x1 + mod(5) * acc
        y_ref[...] = _rms(x2, fn_ref[...]) if final_norm else x2

    @pl.when(is_ctx)
    def _():
        tile(xc_ref, ac_ref, bc_ref, yc_ref)

    @pl.when(jnp.logical_not(is_ctx))
    def _():
        tile(xl_ref, al_ref, bl_ref, yl_ref)


def _outff_call(xc2d, xl2d, ac, al, bc, bl, mod, norm2_w, final_w, wo, w1, w2, *, tm, tiles_per_req, final_norm):
    (Mc, D), Ml = xc2d.shape, xl2d.shape[0]
    n_ctx = Mc // tm
    DA = ac.shape[1]
    DFF = w1.shape[1]
    kern = functools.partial(_outff_kernel, n_ctx=n_ctx, tiles_per_req=tiles_per_req, ff_chunk=FF_CHUNK,
                             final_norm=final_norm)
    once = pl.Buffered(1)
    ctx, lat = _ctx_tile(n_ctx), _lat_tile(n_ctx)
    return pl.pallas_call(
        kern,
        grid=((Mc + Ml) // tm,),
        in_specs=[
            pl.BlockSpec((tm, D), ctx), pl.BlockSpec((tm, D), lat),
            pl.BlockSpec((tm, DA), ctx), pl.BlockSpec((tm, DA), lat),
            pl.BlockSpec((tm, D - DA), ctx), pl.BlockSpec((tm, D - DA), lat),
            pl.BlockSpec(mod.shape, lambda i: (0, 0)),
            pl.BlockSpec((1, D), lambda i: (0, 0)),
            pl.BlockSpec((1, D), lambda i: (0, 0)),
            pl.BlockSpec((D, D), lambda i: (0, 0), pipeline_mode=once),
            pl.BlockSpec((D, DFF), lambda i: (0, 0), pipeline_mode=once),
            pl.BlockSpec((DFF, D), lambda i: (0, 0), pipeline_mode=once),
        ],
        out_specs=[pl.BlockSpec((tm, D), ctx), pl.BlockSpec((tm, D), lat)],
        out_shape=[jax.ShapeDtypeStruct((Mc, D), F32), jax.ShapeDtypeStruct((Ml, D), F32)],
        compiler_params=pltpu.CompilerParams(dimension_semantics=("arbitrary",),
                                             vmem_limit_bytes=VMEM_LIMIT),
        name="outproj_mlp",
    )(xc2d, xl2d, ac, al, bc, bl, mod, norm2_w.reshape(1, D), final_w.reshape(1, D), wo, w1, w2)


def _layer(xc, xl, cond, ada_w, cached, lw, layer, ffw, final_w, final_norm):
    (Bc, Tc, D), (Bl, Tl, _) = xc.shape, xl.shape
    tm = TOKEN_TILE
    assert (Bc * Tc) % tm == 0 and Tl % tm == 0 and (Bc * Tc) % Tl == 0
    xc2d, xl2d = xc.reshape(Bc * Tc, D), xl.reshape(Bl * Tl, D)
    z = _inproj_call(xc2d, xl2d, *cond, *ada_w, lw["norm1_w"], lw["w_in_t"], tm=tm, tiles_per_req=Tl // tm,
                     big_rows=lw["big_rows"], small_rows=lw["small_rows"])
    res_c = _scan_call(z, 0, Bc, Tc, None, lw, layer, grid_w=Tc, write_state=True,
                       casts=((ffw[0], 0), (ffw[1], 0), (ffw[2], 0)), ada=(*cond, *ada_w, MOD_SPLIT * D))
    res_l = _scan_call(z, Bc * Tc, Bl, Tl, cached, lw, layer, grid_w=GRID_W, write_state=False)
    wo_b, w1_b, w2_b, mod_out = res_c[-4:]
    yc, yl = _outff_call(xc2d, xl2d, res_c[0].reshape(Bc * Tc, -1), res_l[0].reshape(Bl * Tl, -1),
                         res_c[1].reshape(Bc * Tc, -1), res_l[1].reshape(Bl * Tl, -1), mod_out, lw["norm2_w"],
                         final_w, wo_b, w1_b, w2_b, tm=tm, tiles_per_req=Tl // tm, final_norm=final_norm)
    return yc.reshape(Bc, Tc, D), yl.reshape(Bl, Tl, D), tuple(res_c[2:6])


def _layer_weights(l, norm1_w, norm2_w, w_in, w_alpha2, b_alpha, b_mgate, conv_w, gnorm_a_w, gnorm_b_w):
    hk_a = w_alpha2.shape[-1]
    d_a = gnorm_a_w.shape[-1]
    d_b = gnorm_b_w.shape[-1]
    hk_b = conv_w.shape[-1] // 2
    sizes = (hk_a, hk_a, d_a, d_a, 2 * R_ALPHA, hk_b, hk_b, d_b, d_b, 4 * H_B)
    assert w_alpha2.shape[2] == R_ALPHA and b_mgate.shape[1] * b_mgate.shape[2] == 4 * H_B
    offs = [0]
    for s in sizes:
        offs.append(offs[-1] + s)
    big_rows = ((offs[0], offs[4] - offs[0]), (offs[5], offs[9] - offs[5]))
    small_rows = ((offs[4], offs[5] - offs[4]), (offs[9], offs[10] - offs[9]))
    assert all(n % LANES == 0 and r % 16 == 0 for r, n in big_rows)
    return dict(
        norm1_w=norm1_w[l], norm2_w=norm2_w[l], w_in_t=jnp.swapaxes(w_in[l], 0, 1),
        big_rows=big_rows, small_rows=small_rows,
        w_alpha2=w_alpha2, b_alpha=b_alpha, b_mgate=b_mgate, conv_w=conv_w[l],
        gnorm_a_w=gnorm_a_w[l], gnorm_b_w=gnorm_b_w[l],
    )


def kernel(x_prompt, x_sample, c, state_gla, state_mlstm_C, state_mlstm_n, state_mlstm_m, c_ctx, w_ada, b_ada, norm1_w, norm2_w, w_in, w_alpha2, b_alpha, b_mgate, conv_w, gnorm_a_w, gnorm_b_w, w_out, w_ff1, w_ff2, final_norm_w):
    depth = w_in.shape[0]
    D = x_prompt.shape[-1]
    Bp, Tp, _ = x_prompt.shape
    Bs = x_sample.shape[0]
    assert 1 + Bs <= COND_ROWS
    cond = (c_ctx.reshape(1, D), c)
    cached = (state_gla, state_mlstm_C, state_mlstm_n, state_mlstm_m)
    xp, xs = x_prompt, x_sample
    s_gla, s_c, s_n, s_m = [], [], [], []
    for l in range(depth):
        lw = _layer_weights(l, norm1_w, norm2_w, w_in, w_alpha2, b_alpha, b_mgate, conv_w,
                            gnorm_a_w, gnorm_b_w)
        xp, xs, ctx = _layer(xp, xs, cond, (w_ada[l], b_ada[l].reshape(1, -1)), cached, lw, l,
                             (w_out[l], w_ff1[l], w_ff2[l]), final_norm_w, l == depth - 1)
        s_gla.append(ctx[0].reshape(Bp, 2, H_A, -1, ctx[0].shape[-1]))
        s_c.append(ctx[1].reshape(Bp, 2, H_B, -1, ctx[1].shape[-1]))
        s_n.append(ctx[2])
        s_m.append(ctx[3])
    dt = x_prompt.dtype
    return (xp, xs, jnp.stack(s_gla, axis=1).astype(dt), jnp.stack(s_c, axis=1).astype(dt),
            jnp.stack(s_n, axis=1).astype(dt), jnp.stack(s_m, axis=1).astype(dt))
```

```python
import functools
import math

import jax
import jax.numpy as jnp
from jax import lax
from jax.experimental import pallas as pl
from jax.experimental.pallas import tpu as pltpu

F32 = jnp.float32
BF16 = jnp.bfloat16

GRID_W = 64
H_A = 4
H_B = 4
R_ALPHA = 16
TAU_GLA = 16.0
CHUNK = 64
EPS = 1e-6
LANES = 128
SUBLANES = 8
BF16_ROWS = 16
COND_ROWS = SUBLANES
SMALL_W = LANES
GATE_LANE0 = 2 * R_ALPHA
VMEM_LIMIT = 56 * 1024 * 1024
SCAN_UNROLL = 4
PIPELINE_STARTS = 8
TOKEN_TILE = 512
FF_CHUNK = 512
GLA_FACTORED_DECAY_MAX = 60.0
MOD_SPLIT = 2


def _sigmoid(x):
    return 1.0 / (1.0 + jnp.exp(-x))


def _silu(x):
    return x * _sigmoid(x)


def _log_sigmoid(x):
    return jnp.minimum(x, 0.0) - jnp.log(1.0 + jnp.exp(-jnp.abs(x)))


def _dot(a, b):
    return jnp.dot(a, b, preferred_element_type=F32)


def _dot_nt(a, b):
    return lax.dot_general(a, b, (((1,), (1,)), ((), ())), preferred_element_type=F32)


def _rms(x, w):
    return x * lax.rsqrt(jnp.mean(x * x, axis=-1, keepdims=True) + EPS) * w


def _tri_sum(tri, x, terms=3):
    acc, rest = None, x
    for t in range(terms):
        part = rest.astype(BF16)
        prod = _dot(tri, part)
        acc = prod if acc is None else acc + prod
        if t + 1 < terms:
            rest = rest - part.astype(F32)
    return acc


def _chunk_masks(L):
    row = lax.broadcasted_iota(jnp.int32, (L, L), 0)
    col = lax.broadcasted_iota(jnp.int32, (L, L), 1)
    lower = row >= col
    upper = row <= col
    return lower, upper


def _ada_tile(cc_ref, c_ref, w_ref, b_ref, o_ref):
    D = cc_ref.shape[1]
    sub = lax.broadcasted_iota(jnp.int32, (COND_ROWS, D), 0)
    cond = jnp.where(sub == 0, cc_ref[...], 0.0)
    for r in range(c_ref.shape[0]):
        cond = jnp.where(sub == 1 + r, c_ref[r:r + 1, :], cond)
    o_ref[...] = _dot(_silu(cond).astype(BF16), w_ref[...].astype(BF16)) + b_ref[...]


def _tile_group(n_ctx, tiles_per_req):
    i = pl.program_id(0)
    is_ctx = i < n_ctx
    row = jnp.where(is_ctx, 0, 1 + jnp.maximum(i - n_ctx, 0) // tiles_per_req)
    return is_ctx, row


def _ctx_tile(n_ctx):
    return lambda i: (jnp.minimum(i, n_ctx - 1), 0)


def _lat_tile(n_ctx):
    return lambda i: (jnp.maximum(i - n_ctx, 0), 0)


def _inproj_kernel(xc_ref, xl_ref, cc_ref, c_ref, wa_ref, ba_ref, nw_ref, wt_ref, z_ref, wb_scr, mod_ref,
                   *, n_ctx, tiles_per_req, big_rows, small_rows):
    D = xc_ref.shape[1]

    @pl.when(pl.program_id(0) == 0)
    def _():
        _ada_tile(cc_ref, c_ref, wa_ref, ba_ref, mod_ref)
        col = 0
        for r0, n in big_rows:
            for k in range(n // LANES):
                blk = wt_ref[r0 + k * LANES:r0 + (k + 1) * LANES, :]
                wb_scr[:, col:col + LANES] = blk.T.astype(BF16)
                col += LANES
        parts = [wt_ref[r0:r0 + n, :] for r0, n in small_rows]
        n_small = sum(n for _, n in small_rows)
        parts.append(jnp.zeros((SMALL_W - n_small, D), F32))
        wb_scr[:, col:col + SMALL_W] = jnp.concatenate(parts, axis=0).T.astype(BF16)

    is_ctx, row = _tile_group(n_ctx, tiles_per_req)

    def tile(x_ref):
        sh1 = mod_ref[pl.ds(row, 1), 0:D]
        sc1 = mod_ref[pl.ds(row, 1), D:2 * D]
        h = _rms(x_ref[...], nw_ref[...]) * (1.0 + sc1) + sh1
        z_ref[...] = _dot(h.astype(BF16), wb_scr[...])

    @pl.when(is_ctx)
    def _():
        tile(xc_ref)

    @pl.when(jnp.logical_not(is_ctx))
    def _():
        tile(xl_ref)


def _inproj_call(xc2d, xl2d, cc, c, w_ada, b_ada, norm_w, w_in_t, *, tm, tiles_per_req, big_rows, small_rows):
    (Mc, D), Ml = xc2d.shape, xl2d.shape[0]
    n_ctx = Mc // tm
    n_out = sum(n for _, n in big_rows) + SMALL_W
    n_mod = MOD_SPLIT * D
    kern = functools.partial(_inproj_kernel, n_ctx=n_ctx, tiles_per_req=tiles_per_req,
                             big_rows=big_rows, small_rows=small_rows)
    once = pl.Buffered(1)
    return pl.pallas_call(
        kern,
        grid=((Mc + Ml) // tm,),
        in_specs=[
            pl.BlockSpec((tm, D), _ctx_tile(n_ctx)),
            pl.BlockSpec((tm, D), _lat_tile(n_ctx)),
            pl.BlockSpec(cc.shape, lambda i: (0, 0)),
            pl.BlockSpec(c.shape, lambda i: (0, 0)),
            pl.BlockSpec((D, n_mod), lambda i: (0, 0), pipeline_mode=once),
            pl.BlockSpec((1, n_mod), lambda i: (0, 0)),
            pl.BlockSpec((1, D), lambda i: (0, 0)),
            pl.BlockSpec(w_in_t.shape, lambda i: (0, 0), pipeline_mode=once),
        ],
        out_specs=pl.BlockSpec((tm, n_out), lambda i: (i, 0)),
        out_shape=jax.ShapeDtypeStruct((Mc + Ml, n_out), F32),
        scratch_shapes=[pltpu.VMEM((D, n_out), BF16), pltpu.VMEM((COND_ROWS, n_mod), F32)],
        compiler_params=pltpu.CompilerParams(dimension_semantics=("arbitrary",),
                                             vmem_limit_bytes=VMEM_LIMIT),
        name="norm_inproj",
    )(xc2d, xl2d, cc, c, w_ada, b_ada, norm_w.reshape(1, D), w_in_t)


def _chunk_loop(n_chunks, unroll, make_units):
    def step(ns):
        pending = list(make_units(ns))
        active = []
        while pending or active:
            for _ in range(min(PIPELINE_STARTS, len(pending))):
                active.append(pending.pop(0))
            alive = []
            for g in active:
                try:
                    next(g)
                    alive.append(g)
                except StopIteration:
                    pass
            active = alive

    if unroll >= n_chunks:
        step(list(range(n_chunks)))
        return

    def body(i, carry):
        step([i * unroll + u for u in range(unroll)])
        return carry

    lax.fori_loop(0, n_chunks // unroll, body, 0)


def _chunk_rows(n):
    if isinstance(n, int):
        return pl.ds(n * CHUNK, CHUNK)
    return pl.ds(pl.multiple_of(n * CHUNK, CHUNK), CHUNK)


def _cast_specs(casts, n_steps):
    in_specs, out_specs, out_shape, args = [], [], [], []
    for w, axis in casts:
        blk = list(w.shape)
        assert blk[axis] % n_steps == 0
        blk[axis] //= n_steps
        assert blk[0] % BF16_ROWS == 0 and blk[1] % LANES == 0
        idx = (lambda b: (b, 0)) if axis == 0 else (lambda b: (0, b))
        in_specs.append(pl.BlockSpec(tuple(blk), idx))
        out_specs.append(pl.BlockSpec(tuple(blk), idx))
        out_shape.append(jax.ShapeDtypeStruct(w.shape, BF16))
        args.append(w)
    return in_specs, out_specs, out_shape, args


def _gla_body(q_ref, k_ref, v_ref, g_ref, sm_ref, s0_ref, wal_ref, bal_ref, gw_ref, out_ref, snew_ref,
              st_scr, sall_scr, qh_scr, qs_scr, kh_scr):
    has_state = s0_ref is not None
    write_state = snew_ref is not None
    T = q_ref.shape[0]
    L = CHUNK
    N = T // L
    HK = q_ref.shape[1]
    DK = HK // H_A
    DV = v_ref.shape[1] // H_A
    scale = DK ** -0.5
    n_pairs = HK // LANES

    lower, upper = _chunk_masks(L)
    tri = (lower.astype(BF16), upper.astype(BF16))
    tmask = (lower, upper)
    lane = lax.broadcasted_iota(jnp.int32, (1, LANES), 1)
    head_mask = (lane < DK, lane >= DK)

    for d in range(2):
        for p in range(n_pairs):
            if has_state:
                st_scr[d, p] = s0_ref[d, p].T
            else:
                st_scr[d, p] = jnp.zeros((LANES, LANES), F32)

    def decay_pre(d, r):
        return _dot(sm_ref[r, :].astype(BF16), wal_ref[:, d * HK:(d + 1) * HK]) + bal_ref[d:d + 1, :]

    neg_pre = jnp.maximum(-(_dot(sm_ref[...].astype(BF16), wal_ref[...])
                            + jnp.concatenate([bal_ref[0:1, :], bal_ref[1:2, :]], axis=1)), 0.0)
    chunk_sums = jnp.sum(neg_pre.reshape(N, L, 2 * HK), axis=1)
    decay_span = (jnp.max(chunk_sums) + L * math.log(2.0)) * (1.0 / TAU_GLA)

    def state_group(ns, dirs=(0, 1)):
        units = [(d, n if d == 0 else N - 1 - n) for n in ns for d in dirs]
        rows = [_chunk_rows(n) for _, n in units]
        vt_all = [[jnp.concatenate([v_ref[r, (2 * p + j) * DV:(2 * p + j + 1) * DV] for j in range(2)],
                                   axis=0).T.astype(BF16) for p in range(n_pairs)] for r in rows]
        yield
        pre = [decay_pre(d, r) for (d, _), r in zip(units, rows)]
        yield
        g = [_log_sigmoid(x) * (1.0 / TAU_GLA) for x in pre]
        yield
        b = [_tri_sum(tri[d], gi, terms=2) for (d, _), gi in zip(units, g)]
        yield
        ks_all, dec_all = [], []
        for (d, _), r, bi in zip(units, rows, b):
            bend = bi[L - 1:L, :] if d == 0 else bi[0:1, :]
            q = q_ref[r, :] * scale
            ks = (k_ref[r, :] * jnp.exp(bend - bi)).astype(BF16)
            qs = q * jnp.exp(bi)
            qh_scr[d, r, :] = (qs * jnp.exp(-bend)).astype(BF16)
            qs_scr[d, r, :] = qs.astype(BF16)
            kh_scr[d, r, :] = ks
            ks_all.append(ks)
            dec_all.append(jnp.exp(bend))
        yield
        upd_all = []
        for vt_u, ks in zip(vt_all, ks_all):
            upd_u = []
            for p in range(n_pairs):
                kp = ks[:, p * LANES:(p + 1) * LANES]
                kk = jnp.concatenate([jnp.where(head_mask[j], kp, jnp.zeros_like(kp)) for j in range(2)], axis=0)
                upd_u.append(_dot(vt_u[p], kk))
            upd_all.append(upd_u)
        yield
        st = {d: [st_scr[d, p] for p in range(n_pairs)] for d in dirs}
        for (d, n), dec, upd in zip(units, dec_all, upd_all):
            for p in range(n_pairs):
                sall_scr[d, n, p] = st[d][p].astype(BF16)
                st[d][p] = st[d][p] * dec[:, p * LANES:(p + 1) * LANES] + upd[p]
        for d in dirs:
            for p in range(n_pairs):
                st_scr[d, p] = st[d][p]

    def stack_heads(x):
        return jnp.concatenate([jnp.where(head_mask[j], x, jnp.zeros_like(x)) for j in range(2)], axis=0)

    tok = lax.broadcasted_iota(jnp.int32, (L, 1), 0)
    row_t = lax.broadcasted_iota(jnp.int32, (2 * L, L), 0) & (L - 1)
    col_s = lax.broadcasted_iota(jnp.int32, (2 * L, L), 1)

    def exact_scores(d, r, p):
        ls = slice(p * LANES, (p + 1) * LANES)
        b = _tri_sum(tri[d], _log_sigmoid(decay_pre(d, r)[:, ls]) * (1.0 / TAU_GLA))
        q = q_ref[r, ls] * scale
        k = k_ref[r, ls]
        acc = jnp.where(row_t == col_s, _dot_nt(stack_heads(q).astype(BF16), k.astype(BF16)), 0.0)
        src = lax.broadcasted_iota(jnp.int32, (L, L), 1)
        h = L // 2
        while h >= 1:
            first = tok & ~(2 * h - 1)
            edge = first + (h - 1 if d == 0 else h)
            b_edge = _tri_sum((src == edge).astype(BF16), b)
            upper = (tok & (2 * h - 1)) >= h
            later, earlier = (upper, ~upper) if d == 0 else (~upper, upper)
            qt = jnp.where(later, q * jnp.exp(b - b_edge), 0.0)
            kt = jnp.where(earlier, k * jnp.exp(b_edge - b), 0.0)
            sc = _dot_nt(stack_heads(qt).astype(BF16), kt.astype(BF16))
            acc = acc + jnp.where((row_t & ~(2 * h - 1)) == (col_s & ~(2 * h - 1)), sc, 0.0)
            h //= 2
        return acc

    def out_group(ns, exact_decay=False):
        pairs = [(d, ni, p) for ni in range(len(ns)) for d in range(2) for p in range(n_pairs)]
        scores, inter = [], []
        for d, ni, p in pairs:
            r = _chunk_rows(ns[ni])
            ls = slice(p * LANES, (p + 1) * LANES)
            if exact_decay:
                scores.append(exact_scores(d, r, p))
            else:
                scores.append(_dot_nt(stack_heads(qh_scr[d, r, ls]), kh_scr[d, r, ls]))
            inter.append(_dot_nt(stack_heads(qs_scr[d, r, ls]), sall_scr[d, ns[ni], p]))
        yield
        probs = [[jnp.where(tmask[d], sc[j * L:(j + 1) * L, :], 0.0).astype(BF16) for j in range(2)]
                 for (d, _, _), sc in zip(pairs, scores)]
        yield
        outs = {}
        for (d, ni, p), pr, it in zip(pairs, probs, inter):
            r = _chunk_rows(ns[ni])
            for j in range(2):
                vs = slice((2 * p + j) * DV, (2 * p + j + 1) * DV)
                outs[(d, ni, 2 * p + j)] = _dot(pr[j], v_ref[r, vs].astype(BF16)) + it[j * L:(j + 1) * L, :]
        yield
        for ni, n in enumerate(ns):
            r = _chunk_rows(n)
            for h in range(H_A):
                vs = slice(h * DV, (h + 1) * DV)
                o = outs[(0, ni, h)] + outs[(1, ni, h)]
                out_ref[r, vs] = (_rms(o, gw_ref[:, vs]) * _silu(g_ref[r, vs])).astype(out_ref.dtype)

    def finish():
        if write_state:
            for d in range(2):
                for p in range(n_pairs):
                    snew_ref[d, p] = st_scr[d, p].T

    return state_group, out_group, finish, decay_span


def _gla_scratch(T, HK):
    n_pairs = HK // LANES
    n_chunks = T // CHUNK
    return [
        pltpu.VMEM((2, n_pairs, LANES, LANES), F32),
        pltpu.VMEM((2, n_chunks, n_pairs, LANES, LANES), BF16),
        pltpu.VMEM((2, T, HK), BF16),
        pltpu.VMEM((2, T, HK), BF16),
        pltpu.VMEM((2, T, HK), BF16),
    ]


def _mlstm_body(qk_ref, v_ref, og_ref, sm_ref, c0_ref, n0_ref, m0_ref, cw_ref, bm_ref, gw_ref,
                out_ref, cnew_ref, nnew_ref, mnew_ref,
                pad_scr, qk_scr, y_scr, c_scr, n_scr, m_scr, call_scr, nall_scr, mall_scr, g_scr, f_scr,
                *, grid_w):
    has_state = c0_ref is not None
    write_state = cnew_ref is not None
    T = qk_ref.shape[0]
    L = CHUNK
    N = T // L
    C2 = qk_ref.shape[1]
    HK = C2 // 2
    DK = HK // H_B
    DV = v_ref.shape[1] // H_B
    scale = DK ** -0.5
    n_pairs = HK // LANES
    P = pad_scr.shape[0] - T
    P0 = P // 2
    rows_img = T // grid_w

    lower, upper = _chunk_masks(L)
    tri = (lower.astype(BF16), upper.astype(BF16))
    tmask = (lower, upper)
    lane = lax.broadcasted_iota(jnp.int32, (1, LANES), 1)
    head_mask = (lane < DK, lane >= DK)
    lane_in = lane & (L - 1)

    def lane_cummax(x, d):
        k = 1
        while k < L:
            if d == 0:
                x = jnp.maximum(x, jnp.where(lane_in >= k, pltpu.roll(x, k, axis=1), -jnp.inf))
            else:
                x = jnp.maximum(x, jnp.where(lane_in < L - k, pltpu.roll(x, LANES - k, axis=1), -jnp.inf))
            k *= 2
        return x

    for d in range(2):
        for p in range(n_pairs):
            if has_state:
                c_scr[d, p] = c0_ref[d, p]
                n_scr[2 * d + p:2 * d + p + 1, :] = jnp.concatenate(
                    [n0_ref[d, 2 * p + j:2 * p + j + 1, :] for j in range(2)], axis=1)
            else:
                c_scr[d, p] = jnp.zeros((LANES, LANES), F32)
                n_scr[2 * d + p:2 * d + p + 1, :] = jnp.zeros((1, LANES), F32)
    eye_h = (lax.broadcasted_iota(jnp.int32, (H_B, H_B), 0) == lax.broadcasted_iota(jnp.int32, (H_B, H_B), 1))

    def to_col(row):
        return jnp.sum(jnp.where(eye_h, row, 0.0), axis=1, keepdims=True)

    def to_row(col):
        return jnp.sum(jnp.where(eye_h, col, 0.0), axis=0, keepdims=True)

    for d in range(2):
        if has_state:
            m_scr[H_B * d:H_B * (d + 1), 0:1] = to_col(m0_ref[d:d + 1, :])
        else:
            m_scr[H_B * d:H_B * (d + 1), 0:1] = jnp.zeros((H_B, 1), F32)

    pad_scr[0:P0, :] = jnp.zeros((P0, C2), F32)
    pad_scr[P0 + T:P + T, :] = jnp.zeros((P - P0, C2), F32)

    def copy_in(i, carry):
        r0 = pl.multiple_of(i * L, L)
        pad_scr[pl.ds(P0 + r0, L), :] = qk_ref[pl.ds(r0, L), :]
        return carry

    lax.fori_loop(0, N, copy_in, 0)

    lane_c = lax.broadcasted_iota(jnp.int32, (1, C2), 1)
    qscale = jnp.where(lane_c < HK, scale, 1.0).astype(F32)
    sub = lax.broadcasted_iota(jnp.int32, (L, 1), 0)
    img_rows = (0,) if rows_img == 1 else (-1, 0, 1)

    def conv_tile(i, carry):
        r0 = pl.multiple_of(i * L, L)
        col = lax.rem(r0, grid_w) + sub
        ok_left = col >= 1
        ok_right = col <= grid_w - 2
        sums = [None, None, None]
        for di in img_rows:
            blk = pad_scr[pl.ds(P0 + r0 + di * grid_w - SUBLANES, L + 2 * SUBLANES), :]
            for k in range(3):
                term = blk * cw_ref[di + 1, k:k + 1, :]
                sums[k] = term if sums[k] is None else sums[k] + term
        S = SUBLANES
        acc = (sums[1][S:S + L, :] + jnp.where(ok_left, sums[0][S - 1:S - 1 + L, :], 0.0)
               + jnp.where(ok_right, sums[2][S + 1:S + 1 + L, :], 0.0))
        qk_scr[pl.ds(r0, L), :] = _silu(acc) * qscale
        return carry

    lax.fori_loop(0, N, conv_tile, 0)

    gl = lane - GATE_LANE0
    is_f = ((gl >= H_B) & (gl < 2 * H_B)) | ((gl >= 3 * H_B) & (gl < 4 * H_B))

    def gate_tile(i, carry):
        rows = pl.ds(pl.multiple_of(i * L, L), L)
        x = sm_ref[rows, :] + bm_ref[...]
        y_scr[rows, :] = jnp.where(is_f, _log_sigmoid(x), x)
        return carry

    lax.fori_loop(0, N, gate_tile, 0)


    def state_group(ns, dirs=(0, 1)):
        units = [(d, n if d == 0 else N - 1 - n) for n in ns for d in dirs]
        rows = [_chunk_rows(n) for _, n in units]
        kt_all = [[qk_scr[r, HK + p * LANES:HK + (p + 1) * LANES].T for p in range(n_pairs)] for r in rows]
        yield
        xs = [y_scr[r, :] for r in rows]
        fsum = [_tri_sum(tri[d], x) for (d, _), x in zip(units, xs)]
        yield
        wk_all, f_end, c_end = [], [], []
        for (d, n), r, x, fs in zip(units, rows, xs, fsum):
            y = jnp.where(is_f, fs, x)
            li0 = GATE_LANE0 + 2 * H_B * d
            blk = jnp.concatenate([y, y], axis=0).T[li0:li0 + 2 * H_B, :]
            frow = pltpu.roll(blk, H_B, axis=0)
            grow = blk - frow
            g_scr[d, n] = grow
            f_scr[d, n] = frow
            e_col = L - 1 if d == 0 else 0
            f_end.append(frow[0:H_B, e_col:e_col + 1])
            ce8 = jnp.max(grow, axis=1, keepdims=True)
            c_end.append(ce8[0:H_B, :])
            wk_all.append(jnp.exp(grow[:, 0:L] - ce8))
        yield
        kv_all, ksum_all = [], []
        for r, wk8, kt_u in zip(rows, wk_all, kt_all):
            kv_u, ks_u = [], []
            wk8b = wk8.astype(BF16)
            for p in range(n_pairs):
                kpb = qk_scr[r, HK + p * LANES:HK + (p + 1) * LANES].astype(BF16)
                ks8 = _dot(wk8b, kpb)
                for j in range(2):
                    h = 2 * p + j
                    kwt = (kt_u[p][j * DK:(j + 1) * DK, :] * wk8[h:h + 1, :]).astype(BF16)
                    kv_u.append(_dot(kwt, v_ref[r, h * DV:(h + 1) * DV].astype(BF16)))
                    ks_u.append(ks8[h:h + 1, :])
            kv_all.append(kv_u)
            ksum_all.append(ks_u)
        yield
        m_run = {d: m_scr[H_B * d:H_B * (d + 1), 0:1] for d in dirs}
        a_all, b_all = [], []
        for (d, n), fe, ce in zip(units, f_end, c_end):
            mall_scr[d, n, 0:H_B, 0:1] = m_run[d]
            mx = jnp.maximum(m_run[d], ce)
            a_all.append(jnp.exp(m_run[d] - mx))
            b_all.append(jnp.exp(ce - mx))
            m_run[d] = fe + mx
        for d in dirs:
            m_scr[H_B * d:H_B * (d + 1), 0:1] = m_run[d]
        yield
        c_run = {d: [[c_scr[d, p, j * DK:(j + 1) * DK, :] for j in range(2)] for p in range(n_pairs)] for d in dirs}
        n_run = {d: [n_scr[2 * d + p:2 * d + p + 1, :] for p in range(n_pairs)] for d in dirs}
        for (d, n), a4, b4, kv_u, ks_u in zip(units, a_all, b_all, kv_all, ksum_all):
            for p in range(n_pairs):
                nall_scr[d, n, p:p + 1, :] = n_run[d][p]
                a_s = [a4[2 * p + j:2 * p + j + 1, :] for j in range(2)]
                b_s = [b4[2 * p + j:2 * p + j + 1, :] for j in range(2)]
                for j in range(2):
                    cj = c_run[d][p][j]
                    call_scr[d, n, p, j * DK:(j + 1) * DK, :] = cj.astype(BF16)
                    c_run[d][p][j] = a_s[j] * cj + b_s[j] * kv_u[2 * p + j]
                n_run[d][p] = (jnp.where(head_mask[0], a_s[0], a_s[1]) * n_run[d][p]
                               + jnp.where(head_mask[0], b_s[0] * ks_u[2 * p], b_s[1] * ks_u[2 * p + 1]))
        for d in dirs:
            for p in range(n_pairs):
                n_scr[2 * d + p:2 * d + p + 1, :] = n_run[d][p]
                for j in range(2):
                    c_scr[d, p, j * DK:(j + 1) * DK, :] = c_run[d][p][j]

    eye = lower & upper
    ones8 = jnp.ones((SUBLANES, L), BF16)
    sub8 = lax.broadcasted_iota(jnp.int32, (SUBLANES, LANES), 0)
    sub_h = lax.broadcasted_iota(jnp.int32, (H_B, L), 0)
    n_rows = [((sub8 == 2 * p) & head_mask[0]) | ((sub8 == 2 * p + 1) & head_mask[1]) for p in range(n_pairs)]

    def head_rows(vals):
        out = vals[0][0:H_B, :]
        for h in range(1, H_B):
            out = jnp.where(sub_h == h, vals[h][0:H_B, :], out)
        return out

    def out_group(ns):
        chunks = [(d, n) for n in ns for d in range(2)]
        pairs = [(d, n, p) for d, n in chunks for p in range(n_pairs)]
        units = [(d, n, p, j) for d, n, p in pairs for j in range(2)]
        cms = [lane_cummax(g_scr[d, n], d)[0:H_B, 0:L] for d, n in chunks]
        qk2s, qc2s, qn2s = [], [], []
        for d, n, p in pairs:
            r = _chunk_rows(n)
            qp = qk_scr[r, p * LANES:(p + 1) * LANES]
            q2 = jnp.concatenate([jnp.where(head_mask[j], qp, 0.0) for j in range(2)], axis=0).astype(BF16)
            qk2s.append(_dot_nt(q2, qk_scr[r, HK + p * LANES:HK + (p + 1) * LANES].astype(BF16)))
            qc2s.append(_dot(q2, call_scr[d, n, p]))
            nsel = jnp.where(n_rows[p], nall_scr[d, n, p:p + 1, :], 0.0).astype(BF16)
            qn2s.append(_dot_nt(nsel, qp.astype(BF16)))
        yield
        s_all = []
        for ui, (d, n, p, j) in enumerate(units):
            grow = g_scr[d, n, 2 * p + j:2 * p + j + 1, 0:L]
            e = jnp.where(tmask[d], grow, -jnp.inf)
            cmax = jnp.max(e, axis=-1, keepdims=True)
            s_all.append((qk2s[ui // 2][j * L:(j + 1) * L, :] * jnp.exp(e - cmax)).astype(BF16))
        yield
        nums =[_dot(s, v_ref[_chunk_rows(n), (2 * p + j) * DV:(2 * p + j + 1) * DV].astype(BF16))
                for (d, n, p, j), s in zip(units, s_all)]
        dens = [_dot_nt(ones8, s) for s in s_all]
        yield
        scales = []
        for ci, (d, n) in enumerate(chunks):
            den_loc = head_rows(dens[ci * H_B:(ci + 1) * H_B])
            qn = qn2s[ci * n_pairs][0:H_B, :]
            for p in range(1, n_pairs):
                qn = qn + qn2s[ci * n_pairs + p][0:H_B, :]
            cm = cms[ci]
            m_prev = mall_scr[d, n, 0:H_B, 0:1]
            delta = cm - m_prev
            t = jnp.exp(-jnp.abs(delta))
            w_loc = jnp.where(delta <= 0.0, t, 1.0)
            w_inter = jnp.where(delta <= 0.0, 1.0, t)
            mt = f_scr[d, n, 0:H_B, 0:L] + jnp.maximum(m_prev, cm)
            den = w_loc * den_loc + w_inter * qn
            rinv = 1.0 / jnp.maximum(jnp.abs(den), jnp.exp(-mt))
            scales.append((w_loc * rinv, w_inter * rinv))
        yield
        hs = []
        for ui, (d, n, p, j) in enumerate(units):
            h = 2 * p + j
            sc_loc, sc_inter = scales[ui // H_B]
            d_loc = jnp.where(eye, sc_loc[h:h + 1, :], 0.0).astype(BF16)
            d_inter = jnp.where(eye, sc_inter[h:h + 1, :], 0.0).astype(BF16)
            hs.append(_dot(d_loc, nums[ui].astype(BF16))
                      + _dot(d_inter, qc2s[ui // 2][j * L:(j + 1) * L, :].astype(BF16)))
        yield
        for ni, n in enumerate(ns):
            r = _chunk_rows(n)
            for h in range(H_B):
                vs = slice(h * DV, (h + 1) * DV)
                o = hs[(2 * ni) * H_B + h] + hs[(2 * ni + 1) * H_B + h]
                out_ref[r, vs] = (_rms(o, gw_ref[:, vs]) * _sigmoid(og_ref[r, vs])).astype(out_ref.dtype)

    def finish():
        if write_state:
            for d in range(2):
                for p in range(n_pairs):
                    cnew_ref[d, p] = c_scr[d, p]
                    for j in range(2):
                        nnew_ref[d, 2 * p + j:2 * p + j + 1, :] = n_scr[2 * d + p:2 * d + p + 1, j * DK:(j + 1) * DK]
                mnew_ref[d:d + 1, :] = to_row(m_scr[H_B * d:H_B * (d + 1), 0:1])

    return state_group, out_group, finish


def _mlstm_scratch(T, C2, grid_w):
    n_pairs = C2 // 2 // LANES
    n_chunks = T // CHUNK
    pad_rows = 2 * (grid_w + SUBLANES) if T // grid_w > 1 else 2 * SUBLANES
    return [
        pltpu.VMEM((T + pad_rows, C2), F32),
        pltpu.VMEM((T, C2), F32),
        pltpu.VMEM((T, SMALL_W), F32),
        pltpu.VMEM((2, n_pairs, LANES, LANES), F32),
        pltpu.VMEM((SUBLANES, LANES), F32),
        pltpu.VMEM((SUBLANES, LANES), F32),
        pltpu.VMEM((2, n_chunks, n_pairs, LANES, LANES), BF16),
        pltpu.VMEM((2, n_chunks, SUBLANES, LANES), F32),
        pltpu.VMEM((2, n_chunks, SUBLANES, LANES), F32),
        pltpu.VMEM((2, n_chunks, SUBLANES, LANES), F32),
        pltpu.VMEM((2, n_chunks, SUBLANES, LANES), F32),
    ]


N_GLA_SCRATCH = 5
N_MLSTM_SCRATCH = 11


def _scan_kernel(*refs, cols, layer, has_state, write_state, n_cast, ride_ada, grid_w, unroll):
    refs = list(refs)
    z_ref = refs.pop(0)
    s0_ref = c0_ref = n0_ref = m0_ref = None
    if has_state:
        s0_ref, c0_ref, n0_ref, m0_ref = refs[:4]
        del refs[:4]
    wa_ref, bal_ref, gwa_ref, cw_ref, bmg_ref, gwb_ref = refs[:6]
    del refs[:6]
    cast_in = refs[:n_cast]
    del refs[:n_cast]
    if ride_ada:
        ada_in = refs[:4]
        del refs[:4]
    outa_ref, outb_ref = refs[:2]
    del refs[:2]
    snew_ref = cnew_ref = nnew_ref = mnew_ref = None
    if write_state:
        snew_ref, cnew_ref, nnew_ref, mnew_ref = refs[:4]
        del refs[:4]
    cast_out = refs[:n_cast]
    del refs[:n_cast]
    if ride_ada:
        ada_out = refs.pop(0)
    wal_scr, bm_scr = refs[:2]
    del refs[:2]
    gla_scr = refs[:N_GLA_SCRATCH]
    mlstm_scr = refs[N_GLA_SCRATCH:]

    for src, dst in zip(cast_in, cast_out):
        dst[...] = src[...].astype(BF16)
    if ride_ada:
        _ada_tile(*ada_in, ada_out)

    R, HK = wa_ref.shape[1], wa_ref.shape[2]
    wal_scr[...] = jnp.zeros(wal_scr.shape, BF16)
    for d in range(2):
        wal_scr[d * R:(d + 1) * R, d * HK:(d + 1) * HK] = wa_ref[d].astype(BF16)
    lane = lax.broadcasted_iota(jnp.int32, (1, LANES), 1)
    bm = jnp.zeros((1, LANES), F32)
    for g in range(bmg_ref.shape[1]):
        for h in range(H_B):
            bm = jnp.where(lane == GATE_LANE0 + H_B * g + h, bmg_ref[layer, g, h], bm)
    bm_scr[0:1, :] = bm

    def view(name):
        c0, w = cols[name]
        return z_ref.at[:, pl.ds(c0, w)]

    sm_ref = view("small")
    n_chunks = z_ref.shape[0] // CHUNK
    gla = _gla_body(view("qa"), view("ka"), view("va"), view("ga"), sm_ref, s0_ref, wal_scr, bal_ref, gwa_ref,
                    outa_ref, snew_ref, *gla_scr)
    mlstm = _mlstm_body(view("qkb"), view("vb"), view("ob"), sm_ref, c0_ref, n0_ref, m0_ref, cw_ref,
                        bm_scr.at[0:1, :], gwb_ref, outb_ref, cnew_ref, nnew_ref, mnew_ref, *mlstm_scr,
                        grid_w=grid_w)
    gla_state, gla_out, gla_finish, decay_span = gla
    mlstm_state, mlstm_out, mlstm_finish = mlstm

    def passes(gla_out_fn):
        _chunk_loop(n_chunks, unroll, lambda ns: [fn([n], (d,)) for n in ns for d in range(2)
                                                  for fn in (mlstm_state, gla_state)])
        _chunk_loop(n_chunks, unroll, lambda ns: [fn([n]) for n in ns for fn in (mlstm_out, gla_out_fn)])

    wide_decay = decay_span > GLA_FACTORED_DECAY_MAX

    @pl.when(jnp.logical_not(wide_decay))
    def _():
        passes(gla_out)

    @pl.when(wide_decay)
    def _():
        passes(functools.partial(gla_out, exact_decay=True))

    gla_finish()
    mlstm_finish()


def _scan_call(z2d, row0, B, T, states, lw, layer, *, grid_w, write_state, casts=(), ada=None):
    n_z = z2d.shape[1]
    assert row0 % T == 0 and z2d.shape[0] % T == 0
    z3 = z2d.reshape(z2d.shape[0] // T, T, n_z)
    blk0 = row0 // T
    HK = lw["w_alpha2"].shape[-1]
    DA = lw["gnorm_a_w"].shape[0]
    C2 = lw["conv_w"].shape[-1]
    DB = lw["gnorm_b_w"].shape[0]
    DK_A, DK_B = HK // H_A, C2 // 2 // H_B
    pa, pb = HK // LANES, C2 // 2 // LANES
    n_chunks = T // CHUNK
    has_state = states is not None
    widths = (("qa", HK), ("ka", HK), ("va", DA), ("ga", DA), ("qkb", C2), ("vb", DB), ("ob", DB),
              ("small", SMALL_W))
    cols, c0 = {}, 0
    for name, w in widths:
        cols[name] = (c0, w)
        c0 += w
    assert c0 == n_z
    cast_in_specs, cast_out_specs, cast_out_shape, cast_args = _cast_specs(casts, B)
    kern = functools.partial(_scan_kernel, cols=cols, layer=layer, has_state=has_state, write_state=write_state,
                             n_cast=len(casts), ride_ada=ada is not None, grid_w=grid_w,
                             unroll=min(n_chunks, SCAN_UNROLL))

    def per_batch(shape):
        nd = len(shape)
        return pl.BlockSpec((None,) + tuple(shape), lambda b: (b,) + (0,) * nd)

    def per_batch_layer(shape):
        nd = len(shape)
        return pl.BlockSpec((None, None) + tuple(shape), lambda b: (b, layer) + (0,) * nd)

    def of_layer(a):
        return pl.BlockSpec((None,) + a.shape[1:], lambda b: (layer,) + (0,) * (a.ndim - 1))

    def whole(a):
        return pl.BlockSpec(a.shape, lambda b: (0,) * a.ndim)

    state_shapes = ((2, pa, LANES, LANES), (2, pb, LANES, LANES), (2, H_B, DK_B), (2, H_B))
    in_specs = [pl.BlockSpec((None, T, n_z), lambda b: (b + blk0, 0, 0))]
    args = [z3]
    if has_state:
        s_gla, s_c, s_n, s_m = states
        depth = s_gla.shape[1]
        args += [s_gla.reshape((B, depth) + state_shapes[0]), s_c.reshape((B, depth) + state_shapes[1]), s_n, s_m]
        in_specs += [per_batch_layer(s) for s in state_shapes]
    args += [lw["w_alpha2"], lw["b_alpha"], lw["gnorm_a_w"].reshape(1, DA), lw["conv_w"], lw["b_mgate"],
             lw["gnorm_b_w"].reshape(1, DB)]
    in_specs += [of_layer(lw["w_alpha2"]), of_layer(lw["b_alpha"]), pl.BlockSpec((1, DA), lambda b: (0, 0)),
                 whole(lw["conv_w"]), pl.BlockSpec(memory_space=pltpu.SMEM), pl.BlockSpec((1, DB), lambda b: (0, 0))]
    args += cast_args
    in_specs += cast_in_specs
    out_specs = [per_batch((T, DA)), per_batch((T, DB))]
    out_shape = [jax.ShapeDtypeStruct((B, T, DA), BF16), jax.ShapeDtypeStruct((B, T, DB), BF16)]
    if write_state:
        out_specs += [per_batch(s) for s in state_shapes]
        out_shape += [jax.ShapeDtypeStruct((B,) + s, F32) for s in state_shapes]
    out_specs += cast_out_specs
    out_shape += cast_out_shape
    if ada is not None:
        cc, c, w_ada, b_ada, col0 = ada
        n_rest = w_ada.shape[1] - col0
        wcol = n_rest // B
        assert n_rest % B == 0 and wcol % LANES == 0 and col0 % wcol == 0
        args += [cc, c, w_ada, b_ada]
        in_specs += [whole(cc), whole(c),
                     pl.BlockSpec((w_ada.shape[0], wcol), lambda b: (0, col0 // wcol + b)),
                     pl.BlockSpec((1, wcol), lambda b: (0, col0 // wcol + b))]
        out_specs.append(pl.BlockSpec((COND_ROWS, wcol), lambda b: (0, b)))
        out_shape.append(jax.ShapeDtypeStruct((COND_ROWS, n_rest), F32))
    scratch = ([pltpu.VMEM((SMALL_W, 2 * HK), BF16), pltpu.VMEM((SUBLANES, LANES), F32)]
               + _gla_scratch(T, HK) + _mlstm_scratch(T, C2, grid_w))
    assert len(scratch) == 2 + N_GLA_SCRATCH + N_MLSTM_SCRATCH
    return pl.pallas_call(
        kern,
        grid=(B,),
        in_specs=in_specs,
        out_specs=out_specs,
        out_shape=out_shape,
        scratch_shapes=scratch,
        compiler_params=pltpu.CompilerParams(dimension_semantics=("arbitrary",),
                                             vmem_limit_bytes=VMEM_LIMIT),
        name="mixer_scans",
    )(*args)


def _outff_kernel(xc_ref, xl_ref, ac_ref, al_ref, bc_ref, bl_ref, mod_ref, n2_ref, fn_ref, wo_ref, w1_ref, w2_ref,
                  yc_ref, yl_ref, *, n_ctx, tiles_per_req, ff_chunk, final_norm):
    D = xc_ref.shape[1]
    DA = ac_ref.shape[1]
    is_ctx, row = _tile_group(n_ctx, tiles_per_req)

    def mod(k):
        return mod_ref[pl.ds(row, 1), (k - MOD_SPLIT) * D:(k - MOD_SPLIT + 1) * D]

    def tile(x_ref, a_ref, b_ref, y_ref):
        y = _dot(a_ref[...], wo_ref[0:DA, :]) + _dot(b_ref[...], wo_ref[DA:, :])
        x1 = x_ref[...] + mod(2) * y
        h2 = (_rms(x1, n2_ref[...]) * (1.0 + mod(4)) + mod(3)).astype(BF16)
        acc = jnp.zeros(x1.shape, F32)
        for c0 in range(0, w1_ref.shape[1], ff_chunk):
            u = jnp.maximum(_dot(h2, w1_ref[:, c0:c0 + ff_chunk]), 0.0)
            acc = acc + _dot((u * u).astype(BF16), w2_ref[c0:c0 + ff_chunk, :])
        x2 = x1 + mod(5) * acc
        y_ref[...] = _rms(x2, fn_ref[...]) if final_norm else x2

    @pl.when(is_ctx)
    def _():
        tile(xc_ref, ac_ref, bc_ref, yc_ref)

    @pl.when(jnp.logical_not(is_ctx))
    def _():
        tile(xl_ref, al_ref, bl_ref, yl_ref)


def _outff_call(xc2d, xl2d, ac, al, bc, bl, mod, norm2_w, final_w, wo, w1, w2, *, tm, tiles_per_req, final_norm):
    (Mc, D), Ml = xc2d.shape, xl2d.shape[0]
    n_ctx = Mc // tm
    DA = ac.shape[1]
    DFF = w1.shape[1]
    kern = functools.partial(_outff_kernel, n_ctx=n_ctx, tiles_per_req=tiles_per_req, ff_chunk=FF_CHUNK,
                             final_norm=final_norm)
    once = pl.Buffered(1)
    ctx, lat = _ctx_tile(n_ctx), _lat_tile(n_ctx)
    return pl.pallas_call(
        kern,
        grid=((Mc + Ml) // tm,),
        in_specs=[
            pl.BlockSpec((tm, D), ctx), pl.BlockSpec((tm, D), lat),
            pl.BlockSpec((tm, DA), ctx), pl.BlockSpec((tm, DA), lat),
            pl.BlockSpec((tm, D - DA), ctx), pl.BlockSpec((tm, D - DA), lat),
            pl.BlockSpec(mod.shape, lambda i: (0, 0)),
            pl.BlockSpec((1, D), lambda i: (0, 0)),
            pl.BlockSpec((1, D), lambda i: (0, 0)),
            pl.BlockSpec((D, D), lambda i: (0, 0), pipeline_mode=once),
            pl.BlockSpec((D, DFF), lambda i: (0, 0), pipeline_mode=once),
            pl.BlockSpec((DFF, D), lambda i: (0, 0), pipeline_mode=once),
        ],
        out_specs=[pl.BlockSpec((tm, D), ctx), pl.BlockSpec((tm, D), lat)],
        out_shape=[jax.ShapeDtypeStruct((Mc, D), F32), jax.ShapeDtypeStruct((Ml, D), F32)],
        compiler_params=pltpu.CompilerParams(dimension_semantics=("arbitrary",),
                                             vmem_limit_bytes=VMEM_LIMIT),
        name="outproj_mlp",
    )(xc2d, xl2d, ac, al, bc, bl, mod, norm2_w.reshape(1, D), final_w.reshape(1, D), wo, w1, w2)


def _layer(xc, xl, cond, ada_w, cached, lw, layer, ffw, final_w, final_norm):
    (Bc, Tc, D), (Bl, Tl, _) = xc.shape, xl.shape
    tm = TOKEN_TILE
    assert (Bc * Tc) % tm == 0 and Tl % tm == 0 and (Bc * Tc) % Tl == 0
    xc2d, xl2d = xc.reshape(Bc * Tc, D), xl.reshape(Bl * Tl, D)
    z = _inproj_call(xc2d, xl2d, *cond, *ada_w, lw["norm1_w"], lw["w_in_t"], tm=tm, tiles_per_req=Tl // tm,
                     big_rows=lw["big_rows"], small_rows=lw["small_rows"])
    res_c = _scan_call(z, 0, Bc, Tc, None, lw, layer, grid_w=Tc, write_state=True,
                       casts=((ffw[0], 0), (ffw[1], 0), (ffw[2], 0)), ada=(*cond, *ada_w, MOD_SPLIT * D))
    res_l = _scan_call(z, Bc * Tc, Bl, Tl, cached, lw, layer, grid_w=GRID_W, write_state=False)
    wo_b, w1_b, w2_b, mod_out = res_c[-4:]
    yc, yl = _outff_call(xc2d, xl2d, res_c[0].reshape(Bc * Tc, -1), res_l[0].reshape(Bl * Tl, -1),
                         res_c[1].reshape(Bc * Tc, -1), res_l[1].reshape(Bl * Tl, -1), mod_out, lw["norm2_w"],
                         final_w, wo_b, w1_b, w2_b, tm=tm, tiles_per_req=Tl // tm, final_norm=final_norm)
    return yc.reshape(Bc, Tc, D), yl.reshape(Bl, Tl, D), tuple(res_c[2:6])


def _layer_weights(l, norm1_w, norm2_w, w_in, w_alpha2, b_alpha, b_mgate, conv_w, gnorm_a_w, gnorm_b_w):
    hk_a = w_alpha2.shape[-1]
    d_a = gnorm_a_w.shape[-1]
    d_b = gnorm_b_w.shape[-1]
    hk_b = conv_w.shape[-1] // 2
    sizes = (hk_a, hk_a, d_a, d_a, 2 * R_ALPHA, hk_b, hk_b, d_b, d_b, 4 * H_B)
    assert w_alpha2.shape[2] == R_ALPHA and b_mgate.shape[1] * b_mgate.shape[2] == 4 * H_B
    offs = [0]
    for s in sizes:
        offs.append(offs[-1] + s)
    big_rows = ((offs[0], offs[4] - offs[0]), (offs[5], offs[9] - offs[5]))
    small_rows = ((offs[4], offs[5] - offs[4]), (offs[9], offs[10] - offs[9]))
    assert all(n % LANES == 0 and r % BF16_ROWS == 0 for r, n in big_rows)
    return dict(
        norm1_w=norm1_w[l], norm2_w=norm2_w[l], w_in_t=jnp.swapaxes(w_in[l], 0, 1),
        big_rows=big_rows, small_rows=small_rows,
        w_alpha2=w_alpha2, b_alpha=b_alpha, b_mgate=b_mgate, conv_w=conv_w[l],
        gnorm_a_w=gnorm_a_w[l], gnorm_b_w=gnorm_b_w[l],
    )


def kernel(x_prompt, x_sample, c, state_gla, state_mlstm_C, state_mlstm_n, state_mlstm_m, c_ctx, w_ada, b_ada, norm1_w, norm2_w, w_in, w_alpha2, b_alpha, b_mgate, conv_w, gnorm_a_w, gnorm_b_w, w_out, w_ff1, w_ff2, final_norm_w):
    depth = w_in.shape[0]
    D = x_prompt.shape[-1]
    Bp, Tp, _ = x_prompt.shape
    Bs = x_sample.shape[0]
    assert 1 + Bs <= COND_ROWS
    cond = (c_ctx.reshape(1, D), c)
    cached = (state_gla, state_mlstm_C, state_mlstm_n, state_mlstm_m)
    xp, xs = x_prompt, x_sample
    s_gla, s_c, s_n, s_m = [], [], [], []
    for l in range(depth):
        lw = _layer_weights(l, norm1_w, norm2_w, w_in, w_alpha2, b_alpha, b_mgate, conv_w,
                            gnorm_a_w, gnorm_b_w)
        xp, xs, ctx = _layer(xp, xs, cond, (w_ada[l], b_ada[l].reshape(1, -1)), cached, lw, l,
                             (w_out[l], w_ff1[l], w_ff2[l]), final_norm_w, l == depth - 1)
        s_gla.append(ctx[0].reshape(Bp, 2, H_A, -1, ctx[0].shape[-1]))
        s_c.append(ctx[1].reshape(Bp, 2, H_B, -1, ctx[1].shape[-1]))
        s_n.append(ctx[2])
        s_m.append(ctx[3])
    dt = x_prompt.dtype
    return (xp, xs, jnp.stack(s_gla, axis=1).astype(dt), jnp.stack(s_c, axis=1).astype(dt),
            jnp.stack(s_n, axis=1).astype(dt), jnp.stack(s_m, axis=1).astype(dt))
```

```python
import functools
import math

import jax
import jax.numpy as jnp
from jax import lax
from jax.experimental import pallas as pl
from jax.experimental.pallas import tpu as pltpu

F32 = jnp.float32
BF16 = jnp.bfloat16

GRID_W = 64
H_A = 4
H_B = 4
R_ALPHA = 16
TAU_GLA = 16.0
CHUNK = 64
EPS = 1e-6
LANES = 128
SUBLANES = 8
BF16_ROWS = 16
COND_ROWS = SUBLANES
SMALL_W = LANES
GATE_LANE0 = 2 * R_ALPHA
VMEM_LIMIT = 56 * 1024 * 1024
SCAN_UNROLL = 4
PIPELINE_STARTS = 8
TOKEN_TILE = 512
FF_CHUNK = 512
GLA_FACTORED_DECAY_MAX = 60.0
MOD_SPLIT = 2


def _sigmoid(x):
    return 1.0 / (1.0 + jnp.exp(-x))


def _silu(x):
    return x * _sigmoid(x)


def _log_sigmoid(x):
    return jnp.minimum(x, 0.0) - jnp.log(1.0 + jnp.exp(-jnp.abs(x)))


def _dot(a, b):
    return jnp.dot(a, b, preferred_element_type=F32)


def _dot_nt(a, b):
    return lax.dot_general(a, b, (((1,), (1,)), ((), ())), preferred_element_type=F32)


def _rms(x, w):
    return x * lax.rsqrt(jnp.mean(x * x, axis=-1, keepdims=True) + EPS) * w


def _tri_sum(tri, x, terms=3):
    acc, rest = None, x
    for t in range(terms):
        part = rest.astype(BF16)
        prod = _dot(tri, part)
        acc = prod if acc is None else acc + prod
        if t + 1 < terms:
            rest = rest - part.astype(F32)
    return acc


def _chunk_masks(L):
    row = lax.broadcasted_iota(jnp.int32, (L, L), 0)
    col = lax.broadcasted_iota(jnp.int32, (L, L), 1)
    lower = row >= col
    upper = row <= col
    return lower, upper


def _ada_tile(cc_ref, c_ref, w_ref, b_ref, o_ref):
    D = cc_ref.shape[1]
    sub = lax.broadcasted_iota(jnp.int32, (COND_ROWS, D), 0)
    cond = jnp.where(sub == 0, cc_ref[...], 0.0)
    for r in range(c_ref.shape[0]):
        cond = jnp.where(sub == 1 + r, c_ref[r:r + 1, :], cond)
    o_ref[...] = _dot(_silu(cond).astype(BF16), w_ref[...].astype(BF16)) + b_ref[...]


def _tile_group(n_ctx, tiles_per_req):
    i = pl.program_id(0)
    is_ctx = i < n_ctx
    row = jnp.where(is_ctx, 0, 1 + jnp.maximum(i - n_ctx, 0) // tiles_per_req)
    return is_ctx, row


def _ctx_tile(n_ctx):
    return lambda i: (jnp.minimum(i, n_ctx - 1), 0)


def _lat_tile(n_ctx):
    return lambda i: (jnp.maximum(i - n_ctx, 0), 0)


def _inproj_kernel(xc_ref, xl_ref, cc_ref, c_ref, wa_ref, ba_ref, nw_ref, wt_ref, zf_ref, zh_ref, wb_scr, mod_ref,
                   *, n_ctx, tiles_per_req, f32_rows, small_rows, bf16_rows):
    D = xc_ref.shape[1]
    n_f32 = zf_ref.shape[1]

    @pl.when(pl.program_id(0) == 0)
    def _():
        _ada_tile(cc_ref, c_ref, wa_ref, ba_ref, mod_ref)

        def wide(rows, col):
            for r0, n in rows:
                for k in range(n // LANES):
                    blk = wt_ref[r0 + k * LANES:r0 + (k + 1) * LANES, :]
                    wb_scr[:, col:col + LANES] = blk.T.astype(BF16)
                    col += LANES
            return col

        col = wide(f32_rows, 0)
        parts = [wt_ref[r0:r0 + n, :] for r0, n in small_rows]
        n_small = sum(n for _, n in small_rows)
        parts.append(jnp.zeros((SMALL_W - n_small, D), F32))
        wb_scr[:, col:col + SMALL_W] = jnp.concatenate(parts, axis=0).T.astype(BF16)
        wide(bf16_rows, col + SMALL_W)

    is_ctx, row = _tile_group(n_ctx, tiles_per_req)

    def tile(x_ref):
        sh1 = mod_ref[pl.ds(row, 1), 0:D]
        sc1 = mod_ref[pl.ds(row, 1), D:2 * D]
        h = (_rms(x_ref[...], nw_ref[...]) * (1.0 + sc1) + sh1).astype(BF16)
        zf_ref[...] = _dot(h, wb_scr[:, 0:n_f32])
        zh_ref[...] = _dot(h, wb_scr[:, n_f32:]).astype(BF16)

    @pl.when(is_ctx)
    def _():
        tile(xc_ref)

    @pl.when(jnp.logical_not(is_ctx))
    def _():
        tile(xl_ref)


def _inproj_call(xc2d, xl2d, cc, c, w_ada, b_ada, norm_w, w_in_t, *, tm, tiles_per_req, f32_rows, small_rows,
                 bf16_rows):
    (Mc, D), Ml = xc2d.shape, xl2d.shape[0]
    n_ctx = Mc // tm
    n_f32 = sum(n for _, n in f32_rows) + SMALL_W
    n_bf16 = sum(n for _, n in bf16_rows)
    n_out = n_f32 + n_bf16
    n_mod = MOD_SPLIT * D
    kern = functools.partial(_inproj_kernel, n_ctx=n_ctx, tiles_per_req=tiles_per_req,
                             f32_rows=f32_rows, small_rows=small_rows, bf16_rows=bf16_rows)
    once = pl.Buffered(1)
    return pl.pallas_call(
        kern,
        grid=((Mc + Ml) // tm,),
        in_specs=[
            pl.BlockSpec((tm, D), _ctx_tile(n_ctx)),
            pl.BlockSpec((tm, D), _lat_tile(n_ctx)),
            pl.BlockSpec(cc.shape, lambda i: (0, 0)),
            pl.BlockSpec(c.shape, lambda i: (0, 0)),
            pl.BlockSpec((D, n_mod), lambda i: (0, 0), pipeline_mode=once),
            pl.BlockSpec((1, n_mod), lambda i: (0, 0)),
            pl.BlockSpec((1, D), lambda i: (0, 0)),
            pl.BlockSpec(w_in_t.shape, lambda i: (0, 0), pipeline_mode=once),
        ],
        out_specs=[pl.BlockSpec((tm, n_f32), lambda i: (i, 0)), pl.BlockSpec((tm, n_bf16), lambda i: (i, 0))],
        out_shape=[jax.ShapeDtypeStruct((Mc + Ml, n_f32), F32), jax.ShapeDtypeStruct((Mc + Ml, n_bf16), BF16)],
        scratch_shapes=[pltpu.VMEM((D, n_out), BF16), pltpu.VMEM((COND_ROWS, n_mod), F32)],
        compiler_params=pltpu.CompilerParams(dimension_semantics=("arbitrary",),
                                             vmem_limit_bytes=VMEM_LIMIT),
        name="norm_inproj",
    )(xc2d, xl2d, cc, c, w_ada, b_ada, norm_w.reshape(1, D), w_in_t)


def _chunk_loop(n_chunks, unroll, make_units):
    def step(ns):
        pending = list(make_units(ns))
        active = []
        while pending or active:
            for _ in range(min(PIPELINE_STARTS, len(pending))):
                active.append(pending.pop(0))
            alive = []
            for g in active:
                try:
                    next(g)
                    alive.append(g)
                except StopIteration:
                    pass
            active = alive

    if unroll >= n_chunks:
        step(list(range(n_chunks)))
        return

    def body(i, carry):
        step([i * unroll + u for u in range(unroll)])
        return carry

    lax.fori_loop(0, n_chunks // unroll, body, 0)


def _chunk_rows(n):
    if isinstance(n, int):
        return pl.ds(n * CHUNK, CHUNK)
    return pl.ds(pl.multiple_of(n * CHUNK, CHUNK), CHUNK)


def _cast_specs(casts, n_steps):
    in_specs, out_specs, out_shape, args = [], [], [], []
    for w, axis in casts:
        blk = list(w.shape)
        assert blk[axis] % n_steps == 0
        blk[axis] //= n_steps
        assert blk[0] % BF16_ROWS == 0 and blk[1] % LANES == 0
        idx = (lambda b: (b, 0)) if axis == 0 else (lambda b: (0, b))
        in_specs.append(pl.BlockSpec(tuple(blk), idx))
        out_specs.append(pl.BlockSpec(tuple(blk), idx))
        out_shape.append(jax.ShapeDtypeStruct(w.shape, BF16))
        args.append(w)
    return in_specs, out_specs, out_shape, args


def _gla_body(q_ref, k_ref, v_ref, g_ref, sm_ref, s0_ref, wal_ref, bal_ref, gw_ref, out_ref, snew_ref,
              st_scr, sall_scr, qh_scr, qs_scr, kh_scr):
    has_state = s0_ref is not None
    write_state = snew_ref is not None
    T = q_ref.shape[0]
    L = CHUNK
    N = T // L
    HK = q_ref.shape[1]
    DK = HK // H_A
    DV = v_ref.shape[1] // H_A
    scale = DK ** -0.5
    n_pairs = HK // LANES

    lower, upper = _chunk_masks(L)
    tri = (lower.astype(BF16), upper.astype(BF16))
    tmask = (lower, upper)
    lane = lax.broadcasted_iota(jnp.int32, (1, LANES), 1)
    head_mask = (lane < DK, lane >= DK)

    for d in range(2):
        for p in range(n_pairs):
            if has_state:
                st_scr[d, p] = s0_ref[d, p].T
            else:
                st_scr[d, p] = jnp.zeros((LANES, LANES), F32)

    def decay_pre(d, r):
        return _dot(sm_ref[r, :].astype(BF16), wal_ref[:, d * HK:(d + 1) * HK]) + bal_ref[d:d + 1, :]

    neg_pre = jnp.maximum(-(_dot(sm_ref[...].astype(BF16), wal_ref[...])
                            + jnp.concatenate([bal_ref[0:1, :], bal_ref[1:2, :]], axis=1)), 0.0)
    chunk_sums = jnp.sum(neg_pre.reshape(N, L, 2 * HK), axis=1)
    decay_span = (jnp.max(chunk_sums) + L * math.log(2.0)) * (1.0 / TAU_GLA)

    def state_group(ns, dirs=(0, 1)):
        units = [(d, n if d == 0 else N - 1 - n) for n in ns for d in dirs]
        rows = [_chunk_rows(n) for _, n in units]
        vt_all = [[jnp.concatenate([v_ref[r, (2 * p + j) * DV:(2 * p + j + 1) * DV] for j in range(2)],
                                   axis=0).astype(F32).T.astype(BF16) for p in range(n_pairs)] for r in rows]
        yield
        pre = [decay_pre(d, r) for (d, _), r in zip(units, rows)]
        yield
        g = [_log_sigmoid(x) * (1.0 / TAU_GLA) for x in pre]
        yield
        b = [_tri_sum(tri[d], gi, terms=2) for (d, _), gi in zip(units, g)]
        yield
        ks_all, dec_all = [], []
        for (d, _), r, bi in zip(units, rows, b):
            bend = bi[L - 1:L, :] if d == 0 else bi[0:1, :]
            q = q_ref[r, :] * scale
            ks = (k_ref[r, :] * jnp.exp(bend - bi)).astype(BF16)
            qs = q * jnp.exp(bi)
            qh_scr[d, r, :] = (qs * jnp.exp(-bend)).astype(BF16)
            qs_scr[d, r, :] = qs.astype(BF16)
            kh_scr[d, r, :] = ks
            ks_all.append(ks)
            dec_all.append(jnp.exp(bend))
        yield
        upd_all = []
        for vt_u, ks in zip(vt_all, ks_all):
            upd_u = []
            for p in range(n_pairs):
                kp = ks[:, p * LANES:(p + 1) * LANES]
                kk = jnp.concatenate([jnp.where(head_mask[j], kp, jnp.zeros_like(kp)) for j in range(2)], axis=0)
                upd_u.append(_dot(vt_u[p], kk))
            upd_all.append(upd_u)
        yield
        st = {d: [st_scr[d, p] for p in range(n_pairs)] for d in dirs}
        for (d, n), dec, upd in zip(units, dec_all, upd_all):
            for p in range(n_pairs):
                sall_scr[d, n, p] = st[d][p].astype(BF16)
                st[d][p] = st[d][p] * dec[:, p * LANES:(p + 1) * LANES] + upd[p]
        for d in dirs:
            for p in range(n_pairs):
                st_scr[d, p] = st[d][p]

    def stack_heads(x):
        return jnp.concatenate([jnp.where(head_mask[j], x, jnp.zeros_like(x)) for j in range(2)], axis=0)

    tok = lax.broadcasted_iota(jnp.int32, (L, 1), 0)
    row_t = lax.broadcasted_iota(jnp.int32, (2 * L, L), 0) & (L - 1)
    col_s = lax.broadcasted_iota(jnp.int32, (2 * L, L), 1)

    def exact_scores(d, r, p):
        ls = slice(p * LANES, (p + 1) * LANES)
        b = _tri_sum(tri[d], _log_sigmoid(decay_pre(d, r)[:, ls]) * (1.0 / TAU_GLA))
        q = q_ref[r, ls] * scale
        k = k_ref[r, ls]
        acc = jnp.where(row_t == col_s, _dot_nt(stack_heads(q).astype(BF16), k.astype(BF16)), 0.0)
        src = lax.broadcasted_iota(jnp.int32, (L, L), 1)
        h = L // 2
        while h >= 1:
            first = tok & ~(2 * h - 1)
            edge = first + (h - 1 if d == 0 else h)
            b_edge = _tri_sum((src == edge).astype(BF16), b)
            upper = (tok & (2 * h - 1)) >= h
            later, earlier = (upper, ~upper) if d == 0 else (~upper, upper)
            qt = jnp.where(later, q * jnp.exp(b - b_edge), 0.0)
            kt = jnp.where(earlier, k * jnp.exp(b_edge - b), 0.0)
            sc = _dot_nt(stack_heads(qt).astype(BF16), kt.astype(BF16))
            acc = acc + jnp.where((row_t & ~(2 * h - 1)) == (col_s & ~(2 * h - 1)), sc, 0.0)
            h //= 2
        return acc

    def out_group(ns, exact_decay=False):
        pairs = [(d, ni, p) for ni in range(len(ns)) for d in range(2) for p in range(n_pairs)]
        scores, inter = [], []
        for d, ni, p in pairs:
            r = _chunk_rows(ns[ni])
            ls = slice(p * LANES, (p + 1) * LANES)
            if exact_decay:
                scores.append(exact_scores(d, r, p))
            else:
                scores.append(_dot_nt(stack_heads(qh_scr[d, r, ls]), kh_scr[d, r, ls]))
            inter.append(_dot_nt(stack_heads(qs_scr[d, r, ls]), sall_scr[d, ns[ni], p]))
        yield
        probs = [[jnp.where(tmask[d], sc[j * L:(j + 1) * L, :], 0.0).astype(BF16) for j in range(2)]
                 for (d, _, _), sc in zip(pairs, scores)]
        yield
        outs = {}
        for (d, ni, p), pr, it in zip(pairs, probs, inter):
            r = _chunk_rows(ns[ni])
            for j in range(2):
                vs = slice((2 * p + j) * DV, (2 * p + j + 1) * DV)
                outs[(d, ni, 2 * p + j)] = _dot(pr[j], v_ref[r, vs]) + it[j * L:(j + 1) * L, :]
        yield
        for ni, n in enumerate(ns):
            r = _chunk_rows(n)
            for h in range(H_A):
                vs = slice(h * DV, (h + 1) * DV)
                o = outs[(0, ni, h)] + outs[(1, ni, h)]
                out_ref[r, vs] = (_rms(o, gw_ref[:, vs]) * _silu(g_ref[r, vs].astype(F32))).astype(out_ref.dtype)

    def finish():
        if write_state:
            for d in range(2):
                for p in range(n_pairs):
                    snew_ref[d, p] = st_scr[d, p].T

    return state_group, out_group, finish, decay_span


def _gla_scratch(T, HK):
    n_pairs = HK // LANES
    n_chunks = T // CHUNK
    return [
        pltpu.VMEM((2, n_pairs, LANES, LANES), F32),
        pltpu.VMEM((2, n_chunks, n_pairs, LANES, LANES), BF16),
        pltpu.VMEM((2, T, HK), BF16),
        pltpu.VMEM((2, T, HK), BF16),
        pltpu.VMEM((2, T, HK), BF16),
    ]


def _mlstm_body(qk_ref, v_ref, og_ref, sm_ref, c0_ref, n0_ref, m0_ref, cw_ref, bm_ref, gw_ref,
                out_ref, cnew_ref, nnew_ref, mnew_ref,
                pad_scr, qk_scr, y_scr, c_scr, n_scr, m_scr, call_scr, nall_scr, mall_scr, g_scr, f_scr,
                *, grid_w):
    has_state = c0_ref is not None
    write_state = cnew_ref is not None
    T = qk_ref.shape[0]
    L = CHUNK
    N = T // L
    C2 = qk_ref.shape[1]
    HK = C2 // 2
    DK = HK // H_B
    DV = v_ref.shape[1] // H_B
    scale = DK ** -0.5
    n_pairs = HK // LANES
    P = pad_scr.shape[0] - T
    P0 = P // 2
    rows_img = T // grid_w

    lower, upper = _chunk_masks(L)
    tri = (lower.astype(BF16), upper.astype(BF16))
    tmask = (lower, upper)
    lane = lax.broadcasted_iota(jnp.int32, (1, LANES), 1)
    head_mask = (lane < DK, lane >= DK)
    lane_in = lane & (L - 1)

    def lane_cummax(x, d):
        k = 1
        while k < L:
            if d == 0:
                x = jnp.maximum(x, jnp.where(lane_in >= k, pltpu.roll(x, k, axis=1), -jnp.inf))
            else:
                x = jnp.maximum(x, jnp.where(lane_in < L - k, pltpu.roll(x, LANES - k, axis=1), -jnp.inf))
            k *= 2
        return x

    for d in range(2):
        for p in range(n_pairs):
            if has_state:
                c_scr[d, p] = c0_ref[d, p]
                n_scr[2 * d + p:2 * d + p + 1, :] = jnp.concatenate(
                    [n0_ref[d, 2 * p + j:2 * p + j + 1, :] for j in range(2)], axis=1)
            else:
                c_scr[d, p] = jnp.zeros((LANES, LANES), F32)
                n_scr[2 * d + p:2 * d + p + 1, :] = jnp.zeros((1, LANES), F32)
    eye_h = (lax.broadcasted_iota(jnp.int32, (H_B, H_B), 0) == lax.broadcasted_iota(jnp.int32, (H_B, H_B), 1))

    def to_col(row):
        return jnp.sum(jnp.where(eye_h, row, 0.0), axis=1, keepdims=True)

    def to_row(col):
        return jnp.sum(jnp.where(eye_h, col, 0.0), axis=0, keepdims=True)

    for d in range(2):
        if has_state:
            m_scr[H_B * d:H_B * (d + 1), 0:1] = to_col(m0_ref[d:d + 1, :])
        else:
            m_scr[H_B * d:H_B * (d + 1), 0:1] = jnp.zeros((H_B, 1), F32)

    pad_scr[0:P0, :] = jnp.zeros((P0, C2), F32)
    pad_scr[P0 + T:P + T, :] = jnp.zeros((P - P0, C2), F32)

    def copy_in(i, carry):
        r0 = pl.multiple_of(i * L, L)
        pad_scr[pl.ds(P0 + r0, L), :] = qk_ref[pl.ds(r0, L), :]
        return carry

    lax.fori_loop(0, N, copy_in, 0)

    lane_c = lax.broadcasted_iota(jnp.int32, (1, C2), 1)
    qscale = jnp.where(lane_c < HK, scale, 1.0).astype(F32)
    sub = lax.broadcasted_iota(jnp.int32, (L, 1), 0)
    img_rows = (0,) if rows_img == 1 else (-1, 0, 1)

    def conv_tile(i, carry):
        r0 = pl.multiple_of(i * L, L)
        col = lax.rem(r0, grid_w) + sub
        ok_left = col >= 1
        ok_right = col <= grid_w - 2
        sums = [None, None, None]
        for di in img_rows:
            blk = pad_scr[pl.ds(P0 + r0 + di * grid_w - SUBLANES, L + 2 * SUBLANES), :]
            for k in range(3):
                term = blk * cw_ref[di + 1, k:k + 1, :]
                sums[k] = term if sums[k] is None else sums[k] + term
        S = SUBLANES
        acc = (sums[1][S:S + L, :] + jnp.where(ok_left, sums[0][S - 1:S - 1 + L, :], 0.0)
               + jnp.where(ok_right, sums[2][S + 1:S + 1 + L, :], 0.0))
        qk_scr[pl.ds(r0, L), :] = _silu(acc) * qscale
        return carry

    lax.fori_loop(0, N, conv_tile, 0)

    gl = lane - GATE_LANE0
    is_f = ((gl >= H_B) & (gl < 2 * H_B)) | ((gl >= 3 * H_B) & (gl < 4 * H_B))

    def gate_tile(i, carry):
        rows = pl.ds(pl.multiple_of(i * L, L), L)
        x = sm_ref[rows, :] + bm_ref[...]
        y_scr[rows, :] = jnp.where(is_f, _log_sigmoid(x), x)
        return carry

    lax.fori_loop(0, N, gate_tile, 0)


    def state_group(ns, dirs=(0, 1)):
        units = [(d, n if d == 0 else N - 1 - n) for n in ns for d in dirs]
        rows = [_chunk_rows(n) for _, n in units]
        kt_all = [[qk_scr[r, HK + p * LANES:HK + (p + 1) * LANES].T for p in range(n_pairs)] for r in rows]
        yield
        xs = [y_scr[r, :] for r in rows]
        fsum = [_tri_sum(tri[d], x) for (d, _), x in zip(units, xs)]
        yield
        wk_all, f_end, c_end = [], [], []
        for (d, n), r, x, fs in zip(units, rows, xs, fsum):
            y = jnp.where(is_f, fs, x)
            li0 = GATE_LANE0 + 2 * H_B * d
            blk = jnp.concatenate([y, y], axis=0).T[li0:li0 + 2 * H_B, :]
            frow = pltpu.roll(blk, H_B, axis=0)
            grow = blk - frow
            g_scr[d, n] = grow
            f_scr[d, n] = frow
            e_col = L - 1 if d == 0 else 0
            f_end.append(frow[0:H_B, e_col:e_col + 1])
            ce8 = jnp.max(grow, axis=1, keepdims=True)
            c_end.append(ce8[0:H_B, :])
            wk_all.append(jnp.exp(grow[:, 0:L] - ce8))
        yield
        kv_all, ksum_all = [], []
        for r, wk8, kt_u in zip(rows, wk_all, kt_all):
            kv_u, ks_u = [], []
            wk8b = wk8.astype(BF16)
            for p in range(n_pairs):
                kpb = qk_scr[r, HK + p * LANES:HK + (p + 1) * LANES].astype(BF16)
                ks8 = _dot(wk8b, kpb)
                for j in range(2):
                    h = 2 * p + j
                    kwt = (kt_u[p][j * DK:(j + 1) * DK, :] * wk8[h:h + 1, :]).astype(BF16)
                    kv_u.append(_dot(kwt, v_ref[r, h * DV:(h + 1) * DV]))
                    ks_u.append(ks8[h:h + 1, :])
            kv_all.append(kv_u)
            ksum_all.append(ks_u)
        yield
        m_run = {d: m_scr[H_B * d:H_B * (d + 1), 0:1] for d in dirs}
        a_all, b_all = [], []
        for (d, n), fe, ce in zip(units, f_end, c_end):
            mall_scr[d, n, 0:H_B, 0:1] = m_run[d]
            mx = jnp.maximum(m_run[d], ce)
            a_all.append(jnp.exp(m_run[d] - mx))
            b_all.append(jnp.exp(ce - mx))
            m_run[d] = fe + mx
        for d in dirs:
            m_scr[H_B * d:H_B * (d + 1), 0:1] = m_run[d]
        yield
        c_run = {d: [[c_scr[d, p, j * DK:(j + 1) * DK, :] for j in range(2)] for p in range(n_pairs)] for d in dirs}
        n_run = {d: [n_scr[2 * d + p:2 * d + p + 1, :] for p in range(n_pairs)] for d in dirs}
        for (d, n), a4, b4, kv_u, ks_u in zip(units, a_all, b_all, kv_all, ksum_all):
            for p in range(n_pairs):
                nall_scr[d, n, p:p + 1, :] = n_run[d][p]
                a_s = [a4[2 * p + j:2 * p + j + 1, :] for j in range(2)]
                b_s = [b4[2 * p + j:2 * p + j + 1, :] for j in range(2)]
                for j in range(2):
                    cj = c_run[d][p][j]
                    call_scr[d, n, p, j * DK:(j + 1) * DK, :] = cj.astype(BF16)
                    c_run[d][p][j] = a_s[j] * cj + b_s[j] * kv_u[2 * p + j]
                n_run[d][p] = (jnp.where(head_mask[0], a_s[0], a_s[1]) * n_run[d][p]
                               + jnp.where(head_mask[0], b_s[0] * ks_u[2 * p], b_s[1] * ks_u[2 * p + 1]))
        for d in dirs:
            for p in range(n_pairs):
                n_scr[2 * d + p:2 * d + p + 1, :] = n_run[d][p]
                for j in range(2):
                    c_scr[d, p, j * DK:(j + 1) * DK, :] = c_run[d][p][j]

    eye = lower & upper
    ones8 = jnp.ones((SUBLANES, L), BF16)
    sub8 = lax.broadcasted_iota(jnp.int32, (SUBLANES, LANES), 0)
    sub_h = lax.broadcasted_iota(jnp.int32, (H_B, L), 0)
    n_rows = [((sub8 == 2 * p) & head_mask[0]) | ((sub8 == 2 * p + 1) & head_mask[1]) for p in range(n_pairs)]

    def head_rows(vals):
        out = vals[0][0:H_B, :]
        for h in range(1, H_B):
            out = jnp.where(sub_h == h, vals[h][0:H_B, :], out)
        return out

    def out_group(ns):
        chunks = [(d, n) for n in ns for d in range(2)]
        pairs = [(d, n, p) for d, n in chunks for p in range(n_pairs)]
        units = [(d, n, p, j) for d, n, p in pairs for j in range(2)]
        cms = [lane_cummax(g_scr[d, n], d)[0:H_B, 0:L] for d, n in chunks]
        qk2s, qc2s, qn2s = [], [], []
        for d, n, p in pairs:
            r = _chunk_rows(n)
            qp = qk_scr[r, p * LANES:(p + 1) * LANES]
            q2 = jnp.concatenate([jnp.where(head_mask[j], qp, 0.0) for j in range(2)], axis=0).astype(BF16)
            qk2s.append(_dot_nt(q2, qk_scr[r, HK + p * LANES:HK + (p + 1) * LANES].astype(BF16)))
            qc2s.append(_dot(q2, call_scr[d, n, p]))
            nsel = jnp.where(n_rows[p], nall_scr[d, n, p:p + 1, :], 0.0).astype(BF16)
            qn2s.append(_dot_nt(nsel, qp.astype(BF16)))
        yield
        s_all = []
        for ui, (d, n, p, j) in enumerate(units):
            grow = g_scr[d, n, 2 * p + j:2 * p + j + 1, 0:L]
            e = jnp.where(tmask[d], grow, -jnp.inf)
            cmax = jnp.max(e, axis=-1, keepdims=True)
            s_all.append((qk2s[ui // 2][j * L:(j + 1) * L, :] * jnp.exp(e - cmax)).astype(BF16))
        yield
        nums = [_dot(s, v_ref[_chunk_rows(n), (2 * p + j) * DV:(2 * p + j + 1) * DV])
                for (d, n, p, j), s in zip(units, s_all)]
        dens = [_dot_nt(ones8, s) for s in s_all]
        yield
        scales = []
        for ci, (d, n) in enumerate(chunks):
            den_loc = head_rows(dens[ci * H_B:(ci + 1) * H_B])
            qn = qn2s[ci * n_pairs][0:H_B, :]
            for p in range(1, n_pairs):
                qn = qn + qn2s[ci * n_pairs + p][0:H_B, :]
            cm = cms[ci]
            m_prev = mall_scr[d, n, 0:H_B, 0:1]
            delta = cm - m_prev
            t = jnp.exp(-jnp.abs(delta))
            w_loc = jnp.where(delta <= 0.0, t, 1.0)
            w_inter = jnp.where(delta <= 0.0, 1.0, t)
            mt = f_scr[d, n, 0:H_B, 0:L] + jnp.maximum(m_prev, cm)
            den = w_loc * den_loc + w_inter * qn
            rinv = 1.0 / jnp.maximum(jnp.abs(den), jnp.exp(-mt))
            scales.append((w_loc * rinv, w_inter * rinv))
        yield
        hs = []
        for ui, (d, n, p, j) in enumerate(units):
            h = 2 * p + j
            sc_loc, sc_inter = scales[ui // H_B]
            d_loc = jnp.where(eye, sc_loc[h:h + 1, :], 0.0).astype(BF16)
            d_inter = jnp.where(eye, sc_inter[h:h + 1, :], 0.0).astype(BF16)
            hs.append(_dot(d_loc, nums[ui].astype(BF16))
                      + _dot(d_inter, qc2s[ui // 2][j * L:(j + 1) * L, :].astype(BF16)))
        yield
        for ni, n in enumerate(ns):
            r = _chunk_rows(n)
            for h in range(H_B):
                vs = slice(h * DV, (h + 1) * DV)
                o = hs[(2 * ni) * H_B + h] + hs[(2 * ni + 1) * H_B + h]
                out_ref[r, vs] = (_rms(o, gw_ref[:, vs]) * _sigmoid(og_ref[r, vs].astype(F32))).astype(out_ref.dtype)

    def finish():
        if write_state:
            for d in range(2):
                for p in range(n_pairs):
                    cnew_ref[d, p] = c_scr[d, p]
                    for j in range(2):
                        nnew_ref[d, 2 * p + j:2 * p + j + 1, :] = n_scr[2 * d + p:2 * d + p + 1, j * DK:(j + 1) * DK]
                mnew_ref[d:d + 1, :] = to_row(m_scr[H_B * d:H_B * (d + 1), 0:1])

    return state_group, out_group, finish


def _mlstm_scratch(T, C2, grid_w):
    n_pairs = C2 // 2 // LANES
    n_chunks = T // CHUNK
    pad_rows = 2 * (grid_w + SUBLANES) if T // grid_w > 1 else 2 * SUBLANES
    return [
        pltpu.VMEM((T + pad_rows, C2), F32),
        pltpu.VMEM((T, C2), F32),
        pltpu.VMEM((T, SMALL_W), F32),
        pltpu.VMEM((2, n_pairs, LANES, LANES), F32),
        pltpu.VMEM((SUBLANES, LANES), F32),
        pltpu.VMEM((SUBLANES, LANES), F32),
        pltpu.VMEM((2, n_chunks, n_pairs, LANES, LANES), BF16),
        pltpu.VMEM((2, n_chunks, SUBLANES, LANES), F32),
        pltpu.VMEM((2, n_chunks, SUBLANES, LANES), F32),
        pltpu.VMEM((2, n_chunks, SUBLANES, LANES), F32),
        pltpu.VMEM((2, n_chunks, SUBLANES, LANES), F32),
    ]


N_GLA_SCRATCH = 5
N_MLSTM_SCRATCH = 11


def _scan_kernel(*refs, cols, layer, has_state, write_state, n_cast, ride_ada, grid_w, unroll):
    refs = list(refs)
    z_refs = refs[:2]
    del refs[:2]
    s0_ref = c0_ref = n0_ref = m0_ref = None
    if has_state:
        s0_ref, c0_ref, n0_ref, m0_ref = refs[:4]
        del refs[:4]
    wa_ref, bal_ref, gwa_ref, cw_ref, bmg_ref, gwb_ref = refs[:6]
    del refs[:6]
    cast_in = refs[:n_cast]
    del refs[:n_cast]
    if ride_ada:
        ada_in = refs[:4]
        del refs[:4]
    outa_ref, outb_ref = refs[:2]
    del refs[:2]
    snew_ref = cnew_ref = nnew_ref = mnew_ref = None
    if write_state:
        snew_ref, cnew_ref, nnew_ref, mnew_ref = refs[:4]
        del refs[:4]
    cast_out = refs[:n_cast]
    del refs[:n_cast]
    if ride_ada:
        ada_out = refs.pop(0)
    wal_scr, bm_scr = refs[:2]
    del refs[:2]
    gla_scr = refs[:N_GLA_SCRATCH]
    mlstm_scr = refs[N_GLA_SCRATCH:]

    for src, dst in zip(cast_in, cast_out):
        dst[...] = src[...].astype(BF16)
    if ride_ada:
        _ada_tile(*ada_in, ada_out)

    R, HK = wa_ref.shape[1], wa_ref.shape[2]
    wal_scr[...] = jnp.zeros(wal_scr.shape, BF16)
    for d in range(2):
        wal_scr[d * R:(d + 1) * R, d * HK:(d + 1) * HK] = wa_ref[d].astype(BF16)
    lane = lax.broadcasted_iota(jnp.int32, (1, LANES), 1)
    bm = jnp.zeros((1, LANES), F32)
    for g in range(bmg_ref.shape[1]):
        for h in range(H_B):
            bm = jnp.where(lane == GATE_LANE0 + H_B * g + h, bmg_ref[layer, g, h], bm)
    bm_scr[0:1, :] = bm

    def view(name):
        a, c0, w = cols[name]
        return z_refs[a].at[:, pl.ds(c0, w)]

    sm_ref = view("small")
    n_chunks = z_refs[0].shape[0] // CHUNK
    gla = _gla_body(view("qa"), view("ka"), view("va"), view("ga"), sm_ref, s0_ref, wal_scr, bal_ref, gwa_ref,
                    outa_ref, snew_ref, *gla_scr)
    mlstm = _mlstm_body(view("qkb"), view("vb"), view("ob"), sm_ref, c0_ref, n0_ref, m0_ref, cw_ref,
                        bm_scr.at[0:1, :], gwb_ref, outb_ref, cnew_ref, nnew_ref, mnew_ref, *mlstm_scr,
                        grid_w=grid_w)
    gla_state, gla_out, gla_finish, decay_span = gla
    mlstm_state, mlstm_out, mlstm_finish = mlstm

    def passes(gla_out_fn):
        _chunk_loop(n_chunks, unroll, lambda ns: [fn([n], (d,)) for n in ns for d in range(2)
                                                  for fn in (mlstm_state, gla_state)])
        _chunk_loop(n_chunks, unroll, lambda ns: [fn([n]) for n in ns for fn in (mlstm_out, gla_out_fn)])

    wide_decay = decay_span > GLA_FACTORED_DECAY_MAX

    @pl.when(jnp.logical_not(wide_decay))
    def _():
        passes(gla_out)

    @pl.when(wide_decay)
    def _():
        passes(functools.partial(gla_out, exact_decay=True))

    gla_finish()
    mlstm_finish()


def _scan_call(z2d, row0, B, T, states, lw, layer, *, grid_w, write_state, casts=(), ada=None):
    assert row0 % T == 0 and all(z.shape[0] % T == 0 for z in z2d)
    z3 = [z.reshape(z.shape[0] // T, T, z.shape[1]) for z in z2d]
    blk0 = row0 // T
    HK = lw["w_alpha2"].shape[-1]
    DA = lw["gnorm_a_w"].shape[0]
    C2 = lw["conv_w"].shape[-1]
    DB = lw["gnorm_b_w"].shape[0]
    DK_A, DK_B = HK // H_A, C2 // 2 // H_B
    pa, pb = HK // LANES, C2 // 2 // LANES
    n_chunks = T // CHUNK
    has_state = states is not None
    widths = ((("qa", HK), ("ka", HK), ("qkb", C2), ("small", SMALL_W)),
              (("va", DA), ("ga", DA), ("vb", DB), ("ob", DB)))
    cols = {}
    for a, groups in enumerate(widths):
        c0 = 0
        for name, w in groups:
            cols[name] = (a, c0, w)
            c0 += w
        assert c0 == z3[a].shape[2]
    cast_in_specs, cast_out_specs, cast_out_shape, cast_args = _cast_specs(casts, B)
    kern = functools.partial(_scan_kernel, cols=cols, layer=layer, has_state=has_state, write_state=write_state,
                             n_cast=len(casts), ride_ada=ada is not None, grid_w=grid_w,
                             unroll=min(n_chunks, SCAN_UNROLL))

    def per_batch(shape):
        nd = len(shape)
        return pl.BlockSpec((None,) + tuple(shape), lambda b: (b,) + (0,) * nd)

    def per_batch_layer(shape):
        nd = len(shape)
        return pl.BlockSpec((None, None) + tuple(shape), lambda b: (b, layer) + (0,) * nd)

    def of_layer(a):
        return pl.BlockSpec((None,) + a.shape[1:], lambda b: (layer,) + (0,) * (a.ndim - 1))

    def whole(a):
        return pl.BlockSpec(a.shape, lambda b: (0,) * a.ndim)

    state_shapes = ((2, pa, LANES, LANES), (2, pb, LANES, LANES), (2, H_B, DK_B), (2, H_B))
    in_specs = [pl.BlockSpec((None, T, z.shape[2]), lambda b: (b + blk0, 0, 0)) for z in z3]
    args = list(z3)
    if has_state:
        s_gla, s_c, s_n, s_m = states
        depth = s_gla.shape[1]
        args += [s_gla.reshape((B, depth) + state_shapes[0]), s_c.reshape((B, depth) + state_shapes[1]), s_n, s_m]
        in_specs += [per_batch_layer(s) for s in state_shapes]
    args += [lw["w_alpha2"], lw["b_alpha"], lw["gnorm_a_w"].reshape(1, DA), lw["conv_w"], lw["b_mgate"],
             lw["gnorm_b_w"].reshape(1, DB)]
    in_specs += [of_layer(lw["w_alpha2"]), of_layer(lw["b_alpha"]), pl.BlockSpec((1, DA), lambda b: (0, 0)),
                 whole(lw["conv_w"]), pl.BlockSpec(memory_space=pltpu.SMEM), pl.BlockSpec((1, DB), lambda b: (0, 0))]
    args += cast_args
    in_specs += cast_in_specs
    out_specs = [per_batch((T, DA)), per_batch((T, DB))]
    out_shape = [jax.ShapeDtypeStruct((B, T, DA), BF16), jax.ShapeDtypeStruct((B, T, DB), BF16)]
    if write_state:
        out_specs += [per_batch(s) for s in state_shapes]
        out_shape += [jax.ShapeDtypeStruct((B,) + s, F32) for s in state_shapes]
    out_specs += cast_out_specs
    out_shape += cast_out_shape
    if ada is not None:
        cc, c, w_ada, b_ada, col0 = ada
        n_rest = w_ada.shape[1] - col0
        wcol = n_rest // B
        assert n_rest % B == 0 and wcol % LANES == 0 and col0 % wcol == 0
        args += [cc, c, w_ada, b_ada]
        in_specs += [whole(cc), whole(c),
                     pl.BlockSpec((w_ada.shape[0], wcol), lambda b: (0, col0 // wcol + b)),
                     pl.BlockSpec((1, wcol), lambda b: (0, col0 // wcol + b))]
        out_specs.append(pl.BlockSpec((COND_ROWS, wcol), lambda b: (0, b)))
        out_shape.append(jax.ShapeDtypeStruct((COND_ROWS, n_rest), F32))
    scratch = ([pltpu.VMEM((SMALL_W, 2 * HK), BF16), pltpu.VMEM((SUBLANES, LANES), F32)]
               + _gla_scratch(T, HK) + _mlstm_scratch(T, C2, grid_w))
    assert len(scratch) == 2 + N_GLA_SCRATCH + N_MLSTM_SCRATCH
    return pl.pallas_call(
        kern,
        grid=(B,),
        in_specs=in_specs,
        out_specs=out_specs,
        out_shape=out_shape,
        scratch_shapes=scratch,
        compiler_params=pltpu.CompilerParams(dimension_semantics=("arbitrary",),
                                             vmem_limit_bytes=VMEM_LIMIT),
        name="mixer_scans",
    )(*args)


def _outff_kernel(xc_ref, xl_ref, ac_ref, al_ref, bc_ref, bl_ref, mod_ref, n2_ref, fn_ref, wo_ref, w1_ref, w2_ref,
                  yc_ref, yl_ref, *, n_ctx, tiles_per_req, ff_chunk, final_norm):
    D = xc_ref.shape[1]
    DA = ac_ref.shape[1]
    is_ctx, row = _tile_group(n_ctx, tiles_per_req)

    def mod(k):
        return mod_ref[pl.ds(row, 1), (k - MOD_SPLIT) * D:(k - MOD_SPLIT + 1) * D]

    def tile(x_ref, a_ref, b_ref, y_ref):
        y = _dot(a_ref[...], wo_ref[0:DA, :]) + _dot(b_ref[...], wo_ref[DA:, :])
        x1 = x_ref[...] + mod(2) * y
        h2 = (_rms(x1, n2_ref[...]) * (1.0 + mod(4)) + mod(3)).astype(BF16)
        acc = jnp.zeros(x1.shape, F32)
        for c0 in range(0, w1_ref.shape[1], ff_chunk):
            u = jnp.maximum(_dot(h2, w1_ref[:, c0:c0 + ff_chunk]), 0.0)
            acc = acc + _dot((u * u).astype(BF16), w2_ref[c0:c0 + ff_chunk, :])
        x2 = x1 + mod(5) * acc
        y_ref[...] = _rms(x2, fn_ref[...]) if final_norm else x2

    @pl.when(is_ctx)
    def _():
        tile(xc_ref, ac_ref, bc_ref, yc_ref)

    @pl.when(jnp.logical_not(is_ctx))
    def _():
        tile(xl_ref, al_ref, bl_ref, yl_ref)


def _outff_call(xc2d, xl2d, ac, al, bc, bl, mod, norm2_w, final_w, wo, w1, w2, *, tm, tiles_per_req, final_norm):
    (Mc, D), Ml = xc2d.shape, xl2d.shape[0]
    n_ctx = Mc // tm
    DA = ac.shape[1]
    DFF = w1.shape[1]
    kern = functools.partial(_outff_kernel, n_ctx=n_ctx, tiles_per_req=tiles_per_req, ff_chunk=FF_CHUNK,
                             final_norm=final_norm)
    once = pl.Buffered(1)
    ctx, lat = _ctx_tile(n_ctx), _lat_tile(n_ctx)
    return pl.pallas_call(
        kern,
        grid=((Mc + Ml) // tm,),
        in_specs=[
            pl.BlockSpec((tm, D), ctx), pl.BlockSpec((tm, D), lat),
            pl.BlockSpec((tm, DA), ctx), pl.BlockSpec((tm, DA), lat),
            pl.BlockSpec((tm, D - DA), ctx), pl.BlockSpec((tm, D - DA), lat),
            pl.BlockSpec(mod.shape, lambda i: (0, 0)),
            pl.BlockSpec((1, D), lambda i: (0, 0)),
            pl.BlockSpec((1, D), lambda i: (0, 0)),
            pl.BlockSpec((D, D), lambda i: (0, 0), pipeline_mode=once),
            pl.BlockSpec((D, DFF), lambda i: (0, 0), pipeline_mode=once),
            pl.BlockSpec((DFF, D), lambda i: (0, 0), pipeline_mode=once),
        ],
        out_specs=[pl.BlockSpec((tm, D), ctx), pl.BlockSpec((tm, D), lat)],
        out_shape=[jax.ShapeDtypeStruct((Mc, D), F32), jax.ShapeDtypeStruct((Ml, D), F32)],
        compiler_params=pltpu.CompilerParams(dimension_semantics=("arbitrary",),
                                             vmem_limit_bytes=VMEM_LIMIT),
        name="outproj_mlp",
    )(xc2d, xl2d, ac, al, bc, bl, mod, norm2_w.reshape(1, D), final_w.reshape(1, D), wo, w1, w2)


def _layer(xc, xl, cond, ada_w, cached, lw, layer, ffw, final_w, final_norm):
    (Bc, Tc, D), (Bl, Tl, _) = xc.shape, xl.shape
    tm = TOKEN_TILE
    assert (Bc * Tc) % tm == 0 and Tl % tm == 0 and (Bc * Tc) % Tl == 0
    xc2d, xl2d = xc.reshape(Bc * Tc, D), xl.reshape(Bl * Tl, D)
    z = _inproj_call(xc2d, xl2d, *cond, *ada_w, lw["norm1_w"], lw["w_in_t"], tm=tm, tiles_per_req=Tl // tm,
                     f32_rows=lw["f32_rows"], small_rows=lw["small_rows"], bf16_rows=lw["bf16_rows"])
    res_c = _scan_call(z, 0, Bc, Tc, None, lw, layer, grid_w=Tc, write_state=True,
                       casts=((ffw[0], 0), (ffw[1], 0), (ffw[2], 0)), ada=(*cond, *ada_w, MOD_SPLIT * D))
    res_l = _scan_call(z, Bc * Tc, Bl, Tl, cached, lw, layer, grid_w=GRID_W, write_state=False)
    wo_b, w1_b, w2_b, mod_out = res_c[-4:]
    yc, yl = _outff_call(xc2d, xl2d, res_c[0].reshape(Bc * Tc, -1), res_l[0].reshape(Bl * Tl, -1),
                         res_c[1].reshape(Bc * Tc, -1), res_l[1].reshape(Bl * Tl, -1), mod_out, lw["norm2_w"],
                         final_w, wo_b, w1_b, w2_b, tm=tm, tiles_per_req=Tl // tm, final_norm=final_norm)
    return yc.reshape(Bc, Tc, D), yl.reshape(Bl, Tl, D), tuple(res_c[2:6])


def _layer_weights(l, norm1_w, norm2_w, w_in, w_alpha2, b_alpha, b_mgate, conv_w, gnorm_a_w, gnorm_b_w):
    hk_a = w_alpha2.shape[-1]
    d_a = gnorm_a_w.shape[-1]
    d_b = gnorm_b_w.shape[-1]
    hk_b = conv_w.shape[-1] // 2
    sizes = (hk_a, hk_a, d_a, d_a, 2 * R_ALPHA, hk_b, hk_b, d_b, d_b, 4 * H_B)
    assert w_alpha2.shape[2] == R_ALPHA and b_mgate.shape[1] * b_mgate.shape[2] == 4 * H_B
    offs = [0]
    for s in sizes:
        offs.append(offs[-1] + s)
    f32_rows = ((offs[0], offs[2] - offs[0]), (offs[5], offs[7] - offs[5]))
    small_rows = ((offs[4], offs[5] - offs[4]), (offs[9], offs[10] - offs[9]))
    bf16_rows = ((offs[2], offs[4] - offs[2]), (offs[7], offs[9] - offs[7]))
    assert all(n % LANES == 0 and r % BF16_ROWS == 0 for r, n in f32_rows + bf16_rows)
    return dict(
        norm1_w=norm1_w[l], norm2_w=norm2_w[l], w_in_t=jnp.swapaxes(w_in[l], 0, 1),
        f32_rows=f32_rows, small_rows=small_rows, bf16_rows=bf16_rows,
        w_alpha2=w_alpha2, b_alpha=b_alpha, b_mgate=b_mgate, conv_w=conv_w[l],
        gnorm_a_w=gnorm_a_w[l], gnorm_b_w=gnorm_b_w[l],
    )


def kernel(x_prompt, x_sample, c, state_gla, state_mlstm_C, state_mlstm_n, state_mlstm_m, c_ctx, w_ada, b_ada, norm1_w, norm2_w, w_in, w_alpha2, b_alpha, b_mgate, conv_w, gnorm_a_w, gnorm_b_w, w_out, w_ff1, w_ff2, final_norm_w):
    depth = w_in.shape[0]
    D = x_prompt.shape[-1]
    Bp, Tp, _ = x_prompt.shape
    Bs = x_sample.shape[0]
    assert 1 + Bs <= COND_ROWS
    cond = (c_ctx.reshape(1, D), c)
    cached = (state_gla, state_mlstm_C, state_mlstm_n, state_mlstm_m)
    xp, xs = x_prompt, x_sample
    s_gla, s_c, s_n, s_m = [], [], [], []
    for l in range(depth):
        lw = _layer_weights(l, norm1_w, norm2_w, w_in, w_alpha2, b_alpha, b_mgate, conv_w,
                            gnorm_a_w, gnorm_b_w)
        xp, xs, ctx = _layer(xp, xs, cond, (w_ada[l], b_ada[l].reshape(1, -1)), cached, lw, l,
                             (w_out[l], w_ff1[l], w_ff2[l]), final_norm_w, l == depth - 1)
        s_gla.append(ctx[0].reshape(Bp, 2, H_A, -1, ctx[0].shape[-1]))
        s_c.append(ctx[1].reshape(Bp, 2, H_B, -1, ctx[1].shape[-1]))
        s_n.append(ctx[2])
        s_m.append(ctx[3])
    dt = x_prompt.dtype
    return (xp, xs, jnp.stack(s_gla, axis=1).astype(dt), jnp.stack(s_c, axis=1).astype(dt),
            jnp.stack(s_n, axis=1).astype(dt), jnp.stack(s_m, axis=1).astype(dt))
```

```python
import functools
import math

import jax
import jax.numpy as jnp
from jax import lax
from jax.experimental import pallas as pl
from jax.experimental.pallas import tpu as pltpu

F32 = jnp.float32
BF16 = jnp.bfloat16

GRID_W = 64
H_A = 4
H_B = 4
R_ALPHA = 16
TAU_GLA = 16.0
CHUNK = 64
EPS = 1e-6
LANES = 128
SUBLANES = 8
BF16_ROWS = 16
COND_ROWS = SUBLANES
SMALL_W = LANES
GATE_LANE0 = 2 * R_ALPHA
VMEM_LIMIT = 56 * 1024 * 1024
SCAN_UNROLL = 4
PIPELINE_STARTS = 8
TOKEN_TILE = 512
FF_CHUNK = 512
GLA_FACTORED_DECAY_MAX = 60.0
MOD_SPLIT = 2


def _sigmoid(x):
    return 1.0 / (1.0 + jnp.exp(-x))


def _silu(x):
    return x * _sigmoid(x)


def _log_sigmoid(x):
    return jnp.minimum(x, 0.0) - jnp.log(1.0 + jnp.exp(-jnp.abs(x)))


def _dot(a, b):
    return jnp.dot(a, b, preferred_element_type=F32)


def _dot_nt(a, b):
    return lax.dot_general(a, b, (((1,), (1,)), ((), ())), preferred_element_type=F32)


def _rms(x, w):
    return x * lax.rsqrt(jnp.mean(x * x, axis=-1, keepdims=True) + EPS) * w


def _tri_sum(tri, x, terms=3):
    acc, rest = None, x
    for t in range(terms):
        part = rest.astype(BF16)
        prod = _dot(tri, part)
        acc = prod if acc is None else acc + prod
        if t + 1 < terms:
            rest = rest - part.astype(F32)
    return acc


def _chunk_masks(L):
    row = lax.broadcasted_iota(jnp.int32, (L, L), 0)
    col = lax.broadcasted_iota(jnp.int32, (L, L), 1)
    lower = row >= col
    upper = row <= col
    return lower, upper


def _ada_tile(cc_ref, c_ref, w_ref, b_ref, o_ref):
    D = cc_ref.shape[1]
    sub = lax.broadcasted_iota(jnp.int32, (COND_ROWS, D), 0)
    cond = jnp.where(sub == 0, cc_ref[...], 0.0)
    for r in range(c_ref.shape[0]):
        cond = jnp.where(sub == 1 + r, c_ref[r:r + 1, :], cond)
    o_ref[...] = _dot(_silu(cond).astype(BF16), w_ref[...].astype(BF16)) + b_ref[...]


def _tile_group(n_ctx, tiles_per_req):
    i = pl.program_id(0)
    is_ctx = i < n_ctx
    row = jnp.where(is_ctx, 0, 1 + jnp.maximum(i - n_ctx, 0) // tiles_per_req)
    return is_ctx, row


def _ctx_tile(n_ctx):
    return lambda i: (jnp.minimum(i, n_ctx - 1), 0)


def _lat_tile(n_ctx):
    return lambda i: (jnp.maximum(i - n_ctx, 0), 0)


def _inproj_kernel(xc_ref, xl_ref, cc_ref, c_ref, wa_ref, ba_ref, nw_ref, wt_ref, zf_ref, zh_ref, wb_scr, mod_ref,
                   *, n_ctx, tiles_per_req, f32_rows, small_rows, bf16_rows):
    D = xc_ref.shape[1]
    n_f32 = zf_ref.shape[1]

    @pl.when(pl.program_id(0) == 0)
    def _():
        _ada_tile(cc_ref, c_ref, wa_ref, ba_ref, mod_ref)

        def wide(rows, col):
            for r0, n in rows:
                for k in range(n // LANES):
                    blk = wt_ref[r0 + k * LANES:r0 + (k + 1) * LANES, :]
                    wb_scr[:, col:col + LANES] = blk.T.astype(BF16)
                    col += LANES
            return col

        col = wide(f32_rows, 0)
        parts = [wt_ref[r0:r0 + n, :] for r0, n in small_rows]
        n_small = sum(n for _, n in small_rows)
        parts.append(jnp.zeros((SMALL_W - n_small, D), F32))
        wb_scr[:, col:col + SMALL_W] = jnp.concatenate(parts, axis=0).T.astype(BF16)
        wide(bf16_rows, col + SMALL_W)

    is_ctx, row = _tile_group(n_ctx, tiles_per_req)

    def tile(x_ref):
        sh1 = mod_ref[pl.ds(row, 1), 0:D]
        sc1 = mod_ref[pl.ds(row, 1), D:2 * D]
        h = (_rms(x_ref[...], nw_ref[...]) * (1.0 + sc1) + sh1).astype(BF16)
        zf_ref[...] = _dot(h, wb_scr[:, 0:n_f32])
        zh_ref[...] = _dot(h, wb_scr[:, n_f32:]).astype(BF16)

    @pl.when(is_ctx)
    def _():
        tile(xc_ref)

    @pl.when(jnp.logical_not(is_ctx))
    def _():
        tile(xl_ref)


def _inproj_call(xc2d, xl2d, cc, c, w_ada, b_ada, norm_w, w_in_t, *, tm, tiles_per_req, f32_rows, small_rows,
                 bf16_rows):
    (Mc, D), Ml = xc2d.shape, xl2d.shape[0]
    n_ctx = Mc // tm
    n_f32 = sum(n for _, n in f32_rows) + SMALL_W
    n_bf16 = sum(n for _, n in bf16_rows)
    n_out = n_f32 + n_bf16
    n_mod = MOD_SPLIT * D
    kern = functools.partial(_inproj_kernel, n_ctx=n_ctx, tiles_per_req=tiles_per_req,
                             f32_rows=f32_rows, small_rows=small_rows, bf16_rows=bf16_rows)
    once = pl.Buffered(1)
    return pl.pallas_call(
        kern,
        grid=((Mc + Ml) // tm,),
        in_specs=[
            pl.BlockSpec((tm, D), _ctx_tile(n_ctx)),
            pl.BlockSpec((tm, D), _lat_tile(n_ctx)),
            pl.BlockSpec(cc.shape, lambda i: (0, 0)),
            pl.BlockSpec(c.shape, lambda i: (0, 0)),
            pl.BlockSpec((D, n_mod), lambda i: (0, 0), pipeline_mode=once),
            pl.BlockSpec((1, n_mod), lambda i: (0, 0)),
            pl.BlockSpec((1, D), lambda i: (0, 0)),
            pl.BlockSpec(w_in_t.shape, lambda i: (0, 0), pipeline_mode=once),
        ],
        out_specs=[pl.BlockSpec((tm, n_f32), lambda i: (i, 0)), pl.BlockSpec((tm, n_bf16), lambda i: (i, 0))],
        out_shape=[jax.ShapeDtypeStruct((Mc + Ml, n_f32), F32), jax.ShapeDtypeStruct((Mc + Ml, n_bf16), BF16)],
        scratch_shapes=[pltpu.VMEM((D, n_out), BF16), pltpu.VMEM((COND_ROWS, n_mod), F32)],
        compiler_params=pltpu.CompilerParams(dimension_semantics=("arbitrary",),
                                             vmem_limit_bytes=VMEM_LIMIT),
        name="norm_inproj",
    )(xc2d, xl2d, cc, c, w_ada, b_ada, norm_w.reshape(1, D), w_in_t)


def _chunk_loop(n_chunks, unroll, make_units):
    def step(ns):
        pending = list(make_units(ns))
        active = []
        while pending or active:
            for _ in range(min(PIPELINE_STARTS, len(pending))):
                active.append(pending.pop(0))
            alive = []
            for g in active:
                try:
                    next(g)
                    alive.append(g)
                except StopIteration:
                    pass
            active = alive

    if unroll >= n_chunks:
        step(list(range(n_chunks)))
        return

    def body(i, carry):
        step([i * unroll + u for u in range(unroll)])
        return carry

    lax.fori_loop(0, n_chunks // unroll, body, 0)


def _chunk_rows(n):
    if isinstance(n, int):
        return pl.ds(n * CHUNK, CHUNK)
    return pl.ds(pl.multiple_of(n * CHUNK, CHUNK), CHUNK)


def _cast_specs(casts, n_steps):
    in_specs, out_specs, out_shape, args = [], [], [], []
    for w, axis in casts:
        blk = list(w.shape)
        assert blk[axis] % n_steps == 0
        blk[axis] //= n_steps
        assert blk[0] % BF16_ROWS == 0 and blk[1] % LANES == 0
        idx = (lambda b: (b, 0)) if axis == 0 else (lambda b: (0, b))
        in_specs.append(pl.BlockSpec(tuple(blk), idx))
        out_specs.append(pl.BlockSpec(tuple(blk), idx))
        out_shape.append(jax.ShapeDtypeStruct(w.shape, BF16))
        args.append(w)
    return in_specs, out_specs, out_shape, args


def _gla_body(q_ref, k_ref, v_ref, g_ref, sm_ref, s0_ref, wal_ref, bal_ref, gw_ref, out_ref, snew_ref,
              st_scr, sall_scr, qh_scr, qs_scr, kh_scr):
    has_state = s0_ref is not None
    write_state = snew_ref is not None
    T = q_ref.shape[0]
    L = CHUNK
    N = T // L
    HK = q_ref.shape[1]
    DK = HK // H_A
    DV = v_ref.shape[1] // H_A
    scale = DK ** -0.5
    n_pairs = HK // LANES

    lower, upper = _chunk_masks(L)
    tri = (lower.astype(BF16), upper.astype(BF16))
    tmask = (lower, upper)
    lane = lax.broadcasted_iota(jnp.int32, (1, LANES), 1)
    head_mask = (lane < DK, lane >= DK)

    for d in range(2):
        for p in range(n_pairs):
            if has_state:
                st_scr[d, p] = s0_ref[d, p].T
            else:
                st_scr[d, p] = jnp.zeros((LANES, LANES), F32)

    def decay_pre(d, r):
        return _dot(sm_ref[r, :].astype(BF16), wal_ref[:, d * HK:(d + 1) * HK]) + bal_ref[d:d + 1, :]

    neg_pre = jnp.maximum(-(_dot(sm_ref[...].astype(BF16), wal_ref[...])
                            + jnp.concatenate([bal_ref[0:1, :], bal_ref[1:2, :]], axis=1)), 0.0)
    chunk_sums = jnp.sum(neg_pre.reshape(N, L, 2 * HK), axis=1)
    decay_span = (jnp.max(chunk_sums) + L * math.log(2.0)) * (1.0 / TAU_GLA)

    def state_group(ns, dirs=(0, 1)):
        units = [(d, n if d == 0 else N - 1 - n) for n in ns for d in dirs]
        rows = [_chunk_rows(n) for _, n in units]
        vt_all = [[jnp.concatenate([v_ref[r, (2 * p + j) * DV:(2 * p + j + 1) * DV] for j in range(2)],
                                   axis=0).astype(F32).T.astype(BF16) for p in range(n_pairs)] for r in rows]
        yield
        pre = [decay_pre(d, r) for (d, _), r in zip(units, rows)]
        yield
        g = [_log_sigmoid(x) * (1.0 / TAU_GLA) for x in pre]
        yield
        b = [_tri_sum(tri[d], gi, terms=2) for (d, _), gi in zip(units, g)]
        yield
        ks_all, dec_all = [], []
        for (d, _), r, bi in zip(units, rows, b):
            bend = bi[L - 1:L, :] if d == 0 else bi[0:1, :]
            q = q_ref[r, :] * scale
            ks = (k_ref[r, :] * jnp.exp(bend - bi)).astype(BF16)
            qs = q * jnp.exp(bi)
            qh_scr[d, r, :] = (qs * jnp.exp(-bend)).astype(BF16)
            qs_scr[d, r, :] = qs.astype(BF16)
            kh_scr[d, r, :] = ks
            ks_all.append(ks)
            dec_all.append(jnp.exp(bend))
        yield
        upd_all = []
        for vt_u, ks in zip(vt_all, ks_all):
            upd_u = []
            for p in range(n_pairs):
                kp = ks[:, p * LANES:(p + 1) * LANES]
                kk = jnp.concatenate([jnp.where(head_mask[j], kp, jnp.zeros_like(kp)) for j in range(2)], axis=0)
                upd_u.append(_dot(vt_u[p], kk))
            upd_all.append(upd_u)
        yield
        st = {d: [st_scr[d, p] for p in range(n_pairs)] for d in dirs}
        for (d, n), dec, upd in zip(units, dec_all, upd_all):
            for p in range(n_pairs):
                sall_scr[d, n, p] = st[d][p].astype(BF16)
                st[d][p] = st[d][p] * dec[:, p * LANES:(p + 1) * LANES] + upd[p]
        for d in dirs:
            for p in range(n_pairs):
                st_scr[d, p] = st[d][p]

    def stack_heads(x):
        return jnp.concatenate([jnp.where(head_mask[j], x, jnp.zeros_like(x)) for j in range(2)], axis=0)

    tok = lax.broadcasted_iota(jnp.int32, (L, 1), 0)
    row_t = lax.broadcasted_iota(jnp.int32, (2 * L, L), 0) & (L - 1)
    col_s = lax.broadcasted_iota(jnp.int32, (2 * L, L), 1)

    def exact_scores(d, r, p):
        ls = slice(p * LANES, (p + 1) * LANES)
        b = _tri_sum(tri[d], _log_sigmoid(decay_pre(d, r)[:, ls]) * (1.0 / TAU_GLA))
        q = q_ref[r, ls] * scale
        k = k_ref[r, ls]
        acc = jnp.where(row_t == col_s, _dot_nt(stack_heads(q).astype(BF16), k.astype(BF16)), 0.0)
        src = lax.broadcasted_iota(jnp.int32, (L, L), 1)
        h = L // 2
        while h >= 1:
            first = tok & ~(2 * h - 1)
            edge = first + (h - 1 if d == 0 else h)
            b_edge = _tri_sum((src == edge).astype(BF16), b)
            upper = (tok & (2 * h - 1)) >= h
            later, earlier = (upper, ~upper) if d == 0 else (~upper, upper)
            qt = jnp.where(later, q * jnp.exp(b - b_edge), 0.0)
            kt = jnp.where(earlier, k * jnp.exp(b_edge - b), 0.0)
            sc = _dot_nt(stack_heads(qt).astype(BF16), kt.astype(BF16))
            acc = acc + jnp.where((row_t & ~(2 * h - 1)) == (col_s & ~(2 * h - 1)), sc, 0.0)
            h //= 2
        return acc

    def out_group(ns, exact_decay=False):
        pairs = [(d, ni, p) for ni in range(len(ns)) for d in range(2) for p in range(n_pairs)]
        scores, inter = [], []
        for d, ni, p in pairs:
            r = _chunk_rows(ns[ni])
            ls = slice(p * LANES, (p + 1) * LANES)
            if exact_decay:
                scores.append(exact_scores(d, r, p))
            else:
                scores.append(_dot_nt(stack_heads(qh_scr[d, r, ls]), kh_scr[d, r, ls]))
            inter.append(_dot_nt(stack_heads(qs_scr[d, r, ls]), sall_scr[d, ns[ni], p]))
        yield
        probs = [[jnp.where(tmask[d], sc[j * L:(j + 1) * L, :], 0.0).astype(BF16) for j in range(2)]
                 for (d, _, _), sc in zip(pairs, scores)]
        yield
        outs = {}
        for (d, ni, p), pr, it in zip(pairs, probs, inter):
            r = _chunk_rows(ns[ni])
            for j in range(2):
                vs = slice((2 * p + j) * DV, (2 * p + j + 1) * DV)
                outs[(d, ni, 2 * p + j)] = _dot(pr[j], v_ref[r, vs]) + it[j * L:(j + 1) * L, :]
        yield
        for ni, n in enumerate(ns):
            r = _chunk_rows(n)
            for h in range(H_A):
                vs = slice(h * DV, (h + 1) * DV)
                o = outs[(0, ni, h)] + outs[(1, ni, h)]
                out_ref[r, vs] = (_rms(o, gw_ref[:, vs]) * _silu(g_ref[r, vs].astype(F32))).astype(out_ref.dtype)

    def finish():
        if write_state:
            for d in range(2):
                for p in range(n_pairs):
                    snew_ref[d, p] = st_scr[d, p].T

    return state_group, out_group, finish, decay_span


def _gla_scratch(T, HK):
    n_pairs = HK // LANES
    n_chunks = T // CHUNK
    return [
        pltpu.VMEM((2, n_pairs, LANES, LANES), F32),
        pltpu.VMEM((2, n_chunks, n_pairs, LANES, LANES), BF16),
        pltpu.VMEM((2, T, HK), BF16),
        pltpu.VMEM((2, T, HK), BF16),
        pltpu.VMEM((2, T, HK), BF16),
    ]


def _mlstm_body(qk_ref, v_ref, og_ref, sm_ref, c0_ref, n0_ref, m0_ref, cw_ref, bm_ref, gw_ref,
                out_ref, cnew_ref, nnew_ref, mnew_ref,
                pad_scr, qk_scr, y_scr, c_scr, n_scr, m_scr, call_scr, nall_scr, mall_scr, g_scr, f_scr,
                *, grid_w):
    has_state = c0_ref is not None
    write_state = cnew_ref is not None
    T = qk_ref.shape[0]
    L = CHUNK
    N = T // L
    C2 = qk_ref.shape[1]
    HK = C2 // 2
    DK = HK // H_B
    DV = v_ref.shape[1] // H_B
    scale = DK ** -0.5
    n_pairs = HK // LANES
    P = pad_scr.shape[0] - T
    P0 = P // 2
    rows_img = T // grid_w

    lower, upper = _chunk_masks(L)
    tri = (lower.astype(BF16), upper.astype(BF16))
    tmask = (lower, upper)
    lane = lax.broadcasted_iota(jnp.int32, (1, LANES), 1)
    head_mask = (lane < DK, lane >= DK)
    lane_in = lane & (L - 1)

    def lane_cummax(x, d):
        k = 1
        while k < L:
            if d == 0:
                x = jnp.maximum(x, jnp.where(lane_in >= k, pltpu.roll(x, k, axis=1), -jnp.inf))
            else:
                x = jnp.maximum(x, jnp.where(lane_in < L - k, pltpu.roll(x, LANES - k, axis=1), -jnp.inf))
            k *= 2
        return x

    for d in range(2):
        for p in range(n_pairs):
            if has_state:
                c_scr[d, p] = c0_ref[d, p]
                n_scr[2 * d + p:2 * d + p + 1, :] = jnp.concatenate(
                    [n0_ref[d, 2 * p + j:2 * p + j + 1, :] for j in range(2)], axis=1)
            else:
                c_scr[d, p] = jnp.zeros((LANES, LANES), F32)
                n_scr[2 * d + p:2 * d + p + 1, :] = jnp.zeros((1, LANES), F32)
    eye_h = (lax.broadcasted_iota(jnp.int32, (H_B, H_B), 0) == lax.broadcasted_iota(jnp.int32, (H_B, H_B), 1))

    def to_col(row):
        return jnp.sum(jnp.where(eye_h, row, 0.0), axis=1, keepdims=True)

    def to_row(col):
        return jnp.sum(jnp.where(eye_h, col, 0.0), axis=0, keepdims=True)

    for d in range(2):
        if has_state:
            m_scr[H_B * d:H_B * (d + 1), 0:1] = to_col(m0_ref[d:d + 1, :])
        else:
            m_scr[H_B * d:H_B * (d + 1), 0:1] = jnp.zeros((H_B, 1), F32)

    pad_scr[0:P0, :] = jnp.zeros((P0, C2), F32)
    pad_scr[P0 + T:P + T, :] = jnp.zeros((P - P0, C2), F32)

    def copy_in(i, carry):
        r0 = pl.multiple_of(i * L, L)
        pad_scr[pl.ds(P0 + r0, L), :] = qk_ref[pl.ds(r0, L), :]
        return carry

    lax.fori_loop(0, N, copy_in, 0)

    lane_c = lax.broadcasted_iota(jnp.int32, (1, C2), 1)
    qscale = jnp.where(lane_c < HK, scale, 1.0).astype(F32)
    sub = lax.broadcasted_iota(jnp.int32, (L, 1), 0)
    img_rows = (0,) if rows_img == 1 else (-1, 0, 1)

    def conv_tile(i, carry):
        r0 = pl.multiple_of(i * L, L)
        col = lax.rem(r0, grid_w) + sub
        ok_left = col >= 1
        ok_right = col <= grid_w - 2
        sums = [None, None, None]
        for di in img_rows:
            blk = pad_scr[pl.ds(P0 + r0 + di * grid_w - SUBLANES, L + 2 * SUBLANES), :]
            for k in range(3):
                term = blk * cw_ref[di + 1, k:k + 1, :]
                sums[k] = term if sums[k] is None else sums[k] + term
        S = SUBLANES
        acc = (sums[1][S:S + L, :] + jnp.where(ok_left, sums[0][S - 1:S - 1 + L, :], 0.0)
               + jnp.where(ok_right, sums[2][S + 1:S + 1 + L, :], 0.0))
        qk_scr[pl.ds(r0, L), :] = _silu(acc) * qscale
        return carry

    lax.fori_loop(0, N, conv_tile, 0)

    gl = lane - GATE_LANE0
    is_f = ((gl >= H_B) & (gl < 2 * H_B)) | ((gl >= 3 * H_B) & (gl < 4 * H_B))

    def gate_tile(i, carry):
        rows = pl.ds(pl.multiple_of(i * L, L), L)
        x = sm_ref[rows, :] + bm_ref[...]
        y_scr[rows, :] = jnp.where(is_f, _log_sigmoid(x), x)
        return carry

    lax.fori_loop(0, N, gate_tile, 0)


    def state_group(ns, dirs=(0, 1)):
        units = [(d, n if d == 0 else N - 1 - n) for n in ns for d in dirs]
        rows = [_chunk_rows(n) for _, n in units]
        kt_all = [[qk_scr[r, HK + p * LANES:HK + (p + 1) * LANES].T for p in range(n_pairs)] for r in rows]
        yield
        xs = [y_scr[r, :] for r in rows]
        fsum = [_tri_sum(tri[d], x) for (d, _), x in zip(units, xs)]
        yield
        wk_all, f_end, c_end = [], [], []
        for (d, n), r, x, fs in zip(units, rows, xs, fsum):
            y = jnp.where(is_f, fs, x)
            li0 = GATE_LANE0 + 2 * H_B * d
            blk = jnp.concatenate([y, y], axis=0).T[li0:li0 + 2 * H_B, :]
            frow = pltpu.roll(blk, H_B, axis=0)
            grow = blk - frow
            g_scr[d, n] = grow
            f_scr[d, n] = frow
            e_col = L - 1 if d == 0 else 0
            f_end.append(frow[0:H_B, e_col:e_col + 1])
            ce8 = jnp.max(grow, axis=1, keepdims=True)
            c_end.append(ce8[0:H_B, :])
            wk_all.append(jnp.exp(grow[:, 0:L] - ce8))
        yield
        kv_all, ksum_all = [], []
        for r, wk8, kt_u in zip(rows, wk_all, kt_all):
            kv_u, ks_u = [], []
            wk8b = wk8.astype(BF16)
            for p in range(n_pairs):
                kpb = qk_scr[r, HK + p * LANES:HK + (p + 1) * LANES].astype(BF16)
                ks8 = _dot(wk8b, kpb)
                for j in range(2):
                    h = 2 * p + j
                    kwt = (kt_u[p][j * DK:(j + 1) * DK, :] * wk8[h:h + 1, :]).astype(BF16)
                    kv_u.append(_dot(kwt, v_ref[r, h * DV:(h + 1) * DV]))
                    ks_u.append(ks8[h:h + 1, :])
            kv_all.append(kv_u)
            ksum_all.append(ks_u)
        yield
        m_run = {d: m_scr[H_B * d:H_B * (d + 1), 0:1] for d in dirs}
        a_all, b_all = [], []
        for (d, n), fe, ce in zip(units, f_end, c_end):
            mall_scr[d, n, 0:H_B, 0:1] = m_run[d]
            mx = jnp.maximum(m_run[d], ce)
            a_all.append(jnp.exp(m_run[d] - mx))
            b_all.append(jnp.exp(ce - mx))
            m_run[d] = fe + mx
        for d in dirs:
            m_scr[H_B * d:H_B * (d + 1), 0:1] = m_run[d]
        yield
        c_run = {d: [[c_scr[d, p, j * DK:(j + 1) * DK, :] for j in range(2)] for p in range(n_pairs)] for d in dirs}
        n_run = {d: [n_scr[2 * d + p:2 * d + p + 1, :] for p in range(n_pairs)] for d in dirs}
        for (d, n), a4, b4, kv_u, ks_u in zip(units, a_all, b_all, kv_all, ksum_all):
            for p in range(n_pairs):
                nall_scr[d, n, p:p + 1, :] = n_run[d][p]
                a_s = [a4[2 * p + j:2 * p + j + 1, :] for j in range(2)]
                b_s = [b4[2 * p + j:2 * p + j + 1, :] for j in range(2)]
                for j in range(2):
                    cj = c_run[d][p][j]
                    call_scr[d, n, p, j * DK:(j + 1) * DK, :] = cj.astype(BF16)
                    c_run[d][p][j] = a_s[j] * cj + b_s[j] * kv_u[2 * p + j]
                n_run[d][p] = (jnp.where(head_mask[0], a_s[0], a_s[1]) * n_run[d][p]
                               + jnp.where(head_mask[0], b_s[0] * ks_u[2 * p], b_s[1] * ks_u[2 * p + 1]))
        for d in dirs:
            for p in range(n_pairs):
                n_scr[2 * d + p:2 * d + p + 1, :] = n_run[d][p]
                for j in range(2):
                    c_scr[d, p, j * DK:(j + 1) * DK, :] = c_run[d][p][j]

    eye = lower & upper
    ones8 = jnp.ones((SUBLANES, L), BF16)
    sub8 = lax.broadcasted_iota(jnp.int32, (SUBLANES, LANES), 0)
    sub_h = lax.broadcasted_iota(jnp.int32, (H_B, L), 0)
    n_rows = [((sub8 == 2 * p) & head_mask[0]) | ((sub8 == 2 * p + 1) & head_mask[1]) for p in range(n_pairs)]

    def head_rows(vals):
        out = vals[0][0:H_B, :]
        for h in range(1, H_B):
            out = jnp.where(sub_h == h, vals[h][0:H_B, :], out)
        return out

    def out_group(ns):
        chunks = [(d, n) for n in ns for d in range(2)]
        pairs = [(d, n, p) for d, n in chunks for p in range(n_pairs)]
        units = [(d, n, p, j) for d, n, p in pairs for j in range(2)]
        cms = [lane_cummax(g_scr[d, n], d)[0:H_B, 0:L] for d, n in chunks]
        qk2s, qc2s, qn2s = [], [], []
        for d, n, p in pairs:
            r = _chunk_rows(n)
            qp = qk_scr[r, p * LANES:(p + 1) * LANES]
            q2 = jnp.concatenate([jnp.where(head_mask[j], qp, 0.0) for j in range(2)], axis=0).astype(BF16)
            qk2s.append(_dot_nt(q2, qk_scr[r, HK + p * LANES:HK + (p + 1) * LANES].astype(BF16)))
            qc2s.append(_dot(q2, call_scr[d, n, p]))
            nsel = jnp.where(n_rows[p], nall_scr[d, n, p:p + 1, :], 0.0).astype(BF16)
            qn2s.append(_dot_nt(nsel, qp.astype(BF16)))
        yield
        s_all = []
        for ui, (d, n, p, j) in enumerate(units):
            grow = g_scr[d, n, 2 * p + j:2 * p + j + 1, 0:L]
            e = jnp.where(tmask[d], grow, -jnp.inf)
            cmax = jnp.max(e, axis=-1, keepdims=True)
            s_all.append((qk2s[ui // 2][j * L:(j + 1) * L, :] * jnp.exp(e - cmax)).astype(BF16))
        yield
        nums = [_dot(s, v_ref[_chunk_rows(n), (2 * p + j) * DV:(2 * p + j + 1) * DV])
                for (d, n, p, j), s in zip(units, s_all)]
        dens = [_dot_nt(ones8, s) for s in s_all]
        yield
        scales = []
        for ci, (d, n) in enumerate(chunks):
            den_loc = head_rows(dens[ci * H_B:(ci + 1) * H_B])
            qn = qn2s[ci * n_pairs][0:H_B, :]
            for p in range(1, n_pairs):
                qn = qn + qn2s[ci * n_pairs + p][0:H_B, :]
            cm = cms[ci]
            m_prev = mall_scr[d, n, 0:H_B, 0:1]
            delta = cm - m_prev
            t = jnp.exp(-jnp.abs(delta))
            w_loc = jnp.where(delta <= 0.0, t, 1.0)
            w_inter = jnp.where(delta <= 0.0, 1.0, t)
            mt = f_scr[d, n, 0:H_B, 0:L] + jnp.maximum(m_prev, cm)
            den = w_loc * den_loc + w_inter * qn
            rinv = 1.0 / jnp.maximum(jnp.abs(den), jnp.exp(-mt))
            scales.append((w_loc * rinv, w_inter * rinv))
        yield
        hs = []
        for ui, (d, n, p, j) in enumerate(units):
            h = 2 * p + j
            sc_loc, sc_inter = scales[ui // H_B]
            d_loc = jnp.where(eye, sc_loc[h:h + 1, :], 0.0).astype(BF16)
            d_inter = jnp.where(eye, sc_inter[h:h + 1, :], 0.0).astype(BF16)
            hs.append(_dot(d_loc, nums[ui].astype(BF16))
                      + _dot(d_inter, qc2s[ui // 2][j * L:(j + 1) * L, :].astype(BF16)))
        yield
        for ni, n in enumerate(ns):
            r = _chunk_rows(n)
            for h in range(H_B):
                vs = slice(h * DV, (h + 1) * DV)
                o = hs[(2 * ni) * H_B + h] + hs[(2 * ni + 1) * H_B + h]
                out_ref[r, vs] = (_rms(o, gw_ref[:, vs]) * _sigmoid(og_ref[r, vs].astype(F32))).astype(out_ref.dtype)

    def finish():
        if write_state:
            for d in range(2):
                for p in range(n_pairs):
                    cnew_ref[d, p] = c_scr[d, p]
                    for j in range(2):
                        nnew_ref[d, 2 * p + j:2 * p + j + 1, :] = n_scr[2 * d + p:2 * d + p + 1, j * DK:(j + 1) * DK]
                mnew_ref[d:d + 1, :] = to_row(m_scr[H_B * d:H_B * (d + 1), 0:1])

    return state_group, out_group, finish


def _mlstm_scratch(T, C2, grid_w):
    n_pairs = C2 // 2 // LANES
    n_chunks = T // CHUNK
    pad_rows = 2 * (grid_w + SUBLANES) if T // grid_w > 1 else 2 * SUBLANES
    return [
        pltpu.VMEM((T + pad_rows, C2), F32),
        pltpu.VMEM((T, C2), F32),
        pltpu.VMEM((T, SMALL_W), F32),
        pltpu.VMEM((2, n_pairs, LANES, LANES), F32),
        pltpu.VMEM((SUBLANES, LANES), F32),
        pltpu.VMEM((SUBLANES, LANES), F32),
        pltpu.VMEM((2, n_chunks, n_pairs, LANES, LANES), BF16),
        pltpu.VMEM((2, n_chunks, SUBLANES, LANES), F32),
        pltpu.VMEM((2, n_chunks, SUBLANES, LANES), F32),
        pltpu.VMEM((2, n_chunks, SUBLANES, LANES), F32),
        pltpu.VMEM((2, n_chunks, SUBLANES, LANES), F32),
    ]


N_GLA_SCRATCH = 5
N_MLSTM_SCRATCH = 11


def _scan_kernel(*refs, cols, layer, has_state, write_state, n_cast, ride_ada, grid_w, unroll):
    refs = list(refs)
    z_refs = refs[:2]
    del refs[:2]
    s0_ref = c0_ref = n0_ref = m0_ref = None
    if has_state:
        s0_ref, c0_ref, n0_ref, m0_ref = refs[:4]
        del refs[:4]
    wa_ref, bal_ref, gwa_ref, cw_ref, bmg_ref, gwb_ref = refs[:6]
    del refs[:6]
    cast_in = refs[:n_cast]
    del refs[:n_cast]
    if ride_ada:
        ada_in = refs[:4]
        del refs[:4]
    outa_ref, outb_ref = refs[:2]
    del refs[:2]
    snew_ref = cnew_ref = nnew_ref = mnew_ref = None
    if write_state:
        snew_ref, cnew_ref, nnew_ref, mnew_ref = refs[:4]
        del refs[:4]
    cast_out = refs[:n_cast]
    del refs[:n_cast]
    if ride_ada:
        ada_out = refs.pop(0)
    wal_scr, bm_scr = refs[:2]
    del refs[:2]
    gla_scr = refs[:N_GLA_SCRATCH]
    mlstm_scr = refs[N_GLA_SCRATCH:]

    for src, dst in zip(cast_in, cast_out):
        dst[...] = src[...].astype(BF16)
    if ride_ada:
        _ada_tile(*ada_in, ada_out)

    R, HK = wa_ref.shape[1], wa_ref.shape[2]
    wal_scr[...] = jnp.zeros(wal_scr.shape, BF16)
    for d in range(2):
        wal_scr[d * R:(d + 1) * R, d * HK:(d + 1) * HK] = wa_ref[d].astype(BF16)
    lane = lax.broadcasted_iota(jnp.int32, (1, LANES), 1)
    bm = jnp.zeros((1, LANES), F32)
    for g in range(bmg_ref.shape[1]):
        for h in range(H_B):
            bm = jnp.where(lane == GATE_LANE0 + H_B * g + h, bmg_ref[layer, g, h], bm)
    bm_scr[0:1, :] = bm

    def view(name):
        a, c0, w = cols[name]
        return z_refs[a].at[:, pl.ds(c0, w)]

    sm_ref = view("small")
    n_chunks = z_refs[0].shape[0] // CHUNK
    gla = _gla_body(view("qa"), view("ka"), view("va"), view("ga"), sm_ref, s0_ref, wal_scr, bal_ref, gwa_ref,
                    outa_ref, snew_ref, *gla_scr)
    mlstm = _mlstm_body(view("qkb"), view("vb"), view("ob"), sm_ref, c0_ref, n0_ref, m0_ref, cw_ref,
                        bm_scr.at[0:1, :], gwb_ref, outb_ref, cnew_ref, nnew_ref, mnew_ref, *mlstm_scr,
                        grid_w=grid_w)
    gla_state, gla_out, gla_finish, decay_span = gla
    mlstm_state, mlstm_out, mlstm_finish = mlstm

    def passes(gla_out_fn):
        _chunk_loop(n_chunks, unroll, lambda ns: [fn([n], (d,)) for n in ns for d in range(2)
                                                  for fn in (mlstm_state, gla_state)])
        _chunk_loop(n_chunks, unroll, lambda ns: [fn([n]) for n in ns for fn in (mlstm_out, gla_out_fn)])

    wide_decay = decay_span > GLA_FACTORED_DECAY_MAX

    @pl.when(jnp.logical_not(wide_decay))
    def _():
        passes(gla_out)

    @pl.when(wide_decay)
    def _():
        passes(functools.partial(gla_out, exact_decay=True))

    gla_finish()
    mlstm_finish()


def _scan_call(z2d, row0, B, T, states, lw, layer, *, grid_w, write_state, casts=(), ada=None):
    assert row0 % T == 0 and all(z.shape[0] % T == 0 for z in z2d)
    z3 = [z.reshape(z.shape[0] // T, T, z.shape[1]) for z in z2d]
    blk0 = row0 // T
    HK = lw["w_alpha2"].shape[-1]
    DA = lw["gnorm_a_w"].shape[0]
    C2 = lw["conv_w"].shape[-1]
    DB = lw["gnorm_b_w"].shape[0]
    DK_A, DK_B = HK // H_A, C2 // 2 // H_B
    pa, pb = HK // LANES, C2 // 2 // LANES
    n_chunks = T // CHUNK
    has_state = states is not None
    widths = ((("qa", HK), ("ka", HK), ("qkb", C2), ("small", SMALL_W)),
              (("va", DA), ("ga", DA), ("vb", DB), ("ob", DB)))
    cols = {}
    for a, groups in enumerate(widths):
        c0 = 0
        for name, w in groups:
            cols[name] = (a, c0, w)
            c0 += w
        assert c0 == z3[a].shape[2]
    cast_in_specs, cast_out_specs, cast_out_shape, cast_args = _cast_specs(casts, B)
    kern = functools.partial(_scan_kernel, cols=cols, layer=layer, has_state=has_state, write_state=write_state,
                             n_cast=len(casts), ride_ada=ada is not None, grid_w=grid_w,
                             unroll=min(n_chunks, SCAN_UNROLL))

    def per_batch(shape):
        nd = len(shape)
        return pl.BlockSpec((None,) + tuple(shape), lambda b: (b,) + (0,) * nd)

    def per_batch_layer(shape):
        nd = len(shape)
        return pl.BlockSpec((None, None) + tuple(shape), lambda b: (b, layer) + (0,) * nd)

    def of_layer(a):
        return pl.BlockSpec((None,) + a.shape[1:], lambda b: (layer,) + (0,) * (a.ndim - 1))

    def whole(a):
        return pl.BlockSpec(a.shape, lambda b: (0,) * a.ndim)

    state_shapes = ((2, pa, LANES, LANES), (2, pb, LANES, LANES), (2, H_B, DK_B), (2, H_B))
    in_specs = [pl.BlockSpec((None, T, z.shape[2]), lambda b: (b + blk0, 0, 0)) for z in z3]
    args = list(z3)
    if has_state:
        s_gla, s_c, s_n, s_m = states
        depth = s_gla.shape[1]
        args += [s_gla.reshape((B, depth) + state_shapes[0]), s_c.reshape((B, depth) + state_shapes[1]), s_n, s_m]
        in_specs += [per_batch_layer(s) for s in state_shapes]
    args += [lw["w_alpha2"], lw["b_alpha"], lw["gnorm_a_w"].reshape(1, DA), lw["conv_w"], lw["b_mgate"],
             lw["gnorm_b_w"].reshape(1, DB)]
    in_specs += [of_layer(lw["w_alpha2"]), of_layer(lw["b_alpha"]), pl.BlockSpec((1, DA), lambda b: (0, 0)),
                 whole(lw["conv_w"]), pl.BlockSpec(memory_space=pltpu.SMEM), pl.BlockSpec((1, DB), lambda b: (0, 0))]
    args += cast_args
    in_specs += cast_in_specs
    out_specs = [per_batch((T, DA)), per_batch((T, DB))]
    out_shape = [jax.ShapeDtypeStruct((B, T, DA), BF16), jax.ShapeDtypeStruct((B, T, DB), BF16)]
    if write_state:
        out_specs += [per_batch(s) for s in state_shapes]
        out_shape += [jax.ShapeDtypeStruct((B,) + s, F32) for s in state_shapes]
    out_specs += cast_out_specs
    out_shape += cast_out_shape
    if ada is not None:
        cc, c, w_ada, b_ada, col0 = ada
        n_rest = w_ada.shape[1] - col0
        wcol = n_rest // B
        assert n_rest % B == 0 and wcol % LANES == 0 and col0 % wcol == 0
        args += [cc, c, w_ada, b_ada]
        in_specs += [whole(cc), whole(c),
                     pl.BlockSpec((w_ada.shape[0], wcol), lambda b: (0, col0 // wcol + b)),
                     pl.BlockSpec((1, wcol), lambda b: (0, col0 // wcol + b))]
        out_specs.append(pl.BlockSpec((COND_ROWS, wcol), lambda b: (0, b)))
        out_shape.append(jax.ShapeDtypeStruct((COND_ROWS, n_rest), F32))
    scratch = ([pltpu.VMEM((SMALL_W, 2 * HK), BF16), pltpu.VMEM((SUBLANES, LANES), F32)]
               + _gla_scratch(T, HK) + _mlstm_scratch(T, C2, grid_w))
    assert len(scratch) == 2 + N_GLA_SCRATCH + N_MLSTM_SCRATCH
    return pl.pallas_call(
        kern,
        grid=(B,),
        in_specs=in_specs,
        out_specs=out_specs,
        out_shape=out_shape,
        scratch_shapes=scratch,
        compiler_params=pltpu.CompilerParams(dimension_semantics=("arbitrary",),
                                             vmem_limit_bytes=VMEM_LIMIT),
        name="mixer_scans",
    )(*args)


def _outff_kernel(xc_ref, xl_ref, ac_ref, al_ref, bc_ref, bl_ref, mod_ref, n2_ref, fn_ref, wo_hbm, w1_hbm, w2_hbm,
                  yc_ref, yl_ref, wo_ref, w1_ref, w2_ref, sem, *, n_ctx, tiles_per_req, ff_chunk, final_norm):
    D = xc_ref.shape[1]
    DA = ac_ref.shape[1]
    n_ff = w1_ref.shape[1] // ff_chunk
    is_ctx, row = _tile_group(n_ctx, tiles_per_req)

    def weight_copies():
        cs = [pltpu.make_async_copy(wo_hbm, wo_ref, sem.at[0])]
        for k in range(n_ff):
            c = pl.ds(k * ff_chunk, ff_chunk)
            cs.append(pltpu.make_async_copy(w1_hbm.at[:, c], w1_ref.at[:, c], sem.at[1 + 2 * k]))
            cs.append(pltpu.make_async_copy(w2_hbm.at[c, :], w2_ref.at[c, :], sem.at[2 + 2 * k]))
        return cs

    def mod(k):
        return mod_ref[pl.ds(row, 1), (k - MOD_SPLIT) * D:(k - MOD_SPLIT + 1) * D]

    def tile(x_ref, a_ref, b_ref, y_ref, fetch_weights=False):
        copies = weight_copies() if fetch_weights else None
        if fetch_weights:
            for cp in copies:
                cp.start()
            copies[0].wait()
        y = _dot(a_ref[...], wo_ref[0:DA, :]) + _dot(b_ref[...], wo_ref[DA:, :])
        x1 = x_ref[...] + mod(2) * y
        h2 = (_rms(x1, n2_ref[...]) * (1.0 + mod(4)) + mod(3)).astype(BF16)
        acc = jnp.zeros(x1.shape, F32)
        for k in range(n_ff):
            c0 = k * ff_chunk
            if fetch_weights:
                copies[1 + 2 * k].wait()
                copies[2 + 2 * k].wait()
            u = jnp.maximum(_dot(h2, w1_ref[:, c0:c0 + ff_chunk]), 0.0)
            acc = acc + _dot((u * u).astype(BF16), w2_ref[c0:c0 + ff_chunk, :])
        x2 = x1 + mod(5) * acc
        y_ref[...] = _rms(x2, fn_ref[...]) if final_norm else x2

    first = pl.program_id(0) == 0

    @pl.when(first)
    def _():
        tile(xc_ref, ac_ref, bc_ref, yc_ref, fetch_weights=True)

    @pl.when(jnp.logical_and(is_ctx, jnp.logical_not(first)))
    def _():
        tile(xc_ref, ac_ref, bc_ref, yc_ref)

    @pl.when(jnp.logical_not(is_ctx))
    def _():
        tile(xl_ref, al_ref, bl_ref, yl_ref)


def _outff_call(xc2d, xl2d, ac, al, bc, bl, mod, norm2_w, final_w, wo, w1, w2, *, tm, tiles_per_req, final_norm):
    (Mc, D), Ml = xc2d.shape, xl2d.shape[0]
    n_ctx = Mc // tm
    DA = ac.shape[1]
    DFF = w1.shape[1]
    kern = functools.partial(_outff_kernel, n_ctx=n_ctx, tiles_per_req=tiles_per_req, ff_chunk=FF_CHUNK,
                             final_norm=final_norm)
    assert n_ctx >= 1 and DFF % FF_CHUNK == 0 and all(w.dtype == BF16 for w in (wo, w1, w2))
    in_hbm = pl.BlockSpec(memory_space=pl.ANY)
    ctx, lat = _ctx_tile(n_ctx), _lat_tile(n_ctx)
    return pl.pallas_call(
        kern,
        grid=((Mc + Ml) // tm,),
        in_specs=[
            pl.BlockSpec((tm, D), ctx), pl.BlockSpec((tm, D), lat),
            pl.BlockSpec((tm, DA), ctx), pl.BlockSpec((tm, DA), lat),
            pl.BlockSpec((tm, D - DA), ctx), pl.BlockSpec((tm, D - DA), lat),
            pl.BlockSpec(mod.shape, lambda i: (0, 0)),
            pl.BlockSpec((1, D), lambda i: (0, 0)),
            pl.BlockSpec((1, D), lambda i: (0, 0)),
            in_hbm, in_hbm, in_hbm,
        ],
        out_specs=[pl.BlockSpec((tm, D), ctx), pl.BlockSpec((tm, D), lat)],
        out_shape=[jax.ShapeDtypeStruct((Mc, D), F32), jax.ShapeDtypeStruct((Ml, D), F32)],
        scratch_shapes=[pltpu.VMEM(wo.shape, BF16), pltpu.VMEM(w1.shape, BF16), pltpu.VMEM(w2.shape, BF16),
                        pltpu.SemaphoreType.DMA((1 + 2 * (DFF // FF_CHUNK),))],
        compiler_params=pltpu.CompilerParams(dimension_semantics=("arbitrary",),
                                             vmem_limit_bytes=VMEM_LIMIT),
        name="outproj_mlp",
    )(xc2d, xl2d, ac, al, bc, bl, mod, norm2_w.reshape(1, D), final_w.reshape(1, D), wo, w1, w2)


def _layer(xc, xl, cond, ada_w, cached, lw, layer, ffw, final_w, final_norm):
    (Bc, Tc, D), (Bl, Tl, _) = xc.shape, xl.shape
    tm = TOKEN_TILE
    assert (Bc * Tc) % tm == 0 and Tl % tm == 0 and (Bc * Tc) % Tl == 0
    xc2d, xl2d = xc.reshape(Bc * Tc, D), xl.reshape(Bl * Tl, D)
    z = _inproj_call(xc2d, xl2d, *cond, *ada_w, lw["norm1_w"], lw["w_in_t"], tm=tm, tiles_per_req=Tl // tm,
                     f32_rows=lw["f32_rows"], small_rows=lw["small_rows"], bf16_rows=lw["bf16_rows"])
    res_c = _scan_call(z, 0, Bc, Tc, None, lw, layer, grid_w=Tc, write_state=True,
                       casts=((ffw[0], 0), (ffw[1], 0), (ffw[2], 0)), ada=(*cond, *ada_w, MOD_SPLIT * D))
    res_l = _scan_call(z, Bc * Tc, Bl, Tl, cached, lw, layer, grid_w=GRID_W, write_state=False)
    wo_b, w1_b, w2_b, mod_out = res_c[-4:]
    yc, yl = _outff_call(xc2d, xl2d, res_c[0].reshape(Bc * Tc, -1), res_l[0].reshape(Bl * Tl, -1),
                         res_c[1].reshape(Bc * Tc, -1), res_l[1].reshape(Bl * Tl, -1), mod_out, lw["norm2_w"],
                         final_w, wo_b, w1_b, w2_b, tm=tm, tiles_per_req=Tl // tm, final_norm=final_norm)
    return yc.reshape(Bc, Tc, D), yl.reshape(Bl, Tl, D), tuple(res_c[2:6])


def _layer_weights(l, norm1_w, norm2_w, w_in, w_alpha2, b_alpha, b_mgate, conv_w, gnorm_a_w, gnorm_b_w):
    hk_a = w_alpha2.shape[-1]
    d_a = gnorm_a_w.shape[-1]
    d_b = gnorm_b_w.shape[-1]
    hk_b = conv_w.shape[-1] // 2
    sizes = (hk_a, hk_a, d_a, d_a, 2 * R_ALPHA, hk_b, hk_b, d_b, d_b, 4 * H_B)
    assert w_alpha2.shape[2] == R_ALPHA and b_mgate.shape[1] * b_mgate.shape[2] == 4 * H_B
    offs = [0]
    for s in sizes:
        offs.append(offs[-1] + s)
    f32_rows = ((offs[0], offs[2] - offs[0]), (offs[5], offs[7] - offs[5]))
    small_rows = ((offs[4], offs[5] - offs[4]), (offs[9], offs[10] - offs[9]))
    bf16_rows = ((offs[2], offs[4] - offs[2]), (offs[7], offs[9] - offs[7]))
    assert all(n % LANES == 0 and r % BF16_ROWS == 0 for r, n in f32_rows + bf16_rows)
    return dict(
        norm1_w=norm1_w[l], norm2_w=norm2_w[l], w_in_t=jnp.swapaxes(w_in[l], 0, 1),
        f32_rows=f32_rows, small_rows=small_rows, bf16_rows=bf16_rows,
        w_alpha2=w_alpha2, b_alpha=b_alpha, b_mgate=b_mgate, conv_w=conv_w[l],
        gnorm_a_w=gnorm_a_w[l], gnorm_b_w=gnorm_b_w[l],
    )


def kernel(x_prompt, x_sample, c, state_gla, state_mlstm_C, state_mlstm_n, state_mlstm_m, c_ctx, w_ada, b_ada, norm1_w, norm2_w, w_in, w_alpha2, b_alpha, b_mgate, conv_w, gnorm_a_w, gnorm_b_w, w_out, w_ff1, w_ff2, final_norm_w):
    depth = w_in.shape[0]
    D = x_prompt.shape[-1]
    Bp, Tp, _ = x_prompt.shape
    Bs = x_sample.shape[0]
    assert 1 + Bs <= COND_ROWS
    cond = (c_ctx.reshape(1, D), c)
    cached = (state_gla, state_mlstm_C, state_mlstm_n, state_mlstm_m)
    xp, xs = x_prompt, x_sample
    s_gla, s_c, s_n, s_m = [], [], [], []
    for l in range(depth):
        lw = _layer_weights(l, norm1_w, norm2_w, w_in, w_alpha2, b_alpha, b_mgate, conv_w,
                            gnorm_a_w, gnorm_b_w)
        xp, xs, ctx = _layer(xp, xs, cond, (w_ada[l], b_ada[l].reshape(1, -1)), cached, lw, l,
                             (w_out[l], w_ff1[l], w_ff2[l]), final_norm_w, l == depth - 1)
        s_gla.append(ctx[0].reshape(Bp, 2, H_A, -1, ctx[0].shape[-1]))
        s_c.append(ctx[1].reshape(Bp, 2, H_B, -1, ctx[1].shape[-1]))
        s_n.append(ctx[2])
        s_m.append(ctx[3])
    dt = x_prompt.dtype
    return (xp, xs, jnp.stack(s_gla, axis=1).astype(dt), jnp.stack(s_c, axis=1).astype(dt),
            jnp.stack(s_n, axis=1).astype(dt), jnp.stack(s_m, axis=1).astype(dt))
```

```python
import functools
import math

import jax
import jax.numpy as jnp
from jax import lax
from jax.experimental import pallas as pl
from jax.experimental.pallas import tpu as pltpu

F32 = jnp.float32
BF16 = jnp.bfloat16

GRID_W = 64
H_A = 4
H_B = 4
R_ALPHA = 16
TAU_GLA = 16.0
CHUNK = 64
EPS = 1e-6
LANES = 128
SUBLANES = 8
BF16_ROWS = 16
COND_ROWS = SUBLANES
SMALL_W = LANES
GATE_LANE0 = 2 * R_ALPHA
VMEM_LIMIT = 56 * 1024 * 1024
SCAN_UNROLL = 4
PIPELINE_STARTS = 8
TOKEN_TILE = 512
FF_CHUNK = 512
GLA_FACTORED_DECAY_MAX = 60.0
MOD_SPLIT = 2


def _sigmoid(x):
    return 1.0 / (1.0 + jnp.exp(-x))


def _silu(x):
    return x * _sigmoid(x)


def _log_sigmoid(x):
    return jnp.minimum(x, 0.0) - jnp.log(1.0 + jnp.exp(-jnp.abs(x)))


def _dot(a, b):
    return jnp.dot(a, b, preferred_element_type=F32)


def _dot_nt(a, b):
    return lax.dot_general(a, b, (((1,), (1,)), ((), ())), preferred_element_type=F32)


def _rms(x, w):
    return x * lax.rsqrt(jnp.mean(x * x, axis=-1, keepdims=True) + EPS) * w


def _tri_sum(tri, x, terms=3):
    acc, rest = None, x
    for t in range(terms):
        part = rest.astype(BF16)
        prod = _dot(tri, part)
        acc = prod if acc is None else acc + prod
        if t + 1 < terms:
            rest = rest - part.astype(F32)
    return acc


def _chunk_masks(L):
    row = lax.broadcasted_iota(jnp.int32, (L, L), 0)
    col = lax.broadcasted_iota(jnp.int32, (L, L), 1)
    lower = row >= col
    upper = row <= col
    return lower, upper


def _ada_tile(cc_ref, c_ref, w_ref, b_ref, o_ref):
    D = cc_ref.shape[1]
    sub = lax.broadcasted_iota(jnp.int32, (COND_ROWS, D), 0)
    cond = jnp.where(sub == 0, cc_ref[...], 0.0)
    for r in range(c_ref.shape[0]):
        cond = jnp.where(sub == 1 + r, c_ref[r:r + 1, :], cond)
    o_ref[...] = _dot(_silu(cond).astype(BF16), w_ref[...].astype(BF16)) + b_ref[...]


def _tile_group(n_ctx, tiles_per_req, tile=None):
    i = pl.program_id(0) if tile is None else tile
    is_ctx = i < n_ctx
    row = jnp.where(is_ctx, 0, 1 + jnp.maximum(i - n_ctx, 0) // tiles_per_req)
    return is_ctx, row


def _ctx_tile(n_ctx, ahead=0):
    return lambda i: (jnp.minimum(i + ahead, n_ctx - 1), 0)


def _lat_tile(n_ctx, n_lat, ahead=0):
    return lambda i: (jnp.clip(i + ahead - n_ctx, 0, n_lat - 1), 0)


def _inproj_kernel(x0_ref, xc_ref, xl_ref, cc_ref, c_ref, wa_ref, ba_ref, nw_ref, wt_ref, zf_ref, zh_ref,
                   wb_scr, mod_ref, h0_scr, h1_scr, *, n_ctx, n_tiles, tiles_per_req, f32_rows, small_rows, bf16_rows):
    D = xc_ref.shape[1]
    n_f32 = zf_ref.shape[1]
    i = pl.program_id(0)

    def normed(x_ref, row):
        sh1 = mod_ref[pl.ds(row, 1), 0:D]
        sc1 = mod_ref[pl.ds(row, 1), D:2 * D]
        return (_rms(x_ref[...], nw_ref[...]) * (1.0 + sc1) + sh1).astype(BF16)

    @pl.when(i == 0)
    def _():
        _ada_tile(cc_ref, c_ref, wa_ref, ba_ref, mod_ref)

        def wide(rows, col):
            for r0, n in rows:
                for k in range(n // LANES):
                    blk = wt_ref[r0 + k * LANES:r0 + (k + 1) * LANES, :]
                    wb_scr[:, col:col + LANES] = blk.T.astype(BF16)
                    col += LANES
            return col

        col = wide(f32_rows, 0)
        parts = [wt_ref[r0:r0 + n, :] for r0, n in small_rows]
        n_small = sum(n for _, n in small_rows)
        parts.append(jnp.zeros((SMALL_W - n_small, D), F32))
        wb_scr[:, col:col + SMALL_W] = jnp.concatenate(parts, axis=0).T.astype(BF16)
        wide(bf16_rows, col + SMALL_W)
        h0_scr[...] = normed(x0_ref, 0)

    next_is_ctx, next_row = _tile_group(n_ctx, tiles_per_req, jnp.minimum(i + 1, n_tiles - 1))
    even = lax.rem(i, 2) == 0

    def tile(x_next_ref, h_scr, h_next_scr):
        h = h_scr[...]
        zf_ref[...] = _dot(h, wb_scr[:, 0:n_f32])
        zh_ref[...] = _dot(h, wb_scr[:, n_f32:]).astype(BF16)
        h_next_scr[...] = normed(x_next_ref, next_row)

    for x_next_ref, in_group in ((xc_ref, next_is_ctx), (xl_ref, jnp.logical_not(next_is_ctx))):
        for h_scr, h_next_scr, parity in ((h0_scr, h1_scr, even), (h1_scr, h0_scr, jnp.logical_not(even))):
            pl.when(jnp.logical_and(in_group, parity))(functools.partial(tile, x_next_ref, h_scr, h_next_scr))


def _inproj_call(xc2d, xl2d, cc, c, w_ada, b_ada, norm_w, w_in_t, *, tm, tiles_per_req, f32_rows, small_rows,
                 bf16_rows):
    (Mc, D), Ml = xc2d.shape, xl2d.shape[0]
    n_ctx = Mc // tm
    n_f32 = sum(n for _, n in f32_rows) + SMALL_W
    n_bf16 = sum(n for _, n in bf16_rows)
    n_out = n_f32 + n_bf16
    n_mod = MOD_SPLIT * D
    n_tiles = (Mc + Ml) // tm
    assert n_ctx >= 1
    kern = functools.partial(_inproj_kernel, n_ctx=n_ctx, n_tiles=n_tiles, tiles_per_req=tiles_per_req,
                             f32_rows=f32_rows, small_rows=small_rows, bf16_rows=bf16_rows)
    once = pl.Buffered(1)
    return pl.pallas_call(
        kern,
        grid=(n_tiles,),
        in_specs=[
            pl.BlockSpec((tm, D), lambda i: (0, 0), pipeline_mode=once),
            pl.BlockSpec((tm, D), _ctx_tile(n_ctx, ahead=1)),
            pl.BlockSpec((tm, D), _lat_tile(n_ctx, Ml // tm, ahead=1)),
            pl.BlockSpec(cc.shape, lambda i: (0, 0)),
            pl.BlockSpec(c.shape, lambda i: (0, 0)),
            pl.BlockSpec((D, n_mod), lambda i: (0, 0), pipeline_mode=once),
            pl.BlockSpec((1, n_mod), lambda i: (0, 0)),
            pl.BlockSpec((1, D), lambda i: (0, 0)),
            pl.BlockSpec(w_in_t.shape, lambda i: (0, 0), pipeline_mode=once),
        ],
        out_specs=[pl.BlockSpec((tm, n_f32), lambda i: (i, 0)), pl.BlockSpec((tm, n_bf16), lambda i: (i, 0))],
        out_shape=[jax.ShapeDtypeStruct((Mc + Ml, n_f32), F32), jax.ShapeDtypeStruct((Mc + Ml, n_bf16), BF16)],
        scratch_shapes=[pltpu.VMEM((D, n_out), BF16), pltpu.VMEM((COND_ROWS, n_mod), F32),
                        pltpu.VMEM((tm, D), BF16), pltpu.VMEM((tm, D), BF16)],
        compiler_params=pltpu.CompilerParams(dimension_semantics=("arbitrary",),
                                             vmem_limit_bytes=VMEM_LIMIT),
        name="norm_inproj",
    )(xc2d, xc2d, xl2d, cc, c, w_ada, b_ada, norm_w.reshape(1, D), w_in_t)


def _chunk_loop(n_chunks, unroll, make_units):
    def step(ns):
        pending = list(make_units(ns))
        active = []
        while pending or active:
            for _ in range(min(PIPELINE_STARTS, len(pending))):
                active.append(pending.pop(0))
            alive = []
            for g in active:
                try:
                    next(g)
                    alive.append(g)
                except StopIteration:
                    pass
            active = alive

    if unroll >= n_chunks:
        step(list(range(n_chunks)))
        return

    def body(i, carry):
        step([i * unroll + u for u in range(unroll)])
        return carry

    lax.fori_loop(0, n_chunks // unroll, body, 0)


def _chunk_rows(n):
    if isinstance(n, int):
        return pl.ds(n * CHUNK, CHUNK)
    return pl.ds(pl.multiple_of(n * CHUNK, CHUNK), CHUNK)


def _cast_specs(casts, n_steps):
    in_specs, out_specs, out_shape, args = [], [], [], []
    for w, axis in casts:
        blk = list(w.shape)
        assert blk[axis] % n_steps == 0
        blk[axis] //= n_steps
        assert blk[0] % BF16_ROWS == 0 and blk[1] % LANES == 0
        idx = (lambda b: (b, 0)) if axis == 0 else (lambda b: (0, b))
        in_specs.append(pl.BlockSpec(tuple(blk), idx))
        out_specs.append(pl.BlockSpec(tuple(blk), idx))
        out_shape.append(jax.ShapeDtypeStruct(w.shape, BF16))
        args.append(w)
    return in_specs, out_specs, out_shape, args


def _gla_body(q_ref, k_ref, v_ref, g_ref, sm_ref, s0_ref, wal_ref, bal_ref, gw_ref, out_ref, snew_ref,
              st_scr, sall_scr, qh_scr, qs_scr, kh_scr):
    has_state = s0_ref is not None
    write_state = snew_ref is not None
    T = q_ref.shape[0]
    L = CHUNK
    N = T // L
    HK = q_ref.shape[1]
    DK = HK // H_A
    DV = v_ref.shape[1] // H_A
    scale = DK ** -0.5
    n_pairs = HK // LANES

    lower, upper = _chunk_masks(L)
    tri = (lower.astype(BF16), upper.astype(BF16))
    tmask = (lower, upper)
    lane = lax.broadcasted_iota(jnp.int32, (1, LANES), 1)
    head_mask = (lane < DK, lane >= DK)

    for d in range(2):
        for p in range(n_pairs):
            if has_state:
                st_scr[d, p] = s0_ref[d, p].T
            else:
                st_scr[d, p] = jnp.zeros((LANES, LANES), F32)

    def decay_pre(d, r):
        return _dot(sm_ref[r, :].astype(BF16), wal_ref[:, d * HK:(d + 1) * HK]) + bal_ref[d:d + 1, :]

    neg_pre = jnp.maximum(-(_dot(sm_ref[...].astype(BF16), wal_ref[...])
                            + jnp.concatenate([bal_ref[0:1, :], bal_ref[1:2, :]], axis=1)), 0.0)
    chunk_sums = jnp.sum(neg_pre.reshape(N, L, 2 * HK), axis=1)
    decay_span = (jnp.max(chunk_sums) + L * math.log(2.0)) * (1.0 / TAU_GLA)

    def state_group(ns, dirs=(0, 1)):
        units = [(d, n if d == 0 else N - 1 - n) for n in ns for d in dirs]
        rows = [_chunk_rows(n) for _, n in units]
        vt_all = [[jnp.concatenate([v_ref[r, (2 * p + j) * DV:(2 * p + j + 1) * DV] for j in range(2)],
                                   axis=0).astype(F32).T.astype(BF16) for p in range(n_pairs)] for r in rows]
        yield
        pre = [decay_pre(d, r) for (d, _), r in zip(units, rows)]
        yield
        g = [_log_sigmoid(x) * (1.0 / TAU_GLA) for x in pre]
        yield
        b = [_tri_sum(tri[d], gi, terms=2) for (d, _), gi in zip(units, g)]
        yield
        ks_all, dec_all = [], []
        for (d, _), r, bi in zip(units, rows, b):
            bend = bi[L - 1:L, :] if d == 0 else bi[0:1, :]
            q = q_ref[r, :] * scale
            ks = (k_ref[r, :] * jnp.exp(bend - bi)).astype(BF16)
            qs = q * jnp.exp(bi)
            qh_scr[d, r, :] = (qs * jnp.exp(-bend)).astype(BF16)
            qs_scr[d, r, :] = qs.astype(BF16)
            kh_scr[d, r, :] = ks
            ks_all.append(ks)
            dec_all.append(jnp.exp(bend))
        yield
        upd_all = []
        for vt_u, ks in zip(vt_all, ks_all):
            upd_u = []
            for p in range(n_pairs):
                kp = ks[:, p * LANES:(p + 1) * LANES]
                kk = jnp.concatenate([jnp.where(head_mask[j], kp, jnp.zeros_like(kp)) for j in range(2)], axis=0)
                upd_u.append(_dot(vt_u[p], kk))
            upd_all.append(upd_u)
        yield
        st = {d: [st_scr[d, p] for p in range(n_pairs)] for d in dirs}
        for (d, n), dec, upd in zip(units, dec_all, upd_all):
            for p in range(n_pairs):
                sall_scr[d, n, p] = st[d][p].astype(BF16)
                st[d][p] = st[d][p] * dec[:, p * LANES:(p + 1) * LANES] + upd[p]
        for d in dirs:
            for p in range(n_pairs):
                st_scr[d, p] = st[d][p]

    def stack_heads(x):
        return jnp.concatenate([jnp.where(head_mask[j], x, jnp.zeros_like(x)) for j in range(2)], axis=0)

    tok = lax.broadcasted_iota(jnp.int32, (L, 1), 0)
    row_t = lax.broadcasted_iota(jnp.int32, (2 * L, L), 0) & (L - 1)
    col_s = lax.broadcasted_iota(jnp.int32, (2 * L, L), 1)

    def exact_scores(d, r, p):
        ls = slice(p * LANES, (p + 1) * LANES)
        b = _tri_sum(tri[d], _log_sigmoid(decay_pre(d, r)[:, ls]) * (1.0 / TAU_GLA))
        q = q_ref[r, ls] * scale
        k = k_ref[r, ls]
        acc = jnp.where(row_t == col_s, _dot_nt(stack_heads(q).astype(BF16), k.astype(BF16)), 0.0)
        src = lax.broadcasted_iota(jnp.int32, (L, L), 1)
        h = L // 2
        while h >= 1:
            first = tok & ~(2 * h - 1)
            edge = first + (h - 1 if d == 0 else h)
            b_edge = _tri_sum((src == edge).astype(BF16), b)
            upper = (tok & (2 * h - 1)) >= h
            later, earlier = (upper, ~upper) if d == 0 else (~upper, upper)
            qt = jnp.where(later, q * jnp.exp(b - b_edge), 0.0)
            kt = jnp.where(earlier, k * jnp.exp(b_edge - b), 0.0)
            sc = _dot_nt(stack_heads(qt).astype(BF16), kt.astype(BF16))
            acc = acc + jnp.where((row_t & ~(2 * h - 1)) == (col_s & ~(2 * h - 1)), sc, 0.0)
            h //= 2
        return acc

    def out_group(ns, exact_decay=False):
        pairs = [(d, ni, p) for ni in range(len(ns)) for d in range(2) for p in range(n_pairs)]
        scores, inter = [], []
        for d, ni, p in pairs:
            r = _chunk_rows(ns[ni])
            ls = slice(p * LANES, (p + 1) * LANES)
            if exact_decay:
                scores.append(exact_scores(d, r, p))
            else:
                scores.append(_dot_nt(stack_heads(qh_scr[d, r, ls]), kh_scr[d, r, ls]))
            inter.append(_dot_nt(stack_heads(qs_scr[d, r, ls]), sall_scr[d, ns[ni], p]))
        yield
        probs = [[jnp.where(tmask[d], sc[j * L:(j + 1) * L, :], 0.0).astype(BF16) for j in range(2)]
                 for (d, _, _), sc in zip(pairs, scores)]
        yield
        outs = {}
        for (d, ni, p), pr, it in zip(pairs, probs, inter):
            r = _chunk_rows(ns[ni])
            for j in range(2):
                vs = slice((2 * p + j) * DV, (2 * p + j + 1) * DV)
                outs[(d, ni, 2 * p + j)] = _dot(pr[j], v_ref[r, vs]) + it[j * L:(j + 1) * L, :]
        yield
        for ni, n in enumerate(ns):
            r = _chunk_rows(n)
            for h in range(H_A):
                vs = slice(h * DV, (h + 1) * DV)
                o = outs[(0, ni, h)] + outs[(1, ni, h)]
                out_ref[r, vs] = (_rms(o, gw_ref[:, vs]) * _silu(g_ref[r, vs].astype(F32))).astype(out_ref.dtype)

    def finish():
        if write_state:
            for d in range(2):
                for p in range(n_pairs):
                    snew_ref[d, p] = st_scr[d, p].T

    return state_group, out_group, finish, decay_span


def _gla_scratch(T, HK):
    n_pairs = HK // LANES
    n_chunks = T // CHUNK
    return [
        pltpu.VMEM((2, n_pairs, LANES, LANES), F32),
        pltpu.VMEM((2, n_chunks, n_pairs, LANES, LANES), BF16),
        pltpu.VMEM((2, T, HK), BF16),
        pltpu.VMEM((2, T, HK), BF16),
        pltpu.VMEM((2, T, HK), BF16),
    ]


def _mlstm_body(qk_ref, v_ref, og_ref, sm_ref, c0_ref, n0_ref, m0_ref, cw_ref, bm_ref, gw_ref,
                out_ref, cnew_ref, nnew_ref, mnew_ref,
                pad_scr, qk_scr, y_scr, c_scr, n_scr, m_scr, call_scr, nall_scr, mall_scr, g_scr, f_scr,
                *, grid_w):
    has_state = c0_ref is not None
    write_state = cnew_ref is not None
    T = qk_ref.shape[0]
    L = CHUNK
    N = T // L
    C2 = qk_ref.shape[1]
    HK = C2 // 2
    DK = HK // H_B
    DV = v_ref.shape[1] // H_B
    scale = DK ** -0.5
    n_pairs = HK // LANES
    P = pad_scr.shape[0] - T
    P0 = P // 2
    rows_img = T // grid_w

    lower, upper = _chunk_masks(L)
    tri = (lower.astype(BF16), upper.astype(BF16))
    tmask = (lower, upper)
    lane = lax.broadcasted_iota(jnp.int32, (1, LANES), 1)
    head_mask = (lane < DK, lane >= DK)
    lane_in = lane & (L - 1)

    def lane_cummax(x, d):
        k = 1
        while k < L:
            if d == 0:
                x = jnp.maximum(x, jnp.where(lane_in >= k, pltpu.roll(x, k, axis=1), -jnp.inf))
            else:
                x = jnp.maximum(x, jnp.where(lane_in < L - k, pltpu.roll(x, LANES - k, axis=1), -jnp.inf))
            k *= 2
        return x

    for d in range(2):
        for p in range(n_pairs):
            if has_state:
                c_scr[d, p] = c0_ref[d, p]
                n_scr[2 * d + p:2 * d + p + 1, :] = jnp.concatenate(
                    [n0_ref[d, 2 * p + j:2 * p + j + 1, :] for j in range(2)], axis=1)
            else:
                c_scr[d, p] = jnp.zeros((LANES, LANES), F32)
                n_scr[2 * d + p:2 * d + p + 1, :] = jnp.zeros((1, LANES), F32)
    eye_h = (lax.broadcasted_iota(jnp.int32, (H_B, H_B), 0) == lax.broadcasted_iota(jnp.int32, (H_B, H_B), 1))

    def to_col(row):
        return jnp.sum(jnp.where(eye_h, row, 0.0), axis=1, keepdims=True)

    def to_row(col):
        return jnp.sum(jnp.where(eye_h, col, 0.0), axis=0, keepdims=True)

    for d in range(2):
        if has_state:
            m_scr[H_B * d:H_B * (d + 1), 0:1] = to_col(m0_ref[d:d + 1, :])
        else:
            m_scr[H_B * d:H_B * (d + 1), 0:1] = jnp.zeros((H_B, 1), F32)

    pad_scr[0:P0, :] = jnp.zeros((P0, C2), F32)
    pad_scr[P0 + T:P + T, :] = jnp.zeros((P - P0, C2), F32)

    def copy_in(i, carry):
        r0 = pl.multiple_of(i * L, L)
        pad_scr[pl.ds(P0 + r0, L), :] = qk_ref[pl.ds(r0, L), :]
        return carry

    lax.fori_loop(0, N, copy_in, 0)

    lane_c = lax.broadcasted_iota(jnp.int32, (1, C2), 1)
    qscale = jnp.where(lane_c < HK, scale, 1.0).astype(F32)
    sub = lax.broadcasted_iota(jnp.int32, (L, 1), 0)
    img_rows = (0,) if rows_img == 1 else (-1, 0, 1)

    def conv_tile(i, carry):
        r0 = pl.multiple_of(i * L, L)
        col = lax.rem(r0, grid_w) + sub
        ok_left = col >= 1
        ok_right = col <= grid_w - 2
        sums = [None, None, None]
        for di in img_rows:
            blk = pad_scr[pl.ds(P0 + r0 + di * grid_w - SUBLANES, L + 2 * SUBLANES), :]
            for k in range(3):
                term = blk * cw_ref[di + 1, k:k + 1, :]
                sums[k] = term if sums[k] is None else sums[k] + term
        S = SUBLANES
        acc = (sums[1][S:S + L, :] + jnp.where(ok_left, sums[0][S - 1:S - 1 + L, :], 0.0)
               + jnp.where(ok_right, sums[2][S + 1:S + 1 + L, :], 0.0))
        qk_scr[pl.ds(r0, L), :] = _silu(acc) * qscale
        return carry

    lax.fori_loop(0, N, conv_tile, 0)

    gl = lane - GATE_LANE0
    is_f = ((gl >= H_B) & (gl < 2 * H_B)) | ((gl >= 3 * H_B) & (gl < 4 * H_B))

    def gate_tile(i, carry):
        rows = pl.ds(pl.multiple_of(i * L, L), L)
        x = sm_ref[rows, :] + bm_ref[...]
        y_scr[rows, :] = jnp.where(is_f, _log_sigmoid(x), x)
        return carry

    lax.fori_loop(0, N, gate_tile, 0)


    def state_group(ns, dirs=(0, 1)):
        units = [(d, n if d == 0 else N - 1 - n) for n in ns for d in dirs]
        rows = [_chunk_rows(n) for _, n in units]
        kt_all = [[qk_scr[r, HK + p * LANES:HK + (p + 1) * LANES].T for p in range(n_pairs)] for r in rows]
        yield
        xs = [y_scr[r, :] for r in rows]
        fsum = [_tri_sum(tri[d], x) for (d, _), x in zip(units, xs)]
        yield
        wk_all, f_end, c_end = [], [], []
        for (d, n), r, x, fs in zip(units, rows, xs, fsum):
            y = jnp.where(is_f, fs, x)
            li0 = GATE_LANE0 + 2 * H_B * d
            blk = jnp.concatenate([y, y], axis=0).T[li0:li0 + 2 * H_B, :]
            frow = pltpu.roll(blk, H_B, axis=0)
            grow = blk - frow
            g_scr[d, n] = grow
            f_scr[d, n] = frow
            e_col = L - 1 if d == 0 else 0
            f_end.append(frow[0:H_B, e_col:e_col + 1])
            ce8 = jnp.max(grow, axis=1, keepdims=True)
            c_end.append(ce8[0:H_B, :])
            wk_all.append(jnp.exp(grow[:, 0:L] - ce8))
        yield
        kv_all, ksum_all = [], []
        for r, wk8, kt_u in zip(rows, wk_all, kt_all):
            kv_u, ks_u = [], []
            wk8b = wk8.astype(BF16)
            for p in range(n_pairs):
                kpb = qk_scr[r, HK + p * LANES:HK + (p + 1) * LANES].astype(BF16)
                ks8 = _dot(wk8b, kpb)
                for j in range(2):
                    h = 2 * p + j
                    kwt = (kt_u[p][j * DK:(j + 1) * DK, :] * wk8[h:h + 1, :]).astype(BF16)
                    kv_u.append(_dot(kwt, v_ref[r, h * DV:(h + 1) * DV]))
                    ks_u.append(ks8[h:h + 1, :])
            kv_all.append(kv_u)
            ksum_all.append(ks_u)
        yield
        m_run = {d: m_scr[H_B * d:H_B * (d + 1), 0:1] for d in dirs}
        a_all, b_all = [], []
        for (d, n), fe, ce in zip(units, f_end, c_end):
            mall_scr[d, n, 0:H_B, 0:1] = m_run[d]
            mx = jnp.maximum(m_run[d], ce)
            a_all.append(jnp.exp(m_run[d] - mx))
            b_all.append(jnp.exp(ce - mx))
            m_run[d] = fe + mx
        for d in dirs:
            m_scr[H_B * d:H_B * (d + 1), 0:1] = m_run[d]
        yield
        c_run = {d: [[c_scr[d, p, j * DK:(j + 1) * DK, :] for j in range(2)] for p in range(n_pairs)] for d in dirs}
        n_run = {d: [n_scr[2 * d + p:2 * d + p + 1, :] for p in range(n_pairs)] for d in dirs}
        for (d, n), a4, b4, kv_u, ks_u in zip(units, a_all, b_all, kv_all, ksum_all):
            for p in range(n_pairs):
                nall_scr[d, n, p:p + 1, :] = n_run[d][p]
                a_s = [a4[2 * p + j:2 * p + j + 1, :] for j in range(2)]
                b_s = [b4[2 * p + j:2 * p + j + 1, :] for j in range(2)]
                for j in range(2):
                    cj = c_run[d][p][j]
                    call_scr[d, n, p, j * DK:(j + 1) * DK, :] = cj.astype(BF16)
                    c_run[d][p][j] = a_s[j] * cj + b_s[j] * kv_u[2 * p + j]
                n_run[d][p] = (jnp.where(head_mask[0], a_s[0], a_s[1]) * n_run[d][p]
                               + jnp.where(head_mask[0], b_s[0] * ks_u[2 * p], b_s[1] * ks_u[2 * p + 1]))
        for d in dirs:
            for p in range(n_pairs):
                n_scr[2 * d + p:2 * d + p + 1, :] = n_run[d][p]
                for j in range(2):
                    c_scr[d, p, j * DK:(j + 1) * DK, :] = c_run[d][p][j]

    eye = lower & upper
    ones8 = jnp.ones((SUBLANES, L), BF16)
    sub8 = lax.broadcasted_iota(jnp.int32, (SUBLANES, LANES), 0)
    sub_h = lax.broadcasted_iota(jnp.int32, (H_B, L), 0)
    n_rows = [((sub8 == 2 * p) & head_mask[0]) | ((sub8 == 2 * p + 1) & head_mask[1]) for p in range(n_pairs)]

    def head_rows(vals):
        out = vals[0][0:H_B, :]
        for h in range(1, H_B):
            out = jnp.where(sub_h == h, vals[h][0:H_B, :], out)
        return out

    def out_group(ns):
        chunks = [(d, n) for n in ns for d in range(2)]
        pairs = [(d, n, p) for d, n in chunks for p in range(n_pairs)]
        units = [(d, n, p, j) for d, n, p in pairs for j in range(2)]
        cms = [lane_cummax(g_scr[d, n], d)[0:H_B, 0:L] for d, n in chunks]
        qk2s, qc2s, qn2s = [], [], []
        for d, n, p in pairs:
            r = _chunk_rows(n)
            qp = qk_scr[r, p * LANES:(p + 1) * LANES]
            q2 = jnp.concatenate([jnp.where(head_mask[j], qp, 0.0) for j in range(2)], axis=0).astype(BF16)
            qk2s.append(_dot_nt(q2, qk_scr[r, HK + p * LANES:HK + (p + 1) * LANES].astype(BF16)))
            qc2s.append(_dot(q2, call_scr[d, n, p]))
            nsel = jnp.where(n_rows[p], nall_scr[d, n, p:p + 1, :], 0.0).astype(BF16)
            qn2s.append(_dot_nt(nsel, qp.astype(BF16)))
        yield
        s_all = []
        for ui, (d, n, p, j) in enumerate(units):
            grow = g_scr[d, n, 2 * p + j:2 * p + j + 1, 0:L]
            e = jnp.where(tmask[d], grow, -jnp.inf)
            cmax = jnp.max(e, axis=-1, keepdims=True)
            s_all.append((qk2s[ui // 2][j * L:(j + 1) * L, :] * jnp.exp(e - cmax)).astype(BF16))
        yield
        nums = [_dot(s, v_ref[_chunk_rows(n), (2 * p + j) * DV:(2 * p + j + 1) * DV])
                for (d, n, p, j), s in zip(units, s_all)]
        dens = [_dot_nt(ones8, s) for s in s_all]
        yield
        scales = []
        for ci, (d, n) in enumerate(chunks):
            den_loc = head_rows(dens[ci * H_B:(ci + 1) * H_B])
            qn = qn2s[ci * n_pairs][0:H_B, :]
            for p in range(1, n_pairs):
                qn = qn + qn2s[ci * n_pairs + p][0:H_B, :]
            cm = cms[ci]
            m_prev = mall_scr[d, n, 0:H_B, 0:1]
            delta = cm - m_prev
            t = jnp.exp(-jnp.abs(delta))
            w_loc = jnp.where(delta <= 0.0, t, 1.0)
            w_inter = jnp.where(delta <= 0.0, 1.0, t)
            mt = f_scr[d, n, 0:H_B, 0:L] + jnp.maximum(m_prev, cm)
            den = w_loc * den_loc + w_inter * qn
            rinv = 1.0 / jnp.maximum(jnp.abs(den), jnp.exp(-mt))
            scales.append((w_loc * rinv, w_inter * rinv))
        yield
        hs = []
        for ui, (d, n, p, j) in enumerate(units):
            h = 2 * p + j
            sc_loc, sc_inter = scales[ui // H_B]
            d_loc = jnp.where(eye, sc_loc[h:h + 1, :], 0.0).astype(BF16)
            d_inter = jnp.where(eye, sc_inter[h:h + 1, :], 0.0).astype(BF16)
            hs.append(_dot(d_loc, nums[ui].astype(BF16))
                      + _dot(d_inter, qc2s[ui // 2][j * L:(j + 1) * L, :].astype(BF16)))
        yield
        for ni, n in enumerate(ns):
            r = _chunk_rows(n)
            for h in range(H_B):
                vs = slice(h * DV, (h + 1) * DV)
                o = hs[(2 * ni) * H_B + h] + hs[(2 * ni + 1) * H_B + h]
                out_ref[r, vs] = (_rms(o, gw_ref[:, vs]) * _sigmoid(og_ref[r, vs].astype(F32))).astype(out_ref.dtype)

    def finish():
        if write_state:
            for d in range(2):
                for p in range(n_pairs):
                    cnew_ref[d, p] = c_scr[d, p]
                    for j in range(2):
                        nnew_ref[d, 2 * p + j:2 * p + j + 1, :] = n_scr[2 * d + p:2 * d + p + 1, j * DK:(j + 1) * DK]
                mnew_ref[d:d + 1, :] = to_row(m_scr[H_B * d:H_B * (d + 1), 0:1])

    return state_group, out_group, finish


def _mlstm_scratch(T, C2, grid_w):
    n_pairs = C2 // 2 // LANES
    n_chunks = T // CHUNK
    pad_rows = 2 * (grid_w + SUBLANES) if T // grid_w > 1 else 2 * SUBLANES
    return [
        pltpu.VMEM((T + pad_rows, C2), F32),
        pltpu.VMEM((T, C2), F32),
        pltpu.VMEM((T, SMALL_W), F32),
        pltpu.VMEM((2, n_pairs, LANES, LANES), F32),
        pltpu.VMEM((SUBLANES, LANES), F32),
        pltpu.VMEM((SUBLANES, LANES), F32),
        pltpu.VMEM((2, n_chunks, n_pairs, LANES, LANES), BF16),
        pltpu.VMEM((2, n_chunks, SUBLANES, LANES), F32),
        pltpu.VMEM((2, n_chunks, SUBLANES, LANES), F32),
        pltpu.VMEM((2, n_chunks, SUBLANES, LANES), F32),
        pltpu.VMEM((2, n_chunks, SUBLANES, LANES), F32),
    ]


N_GLA_SCRATCH = 5
N_MLSTM_SCRATCH = 11


def _scan_kernel(*refs, cols, layer, has_state, write_state, n_cast, ride_ada, grid_w, unroll):
    refs = list(refs)
    z_refs = refs[:2]
    del refs[:2]
    s0_ref = c0_ref = n0_ref = m0_ref = None
    if has_state:
        s0_ref, c0_ref, n0_ref, m0_ref = refs[:4]
        del refs[:4]
    wa_ref, bal_ref, gwa_ref, cw_ref, bmg_ref, gwb_ref = refs[:6]
    del refs[:6]
    cast_in = refs[:n_cast]
    del refs[:n_cast]
    if ride_ada:
        ada_in = refs[:4]
        del refs[:4]
    outa_ref, outb_ref = refs[:2]
    del refs[:2]
    snew_ref = cnew_ref = nnew_ref = mnew_ref = None
    if write_state:
        snew_ref, cnew_ref, nnew_ref, mnew_ref = refs[:4]
        del refs[:4]
    cast_out = refs[:n_cast]
    del refs[:n_cast]
    if ride_ada:
        ada_out = refs.pop(0)
    wal_scr, bm_scr = refs[:2]
    del refs[:2]
    gla_scr = refs[:N_GLA_SCRATCH]
    mlstm_scr = refs[N_GLA_SCRATCH:]

    for src, dst in zip(cast_in, cast_out):
        dst[...] = src[...].astype(BF16)
    if ride_ada:
        _ada_tile(*ada_in, ada_out)

    R, HK = wa_ref.shape[1], wa_ref.shape[2]
    wal_scr[...] = jnp.zeros(wal_scr.shape, BF16)
    for d in range(2):
        wal_scr[d * R:(d + 1) * R, d * HK:(d + 1) * HK] = wa_ref[d].astype(BF16)
    lane = lax.broadcasted_iota(jnp.int32, (1, LANES), 1)
    bm = jnp.zeros((1, LANES), F32)
    for g in range(bmg_ref.shape[1]):
        for h in range(H_B):
            bm = jnp.where(lane == GATE_LANE0 + H_B * g + h, bmg_ref[layer, g, h], bm)
    bm_scr[0:1, :] = bm

    def view(name):
        a, c0, w = cols[name]
        return z_refs[a].at[:, pl.ds(c0, w)]

    sm_ref = view("small")
    n_chunks = z_refs[0].shape[0] // CHUNK
    gla = _gla_body(view("qa"), view("ka"), view("va"), view("ga"), sm_ref, s0_ref, wal_scr, bal_ref, gwa_ref,
                    outa_ref, snew_ref, *gla_scr)
    mlstm = _mlstm_body(view("qkb"), view("vb"), view("ob"), sm_ref, c0_ref, n0_ref, m0_ref, cw_ref,
                        bm_scr.at[0:1, :], gwb_ref, outb_ref, cnew_ref, nnew_ref, mnew_ref, *mlstm_scr,
                        grid_w=grid_w)
    gla_state, gla_out, gla_finish, decay_span = gla
    mlstm_state, mlstm_out, mlstm_finish = mlstm

    def passes(gla_out_fn):
        _chunk_loop(n_chunks, unroll, lambda ns: [fn([n], (d,)) for n in ns for d in range(2)
                                                  for fn in (mlstm_state, gla_state)])
        _chunk_loop(n_chunks, unroll, lambda ns: [fn([n]) for n in ns for fn in (mlstm_out, gla_out_fn)])

    wide_decay = decay_span > GLA_FACTORED_DECAY_MAX

    @pl.when(jnp.logical_not(wide_decay))
    def _():
        passes(gla_out)

    @pl.when(wide_decay)
    def _():
        passes(functools.partial(gla_out, exact_decay=True))

    gla_finish()
    mlstm_finish()


def _scan_call(z2d, row0, B, T, states, lw, layer, *, grid_w, write_state, casts=(), ada=None):
    assert row0 % T == 0 and all(z.shape[0] % T == 0 for z in z2d)
    z3 = [z.reshape(z.shape[0] // T, T, z.shape[1]) for z in z2d]
    blk0 = row0 // T
    HK = lw["w_alpha2"].shape[-1]
    DA = lw["gnorm_a_w"].shape[0]
    C2 = lw["conv_w"].shape[-1]
    DB = lw["gnorm_b_w"].shape[0]
    DK_A, DK_B = HK // H_A, C2 // 2 // H_B
    pa, pb = HK // LANES, C2 // 2 // LANES
    n_chunks = T // CHUNK
    has_state = states is not None
    widths = ((("qa", HK), ("ka", HK), ("qkb", C2), ("small", SMALL_W)),
              (("va", DA), ("ga", DA), ("vb", DB), ("ob", DB)))
    cols = {}
    for a, groups in enumerate(widths):
        c0 = 0
        for name, w in groups:
            cols[name] = (a, c0, w)
            c0 += w
        assert c0 == z3[a].shape[2]
    cast_in_specs, cast_out_specs, cast_out_shape, cast_args = _cast_specs(casts, B)
    kern = functools.partial(_scan_kernel, cols=cols, layer=layer, has_state=has_state, write_state=write_state,
                             n_cast=len(casts), ride_ada=ada is not None, grid_w=grid_w,
                             unroll=min(n_chunks, SCAN_UNROLL))

    def per_batch(shape):
        nd = len(shape)
        return pl.BlockSpec((None,) + tuple(shape), lambda b: (b,) + (0,) * nd)

    def per_batch_layer(shape):
        nd = len(shape)
        return pl.BlockSpec((None, None) + tuple(shape), lambda b: (b, layer) + (0,) * nd)

    def of_layer(a):
        return pl.BlockSpec((None,) + a.shape[1:], lambda b: (layer,) + (0,) * (a.ndim - 1))

    def whole(a):
        return pl.BlockSpec(a.shape, lambda b: (0,) * a.ndim)

    state_shapes = ((2, pa, LANES, LANES), (2, pb, LANES, LANES), (2, H_B, DK_B), (2, H_B))
    in_specs = [pl.BlockSpec((None, T, z.shape[2]), lambda b: (b + blk0, 0, 0)) for z in z3]
    args = list(z3)
    if has_state:
        s_gla, s_c, s_n, s_m = states
        depth = s_gla.shape[1]
        args += [s_gla.reshape((B, depth) + state_shapes[0]), s_c.reshape((B, depth) + state_shapes[1]), s_n, s_m]
        in_specs += [per_batch_layer(s) for s in state_shapes]
    args += [lw["w_alpha2"], lw["b_alpha"], lw["gnorm_a_w"].reshape(1, DA), lw["conv_w"], lw["b_mgate"],
             lw["gnorm_b_w"].reshape(1, DB)]
    in_specs += [of_layer(lw["w_alpha2"]), of_layer(lw["b_alpha"]), pl.BlockSpec((1, DA), lambda b: (0, 0)),
                 whole(lw["conv_w"]), pl.BlockSpec(memory_space=pltpu.SMEM), pl.BlockSpec((1, DB), lambda b: (0, 0))]
    args += cast_args
    in_specs += cast_in_specs
    out_specs = [per_batch((T, DA)), per_batch((T, DB))]
    out_shape = [jax.ShapeDtypeStruct((B, T, DA), BF16), jax.ShapeDtypeStruct((B, T, DB), BF16)]
    if write_state:
        out_specs += [per_batch(s) for s in state_shapes]
        out_shape += [jax.ShapeDtypeStruct((B,) + s, F32) for s in state_shapes]
    out_specs += cast_out_specs
    out_shape += cast_out_shape
    if ada is not None:
        cc, c, w_ada, b_ada, col0 = ada
        n_rest = w_ada.shape[1] - col0
        wcol = n_rest // B
        assert n_rest % B == 0 and wcol % LANES == 0 and col0 % wcol == 0
        args += [cc, c, w_ada, b_ada]
        in_specs += [whole(cc), whole(c),
                     pl.BlockSpec((w_ada.shape[0], wcol), lambda b: (0, col0 // wcol + b)),
                     pl.BlockSpec((1, wcol), lambda b: (0, col0 // wcol + b))]
        out_specs.append(pl.BlockSpec((COND_ROWS, wcol), lambda b: (0, b)))
        out_shape.append(jax.ShapeDtypeStruct((COND_ROWS, n_rest), F32))
    scratch = ([pltpu.VMEM((SMALL_W, 2 * HK), BF16), pltpu.VMEM((SUBLANES, LANES), F32)]
               + _gla_scratch(T, HK) + _mlstm_scratch(T, C2, grid_w))
    assert len(scratch) == 2 + N_GLA_SCRATCH + N_MLSTM_SCRATCH
    return pl.pallas_call(
        kern,
        grid=(B,),
        in_specs=in_specs,
        out_specs=out_specs,
        out_shape=out_shape,
        scratch_shapes=scratch,
        compiler_params=pltpu.CompilerParams(dimension_semantics=("arbitrary",),
                                             vmem_limit_bytes=VMEM_LIMIT),
        name="mixer_scans",
    )(*args)


def _outff_kernel(xc_ref, xl_ref, ac_ref, al_ref, bc_ref, bl_ref, mod_ref, n2_ref, fn_ref, wo_ref, w1_ref, w2_ref,
                  yc_ref, yl_ref, *, n_ctx, tiles_per_req, ff_chunk, final_norm):
    D = xc_ref.shape[1]
    DA = ac_ref.shape[1]
    is_ctx, row = _tile_group(n_ctx, tiles_per_req)

    def mod(k):
        return mod_ref[pl.ds(row, 1), (k - MOD_SPLIT) * D:(k - MOD_SPLIT + 1) * D]

    def tile(x_ref, a_ref, b_ref, y_ref):
        y = _dot(a_ref[...], wo_ref[0:DA, :]) + _dot(b_ref[...], wo_ref[DA:, :])
        x1 = x_ref[...] + mod(2) * y
        h2 = (_rms(x1, n2_ref[...]) * (1.0 + mod(4)) + mod(3)).astype(BF16)
        acc = jnp.zeros(x1.shape, F32)
        for c0 in range(0, w1_ref.shape[1], ff_chunk):
            u = jnp.maximum(_dot(h2, w1_ref[:, c0:c0 + ff_chunk]), 0.0)
            acc = acc + _dot((u * u).astype(BF16), w2_ref[c0:c0 + ff_chunk, :])
        x2 = x1 + mod(5) * acc
        y_ref[...] = _rms(x2, fn_ref[...]) if final_norm else x2

    @pl.when(is_ctx)
    def _():
        tile(xc_ref, ac_ref, bc_ref, yc_ref)

    @pl.when(jnp.logical_not(is_ctx))
    def _():
        tile(xl_ref, al_ref, bl_ref, yl_ref)


def _outff_call(xc2d, xl2d, ac, al, bc, bl, mod, norm2_w, final_w, wo, w1, w2, *, tm, tiles_per_req, final_norm):
    (Mc, D), Ml = xc2d.shape, xl2d.shape[0]
    n_ctx = Mc // tm
    DA = ac.shape[1]
    DFF = w1.shape[1]
    kern = functools.partial(_outff_kernel, n_ctx=n_ctx, tiles_per_req=tiles_per_req, ff_chunk=FF_CHUNK,
                             final_norm=final_norm)
    once = pl.Buffered(1)
    ctx, lat = _ctx_tile(n_ctx), _lat_tile(n_ctx, Ml // tm)
    return pl.pallas_call(
        kern,
        grid=((Mc + Ml) // tm,),
        in_specs=[
            pl.BlockSpec((tm, D), ctx), pl.BlockSpec((tm, D), lat),
            pl.BlockSpec((tm, DA), ctx), pl.BlockSpec((tm, DA), lat),
            pl.BlockSpec((tm, D - DA), ctx), pl.BlockSpec((tm, D - DA), lat),
            pl.BlockSpec(mod.shape, lambda i: (0, 0)),
            pl.BlockSpec((1, D), lambda i: (0, 0)),
            pl.BlockSpec((1, D), lambda i: (0, 0)),
            pl.BlockSpec((D, D), lambda i: (0, 0), pipeline_mode=once),
            pl.BlockSpec((D, DFF), lambda i: (0, 0), pipeline_mode=once),
            pl.BlockSpec((DFF, D), lambda i: (0, 0), pipeline_mode=once),
        ],
        out_specs=[pl.BlockSpec((tm, D), ctx), pl.BlockSpec((tm, D), lat)],
        out_shape=[jax.ShapeDtypeStruct((Mc, D), F32), jax.ShapeDtypeStruct((Ml, D), F32)],
        compiler_params=pltpu.CompilerParams(dimension_semantics=("arbitrary",),
                                             vmem_limit_bytes=VMEM_LIMIT),
        name="outproj_mlp",
    )(xc2d, xl2d, ac, al, bc, bl, mod, norm2_w.reshape(1, D), final_w.reshape(1, D), wo, w1, w2)


def _layer(xc, xl, cond, ada_w, cached, lw, layer, ffw, final_w, final_norm):
    (Bc, Tc, D), (Bl, Tl, _) = xc.shape, xl.shape
    tm = TOKEN_TILE
    assert (Bc * Tc) % tm == 0 and Tl % tm == 0 and (Bc * Tc) % Tl == 0
    xc2d, xl2d = xc.reshape(Bc * Tc, D), xl.reshape(Bl * Tl, D)
    z = _inproj_call(xc2d, xl2d, *cond, *ada_w, lw["norm1_w"], lw["w_in_t"], tm=tm, tiles_per_req=Tl // tm,
                     f32_rows=lw["f32_rows"], small_rows=lw["small_rows"], bf16_rows=lw["bf16_rows"])
    res_c = _scan_call(z, 0, Bc, Tc, None, lw, layer, grid_w=Tc, write_state=True,
                       casts=((ffw[0], 0), (ffw[1], 0), (ffw[2], 0)), ada=(*cond, *ada_w, MOD_SPLIT * D))
    res_l = _scan_call(z, Bc * Tc, Bl, Tl, cached, lw, layer, grid_w=GRID_W, write_state=False)
    wo_b, w1_b, w2_b, mod_out = res_c[-4:]
    yc, yl = _outff_call(xc2d, xl2d, res_c[0].reshape(Bc * Tc, -1), res_l[0].reshape(Bl * Tl, -1),
                         res_c[1].reshape(Bc * Tc, -1), res_l[1].reshape(Bl * Tl, -1), mod_out, lw["norm2_w"],
                         final_w, wo_b, w1_b, w2_b, tm=tm, tiles_per_req=Tl // tm, final_norm=final_norm)
    return yc.reshape(Bc, Tc, D), yl.reshape(Bl, Tl, D), tuple(res_c[2:6])


def _layer_weights(l, norm1_w, norm2_w, w_in, w_alpha2, b_alpha, b_mgate, conv_w, gnorm_a_w, gnorm_b_w):
    hk_a = w_alpha2.shape[-1]
    d_a = gnorm_a_w.shape[-1]
    d_b = gnorm_b_w.shape[-1]
    hk_b = conv_w.shape[-1] // 2
    sizes = (hk_a, hk_a, d_a, d_a, 2 * R_ALPHA, hk_b, hk_b, d_b, d_b, 4 * H_B)
    assert w_alpha2.shape[2] == R_ALPHA and b_mgate.shape[1] * b_mgate.shape[2] == 4 * H_B
    offs = [0]
    for s in sizes:
        offs.append(offs[-1] + s)
    f32_rows = ((offs[0], offs[2] - offs[0]), (offs[5], offs[7] - offs[5]))
    small_rows = ((offs[4], offs[5] - offs[4]), (offs[9], offs[10] - offs[9]))
    bf16_rows = ((offs[2], offs[4] - offs[2]), (offs[7], offs[9] - offs[7]))
    assert all(n % LANES == 0 and r % BF16_ROWS == 0 for r, n in f32_rows + bf16_rows)
    return dict(
        norm1_w=norm1_w[l], norm2_w=norm2_w[l], w_in_t=jnp.swapaxes(w_in[l], 0, 1),
        f32_rows=f32_rows, small_rows=small_rows, bf16_rows=bf16_rows,
        w_alpha2=w_alpha2, b_alpha=b_alpha, b_mgate=b_mgate, conv_w=conv_w[l],
        gnorm_a_w=gnorm_a_w[l], gnorm_b_w=gnorm_b_w[l],
    )


def kernel(x_prompt, x_sample, c, state_gla, state_mlstm_C, state_mlstm_n, state_mlstm_m, c_ctx, w_ada, b_ada, norm1_w, norm2_w, w_in, w_alpha2, b_alpha, b_mgate, conv_w, gnorm_a_w, gnorm_b_w, w_out, w_ff1, w_ff2, final_norm_w):
    depth = w_in.shape[0]
    D = x_prompt.shape[-1]
    Bp, Tp, _ = x_prompt.shape
    Bs = x_sample.shape[0]
    assert 1 + Bs <= COND_ROWS
    cond = (c_ctx.reshape(1, D), c)
    cached = (state_gla, state_mlstm_C, state_mlstm_n, state_mlstm_m)
    xp, xs = x_prompt, x_sample
    s_gla, s_c, s_n, s_m = [], [], [], []
    for l in range(depth):
        lw = _layer_weights(l, norm1_w, norm2_w, w_in, w_alpha2, b_alpha, b_mgate, conv_w,
                            gnorm_a_w, gnorm_b_w)
        xp, xs, ctx = _layer(xp, xs, cond, (w_ada[l], b_ada[l].reshape(1, -1)), cached, lw, l,
                             (w_out[l], w_ff1[l], w_ff2[l]), final_norm_w, l == depth - 1)
        s_gla.append(ctx[0].reshape(Bp, 2, H_A, -1, ctx[0].shape[-1]))
        s_c.append(ctx[1].reshape(Bp, 2, H_B, -1, ctx[1].shape[-1]))
        s_n.append(ctx[2])
        s_m.append(ctx[3])
    dt = x_prompt.dtype
    return (xp, xs, jnp.stack(s_gla, axis=1).astype(dt), jnp.stack(s_c, axis=1).astype(dt),
            jnp.stack(s_n, axis=1).astype(dt), jnp.stack(s_m, axis=1).astype(dt))
```

```python
import functools
import math

import jax
import jax.numpy as jnp
from jax import lax
from jax.experimental import pallas as pl
from jax.experimental.pallas import tpu as pltpu

F32 = jnp.float32
BF16 = jnp.bfloat16

GRID_W = 64
H_A = 4
H_B = 4
R_ALPHA = 16
TAU_GLA = 16.0
CHUNK = 64
EPS = 1e-6
LANES = 128
SUBLANES = 8
BF16_ROWS = 16
COND_ROWS = SUBLANES
SMALL_W = LANES
GATE_LANE0 = 2 * R_ALPHA
VMEM_LIMIT = 56 * 1024 * 1024
SCAN_UNROLL = 2
PIPELINE_STARTS = 8
TOKEN_TILE = 512
FF_CHUNK = 512
GLA_FACTORED_DECAY_MAX = 60.0
MOD_SPLIT = 2


def _sigmoid(x):
    return 1.0 / (1.0 + jnp.exp(-x))


def _silu(x):
    return x * _sigmoid(x)


def _log_sigmoid(x):
    return jnp.minimum(x, 0.0) - jnp.log(1.0 + jnp.exp(-jnp.abs(x)))


def _dot(a, b):
    return jnp.dot(a, b, preferred_element_type=F32)


def _dot_nt(a, b):
    return lax.dot_general(a, b, (((1,), (1,)), ((), ())), preferred_element_type=F32)


def _rms(x, w):
    return x * lax.rsqrt(jnp.mean(x * x, axis=-1, keepdims=True) + EPS) * w


def _tri_sum(tri, x, terms=3):
    acc, rest = None, x
    for t in range(terms):
        part = rest.astype(BF16)
        prod = _dot(tri, part)
        acc = prod if acc is None else acc + prod
        if t + 1 < terms:
            rest = rest - part.astype(F32)
    return acc


def _chunk_masks(L):
    row = lax.broadcasted_iota(jnp.int32, (L, L), 0)
    col = lax.broadcasted_iota(jnp.int32, (L, L), 1)
    lower = row >= col
    upper = row <= col
    return lower, upper


def _ada_tile(cc_ref, c_ref, w_ref, b_ref, o_ref):
    D = cc_ref.shape[1]
    sub = lax.broadcasted_iota(jnp.int32, (COND_ROWS, D), 0)
    cond = jnp.where(sub == 0, cc_ref[...], 0.0)
    for r in range(c_ref.shape[0]):
        cond = jnp.where(sub == 1 + r, c_ref[r:r + 1, :], cond)
    o_ref[...] = _dot(_silu(cond).astype(BF16), w_ref[...].astype(BF16)) + b_ref[...]


def _tile_group(n_ctx, tiles_per_req):
    i = pl.program_id(0)
    is_ctx = i < n_ctx
    row = jnp.where(is_ctx, 0, 1 + jnp.maximum(i - n_ctx, 0) // tiles_per_req)
    return is_ctx, row


def _ctx_tile(n_ctx):
    return lambda i: (jnp.minimum(i, n_ctx - 1), 0)


def _lat_tile(n_ctx):
    return lambda i: (jnp.maximum(i - n_ctx, 0), 0)


def _inproj_kernel(xc_ref, xl_ref, cc_ref, c_ref, wa_ref, ba_ref, nw_ref, wt_ref, zf_ref, zh_ref, wb_scr, mod_ref,
                   *, n_ctx, tiles_per_req, f32_rows, small_rows, bf16_rows):
    D = xc_ref.shape[1]
    n_f32 = zf_ref.shape[1]

    @pl.when(pl.program_id(0) == 0)
    def _():
        _ada_tile(cc_ref, c_ref, wa_ref, ba_ref, mod_ref)

        def wide(rows, col):
            for r0, n in rows:
                for k in range(n // LANES):
                    blk = wt_ref[r0 + k * LANES:r0 + (k + 1) * LANES, :]
                    wb_scr[:, col:col + LANES] = blk.T.astype(BF16)
                    col += LANES
            return col

        col = wide(f32_rows, 0)
        parts = [wt_ref[r0:r0 + n, :] for r0, n in small_rows]
        n_small = sum(n for _, n in small_rows)
        parts.append(jnp.zeros((SMALL_W - n_small, D), F32))
        wb_scr[:, col:col + SMALL_W] = jnp.concatenate(parts, axis=0).T.astype(BF16)
        wide(bf16_rows, col + SMALL_W)

    is_ctx, row = _tile_group(n_ctx, tiles_per_req)

    def tile(x_ref):
        sh1 = mod_ref[pl.ds(row, 1), 0:D]
        sc1 = mod_ref[pl.ds(row, 1), D:2 * D]
        h = (_rms(x_ref[...], nw_ref[...]) * (1.0 + sc1) + sh1).astype(BF16)
        zf_ref[...] = _dot(h, wb_scr[:, 0:n_f32])
        zh_ref[...] = _dot(h, wb_scr[:, n_f32:]).astype(BF16)

    @pl.when(is_ctx)
    def _():
        tile(xc_ref)

    @pl.when(jnp.logical_not(is_ctx))
    def _():
        tile(xl_ref)


def _inproj_call(xc2d, xl2d, cc, c, w_ada, b_ada, norm_w, w_in_t, *, tm, tiles_per_req, f32_rows, small_rows,
                 bf16_rows):
    (Mc, D), Ml = xc2d.shape, xl2d.shape[0]
    n_ctx = Mc // tm
    n_f32 = sum(n for _, n in f32_rows) + SMALL_W
    n_bf16 = sum(n for _, n in bf16_rows)
    n_out = n_f32 + n_bf16
    n_mod = MOD_SPLIT * D
    kern = functools.partial(_inproj_kernel, n_ctx=n_ctx, tiles_per_req=tiles_per_req,
                             f32_rows=f32_rows, small_rows=small_rows, bf16_rows=bf16_rows)
    once = pl.Buffered(1)
    return pl.pallas_call(
        kern,
        grid=((Mc + Ml) // tm,),
        in_specs=[
            pl.BlockSpec((tm, D), _ctx_tile(n_ctx)),
            pl.BlockSpec((tm, D), _lat_tile(n_ctx)),
            pl.BlockSpec(cc.shape, lambda i: (0, 0)),
            pl.BlockSpec(c.shape, lambda i: (0, 0)),
            pl.BlockSpec((D, n_mod), lambda i: (0, 0), pipeline_mode=once),
            pl.BlockSpec((1, n_mod), lambda i: (0, 0)),
            pl.BlockSpec((1, D), lambda i: (0, 0)),
            pl.BlockSpec(w_in_t.shape, lambda i: (0, 0), pipeline_mode=once),
        ],
        out_specs=[pl.BlockSpec((tm, n_f32), lambda i: (i, 0)), pl.BlockSpec((tm, n_bf16), lambda i: (i, 0))],
        out_shape=[jax.ShapeDtypeStruct((Mc + Ml, n_f32), F32), jax.ShapeDtypeStruct((Mc + Ml, n_bf16), BF16)],
        scratch_shapes=[pltpu.VMEM((D, n_out), BF16), pltpu.VMEM((COND_ROWS, n_mod), F32)],
        compiler_params=pltpu.CompilerParams(dimension_semantics=("arbitrary",),
                                             vmem_limit_bytes=VMEM_LIMIT),
        name="norm_inproj",
    )(xc2d, xl2d, cc, c, w_ada, b_ada, norm_w.reshape(1, D), w_in_t)


def _chunk_loop(n_chunks, unroll, make_units):
    def step(ns):
        pending = list(make_units(ns))
        active = []
        while pending or active:
            for _ in range(min(PIPELINE_STARTS, len(pending))):
                active.append(pending.pop(0))
            alive = []
            for g in active:
                try:
                    next(g)
                    alive.append(g)
                except StopIteration:
                    pass
            active = alive

    if unroll >= n_chunks:
        step(list(range(n_chunks)))
        return

    def body(i, carry):
        step([i * unroll + u for u in range(unroll)])
        return carry

    lax.fori_loop(0, n_chunks // unroll, body, 0)


def _chunk_rows(n):
    if isinstance(n, int):
        return pl.ds(n * CHUNK, CHUNK)
    return pl.ds(pl.multiple_of(n * CHUNK, CHUNK), CHUNK)


def _cast_specs(casts, n_steps):
    in_specs, out_specs, out_shape, args = [], [], [], []
    for w, axis in casts:
        blk = list(w.shape)
        assert blk[axis] % n_steps == 0
        blk[axis] //= n_steps
        assert blk[0] % BF16_ROWS == 0 and blk[1] % LANES == 0
        idx = (lambda b: (b, 0)) if axis == 0 else (lambda b: (0, b))
        in_specs.append(pl.BlockSpec(tuple(blk), idx))
        out_specs.append(pl.BlockSpec(tuple(blk), idx))
        out_shape.append(jax.ShapeDtypeStruct(w.shape, BF16))
        args.append(w)
    return in_specs, out_specs, out_shape, args


def _gla_body(q_ref, k_ref, v_ref, g_ref, sm_ref, s0_ref, wal_ref, bal_ref, gw_ref, out_ref, snew_ref,
              st_scr, sall_scr, qh_scr, qs_scr, kh_scr):
    has_state = s0_ref is not None
    write_state = snew_ref is not None
    T = q_ref.shape[0]
    L = CHUNK
    N = T // L
    HK = q_ref.shape[1]
    DK = HK // H_A
    DV = v_ref.shape[1] // H_A
    scale = DK ** -0.5
    n_pairs = HK // LANES

    lower, upper = _chunk_masks(L)
    tri = (lower.astype(BF16), upper.astype(BF16))
    tmask = (lower, upper)
    lane = lax.broadcasted_iota(jnp.int32, (1, LANES), 1)
    head_mask = (lane < DK, lane >= DK)

    for d in range(2):
        for p in range(n_pairs):
            if has_state:
                st_scr[d, p] = s0_ref[d, p].T
            else:
                st_scr[d, p] = jnp.zeros((LANES, LANES), F32)

    def decay_pre(d, r):
        return _dot(sm_ref[r, :].astype(BF16), wal_ref[:, d * HK:(d + 1) * HK]) + bal_ref[d:d + 1, :]

    neg_pre = jnp.maximum(-(_dot(sm_ref[...].astype(BF16), wal_ref[...])
                            + jnp.concatenate([bal_ref[0:1, :], bal_ref[1:2, :]], axis=1)), 0.0)
    chunk_sums = jnp.sum(neg_pre.reshape(N, L, 2 * HK), axis=1)
    decay_span = (jnp.max(chunk_sums) + L * math.log(2.0)) * (1.0 / TAU_GLA)

    def state_group(ns, dirs=(0, 1)):
        units = [(d, n if d == 0 else N - 1 - n) for n in ns for d in dirs]
        rows = [_chunk_rows(n) for _, n in units]
        vt_all = [[jnp.concatenate([v_ref[r, (2 * p + j) * DV:(2 * p + j + 1) * DV] for j in range(2)],
                                   axis=0).astype(F32).T.astype(BF16) for p in range(n_pairs)] for r in rows]
        yield
        pre = [decay_pre(d, r) for (d, _), r in zip(units, rows)]
        yield
        g = [_log_sigmoid(x) * (1.0 / TAU_GLA) for x in pre]
        yield
        b = [_tri_sum(tri[d], gi, terms=2) for (d, _), gi in zip(units, g)]
        yield
        ks_all, dec_all = [], []
        for (d, _), r, bi in zip(units, rows, b):
            bend = bi[L - 1:L, :] if d == 0 else bi[0:1, :]
            q = q_ref[r, :] * scale
            ks = (k_ref[r, :] * jnp.exp(bend - bi)).astype(BF16)
            qs = q * jnp.exp(bi)
            qh_scr[d, r, :] = (qs * jnp.exp(-bend)).astype(BF16)
            qs_scr[d, r, :] = qs.astype(BF16)
            kh_scr[d, r, :] = ks
            ks_all.append(ks)
            dec_all.append(jnp.exp(bend))
        yield
        upd_all = []
        for vt_u, ks in zip(vt_all, ks_all):
            upd_u = []
            for p in range(n_pairs):
                kp = ks[:, p * LANES:(p + 1) * LANES]
                kk = jnp.concatenate([jnp.where(head_mask[j], kp, jnp.zeros_like(kp)) for j in range(2)], axis=0)
                upd_u.append(_dot(vt_u[p], kk))
            upd_all.append(upd_u)
        yield
        st = {d: [st_scr[d, p] for p in range(n_pairs)] for d in dirs}
        for (d, n), dec, upd in zip(units, dec_all, upd_all):
            for p in range(n_pairs):
                sall_scr[d, n, p] = st[d][p].astype(BF16)
                st[d][p] = st[d][p] * dec[:, p * LANES:(p + 1) * LANES] + upd[p]
        for d in dirs:
            for p in range(n_pairs):
                st_scr[d, p] = st[d][p]

    def stack_heads(x):
        return jnp.concatenate([jnp.where(head_mask[j], x, jnp.zeros_like(x)) for j in range(2)], axis=0)

    tok = lax.broadcasted_iota(jnp.int32, (L, 1), 0)
    row_t = lax.broadcasted_iota(jnp.int32, (2 * L, L), 0) & (L - 1)
    col_s = lax.broadcasted_iota(jnp.int32, (2 * L, L), 1)

    def exact_scores(d, r, p):
        ls = slice(p * LANES, (p + 1) * LANES)
        b = _tri_sum(tri[d], _log_sigmoid(decay_pre(d, r)[:, ls]) * (1.0 / TAU_GLA))
        q = q_ref[r, ls] * scale
        k = k_ref[r, ls]
        acc = jnp.where(row_t == col_s, _dot_nt(stack_heads(q).astype(BF16), k.astype(BF16)), 0.0)
        src = lax.broadcasted_iota(jnp.int32, (L, L), 1)
        h = L // 2
        while h >= 1:
            first = tok & ~(2 * h - 1)
            edge = first + (h - 1 if d == 0 else h)
            b_edge = _tri_sum((src == edge).astype(BF16), b)
            upper = (tok & (2 * h - 1)) >= h
            later, earlier = (upper, ~upper) if d == 0 else (~upper, upper)
            qt = jnp.where(later, q * jnp.exp(b - b_edge), 0.0)
            kt = jnp.where(earlier, k * jnp.exp(b_edge - b), 0.0)
            sc = _dot_nt(stack_heads(qt).astype(BF16), kt.astype(BF16))
            acc = acc + jnp.where((row_t & ~(2 * h - 1)) == (col_s & ~(2 * h - 1)), sc, 0.0)
            h //= 2
        return acc

    def out_group(ns, exact_decay=False):
        pairs = [(d, ni, p) for ni in range(len(ns)) for d in range(2) for p in range(n_pairs)]
        scores, inter = [], []
        for d, ni, p in pairs:
            r = _chunk_rows(ns[ni])
            ls = slice(p * LANES, (p + 1) * LANES)
            if exact_decay:
                scores.append(exact_scores(d, r, p))
            else:
                scores.append(_dot_nt(stack_heads(qh_scr[d, r, ls]), kh_scr[d, r, ls]))
            inter.append(_dot_nt(stack_heads(qs_scr[d, r, ls]), sall_scr[d, ns[ni], p]))
        yield
        probs = [[jnp.where(tmask[d], sc[j * L:(j + 1) * L, :], 0.0).astype(BF16) for j in range(2)]
                 for (d, _, _), sc in zip(pairs, scores)]
        yield
        outs = {}
        for (d, ni, p), pr, it in zip(pairs, probs, inter):
            r = _chunk_rows(ns[ni])
            for j in range(2):
                vs = slice((2 * p + j) * DV, (2 * p + j + 1) * DV)
                outs[(d, ni, 2 * p + j)] = _dot(pr[j], v_ref[r, vs]) + it[j * L:(j + 1) * L, :]
        yield
        for ni, n in enumerate(ns):
            r = _chunk_rows(n)
            for h in range(H_A):
                vs = slice(h * DV, (h + 1) * DV)
                o = outs[(0, ni, h)] + outs[(1, ni, h)]
                out_ref[r, vs] = (_rms(o, gw_ref[:, vs]) * _silu(g_ref[r, vs].astype(F32))).astype(out_ref.dtype)

    def finish():
        if write_state:
            for d in range(2):
                for p in range(n_pairs):
                    snew_ref[d, p] = st_scr[d, p].T

    return state_group, out_group, finish, decay_span


def _gla_scratch(T, HK):
    n_pairs = HK // LANES
    n_chunks = T // CHUNK
    return [
        pltpu.VMEM((2, n_pairs, LANES, LANES), F32),
        pltpu.VMEM((2, n_chunks, n_pairs, LANES, LANES), BF16),
        pltpu.VMEM((2, T, HK), BF16),
        pltpu.VMEM((2, T, HK), BF16),
        pltpu.VMEM((2, T, HK), BF16),
    ]


def _mlstm_body(qk_ref, v_ref, og_ref, sm_ref, c0_ref, n0_ref, m0_ref, cw_ref, bm_ref, gw_ref,
                out_ref, cnew_ref, nnew_ref, mnew_ref,
                pad_scr, qk_scr, y_scr, c_scr, n_scr, m_scr, call_scr, nall_scr, mall_scr, g_scr, f_scr,
                *, grid_w):
    has_state = c0_ref is not None
    write_state = cnew_ref is not None
    T = qk_ref.shape[0]
    L = CHUNK
    N = T // L
    C2 = qk_ref.shape[1]
    HK = C2 // 2
    DK = HK // H_B
    DV = v_ref.shape[1] // H_B
    scale = DK ** -0.5
    n_pairs = HK // LANES
    P = pad_scr.shape[0] - T
    P0 = P // 2
    rows_img = T // grid_w

    lower, upper = _chunk_masks(L)
    tri = (lower.astype(BF16), upper.astype(BF16))
    tmask = (lower, upper)
    lane = lax.broadcasted_iota(jnp.int32, (1, LANES), 1)
    head_mask = (lane < DK, lane >= DK)
    lane_in = lane & (L - 1)

    def lane_cummax(x, d):
        k = 1
        while k < L:
            if d == 0:
                x = jnp.maximum(x, jnp.where(lane_in >= k, pltpu.roll(x, k, axis=1), -jnp.inf))
            else:
                x = jnp.maximum(x, jnp.where(lane_in < L - k, pltpu.roll(x, LANES - k, axis=1), -jnp.inf))
            k *= 2
        return x

    for d in range(2):
        for p in range(n_pairs):
            if has_state:
                c_scr[d, p] = c0_ref[d, p]
                n_scr[2 * d + p:2 * d + p + 1, :] = jnp.concatenate(
                    [n0_ref[d, 2 * p + j:2 * p + j + 1, :] for j in range(2)], axis=1)
            else:
                c_scr[d, p] = jnp.zeros((LANES, LANES), F32)
                n_scr[2 * d + p:2 * d + p + 1, :] = jnp.zeros((1, LANES), F32)
    eye_h = (lax.broadcasted_iota(jnp.int32, (H_B, H_B), 0) == lax.broadcasted_iota(jnp.int32, (H_B, H_B), 1))

    def to_col(row):
        return jnp.sum(jnp.where(eye_h, row, 0.0), axis=1, keepdims=True)

    def to_row(col):
        return jnp.sum(jnp.where(eye_h, col, 0.0), axis=0, keepdims=True)

    for d in range(2):
        if has_state:
            m_scr[H_B * d:H_B * (d + 1), 0:1] = to_col(m0_ref[d:d + 1, :])
        else:
            m_scr[H_B * d:H_B * (d + 1), 0:1] = jnp.zeros((H_B, 1), F32)

    pad_scr[0:P0, :] = jnp.zeros((P0, C2), F32)
    pad_scr[P0 + T:P + T, :] = jnp.zeros((P - P0, C2), F32)

    def copy_in(i, carry):
        r0 = pl.multiple_of(i * L, L)
        pad_scr[pl.ds(P0 + r0, L), :] = qk_ref[pl.ds(r0, L), :]
        return carry

    lax.fori_loop(0, N, copy_in, 0)

    lane_c = lax.broadcasted_iota(jnp.int32, (1, C2), 1)
    qscale = jnp.where(lane_c < HK, scale, 1.0).astype(F32)
    sub = lax.broadcasted_iota(jnp.int32, (L, 1), 0)
    img_rows = (0,) if rows_img == 1 else (-1, 0, 1)

    def conv_tile(i, carry):
        r0 = pl.multiple_of(i * L, L)
        col = lax.rem(r0, grid_w) + sub
        ok_left = col >= 1
        ok_right = col <= grid_w - 2
        sums = [None, None, None]
        for di in img_rows:
            blk = pad_scr[pl.ds(P0 + r0 + di * grid_w - SUBLANES, L + 2 * SUBLANES), :]
            for k in range(3):
                term = blk * cw_ref[di + 1, k:k + 1, :]
                sums[k] = term if sums[k] is None else sums[k] + term
        S = SUBLANES
        acc = (sums[1][S:S + L, :] + jnp.where(ok_left, sums[0][S - 1:S - 1 + L, :], 0.0)
               + jnp.where(ok_right, sums[2][S + 1:S + 1 + L, :], 0.0))
        qk_scr[pl.ds(r0, L), :] = _silu(acc) * qscale
        return carry

    lax.fori_loop(0, N, conv_tile, 0)

    gl = lane - GATE_LANE0
    is_f = ((gl >= H_B) & (gl < 2 * H_B)) | ((gl >= 3 * H_B) & (gl < 4 * H_B))

    def gate_tile(i, carry):
        rows = pl.ds(pl.multiple_of(i * L, L), L)
        x = sm_ref[rows, :] + bm_ref[...]
        y_scr[rows, :] = jnp.where(is_f, _log_sigmoid(x), x)
        return carry

    lax.fori_loop(0, N, gate_tile, 0)


    def state_group(ns, dirs=(0, 1)):
        units = [(d, n if d == 0 else N - 1 - n) for n in ns for d in dirs]
        rows = [_chunk_rows(n) for _, n in units]
        kt_all = [[qk_scr[r, HK + p * LANES:HK + (p + 1) * LANES].T for p in range(n_pairs)] for r in rows]
        yield
        xs = [y_scr[r, :] for r in rows]
        fsum = [_tri_sum(tri[d], x) for (d, _), x in zip(units, xs)]
        yield
        wk_all, f_end, c_end = [], [], []
        for (d, n), r, x, fs in zip(units, rows, xs, fsum):
            y = jnp.where(is_f, fs, x)
            li0 = GATE_LANE0 + 2 * H_B * d
            blk = jnp.concatenate([y, y], axis=0).T[li0:li0 + 2 * H_B, :]
            frow = pltpu.roll(blk, H_B, axis=0)
            grow = blk - frow
            g_scr[d, n] = grow
            f_scr[d, n] = frow
            e_col = L - 1 if d == 0 else 0
            f_end.append(frow[0:H_B, e_col:e_col + 1])
            ce8 = jnp.max(grow, axis=1, keepdims=True)
            c_end.append(ce8[0:H_B, :])
            wk_all.append(jnp.exp(grow[:, 0:L] - ce8))
        yield
        kv_all, ksum_all = [], []
        for r, wk8, kt_u in zip(rows, wk_all, kt_all):
            kv_u, ks_u = [], []
            wk8b = wk8.astype(BF16)
            for p in range(n_pairs):
                kpb = qk_scr[r, HK + p * LANES:HK + (p + 1) * LANES].astype(BF16)
                ks8 = _dot(wk8b, kpb)
                for j in range(2):
                    h = 2 * p + j
                    kwt = (kt_u[p][j * DK:(j + 1) * DK, :] * wk8[h:h + 1, :]).astype(BF16)
                    kv_u.append(_dot(kwt, v_ref[r, h * DV:(h + 1) * DV]))
                    ks_u.append(ks8[h:h + 1, :])
            kv_all.append(kv_u)
            ksum_all.append(ks_u)
        yield
        m_run = {d: m_scr[H_B * d:H_B * (d + 1), 0:1] for d in dirs}
        a_all, b_all = [], []
        for (d, n), fe, ce in zip(units, f_end, c_end):
            mall_scr[d, n, 0:H_B, 0:1] = m_run[d]
            mx = jnp.maximum(m_run[d], ce)
            a_all.append(jnp.exp(m_run[d] - mx))
            b_all.append(jnp.exp(ce - mx))
            m_run[d] = fe + mx
        for d in dirs:
            m_scr[H_B * d:H_B * (d + 1), 0:1] = m_run[d]
        yield
        c_run = {d: [[c_scr[d, p, j * DK:(j + 1) * DK, :] for j in range(2)] for p in range(n_pairs)] for d in dirs}
        n_run = {d: [n_scr[2 * d + p:2 * d + p + 1, :] for p in range(n_pairs)] for d in dirs}
        for (d, n), a4, b4, kv_u, ks_u in zip(units, a_all, b_all, kv_all, ksum_all):
            for p in range(n_pairs):
                nall_scr[d, n, p:p + 1, :] = n_run[d][p]
                a_s = [a4[2 * p + j:2 * p + j + 1, :] for j in range(2)]
                b_s = [b4[2 * p + j:2 * p + j + 1, :] for j in range(2)]
                for j in range(2):
                    cj = c_run[d][p][j]
                    call_scr[d, n, p, j * DK:(j + 1) * DK, :] = cj.astype(BF16)
                    c_run[d][p][j] = a_s[j] * cj + b_s[j] * kv_u[2 * p + j]
                n_run[d][p] = (jnp.where(head_mask[0], a_s[0], a_s[1]) * n_run[d][p]
                               + jnp.where(head_mask[0], b_s[0] * ks_u[2 * p], b_s[1] * ks_u[2 * p + 1]))
        for d in dirs:
            for p in range(n_pairs):
                n_scr[2 * d + p:2 * d + p + 1, :] = n_run[d][p]
                for j in range(2):
                    c_scr[d, p, j * DK:(j + 1) * DK, :] = c_run[d][p][j]

    eye = lower & upper
    ones8 = jnp.ones((SUBLANES, L), BF16)
    sub8 = lax.broadcasted_iota(jnp.int32, (SUBLANES, LANES), 0)
    sub_h = lax.broadcasted_iota(jnp.int32, (H_B, L), 0)
    n_rows = [((sub8 == 2 * p) & head_mask[0]) | ((sub8 == 2 * p + 1) & head_mask[1]) for p in range(n_pairs)]

    def head_rows(vals):
        out = vals[0][0:H_B, :]
        for h in range(1, H_B):
            out = jnp.where(sub_h == h, vals[h][0:H_B, :], out)
        return out

    def out_group(ns):
        chunks = [(d, n) for n in ns for d in range(2)]
        pairs = [(d, n, p) for d, n in chunks for p in range(n_pairs)]
        units = [(d, n, p, j) for d, n, p in pairs for j in range(2)]
        cms = [lane_cummax(g_scr[d, n], d)[0:H_B, 0:L] for d, n in chunks]
        qk2s, qc2s, qn2s = [], [], []
        for d, n, p in pairs:
            r = _chunk_rows(n)
            qp = qk_scr[r, p * LANES:(p + 1) * LANES]
            q2 = jnp.concatenate([jnp.where(head_mask[j], qp, 0.0) for j in range(2)], axis=0).astype(BF16)
            qk2s.append(_dot_nt(q2, qk_scr[r, HK + p * LANES:HK + (p + 1) * LANES].astype(BF16)))
            qc2s.append(_dot(q2, call_scr[d, n, p]))
            nsel = jnp.where(n_rows[p], nall_scr[d, n, p:p + 1, :], 0.0).astype(BF16)
            qn2s.append(_dot_nt(nsel, qp.astype(BF16)))
        yield
        s_all = []
        for ui, (d, n, p, j) in enumerate(units):
            grow = g_scr[d, n, 2 * p + j:2 * p + j + 1, 0:L]
            e = jnp.where(tmask[d], grow, -jnp.inf)
            cmax = jnp.max(e, axis=-1, keepdims=True)
            s_all.append((qk2s[ui // 2][j * L:(j + 1) * L, :] * jnp.exp(e - cmax)).astype(BF16))
        yield
        nums = [_dot(s, v_ref[_chunk_rows(n), (2 * p + j) * DV:(2 * p + j + 1) * DV])
                for (d, n, p, j), s in zip(units, s_all)]
        dens = [_dot_nt(ones8, s) for s in s_all]
        yield
        scales = []
        for ci, (d, n) in enumerate(chunks):
            den_loc = head_rows(dens[ci * H_B:(ci + 1) * H_B])
            qn = qn2s[ci * n_pairs][0:H_B, :]
            for p in range(1, n_pairs):
                qn = qn + qn2s[ci * n_pairs + p][0:H_B, :]
            cm = cms[ci]
            m_prev = mall_scr[d, n, 0:H_B, 0:1]
            delta = cm - m_prev
            t = jnp.exp(-jnp.abs(delta))
            w_loc = jnp.where(delta <= 0.0, t, 1.0)
            w_inter = jnp.where(delta <= 0.0, 1.0, t)
            mt = f_scr[d, n, 0:H_B, 0:L] + jnp.maximum(m_prev, cm)
            den = w_loc * den_loc + w_inter * qn
            rinv = 1.0 / jnp.maximum(jnp.abs(den), jnp.exp(-mt))
            scales.append((w_loc * rinv, w_inter * rinv))
        yield
        hs = []
        for ui, (d, n, p, j) in enumerate(units):
            h = 2 * p + j
            sc_loc, sc_inter = scales[ui // H_B]
            d_loc = jnp.where(eye, sc_loc[h:h + 1, :], 0.0).astype(BF16)
            d_inter = jnp.where(eye, sc_inter[h:h + 1, :], 0.0).astype(BF16)
            hs.append(_dot(d_loc, nums[ui].astype(BF16))
                      + _dot(d_inter, qc2s[ui // 2][j * L:(j + 1) * L, :].astype(BF16)))
        yield
        for ni, n in enumerate(ns):
            r = _chunk_rows(n)
            for h in range(H_B):
                vs = slice(h * DV, (h + 1) * DV)
                o = hs[(2 * ni) * H_B + h] + hs[(2 * ni + 1) * H_B + h]
                out_ref[r, vs] = (_rms(o, gw_ref[:, vs]) * _sigmoid(og_ref[r, vs].astype(F32))).astype(out_ref.dtype)

    def finish():
        if write_state:
            for d in range(2):
                for p in range(n_pairs):
                    cnew_ref[d, p] = c_scr[d, p]
                    for j in range(2):
                        nnew_ref[d, 2 * p + j:2 * p + j + 1, :] = n_scr[2 * d + p:2 * d + p + 1, j * DK:(j + 1) * DK]
                mnew_ref[d:d + 1, :] = to_row(m_scr[H_B * d:H_B * (d + 1), 0:1])

    return state_group, out_group, finish


def _mlstm_scratch(T, C2, grid_w):
    n_pairs = C2 // 2 // LANES
    n_chunks = T // CHUNK
    pad_rows = 2 * (grid_w + SUBLANES) if T // grid_w > 1 else 2 * SUBLANES
    return [
        pltpu.VMEM((T + pad_rows, C2), F32),
        pltpu.VMEM((T, C2), F32),
        pltpu.VMEM((T, SMALL_W), F32),
        pltpu.VMEM((2, n_pairs, LANES, LANES), F32),
        pltpu.VMEM((SUBLANES, LANES), F32),
        pltpu.VMEM((SUBLANES, LANES), F32),
        pltpu.VMEM((2, n_chunks, n_pairs, LANES, LANES), BF16),
        pltpu.VMEM((2, n_chunks, SUBLANES, LANES), F32),
        pltpu.VMEM((2, n_chunks, SUBLANES, LANES), F32),
        pltpu.VMEM((2, n_chunks, SUBLANES, LANES), F32),
        pltpu.VMEM((2, n_chunks, SUBLANES, LANES), F32),
    ]


N_GLA_SCRATCH = 5
N_MLSTM_SCRATCH = 11


def _scan_kernel(*refs, cols, layer, has_state, write_state, n_cast, ride_ada, grid_w, unroll):
    refs = list(refs)
    z_refs = refs[:2]
    del refs[:2]
    s0_ref = c0_ref = n0_ref = m0_ref = None
    if has_state:
        s0_ref, c0_ref, n0_ref, m0_ref = refs[:4]
        del refs[:4]
    wa_ref, bal_ref, gwa_ref, cw_ref, bmg_ref, gwb_ref = refs[:6]
    del refs[:6]
    cast_in = refs[:n_cast]
    del refs[:n_cast]
    if ride_ada:
        ada_in = refs[:4]
        del refs[:4]
    outa_ref, outb_ref = refs[:2]
    del refs[:2]
    snew_ref = cnew_ref = nnew_ref = mnew_ref = None
    if write_state:
        snew_ref, cnew_ref, nnew_ref, mnew_ref = refs[:4]
        del refs[:4]
    cast_out = refs[:n_cast]
    del refs[:n_cast]
    if ride_ada:
        ada_out = refs.pop(0)
    wal_scr, bm_scr = refs[:2]
    del refs[:2]
    gla_scr = refs[:N_GLA_SCRATCH]
    mlstm_scr = refs[N_GLA_SCRATCH:]

    for src, dst in zip(cast_in, cast_out):
        dst[...] = src[...].astype(BF16)
    if ride_ada:
        _ada_tile(*ada_in, ada_out)

    R, HK = wa_ref.shape[1], wa_ref.shape[2]
    wal_scr[...] = jnp.zeros(wal_scr.shape, BF16)
    for d in range(2):
        wal_scr[d * R:(d + 1) * R, d * HK:(d + 1) * HK] = wa_ref[d].astype(BF16)
    lane = lax.broadcasted_iota(jnp.int32, (1, LANES), 1)
    bm = jnp.zeros((1, LANES), F32)
    for g in range(bmg_ref.shape[1]):
        for h in range(H_B):
            bm = jnp.where(lane == GATE_LANE0 + H_B * g + h, bmg_ref[layer, g, h], bm)
    bm_scr[0:1, :] = bm

    def view(name):
        a, c0, w = cols[name]
        return z_refs[a].at[:, pl.ds(c0, w)]

    sm_ref = view("small")
    n_chunks = z_refs[0].shape[0] // CHUNK
    gla = _gla_body(view("qa"), view("ka"), view("va"), view("ga"), sm_ref, s0_ref, wal_scr, bal_ref, gwa_ref,
                    outa_ref, snew_ref, *gla_scr)
    mlstm = _mlstm_body(view("qkb"), view("vb"), view("ob"), sm_ref, c0_ref, n0_ref, m0_ref, cw_ref,
                        bm_scr.at[0:1, :], gwb_ref, outb_ref, cnew_ref, nnew_ref, mnew_ref, *mlstm_scr,
                        grid_w=grid_w)
    gla_state, gla_out, gla_finish, decay_span = gla
    mlstm_state, mlstm_out, mlstm_finish = mlstm

    def passes(gla_out_fn):
        _chunk_loop(n_chunks, unroll, lambda ns: [fn([n], (d,)) for n in ns for d in range(2)
                                                  for fn in (mlstm_state, gla_state)])
        _chunk_loop(n_chunks, unroll, lambda ns: [fn([n]) for n in ns for fn in (mlstm_out, gla_out_fn)])

    wide_decay = decay_span > GLA_FACTORED_DECAY_MAX

    @pl.when(jnp.logical_not(wide_decay))
    def _():
        passes(gla_out)

    @pl.when(wide_decay)
    def _():
        passes(functools.partial(gla_out, exact_decay=True))

    gla_finish()
    mlstm_finish()


def _scan_call(z2d, row0, B, T, states, lw, layer, *, grid_w, write_state, casts=(), ada=None):
    assert row0 % T == 0 and all(z.shape[0] % T == 0 for z in z2d)
    z3 = [z.reshape(z.shape[0] // T, T, z.shape[1]) for z in z2d]
    blk0 = row0 // T
    HK = lw["w_alpha2"].shape[-1]
    DA = lw["gnorm_a_w"].shape[0]
    C2 = lw["conv_w"].shape[-1]
    DB = lw["gnorm_b_w"].shape[0]
    DK_A, DK_B = HK // H_A, C2 // 2 // H_B
    pa, pb = HK // LANES, C2 // 2 // LANES
    n_chunks = T // CHUNK
    has_state = states is not None
    widths = ((("qa", HK), ("ka", HK), ("qkb", C2), ("small", SMALL_W)),
              (("va", DA), ("ga", DA), ("vb", DB), ("ob", DB)))
    cols = {}
    for a, groups in enumerate(widths):
        c0 = 0
        for name, w in groups:
            cols[name] = (a, c0, w)
            c0 += w
        assert c0 == z3[a].shape[2]
    cast_in_specs, cast_out_specs, cast_out_shape, cast_args = _cast_specs(casts, B)
    kern = functools.partial(_scan_kernel, cols=cols, layer=layer, has_state=has_state, write_state=write_state,
                             n_cast=len(casts), ride_ada=ada is not None, grid_w=grid_w,
                             unroll=min(n_chunks, SCAN_UNROLL))

    def per_batch(shape):
        nd = len(shape)
        return pl.BlockSpec((None,) + tuple(shape), lambda b: (b,) + (0,) * nd)

    def per_batch_layer(shape):
        nd = len(shape)
        return pl.BlockSpec((None, None) + tuple(shape), lambda b: (b, layer) + (0,) * nd)

    def of_layer(a):
        return pl.BlockSpec((None,) + a.shape[1:], lambda b: (layer,) + (0,) * (a.ndim - 1))

    def whole(a):
        return pl.BlockSpec(a.shape, lambda b: (0,) * a.ndim)

    state_shapes = ((2, pa, LANES, LANES), (2, pb, LANES, LANES), (2, H_B, DK_B), (2, H_B))
    in_specs = [pl.BlockSpec((None, T, z.shape[2]), lambda b: (b + blk0, 0, 0)) for z in z3]
    args = list(z3)
    if has_state:
        s_gla, s_c, s_n, s_m = states
        depth = s_gla.shape[1]
        args += [s_gla.reshape((B, depth) + state_shapes[0]), s_c.reshape((B, depth) + state_shapes[1]), s_n, s_m]
        in_specs += [per_batch_layer(s) for s in state_shapes]
    args += [lw["w_alpha2"], lw["b_alpha"], lw["gnorm_a_w"].reshape(1, DA), lw["conv_w"], lw["b_mgate"],
             lw["gnorm_b_w"].reshape(1, DB)]
    in_specs += [of_layer(lw["w_alpha2"]), of_layer(lw["b_alpha"]), pl.BlockSpec((1, DA), lambda b: (0, 0)),
                 whole(lw["conv_w"]), pl.BlockSpec(memory_space=pltpu.SMEM), pl.BlockSpec((1, DB), lambda b: (0, 0))]
    args += cast_args
    in_specs += cast_in_specs
    out_specs = [per_batch((T, DA)), per_batch((T, DB))]
    out_shape = [jax.ShapeDtypeStruct((B, T, DA), BF16), jax.ShapeDtypeStruct((B, T, DB), BF16)]
    if write_state:
        out_specs += [per_batch(s) for s in state_shapes]
        out_shape += [jax.ShapeDtypeStruct((B,) + s, F32) for s in state_shapes]
    out_specs += cast_out_specs
    out_shape += cast_out_shape
    if ada is not None:
        cc, c, w_ada, b_ada, col0 = ada
        n_rest = w_ada.shape[1] - col0
        wcol = n_rest // B
        assert n_rest % B == 0 and wcol % LANES == 0 and col0 % wcol == 0
        args += [cc, c, w_ada, b_ada]
        in_specs += [whole(cc), whole(c),
                     pl.BlockSpec((w_ada.shape[0], wcol), lambda b: (0, col0 // wcol + b)),
                     pl.BlockSpec((1, wcol), lambda b: (0, col0 // wcol + b))]
        out_specs.append(pl.BlockSpec((COND_ROWS, wcol), lambda b: (0, b)))
        out_shape.append(jax.ShapeDtypeStruct((COND_ROWS, n_rest), F32))
    scratch = ([pltpu.VMEM((SMALL_W, 2 * HK), BF16), pltpu.VMEM((SUBLANES, LANES), F32)]
               + _gla_scratch(T, HK) + _mlstm_scratch(T, C2, grid_w))
    assert len(scratch) == 2 + N_GLA_SCRATCH + N_MLSTM_SCRATCH
    return pl.pallas_call(
        kern,
        grid=(B,),
        in_specs=in_specs,
        out_specs=out_specs,
        out_shape=out_shape,
        scratch_shapes=scratch,
        compiler_params=pltpu.CompilerParams(dimension_semantics=("arbitrary",),
                                             vmem_limit_bytes=VMEM_LIMIT),
        name="mixer_scans",
    )(*args)


def _outff_kernel(xc_ref, xl_ref, ac_ref, al_ref, bc_ref, bl_ref, mod_ref, n2_ref, fn_ref, wo_ref, w1_ref, w2_ref,
                  yc_ref, yl_ref, *, n_ctx, tiles_per_req, ff_chunk, final_norm):
    D = xc_ref.shape[1]
    DA = ac_ref.shape[1]
    is_ctx, row = _tile_group(n_ctx, tiles_per_req)

    def mod(k):
        return mod_ref[pl.ds(row, 1), (k - MOD_SPLIT) * D:(k - MOD_SPLIT + 1) * D]

    def tile(x_ref, a_ref, b_ref, y_ref):
        y = _dot(a_ref[...], wo_ref[0:DA, :]) + _dot(b_ref[...], wo_ref[DA:, :])
        x1 = x_ref[...] + mod(2) * y
        h2 = (_rms(x1, n2_ref[...]) * (1.0 + mod(4)) + mod(3)).astype(BF16)
        acc = jnp.zeros(x1.shape, F32)
        for c0 in range(0, w1_ref.shape[1], ff_chunk):
            u = jnp.maximum(_dot(h2, w1_ref[:, c0:c0 + ff_chunk]), 0.0)
            acc = acc + _dot((u * u).astype(BF16), w2_ref[c0:c0 + ff_chunk, :])
        x2 = x1 + mod(5) * acc
        y_ref[...] = _rms(x2, fn_ref[...]) if final_norm else x2

    @pl.when(is_ctx)
    def _():
        tile(xc_ref, ac_ref, bc_ref, yc_ref)

    @pl.when(jnp.logical_not(is_ctx))
    def _():
        tile(xl_ref, al_ref, bl_ref, yl_ref)


def _outff_call(xc2d, xl2d, ac, al, bc, bl, mod, norm2_w, final_w, wo, w1, w2, *, tm, tiles_per_req, final_norm):
    (Mc, D), Ml = xc2d.shape, xl2d.shape[0]
    n_ctx = Mc // tm
    DA = ac.shape[1]
    DFF = w1.shape[1]
    kern = functools.partial(_outff_kernel, n_ctx=n_ctx, tiles_per_req=tiles_per_req, ff_chunk=FF_CHUNK,
                             final_norm=final_norm)
    once = pl.Buffered(1)
    ctx, lat = _ctx_tile(n_ctx), _lat_tile(n_ctx)
    return pl.pallas_call(
        kern,
        grid=((Mc + Ml) // tm,),
        in_specs=[
            pl.BlockSpec((tm, D), ctx), pl.BlockSpec((tm, D), lat),
            pl.BlockSpec((tm, DA), ctx), pl.BlockSpec((tm, DA), lat),
            pl.BlockSpec((tm, D - DA), ctx), pl.BlockSpec((tm, D - DA), lat),
            pl.BlockSpec(mod.shape, lambda i: (0, 0)),
            pl.BlockSpec((1, D), lambda i: (0, 0)),
            pl.BlockSpec((1, D), lambda i: (0, 0)),
            pl.BlockSpec((D, D), lambda i: (0, 0), pipeline_mode=once),
            pl.BlockSpec((D, DFF), lambda i: (0, 0), pipeline_mode=once),
            pl.BlockSpec((DFF, D), lambda i: (0, 0), pipeline_mode=once),
        ],
        out_specs=[pl.BlockSpec((tm, D), ctx), pl.BlockSpec((tm, D), lat)],
        out_shape=[jax.ShapeDtypeStruct((Mc, D), F32), jax.ShapeDtypeStruct((Ml, D), F32)],
        compiler_params=pltpu.CompilerParams(dimension_semantics=("arbitrary",),
                                             vmem_limit_bytes=VMEM_LIMIT),
        name="outproj_mlp",
    )(xc2d, xl2d, ac, al, bc, bl, mod, norm2_w.reshape(1, D), final_w.reshape(1, D), wo, w1, w2)


def _layer(xc, xl, cond, ada_w, cached, lw, layer, ffw, final_w, final_norm):
    (Bc, Tc, D), (Bl, Tl, _) = xc.shape, xl.shape
    tm = TOKEN_TILE
    assert (Bc * Tc) % tm == 0 and Tl % tm == 0 and (Bc * Tc) % Tl == 0
    xc2d, xl2d = xc.reshape(Bc * Tc, D), xl.reshape(Bl * Tl, D)
    z = _inproj_call(xc2d, xl2d, *cond, *ada_w, lw["norm1_w"], lw["w_in_t"], tm=tm, tiles_per_req=Tl // tm,
                     f32_rows=lw["f32_rows"], small_rows=lw["small_rows"], bf16_rows=lw["bf16_rows"])
    res_c = _scan_call(z, 0, Bc, Tc, None, lw, layer, grid_w=Tc, write_state=True,
                       casts=((ffw[0], 0), (ffw[1], 0), (ffw[2], 0)), ada=(*cond, *ada_w, MOD_SPLIT * D))
    res_l = _scan_call(z, Bc * Tc, Bl, Tl, cached, lw, layer, grid_w=GRID_W, write_state=False)
    wo_b, w1_b, w2_b, mod_out = res_c[-4:]
    yc, yl = _outff_call(xc2d, xl2d, res_c[0].reshape(Bc * Tc, -1), res_l[0].reshape(Bl * Tl, -1),
                         res_c[1].reshape(Bc * Tc, -1), res_l[1].reshape(Bl * Tl, -1), mod_out, lw["norm2_w"],
                         final_w, wo_b, w1_b, w2_b, tm=tm, tiles_per_req=Tl // tm, final_norm=final_norm)
    return yc.reshape(Bc, Tc, D), yl.reshape(Bl, Tl, D), tuple(res_c[2:6])


def _layer_weights(l, norm1_w, norm2_w, w_in, w_alpha2, b_alpha, b_mgate, conv_w, gnorm_a_w, gnorm_b_w):
    hk_a = w_alpha2.shape[-1]
    d_a = gnorm_a_w.shape[-1]
    d_b = gnorm_b_w.shape[-1]
    hk_b = conv_w.shape[-1] // 2
    sizes = (hk_a, hk_a, d_a, d_a, 2 * R_ALPHA, hk_b, hk_b, d_b, d_b, 4 * H_B)
    assert w_alpha2.shape[2] == R_ALPHA and b_mgate.shape[1] * b_mgate.shape[2] == 4 * H_B
    offs = [0]
    for s in sizes:
        offs.append(offs[-1] + s)
    f32_rows = ((offs[0], offs[2] - offs[0]), (offs[5], offs[7] - offs[5]))
    small_rows = ((offs[4], offs[5] - offs[4]), (offs[9], offs[10] - offs[9]))
    bf16_rows = ((offs[2], offs[4] - offs[2]), (offs[7], offs[9] - offs[7]))
    assert all(n % LANES == 0 and r % BF16_ROWS == 0 for r, n in f32_rows + bf16_rows)
    return dict(
        norm1_w=norm1_w[l], norm2_w=norm2_w[l], w_in_t=jnp.swapaxes(w_in[l], 0, 1),
        f32_rows=f32_rows, small_rows=small_rows, bf16_rows=bf16_rows,
        w_alpha2=w_alpha2, b_alpha=b_alpha, b_mgate=b_mgate, conv_w=conv_w[l],
        gnorm_a_w=gnorm_a_w[l], gnorm_b_w=gnorm_b_w[l],
    )


def kernel(x_prompt, x_sample, c, state_gla, state_mlstm_C, state_mlstm_n, state_mlstm_m, c_ctx, w_ada, b_ada, norm1_w, norm2_w, w_in, w_alpha2, b_alpha, b_mgate, conv_w, gnorm_a_w, gnorm_b_w, w_out, w_ff1, w_ff2, final_norm_w):
    depth = w_in.shape[0]
    D = x_prompt.shape[-1]
    Bp, Tp, _ = x_prompt.shape
    Bs = x_sample.shape[0]
    assert 1 + Bs <= COND_ROWS
    cond = (c_ctx.reshape(1, D), c)
    cached = (state_gla, state_mlstm_C, state_mlstm_n, state_mlstm_m)
    xp, xs = x_prompt, x_sample
    s_gla, s_c, s_n, s_m = [], [], [], []
    for l in range(depth):
        lw = _layer_weights(l, norm1_w, norm2_w, w_in, w_alpha2, b_alpha, b_mgate, conv_w,
                            gnorm_a_w, gnorm_b_w)
        xp, xs, ctx = _layer(xp, xs, cond, (w_ada[l], b_ada[l].reshape(1, -1)), cached, lw, l,
                             (w_out[l], w_ff1[l], w_ff2[l]), final_norm_w, l == depth - 1)
        s_gla.append(ctx[0].reshape(Bp, 2, H_A, -1, ctx[0].shape[-1]))
        s_c.append(ctx[1].reshape(Bp, 2, H_B, -1, ctx[1].shape[-1]))
        s_n.append(ctx[2])
        s_m.append(ctx[3])
    dt = x_prompt.dtype
    return (xp, xs, jnp.stack(s_gla, axis=1).astype(dt), jnp.stack(s_c, axis=1).astype(dt),
            jnp.stack(s_n, axis=1).astype(dt), jnp.stack(s_m, axis=1).astype(dt))
```

```python
import functools
import math

import jax
import jax.numpy as jnp
from jax import lax
from jax.experimental import pallas as pl
from jax.experimental.pallas import tpu as pltpu

F32 = jnp.float32
BF16 = jnp.bfloat16

GRID_W = 64
H_A = 4
H_B = 4
R_ALPHA = 16
TAU_GLA = 16.0
CHUNK = 64
EPS = 1e-6
LANES = 128
SUBLANES = 8
BF16_ROWS = 16
COND_ROWS = SUBLANES
SMALL_W = LANES
GATE_LANE0 = 2 * R_ALPHA
VMEM_LIMIT = 56 * 1024 * 1024
SCAN_UNROLL = 4
PIPELINE_STARTS = 8
TOKEN_TILE = 512
FF_CHUNK = 512
GLA_FACTORED_DECAY_MAX = 60.0
MOD_SPLIT = 2


def _sigmoid(x):
    return 1.0 / (1.0 + jnp.exp(-x))


def _silu(x):
    return x * _sigmoid(x)


def _log_sigmoid(x):
    return jnp.minimum(x, 0.0) - jnp.log(1.0 + jnp.exp(-jnp.abs(x)))


def _dot(a, b):
    return jnp.dot(a, b, preferred_element_type=F32)


def _dot_nt(a, b):
    return lax.dot_general(a, b, (((1,), (1,)), ((), ())), preferred_element_type=F32)


def _rms(x, w):
    return x * lax.rsqrt(jnp.mean(x * x, axis=-1, keepdims=True) + EPS) * w


def _tri_sum(tri, x, terms=3):
    acc, rest = None, x
    for t in range(terms):
        part = rest.astype(BF16)
        prod = _dot(tri, part)
        acc = prod if acc is None else acc + prod
        if t + 1 < terms:
            rest = rest - part.astype(F32)
    return acc


def _chunk_masks(L):
    row = lax.broadcasted_iota(jnp.int32, (L, L), 0)
    col = lax.broadcasted_iota(jnp.int32, (L, L), 1)
    lower = row >= col
    upper = row <= col
    return lower, upper


def _ada_tile(cc_ref, c_ref, w_ref, b_ref, o_ref):
    D = cc_ref.shape[1]
    sub = lax.broadcasted_iota(jnp.int32, (COND_ROWS, D), 0)
    cond = jnp.where(sub == 0, cc_ref[...], 0.0)
    for r in range(c_ref.shape[0]):
        cond = jnp.where(sub == 1 + r, c_ref[r:r + 1, :], cond)
    o_ref[...] = _dot(_silu(cond).astype(BF16), w_ref[...].astype(BF16)) + b_ref[...]


def _tile_group(n_ctx, tiles_per_req):
    i = pl.program_id(0)
    is_ctx = i < n_ctx
    row = jnp.where(is_ctx, 0, 1 + jnp.maximum(i - n_ctx, 0) // tiles_per_req)
    return is_ctx, row


def _ctx_tile(n_ctx):
    return lambda i: (jnp.minimum(i, n_ctx - 1), 0)


def _lat_tile(n_ctx):
    return lambda i: (jnp.maximum(i - n_ctx, 0), 0)


def _inproj_kernel(xc_ref, xl_ref, cc_ref, c_ref, wa_ref, ba_ref, nw_ref, wt_ref, zf_ref, zh_ref, wb_scr, mod_ref,
                   *, n_ctx, tiles_per_req, f32_rows, small_rows, bf16_rows):
    D = xc_ref.shape[1]
    n_f32 = zf_ref.shape[1]

    @pl.when(pl.program_id(0) == 0)
    def _():
        _ada_tile(cc_ref, c_ref, wa_ref, ba_ref, mod_ref)

        def wide(rows, col):
            for r0, n in rows:
                wb_scr[col:col + n, :] = wt_ref[r0:r0 + n, :].astype(BF16)
                col += n
            return col

        col = wide(f32_rows, 0)
        parts = [wt_ref[r0:r0 + n, :] for r0, n in small_rows]
        n_small = sum(n for _, n in small_rows)
        parts.append(jnp.zeros((SMALL_W - n_small, D), F32))
        wb_scr[col:col + SMALL_W, :] = jnp.concatenate(parts, axis=0).astype(BF16)
        wide(bf16_rows, col + SMALL_W)

    is_ctx, row = _tile_group(n_ctx, tiles_per_req)

    def tile(x_ref):
        sh1 = mod_ref[pl.ds(row, 1), 0:D]
        sc1 = mod_ref[pl.ds(row, 1), D:2 * D]
        h = (_rms(x_ref[...], nw_ref[...]) * (1.0 + sc1) + sh1).astype(BF16)
        zf_ref[...] = _dot_nt(h, wb_scr[0:n_f32, :])
        zh_ref[...] = _dot_nt(h, wb_scr[n_f32:, :]).astype(BF16)

    @pl.when(is_ctx)
    def _():
        tile(xc_ref)

    @pl.when(jnp.logical_not(is_ctx))
    def _():
        tile(xl_ref)


def _inproj_call(xc2d, xl2d, cc, c, w_ada, b_ada, norm_w, w_in_t, *, tm, tiles_per_req, f32_rows, small_rows,
                 bf16_rows):
    (Mc, D), Ml = xc2d.shape, xl2d.shape[0]
    n_ctx = Mc // tm
    n_f32 = sum(n for _, n in f32_rows) + SMALL_W
    n_bf16 = sum(n for _, n in bf16_rows)
    n_out = n_f32 + n_bf16
    n_mod = MOD_SPLIT * D
    kern = functools.partial(_inproj_kernel, n_ctx=n_ctx, tiles_per_req=tiles_per_req,
                             f32_rows=f32_rows, small_rows=small_rows, bf16_rows=bf16_rows)
    once = pl.Buffered(1)
    return pl.pallas_call(
        kern,
        grid=((Mc + Ml) // tm,),
        in_specs=[
            pl.BlockSpec((tm, D), _ctx_tile(n_ctx)),
            pl.BlockSpec((tm, D), _lat_tile(n_ctx)),
            pl.BlockSpec(cc.shape, lambda i: (0, 0)),
            pl.BlockSpec(c.shape, lambda i: (0, 0)),
            pl.BlockSpec((D, n_mod), lambda i: (0, 0), pipeline_mode=once),
            pl.BlockSpec((1, n_mod), lambda i: (0, 0)),
            pl.BlockSpec((1, D), lambda i: (0, 0)),
            pl.BlockSpec(w_in_t.shape, lambda i: (0, 0), pipeline_mode=once),
        ],
        out_specs=[pl.BlockSpec((tm, n_f32), lambda i: (i, 0)), pl.BlockSpec((tm, n_bf16), lambda i: (i, 0))],
        out_shape=[jax.ShapeDtypeStruct((Mc + Ml, n_f32), F32), jax.ShapeDtypeStruct((Mc + Ml, n_bf16), BF16)],
        scratch_shapes=[pltpu.VMEM((n_out, D), BF16), pltpu.VMEM((COND_ROWS, n_mod), F32)],
        compiler_params=pltpu.CompilerParams(dimension_semantics=("arbitrary",),
                                             vmem_limit_bytes=VMEM_LIMIT),
        name="norm_inproj",
    )(xc2d, xl2d, cc, c, w_ada, b_ada, norm_w.reshape(1, D), w_in_t)


def _chunk_loop(n_chunks, unroll, make_units):
    def step(ns):
        pending = list(make_units(ns))
        active = []
        while pending or active:
            for _ in range(min(PIPELINE_STARTS, len(pending))):
                active.append(pending.pop(0))
            alive = []
            for g in active:
                try:
                    next(g)
                    alive.append(g)
                except StopIteration:
                    pass
            active = alive

    if unroll >= n_chunks:
        step(list(range(n_chunks)))
        return

    def body(i, carry):
        step([i * unroll + u for u in range(unroll)])
        return carry

    lax.fori_loop(0, n_chunks // unroll, body, 0)


def _chunk_rows(n):
    if isinstance(n, int):
        return pl.ds(n * CHUNK, CHUNK)
    return pl.ds(pl.multiple_of(n * CHUNK, CHUNK), CHUNK)


def _cast_specs(casts, n_steps):
    in_specs, out_specs, out_shape, args = [], [], [], []
    for w, axis in casts:
        blk = list(w.shape)
        assert blk[axis] % n_steps == 0
        blk[axis] //= n_steps
        assert blk[0] % BF16_ROWS == 0 and blk[1] % LANES == 0
        idx = (lambda b: (b, 0)) if axis == 0 else (lambda b: (0, b))
        in_specs.append(pl.BlockSpec(tuple(blk), idx))
        out_specs.append(pl.BlockSpec(tuple(blk), idx))
        out_shape.append(jax.ShapeDtypeStruct(w.shape, BF16))
        args.append(w)
    return in_specs, out_specs, out_shape, args


def _gla_body(q_ref, k_ref, v_ref, g_ref, sm_ref, s0_ref, wal_ref, bal_ref, gw_ref, out_ref, snew_ref,
              st_scr, sall_scr, qh_scr, qs_scr, kh_scr):
    has_state = s0_ref is not None
    write_state = snew_ref is not None
    T = q_ref.shape[0]
    L = CHUNK
    N = T // L
    HK = q_ref.shape[1]
    DK = HK // H_A
    DV = v_ref.shape[1] // H_A
    scale = DK ** -0.5
    n_pairs = HK // LANES

    lower, upper = _chunk_masks(L)
    tri = (lower.astype(BF16), upper.astype(BF16))
    tmask = (lower, upper)
    lane = lax.broadcasted_iota(jnp.int32, (1, LANES), 1)
    head_mask = (lane < DK, lane >= DK)

    for d in range(2):
        for p in range(n_pairs):
            if has_state:
                st_scr[d, p] = s0_ref[d, p].T
            else:
                st_scr[d, p] = jnp.zeros((LANES, LANES), F32)

    def decay_pre(d, r):
        return _dot(sm_ref[r, :].astype(BF16), wal_ref[:, d * HK:(d + 1) * HK]) + bal_ref[d:d + 1, :]

    neg_pre = jnp.maximum(-(_dot(sm_ref[...].astype(BF16), wal_ref[...])
                            + jnp.concatenate([bal_ref[0:1, :], bal_ref[1:2, :]], axis=1)), 0.0)
    chunk_sums = jnp.sum(neg_pre.reshape(N, L, 2 * HK), axis=1)
    decay_span = (jnp.max(chunk_sums) + L * math.log(2.0)) * (1.0 / TAU_GLA)

    def state_group(ns, dirs=(0, 1)):
        units = [(d, n if d == 0 else N - 1 - n) for n in ns for d in dirs]
        rows = [_chunk_rows(n) for _, n in units]
        vt_all = [[jnp.concatenate([v_ref[r, (2 * p + j) * DV:(2 * p + j + 1) * DV] for j in range(2)],
                                   axis=0).astype(F32).T.astype(BF16) for p in range(n_pairs)] for r in rows]
        yield
        pre = [decay_pre(d, r) for (d, _), r in zip(units, rows)]
        yield
        g = [_log_sigmoid(x) * (1.0 / TAU_GLA) for x in pre]
        yield
        b = [_tri_sum(tri[d], gi, terms=2) for (d, _), gi in zip(units, g)]
        yield
        ks_all, dec_all = [], []
        for (d, _), r, bi in zip(units, rows, b):
            bend = bi[L - 1:L, :] if d == 0 else bi[0:1, :]
            q = q_ref[r, :] * scale
            ks = (k_ref[r, :] * jnp.exp(bend - bi)).astype(BF16)
            qs = q * jnp.exp(bi)
            qh_scr[d, r, :] = (qs * jnp.exp(-bend)).astype(BF16)
            qs_scr[d, r, :] = qs.astype(BF16)
            kh_scr[d, r, :] = ks
            ks_all.append(ks)
            dec_all.append(jnp.exp(bend))
        yield
        upd_all = []
        for vt_u, ks in zip(vt_all, ks_all):
            upd_u = []
            for p in range(n_pairs):
                kp = ks[:, p * LANES:(p + 1) * LANES]
                kk = jnp.concatenate([jnp.where(head_mask[j], kp, jnp.zeros_like(kp)) for j in range(2)], axis=0)
                upd_u.append(_dot(vt_u[p], kk))
            upd_all.append(upd_u)
        yield
        st = {d: [st_scr[d, p] for p in range(n_pairs)] for d in dirs}
        for (d, n), dec, upd in zip(units, dec_all, upd_all):
            for p in range(n_pairs):
                sall_scr[d, n, p] = st[d][p].astype(BF16)
                st[d][p] = st[d][p] * dec[:, p * LANES:(p + 1) * LANES] + upd[p]
        for d in dirs:
            for p in range(n_pairs):
                st_scr[d, p] = st[d][p]

    def stack_heads(x):
        return jnp.concatenate([jnp.where(head_mask[j], x, jnp.zeros_like(x)) for j in range(2)], axis=0)

    tok = lax.broadcasted_iota(jnp.int32, (L, 1), 0)
    row_t = lax.broadcasted_iota(jnp.int32, (2 * L, L), 0) & (L - 1)
    col_s = lax.broadcasted_iota(jnp.int32, (2 * L, L), 1)

    def exact_scores(d, r, p):
        ls = slice(p * LANES, (p + 1) * LANES)
        b = _tri_sum(tri[d], _log_sigmoid(decay_pre(d, r)[:, ls]) * (1.0 / TAU_GLA))
        q = q_ref[r, ls] * scale
        k = k_ref[r, ls]
        acc = jnp.where(row_t == col_s, _dot_nt(stack_heads(q).astype(BF16), k.astype(BF16)), 0.0)
        src = lax.broadcasted_iota(jnp.int32, (L, L), 1)
        h = L // 2
        while h >= 1:
            first = tok & ~(2 * h - 1)
            edge = first + (h - 1 if d == 0 else h)
            b_edge = _tri_sum((src == edge).astype(BF16), b)
            upper = (tok & (2 * h - 1)) >= h
            later, earlier = (upper, ~upper) if d == 0 else (~upper, upper)
            qt = jnp.where(later, q * jnp.exp(b - b_edge), 0.0)
            kt = jnp.where(earlier, k * jnp.exp(b_edge - b), 0.0)
            sc = _dot_nt(stack_heads(qt).astype(BF16), kt.astype(BF16))
            acc = acc + jnp.where((row_t & ~(2 * h - 1)) == (col_s & ~(2 * h - 1)), sc, 0.0)
            h //= 2
        return acc

    def out_group(ns, exact_decay=False):
        pairs = [(d, ni, p) for ni in range(len(ns)) for d in range(2) for p in range(n_pairs)]
        scores, inter = [], []
        for d, ni, p in pairs:
            r = _chunk_rows(ns[ni])
            ls = slice(p * LANES, (p + 1) * LANES)
            if exact_decay:
                scores.append(exact_scores(d, r, p))
            else:
                scores.append(_dot_nt(stack_heads(qh_scr[d, r, ls]), kh_scr[d, r, ls]))
            inter.append(_dot_nt(stack_heads(qs_scr[d, r, ls]), sall_scr[d, ns[ni], p]))
        yield
        probs = [[jnp.where(tmask[d], sc[j * L:(j + 1) * L, :], 0.0).astype(BF16) for j in range(2)]
                 for (d, _, _), sc in zip(pairs, scores)]
        yield
        outs = {}
        for (d, ni, p), pr, it in zip(pairs, probs, inter):
            r = _chunk_rows(ns[ni])
            for j in range(2):
                vs = slice((2 * p + j) * DV, (2 * p + j + 1) * DV)
                outs[(d, ni, 2 * p + j)] = _dot(pr[j], v_ref[r, vs]) + it[j * L:(j + 1) * L, :]
        yield
        for ni, n in enumerate(ns):
            r = _chunk_rows(n)
            for h in range(H_A):
                vs = slice(h * DV, (h + 1) * DV)
                o = outs[(0, ni, h)] + outs[(1, ni, h)]
                out_ref[r, vs] = (_rms(o, gw_ref[:, vs]) * _silu(g_ref[r, vs].astype(F32))).astype(out_ref.dtype)

    def finish():
        if write_state:
            for d in range(2):
                for p in range(n_pairs):
                    snew_ref[d, p] = st_scr[d, p].T

    return state_group, out_group, finish, decay_span


def _gla_scratch(T, HK):
    n_pairs = HK // LANES
    n_chunks = T // CHUNK
    return [
        pltpu.VMEM((2, n_pairs, LANES, LANES), F32),
        pltpu.VMEM((2, n_chunks, n_pairs, LANES, LANES), BF16),
        pltpu.VMEM((2, T, HK), BF16),
        pltpu.VMEM((2, T, HK), BF16),
        pltpu.VMEM((2, T, HK), BF16),
    ]


def _mlstm_body(qk_ref, v_ref, og_ref, sm_ref, c0_ref, n0_ref, m0_ref, cw_ref, bm_ref, gw_ref,
                out_ref, cnew_ref, nnew_ref, mnew_ref,
                pad_scr, qk_scr, y_scr, c_scr, n_scr, m_scr, call_scr, nall_scr, mall_scr, g_scr, f_scr,
                *, grid_w):
    has_state = c0_ref is not None
    write_state = cnew_ref is not None
    T = qk_ref.shape[0]
    L = CHUNK
    N = T // L
    C2 = qk_ref.shape[1]
    HK = C2 // 2
    DK = HK // H_B
    DV = v_ref.shape[1] // H_B
    scale = DK ** -0.5
    n_pairs = HK // LANES
    P = pad_scr.shape[0] - T
    P0 = P // 2
    rows_img = T // grid_w

    lower, upper = _chunk_masks(L)
    tri = (lower.astype(BF16), upper.astype(BF16))
    tmask = (lower, upper)
    lane = lax.broadcasted_iota(jnp.int32, (1, LANES), 1)
    head_mask = (lane < DK, lane >= DK)
    lane_in = lane & (L - 1)

    def lane_cummax(x, d):
        k = 1
        while k < L:
            if d == 0:
                x = jnp.maximum(x, jnp.where(lane_in >= k, pltpu.roll(x, k, axis=1), -jnp.inf))
            else:
                x = jnp.maximum(x, jnp.where(lane_in < L - k, pltpu.roll(x, LANES - k, axis=1), -jnp.inf))
            k *= 2
        return x

    for d in range(2):
        for p in range(n_pairs):
            if has_state:
                c_scr[d, p] = c0_ref[d, p]
                n_scr[2 * d + p:2 * d + p + 1, :] = jnp.concatenate(
                    [n0_ref[d, 2 * p + j:2 * p + j + 1, :] for j in range(2)], axis=1)
            else:
                c_scr[d, p] = jnp.zeros((LANES, LANES), F32)
                n_scr[2 * d + p:2 * d + p + 1, :] = jnp.zeros((1, LANES), F32)
    eye_h = (lax.broadcasted_iota(jnp.int32, (H_B, H_B), 0) == lax.broadcasted_iota(jnp.int32, (H_B, H_B), 1))

    def to_col(row):
        return jnp.sum(jnp.where(eye_h, row, 0.0), axis=1, keepdims=True)

    def to_row(col):
        return jnp.sum(jnp.where(eye_h, col, 0.0), axis=0, keepdims=True)

    for d in range(2):
        if has_state:
            m_scr[H_B * d:H_B * (d + 1), 0:1] = to_col(m0_ref[d:d + 1, :])
        else:
            m_scr[H_B * d:H_B * (d + 1), 0:1] = jnp.zeros((H_B, 1), F32)

    pad_scr[0:P0, :] = jnp.zeros((P0, C2), F32)
    pad_scr[P0 + T:P + T, :] = jnp.zeros((P - P0, C2), F32)

    def copy_in(i, carry):
        r0 = pl.multiple_of(i * L, L)
        pad_scr[pl.ds(P0 + r0, L), :] = qk_ref[pl.ds(r0, L), :]
        return carry

    lax.fori_loop(0, N, copy_in, 0)

    lane_c = lax.broadcasted_iota(jnp.int32, (1, C2), 1)
    qscale = jnp.where(lane_c < HK, scale, 1.0).astype(F32)
    sub = lax.broadcasted_iota(jnp.int32, (L, 1), 0)
    img_rows = (0,) if rows_img == 1 else (-1, 0, 1)

    def conv_tile(i, carry):
        r0 = pl.multiple_of(i * L, L)
        col = lax.rem(r0, grid_w) + sub
        ok_left = col >= 1
        ok_right = col <= grid_w - 2
        sums = [None, None, None]
        for di in img_rows:
            blk = pad_scr[pl.ds(P0 + r0 + di * grid_w - SUBLANES, L + 2 * SUBLANES), :]
            for k in range(3):
                term = blk * cw_ref[di + 1, k:k + 1, :]
                sums[k] = term if sums[k] is None else sums[k] + term
        S = SUBLANES
        acc = (sums[1][S:S + L, :] + jnp.where(ok_left, sums[0][S - 1:S - 1 + L, :], 0.0)
               + jnp.where(ok_right, sums[2][S + 1:S + 1 + L, :], 0.0))
        qk_scr[pl.ds(r0, L), :] = _silu(acc) * qscale
        return carry

    lax.fori_loop(0, N, conv_tile, 0)

    gl = lane - GATE_LANE0
    is_f = ((gl >= H_B) & (gl < 2 * H_B)) | ((gl >= 3 * H_B) & (gl < 4 * H_B))

    def gate_tile(i, carry):
        rows = pl.ds(pl.multiple_of(i * L, L), L)
        x = sm_ref[rows, :] + bm_ref[...]
        y_scr[rows, :] = jnp.where(is_f, _log_sigmoid(x), x)
        return carry

    lax.fori_loop(0, N, gate_tile, 0)


    def state_group(ns, dirs=(0, 1)):
        units = [(d, n if d == 0 else N - 1 - n) for n in ns for d in dirs]
        rows = [_chunk_rows(n) for _, n in units]
        kt_all = [[qk_scr[r, HK + p * LANES:HK + (p + 1) * LANES].T for p in range(n_pairs)] for r in rows]
        yield
        xs = [y_scr[r, :] for r in rows]
        fsum = [_tri_sum(tri[d], x) for (d, _), x in zip(units, xs)]
        yield
        wk_all, f_end, c_end = [], [], []
        for (d, n), r, x, fs in zip(units, rows, xs, fsum):
            y = jnp.where(is_f, fs, x)
            li0 = GATE_LANE0 + 2 * H_B * d
            blk = jnp.concatenate([y, y], axis=0).T[li0:li0 + 2 * H_B, :]
            frow = pltpu.roll(blk, H_B, axis=0)
            grow = blk - frow
            g_scr[d, n] = grow
            f_scr[d, n] = frow
            e_col = L - 1 if d == 0 else 0
            f_end.append(frow[0:H_B, e_col:e_col + 1])
            ce8 = jnp.max(grow, axis=1, keepdims=True)
            c_end.append(ce8[0:H_B, :])
            wk_all.append(jnp.exp(grow[:, 0:L] - ce8))
        yield
        kv_all, ksum_all = [], []
        for r, wk8, kt_u in zip(rows, wk_all, kt_all):
            kv_u, ks_u = [], []
            wk8b = wk8.astype(BF16)
            for p in range(n_pairs):
                kpb = qk_scr[r, HK + p * LANES:HK + (p + 1) * LANES].astype(BF16)
                ks8 = _dot(wk8b, kpb)
                for j in range(2):
                    h = 2 * p + j
                    kwt = (kt_u[p][j * DK:(j + 1) * DK, :] * wk8[h:h + 1, :]).astype(BF16)
                    kv_u.append(_dot(kwt, v_ref[r, h * DV:(h + 1) * DV]))
                    ks_u.append(ks8[h:h + 1, :])
            kv_all.append(kv_u)
            ksum_all.append(ks_u)
        yield
        m_run = {d: m_scr[H_B * d:H_B * (d + 1), 0:1] for d in dirs}
        a_all, b_all = [], []
        for (d, n), fe, ce in zip(units, f_end, c_end):
            mall_scr[d, n, 0:H_B, 0:1] = m_run[d]
            mx = jnp.maximum(m_run[d], ce)
            a_all.append(jnp.exp(m_run[d] - mx))
            b_all.append(jnp.exp(ce - mx))
            m_run[d] = fe + mx
        for d in dirs:
            m_scr[H_B * d:H_B * (d + 1), 0:1] = m_run[d]
        yield
        c_run = {d: [[c_scr[d, p, j * DK:(j + 1) * DK, :] for j in range(2)] for p in range(n_pairs)] for d in dirs}
        n_run = {d: [n_scr[2 * d + p:2 * d + p + 1, :] for p in range(n_pairs)] for d in dirs}
        for (d, n), a4, b4, kv_u, ks_u in zip(units, a_all, b_all, kv_all, ksum_all):
            for p in range(n_pairs):
                nall_scr[d, n, p:p + 1, :] = n_run[d][p]
                a_s = [a4[2 * p + j:2 * p + j + 1, :] for j in range(2)]
                b_s = [b4[2 * p + j:2 * p + j + 1, :] for j in range(2)]
                for j in range(2):
                    cj = c_run[d][p][j]
                    call_scr[d, n, p, j * DK:(j + 1) * DK, :] = cj.astype(BF16)
                    c_run[d][p][j] = a_s[j] * cj + b_s[j] * kv_u[2 * p + j]
                n_run[d][p] = (jnp.where(head_mask[0], a_s[0], a_s[1]) * n_run[d][p]
                               + jnp.where(head_mask[0], b_s[0] * ks_u[2 * p], b_s[1] * ks_u[2 * p + 1]))
        for d in dirs:
            for p in range(n_pairs):
                n_scr[2 * d + p:2 * d + p + 1, :] = n_run[d][p]
                for j in range(2):
                    c_scr[d, p, j * DK:(j + 1) * DK, :] = c_run[d][p][j]

    eye = lower & upper
    ones8 = jnp.ones((SUBLANES, L), BF16)
    sub8 = lax.broadcasted_iota(jnp.int32, (SUBLANES, LANES), 0)
    sub_h = lax.broadcasted_iota(jnp.int32, (H_B, L), 0)
    n_rows = [((sub8 == 2 * p) & head_mask[0]) | ((sub8 == 2 * p + 1) & head_mask[1]) for p in range(n_pairs)]

    def head_rows(vals):
        out = vals[0][0:H_B, :]
        for h in range(1, H_B):
            out = jnp.where(sub_h == h, vals[h][0:H_B, :], out)
        return out

    def out_group(ns):
        chunks = [(d, n) for n in ns for d in range(2)]
        pairs = [(d, n, p) for d, n in chunks for p in range(n_pairs)]
        units = [(d, n, p, j) for d, n, p in pairs for j in range(2)]
        cms = [lane_cummax(g_scr[d, n], d)[0:H_B, 0:L] for d, n in chunks]
        qk2s, qc2s, qn2s = [], [], []
        for d, n, p in pairs:
            r = _chunk_rows(n)
            qp = qk_scr[r, p * LANES:(p + 1) * LANES]
            q2 = jnp.concatenate([jnp.where(head_mask[j], qp, 0.0) for j in range(2)], axis=0).astype(BF16)
            qk2s.append(_dot_nt(q2, qk_scr[r, HK + p * LANES:HK + (p + 1) * LANES].astype(BF16)))
            qc2s.append(_dot(q2, call_scr[d, n, p]))
            nsel = jnp.where(n_rows[p], nall_scr[d, n, p:p + 1, :], 0.0).astype(BF16)
            qn2s.append(_dot_nt(nsel, qp.astype(BF16)))
        yield
        s_all = []
        for ui, (d, n, p, j) in enumerate(units):
            grow = g_scr[d, n, 2 * p + j:2 * p + j + 1, 0:L]
            e = jnp.where(tmask[d], grow, -jnp.inf)
            cmax = jnp.max(e, axis=-1, keepdims=True)
            s_all.append((qk2s[ui // 2][j * L:(j + 1) * L, :] * jnp.exp(e - cmax)).astype(BF16))
        yield
        nums = [_dot(s, v_ref[_chunk_rows(n), (2 * p + j) * DV:(2 * p + j + 1) * DV])
                for (d, n, p, j), s in zip(units, s_all)]
        dens = [_dot_nt(ones8, s) for s in s_all]
        yield
        scales = []
        for ci, (d, n) in enumerate(chunks):
            den_loc = head_rows(dens[ci * H_B:(ci + 1) * H_B])
            qn = qn2s[ci * n_pairs][0:H_B, :]
            for p in range(1, n_pairs):
                qn = qn + qn2s[ci * n_pairs + p][0:H_B, :]
            cm = cms[ci]
            m_prev = mall_scr[d, n, 0:H_B, 0:1]
            delta = cm - m_prev
            t = jnp.exp(-jnp.abs(delta))
            w_loc = jnp.where(delta <= 0.0, t, 1.0)
            w_inter = jnp.where(delta <= 0.0, 1.0, t)
            mt = f_scr[d, n, 0:H_B, 0:L] + jnp.maximum(m_prev, cm)
            den = w_loc * den_loc + w_inter * qn
            rinv = 1.0 / jnp.maximum(jnp.abs(den), jnp.exp(-mt))
            scales.append((w_loc * rinv, w_inter * rinv))
        yield
        hs = []
        for ui, (d, n, p, j) in enumerate(units):
            h = 2 * p + j
            sc_loc, sc_inter = scales[ui // H_B]
            d_loc = jnp.where(eye, sc_loc[h:h + 1, :], 0.0).astype(BF16)
            d_inter = jnp.where(eye, sc_inter[h:h + 1, :], 0.0).astype(BF16)
            hs.append(_dot(d_loc, nums[ui].astype(BF16))
                      + _dot(d_inter, qc2s[ui // 2][j * L:(j + 1) * L, :].astype(BF16)))
        yield
        for ni, n in enumerate(ns):
            r = _chunk_rows(n)
            for h in range(H_B):
                vs = slice(h * DV, (h + 1) * DV)
                o = hs[(2 * ni) * H_B + h] + hs[(2 * ni + 1) * H_B + h]
                out_ref[r, vs] = (_rms(o, gw_ref[:, vs]) * _sigmoid(og_ref[r, vs].astype(F32))).astype(out_ref.dtype)

    def finish():
        if write_state:
            for d in range(2):
                for p in range(n_pairs):
                    cnew_ref[d, p] = c_scr[d, p]
                    for j in range(2):
                        nnew_ref[d, 2 * p + j:2 * p + j + 1, :] = n_scr[2 * d + p:2 * d + p + 1, j * DK:(j + 1) * DK]
                mnew_ref[d:d + 1, :] = to_row(m_scr[H_B * d:H_B * (d + 1), 0:1])

    return state_group, out_group, finish


def _mlstm_scratch(T, C2, grid_w):
    n_pairs = C2 // 2 // LANES
    n_chunks = T // CHUNK
    pad_rows = 2 * (grid_w + SUBLANES) if T // grid_w > 1 else 2 * SUBLANES
    return [
        pltpu.VMEM((T + pad_rows, C2), F32),
        pltpu.VMEM((T, C2), F32),
        pltpu.VMEM((T, SMALL_W), F32),
        pltpu.VMEM((2, n_pairs, LANES, LANES), F32),
        pltpu.VMEM((SUBLANES, LANES), F32),
        pltpu.VMEM((SUBLANES, LANES), F32),
        pltpu.VMEM((2, n_chunks, n_pairs, LANES, LANES), BF16),
        pltpu.VMEM((2, n_chunks, SUBLANES, LANES), F32),
        pltpu.VMEM((2, n_chunks, SUBLANES, LANES), F32),
        pltpu.VMEM((2, n_chunks, SUBLANES, LANES), F32),
        pltpu.VMEM((2, n_chunks, SUBLANES, LANES), F32),
    ]


N_GLA_SCRATCH = 5
N_MLSTM_SCRATCH = 11


def _scan_kernel(*refs, cols, layer, has_state, write_state, n_cast, ride_ada, grid_w, unroll):
    refs = list(refs)
    z_refs = refs[:2]
    del refs[:2]
    s0_ref = c0_ref = n0_ref = m0_ref = None
    if has_state:
        s0_ref, c0_ref, n0_ref, m0_ref = refs[:4]
        del refs[:4]
    wa_ref, bal_ref, gwa_ref, cw_ref, bmg_ref, gwb_ref = refs[:6]
    del refs[:6]
    cast_in = refs[:n_cast]
    del refs[:n_cast]
    if ride_ada:
        ada_in = refs[:4]
        del refs[:4]
    outa_ref, outb_ref = refs[:2]
    del refs[:2]
    snew_ref = cnew_ref = nnew_ref = mnew_ref = None
    if write_state:
        snew_ref, cnew_ref, nnew_ref, mnew_ref = refs[:4]
        del refs[:4]
    cast_out = refs[:n_cast]
    del refs[:n_cast]
    if ride_ada:
        ada_out = refs.pop(0)
    wal_scr, bm_scr = refs[:2]
    del refs[:2]
    gla_scr = refs[:N_GLA_SCRATCH]
    mlstm_scr = refs[N_GLA_SCRATCH:]

    for src, dst in zip(cast_in, cast_out):
        dst[...] = src[...].astype(BF16)
    if ride_ada:
        _ada_tile(*ada_in, ada_out)

    R, HK = wa_ref.shape[1], wa_ref.shape[2]
    wal_scr[...] = jnp.zeros(wal_scr.shape, BF16)
    for d in range(2):
        wal_scr[d * R:(d + 1) * R, d * HK:(d + 1) * HK] = wa_ref[d].astype(BF16)
    lane = lax.broadcasted_iota(jnp.int32, (1, LANES), 1)
    bm = jnp.zeros((1, LANES), F32)
    for g in range(bmg_ref.shape[1]):
        for h in range(H_B):
            bm = jnp.where(lane == GATE_LANE0 + H_B * g + h, bmg_ref[layer, g, h], bm)
    bm_scr[0:1, :] = bm

    def view(name):
        a, c0, w = cols[name]
        return z_refs[a].at[:, pl.ds(c0, w)]

    sm_ref = view("small")
    n_chunks = z_refs[0].shape[0] // CHUNK
    gla = _gla_body(view("qa"), view("ka"), view("va"), view("ga"), sm_ref, s0_ref, wal_scr, bal_ref, gwa_ref,
                    outa_ref, snew_ref, *gla_scr)
    mlstm = _mlstm_body(view("qkb"), view("vb"), view("ob"), sm_ref, c0_ref, n0_ref, m0_ref, cw_ref,
                        bm_scr.at[0:1, :], gwb_ref, outb_ref, cnew_ref, nnew_ref, mnew_ref, *mlstm_scr,
                        grid_w=grid_w)
    gla_state, gla_out, gla_finish, decay_span = gla
    mlstm_state, mlstm_out, mlstm_finish = mlstm

    def passes(gla_out_fn):
        _chunk_loop(n_chunks, unroll, lambda ns: [fn([n], (d,)) for n in ns for d in range(2)
                                                  for fn in (mlstm_state, gla_state)])
        _chunk_loop(n_chunks, unroll, lambda ns: [fn([n]) for n in ns for fn in (mlstm_out, gla_out_fn)])

    wide_decay = decay_span > GLA_FACTORED_DECAY_MAX

    @pl.when(jnp.logical_not(wide_decay))
    def _():
        passes(gla_out)

    @pl.when(wide_decay)
    def _():
        passes(functools.partial(gla_out, exact_decay=True))

    gla_finish()
    mlstm_finish()


def _scan_call(z2d, row0, B, T, states, lw, layer, *, grid_w, write_state, casts=(), ada=None):
    assert row0 % T == 0 and all(z.shape[0] % T == 0 for z in z2d)
    z3 = [z.reshape(z.shape[0] // T, T, z.shape[1]) for z in z2d]
    blk0 = row0 // T
    HK = lw["w_alpha2"].shape[-1]
    DA = lw["gnorm_a_w"].shape[0]
    C2 = lw["conv_w"].shape[-1]
    DB = lw["gnorm_b_w"].shape[0]
    DK_A, DK_B = HK // H_A, C2 // 2 // H_B
    pa, pb = HK // LANES, C2 // 2 // LANES
    n_chunks = T // CHUNK
    has_state = states is not None
    widths = ((("qa", HK), ("ka", HK), ("qkb", C2), ("small", SMALL_W)),
              (("va", DA), ("ga", DA), ("vb", DB), ("ob", DB)))
    cols = {}
    for a, groups in enumerate(widths):
        c0 = 0
        for name, w in groups:
            cols[name] = (a, c0, w)
            c0 += w
        assert c0 == z3[a].shape[2]
    cast_in_specs, cast_out_specs, cast_out_shape, cast_args = _cast_specs(casts, B)
    kern = functools.partial(_scan_kernel, cols=cols, layer=layer, has_state=has_state, write_state=write_state,
                             n_cast=len(casts), ride_ada=ada is not None, grid_w=grid_w,
                             unroll=min(n_chunks, SCAN_UNROLL))

    def per_batch(shape):
        nd = len(shape)
        return pl.BlockSpec((None,) + tuple(shape), lambda b: (b,) + (0,) * nd)

    def per_batch_layer(shape):
        nd = len(shape)
        return pl.BlockSpec((None, None) + tuple(shape), lambda b: (b, layer) + (0,) * nd)

    def of_layer(a):
        return pl.BlockSpec((None,) + a.shape[1:], lambda b: (layer,) + (0,) * (a.ndim - 1))

    def whole(a):
        return pl.BlockSpec(a.shape, lambda b: (0,) * a.ndim)

    state_shapes = ((2, pa, LANES, LANES), (2, pb, LANES, LANES), (2, H_B, DK_B), (2, H_B))
    in_specs = [pl.BlockSpec((None, T, z.shape[2]), lambda b: (b + blk0, 0, 0)) for z in z3]
    args = list(z3)
    if has_state:
        s_gla, s_c, s_n, s_m = states
        depth = s_gla.shape[1]
        args += [s_gla.reshape((B, depth) + state_shapes[0]), s_c.reshape((B, depth) + state_shapes[1]), s_n, s_m]
        in_specs += [per_batch_layer(s) for s in state_shapes]
    args += [lw["w_alpha2"], lw["b_alpha"], lw["gnorm_a_w"].reshape(1, DA), lw["conv_w"], lw["b_mgate"],
             lw["gnorm_b_w"].reshape(1, DB)]
    in_specs += [of_layer(lw["w_alpha2"]), of_layer(lw["b_alpha"]), pl.BlockSpec((1, DA), lambda b: (0, 0)),
                 whole(lw["conv_w"]), pl.BlockSpec(memory_space=pltpu.SMEM), pl.BlockSpec((1, DB), lambda b: (0, 0))]
    args += cast_args
    in_specs += cast_in_specs
    out_specs = [per_batch((T, DA)), per_batch((T, DB))]
    out_shape = [jax.ShapeDtypeStruct((B, T, DA), BF16), jax.ShapeDtypeStruct((B, T, DB), BF16)]
    if write_state:
        out_specs += [per_batch(s) for s in state_shapes]
        out_shape += [jax.ShapeDtypeStruct((B,) + s, F32) for s in state_shapes]
    out_specs += cast_out_specs
    out_shape += cast_out_shape
    if ada is not None:
        cc, c, w_ada, b_ada, col0 = ada
        n_rest = w_ada.shape[1] - col0
        wcol = n_rest // B
        assert n_rest % B == 0 and wcol % LANES == 0 and col0 % wcol == 0
        args += [cc, c, w_ada, b_ada]
        in_specs += [whole(cc), whole(c),
                     pl.BlockSpec((w_ada.shape[0], wcol), lambda b: (0, col0 // wcol + b)),
                     pl.BlockSpec((1, wcol), lambda b: (0, col0 // wcol + b))]
        out_specs.append(pl.BlockSpec((COND_ROWS, wcol), lambda b: (0, b)))
        out_shape.append(jax.ShapeDtypeStruct((COND_ROWS, n_rest), F32))
    scratch = ([pltpu.VMEM((SMALL_W, 2 * HK), BF16), pltpu.VMEM((SUBLANES, LANES), F32)]
               + _gla_scratch(T, HK) + _mlstm_scratch(T, C2, grid_w))
    assert len(scratch) == 2 + N_GLA_SCRATCH + N_MLSTM_SCRATCH
    return pl.pallas_call(
        kern,
        grid=(B,),
        in_specs=in_specs,
        out_specs=out_specs,
        out_shape=out_shape,
        scratch_shapes=scratch,
        compiler_params=pltpu.CompilerParams(dimension_semantics=("arbitrary",),
                                             vmem_limit_bytes=VMEM_LIMIT),
        name="mixer_scans",
    )(*args)


def _outff_kernel(xc_ref, xl_ref, ac_ref, al_ref, bc_ref, bl_ref, mod_ref, n2_ref, fn_ref, wo_ref, w1_ref, w2_ref,
                  yc_ref, yl_ref, *, n_ctx, tiles_per_req, ff_chunk, final_norm):
    D = xc_ref.shape[1]
    DA = ac_ref.shape[1]
    is_ctx, row = _tile_group(n_ctx, tiles_per_req)

    def mod(k):
        return mod_ref[pl.ds(row, 1), (k - MOD_SPLIT) * D:(k - MOD_SPLIT + 1) * D]

    def tile(x_ref, a_ref, b_ref, y_ref):
        y = _dot(a_ref[...], wo_ref[0:DA, :]) + _dot(b_ref[...], wo_ref[DA:, :])
        x1 = x_ref[...] + mod(2) * y
        h2 = (_rms(x1, n2_ref[...]) * (1.0 + mod(4)) + mod(3)).astype(BF16)
        acc = jnp.zeros(x1.shape, F32)
        for c0 in range(0, w1_ref.shape[1], ff_chunk):
            u = jnp.maximum(_dot(h2, w1_ref[:, c0:c0 + ff_chunk]), 0.0)
            acc = acc + _dot((u * u).astype(BF16), w2_ref[c0:c0 + ff_chunk, :])
        x2 = x1 + mod(5) * acc
        y_ref[...] = _rms(x2, fn_ref[...]) if final_norm else x2

    @pl.when(is_ctx)
    def _():
        tile(xc_ref, ac_ref, bc_ref, yc_ref)

    @pl.when(jnp.logical_not(is_ctx))
    def _():
        tile(xl_ref, al_ref, bl_ref, yl_ref)


def _outff_call(xc2d, xl2d, ac, al, bc, bl, mod, norm2_w, final_w, wo, w1, w2, *, tm, tiles_per_req, final_norm):
    (Mc, D), Ml = xc2d.shape, xl2d.shape[0]
    n_ctx = Mc // tm
    DA = ac.shape[1]
    DFF = w1.shape[1]
    kern = functools.partial(_outff_kernel, n_ctx=n_ctx, tiles_per_req=tiles_per_req, ff_chunk=FF_CHUNK,
                             final_norm=final_norm)
    once = pl.Buffered(1)
    ctx, lat = _ctx_tile(n_ctx), _lat_tile(n_ctx)
    return pl.pallas_call(
        kern,
        grid=((Mc + Ml) // tm,),
        in_specs=[
            pl.BlockSpec((tm, D), ctx), pl.BlockSpec((tm, D), lat),
            pl.BlockSpec((tm, DA), ctx), pl.BlockSpec((tm, DA), lat),
            pl.BlockSpec((tm, D - DA), ctx), pl.BlockSpec((tm, D - DA), lat),
            pl.BlockSpec(mod.shape, lambda i: (0, 0)),
            pl.BlockSpec((1, D), lambda i: (0, 0)),
            pl.BlockSpec((1, D), lambda i: (0, 0)),
            pl.BlockSpec((D, D), lambda i: (0, 0), pipeline_mode=once),
            pl.BlockSpec((D, DFF), lambda i: (0, 0), pipeline_mode=once),
            pl.BlockSpec((DFF, D), lambda i: (0, 0), pipeline_mode=once),
        ],
        out_specs=[pl.BlockSpec((tm, D), ctx), pl.BlockSpec((tm, D), lat)],
        out_shape=[jax.ShapeDtypeStruct((Mc, D), F32), jax.ShapeDtypeStruct((Ml, D), F32)],
        compiler_params=pltpu.CompilerParams(dimension_semantics=("arbitrary",),
                                             vmem_limit_bytes=VMEM_LIMIT),
        name="outproj_mlp",
    )(xc2d, xl2d, ac, al, bc, bl, mod, norm2_w.reshape(1, D), final_w.reshape(1, D), wo, w1, w2)


def _layer(xc, xl, cond, ada_w, cached, lw, layer, ffw, final_w, final_norm):
    (Bc, Tc, D), (Bl, Tl, _) = xc.shape, xl.shape
    tm = TOKEN_TILE
    assert (Bc * Tc) % tm == 0 and Tl % tm == 0 and (Bc * Tc) % Tl == 0
    xc2d, xl2d = xc.reshape(Bc * Tc, D), xl.reshape(Bl * Tl, D)
    z = _inproj_call(xc2d, xl2d, *cond, *ada_w, lw["norm1_w"], lw["w_in_t"], tm=tm, tiles_per_req=Tl // tm,
                     f32_rows=lw["f32_rows"], small_rows=lw["small_rows"], bf16_rows=lw["bf16_rows"])
    res_c = _scan_call(z, 0, Bc, Tc, None, lw, layer, grid_w=Tc, write_state=True,
                       casts=((ffw[0], 0), (ffw[1], 0), (ffw[2], 0)), ada=(*cond, *ada_w, MOD_SPLIT * D))
    res_l = _scan_call(z, Bc * Tc, Bl, Tl, cached, lw, layer, grid_w=GRID_W, write_state=False)
    wo_b, w1_b, w2_b, mod_out = res_c[-4:]
    yc, yl = _outff_call(xc2d, xl2d, res_c[0].reshape(Bc * Tc, -1), res_l[0].reshape(Bl * Tl, -1),
                         res_c[1].reshape(Bc * Tc, -1), res_l[1].reshape(Bl * Tl, -1), mod_out, lw["norm2_w"],
                         final_w, wo_b, w1_b, w2_b, tm=tm, tiles_per_req=Tl // tm, final_norm=final_norm)
    return yc.reshape(Bc, Tc, D), yl.reshape(Bl, Tl, D), tuple(res_c[2:6])


def _layer_weights(l, norm1_w, norm2_w, w_in, w_alpha2, b_alpha, b_mgate, conv_w, gnorm_a_w, gnorm_b_w):
    hk_a = w_alpha2.shape[-1]
    d_a = gnorm_a_w.shape[-1]
    d_b = gnorm_b_w.shape[-1]
    hk_b = conv_w.shape[-1] // 2
    sizes = (hk_a, hk_a, d_a, d_a, 2 * R_ALPHA, hk_b, hk_b, d_b, d_b, 4 * H_B)
    assert w_alpha2.shape[2] == R_ALPHA and b_mgate.shape[1] * b_mgate.shape[2] == 4 * H_B
    offs = [0]
    for s in sizes:
        offs.append(offs[-1] + s)
    f32_rows = ((offs[0], offs[2] - offs[0]), (offs[5], offs[7] - offs[5]))
    small_rows = ((offs[4], offs[5] - offs[4]), (offs[9], offs[10] - offs[9]))
    bf16_rows = ((offs[2], offs[4] - offs[2]), (offs[7], offs[9] - offs[7]))
    assert all(n % LANES == 0 and r % BF16_ROWS == 0 for r, n in f32_rows + bf16_rows)
    return dict(
        norm1_w=norm1_w[l], norm2_w=norm2_w[l], w_in_t=jnp.swapaxes(w_in[l], 0, 1),
        f32_rows=f32_rows, small_rows=small_rows, bf16_rows=bf16_rows,
        w_alpha2=w_alpha2, b_alpha=b_alpha, b_mgate=b_mgate, conv_w=conv_w[l],
        gnorm_a_w=gnorm_a_w[l], gnorm_b_w=gnorm_b_w[l],
    )


def kernel(x_prompt, x_sample, c, state_gla, state_mlstm_C, state_mlstm_n, state_mlstm_m, c_ctx, w_ada, b_ada, norm1_w, norm2_w, w_in, w_alpha2, b_alpha, b_mgate, conv_w, gnorm_a_w, gnorm_b_w, w_out, w_ff1, w_ff2, final_norm_w):
    depth = w_in.shape[0]
    D = x_prompt.shape[-1]
    Bp, Tp, _ = x_prompt.shape
    Bs = x_sample.shape[0]
    assert 1 + Bs <= COND_ROWS
    cond = (c_ctx.reshape(1, D), c)
    cached = (state_gla, state_mlstm_C, state_mlstm_n, state_mlstm_m)
    xp, xs = x_prompt, x_sample
    s_gla, s_c, s_n, s_m = [], [], [], []
    for l in range(depth):
        lw = _layer_weights(l, norm1_w, norm2_w, w_in, w_alpha2, b_alpha, b_mgate, conv_w,
                            gnorm_a_w, gnorm_b_w)
        xp, xs, ctx = _layer(xp, xs, cond, (w_ada[l], b_ada[l].reshape(1, -1)), cached, lw, l,
                             (w_out[l], w_ff1[l], w_ff2[l]), final_norm_w, l == depth - 1)
        s_gla.append(ctx[0].reshape(Bp, 2, H_A, -1, ctx[0].shape[-1]))
        s_c.append(ctx[1].reshape(Bp, 2, H_B, -1, ctx[1].shape[-1]))
        s_n.append(ctx[2])
        s_m.append(ctx[3])
    dt = x_prompt.dtype
    return (xp, xs, jnp.stack(s_gla, axis=1).astype(dt), jnp.stack(s_c, axis=1).astype(dt),
            jnp.stack(s_n, axis=1).astype(dt), jnp.stack(s_m, axis=1).astype(dt))
```

```python
import functools
import math

import jax
import jax.numpy as jnp
from jax import lax
from jax.experimental import pallas as pl
from jax.experimental.pallas import tpu as pltpu

F32 = jnp.float32
BF16 = jnp.bfloat16

GRID_W = 64
H_A = 4
H_B = 4
R_ALPHA = 16
TAU_GLA = 16.0
CHUNK = 64
EPS = 1e-6
LANES = 128
SUBLANES = 8
BF16_ROWS = 16
COND_ROWS = SUBLANES
SMALL_W = LANES
GATE_LANE0 = 2 * R_ALPHA
VMEM_LIMIT = 56 * 1024 * 1024
SCAN_UNROLL = 4
PIPELINE_STARTS = 8
TOKEN_TILE = 512
FF_CHUNK = 512
GLA_FACTORED_DECAY_MAX = 60.0
MOD_SPLIT = 2


def _sigmoid(x):
    return 1.0 / (1.0 + jnp.exp(-x))


def _silu(x):
    return x * _sigmoid(x)


def _log_sigmoid(x):
    return jnp.minimum(x, 0.0) - jnp.log(1.0 + jnp.exp(-jnp.abs(x)))


def _dot(a, b):
    return jnp.dot(a, b, preferred_element_type=F32)


def _dot_nt(a, b):
    return lax.dot_general(a, b, (((1,), (1,)), ((), ())), preferred_element_type=F32)


def _rms(x, w):
    return x * lax.rsqrt(jnp.mean(x * x, axis=-1, keepdims=True) + EPS) * w


def _tri_sum(tri, x, terms=3):
    acc, rest = None, x
    for t in range(terms):
        part = rest.astype(BF16)
        prod = _dot(tri, part)
        acc = prod if acc is None else acc + prod
        if t + 1 < terms:
            rest = rest - part.astype(F32)
    return acc


def _chunk_masks(L):
    row = lax.broadcasted_iota(jnp.int32, (L, L), 0)
    col = lax.broadcasted_iota(jnp.int32, (L, L), 1)
    lower = row >= col
    upper = row <= col
    return lower, upper


def _ada_tile(cc_ref, c_ref, w_ref, b_ref, o_ref):
    D = cc_ref.shape[1]
    sub = lax.broadcasted_iota(jnp.int32, (COND_ROWS, D), 0)
    cond = jnp.where(sub == 0, cc_ref[...], 0.0)
    for r in range(c_ref.shape[0]):
        cond = jnp.where(sub == 1 + r, c_ref[r:r + 1, :], cond)
    o_ref[...] = _dot(_silu(cond).astype(BF16), w_ref[...].astype(BF16)) + b_ref[...]


def _tile_group(n_ctx, tiles_per_req):
    i = pl.program_id(0)
    is_ctx = i < n_ctx
    row = jnp.where(is_ctx, 0, 1 + jnp.maximum(i - n_ctx, 0) // tiles_per_req)
    return is_ctx, row


def _ctx_tile(n_ctx):
    return lambda i: (jnp.minimum(i, n_ctx - 1), 0)


def _lat_tile(n_ctx):
    return lambda i: (jnp.maximum(i - n_ctx, 0), 0)


def _inproj_kernel(xc_ref, xl_ref, cc_ref, c_ref, wa_ref, ba_ref, nw_ref, wt_ref, zf_ref, zh_ref, wb_scr, mod_ref,
                   *, n_ctx, tiles_per_req, f32_rows, small_rows, bf16_rows):
    D = xc_ref.shape[1]
    n_f32 = zf_ref.shape[1]

    @pl.when(pl.program_id(0) == 0)
    def _():
        _ada_tile(cc_ref, c_ref, wa_ref, ba_ref, mod_ref)

        def wide(rows, col):
            for r0, n in rows:
                wb_scr[col:col + n, :] = wt_ref[r0:r0 + n, :].astype(BF16)
                col += n
            return col

        col = wide(f32_rows, 0)
        parts = [wt_ref[r0:r0 + n, :] for r0, n in small_rows]
        n_small = sum(n for _, n in small_rows)
        parts.append(jnp.zeros((SMALL_W - n_small, D), F32))
        wb_scr[col:col + SMALL_W, :] = jnp.concatenate(parts, axis=0).astype(BF16)
        wide(bf16_rows, col + SMALL_W)

    is_ctx, row = _tile_group(n_ctx, tiles_per_req)

    def tile(x_ref):
        sh1 = mod_ref[pl.ds(row, 1), 0:D]
        sc1 = mod_ref[pl.ds(row, 1), D:2 * D]
        h = (_rms(x_ref[...], nw_ref[...]) * (1.0 + sc1) + sh1).astype(BF16)
        zf_ref[...] = _dot_nt(h, wb_scr[0:n_f32, :])
        zh_ref[...] = _dot_nt(h, wb_scr[n_f32:, :]).astype(BF16)

    @pl.when(is_ctx)
    def _():
        tile(xc_ref)

    @pl.when(jnp.logical_not(is_ctx))
    def _():
        tile(xl_ref)


def _inproj_call(xc2d, xl2d, cc, c, w_ada, b_ada, norm_w, w_in_t, *, tm, tiles_per_req, f32_rows, small_rows,
                 bf16_rows):
    (Mc, D), Ml = xc2d.shape, xl2d.shape[0]
    n_ctx = Mc // tm
    n_f32 = sum(n for _, n in f32_rows) + SMALL_W
    n_bf16 = sum(n for _, n in bf16_rows)
    n_out = n_f32 + n_bf16
    n_mod = MOD_SPLIT * D
    kern = functools.partial(_inproj_kernel, n_ctx=n_ctx, tiles_per_req=tiles_per_req,
                             f32_rows=f32_rows, small_rows=small_rows, bf16_rows=bf16_rows)
    once = pl.Buffered(1)
    return pl.pallas_call(
        kern,
        grid=((Mc + Ml) // tm,),
        in_specs=[
            pl.BlockSpec((tm, D), _ctx_tile(n_ctx)),
            pl.BlockSpec((tm, D), _lat_tile(n_ctx)),
            pl.BlockSpec(cc.shape, lambda i: (0, 0)),
            pl.BlockSpec(c.shape, lambda i: (0, 0)),
            pl.BlockSpec((D, n_mod), lambda i: (0, 0), pipeline_mode=once),
            pl.BlockSpec((1, n_mod), lambda i: (0, 0)),
            pl.BlockSpec((1, D), lambda i: (0, 0)),
            pl.BlockSpec(w_in_t.shape, lambda i: (0, 0), pipeline_mode=once),
        ],
        out_specs=[pl.BlockSpec((tm, n_f32), lambda i: (i, 0)), pl.BlockSpec((tm, n_bf16), lambda i: (i, 0))],
        out_shape=[jax.ShapeDtypeStruct((Mc + Ml, n_f32), F32), jax.ShapeDtypeStruct((Mc + Ml, n_bf16), BF16)],
        scratch_shapes=[pltpu.VMEM((n_out, D), BF16), pltpu.VMEM((COND_ROWS, n_mod), F32)],
        compiler_params=pltpu.CompilerParams(dimension_semantics=("arbitrary",),
                                             vmem_limit_bytes=VMEM_LIMIT),
        name="norm_inproj",
    )(xc2d, xl2d, cc, c, w_ada, b_ada, norm_w.reshape(1, D), w_in_t)


def _chunk_loop(n_chunks, unroll, make_units):
    def step(ns):
        pending = list(make_units(ns))
        active = []
        while pending or active:
            for _ in range(min(PIPELINE_STARTS, len(pending))):
                active.append(pending.pop(0))
            alive = []
            for g in active:
                try:
                    next(g)
                    alive.append(g)
                except StopIteration:
                    pass
            active = alive

    if unroll >= n_chunks:
        step(list(range(n_chunks)))
        return

    def body(i, carry):
        step([i * unroll + u for u in range(unroll)])
        return carry

    lax.fori_loop(0, n_chunks // unroll, body, 0)


def _chunk_rows(n):
    if isinstance(n, int):
        return pl.ds(n * CHUNK, CHUNK)
    return pl.ds(pl.multiple_of(n * CHUNK, CHUNK), CHUNK)


def _cast_specs(casts, n_steps):
    in_specs, out_specs, out_shape, args = [], [], [], []
    for w, axis in casts:
        blk = list(w.shape)
        assert blk[axis] % n_steps == 0
        blk[axis] //= n_steps
        assert blk[0] % BF16_ROWS == 0 and blk[1] % LANES == 0
        idx = (lambda b: (b, 0)) if axis == 0 else (lambda b: (0, b))
        in_specs.append(pl.BlockSpec(tuple(blk), idx))
        out_specs.append(pl.BlockSpec(tuple(blk), idx))
        out_shape.append(jax.ShapeDtypeStruct(w.shape, BF16))
        args.append(w)
    return in_specs, out_specs, out_shape, args


def _gla_body(q_ref, k_ref, v_ref, g_ref, sm_ref, s0_ref, wal_ref, bal_ref, gw_ref, out_ref, snew_ref,
              st_scr, sall_scr, qh_scr, qs_scr, kh_scr):
    has_state = s0_ref is not None
    write_state = snew_ref is not None
    T = q_ref.shape[0]
    L = CHUNK
    N = T // L
    HK = q_ref.shape[1]
    DK = HK // H_A
    DV = v_ref.shape[1] // H_A
    scale = DK ** -0.5
    n_pairs = HK // LANES

    lower, upper = _chunk_masks(L)
    tri = (lower.astype(BF16), upper.astype(BF16))
    tmask = (lower, upper)
    lane = lax.broadcasted_iota(jnp.int32, (1, LANES), 1)
    head_mask = (lane < DK, lane >= DK)

    for d in range(2):
        for p in range(n_pairs):
            if has_state:
                st_scr[d, p] = s0_ref[d, p].T
            else:
                st_scr[d, p] = jnp.zeros((LANES, LANES), F32)

    def decay_pre(d, r):
        return _dot(sm_ref[r, :].astype(BF16), wal_ref[:, d * HK:(d + 1) * HK]) + bal_ref[d:d + 1, :]

    neg_pre = jnp.maximum(-(_dot(sm_ref[...].astype(BF16), wal_ref[...])
                            + jnp.concatenate([bal_ref[0:1, :], bal_ref[1:2, :]], axis=1)), 0.0)
    chunk_sums = jnp.sum(neg_pre.reshape(N, L, 2 * HK), axis=1)
    decay_span = (jnp.max(chunk_sums) + L * math.log(2.0)) * (1.0 / TAU_GLA)

    def state_group(ns, dirs=(0, 1)):
        units = [(d, n if d == 0 else N - 1 - n) for n in ns for d in dirs]
        rows = [_chunk_rows(n) for _, n in units]
        vt_all = [[jnp.concatenate([v_ref[r, (2 * p + j) * DV:(2 * p + j + 1) * DV] for j in range(2)],
                                   axis=0).T for p in range(n_pairs)] for r in rows]
        yield
        pre = [decay_pre(d, r) for (d, _), r in zip(units, rows)]
        yield
        g = [_log_sigmoid(x) * (1.0 / TAU_GLA) for x in pre]
        yield
        b = [_tri_sum(tri[d], gi, terms=2) for (d, _), gi in zip(units, g)]
        yield
        ks_all, dec_all = [], []
        for (d, _), r, bi in zip(units, rows, b):
            bend = bi[L - 1:L, :] if d == 0 else bi[0:1, :]
            q = q_ref[r, :] * scale
            ks = (k_ref[r, :] * jnp.exp(bend - bi)).astype(BF16)
            qs = q * jnp.exp(bi)
            qh_scr[d, r, :] = (qs * jnp.exp(-bend)).astype(BF16)
            qs_scr[d, r, :] = qs.astype(BF16)
            kh_scr[d, r, :] = ks
            ks_all.append(ks)
            dec_all.append(jnp.exp(bend))
        yield
        upd_all = []
        for vt_u, ks in zip(vt_all, ks_all):
            upd_u = []
            for p in range(n_pairs):
                kp = ks[:, p * LANES:(p + 1) * LANES]
                kk = jnp.concatenate([jnp.where(head_mask[j], kp, jnp.zeros_like(kp)) for j in range(2)], axis=0)
                upd_u.append(_dot(vt_u[p], kk))
            upd_all.append(upd_u)
        yield
        st = {d: [st_scr[d, p] for p in range(n_pairs)] for d in dirs}
        for (d, n), dec, upd in zip(units, dec_all, upd_all):
            for p in range(n_pairs):
                sall_scr[d, n, p] = st[d][p].astype(BF16)
                st[d][p] = st[d][p] * dec[:, p * LANES:(p + 1) * LANES] + upd[p]
        for d in dirs:
            for p in range(n_pairs):
                st_scr[d, p] = st[d][p]

    def stack_heads(x):
        return jnp.concatenate([jnp.where(head_mask[j], x, jnp.zeros_like(x)) for j in range(2)], axis=0)

    tok = lax.broadcasted_iota(jnp.int32, (L, 1), 0)
    row_t = lax.broadcasted_iota(jnp.int32, (2 * L, L), 0) & (L - 1)
    col_s = lax.broadcasted_iota(jnp.int32, (2 * L, L), 1)

    def exact_scores(d, r, p):
        ls = slice(p * LANES, (p + 1) * LANES)
        b = _tri_sum(tri[d], _log_sigmoid(decay_pre(d, r)[:, ls]) * (1.0 / TAU_GLA))
        q = q_ref[r, ls] * scale
        k = k_ref[r, ls]
        acc = jnp.where(row_t == col_s, _dot_nt(stack_heads(q).astype(BF16), k.astype(BF16)), 0.0)
        src = lax.broadcasted_iota(jnp.int32, (L, L), 1)
        h = L // 2
        while h >= 1:
            first = tok & ~(2 * h - 1)
            edge = first + (h - 1 if d == 0 else h)
            b_edge = _tri_sum((src == edge).astype(BF16), b)
            upper = (tok & (2 * h - 1)) >= h
            later, earlier = (upper, ~upper) if d == 0 else (~upper, upper)
            qt = jnp.where(later, q * jnp.exp(b - b_edge), 0.0)
            kt = jnp.where(earlier, k * jnp.exp(b_edge - b), 0.0)
            sc = _dot_nt(stack_heads(qt).astype(BF16), kt.astype(BF16))
            acc = acc + jnp.where((row_t & ~(2 * h - 1)) == (col_s & ~(2 * h - 1)), sc, 0.0)
            h //= 2
        return acc

    def out_group(ns, exact_decay=False):
        pairs = [(d, ni, p) for ni in range(len(ns)) for d in range(2) for p in range(n_pairs)]
        scores, inter = [], []
        for d, ni, p in pairs:
            r = _chunk_rows(ns[ni])
            ls = slice(p * LANES, (p + 1) * LANES)
            if exact_decay:
                scores.append(exact_scores(d, r, p))
            else:
                scores.append(_dot_nt(stack_heads(qh_scr[d, r, ls]), kh_scr[d, r, ls]))
            inter.append(_dot_nt(stack_heads(qs_scr[d, r, ls]), sall_scr[d, ns[ni], p]))
        yield
        probs = [[jnp.where(tmask[d], sc[j * L:(j + 1) * L, :], 0.0).astype(BF16) for j in range(2)]
                 for (d, _, _), sc in zip(pairs, scores)]
        yield
        outs = {}
        for (d, ni, p), pr, it in zip(pairs, probs, inter):
            r = _chunk_rows(ns[ni])
            for j in range(2):
                vs = slice((2 * p + j) * DV, (2 * p + j + 1) * DV)
                outs[(d, ni, 2 * p + j)] = _dot(pr[j], v_ref[r, vs]) + it[j * L:(j + 1) * L, :]
        yield
        for ni, n in enumerate(ns):
            r = _chunk_rows(n)
            for h in range(H_A):
                vs = slice(h * DV, (h + 1) * DV)
                o = outs[(0, ni, h)] + outs[(1, ni, h)]
                out_ref[r, vs] = (_rms(o, gw_ref[:, vs]) * _silu(g_ref[r, vs].astype(F32))).astype(out_ref.dtype)

    def finish():
        if write_state:
            for d in range(2):
                for p in range(n_pairs):
                    snew_ref[d, p] = st_scr[d, p].T

    return state_group, out_group, finish, decay_span


def _gla_scratch(T, HK):
    n_pairs = HK // LANES
    n_chunks = T // CHUNK
    return [
        pltpu.VMEM((2, n_pairs, LANES, LANES), F32),
        pltpu.VMEM((2, n_chunks, n_pairs, LANES, LANES), BF16),
        pltpu.VMEM((2, T, HK), BF16),
        pltpu.VMEM((2, T, HK), BF16),
        pltpu.VMEM((2, T, HK), BF16),
    ]


def _mlstm_body(qk_ref, v_ref, og_ref, sm_ref, c0_ref, n0_ref, m0_ref, cw_ref, bm_ref, gw_ref,
                out_ref, cnew_ref, nnew_ref, mnew_ref,
                pad_scr, qk_scr, y_scr, c_scr, n_scr, m_scr, call_scr, nall_scr, mall_scr, g_scr, f_scr,
                *, grid_w):
    has_state = c0_ref is not None
    write_state = cnew_ref is not None
    T = qk_ref.shape[0]
    L = CHUNK
    N = T // L
    C2 = qk_ref.shape[1]
    HK = C2 // 2
    DK = HK // H_B
    DV = v_ref.shape[1] // H_B
    scale = DK ** -0.5
    n_pairs = HK // LANES
    P = pad_scr.shape[0] - T
    P0 = P // 2
    rows_img = T // grid_w

    lower, upper = _chunk_masks(L)
    tri = (lower.astype(BF16), upper.astype(BF16))
    tmask = (lower, upper)
    lane = lax.broadcasted_iota(jnp.int32, (1, LANES), 1)
    head_mask = (lane < DK, lane >= DK)
    lane_in = lane & (L - 1)

    def lane_cummax(x, d):
        k = 1
        while k < L:
            if d == 0:
                x = jnp.maximum(x, jnp.where(lane_in >= k, pltpu.roll(x, k, axis=1), -jnp.inf))
            else:
                x = jnp.maximum(x, jnp.where(lane_in < L - k, pltpu.roll(x, LANES - k, axis=1), -jnp.inf))
            k *= 2
        return x

    for d in range(2):
        for p in range(n_pairs):
            if has_state:
                c_scr[d, p] = c0_ref[d, p]
                n_scr[2 * d + p:2 * d + p + 1, :] = jnp.concatenate(
                    [n0_ref[d, 2 * p + j:2 * p + j + 1, :] for j in range(2)], axis=1)
            else:
                c_scr[d, p] = jnp.zeros((LANES, LANES), F32)
                n_scr[2 * d + p:2 * d + p + 1, :] = jnp.zeros((1, LANES), F32)
    eye_h = (lax.broadcasted_iota(jnp.int32, (H_B, H_B), 0) == lax.broadcasted_iota(jnp.int32, (H_B, H_B), 1))

    def to_col(row):
        return jnp.sum(jnp.where(eye_h, row, 0.0), axis=1, keepdims=True)

    def to_row(col):
        return jnp.sum(jnp.where(eye_h, col, 0.0), axis=0, keepdims=True)

    for d in range(2):
        if has_state:
            m_scr[H_B * d:H_B * (d + 1), 0:1] = to_col(m0_ref[d:d + 1, :])
        else:
            m_scr[H_B * d:H_B * (d + 1), 0:1] = jnp.zeros((H_B, 1), F32)

    pad_scr[0:P0, :] = jnp.zeros((P0, C2), F32)
    pad_scr[P0 + T:P + T, :] = jnp.zeros((P - P0, C2), F32)

    def copy_in(i, carry):
        r0 = pl.multiple_of(i * L, L)
        pad_scr[pl.ds(P0 + r0, L), :] = qk_ref[pl.ds(r0, L), :]
        return carry

    lax.fori_loop(0, N, copy_in, 0)

    lane_c = lax.broadcasted_iota(jnp.int32, (1, C2), 1)
    qscale = jnp.where(lane_c < HK, scale, 1.0).astype(F32)
    sub = lax.broadcasted_iota(jnp.int32, (L, 1), 0)
    img_rows = (0,) if rows_img == 1 else (-1, 0, 1)

    def conv_tile(i, carry):
        r0 = pl.multiple_of(i * L, L)
        col = lax.rem(r0, grid_w) + sub
        ok_left = col >= 1
        ok_right = col <= grid_w - 2
        sums = [None, None, None]
        for di in img_rows:
            blk = pad_scr[pl.ds(P0 + r0 + di * grid_w - SUBLANES, L + 2 * SUBLANES), :]
            for k in range(3):
                term = blk * cw_ref[di + 1, k:k + 1, :]
                sums[k] = term if sums[k] is None else sums[k] + term
        S = SUBLANES
        acc = (sums[1][S:S + L, :] + jnp.where(ok_left, sums[0][S - 1:S - 1 + L, :], 0.0)
               + jnp.where(ok_right, sums[2][S + 1:S + 1 + L, :], 0.0))
        qk_scr[pl.ds(r0, L), :] = _silu(acc) * qscale
        return carry

    lax.fori_loop(0, N, conv_tile, 0)

    gl = lane - GATE_LANE0
    is_f = ((gl >= H_B) & (gl < 2 * H_B)) | ((gl >= 3 * H_B) & (gl < 4 * H_B))

    def gate_tile(i, carry):
        rows = pl.ds(pl.multiple_of(i * L, L), L)
        x = sm_ref[rows, :] + bm_ref[...]
        y_scr[rows, :] = jnp.where(is_f, _log_sigmoid(x), x)
        return carry

    lax.fori_loop(0, N, gate_tile, 0)


    def state_group(ns, dirs=(0, 1)):
        units = [(d, n if d == 0 else N - 1 - n) for n in ns for d in dirs]
        rows = [_chunk_rows(n) for _, n in units]
        kt_all = [[qk_scr[r, HK + p * LANES:HK + (p + 1) * LANES].T for p in range(n_pairs)] for r in rows]
        yield
        xs = [y_scr[r, :] for r in rows]
        fsum = [_tri_sum(tri[d], x) for (d, _), x in zip(units, xs)]
        yield
        wk_all, f_end, c_end = [], [], []
        for (d, n), r, x, fs in zip(units, rows, xs, fsum):
            y = jnp.where(is_f, fs, x)
            li0 = GATE_LANE0 + 2 * H_B * d
            blk = jnp.concatenate([y, y], axis=0).T[li0:li0 + 2 * H_B, :]
            frow = pltpu.roll(blk, H_B, axis=0)
            grow = blk - frow
            g_scr[d, n] = grow
            f_scr[d, n] = frow
            e_col = L - 1 if d == 0 else 0
            f_end.append(frow[0:H_B, e_col:e_col + 1])
            ce8 = jnp.max(grow, axis=1, keepdims=True)
            c_end.append(ce8[0:H_B, :])
            wk_all.append(jnp.exp(grow[:, 0:L] - ce8))
        yield
        kv_all, ksum_all = [], []
        for r, wk8, kt_u in zip(rows, wk_all, kt_all):
            kv_u, ks_u = [], []
            wk8b = wk8.astype(BF16)
            for p in range(n_pairs):
                kpb = qk_scr[r, HK + p * LANES:HK + (p + 1) * LANES].astype(BF16)
                ks8 = _dot(wk8b, kpb)
                for j in range(2):
                    h = 2 * p + j
                    kwt = (kt_u[p][j * DK:(j + 1) * DK, :] * wk8[h:h + 1, :]).astype(BF16)
                    kv_u.append(_dot(kwt, v_ref[r, h * DV:(h + 1) * DV]))
                    ks_u.append(ks8[h:h + 1, :])
            kv_all.append(kv_u)
            ksum_all.append(ks_u)
        yield
        m_run = {d: m_scr[H_B * d:H_B * (d + 1), 0:1] for d in dirs}
        a_all, b_all = [], []
        for (d, n), fe, ce in zip(units, f_end, c_end):
            mall_scr[d, n, 0:H_B, 0:1] = m_run[d]
            mx = jnp.maximum(m_run[d], ce)
            a_all.append(jnp.exp(m_run[d] - mx))
            b_all.append(jnp.exp(ce - mx))
            m_run[d] = fe + mx
        for d in dirs:
            m_scr[H_B * d:H_B * (d + 1), 0:1] = m_run[d]
        yield
        c_run = {d: [[c_scr[d, p, j * DK:(j + 1) * DK, :] for j in range(2)] for p in range(n_pairs)] for d in dirs}
        n_run = {d: [n_scr[2 * d + p:2 * d + p + 1, :] for p in range(n_pairs)] for d in dirs}
        for (d, n), a4, b4, kv_u, ks_u in zip(units, a_all, b_all, kv_all, ksum_all):
            for p in range(n_pairs):
                nall_scr[d, n, p:p + 1, :] = n_run[d][p]
                a_s = [a4[2 * p + j:2 * p + j + 1, :] for j in range(2)]
                b_s = [b4[2 * p + j:2 * p + j + 1, :] for j in range(2)]
                for j in range(2):
                    cj = c_run[d][p][j]
                    call_scr[d, n, p, j * DK:(j + 1) * DK, :] = cj.astype(BF16)
                    c_run[d][p][j] = a_s[j] * cj + b_s[j] * kv_u[2 * p + j]
                n_run[d][p] = (jnp.where(head_mask[0], a_s[0], a_s[1]) * n_run[d][p]
                               + jnp.where(head_mask[0], b_s[0] * ks_u[2 * p], b_s[1] * ks_u[2 * p + 1]))
        for d in dirs:
            for p in range(n_pairs):
                n_scr[2 * d + p:2 * d + p + 1, :] = n_run[d][p]
                for j in range(2):
                    c_scr[d, p, j * DK:(j + 1) * DK, :] = c_run[d][p][j]

    eye = lower & upper
    ones8 = jnp.ones((SUBLANES, L), BF16)
    sub8 = lax.broadcasted_iota(jnp.int32, (SUBLANES, LANES), 0)
    sub_h = lax.broadcasted_iota(jnp.int32, (H_B, L), 0)
    n_rows = [((sub8 == 2 * p) & head_mask[0]) | ((sub8 == 2 * p + 1) & head_mask[1]) for p in range(n_pairs)]

    def head_rows(vals):
        out = vals[0][0:H_B, :]
        for h in range(1, H_B):
            out = jnp.where(sub_h == h, vals[h][0:H_B, :], out)
        return out

    def out_group(ns):
        chunks = [(d, n) for n in ns for d in range(2)]
        pairs = [(d, n, p) for d, n in chunks for p in range(n_pairs)]
        units = [(d, n, p, j) for d, n, p in pairs for j in range(2)]
        cms = [lane_cummax(g_scr[d, n], d)[0:H_B, 0:L] for d, n in chunks]
        qk2s, qc2s, qn2s = [], [], []
        for d, n, p in pairs:
            r = _chunk_rows(n)
            qp = qk_scr[r, p * LANES:(p + 1) * LANES]
            q2 = jnp.concatenate([jnp.where(head_mask[j], qp, 0.0) for j in range(2)], axis=0).astype(BF16)
            qk2s.append(_dot_nt(q2, qk_scr[r, HK + p * LANES:HK + (p + 1) * LANES].astype(BF16)))
            qc2s.append(_dot(q2, call_scr[d, n, p]))
            nsel = jnp.where(n_rows[p], nall_scr[d, n, p:p + 1, :], 0.0).astype(BF16)
            qn2s.append(_dot_nt(nsel, qp.astype(BF16)))
        yield
        s_all = []
        for ui, (d, n, p, j) in enumerate(units):
            grow = g_scr[d, n, 2 * p + j:2 * p + j + 1, 0:L]
            e = jnp.where(tmask[d], grow, -jnp.inf)
            cmax = jnp.max(e, axis=-1, keepdims=True)
            s_all.append((qk2s[ui // 2][j * L:(j + 1) * L, :] * jnp.exp(e - cmax)).astype(BF16))
        yield
        nums = [_dot(s, v_ref[_chunk_rows(n), (2 * p + j) * DV:(2 * p + j + 1) * DV])
                for (d, n, p, j), s in zip(units, s_all)]
        dens = [_dot_nt(ones8, s) for s in s_all]
        yield
        scales = []
        for ci, (d, n) in enumerate(chunks):
            den_loc = head_rows(dens[ci * H_B:(ci + 1) * H_B])
            qn = qn2s[ci * n_pairs][0:H_B, :]
            for p in range(1, n_pairs):
                qn = qn + qn2s[ci * n_pairs + p][0:H_B, :]
            cm = cms[ci]
            m_prev = mall_scr[d, n, 0:H_B, 0:1]
            delta = cm - m_prev
            t = jnp.exp(-jnp.abs(delta))
            w_loc = jnp.where(delta <= 0.0, t, 1.0)
            w_inter = jnp.where(delta <= 0.0, 1.0, t)
            mt = f_scr[d, n, 0:H_B, 0:L] + jnp.maximum(m_prev, cm)
            den = w_loc * den_loc + w_inter * qn
            rinv = 1.0 / jnp.maximum(jnp.abs(den), jnp.exp(-mt))
            scales.append((w_loc * rinv, w_inter * rinv))
        yield
        hs = []
        for ui, (d, n, p, j) in enumerate(units):
            h = 2 * p + j
            sc_loc, sc_inter = scales[ui // H_B]
            d_loc = jnp.where(eye, sc_loc[h:h + 1, :], 0.0).astype(BF16)
            d_inter = jnp.where(eye, sc_inter[h:h + 1, :], 0.0).astype(BF16)
            hs.append(_dot(d_loc, nums[ui].astype(BF16))
                      + _dot(d_inter, qc2s[ui // 2][j * L:(j + 1) * L, :].astype(BF16)))
        yield
        for ni, n in enumerate(ns):
            r = _chunk_rows(n)
            for h in range(H_B):
                vs = slice(h * DV, (h + 1) * DV)
                o = hs[(2 * ni) * H_B + h] + hs[(2 * ni + 1) * H_B + h]
                out_ref[r, vs] = (_rms(o, gw_ref[:, vs]) * _sigmoid(og_ref[r, vs].astype(F32))).astype(out_ref.dtype)

    def finish():
        if write_state:
            for d in range(2):
                for p in range(n_pairs):
                    cnew_ref[d, p] = c_scr[d, p]
                    for j in range(2):
                        nnew_ref[d, 2 * p + j:2 * p + j + 1, :] = n_scr[2 * d + p:2 * d + p + 1, j * DK:(j + 1) * DK]
                mnew_ref[d:d + 1, :] = to_row(m_scr[H_B * d:H_B * (d + 1), 0:1])

    return state_group, out_group, finish


def _mlstm_scratch(T, C2, grid_w):
    n_pairs = C2 // 2 // LANES
    n_chunks = T // CHUNK
    pad_rows = 2 * (grid_w + SUBLANES) if T // grid_w > 1 else 2 * SUBLANES
    return [
        pltpu.VMEM((T + pad_rows, C2), F32),
        pltpu.VMEM((T, C2), F32),
        pltpu.VMEM((T, SMALL_W), F32),
        pltpu.VMEM((2, n_pairs, LANES, LANES), F32),
        pltpu.VMEM((SUBLANES, LANES), F32),
        pltpu.VMEM((SUBLANES, LANES), F32),
        pltpu.VMEM((2, n_chunks, n_pairs, LANES, LANES), BF16),
        pltpu.VMEM((2, n_chunks, SUBLANES, LANES), F32),
        pltpu.VMEM((2, n_chunks, SUBLANES, LANES), F32),
        pltpu.VMEM((2, n_chunks, SUBLANES, LANES), F32),
        pltpu.VMEM((2, n_chunks, SUBLANES, LANES), F32),
    ]


N_GLA_SCRATCH = 5
N_MLSTM_SCRATCH = 11


def _scan_kernel(*refs, cols, layer, has_state, write_state, n_cast, ride_ada, grid_w, unroll):
    refs = list(refs)
    z_refs = refs[:2]
    del refs[:2]
    s0_ref = c0_ref = n0_ref = m0_ref = None
    if has_state:
        s0_ref, c0_ref, n0_ref, m0_ref = refs[:4]
        del refs[:4]
    wa_ref, bal_ref, gwa_ref, cw_ref, bmg_ref, gwb_ref = refs[:6]
    del refs[:6]
    cast_in = refs[:n_cast]
    del refs[:n_cast]
    if ride_ada:
        ada_in = refs[:4]
        del refs[:4]
    outa_ref, outb_ref = refs[:2]
    del refs[:2]
    snew_ref = cnew_ref = nnew_ref = mnew_ref = None
    if write_state:
        snew_ref, cnew_ref, nnew_ref, mnew_ref = refs[:4]
        del refs[:4]
    cast_out = refs[:n_cast]
    del refs[:n_cast]
    if ride_ada:
        ada_out = refs.pop(0)
    wal_scr, bm_scr = refs[:2]
    del refs[:2]
    gla_scr = refs[:N_GLA_SCRATCH]
    mlstm_scr = refs[N_GLA_SCRATCH:]

    for src, dst in zip(cast_in, cast_out):
        dst[...] = src[...].astype(BF16)
    if ride_ada:
        _ada_tile(*ada_in, ada_out)

    R, HK = wa_ref.shape[1], wa_ref.shape[2]
    wal_scr[...] = jnp.zeros(wal_scr.shape, BF16)
    for d in range(2):
        wal_scr[d * R:(d + 1) * R, d * HK:(d + 1) * HK] = wa_ref[d].astype(BF16)
    lane = lax.broadcasted_iota(jnp.int32, (1, LANES), 1)
    bm = jnp.zeros((1, LANES), F32)
    for g in range(bmg_ref.shape[1]):
        for h in range(H_B):
            bm = jnp.where(lane == GATE_LANE0 + H_B * g + h, bmg_ref[layer, g, h], bm)
    bm_scr[0:1, :] = bm

    def view(name):
        a, c0, w = cols[name]
        return z_refs[a].at[:, pl.ds(c0, w)]

    sm_ref = view("small")
    n_chunks = z_refs[0].shape[0] // CHUNK
    gla = _gla_body(view("qa"), view("ka"), view("va"), view("ga"), sm_ref, s0_ref, wal_scr, bal_ref, gwa_ref,
                    outa_ref, snew_ref, *gla_scr)
    mlstm = _mlstm_body(view("qkb"), view("vb"), view("ob"), sm_ref, c0_ref, n0_ref, m0_ref, cw_ref,
                        bm_scr.at[0:1, :], gwb_ref, outb_ref, cnew_ref, nnew_ref, mnew_ref, *mlstm_scr,
                        grid_w=grid_w)
    gla_state, gla_out, gla_finish, decay_span = gla
    mlstm_state, mlstm_out, mlstm_finish = mlstm

    def passes(gla_out_fn):
        _chunk_loop(n_chunks, unroll, lambda ns: [fn([n], (d,)) for n in ns for d in range(2)
                                                  for fn in (mlstm_state, gla_state)])
        _chunk_loop(n_chunks, unroll, lambda ns: [fn([n]) for n in ns for fn in (mlstm_out, gla_out_fn)])

    wide_decay = decay_span > GLA_FACTORED_DECAY_MAX

    @pl.when(jnp.logical_not(wide_decay))
    def _():
        passes(gla_out)

    @pl.when(wide_decay)
    def _():
        passes(functools.partial(gla_out, exact_decay=True))

    gla_finish()
    mlstm_finish()


def _scan_call(z2d, row0, B, T, states, lw, layer, *, grid_w, write_state, casts=(), ada=None):
    assert row0 % T == 0 and all(z.shape[0] % T == 0 for z in z2d)
    z3 = [z.reshape(z.shape[0] // T, T, z.shape[1]) for z in z2d]
    blk0 = row0 // T
    HK = lw["w_alpha2"].shape[-1]
    DA = lw["gnorm_a_w"].shape[0]
    C2 = lw["conv_w"].shape[-1]
    DB = lw["gnorm_b_w"].shape[0]
    DK_A, DK_B = HK // H_A, C2 // 2 // H_B
    pa, pb = HK // LANES, C2 // 2 // LANES
    n_chunks = T // CHUNK
    has_state = states is not None
    widths = ((("qa", HK), ("ka", HK), ("qkb", C2), ("small", SMALL_W)),
              (("va", DA), ("ga", DA), ("vb", DB), ("ob", DB)))
    cols = {}
    for a, groups in enumerate(widths):
        c0 = 0
        for name, w in groups:
            cols[name] = (a, c0, w)
            c0 += w
        assert c0 == z3[a].shape[2]
    cast_in_specs, cast_out_specs, cast_out_shape, cast_args = _cast_specs(casts, B)
    kern = functools.partial(_scan_kernel, cols=cols, layer=layer, has_state=has_state, write_state=write_state,
                             n_cast=len(casts), ride_ada=ada is not None, grid_w=grid_w,
                             unroll=min(n_chunks, SCAN_UNROLL))

    def per_batch(shape):
        nd = len(shape)
        return pl.BlockSpec((None,) + tuple(shape), lambda b: (b,) + (0,) * nd)

    def per_batch_layer(shape):
        nd = len(shape)
        return pl.BlockSpec((None, None) + tuple(shape), lambda b: (b, layer) + (0,) * nd)

    def of_layer(a):
        return pl.BlockSpec((None,) + a.shape[1:], lambda b: (layer,) + (0,) * (a.ndim - 1))

    def whole(a):
        return pl.BlockSpec(a.shape, lambda b: (0,) * a.ndim)

    state_shapes = ((2, pa, LANES, LANES), (2, pb, LANES, LANES), (2, H_B, DK_B), (2, H_B))
    in_specs = [pl.BlockSpec((None, T, z.shape[2]), lambda b: (b + blk0, 0, 0)) for z in z3]
    args = list(z3)
    if has_state:
        s_gla, s_c, s_n, s_m = states
        depth = s_gla.shape[1]
        args += [s_gla.reshape((B, depth) + state_shapes[0]), s_c.reshape((B, depth) + state_shapes[1]), s_n, s_m]
        in_specs += [per_batch_layer(s) for s in state_shapes]
    args += [lw["w_alpha2"], lw["b_alpha"], lw["gnorm_a_w"].reshape(1, DA), lw["conv_w"], lw["b_mgate"],
             lw["gnorm_b_w"].reshape(1, DB)]
    in_specs += [of_layer(lw["w_alpha2"]), of_layer(lw["b_alpha"]), pl.BlockSpec((1, DA), lambda b: (0, 0)),
                 whole(lw["conv_w"]), pl.BlockSpec(memory_space=pltpu.SMEM), pl.BlockSpec((1, DB), lambda b: (0, 0))]
    args += cast_args
    in_specs += cast_in_specs
    out_specs = [per_batch((T, DA)), per_batch((T, DB))]
    out_shape = [jax.ShapeDtypeStruct((B, T, DA), BF16), jax.ShapeDtypeStruct((B, T, DB), BF16)]
    if write_state:
        out_specs += [per_batch(s) for s in state_shapes]
        out_shape += [jax.ShapeDtypeStruct((B,) + s, F32) for s in state_shapes]
    out_specs += cast_out_specs
    out_shape += cast_out_shape
    if ada is not None:
        cc, c, w_ada, b_ada, col0 = ada
        n_rest = w_ada.shape[1] - col0
        wcol = n_rest // B
        assert n_rest % B == 0 and wcol % LANES == 0 and col0 % wcol == 0
        args += [cc, c, w_ada, b_ada]
        in_specs += [whole(cc), whole(c),
                     pl.BlockSpec((w_ada.shape[0], wcol), lambda b: (0, col0 // wcol + b)),
                     pl.BlockSpec((1, wcol), lambda b: (0, col0 // wcol + b))]
        out_specs.append(pl.BlockSpec((COND_ROWS, wcol), lambda b: (0, b)))
        out_shape.append(jax.ShapeDtypeStruct((COND_ROWS, n_rest), F32))
    scratch = ([pltpu.VMEM((SMALL_W, 2 * HK), BF16), pltpu.VMEM((SUBLANES, LANES), F32)]
               + _gla_scratch(T, HK) + _mlstm_scratch(T, C2, grid_w))
    assert len(scratch) == 2 + N_GLA_SCRATCH + N_MLSTM_SCRATCH
    return pl.pallas_call(
        kern,
        grid=(B,),
        in_specs=in_specs,
        out_specs=out_specs,
        out_shape=out_shape,
        scratch_shapes=scratch,
        compiler_params=pltpu.CompilerParams(dimension_semantics=("arbitrary",),
                                             vmem_limit_bytes=VMEM_LIMIT),
        name="mixer_scans",
    )(*args)


def _outff_kernel(xc_ref, xl_ref, ac_ref, al_ref, bc_ref, bl_ref, mod_ref, n2_ref, fn_ref, wo_ref, w1_ref, w2_ref,
                  yc_ref, yl_ref, *, n_ctx, tiles_per_req, ff_chunk, final_norm):
    D = xc_ref.shape[1]
    DA = ac_ref.shape[1]
    is_ctx, row = _tile_group(n_ctx, tiles_per_req)

    def mod(k):
        return mod_ref[pl.ds(row, 1), (k - MOD_SPLIT) * D:(k - MOD_SPLIT + 1) * D]

    def tile(x_ref, a_ref, b_ref, y_ref):
        y = _dot(a_ref[...], wo_ref[0:DA, :]) + _dot(b_ref[...], wo_ref[DA:, :])
        x1 = x_ref[...] + mod(2) * y
        h2 = (_rms(x1, n2_ref[...]) * (1.0 + mod(4)) + mod(3)).astype(BF16)
        acc = jnp.zeros(x1.shape, F32)
        for c0 in range(0, w1_ref.shape[1], ff_chunk):
            u = jnp.maximum(_dot(h2, w1_ref[:, c0:c0 + ff_chunk]), 0.0)
            acc = acc + _dot((u * u).astype(BF16), w2_ref[c0:c0 + ff_chunk, :])
        x2 = x1 + mod(5) * acc
        y_ref[...] = _rms(x2, fn_ref[...]) if final_norm else x2

    @pl.when(is_ctx)
    def _():
        tile(xc_ref, ac_ref, bc_ref, yc_ref)

    @pl.when(jnp.logical_not(is_ctx))
    def _():
        tile(xl_ref, al_ref, bl_ref, yl_ref)


def _outff_call(xc2d, xl2d, ac, al, bc, bl, mod, norm2_w, final_w, wo, w1, w2, *, tm, tiles_per_req, final_norm):
    (Mc, D), Ml = xc2d.shape, xl2d.shape[0]
    n_ctx = Mc // tm
    DA = ac.shape[1]
    DFF = w1.shape[1]
    kern = functools.partial(_outff_kernel, n_ctx=n_ctx, tiles_per_req=tiles_per_req, ff_chunk=FF_CHUNK,
                             final_norm=final_norm)
    once = pl.Buffered(1)
    ctx, lat = _ctx_tile(n_ctx), _lat_tile(n_ctx)
    return pl.pallas_call(
        kern,
        grid=((Mc + Ml) // tm,),
        in_specs=[
            pl.BlockSpec((tm, D), ctx), pl.BlockSpec((tm, D), lat),
            pl.BlockSpec((tm, DA), ctx), pl.BlockSpec((tm, DA), lat),
            pl.BlockSpec((tm, D - DA), ctx), pl.BlockSpec((tm, D - DA), lat),
            pl.BlockSpec(mod.shape, lambda i: (0, 0)),
            pl.BlockSpec((1, D), lambda i: (0, 0)),
            pl.BlockSpec((1, D), lambda i: (0, 0)),
            pl.BlockSpec((D, D), lambda i: (0, 0), pipeline_mode=once),
            pl.BlockSpec((D, DFF), lambda i: (0, 0), pipeline_mode=once),
            pl.BlockSpec((DFF, D), lambda i: (0, 0), pipeline_mode=once),
        ],
        out_specs=[pl.BlockSpec((tm, D), ctx), pl.BlockSpec((tm, D), lat)],
        out_shape=[jax.ShapeDtypeStruct((Mc, D), F32), jax.ShapeDtypeStruct((Ml, D), F32)],
        compiler_params=pltpu.CompilerParams(dimension_semantics=("arbitrary",),
                                             vmem_limit_bytes=VMEM_LIMIT),
        name="outproj_mlp",
    )(xc2d, xl2d, ac, al, bc, bl, mod, norm2_w.reshape(1, D), final_w.reshape(1, D), wo, w1, w2)


def _layer(xc, xl, cond, ada_w, cached, lw, layer, ffw, final_w, final_norm):
    (Bc, Tc, D), (Bl, Tl, _) = xc.shape, xl.shape
    tm = TOKEN_TILE
    assert (Bc * Tc) % tm == 0 and Tl % tm == 0 and (Bc * Tc) % Tl == 0
    xc2d, xl2d = xc.reshape(Bc * Tc, D), xl.reshape(Bl * Tl, D)
    z = _inproj_call(xc2d, xl2d, *cond, *ada_w, lw["norm1_w"], lw["w_in_t"], tm=tm, tiles_per_req=Tl // tm,
                     f32_rows=lw["f32_rows"], small_rows=lw["small_rows"], bf16_rows=lw["bf16_rows"])
    res_c = _scan_call(z, 0, Bc, Tc, None, lw, layer, grid_w=Tc, write_state=True,
                       casts=((ffw[0], 0), (ffw[1], 0), (ffw[2], 0)), ada=(*cond, *ada_w, MOD_SPLIT * D))
    res_l = _scan_call(z, Bc * Tc, Bl, Tl, cached, lw, layer, grid_w=GRID_W, write_state=False)
    wo_b, w1_b, w2_b, mod_out = res_c[-4:]
    yc, yl = _outff_call(xc2d, xl2d, res_c[0].reshape(Bc * Tc, -1), res_l[0].reshape(Bl * Tl, -1),
                         res_c[1].reshape(Bc * Tc, -1), res_l[1].reshape(Bl * Tl, -1), mod_out, lw["norm2_w"],
                         final_w, wo_b, w1_b, w2_b, tm=tm, tiles_per_req=Tl // tm, final_norm=final_norm)
    return yc.reshape(Bc, Tc, D), yl.reshape(Bl, Tl, D), tuple(res_c[2:6])


def _layer_weights(l, norm1_w, norm2_w, w_in, w_alpha2, b_alpha, b_mgate, conv_w, gnorm_a_w, gnorm_b_w):
    hk_a = w_alpha2.shape[-1]
    d_a = gnorm_a_w.shape[-1]
    d_b = gnorm_b_w.shape[-1]
    hk_b = conv_w.shape[-1] // 2
    sizes = (hk_a, hk_a, d_a, d_a, 2 * R_ALPHA, hk_b, hk_b, d_b, d_b, 4 * H_B)
    assert w_alpha2.shape[2] == R_ALPHA and b_mgate.shape[1] * b_mgate.shape[2] == 4 * H_B
    offs = [0]
    for s in sizes:
        offs.append(offs[-1] + s)
    f32_rows = ((offs[0], offs[2] - offs[0]), (offs[5], offs[7] - offs[5]))
    small_rows = ((offs[4], offs[5] - offs[4]), (offs[9], offs[10] - offs[9]))
    bf16_rows = ((offs[2], offs[4] - offs[2]), (offs[7], offs[9] - offs[7]))
    assert all(n % LANES == 0 and r % BF16_ROWS == 0 for r, n in f32_rows + bf16_rows)
    return dict(
        norm1_w=norm1_w[l], norm2_w=norm2_w[l], w_in_t=jnp.swapaxes(w_in[l], 0, 1),
        f32_rows=f32_rows, small_rows=small_rows, bf16_rows=bf16_rows,
        w_alpha2=w_alpha2, b_alpha=b_alpha, b_mgate=b_mgate, conv_w=conv_w[l],
        gnorm_a_w=gnorm_a_w[l], gnorm_b_w=gnorm_b_w[l],
    )


def kernel(x_prompt, x_sample, c, state_gla, state_mlstm_C, state_mlstm_n, state_mlstm_m, c_ctx, w_ada, b_ada, norm1_w, norm2_w, w_in, w_alpha2, b_alpha, b_mgate, conv_w, gnorm_a_w, gnorm_b_w, w_out, w_ff1, w_ff2, final_norm_w):
    depth = w_in.shape[0]
    D = x_prompt.shape[-1]
    Bp, Tp, _ = x_prompt.shape
    Bs = x_sample.shape[0]
    assert 1 + Bs <= COND_ROWS
    cond = (c_ctx.reshape(1, D), c)
    cached = (state_gla, state_mlstm_C, state_mlstm_n, state_mlstm_m)
    xp, xs = x_prompt, x_sample
    s_gla, s_c, s_n, s_m = [], [], [], []
    for l in range(depth):
        lw = _layer_weights(l, norm1_w, norm2_w, w_in, w_alpha2, b_alpha, b_mgate, conv_w,
                            gnorm_a_w, gnorm_b_w)
        xp, xs, ctx = _layer(xp, xs, cond, (w_ada[l], b_ada[l].reshape(1, -1)), cached, lw, l,
                             (w_out[l], w_ff1[l], w_ff2[l]), final_norm_w, l == depth - 1)
        s_gla.append(ctx[0].reshape(Bp, 2, H_A, -1, ctx[0].shape[-1]))
        s_c.append(ctx[1].reshape(Bp, 2, H_B, -1, ctx[1].shape[-1]))
        s_n.append(ctx[2])
        s_m.append(ctx[3])
    dt = x_prompt.dtype
    return (xp, xs, jnp.stack(s_gla, axis=1).astype(dt), jnp.stack(s_c, axis=1).astype(dt),
            jnp.stack(s_n, axis=1).astype(dt), jnp.stack(s_m, axis=1).astype(dt))
```

```python
import functools
import math

import jax
import jax.numpy as jnp
from jax import lax
from jax.experimental import pallas as pl
from jax.experimental.pallas import tpu as pltpu

F32 = jnp.float32
BF16 = jnp.bfloat16

GRID_W = 64
H_A = 4
H_B = 4
R_ALPHA = 16
TAU_GLA = 16.0
CHUNK = 64
EPS = 1e-6
LANES = 128
SUBLANES = 8
BF16_ROWS = 16
COND_ROWS = SUBLANES
SMALL_W = LANES
GATE_LANE0 = 2 * R_ALPHA
VMEM_LIMIT = 56 * 1024 * 1024
SCAN_UNROLL = 4
PIPELINE_STARTS = 8
TOKEN_TILE = 512
FF_CHUNK = 1024
GLA_FACTORED_DECAY_MAX = 60.0
MOD_SPLIT = 2


def _sigmoid(x):
    return 1.0 / (1.0 + jnp.exp(-x))


def _silu(x):
    return x * _sigmoid(x)


def _log_sigmoid(x):
    return jnp.minimum(x, 0.0) - jnp.log(1.0 + jnp.exp(-jnp.abs(x)))


def _dot(a, b):
    return jnp.dot(a, b, preferred_element_type=F32)


def _dot_nt(a, b):
    return lax.dot_general(a, b, (((1,), (1,)), ((), ())), preferred_element_type=F32)


def _rms(x, w):
    return x * lax.rsqrt(jnp.mean(x * x, axis=-1, keepdims=True) + EPS) * w


def _tri_sum(tri, x, terms=3):
    acc, rest = None, x
    for t in range(terms):
        part = rest.astype(BF16)
        prod = _dot(tri, part)
        acc = prod if acc is None else acc + prod
        if t + 1 < terms:
            rest = rest - part.astype(F32)
    return acc


def _chunk_masks(L):
    row = lax.broadcasted_iota(jnp.int32, (L, L), 0)
    col = lax.broadcasted_iota(jnp.int32, (L, L), 1)
    lower = row >= col
    upper = row <= col
    return lower, upper


def _ada_tile(cc_ref, c_ref, w_ref, b_ref, o_ref):
    D = cc_ref.shape[1]
    sub = lax.broadcasted_iota(jnp.int32, (COND_ROWS, D), 0)
    cond = jnp.where(sub == 0, cc_ref[...], 0.0)
    for r in range(c_ref.shape[0]):
        cond = jnp.where(sub == 1 + r, c_ref[r:r + 1, :], cond)
    o_ref[...] = _dot(_silu(cond).astype(BF16), w_ref[...].astype(BF16)) + b_ref[...]


def _tile_group(n_ctx, tiles_per_req):
    i = pl.program_id(0)
    is_ctx = i < n_ctx
    row = jnp.where(is_ctx, 0, 1 + jnp.maximum(i - n_ctx, 0) // tiles_per_req)
    return is_ctx, row


def _ctx_tile(n_ctx):
    return lambda i: (jnp.minimum(i, n_ctx - 1), 0)


def _lat_tile(n_ctx):
    return lambda i: (jnp.maximum(i - n_ctx, 0), 0)


def _inproj_kernel(xc_ref, xl_ref, cc_ref, c_ref, wa_ref, ba_ref, nw_ref, wt_ref, zf_ref, zh_ref, wb_scr, mod_ref,
                   *, n_ctx, tiles_per_req, f32_rows, small_rows, bf16_rows):
    D = xc_ref.shape[1]
    n_f32 = zf_ref.shape[1]

    @pl.when(pl.program_id(0) == 0)
    def _():
        _ada_tile(cc_ref, c_ref, wa_ref, ba_ref, mod_ref)

        def wide(rows, col):
            for r0, n in rows:
                wb_scr[col:col + n, :] = wt_ref[r0:r0 + n, :].astype(BF16)
                col += n
            return col

        col = wide(f32_rows, 0)
        parts = [wt_ref[r0:r0 + n, :] for r0, n in small_rows]
        n_small = sum(n for _, n in small_rows)
        parts.append(jnp.zeros((SMALL_W - n_small, D), F32))
        wb_scr[col:col + SMALL_W, :] = jnp.concatenate(parts, axis=0).astype(BF16)
        wide(bf16_rows, col + SMALL_W)

    is_ctx, row = _tile_group(n_ctx, tiles_per_req)

    def tile(x_ref):
        sh1 = mod_ref[pl.ds(row, 1), 0:D]
        sc1 = mod_ref[pl.ds(row, 1), D:2 * D]
        h = (_rms(x_ref[...], nw_ref[...]) * (1.0 + sc1) + sh1).astype(BF16)
        zf_ref[...] = _dot_nt(h, wb_scr[0:n_f32, :])
        zh_ref[...] = _dot_nt(h, wb_scr[n_f32:, :]).astype(BF16)

    @pl.when(is_ctx)
    def _():
        tile(xc_ref)

    @pl.when(jnp.logical_not(is_ctx))
    def _():
        tile(xl_ref)


def _inproj_call(xc2d, xl2d, cc, c, w_ada, b_ada, norm_w, w_in_t, *, tm, tiles_per_req, f32_rows, small_rows,
                 bf16_rows):
    (Mc, D), Ml = xc2d.shape, xl2d.shape[0]
    n_ctx = Mc // tm
    n_f32 = sum(n for _, n in f32_rows) + SMALL_W
    n_bf16 = sum(n for _, n in bf16_rows)
    n_out = n_f32 + n_bf16
    n_mod = MOD_SPLIT * D
    kern = functools.partial(_inproj_kernel, n_ctx=n_ctx, tiles_per_req=tiles_per_req,
                             f32_rows=f32_rows, small_rows=small_rows, bf16_rows=bf16_rows)
    once = pl.Buffered(1)
    return pl.pallas_call(
        kern,
        grid=((Mc + Ml) // tm,),
        in_specs=[
            pl.BlockSpec((tm, D), _ctx_tile(n_ctx)),
            pl.BlockSpec((tm, D), _lat_tile(n_ctx)),
            pl.BlockSpec(cc.shape, lambda i: (0, 0)),
            pl.BlockSpec(c.shape, lambda i: (0, 0)),
            pl.BlockSpec((D, n_mod), lambda i: (0, 0), pipeline_mode=once),
            pl.BlockSpec((1, n_mod), lambda i: (0, 0)),
            pl.BlockSpec((1, D), lambda i: (0, 0)),
            pl.BlockSpec(w_in_t.shape, lambda i: (0, 0), pipeline_mode=once),
        ],
        out_specs=[pl.BlockSpec((tm, n_f32), lambda i: (i, 0)), pl.BlockSpec((tm, n_bf16), lambda i: (i, 0))],
        out_shape=[jax.ShapeDtypeStruct((Mc + Ml, n_f32), F32), jax.ShapeDtypeStruct((Mc + Ml, n_bf16), BF16)],
        scratch_shapes=[pltpu.VMEM((n_out, D), BF16), pltpu.VMEM((COND_ROWS, n_mod), F32)],
        compiler_params=pltpu.CompilerParams(dimension_semantics=("arbitrary",),
                                             vmem_limit_bytes=VMEM_LIMIT),
        name="norm_inproj",
    )(xc2d, xl2d, cc, c, w_ada, b_ada, norm_w.reshape(1, D), w_in_t)


def _chunk_loop(n_chunks, unroll, make_units):
    def step(ns):
        pending = list(make_units(ns))
        active = []
        while pending or active:
            for _ in range(min(PIPELINE_STARTS, len(pending))):
                active.append(pending.pop(0))
            alive = []
            for g in active:
                try:
                    next(g)
                    alive.append(g)
                except StopIteration:
                    pass
            active = alive

    if unroll >= n_chunks:
        step(list(range(n_chunks)))
        return

    def body(i, carry):
        step([i * unroll + u for u in range(unroll)])
        return carry

    lax.fori_loop(0, n_chunks // unroll, body, 0)


def _chunk_rows(n):
    if isinstance(n, int):
        return pl.ds(n * CHUNK, CHUNK)
    return pl.ds(pl.multiple_of(n * CHUNK, CHUNK), CHUNK)


def _cast_specs(casts, n_steps):
    in_specs, out_specs, out_shape, args = [], [], [], []
    for w, axis in casts:
        blk = list(w.shape)
        assert blk[axis] % n_steps == 0
        blk[axis] //= n_steps
        assert blk[0] % BF16_ROWS == 0 and blk[1] % LANES == 0
        idx = (lambda b: (b, 0)) if axis == 0 else (lambda b: (0, b))
        in_specs.append(pl.BlockSpec(tuple(blk), idx))
        out_specs.append(pl.BlockSpec(tuple(blk), idx))
        out_shape.append(jax.ShapeDtypeStruct(w.shape, BF16))
        args.append(w)
    return in_specs, out_specs, out_shape, args


def _gla_body(q_ref, k_ref, v_ref, g_ref, sm_ref, s0_ref, wal_ref, bal_ref, gw_ref, out_ref, snew_ref,
              st_scr, sall_scr, qh_scr, qs_scr, kh_scr):
    has_state = s0_ref is not None
    write_state = snew_ref is not None
    T = q_ref.shape[0]
    L = CHUNK
    N = T // L
    HK = q_ref.shape[1]
    DK = HK // H_A
    DV = v_ref.shape[1] // H_A
    scale = DK ** -0.5
    n_pairs = HK // LANES

    lower, upper = _chunk_masks(L)
    tri = (lower.astype(BF16), upper.astype(BF16))
    tmask = (lower, upper)
    lane = lax.broadcasted_iota(jnp.int32, (1, LANES), 1)
    head_mask = (lane < DK, lane >= DK)

    for d in range(2):
        for p in range(n_pairs):
            if has_state:
                st_scr[d, p] = s0_ref[d, p].T
            else:
                st_scr[d, p] = jnp.zeros((LANES, LANES), F32)

    def decay_pre(d, r):
        return _dot(sm_ref[r, :].astype(BF16), wal_ref[:, d * HK:(d + 1) * HK]) + bal_ref[d:d + 1, :]

    neg_pre = jnp.maximum(-(_dot(sm_ref[...].astype(BF16), wal_ref[...])
                            + jnp.concatenate([bal_ref[0:1, :], bal_ref[1:2, :]], axis=1)), 0.0)
    chunk_sums = jnp.sum(neg_pre.reshape(N, L, 2 * HK), axis=1)
    decay_span = (jnp.max(chunk_sums) + L * math.log(2.0)) * (1.0 / TAU_GLA)

    def state_group(ns, dirs=(0, 1)):
        units = [(d, n if d == 0 else N - 1 - n) for n in ns for d in dirs]
        rows = [_chunk_rows(n) for _, n in units]
        vt_all = [[jnp.concatenate([v_ref[r, (2 * p + j) * DV:(2 * p + j + 1) * DV] for j in range(2)],
                                   axis=0).astype(F32).T.astype(BF16) for p in range(n_pairs)] for r in rows]
        yield
        pre = [decay_pre(d, r) for (d, _), r in zip(units, rows)]
        yield
        g = [_log_sigmoid(x) * (1.0 / TAU_GLA) for x in pre]
        yield
        b = [_tri_sum(tri[d], gi, terms=2) for (d, _), gi in zip(units, g)]
        yield
        ks_all, dec_all = [], []
        for (d, _), r, bi in zip(units, rows, b):
            bend = bi[L - 1:L, :] if d == 0 else bi[0:1, :]
            q = q_ref[r, :] * scale
            ks = (k_ref[r, :] * jnp.exp(bend - bi)).astype(BF16)
            qs = q * jnp.exp(bi)
            qh_scr[d, r, :] = (qs * jnp.exp(-bend)).astype(BF16)
            qs_scr[d, r, :] = qs.astype(BF16)
            kh_scr[d, r, :] = ks
            ks_all.append(ks)
            dec_all.append(jnp.exp(bend))
        yield
        upd_all = []
        for vt_u, ks in zip(vt_all, ks_all):
            upd_u = []
            for p in range(n_pairs):
                kp = ks[:, p * LANES:(p + 1) * LANES]
                kk = jnp.concatenate([jnp.where(head_mask[j], kp, jnp.zeros_like(kp)) for j in range(2)], axis=0)
                upd_u.append(_dot(vt_u[p], kk))
            upd_all.append(upd_u)
        yield
        st = {d: [st_scr[d, p] for p in range(n_pairs)] for d in dirs}
        for (d, n), dec, upd in zip(units, dec_all, upd_all):
            for p in range(n_pairs):
                sall_scr[d, n, p] = st[d][p].astype(BF16)
                st[d][p] = st[d][p] * dec[:, p * LANES:(p + 1) * LANES] + upd[p]
        for d in dirs:
            for p in range(n_pairs):
                st_scr[d, p] = st[d][p]

    def stack_heads(x):
        return jnp.concatenate([jnp.where(head_mask[j], x, jnp.zeros_like(x)) for j in range(2)], axis=0)

    tok = lax.broadcasted_iota(jnp.int32, (L, 1), 0)
    row_t = lax.broadcasted_iota(jnp.int32, (2 * L, L), 0) & (L - 1)
    col_s = lax.broadcasted_iota(jnp.int32, (2 * L, L), 1)

    def exact_scores(d, r, p):
        ls = slice(p * LANES, (p + 1) * LANES)
        b = _tri_sum(tri[d], _log_sigmoid(decay_pre(d, r)[:, ls]) * (1.0 / TAU_GLA))
        q = q_ref[r, ls] * scale
        k = k_ref[r, ls]
        acc = jnp.where(row_t == col_s, _dot_nt(stack_heads(q).astype(BF16), k.astype(BF16)), 0.0)
        src = lax.broadcasted_iota(jnp.int32, (L, L), 1)
        h = L // 2
        while h >= 1:
            first = tok & ~(2 * h - 1)
            edge = first + (h - 1 if d == 0 else h)
            b_edge = _tri_sum((src == edge).astype(BF16), b)
            upper = (tok & (2 * h - 1)) >= h
            later, earlier = (upper, ~upper) if d == 0 else (~upper, upper)
            qt = jnp.where(later, q * jnp.exp(b - b_edge), 0.0)
            kt = jnp.where(earlier, k * jnp.exp(b_edge - b), 0.0)
            sc = _dot_nt(stack_heads(qt).astype(BF16), kt.astype(BF16))
            acc = acc + jnp.where((row_t & ~(2 * h - 1)) == (col_s & ~(2 * h - 1)), sc, 0.0)
            h //= 2
        return acc

    def out_group(ns, exact_decay=False):
        pairs = [(d, ni, p) for ni in range(len(ns)) for d in range(2) for p in range(n_pairs)]
        scores, inter = [], []
        for d, ni, p in pairs:
            r = _chunk_rows(ns[ni])
            ls = slice(p * LANES, (p + 1) * LANES)
            if exact_decay:
                scores.append(exact_scores(d, r, p))
            else:
                scores.append(_dot_nt(stack_heads(qh_scr[d, r, ls]), kh_scr[d, r, ls]))
            inter.append(_dot_nt(stack_heads(qs_scr[d, r, ls]), sall_scr[d, ns[ni], p]))
        yield
        probs = [[jnp.where(tmask[d], sc[j * L:(j + 1) * L, :], 0.0).astype(BF16) for j in range(2)]
                 for (d, _, _), sc in zip(pairs, scores)]
        yield
        outs = {}
        for (d, ni, p), pr, it in zip(pairs, probs, inter):
            r = _chunk_rows(ns[ni])
            for j in range(2):
                vs = slice((2 * p + j) * DV, (2 * p + j + 1) * DV)
                outs[(d, ni, 2 * p + j)] = _dot(pr[j], v_ref[r, vs]) + it[j * L:(j + 1) * L, :]
        yield
        for ni, n in enumerate(ns):
            r = _chunk_rows(n)
            for h in range(H_A):
                vs = slice(h * DV, (h + 1) * DV)
                o = outs[(0, ni, h)] + outs[(1, ni, h)]
                out_ref[r, vs] = (_rms(o, gw_ref[:, vs]) * _silu(g_ref[r, vs].astype(F32))).astype(out_ref.dtype)

    def finish():
        if write_state:
            for d in range(2):
                for p in range(n_pairs):
                    snew_ref[d, p] = st_scr[d, p].T

    return state_group, out_group, finish, decay_span


def _gla_scratch(T, HK):
    n_pairs = HK // LANES
    n_chunks = T // CHUNK
    return [
        pltpu.VMEM((2, n_pairs, LANES, LANES), F32),
        pltpu.VMEM((2, n_chunks, n_pairs, LANES, LANES), BF16),
        pltpu.VMEM((2, T, HK), BF16),
        pltpu.VMEM((2, T, HK), BF16),
        pltpu.VMEM((2, T, HK), BF16),
    ]


def _mlstm_body(qk_ref, v_ref, og_ref, sm_ref, c0_ref, n0_ref, m0_ref, cw_ref, bm_ref, gw_ref,
                out_ref, cnew_ref, nnew_ref, mnew_ref,
                pad_scr, qk_scr, y_scr, c_scr, n_scr, m_scr, call_scr, nall_scr, mall_scr, g_scr, f_scr,
                *, grid_w):
    has_state = c0_ref is not None
    write_state = cnew_ref is not None
    T = qk_ref.shape[0]
    L = CHUNK
    N = T // L
    C2 = qk_ref.shape[1]
    HK = C2 // 2
    DK = HK // H_B
    DV = v_ref.shape[1] // H_B
    scale = DK ** -0.5
    n_pairs = HK // LANES
    P = pad_scr.shape[0] - T
    P0 = P // 2
    rows_img = T // grid_w

    lower, upper = _chunk_masks(L)
    tri = (lower.astype(BF16), upper.astype(BF16))
    tmask = (lower, upper)
    lane = lax.broadcasted_iota(jnp.int32, (1, LANES), 1)
    head_mask = (lane < DK, lane >= DK)
    lane_in = lane & (L - 1)

    def lane_cummax(x, d):
        k = 1
        while k < L:
            if d == 0:
                x = jnp.maximum(x, jnp.where(lane_in >= k, pltpu.roll(x, k, axis=1), -jnp.inf))
            else:
                x = jnp.maximum(x, jnp.where(lane_in < L - k, pltpu.roll(x, LANES - k, axis=1), -jnp.inf))
            k *= 2
        return x

    for d in range(2):
        for p in range(n_pairs):
            if has_state:
                c_scr[d, p] = c0_ref[d, p]
                n_scr[2 * d + p:2 * d + p + 1, :] = jnp.concatenate(
                    [n0_ref[d, 2 * p + j:2 * p + j + 1, :] for j in range(2)], axis=1)
            else:
                c_scr[d, p] = jnp.zeros((LANES, LANES), F32)
                n_scr[2 * d + p:2 * d + p + 1, :] = jnp.zeros((1, LANES), F32)
    eye_h = (lax.broadcasted_iota(jnp.int32, (H_B, H_B), 0) == lax.broadcasted_iota(jnp.int32, (H_B, H_B), 1))

    def to_col(row):
        return jnp.sum(jnp.where(eye_h, row, 0.0), axis=1, keepdims=True)

    def to_row(col):
        return jnp.sum(jnp.where(eye_h, col, 0.0), axis=0, keepdims=True)

    for d in range(2):
        if has_state:
            m_scr[H_B * d:H_B * (d + 1), 0:1] = to_col(m0_ref[d:d + 1, :])
        else:
            m_scr[H_B * d:H_B * (d + 1), 0:1] = jnp.zeros((H_B, 1), F32)

    pad_scr[0:P0, :] = jnp.zeros((P0, C2), F32)
    pad_scr[P0 + T:P + T, :] = jnp.zeros((P - P0, C2), F32)

    def copy_in(i, carry):
        r0 = pl.multiple_of(i * L, L)
        pad_scr[pl.ds(P0 + r0, L), :] = qk_ref[pl.ds(r0, L), :]
        return carry

    lax.fori_loop(0, N, copy_in, 0)

    lane_c = lax.broadcasted_iota(jnp.int32, (1, C2), 1)
    qscale = jnp.where(lane_c < HK, scale, 1.0).astype(F32)
    sub = lax.broadcasted_iota(jnp.int32, (L, 1), 0)
    img_rows = (0,) if rows_img == 1 else (-1, 0, 1)

    def conv_tile(i, carry):
        r0 = pl.multiple_of(i * L, L)
        col = lax.rem(r0, grid_w) + sub
        ok_left = col >= 1
        ok_right = col <= grid_w - 2
        sums = [None, None, None]
        for di in img_rows:
            blk = pad_scr[pl.ds(P0 + r0 + di * grid_w - SUBLANES, L + 2 * SUBLANES), :]
            for k in range(3):
                term = blk * cw_ref[di + 1, k:k + 1, :]
                sums[k] = term if sums[k] is None else sums[k] + term
        S = SUBLANES
        acc = (sums[1][S:S + L, :] + jnp.where(ok_left, sums[0][S - 1:S - 1 + L, :], 0.0)
               + jnp.where(ok_right, sums[2][S + 1:S + 1 + L, :], 0.0))
        qk_scr[pl.ds(r0, L), :] = _silu(acc) * qscale
        return carry

    lax.fori_loop(0, N, conv_tile, 0)

    gl = lane - GATE_LANE0
    is_f = ((gl >= H_B) & (gl < 2 * H_B)) | ((gl >= 3 * H_B) & (gl < 4 * H_B))

    def gate_tile(i, carry):
        rows = pl.ds(pl.multiple_of(i * L, L), L)
        x = sm_ref[rows, :] + bm_ref[...]
        y_scr[rows, :] = jnp.where(is_f, _log_sigmoid(x), x)
        return carry

    lax.fori_loop(0, N, gate_tile, 0)


    def state_group(ns, dirs=(0, 1)):
        units = [(d, n if d == 0 else N - 1 - n) for n in ns for d in dirs]
        rows = [_chunk_rows(n) for _, n in units]
        kt_all = [[qk_scr[r, HK + p * LANES:HK + (p + 1) * LANES].T for p in range(n_pairs)] for r in rows]
        yield
        xs = [y_scr[r, :] for r in rows]
        fsum = [_tri_sum(tri[d], x) for (d, _), x in zip(units, xs)]
        yield
        wk_all, f_end, c_end = [], [], []
        for (d, n), r, x, fs in zip(units, rows, xs, fsum):
            y = jnp.where(is_f, fs, x)
            li0 = GATE_LANE0 + 2 * H_B * d
            blk = jnp.concatenate([y, y], axis=0).T[li0:li0 + 2 * H_B, :]
            frow = pltpu.roll(blk, H_B, axis=0)
            grow = blk - frow
            g_scr[d, n] = grow
            f_scr[d, n] = frow
            e_col = L - 1 if d == 0 else 0
            f_end.append(frow[0:H_B, e_col:e_col + 1])
            ce8 = jnp.max(grow, axis=1, keepdims=True)
            c_end.append(ce8[0:H_B, :])
            wk_all.append(jnp.exp(grow[:, 0:L] - ce8))
        yield
        kv_all, ksum_all = [], []
        for r, wk8, kt_u in zip(rows, wk_all, kt_all):
            kv_u, ks_u = [], []
            wk8b = wk8.astype(BF16)
            for p in range(n_pairs):
                kpb = qk_scr[r, HK + p * LANES:HK + (p + 1) * LANES].astype(BF16)
                ks8 = _dot(wk8b, kpb)
                for j in range(2):
                    h = 2 * p + j
                    kwt = (kt_u[p][j * DK:(j + 1) * DK, :] * wk8[h:h + 1, :]).astype(BF16)
                    kv_u.append(_dot(kwt, v_ref[r, h * DV:(h + 1) * DV]))
                    ks_u.append(ks8[h:h + 1, :])
            kv_all.append(kv_u)
            ksum_all.append(ks_u)
        yield
        m_run = {d: m_scr[H_B * d:H_B * (d + 1), 0:1] for d in dirs}
        a_all, b_all = [], []
        for (d, n), fe, ce in zip(units, f_end, c_end):
            mall_scr[d, n, 0:H_B, 0:1] = m_run[d]
            mx = jnp.maximum(m_run[d], ce)
            a_all.append(jnp.exp(m_run[d] - mx))
            b_all.append(jnp.exp(ce - mx))
            m_run[d] = fe + mx
        for d in dirs:
            m_scr[H_B * d:H_B * (d + 1), 0:1] = m_run[d]
        yield
        c_run = {d: [[c_scr[d, p, j * DK:(j + 1) * DK, :] for j in range(2)] for p in range(n_pairs)] for d in dirs}
        n_run = {d: [n_scr[2 * d + p:2 * d + p + 1, :] for p in range(n_pairs)] for d in dirs}
        for (d, n), a4, b4, kv_u, ks_u in zip(units, a_all, b_all, kv_all, ksum_all):
            for p in range(n_pairs):
                nall_scr[d, n, p:p + 1, :] = n_run[d][p]
                a_s = [a4[2 * p + j:2 * p + j + 1, :] for j in range(2)]
                b_s = [b4[2 * p + j:2 * p + j + 1, :] for j in range(2)]
                for j in range(2):
                    cj = c_run[d][p][j]
                    call_scr[d, n, p, j * DK:(j + 1) * DK, :] = cj.astype(BF16)
                    c_run[d][p][j] = a_s[j] * cj + b_s[j] * kv_u[2 * p + j]
                n_run[d][p] = (jnp.where(head_mask[0], a_s[0], a_s[1]) * n_run[d][p]
                               + jnp.where(head_mask[0], b_s[0] * ks_u[2 * p], b_s[1] * ks_u[2 * p + 1]))
        for d in dirs:
            for p in range(n_pairs):
                n_scr[2 * d + p:2 * d + p + 1, :] = n_run[d][p]
                for j in range(2):
                    c_scr[d, p, j * DK:(j + 1) * DK, :] = c_run[d][p][j]

    eye = lower & upper
    ones8 = jnp.ones((SUBLANES, L), BF16)
    sub8 = lax.broadcasted_iota(jnp.int32, (SUBLANES, LANES), 0)
    sub_h = lax.broadcasted_iota(jnp.int32, (H_B, L), 0)
    n_rows = [((sub8 == 2 * p) & head_mask[0]) | ((sub8 == 2 * p + 1) & head_mask[1]) for p in range(n_pairs)]

    def head_rows(vals):
        out = vals[0][0:H_B, :]
        for h in range(1, H_B):
            out = jnp.where(sub_h == h, vals[h][0:H_B, :], out)
        return out

    def out_group(ns):
        chunks = [(d, n) for n in ns for d in range(2)]
        pairs = [(d, n, p) for d, n in chunks for p in range(n_pairs)]
        units = [(d, n, p, j) for d, n, p in pairs for j in range(2)]
        cms = [lane_cummax(g_scr[d, n], d)[0:H_B, 0:L] for d, n in chunks]
        qk2s, qc2s, qn2s = [], [], []
        for d, n, p in pairs:
            r = _chunk_rows(n)
            qp = qk_scr[r, p * LANES:(p + 1) * LANES]
            q2 = jnp.concatenate([jnp.where(head_mask[j], qp, 0.0) for j in range(2)], axis=0).astype(BF16)
            qk2s.append(_dot_nt(q2, qk_scr[r, HK + p * LANES:HK + (p + 1) * LANES].astype(BF16)))
            qc2s.append(_dot(q2, call_scr[d, n, p]))
            nsel = jnp.where(n_rows[p], nall_scr[d, n, p:p + 1, :], 0.0).astype(BF16)
            qn2s.append(_dot_nt(nsel, qp.astype(BF16)))
        yield
        s_all = []
        for ui, (d, n, p, j) in enumerate(units):
            grow = g_scr[d, n, 2 * p + j:2 * p + j + 1, 0:L]
            e = jnp.where(tmask[d], grow, -jnp.inf)
            cmax = jnp.max(e, axis=-1, keepdims=True)
            s_all.append((qk2s[ui // 2][j * L:(j + 1) * L, :] * jnp.exp(e - cmax)).astype(BF16))
        yield
        nums = [_dot(s, v_ref[_chunk_rows(n), (2 * p + j) * DV:(2 * p + j + 1) * DV])
                for (d, n, p, j), s in zip(units, s_all)]
        dens = [_dot_nt(ones8, s) for s in s_all]
        yield
        scales = []
        for ci, (d, n) in enumerate(chunks):
            den_loc = head_rows(dens[ci * H_B:(ci + 1) * H_B])
            qn = qn2s[ci * n_pairs][0:H_B, :]
            for p in range(1, n_pairs):
                qn = qn + qn2s[ci * n_pairs + p][0:H_B, :]
            cm = cms[ci]
            m_prev = mall_scr[d, n, 0:H_B, 0:1]
            delta = cm - m_prev
            t = jnp.exp(-jnp.abs(delta))
            w_loc = jnp.where(delta <= 0.0, t, 1.0)
            w_inter = jnp.where(delta <= 0.0, 1.0, t)
            mt = f_scr[d, n, 0:H_B, 0:L] + jnp.maximum(m_prev, cm)
            den = w_loc * den_loc + w_inter * qn
            rinv = 1.0 / jnp.maximum(jnp.abs(den), jnp.exp(-mt))
            scales.append((w_loc * rinv, w_inter * rinv))
        yield
        hs = []
        for ui, (d, n, p, j) in enumerate(units):
            h = 2 * p + j
            sc_loc, sc_inter = scales[ui // H_B]
            d_loc = jnp.where(eye, sc_loc[h:h + 1, :], 0.0).astype(BF16)
            d_inter = jnp.where(eye, sc_inter[h:h + 1, :], 0.0).astype(BF16)
            hs.append(_dot(d_loc, nums[ui].astype(BF16))
                      + _dot(d_inter, qc2s[ui // 2][j * L:(j + 1) * L, :].astype(BF16)))
        yield
        for ni, n in enumerate(ns):
            r = _chunk_rows(n)
            for h in range(H_B):
                vs = slice(h * DV, (h + 1) * DV)
                o = hs[(2 * ni) * H_B + h] + hs[(2 * ni + 1) * H_B + h]
                out_ref[r, vs] = (_rms(o, gw_ref[:, vs]) * _sigmoid(og_ref[r, vs].astype(F32))).astype(out_ref.dtype)

    def finish():
        if write_state:
            for d in range(2):
                for p in range(n_pairs):
                    cnew_ref[d, p] = c_scr[d, p]
                    for j in range(2):
                        nnew_ref[d, 2 * p + j:2 * p + j + 1, :] = n_scr[2 * d + p:2 * d + p + 1, j * DK:(j + 1) * DK]
                mnew_ref[d:d + 1, :] = to_row(m_scr[H_B * d:H_B * (d + 1), 0:1])

    return state_group, out_group, finish


def _mlstm_scratch(T, C2, grid_w):
    n_pairs = C2 // 2 // LANES
    n_chunks = T // CHUNK
    pad_rows = 2 * (grid_w + SUBLANES) if T // grid_w > 1 else 2 * SUBLANES
    return [
        pltpu.VMEM((T + pad_rows, C2), F32),
        pltpu.VMEM((T, C2), F32),
        pltpu.VMEM((T, SMALL_W), F32),
        pltpu.VMEM((2, n_pairs, LANES, LANES), F32),
        pltpu.VMEM((SUBLANES, LANES), F32),
        pltpu.VMEM((SUBLANES, LANES), F32),
        pltpu.VMEM((2, n_chunks, n_pairs, LANES, LANES), BF16),
        pltpu.VMEM((2, n_chunks, SUBLANES, LANES), F32),
        pltpu.VMEM((2, n_chunks, SUBLANES, LANES), F32),
        pltpu.VMEM((2, n_chunks, SUBLANES, LANES), F32),
        pltpu.VMEM((2, n_chunks, SUBLANES, LANES), F32),
    ]


N_GLA_SCRATCH = 5
N_MLSTM_SCRATCH = 11


def _scan_kernel(*refs, cols, layer, has_state, write_state, n_cast, ride_ada, grid_w, unroll):
    refs = list(refs)
    z_refs = refs[:2]
    del refs[:2]
    s0_ref = c0_ref = n0_ref = m0_ref = None
    if has_state:
        s0_ref, c0_ref, n0_ref, m0_ref = refs[:4]
        del refs[:4]
    wa_ref, bal_ref, gwa_ref, cw_ref, bmg_ref, gwb_ref = refs[:6]
    del refs[:6]
    cast_in = refs[:n_cast]
    del refs[:n_cast]
    if ride_ada:
        ada_in = refs[:4]
        del refs[:4]
    outa_ref, outb_ref = refs[:2]
    del refs[:2]
    snew_ref = cnew_ref = nnew_ref = mnew_ref = None
    if write_state:
        snew_ref, cnew_ref, nnew_ref, mnew_ref = refs[:4]
        del refs[:4]
    cast_out = refs[:n_cast]
    del refs[:n_cast]
    if ride_ada:
        ada_out = refs.pop(0)
    wal_scr, bm_scr = refs[:2]
    del refs[:2]
    gla_scr = refs[:N_GLA_SCRATCH]
    mlstm_scr = refs[N_GLA_SCRATCH:]

    for src, dst in zip(cast_in, cast_out):
        dst[...] = src[...].astype(BF16)
    if ride_ada:
        _ada_tile(*ada_in, ada_out)

    R, HK = wa_ref.shape[1], wa_ref.shape[2]
    wal_scr[...] = jnp.zeros(wal_scr.shape, BF16)
    for d in range(2):
        wal_scr[d * R:(d + 1) * R, d * HK:(d + 1) * HK] = wa_ref[d].astype(BF16)
    lane = lax.broadcasted_iota(jnp.int32, (1, LANES), 1)
    bm = jnp.zeros((1, LANES), F32)
    for g in range(bmg_ref.shape[1]):
        for h in range(H_B):
            bm = jnp.where(lane == GATE_LANE0 + H_B * g + h, bmg_ref[layer, g, h], bm)
    bm_scr[0:1, :] = bm

    def view(name):
        a, c0, w = cols[name]
        return z_refs[a].at[:, pl.ds(c0, w)]

    sm_ref = view("small")
    n_chunks = z_refs[0].shape[0] // CHUNK
    gla = _gla_body(view("qa"), view("ka"), view("va"), view("ga"), sm_ref, s0_ref, wal_scr, bal_ref, gwa_ref,
                    outa_ref, snew_ref, *gla_scr)
    mlstm = _mlstm_body(view("qkb"), view("vb"), view("ob"), sm_ref, c0_ref, n0_ref, m0_ref, cw_ref,
                        bm_scr.at[0:1, :], gwb_ref, outb_ref, cnew_ref, nnew_ref, mnew_ref, *mlstm_scr,
                        grid_w=grid_w)
    gla_state, gla_out, gla_finish, decay_span = gla
    mlstm_state, mlstm_out, mlstm_finish = mlstm

    def passes(gla_out_fn):
        _chunk_loop(n_chunks, unroll, lambda ns: [fn([n], (d,)) for n in ns for d in range(2)
                                                  for fn in (mlstm_state, gla_state)])
        _chunk_loop(n_chunks, unroll, lambda ns: [fn([n]) for n in ns for fn in (mlstm_out, gla_out_fn)])

    wide_decay = decay_span > GLA_FACTORED_DECAY_MAX

    @pl.when(jnp.logical_not(wide_decay))
    def _():
        passes(gla_out)

    @pl.when(wide_decay)
    def _():
        passes(functools.partial(gla_out, exact_decay=True))

    gla_finish()
    mlstm_finish()


def _scan_call(z2d, row0, B, T, states, lw, layer, *, grid_w, write_state, casts=(), ada=None):
    assert row0 % T == 0 and all(z.shape[0] % T == 0 for z in z2d)
    z3 = [z.reshape(z.shape[0] // T, T, z.shape[1]) for z in z2d]
    blk0 = row0 // T
    HK = lw["w_alpha2"].shape[-1]
    DA = lw["gnorm_a_w"].shape[0]
    C2 = lw["conv_w"].shape[-1]
    DB = lw["gnorm_b_w"].shape[0]
    DK_A, DK_B = HK // H_A, C2 // 2 // H_B
    pa, pb = HK // LANES, C2 // 2 // LANES
    n_chunks = T // CHUNK
    has_state = states is not None
    widths = ((("qa", HK), ("ka", HK), ("qkb", C2), ("small", SMALL_W)),
              (("va", DA), ("ga", DA), ("vb", DB), ("ob", DB)))
    cols = {}
    for a, groups in enumerate(widths):
        c0 = 0
        for name, w in groups:
            cols[name] = (a, c0, w)
            c0 += w
        assert c0 == z3[a].shape[2]
    cast_in_specs, cast_out_specs, cast_out_shape, cast_args = _cast_specs(casts, B)
    kern = functools.partial(_scan_kernel, cols=cols, layer=layer, has_state=has_state, write_state=write_state,
                             n_cast=len(casts), ride_ada=ada is not None, grid_w=grid_w,
                             unroll=min(n_chunks, SCAN_UNROLL))

    def per_batch(shape):
        nd = len(shape)
        return pl.BlockSpec((None,) + tuple(shape), lambda b: (b,) + (0,) * nd)

    def per_batch_layer(shape):
        nd = len(shape)
        return pl.BlockSpec((None, None) + tuple(shape), lambda b: (b, layer) + (0,) * nd)

    def of_layer(a):
        return pl.BlockSpec((None,) + a.shape[1:], lambda b: (layer,) + (0,) * (a.ndim - 1))

    def whole(a):
        return pl.BlockSpec(a.shape, lambda b: (0,) * a.ndim)

    state_shapes = ((2, pa, LANES, LANES), (2, pb, LANES, LANES), (2, H_B, DK_B), (2, H_B))
    in_specs = [pl.BlockSpec((None, T, z.shape[2]), lambda b: (b + blk0, 0, 0)) for z in z3]
    args = list(z3)
    if has_state:
        s_gla, s_c, s_n, s_m = states
        depth = s_gla.shape[1]
        args += [s_gla.reshape((B, depth) + state_shapes[0]), s_c.reshape((B, depth) + state_shapes[1]), s_n, s_m]
        in_specs += [per_batch_layer(s) for s in state_shapes]
    args += [lw["w_alpha2"], lw["b_alpha"], lw["gnorm_a_w"].reshape(1, DA), lw["conv_w"], lw["b_mgate"],
             lw["gnorm_b_w"].reshape(1, DB)]
    in_specs += [of_layer(lw["w_alpha2"]), of_layer(lw["b_alpha"]), pl.BlockSpec((1, DA), lambda b: (0, 0)),
                 whole(lw["conv_w"]), pl.BlockSpec(memory_space=pltpu.SMEM), pl.BlockSpec((1, DB), lambda b: (0, 0))]
    args += cast_args
    in_specs += cast_in_specs
    out_specs = [per_batch((T, DA)), per_batch((T, DB))]
    out_shape = [jax.ShapeDtypeStruct((B, T, DA), BF16), jax.ShapeDtypeStruct((B, T, DB), BF16)]
    if write_state:
        out_specs += [per_batch(s) for s in state_shapes]
        out_shape += [jax.ShapeDtypeStruct((B,) + s, F32) for s in state_shapes]
    out_specs += cast_out_specs
    out_shape += cast_out_shape
    if ada is not None:
        cc, c, w_ada, b_ada, col0 = ada
        n_rest = w_ada.shape[1] - col0
        wcol = n_rest // B
        assert n_rest % B == 0 and wcol % LANES == 0 and col0 % wcol == 0
        args += [cc, c, w_ada, b_ada]
        in_specs += [whole(cc), whole(c),
                     pl.BlockSpec((w_ada.shape[0], wcol), lambda b: (0, col0 // wcol + b)),
                     pl.BlockSpec((1, wcol), lambda b: (0, col0 // wcol + b))]
        out_specs.append(pl.BlockSpec((COND_ROWS, wcol), lambda b: (0, b)))
        out_shape.append(jax.ShapeDtypeStruct((COND_ROWS, n_rest), F32))
    scratch = ([pltpu.VMEM((SMALL_W, 2 * HK), BF16), pltpu.VMEM((SUBLANES, LANES), F32)]
               + _gla_scratch(T, HK) + _mlstm_scratch(T, C2, grid_w))
    assert len(scratch) == 2 + N_GLA_SCRATCH + N_MLSTM_SCRATCH
    return pl.pallas_call(
        kern,
        grid=(B,),
        in_specs=in_specs,
        out_specs=out_specs,
        out_shape=out_shape,
        scratch_shapes=scratch,
        compiler_params=pltpu.CompilerParams(dimension_semantics=("arbitrary",),
                                             vmem_limit_bytes=VMEM_LIMIT),
        name="mixer_scans",
    )(*args)


def _outff_kernel(xc_ref, xl_ref, ac_ref, al_ref, bc_ref, bl_ref, mod_ref, n2_ref, fn_ref, wo_ref, w1_ref, w2_ref,
                  yc_ref, yl_ref, *, n_ctx, tiles_per_req, ff_chunk, final_norm):
    D = xc_ref.shape[1]
    DA = ac_ref.shape[1]
    is_ctx, row = _tile_group(n_ctx, tiles_per_req)

    def mod(k):
        return mod_ref[pl.ds(row, 1), (k - MOD_SPLIT) * D:(k - MOD_SPLIT + 1) * D]

    def tile(x_ref, a_ref, b_ref, y_ref):
        y = _dot(a_ref[...], wo_ref[0:DA, :]) + _dot(b_ref[...], wo_ref[DA:, :])
        x1 = x_ref[...] + mod(2) * y
        h2 = (_rms(x1, n2_ref[...]) * (1.0 + mod(4)) + mod(3)).astype(BF16)
        acc = jnp.zeros(x1.shape, F32)
        for c0 in range(0, w1_ref.shape[1], ff_chunk):
            u = jnp.maximum(_dot(h2, w1_ref[:, c0:c0 + ff_chunk]), 0.0)
            acc = acc + _dot((u * u).astype(BF16), w2_ref[c0:c0 + ff_chunk, :])
        x2 = x1 + mod(5) * acc
        y_ref[...] = _rms(x2, fn_ref[...]) if final_norm else x2

    @pl.when(is_ctx)
    def _():
        tile(xc_ref, ac_ref, bc_ref, yc_ref)

    @pl.when(jnp.logical_not(is_ctx))
    def _():
        tile(xl_ref, al_ref, bl_ref, yl_ref)


def _outff_call(xc2d, xl2d, ac, al, bc, bl, mod, norm2_w, final_w, wo, w1, w2, *, tm, tiles_per_req, final_norm):
    (Mc, D), Ml = xc2d.shape, xl2d.shape[0]
    n_ctx = Mc // tm
    DA = ac.shape[1]
    DFF = w1.shape[1]
    kern = functools.partial(_outff_kernel, n_ctx=n_ctx, tiles_per_req=tiles_per_req, ff_chunk=FF_CHUNK,
                             final_norm=final_norm)
    once = pl.Buffered(1)
    ctx, lat = _ctx_tile(n_ctx), _lat_tile(n_ctx)
    return pl.pallas_call(
        kern,
        grid=((Mc + Ml) // tm,),
        in_specs=[
            pl.BlockSpec((tm, D), ctx), pl.BlockSpec((tm, D), lat),
            pl.BlockSpec((tm, DA), ctx), pl.BlockSpec((tm, DA), lat),
            pl.BlockSpec((tm, D - DA), ctx), pl.BlockSpec((tm, D - DA), lat),
            pl.BlockSpec(mod.shape, lambda i: (0, 0)),
            pl.BlockSpec((1, D), lambda i: (0, 0)),
            pl.BlockSpec((1, D), lambda i: (0, 0)),
            pl.BlockSpec((D, D), lambda i: (0, 0), pipeline_mode=once),
            pl.BlockSpec((D, DFF), lambda i: (0, 0), pipeline_mode=once),
            pl.BlockSpec((DFF, D), lambda i: (0, 0), pipeline_mode=once),
        ],
        out_specs=[pl.BlockSpec((tm, D), ctx), pl.BlockSpec((tm, D), lat)],
        out_shape=[jax.ShapeDtypeStruct((Mc, D), F32), jax.ShapeDtypeStruct((Ml, D), F32)],
        compiler_params=pltpu.CompilerParams(dimension_semantics=("arbitrary",),
                                             vmem_limit_bytes=VMEM_LIMIT),
        name="outproj_mlp",
    )(xc2d, xl2d, ac, al, bc, bl, mod, norm2_w.reshape(1, D), final_w.reshape(1, D), wo, w1, w2)


def _layer(xc, xl, cond, ada_w, cached, lw, layer, ffw, final_w, final_norm):
    (Bc, Tc, D), (Bl, Tl, _) = xc.shape, xl.shape
    tm = TOKEN_TILE
    assert (Bc * Tc) % tm == 0 and Tl % tm == 0 and (Bc * Tc) % Tl == 0
    xc2d, xl2d = xc.reshape(Bc * Tc, D), xl.reshape(Bl * Tl, D)
    z = _inproj_call(xc2d, xl2d, *cond, *ada_w, lw["norm1_w"], lw["w_in_t"], tm=tm, tiles_per_req=Tl // tm,
                     f32_rows=lw["f32_rows"], small_rows=lw["small_rows"], bf16_rows=lw["bf16_rows"])
    res_c = _scan_call(z, 0, Bc, Tc, None, lw, layer, grid_w=Tc, write_state=True,
                       casts=((ffw[0], 0), (ffw[1], 0), (ffw[2], 0)), ada=(*cond, *ada_w, MOD_SPLIT * D))
    res_l = _scan_call(z, Bc * Tc, Bl, Tl, cached, lw, layer, grid_w=GRID_W, write_state=False)
    wo_b, w1_b, w2_b, mod_out = res_c[-4:]
    yc, yl = _outff_call(xc2d, xl2d, res_c[0].reshape(Bc * Tc, -1), res_l[0].reshape(Bl * Tl, -1),
                         res_c[1].reshape(Bc * Tc, -1), res_l[1].reshape(Bl * Tl, -1), mod_out, lw["norm2_w"],
                         final_w, wo_b, w1_b, w2_b, tm=tm, tiles_per_req=Tl // tm, final_norm=final_norm)
    return yc.reshape(Bc, Tc, D), yl.reshape(Bl, Tl, D), tuple(res_c[2:6])


def _layer_weights(l, norm1_w, norm2_w, w_in, w_alpha2, b_alpha, b_mgate, conv_w, gnorm_a_w, gnorm_b_w):
    hk_a = w_alpha2.shape[-1]
    d_a = gnorm_a_w.shape[-1]
    d_b = gnorm_b_w.shape[-1]
    hk_b = conv_w.shape[-1] // 2
    sizes = (hk_a, hk_a, d_a, d_a, 2 * R_ALPHA, hk_b, hk_b, d_b, d_b, 4 * H_B)
    assert w_alpha2.shape[2] == R_ALPHA and b_mgate.shape[1] * b_mgate.shape[2] == 4 * H_B
    offs = [0]
    for s in sizes:
        offs.append(offs[-1] + s)
    f32_rows = ((offs[0], offs[2] - offs[0]), (offs[5], offs[7] - offs[5]))
    small_rows = ((offs[4], offs[5] - offs[4]), (offs[9], offs[10] - offs[9]))
    bf16_rows = ((offs[2], offs[4] - offs[2]), (offs[7], offs[9] - offs[7]))
    assert all(n % LANES == 0 and r % BF16_ROWS == 0 for r, n in f32_rows + bf16_rows)
    return dict(
        norm1_w=norm1_w[l], norm2_w=norm2_w[l], w_in_t=jnp.swapaxes(w_in[l], 0, 1),
        f32_rows=f32_rows, small_rows=small_rows, bf16_rows=bf16_rows,
        w_alpha2=w_alpha2, b_alpha=b_alpha, b_mgate=b_mgate, conv_w=conv_w[l],
        gnorm_a_w=gnorm_a_w[l], gnorm_b_w=gnorm_b_w[l],
    )


def kernel(x_prompt, x_sample, c, state_gla, state_mlstm_C, state_mlstm_n, state_mlstm_m, c_ctx, w_ada, b_ada, norm1_w, norm2_w, w_in, w_alpha2, b_alpha, b_mgate, conv_w, gnorm_a_w, gnorm_b_w, w_out, w_ff1, w_ff2, final_norm_w):
    depth = w_in.shape[0]
    D = x_prompt.shape[-1]
    Bp, Tp, _ = x_prompt.shape
    Bs = x_sample.shape[0]
    assert 1 + Bs <= COND_ROWS
    cond = (c_ctx.reshape(1, D), c)
    cached = (state_gla, state_mlstm_C, state_mlstm_n, state_mlstm_m)
    xp, xs = x_prompt, x_sample
    s_gla, s_c, s_n, s_m = [], [], [], []
    for l in range(depth):
        lw = _layer_weights(l, norm1_w, norm2_w, w_in, w_alpha2, b_alpha, b_mgate, conv_w,
                            gnorm_a_w, gnorm_b_w)
        xp, xs, ctx = _layer(xp, xs, cond, (w_ada[l], b_ada[l].reshape(1, -1)), cached, lw, l,
                             (w_out[l], w_ff1[l], w_ff2[l]), final_norm_w, l == depth - 1)
        s_gla.append(ctx[0].reshape(Bp, 2, H_A, -1, ctx[0].shape[-1]))
        s_c.append(ctx[1].reshape(Bp, 2, H_B, -1, ctx[1].shape[-1]))
        s_n.append(ctx[2])
        s_m.append(ctx[3])
    dt = x_prompt.dtype
    return (xp, xs, jnp.stack(s_gla, axis=1).astype(dt), jnp.stack(s_c, axis=1).astype(dt),
            jnp.stack(s_n, axis=1).astype(dt), jnp.stack(s_m, axis=1).astype(dt))
```

```python
import functools
import math

import jax
import jax.numpy as jnp
from jax import lax
from jax.experimental import pallas as pl
from jax.experimental.pallas import tpu as pltpu

F32 = jnp.float32
BF16 = jnp.bfloat16

GRID_W = 64
H_A = 4
H_B = 4
R_ALPHA = 16
TAU_GLA = 16.0
CHUNK = 64
EPS = 1e-6
LANES = 128
SUBLANES = 8
BF16_ROWS = 16
COND_ROWS = SUBLANES
SMALL_W = LANES
GATE_LANE0 = 2 * R_ALPHA
VMEM_LIMIT = 56 * 1024 * 1024
SCAN_UNROLL = 4
PIPELINE_STARTS = 8
TOKEN_TILE = 512
FF_CHUNK = 2048
GLA_FACTORED_DECAY_MAX = 60.0
MOD_SPLIT = 2


def _sigmoid(x):
    return 1.0 / (1.0 + jnp.exp(-x))


def _silu(x):
    return x * _sigmoid(x)


def _log_sigmoid(x):
    return jnp.minimum(x, 0.0) - jnp.log(1.0 + jnp.exp(-jnp.abs(x)))


def _dot(a, b):
    return jnp.dot(a, b, preferred_element_type=F32)


def _dot_nt(a, b):
    return lax.dot_general(a, b, (((1,), (1,)), ((), ())), preferred_element_type=F32)


def _rms(x, w):
    return x * lax.rsqrt(jnp.mean(x * x, axis=-1, keepdims=True) + EPS) * w


def _tri_sum(tri, x, terms=3):
    acc, rest = None, x
    for t in range(terms):
        part = rest.astype(BF16)
        prod = _dot(tri, part)
        acc = prod if acc is None else acc + prod
        if t + 1 < terms:
            rest = rest - part.astype(F32)
    return acc


def _chunk_masks(L):
    row = lax.broadcasted_iota(jnp.int32, (L, L), 0)
    col = lax.broadcasted_iota(jnp.int32, (L, L), 1)
    lower = row >= col
    upper = row <= col
    return lower, upper


def _ada_tile(cc_ref, c_ref, w_ref, b_ref, o_ref):
    D = cc_ref.shape[1]
    sub = lax.broadcasted_iota(jnp.int32, (COND_ROWS, D), 0)
    cond = jnp.where(sub == 0, cc_ref[...], 0.0)
    for r in range(c_ref.shape[0]):
        cond = jnp.where(sub == 1 + r, c_ref[r:r + 1, :], cond)
    o_ref[...] = _dot(_silu(cond).astype(BF16), w_ref[...].astype(BF16)) + b_ref[...]


def _tile_group(n_ctx, tiles_per_req):
    i = pl.program_id(0)
    is_ctx = i < n_ctx
    row = jnp.where(is_ctx, 0, 1 + jnp.maximum(i - n_ctx, 0) // tiles_per_req)
    return is_ctx, row


def _ctx_tile(n_ctx):
    return lambda i: (jnp.minimum(i, n_ctx - 1), 0)


def _lat_tile(n_ctx):
    return lambda i: (jnp.maximum(i - n_ctx, 0), 0)


def _inproj_kernel(xc_ref, xl_ref, cc_ref, c_ref, wa_ref, ba_ref, nw_ref, wt_ref, zf_ref, zh_ref, wb_scr, mod_ref,
                   *, n_ctx, tiles_per_req, f32_rows, small_rows, bf16_rows):
    D = xc_ref.shape[1]
    n_f32 = zf_ref.shape[1]

    @pl.when(pl.program_id(0) == 0)
    def _():
        _ada_tile(cc_ref, c_ref, wa_ref, ba_ref, mod_ref)

        def wide(rows, col):
            for r0, n in rows:
                wb_scr[col:col + n, :] = wt_ref[r0:r0 + n, :].astype(BF16)
                col += n
            return col

        col = wide(f32_rows, 0)
        parts = [wt_ref[r0:r0 + n, :] for r0, n in small_rows]
        n_small = sum(n for _, n in small_rows)
        parts.append(jnp.zeros((SMALL_W - n_small, D), F32))
        wb_scr[col:col + SMALL_W, :] = jnp.concatenate(parts, axis=0).astype(BF16)
        wide(bf16_rows, col + SMALL_W)

    is_ctx, row = _tile_group(n_ctx, tiles_per_req)

    def tile(x_ref):
        sh1 = mod_ref[pl.ds(row, 1), 0:D]
        sc1 = mod_ref[pl.ds(row, 1), D:2 * D]
        h = (_rms(x_ref[...], nw_ref[...]) * (1.0 + sc1) + sh1).astype(BF16)
        zf_ref[...] = _dot_nt(h, wb_scr[0:n_f32, :])
        zh_ref[...] = _dot_nt(h, wb_scr[n_f32:, :]).astype(BF16)

    @pl.when(is_ctx)
    def _():
        tile(xc_ref)

    @pl.when(jnp.logical_not(is_ctx))
    def _():
        tile(xl_ref)


def _inproj_call(xc2d, xl2d, cc, c, w_ada, b_ada, norm_w, w_in_t, *, tm, tiles_per_req, f32_rows, small_rows,
                 bf16_rows):
    (Mc, D), Ml = xc2d.shape, xl2d.shape[0]
    n_ctx = Mc // tm
    n_f32 = sum(n for _, n in f32_rows) + SMALL_W
    n_bf16 = sum(n for _, n in bf16_rows)
    n_out = n_f32 + n_bf16
    n_mod = MOD_SPLIT * D
    kern = functools.partial(_inproj_kernel, n_ctx=n_ctx, tiles_per_req=tiles_per_req,
                             f32_rows=f32_rows, small_rows=small_rows, bf16_rows=bf16_rows)
    once = pl.Buffered(1)
    return pl.pallas_call(
        kern,
        grid=((Mc + Ml) // tm,),
        in_specs=[
            pl.BlockSpec((tm, D), _ctx_tile(n_ctx)),
            pl.BlockSpec((tm, D), _lat_tile(n_ctx)),
            pl.BlockSpec(cc.shape, lambda i: (0, 0)),
            pl.BlockSpec(c.shape, lambda i: (0, 0)),
            pl.BlockSpec((D, n_mod), lambda i: (0, 0), pipeline_mode=once),
            pl.BlockSpec((1, n_mod), lambda i: (0, 0)),
            pl.BlockSpec((1, D), lambda i: (0, 0)),
            pl.BlockSpec(w_in_t.shape, lambda i: (0, 0), pipeline_mode=once),
        ],
        out_specs=[pl.BlockSpec((tm, n_f32), lambda i: (i, 0)), pl.BlockSpec((tm, n_bf16), lambda i: (i, 0))],
        out_shape=[jax.ShapeDtypeStruct((Mc + Ml, n_f32), F32), jax.ShapeDtypeStruct((Mc + Ml, n_bf16), BF16)],
        scratch_shapes=[pltpu.VMEM((n_out, D), BF16), pltpu.VMEM((COND_ROWS, n_mod), F32)],
        compiler_params=pltpu.CompilerParams(dimension_semantics=("arbitrary",),
                                             vmem_limit_bytes=VMEM_LIMIT),
        name="norm_inproj",
    )(xc2d, xl2d, cc, c, w_ada, b_ada, norm_w.reshape(1, D), w_in_t)


def _chunk_loop(n_chunks, unroll, make_units):
    def step(ns):
        pending = list(make_units(ns))
        active = []
        while pending or active:
            for _ in range(min(PIPELINE_STARTS, len(pending))):
                active.append(pending.pop(0))
            alive = []
            for g in active:
                try:
                    next(g)
                    alive.append(g)
                except StopIteration:
                    pass
            active = alive

    if unroll >= n_chunks:
        step(list(range(n_chunks)))
        return

    def body(i, carry):
        step([i * unroll + u for u in range(unroll)])
        return carry

    lax.fori_loop(0, n_chunks // unroll, body, 0)


def _chunk_rows(n):
    if isinstance(n, int):
        return pl.ds(n * CHUNK, CHUNK)
    return pl.ds(pl.multiple_of(n * CHUNK, CHUNK), CHUNK)


def _cast_specs(casts, n_steps):
    in_specs, out_specs, out_shape, args = [], [], [], []
    for w, axis in casts:
        blk = list(w.shape)
        assert blk[axis] % n_steps == 0
        blk[axis] //= n_steps
        assert blk[0] % BF16_ROWS == 0 and blk[1] % LANES == 0
        idx = (lambda b: (b, 0)) if axis == 0 else (lambda b: (0, b))
        in_specs.append(pl.BlockSpec(tuple(blk), idx))
        out_specs.append(pl.BlockSpec(tuple(blk), idx))
        out_shape.append(jax.ShapeDtypeStruct(w.shape, BF16))
        args.append(w)
    return in_specs, out_specs, out_shape, args


def _gla_body(q_ref, k_ref, v_ref, g_ref, sm_ref, s0_ref, wal_ref, bal_ref, gw_ref, out_ref, snew_ref,
              st_scr, sall_scr, qh_scr, qs_scr, kh_scr):
    has_state = s0_ref is not None
    write_state = snew_ref is not None
    T = q_ref.shape[0]
    L = CHUNK
    N = T // L
    HK = q_ref.shape[1]
    DK = HK // H_A
    DV = v_ref.shape[1] // H_A
    scale = DK ** -0.5
    n_pairs = HK // LANES

    lower, upper = _chunk_masks(L)
    tri = (lower.astype(BF16), upper.astype(BF16))
    tmask = (lower, upper)
    lane = lax.broadcasted_iota(jnp.int32, (1, LANES), 1)
    head_mask = (lane < DK, lane >= DK)

    for d in range(2):
        for p in range(n_pairs):
            if has_state:
                st_scr[d, p] = s0_ref[d, p].T
            else:
                st_scr[d, p] = jnp.zeros((LANES, LANES), F32)

    def decay_pre(d, r):
        return _dot(sm_ref[r, :].astype(BF16), wal_ref[:, d * HK:(d + 1) * HK]) + bal_ref[d:d + 1, :]

    neg_pre = jnp.maximum(-(_dot(sm_ref[...].astype(BF16), wal_ref[...])
                            + jnp.concatenate([bal_ref[0:1, :], bal_ref[1:2, :]], axis=1)), 0.0)
    chunk_sums = jnp.sum(neg_pre.reshape(N, L, 2 * HK), axis=1)
    decay_span = (jnp.max(chunk_sums) + L * math.log(2.0)) * (1.0 / TAU_GLA)

    def state_group(ns, dirs=(0, 1)):
        units = [(d, n if d == 0 else N - 1 - n) for n in ns for d in dirs]
        rows = [_chunk_rows(n) for _, n in units]
        vt_all = [[jnp.concatenate([v_ref[r, (2 * p + j) * DV:(2 * p + j + 1) * DV] for j in range(2)],
                                   axis=0).astype(F32).T.astype(BF16) for p in range(n_pairs)] for r in rows]
        yield
        pre = [decay_pre(d, r) for (d, _), r in zip(units, rows)]
        yield
        g = [_log_sigmoid(x) * (1.0 / TAU_GLA) for x in pre]
        yield
        b = [_tri_sum(tri[d], gi, terms=2) for (d, _), gi in zip(units, g)]
        yield
        ks_all, dec_all = [], []
        for (d, _), r, bi in zip(units, rows, b):
            bend = bi[L - 1:L, :] if d == 0 else bi[0:1, :]
            q = q_ref[r, :] * scale
            ks = (k_ref[r, :] * jnp.exp(bend - bi)).astype(BF16)
            qs = q * jnp.exp(bi)
            qh_scr[d, r, :] = (qs * jnp.exp(-bend)).astype(BF16)
            qs_scr[d, r, :] = qs.astype(BF16)
            kh_scr[d, r, :] = ks
            ks_all.append(ks)
            dec_all.append(jnp.exp(bend))
        yield
        upd_all = []
        for vt_u, ks in zip(vt_all, ks_all):
            upd_u = []
            for p in range(n_pairs):
                kp = ks[:, p * LANES:(p + 1) * LANES]
                kk = jnp.concatenate([jnp.where(head_mask[j], kp, jnp.zeros_like(kp)) for j in range(2)], axis=0)
                upd_u.append(_dot(vt_u[p], kk))
            upd_all.append(upd_u)
        yield
        st = {d: [st_scr[d, p] for p in range(n_pairs)] for d in dirs}
        for (d, n), dec, upd in zip(units, dec_all, upd_all):
            for p in range(n_pairs):
                sall_scr[d, n, p] = st[d][p].astype(BF16)
                st[d][p] = st[d][p] * dec[:, p * LANES:(p + 1) * LANES] + upd[p]
        for d in dirs:
            for p in range(n_pairs):
                st_scr[d, p] = st[d][p]

    def stack_heads(x):
        return jnp.concatenate([jnp.where(head_mask[j], x, jnp.zeros_like(x)) for j in range(2)], axis=0)

    tok = lax.broadcasted_iota(jnp.int32, (L, 1), 0)
    row_t = lax.broadcasted_iota(jnp.int32, (2 * L, L), 0) & (L - 1)
    col_s = lax.broadcasted_iota(jnp.int32, (2 * L, L), 1)

    def exact_scores(d, r, p):
        ls = slice(p * LANES, (p + 1) * LANES)
        b = _tri_sum(tri[d], _log_sigmoid(decay_pre(d, r)[:, ls]) * (1.0 / TAU_GLA))
        q = q_ref[r, ls] * scale
        k = k_ref[r, ls]
        acc = jnp.where(row_t == col_s, _dot_nt(stack_heads(q).astype(BF16), k.astype(BF16)), 0.0)
        src = lax.broadcasted_iota(jnp.int32, (L, L), 1)
        h = L // 2
        while h >= 1:
            first = tok & ~(2 * h - 1)
            edge = first + (h - 1 if d == 0 else h)
            b_edge = _tri_sum((src == edge).astype(BF16), b)
            upper = (tok & (2 * h - 1)) >= h
            later, earlier = (upper, ~upper) if d == 0 else (~upper, upper)
            qt = jnp.where(later, q * jnp.exp(b - b_edge), 0.0)
            kt = jnp.where(earlier, k * jnp.exp(b_edge - b), 0.0)
            sc = _dot_nt(stack_heads(qt).astype(BF16), kt.astype(BF16))
            acc = acc + jnp.where((row_t & ~(2 * h - 1)) == (col_s & ~(2 * h - 1)), sc, 0.0)
            h //= 2
        return acc

    def out_group(ns, exact_decay=False):
        pairs = [(d, ni, p) for ni in range(len(ns)) for d in range(2) for p in range(n_pairs)]
        scores, inter = [], []
        for d, ni, p in pairs:
            r = _chunk_rows(ns[ni])
            ls = slice(p * LANES, (p + 1) * LANES)
            if exact_decay:
                scores.append(exact_scores(d, r, p))
            else:
                scores.append(_dot_nt(stack_heads(qh_scr[d, r, ls]), kh_scr[d, r, ls]))
            inter.append(_dot_nt(stack_heads(qs_scr[d, r, ls]), sall_scr[d, ns[ni], p]))
        yield
        probs = [[jnp.where(tmask[d], sc[j * L:(j + 1) * L, :], 0.0).astype(BF16) for j in range(2)]
                 for (d, _, _), sc in zip(pairs, scores)]
        yield
        outs = {}
        for (d, ni, p), pr, it in zip(pairs, probs, inter):
            r = _chunk_rows(ns[ni])
            for j in range(2):
                vs = slice((2 * p + j) * DV, (2 * p + j + 1) * DV)
                outs[(d, ni, 2 * p + j)] = _dot(pr[j], v_ref[r, vs]) + it[j * L:(j + 1) * L, :]
        yield
        for ni, n in enumerate(ns):
            r = _chunk_rows(n)
            for h in range(H_A):
                vs = slice(h * DV, (h + 1) * DV)
                o = outs[(0, ni, h)] + outs[(1, ni, h)]
                out_ref[r, vs] = (_rms(o, gw_ref[:, vs]) * _silu(g_ref[r, vs].astype(F32))).astype(out_ref.dtype)

    def finish():
        if write_state:
            for d in range(2):
                for p in range(n_pairs):
                    snew_ref[d, p] = st_scr[d, p].T

    return state_group, out_group, finish, decay_span


def _gla_scratch(T, HK):
    n_pairs = HK // LANES
    n_chunks = T // CHUNK
    return [
        pltpu.VMEM((2, n_pairs, LANES, LANES), F32),
        pltpu.VMEM((2, n_chunks, n_pairs, LANES, LANES), BF16),
        pltpu.VMEM((2, T, HK), BF16),
        pltpu.VMEM((2, T, HK), BF16),
        pltpu.VMEM((2, T, HK), BF16),
    ]


def _mlstm_body(qk_ref, v_ref, og_ref, sm_ref, c0_ref, n0_ref, m0_ref, cw_ref, bm_ref, gw_ref,
                out_ref, cnew_ref, nnew_ref, mnew_ref,
                pad_scr, qk_scr, y_scr, c_scr, n_scr, m_scr, call_scr, nall_scr, mall_scr, g_scr, f_scr,
                *, grid_w):
    has_state = c0_ref is not None
    write_state = cnew_ref is not None
    T = qk_ref.shape[0]
    L = CHUNK
    N = T // L
    C2 = qk_ref.shape[1]
    HK = C2 // 2
    DK = HK // H_B
    DV = v_ref.shape[1] // H_B
    scale = DK ** -0.5
    n_pairs = HK // LANES
    P = pad_scr.shape[0] - T
    P0 = P // 2
    rows_img = T // grid_w

    lower, upper = _chunk_masks(L)
    tri = (lower.astype(BF16), upper.astype(BF16))
    tmask = (lower, upper)
    lane = lax.broadcasted_iota(jnp.int32, (1, LANES), 1)
    head_mask = (lane < DK, lane >= DK)
    lane_in = lane & (L - 1)

    def lane_cummax(x, d):
        k = 1
        while k < L:
            if d == 0:
                x = jnp.maximum(x, jnp.where(lane_in >= k, pltpu.roll(x, k, axis=1), -jnp.inf))
            else:
                x = jnp.maximum(x, jnp.where(lane_in < L - k, pltpu.roll(x, LANES - k, axis=1), -jnp.inf))
            k *= 2
        return x

    for d in range(2):
        for p in range(n_pairs):
            if has_state:
                c_scr[d, p] = c0_ref[d, p]
                n_scr[2 * d + p:2 * d + p + 1, :] = jnp.concatenate(
                    [n0_ref[d, 2 * p + j:2 * p + j + 1, :] for j in range(2)], axis=1)
            else:
                c_scr[d, p] = jnp.zeros((LANES, LANES), F32)
                n_scr[2 * d + p:2 * d + p + 1, :] = jnp.zeros((1, LANES), F32)
    eye_h = (lax.broadcasted_iota(jnp.int32, (H_B, H_B), 0) == lax.broadcasted_iota(jnp.int32, (H_B, H_B), 1))

    def to_col(row):
        return jnp.sum(jnp.where(eye_h, row, 0.0), axis=1, keepdims=True)

    def to_row(col):
        return jnp.sum(jnp.where(eye_h, col, 0.0), axis=0, keepdims=True)

    for d in range(2):
        if has_state:
            m_scr[H_B * d:H_B * (d + 1), 0:1] = to_col(m0_ref[d:d + 1, :])
        else:
            m_scr[H_B * d:H_B * (d + 1), 0:1] = jnp.zeros((H_B, 1), F32)

    pad_scr[0:P0, :] = jnp.zeros((P0, C2), F32)
    pad_scr[P0 + T:P + T, :] = jnp.zeros((P - P0, C2), F32)

    def copy_in(i, carry):
        r0 = pl.multiple_of(i * L, L)
        pad_scr[pl.ds(P0 + r0, L), :] = qk_ref[pl.ds(r0, L), :]
        return carry

    lax.fori_loop(0, N, copy_in, 0)

    lane_c = lax.broadcasted_iota(jnp.int32, (1, C2), 1)
    qscale = jnp.where(lane_c < HK, scale, 1.0).astype(F32)
    sub = lax.broadcasted_iota(jnp.int32, (L, 1), 0)
    img_rows = (0,) if rows_img == 1 else (-1, 0, 1)

    def conv_tile(i, carry):
        r0 = pl.multiple_of(i * L, L)
        col = lax.rem(r0, grid_w) + sub
        ok_left = col >= 1
        ok_right = col <= grid_w - 2
        sums = [None, None, None]
        for di in img_rows:
            blk = pad_scr[pl.ds(P0 + r0 + di * grid_w - SUBLANES, L + 2 * SUBLANES), :]
            for k in range(3):
                term = blk * cw_ref[di + 1, k:k + 1, :]
                sums[k] = term if sums[k] is None else sums[k] + term
        S = SUBLANES
        acc = (sums[1][S:S + L, :] + jnp.where(ok_left, sums[0][S - 1:S - 1 + L, :], 0.0)
               + jnp.where(ok_right, sums[2][S + 1:S + 1 + L, :], 0.0))
        qk_scr[pl.ds(r0, L), :] = _silu(acc) * qscale
        return carry

    lax.fori_loop(0, N, conv_tile, 0)

    gl = lane - GATE_LANE0
    is_f = ((gl >= H_B) & (gl < 2 * H_B)) | ((gl >= 3 * H_B) & (gl < 4 * H_B))

    def gate_tile(i, carry):
        rows = pl.ds(pl.multiple_of(i * L, L), L)
        x = sm_ref[rows, :] + bm_ref[...]
        y_scr[rows, :] = jnp.where(is_f, _log_sigmoid(x), x)
        return carry

    lax.fori_loop(0, N, gate_tile, 0)


    def state_group(ns, dirs=(0, 1)):
        units = [(d, n if d == 0 else N - 1 - n) for n in ns for d in dirs]
        rows = [_chunk_rows(n) for _, n in units]
        kt_all = [[qk_scr[r, HK + p * LANES:HK + (p + 1) * LANES].T for p in range(n_pairs)] for r in rows]
        yield
        xs = [y_scr[r, :] for r in rows]
        fsum = [_tri_sum(tri[d], x) for (d, _), x in zip(units, xs)]
        yield
        wk_all, f_end, c_end = [], [], []
        for (d, n), r, x, fs in zip(units, rows, xs, fsum):
            y = jnp.where(is_f, fs, x)
            li0 = GATE_LANE0 + 2 * H_B * d
            blk = jnp.concatenate([y, y], axis=0).T[li0:li0 + 2 * H_B, :]
            frow = pltpu.roll(blk, H_B, axis=0)
            grow = blk - frow
            g_scr[d, n] = grow
            f_scr[d, n] = frow
            e_col = L - 1 if d == 0 else 0
            f_end.append(frow[0:H_B, e_col:e_col + 1])
            ce8 = jnp.max(grow, axis=1, keepdims=True)
            c_end.append(ce8[0:H_B, :])
            wk_all.append(jnp.exp(grow[:, 0:L] - ce8))
        yield
        kv_all, ksum_all = [], []
        for r, wk8, kt_u in zip(rows, wk_all, kt_all):
            kv_u, ks_u = [], []
            wk8b = wk8.astype(BF16)
            for p in range(n_pairs):
                kpb = qk_scr[r, HK + p * LANES:HK + (p + 1) * LANES].astype(BF16)
                ks8 = _dot(wk8b, kpb)
                for j in range(2):
                    h = 2 * p + j
                    kwt = (kt_u[p][j * DK:(j + 1) * DK, :] * wk8[h:h + 1, :]).astype(BF16)
                    kv_u.append(_dot(kwt, v_ref[r, h * DV:(h + 1) * DV]))
                    ks_u.append(ks8[h:h + 1, :])
            kv_all.append(kv_u)
            ksum_all.append(ks_u)
        yield
        m_run = {d: m_scr[H_B * d:H_B * (d + 1), 0:1] for d in dirs}
        a_all, b_all = [], []
        for (d, n), fe, ce in zip(units, f_end, c_end):
            mall_scr[d, n, 0:H_B, 0:1] = m_run[d]
            mx = jnp.maximum(m_run[d], ce)
            a_all.append(jnp.exp(m_run[d] - mx))
            b_all.append(jnp.exp(ce - mx))
            m_run[d] = fe + mx
        for d in dirs:
            m_scr[H_B * d:H_B * (d + 1), 0:1] = m_run[d]
        yield
        c_run = {d: [[c_scr[d, p, j * DK:(j + 1) * DK, :] for j in range(2)] for p in range(n_pairs)] for d in dirs}
        n_run = {d: [n_scr[2 * d + p:2 * d + p + 1, :] for p in range(n_pairs)] for d in dirs}
        for (d, n), a4, b4, kv_u, ks_u in zip(units, a_all, b_all, kv_all, ksum_all):
            for p in range(n_pairs):
                nall_scr[d, n, p:p + 1, :] = n_run[d][p]
                a_s = [a4[2 * p + j:2 * p + j + 1, :] for j in range(2)]
                b_s = [b4[2 * p + j:2 * p + j + 1, :] for j in range(2)]
                for j in range(2):
                    cj = c_run[d][p][j]
                    call_scr[d, n, p, j * DK:(j + 1) * DK, :] = cj.astype(BF16)
                    c_run[d][p][j] = a_s[j] * cj + b_s[j] * kv_u[2 * p + j]
                n_run[d][p] = (jnp.where(head_mask[0], a_s[0], a_s[1]) * n_run[d][p]
                               + jnp.where(head_mask[0], b_s[0] * ks_u[2 * p], b_s[1] * ks_u[2 * p + 1]))
        for d in dirs:
            for p in range(n_pairs):
                n_scr[2 * d + p:2 * d + p + 1, :] = n_run[d][p]
                for j in range(2):
                    c_scr[d, p, j * DK:(j + 1) * DK, :] = c_run[d][p][j]

    eye = lower & upper
    ones8 = jnp.ones((SUBLANES, L), BF16)
    sub8 = lax.broadcasted_iota(jnp.int32, (SUBLANES, LANES), 0)
    sub_h = lax.broadcasted_iota(jnp.int32, (H_B, L), 0)
    n_rows = [((sub8 == 2 * p) & head_mask[0]) | ((sub8 == 2 * p + 1) & head_mask[1]) for p in range(n_pairs)]

    def head_rows(vals):
        out = vals[0][0:H_B, :]
        for h in range(1, H_B):
            out = jnp.where(sub_h == h, vals[h][0:H_B, :], out)
        return out

    def out_group(ns):
        chunks = [(d, n) for n in ns for d in range(2)]
        pairs = [(d, n, p) for d, n in chunks for p in range(n_pairs)]
        units = [(d, n, p, j) for d, n, p in pairs for j in range(2)]
        cms = [lane_cummax(g_scr[d, n], d)[0:H_B, 0:L] for d, n in chunks]
        qk2s, qc2s, qn2s = [], [], []
        for d, n, p in pairs:
            r = _chunk_rows(n)
            qp = qk_scr[r, p * LANES:(p + 1) * LANES]
            q2 = jnp.concatenate([jnp.where(head_mask[j], qp, 0.0) for j in range(2)], axis=0).astype(BF16)
            qk2s.append(_dot_nt(q2, qk_scr[r, HK + p * LANES:HK + (p + 1) * LANES].astype(BF16)))
            qc2s.append(_dot(q2, call_scr[d, n, p]))
            nsel = jnp.where(n_rows[p], nall_scr[d, n, p:p + 1, :], 0.0).astype(BF16)
            qn2s.append(_dot_nt(nsel, qp.astype(BF16)))
        yield
        s_all = []
        for ui, (d, n, p, j) in enumerate(units):
            grow = g_scr[d, n, 2 * p + j:2 * p + j + 1, 0:L]
            e = jnp.where(tmask[d], grow, -jnp.inf)
            cmax = jnp.max(e, axis=-1, keepdims=True)
            s_all.append((qk2s[ui // 2][j * L:(j + 1) * L, :] * jnp.exp(e - cmax)).astype(BF16))
        yield
        nums = [_dot(s, v_ref[_chunk_rows(n), (2 * p + j) * DV:(2 * p + j + 1) * DV])
                for (d, n, p, j), s in zip(units, s_all)]
        dens = [_dot_nt(ones8, s) for s in s_all]
        yield
        scales = []
        for ci, (d, n) in enumerate(chunks):
            den_loc = head_rows(dens[ci * H_B:(ci + 1) * H_B])
            qn = qn2s[ci * n_pairs][0:H_B, :]
            for p in range(1, n_pairs):
                qn = qn + qn2s[ci * n_pairs + p][0:H_B, :]
            cm = cms[ci]
            m_prev = mall_scr[d, n, 0:H_B, 0:1]
            delta = cm - m_prev
            t = jnp.exp(-jnp.abs(delta))
            w_loc = jnp.where(delta <= 0.0, t, 1.0)
            w_inter = jnp.where(delta <= 0.0, 1.0, t)
            mt = f_scr[d, n, 0:H_B, 0:L] + jnp.maximum(m_prev, cm)
            den = w_loc * den_loc + w_inter * qn
            rinv = 1.0 / jnp.maximum(jnp.abs(den), jnp.exp(-mt))
            scales.append((w_loc * rinv, w_inter * rinv))
        yield
        hs = []
        for ui, (d, n, p, j) in enumerate(units):
            h = 2 * p + j
            sc_loc, sc_inter = scales[ui // H_B]
            d_loc = jnp.where(eye, sc_loc[h:h + 1, :], 0.0).astype(BF16)
            d_inter = jnp.where(eye, sc_inter[h:h + 1, :], 0.0).astype(BF16)
            hs.append(_dot(d_loc, nums[ui].astype(BF16))
                      + _dot(d_inter, qc2s[ui // 2][j * L:(j + 1) * L, :].astype(BF16)))
        yield
        for ni, n in enumerate(ns):
            r = _chunk_rows(n)
            for h in range(H_B):
                vs = slice(h * DV, (h + 1) * DV)
                o = hs[(2 * ni) * H_B + h] + hs[(2 * ni + 1) * H_B + h]
                out_ref[r, vs] = (_rms(o, gw_ref[:, vs]) * _sigmoid(og_ref[r, vs].astype(F32))).astype(out_ref.dtype)

    def finish():
        if write_state:
            for d in range(2):
                for p in range(n_pairs):
                    cnew_ref[d, p] = c_scr[d, p]
                    for j in range(2):
                        nnew_ref[d, 2 * p + j:2 * p + j + 1, :] = n_scr[2 * d + p:2 * d + p + 1, j * DK:(j + 1) * DK]
                mnew_ref[d:d + 1, :] = to_row(m_scr[H_B * d:H_B * (d + 1), 0:1])

    return state_group, out_group, finish


def _mlstm_scratch(T, C2, grid_w):
    n_pairs = C2 // 2 // LANES
    n_chunks = T // CHUNK
    pad_rows = 2 * (grid_w + SUBLANES) if T // grid_w > 1 else 2 * SUBLANES
    return [
        pltpu.VMEM((T + pad_rows, C2), F32),
        pltpu.VMEM((T, C2), F32),
        pltpu.VMEM((T, SMALL_W), F32),
        pltpu.VMEM((2, n_pairs, LANES, LANES), F32),
        pltpu.VMEM((SUBLANES, LANES), F32),
        pltpu.VMEM((SUBLANES, LANES), F32),
        pltpu.VMEM((2, n_chunks, n_pairs, LANES, LANES), BF16),
        pltpu.VMEM((2, n_chunks, SUBLANES, LANES), F32),
        pltpu.VMEM((2, n_chunks, SUBLANES, LANES), F32),
        pltpu.VMEM((2, n_chunks, SUBLANES, LANES), F32),
        pltpu.VMEM((2, n_chunks, SUBLANES, LANES), F32),
    ]


N_GLA_SCRATCH = 5
N_MLSTM_SCRATCH = 11


def _scan_kernel(*refs, cols, layer, has_state, write_state, n_cast, ride_ada, grid_w, unroll):
    refs = list(refs)
    z_refs = refs[:2]
    del refs[:2]
    s0_ref = c0_ref = n0_ref = m0_ref = None
    if has_state:
        s0_ref, c0_ref, n0_ref, m0_ref = refs[:4]
        del refs[:4]
    wa_ref, bal_ref, gwa_ref, cw_ref, bmg_ref, gwb_ref = refs[:6]
    del refs[:6]
    cast_in = refs[:n_cast]
    del refs[:n_cast]
    if ride_ada:
        ada_in = refs[:4]
        del refs[:4]
    outa_ref, outb_ref = refs[:2]
    del refs[:2]
    snew_ref = cnew_ref = nnew_ref = mnew_ref = None
    if write_state:
        snew_ref, cnew_ref, nnew_ref, mnew_ref = refs[:4]
        del refs[:4]
    cast_out = refs[:n_cast]
    del refs[:n_cast]
    if ride_ada:
        ada_out = refs.pop(0)
    wal_scr, bm_scr = refs[:2]
    del refs[:2]
    gla_scr = refs[:N_GLA_SCRATCH]
    mlstm_scr = refs[N_GLA_SCRATCH:]

    for src, dst in zip(cast_in, cast_out):
        dst[...] = src[...].astype(BF16)
    if ride_ada:
        _ada_tile(*ada_in, ada_out)

    R, HK = wa_ref.shape[1], wa_ref.shape[2]
    wal_scr[...] = jnp.zeros(wal_scr.shape, BF16)
    for d in range(2):
        wal_scr[d * R:(d + 1) * R, d * HK:(d + 1) * HK] = wa_ref[d].astype(BF16)
    lane = lax.broadcasted_iota(jnp.int32, (1, LANES), 1)
    bm = jnp.zeros((1, LANES), F32)
    for g in range(bmg_ref.shape[1]):
        for h in range(H_B):
            bm = jnp.where(lane == GATE_LANE0 + H_B * g + h, bmg_ref[layer, g, h], bm)
    bm_scr[0:1, :] = bm

    def view(name):
        a, c0, w = cols[name]
        return z_refs[a].at[:, pl.ds(c0, w)]

    sm_ref = view("small")
    n_chunks = z_refs[0].shape[0] // CHUNK
    gla = _gla_body(view("qa"), view("ka"), view("va"), view("ga"), sm_ref, s0_ref, wal_scr, bal_ref, gwa_ref,
                    outa_ref, snew_ref, *gla_scr)
    mlstm = _mlstm_body(view("qkb"), view("vb"), view("ob"), sm_ref, c0_ref, n0_ref, m0_ref, cw_ref,
                        bm_scr.at[0:1, :], gwb_ref, outb_ref, cnew_ref, nnew_ref, mnew_ref, *mlstm_scr,
                        grid_w=grid_w)
    gla_state, gla_out, gla_finish, decay_span = gla
    mlstm_state, mlstm_out, mlstm_finish = mlstm

    def passes(gla_out_fn):
        _chunk_loop(n_chunks, unroll, lambda ns: [fn([n], (d,)) for n in ns for d in range(2)
                                                  for fn in (mlstm_state, gla_state)])
        _chunk_loop(n_chunks, unroll, lambda ns: [fn([n]) for n in ns for fn in (mlstm_out, gla_out_fn)])

    wide_decay = decay_span > GLA_FACTORED_DECAY_MAX

    @pl.when(jnp.logical_not(wide_decay))
    def _():
        passes(gla_out)

    @pl.when(wide_decay)
    def _():
        passes(functools.partial(gla_out, exact_decay=True))

    gla_finish()
    mlstm_finish()


def _scan_call(z2d, row0, B, T, states, lw, layer, *, grid_w, write_state, casts=(), ada=None):
    assert row0 % T == 0 and all(z.shape[0] % T == 0 for z in z2d)
    z3 = [z.reshape(z.shape[0] // T, T, z.shape[1]) for z in z2d]
    blk0 = row0 // T
    HK = lw["w_alpha2"].shape[-1]
    DA = lw["gnorm_a_w"].shape[0]
    C2 = lw["conv_w"].shape[-1]
    DB = lw["gnorm_b_w"].shape[0]
    DK_A, DK_B = HK // H_A, C2 // 2 // H_B
    pa, pb = HK // LANES, C2 // 2 // LANES
    n_chunks = T // CHUNK
    has_state = states is not None
    widths = ((("qa", HK), ("ka", HK), ("qkb", C2), ("small", SMALL_W)),
              (("va", DA), ("ga", DA), ("vb", DB), ("ob", DB)))
    cols = {}
    for a, groups in enumerate(widths):
        c0 = 0
        for name, w in groups:
            cols[name] = (a, c0, w)
            c0 += w
        assert c0 == z3[a].shape[2]
    cast_in_specs, cast_out_specs, cast_out_shape, cast_args = _cast_specs(casts, B)
    kern = functools.partial(_scan_kernel, cols=cols, layer=layer, has_state=has_state, write_state=write_state,
                             n_cast=len(casts), ride_ada=ada is not None, grid_w=grid_w,
                             unroll=min(n_chunks, SCAN_UNROLL))

    def per_batch(shape):
        nd = len(shape)
        return pl.BlockSpec((None,) + tuple(shape), lambda b: (b,) + (0,) * nd)

    def per_batch_layer(shape):
        nd = len(shape)
        return pl.BlockSpec((None, None) + tuple(shape), lambda b: (b, layer) + (0,) * nd)

    def of_layer(a):
        return pl.BlockSpec((None,) + a.shape[1:], lambda b: (layer,) + (0,) * (a.ndim - 1))

    def whole(a):
        return pl.BlockSpec(a.shape, lambda b: (0,) * a.ndim)

    state_shapes = ((2, pa, LANES, LANES), (2, pb, LANES, LANES), (2, H_B, DK_B), (2, H_B))
    in_specs = [pl.BlockSpec((None, T, z.shape[2]), lambda b: (b + blk0, 0, 0)) for z in z3]
    args = list(z3)
    if has_state:
        s_gla, s_c, s_n, s_m = states
        depth = s_gla.shape[1]
        args += [s_gla.reshape((B, depth) + state_shapes[0]), s_c.reshape((B, depth) + state_shapes[1]), s_n, s_m]
        in_specs += [per_batch_layer(s) for s in state_shapes]
    args += [lw["w_alpha2"], lw["b_alpha"], lw["gnorm_a_w"].reshape(1, DA), lw["conv_w"], lw["b_mgate"],
             lw["gnorm_b_w"].reshape(1, DB)]
    in_specs += [of_layer(lw["w_alpha2"]), of_layer(lw["b_alpha"]), pl.BlockSpec((1, DA), lambda b: (0, 0)),
                 whole(lw["conv_w"]), pl.BlockSpec(memory_space=pltpu.SMEM), pl.BlockSpec((1, DB), lambda b: (0, 0))]
    args += cast_args
    in_specs += cast_in_specs
    out_specs = [per_batch((T, DA)), per_batch((T, DB))]
    out_shape = [jax.ShapeDtypeStruct((B, T, DA), BF16), jax.ShapeDtypeStruct((B, T, DB), BF16)]
    if write_state:
        out_specs += [per_batch(s) for s in state_shapes]
        out_shape += [jax.ShapeDtypeStruct((B,) + s, F32) for s in state_shapes]
    out_specs += cast_out_specs
    out_shape += cast_out_shape
    if ada is not None:
        cc, c, w_ada, b_ada, col0 = ada
        n_rest = w_ada.shape[1] - col0
        wcol = n_rest // B
        assert n_rest % B == 0 and wcol % LANES == 0 and col0 % wcol == 0
        args += [cc, c, w_ada, b_ada]
        in_specs += [whole(cc), whole(c),
                     pl.BlockSpec((w_ada.shape[0], wcol), lambda b: (0, col0 // wcol + b)),
                     pl.BlockSpec((1, wcol), lambda b: (0, col0 // wcol + b))]
        out_specs.append(pl.BlockSpec((COND_ROWS, wcol), lambda b: (0, b)))
        out_shape.append(jax.ShapeDtypeStruct((COND_ROWS, n_rest), F32))
    scratch = ([pltpu.VMEM((SMALL_W, 2 * HK), BF16), pltpu.VMEM((SUBLANES, LANES), F32)]
               + _gla_scratch(T, HK) + _mlstm_scratch(T, C2, grid_w))
    assert len(scratch) == 2 + N_GLA_SCRATCH + N_MLSTM_SCRATCH
    return pl.pallas_call(
        kern,
        grid=(B,),
        in_specs=in_specs,
        out_specs=out_specs,
        out_shape=out_shape,
        scratch_shapes=scratch,
        compiler_params=pltpu.CompilerParams(dimension_semantics=("arbitrary",),
                                             vmem_limit_bytes=VMEM_LIMIT),
        name="mixer_scans",
    )(*args)


def _outff_kernel(xc_ref, xl_ref, ac_ref, al_ref, bc_ref, bl_ref, mod_ref, n2_ref, fn_ref, wo_ref, w1_ref, w2_ref,
                  yc_ref, yl_ref, *, n_ctx, tiles_per_req, ff_chunk, final_norm):
    D = xc_ref.shape[1]
    DA = ac_ref.shape[1]
    is_ctx, row = _tile_group(n_ctx, tiles_per_req)

    def mod(k):
        return mod_ref[pl.ds(row, 1), (k - MOD_SPLIT) * D:(k - MOD_SPLIT + 1) * D]

    def tile(x_ref, a_ref, b_ref, y_ref):
        y = _dot(a_ref[...], wo_ref[0:DA, :]) + _dot(b_ref[...], wo_ref[DA:, :])
        x1 = x_ref[...] + mod(2) * y
        h2 = (_rms(x1, n2_ref[...]) * (1.0 + mod(4)) + mod(3)).astype(BF16)
        acc = jnp.zeros(x1.shape, F32)
        for c0 in range(0, w1_ref.shape[1], ff_chunk):
            u = jnp.maximum(_dot(h2, w1_ref[:, c0:c0 + ff_chunk]), 0.0)
            acc = acc + _dot((u * u).astype(BF16), w2_ref[c0:c0 + ff_chunk, :])
        x2 = x1 + mod(5) * acc
        y_ref[...] = _rms(x2, fn_ref[...]) if final_norm else x2

    @pl.when(is_ctx)
    def _():
        tile(xc_ref, ac_ref, bc_ref, yc_ref)

    @pl.when(jnp.logical_not(is_ctx))
    def _():
        tile(xl_ref, al_ref, bl_ref, yl_ref)


def _outff_call(xc2d, xl2d, ac, al, bc, bl, mod, norm2_w, final_w, wo, w1, w2, *, tm, tiles_per_req, final_norm):
    (Mc, D), Ml = xc2d.shape, xl2d.shape[0]
    n_ctx = Mc // tm
    DA = ac.shape[1]
    DFF = w1.shape[1]
    kern = functools.partial(_outff_kernel, n_ctx=n_ctx, tiles_per_req=tiles_per_req, ff_chunk=FF_CHUNK,
                             final_norm=final_norm)
    once = pl.Buffered(1)
    ctx, lat = _ctx_tile(n_ctx), _lat_tile(n_ctx)
    return pl.pallas_call(
        kern,
        grid=((Mc + Ml) // tm,),
        in_specs=[
            pl.BlockSpec((tm, D), ctx), pl.BlockSpec((tm, D), lat),
            pl.BlockSpec((tm, DA), ctx), pl.BlockSpec((tm, DA), lat),
            pl.BlockSpec((tm, D - DA), ctx), pl.BlockSpec((tm, D - DA), lat),
            pl.BlockSpec(mod.shape, lambda i: (0, 0)),
            pl.BlockSpec((1, D), lambda i: (0, 0)),
            pl.BlockSpec((1, D), lambda i: (0, 0)),
            pl.BlockSpec((D, D), lambda i: (0, 0), pipeline_mode=once),
            pl.BlockSpec((D, DFF), lambda i: (0, 0), pipeline_mode=once),
            pl.BlockSpec((DFF, D), lambda i: (0, 0), pipeline_mode=once),
        ],
        out_specs=[pl.BlockSpec((tm, D), ctx), pl.BlockSpec((tm, D), lat)],
        out_shape=[jax.ShapeDtypeStruct((Mc, D), F32), jax.ShapeDtypeStruct((Ml, D), F32)],
        compiler_params=pltpu.CompilerParams(dimension_semantics=("arbitrary",),
                                             vmem_limit_bytes=VMEM_LIMIT),
        name="outproj_mlp",
    )(xc2d, xl2d, ac, al, bc, bl, mod, norm2_w.reshape(1, D), final_w.reshape(1, D), wo, w1, w2)


def _layer(xc, xl, cond, ada_w, cached, lw, layer, ffw, final_w, final_norm):
    (Bc, Tc, D), (Bl, Tl, _) = xc.shape, xl.shape
    tm = TOKEN_TILE
    assert (Bc * Tc) % tm == 0 and Tl % tm == 0 and (Bc * Tc) % Tl == 0
    xc2d, xl2d = xc.reshape(Bc * Tc, D), xl.reshape(Bl * Tl, D)
    z = _inproj_call(xc2d, xl2d, *cond, *ada_w, lw["norm1_w"], lw["w_in_t"], tm=tm, tiles_per_req=Tl // tm,
                     f32_rows=lw["f32_rows"], small_rows=lw["small_rows"], bf16_rows=lw["bf16_rows"])
    res_c = _scan_call(z, 0, Bc, Tc, None, lw, layer, grid_w=Tc, write_state=True,
                       casts=((ffw[0], 0), (ffw[1], 0), (ffw[2], 0)), ada=(*cond, *ada_w, MOD_SPLIT * D))
    res_l = _scan_call(z, Bc * Tc, Bl, Tl, cached, lw, layer, grid_w=GRID_W, write_state=False)
    wo_b, w1_b, w2_b, mod_out = res_c[-4:]
    yc, yl = _outff_call(xc2d, xl2d, res_c[0].reshape(Bc * Tc, -1), res_l[0].reshape(Bl * Tl, -1),
                         res_c[1].reshape(Bc * Tc, -1), res_l[1].reshape(Bl * Tl, -1), mod_out, lw["norm2_w"],
                         final_w, wo_b, w1_b, w2_b, tm=tm, tiles_per_req=Tl // tm, final_norm=final_norm)
    return yc.reshape(Bc, Tc, D), yl.reshape(Bl, Tl, D), tuple(res_c[2:6])


def _layer_weights(l, norm1_w, norm2_w, w_in, w_alpha2, b_alpha, b_mgate, conv_w, gnorm_a_w, gnorm_b_w):
    hk_a = w_alpha2.shape[-1]
    d_a = gnorm_a_w.shape[-1]
    d_b = gnorm_b_w.shape[-1]
    hk_b = conv_w.shape[-1] // 2
    sizes = (hk_a, hk_a, d_a, d_a, 2 * R_ALPHA, hk_b, hk_b, d_b, d_b, 4 * H_B)
    assert w_alpha2.shape[2] == R_ALPHA and b_mgate.shape[1] * b_mgate.shape[2] == 4 * H_B
    offs = [0]
    for s in sizes:
        offs.append(offs[-1] + s)
    f32_rows = ((offs[0], offs[2] - offs[0]), (offs[5], offs[7] - offs[5]))
    small_rows = ((offs[4], offs[5] - offs[4]), (offs[9], offs[10] - offs[9]))
    bf16_rows = ((offs[2], offs[4] - offs[2]), (offs[7], offs[9] - offs[7]))
    assert all(n % LANES == 0 and r % BF16_ROWS == 0 for r, n in f32_rows + bf16_rows)
    return dict(
        norm1_w=norm1_w[l], norm2_w=norm2_w[l], w_in_t=jnp.swapaxes(w_in[l], 0, 1),
        f32_rows=f32_rows, small_rows=small_rows, bf16_rows=bf16_rows,
        w_alpha2=w_alpha2, b_alpha=b_alpha, b_mgate=b_mgate, conv_w=conv_w[l],
        gnorm_a_w=gnorm_a_w[l], gnorm_b_w=gnorm_b_w[l],
    )


def kernel(x_prompt, x_sample, c, state_gla, state_mlstm_C, state_mlstm_n, state_mlstm_m, c_ctx, w_ada, b_ada, norm1_w, norm2_w, w_in, w_alpha2, b_alpha, b_mgate, conv_w, gnorm_a_w, gnorm_b_w, w_out, w_ff1, w_ff2, final_norm_w):
    depth = w_in.shape[0]
    D = x_prompt.shape[-1]
    Bp, Tp, _ = x_prompt.shape
    Bs = x_sample.shape[0]
    assert 1 + Bs <= COND_ROWS
    cond = (c_ctx.reshape(1, D), c)
    cached = (state_gla, state_mlstm_C, state_mlstm_n, state_mlstm_m)
    xp, xs = x_prompt, x_sample
    s_gla, s_c, s_n, s_m = [], [], [], []
    for l in range(depth):
        lw = _layer_weights(l, norm1_w, norm2_w, w_in, w_alpha2, b_alpha, b_mgate, conv_w,
                            gnorm_a_w, gnorm_b_w)
        xp, xs, ctx = _layer(xp, xs, cond, (w_ada[l], b_ada[l].reshape(1, -1)), cached, lw, l,
                             (w_out[l], w_ff1[l], w_ff2[l]), final_norm_w, l == depth - 1)
        s_gla.append(ctx[0].reshape(Bp, 2, H_A, -1, ctx[0].shape[-1]))
        s_c.append(ctx[1].reshape(Bp, 2, H_B, -1, ctx[1].shape[-1]))
        s_n.append(ctx[2])
        s_m.append(ctx[3])
    dt = x_prompt.dtype
    return (xp, xs, jnp.stack(s_gla, axis=1).astype(dt), jnp.stack(s_c, axis=1).astype(dt),
            jnp.stack(s_n, axis=1).astype(dt), jnp.stack(s_m, axis=1).astype(dt))
```

```python
import functools
import math

import jax
import jax.numpy as jnp
from jax import lax
from jax.experimental import pallas as pl
from jax.experimental.pallas import tpu as pltpu

F32 = jnp.float32
BF16 = jnp.bfloat16

GRID_W = 64
H_A = 4
H_B = 4
R_ALPHA = 16
TAU_GLA = 16.0
CHUNK = 64
EPS = 1e-6
LANES = 128
SUBLANES = 8
BF16_ROWS = 16
COND_ROWS = SUBLANES
SMALL_W = LANES
GATE_LANE0 = 2 * R_ALPHA
VMEM_LIMIT = 56 * 1024 * 1024
SCAN_UNROLL = 4
PIPELINE_STARTS = 8
TOKEN_TILE = 512
FF_CHUNK = 4096
GLA_FACTORED_DECAY_MAX = 60.0
MOD_SPLIT = 2


def _sigmoid(x):
    return 1.0 / (1.0 + jnp.exp(-x))


def _silu(x):
    return x * _sigmoid(x)


def _log_sigmoid(x):
    return jnp.minimum(x, 0.0) - jnp.log(1.0 + jnp.exp(-jnp.abs(x)))


def _dot(a, b):
    return jnp.dot(a, b, preferred_element_type=F32)


def _dot_nt(a, b):
    return lax.dot_general(a, b, (((1,), (1,)), ((), ())), preferred_element_type=F32)


def _rms(x, w):
    return x * lax.rsqrt(jnp.mean(x * x, axis=-1, keepdims=True) + EPS) * w


def _tri_sum(tri, x, terms=3):
    acc, rest = None, x
    for t in range(terms):
        part = rest.astype(BF16)
        prod = _dot(tri, part)
        acc = prod if acc is None else acc + prod
        if t + 1 < terms:
            rest = rest - part.astype(F32)
    return acc


def _chunk_masks(L):
    row = lax.broadcasted_iota(jnp.int32, (L, L), 0)
    col = lax.broadcasted_iota(jnp.int32, (L, L), 1)
    lower = row >= col
    upper = row <= col
    return lower, upper


def _ada_tile(cc_ref, c_ref, w_ref, b_ref, o_ref):
    D = cc_ref.shape[1]
    sub = lax.broadcasted_iota(jnp.int32, (COND_ROWS, D), 0)
    cond = jnp.where(sub == 0, cc_ref[...], 0.0)
    for r in range(c_ref.shape[0]):
        cond = jnp.where(sub == 1 + r, c_ref[r:r + 1, :], cond)
    o_ref[...] = _dot(_silu(cond).astype(BF16), w_ref[...].astype(BF16)) + b_ref[...]


def _tile_group(n_ctx, tiles_per_req):
    i = pl.program_id(0)
    is_ctx = i < n_ctx
    row = jnp.where(is_ctx, 0, 1 + jnp.maximum(i - n_ctx, 0) // tiles_per_req)
    return is_ctx, row


def _ctx_tile(n_ctx):
    return lambda i: (jnp.minimum(i, n_ctx - 1), 0)


def _lat_tile(n_ctx):
    return lambda i: (jnp.maximum(i - n_ctx, 0), 0)


def _inproj_kernel(xc_ref, xl_ref, cc_ref, c_ref, wa_ref, ba_ref, nw_ref, wt_ref, zf_ref, zh_ref, wb_scr, mod_ref,
                   *, n_ctx, tiles_per_req, f32_rows, small_rows, bf16_rows):
    D = xc_ref.shape[1]
    n_f32 = zf_ref.shape[1]

    @pl.when(pl.program_id(0) == 0)
    def _():
        _ada_tile(cc_ref, c_ref, wa_ref, ba_ref, mod_ref)

        def wide(rows, col):
            for r0, n in rows:
                wb_scr[col:col + n, :] = wt_ref[r0:r0 + n, :].astype(BF16)
                col += n
            return col

        col = wide(f32_rows, 0)
        parts = [wt_ref[r0:r0 + n, :] for r0, n in small_rows]
        n_small = sum(n for _, n in small_rows)
        parts.append(jnp.zeros((SMALL_W - n_small, D), F32))
        wb_scr[col:col + SMALL_W, :] = jnp.concatenate(parts, axis=0).astype(BF16)
        wide(bf16_rows, col + SMALL_W)

    is_ctx, row = _tile_group(n_ctx, tiles_per_req)

    def tile(x_ref):
        sh1 = mod_ref[pl.ds(row, 1), 0:D]
        sc1 = mod_ref[pl.ds(row, 1), D:2 * D]
        h = (_rms(x_ref[...], nw_ref[...]) * (1.0 + sc1) + sh1).astype(BF16)
        zf_ref[...] = _dot_nt(h, wb_scr[0:n_f32, :])
        zh_ref[...] = _dot_nt(h, wb_scr[n_f32:, :]).astype(BF16)

    @pl.when(is_ctx)
    def _():
        tile(xc_ref)

    @pl.when(jnp.logical_not(is_ctx))
    def _():
        tile(xl_ref)


def _inproj_call(xc2d, xl2d, cc, c, w_ada, b_ada, norm_w, w_in_t, *, tm, tiles_per_req, f32_rows, small_rows,
                 bf16_rows):
    (Mc, D), Ml = xc2d.shape, xl2d.shape[0]
    n_ctx = Mc // tm
    n_f32 = sum(n for _, n in f32_rows) + SMALL_W
    n_bf16 = sum(n for _, n in bf16_rows)
    n_out = n_f32 + n_bf16
    n_mod = MOD_SPLIT * D
    kern = functools.partial(_inproj_kernel, n_ctx=n_ctx, tiles_per_req=tiles_per_req,
                             f32_rows=f32_rows, small_rows=small_rows, bf16_rows=bf16_rows)
    once = pl.Buffered(1)
    return pl.pallas_call(
        kern,
        grid=((Mc + Ml) // tm,),
        in_specs=[
            pl.BlockSpec((tm, D), _ctx_tile(n_ctx)),
            pl.BlockSpec((tm, D), _lat_tile(n_ctx)),
            pl.BlockSpec(cc.shape, lambda i: (0, 0)),
            pl.BlockSpec(c.shape, lambda i: (0, 0)),
            pl.BlockSpec((D, n_mod), lambda i: (0, 0), pipeline_mode=once),
            pl.BlockSpec((1, n_mod), lambda i: (0, 0)),
            pl.BlockSpec((1, D), lambda i: (0, 0)),
            pl.BlockSpec(w_in_t.shape, lambda i: (0, 0), pipeline_mode=once),
        ],
        out_specs=[pl.BlockSpec((tm, n_f32), lambda i: (i, 0)), pl.BlockSpec((tm, n_bf16), lambda i: (i, 0))],
        out_shape=[jax.ShapeDtypeStruct((Mc + Ml, n_f32), F32), jax.ShapeDtypeStruct((Mc + Ml, n_bf16), BF16)],
        scratch_shapes=[pltpu.VMEM((n_out, D), BF16), pltpu.VMEM((COND_ROWS, n_mod), F32)],
        compiler_params=pltpu.CompilerParams(dimension_semantics=("arbitrary",),
                                             vmem_limit_bytes=VMEM_LIMIT),
        name="norm_inproj",
    )(xc2d, xl2d, cc, c, w_ada, b_ada, norm_w.reshape(1, D), w_in_t)


def _chunk_loop(n_chunks, unroll, make_units):
    def step(ns):
        pending = list(make_units(ns))
        active = []
        while pending or active:
            for _ in range(min(PIPELINE_STARTS, len(pending))):
                active.append(pending.pop(0))
            alive = []
            for g in active:
                try:
                    next(g)
                    alive.append(g)
                except StopIteration:
                    pass
            active = alive

    if unroll >= n_chunks:
        step(list(range(n_chunks)))
        return

    def body(i, carry):
        step([i * unroll + u for u in range(unroll)])
        return carry

    lax.fori_loop(0, n_chunks // unroll, body, 0)


def _chunk_rows(n):
    if isinstance(n, int):
        return pl.ds(n * CHUNK, CHUNK)
    return pl.ds(pl.multiple_of(n * CHUNK, CHUNK), CHUNK)


def _cast_specs(casts, n_steps):
    in_specs, out_specs, out_shape, args = [], [], [], []
    for w, axis in casts:
        blk = list(w.shape)
        assert blk[axis] % n_steps == 0
        blk[axis] //= n_steps
        assert blk[0] % BF16_ROWS == 0 and blk[1] % LANES == 0
        idx = (lambda b: (b, 0)) if axis == 0 else (lambda b: (0, b))
        in_specs.append(pl.BlockSpec(tuple(blk), idx))
        out_specs.append(pl.BlockSpec(tuple(blk), idx))
        out_shape.append(jax.ShapeDtypeStruct(w.shape, BF16))
        args.append(w)
    return in_specs, out_specs, out_shape, args


def _gla_body(q_ref, k_ref, v_ref, g_ref, sm_ref, s0_ref, wal_ref, bal_ref, gw_ref, out_ref, snew_ref,
              st_scr, sall_scr, qh_scr, qs_scr, kh_scr):
    has_state = s0_ref is not None
    write_state = snew_ref is not None
    T = q_ref.shape[0]
    L = CHUNK
    N = T // L
    HK = q_ref.shape[1]
    DK = HK // H_A
    DV = v_ref.shape[1] // H_A
    scale = DK ** -0.5
    n_pairs = HK // LANES

    lower, upper = _chunk_masks(L)
    tri = (lower.astype(BF16), upper.astype(BF16))
    tmask = (lower, upper)
    lane = lax.broadcasted_iota(jnp.int32, (1, LANES), 1)
    head_mask = (lane < DK, lane >= DK)

    for d in range(2):
        for p in range(n_pairs):
            if has_state:
                st_scr[d, p] = s0_ref[d, p].T
            else:
                st_scr[d, p] = jnp.zeros((LANES, LANES), F32)

    def decay_pre(d, r):
        return _dot(sm_ref[r, :].astype(BF16), wal_ref[:, d * HK:(d + 1) * HK]) + bal_ref[d:d + 1, :]

    neg_pre = jnp.maximum(-(_dot(sm_ref[...].astype(BF16), wal_ref[...])
                            + jnp.concatenate([bal_ref[0:1, :], bal_ref[1:2, :]], axis=1)), 0.0)
    chunk_sums = jnp.sum(neg_pre.reshape(N, L, 2 * HK), axis=1)
    decay_span = (jnp.max(chunk_sums) + L * math.log(2.0)) * (1.0 / TAU_GLA)

    def state_group(ns, dirs=(0, 1)):
        units = [(d, n if d == 0 else N - 1 - n) for n in ns for d in dirs]
        rows = [_chunk_rows(n) for _, n in units]
        vt_all = [[jnp.concatenate([v_ref[r, (2 * p + j) * DV:(2 * p + j + 1) * DV] for j in range(2)],
                                   axis=0).astype(F32).T.astype(BF16) for p in range(n_pairs)] for r in rows]
        yield
        pre = [decay_pre(d, r) for (d, _), r in zip(units, rows)]
        yield
        g = [_log_sigmoid(x) * (1.0 / TAU_GLA) for x in pre]
        yield
        b = [_tri_sum(tri[d], gi, terms=2) for (d, _), gi in zip(units, g)]
        yield
        ks_all, dec_all = [], []
        for (d, _), r, bi in zip(units, rows, b):
            bend = bi[L - 1:L, :] if d == 0 else bi[0:1, :]
            q = q_ref[r, :] * scale
            ks = (k_ref[r, :] * jnp.exp(bend - bi)).astype(BF16)
            qs = q * jnp.exp(bi)
            qh_scr[d, r, :] = (qs * jnp.exp(-bend)).astype(BF16)
            qs_scr[d, r, :] = qs.astype(BF16)
            kh_scr[d, r, :] = ks
            ks_all.append(ks)
            dec_all.append(jnp.exp(bend))
        yield
        upd_all = []
        for vt_u, ks in zip(vt_all, ks_all):
            upd_u = []
            for p in range(n_pairs):
                kp = ks[:, p * LANES:(p + 1) * LANES]
                kk = jnp.concatenate([jnp.where(head_mask[j], kp, jnp.zeros_like(kp)) for j in range(2)], axis=0)
                upd_u.append(_dot(vt_u[p], kk))
            upd_all.append(upd_u)
        yield
        st = {d: [st_scr[d, p] for p in range(n_pairs)] for d in dirs}
        for (d, n), dec, upd in zip(units, dec_all, upd_all):
            for p in range(n_pairs):
                sall_scr[d, n, p] = st[d][p].astype(BF16)
                st[d][p] = st[d][p] * dec[:, p * LANES:(p + 1) * LANES] + upd[p]
        for d in dirs:
            for p in range(n_pairs):
                st_scr[d, p] = st[d][p]

    def stack_heads(x):
        return jnp.concatenate([jnp.where(head_mask[j], x, jnp.zeros_like(x)) for j in range(2)], axis=0)

    tok = lax.broadcasted_iota(jnp.int32, (L, 1), 0)
    row_t = lax.broadcasted_iota(jnp.int32, (2 * L, L), 0) & (L - 1)
    col_s = lax.broadcasted_iota(jnp.int32, (2 * L, L), 1)

    def exact_scores(d, r, p):
        ls = slice(p * LANES, (p + 1) * LANES)
        b = _tri_sum(tri[d], _log_sigmoid(decay_pre(d, r)[:, ls]) * (1.0 / TAU_GLA))
        q = q_ref[r, ls] * scale
        k = k_ref[r, ls]
        acc = jnp.where(row_t == col_s, _dot_nt(stack_heads(q).astype(BF16), k.astype(BF16)), 0.0)
        src = lax.broadcasted_iota(jnp.int32, (L, L), 1)
        h = L // 2
        while h >= 1:
            first = tok & ~(2 * h - 1)
            edge = first + (h - 1 if d == 0 else h)
            b_edge = _tri_sum((src == edge).astype(BF16), b)
            upper = (tok & (2 * h - 1)) >= h
            later, earlier = (upper, ~upper) if d == 0 else (~upper, upper)
            qt = jnp.where(later, q * jnp.exp(b - b_edge), 0.0)
            kt = jnp.where(earlier, k * jnp.exp(b_edge - b), 0.0)
            sc = _dot_nt(stack_heads(qt).astype(BF16), kt.astype(BF16))
            acc = acc + jnp.where((row_t & ~(2 * h - 1)) == (col_s & ~(2 * h - 1)), sc, 0.0)
            h //= 2
        return acc

    def out_group(ns, exact_decay=False):
        pairs = [(d, ni, p) for ni in range(len(ns)) for d in range(2) for p in range(n_pairs)]
        scores, inter = [], []
        for d, ni, p in pairs:
            r = _chunk_rows(ns[ni])
            ls = slice(p * LANES, (p + 1) * LANES)
            if exact_decay:
                scores.append(exact_scores(d, r, p))
            else:
                scores.append(_dot_nt(stack_heads(qh_scr[d, r, ls]), kh_scr[d, r, ls]))
            inter.append(_dot_nt(stack_heads(qs_scr[d, r, ls]), sall_scr[d, ns[ni], p]))
        yield
        probs = [[jnp.where(tmask[d], sc[j * L:(j + 1) * L, :], 0.0).astype(BF16) for j in range(2)]
                 for (d, _, _), sc in zip(pairs, scores)]
        yield
        outs = {}
        for (d, ni, p), pr, it in zip(pairs, probs, inter):
            r = _chunk_rows(ns[ni])
            for j in range(2):
                vs = slice((2 * p + j) * DV, (2 * p + j + 1) * DV)
                outs[(d, ni, 2 * p + j)] = _dot(pr[j], v_ref[r, vs]) + it[j * L:(j + 1) * L, :]
        yield
        for ni, n in enumerate(ns):
            r = _chunk_rows(n)
            for h in range(H_A):
                vs = slice(h * DV, (h + 1) * DV)
                o = outs[(0, ni, h)] + outs[(1, ni, h)]
                out_ref[r, vs] = (_rms(o, gw_ref[:, vs]) * _silu(g_ref[r, vs].astype(F32))).astype(out_ref.dtype)

    def finish():
        if write_state:
            for d in range(2):
                for p in range(n_pairs):
                    snew_ref[d, p] = st_scr[d, p].T

    return state_group, out_group, finish, decay_span


def _gla_scratch(T, HK):
    n_pairs = HK // LANES
    n_chunks = T // CHUNK
    return [
        pltpu.VMEM((2, n_pairs, LANES, LANES), F32),
        pltpu.VMEM((2, n_chunks, n_pairs, LANES, LANES), BF16),
        pltpu.VMEM((2, T, HK), BF16),
        pltpu.VMEM((2, T, HK), BF16),
        pltpu.VMEM((2, T, HK), BF16),
    ]


def _mlstm_body(qk_ref, v_ref, og_ref, sm_ref, c0_ref, n0_ref, m0_ref, cw_ref, bm_ref, gw_ref,
                out_ref, cnew_ref, nnew_ref, mnew_ref,
                pad_scr, qk_scr, y_scr, c_scr, n_scr, m_scr, call_scr, nall_scr, mall_scr, g_scr, f_scr,
                *, grid_w):
    has_state = c0_ref is not None
    write_state = cnew_ref is not None
    T = qk_ref.shape[0]
    L = CHUNK
    N = T // L
    C2 = qk_ref.shape[1]
    HK = C2 // 2
    DK = HK // H_B
    DV = v_ref.shape[1] // H_B
    scale = DK ** -0.5
    n_pairs = HK // LANES
    P = pad_scr.shape[0] - T
    P0 = P // 2
    rows_img = T // grid_w

    lower, upper = _chunk_masks(L)
    tri = (lower.astype(BF16), upper.astype(BF16))
    tmask = (lower, upper)
    lane = lax.broadcasted_iota(jnp.int32, (1, LANES), 1)
    head_mask = (lane < DK, lane >= DK)
    lane_in = lane & (L - 1)

    def lane_cummax(x, d):
        k = 1
        while k < L:
            if d == 0:
                x = jnp.maximum(x, jnp.where(lane_in >= k, pltpu.roll(x, k, axis=1), -jnp.inf))
            else:
                x = jnp.maximum(x, jnp.where(lane_in < L - k, pltpu.roll(x, LANES - k, axis=1), -jnp.inf))
            k *= 2
        return x

    for d in range(2):
        for p in range(n_pairs):
            if has_state:
                c_scr[d, p] = c0_ref[d, p]
                n_scr[2 * d + p:2 * d + p + 1, :] = jnp.concatenate(
                    [n0_ref[d, 2 * p + j:2 * p + j + 1, :] for j in range(2)], axis=1)
            else:
                c_scr[d, p] = jnp.zeros((LANES, LANES), F32)
                n_scr[2 * d + p:2 * d + p + 1, :] = jnp.zeros((1, LANES), F32)
    eye_h = (lax.broadcasted_iota(jnp.int32, (H_B, H_B), 0) == lax.broadcasted_iota(jnp.int32, (H_B, H_B), 1))

    def to_col(row):
        return jnp.sum(jnp.where(eye_h, row, 0.0), axis=1, keepdims=True)

    def to_row(col):
        return jnp.sum(jnp.where(eye_h, col, 0.0), axis=0, keepdims=True)

    for d in range(2):
        if has_state:
            m_scr[H_B * d:H_B * (d + 1), 0:1] = to_col(m0_ref[d:d + 1, :])
        else:
            m_scr[H_B * d:H_B * (d + 1), 0:1] = jnp.zeros((H_B, 1), F32)

    pad_scr[0:P0, :] = jnp.zeros((P0, C2), F32)
    pad_scr[P0 + T:P + T, :] = jnp.zeros((P - P0, C2), F32)

    def copy_in(i, carry):
        r0 = pl.multiple_of(i * L, L)
        pad_scr[pl.ds(P0 + r0, L), :] = qk_ref[pl.ds(r0, L), :]
        return carry

    lax.fori_loop(0, N, copy_in, 0)

    lane_c = lax.broadcasted_iota(jnp.int32, (1, C2), 1)
    qscale = jnp.where(lane_c < HK, scale, 1.0).astype(F32)
    sub = lax.broadcasted_iota(jnp.int32, (L, 1), 0)
    img_rows = (0,) if rows_img == 1 else (-1, 0, 1)

    def conv_tile(i, carry):
        r0 = pl.multiple_of(i * L, L)
        col = lax.rem(r0, grid_w) + sub
        ok_left = col >= 1
        ok_right = col <= grid_w - 2
        sums = [None, None, None]
        for di in img_rows:
            blk = pad_scr[pl.ds(P0 + r0 + di * grid_w - SUBLANES, L + 2 * SUBLANES), :]
            for k in range(3):
                term = blk * cw_ref[di + 1, k:k + 1, :]
                sums[k] = term if sums[k] is None else sums[k] + term
        S = SUBLANES
        acc = (sums[1][S:S + L, :] + jnp.where(ok_left, sums[0][S - 1:S - 1 + L, :], 0.0)
               + jnp.where(ok_right, sums[2][S + 1:S + 1 + L, :], 0.0))
        qk_scr[pl.ds(r0, L), :] = _silu(acc) * qscale
        return carry

    lax.fori_loop(0, N, conv_tile, 0)

    gl = lane - GATE_LANE0
    is_f = ((gl >= H_B) & (gl < 2 * H_B)) | ((gl >= 3 * H_B) & (gl < 4 * H_B))

    def gate_tile(i, carry):
        rows = pl.ds(pl.multiple_of(i * L, L), L)
        x = sm_ref[rows, :] + bm_ref[...]
        y_scr[rows, :] = jnp.where(is_f, _log_sigmoid(x), x)
        return carry

    lax.fori_loop(0, N, gate_tile, 0)


    def state_group(ns, dirs=(0, 1)):
        units = [(d, n if d == 0 else N - 1 - n) for n in ns for d in dirs]
        rows = [_chunk_rows(n) for _, n in units]
        kt_all = [[qk_scr[r, HK + p * LANES:HK + (p + 1) * LANES].T for p in range(n_pairs)] for r in rows]
        yield
        xs = [y_scr[r, :] for r in rows]
        fsum = [_tri_sum(tri[d], x) for (d, _), x in zip(units, xs)]
        yield
        wk_all, f_end, c_end = [], [], []
        for (d, n), r, x, fs in zip(units, rows, xs, fsum):
            y = jnp.where(is_f, fs, x)
            li0 = GATE_LANE0 + 2 * H_B * d
            blk = jnp.concatenate([y, y], axis=0).T[li0:li0 + 2 * H_B, :]
            frow = pltpu.roll(blk, H_B, axis=0)
            grow = blk - frow
            g_scr[d, n] = grow
            f_scr[d, n] = frow
            e_col = L - 1 if d == 0 else 0
            f_end.append(frow[0:H_B, e_col:e_col + 1])
            ce8 = jnp.max(grow, axis=1, keepdims=True)
            c_end.append(ce8[0:H_B, :])
            wk_all.append(jnp.exp(grow[:, 0:L] - ce8))
        yield
        kv_all, ksum_all = [], []
        for r, wk8, kt_u in zip(rows, wk_all, kt_all):
            kv_u, ks_u = [], []
            wk8b = wk8.astype(BF16)
            for p in range(n_pairs):
                kpb = qk_scr[r, HK + p * LANES:HK + (p + 1) * LANES].astype(BF16)
                ks8 = _dot(wk8b, kpb)
                for j in range(2):
                    h = 2 * p + j
                    kwt = (kt_u[p][j * DK:(j + 1) * DK, :] * wk8[h:h + 1, :]).astype(BF16)
                    kv_u.append(_dot(kwt, v_ref[r, h * DV:(h + 1) * DV]))
                    ks_u.append(ks8[h:h + 1, :])
            kv_all.append(kv_u)
            ksum_all.append(ks_u)
        yield
        m_run = {d: m_scr[H_B * d:H_B * (d + 1), 0:1] for d in dirs}
        a_all, b_all = [], []
        for (d, n), fe, ce in zip(units, f_end, c_end):
            mall_scr[d, n, 0:H_B, 0:1] = m_run[d]
            mx = jnp.maximum(m_run[d], ce)
            a_all.append(jnp.exp(m_run[d] - mx))
            b_all.append(jnp.exp(ce - mx))
            m_run[d] = fe + mx
        for d in dirs:
            m_scr[H_B * d:H_B * (d + 1), 0:1] = m_run[d]
        yield
        c_run = {d: [[c_scr[d, p, j * DK:(j + 1) * DK, :] for j in range(2)] for p in range(n_pairs)] for d in dirs}
        n_run = {d: [n_scr[2 * d + p:2 * d + p + 1, :] for p in range(n_pairs)] for d in dirs}
        for (d, n), a4, b4, kv_u, ks_u in zip(units, a_all, b_all, kv_all, ksum_all):
            for p in range(n_pairs):
                nall_scr[d, n, p:p + 1, :] = n_run[d][p]
                a_s = [a4[2 * p + j:2 * p + j + 1, :] for j in range(2)]
                b_s = [b4[2 * p + j:2 * p + j + 1, :] for j in range(2)]
                for j in range(2):
                    cj = c_run[d][p][j]
                    call_scr[d, n, p, j * DK:(j + 1) * DK, :] = cj.astype(BF16)
                    c_run[d][p][j] = a_s[j] * cj + b_s[j] * kv_u[2 * p + j]
                n_run[d][p] = (jnp.where(head_mask[0], a_s[0], a_s[1]) * n_run[d][p]
                               + jnp.where(head_mask[0], b_s[0] * ks_u[2 * p], b_s[1] * ks_u[2 * p + 1]))
        for d in dirs:
            for p in range(n_pairs):
                n_scr[2 * d + p:2 * d + p + 1, :] = n_run[d][p]
                for j in range(2):
                    c_scr[d, p, j * DK:(j + 1) * DK, :] = c_run[d][p][j]

    eye = lower & upper
    ones8 = jnp.ones((SUBLANES, L), BF16)
    sub8 = lax.broadcasted_iota(jnp.int32, (SUBLANES, LANES), 0)
    sub_h = lax.broadcasted_iota(jnp.int32, (H_B, L), 0)
    n_rows = [((sub8 == 2 * p) & head_mask[0]) | ((sub8 == 2 * p + 1) & head_mask[1]) for p in range(n_pairs)]

    def head_rows(vals):
        out = vals[0][0:H_B, :]
        for h in range(1, H_B):
            out = jnp.where(sub_h == h, vals[h][0:H_B, :], out)
        return out

    def out_group(ns):
        chunks = [(d, n) for n in ns for d in range(2)]
        pairs = [(d, n, p) for d, n in chunks for p in range(n_pairs)]
        units = [(d, n, p, j) for d, n, p in pairs for j in range(2)]
        cms = [lane_cummax(g_scr[d, n], d)[0:H_B, 0:L] for d, n in chunks]
        qk2s, qc2s, qn2s = [], [], []
        for d, n, p in pairs:
            r = _chunk_rows(n)
            qp = qk_scr[r, p * LANES:(p + 1) * LANES]
            q2 = jnp.concatenate([jnp.where(head_mask[j], qp, 0.0) for j in range(2)], axis=0).astype(BF16)
            qk2s.append(_dot_nt(q2, qk_scr[r, HK + p * LANES:HK + (p + 1) * LANES].astype(BF16)))
            qc2s.append(_dot(q2, call_scr[d, n, p]))
            nsel = jnp.where(n_rows[p], nall_scr[d, n, p:p + 1, :], 0.0).astype(BF16)
            qn2s.append(_dot_nt(nsel, qp.astype(BF16)))
        yield
        s_all = []
        for ui, (d, n, p, j) in enumerate(units):
            grow = g_scr[d, n, 2 * p + j:2 * p + j + 1, 0:L]
            e = jnp.where(tmask[d], grow, -jnp.inf)
            cmax = jnp.max(e, axis=-1, keepdims=True)
            s_all.append((qk2s[ui // 2][j * L:(j + 1) * L, :] * jnp.exp(e - cmax)).astype(BF16))
        yield
        nums = [_dot(s, v_ref[_chunk_rows(n), (2 * p + j) * DV:(2 * p + j + 1) * DV])
                for (d, n, p, j), s in zip(units, s_all)]
        dens = [_dot_nt(ones8, s) for s in s_all]
        yield
        scales = []
        for ci, (d, n) in enumerate(chunks):
            den_loc = head_rows(dens[ci * H_B:(ci + 1) * H_B])
            qn = qn2s[ci * n_pairs][0:H_B, :]
            for p in range(1, n_pairs):
                qn = qn + qn2s[ci * n_pairs + p][0:H_B, :]
            cm = cms[ci]
            m_prev = mall_scr[d, n, 0:H_B, 0:1]
            delta = cm - m_prev
            t = jnp.exp(-jnp.abs(delta))
            w_loc = jnp.where(delta <= 0.0, t, 1.0)
            w_inter = jnp.where(delta <= 0.0, 1.0, t)
            mt = f_scr[d, n, 0:H_B, 0:L] + jnp.maximum(m_prev, cm)
            den = w_loc * den_loc + w_inter * qn
            rinv = 1.0 / jnp.maximum(jnp.abs(den), jnp.exp(-mt))
            scales.append((w_loc * rinv, w_inter * rinv))
        yield
        hs = []
        for ui, (d, n, p, j) in enumerate(units):
            h = 2 * p + j
            sc_loc, sc_inter = scales[ui // H_B]
            d_loc = jnp.where(eye, sc_loc[h:h + 1, :], 0.0).astype(BF16)
            d_inter = jnp.where(eye, sc_inter[h:h + 1, :], 0.0).astype(BF16)
            hs.append(_dot(d_loc, nums[ui].astype(BF16))
                      + _dot(d_inter, qc2s[ui // 2][j * L:(j + 1) * L, :].astype(BF16)))
        yield
        for ni, n in enumerate(ns):
            r = _chunk_rows(n)
            for h in range(H_B):
                vs = slice(h * DV, (h + 1) * DV)
                o = hs[(2 * ni) * H_B + h] + hs[(2 * ni + 1) * H_B + h]
                out_ref[r, vs] = (_rms(o, gw_ref[:, vs]) * _sigmoid(og_ref[r, vs].astype(F32))).astype(out_ref.dtype)

    def finish():
        if write_state:
            for d in range(2):
                for p in range(n_pairs):
                    cnew_ref[d, p] = c_scr[d, p]
                    for j in range(2):
                        nnew_ref[d, 2 * p + j:2 * p + j + 1, :] = n_scr[2 * d + p:2 * d + p + 1, j * DK:(j + 1) * DK]
                mnew_ref[d:d + 1, :] = to_row(m_scr[H_B * d:H_B * (d + 1), 0:1])

    return state_group, out_group, finish


def _mlstm_scratch(T, C2, grid_w):
    n_pairs = C2 // 2 // LANES
    n_chunks = T // CHUNK
    pad_rows = 2 * (grid_w + SUBLANES) if T // grid_w > 1 else 2 * SUBLANES
    return [
        pltpu.VMEM((T + pad_rows, C2), F32),
        pltpu.VMEM((T, C2), F32),
        pltpu.VMEM((T, SMALL_W), F32),
        pltpu.VMEM((2, n_pairs, LANES, LANES), F32),
        pltpu.VMEM((SUBLANES, LANES), F32),
        pltpu.VMEM((SUBLANES, LANES), F32),
        pltpu.VMEM((2, n_chunks, n_pairs, LANES, LANES), BF16),
        pltpu.VMEM((2, n_chunks, SUBLANES, LANES), F32),
        pltpu.VMEM((2, n_chunks, SUBLANES, LANES), F32),
        pltpu.VMEM((2, n_chunks, SUBLANES, LANES), F32),
        pltpu.VMEM((2, n_chunks, SUBLANES, LANES), F32),
    ]


N_GLA_SCRATCH = 5
N_MLSTM_SCRATCH = 11


def _scan_kernel(*refs, cols, layer, has_state, write_state, n_cast, ride_ada, grid_w, unroll):
    refs = list(refs)
    z_refs = refs[:2]
    del refs[:2]
    s0_ref = c0_ref = n0_ref = m0_ref = None
    if has_state:
        s0_ref, c0_ref, n0_ref, m0_ref = refs[:4]
        del refs[:4]
    wa_ref, bal_ref, gwa_ref, cw_ref, bmg_ref, gwb_ref = refs[:6]
    del refs[:6]
    cast_in = refs[:n_cast]
    del refs[:n_cast]
    if ride_ada:
        ada_in = refs[:4]
        del refs[:4]
    outa_ref, outb_ref = refs[:2]
    del refs[:2]
    snew_ref = cnew_ref = nnew_ref = mnew_ref = None
    if write_state:
        snew_ref, cnew_ref, nnew_ref, mnew_ref = refs[:4]
        del refs[:4]
    cast_out = refs[:n_cast]
    del refs[:n_cast]
    if ride_ada:
        ada_out = refs.pop(0)
    wal_scr, bm_scr = refs[:2]
    del refs[:2]
    gla_scr = refs[:N_GLA_SCRATCH]
    mlstm_scr = refs[N_GLA_SCRATCH:]

    for src, dst in zip(cast_in, cast_out):
        dst[...] = src[...].astype(BF16)
    if ride_ada:
        _ada_tile(*ada_in, ada_out)

    R, HK = wa_ref.shape[1], wa_ref.shape[2]
    wal_scr[...] = jnp.zeros(wal_scr.shape, BF16)
    for d in range(2):
        wal_scr[d * R:(d + 1) * R, d * HK:(d + 1) * HK] = wa_ref[d].astype(BF16)
    lane = lax.broadcasted_iota(jnp.int32, (1, LANES), 1)
    bm = jnp.zeros((1, LANES), F32)
    for g in range(bmg_ref.shape[1]):
        for h in range(H_B):
            bm = jnp.where(lane == GATE_LANE0 + H_B * g + h, bmg_ref[layer, g, h], bm)
    bm_scr[0:1, :] = bm

    def view(name):
        a, c0, w = cols[name]
        return z_refs[a].at[:, pl.ds(c0, w)]

    sm_ref = view("small")
    n_chunks = z_refs[0].shape[0] // CHUNK
    gla = _gla_body(view("qa"), view("ka"), view("va"), view("ga"), sm_ref, s0_ref, wal_scr, bal_ref, gwa_ref,
                    outa_ref, snew_ref, *gla_scr)
    mlstm = _mlstm_body(view("qkb"), view("vb"), view("ob"), sm_ref, c0_ref, n0_ref, m0_ref, cw_ref,
                        bm_scr.at[0:1, :], gwb_ref, outb_ref, cnew_ref, nnew_ref, mnew_ref, *mlstm_scr,
                        grid_w=grid_w)
    gla_state, gla_out, gla_finish, decay_span = gla
    mlstm_state, mlstm_out, mlstm_finish = mlstm

    def passes(gla_out_fn):
        _chunk_loop(n_chunks, unroll, lambda ns: [fn([n], (d,)) for n in ns for d in range(2)
                                                  for fn in (mlstm_state, gla_state)])
        _chunk_loop(n_chunks, unroll, lambda ns: [fn([n]) for n in ns for fn in (mlstm_out, gla_out_fn)])

    wide_decay = decay_span > GLA_FACTORED_DECAY_MAX

    @pl.when(jnp.logical_not(wide_decay))
    def _():
        passes(gla_out)

    @pl.when(wide_decay)
    def _():
        passes(functools.partial(gla_out, exact_decay=True))

    gla_finish()
    mlstm_finish()


def _scan_call(z2d, row0, B, T, states, lw, layer, *, grid_w, write_state, casts=(), ada=None):
    assert row0 % T == 0 and all(z.shape[0] % T == 0 for z in z2d)
    z3 = [z.reshape(z.shape[0] // T, T, z.shape[1]) for z in z2d]
    blk0 = row0 // T
    HK = lw["w_alpha2"].shape[-1]
    DA = lw["gnorm_a_w"].shape[0]
    C2 = lw["conv_w"].shape[-1]
    DB = lw["gnorm_b_w"].shape[0]
    DK_A, DK_B = HK // H_A, C2 // 2 // H_B
    pa, pb = HK // LANES, C2 // 2 // LANES
    n_chunks = T // CHUNK
    has_state = states is not None
    widths = ((("qa", HK), ("ka", HK), ("qkb", C2), ("small", SMALL_W)),
              (("va", DA), ("ga", DA), ("vb", DB), ("ob", DB)))
    cols = {}
    for a, groups in enumerate(widths):
        c0 = 0
        for name, w in groups:
            cols[name] = (a, c0, w)
            c0 += w
        assert c0 == z3[a].shape[2]
    cast_in_specs, cast_out_specs, cast_out_shape, cast_args = _cast_specs(casts, B)
    kern = functools.partial(_scan_kernel, cols=cols, layer=layer, has_state=has_state, write_state=write_state,
                             n_cast=len(casts), ride_ada=ada is not None, grid_w=grid_w,
                             unroll=min(n_chunks, SCAN_UNROLL))

    def per_batch(shape):
        nd = len(shape)
        return pl.BlockSpec((None,) + tuple(shape), lambda b: (b,) + (0,) * nd)

    def per_batch_layer(shape):
        nd = len(shape)
        return pl.BlockSpec((None, None) + tuple(shape), lambda b: (b, layer) + (0,) * nd)

    def of_layer(a):
        return pl.BlockSpec((None,) + a.shape[1:], lambda b: (layer,) + (0,) * (a.ndim - 1))

    def whole(a):
        return pl.BlockSpec(a.shape, lambda b: (0,) * a.ndim)

    state_shapes = ((2, pa, LANES, LANES), (2, pb, LANES, LANES), (2, H_B, DK_B), (2, H_B))
    in_specs = [pl.BlockSpec((None, T, z.shape[2]), lambda b: (b + blk0, 0, 0)) for z in z3]
    args = list(z3)
    if has_state:
        s_gla, s_c, s_n, s_m = states
        depth = s_gla.shape[1]
        args += [s_gla.reshape((B, depth) + state_shapes[0]), s_c.reshape((B, depth) + state_shapes[1]), s_n, s_m]
        in_specs += [per_batch_layer(s) for s in state_shapes]
    args += [lw["w_alpha2"], lw["b_alpha"], lw["gnorm_a_w"].reshape(1, DA), lw["conv_w"], lw["b_mgate"],
             lw["gnorm_b_w"].reshape(1, DB)]
    in_specs += [of_layer(lw["w_alpha2"]), of_layer(lw["b_alpha"]), pl.BlockSpec((1, DA), lambda b: (0, 0)),
                 whole(lw["conv_w"]), pl.BlockSpec(memory_space=pltpu.SMEM), pl.BlockSpec((1, DB), lambda b: (0, 0))]
    args += cast_args
    in_specs += cast_in_specs
    out_specs = [per_batch((T, DA)), per_batch((T, DB))]
    out_shape = [jax.ShapeDtypeStruct((B, T, DA), BF16), jax.ShapeDtypeStruct((B, T, DB), BF16)]
    if write_state:
        out_specs += [per_batch(s) for s in state_shapes]
        out_shape += [jax.ShapeDtypeStruct((B,) + s, F32) for s in state_shapes]
    out_specs += cast_out_specs
    out_shape += cast_out_shape
    if ada is not None:
        cc, c, w_ada, b_ada, col0 = ada
        n_rest = w_ada.shape[1] - col0
        wcol = n_rest // B
        assert n_rest % B == 0 and wcol % LANES == 0 and col0 % wcol == 0
        args += [cc, c, w_ada, b_ada]
        in_specs += [whole(cc), whole(c),
                     pl.BlockSpec((w_ada.shape[0], wcol), lambda b: (0, col0 // wcol + b)),
                     pl.BlockSpec((1, wcol), lambda b: (0, col0 // wcol + b))]
        out_specs.append(pl.BlockSpec((COND_ROWS, wcol), lambda b: (0, b)))
        out_shape.append(jax.ShapeDtypeStruct((COND_ROWS, n_rest), F32))
    scratch = ([pltpu.VMEM((SMALL_W, 2 * HK), BF16), pltpu.VMEM((SUBLANES, LANES), F32)]
               + _gla_scratch(T, HK) + _mlstm_scratch(T, C2, grid_w))
    assert len(scratch) == 2 + N_GLA_SCRATCH + N_MLSTM_SCRATCH
    return pl.pallas_call(
        kern,
        grid=(B,),
        in_specs=in_specs,
        out_specs=out_specs,
        out_shape=out_shape,
        scratch_shapes=scratch,
        compiler_params=pltpu.CompilerParams(dimension_semantics=("arbitrary",),
                                             vmem_limit_bytes=VMEM_LIMIT),
        name="mixer_scans",
    )(*args)


def _outff_kernel(xc_ref, xl_ref, ac_ref, al_ref, bc_ref, bl_ref, mod_ref, n2_ref, fn_ref, wo_ref, w1_ref, w2_ref,
                  yc_ref, yl_ref, *, n_ctx, tiles_per_req, ff_chunk, final_norm):
    D = xc_ref.shape[1]
    DA = ac_ref.shape[1]
    is_ctx, row = _tile_group(n_ctx, tiles_per_req)

    def mod(k):
        return mod_ref[pl.ds(row, 1), (k - MOD_SPLIT) * D:(k - MOD_SPLIT + 1) * D]

    def tile(x_ref, a_ref, b_ref, y_ref):
        y = _dot(a_ref[...], wo_ref[0:DA, :]) + _dot(b_ref[...], wo_ref[DA:, :])
        x1 = x_ref[...] + mod(2) * y
        h2 = (_rms(x1, n2_ref[...]) * (1.0 + mod(4)) + mod(3)).astype(BF16)
        acc = jnp.zeros(x1.shape, F32)
        for c0 in range(0, w1_ref.shape[1], ff_chunk):
            u = jnp.maximum(_dot(h2, w1_ref[:, c0:c0 + ff_chunk]), 0.0)
            acc = acc + _dot((u * u).astype(BF16), w2_ref[c0:c0 + ff_chunk, :])
        x2 = x1 + mod(5) * acc
        y_ref[...] = _rms(x2, fn_ref[...]) if final_norm else x2

    @pl.when(is_ctx)
    def _():
        tile(xc_ref, ac_ref, bc_ref, yc_ref)

    @pl.when(jnp.logical_not(is_ctx))
    def _():
        tile(xl_ref, al_ref, bl_ref, yl_ref)


def _outff_call(xc2d, xl2d, ac, al, bc, bl, mod, norm2_w, final_w, wo, w1, w2, *, tm, tiles_per_req, final_norm):
    (Mc, D), Ml = xc2d.shape, xl2d.shape[0]
    n_ctx = Mc // tm
    DA = ac.shape[1]
    DFF = w1.shape[1]
    kern = functools.partial(_outff_kernel, n_ctx=n_ctx, tiles_per_req=tiles_per_req, ff_chunk=FF_CHUNK,
                             final_norm=final_norm)
    once = pl.Buffered(1)
    ctx, lat = _ctx_tile(n_ctx), _lat_tile(n_ctx)
    return pl.pallas_call(
        kern,
        grid=((Mc + Ml) // tm,),
        in_specs=[
            pl.BlockSpec((tm, D), ctx), pl.BlockSpec((tm, D), lat),
            pl.BlockSpec((tm, DA), ctx), pl.BlockSpec((tm, DA), lat),
            pl.BlockSpec((tm, D - DA), ctx), pl.BlockSpec((tm, D - DA), lat),
            pl.BlockSpec(mod.shape, lambda i: (0, 0)),
            pl.BlockSpec((1, D), lambda i: (0, 0)),
            pl.BlockSpec((1, D), lambda i: (0, 0)),
            pl.BlockSpec((D, D), lambda i: (0, 0), pipeline_mode=once),
            pl.BlockSpec((D, DFF), lambda i: (0, 0), pipeline_mode=once),
            pl.BlockSpec((DFF, D), lambda i: (0, 0), pipeline_mode=once),
        ],
        out_specs=[pl.BlockSpec((tm, D), ctx), pl.BlockSpec((tm, D), lat)],
        out_shape=[jax.ShapeDtypeStruct((Mc, D), F32), jax.ShapeDtypeStruct((Ml, D), F32)],
        compiler_params=pltpu.CompilerParams(dimension_semantics=("arbitrary",),
                                             vmem_limit_bytes=VMEM_LIMIT),
        name="outproj_mlp",
    )(xc2d, xl2d, ac, al, bc, bl, mod, norm2_w.reshape(1, D), final_w.reshape(1, D), wo, w1, w2)


def _layer(xc, xl, cond, ada_w, cached, lw, layer, ffw, final_w, final_norm):
    (Bc, Tc, D), (Bl, Tl, _) = xc.shape, xl.shape
    tm = TOKEN_TILE
    assert (Bc * Tc) % tm == 0 and Tl % tm == 0 and (Bc * Tc) % Tl == 0
    xc2d, xl2d = xc.reshape(Bc * Tc, D), xl.reshape(Bl * Tl, D)
    z = _inproj_call(xc2d, xl2d, *cond, *ada_w, lw["norm1_w"], lw["w_in_t"], tm=tm, tiles_per_req=Tl // tm,
                     f32_rows=lw["f32_rows"], small_rows=lw["small_rows"], bf16_rows=lw["bf16_rows"])
    res_c = _scan_call(z, 0, Bc, Tc, None, lw, layer, grid_w=Tc, write_state=True,
                       casts=((ffw[0], 0), (ffw[1], 0), (ffw[2], 0)), ada=(*cond, *ada_w, MOD_SPLIT * D))
    res_l = _scan_call(z, Bc * Tc, Bl, Tl, cached, lw, layer, grid_w=GRID_W, write_state=False)
    wo_b, w1_b, w2_b, mod_out = res_c[-4:]
    yc, yl = _outff_call(xc2d, xl2d, res_c[0].reshape(Bc * Tc, -1), res_l[0].reshape(Bl * Tl, -1),
                         res_c[1].reshape(Bc * Tc, -1), res_l[1].reshape(Bl * Tl, -1), mod_out, lw["norm2_w"],
                         final_w, wo_b, w1_b, w2_b, tm=tm, tiles_per_req=Tl // tm, final_norm=final_norm)
    return yc.reshape(Bc, Tc, D), yl.reshape(Bl, Tl, D), tuple(res_c[2:6])


def _layer_weights(l, norm1_w, norm2_w, w_in, w_alpha2, b_alpha, b_mgate, conv_w, gnorm_a_w, gnorm_b_w):
    hk_a = w_alpha2.shape[-1]
    d_a = gnorm_a_w.shape[-1]
    d_b = gnorm_b_w.shape[-1]
    hk_b = conv_w.shape[-1] // 2
    sizes = (hk_a, hk_a, d_a, d_a, 2 * R_ALPHA, hk_b, hk_b, d_b, d_b, 4 * H_B)
    assert w_alpha2.shape[2] == R_ALPHA and b_mgate.shape[1] * b_mgate.shape[2] == 4 * H_B
    offs = [0]
    for s in sizes:
        offs.append(offs[-1] + s)
    f32_rows = ((offs[0], offs[2] - offs[0]), (offs[5], offs[7] - offs[5]))
    small_rows = ((offs[4], offs[5] - offs[4]), (offs[9], offs[10] - offs[9]))
    bf16_rows = ((offs[2], offs[4] - offs[2]), (offs[7], offs[9] - offs[7]))
    assert all(n % LANES == 0 and r % BF16_ROWS == 0 for r, n in f32_rows + bf16_rows)
    return dict(
        norm1_w=norm1_w[l], norm2_w=norm2_w[l], w_in_t=jnp.swapaxes(w_in[l], 0, 1),
        f32_rows=f32_rows, small_rows=small_rows, bf16_rows=bf16_rows,
        w_alpha2=w_alpha2, b_alpha=b_alpha, b_mgate=b_mgate, conv_w=conv_w[l],
        gnorm_a_w=gnorm_a_w[l], gnorm_b_w=gnorm_b_w[l],
    )


def kernel(x_prompt, x_sample, c, state_gla, state_mlstm_C, state_mlstm_n, state_mlstm_m, c_ctx, w_ada, b_ada, norm1_w, norm2_w, w_in, w_alpha2, b_alpha, b_mgate, conv_w, gnorm_a_w, gnorm_b_w, w_out, w_ff1, w_ff2, final_norm_w):
    depth = w_in.shape[0]
    D = x_prompt.shape[-1]
    Bp, Tp, _ = x_prompt.shape
    Bs = x_sample.shape[0]
    assert 1 + Bs <= COND_ROWS
    cond = (c_ctx.reshape(1, D), c)
    cached = (state_gla, state_mlstm_C, state_mlstm_n, state_mlstm_m)
    xp, xs = x_prompt, x_sample
    s_gla, s_c, s_n, s_m = [], [], [], []
    for l in range(depth):
        lw = _layer_weights(l, norm1_w, norm2_w, w_in, w_alpha2, b_alpha, b_mgate, conv_w,
                            gnorm_a_w, gnorm_b_w)
        xp, xs, ctx = _layer(xp, xs, cond, (w_ada[l], b_ada[l].reshape(1, -1)), cached, lw, l,
                             (w_out[l], w_ff1[l], w_ff2[l]), final_norm_w, l == depth - 1)
        s_gla.append(ctx[0].reshape(Bp, 2, H_A, -1, ctx[0].shape[-1]))
        s_c.append(ctx[1].reshape(Bp, 2, H_B, -1, ctx[1].shape[-1]))
        s_n.append(ctx[2])
        s_m.append(ctx[3])
    dt = x_prompt.dtype
    return (xp, xs, jnp.stack(s_gla, axis=1).astype(dt), jnp.stack(s_c, axis=1).astype(dt),
            jnp.stack(s_n, axis=1).astype(dt), jnp.stack(s_m, axis=1).astype(dt))
```

```python
import functools
import math

import jax
import jax.numpy as jnp
from jax import lax
from jax.experimental import pallas as pl
from jax.experimental.pallas import tpu as pltpu

F32 = jnp.float32
BF16 = jnp.bfloat16

GRID_W = 64
H_A = 4
H_B = 4
R_ALPHA = 16
TAU_GLA = 16.0
CHUNK = 64
EPS = 1e-6
LANES = 128
SUBLANES = 8
BF16_ROWS = 16
COND_ROWS = SUBLANES
SMALL_W = LANES
GATE_LANE0 = 2 * R_ALPHA
VMEM_LIMIT = 56 * 1024 * 1024
SCAN_UNROLL = 4
PIPELINE_STARTS = 8
TOKEN_TILE = 512
GLA_FACTORED_DECAY_MAX = 60.0
MOD_SPLIT = 2


def _sigmoid(x):
    return 1.0 / (1.0 + jnp.exp(-x))


def _silu(x):
    return x * _sigmoid(x)


def _log_sigmoid(x):
    return jnp.minimum(x, 0.0) - jnp.log(1.0 + jnp.exp(-jnp.abs(x)))


def _dot(a, b):
    return jnp.dot(a, b, preferred_element_type=F32)


def _dot_nt(a, b):
    return lax.dot_general(a, b, (((1,), (1,)), ((), ())), preferred_element_type=F32)


def _rms(x, w):
    return x * lax.rsqrt(jnp.mean(x * x, axis=-1, keepdims=True) + EPS) * w


def _tri_sum(tri, x, terms=3):
    acc, rest = None, x
    for t in range(terms):
        part = rest.astype(BF16)
        prod = _dot(tri, part)
        acc = prod if acc is None else acc + prod
        if t + 1 < terms:
            rest = rest - part.astype(F32)
    return acc


def _chunk_masks(L):
    row = lax.broadcasted_iota(jnp.int32, (L, L), 0)
    col = lax.broadcasted_iota(jnp.int32, (L, L), 1)
    lower = row >= col
    upper = row <= col
    return lower, upper


def _ada_tile(cc_ref, c_ref, w_ref, b_ref, o_ref):
    D = cc_ref.shape[1]
    sub = lax.broadcasted_iota(jnp.int32, (COND_ROWS, D), 0)
    cond = jnp.where(sub == 0, cc_ref[...], 0.0)
    for r in range(c_ref.shape[0]):
        cond = jnp.where(sub == 1 + r, c_ref[r:r + 1, :], cond)
    o_ref[...] = _dot(_silu(cond).astype(BF16), w_ref[...].astype(BF16)) + b_ref[...]


def _tile_group(n_ctx, tiles_per_req):
    i = pl.program_id(0)
    is_ctx = i < n_ctx
    row = jnp.where(is_ctx, 0, 1 + jnp.maximum(i - n_ctx, 0) // tiles_per_req)
    return is_ctx, row


def _ctx_tile(n_ctx):
    return lambda i: (jnp.minimum(i, n_ctx - 1), 0)


def _lat_tile(n_ctx):
    return lambda i: (jnp.maximum(i - n_ctx, 0), 0)


def _inproj_kernel(xc_ref, xl_ref, cc_ref, c_ref, wa_ref, ba_ref, nw_ref, wt_ref, zf_ref, zh_ref, wb_scr, mod_ref,
                   *, n_ctx, tiles_per_req, f32_rows, small_rows, bf16_rows):
    D = xc_ref.shape[1]
    n_f32 = zf_ref.shape[1]

    @pl.when(pl.program_id(0) == 0)
    def _():
        _ada_tile(cc_ref, c_ref, wa_ref, ba_ref, mod_ref)

        def wide(rows, col):
            for r0, n in rows:
                wb_scr[col:col + n, :] = wt_ref[r0:r0 + n, :].astype(BF16)
                col += n
            return col

        col = wide(f32_rows, 0)
        parts = [wt_ref[r0:r0 + n, :] for r0, n in small_rows]
        n_small = sum(n for _, n in small_rows)
        parts.append(jnp.zeros((SMALL_W - n_small, D), F32))
        wb_scr[col:col + SMALL_W, :] = jnp.concatenate(parts, axis=0).astype(BF16)
        wide(bf16_rows, col + SMALL_W)

    is_ctx, row = _tile_group(n_ctx, tiles_per_req)

    def tile(x_ref):
        sh1 = mod_ref[pl.ds(row, 1), 0:D]
        sc1 = mod_ref[pl.ds(row, 1), D:2 * D]
        h = (_rms(x_ref[...], nw_ref[...]) * (1.0 + sc1) + sh1).astype(BF16)
        zf_ref[...] = _dot_nt(h, wb_scr[0:n_f32, :])
        zh_ref[...] = _dot_nt(h, wb_scr[n_f32:, :]).astype(BF16)

    @pl.when(is_ctx)
    def _():
        tile(xc_ref)

    @pl.when(jnp.logical_not(is_ctx))
    def _():
        tile(xl_ref)


def _inproj_call(xc2d, xl2d, cc, c, w_ada, b_ada, norm_w, w_in_t, *, tm, tiles_per_req, f32_rows, small_rows,
                 bf16_rows):
    (Mc, D), Ml = xc2d.shape, xl2d.shape[0]
    n_ctx = Mc // tm
    n_f32 = sum(n for _, n in f32_rows) + SMALL_W
    n_bf16 = sum(n for _, n in bf16_rows)
    n_out = n_f32 + n_bf16
    n_mod = MOD_SPLIT * D
    kern = functools.partial(_inproj_kernel, n_ctx=n_ctx, tiles_per_req=tiles_per_req,
                             f32_rows=f32_rows, small_rows=small_rows, bf16_rows=bf16_rows)
    once = pl.Buffered(1)
    return pl.pallas_call(
        kern,
        grid=((Mc + Ml) // tm,),
        in_specs=[
            pl.BlockSpec((tm, D), _ctx_tile(n_ctx)),
            pl.BlockSpec((tm, D), _lat_tile(n_ctx)),
            pl.BlockSpec(cc.shape, lambda i: (0, 0)),
            pl.BlockSpec(c.shape, lambda i: (0, 0)),
            pl.BlockSpec((D, n_mod), lambda i: (0, 0), pipeline_mode=once),
            pl.BlockSpec((1, n_mod), lambda i: (0, 0)),
            pl.BlockSpec((1, D), lambda i: (0, 0)),
            pl.BlockSpec(w_in_t.shape, lambda i: (0, 0), pipeline_mode=once),
        ],
        out_specs=[pl.BlockSpec((tm, n_f32), lambda i: (i, 0)), pl.BlockSpec((tm, n_bf16), lambda i: (i, 0))],
        out_shape=[jax.ShapeDtypeStruct((Mc + Ml, n_f32), F32), jax.ShapeDtypeStruct((Mc + Ml, n_bf16), BF16)],
        scratch_shapes=[pltpu.VMEM((n_out, D), BF16), pltpu.VMEM((COND_ROWS, n_mod), F32)],
        compiler_params=pltpu.CompilerParams(dimension_semantics=("arbitrary",),
                                             vmem_limit_bytes=VMEM_LIMIT),
        name="norm_inproj",
    )(xc2d, xl2d, cc, c, w_ada, b_ada, norm_w.reshape(1, D), w_in_t)


def _chunk_loop(n_chunks, unroll, make_units):
    def step(ns):
        pending = list(make_units(ns))
        active = []
        while pending or active:
            for _ in range(min(PIPELINE_STARTS, len(pending))):
                active.append(pending.pop(0))
            alive = []
            for g in active:
                try:
                    next(g)
                    alive.append(g)
                except StopIteration:
                    pass
            active = alive

    if unroll >= n_chunks:
        step(list(range(n_chunks)))
        return

    def body(i, carry):
        step([i * unroll + u for u in range(unroll)])
        return carry

    lax.fori_loop(0, n_chunks // unroll, body, 0)


def _chunk_rows(n):
    if isinstance(n, int):
        return pl.ds(n * CHUNK, CHUNK)
    return pl.ds(pl.multiple_of(n * CHUNK, CHUNK), CHUNK)


def _cast_specs(casts, n_steps):
    in_specs, out_specs, out_shape, args = [], [], [], []
    for w, axis in casts:
        blk = list(w.shape)
        assert blk[axis] % n_steps == 0
        blk[axis] //= n_steps
        assert blk[0] % BF16_ROWS == 0 and blk[1] % LANES == 0
        idx = (lambda b: (b, 0)) if axis == 0 else (lambda b: (0, b))
        in_specs.append(pl.BlockSpec(tuple(blk), idx))
        out_specs.append(pl.BlockSpec(tuple(blk), idx))
        out_shape.append(jax.ShapeDtypeStruct(w.shape, BF16))
        args.append(w)
    return in_specs, out_specs, out_shape, args


def _gla_body(q_ref, k_ref, v_ref, g_ref, sm_ref, s0_ref, wal_ref, bal_ref, gw_ref, out_ref, snew_ref,
              st_scr, sall_scr, qh_scr, qs_scr, kh_scr):
    has_state = s0_ref is not None
    write_state = snew_ref is not None
    T = q_ref.shape[0]
    L = CHUNK
    N = T // L
    HK = q_ref.shape[1]
    DK = HK // H_A
    DV = v_ref.shape[1] // H_A
    scale = DK ** -0.5
    n_pairs = HK // LANES

    lower, upper = _chunk_masks(L)
    tri = (lower.astype(BF16), upper.astype(BF16))
    tmask = (lower, upper)
    lane = lax.broadcasted_iota(jnp.int32, (1, LANES), 1)
    head_mask = (lane < DK, lane >= DK)

    for d in range(2):
        for p in range(n_pairs):
            if has_state:
                st_scr[d, p] = s0_ref[d, p].T
            else:
                st_scr[d, p] = jnp.zeros((LANES, LANES), F32)

    def decay_pre(d, r):
        return _dot(sm_ref[r, :].astype(BF16), wal_ref[:, d * HK:(d + 1) * HK]) + bal_ref[d:d + 1, :]

    neg_pre = jnp.maximum(-(_dot(sm_ref[...].astype(BF16), wal_ref[...])
                            + jnp.concatenate([bal_ref[0:1, :], bal_ref[1:2, :]], axis=1)), 0.0)
    chunk_sums = jnp.sum(neg_pre.reshape(N, L, 2 * HK), axis=1)
    decay_span = (jnp.max(chunk_sums) + L * math.log(2.0)) * (1.0 / TAU_GLA)

    def state_group(ns, dirs=(0, 1)):
        units = [(d, n if d == 0 else N - 1 - n) for n in ns for d in dirs]
        rows = [_chunk_rows(n) for _, n in units]
        vt_all = [[jnp.concatenate([v_ref[r, (2 * p + j) * DV:(2 * p + j + 1) * DV] for j in range(2)],
                                   axis=0).astype(F32).T.astype(BF16) for p in range(n_pairs)] for r in rows]
        yield
        pre = [decay_pre(d, r) for (d, _), r in zip(units, rows)]
        yield
        g = [_log_sigmoid(x) * (1.0 / TAU_GLA) for x in pre]
        yield
        b = [_tri_sum(tri[d], gi, terms=2) for (d, _), gi in zip(units, g)]
        yield
        ks_all, dec_all = [], []
        for (d, _), r, bi in zip(units, rows, b):
            bend = bi[L - 1:L, :] if d == 0 else bi[0:1, :]
            q = q_ref[r, :] * scale
            ks = (k_ref[r, :] * jnp.exp(bend - bi)).astype(BF16)
            qs = q * jnp.exp(bi)
            qh_scr[d, r, :] = (qs * jnp.exp(-bend)).astype(BF16)
            qs_scr[d, r, :] = qs.astype(BF16)
            kh_scr[d, r, :] = ks
            ks_all.append(ks)
            dec_all.append(jnp.exp(bend))
        yield
        upd_all = []
        for vt_u, ks in zip(vt_all, ks_all):
            upd_u = []
            for p in range(n_pairs):
                kp = ks[:, p * LANES:(p + 1) * LANES]
                kk = jnp.concatenate([jnp.where(head_mask[j], kp, jnp.zeros_like(kp)) for j in range(2)], axis=0)
                upd_u.append(_dot(vt_u[p], kk))
            upd_all.append(upd_u)
        yield
        st = {d: [st_scr[d, p] for p in range(n_pairs)] for d in dirs}
        for (d, n), dec, upd in zip(units, dec_all, upd_all):
            for p in range(n_pairs):
                sall_scr[d, n, p] = st[d][p].astype(BF16)
                st[d][p] = st[d][p] * dec[:, p * LANES:(p + 1) * LANES] + upd[p]
        for d in dirs:
            for p in range(n_pairs):
                st_scr[d, p] = st[d][p]

    def stack_heads(x):
        return jnp.concatenate([jnp.where(head_mask[j], x, jnp.zeros_like(x)) for j in range(2)], axis=0)

    tok = lax.broadcasted_iota(jnp.int32, (L, 1), 0)
    row_t = lax.broadcasted_iota(jnp.int32, (2 * L, L), 0) & (L - 1)
    col_s = lax.broadcasted_iota(jnp.int32, (2 * L, L), 1)

    def exact_scores(d, r, p):
        ls = slice(p * LANES, (p + 1) * LANES)
        b = _tri_sum(tri[d], _log_sigmoid(decay_pre(d, r)[:, ls]) * (1.0 / TAU_GLA))
        q = q_ref[r, ls] * scale
        k = k_ref[r, ls]
        acc = jnp.where(row_t == col_s, _dot_nt(stack_heads(q).astype(BF16), k.astype(BF16)), 0.0)
        src = lax.broadcasted_iota(jnp.int32, (L, L), 1)
        h = L // 2
        while h >= 1:
            first = tok & ~(2 * h - 1)
            edge = first + (h - 1 if d == 0 else h)
            b_edge = _tri_sum((src == edge).astype(BF16), b)
            upper = (tok & (2 * h - 1)) >= h
            later, earlier = (upper, ~upper) if d == 0 else (~upper, upper)
            qt = jnp.where(later, q * jnp.exp(b - b_edge), 0.0)
            kt = jnp.where(earlier, k * jnp.exp(b_edge - b), 0.0)
            sc = _dot_nt(stack_heads(qt).astype(BF16), kt.astype(BF16))
            acc = acc + jnp.where((row_t & ~(2 * h - 1)) == (col_s & ~(2 * h - 1)), sc, 0.0)
            h //= 2
        return acc

    def out_group(ns, exact_decay=False):
        pairs = [(d, ni, p) for ni in range(len(ns)) for d in range(2) for p in range(n_pairs)]
        scores, inter = [], []
        for d, ni, p in pairs:
            r = _chunk_rows(ns[ni])
            ls = slice(p * LANES, (p + 1) * LANES)
            if exact_decay:
                scores.append(exact_scores(d, r, p))
            else:
                scores.append(_dot_nt(stack_heads(qh_scr[d, r, ls]), kh_scr[d, r, ls]))
            inter.append(_dot_nt(stack_heads(qs_scr[d, r, ls]), sall_scr[d, ns[ni], p]))
        yield
        probs = [[jnp.where(tmask[d], sc[j * L:(j + 1) * L, :], 0.0).astype(BF16) for j in range(2)]
                 for (d, _, _), sc in zip(pairs, scores)]
        yield
        outs = {}
        for (d, ni, p), pr, it in zip(pairs, probs, inter):
            r = _chunk_rows(ns[ni])
            for j in range(2):
                vs = slice((2 * p + j) * DV, (2 * p + j + 1) * DV)
                outs[(d, ni, 2 * p + j)] = _dot(pr[j], v_ref[r, vs]) + it[j * L:(j + 1) * L, :]
        yield
        for ni, n in enumerate(ns):
            r = _chunk_rows(n)
            for h in range(H_A):
                vs = slice(h * DV, (h + 1) * DV)
                o = outs[(0, ni, h)] + outs[(1, ni, h)]
                out_ref[r, vs] = (_rms(o, gw_ref[:, vs]) * _silu(g_ref[r, vs].astype(F32))).astype(out_ref.dtype)

    def finish():
        if write_state:
            for d in range(2):
                for p in range(n_pairs):
                    snew_ref[d, p] = st_scr[d, p].T

    return state_group, out_group, finish, decay_span


def _gla_scratch(T, HK):
    n_pairs = HK // LANES
    n_chunks = T // CHUNK
    return [
        pltpu.VMEM((2, n_pairs, LANES, LANES), F32),
        pltpu.VMEM((2, n_chunks, n_pairs, LANES, LANES), BF16),
        pltpu.VMEM((2, T, HK), BF16),
        pltpu.VMEM((2, T, HK), BF16),
        pltpu.VMEM((2, T, HK), BF16),
    ]


def _mlstm_body(qk_ref, v_ref, og_ref, sm_ref, c0_ref, n0_ref, m0_ref, cw_ref, bm_ref, gw_ref,
                out_ref, cnew_ref, nnew_ref, mnew_ref,
                pad_scr, qk_scr, y_scr, c_scr, n_scr, m_scr, call_scr, nall_scr, mall_scr, g_scr, f_scr,
                *, grid_w):
    has_state = c0_ref is not None
    write_state = cnew_ref is not None
    T = qk_ref.shape[0]
    L = CHUNK
    N = T // L
    C2 = qk_ref.shape[1]
    HK = C2 // 2
    DK = HK // H_B
    DV = v_ref.shape[1] // H_B
    scale = DK ** -0.5
    n_pairs = HK // LANES
    P = pad_scr.shape[0] - T
    P0 = P // 2
    rows_img = T // grid_w

    lower, upper = _chunk_masks(L)
    tri = (lower.astype(BF16), upper.astype(BF16))
    tmask = (lower, upper)
    lane = lax.broadcasted_iota(jnp.int32, (1, LANES), 1)
    head_mask = (lane < DK, lane >= DK)
    lane_in = lane & (L - 1)

    def lane_cummax(x, d):
        k = 1
        while k < L:
            if d == 0:
                x = jnp.maximum(x, jnp.where(lane_in >= k, pltpu.roll(x, k, axis=1), -jnp.inf))
            else:
                x = jnp.maximum(x, jnp.where(lane_in < L - k, pltpu.roll(x, LANES - k, axis=1), -jnp.inf))
            k *= 2
        return x

    for d in range(2):
        for p in range(n_pairs):
            if has_state:
                c_scr[d, p] = c0_ref[d, p]
                n_scr[2 * d + p:2 * d + p + 1, :] = jnp.concatenate(
                    [n0_ref[d, 2 * p + j:2 * p + j + 1, :] for j in range(2)], axis=1)
            else:
                c_scr[d, p] = jnp.zeros((LANES, LANES), F32)
                n_scr[2 * d + p:2 * d + p + 1, :] = jnp.zeros((1, LANES), F32)
    eye_h = (lax.broadcasted_iota(jnp.int32, (H_B, H_B), 0) == lax.broadcasted_iota(jnp.int32, (H_B, H_B), 1))

    def to_col(row):
        return jnp.sum(jnp.where(eye_h, row, 0.0), axis=1, keepdims=True)

    def to_row(col):
        return jnp.sum(jnp.where(eye_h, col, 0.0), axis=0, keepdims=True)

    for d in range(2):
        if has_state:
            m_scr[H_B * d:H_B * (d + 1), 0:1] = to_col(m0_ref[d:d + 1, :])
        else:
            m_scr[H_B * d:H_B * (d + 1), 0:1] = jnp.zeros((H_B, 1), F32)

    pad_scr[0:P0, :] = jnp.zeros((P0, C2), F32)
    pad_scr[P0 + T:P + T, :] = jnp.zeros((P - P0, C2), F32)

    def copy_in(i, carry):
        r0 = pl.multiple_of(i * L, L)
        pad_scr[pl.ds(P0 + r0, L), :] = qk_ref[pl.ds(r0, L), :]
        return carry

    lax.fori_loop(0, N, copy_in, 0)

    lane_c = lax.broadcasted_iota(jnp.int32, (1, C2), 1)
    qscale = jnp.where(lane_c < HK, scale, 1.0).astype(F32)
    sub = lax.broadcasted_iota(jnp.int32, (L, 1), 0)
    img_rows = (0,) if rows_img == 1 else (-1, 0, 1)

    def conv_tile(i, carry):
        r0 = pl.multiple_of(i * L, L)
        col = lax.rem(r0, grid_w) + sub
        ok_left = col >= 1
        ok_right = col <= grid_w - 2
        sums = [None, None, None]
        for di in img_rows:
            blk = pad_scr[pl.ds(P0 + r0 + di * grid_w - SUBLANES, L + 2 * SUBLANES), :]
            for k in range(3):
                term = blk * cw_ref[di + 1, k:k + 1, :]
                sums[k] = term if sums[k] is None else sums[k] + term
        S = SUBLANES
        acc = (sums[1][S:S + L, :] + jnp.where(ok_left, sums[0][S - 1:S - 1 + L, :], 0.0)
               + jnp.where(ok_right, sums[2][S + 1:S + 1 + L, :], 0.0))
        qk_scr[pl.ds(r0, L), :] = _silu(acc) * qscale
        return carry

    lax.fori_loop(0, N, conv_tile, 0)

    gl = lane - GATE_LANE0
    is_f = ((gl >= H_B) & (gl < 2 * H_B)) | ((gl >= 3 * H_B) & (gl < 4 * H_B))

    def gate_tile(i, carry):
        rows = pl.ds(pl.multiple_of(i * L, L), L)
        x = sm_ref[rows, :] + bm_ref[...]
        y_scr[rows, :] = jnp.where(is_f, _log_sigmoid(x), x)
        return carry

    lax.fori_loop(0, N, gate_tile, 0)


    def state_group(ns, dirs=(0, 1)):
        units = [(d, n if d == 0 else N - 1 - n) for n in ns for d in dirs]
        rows = [_chunk_rows(n) for _, n in units]
        kt_all = [[qk_scr[r, HK + p * LANES:HK + (p + 1) * LANES].T for p in range(n_pairs)] for r in rows]
        yield
        xs = [y_scr[r, :] for r in rows]
        fsum = [_tri_sum(tri[d], x) for (d, _), x in zip(units, xs)]
        yield
        wk_all, f_end, c_end = [], [], []
        for (d, n), r, x, fs in zip(units, rows, xs, fsum):
            y = jnp.where(is_f, fs, x)
            li0 = GATE_LANE0 + 2 * H_B * d
            blk = jnp.concatenate([y, y], axis=0).T[li0:li0 + 2 * H_B, :]
            frow = pltpu.roll(blk, H_B, axis=0)
            grow = blk - frow
            g_scr[d, n] = grow
            f_scr[d, n] = frow
            e_col = L - 1 if d == 0 else 0
            f_end.append(frow[0:H_B, e_col:e_col + 1])
            ce8 = jnp.max(grow, axis=1, keepdims=True)
            c_end.append(ce8[0:H_B, :])
            wk_all.append(jnp.exp(grow[:, 0:L] - ce8))
        yield
        kv_all, ksum_all = [], []
        for r, wk8, kt_u in zip(rows, wk_all, kt_all):
            kv_u, ks_u = [], []
            wk8b = wk8.astype(BF16)
            for p in range(n_pairs):
                kpb = qk_scr[r, HK + p * LANES:HK + (p + 1) * LANES].astype(BF16)
                ks8 = _dot(wk8b, kpb)
                for j in range(2):
                    h = 2 * p + j
                    kwt = (kt_u[p][j * DK:(j + 1) * DK, :] * wk8[h:h + 1, :]).astype(BF16)
                    kv_u.append(_dot(kwt, v_ref[r, h * DV:(h + 1) * DV]))
                    ks_u.append(ks8[h:h + 1, :])
            kv_all.append(kv_u)
            ksum_all.append(ks_u)
        yield
        m_run = {d: m_scr[H_B * d:H_B * (d + 1), 0:1] for d in dirs}
        a_all, b_all = [], []
        for (d, n), fe, ce in zip(units, f_end, c_end):
            mall_scr[d, n, 0:H_B, 0:1] = m_run[d]
            mx = jnp.maximum(m_run[d], ce)
            a_all.append(jnp.exp(m_run[d] - mx))
            b_all.append(jnp.exp(ce - mx))
            m_run[d] = fe + mx
        for d in dirs:
            m_scr[H_B * d:H_B * (d + 1), 0:1] = m_run[d]
        yield
        c_run = {d: [[c_scr[d, p, j * DK:(j + 1) * DK, :] for j in range(2)] for p in range(n_pairs)] for d in dirs}
        n_run = {d: [n_scr[2 * d + p:2 * d + p + 1, :] for p in range(n_pairs)] for d in dirs}
        for (d, n), a4, b4, kv_u, ks_u in zip(units, a_all, b_all, kv_all, ksum_all):
            for p in range(n_pairs):
                nall_scr[d, n, p:p + 1, :] = n_run[d][p]
                a_s = [a4[2 * p + j:2 * p + j + 1, :] for j in range(2)]
                b_s = [b4[2 * p + j:2 * p + j + 1, :] for j in range(2)]
                for j in range(2):
                    cj = c_run[d][p][j]
                    call_scr[d, n, p, j * DK:(j + 1) * DK, :] = cj.astype(BF16)
                    c_run[d][p][j] = a_s[j] * cj + b_s[j] * kv_u[2 * p + j]
                n_run[d][p] = (jnp.where(head_mask[0], a_s[0], a_s[1]) * n_run[d][p]
                               + jnp.where(head_mask[0], b_s[0] * ks_u[2 * p], b_s[1] * ks_u[2 * p + 1]))
        for d in dirs:
            for p in range(n_pairs):
                n_scr[2 * d + p:2 * d + p + 1, :] = n_run[d][p]
                for j in range(2):
                    c_scr[d, p, j * DK:(j + 1) * DK, :] = c_run[d][p][j]

    eye = lower & upper
    ones8 = jnp.ones((SUBLANES, L), BF16)
    sub8 = lax.broadcasted_iota(jnp.int32, (SUBLANES, LANES), 0)
    sub_h = lax.broadcasted_iota(jnp.int32, (H_B, L), 0)
    n_rows = [((sub8 == 2 * p) & head_mask[0]) | ((sub8 == 2 * p + 1) & head_mask[1]) for p in range(n_pairs)]

    def head_rows(vals):
        out = vals[0][0:H_B, :]
        for h in range(1, H_B):
            out = jnp.where(sub_h == h, vals[h][0:H_B, :], out)
        return out

    def out_group(ns):
        chunks = [(d, n) for n in ns for d in range(2)]
        pairs = [(d, n, p) for d, n in chunks for p in range(n_pairs)]
        units = [(d, n, p, j) for d, n, p in pairs for j in range(2)]
        cms = [lane_cummax(g_scr[d, n], d)[0:H_B, 0:L] for d, n in chunks]
        qk2s, qc2s, qn2s = [], [], []
        for d, n, p in pairs:
            r = _chunk_rows(n)
            qp = qk_scr[r, p * LANES:(p + 1) * LANES]
            q2 = jnp.concatenate([jnp.where(head_mask[j], qp, 0.0) for j in range(2)], axis=0).astype(BF16)
            qk2s.append(_dot_nt(q2, qk_scr[r, HK + p * LANES:HK + (p + 1) * LANES].astype(BF16)))
            qc2s.append(_dot(q2, call_scr[d, n, p]))
            nsel = jnp.where(n_rows[p], nall_scr[d, n, p:p + 1, :], 0.0).astype(BF16)
            qn2s.append(_dot_nt(nsel, qp.astype(BF16)))
        yield
        s_all = []
        for ui, (d, n, p, j) in enumerate(units):
            grow = g_scr[d, n, 2 * p + j:2 * p + j + 1, 0:L]
            e = jnp.where(tmask[d], grow, -jnp.inf)
            cmax = jnp.max(e, axis=-1, keepdims=True)
            s_all.append((qk2s[ui // 2][j * L:(j + 1) * L, :] * jnp.exp(e - cmax)).astype(BF16))
        yield
        nums = [_dot(s, v_ref[_chunk_rows(n), (2 * p + j) * DV:(2 * p + j + 1) * DV])
                for (d, n, p, j), s in zip(units, s_all)]
        dens = [_dot_nt(ones8, s) for s in s_all]
        yield
        scales = []
        for ci, (d, n) in enumerate(chunks):
            den_loc = head_rows(dens[ci * H_B:(ci + 1) * H_B])
            qn = qn2s[ci * n_pairs][0:H_B, :]
            for p in range(1, n_pairs):
                qn = qn + qn2s[ci * n_pairs + p][0:H_B, :]
            cm = cms[ci]
            m_prev = mall_scr[d, n, 0:H_B, 0:1]
            delta = cm - m_prev
            t = jnp.exp(-jnp.abs(delta))
            w_loc = jnp.where(delta <= 0.0, t, 1.0)
            w_inter = jnp.where(delta <= 0.0, 1.0, t)
            mt = f_scr[d, n, 0:H_B, 0:L] + jnp.maximum(m_prev, cm)
            den = w_loc * den_loc + w_inter * qn
            rinv = 1.0 / jnp.maximum(jnp.abs(den), jnp.exp(-mt))
            scales.append((w_loc * rinv, w_inter * rinv))
        yield
        hs = []
        for ui, (d, n, p, j) in enumerate(units):
            h = 2 * p + j
            sc_loc, sc_inter = scales[ui // H_B]
            d_loc = jnp.where(eye, sc_loc[h:h + 1, :], 0.0).astype(BF16)
            d_inter = jnp.where(eye, sc_inter[h:h + 1, :], 0.0).astype(BF16)
            hs.append(_dot(d_loc, nums[ui].astype(BF16))
                      + _dot(d_inter, qc2s[ui // 2][j * L:(j + 1) * L, :].astype(BF16)))
        yield
        for ni, n in enumerate(ns):
            r = _chunk_rows(n)
            for h in range(H_B):
                vs = slice(h * DV, (h + 1) * DV)
                o = hs[(2 * ni) * H_B + h] + hs[(2 * ni + 1) * H_B + h]
                out_ref[r, vs] = (_rms(o, gw_ref[:, vs]) * _sigmoid(og_ref[r, vs].astype(F32))).astype(out_ref.dtype)

    def finish():
        if write_state:
            for d in range(2):
                for p in range(n_pairs):
                    cnew_ref[d, p] = c_scr[d, p]
                    for j in range(2):
                        nnew_ref[d, 2 * p + j:2 * p + j + 1, :] = n_scr[2 * d + p:2 * d + p + 1, j * DK:(j + 1) * DK]
                mnew_ref[d:d + 1, :] = to_row(m_scr[H_B * d:H_B * (d + 1), 0:1])

    return state_group, out_group, finish


def _mlstm_scratch(T, C2, grid_w):
    n_pairs = C2 // 2 // LANES
    n_chunks = T // CHUNK
    pad_rows = 2 * (grid_w + SUBLANES) if T // grid_w > 1 else 2 * SUBLANES
    return [
        pltpu.VMEM((T + pad_rows, C2), F32),
        pltpu.VMEM((T, C2), F32),
        pltpu.VMEM((T, SMALL_W), F32),
        pltpu.VMEM((2, n_pairs, LANES, LANES), F32),
        pltpu.VMEM((SUBLANES, LANES), F32),
        pltpu.VMEM((SUBLANES, LANES), F32),
        pltpu.VMEM((2, n_chunks, n_pairs, LANES, LANES), BF16),
        pltpu.VMEM((2, n_chunks, SUBLANES, LANES), F32),
        pltpu.VMEM((2, n_chunks, SUBLANES, LANES), F32),
        pltpu.VMEM((2, n_chunks, SUBLANES, LANES), F32),
        pltpu.VMEM((2, n_chunks, SUBLANES, LANES), F32),
    ]


N_GLA_SCRATCH = 5
N_MLSTM_SCRATCH = 11


def _scan_kernel(*refs, cols, layer, has_state, write_state, n_cast, ride_ada, grid_w, unroll):
    refs = list(refs)
    z_refs = refs[:2]
    del refs[:2]
    s0_ref = c0_ref = n0_ref = m0_ref = None
    if has_state:
        s0_ref, c0_ref, n0_ref, m0_ref = refs[:4]
        del refs[:4]
    wa_ref, bal_ref, gwa_ref, cw_ref, bmg_ref, gwb_ref = refs[:6]
    del refs[:6]
    cast_in = refs[:n_cast]
    del refs[:n_cast]
    if ride_ada:
        ada_in = refs[:4]
        del refs[:4]
    outa_ref, outb_ref = refs[:2]
    del refs[:2]
    snew_ref = cnew_ref = nnew_ref = mnew_ref = None
    if write_state:
        snew_ref, cnew_ref, nnew_ref, mnew_ref = refs[:4]
        del refs[:4]
    cast_out = refs[:n_cast]
    del refs[:n_cast]
    if ride_ada:
        ada_out = refs.pop(0)
    wal_scr, bm_scr = refs[:2]
    del refs[:2]
    gla_scr = refs[:N_GLA_SCRATCH]
    mlstm_scr = refs[N_GLA_SCRATCH:]

    for src, dst in zip(cast_in, cast_out):
        dst[...] = src[...].astype(BF16)
    if ride_ada:
        _ada_tile(*ada_in, ada_out)

    R, HK = wa_ref.shape[1], wa_ref.shape[2]
    wal_scr[...] = jnp.zeros(wal_scr.shape, BF16)
    for d in range(2):
        wal_scr[d * R:(d + 1) * R, d * HK:(d + 1) * HK] = wa_ref[d].astype(BF16)
    lane = lax.broadcasted_iota(jnp.int32, (1, LANES), 1)
    bm = jnp.zeros((1, LANES), F32)
    for g in range(bmg_ref.shape[1]):
        for h in range(H_B):
            bm = jnp.where(lane == GATE_LANE0 + H_B * g + h, bmg_ref[layer, g, h], bm)
    bm_scr[0:1, :] = bm

    def view(name):
        a, c0, w = cols[name]
        return z_refs[a].at[:, pl.ds(c0, w)]

    sm_ref = view("small")
    n_chunks = z_refs[0].shape[0] // CHUNK
    gla = _gla_body(view("qa"), view("ka"), view("va"), view("ga"), sm_ref, s0_ref, wal_scr, bal_ref, gwa_ref,
                    outa_ref, snew_ref, *gla_scr)
    mlstm = _mlstm_body(view("qkb"), view("vb"), view("ob"), sm_ref, c0_ref, n0_ref, m0_ref, cw_ref,
                        bm_scr.at[0:1, :], gwb_ref, outb_ref, cnew_ref, nnew_ref, mnew_ref, *mlstm_scr,
                        grid_w=grid_w)
    gla_state, gla_out, gla_finish, decay_span = gla
    mlstm_state, mlstm_out, mlstm_finish = mlstm

    def passes(gla_out_fn):
        _chunk_loop(n_chunks, unroll, lambda ns: [fn([n], (d,)) for n in ns for d in range(2)
                                                  for fn in (mlstm_state, gla_state)])
        _chunk_loop(n_chunks, unroll, lambda ns: [fn([n]) for n in ns for fn in (mlstm_out, gla_out_fn)])

    wide_decay = decay_span > GLA_FACTORED_DECAY_MAX

    @pl.when(jnp.logical_not(wide_decay))
    def _():
        passes(gla_out)

    @pl.when(wide_decay)
    def _():
        passes(functools.partial(gla_out, exact_decay=True))

    gla_finish()
    mlstm_finish()


def _scan_call(z2d, row0, B, T, states, lw, layer, *, grid_w, write_state, casts=(), ada=None):
    assert row0 % T == 0 and all(z.shape[0] % T == 0 for z in z2d)
    z3 = [z.reshape(z.shape[0] // T, T, z.shape[1]) for z in z2d]
    blk0 = row0 // T
    HK = lw["w_alpha2"].shape[-1]
    DA = lw["gnorm_a_w"].shape[0]
    C2 = lw["conv_w"].shape[-1]
    DB = lw["gnorm_b_w"].shape[0]
    DK_A, DK_B = HK // H_A, C2 // 2 // H_B
    pa, pb = HK // LANES, C2 // 2 // LANES
    n_chunks = T // CHUNK
    has_state = states is not None
    widths = ((("qa", HK), ("ka", HK), ("qkb", C2), ("small", SMALL_W)),
              (("va", DA), ("ga", DA), ("vb", DB), ("ob", DB)))
    cols = {}
    for a, groups in enumerate(widths):
        c0 = 0
        for name, w in groups:
            cols[name] = (a, c0, w)
            c0 += w
        assert c0 == z3[a].shape[2]
    cast_in_specs, cast_out_specs, cast_out_shape, cast_args = _cast_specs(casts, B)
    kern = functools.partial(_scan_kernel, cols=cols, layer=layer, has_state=has_state, write_state=write_state,
                             n_cast=len(casts), ride_ada=ada is not None, grid_w=grid_w,
                             unroll=min(n_chunks, SCAN_UNROLL))

    def per_batch(shape):
        nd = len(shape)
        return pl.BlockSpec((None,) + tuple(shape), lambda b: (b,) + (0,) * nd)

    def per_batch_layer(shape):
        nd = len(shape)
        return pl.BlockSpec((None, None) + tuple(shape), lambda b: (b, layer) + (0,) * nd)

    def of_layer(a):
        return pl.BlockSpec((None,) + a.shape[1:], lambda b: (layer,) + (0,) * (a.ndim - 1))

    def whole(a):
        return pl.BlockSpec(a.shape, lambda b: (0,) * a.ndim)

    state_shapes = ((2, pa, LANES, LANES), (2, pb, LANES, LANES), (2, H_B, DK_B), (2, H_B))
    in_specs = [pl.BlockSpec((None, T, z.shape[2]), lambda b: (b + blk0, 0, 0)) for z in z3]
    args = list(z3)
    if has_state:
        s_gla, s_c, s_n, s_m = states
        depth = s_gla.shape[1]
        args += [s_gla.reshape((B, depth) + state_shapes[0]), s_c.reshape((B, depth) + state_shapes[1]), s_n, s_m]
        in_specs += [per_batch_layer(s) for s in state_shapes]
    args += [lw["w_alpha2"], lw["b_alpha"], lw["gnorm_a_w"].reshape(1, DA), lw["conv_w"], lw["b_mgate"],
             lw["gnorm_b_w"].reshape(1, DB)]
    in_specs += [of_layer(lw["w_alpha2"]), of_layer(lw["b_alpha"]), pl.BlockSpec((1, DA), lambda b: (0, 0)),
                 whole(lw["conv_w"]), pl.BlockSpec(memory_space=pltpu.SMEM), pl.BlockSpec((1, DB), lambda b: (0, 0))]
    args += cast_args
    in_specs += cast_in_specs
    out_specs = [per_batch((T, DA)), per_batch((T, DB))]
    out_shape = [jax.ShapeDtypeStruct((B, T, DA), BF16), jax.ShapeDtypeStruct((B, T, DB), BF16)]
    if write_state:
        out_specs += [per_batch(s) for s in state_shapes]
        out_shape += [jax.ShapeDtypeStruct((B,) + s, F32) for s in state_shapes]
    out_specs += cast_out_specs
    out_shape += cast_out_shape
    if ada is not None:
        cc, c, w_ada, b_ada, col0 = ada
        n_rest = w_ada.shape[1] - col0
        wcol = n_rest // B
        assert n_rest % B == 0 and wcol % LANES == 0 and col0 % wcol == 0
        args += [cc, c, w_ada, b_ada]
        in_specs += [whole(cc), whole(c),
                     pl.BlockSpec((w_ada.shape[0], wcol), lambda b: (0, col0 // wcol + b)),
                     pl.BlockSpec((1, wcol), lambda b: (0, col0 // wcol + b))]
        out_specs.append(pl.BlockSpec((COND_ROWS, wcol), lambda b: (0, b)))
        out_shape.append(jax.ShapeDtypeStruct((COND_ROWS, n_rest), F32))
    scratch = ([pltpu.VMEM((SMALL_W, 2 * HK), BF16), pltpu.VMEM((SUBLANES, LANES), F32)]
               + _gla_scratch(T, HK) + _mlstm_scratch(T, C2, grid_w))
    assert len(scratch) == 2 + N_GLA_SCRATCH + N_MLSTM_SCRATCH
    return pl.pallas_call(
        kern,
        grid=(B,),
        in_specs=in_specs,
        out_specs=out_specs,
        out_shape=out_shape,
        scratch_shapes=scratch,
        compiler_params=pltpu.CompilerParams(dimension_semantics=("arbitrary",),
                                             vmem_limit_bytes=VMEM_LIMIT),
        name="mixer_scans",
    )(*args)


def _outff_kernel(xc_ref, xl_ref, ac_ref, al_ref, bc_ref, bl_ref, mod_ref, n2_ref, fn_ref, wo_ref, w1_ref, w2_ref,
                  yc_ref, yl_ref, *, n_ctx, tiles_per_req, final_norm):
    D = xc_ref.shape[1]
    DA = ac_ref.shape[1]
    is_ctx, row = _tile_group(n_ctx, tiles_per_req)

    def mod(k):
        return mod_ref[pl.ds(row, 1), (k - MOD_SPLIT) * D:(k - MOD_SPLIT + 1) * D]

    def tile(x_ref, a_ref, b_ref, y_ref):
        y = _dot(a_ref[...], wo_ref[0:DA, :]) + _dot(b_ref[...], wo_ref[DA:, :])
        x1 = x_ref[...] + mod(2) * y
        h2 = (_rms(x1, n2_ref[...]) * (1.0 + mod(4)) + mod(3)).astype(BF16)
        u = jnp.maximum(_dot(h2, w1_ref[...]), 0.0)
        x2 = x1 + mod(5) * _dot((u * u).astype(BF16), w2_ref[...])
        y_ref[...] = _rms(x2, fn_ref[...]) if final_norm else x2

    @pl.when(is_ctx)
    def _():
        tile(xc_ref, ac_ref, bc_ref, yc_ref)

    @pl.when(jnp.logical_not(is_ctx))
    def _():
        tile(xl_ref, al_ref, bl_ref, yl_ref)


def _outff_call(xc2d, xl2d, ac, al, bc, bl, mod, norm2_w, final_w, wo, w1, w2, *, tm, tiles_per_req, final_norm):
    (Mc, D), Ml = xc2d.shape, xl2d.shape[0]
    n_ctx = Mc // tm
    DA = ac.shape[1]
    DFF = w1.shape[1]
    kern = functools.partial(_outff_kernel, n_ctx=n_ctx, tiles_per_req=tiles_per_req, final_norm=final_norm)
    once = pl.Buffered(1)
    ctx, lat = _ctx_tile(n_ctx), _lat_tile(n_ctx)
    return pl.pallas_call(
        kern,
        grid=((Mc + Ml) // tm,),
        in_specs=[
            pl.BlockSpec((tm, D), ctx), pl.BlockSpec((tm, D), lat),
            pl.BlockSpec((tm, DA), ctx), pl.BlockSpec((tm, DA), lat),
            pl.BlockSpec((tm, D - DA), ctx), pl.BlockSpec((tm, D - DA), lat),
            pl.BlockSpec(mod.shape, lambda i: (0, 0)),
            pl.BlockSpec((1, D), lambda i: (0, 0)),
            pl.BlockSpec((1, D), lambda i: (0, 0)),
            pl.BlockSpec((D, D), lambda i: (0, 0), pipeline_mode=once),
            pl.BlockSpec((D, DFF), lambda i: (0, 0), pipeline_mode=once),
            pl.BlockSpec((DFF, D), lambda i: (0, 0), pipeline_mode=once),
        ],
        out_specs=[pl.BlockSpec((tm, D), ctx), pl.BlockSpec((tm, D), lat)],
        out_shape=[jax.ShapeDtypeStruct((Mc, D), F32), jax.ShapeDtypeStruct((Ml, D), F32)],
        compiler_params=pltpu.CompilerParams(dimension_semantics=("arbitrary",),
                                             vmem_limit_bytes=VMEM_LIMIT),
        name="outproj_mlp",
    )(xc2d, xl2d, ac, al, bc, bl, mod, norm2_w.reshape(1, D), final_w.reshape(1, D), wo, w1, w2)


def _layer(xc, xl, cond, ada_w, cached, lw, layer, ffw, final_w, final_norm):
    (Bc, Tc, D), (Bl, Tl, _) = xc.shape, xl.shape
    tm = TOKEN_TILE
    assert (Bc * Tc) % tm == 0 and Tl % tm == 0 and (Bc * Tc) % Tl == 0
    xc2d, xl2d = xc.reshape(Bc * Tc, D), xl.reshape(Bl * Tl, D)
    z = _inproj_call(xc2d, xl2d, *cond, *ada_w, lw["norm1_w"], lw["w_in_t"], tm=tm, tiles_per_req=Tl // tm,
                     f32_rows=lw["f32_rows"], small_rows=lw["small_rows"], bf16_rows=lw["bf16_rows"])
    res_c = _scan_call(z, 0, Bc, Tc, None, lw, layer, grid_w=Tc, write_state=True,
                       casts=((ffw[0], 0), (ffw[1], 0), (ffw[2], 0)), ada=(*cond, *ada_w, MOD_SPLIT * D))
    res_l = _scan_call(z, Bc * Tc, Bl, Tl, cached, lw, layer, grid_w=GRID_W, write_state=False)
    wo_b, w1_b, w2_b, mod_out = res_c[-4:]
    yc, yl = _outff_call(xc2d, xl2d, res_c[0].reshape(Bc * Tc, -1), res_l[0].reshape(Bl * Tl, -1),
                         res_c[1].reshape(Bc * Tc, -1), res_l[1].reshape(Bl * Tl, -1), mod_out, lw["norm2_w"],
                         final_w, wo_b, w1_b, w2_b, tm=tm, tiles_per_req=Tl // tm, final_norm=final_norm)
    return yc.reshape(Bc, Tc, D), yl.reshape(Bl, Tl, D), tuple(res_c[2:6])


def _layer_weights(l, norm1_w, norm2_w, w_in, w_alpha2, b_alpha, b_mgate, conv_w, gnorm_a_w, gnorm_b_w):
    hk_a = w_alpha2.shape[-1]
    d_a = gnorm_a_w.shape[-1]
    d_b = gnorm_b_w.shape[-1]
    hk_b = conv_w.shape[-1] // 2
    sizes = (hk_a, hk_a, d_a, d_a, 2 * R_ALPHA, hk_b, hk_b, d_b, d_b, 4 * H_B)
    assert w_alpha2.shape[2] == R_ALPHA and b_mgate.shape[1] * b_mgate.shape[2] == 4 * H_B
    offs = [0]
    for s in sizes:
        offs.append(offs[-1] + s)
    f32_rows = ((offs[0], offs[2] - offs[0]), (offs[5], offs[7] - offs[5]))
    small_rows = ((offs[4], offs[5] - offs[4]), (offs[9], offs[10] - offs[9]))
    bf16_rows = ((offs[2], offs[4] - offs[2]), (offs[7], offs[9] - offs[7]))
    assert all(n % LANES == 0 and r % BF16_ROWS == 0 for r, n in f32_rows + bf16_rows)
    return dict(
        norm1_w=norm1_w[l], norm2_w=norm2_w[l], w_in_t=jnp.swapaxes(w_in[l], 0, 1),
        f32_rows=f32_rows, small_rows=small_rows, bf16_rows=bf16_rows,
        w_alpha2=w_alpha2, b_alpha=b_alpha, b_mgate=b_mgate, conv_w=conv_w[l],
        gnorm_a_w=gnorm_a_w[l], gnorm_b_w=gnorm_b_w[l],
    )


def kernel(x_prompt, x_sample, c, state_gla, state_mlstm_C, state_mlstm_n, state_mlstm_m, c_ctx, w_ada, b_ada, norm1_w, norm2_w, w_in, w_alpha2, b_alpha, b_mgate, conv_w, gnorm_a_w, gnorm_b_w, w_out, w_ff1, w_ff2, final_norm_w):
    depth = w_in.shape[0]
    D = x_prompt.shape[-1]
    Bp, Tp, _ = x_prompt.shape
    Bs = x_sample.shape[0]
    assert 1 + Bs <= COND_ROWS
    cond = (c_ctx.reshape(1, D), c)
    cached = (state_gla, state_mlstm_C, state_mlstm_n, state_mlstm_m)
    xp, xs = x_prompt, x_sample
    s_gla, s_c, s_n, s_m = [], [], [], []
    for l in range(depth):
        lw = _layer_weights(l, norm1_w, norm2_w, w_in, w_alpha2, b_alpha, b_mgate, conv_w,
                            gnorm_a_w, gnorm_b_w)
        xp, xs, ctx = _layer(xp, xs, cond, (w_ada[l], b_ada[l].reshape(1, -1)), cached, lw, l,
                             (w_out[l], w_ff1[l], w_ff2[l]), final_norm_w, l == depth - 1)
        s_gla.append(ctx[0].reshape(Bp, 2, H_A, -1, ctx[0].shape[-1]))
        s_c.append(ctx[1].reshape(Bp, 2, H_B, -1, ctx[1].shape[-1]))
        s_n.append(ctx[2])
        s_m.append(ctx[3])
    dt = x_prompt.dtype
    return (xp, xs, jnp.stack(s_gla, axis=1).astype(dt), jnp.stack(s_c, axis=1).astype(dt),
            jnp.stack(s_n, axis=1).astype(dt), jnp.stack(s_m, axis=1).astype(dt))
```

```python
import functools
import math

import jax
import jax.numpy as jnp
from jax import lax
from jax.experimental import pallas as pl
from jax.experimental.pallas import tpu as pltpu

F32 = jnp.float32
BF16 = jnp.bfloat16

GRID_W = 64
H_A = 4
H_B = 4
R_ALPHA = 16
TAU_GLA = 16.0
CHUNK = 64
EPS = 1e-6
LANES = 128
SUBLANES = 8
BF16_ROWS = 16
COND_ROWS = SUBLANES
SMALL_W = LANES
GATE_LANE0 = 2 * R_ALPHA
VMEM_LIMIT = 56 * 1024 * 1024
SCAN_UNROLL = 4
PIPELINE_STARTS = 8
TOKEN_TILE = 512
GLA_FACTORED_DECAY_MAX = 60.0
MOD_SPLIT = 2


def _sigmoid(x):
    return 1.0 / (1.0 + jnp.exp(-x))


def _silu(x):
    return x * _sigmoid(x)


def _log_sigmoid(x):
    return jnp.minimum(x, 0.0) - jnp.log(1.0 + jnp.exp(-jnp.abs(x)))


def _dot(a, b):
    return jnp.dot(a, b, preferred_element_type=F32)


def _dot_nt(a, b):
    return lax.dot_general(a, b, (((1,), (1,)), ((), ())), preferred_element_type=F32)


def _rms(x, w):
    return x * lax.rsqrt(jnp.mean(x * x, axis=-1, keepdims=True) + EPS) * w


def _tri_sum(tri, x, terms=3):
    acc, rest = None, x
    for t in range(terms):
        part = rest.astype(BF16)
        prod = _dot(tri, part)
        acc = prod if acc is None else acc + prod
        if t + 1 < terms:
            rest = rest - part.astype(F32)
    return acc


def _chunk_masks(L):
    row = lax.broadcasted_iota(jnp.int32, (L, L), 0)
    col = lax.broadcasted_iota(jnp.int32, (L, L), 1)
    lower = row >= col
    upper = row <= col
    return lower, upper


def _ada_tile(cc_ref, c_ref, w_ref, b_ref, o_ref):
    D = cc_ref.shape[1]
    sub = lax.broadcasted_iota(jnp.int32, (COND_ROWS, D), 0)
    cond = jnp.where(sub == 0, cc_ref[...], 0.0)
    for r in range(c_ref.shape[0]):
        cond = jnp.where(sub == 1 + r, c_ref[r:r + 1, :], cond)
    o_ref[...] = _dot(_silu(cond).astype(BF16), w_ref[...].astype(BF16)) + b_ref[...]


def _tile_group(n_ctx, tiles_per_req):
    i = pl.program_id(0)
    is_ctx = i < n_ctx
    row = jnp.where(is_ctx, 0, 1 + jnp.maximum(i - n_ctx, 0) // tiles_per_req)
    return is_ctx, row


def _ctx_tile(n_ctx):
    return lambda i: (jnp.minimum(i, n_ctx - 1), 0)


def _lat_tile(n_ctx):
    return lambda i: (jnp.maximum(i - n_ctx, 0), 0)


X_RING = 3


def _inproj_kernel(xc_hbm, xl_hbm, cc_ref, c_ref, wa_ref, ba_ref, nw_ref, wt_ref, zf_ref, zh_ref, wb_scr, mod_ref,
                   x_ring, x_sem, *, n_ctx, n_tiles, tiles_per_req, f32_rows, small_rows, bf16_rows):
    tm, D = x_ring.shape[1], x_ring.shape[2]
    n_f32 = zf_ref.shape[1]
    i = pl.program_id(0)

    def on_x_copy(t, fn):
        slot = lax.rem(t, X_RING)

        def run(src_hbm, blk):
            fn(pltpu.make_async_copy(src_hbm.at[pl.ds(blk * tm, tm), :], x_ring.at[slot], x_sem.at[slot]))

        pl.when(t < n_ctx)(lambda: run(xc_hbm, t))
        pl.when(t >= n_ctx)(lambda: run(xl_hbm, t - n_ctx))

    @pl.when(i == 0)
    def _():
        for t in range(min(X_RING - 1, n_tiles)):
            on_x_copy(jnp.int32(t), lambda cp: cp.start())

    @pl.when(i + (X_RING - 1) < n_tiles)
    def _():
        on_x_copy(i + (X_RING - 1), lambda cp: cp.start())

    @pl.when(i == 0)
    def _():
        _ada_tile(cc_ref, c_ref, wa_ref, ba_ref, mod_ref)

        def wide(rows, col):
            for r0, n in rows:
                wb_scr[col:col + n, :] = wt_ref[r0:r0 + n, :].astype(BF16)
                col += n
            return col

        col = wide(f32_rows, 0)
        parts = [wt_ref[r0:r0 + n, :] for r0, n in small_rows]
        n_small = sum(n for _, n in small_rows)
        parts.append(jnp.zeros((SMALL_W - n_small, D), F32))
        wb_scr[col:col + SMALL_W, :] = jnp.concatenate(parts, axis=0).astype(BF16)
        wide(bf16_rows, col + SMALL_W)

    _, row = _tile_group(n_ctx, tiles_per_req)
    on_x_copy(i, lambda cp: cp.wait())
    sh1 = mod_ref[pl.ds(row, 1), 0:D]
    sc1 = mod_ref[pl.ds(row, 1), D:2 * D]
    h = (_rms(x_ring[lax.rem(i, X_RING)], nw_ref[...]) * (1.0 + sc1) + sh1).astype(BF16)
    zf_ref[...] = _dot_nt(h, wb_scr[0:n_f32, :])
    zh_ref[...] = _dot_nt(h, wb_scr[n_f32:, :]).astype(BF16)


def _inproj_call(xc2d, xl2d, cc, c, w_ada, b_ada, norm_w, w_in_t, *, tm, tiles_per_req, f32_rows, small_rows,
                 bf16_rows):
    (Mc, D), Ml = xc2d.shape, xl2d.shape[0]
    n_ctx = Mc // tm
    n_f32 = sum(n for _, n in f32_rows) + SMALL_W
    n_bf16 = sum(n for _, n in bf16_rows)
    n_out = n_f32 + n_bf16
    n_mod = MOD_SPLIT * D
    n_tiles = (Mc + Ml) // tm
    kern = functools.partial(_inproj_kernel, n_ctx=n_ctx, n_tiles=n_tiles, tiles_per_req=tiles_per_req,
                             f32_rows=f32_rows, small_rows=small_rows, bf16_rows=bf16_rows)
    once = pl.Buffered(1)
    return pl.pallas_call(
        kern,
        grid=(n_tiles,),
        in_specs=[
            pl.BlockSpec(memory_space=pl.ANY),
            pl.BlockSpec(memory_space=pl.ANY),
            pl.BlockSpec(cc.shape, lambda i: (0, 0)),
            pl.BlockSpec(c.shape, lambda i: (0, 0)),
            pl.BlockSpec((D, n_mod), lambda i: (0, 0), pipeline_mode=once),
            pl.BlockSpec((1, n_mod), lambda i: (0, 0)),
            pl.BlockSpec((1, D), lambda i: (0, 0)),
            pl.BlockSpec(w_in_t.shape, lambda i: (0, 0), pipeline_mode=once),
        ],
        out_specs=[pl.BlockSpec((tm, n_f32), lambda i: (i, 0)), pl.BlockSpec((tm, n_bf16), lambda i: (i, 0))],
        out_shape=[jax.ShapeDtypeStruct((Mc + Ml, n_f32), F32), jax.ShapeDtypeStruct((Mc + Ml, n_bf16), BF16)],
        scratch_shapes=[pltpu.VMEM((n_out, D), BF16), pltpu.VMEM((COND_ROWS, n_mod), F32),
                        pltpu.VMEM((X_RING, tm, D), F32), pltpu.SemaphoreType.DMA((X_RING,))],
        compiler_params=pltpu.CompilerParams(dimension_semantics=("arbitrary",),
                                             vmem_limit_bytes=VMEM_LIMIT),
        name="norm_inproj",
    )(xc2d, xl2d, cc, c, w_ada, b_ada, norm_w.reshape(1, D), w_in_t)


def _chunk_loop(n_chunks, unroll, make_units):
    def step(ns):
        pending = list(make_units(ns))
        active = []
        while pending or active:
            for _ in range(min(PIPELINE_STARTS, len(pending))):
                active.append(pending.pop(0))
            alive = []
            for g in active:
                try:
                    next(g)
                    alive.append(g)
                except StopIteration:
                    pass
            active = alive

    if unroll >= n_chunks:
        step(list(range(n_chunks)))
        return

    def body(i, carry):
        step([i * unroll + u for u in range(unroll)])
        return carry

    lax.fori_loop(0, n_chunks // unroll, body, 0)


def _chunk_rows(n):
    if isinstance(n, int):
        return pl.ds(n * CHUNK, CHUNK)
    return pl.ds(pl.multiple_of(n * CHUNK, CHUNK), CHUNK)


def _cast_specs(casts, n_steps):
    in_specs, out_specs, out_shape, args = [], [], [], []
    for w, axis in casts:
        blk = list(w.shape)
        assert blk[axis] % n_steps == 0
        blk[axis] //= n_steps
        assert blk[0] % BF16_ROWS == 0 and blk[1] % LANES == 0
        idx = (lambda b: (b, 0)) if axis == 0 else (lambda b: (0, b))
        in_specs.append(pl.BlockSpec(tuple(blk), idx))
        out_specs.append(pl.BlockSpec(tuple(blk), idx))
        out_shape.append(jax.ShapeDtypeStruct(w.shape, BF16))
        args.append(w)
    return in_specs, out_specs, out_shape, args


def _gla_body(q_ref, k_ref, v_ref, g_ref, sm_ref, s0_ref, wal_ref, bal_ref, gw_ref, out_ref, snew_ref,
              st_scr, sall_scr, qh_scr, qs_scr, kh_scr):
    has_state = s0_ref is not None
    write_state = snew_ref is not None
    T = q_ref.shape[0]
    L = CHUNK
    N = T // L
    HK = q_ref.shape[1]
    DK = HK // H_A
    DV = v_ref.shape[1] // H_A
    scale = DK ** -0.5
    n_pairs = HK // LANES

    lower, upper = _chunk_masks(L)
    tri = (lower.astype(BF16), upper.astype(BF16))
    tmask = (lower, upper)
    lane = lax.broadcasted_iota(jnp.int32, (1, LANES), 1)
    head_mask = (lane < DK, lane >= DK)

    for d in range(2):
        for p in range(n_pairs):
            if has_state:
                st_scr[d, p] = s0_ref[d, p].T
            else:
                st_scr[d, p] = jnp.zeros((LANES, LANES), F32)

    def decay_pre(d, r):
        return _dot(sm_ref[r, :].astype(BF16), wal_ref[:, d * HK:(d + 1) * HK]) + bal_ref[d:d + 1, :]

    neg_pre = jnp.maximum(-(_dot(sm_ref[...].astype(BF16), wal_ref[...])
                            + jnp.concatenate([bal_ref[0:1, :], bal_ref[1:2, :]], axis=1)), 0.0)
    chunk_sums = jnp.sum(neg_pre.reshape(N, L, 2 * HK), axis=1)
    decay_span = (jnp.max(chunk_sums) + L * math.log(2.0)) * (1.0 / TAU_GLA)

    def state_group(ns, dirs=(0, 1)):
        units = [(d, n if d == 0 else N - 1 - n) for n in ns for d in dirs]
        rows = [_chunk_rows(n) for _, n in units]
        vt_all = [[jnp.concatenate([v_ref[r, (2 * p + j) * DV:(2 * p + j + 1) * DV] for j in range(2)],
                                   axis=0).astype(F32).T.astype(BF16) for p in range(n_pairs)] for r in rows]
        yield
        pre = [decay_pre(d, r) for (d, _), r in zip(units, rows)]
        yield
        g = [_log_sigmoid(x) * (1.0 / TAU_GLA) for x in pre]
        yield
        b = [_tri_sum(tri[d], gi, terms=2) for (d, _), gi in zip(units, g)]
        yield
        ks_all, dec_all = [], []
        for (d, _), r, bi in zip(units, rows, b):
            bend = bi[L - 1:L, :] if d == 0 else bi[0:1, :]
            q = q_ref[r, :] * scale
            ks = (k_ref[r, :] * jnp.exp(bend - bi)).astype(BF16)
            qs = q * jnp.exp(bi)
            qh_scr[d, r, :] = (qs * jnp.exp(-bend)).astype(BF16)
            qs_scr[d, r, :] = qs.astype(BF16)
            kh_scr[d, r, :] = ks
            ks_all.append(ks)
            dec_all.append(jnp.exp(bend))
        yield
        upd_all = []
        for vt_u, ks in zip(vt_all, ks_all):
            upd_u = []
            for p in range(n_pairs):
                kp = ks[:, p * LANES:(p + 1) * LANES]
                kk = jnp.concatenate([jnp.where(head_mask[j], kp, jnp.zeros_like(kp)) for j in range(2)], axis=0)
                upd_u.append(_dot(vt_u[p], kk))
            upd_all.append(upd_u)
        yield
        st = {d: [st_scr[d, p] for p in range(n_pairs)] for d in dirs}
        for (d, n), dec, upd in zip(units, dec_all, upd_all):
            for p in range(n_pairs):
                sall_scr[d, n, p] = st[d][p].astype(BF16)
                st[d][p] = st[d][p] * dec[:, p * LANES:(p + 1) * LANES] + upd[p]
        for d in dirs:
            for p in range(n_pairs):
                st_scr[d, p] = st[d][p]

    def stack_heads(x):
        return jnp.concatenate([jnp.where(head_mask[j], x, jnp.zeros_like(x)) for j in range(2)], axis=0)

    tok = lax.broadcasted_iota(jnp.int32, (L, 1), 0)
    row_t = lax.broadcasted_iota(jnp.int32, (2 * L, L), 0) & (L - 1)
    col_s = lax.broadcasted_iota(jnp.int32, (2 * L, L), 1)

    def exact_scores(d, r, p):
        ls = slice(p * LANES, (p + 1) * LANES)
        b = _tri_sum(tri[d], _log_sigmoid(decay_pre(d, r)[:, ls]) * (1.0 / TAU_GLA))
        q = q_ref[r, ls] * scale
        k = k_ref[r, ls]
        acc = jnp.where(row_t == col_s, _dot_nt(stack_heads(q).astype(BF16), k.astype(BF16)), 0.0)
        src = lax.broadcasted_iota(jnp.int32, (L, L), 1)
        h = L // 2
        while h >= 1:
            first = tok & ~(2 * h - 1)
            edge = first + (h - 1 if d == 0 else h)
            b_edge = _tri_sum((src == edge).astype(BF16), b)
            upper = (tok & (2 * h - 1)) >= h
            later, earlier = (upper, ~upper) if d == 0 else (~upper, upper)
            qt = jnp.where(later, q * jnp.exp(b - b_edge), 0.0)
            kt = jnp.where(earlier, k * jnp.exp(b_edge - b), 0.0)
            sc = _dot_nt(stack_heads(qt).astype(BF16), kt.astype(BF16))
            acc = acc + jnp.where((row_t & ~(2 * h - 1)) == (col_s & ~(2 * h - 1)), sc, 0.0)
            h //= 2
        return acc

    def out_group(ns, exact_decay=False):
        pairs = [(d, ni, p) for ni in range(len(ns)) for d in range(2) for p in range(n_pairs)]
        scores, inter = [], []
        for d, ni, p in pairs:
            r = _chunk_rows(ns[ni])
            ls = slice(p * LANES, (p + 1) * LANES)
            if exact_decay:
                scores.append(exact_scores(d, r, p))
            else:
                scores.append(_dot_nt(stack_heads(qh_scr[d, r, ls]), kh_scr[d, r, ls]))
            inter.append(_dot_nt(stack_heads(qs_scr[d, r, ls]), sall_scr[d, ns[ni], p]))
        yield
        probs = [[jnp.where(tmask[d], sc[j * L:(j + 1) * L, :], 0.0).astype(BF16) for j in range(2)]
                 for (d, _, _), sc in zip(pairs, scores)]
        yield
        outs = {}
        for (d, ni, p), pr, it in zip(pairs, probs, inter):
            r = _chunk_rows(ns[ni])
            for j in range(2):
                vs = slice((2 * p + j) * DV, (2 * p + j + 1) * DV)
                outs[(d, ni, 2 * p + j)] = _dot(pr[j], v_ref[r, vs]) + it[j * L:(j + 1) * L, :]
        yield
        for ni, n in enumerate(ns):
            r = _chunk_rows(n)
            for h in range(H_A):
                vs = slice(h * DV, (h + 1) * DV)
                o = outs[(0, ni, h)] + outs[(1, ni, h)]
                out_ref[r, vs] = (_rms(o, gw_ref[:, vs]) * _silu(g_ref[r, vs].astype(F32))).astype(out_ref.dtype)

    def finish():
        if write_state:
            for d in range(2):
                for p in range(n_pairs):
                    snew_ref[d, p] = st_scr[d, p].T

    return state_group, out_group, finish, decay_span


def _gla_scratch(T, HK):
    n_pairs = HK // LANES
    n_chunks = T // CHUNK
    return [
        pltpu.VMEM((2, n_pairs, LANES, LANES), F32),
        pltpu.VMEM((2, n_chunks, n_pairs, LANES, LANES), BF16),
        pltpu.VMEM((2, T, HK), BF16),
        pltpu.VMEM((2, T, HK), BF16),
        pltpu.VMEM((2, T, HK), BF16),
    ]


def _mlstm_body(qk_ref, v_ref, og_ref, sm_ref, c0_ref, n0_ref, m0_ref, cw_ref, bm_ref, gw_ref,
                out_ref, cnew_ref, nnew_ref, mnew_ref,
                pad_scr, qk_scr, y_scr, c_scr, n_scr, m_scr, call_scr, nall_scr, mall_scr, g_scr, f_scr,
                *, grid_w):
    has_state = c0_ref is not None
    write_state = cnew_ref is not None
    T = qk_ref.shape[0]
    L = CHUNK
    N = T // L
    C2 = qk_ref.shape[1]
    HK = C2 // 2
    DK = HK // H_B
    DV = v_ref.shape[1] // H_B
    scale = DK ** -0.5
    n_pairs = HK // LANES
    P = pad_scr.shape[0] - T
    P0 = P // 2
    rows_img = T // grid_w

    lower, upper = _chunk_masks(L)
    tri = (lower.astype(BF16), upper.astype(BF16))
    tmask = (lower, upper)
    lane = lax.broadcasted_iota(jnp.int32, (1, LANES), 1)
    head_mask = (lane < DK, lane >= DK)
    lane_in = lane & (L - 1)

    def lane_cummax(x, d):
        k = 1
        while k < L:
            if d == 0:
                x = jnp.maximum(x, jnp.where(lane_in >= k, pltpu.roll(x, k, axis=1), -jnp.inf))
            else:
                x = jnp.maximum(x, jnp.where(lane_in < L - k, pltpu.roll(x, LANES - k, axis=1), -jnp.inf))
            k *= 2
        return x

    for d in range(2):
        for p in range(n_pairs):
            if has_state:
                c_scr[d, p] = c0_ref[d, p]
                n_scr[2 * d + p:2 * d + p + 1, :] = jnp.concatenate(
                    [n0_ref[d, 2 * p + j:2 * p + j + 1, :] for j in range(2)], axis=1)
            else:
                c_scr[d, p] = jnp.zeros((LANES, LANES), F32)
                n_scr[2 * d + p:2 * d + p + 1, :] = jnp.zeros((1, LANES), F32)
    eye_h = (lax.broadcasted_iota(jnp.int32, (H_B, H_B), 0) == lax.broadcasted_iota(jnp.int32, (H_B, H_B), 1))

    def to_col(row):
        return jnp.sum(jnp.where(eye_h, row, 0.0), axis=1, keepdims=True)

    def to_row(col):
        return jnp.sum(jnp.where(eye_h, col, 0.0), axis=0, keepdims=True)

    for d in range(2):
        if has_state:
            m_scr[H_B * d:H_B * (d + 1), 0:1] = to_col(m0_ref[d:d + 1, :])
        else:
            m_scr[H_B * d:H_B * (d + 1), 0:1] = jnp.zeros((H_B, 1), F32)

    pad_scr[0:P0, :] = jnp.zeros((P0, C2), F32)
    pad_scr[P0 + T:P + T, :] = jnp.zeros((P - P0, C2), F32)

    def copy_in(i, carry):
        r0 = pl.multiple_of(i * L, L)
        pad_scr[pl.ds(P0 + r0, L), :] = qk_ref[pl.ds(r0, L), :]
        return carry

    lax.fori_loop(0, N, copy_in, 0)

    lane_c = lax.broadcasted_iota(jnp.int32, (1, C2), 1)
    qscale = jnp.where(lane_c < HK, scale, 1.0).astype(F32)
    sub = lax.broadcasted_iota(jnp.int32, (L, 1), 0)
    img_rows = (0,) if rows_img == 1 else (-1, 0, 1)

    def conv_tile(i, carry):
        r0 = pl.multiple_of(i * L, L)
        col = lax.rem(r0, grid_w) + sub
        ok_left = col >= 1
        ok_right = col <= grid_w - 2
        sums = [None, None, None]
        for di in img_rows:
            blk = pad_scr[pl.ds(P0 + r0 + di * grid_w - SUBLANES, L + 2 * SUBLANES), :]
            for k in range(3):
                term = blk * cw_ref[di + 1, k:k + 1, :]
                sums[k] = term if sums[k] is None else sums[k] + term
        S = SUBLANES
        acc = (sums[1][S:S + L, :] + jnp.where(ok_left, sums[0][S - 1:S - 1 + L, :], 0.0)
               + jnp.where(ok_right, sums[2][S + 1:S + 1 + L, :], 0.0))
        qk_scr[pl.ds(r0, L), :] = _silu(acc) * qscale
        return carry

    lax.fori_loop(0, N, conv_tile, 0)

    gl = lane - GATE_LANE0
    is_f = ((gl >= H_B) & (gl < 2 * H_B)) | ((gl >= 3 * H_B) & (gl < 4 * H_B))

    def gate_tile(i, carry):
        rows = pl.ds(pl.multiple_of(i * L, L), L)
        x = sm_ref[rows, :] + bm_ref[...]
        y_scr[rows, :] = jnp.where(is_f, _log_sigmoid(x), x)
        return carry

    lax.fori_loop(0, N, gate_tile, 0)


    def state_group(ns, dirs=(0, 1)):
        units = [(d, n if d == 0 else N - 1 - n) for n in ns for d in dirs]
        rows = [_chunk_rows(n) for _, n in units]
        kt_all = [[qk_scr[r, HK + p * LANES:HK + (p + 1) * LANES].T for p in range(n_pairs)] for r in rows]
        yield
        xs = [y_scr[r, :] for r in rows]
        fsum = [_tri_sum(tri[d], x) for (d, _), x in zip(units, xs)]
        yield
        wk_all, f_end, c_end = [], [], []
        for (d, n), r, x, fs in zip(units, rows, xs, fsum):
            y = jnp.where(is_f, fs, x)
            li0 = GATE_LANE0 + 2 * H_B * d
            blk = jnp.concatenate([y, y], axis=0).T[li0:li0 + 2 * H_B, :]
            frow = pltpu.roll(blk, H_B, axis=0)
            grow = blk - frow
            g_scr[d, n] = grow
            f_scr[d, n] = frow
            e_col = L - 1 if d == 0 else 0
            f_end.append(frow[0:H_B, e_col:e_col + 1])
            ce8 = jnp.max(grow, axis=1, keepdims=True)
            c_end.append(ce8[0:H_B, :])
            wk_all.append(jnp.exp(grow[:, 0:L] - ce8))
        yield
        kv_all, ksum_all = [], []
        for r, wk8, kt_u in zip(rows, wk_all, kt_all):
            kv_u, ks_u = [], []
            wk8b = wk8.astype(BF16)
            for p in range(n_pairs):
                kpb = qk_scr[r, HK + p * LANES:HK + (p + 1) * LANES].astype(BF16)
                ks8 = _dot(wk8b, kpb)
                for j in range(2):
                    h = 2 * p + j
                    kwt = (kt_u[p][j * DK:(j + 1) * DK, :] * wk8[h:h + 1, :]).astype(BF16)
                    kv_u.append(_dot(kwt, v_ref[r, h * DV:(h + 1) * DV]))
                    ks_u.append(ks8[h:h + 1, :])
            kv_all.append(kv_u)
            ksum_all.append(ks_u)
        yield
        m_run = {d: m_scr[H_B * d:H_B * (d + 1), 0:1] for d in dirs}
        a_all, b_all = [], []
        for (d, n), fe, ce in zip(units, f_end, c_end):
            mall_scr[d, n, 0:H_B, 0:1] = m_run[d]
            mx = jnp.maximum(m_run[d], ce)
            a_all.append(jnp.exp(m_run[d] - mx))
            b_all.append(jnp.exp(ce - mx))
            m_run[d] = fe + mx
        for d in dirs:
            m_scr[H_B * d:H_B * (d + 1), 0:1] = m_run[d]
        yield
        c_run = {d: [[c_scr[d, p, j * DK:(j + 1) * DK, :] for j in range(2)] for p in range(n_pairs)] for d in dirs}
        n_run = {d: [n_scr[2 * d + p:2 * d + p + 1, :] for p in range(n_pairs)] for d in dirs}
        for (d, n), a4, b4, kv_u, ks_u in zip(units, a_all, b_all, kv_all, ksum_all):
            for p in range(n_pairs):
                nall_scr[d, n, p:p + 1, :] = n_run[d][p]
                a_s = [a4[2 * p + j:2 * p + j + 1, :] for j in range(2)]
                b_s = [b4[2 * p + j:2 * p + j + 1, :] for j in range(2)]
                for j in range(2):
                    cj = c_run[d][p][j]
                    call_scr[d, n, p, j * DK:(j + 1) * DK, :] = cj.astype(BF16)
                    c_run[d][p][j] = a_s[j] * cj + b_s[j] * kv_u[2 * p + j]
                n_run[d][p] = (jnp.where(head_mask[0], a_s[0], a_s[1]) * n_run[d][p]
                               + jnp.where(head_mask[0], b_s[0] * ks_u[2 * p], b_s[1] * ks_u[2 * p + 1]))
        for d in dirs:
            for p in range(n_pairs):
                n_scr[2 * d + p:2 * d + p + 1, :] = n_run[d][p]
                for j in range(2):
                    c_scr[d, p, j * DK:(j + 1) * DK, :] = c_run[d][p][j]

    eye = lower & upper
    ones8 = jnp.ones((SUBLANES, L), BF16)
    sub8 = lax.broadcasted_iota(jnp.int32, (SUBLANES, LANES), 0)
    sub_h = lax.broadcasted_iota(jnp.int32, (H_B, L), 0)
    n_rows = [((sub8 == 2 * p) & head_mask[0]) | ((sub8 == 2 * p + 1) & head_mask[1]) for p in range(n_pairs)]

    def head_rows(vals):
        out = vals[0][0:H_B, :]
        for h in range(1, H_B):
            out = jnp.where(sub_h == h, vals[h][0:H_B, :], out)
        return out

    def out_group(ns):
        chunks = [(d, n) for n in ns for d in range(2)]
        pairs = [(d, n, p) for d, n in chunks for p in range(n_pairs)]
        units = [(d, n, p, j) for d, n, p in pairs for j in range(2)]
        cms = [lane_cummax(g_scr[d, n], d)[0:H_B, 0:L] for d, n in chunks]
        qk2s, qc2s, qn2s = [], [], []
        for d, n, p in pairs:
            r = _chunk_rows(n)
            qp = qk_scr[r, p * LANES:(p + 1) * LANES]
            q2 = jnp.concatenate([jnp.where(head_mask[j], qp, 0.0) for j in range(2)], axis=0).astype(BF16)
            qk2s.append(_dot_nt(q2, qk_scr[r, HK + p * LANES:HK + (p + 1) * LANES].astype(BF16)))
            qc2s.append(_dot(q2, call_scr[d, n, p]))
            nsel = jnp.where(n_rows[p], nall_scr[d, n, p:p + 1, :], 0.0).astype(BF16)
            qn2s.append(_dot_nt(nsel, qp.astype(BF16)))
        yield
        s_all = []
        for ui, (d, n, p, j) in enumerate(units):
            grow = g_scr[d, n, 2 * p + j:2 * p + j + 1, 0:L]
            e = jnp.where(tmask[d], grow, -jnp.inf)
            cmax = jnp.max(e, axis=-1, keepdims=True)
            s_all.append((qk2s[ui // 2][j * L:(j + 1) * L, :] * jnp.exp(e - cmax)).astype(BF16))
        yield
        nums = [_dot(s, v_ref[_chunk_rows(n), (2 * p + j) * DV:(2 * p + j + 1) * DV])
                for (d, n, p, j), s in zip(units, s_all)]
        dens = [_dot_nt(ones8, s) for s in s_all]
        yield
        scales = []
        for ci, (d, n) in enumerate(chunks):
            den_loc = head_rows(dens[ci * H_B:(ci + 1) * H_B])
            qn = qn2s[ci * n_pairs][0:H_B, :]
            for p in range(1, n_pairs):
                qn = qn + qn2s[ci * n_pairs + p][0:H_B, :]
            cm = cms[ci]
            m_prev = mall_scr[d, n, 0:H_B, 0:1]
            delta = cm - m_prev
            t = jnp.exp(-jnp.abs(delta))
            w_loc = jnp.where(delta <= 0.0, t, 1.0)
            w_inter = jnp.where(delta <= 0.0, 1.0, t)
            mt = f_scr[d, n, 0:H_B, 0:L] + jnp.maximum(m_prev, cm)
            den = w_loc * den_loc + w_inter * qn
            rinv = 1.0 / jnp.maximum(jnp.abs(den), jnp.exp(-mt))
            scales.append((w_loc * rinv, w_inter * rinv))
        yield
        hs = []
        for ui, (d, n, p, j) in enumerate(units):
            h = 2 * p + j
            sc_loc, sc_inter = scales[ui // H_B]
            d_loc = jnp.where(eye, sc_loc[h:h + 1, :], 0.0).astype(BF16)
            d_inter = jnp.where(eye, sc_inter[h:h + 1, :], 0.0).astype(BF16)
            hs.append(_dot(d_loc, nums[ui].astype(BF16))
                      + _dot(d_inter, qc2s[ui // 2][j * L:(j + 1) * L, :].astype(BF16)))
        yield
        for ni, n in enumerate(ns):
            r = _chunk_rows(n)
            for h in range(H_B):
                vs = slice(h * DV, (h + 1) * DV)
                o = hs[(2 * ni) * H_B + h] + hs[(2 * ni + 1) * H_B + h]
                out_ref[r, vs] = (_rms(o, gw_ref[:, vs]) * _sigmoid(og_ref[r, vs].astype(F32))).astype(out_ref.dtype)

    def finish():
        if write_state:
            for d in range(2):
                for p in range(n_pairs):
                    cnew_ref[d, p] = c_scr[d, p]
                    for j in range(2):
                        nnew_ref[d, 2 * p + j:2 * p + j + 1, :] = n_scr[2 * d + p:2 * d + p + 1, j * DK:(j + 1) * DK]
                mnew_ref[d:d + 1, :] = to_row(m_scr[H_B * d:H_B * (d + 1), 0:1])

    return state_group, out_group, finish


def _mlstm_scratch(T, C2, grid_w):
    n_pairs = C2 // 2 // LANES
    n_chunks = T // CHUNK
    pad_rows = 2 * (grid_w + SUBLANES) if T // grid_w > 1 else 2 * SUBLANES
    return [
        pltpu.VMEM((T + pad_rows, C2), F32),
        pltpu.VMEM((T, C2), F32),
        pltpu.VMEM((T, SMALL_W), F32),
        pltpu.VMEM((2, n_pairs, LANES, LANES), F32),
        pltpu.VMEM((SUBLANES, LANES), F32),
        pltpu.VMEM((SUBLANES, LANES), F32),
        pltpu.VMEM((2, n_chunks, n_pairs, LANES, LANES), BF16),
        pltpu.VMEM((2, n_chunks, SUBLANES, LANES), F32),
        pltpu.VMEM((2, n_chunks, SUBLANES, LANES), F32),
        pltpu.VMEM((2, n_chunks, SUBLANES, LANES), F32),
        pltpu.VMEM((2, n_chunks, SUBLANES, LANES), F32),
    ]


N_GLA_SCRATCH = 5
N_MLSTM_SCRATCH = 11


def _scan_kernel(*refs, cols, layer, has_state, write_state, n_cast, ride_ada, grid_w, unroll):
    refs = list(refs)
    z_refs = refs[:2]
    del refs[:2]
    s0_ref = c0_ref = n0_ref = m0_ref = None
    if has_state:
        s0_ref, c0_ref, n0_ref, m0_ref = refs[:4]
        del refs[:4]
    wa_ref, bal_ref, gwa_ref, cw_ref, bmg_ref, gwb_ref = refs[:6]
    del refs[:6]
    cast_in = refs[:n_cast]
    del refs[:n_cast]
    if ride_ada:
        ada_in = refs[:4]
        del refs[:4]
    outa_ref, outb_ref = refs[:2]
    del refs[:2]
    snew_ref = cnew_ref = nnew_ref = mnew_ref = None
    if write_state:
        snew_ref, cnew_ref, nnew_ref, mnew_ref = refs[:4]
        del refs[:4]
    cast_out = refs[:n_cast]
    del refs[:n_cast]
    if ride_ada:
        ada_out = refs.pop(0)
    wal_scr, bm_scr = refs[:2]
    del refs[:2]
    gla_scr = refs[:N_GLA_SCRATCH]
    mlstm_scr = refs[N_GLA_SCRATCH:]

    for src, dst in zip(cast_in, cast_out):
        dst[...] = src[...].astype(BF16)
    if ride_ada:
        _ada_tile(*ada_in, ada_out)

    R, HK = wa_ref.shape[1], wa_ref.shape[2]
    wal_scr[...] = jnp.zeros(wal_scr.shape, BF16)
    for d in range(2):
        wal_scr[d * R:(d + 1) * R, d * HK:(d + 1) * HK] = wa_ref[d].astype(BF16)
    lane = lax.broadcasted_iota(jnp.int32, (1, LANES), 1)
    bm = jnp.zeros((1, LANES), F32)
    for g in range(bmg_ref.shape[1]):
        for h in range(H_B):
            bm = jnp.where(lane == GATE_LANE0 + H_B * g + h, bmg_ref[layer, g, h], bm)
    bm_scr[0:1, :] = bm

    def view(name):
        a, c0, w = cols[name]
        return z_refs[a].at[:, pl.ds(c0, w)]

    sm_ref = view("small")
    n_chunks = z_refs[0].shape[0] // CHUNK
    gla = _gla_body(view("qa"), view("ka"), view("va"), view("ga"), sm_ref, s0_ref, wal_scr, bal_ref, gwa_ref,
                    outa_ref, snew_ref, *gla_scr)
    mlstm = _mlstm_body(view("qkb"), view("vb"), view("ob"), sm_ref, c0_ref, n0_ref, m0_ref, cw_ref,
                        bm_scr.at[0:1, :], gwb_ref, outb_ref, cnew_ref, nnew_ref, mnew_ref, *mlstm_scr,
                        grid_w=grid_w)
    gla_state, gla_out, gla_finish, decay_span = gla
    mlstm_state, mlstm_out, mlstm_finish = mlstm

    def passes(gla_out_fn):
        _chunk_loop(n_chunks, unroll, lambda ns: [fn([n], (d,)) for n in ns for d in range(2)
                                                  for fn in (mlstm_state, gla_state)])
        _chunk_loop(n_chunks, unroll, lambda ns: [fn([n]) for n in ns for fn in (mlstm_out, gla_out_fn)])

    wide_decay = decay_span > GLA_FACTORED_DECAY_MAX

    @pl.when(jnp.logical_not(wide_decay))
    def _():
        passes(gla_out)

    @pl.when(wide_decay)
    def _():
        passes(functools.partial(gla_out, exact_decay=True))

    gla_finish()
    mlstm_finish()


def _scan_call(z2d, row0, B, T, states, lw, layer, *, grid_w, write_state, casts=(), ada=None):
    assert row0 % T == 0 and all(z.shape[0] % T == 0 for z in z2d)
    z3 = [z.reshape(z.shape[0] // T, T, z.shape[1]) for z in z2d]
    blk0 = row0 // T
    HK = lw["w_alpha2"].shape[-1]
    DA = lw["gnorm_a_w"].shape[0]
    C2 = lw["conv_w"].shape[-1]
    DB = lw["gnorm_b_w"].shape[0]
    DK_A, DK_B = HK // H_A, C2 // 2 // H_B
    pa, pb = HK // LANES, C2 // 2 // LANES
    n_chunks = T // CHUNK
    has_state = states is not None
    widths = ((("qa", HK), ("ka", HK), ("qkb", C2), ("small", SMALL_W)),
              (("va", DA), ("ga", DA), ("vb", DB), ("ob", DB)))
    cols = {}
    for a, groups in enumerate(widths):
        c0 = 0
        for name, w in groups:
            cols[name] = (a, c0, w)
            c0 += w
        assert c0 == z3[a].shape[2]
    cast_in_specs, cast_out_specs, cast_out_shape, cast_args = _cast_specs(casts, B)
    kern = functools.partial(_scan_kernel, cols=cols, layer=layer, has_state=has_state, write_state=write_state,
                             n_cast=len(casts), ride_ada=ada is not None, grid_w=grid_w,
                             unroll=min(n_chunks, SCAN_UNROLL))

    def per_batch(shape):
        nd = len(shape)
        return pl.BlockSpec((None,) + tuple(shape), lambda b: (b,) + (0,) * nd)

    def per_batch_layer(shape):
        nd = len(shape)
        return pl.BlockSpec((None, None) + tuple(shape), lambda b: (b, layer) + (0,) * nd)

    def of_layer(a):
        return pl.BlockSpec((None,) + a.shape[1:], lambda b: (layer,) + (0,) * (a.ndim - 1))

    def whole(a):
        return pl.BlockSpec(a.shape, lambda b: (0,) * a.ndim)

    state_shapes = ((2, pa, LANES, LANES), (2, pb, LANES, LANES), (2, H_B, DK_B), (2, H_B))
    in_specs = [pl.BlockSpec((None, T, z.shape[2]), lambda b: (b + blk0, 0, 0)) for z in z3]
    args = list(z3)
    if has_state:
        s_gla, s_c, s_n, s_m = states
        depth = s_gla.shape[1]
        args += [s_gla.reshape((B, depth) + state_shapes[0]), s_c.reshape((B, depth) + state_shapes[1]), s_n, s_m]
        in_specs += [per_batch_layer(s) for s in state_shapes]
    args += [lw["w_alpha2"], lw["b_alpha"], lw["gnorm_a_w"].reshape(1, DA), lw["conv_w"], lw["b_mgate"],
             lw["gnorm_b_w"].reshape(1, DB)]
    in_specs += [of_layer(lw["w_alpha2"]), of_layer(lw["b_alpha"]), pl.BlockSpec((1, DA), lambda b: (0, 0)),
                 whole(lw["conv_w"]), pl.BlockSpec(memory_space=pltpu.SMEM), pl.BlockSpec((1, DB), lambda b: (0, 0))]
    args += cast_args
    in_specs += cast_in_specs
    out_specs = [per_batch((T, DA)), per_batch((T, DB))]
    out_shape = [jax.ShapeDtypeStruct((B, T, DA), BF16), jax.ShapeDtypeStruct((B, T, DB), BF16)]
    if write_state:
        out_specs += [per_batch(s) for s in state_shapes]
        out_shape += [jax.ShapeDtypeStruct((B,) + s, F32) for s in state_shapes]
    out_specs += cast_out_specs
    out_shape += cast_out_shape
    if ada is not None:
        cc, c, w_ada, b_ada, col0 = ada
        n_rest = w_ada.shape[1] - col0
        wcol = n_rest // B
        assert n_rest % B == 0 and wcol % LANES == 0 and col0 % wcol == 0
        args += [cc, c, w_ada, b_ada]
        in_specs += [whole(cc), whole(c),
                     pl.BlockSpec((w_ada.shape[0], wcol), lambda b: (0, col0 // wcol + b)),
                     pl.BlockSpec((1, wcol), lambda b: (0, col0 // wcol + b))]
        out_specs.append(pl.BlockSpec((COND_ROWS, wcol), lambda b: (0, b)))
        out_shape.append(jax.ShapeDtypeStruct((COND_ROWS, n_rest), F32))
    scratch = ([pltpu.VMEM((SMALL_W, 2 * HK), BF16), pltpu.VMEM((SUBLANES, LANES), F32)]
               + _gla_scratch(T, HK) + _mlstm_scratch(T, C2, grid_w))
    assert len(scratch) == 2 + N_GLA_SCRATCH + N_MLSTM_SCRATCH
    return pl.pallas_call(
        kern,
        grid=(B,),
        in_specs=in_specs,
        out_specs=out_specs,
        out_shape=out_shape,
        scratch_shapes=scratch,
        compiler_params=pltpu.CompilerParams(dimension_semantics=("arbitrary",),
                                             vmem_limit_bytes=VMEM_LIMIT),
        name="mixer_scans",
    )(*args)


def _outff_kernel(xc_ref, xl_ref, ac_ref, al_ref, bc_ref, bl_ref, mod_ref, n2_ref, fn_ref, wo_ref, w1_ref, w2_ref,
                  yc_ref, yl_ref, *, n_ctx, tiles_per_req, final_norm):
    D = xc_ref.shape[1]
    DA = ac_ref.shape[1]
    is_ctx, row = _tile_group(n_ctx, tiles_per_req)

    def mod(k):
        return mod_ref[pl.ds(row, 1), (k - MOD_SPLIT) * D:(k - MOD_SPLIT + 1) * D]

    def tile(x_ref, a_ref, b_ref, y_ref):
        y = _dot(a_ref[...], wo_ref[0:DA, :]) + _dot(b_ref[...], wo_ref[DA:, :])
        x1 = x_ref[...] + mod(2) * y
        h2 = (_rms(x1, n2_ref[...]) * (1.0 + mod(4)) + mod(3)).astype(BF16)
        u = jnp.maximum(_dot(h2, w1_ref[...]), 0.0)
        x2 = x1 + mod(5) * _dot((u * u).astype(BF16), w2_ref[...])
        y_ref[...] = _rms(x2, fn_ref[...]) if final_norm else x2

    @pl.when(is_ctx)
    def _():
        tile(xc_ref, ac_ref, bc_ref, yc_ref)

    @pl.when(jnp.logical_not(is_ctx))
    def _():
        tile(xl_ref, al_ref, bl_ref, yl_ref)


def _outff_call(xc2d, xl2d, ac, al, bc, bl, mod, norm2_w, final_w, wo, w1, w2, *, tm, tiles_per_req, final_norm):
    (Mc, D), Ml = xc2d.shape, xl2d.shape[0]
    n_ctx = Mc // tm
    DA = ac.shape[1]
    DFF = w1.shape[1]
    kern = functools.partial(_outff_kernel, n_ctx=n_ctx, tiles_per_req=tiles_per_req, final_norm=final_norm)
    once = pl.Buffered(1)
    ctx, lat = _ctx_tile(n_ctx), _lat_tile(n_ctx)
    return pl.pallas_call(
        kern,
        grid=((Mc + Ml) // tm,),
        in_specs=[
            pl.BlockSpec((tm, D), ctx), pl.BlockSpec((tm, D), lat),
            pl.BlockSpec((tm, DA), ctx), pl.BlockSpec((tm, DA), lat),
            pl.BlockSpec((tm, D - DA), ctx), pl.BlockSpec((tm, D - DA), lat),
            pl.BlockSpec(mod.shape, lambda i: (0, 0)),
            pl.BlockSpec((1, D), lambda i: (0, 0)),
            pl.BlockSpec((1, D), lambda i: (0, 0)),
            pl.BlockSpec((D, D), lambda i: (0, 0), pipeline_mode=once),
            pl.BlockSpec((D, DFF), lambda i: (0, 0), pipeline_mode=once),
            pl.BlockSpec((DFF, D), lambda i: (0, 0), pipeline_mode=once),
        ],
        out_specs=[pl.BlockSpec((tm, D), ctx), pl.BlockSpec((tm, D), lat)],
        out_shape=[jax.ShapeDtypeStruct((Mc, D), F32), jax.ShapeDtypeStruct((Ml, D), F32)],
        compiler_params=pltpu.CompilerParams(dimension_semantics=("arbitrary",),
                                             vmem_limit_bytes=VMEM_LIMIT),
        name="outproj_mlp",
    )(xc2d, xl2d, ac, al, bc, bl, mod, norm2_w.reshape(1, D), final_w.reshape(1, D), wo, w1, w2)


def _layer(xc, xl, cond, ada_w, cached, lw, layer, ffw, final_w, final_norm):
    (Bc, Tc, D), (Bl, Tl, _) = xc.shape, xl.shape
    tm = TOKEN_TILE
    assert (Bc * Tc) % tm == 0 and Tl % tm == 0 and (Bc * Tc) % Tl == 0
    xc2d, xl2d = xc.reshape(Bc * Tc, D), xl.reshape(Bl * Tl, D)
    z = _inproj_call(xc2d, xl2d, *cond, *ada_w, lw["norm1_w"], lw["w_in_t"], tm=tm, tiles_per_req=Tl // tm,
                     f32_rows=lw["f32_rows"], small_rows=lw["small_rows"], bf16_rows=lw["bf16_rows"])
    res_c = _scan_call(z, 0, Bc, Tc, None, lw, layer, grid_w=Tc, write_state=True,
                       casts=((ffw[0], 0), (ffw[1], 0), (ffw[2], 0)), ada=(*cond, *ada_w, MOD_SPLIT * D))
    res_l = _scan_call(z, Bc * Tc, Bl, Tl, cached, lw, layer, grid_w=GRID_W, write_state=False)
    wo_b, w1_b, w2_b, mod_out = res_c[-4:]
    yc, yl = _outff_call(xc2d, xl2d, res_c[0].reshape(Bc * Tc, -1), res_l[0].reshape(Bl * Tl, -1),
                         res_c[1].reshape(Bc * Tc, -1), res_l[1].reshape(Bl * Tl, -1), mod_out, lw["norm2_w"],
                         final_w, wo_b, w1_b, w2_b, tm=tm, tiles_per_req=Tl // tm, final_norm=final_norm)
    return yc.reshape(Bc, Tc, D), yl.reshape(Bl, Tl, D), tuple(res_c[2:6])


def _layer_weights(l, norm1_w, norm2_w, w_in, w_alpha2, b_alpha, b_mgate, conv_w, gnorm_a_w, gnorm_b_w):
    hk_a = w_alpha2.shape[-1]
    d_a = gnorm_a_w.shape[-1]
    d_b = gnorm_b_w.shape[-1]
    hk_b = conv_w.shape[-1] // 2
    sizes = (hk_a, hk_a, d_a, d_a, 2 * R_ALPHA, hk_b, hk_b, d_b, d_b, 4 * H_B)
    assert w_alpha2.shape[2] == R_ALPHA and b_mgate.shape[1] * b_mgate.shape[2] == 4 * H_B
    offs = [0]
    for s in sizes:
        offs.append(offs[-1] + s)
    f32_rows = ((offs[0], offs[2] - offs[0]), (offs[5], offs[7] - offs[5]))
    small_rows = ((offs[4], offs[5] - offs[4]), (offs[9], offs[10] - offs[9]))
    bf16_rows = ((offs[2], offs[4] - offs[2]), (offs[7], offs[9] - offs[7]))
    assert all(n % LANES == 0 and r % BF16_ROWS == 0 for r, n in f32_rows + bf16_rows)
    return dict(
        norm1_w=norm1_w[l], norm2_w=norm2_w[l], w_in_t=jnp.swapaxes(w_in[l], 0, 1),
        f32_rows=f32_rows, small_rows=small_rows, bf16_rows=bf16_rows,
        w_alpha2=w_alpha2, b_alpha=b_alpha, b_mgate=b_mgate, conv_w=conv_w[l],
        gnorm_a_w=gnorm_a_w[l], gnorm_b_w=gnorm_b_w[l],
    )


def kernel(x_prompt, x_sample, c, state_gla, state_mlstm_C, state_mlstm_n, state_mlstm_m, c_ctx, w_ada, b_ada, norm1_w, norm2_w, w_in, w_alpha2, b_alpha, b_mgate, conv_w, gnorm_a_w, gnorm_b_w, w_out, w_ff1, w_ff2, final_norm_w):
    depth = w_in.shape[0]
    D = x_prompt.shape[-1]
    Bp, Tp, _ = x_prompt.shape
    Bs = x_sample.shape[0]
    assert 1 + Bs <= COND_ROWS
    cond = (c_ctx.reshape(1, D), c)
    cached = (state_gla, state_mlstm_C, state_mlstm_n, state_mlstm_m)
    xp, xs = x_prompt, x_sample
    s_gla, s_c, s_n, s_m = [], [], [], []
    for l in range(depth):
        lw = _layer_weights(l, norm1_w, norm2_w, w_in, w_alpha2, b_alpha, b_mgate, conv_w,
                            gnorm_a_w, gnorm_b_w)
        xp, xs, ctx = _layer(xp, xs, cond, (w_ada[l], b_ada[l].reshape(1, -1)), cached, lw, l,
                             (w_out[l], w_ff1[l], w_ff2[l]), final_norm_w, l == depth - 1)
        s_gla.append(ctx[0].reshape(Bp, 2, H_A, -1, ctx[0].shape[-1]))
        s_c.append(ctx[1].reshape(Bp, 2, H_B, -1, ctx[1].shape[-1]))
        s_n.append(ctx[2])
        s_m.append(ctx[3])
    dt = x_prompt.dtype
    return (xp, xs, jnp.stack(s_gla, axis=1).astype(dt), jnp.stack(s_c, axis=1).astype(dt),
            jnp.stack(s_n, axis=1).astype(dt), jnp.stack(s_m, axis=1).astype(dt))
```

```python
import functools
import math

import jax
import jax.numpy as jnp
from jax import lax
from jax.experimental import pallas as pl
from jax.experimental.pallas import tpu as pltpu

F32 = jnp.float32
BF16 = jnp.bfloat16

GRID_W = 64
H_A = 4
H_B = 4
R_ALPHA = 16
TAU_GLA = 16.0
CHUNK = 64
EPS = 1e-6
LANES = 128
SUBLANES = 8
BF16_ROWS = 16
COND_ROWS = SUBLANES
SMALL_W = LANES
GATE_LANE0 = 2 * R_ALPHA
VMEM_LIMIT = 56 * 1024 * 1024
SCAN_UNROLL = 4
PIPELINE_STARTS = 8
TOKEN_TILE = 512
GLA_FACTORED_DECAY_MAX = 60.0
MOD_SPLIT = 2


def _sigmoid(x):
    return 1.0 / (1.0 + jnp.exp(-x))


def _silu(x):
    return x * _sigmoid(x)


def _log_sigmoid(x):
    return jnp.minimum(x, 0.0) - jnp.log(1.0 + jnp.exp(-jnp.abs(x)))


def _dot(a, b):
    return jnp.dot(a, b, preferred_element_type=F32)


def _dot_nt(a, b):
    return lax.dot_general(a, b, (((1,), (1,)), ((), ())), preferred_element_type=F32)


def _rms(x, w):
    return x * lax.rsqrt(jnp.mean(x * x, axis=-1, keepdims=True) + EPS) * w


def _tri_sum(tri, x, terms=3):
    acc, rest = None, x
    for t in range(terms):
        part = rest.astype(BF16)
        prod = _dot(tri, part)
        acc = prod if acc is None else acc + prod
        if t + 1 < terms:
            rest = rest - part.astype(F32)
    return acc


def _chunk_masks(L):
    row = lax.broadcasted_iota(jnp.int32, (L, L), 0)
    col = lax.broadcasted_iota(jnp.int32, (L, L), 1)
    lower = row >= col
    upper = row <= col
    return lower, upper


def _ada_tile(cc_ref, c_ref, w_ref, b_ref, o_ref):
    D = cc_ref.shape[1]
    sub = lax.broadcasted_iota(jnp.int32, (COND_ROWS, D), 0)
    cond = jnp.where(sub == 0, cc_ref[...], 0.0)
    for r in range(c_ref.shape[0]):
        cond = jnp.where(sub == 1 + r, c_ref[r:r + 1, :], cond)
    o_ref[...] = _dot(_silu(cond).astype(BF16), w_ref[...].astype(BF16)) + b_ref[...]


def _tile_group(n_ctx, tiles_per_req):
    i = pl.program_id(0)
    is_ctx = i < n_ctx
    row = jnp.where(is_ctx, 0, 1 + jnp.maximum(i - n_ctx, 0) // tiles_per_req)
    return is_ctx, row


def _ctx_tile(n_ctx):
    return lambda i: (jnp.minimum(i, n_ctx - 1), 0)


def _lat_tile(n_ctx):
    return lambda i: (jnp.maximum(i - n_ctx, 0), 0)


X_RING = 4


def _inproj_kernel(xc_hbm, xl_hbm, cc_ref, c_ref, wa_ref, ba_ref, nw_ref, wt_ref, zf_ref, zh_ref, wb_scr, mod_ref,
                   x_ring, x_sem, *, n_ctx, n_tiles, tiles_per_req, f32_rows, small_rows, bf16_rows):
    tm, D = x_ring.shape[1], x_ring.shape[2]
    n_f32 = zf_ref.shape[1]
    i = pl.program_id(0)

    def on_x_copy(t, fn):
        slot = lax.rem(t, X_RING)

        def run(src_hbm, blk):
            fn(pltpu.make_async_copy(src_hbm.at[pl.ds(blk * tm, tm), :], x_ring.at[slot], x_sem.at[slot]))

        pl.when(t < n_ctx)(lambda: run(xc_hbm, t))
        pl.when(t >= n_ctx)(lambda: run(xl_hbm, t - n_ctx))

    @pl.when(i == 0)
    def _():
        for t in range(min(X_RING - 1, n_tiles)):
            on_x_copy(jnp.int32(t), lambda cp: cp.start())

    @pl.when(i + (X_RING - 1) < n_tiles)
    def _():
        on_x_copy(i + (X_RING - 1), lambda cp: cp.start())

    @pl.when(i == 0)
    def _():
        _ada_tile(cc_ref, c_ref, wa_ref, ba_ref, mod_ref)

        def wide(rows, col):
            for r0, n in rows:
                wb_scr[col:col + n, :] = wt_ref[r0:r0 + n, :].astype(BF16)
                col += n
            return col

        col = wide(f32_rows, 0)
        parts = [wt_ref[r0:r0 + n, :] for r0, n in small_rows]
        n_small = sum(n for _, n in small_rows)
        parts.append(jnp.zeros((SMALL_W - n_small, D), F32))
        wb_scr[col:col + SMALL_W, :] = jnp.concatenate(parts, axis=0).astype(BF16)
        wide(bf16_rows, col + SMALL_W)

    _, row = _tile_group(n_ctx, tiles_per_req)
    on_x_copy(i, lambda cp: cp.wait())
    sh1 = mod_ref[pl.ds(row, 1), 0:D]
    sc1 = mod_ref[pl.ds(row, 1), D:2 * D]
    h = (_rms(x_ring[lax.rem(i, X_RING)], nw_ref[...]) * (1.0 + sc1) + sh1).astype(BF16)
    zf_ref[...] = _dot_nt(h, wb_scr[0:n_f32, :])
    zh_ref[...] = _dot_nt(h, wb_scr[n_f32:, :]).astype(BF16)


def _inproj_call(xc2d, xl2d, cc, c, w_ada, b_ada, norm_w, w_in_t, *, tm, tiles_per_req, f32_rows, small_rows,
                 bf16_rows):
    (Mc, D), Ml = xc2d.shape, xl2d.shape[0]
    n_ctx = Mc // tm
    n_f32 = sum(n for _, n in f32_rows) + SMALL_W
    n_bf16 = sum(n for _, n in bf16_rows)
    n_out = n_f32 + n_bf16
    n_mod = MOD_SPLIT * D
    n_tiles = (Mc + Ml) // tm
    kern = functools.partial(_inproj_kernel, n_ctx=n_ctx, n_tiles=n_tiles, tiles_per_req=tiles_per_req,
                             f32_rows=f32_rows, small_rows=small_rows, bf16_rows=bf16_rows)
    once = pl.Buffered(1)
    return pl.pallas_call(
        kern,
        grid=(n_tiles,),
        in_specs=[
            pl.BlockSpec(memory_space=pl.ANY),
            pl.BlockSpec(memory_space=pl.ANY),
            pl.BlockSpec(cc.shape, lambda i: (0, 0)),
            pl.BlockSpec(c.shape, lambda i: (0, 0)),
            pl.BlockSpec((D, n_mod), lambda i: (0, 0), pipeline_mode=once),
            pl.BlockSpec((1, n_mod), lambda i: (0, 0)),
            pl.BlockSpec((1, D), lambda i: (0, 0)),
            pl.BlockSpec(w_in_t.shape, lambda i: (0, 0), pipeline_mode=once),
        ],
        out_specs=[pl.BlockSpec((tm, n_f32), lambda i: (i, 0)), pl.BlockSpec((tm, n_bf16), lambda i: (i, 0))],
        out_shape=[jax.ShapeDtypeStruct((Mc + Ml, n_f32), F32), jax.ShapeDtypeStruct((Mc + Ml, n_bf16), BF16)],
        scratch_shapes=[pltpu.VMEM((n_out, D), BF16), pltpu.VMEM((COND_ROWS, n_mod), F32),
                        pltpu.VMEM((X_RING, tm, D), F32), pltpu.SemaphoreType.DMA((X_RING,))],
        compiler_params=pltpu.CompilerParams(dimension_semantics=("arbitrary",),
                                             vmem_limit_bytes=VMEM_LIMIT),
        name="norm_inproj",
    )(xc2d, xl2d, cc, c, w_ada, b_ada, norm_w.reshape(1, D), w_in_t)


def _chunk_loop(n_chunks, unroll, make_units):
    def step(ns):
        pending = list(make_units(ns))
        active = []
        while pending or active:
            for _ in range(min(PIPELINE_STARTS, len(pending))):
                active.append(pending.pop(0))
            alive = []
            for g in active:
                try:
                    next(g)
                    alive.append(g)
                except StopIteration:
                    pass
            active = alive

    if unroll >= n_chunks:
        step(list(range(n_chunks)))
        return

    def body(i, carry):
        step([i * unroll + u for u in range(unroll)])
        return carry

    lax.fori_loop(0, n_chunks // unroll, body, 0)


def _chunk_rows(n):
    if isinstance(n, int):
        return pl.ds(n * CHUNK, CHUNK)
    return pl.ds(pl.multiple_of(n * CHUNK, CHUNK), CHUNK)


def _cast_specs(casts, n_steps):
    in_specs, out_specs, out_shape, args = [], [], [], []
    for w, axis in casts:
        blk = list(w.shape)
        assert blk[axis] % n_steps == 0
        blk[axis] //= n_steps
        assert blk[0] % BF16_ROWS == 0 and blk[1] % LANES == 0
        idx = (lambda b: (b, 0)) if axis == 0 else (lambda b: (0, b))
        in_specs.append(pl.BlockSpec(tuple(blk), idx))
        out_specs.append(pl.BlockSpec(tuple(blk), idx))
        out_shape.append(jax.ShapeDtypeStruct(w.shape, BF16))
        args.append(w)
    return in_specs, out_specs, out_shape, args


def _gla_body(q_ref, k_ref, v_ref, g_ref, sm_ref, s0_ref, wal_ref, bal_ref, gw_ref, out_ref, snew_ref,
              st_scr, sall_scr, qh_scr, qs_scr, kh_scr):
    has_state = s0_ref is not None
    write_state = snew_ref is not None
    T = q_ref.shape[0]
    L = CHUNK
    N = T // L
    HK = q_ref.shape[1]
    DK = HK // H_A
    DV = v_ref.shape[1] // H_A
    scale = DK ** -0.5
    n_pairs = HK // LANES

    lower, upper = _chunk_masks(L)
    tri = (lower.astype(BF16), upper.astype(BF16))
    tmask = (lower, upper)
    lane = lax.broadcasted_iota(jnp.int32, (1, LANES), 1)
    head_mask = (lane < DK, lane >= DK)

    for d in range(2):
        for p in range(n_pairs):
            if has_state:
                st_scr[d, p] = s0_ref[d, p].T
            else:
                st_scr[d, p] = jnp.zeros((LANES, LANES), F32)

    def decay_pre(d, r):
        return _dot(sm_ref[r, :].astype(BF16), wal_ref[:, d * HK:(d + 1) * HK]) + bal_ref[d:d + 1, :]

    neg_pre = jnp.maximum(-(_dot(sm_ref[...].astype(BF16), wal_ref[...])
                            + jnp.concatenate([bal_ref[0:1, :], bal_ref[1:2, :]], axis=1)), 0.0)
    chunk_sums = jnp.sum(neg_pre.reshape(N, L, 2 * HK), axis=1)
    decay_span = (jnp.max(chunk_sums) + L * math.log(2.0)) * (1.0 / TAU_GLA)

    def state_group(ns, dirs=(0, 1)):
        units = [(d, n if d == 0 else N - 1 - n) for n in ns for d in dirs]
        rows = [_chunk_rows(n) for _, n in units]
        vt_all = [[jnp.concatenate([v_ref[r, (2 * p + j) * DV:(2 * p + j + 1) * DV] for j in range(2)],
                                   axis=0).astype(F32).T.astype(BF16) for p in range(n_pairs)] for r in rows]
        yield
        pre = [decay_pre(d, r) for (d, _), r in zip(units, rows)]
        yield
        g = [_log_sigmoid(x) * (1.0 / TAU_GLA) for x in pre]
        yield
        b = [_tri_sum(tri[d], gi, terms=2) for (d, _), gi in zip(units, g)]
        yield
        ks_all, dec_all = [], []
        for (d, _), r, bi in zip(units, rows, b):
            bend = bi[L - 1:L, :] if d == 0 else bi[0:1, :]
            q = q_ref[r, :] * scale
            ks = (k_ref[r, :] * jnp.exp(bend - bi)).astype(BF16)
            qs = q * jnp.exp(bi)
            qh_scr[d, r, :] = (qs * jnp.exp(-bend)).astype(BF16)
            qs_scr[d, r, :] = qs.astype(BF16)
            kh_scr[d, r, :] = ks
            ks_all.append(ks)
            dec_all.append(jnp.exp(bend))
        yield
        upd_all = []
        for vt_u, ks in zip(vt_all, ks_all):
            upd_u = []
            for p in range(n_pairs):
                kp = ks[:, p * LANES:(p + 1) * LANES]
                kk = jnp.concatenate([jnp.where(head_mask[j], kp, jnp.zeros_like(kp)) for j in range(2)], axis=0)
                upd_u.append(_dot(vt_u[p], kk))
            upd_all.append(upd_u)
        yield
        st = {d: [st_scr[d, p] for p in range(n_pairs)] for d in dirs}
        for (d, n), dec, upd in zip(units, dec_all, upd_all):
            for p in range(n_pairs):
                sall_scr[d, n, p] = st[d][p].astype(BF16)
                st[d][p] = st[d][p] * dec[:, p * LANES:(p + 1) * LANES] + upd[p]
        for d in dirs:
            for p in range(n_pairs):
                st_scr[d, p] = st[d][p]

    def stack_heads(x):
        return jnp.concatenate([jnp.where(head_mask[j], x, jnp.zeros_like(x)) for j in range(2)], axis=0)

    tok = lax.broadcasted_iota(jnp.int32, (L, 1), 0)
    row_t = lax.broadcasted_iota(jnp.int32, (2 * L, L), 0) & (L - 1)
    col_s = lax.broadcasted_iota(jnp.int32, (2 * L, L), 1)

    def exact_scores(d, r, p):
        ls = slice(p * LANES, (p + 1) * LANES)
        b = _tri_sum(tri[d], _log_sigmoid(decay_pre(d, r)[:, ls]) * (1.0 / TAU_GLA))
        q = q_ref[r, ls] * scale
        k = k_ref[r, ls]
        acc = jnp.where(row_t == col_s, _dot_nt(stack_heads(q).astype(BF16), k.astype(BF16)), 0.0)
        src = lax.broadcasted_iota(jnp.int32, (L, L), 1)
        h = L // 2
        while h >= 1:
            first = tok & ~(2 * h - 1)
            edge = first + (h - 1 if d == 0 else h)
            b_edge = _tri_sum((src == edge).astype(BF16), b)
            upper = (tok & (2 * h - 1)) >= h
            later, earlier = (upper, ~upper) if d == 0 else (~upper, upper)
            qt = jnp.where(later, q * jnp.exp(b - b_edge), 0.0)
            kt = jnp.where(earlier, k * jnp.exp(b_edge - b), 0.0)
            sc = _dot_nt(stack_heads(qt).astype(BF16), kt.astype(BF16))
            acc = acc + jnp.where((row_t & ~(2 * h - 1)) == (col_s & ~(2 * h - 1)), sc, 0.0)
            h //= 2
        return acc

    def out_group(ns, exact_decay=False):
        pairs = [(d, ni, p) for ni in range(len(ns)) for d in range(2) for p in range(n_pairs)]
        scores, inter = [], []
        for d, ni, p in pairs:
            r = _chunk_rows(ns[ni])
            ls = slice(p * LANES, (p + 1) * LANES)
            if exact_decay:
                scores.append(exact_scores(d, r, p))
            else:
                scores.append(_dot_nt(stack_heads(qh_scr[d, r, ls]), kh_scr[d, r, ls]))
            inter.append(_dot_nt(stack_heads(qs_scr[d, r, ls]), sall_scr[d, ns[ni], p]))
        yield
        probs = [[jnp.where(tmask[d], sc[j * L:(j + 1) * L, :], 0.0).astype(BF16) for j in range(2)]
                 for (d, _, _), sc in zip(pairs, scores)]
        yield
        outs = {}
        for (d, ni, p), pr, it in zip(pairs, probs, inter):
            r = _chunk_rows(ns[ni])
            for j in range(2):
                vs = slice((2 * p + j) * DV, (2 * p + j + 1) * DV)
                outs[(d, ni, 2 * p + j)] = _dot(pr[j], v_ref[r, vs]) + it[j * L:(j + 1) * L, :]
        yield
        for ni, n in enumerate(ns):
            r = _chunk_rows(n)
            for h in range(H_A):
                vs = slice(h * DV, (h + 1) * DV)
                o = outs[(0, ni, h)] + outs[(1, ni, h)]
                out_ref[r, vs] = (_rms(o, gw_ref[:, vs]) * _silu(g_ref[r, vs].astype(F32))).astype(out_ref.dtype)

    def finish():
        if write_state:
            for d in range(2):
                for p in range(n_pairs):
                    snew_ref[d, p] = st_scr[d, p].T

    return state_group, out_group, finish, decay_span


def _gla_scratch(T, HK):
    n_pairs = HK // LANES
    n_chunks = T // CHUNK
    return [
        pltpu.VMEM((2, n_pairs, LANES, LANES), F32),
        pltpu.VMEM((2, n_chunks, n_pairs, LANES, LANES), BF16),
        pltpu.VMEM((2, T, HK), BF16),
        pltpu.VMEM((2, T, HK), BF16),
        pltpu.VMEM((2, T, HK), BF16),
    ]


def _mlstm_body(qk_ref, v_ref, og_ref, sm_ref, c0_ref, n0_ref, m0_ref, cw_ref, bm_ref, gw_ref,
                out_ref, cnew_ref, nnew_ref, mnew_ref,
                pad_scr, qk_scr, y_scr, c_scr, n_scr, m_scr, call_scr, nall_scr, mall_scr, g_scr, f_scr,
                *, grid_w):
    has_state = c0_ref is not None
    write_state = cnew_ref is not None
    T = qk_ref.shape[0]
    L = CHUNK
    N = T // L
    C2 = qk_ref.shape[1]
    HK = C2 // 2
    DK = HK // H_B
    DV = v_ref.shape[1] // H_B
    scale = DK ** -0.5
    n_pairs = HK // LANES
    P = pad_scr.shape[0] - T
    P0 = P // 2
    rows_img = T // grid_w

    lower, upper = _chunk_masks(L)
    tri = (lower.astype(BF16), upper.astype(BF16))
    tmask = (lower, upper)
    lane = lax.broadcasted_iota(jnp.int32, (1, LANES), 1)
    head_mask = (lane < DK, lane >= DK)
    lane_in = lane & (L - 1)

    def lane_cummax(x, d):
        k = 1
        while k < L:
            if d == 0:
                x = jnp.maximum(x, jnp.where(lane_in >= k, pltpu.roll(x, k, axis=1), -jnp.inf))
            else:
                x = jnp.maximum(x, jnp.where(lane_in < L - k, pltpu.roll(x, LANES - k, axis=1), -jnp.inf))
            k *= 2
        return x

    for d in range(2):
        for p in range(n_pairs):
            if has_state:
                c_scr[d, p] = c0_ref[d, p]
                n_scr[2 * d + p:2 * d + p + 1, :] = jnp.concatenate(
                    [n0_ref[d, 2 * p + j:2 * p + j + 1, :] for j in range(2)], axis=1)
            else:
                c_scr[d, p] = jnp.zeros((LANES, LANES), F32)
                n_scr[2 * d + p:2 * d + p + 1, :] = jnp.zeros((1, LANES), F32)
    eye_h = (lax.broadcasted_iota(jnp.int32, (H_B, H_B), 0) == lax.broadcasted_iota(jnp.int32, (H_B, H_B), 1))

    def to_col(row):
        return jnp.sum(jnp.where(eye_h, row, 0.0), axis=1, keepdims=True)

    def to_row(col):
        return jnp.sum(jnp.where(eye_h, col, 0.0), axis=0, keepdims=True)

    for d in range(2):
        if has_state:
            m_scr[H_B * d:H_B * (d + 1), 0:1] = to_col(m0_ref[d:d + 1, :])
        else:
            m_scr[H_B * d:H_B * (d + 1), 0:1] = jnp.zeros((H_B, 1), F32)

    pad_scr[0:P0, :] = jnp.zeros((P0, C2), F32)
    pad_scr[P0 + T:P + T, :] = jnp.zeros((P - P0, C2), F32)

    def copy_in(i, carry):
        r0 = pl.multiple_of(i * L, L)
        pad_scr[pl.ds(P0 + r0, L), :] = qk_ref[pl.ds(r0, L), :]
        return carry

    lax.fori_loop(0, N, copy_in, 0)

    lane_c = lax.broadcasted_iota(jnp.int32, (1, C2), 1)
    qscale = jnp.where(lane_c < HK, scale, 1.0).astype(F32)
    sub = lax.broadcasted_iota(jnp.int32, (L, 1), 0)
    img_rows = (0,) if rows_img == 1 else (-1, 0, 1)

    def conv_tile(i, carry):
        r0 = pl.multiple_of(i * L, L)
        col = lax.rem(r0, grid_w) + sub
        ok_left = col >= 1
        ok_right = col <= grid_w - 2
        sums = [None, None, None]
        for di in img_rows:
            blk = pad_scr[pl.ds(P0 + r0 + di * grid_w - SUBLANES, L + 2 * SUBLANES), :]
            for k in range(3):
                term = blk * cw_ref[di + 1, k:k + 1, :]
                sums[k] = term if sums[k] is None else sums[k] + term
        S = SUBLANES
        acc = (sums[1][S:S + L, :] + jnp.where(ok_left, sums[0][S - 1:S - 1 + L, :], 0.0)
               + jnp.where(ok_right, sums[2][S + 1:S + 1 + L, :], 0.0))
        qk_scr[pl.ds(r0, L), :] = _silu(acc) * qscale
        return carry

    lax.fori_loop(0, N, conv_tile, 0)

    gl = lane - GATE_LANE0
    is_f = ((gl >= H_B) & (gl < 2 * H_B)) | ((gl >= 3 * H_B) & (gl < 4 * H_B))

    def gate_tile(i, carry):
        rows = pl.ds(pl.multiple_of(i * L, L), L)
        x = sm_ref[rows, :] + bm_ref[...]
        y_scr[rows, :] = jnp.where(is_f, _log_sigmoid(x), x)
        return carry

    lax.fori_loop(0, N, gate_tile, 0)


    def state_group(ns, dirs=(0, 1)):
        units = [(d, n if d == 0 else N - 1 - n) for n in ns for d in dirs]
        rows = [_chunk_rows(n) for _, n in units]
        kt_all = [[qk_scr[r, HK + p * LANES:HK + (p + 1) * LANES].T for p in range(n_pairs)] for r in rows]
        yield
        xs = [y_scr[r, :] for r in rows]
        fsum = [_tri_sum(tri[d], x) for (d, _), x in zip(units, xs)]
        yield
        wk_all, f_end, c_end = [], [], []
        for (d, n), r, x, fs in zip(units, rows, xs, fsum):
            y = jnp.where(is_f, fs, x)
            li0 = GATE_LANE0 + 2 * H_B * d
            blk = jnp.concatenate([y, y], axis=0).T[li0:li0 + 2 * H_B, :]
            frow = pltpu.roll(blk, H_B, axis=0)
            grow = blk - frow
            g_scr[d, n] = grow
            f_scr[d, n] = frow
            e_col = L - 1 if d == 0 else 0
            f_end.append(frow[0:H_B, e_col:e_col + 1])
            ce8 = jnp.max(grow, axis=1, keepdims=True)
            c_end.append(ce8[0:H_B, :])
            wk_all.append(jnp.exp(grow[:, 0:L] - ce8))
        yield
        kv_all, ksum_all = [], []
        for r, wk8, kt_u in zip(rows, wk_all, kt_all):
            kv_u, ks_u = [], []
            wk8b = wk8.astype(BF16)
            for p in range(n_pairs):
                kpb = qk_scr[r, HK + p * LANES:HK + (p + 1) * LANES].astype(BF16)
                ks8 = _dot(wk8b, kpb)
                for j in range(2):
                    h = 2 * p + j
                    kwt = (kt_u[p][j * DK:(j + 1) * DK, :] * wk8[h:h + 1, :]).astype(BF16)
                    kv_u.append(_dot(kwt, v_ref[r, h * DV:(h + 1) * DV]))
                    ks_u.append(ks8[h:h + 1, :])
            kv_all.append(kv_u)
            ksum_all.append(ks_u)
        yield
        m_run = {d: m_scr[H_B * d:H_B * (d + 1), 0:1] for d in dirs}
        a_all, b_all = [], []
        for (d, n), fe, ce in zip(units, f_end, c_end):
            mall_scr[d, n, 0:H_B, 0:1] = m_run[d]
            mx = jnp.maximum(m_run[d], ce)
            a_all.append(jnp.exp(m_run[d] - mx))
            b_all.append(jnp.exp(ce - mx))
            m_run[d] = fe + mx
        for d in dirs:
            m_scr[H_B * d:H_B * (d + 1), 0:1] = m_run[d]
        yield
        c_run = {d: [[c_scr[d, p, j * DK:(j + 1) * DK, :] for j in range(2)] for p in range(n_pairs)] for d in dirs}
        n_run = {d: [n_scr[2 * d + p:2 * d + p + 1, :] for p in range(n_pairs)] for d in dirs}
        for (d, n), a4, b4, kv_u, ks_u in zip(units, a_all, b_all, kv_all, ksum_all):
            for p in range(n_pairs):
                nall_scr[d, n, p:p + 1, :] = n_run[d][p]
                a_s = [a4[2 * p + j:2 * p + j + 1, :] for j in range(2)]
                b_s = [b4[2 * p + j:2 * p + j + 1, :] for j in range(2)]
                for j in range(2):
                    cj = c_run[d][p][j]
                    call_scr[d, n, p, j * DK:(j + 1) * DK, :] = cj.astype(BF16)
                    c_run[d][p][j] = a_s[j] * cj + b_s[j] * kv_u[2 * p + j]
                n_run[d][p] = (jnp.where(head_mask[0], a_s[0], a_s[1]) * n_run[d][p]
                               + jnp.where(head_mask[0], b_s[0] * ks_u[2 * p], b_s[1] * ks_u[2 * p + 1]))
        for d in dirs:
            for p in range(n_pairs):
                n_scr[2 * d + p:2 * d + p + 1, :] = n_run[d][p]
                for j in range(2):
                    c_scr[d, p, j * DK:(j + 1) * DK, :] = c_run[d][p][j]

    eye = lower & upper
    ones8 = jnp.ones((SUBLANES, L), BF16)
    sub8 = lax.broadcasted_iota(jnp.int32, (SUBLANES, LANES), 0)
    sub_h = lax.broadcasted_iota(jnp.int32, (H_B, L), 0)
    n_rows = [((sub8 == 2 * p) & head_mask[0]) | ((sub8 == 2 * p + 1) & head_mask[1]) for p in range(n_pairs)]

    def head_rows(vals):
        out = vals[0][0:H_B, :]
        for h in range(1, H_B):
            out = jnp.where(sub_h == h, vals[h][0:H_B, :], out)
        return out

    def out_group(ns):
        chunks = [(d, n) for n in ns for d in range(2)]
        pairs = [(d, n, p) for d, n in chunks for p in range(n_pairs)]
        units = [(d, n, p, j) for d, n, p in pairs for j in range(2)]
        cms = [lane_cummax(g_scr[d, n], d)[0:H_B, 0:L] for d, n in chunks]
        qk2s, qc2s, qn2s = [], [], []
        for d, n, p in pairs:
            r = _chunk_rows(n)
            qp = qk_scr[r, p * LANES:(p + 1) * LANES]
            q2 = jnp.concatenate([jnp.where(head_mask[j], qp, 0.0) for j in range(2)], axis=0).astype(BF16)
            qk2s.append(_dot_nt(q2, qk_scr[r, HK + p * LANES:HK + (p + 1) * LANES].astype(BF16)))
            qc2s.append(_dot(q2, call_scr[d, n, p]))
            nsel = jnp.where(n_rows[p], nall_scr[d, n, p:p + 1, :], 0.0).astype(BF16)
            qn2s.append(_dot_nt(nsel, qp.astype(BF16)))
        yield
        s_all = []
        for ui, (d, n, p, j) in enumerate(units):
            grow = g_scr[d, n, 2 * p + j:2 * p + j + 1, 0:L]
            e = jnp.where(tmask[d], grow, -jnp.inf)
            cmax = jnp.max(e, axis=-1, keepdims=True)
            s_all.append((qk2s[ui // 2][j * L:(j + 1) * L, :] * jnp.exp(e - cmax)).astype(BF16))
        yield
        nums = [_dot(s, v_ref[_chunk_rows(n), (2 * p + j) * DV:(2 * p + j + 1) * DV])
                for (d, n, p, j), s in zip(units, s_all)]
        dens = [_dot_nt(ones8, s) for s in s_all]
        yield
        scales = []
        for ci, (d, n) in enumerate(chunks):
            den_loc = head_rows(dens[ci * H_B:(ci + 1) * H_B])
            qn = qn2s[ci * n_pairs][0:H_B, :]
            for p in range(1, n_pairs):
                qn = qn + qn2s[ci * n_pairs + p][0:H_B, :]
            cm = cms[ci]
            m_prev = mall_scr[d, n, 0:H_B, 0:1]
            delta = cm - m_prev
            t = jnp.exp(-jnp.abs(delta))
            w_loc = jnp.where(delta <= 0.0, t, 1.0)
            w_inter = jnp.where(delta <= 0.0, 1.0, t)
            mt = f_scr[d, n, 0:H_B, 0:L] + jnp.maximum(m_prev, cm)
            den = w_loc * den_loc + w_inter * qn
            rinv = 1.0 / jnp.maximum(jnp.abs(den), jnp.exp(-mt))
            scales.append((w_loc * rinv, w_inter * rinv))
        yield
        hs = []
        for ui, (d, n, p, j) in enumerate(units):
            h = 2 * p + j
            sc_loc, sc_inter = scales[ui // H_B]
            d_loc = jnp.where(eye, sc_loc[h:h + 1, :], 0.0).astype(BF16)
            d_inter = jnp.where(eye, sc_inter[h:h + 1, :], 0.0).astype(BF16)
            hs.append(_dot(d_loc, nums[ui].astype(BF16))
                      + _dot(d_inter, qc2s[ui // 2][j * L:(j + 1) * L, :].astype(BF16)))
        yield
        for ni, n in enumerate(ns):
            r = _chunk_rows(n)
            for h in range(H_B):
                vs = slice(h * DV, (h + 1) * DV)
                o = hs[(2 * ni) * H_B + h] + hs[(2 * ni + 1) * H_B + h]
                out_ref[r, vs] = (_rms(o, gw_ref[:, vs]) * _sigmoid(og_ref[r, vs].astype(F32))).astype(out_ref.dtype)

    def finish():
        if write_state:
            for d in range(2):
                for p in range(n_pairs):
                    cnew_ref[d, p] = c_scr[d, p]
                    for j in range(2):
                        nnew_ref[d, 2 * p + j:2 * p + j + 1, :] = n_scr[2 * d + p:2 * d + p + 1, j * DK:(j + 1) * DK]
                mnew_ref[d:d + 1, :] = to_row(m_scr[H_B * d:H_B * (d + 1), 0:1])

    return state_group, out_group, finish


def _mlstm_scratch(T, C2, grid_w):
    n_pairs = C2 // 2 // LANES
    n_chunks = T // CHUNK
    pad_rows = 2 * (grid_w + SUBLANES) if T // grid_w > 1 else 2 * SUBLANES
    return [
        pltpu.VMEM((T + pad_rows, C2), F32),
        pltpu.VMEM((T, C2), F32),
        pltpu.VMEM((T, SMALL_W), F32),
        pltpu.VMEM((2, n_pairs, LANES, LANES), F32),
        pltpu.VMEM((SUBLANES, LANES), F32),
        pltpu.VMEM((SUBLANES, LANES), F32),
        pltpu.VMEM((2, n_chunks, n_pairs, LANES, LANES), BF16),
        pltpu.VMEM((2, n_chunks, SUBLANES, LANES), F32),
        pltpu.VMEM((2, n_chunks, SUBLANES, LANES), F32),
        pltpu.VMEM((2, n_chunks, SUBLANES, LANES), F32),
        pltpu.VMEM((2, n_chunks, SUBLANES, LANES), F32),
    ]


N_GLA_SCRATCH = 5
N_MLSTM_SCRATCH = 11


def _scan_kernel(*refs, cols, layer, has_state, write_state, n_cast, ride_ada, grid_w, unroll):
    refs = list(refs)
    z_refs = refs[:2]
    del refs[:2]
    s0_ref = c0_ref = n0_ref = m0_ref = None
    if has_state:
        s0_ref, c0_ref, n0_ref, m0_ref = refs[:4]
        del refs[:4]
    wa_ref, bal_ref, gwa_ref, cw_ref, bmg_ref, gwb_ref = refs[:6]
    del refs[:6]
    cast_in = refs[:n_cast]
    del refs[:n_cast]
    if ride_ada:
        ada_in = refs[:4]
        del refs[:4]
    outa_ref, outb_ref = refs[:2]
    del refs[:2]
    snew_ref = cnew_ref = nnew_ref = mnew_ref = None
    if write_state:
        snew_ref, cnew_ref, nnew_ref, mnew_ref = refs[:4]
        del refs[:4]
    cast_out = refs[:n_cast]
    del refs[:n_cast]
    if ride_ada:
        ada_out = refs.pop(0)
    wal_scr, bm_scr = refs[:2]
    del refs[:2]
    gla_scr = refs[:N_GLA_SCRATCH]
    mlstm_scr = refs[N_GLA_SCRATCH:]

    for src, dst in zip(cast_in, cast_out):
        dst[...] = src[...].astype(BF16)
    if ride_ada:
        _ada_tile(*ada_in, ada_out)

    R, HK = wa_ref.shape[1], wa_ref.shape[2]
    wal_scr[...] = jnp.zeros(wal_scr.shape, BF16)
    for d in range(2):
        wal_scr[d * R:(d + 1) * R, d * HK:(d + 1) * HK] = wa_ref[d].astype(BF16)
    lane = lax.broadcasted_iota(jnp.int32, (1, LANES), 1)
    bm = jnp.zeros((1, LANES), F32)
    for g in range(bmg_ref.shape[1]):
        for h in range(H_B):
            bm = jnp.where(lane == GATE_LANE0 + H_B * g + h, bmg_ref[layer, g, h], bm)
    bm_scr[0:1, :] = bm

    def view(name):
        a, c0, w = cols[name]
        return z_refs[a].at[:, pl.ds(c0, w)]

    sm_ref = view("small")
    n_chunks = z_refs[0].shape[0] // CHUNK
    gla = _gla_body(view("qa"), view("ka"), view("va"), view("ga"), sm_ref, s0_ref, wal_scr, bal_ref, gwa_ref,
                    outa_ref, snew_ref, *gla_scr)
    mlstm = _mlstm_body(view("qkb"), view("vb"), view("ob"), sm_ref, c0_ref, n0_ref, m0_ref, cw_ref,
                        bm_scr.at[0:1, :], gwb_ref, outb_ref, cnew_ref, nnew_ref, mnew_ref, *mlstm_scr,
                        grid_w=grid_w)
    gla_state, gla_out, gla_finish, decay_span = gla
    mlstm_state, mlstm_out, mlstm_finish = mlstm

    def passes(gla_out_fn):
        _chunk_loop(n_chunks, unroll, lambda ns: [fn([n], (d,)) for n in ns for d in range(2)
                                                  for fn in (mlstm_state, gla_state)])
        _chunk_loop(n_chunks, unroll, lambda ns: [fn([n]) for n in ns for fn in (mlstm_out, gla_out_fn)])

    wide_decay = decay_span > GLA_FACTORED_DECAY_MAX

    @pl.when(jnp.logical_not(wide_decay))
    def _():
        passes(gla_out)

    @pl.when(wide_decay)
    def _():
        passes(functools.partial(gla_out, exact_decay=True))

    gla_finish()
    mlstm_finish()


def _scan_call(z2d, row0, B, T, states, lw, layer, *, grid_w, write_state, casts=(), ada=None):
    assert row0 % T == 0 and all(z.shape[0] % T == 0 for z in z2d)
    z3 = [z.reshape(z.shape[0] // T, T, z.shape[1]) for z in z2d]
    blk0 = row0 // T
    HK = lw["w_alpha2"].shape[-1]
    DA = lw["gnorm_a_w"].shape[0]
    C2 = lw["conv_w"].shape[-1]
    DB = lw["gnorm_b_w"].shape[0]
    DK_A, DK_B = HK // H_A, C2 // 2 // H_B
    pa, pb = HK // LANES, C2 // 2 // LANES
    n_chunks = T // CHUNK
    has_state = states is not None
    widths = ((("qa", HK), ("ka", HK), ("qkb", C2), ("small", SMALL_W)),
              (("va", DA), ("ga", DA), ("vb", DB), ("ob", DB)))
    cols = {}
    for a, groups in enumerate(widths):
        c0 = 0
        for name, w in groups:
            cols[name] = (a, c0, w)
            c0 += w
        assert c0 == z3[a].shape[2]
    cast_in_specs, cast_out_specs, cast_out_shape, cast_args = _cast_specs(casts, B)
    kern = functools.partial(_scan_kernel, cols=cols, layer=layer, has_state=has_state, write_state=write_state,
                             n_cast=len(casts), ride_ada=ada is not None, grid_w=grid_w,
                             unroll=min(n_chunks, SCAN_UNROLL))

    def per_batch(shape):
        nd = len(shape)
        return pl.BlockSpec((None,) + tuple(shape), lambda b: (b,) + (0,) * nd)

    def per_batch_layer(shape):
        nd = len(shape)
        return pl.BlockSpec((None, None) + tuple(shape), lambda b: (b, layer) + (0,) * nd)

    def of_layer(a):
        return pl.BlockSpec((None,) + a.shape[1:], lambda b: (layer,) + (0,) * (a.ndim - 1))

    def whole(a):
        return pl.BlockSpec(a.shape, lambda b: (0,) * a.ndim)

    state_shapes = ((2, pa, LANES, LANES), (2, pb, LANES, LANES), (2, H_B, DK_B), (2, H_B))
    in_specs = [pl.BlockSpec((None, T, z.shape[2]), lambda b: (b + blk0, 0, 0)) for z in z3]
    args = list(z3)
    if has_state:
        s_gla, s_c, s_n, s_m = states
        depth = s_gla.shape[1]
        args += [s_gla.reshape((B, depth) + state_shapes[0]), s_c.reshape((B, depth) + state_shapes[1]), s_n, s_m]
        in_specs += [per_batch_layer(s) for s in state_shapes]
    args += [lw["w_alpha2"], lw["b_alpha"], lw["gnorm_a_w"].reshape(1, DA), lw["conv_w"], lw["b_mgate"],
             lw["gnorm_b_w"].reshape(1, DB)]
    in_specs += [of_layer(lw["w_alpha2"]), of_layer(lw["b_alpha"]), pl.BlockSpec((1, DA), lambda b: (0, 0)),
                 whole(lw["conv_w"]), pl.BlockSpec(memory_space=pltpu.SMEM), pl.BlockSpec((1, DB), lambda b: (0, 0))]
    args += cast_args
    in_specs += cast_in_specs
    out_specs = [per_batch((T, DA)), per_batch((T, DB))]
    out_shape = [jax.ShapeDtypeStruct((B, T, DA), BF16), jax.ShapeDtypeStruct((B, T, DB), BF16)]
    if write_state:
        out_specs += [per_batch(s) for s in state_shapes]
        out_shape += [jax.ShapeDtypeStruct((B,) + s, F32) for s in state_shapes]
    out_specs += cast_out_specs
    out_shape += cast_out_shape
    if ada is not None:
        cc, c, w_ada, b_ada, col0 = ada
        n_rest = w_ada.shape[1] - col0
        wcol = n_rest // B
        assert n_rest % B == 0 and wcol % LANES == 0 and col0 % wcol == 0
        args += [cc, c, w_ada, b_ada]
        in_specs += [whole(cc), whole(c),
                     pl.BlockSpec((w_ada.shape[0], wcol), lambda b: (0, col0 // wcol + b)),
                     pl.BlockSpec((1, wcol), lambda b: (0, col0 // wcol + b))]
        out_specs.append(pl.BlockSpec((COND_ROWS, wcol), lambda b: (0, b)))
        out_shape.append(jax.ShapeDtypeStruct((COND_ROWS, n_rest), F32))
    scratch = ([pltpu.VMEM((SMALL_W, 2 * HK), BF16), pltpu.VMEM((SUBLANES, LANES), F32)]
               + _gla_scratch(T, HK) + _mlstm_scratch(T, C2, grid_w))
    assert len(scratch) == 2 + N_GLA_SCRATCH + N_MLSTM_SCRATCH
    return pl.pallas_call(
        kern,
        grid=(B,),
        in_specs=in_specs,
        out_specs=out_specs,
        out_shape=out_shape,
        scratch_shapes=scratch,
        compiler_params=pltpu.CompilerParams(dimension_semantics=("arbitrary",),
                                             vmem_limit_bytes=VMEM_LIMIT),
        name="mixer_scans",
    )(*args)


def _outff_kernel(xc_ref, xl_ref, ac_ref, al_ref, bc_ref, bl_ref, mod_ref, n2_ref, fn_ref, wo_ref, w1_ref, w2_ref,
                  yc_ref, yl_ref, *, n_ctx, tiles_per_req, final_norm):
    D = xc_ref.shape[1]
    DA = ac_ref.shape[1]
    is_ctx, row = _tile_group(n_ctx, tiles_per_req)

    def mod(k):
        return mod_ref[pl.ds(row, 1), (k - MOD_SPLIT) * D:(k - MOD_SPLIT + 1) * D]

    def tile(x_ref, a_ref, b_ref, y_ref):
        y = _dot(a_ref[...], wo_ref[0:DA, :]) + _dot(b_ref[...], wo_ref[DA:, :])
        x1 = x_ref[...] + mod(2) * y
        h2 = (_rms(x1, n2_ref[...]) * (1.0 + mod(4)) + mod(3)).astype(BF16)
        u = jnp.maximum(_dot(h2, w1_ref[...]), 0.0)
        x2 = x1 + mod(5) * _dot((u * u).astype(BF16), w2_ref[...])
        y_ref[...] = _rms(x2, fn_ref[...]) if final_norm else x2

    @pl.when(is_ctx)
    def _():
        tile(xc_ref, ac_ref, bc_ref, yc_ref)

    @pl.when(jnp.logical_not(is_ctx))
    def _():
        tile(xl_ref, al_ref, bl_ref, yl_ref)


def _outff_call(xc2d, xl2d, ac, al, bc, bl, mod, norm2_w, final_w, wo, w1, w2, *, tm, tiles_per_req, final_norm):
    (Mc, D), Ml = xc2d.shape, xl2d.shape[0]
    n_ctx = Mc // tm
    DA = ac.shape[1]
    DFF = w1.shape[1]
    kern = functools.partial(_outff_kernel, n_ctx=n_ctx, tiles_per_req=tiles_per_req, final_norm=final_norm)
    once = pl.Buffered(1)
    ctx, lat = _ctx_tile(n_ctx), _lat_tile(n_ctx)
    return pl.pallas_call(
        kern,
        grid=((Mc + Ml) // tm,),
        in_specs=[
            pl.BlockSpec((tm, D), ctx), pl.BlockSpec((tm, D), lat),
            pl.BlockSpec((tm, DA), ctx), pl.BlockSpec((tm, DA), lat),
            pl.BlockSpec((tm, D - DA), ctx), pl.BlockSpec((tm, D - DA), lat),
            pl.BlockSpec(mod.shape, lambda i: (0, 0)),
            pl.BlockSpec((1, D), lambda i: (0, 0)),
            pl.BlockSpec((1, D), lambda i: (0, 0)),
            pl.BlockSpec((D, D), lambda i: (0, 0), pipeline_mode=once),
            pl.BlockSpec((D, DFF), lambda i: (0, 0), pipeline_mode=once),
            pl.BlockSpec((DFF, D), lambda i: (0, 0), pipeline_mode=once),
        ],
        out_specs=[pl.BlockSpec((tm, D), ctx), pl.BlockSpec((tm, D), lat)],
        out_shape=[jax.ShapeDtypeStruct((Mc, D), F32), jax.ShapeDtypeStruct((Ml, D), F32)],
        compiler_params=pltpu.CompilerParams(dimension_semantics=("arbitrary",),
                                             vmem_limit_bytes=VMEM_LIMIT),
        name="outproj_mlp",
    )(xc2d, xl2d, ac, al, bc, bl, mod, norm2_w.reshape(1, D), final_w.reshape(1, D), wo, w1, w2)


def _layer(xc, xl, cond, ada_w, cached, lw, layer, ffw, final_w, final_norm):
    (Bc, Tc, D), (Bl, Tl, _) = xc.shape, xl.shape
    tm = TOKEN_TILE
    assert (Bc * Tc) % tm == 0 and Tl % tm == 0 and (Bc * Tc) % Tl == 0
    xc2d, xl2d = xc.reshape(Bc * Tc, D), xl.reshape(Bl * Tl, D)
    z = _inproj_call(xc2d, xl2d, *cond, *ada_w, lw["norm1_w"], lw["w_in_t"], tm=tm, tiles_per_req=Tl // tm,
                     f32_rows=lw["f32_rows"], small_rows=lw["small_rows"], bf16_rows=lw["bf16_rows"])
    res_c = _scan_call(z, 0, Bc, Tc, None, lw, layer, grid_w=Tc, write_state=True,
                       casts=((ffw[0], 0), (ffw[1], 0), (ffw[2], 0)), ada=(*cond, *ada_w, MOD_SPLIT * D))
    res_l = _scan_call(z, Bc * Tc, Bl, Tl, cached, lw, layer, grid_w=GRID_W, write_state=False)
    wo_b, w1_b, w2_b, mod_out = res_c[-4:]
    yc, yl = _outff_call(xc2d, xl2d, res_c[0].reshape(Bc * Tc, -1), res_l[0].reshape(Bl * Tl, -1),
                         res_c[1].reshape(Bc * Tc, -1), res_l[1].reshape(Bl * Tl, -1), mod_out, lw["norm2_w"],
                         final_w, wo_b, w1_b, w2_b, tm=tm, tiles_per_req=Tl // tm, final_norm=final_norm)
    return yc.reshape(Bc, Tc, D), yl.reshape(Bl, Tl, D), tuple(res_c[2:6])


def _layer_weights(l, norm1_w, norm2_w, w_in, w_alpha2, b_alpha, b_mgate, conv_w, gnorm_a_w, gnorm_b_w):
    hk_a = w_alpha2.shape[-1]
    d_a = gnorm_a_w.shape[-1]
    d_b = gnorm_b_w.shape[-1]
    hk_b = conv_w.shape[-1] // 2
    sizes = (hk_a, hk_a, d_a, d_a, 2 * R_ALPHA, hk_b, hk_b, d_b, d_b, 4 * H_B)
    assert w_alpha2.shape[2] == R_ALPHA and b_mgate.shape[1] * b_mgate.shape[2] == 4 * H_B
    offs = [0]
    for s in sizes:
        offs.append(offs[-1] + s)
    f32_rows = ((offs[0], offs[2] - offs[0]), (offs[5], offs[7] - offs[5]))
    small_rows = ((offs[4], offs[5] - offs[4]), (offs[9], offs[10] - offs[9]))
    bf16_rows = ((offs[2], offs[4] - offs[2]), (offs[7], offs[9] - offs[7]))
    assert all(n % LANES == 0 and r % BF16_ROWS == 0 for r, n in f32_rows + bf16_rows)
    return dict(
        norm1_w=norm1_w[l], norm2_w=norm2_w[l], w_in_t=jnp.swapaxes(w_in[l], 0, 1),
        f32_rows=f32_rows, small_rows=small_rows, bf16_rows=bf16_rows,
        w_alpha2=w_alpha2, b_alpha=b_alpha, b_mgate=b_mgate, conv_w=conv_w[l],
        gnorm_a_w=gnorm_a_w[l], gnorm_b_w=gnorm_b_w[l],
    )


def kernel(x_prompt, x_sample, c, state_gla, state_mlstm_C, state_mlstm_n, state_mlstm_m, c_ctx, w_ada, b_ada, norm1_w, norm2_w, w_in, w_alpha2, b_alpha, b_mgate, conv_w, gnorm_a_w, gnorm_b_w, w_out, w_ff1, w_ff2, final_norm_w):
    depth = w_in.shape[0]
    D = x_prompt.shape[-1]
    Bp, Tp, _ = x_prompt.shape
    Bs = x_sample.shape[0]
    assert 1 + Bs <= COND_ROWS
    cond = (c_ctx.reshape(1, D), c)
    cached = (state_gla, state_mlstm_C, state_mlstm_n, state_mlstm_m)
    xp, xs = x_prompt, x_sample
    s_gla, s_c, s_n, s_m = [], [], [], []
    for l in range(depth):
        lw = _layer_weights(l, norm1_w, norm2_w, w_in, w_alpha2, b_alpha, b_mgate, conv_w,
                            gnorm_a_w, gnorm_b_w)
        xp, xs, ctx = _layer(xp, xs, cond, (w_ada[l], b_ada[l].reshape(1, -1)), cached, lw, l,
                             (w_out[l], w_ff1[l], w_ff2[l]), final_norm_w, l == depth - 1)
        s_gla.append(ctx[0].reshape(Bp, 2, H_A, -1, ctx[0].shape[-1]))
        s_c.append(ctx[1].reshape(Bp, 2, H_B, -1, ctx[1].shape[-1]))
        s_n.append(ctx[2])
        s_m.append(ctx[3])
    dt = x_prompt.dtype
    return (xp, xs, jnp.stack(s_gla, axis=1).astype(dt), jnp.stack(s_c, axis=1).astype(dt),
            jnp.stack(s_n, axis=1).astype(dt), jnp.stack(s_m, axis=1).astype(dt))
```
